```python
import math
import jax, jax.numpy as jnp
from jax import lax
import numpy as np

D_MODEL = 1024
BATCH = 8
SEQ = 4096
DEPTH = 1

CHUNK = 64
N_META = 16
DN_HEADS = 8
DN_DK = 128
DN_DV = 256
DN_CONV = 4
DN_QK = DN_HEADS * DN_DK
DN_V = DN_HEADS * DN_DV
SB_HEADS = 8
SB_DH = 128
SB_W = SB_HEADS * SB_DH
SB_BLOCK = 128
D_FF = -(-8 * D_MODEL // (3 * 256)) * 256
PROJ_WIDTH = 2 * DN_QK + 2 * DN_V + 2 * DN_HEADS + 3 * SB_W + 2 * D_MODEL
RMS_EPS = 1e-6
L2_EPS = 1e-6

kernel_name = 'hybrid_gdn_stickbreak_block'


def _split_points():
    widths = (DN_QK, DN_QK, DN_V, DN_V, DN_HEADS, DN_HEADS, SB_W, SB_W, SB_W, D_MODEL, D_MODEL)
    return [int(s) for s in np.cumsum(widths)[:-1]]


def rmsnorm(x, gain):
    xf = x.astype(jnp.float32)
    y = xf * lax.rsqrt(jnp.mean(xf * xf, axis=-1, keepdims=True) + RMS_EPS)
    return (y * gain.astype(jnp.float32)).astype(x.dtype)


def l2norm(x):
    xf = x.astype(jnp.float32)
    return xf * lax.rsqrt(jnp.sum(xf * xf, axis=-1, keepdims=True) + L2_EPS)


def causal_depthwise_conv(x, w):
    K, C = w.shape
    return lax.conv_general_dilated(
        x, w[:, None, :].astype(x.dtype), window_strides=(1,), padding=[(K - 1, 0)],
        dimension_numbers=('NWC', 'WIO', 'NWC'), feature_group_count=C)


def gated_delta_rule(q, k, v, g, beta):
    B, T, H, DK = q.shape
    DV = v.shape[-1]
    N = T // CHUNK
    f32 = jnp.float32

    def to_chunks(a):
        a = a.astype(f32).reshape((B, N, CHUNK, H) + a.shape[3:])
        return jnp.moveaxis(a, (1, 3), (0, 2))

    q = to_chunks(q) * (DK ** -0.5)
    k = to_chunks(k)
    v = to_chunks(v)
    beta = to_chunks(beta)
    g = jnp.cumsum(to_chunks(g), axis=-1)
    idx = jnp.arange(CHUNK)
    incl = idx[:, None] >= idx[None, :]
    strict = idx[:, None] > idx[None, :]
    decay = jnp.exp(jnp.where(incl, g[..., :, None] - g[..., None, :], -jnp.inf))
    kb = k * beta[..., None]
    lower = jnp.where(strict, jnp.einsum('nbhid,nbhjd->nbhij', kb, k) * decay, 0.0)
    eye = jnp.eye(CHUNK, dtype=f32)
    rhs = jnp.concatenate([v * beta[..., None], kb * jnp.exp(g)[..., None]], axis=-1)
    sol = lax.linalg.triangular_solve(eye + lower, rhs, left_side=True, lower=True)
    u, w = sol[..., :DV], sol[..., DV:]
    attn = jnp.einsum('nbhid,nbhjd->nbhij', q, k) * decay
    q_dec = q * jnp.exp(g)[..., None]
    k_dec = k * jnp.exp(g[..., -1:] - g)[..., None]
    g_last = jnp.exp(g[..., -1])

    def step(S, xs):
        u_c, w_c, attn_c, qd_c, kd_c, gl_c = xs
        v_new = u_c - jnp.einsum('bhcd,bhde->bhce', w_c, S)
        o = jnp.einsum('bhcd,bhde->bhce', qd_c, S) + jnp.einsum('bhij,bhje->bhie', attn_c, v_new)
        S = S * gl_c[..., None, None] + jnp.einsum('bhcd,bhce->bhde', kd_c, v_new)
        return S, o

    S0 = jnp.zeros((B, H, DK, DV), f32)
    _, o = lax.scan(step, S0, (u, w, attn, q_dec, k_dec, g_last))
    return jnp.moveaxis(o, (0, 2), (1, 3)).reshape(B, T, H, DV)


def stick_breaking_attention(q, k, v):
    B, T, H, D = q.shape
    nq = -(-T // SB_BLOCK)
    Tp = nq * SB_BLOCK
    pad = ((0, 0), (0, Tp - T), (0, 0), (0, 0))
    f32 = jnp.float32
    qh = jnp.pad(q.astype(f32), pad).reshape(B, nq, SB_BLOCK, H, D).transpose(1, 0, 3, 2, 4)
    kh = jnp.pad(k.astype(f32), pad).transpose(0, 2, 1, 3)
    vh = jnp.pad(v.astype(f32), pad).transpose(0, 2, 1, 3)
    key_pos = jnp.arange(Tp)
    scale = D ** -0.5

    def block(args):
        q_blk, start = args
        z = jnp.einsum('bhqd,bhkd->bhqk', q_blk, kh) * scale
        q_pos = start + jnp.arange(SB_BLOCK)
        visible = key_pos[None, :] < q_pos[:, None]
        log_keep = jnp.where(visible, jax.nn.log_sigmoid(-z), 0.0)
        log_w = jax.nn.log_sigmoid(z) + lax.cumsum(log_keep, axis=3, reverse=True) - log_keep
        w = jnp.where(visible, jnp.exp(log_w), 0.0)
        return jnp.einsum('bhqk,bhkd->bhqd', w, vh)

    o = lax.map(block, (qh, jnp.arange(nq) * SB_BLOCK))
    return o.transpose(1, 0, 3, 2, 4).reshape(B, Tp, H, D)[:, :T]


def hybrid_mixer(hn, w_in, conv_q, conv_k, conv_v, a_log, dt_bias, dn_gain, sbq_gain, sbk_gain,
                 w_branch_dn, w_branch_sb, w_out):
    B, T, _ = hn.shape
    proj = hn @ w_in
    (dq, dk, dv, dz, da, db, sq, sk, sv, gate_dn, gate_sb) = jnp.split(proj, _split_points(), axis=-1)

    def heads(a, d):
        return a.reshape(B, T, -1, d)

    q = l2norm(heads(jax.nn.silu(causal_depthwise_conv(dq, conv_q)), DN_DK))
    k = l2norm(heads(jax.nn.silu(causal_depthwise_conv(dk, conv_k)), DN_DK))
    v = heads(jax.nn.silu(causal_depthwise_conv(dv, conv_v)), DN_DV)
    g = -jnp.exp(a_log.astype(jnp.float32)) * jax.nn.softplus(da.astype(jnp.float32) + dt_bias.astype(jnp.float32))
    beta = jax.nn.sigmoid(db.astype(jnp.float32))
    pad_l = (-N_META) % CHUNK
    pad_r = (-(T + pad_l)) % CHUNK

    def chunk_pad(a):
        return jnp.pad(a, ((0, 0), (pad_l, pad_r)) + ((0, 0),) * (a.ndim - 2))

    o_dn = gated_delta_rule(chunk_pad(q), chunk_pad(k), chunk_pad(v), chunk_pad(g), chunk_pad(beta))
    o_dn = o_dn[:, pad_l:pad_l + T]
    o_dn = rmsnorm(o_dn, dn_gain) * jax.nn.silu(heads(dz, DN_DV).astype(jnp.float32))
    o_dn = o_dn.astype(hn.dtype).reshape(B, T, DN_V)

    qs = rmsnorm(heads(sq, SB_DH), sbq_gain)
    ks = rmsnorm(heads(sk, SB_DH), sbk_gain)
    o_sb = stick_breaking_attention(qs, ks, heads(sv, SB_DH)).astype(hn.dtype).reshape(B, T, SB_W)

    merged = jax.nn.sigmoid(gate_dn) * (o_dn @ w_branch_dn) + jax.nn.sigmoid(gate_sb) * (o_sb @ w_branch_sb)
    return merged @ w_out


def swiglu(hn, w_ffn_in, w_ffn_out):
    gate, up = jnp.split(hn @ w_ffn_in, 2, axis=-1)
    return (jax.nn.silu(gate) * up) @ w_ffn_out


def _fwd_setup_inputs(seed: int = 0) -> dict:
    key = jax.random.key(seed)
    ks = jax.random.split(key, 20)
    f32 = jnp.float32

    def nrm(k, shape, fan_in):
        return jax.random.normal(k, shape, f32) * (fan_in ** -0.5)

    def gain(k, n):
        return 1.0 + 0.02 * jax.random.normal(k, (DEPTH, n), f32)

    dt = jnp.exp(jax.random.uniform(ks[7], (DEPTH, DN_HEADS), f32) * (math.log(0.1) - math.log(1e-3)) + math.log(1e-3))
    return {
        'x': jax.random.normal(ks[0], (BATCH, SEQ, D_MODEL), f32),
        'meta_tokens': jax.random.normal(ks[1], (N_META, D_MODEL), f32),
        'norm_mix_gain': gain(ks[2], D_MODEL),
        'w_in': nrm(ks[3], (DEPTH, D_MODEL, PROJ_WIDTH), D_MODEL),
        'conv_q': nrm(ks[4], (DEPTH, DN_CONV, DN_QK), DN_CONV),
        'conv_k': nrm(ks[5], (DEPTH, DN_CONV, DN_QK), DN_CONV),
        'conv_v': nrm(ks[6], (DEPTH, DN_CONV, DN_V), DN_CONV),
        'dn_a_log': jnp.log(jax.random.uniform(ks[8], (DEPTH, DN_HEADS), f32, 1.0, 16.0)),
        'dn_dt_bias': dt + jnp.log(-jnp.expm1(-dt)),
        'dn_out_norm_gain': gain(ks[9], DN_DV),
        'sb_q_norm_gain': gain(ks[10], SB_DH),
        'sb_k_norm_gain': gain(ks[11], SB_DH),
        'w_branch_dn': nrm(ks[12], (DEPTH, DN_V, D_MODEL), DN_V),
        'w_branch_sb': nrm(ks[13], (DEPTH, SB_W, D_MODEL), SB_W),
        'w_out': nrm(ks[14], (DEPTH, D_MODEL, D_MODEL), D_MODEL),
        'norm_ffn_gain': gain(ks[15], D_MODEL),
        'w_ffn_in': nrm(ks[16], (DEPTH, D_MODEL, 2 * D_FF), D_MODEL),
        'w_ffn_out': nrm(ks[17], (DEPTH, D_FF, D_MODEL), D_FF),
    }


def _fwd_reference(x, meta_tokens, norm_mix_gain, w_in, conv_q, conv_k, conv_v, dn_a_log, dn_dt_bias,
              dn_out_norm_gain, sb_q_norm_gain, sb_k_norm_gain, w_branch_dn, w_branch_sb, w_out,
              norm_ffn_gain, w_ffn_in, w_ffn_out):
    B = x.shape[0]
    meta = jnp.broadcast_to(meta_tokens[None].astype(x.dtype), (B, N_META, D_MODEL))
    h = jnp.concatenate([meta, x], axis=1)
    for l in range(DEPTH):
        h = h + hybrid_mixer(rmsnorm(h, norm_mix_gain[l]), w_in[l], conv_q[l], conv_k[l], conv_v[l],
                             dn_a_log[l], dn_dt_bias[l], dn_out_norm_gain[l], sb_q_norm_gain[l],
                             sb_k_norm_gain[l], w_branch_dn[l], w_branch_sb[l], w_out[l])
        h = h + swiglu(rmsnorm(h, norm_ffn_gain[l]), w_ffn_in[l], w_ffn_out[l])
    return h[:, N_META:]


import jax as _jax
import jax.numpy as _jnp

TWIN_FORMAT = 'train_step'
FWD_PARAMS = ['x', 'meta_tokens', 'norm_mix_gain', 'w_in', 'conv_q', 'conv_k', 'conv_v', 'dn_a_log', 'dn_dt_bias', 'dn_out_norm_gain', 'sb_q_norm_gain', 'sb_k_norm_gain', 'w_branch_dn', 'w_branch_sb', 'w_out', 'norm_ffn_gain', 'w_ffn_in', 'w_ffn_out']
TWIN_WEIGHTS = ['meta_tokens', 'norm_mix_gain', 'w_in', 'conv_q', 'conv_k', 'conv_v', 'dn_a_log', 'dn_dt_bias', 'dn_out_norm_gain', 'sb_q_norm_gain', 'sb_k_norm_gain', 'w_branch_dn', 'w_branch_sb', 'w_out', 'norm_ffn_gain', 'w_ffn_in', 'w_ffn_out']
TWIN_DIFF_INPUT = 'x'
TWIN_INPUTS = ['x', 'meta_tokens', 'norm_mix_gain', 'w_in', 'conv_q', 'conv_k', 'conv_v', 'dn_a_log', 'dn_dt_bias', 'dn_out_norm_gain', 'sb_q_norm_gain', 'sb_k_norm_gain', 'w_branch_dn', 'w_branch_sb', 'w_out', 'norm_ffn_gain', 'w_ffn_in', 'w_ffn_out', 'loss_target', 'm_meta_tokens', 'm_norm_mix_gain', 'm_w_in', 'm_conv_q', 'm_conv_k', 'm_conv_v', 'm_dn_a_log', 'm_dn_dt_bias', 'm_dn_out_norm_gain', 'm_sb_q_norm_gain', 'm_sb_k_norm_gain', 'm_w_branch_dn', 'm_w_branch_sb', 'm_w_out', 'm_norm_ffn_gain', 'm_w_ffn_in', 'm_w_ffn_out', 'v_meta_tokens', 'v_norm_mix_gain', 'v_w_in', 'v_conv_q', 'v_conv_k', 'v_conv_v', 'v_dn_a_log', 'v_dn_dt_bias', 'v_dn_out_norm_gain', 'v_sb_q_norm_gain', 'v_sb_k_norm_gain', 'v_w_branch_dn', 'v_w_branch_sb', 'v_w_out', 'v_norm_ffn_gain', 'v_w_ffn_in', 'v_w_ffn_out']
TWIN_OUTPUTS = ['loss', 'grad_x', 'grad_meta_tokens', 'grad_norm_mix_gain', 'grad_w_in', 'grad_conv_q', 'grad_conv_k', 'grad_conv_v', 'grad_dn_a_log', 'grad_dn_dt_bias', 'grad_dn_out_norm_gain', 'grad_sb_q_norm_gain', 'grad_sb_k_norm_gain', 'grad_w_branch_dn', 'grad_w_branch_sb', 'grad_w_out', 'grad_norm_ffn_gain', 'grad_w_ffn_in', 'grad_w_ffn_out', 'delta_meta_tokens', 'delta_norm_mix_gain', 'delta_w_in', 'delta_conv_q', 'delta_conv_k', 'delta_conv_v', 'delta_dn_a_log', 'delta_dn_dt_bias', 'delta_dn_out_norm_gain', 'delta_sb_q_norm_gain', 'delta_sb_k_norm_gain', 'delta_w_branch_dn', 'delta_w_branch_sb', 'delta_w_out', 'delta_norm_ffn_gain', 'delta_w_ffn_in', 'delta_w_ffn_out', 'new_m_meta_tokens', 'new_m_norm_mix_gain', 'new_m_w_in', 'new_m_conv_q', 'new_m_conv_k', 'new_m_conv_v', 'new_m_dn_a_log', 'new_m_dn_dt_bias', 'new_m_dn_out_norm_gain', 'new_m_sb_q_norm_gain', 'new_m_sb_k_norm_gain', 'new_m_w_branch_dn', 'new_m_w_branch_sb', 'new_m_w_out', 'new_m_norm_ffn_gain', 'new_m_w_ffn_in', 'new_m_w_ffn_out', 'new_v_meta_tokens', 'new_v_norm_mix_gain', 'new_v_w_in', 'new_v_conv_q', 'new_v_conv_k', 'new_v_conv_v', 'new_v_dn_a_log', 'new_v_dn_dt_bias', 'new_v_dn_out_norm_gain', 'new_v_sb_q_norm_gain', 'new_v_sb_k_norm_gain', 'new_v_w_branch_dn', 'new_v_w_branch_sb', 'new_v_w_out', 'new_v_norm_ffn_gain', 'new_v_w_ffn_in', 'new_v_w_ffn_out']
TWIN_LEAF_KINDS = {'loss': 'loss', 'grad_x': 'grad_x', 'grad_meta_tokens': 'grad_w', 'grad_norm_mix_gain': 'grad_w', 'grad_w_in': 'grad_w', 'grad_conv_q': 'grad_w', 'grad_conv_k': 'grad_w', 'grad_conv_v': 'grad_w', 'grad_dn_a_log': 'grad_w', 'grad_dn_dt_bias': 'grad_w', 'grad_dn_out_norm_gain': 'grad_w', 'grad_sb_q_norm_gain': 'grad_w', 'grad_sb_k_norm_gain': 'grad_w', 'grad_w_branch_dn': 'grad_w', 'grad_w_branch_sb': 'grad_w', 'grad_w_out': 'grad_w', 'grad_norm_ffn_gain': 'grad_w', 'grad_w_ffn_in': 'grad_w', 'grad_w_ffn_out': 'grad_w', 'delta_meta_tokens': 'delta_w', 'delta_norm_mix_gain': 'delta_w', 'delta_w_in': 'delta_w', 'delta_conv_q': 'delta_w', 'delta_conv_k': 'delta_w', 'delta_conv_v': 'delta_w', 'delta_dn_a_log': 'delta_w', 'delta_dn_dt_bias': 'delta_w', 'delta_dn_out_norm_gain': 'delta_w', 'delta_sb_q_norm_gain': 'delta_w', 'delta_sb_k_norm_gain': 'delta_w', 'delta_w_branch_dn': 'delta_w', 'delta_w_branch_sb': 'delta_w', 'delta_w_out': 'delta_w', 'delta_norm_ffn_gain': 'delta_w', 'delta_w_ffn_in': 'delta_w', 'delta_w_ffn_out': 'delta_w', 'new_m_meta_tokens': 'new_m', 'new_m_norm_mix_gain': 'new_m', 'new_m_w_in': 'new_m', 'new_m_conv_q': 'new_m', 'new_m_conv_k': 'new_m', 'new_m_conv_v': 'new_m', 'new_m_dn_a_log': 'new_m', 'new_m_dn_dt_bias': 'new_m', 'new_m_dn_out_norm_gain': 'new_m', 'new_m_sb_q_norm_gain': 'new_m', 'new_m_sb_k_norm_gain': 'new_m', 'new_m_w_branch_dn': 'new_m', 'new_m_w_branch_sb': 'new_m', 'new_m_w_out': 'new_m', 'new_m_norm_ffn_gain': 'new_m', 'new_m_w_ffn_in': 'new_m', 'new_m_w_ffn_out': 'new_m', 'new_v_meta_tokens': 'new_v', 'new_v_norm_mix_gain': 'new_v', 'new_v_w_in': 'new_v', 'new_v_conv_q': 'new_v', 'new_v_conv_k': 'new_v', 'new_v_conv_v': 'new_v', 'new_v_dn_a_log': 'new_v', 'new_v_dn_dt_bias': 'new_v', 'new_v_dn_out_norm_gain': 'new_v', 'new_v_sb_q_norm_gain': 'new_v', 'new_v_sb_k_norm_gain': 'new_v', 'new_v_w_branch_dn': 'new_v', 'new_v_w_branch_sb': 'new_v', 'new_v_w_out': 'new_v', 'new_v_norm_ffn_gain': 'new_v', 'new_v_w_ffn_in': 'new_v', 'new_v_w_ffn_out': 'new_v'}


def _forward(args):
    return _fwd_reference(*[args[k] for k in FWD_PARAMS])


def _output_shape():
    out = _jax.eval_shape(lambda: _forward(_fwd_setup_inputs(0)))
    return out.shape, out.dtype

N_MICROBATCH = 1
ADAM_LR = 0.001
ADAM_B1 = 0.9
ADAM_B2 = 0.999
ADAM_EPS = 1e-08
ADAM_WD = 0.01
ADAM_STEP = 10
PER_EXAMPLE_BATCH_AXIS = {'x': 0, 'loss_target': 0}
SHARED_INPUTS = []
_WEIGHT_DTYPES = {'meta_tokens': _jnp.float32, 'norm_mix_gain': _jnp.float32, 'w_in': _jnp.float32, 'conv_q': _jnp.float32, 'conv_k': _jnp.float32, 'conv_v': _jnp.float32, 'dn_a_log': _jnp.float32, 'dn_dt_bias': _jnp.float32, 'dn_out_norm_gain': _jnp.float32, 'sb_q_norm_gain': _jnp.float32, 'sb_k_norm_gain': _jnp.float32, 'w_branch_dn': _jnp.float32, 'w_branch_sb': _jnp.float32, 'w_out': _jnp.float32, 'norm_ffn_gain': _jnp.float32, 'w_ffn_in': _jnp.float32, 'w_ffn_out': _jnp.float32}
MOMENT_SCALE = {'meta_tokens': 9.237102e-03, 'norm_mix_gain': 8.463995e+00, 'w_in': 1.036013e-01, 'conv_q': 9.430549e-02, 'conv_k': 9.849154e-02, 'conv_v': 1.826503e-01, 'dn_a_log': 2.708920e+00, 'dn_dt_bias': 2.625840e+00, 'dn_out_norm_gain': 1.398397e+01, 'sb_q_norm_gain': 4.633899e+00, 'sb_k_norm_gain': 4.622703e+00, 'w_branch_dn': 3.520362e-01, 'w_branch_sb': 1.840461e-01, 'w_out': 3.736660e-01, 'norm_ffn_gain': 2.486582e+01, 'w_ffn_in': 1.822878e-01, 'w_ffn_out': 2.555185e-01}


def _to_microbatches(a, axis):
    t = _jnp.moveaxis(a, axis, 0)
    t = t.reshape((N_MICROBATCH, t.shape[0] // N_MICROBATCH) + t.shape[1:])
    return _jnp.moveaxis(t, 1, axis + 1)


def setup_inputs(seed: int = 0) -> dict:
    inp = _fwd_setup_inputs(seed)
    key = _jax.random.fold_in(_jax.random.key(seed), 7919)
    shape, _ = _output_shape()
    out = dict(inp)
    out["loss_target"] = _jax.random.normal(_jax.random.fold_in(key, 0), shape, _jnp.float32)
    for i, name in enumerate(TWIN_WEIGHTS):
        w = inp[name].astype(_jnp.float32)
        if MOMENT_SCALE is None:
            s = _jnp.sqrt(_jnp.mean(_jnp.square(w)) + 1e-30)
        else:
            s = MOMENT_SCALE[name]
        km, kv = _jax.random.split(_jax.random.fold_in(key, i + 1))
        out[name] = w
        out["m_" + name] = s * _jax.random.normal(km, w.shape, _jnp.float32)
        out["v_" + name] = (s * s) * _jax.random.uniform(kv, w.shape, _jnp.float32, 0.5, 1.5)
    if N_MICROBATCH > 1:
        for name, axis in PER_EXAMPLE_BATCH_AXIS.items():
            out[name] = _to_microbatches(out[name], axis)
    return {'x': out['x'], 'meta_tokens': out['meta_tokens'], 'norm_mix_gain': out['norm_mix_gain'], 'w_in': out['w_in'], 'conv_q': out['conv_q'], 'conv_k': out['conv_k'], 'conv_v': out['conv_v'], 'dn_a_log': out['dn_a_log'], 'dn_dt_bias': out['dn_dt_bias'], 'dn_out_norm_gain': out['dn_out_norm_gain'], 'sb_q_norm_gain': out['sb_q_norm_gain'], 'sb_k_norm_gain': out['sb_k_norm_gain'], 'w_branch_dn': out['w_branch_dn'], 'w_branch_sb': out['w_branch_sb'], 'w_out': out['w_out'], 'norm_ffn_gain': out['norm_ffn_gain'], 'w_ffn_in': out['w_ffn_in'], 'w_ffn_out': out['w_ffn_out'], 'loss_target': out['loss_target'], 'm_meta_tokens': out['m_meta_tokens'], 'm_norm_mix_gain': out['m_norm_mix_gain'], 'm_w_in': out['m_w_in'], 'm_conv_q': out['m_conv_q'], 'm_conv_k': out['m_conv_k'], 'm_conv_v': out['m_conv_v'], 'm_dn_a_log': out['m_dn_a_log'], 'm_dn_dt_bias': out['m_dn_dt_bias'], 'm_dn_out_norm_gain': out['m_dn_out_norm_gain'], 'm_sb_q_norm_gain': out['m_sb_q_norm_gain'], 'm_sb_k_norm_gain': out['m_sb_k_norm_gain'], 'm_w_branch_dn': out['m_w_branch_dn'], 'm_w_branch_sb': out['m_w_branch_sb'], 'm_w_out': out['m_w_out'], 'm_norm_ffn_gain': out['m_norm_ffn_gain'], 'm_w_ffn_in': out['m_w_ffn_in'], 'm_w_ffn_out': out['m_w_ffn_out'], 'v_meta_tokens': out['v_meta_tokens'], 'v_norm_mix_gain': out['v_norm_mix_gain'], 'v_w_in': out['v_w_in'], 'v_conv_q': out['v_conv_q'], 'v_conv_k': out['v_conv_k'], 'v_conv_v': out['v_conv_v'], 'v_dn_a_log': out['v_dn_a_log'], 'v_dn_dt_bias': out['v_dn_dt_bias'], 'v_dn_out_norm_gain': out['v_dn_out_norm_gain'], 'v_sb_q_norm_gain': out['v_sb_q_norm_gain'], 'v_sb_k_norm_gain': out['v_sb_k_norm_gain'], 'v_w_branch_dn': out['v_w_branch_dn'], 'v_w_branch_sb': out['v_w_branch_sb'], 'v_w_out': out['v_w_out'], 'v_norm_ffn_gain': out['v_norm_ffn_gain'], 'v_w_ffn_in': out['v_w_ffn_in'], 'v_w_ffn_out': out['v_w_ffn_out']}


def _loss(weights, diff, rest, loss_target):
    with _jax.named_scope("forward"):
        args = {**rest, TWIN_DIFF_INPUT: diff, **{k: w.astype(_WEIGHT_DTYPES[k]) for k, w in weights.items()}}
        y = _forward(args)
    with _jax.named_scope("loss_head"):
        err = _jnp.square(y.astype(_jnp.float32) - loss_target)
        return 0.5 * _jnp.sum(_jnp.mean(err, axis=-1)) if err.ndim else 0.5 * err


def _adamw(w, g, m, v):
    m = ADAM_B1 * m + (1.0 - ADAM_B1) * g
    v = ADAM_B2 * v + (1.0 - ADAM_B2) * _jnp.square(g)
    m_hat = m / (1.0 - ADAM_B1 ** ADAM_STEP)
    v_hat = v / (1.0 - ADAM_B2 ** ADAM_STEP)
    delta = -ADAM_LR * (m_hat / (_jnp.sqrt(v_hat) + ADAM_EPS) + ADAM_WD * w)
    return delta, m, v


def reference(x, meta_tokens, norm_mix_gain, w_in, conv_q, conv_k, conv_v, dn_a_log, dn_dt_bias, dn_out_norm_gain, sb_q_norm_gain, sb_k_norm_gain, w_branch_dn, w_branch_sb, w_out, norm_ffn_gain, w_ffn_in, w_ffn_out, loss_target, m_meta_tokens, m_norm_mix_gain, m_w_in, m_conv_q, m_conv_k, m_conv_v, m_dn_a_log, m_dn_dt_bias, m_dn_out_norm_gain, m_sb_q_norm_gain, m_sb_k_norm_gain, m_w_branch_dn, m_w_branch_sb, m_w_out, m_norm_ffn_gain, m_w_ffn_in, m_w_ffn_out, v_meta_tokens, v_norm_mix_gain, v_w_in, v_conv_q, v_conv_k, v_conv_v, v_dn_a_log, v_dn_dt_bias, v_dn_out_norm_gain, v_sb_q_norm_gain, v_sb_k_norm_gain, v_w_branch_dn, v_w_branch_sb, v_w_out, v_norm_ffn_gain, v_w_ffn_in, v_w_ffn_out):
    given = dict(x=x, meta_tokens=meta_tokens, norm_mix_gain=norm_mix_gain, w_in=w_in, conv_q=conv_q, conv_k=conv_k, conv_v=conv_v, dn_a_log=dn_a_log, dn_dt_bias=dn_dt_bias, dn_out_norm_gain=dn_out_norm_gain, sb_q_norm_gain=sb_q_norm_gain, sb_k_norm_gain=sb_k_norm_gain, w_branch_dn=w_branch_dn, w_branch_sb=w_branch_sb, w_out=w_out, norm_ffn_gain=norm_ffn_gain, w_ffn_in=w_ffn_in, w_ffn_out=w_ffn_out, loss_target=loss_target, m_meta_tokens=m_meta_tokens, m_norm_mix_gain=m_norm_mix_gain, m_w_in=m_w_in, m_conv_q=m_conv_q, m_conv_k=m_conv_k, m_conv_v=m_conv_v, m_dn_a_log=m_dn_a_log, m_dn_dt_bias=m_dn_dt_bias, m_dn_out_norm_gain=m_dn_out_norm_gain, m_sb_q_norm_gain=m_sb_q_norm_gain, m_sb_k_norm_gain=m_sb_k_norm_gain, m_w_branch_dn=m_w_branch_dn, m_w_branch_sb=m_w_branch_sb, m_w_out=m_w_out, m_norm_ffn_gain=m_norm_ffn_gain, m_w_ffn_in=m_w_ffn_in, m_w_ffn_out=m_w_ffn_out, v_meta_tokens=v_meta_tokens, v_norm_mix_gain=v_norm_mix_gain, v_w_in=v_w_in, v_conv_q=v_conv_q, v_conv_k=v_conv_k, v_conv_v=v_conv_v, v_dn_a_log=v_dn_a_log, v_dn_dt_bias=v_dn_dt_bias, v_dn_out_norm_gain=v_dn_out_norm_gain, v_sb_q_norm_gain=v_sb_q_norm_gain, v_sb_k_norm_gain=v_sb_k_norm_gain, v_w_branch_dn=v_w_branch_dn, v_w_branch_sb=v_w_branch_sb, v_w_out=v_w_out, v_norm_ffn_gain=v_norm_ffn_gain, v_w_ffn_in=v_w_ffn_in, v_w_ffn_out=v_w_ffn_out)
    weights = {n: given[n] for n in TWIN_WEIGHTS}
    shared = {n: given[n] for n in SHARED_INPUTS}
    per_example = {n: given[n] for n in ['x']}
    grad_fn = _jax.value_and_grad(_loss, argnums=(0, 1))

    def one_microbatch(ex, loss_target):
        ex = dict(ex)
        diff = ex.pop(TWIN_DIFF_INPUT)
        return grad_fn(weights, diff, {**shared, **ex}, loss_target)

    if N_MICROBATCH == 1:
        loss, (grad_w, grad_x) = one_microbatch(per_example, given["loss_target"])
    else:
        def body(carry, xs):
            loss_sum, grad_sum = carry
            l_k, (gw_k, gx_k) = one_microbatch(xs[0], xs[1])
            with _jax.named_scope("update"):
                return (loss_sum + l_k, _jax.tree.map(_jnp.add, grad_sum, gw_k)), gx_k

        init = (_jnp.zeros((), _jnp.float32), _jax.tree.map(_jnp.zeros_like, weights))
        (loss, grad_w), grad_x = _jax.lax.scan(body, init, (per_example, given["loss_target"]))
    with _jax.named_scope("update"):
        delta_w, new_m, new_v = {}, {}, {}
        for n in TWIN_WEIGHTS:
            delta_w[n], new_m[n], new_v[n] = _adamw(weights[n], grad_w[n], given["m_" + n], given["v_" + n])
    return (loss, grad_x, *[grad_w[n] for n in TWIN_WEIGHTS], *[delta_w[n] for n in TWIN_WEIGHTS],
            *[new_m[n] for n in TWIN_WEIGHTS], *[new_v[n] for n in TWIN_WEIGHTS])
```

```python
import functools
import math

import jax
import jax.numpy as jnp
from jax import lax
from jax.experimental import pallas as pl
from jax.experimental.pallas import tpu as pltpu

F32 = jnp.float32
BF16 = jnp.bfloat16

N_DEV = 8
CHUNK = 64
CHUNK_SHIFT = 6
GDN_ROWS = 2 * CHUNK
SB_BLOCK = 128
LANES = 128
RMS_EPS = 1e-6
L2_EPS = 1e-6
ADAM_LR = 0.001
ADAM_B1 = 0.9
ADAM_B2 = 0.999
ADAM_EPS = 1e-08
ADAM_WD = 0.01
ADAM_STEP = 10
V7X_VMEM_LIMIT_BYTES = 56 * 1024 * 1024

MESH = pl.DeviceIdType.MESH


def _params(*sem):
    return pltpu.CompilerParams(dimension_semantics=sem or None, vmem_limit_bytes=V7X_VMEM_LIMIT_BYTES)


def _pick(n, cands):
    for c in cands:
        if n % c == 0:
            return c
    raise ValueError(f"no block size among {cands} divides {n}")


def _bf(x):
    return x.astype(BF16)


def _dot(a, b):
    return jnp.dot(a, b, preferred_element_type=F32)


def _dot_nt(a, b):
    return lax.dot_general(a, b, (((1,), (1,)), ((), ())), preferred_element_type=F32)


def _dot_tn(a, b):
    return lax.dot_general(a, b, (((0,), (0,)), ((), ())), preferred_element_type=F32)


def _split2(x):
    hi = _bf(x)
    return hi, _bf(x - hi.astype(F32))


def _split3(x):
    hi = _bf(x)
    r = x - hi.astype(F32)
    mid = _bf(r)
    return hi, mid, _bf(r - mid.astype(F32))


def _dot_hp(a, b, dot=_dot):
    ah, al = _split2(a)
    bh, bl = _split2(b)
    return dot(ah, bh) + dot(ah, bl) + dot(al, bh)


def _dot_exact_r(x, m, dot=_dot):
    h, mi, lo = _split3(x)
    return dot(h, m) + dot(mi, m) + dot(lo, m)


def _dot_exact_l(m, x, dot=_dot):
    h, mi, lo = _split3(x)
    return dot(m, h) + dot(m, mi) + dot(m, lo)


def _sigmoid(x):
    return 1.0 / (1.0 + jnp.exp(-x))


def _silu(x):
    return x * _sigmoid(x)


def _silu_grad(x):
    s = _sigmoid(x)
    return s * (1.0 + x * (1.0 - s))


def _softplus(x):
    return jnp.maximum(x, 0.0) + jnp.log(1.0 + jnp.exp(-jnp.abs(x)))


def _rms_fwd(h, gain):
    r = lax.rsqrt(jnp.mean(h * h, axis=-1, keepdims=True) + RMS_EPS)
    return h * r * gain


def _rms_bwd(h, gain, dy):
    r = lax.rsqrt(jnp.mean(h * h, axis=-1, keepdims=True) + RMS_EPS)
    dyg = dy * gain
    dh = r * dyg - h * (r * r * r) * jnp.mean(dyg * h, axis=-1, keepdims=True)
    return dh, dy * h * r


def _iota(shape, dim):
    return lax.broadcasted_iota(jnp.int32, shape, dim)


def _lane_pick(x, idx):
    return jnp.sum(jnp.where(_iota(x.shape, 1) == idx, x, 0.0), axis=1, keepdims=True)


def _mm_nn(a, b, *, out_dtype, name):
    m, k = a.shape
    n = b.shape[1]
    tm, tn = _pick(m, (384, 256, 128)), _pick(n, (1024, 512, 256, 128))

    def body(a_ref, b_ref, o_ref):
        o_ref[...] = _dot(a_ref[...], b_ref[...]).astype(out_dtype)

    return pl.pallas_call(
        body, grid=(m // tm, n // tn),
        in_specs=[pl.BlockSpec((tm, k), lambda i, j: (i, 0)), pl.BlockSpec((k, tn), lambda i, j: (0, j))],
        out_specs=pl.BlockSpec((tm, tn), lambda i, j: (i, j)),
        out_shape=jax.ShapeDtypeStruct((m, n), out_dtype), name=name,
        compiler_params=_params("parallel", "parallel"))(a, b)


def _mm_nt(a, b, *, out_dtype, name):
    m, k = a.shape
    n = b.shape[0]
    tm, tn = _pick(m, (384, 256, 128)), _pick(n, (1024, 512, 256, 128))

    def body(a_ref, b_ref, o_ref):
        o_ref[...] = _dot_nt(a_ref[...], b_ref[...]).astype(out_dtype)

    return pl.pallas_call(
        body, grid=(m // tm, n // tn),
        in_specs=[pl.BlockSpec((tm, k), lambda i, j: (i, 0)), pl.BlockSpec((tn, k), lambda i, j: (j, 0))],
        out_specs=pl.BlockSpec((tm, tn), lambda i, j: (i, j)),
        out_shape=jax.ShapeDtypeStruct((m, n), out_dtype), name=name,
        compiler_params=_params("parallel", "parallel"))(a, b)


def _mm_tn(a, b, *, name):
    t, m = a.shape
    n = b.shape[1]
    tm = _pick(m, (1024, 512, 256, 128))
    tn = _pick(n, (1024, 512, 256, 128))
    tk = _pick(t, (384, 256, 128))

    def body(a_ref, b_ref, o_ref):
        @pl.when(pl.program_id(2) == 0)
        def _():
            o_ref[...] = jnp.zeros_like(o_ref)

        o_ref[...] += _dot_tn(a_ref[...], b_ref[...])

    return pl.pallas_call(
        body, grid=(m // tm, n // tn, t // tk),
        in_specs=[pl.BlockSpec((tk, tm), lambda i, j, k: (k, i)), pl.BlockSpec((tk, tn), lambda i, j, k: (k, j))],
        out_specs=pl.BlockSpec((tm, tn), lambda i, j, k: (i, j)),
        out_shape=jax.ShapeDtypeStruct((m, n), F32), name=name,
        compiler_params=_params("parallel", "parallel", "arbitrary"))(a, b)


def _mm_norm(h, gain, w, *, name):
    m, k = h.shape
    n = w.shape[1]
    tm, tn = _pick(m, (384, 256, 128)), _pick(n, (1024, 512, 256, 128))

    def body(h_ref, g_ref, w_ref, o_ref, n_ref):
        @pl.when(pl.program_id(1) == 0)
        def _():
            n_ref[...] = _bf(_rms_fwd(h_ref[...], g_ref[...]))

        o_ref[...] = _dot(n_ref[...], w_ref[...])

    return pl.pallas_call(
        body, grid=(m // tm, n // tn),
        in_specs=[pl.BlockSpec((tm, k), lambda i, j: (i, 0)), pl.BlockSpec((1, k), lambda i, j: (0, 0)),
                  pl.BlockSpec((k, tn), lambda i, j: (0, j))],
        out_specs=[pl.BlockSpec((tm, tn), lambda i, j: (i, j)), pl.BlockSpec((tm, k), lambda i, j: (i, 0))],
        out_shape=[jax.ShapeDtypeStruct((m, n), F32), jax.ShapeDtypeStruct((m, k), BF16)], name=name,
        compiler_params=_params("parallel", "arbitrary"))(h, gain, w)


def _mm_norm_swiglu(h, gain, wg, wu, *, name):
    m, k = h.shape
    n = wg.shape[1]
    tm, tn = _pick(m, (384, 256, 128)), _pick(n, (256, 128))

    def body(h_ref, g_ref, wg_ref, wu_ref, gate_ref, up_ref, act_ref, n_ref):
        @pl.when(pl.program_id(1) == 0)
        def _():
            n_ref[...] = _bf(_rms_fwd(h_ref[...], g_ref[...]))

        gate = _dot(n_ref[...], wg_ref[...])
        up = _dot(n_ref[...], wu_ref[...])
        gate_ref[...] = gate
        up_ref[...] = up
        act_ref[...] = _bf(_silu(gate) * up)

    wspec = pl.BlockSpec((k, tn), lambda i, j: (0, j))
    ospec = pl.BlockSpec((tm, tn), lambda i, j: (i, j))
    return pl.pallas_call(
        body, grid=(m // tm, n // tn),
        in_specs=[pl.BlockSpec((tm, k), lambda i, j: (i, 0)), pl.BlockSpec((1, k), lambda i, j: (0, 0)), wspec, wspec],
        out_specs=[ospec, ospec, ospec, pl.BlockSpec((tm, k), lambda i, j: (i, 0))],
        out_shape=[jax.ShapeDtypeStruct((m, n), F32), jax.ShapeDtypeStruct((m, n), F32),
                   jax.ShapeDtypeStruct((m, n), BF16), jax.ShapeDtypeStruct((m, k), BF16)], name=name,
        compiler_params=_params("parallel", "arbitrary"))(h, gain, wg, wu)


def _mm_res(res, a, b, *, name):
    m, k = a.shape
    n = b.shape[1]
    tm, tn = _pick(m, (384, 256, 128)), _pick(n, (1024, 512, 256, 128))

    def body(r_ref, a_ref, b_ref, o_ref):
        o_ref[...] = r_ref[...] + _dot(a_ref[...], b_ref[...])

    return pl.pallas_call(
        body, grid=(m // tm, n // tn),
        in_specs=[pl.BlockSpec((tm, tn), lambda i, j: (i, j)), pl.BlockSpec((tm, k), lambda i, j: (i, 0)),
                  pl.BlockSpec((k, tn), lambda i, j: (0, j))],
        out_specs=pl.BlockSpec((tm, tn), lambda i, j: (i, j)),
        out_shape=jax.ShapeDtypeStruct((m, n), F32), name=name,
        compiler_params=_params("parallel", "parallel"))(res, a, b)


def _mm_res_loss(res, a, b, target, *, row0, nrows, name):
    m, k = a.shape
    n = b.shape[1]
    tm = _pick(m, (384, 256, 128))

    def body(r_ref, a_ref, b_ref, t_ref, dy_ref, dyb_ref, ls_ref):
        i = pl.program_id(0)

        @pl.when(i == 0)
        def _():
            ls_ref[...] = jnp.zeros_like(ls_ref)

        y = r_ref[...] + _dot(a_ref[...], b_ref[...])
        row = i * tm + _iota((tm, n), 0)
        e = jnp.where((row >= row0) & (row < row0 + nrows), y - t_ref[...], 0.0)
        dy = e / n
        dy_ref[...] = dy
        dyb_ref[...] = _bf(dy)
        ls_ref[...] += jnp.sum(e * e, axis=0, keepdims=True)

    rspec = pl.BlockSpec((tm, n), lambda i: (i, 0))
    return pl.pallas_call(
        body, grid=(m // tm,),
        in_specs=[rspec, pl.BlockSpec((tm, k), lambda i: (i, 0)), pl.BlockSpec((k, n), lambda i: (0, 0)), rspec],
        out_specs=[rspec, rspec, pl.BlockSpec((1, n), lambda i: (0, 0))],
        out_shape=[jax.ShapeDtypeStruct((m, n), F32), jax.ShapeDtypeStruct((m, n), BF16),
                   jax.ShapeDtypeStruct((1, n), F32)], name=name,
        compiler_params=_params("arbitrary"))(res, a, b, target)


def _merge_fwd(o_dn, o_sb, wbd, wbs, proj, *, col_gd, col_gs, name):
    m, kd = o_dn.shape
    ks = o_sb.shape[1]
    n = wbd.shape[1]
    tm = _pick(m, (384, 256, 128))
    tn = _pick(math.gcd(n, math.gcd(col_gd, col_gs)), (512, 256, 128))

    def body(od_ref, os_ref, wd_ref, ws_ref, gd_ref, gs_ref, mg_ref, bd_ref, bs_ref):
        bd = _dot(od_ref[...], wd_ref[...])
        bs = _dot(os_ref[...], ws_ref[...])
        bd_ref[...] = bd
        bs_ref[...] = bs
        mg_ref[...] = _bf(_sigmoid(gd_ref[...]) * bd + _sigmoid(gs_ref[...]) * bs)

    ospec = pl.BlockSpec((tm, tn), lambda i, j: (i, j))
    return pl.pallas_call(
        body, grid=(m // tm, n // tn),
        in_specs=[pl.BlockSpec((tm, kd), lambda i, j: (i, 0)), pl.BlockSpec((tm, ks), lambda i, j: (i, 0)),
                  pl.BlockSpec((kd, tn), lambda i, j: (0, j)), pl.BlockSpec((ks, tn), lambda i, j: (0, j)),
                  pl.BlockSpec((tm, tn), lambda i, j: (i, col_gd // tn + j)),
                  pl.BlockSpec((tm, tn), lambda i, j: (i, col_gs // tn + j))],
        out_specs=[ospec, ospec, ospec],
        out_shape=[jax.ShapeDtypeStruct((m, n), BF16), jax.ShapeDtypeStruct((m, n), F32),
                   jax.ShapeDtypeStruct((m, n), F32)], name=name,
        compiler_params=_params("parallel", "parallel"))(o_dn, o_sb, wbd, wbs, proj, proj)


def _merge_bwd(dh, w_out, proj, br_dn, br_sb, *, col_gd, col_gs, name):
    m, k = dh.shape
    n = w_out.shape[0]
    tm = _pick(m, (384, 256, 128))
    tn = _pick(math.gcd(n, math.gcd(col_gd, col_gs)), (512, 256, 128))

    def body(dh_ref, w_ref, gd_ref, gs_ref, bd_ref, bs_ref, dbd_ref, dbs_ref, dgd_ref, dgs_ref):
        dm = _dot_nt(dh_ref[...], w_ref[...])
        sd = _sigmoid(gd_ref[...])
        ss = _sigmoid(gs_ref[...])
        dbd_ref[...] = _bf(dm * sd)
        dbs_ref[...] = _bf(dm * ss)
        dgd_ref[...] = _bf(dm * bd_ref[...] * sd * (1.0 - sd))
        dgs_ref[...] = _bf(dm * bs_ref[...] * ss * (1.0 - ss))

    ospec = pl.BlockSpec((tm, tn), lambda i, j: (i, j))
    return pl.pallas_call(
        body, grid=(m // tm, n // tn),
        in_specs=[pl.BlockSpec((tm, k), lambda i, j: (i, 0)), pl.BlockSpec((tn, k), lambda i, j: (j, 0)),
                  pl.BlockSpec((tm, tn), lambda i, j: (i, col_gd // tn + j)),
                  pl.BlockSpec((tm, tn), lambda i, j: (i, col_gs // tn + j)), ospec, ospec],
        out_specs=[ospec] * 4,
        out_shape=[jax.ShapeDtypeStruct((m, n), BF16)] * 4, name=name,
        compiler_params=_params("parallel", "parallel"))(dh, w_out, proj, proj, br_dn, br_sb)


def _swiglu_bwd(dy, wfo, gate, up, *, name):
    m, k = dy.shape
    n = wfo.shape[0]
    tm, tn = _pick(m, (384, 256, 128)), _pick(n, (256, 128))

    def body(dy_ref, w_ref, g_ref, u_ref, dg_ref, du_ref):
        da = _dot_nt(dy_ref[...], w_ref[...])
        g = g_ref[...]
        dg_ref[...] = _bf(da * u_ref[...] * _silu_grad(g))
        du_ref[...] = _bf(da * _silu(g))

    ospec = pl.BlockSpec((tm, tn), lambda i, j: (i, j))
    return pl.pallas_call(
        body, grid=(m // tm, n // tn),
        in_specs=[pl.BlockSpec((tm, k), lambda i, j: (i, 0)), pl.BlockSpec((tn, k), lambda i, j: (j, 0)), ospec, ospec],
        out_specs=[ospec, ospec], out_shape=[jax.ShapeDtypeStruct((m, n), BF16)] * 2, name=name,
        compiler_params=_params("parallel", "parallel"))(dy, wfo, gate, up)


def _mm_nt_rmsbwd(pairs, extra, h, gain, dres, *, name):
    m, k = pairs[0][0].shape
    n = h.shape[1]
    tm = _pick(m, (384, 256, 128))
    tk = _pick(k, (1408, 1024, 512, 256, 128))
    nk = k // tk
    np_ = len(pairs)

    def body(*refs):
        ab = refs[:2 * np_]
        pos = 2 * np_
        ex = refs[pos:pos + 2] if extra is not None else ()
        pos += len(ex)
        h_ref, g_ref, r_ref, dh_ref, dhb_ref, dg_ref, acc_ref = refs[pos:]
        i, kk = pl.program_id(0), pl.program_id(1)

        @pl.when((i == 0) & (kk == 0))
        def _():
            dg_ref[...] = jnp.zeros_like(dg_ref)

        part = _dot_nt(ab[0][...], ab[1][...])
        for p in range(1, np_):
            part += _dot_nt(ab[2 * p][...], ab[2 * p + 1][...])

        @pl.when(kk == 0)
        def _():
            first = part
            if ex:
                first = first + _dot_nt(ex[0][...], ex[1][...])
            acc_ref[...] = first

        @pl.when(kk > 0)
        def _():
            acc_ref[...] += part

        @pl.when(kk == nk - 1)
        def _():
            dh, dgr = _rms_bwd(h_ref[...], g_ref[...], acc_ref[...])
            dh = dh + r_ref[...]
            dh_ref[...] = dh
            dhb_ref[...] = _bf(dh)
            dg_ref[...] += jnp.sum(dgr, axis=0, keepdims=True)

    in_specs, args = [], []
    for a, b in pairs:
        in_specs += [pl.BlockSpec((tm, tk), lambda i, kk: (i, kk)), pl.BlockSpec((n, tk), lambda i, kk: (0, kk))]
        args += [a, b]
    if extra is not None:
        k2 = extra[0].shape[1]
        in_specs += [pl.BlockSpec((tm, k2), lambda i, kk: (i, 0)), pl.BlockSpec((n, k2), lambda i, kk: (0, 0))]
        args += list(extra)
    rspec = pl.BlockSpec((tm, n), lambda i, kk: (i, 0))
    in_specs += [rspec, pl.BlockSpec((1, n), lambda i, kk: (0, 0)), rspec]
    return pl.pallas_call(
        body, grid=(m // tm, nk), in_specs=in_specs,
        out_specs=[rspec, rspec, pl.BlockSpec((1, n), lambda i, kk: (0, 0))],
        out_shape=[jax.ShapeDtypeStruct((m, n), F32), jax.ShapeDtypeStruct((m, n), BF16),
                   jax.ShapeDtypeStruct((1, n), F32)],
        scratch_shapes=[pltpu.VMEM((tm, n), F32)], name=name,
        compiler_params=_params("arbitrary", "arbitrary"))(*args, h, gain, dres)


def _conv_taps(cur, prev8, w_ref, first):
    nk = w_ref.shape[0]
    out = cur * w_ref[nk - 1:nk, :]
    for s in range(1, nk):
        out += _shift_down(cur, prev8, s, first) * w_ref[nk - 1 - s:nk - s, :]
    return out


def _shift_up(cur, next8, s, last):
    rows = cur.shape[0]
    row = _iota(cur.shape, 0)
    next8 = jnp.where(last, 0.0, next8)
    sh = pltpu.roll(cur, rows - s, axis=0)
    nh = jnp.tile(pltpu.roll(next8, 8 - s, axis=0), (rows // 8, 1))
    return jnp.where(row >= rows - s, nh, sh)


def _shift_down(cur, prev8, s, first):
    rows = cur.shape[0]
    row = _iota(cur.shape, 0)
    prev8 = jnp.where(first, 0.0, prev8)
    sh = pltpu.roll(cur, s, axis=0)
    ph = jnp.tile(pltpu.roll(prev8, s, axis=0), (rows // 8, 1))
    return jnp.where(row < s, ph, sh)


def _gdn_pre(proj, pab, cq, ck, cv, a_log, dt_bias, *, heads, dk, dv, col_q, col_k, col_v, row_lo, row_hi, name):
    t = proj.shape[0]
    tm = _pick(t, (384, 256, 128))
    nb = t // tm

    def body(pq_ref, pqp_ref, pk_ref, pkp_ref, pv_ref, pvp_ref, ab_ref, cq_ref, ck_ref, cv_ref, al_ref, dt_ref,
             qn_ref, kn_ref, v_ref, g_ref, b_ref):
        h, i = pl.program_id(0), pl.program_id(1)
        first = i == 0
        row = i * tm + _iota((tm, 1), 0)
        valid = (row >= row_lo) & (row < row_hi)
        q1 = _silu(_conv_taps(pq_ref[...], pqp_ref[...], cq_ref, first))
        k1 = _silu(_conv_taps(pk_ref[...], pkp_ref[...], ck_ref, first))
        v1 = _silu(_conv_taps(pv_ref[...], pvp_ref[...], cv_ref, first))
        qn_ref[...] = jnp.where(valid, q1 * lax.rsqrt(jnp.sum(q1 * q1, axis=-1, keepdims=True) + L2_EPS), 0.0)
        kn_ref[...] = jnp.where(valid, k1 * lax.rsqrt(jnp.sum(k1 * k1, axis=-1, keepdims=True) + L2_EPS), 0.0)
        v_ref[...] = jnp.where(valid, v1, 0.0)
        ab = ab_ref[...]
        da = _lane_pick(ab, h)
        db = _lane_pick(ab, heads + h)
        a = _lane_pick(al_ref[...], h)
        dtb = _lane_pick(dt_ref[...], h)
        g_ref[...] = jnp.where(valid, -jnp.exp(a) * _softplus(da + dtb), 0.0)
        b_ref[...] = jnp.where(valid, _sigmoid(db), 0.0)

    def cur(width, col):
        return pl.BlockSpec((tm, width), lambda h, i: (i, col // width + h))

    def prev(width, col):
        return pl.BlockSpec((8, width), lambda h, i: (jnp.maximum(i * (tm // 8) - 1, 0), col // width + h))

    def out(width):
        return pl.BlockSpec((None, tm, width), lambda h, i: (h, i, 0))

    small = pl.BlockSpec((1, LANES), lambda h, i: (0, 0))
    return pl.pallas_call(
        body, grid=(heads, nb),
        in_specs=[cur(dk, col_q), prev(dk, col_q), cur(dk, col_k), prev(dk, col_k), cur(dv, col_v), prev(dv, col_v),
                  pl.BlockSpec((tm, LANES), lambda h, i: (i, 0)),
                  pl.BlockSpec((cq.shape[0], dk), lambda h, i: (0, h)), pl.BlockSpec((ck.shape[0], dk), lambda h, i: (0, h)),
                  pl.BlockSpec((cv.shape[0], dv), lambda h, i: (0, h)), small, small],
        out_specs=[out(dk), out(dk), out(dv), out(1), out(1)],
        out_shape=[jax.ShapeDtypeStruct((heads, t, dk), F32), jax.ShapeDtypeStruct((heads, t, dk), F32),
                   jax.ShapeDtypeStruct((heads, t, dv), F32), jax.ShapeDtypeStruct((heads, t, 1), F32),
                   jax.ShapeDtypeStruct((heads, t, 1), F32)], name=name,
        compiler_params=_params("parallel", "parallel"))(proj, proj, proj, proj, proj, proj, pab, cq, ck, cv, a_log, dt_bias)


def _chunk_masks(rows=GDN_ROWS, row0=0):
    ri = row0 + _iota((rows, GDN_ROWS), 0)
    ci = _iota((rows, GDN_ROWS), 1)
    same = jnp.right_shift(ri, CHUNK_SHIFT) == jnp.right_shift(ci, CHUNK_SHIFT)
    return same, same & (ri >= ci), same & (ri > ci), ri == ci


def _col_to_row(col, eye):
    return jnp.sum(jnp.where(eye, col, 0.0), axis=0, keepdims=True)


def _row_to_col(row, eye):
    return jnp.sum(jnp.where(eye, row, 0.0), axis=1, keepdims=True)


def _chunk_common(qn, kn, g, beta, dk_scale):
    same, incl, strict, eye = _chunk_masks()
    gb = jnp.broadcast_to(g, (GDN_ROWS, LANES))
    gam = jnp.max(_dot_exact_l(jnp.where(incl, 1.0, 0.0).astype(BF16), gb), axis=1, keepdims=True)
    gam_last = jnp.max(_dot_exact_l(jnp.where(same, 1.0, 0.0).astype(BF16), gb), axis=1, keepdims=True)
    diff = gam - _col_to_row(gam, eye)
    decay = jnp.where(incl, jnp.exp(jnp.where(incl, diff, 0.0)), 0.0)
    eg = jnp.exp(gam)
    ek = jnp.exp(gam_last - gam)
    kb = kn * beta
    qt = qn * dk_scale
    lmat = jnp.where(strict, _dot_nt(_bf(kb), _bf(kn)) * decay, 0.0)
    pmat = jnp.where(incl, _dot_nt(_bf(qt), _bf(kn)) * decay, 0.0)
    return dict(incl=incl, strict=strict, eye=eye, decay=decay, eg=eg, ek=ek, egl=jnp.exp(gam_last),
                kb=kb, qt=qt, lmat=lmat, pmat=pmat)


def _gdn_prep(qn, kn, v, g, beta, *, name):
    heads, t, dk = qn.shape
    dv = v.shape[2]
    nb = t // GDN_ROWS
    dk_scale = dk ** -0.5

    def body(q_ref, k_ref, v_ref, g_ref, b_ref, u_ref, w_ref, p_ref, qd_ref, kd_ref, egl_ref, t_ref):
        kn_, beta_ = k_ref[...], b_ref[...]
        c = _chunk_common(q_ref[...], kn_, g_ref[...], beta_, dk_scale)
        eye_f = jnp.where(c["eye"], 1.0, 0.0)
        x = c["lmat"]
        tinv = eye_f - x
        for _ in range(int(math.log2(CHUNK)) - 1):
            x = _dot_hp(x, x)
            tinv = tinv + _dot_hp(tinv, x)
        u_ref[...] = _dot_hp(tinv, v_ref[...] * beta_)
        w_ref[...] = _dot_hp(tinv, c["kb"] * c["eg"])
        p_ref[...] = c["pmat"]
        qd_ref[...] = c["qt"] * c["eg"]
        kd_ref[...] = kn_ * c["ek"]
        egl_ref[...] = c["egl"]
        t_ref[...] = tinv

    def blk(width):
        return pl.BlockSpec((None, GDN_ROWS, width), lambda h, i: (h, i, 0))

    def shp(width):
        return jax.ShapeDtypeStruct((heads, t, width), F32)

    return pl.pallas_call(
        body, grid=(heads, nb),
        in_specs=[blk(dk), blk(dk), blk(dv), blk(1), blk(1)],
        out_specs=[blk(dv), blk(dk), blk(GDN_ROWS), blk(dk), blk(dk), blk(1), blk(GDN_ROWS)],
        out_shape=[shp(dv), shp(dk), shp(GDN_ROWS), shp(dk), shp(dk), shp(1), shp(GDN_ROWS)], name=name,
        compiler_params=_params("parallel", "parallel"))(qn, kn, v, g, beta)


def _gdn_scan(u, w, p, qd, kd, egl, proj, gain, *, col_z, name):
    heads, t, dv = u.shape
    dk = w.shape[2]
    nb = t // GDN_ROWS
    sub = GDN_ROWS // CHUNK

    def body(u_ref, w_ref, p_ref, qd_ref, kd_ref, egl_ref, z_ref, gn_ref, o_ref, og_ref, st_ref, s_ref):
        @pl.when(pl.program_id(1) == 0)
        def _():
            s_ref[...] = jnp.zeros_like(s_ref)

        vn_parts = [jnp.zeros((CHUNK, dv), F32)] * sub
        for c in range(sub):
            r = pl.ds(c * CHUNK, CHUNK)
            s = s_ref[...]
            st_ref[c] = s
            sb = _bf(s)
            vn = u_ref[r, :] - _dot(_bf(w_ref[r, :]), sb)
            vn_parts[c] = vn
            vfull = _bf(jnp.concatenate(vn_parts, axis=0))
            o = _dot(_bf(qd_ref[r, :]), sb) + _dot(_bf(p_ref[r, :]), vfull)
            gl = egl_ref[pl.ds(c * CHUNK, 1), :]
            s_ref[...] = s * gl + _dot_tn(_bf(kd_ref[r, :]), _bf(vn))
            o_ref[r, :] = o
            og_ref[r, :] = _bf(_rms_fwd(o, gn_ref[...]) * _silu(z_ref[r, :]))

    def blk(width):
        return pl.BlockSpec((None, GDN_ROWS, width), lambda h, i: (h, i, 0))

    return pl.pallas_call(
        body, grid=(heads, nb),
        in_specs=[blk(dv), blk(dk), blk(GDN_ROWS), blk(dk), blk(dk), blk(1),
                  pl.BlockSpec((GDN_ROWS, dv), lambda h, i: (i, col_z // dv + h)),
                  pl.BlockSpec((1, dv), lambda h, i: (0, 0))],
        out_specs=[blk(dv), pl.BlockSpec((GDN_ROWS, dv), lambda h, i: (i, h)),
                   pl.BlockSpec((None, sub, dk, dv), lambda h, i: (h, i, 0, 0))],
        out_shape=[jax.ShapeDtypeStruct((heads, t, dv), F32), jax.ShapeDtypeStruct((t, heads * dv), BF16),
                   jax.ShapeDtypeStruct((heads, t // CHUNK, dk, dv), F32)],
        scratch_shapes=[pltpu.VMEM((dk, dv), F32)], name=name,
        compiler_params=_params("parallel", "arbitrary"))(u, w, p, qd, kd, egl, proj, gain)


def _gdn_post_bwd(o, proj, gain, dout, *, col_z, name):
    heads, t, dv = o.shape
    tm = _pick(t, (384, 256, 128))

    def body(o_ref, z_ref, gn_ref, d_ref, do_ref, dz_ref, dg_ref):
        @pl.when((pl.program_id(0) == 0) & (pl.program_id(1) == 0))
        def _():
            dg_ref[...] = jnp.zeros_like(dg_ref)

        o_, z, d = o_ref[...], z_ref[...], d_ref[...]
        y = _rms_fwd(o_, gn_ref[...])
        dz_ref[...] = _bf(d * y * _silu_grad(z))
        do, dgr = _rms_bwd(o_, gn_ref[...], d * _silu(z))
        do_ref[...] = do
        dg_ref[...] += jnp.sum(dgr, axis=0, keepdims=True)

    return pl.pallas_call(
        body, grid=(t // tm, heads),
        in_specs=[pl.BlockSpec((None, tm, dv), lambda i, h: (h, i, 0)),
                  pl.BlockSpec((tm, dv), lambda i, h: (i, col_z // dv + h)),
                  pl.BlockSpec((1, dv), lambda i, h: (0, 0)), pl.BlockSpec((tm, dv), lambda i, h: (i, h))],
        out_specs=[pl.BlockSpec((None, tm, dv), lambda i, h: (h, i, 0)), pl.BlockSpec((tm, dv), lambda i, h: (i, h)),
                   pl.BlockSpec((1, dv), lambda i, h: (0, 0))],
        out_shape=[jax.ShapeDtypeStruct((heads, t, dv), F32), jax.ShapeDtypeStruct((t, heads * dv), BF16),
                   jax.ShapeDtypeStruct((1, dv), F32)], name=name,
        compiler_params=_params("arbitrary", "arbitrary"))(o, proj, gain, dout)


def _gdn_bwd_scan(u, w, p, qd, kd, egl, st, do, *, name):
    heads, t, dv = u.shape
    dk = w.shape[2]
    nb = t // GDN_ROWS
    sub = GDN_ROWS // CHUNK

    def body(u_ref, w_ref, p_ref, qd_ref, kd_ref, egl_ref, st_ref, do_ref,
             du_ref, dw_ref, dp_ref, dqd_ref, dkd_ref, dgl_ref, ds_ref):
        @pl.when(pl.program_id(1) == 0)
        def _():
            ds_ref[...] = jnp.zeros_like(ds_ref)

        for c in reversed(range(sub)):
            r = pl.ds(c * CHUNK, CHUNK)
            s = st_ref[c]
            sb = _bf(s)
            ds = ds_ref[...]
            dsb = _bf(ds)
            dob = _bf(do_ref[r, :])
            wb, pb, qdb, kdb = _bf(w_ref[r, :]), _bf(p_ref[r, :]), _bf(qd_ref[r, :]), _bf(kd_ref[r, :])
            vn = u_ref[r, :] - _dot(wb, sb)
            zeros = jnp.zeros((CHUNK, dv), BF16)
            vfull = jnp.concatenate([_bf(vn) if cc == c else zeros for cc in range(sub)], axis=0)
            dvn = _dot_tn(pb, dob)[c * CHUNK:(c + 1) * CHUNK, :] + _dot(kdb, dsb)
            dvb = _bf(dvn)
            gl = egl_ref[pl.ds(c * CHUNK, 1), :]
            du_ref[r, :] = dvn
            dw_ref[r, :] = -_dot_nt(dvb, sb)
            dp_ref[r, :] = jnp.where(_chunk_masks(CHUNK, c * CHUNK)[1], _dot_nt(dob, vfull), 0.0)
            dqd_ref[r, :] = _dot_nt(dob, sb)
            dkd_ref[r, :] = _dot_nt(_bf(vn), dsb)
            dgl = jnp.sum(jnp.sum(ds * s, axis=1, keepdims=True), axis=0, keepdims=True)
            dgl_ref[r, :] = jnp.where(_iota((CHUNK, 1), 0) == CHUNK - 1, dgl, 0.0)
            ds_ref[...] = ds * gl + _dot_tn(qdb, dob) - _dot_tn(wb, dvb)

    def blk(width):
        return pl.BlockSpec((None, GDN_ROWS, width), lambda h, i: (h, nb - 1 - i, 0))

    def shp(width):
        return jax.ShapeDtypeStruct((heads, t, width), F32)

    return pl.pallas_call(
        body, grid=(heads, nb),
        in_specs=[blk(dv), blk(dk), blk(GDN_ROWS), blk(dk), blk(dk), blk(1),
                  pl.BlockSpec((None, sub, dk, dv), lambda h, i: (h, nb - 1 - i, 0, 0)), blk(dv)],
        out_specs=[blk(dv), blk(dk), blk(GDN_ROWS), blk(dk), blk(dk), blk(1)],
        out_shape=[shp(dv), shp(dk), shp(GDN_ROWS), shp(dk), shp(dk), shp(1)],
        scratch_shapes=[pltpu.VMEM((dk, dv), F32)], name=name,
        compiler_params=_params("parallel", "arbitrary"))(u, w, p, qd, kd, egl, st, do)


def _gdn_bwd_prep(qn, kn, v, g, beta, tinv, u, w, du, dw, dp, dqd, dkd, dgl, *, name):
    heads, t, dk = qn.shape
    dv = v.shape[2]
    nb = t // GDN_ROWS
    dk_scale = dk ** -0.5

    def rowsum(x):
        return jnp.sum(x, axis=1, keepdims=True)

    def body(q_ref, k_ref, v_ref, g_ref, b_ref, t_ref, u_ref, w_ref, du_ref, dw_ref, dp_ref, dqd_ref, dkd_ref, dgl_ref,
             dq_ref, dkk_ref, dvv_ref, dg_ref, db_ref):
        kn_, beta_, v_ = k_ref[...], b_ref[...], v_ref[...]
        c = _chunk_common(q_ref[...], kn_, g_ref[...], beta_, dk_scale)
        eye, incl, strict, decay = c["eye"], c["incl"], c["strict"], c["decay"]
        kb, qt, eg, ek = c["kb"], c["qt"], c["eg"], c["ek"]
        tinv_ = t_ref[...]
        dbv = _dot_hp(tinv_, du_ref[...], _dot_tn)
        dbw = _dot_hp(tinv_, dw_ref[...], _dot_tn)
        da = -(_dot_nt(_bf(dbv), _bf(u_ref[...])) + _dot_nt(_bf(dbw), _bf(w_ref[...])))
        dl = jnp.where(strict, da, 0.0)
        dp_ = dp_ref[...]
        dm = _bf(dl * decay)
        dn = _bf(dp_ * decay)
        knb = _bf(kn_)
        dkb = _dot(dm, knb) + dbw * eg
        dkn = _dot_tn(dm, _bf(kb)) + _dot_tn(dn, _bf(qt))
        dqt = _dot(dn, knb)
        gmat = dl * c["lmat"] + dp_ * c["pmat"]
        dqd_, dkd_ = dqd_ref[...], dkd_ref[...]
        qd = qt * eg
        kd = kn_ * ek
        bw = kb * eg
        kdsum = rowsum(dkd_ * kd)
        dgam = rowsum(gmat) - _row_to_col(jnp.sum(gmat, axis=0, keepdims=True), eye)
        dgam += rowsum(dbw * bw) + rowsum(dqd_ * qd) - kdsum
        last = (_iota((GDN_ROWS, 1), 0) & (CHUNK - 1)) == CHUNK - 1
        same_f = jnp.where(_chunk_masks()[0], 1.0, 0.0).astype(BF16)
        chunk_tot = jnp.max(_dot_exact_l(same_f, jnp.broadcast_to(kdsum, (GDN_ROWS, LANES))), axis=1, keepdims=True)
        dgam += jnp.where(last, chunk_tot, 0.0) + dgl_ref[...] * c["egl"]
        dq_ref[...] = (dqt + dqd_ * eg) * dk_scale
        dkk_ref[...] = dkn + dkd_ * ek + dkb * beta_
        dvv_ref[...] = dbv * beta_
        db_ref[...] = rowsum(dbv * v_) + rowsum(dkb * kn_)
        upper = jnp.where(_chunk_masks()[0] & (_iota((GDN_ROWS, GDN_ROWS), 0) <= _iota((GDN_ROWS, GDN_ROWS), 1)), 1.0, 0.0)
        dgb = _dot_exact_l(upper.astype(BF16), jnp.broadcast_to(dgam, (GDN_ROWS, LANES)))
        dg_ref[...] = _lane_pick(dgb, 0)

    def blk(width):
        return pl.BlockSpec((None, GDN_ROWS, width), lambda h, i: (h, i, 0))

    def shp(width):
        return jax.ShapeDtypeStruct((heads, t, width), F32)

    return pl.pallas_call(
        body, grid=(heads, nb),
        in_specs=[blk(dk), blk(dk), blk(dv), blk(1), blk(1), blk(GDN_ROWS), blk(dv), blk(dk),
                  blk(dv), blk(dk), blk(GDN_ROWS), blk(dk), blk(dk), blk(1)],
        out_specs=[blk(dk), blk(dk), blk(dv), blk(1), blk(1)],
        out_shape=[shp(dk), shp(dk), shp(dv), shp(1), shp(1)], name=name,
        compiler_params=_params("parallel", "parallel"))(qn, kn, v, g, beta, tinv, u, w, du, dw, dp, dqd, dkd, dgl)


def _gdn_pre_bwd_a(proj, pab, cq, ck, cv, a_log, dt_bias, dqn, dkn, dvv, dg, dbeta, *,
                   heads, dk, dv, col_q, col_k, col_v, row_lo, row_hi, name):
    t = proj.shape[0]
    tm = _pick(t, (384, 256, 128))
    nb = t // tm

    def body(pq_ref, pqp_ref, pk_ref, pkp_ref, pv_ref, pvp_ref, ab_ref, cq_ref, ck_ref, cv_ref, al_ref, dt_ref,
             dqn_ref, dkn_ref, dvv_ref, dg_ref, db_ref, dcq_ref, dck_ref, dcv_ref, dab_ref, dal_ref, ddt_ref):
        i, h = pl.program_id(0), pl.program_id(1)
        first = i == 0

        @pl.when((i == 0) & (h == 0))
        def _():
            dal_ref[...] = jnp.zeros_like(dal_ref)
            ddt_ref[...] = jnp.zeros_like(ddt_ref)

        @pl.when(h == 0)
        def _():
            dab_ref[...] = jnp.zeros_like(dab_ref)

        row = i * tm + _iota((tm, 1), 0)
        valid = (row >= row_lo) & (row < row_hi)

        def l2_bwd(c1, dn):
            x1 = _silu(c1)
            r = lax.rsqrt(jnp.sum(x1 * x1, axis=-1, keepdims=True) + L2_EPS)
            dn = jnp.where(valid, dn, 0.0)
            d1 = r * dn - x1 * (r * r * r) * jnp.sum(dn * x1, axis=-1, keepdims=True)
            return d1 * _silu_grad(c1)

        dcq_ref[...] = l2_bwd(_conv_taps(pq_ref[...], pqp_ref[...], cq_ref, first), dqn_ref[...])
        dck_ref[...] = l2_bwd(_conv_taps(pk_ref[...], pkp_ref[...], ck_ref, first), dkn_ref[...])
        cv1 = _conv_taps(pv_ref[...], pvp_ref[...], cv_ref, first)
        dcv_ref[...] = jnp.where(valid, dvv_ref[...], 0.0) * _silu_grad(cv1)
        ab = ab_ref[...]
        da = _lane_pick(ab, h)
        db = _lane_pick(ab, heads + h)
        a = _lane_pick(al_ref[...], h)
        dtb = _lane_pick(dt_ref[...], h)
        dgv = jnp.where(valid, dg_ref[...], 0.0)
        ea = jnp.exp(a)
        g = -ea * _softplus(da + dtb)
        dda = dgv * (-ea) * _sigmoid(da + dtb)
        beta = _sigmoid(db)
        ddb = jnp.where(valid, db_ref[...], 0.0) * beta * (1.0 - beta)
        lane = _iota((tm, LANES), 1)
        dab_ref[...] += jnp.where(lane == h, dda, 0.0) + jnp.where(lane == heads + h, ddb, 0.0)
        lane1 = _iota((1, LANES), 1)
        dal_ref[...] += jnp.where(lane1 == h, jnp.sum(dgv * g, axis=0, keepdims=True), 0.0)
        ddt_ref[...] += jnp.where(lane1 == h, jnp.sum(dda, axis=0, keepdims=True), 0.0)

    def cur(width, col):
        return pl.BlockSpec((tm, width), lambda i, h: (i, col // width + h))

    def prev(width, col):
        return pl.BlockSpec((8, width), lambda i, h: (jnp.maximum(i * (tm // 8) - 1, 0), col // width + h))

    def hd(width):
        return pl.BlockSpec((None, tm, width), lambda i, h: (h, i, 0))

    small = pl.BlockSpec((1, LANES), lambda i, h: (0, 0))
    return pl.pallas_call(
        body, grid=(nb, heads),
        in_specs=[cur(dk, col_q), prev(dk, col_q), cur(dk, col_k), prev(dk, col_k), cur(dv, col_v), prev(dv, col_v),
                  pl.BlockSpec((tm, LANES), lambda i, h: (i, 0)),
                  pl.BlockSpec((cq.shape[0], dk), lambda i, h: (0, h)), pl.BlockSpec((ck.shape[0], dk), lambda i, h: (0, h)),
                  pl.BlockSpec((cv.shape[0], dv), lambda i, h: (0, h)), small, small,
                  hd(dk), hd(dk), hd(dv), hd(1), hd(1)],
        out_specs=[hd(dk), hd(dk), hd(dv), pl.BlockSpec((tm, LANES), lambda i, h: (i, 0)), small, small],
        out_shape=[jax.ShapeDtypeStruct((heads, t, dk), F32), jax.ShapeDtypeStruct((heads, t, dk), F32),
                   jax.ShapeDtypeStruct((heads, t, dv), F32), jax.ShapeDtypeStruct((t, LANES), F32),
                   jax.ShapeDtypeStruct((1, LANES), F32), jax.ShapeDtypeStruct((1, LANES), F32)], name=name,
        compiler_params=_params("arbitrary", "arbitrary"))(
            proj, proj, proj, proj, proj, proj, pab, cq, ck, cv, a_log, dt_bias, dqn, dkn, dvv, dg, dbeta)


def _conv_bwd(proj, dc, cw, *, heads, width, col, name):
    t = proj.shape[0]
    tm = _pick(t, (384, 256, 128))
    nb = t // tm
    nk = cw.shape[0]

    def body(p_ref, pp_ref, d_ref, dn_ref, w_ref, dp_ref, dw_ref):
        i = pl.program_id(1)
        first, last = i == 0, i == nb - 1

        @pl.when(first)
        def _():
            dw_ref[...] = jnp.zeros_like(dw_ref)

        x, d = p_ref[...], d_ref[...]
        dx = d * w_ref[nk - 1:nk, :]
        dw_ref[nk - 1:nk, :] += jnp.sum(d * x, axis=0, keepdims=True)
        for s in range(1, nk):
            dx += _shift_up(d, dn_ref[...], s, last) * w_ref[nk - 1 - s:nk - s, :]
            dw_ref[nk - 1 - s:nk - s, :] += jnp.sum(d * _shift_down(x, pp_ref[...], s, first), axis=0, keepdims=True)
        dp_ref[...] = _bf(dx)

    return pl.pallas_call(
        body, grid=(heads, nb),
        in_specs=[pl.BlockSpec((tm, width), lambda h, i: (i, col // width + h)),
                  pl.BlockSpec((8, width), lambda h, i: (jnp.maximum(i * (tm // 8) - 1, 0), col // width + h)),
                  pl.BlockSpec((None, tm, width), lambda h, i: (h, i, 0)),
                  pl.BlockSpec((None, 8, width), lambda h, i: (h, jnp.minimum((i + 1) * (tm // 8), t // 8 - 1), 0)),
                  pl.BlockSpec((nk, width), lambda h, i: (0, h))],
        out_specs=[pl.BlockSpec((tm, width), lambda h, i: (i, h)), pl.BlockSpec((nk, width), lambda h, i: (0, h))],
        out_shape=[jax.ShapeDtypeStruct((t, heads * width), BF16), jax.ShapeDtypeStruct((nk, heads * width), F32)],
        name=name, compiler_params=_params("parallel", "arbitrary"))(proj, proj, dc, dc, cw)


def _sb_pre(proj, gq, gk, *, heads, dh, col_q, col_k, col_v, name):
    t = proj.shape[0]
    tm = _pick(t, (384, 256, 128))

    def body(q_ref, k_ref, v_ref, gq_ref, gk_ref, qo_ref, ko_ref, vo_ref):
        qo_ref[...] = _bf(_rms_fwd(q_ref[...], gq_ref[...]))
        ko_ref[...] = _bf(_rms_fwd(k_ref[...], gk_ref[...]))
        vo_ref[...] = _bf(v_ref[...])

    def cur(col):
        return pl.BlockSpec((tm, dh), lambda i, h: (i, col // dh + h))

    gspec = pl.BlockSpec((1, dh), lambda i, h: (0, 0))
    ospec = pl.BlockSpec((tm, dh), lambda i, h: (i, h))
    return pl.pallas_call(
        body, grid=(t // tm, heads), in_specs=[cur(col_q), cur(col_k), cur(col_v), gspec, gspec],
        out_specs=[ospec] * 3, out_shape=[jax.ShapeDtypeStruct((t, heads * dh), BF16)] * 3, name=name,
        compiler_params=_params("parallel", "parallel"))(proj, proj, proj, gq, gk)


def _sb_tile(q, kb, i, j, key_lo, scale):
    z = _dot_nt(q, kb) * scale
    qpos = i * SB_BLOCK + _iota((SB_BLOCK, SB_BLOCK), 0)
    kpos = j * SB_BLOCK + _iota((SB_BLOCK, SB_BLOCK), 1)
    vis = (kpos < qpos) & (kpos >= key_lo)
    ls = jnp.minimum(z, 0.0) - jnp.log(1.0 + jnp.exp(-jnp.abs(z)))
    return vis, ls, jnp.where(vis, ls - z, 0.0)


def _sb_fwd(qs, ks, vs, *, heads, dh, key_lo, name):
    t = qs.shape[0]
    nq = t // SB_BLOCK
    assert nq <= LANES
    scale = dh ** -0.5

    def body(q_ref, k_ref, v_ref, o_ref, c_ref):
        i = pl.program_id(1)
        q = q_ref[...]
        later = jnp.where(_iota((SB_BLOCK, SB_BLOCK), 0) > _iota((SB_BLOCK, SB_BLOCK), 1), 1.0, 0.0).astype(BF16)
        lane = _iota((SB_BLOCK, LANES), 1)
        c_ref[...] = jnp.zeros_like(c_ref)

        def step(n, carry):
            acc, run = carry
            j = i - n
            off = pl.multiple_of(j * SB_BLOCK, SB_BLOCK)
            vis, ls, lk = _sb_tile(q, k_ref[pl.ds(off, SB_BLOCK), :], i, j, key_lo, scale)
            wgt = jnp.where(vis, jnp.exp(ls + _dot_exact_r(lk, later) + run), 0.0)
            acc = acc + _dot(_bf(wgt), v_ref[pl.ds(off, SB_BLOCK), :])
            c_ref[...] = jnp.where(lane == j, run, c_ref[...])
            return acc, run + jnp.sum(lk, axis=1, keepdims=True)

        acc, _ = lax.fori_loop(0, i + 1, step, (jnp.zeros((SB_BLOCK, dh), F32), jnp.zeros((SB_BLOCK, 1), F32)))
        o_ref[...] = _bf(acc)

    full = pl.BlockSpec((t, dh), lambda h, i: (0, h))
    return pl.pallas_call(
        body, grid=(heads, nq),
        in_specs=[pl.BlockSpec((SB_BLOCK, dh), lambda h, i: (i, h)), full, full],
        out_specs=[pl.BlockSpec((SB_BLOCK, dh), lambda h, i: (i, h)),
                   pl.BlockSpec((None, SB_BLOCK, LANES), lambda h, i: (h, i, 0))],
        out_shape=[jax.ShapeDtypeStruct((t, heads * dh), BF16), jax.ShapeDtypeStruct((heads, t, LANES), F32)],
        name=name, compiler_params=_params("parallel", "parallel"))(qs, ks, vs)


def _sb_bwd(qs, ks, vs, do, carry, *, heads, dh, key_lo, name):
    t = qs.shape[0]
    nq = t // SB_BLOCK
    scale = dh ** -0.5

    def body(q_ref, k_ref, v_ref, do_ref, c_ref, dq_ref, dk_ref, dv_ref):
        i = pl.program_id(1)

        @pl.when(i == 0)
        def _():
            dk_ref[...] = jnp.zeros_like(dk_ref)
            dv_ref[...] = jnp.zeros_like(dv_ref)

        q, dob, tab = q_ref[...], do_ref[...], c_ref[...]
        r0 = _iota((SB_BLOCK, SB_BLOCK), 0)
        r1 = _iota((SB_BLOCK, SB_BLOCK), 1)
        later = jnp.where(r0 > r1, 1.0, 0.0).astype(BF16)
        earlier = jnp.where(r0 < r1, 1.0, 0.0).astype(BF16)

        def step(j, carry_):
            dq, pre = carry_
            off = pl.multiple_of(j * SB_BLOCK, SB_BLOCK)
            rows = pl.ds(off, SB_BLOCK)
            kb, vb = k_ref[rows, :], v_ref[rows, :]
            vis, ls, lk = _sb_tile(q, kb, i, j, key_lo, scale)
            wgt = jnp.where(vis, jnp.exp(ls + _dot_exact_r(lk, later) + _lane_pick(tab, j)), 0.0)
            e = wgt * _dot_nt(dob, vb)
            before = jnp.where(vis, _dot_exact_r(e, earlier) + pre, 0.0)
            sig = jnp.exp(ls)
            dz = _bf((e * (1.0 - sig) - before * sig) * scale)
            dk_ref[rows, :] += _dot_tn(dz, q)
            dv_ref[rows, :] += _dot_tn(_bf(wgt), dob)
            return dq + _dot(dz, kb), pre + jnp.sum(e, axis=1, keepdims=True)

        dq, _ = lax.fori_loop(0, i + 1, step, (jnp.zeros((SB_BLOCK, dh), F32), jnp.zeros((SB_BLOCK, 1), F32)))
        dq_ref[...] = dq

    full = pl.BlockSpec((t, dh), lambda h, i: (0, h))
    blk = pl.BlockSpec((SB_BLOCK, dh), lambda h, i: (i, h))
    return pl.pallas_call(
        body, grid=(heads, nq),
        in_specs=[blk, full, full, blk, pl.BlockSpec((None, SB_BLOCK, LANES), lambda h, i: (h, i, 0))],
        out_specs=[blk, full, full], out_shape=[jax.ShapeDtypeStruct((t, heads * dh), F32)] * 3, name=name,
        compiler_params=_params("parallel", "arbitrary"))(qs, ks, vs, do, carry)


def _sb_pre_bwd(proj, gq, gk, dq, dk, dv, *, heads, dh, col_q, col_k, name):
    t = proj.shape[0]
    tm = _pick(t, (384, 256, 128))

    def body(q_ref, k_ref, gq_ref, gk_ref, dq_ref, dk_ref, dv_ref, oq_ref, ok_ref, ov_ref, dgq_ref, dgk_ref):
        @pl.when((pl.program_id(0) == 0) & (pl.program_id(1) == 0))
        def _():
            dgq_ref[...] = jnp.zeros_like(dgq_ref)
            dgk_ref[...] = jnp.zeros_like(dgk_ref)

        dq_, gq_r = _rms_bwd(q_ref[...], gq_ref[...], dq_ref[...])
        dk_, gk_r = _rms_bwd(k_ref[...], gk_ref[...], dk_ref[...])
        oq_ref[...] = _bf(dq_)
        ok_ref[...] = _bf(dk_)
        ov_ref[...] = _bf(dv_ref[...])
        dgq_ref[...] += jnp.sum(gq_r, axis=0, keepdims=True)
        dgk_ref[...] += jnp.sum(gk_r, axis=0, keepdims=True)

    def cur(col):
        return pl.BlockSpec((tm, dh), lambda i, h: (i, col // dh + h))

    gspec = pl.BlockSpec((1, dh), lambda i, h: (0, 0))
    ospec = pl.BlockSpec((tm, dh), lambda i, h: (i, h))
    return pl.pallas_call(
        body, grid=(t // tm, heads), in_specs=[cur(col_q), cur(col_k), gspec, gspec, ospec, ospec, ospec],
        out_specs=[ospec, ospec, ospec, gspec, gspec],
        out_shape=[jax.ShapeDtypeStruct((t, heads * dh), BF16)] * 3 + [jax.ShapeDtypeStruct((1, dh), F32)] * 2,
        name=name, compiler_params=_params("arbitrary", "arbitrary"))(proj, proj, gq, gk, dq, dk, dv)


def _exchange(srcs, *, scatter, name):
    n = len(srcs)
    peers = N_DEV - 1

    def body(*refs):
        ins, outs = refs[:n], refs[n:2 * n]
        send_sems, recv_sems, local_sems = refs[2 * n:]
        x, y, c = lax.axis_index("x"), lax.axis_index("y"), lax.axis_index("c")
        me = 4 * x + 2 * y + c
        copies = []
        for a in range(n):
            own = ins[a].at[me] if scatter else ins[a]
            local = pltpu.make_async_copy(own, outs[a].at[me], local_sems.at[a])
            local.start()
            copies.append(local)
            for k in range(1, N_DEV):
                px = (x + (k >> 2 & 1)) % 2
                py = (y + (k >> 1 & 1)) % 2
                pc = (c + (k & 1)) % 2
                src = ins[a].at[4 * px + 2 * py + pc] if scatter else ins[a]
                cp = pltpu.make_async_remote_copy(
                    src_ref=src, dst_ref=outs[a].at[me], send_sem=send_sems.at[a * peers + k - 1],
                    recv_sem=recv_sems.at[a * peers + k - 1], device_id=(px, py, pc), device_id_type=MESH)
                cp.start()
                copies.append(cp)
        for cp in copies:
            cp.wait()

    any_spec = pl.BlockSpec(memory_space=pl.ANY)
    out_shape = [jax.ShapeDtypeStruct(s.shape if scatter else (N_DEV,) + s.shape, s.dtype) for s in srcs]
    return pl.pallas_call(
        body, in_specs=[any_spec] * n, out_specs=[any_spec] * n, out_shape=out_shape,
        scratch_shapes=[pltpu.SemaphoreType.DMA((n * peers,)), pltpu.SemaphoreType.DMA((n * peers,)),
                        pltpu.SemaphoreType.DMA((n,))],
        name=name, compiler_params=pltpu.CompilerParams(has_side_effects=True))(*srcs)


def _adam_math(g, w, m, v):
    m2 = ADAM_B1 * m + (1.0 - ADAM_B1) * g
    v2 = ADAM_B2 * v + (1.0 - ADAM_B2) * (g * g)
    m_hat = m2 / (1.0 - ADAM_B1 ** ADAM_STEP)
    v_hat = v2 / (1.0 - ADAM_B2 ** ADAM_STEP)
    return -ADAM_LR * (m_hat / (jnp.sqrt(v_hat) + ADAM_EPS) + ADAM_WD * w), m2, v2


def _adamw_slabs(slabs, w, m, v, *, name):
    r, c = w.shape
    tr = _pick(r, (128, 64, 32, 16, 8)) if r % 8 == 0 else r

    def body(s_ref, w_ref, m_ref, v_ref, g_ref, d_ref, mo_ref, vo_ref):
        g = s_ref[0].astype(F32)
        for p in range(1, N_DEV):
            g = g + s_ref[p].astype(F32)
        g_ref[...] = g
        d_ref[...], mo_ref[...], vo_ref[...] = _adam_math(g, w_ref[...], m_ref[...], v_ref[...])

    spec = pl.BlockSpec((tr, c), lambda i: (i, 0))
    return pl.pallas_call(
        body, grid=(r // tr,), in_specs=[pl.BlockSpec((N_DEV, tr, c), lambda i: (0, i, 0)), spec, spec, spec],
        out_specs=[spec] * 4, out_shape=[jax.ShapeDtypeStruct((r, c), F32)] * 4, name=name,
        compiler_params=_params("parallel"))(slabs, w, m, v)


def _adamw_small(g, w, m, v, *, name):
    def body(g_ref, w_ref, m_ref, v_ref, d_ref, mo_ref, vo_ref):
        d_ref[...], mo_ref[...], vo_ref[...] = _adam_math(g_ref[...], w_ref[...], m_ref[...], v_ref[...])

    return pl.pallas_call(body, out_shape=[jax.ShapeDtypeStruct(w.shape, F32)] * 3, name=name)(g, w, m, v)


def _sum_slabs(slabs, *, name):
    def body(s_ref, o_ref):
        acc = s_ref[0]
        for p in range(1, N_DEV):
            acc = acc + s_ref[p]
        o_ref[...] = acc

    return pl.pallas_call(body, out_shape=jax.ShapeDtypeStruct(slabs.shape[1:], F32), name=name)(slabs)


def _gather_cols(g):
    return jnp.transpose(g, (1, 0, 2)).reshape(g.shape[1], -1)


def _col_slabs(a):
    return jnp.transpose(a.reshape(a.shape[0], N_DEV, -1), (1, 0, 2))


def _pad_lanes(a):
    return jnp.pad(a, ((0, 0), (0, LANES - a.shape[1])))


def _local_step(x, target, meta, g_mix, w_main, w_ab, cq, ck, cv, a_log, dt_bias, g_dn, g_sbq, g_sbk,
                w_bd, w_bs, w_out, g_ffn, w_fg, w_fu, w_fo):
    seq, d = x.shape
    n_meta = meta.shape[0]
    heads = a_log.shape[1]
    qk = cq.shape[1]
    dvt = cv.shape[1]
    dk, dv = qk // heads, dvt // heads
    dh = g_sbq.shape[1]
    sbw = w_bs.shape[0]
    sb_heads = sbw // dh
    pad_l = (-n_meta) % CHUNK
    row_x = pad_l + n_meta
    rows = row_x + seq
    t = -(-rows // GDN_ROWS) * GDN_ROWS
    col_q, col_k, col_v, col_z = 0, qk, 2 * qk, 2 * qk + dvt
    col_sq = 2 * qk + 2 * dvt
    col_sk, col_sv, col_gd, col_gs = col_sq + sbw, col_sq + 2 * sbw, col_sq + 3 * sbw, col_sq + 3 * sbw + d

    def rows_pad(a):
        return jnp.concatenate([jnp.zeros((row_x, d), F32), a, jnp.zeros((t - rows, d), F32)], axis=0)

    h0 = jnp.concatenate([jnp.zeros((pad_l, d), F32), meta, x, jnp.zeros((t - rows, d), F32)], axis=0)
    tgt = rows_pad(target)
    a_log_p, dt_p = _pad_lanes(a_log), _pad_lanes(dt_bias)

    proj, n1 = _mm_norm(h0, g_mix, w_main, name="proj")
    pab = _mm_nn(n1, w_ab, out_dtype=F32, name="proj_ab")
    gk = dict(heads=heads, dk=dk, dv=dv, col_q=col_q, col_k=col_k, col_v=col_v, row_lo=pad_l, row_hi=rows)
    qn, kn, vv, g, beta = _gdn_pre(proj, pab, cq, ck, cv, a_log_p, dt_p, name="gdn_pre", **gk)
    u, w, pm, qd, kd, egl, tinv = _gdn_prep(qn, kn, vv, g, beta, name="gdn_prep")
    o_raw, o_dn, states = _gdn_scan(u, w, pm, qd, kd, egl, proj, g_dn, col_z=col_z, name="gdn_scan")
    qs, ks, vs = _sb_pre(proj, g_sbq, g_sbk, heads=sb_heads, dh=dh, col_q=col_sq, col_k=col_sk, col_v=col_sv,
                         name="sb_pre")
    o_sb, carry = _sb_fwd(qs, ks, vs, heads=sb_heads, dh=dh, key_lo=pad_l, name="sb_fwd")
    merged, br_dn, br_sb = _merge_fwd(o_dn, o_sb, w_bd, w_bs, proj, col_gd=col_gd, col_gs=col_gs, name="merge")
    h1 = _mm_res(h0, merged, w_out, name="mix_out")
    gate, up, act, n2 = _mm_norm_swiglu(h1, g_ffn, w_fg, w_fu, name="ffn_in")
    dy, dyb, lsum = _mm_res_loss(h1, act, w_fo, tgt, row0=row_x, nrows=seq, name="ffn_out_loss")

    dgate, dup = _swiglu_bwd(dyb, w_fo, gate, up, name="ffn_out_bwd")
    d_w_fo = _mm_tn(act, dyb, name="dw_ffn_out")
    d_w_fg = _mm_tn(n2, dgate, name="dw_ffn_gate")
    d_w_fu = _mm_tn(n2, dup, name="dw_ffn_up")
    dh1, dh1b, d_g_ffn = _mm_nt_rmsbwd([(dgate, w_fg), (dup, w_fu)], None, h1, g_ffn, dy, name="ffn_in_bwd")

    dbd, dbs, dgd, dgs = _merge_bwd(dh1b, w_out, proj, br_dn, br_sb, col_gd=col_gd, col_gs=col_gs, name="mix_out_bwd")
    d_w_out = _mm_tn(merged, dh1b, name="dw_out")
    d_w_bd = _mm_tn(o_dn, dbd, name="dw_branch_dn")
    d_w_bs = _mm_tn(o_sb, dbs, name="dw_branch_sb")
    do_dn = _mm_nt(dbd, w_bd, out_dtype=F32, name="branch_dn_bwd")
    do_sb = _mm_nt(dbs, w_bs, out_dtype=BF16, name="branch_sb_bwd")

    do_raw, dz, d_g_dn = _gdn_post_bwd(o_raw, proj, g_dn, do_dn, col_z=col_z, name="gdn_post_bwd")
    du, dw, dp, dqd, dkd, dgl = _gdn_bwd_scan(u, w, pm, qd, kd, egl, states, do_raw, name="gdn_bwd_scan")
    dqn, dkn, dvv, dg, dbeta = _gdn_bwd_prep(qn, kn, vv, g, beta, tinv, u, w, du, dw, dp, dqd, dkd, dgl,
                                            name="gdn_bwd_prep")
    dcq, dck, dcv, dpab, d_a_log, d_dt = _gdn_pre_bwd_a(proj, pab, cq, ck, cv, a_log_p, dt_p, dqn, dkn, dvv, dg, dbeta,
                                                        name="gdn_pre_bwd", **gk)
    dpq, d_cq = _conv_bwd(proj, dcq, cq, heads=heads, width=dk, col=col_q, name="conv_q_bwd")
    dpk, d_ck = _conv_bwd(proj, dck, ck, heads=heads, width=dk, col=col_k, name="conv_k_bwd")
    dpv, d_cv = _conv_bwd(proj, dcv, cv, heads=heads, width=dv, col=col_v, name="conv_v_bwd")

    dqs, dks, dvs = _sb_bwd(qs, ks, vs, do_sb, carry, heads=sb_heads, dh=dh, key_lo=pad_l, name="sb_bwd")
    dsq, dsk, dsv, d_g_sbq, d_g_sbk = _sb_pre_bwd(proj, g_sbq, g_sbk, dqs, dks, dvs, heads=sb_heads, dh=dh,
                                                   col_q=col_sq, col_k=col_sk, name="sb_pre_bwd")

    dproj = jnp.concatenate([dpq, dpk, dpv, dz, dsq, dsk, dsv, dgd, dgs], axis=1)
    dpab_b = _bf(dpab)
    d_w_main = _mm_tn(n1, dproj, name="dw_in_main")
    d_w_ab = _mm_tn(n1, dpab_b, name="dw_in_ab")
    dh0, _, d_g_mix = _mm_nt_rmsbwd([(dproj, w_main)], (dpab_b, w_ab), h0, g_mix, dh1, name="proj_bwd")

    return dict(lsum=lsum, grad_x=dh0[row_x:rows], d_meta=dh0[pad_l:row_x], d_g_mix=d_g_mix, d_w_main=d_w_main,
                d_w_ab=d_w_ab, d_cq=d_cq, d_ck=d_ck, d_cv=d_cv, d_a_log=d_a_log[:, :heads], d_dt=d_dt[:, :heads],
                d_g_dn=d_g_dn, d_g_sbq=d_g_sbq, d_g_sbk=d_g_sbk, d_w_bd=d_w_bd, d_w_bs=d_w_bs, d_w_out=d_w_out,
                d_g_ffn=d_g_ffn, d_w_fg=d_w_fg, d_w_fu=d_w_fu, d_w_fo=d_w_fo)


def _pack(parts):
    flat = []
    for a in parts:
        a = a.reshape(-1)
        flat.append(jnp.pad(a, (0, (-a.shape[0]) % LANES)))
    v = jnp.concatenate(flat)
    v = jnp.pad(v, (0, (-v.shape[0]) % (8 * LANES)))
    return v.reshape(-1, LANES)


def _unpack(packed, shapes):
    flat = packed.reshape(-1)
    out, pos = [], 0
    for s in shapes:
        n = math.prod(s)
        out.append(flat[pos:pos + n].reshape(s))
        pos += n + (-n) % LANES
    return out


def kernel(x, meta_tokens, norm_mix_gain, w_in, conv_q, conv_k, conv_v, dn_a_log, dn_dt_bias, dn_out_norm_gain, sb_q_norm_gain, sb_k_norm_gain, w_branch_dn, w_branch_sb, w_out, norm_ffn_gain, w_ffn_in, w_ffn_out, loss_target, m_meta_tokens, m_norm_mix_gain, m_w_in, m_conv_q, m_conv_k, m_conv_v, m_dn_a_log, m_dn_dt_bias, m_dn_out_norm_gain, m_sb_q_norm_gain, m_sb_k_norm_gain, m_w_branch_dn, m_w_branch_sb, m_w_out, m_norm_ffn_gain, m_w_ffn_in, m_w_ffn_out, v_meta_tokens, v_norm_mix_gain, v_w_in, v_conv_q, v_conv_k, v_conv_v, v_dn_a_log, v_dn_dt_bias, v_dn_out_norm_gain, v_sb_q_norm_gain, v_sb_k_norm_gain, v_w_branch_dn, v_w_branch_sb, v_w_out, v_norm_ffn_gain, v_w_ffn_in, v_w_ffn_out):
    me = 4 * lax.axis_index("x") + 2 * lax.axis_index("y") + lax.axis_index("c")
    heads = dn_a_log.shape[1]
    d = x.shape[2]
    qk = conv_q.shape[2] * N_DEV
    dvt = conv_v.shape[2] * N_DEV
    d_ff = w_ffn_out.shape[1] * N_DEV
    n_meta = meta_tokens.shape[0]
    col_ab = 2 * qk + 2 * dvt

    small_shapes = [meta_tokens.shape, conv_q.shape[1:], conv_k.shape[1:], conv_v.shape[1:]]
    small = _pack([meta_tokens, conv_q[0], conv_k[0], conv_v[0]])
    g_in, g_fi, g_bd, g_bs, g_out, g_fo, g_small = _exchange(
        [_bf(w_in[0]), _bf(w_ffn_in[0]), _bf(w_branch_dn[0]), _bf(w_branch_sb[0]), _bf(w_out[0]), _bf(w_ffn_out[0]), small],
        scatter=False, name="gather_weights")
    w_full = _gather_cols(g_in)
    w_main = jnp.concatenate([w_full[:, :col_ab], w_full[:, col_ab + 2 * heads:]], axis=1)
    w_ab = _pad_lanes(w_full[:, col_ab:col_ab + 2 * heads])
    w_fi = _gather_cols(g_fi)
    parts = [_unpack(g_small[p], small_shapes) for p in range(N_DEV)]
    meta_f, cq_f, ck_f, cv_f = (jnp.concatenate([parts[p][a] for p in range(N_DEV)], axis=1) for a in range(4))

    r = _local_step(x[0], loss_target[0], meta_f, norm_mix_gain, w_main, w_ab, cq_f, ck_f, cv_f, dn_a_log, dn_dt_bias,
                    dn_out_norm_gain, sb_q_norm_gain, sb_k_norm_gain, g_bd.reshape(-1, d), g_bs.reshape(-1, d),
                    g_out.reshape(-1, d), norm_ffn_gain, w_fi[:, :d_ff], w_fi[:, d_ff:], g_fo.reshape(-1, d))

    d_w_in = jnp.concatenate([r["d_w_main"][:, :col_ab], r["d_w_ab"][:, :2 * heads], r["d_w_main"][:, col_ab:]], axis=1)
    d_w_fi = jnp.concatenate([r["d_w_fg"], r["d_w_fu"]], axis=1)
    loss_part = (0.5 / d) * jnp.sum(r["lsum"], axis=1, keepdims=True)
    small_g = [r["d_meta"], r["d_g_mix"], r["d_cq"], r["d_ck"], r["d_cv"], r["d_a_log"], r["d_dt"], r["d_g_dn"],
               r["d_g_sbq"], r["d_g_sbk"], r["d_g_ffn"], loss_part]
    s_in, s_fi, s_bd, s_bs, s_out, s_fo = _exchange(
        [_bf(_col_slabs(d_w_in)), _bf(_col_slabs(d_w_fi)), _bf(r["d_w_bd"]).reshape(N_DEV, -1, d),
         _bf(r["d_w_bs"]).reshape(N_DEV, -1, d), _bf(r["d_w_out"]).reshape(N_DEV, -1, d),
         _bf(r["d_w_fo"]).reshape(N_DEV, -1, d)], scatter=True, name="scatter_grads")
    (g_packs,) = _exchange([_pack(small_g)], scatter=False, name="gather_small_grads")
    (g_meta, g_mix, g_cq, g_ck, g_cv, g_al, g_dt, g_gdn, g_sbq, g_sbk, g_ffn, loss) = _unpack(
        _sum_slabs(g_packs, name="sum_small_grads"), [a.shape for a in small_g])

    def mine(a, width):
        return lax.dynamic_slice_in_dim(a, me * width, width, axis=1)

    big = dict(w_in=(s_in, w_in, m_w_in, v_w_in), w_branch_dn=(s_bd, w_branch_dn, m_w_branch_dn, v_w_branch_dn),
               w_branch_sb=(s_bs, w_branch_sb, m_w_branch_sb, v_w_branch_sb), w_out=(s_out, w_out, m_w_out, v_w_out),
               w_ffn_in=(s_fi, w_ffn_in, m_w_ffn_in, v_w_ffn_in), w_ffn_out=(s_fo, w_ffn_out, m_w_ffn_out, v_w_ffn_out))
    tiny = dict(meta_tokens=(mine(g_meta, d // N_DEV), meta_tokens, m_meta_tokens, v_meta_tokens),
                norm_mix_gain=(g_mix, norm_mix_gain, m_norm_mix_gain, v_norm_mix_gain),
                conv_q=(mine(g_cq, qk // N_DEV), conv_q[0], m_conv_q[0], v_conv_q[0]),
                conv_k=(mine(g_ck, qk // N_DEV), conv_k[0], m_conv_k[0], v_conv_k[0]),
                conv_v=(mine(g_cv, dvt // N_DEV), conv_v[0], m_conv_v[0], v_conv_v[0]),
                dn_a_log=(g_al, dn_a_log, m_dn_a_log, v_dn_a_log), dn_dt_bias=(g_dt, dn_dt_bias, m_dn_dt_bias, v_dn_dt_bias),
                dn_out_norm_gain=(g_gdn, dn_out_norm_gain, m_dn_out_norm_gain, v_dn_out_norm_gain),
                sb_q_norm_gain=(g_sbq, sb_q_norm_gain, m_sb_q_norm_gain, v_sb_q_norm_gain),
                sb_k_norm_gain=(g_sbk, sb_k_norm_gain, m_sb_k_norm_gain, v_sb_k_norm_gain),
                norm_ffn_gain=(g_ffn, norm_ffn_gain, m_norm_ffn_gain, v_norm_ffn_gain))
    order = ["meta_tokens", "norm_mix_gain", "w_in", "conv_q", "conv_k", "conv_v", "dn_a_log", "dn_dt_bias",
             "dn_out_norm_gain", "sb_q_norm_gain", "sb_k_norm_gain", "w_branch_dn", "w_branch_sb", "w_out",
             "norm_ffn_gain", "w_ffn_in", "w_ffn_out"]
    grads, deltas, new_m, new_v = [], [], [], []
    for name in order:
        if name in big:
            slabs, w, m, v = big[name]
            g, dl, mo, vo = _adamw_slabs(slabs, w[0], m[0], v[0], name="adamw_" + name)
            like = w.shape
        else:
            g, w, m, v = tiny[name]
            like = dict(conv_q=conv_q, conv_k=conv_k, conv_v=conv_v).get(name, w).shape
            dl, mo, vo = _adamw_small(g, w, m, v, name="adamw_" + name)
        for lst, a in ((grads, g), (deltas, dl), (new_m, mo), (new_v, vo)):
            lst.append(a.reshape(like))
    return (loss.reshape(()), r["grad_x"][None], *grads, *deltas, *new_m, *new_v)
```

```python
import functools
import math

import jax
import jax.numpy as jnp
from jax import lax
from jax.experimental import pallas as pl
from jax.experimental.pallas import tpu as pltpu

F32 = jnp.float32
BF16 = jnp.bfloat16

N_DEV = 8
CHUNK = 64
CHUNK_SHIFT = 6
GDN_ROWS = 2 * CHUNK
SCAN_HEADS = 2
SB_BLOCK = 128
LANES = 128
RMS_EPS = 1e-6
L2_EPS = 1e-6
ADAM_LR = 0.001
ADAM_B1 = 0.9
ADAM_B2 = 0.999
ADAM_EPS = 1e-08
ADAM_WD = 0.01
ADAM_STEP = 10
V7X_VMEM_LIMIT_BYTES = 56 * 1024 * 1024

MESH = pl.DeviceIdType.MESH


def _params(*sem):
    return pltpu.CompilerParams(dimension_semantics=sem or None, vmem_limit_bytes=V7X_VMEM_LIMIT_BYTES)


def _pick(n, cands):
    for c in cands:
        if n % c == 0:
            return c
    raise ValueError(f"no block size among {cands} divides {n}")


def _bf(x):
    return x.astype(BF16)


def _dot(a, b):
    return jnp.dot(a, b, preferred_element_type=F32)


def _dot_nt(a, b):
    return lax.dot_general(a, b, (((1,), (1,)), ((), ())), preferred_element_type=F32)


def _dot_tn(a, b):
    return lax.dot_general(a, b, (((0,), (0,)), ((), ())), preferred_element_type=F32)


def _split2(x):
    hi = _bf(x)
    return hi, _bf(x - hi.astype(F32))


def _split3(x):
    hi = _bf(x)
    r = x - hi.astype(F32)
    mid = _bf(r)
    return hi, mid, _bf(r - mid.astype(F32))


def _dot_hp(a, b, dot=_dot):
    ah, al = _split2(a)
    bh, bl = _split2(b)
    return dot(ah, bh) + dot(ah, bl) + dot(al, bh)


def _dot_exact_r(x, m, dot=_dot):
    h, mi, lo = _split3(x)
    return dot(h, m) + dot(mi, m) + dot(lo, m)


def _dot_exact_l(m, x, dot=_dot):
    h, mi, lo = _split3(x)
    return dot(m, h) + dot(m, mi) + dot(m, lo)


def _sigmoid(x):
    return 1.0 / (1.0 + jnp.exp(-x))


def _silu(x):
    return x * _sigmoid(x)


def _silu_grad(x):
    s = _sigmoid(x)
    return s * (1.0 + x * (1.0 - s))


def _softplus(x):
    return jnp.maximum(x, 0.0) + jnp.log(1.0 + jnp.exp(-jnp.abs(x)))


def _rms_fwd(h, gain):
    r = lax.rsqrt(jnp.mean(h * h, axis=-1, keepdims=True) + RMS_EPS)
    return h * r * gain


def _rms_bwd(h, gain, dy):
    r = lax.rsqrt(jnp.mean(h * h, axis=-1, keepdims=True) + RMS_EPS)
    dyg = dy * gain
    dh = r * dyg - h * (r * r * r) * jnp.mean(dyg * h, axis=-1, keepdims=True)
    return dh, dy * h * r


def _iota(shape, dim):
    return lax.broadcasted_iota(jnp.int32, shape, dim)


def _lane_pick(x, idx):
    return jnp.sum(jnp.where(_iota(x.shape, 1) == idx, x, 0.0), axis=1, keepdims=True)


def _mm_nn(a, b, *, out_dtype, name):
    m, k = a.shape
    n = b.shape[1]
    tm, tn = _pick(m, (384, 256, 128)), _pick(n, (1024, 512, 256, 128))

    def body(a_ref, b_ref, o_ref):
        o_ref[...] = _dot(a_ref[...], b_ref[...]).astype(out_dtype)

    return pl.pallas_call(
        body, grid=(m // tm, n // tn),
        in_specs=[pl.BlockSpec((tm, k), lambda i, j: (i, 0)), pl.BlockSpec((k, tn), lambda i, j: (0, j))],
        out_specs=pl.BlockSpec((tm, tn), lambda i, j: (i, j)),
        out_shape=jax.ShapeDtypeStruct((m, n), out_dtype), name=name,
        compiler_params=_params("parallel", "parallel"))(a, b)


def _mm_nt(a, b, *, out_dtype, name):
    m, k = a.shape
    n = b.shape[0]
    tm, tn = _pick(m, (384, 256, 128)), _pick(n, (1024, 512, 256, 128))

    def body(a_ref, b_ref, o_ref):
        o_ref[...] = _dot_nt(a_ref[...], b_ref[...]).astype(out_dtype)

    return pl.pallas_call(
        body, grid=(m // tm, n // tn),
        in_specs=[pl.BlockSpec((tm, k), lambda i, j: (i, 0)), pl.BlockSpec((tn, k), lambda i, j: (j, 0))],
        out_specs=pl.BlockSpec((tm, tn), lambda i, j: (i, j)),
        out_shape=jax.ShapeDtypeStruct((m, n), out_dtype), name=name,
        compiler_params=_params("parallel", "parallel"))(a, b)


def _mm_tn(a, b, *, name):
    t, m = a.shape
    n = b.shape[1]
    tm = _pick(m, (1024, 512, 256, 128))
    tn = _pick(n, (1024, 512, 256, 128))
    tk = _pick(t, (384, 256, 128))

    def body(a_ref, b_ref, o_ref):
        @pl.when(pl.program_id(2) == 0)
        def _():
            o_ref[...] = jnp.zeros_like(o_ref)

        o_ref[...] += _dot_tn(a_ref[...], b_ref[...])

    return pl.pallas_call(
        body, grid=(m // tm, n // tn, t // tk),
        in_specs=[pl.BlockSpec((tk, tm), lambda i, j, k: (k, i)), pl.BlockSpec((tk, tn), lambda i, j, k: (k, j))],
        out_specs=pl.BlockSpec((tm, tn), lambda i, j, k: (i, j)),
        out_shape=jax.ShapeDtypeStruct((m, n), F32), name=name,
        compiler_params=_params("parallel", "parallel", "arbitrary"))(a, b)


def _mm_norm(h, gain, w, *, name):
    m, k = h.shape
    n = w.shape[1]
    tm, tn = _pick(m, (384, 256, 128)), _pick(n, (1024, 512, 256, 128))

    def body(h_ref, g_ref, w_ref, o_ref, n_ref):
        @pl.when(pl.program_id(1) == 0)
        def _():
            n_ref[...] = _bf(_rms_fwd(h_ref[...], g_ref[...]))

        o_ref[...] = _dot(n_ref[...], w_ref[...])

    return pl.pallas_call(
        body, grid=(m // tm, n // tn),
        in_specs=[pl.BlockSpec((tm, k), lambda i, j: (i, 0)), pl.BlockSpec((1, k), lambda i, j: (0, 0)),
                  pl.BlockSpec((k, tn), lambda i, j: (0, j))],
        out_specs=[pl.BlockSpec((tm, tn), lambda i, j: (i, j)), pl.BlockSpec((tm, k), lambda i, j: (i, 0))],
        out_shape=[jax.ShapeDtypeStruct((m, n), F32), jax.ShapeDtypeStruct((m, k), BF16)], name=name,
        compiler_params=_params("parallel", "arbitrary"))(h, gain, w)


def _mm_norm_swiglu(h, gain, wg, wu, *, name):
    m, k = h.shape
    n = wg.shape[1]
    tm, tn = _pick(m, (384, 256, 128)), _pick(n, (256, 128))

    def body(h_ref, g_ref, wg_ref, wu_ref, gate_ref, up_ref, act_ref, n_ref):
        @pl.when(pl.program_id(1) == 0)
        def _():
            n_ref[...] = _bf(_rms_fwd(h_ref[...], g_ref[...]))

        gate = _dot(n_ref[...], wg_ref[...])
        up = _dot(n_ref[...], wu_ref[...])
        gate_ref[...] = gate
        up_ref[...] = up
        act_ref[...] = _bf(_silu(gate) * up)

    wspec = pl.BlockSpec((k, tn), lambda i, j: (0, j))
    ospec = pl.BlockSpec((tm, tn), lambda i, j: (i, j))
    return pl.pallas_call(
        body, grid=(m // tm, n // tn),
        in_specs=[pl.BlockSpec((tm, k), lambda i, j: (i, 0)), pl.BlockSpec((1, k), lambda i, j: (0, 0)), wspec, wspec],
        out_specs=[ospec, ospec, ospec, pl.BlockSpec((tm, k), lambda i, j: (i, 0))],
        out_shape=[jax.ShapeDtypeStruct((m, n), F32), jax.ShapeDtypeStruct((m, n), F32),
                   jax.ShapeDtypeStruct((m, n), BF16), jax.ShapeDtypeStruct((m, k), BF16)], name=name,
        compiler_params=_params("parallel", "arbitrary"))(h, gain, wg, wu)


def _mm_res(res, a, b, *, name):
    m, k = a.shape
    n = b.shape[1]
    tm, tn = _pick(m, (384, 256, 128)), _pick(n, (1024, 512, 256, 128))

    def body(r_ref, a_ref, b_ref, o_ref):
        o_ref[...] = r_ref[...] + _dot(a_ref[...], b_ref[...])

    return pl.pallas_call(
        body, grid=(m // tm, n // tn),
        in_specs=[pl.BlockSpec((tm, tn), lambda i, j: (i, j)), pl.BlockSpec((tm, k), lambda i, j: (i, 0)),
                  pl.BlockSpec((k, tn), lambda i, j: (0, j))],
        out_specs=pl.BlockSpec((tm, tn), lambda i, j: (i, j)),
        out_shape=jax.ShapeDtypeStruct((m, n), F32), name=name,
        compiler_params=_params("parallel", "parallel"))(res, a, b)


def _mm_res_loss(res, a, b, target, *, row0, nrows, name):
    m, k = a.shape
    n = b.shape[1]
    tm = _pick(m, (384, 256, 128))

    def body(r_ref, a_ref, b_ref, t_ref, dy_ref, dyb_ref, ls_ref):
        i = pl.program_id(0)

        @pl.when(i == 0)
        def _():
            ls_ref[...] = jnp.zeros_like(ls_ref)

        y = r_ref[...] + _dot(a_ref[...], b_ref[...])
        row = i * tm + _iota((tm, n), 0)
        e = jnp.where((row >= row0) & (row < row0 + nrows), y - t_ref[...], 0.0)
        dy = e / n
        dy_ref[...] = dy
        dyb_ref[...] = _bf(dy)
        ls_ref[...] += jnp.sum(e * e, axis=0, keepdims=True)

    rspec = pl.BlockSpec((tm, n), lambda i: (i, 0))
    return pl.pallas_call(
        body, grid=(m // tm,),
        in_specs=[rspec, pl.BlockSpec((tm, k), lambda i: (i, 0)), pl.BlockSpec((k, n), lambda i: (0, 0)), rspec],
        out_specs=[rspec, rspec, pl.BlockSpec((1, n), lambda i: (0, 0))],
        out_shape=[jax.ShapeDtypeStruct((m, n), F32), jax.ShapeDtypeStruct((m, n), BF16),
                   jax.ShapeDtypeStruct((1, n), F32)], name=name,
        compiler_params=_params("arbitrary"))(res, a, b, target)


def _merge_fwd(o_dn, o_sb, wbd, wbs, proj, *, col_gd, col_gs, name):
    m, kd = o_dn.shape
    ks = o_sb.shape[1]
    n = wbd.shape[1]
    tm = _pick(m, (384, 256, 128))
    tn = _pick(math.gcd(n, math.gcd(col_gd, col_gs)), (512, 256, 128))

    def body(od_ref, os_ref, wd_ref, ws_ref, gd_ref, gs_ref, mg_ref, bd_ref, bs_ref):
        bd = _dot(od_ref[...], wd_ref[...])
        bs = _dot(os_ref[...], ws_ref[...])
        bd_ref[...] = bd
        bs_ref[...] = bs
        mg_ref[...] = _bf(_sigmoid(gd_ref[...]) * bd + _sigmoid(gs_ref[...]) * bs)

    ospec = pl.BlockSpec((tm, tn), lambda i, j: (i, j))
    return pl.pallas_call(
        body, grid=(m // tm, n // tn),
        in_specs=[pl.BlockSpec((tm, kd), lambda i, j: (i, 0)), pl.BlockSpec((tm, ks), lambda i, j: (i, 0)),
                  pl.BlockSpec((kd, tn), lambda i, j: (0, j)), pl.BlockSpec((ks, tn), lambda i, j: (0, j)),
                  pl.BlockSpec((tm, tn), lambda i, j: (i, col_gd // tn + j)),
                  pl.BlockSpec((tm, tn), lambda i, j: (i, col_gs // tn + j))],
        out_specs=[ospec, ospec, ospec],
        out_shape=[jax.ShapeDtypeStruct((m, n), BF16), jax.ShapeDtypeStruct((m, n), F32),
                   jax.ShapeDtypeStruct((m, n), F32)], name=name,
        compiler_params=_params("parallel", "parallel"))(o_dn, o_sb, wbd, wbs, proj, proj)


def _merge_bwd(dh, w_out, proj, br_dn, br_sb, *, col_gd, col_gs, name):
    m, k = dh.shape
    n = w_out.shape[0]
    tm = _pick(m, (384, 256, 128))
    tn = _pick(math.gcd(n, math.gcd(col_gd, col_gs)), (512, 256, 128))

    def body(dh_ref, w_ref, gd_ref, gs_ref, bd_ref, bs_ref, dbd_ref, dbs_ref, dgd_ref, dgs_ref):
        dm = _dot_nt(dh_ref[...], w_ref[...])
        sd = _sigmoid(gd_ref[...])
        ss = _sigmoid(gs_ref[...])
        dbd_ref[...] = _bf(dm * sd)
        dbs_ref[...] = _bf(dm * ss)
        dgd_ref[...] = _bf(dm * bd_ref[...] * sd * (1.0 - sd))
        dgs_ref[...] = _bf(dm * bs_ref[...] * ss * (1.0 - ss))

    ospec = pl.BlockSpec((tm, tn), lambda i, j: (i, j))
    return pl.pallas_call(
        body, grid=(m // tm, n // tn),
        in_specs=[pl.BlockSpec((tm, k), lambda i, j: (i, 0)), pl.BlockSpec((tn, k), lambda i, j: (j, 0)),
                  pl.BlockSpec((tm, tn), lambda i, j: (i, col_gd // tn + j)),
                  pl.BlockSpec((tm, tn), lambda i, j: (i, col_gs // tn + j)), ospec, ospec],
        out_specs=[ospec] * 4,
        out_shape=[jax.ShapeDtypeStruct((m, n), BF16)] * 4, name=name,
        compiler_params=_params("parallel", "parallel"))(dh, w_out, proj, proj, br_dn, br_sb)


def _swiglu_bwd(dy, wfo, gate, up, *, name):
    m, k = dy.shape
    n = wfo.shape[0]
    tm, tn = _pick(m, (384, 256, 128)), _pick(n, (256, 128))

    def body(dy_ref, w_ref, g_ref, u_ref, dg_ref, du_ref):
        da = _dot_nt(dy_ref[...], w_ref[...])
        g = g_ref[...]
        dg_ref[...] = _bf(da * u_ref[...] * _silu_grad(g))
        du_ref[...] = _bf(da * _silu(g))

    ospec = pl.BlockSpec((tm, tn), lambda i, j: (i, j))
    return pl.pallas_call(
        body, grid=(m // tm, n // tn),
        in_specs=[pl.BlockSpec((tm, k), lambda i, j: (i, 0)), pl.BlockSpec((tn, k), lambda i, j: (j, 0)), ospec, ospec],
        out_specs=[ospec, ospec], out_shape=[jax.ShapeDtypeStruct((m, n), BF16)] * 2, name=name,
        compiler_params=_params("parallel", "parallel"))(dy, wfo, gate, up)


def _mm_nt_rmsbwd(pairs, extra, h, gain, dres, *, name):
    m, k = pairs[0][0].shape
    n = h.shape[1]
    tm = _pick(m, (384, 256, 128))
    tk = _pick(k, (1408, 1024, 512, 256, 128))
    nk = k // tk
    np_ = len(pairs)

    def body(*refs):
        ab = refs[:2 * np_]
        pos = 2 * np_
        ex = refs[pos:pos + 2] if extra is not None else ()
        pos += len(ex)
        h_ref, g_ref, r_ref, dh_ref, dhb_ref, dg_ref, acc_ref = refs[pos:]
        i, kk = pl.program_id(0), pl.program_id(1)

        @pl.when((i == 0) & (kk == 0))
        def _():
            dg_ref[...] = jnp.zeros_like(dg_ref)

        part = _dot_nt(ab[0][...], ab[1][...])
        for p in range(1, np_):
            part += _dot_nt(ab[2 * p][...], ab[2 * p + 1][...])

        @pl.when(kk == 0)
        def _():
            first = part
            if ex:
                first = first + _dot_nt(ex[0][...], ex[1][...])
            acc_ref[...] = first

        @pl.when(kk > 0)
        def _():
            acc_ref[...] += part

        @pl.when(kk == nk - 1)
        def _():
            dh, dgr = _rms_bwd(h_ref[...], g_ref[...], acc_ref[...])
            dh = dh + r_ref[...]
            dh_ref[...] = dh
            dhb_ref[...] = _bf(dh)
            dg_ref[...] += jnp.sum(dgr, axis=0, keepdims=True)

    in_specs, args = [], []
    for a, b in pairs:
        in_specs += [pl.BlockSpec((tm, tk), lambda i, kk: (i, kk)), pl.BlockSpec((n, tk), lambda i, kk: (0, kk))]
        args += [a, b]
    if extra is not None:
        k2 = extra[0].shape[1]
        in_specs += [pl.BlockSpec((tm, k2), lambda i, kk: (i, 0)), pl.BlockSpec((n, k2), lambda i, kk: (0, 0))]
        args += list(extra)
    rspec = pl.BlockSpec((tm, n), lambda i, kk: (i, 0))
    in_specs += [rspec, pl.BlockSpec((1, n), lambda i, kk: (0, 0)), rspec]
    return pl.pallas_call(
        body, grid=(m // tm, nk), in_specs=in_specs,
        out_specs=[rspec, rspec, pl.BlockSpec((1, n), lambda i, kk: (0, 0))],
        out_shape=[jax.ShapeDtypeStruct((m, n), F32), jax.ShapeDtypeStruct((m, n), BF16),
                   jax.ShapeDtypeStruct((1, n), F32)],
        scratch_shapes=[pltpu.VMEM((tm, n), F32)], name=name,
        compiler_params=_params("arbitrary", "arbitrary"))(*args, h, gain, dres)


def _conv_taps(cur, prev8, w_ref, first):
    nk = w_ref.shape[0]
    out = cur * w_ref[nk - 1:nk, :]
    for s in range(1, nk):
        out += _shift_down(cur, prev8, s, first) * w_ref[nk - 1 - s:nk - s, :]
    return out


def _shift_up(cur, next8, s, last):
    rows = cur.shape[0]
    row = _iota(cur.shape, 0)
    next8 = jnp.where(last, 0.0, next8)
    sh = pltpu.roll(cur, rows - s, axis=0)
    nh = jnp.tile(pltpu.roll(next8, 8 - s, axis=0), (rows // 8, 1))
    return jnp.where(row >= rows - s, nh, sh)


def _shift_down(cur, prev8, s, first):
    rows = cur.shape[0]
    row = _iota(cur.shape, 0)
    prev8 = jnp.where(first, 0.0, prev8)
    sh = pltpu.roll(cur, s, axis=0)
    ph = jnp.tile(pltpu.roll(prev8, s, axis=0), (rows // 8, 1))
    return jnp.where(row < s, ph, sh)


def _gdn_pre(proj, pab, cq, ck, cv, a_log, dt_bias, *, heads, dk, dv, col_q, col_k, col_v, row_lo, row_hi, name):
    t = proj.shape[0]
    tm = _pick(t, (384, 256, 128))
    nb = t // tm

    def body(pq_ref, pqp_ref, pk_ref, pkp_ref, pv_ref, pvp_ref, ab_ref, cq_ref, ck_ref, cv_ref, al_ref, dt_ref,
             qn_ref, kn_ref, v_ref, g_ref, b_ref):
        h, i = pl.program_id(0), pl.program_id(1)
        first = i == 0
        row = i * tm + _iota((tm, 1), 0)
        valid = (row >= row_lo) & (row < row_hi)
        q1 = _silu(_conv_taps(pq_ref[...], pqp_ref[...], cq_ref, first))
        k1 = _silu(_conv_taps(pk_ref[...], pkp_ref[...], ck_ref, first))
        v1 = _silu(_conv_taps(pv_ref[...], pvp_ref[...], cv_ref, first))
        qn_ref[...] = jnp.where(valid, q1 * lax.rsqrt(jnp.sum(q1 * q1, axis=-1, keepdims=True) + L2_EPS), 0.0)
        kn_ref[...] = jnp.where(valid, k1 * lax.rsqrt(jnp.sum(k1 * k1, axis=-1, keepdims=True) + L2_EPS), 0.0)
        v_ref[...] = jnp.where(valid, v1, 0.0)
        ab = ab_ref[...]
        da = _lane_pick(ab, h)
        db = _lane_pick(ab, heads + h)
        a = _lane_pick(al_ref[...], h)
        dtb = _lane_pick(dt_ref[...], h)
        g_ref[...] = jnp.where(valid, -jnp.exp(a) * _softplus(da + dtb), 0.0)
        b_ref[...] = jnp.where(valid, _sigmoid(db), 0.0)

    def cur(width, col):
        return pl.BlockSpec((tm, width), lambda h, i: (i, col // width + h))

    def prev(width, col):
        return pl.BlockSpec((8, width), lambda h, i: (jnp.maximum(i * (tm // 8) - 1, 0), col // width + h))

    def out(width):
        return pl.BlockSpec((None, tm, width), lambda h, i: (h, i, 0))

    small = pl.BlockSpec((1, LANES), lambda h, i: (0, 0))
    return pl.pallas_call(
        body, grid=(heads, nb),
        in_specs=[cur(dk, col_q), prev(dk, col_q), cur(dk, col_k), prev(dk, col_k), cur(dv, col_v), prev(dv, col_v),
                  pl.BlockSpec((tm, LANES), lambda h, i: (i, 0)),
                  pl.BlockSpec((cq.shape[0], dk), lambda h, i: (0, h)), pl.BlockSpec((ck.shape[0], dk), lambda h, i: (0, h)),
                  pl.BlockSpec((cv.shape[0], dv), lambda h, i: (0, h)), small, small],
        out_specs=[out(dk), out(dk), out(dv), out(1), out(1)],
        out_shape=[jax.ShapeDtypeStruct((heads, t, dk), F32), jax.ShapeDtypeStruct((heads, t, dk), F32),
                   jax.ShapeDtypeStruct((heads, t, dv), F32), jax.ShapeDtypeStruct((heads, t, 1), F32),
                   jax.ShapeDtypeStruct((heads, t, 1), F32)], name=name,
        compiler_params=_params("parallel", "parallel"))(proj, proj, proj, proj, proj, proj, pab, cq, ck, cv, a_log, dt_bias)


def _chunk_masks(rows=GDN_ROWS, row0=0):
    ri = row0 + _iota((rows, GDN_ROWS), 0)
    ci = _iota((rows, GDN_ROWS), 1)
    same = jnp.right_shift(ri, CHUNK_SHIFT) == jnp.right_shift(ci, CHUNK_SHIFT)
    return same, same & (ri >= ci), same & (ri > ci), ri == ci


def _col_to_row(col, eye):
    return jnp.sum(jnp.where(eye, col, 0.0), axis=0, keepdims=True)


def _row_to_col(row, eye):
    return jnp.sum(jnp.where(eye, row, 0.0), axis=1, keepdims=True)


def _chunk_common(qn, kn, g, beta, dk_scale):
    same, incl, strict, eye = _chunk_masks()
    gb = jnp.broadcast_to(g, (GDN_ROWS, LANES))
    gam = jnp.max(_dot_exact_l(jnp.where(incl, 1.0, 0.0).astype(BF16), gb), axis=1, keepdims=True)
    gam_last = jnp.max(_dot_exact_l(jnp.where(same, 1.0, 0.0).astype(BF16), gb), axis=1, keepdims=True)
    diff = gam - _col_to_row(gam, eye)
    decay = jnp.where(incl, jnp.exp(jnp.where(incl, diff, 0.0)), 0.0)
    eg = jnp.exp(gam)
    ek = jnp.exp(gam_last - gam)
    kb = kn * beta
    qt = qn * dk_scale
    lmat = jnp.where(strict, _dot_nt(_bf(kb), _bf(kn)) * decay, 0.0)
    pmat = jnp.where(incl, _dot_nt(_bf(qt), _bf(kn)) * decay, 0.0)
    return dict(incl=incl, strict=strict, eye=eye, decay=decay, eg=eg, ek=ek, egl=jnp.exp(gam_last),
                kb=kb, qt=qt, lmat=lmat, pmat=pmat)


def _gdn_prep(qn, kn, v, g, beta, *, name):
    heads, t, dk = qn.shape
    dv = v.shape[2]
    rows = _pick(t, (3 * GDN_ROWS, 2 * GDN_ROWS, GDN_ROWS))
    dk_scale = dk ** -0.5

    def body(q_ref, k_ref, v_ref, g_ref, b_ref, u_ref, w_ref, p_ref, qd_ref, kd_ref, egl_ref, t_ref):
        for b in range(rows // GDN_ROWS):
            r = pl.ds(b * GDN_ROWS, GDN_ROWS)
            kn_, beta_ = k_ref[r, :], b_ref[r, :]
            c = _chunk_common(q_ref[r, :], kn_, g_ref[r, :], beta_, dk_scale)
            eye_f = jnp.where(c["eye"], 1.0, 0.0)
            x = c["lmat"]
            tinv = eye_f - x
            for _ in range(CHUNK_SHIFT - 1):
                x = _dot_hp(x, x)
                tinv = tinv + _dot_hp(tinv, x)
            u_ref[r, :] = _dot_hp(tinv, v_ref[r, :] * beta_)
            w_ref[r, :] = _dot_hp(tinv, c["kb"] * c["eg"])
            p_ref[r, :] = c["pmat"]
            qd_ref[r, :] = c["qt"] * c["eg"]
            kd_ref[r, :] = kn_ * c["ek"]
            egl_ref[r, :] = c["egl"]
            t_ref[r, :] = tinv

    def blk(width):
        return pl.BlockSpec((None, rows, width), lambda h, i: (h, i, 0))

    def shp(width):
        return jax.ShapeDtypeStruct((heads, t, width), F32)

    return pl.pallas_call(
        body, grid=(heads, t // rows),
        in_specs=[blk(dk), blk(dk), blk(dv), blk(1), blk(1)],
        out_specs=[blk(dv), blk(dk), blk(GDN_ROWS), blk(dk), blk(dk), blk(1), blk(GDN_ROWS)],
        out_shape=[shp(dv), shp(dk), shp(GDN_ROWS), shp(dk), shp(dk), shp(1), shp(GDN_ROWS)], name=name,
        compiler_params=_params("parallel", "parallel"))(qn, kn, v, g, beta)


def _gdn_scan(u, w, p, qd, kd, egl, proj, gain, *, col_z, name):
    heads, t, dv = u.shape
    dk = w.shape[2]
    nb = t // GDN_ROWS
    sub = GDN_ROWS // CHUNK
    hp = SCAN_HEADS

    def body(u_ref, w_ref, p_ref, qd_ref, kd_ref, egl_ref, z_ref, gn_ref, o_ref, og_ref, st_ref, s_ref):
        @pl.when(pl.program_id(1) == 0)
        def _():
            s_ref[...] = jnp.zeros_like(s_ref)

        for hh in range(hp):
            cols = pl.ds(hh * dv, dv)
            vn_parts = [jnp.zeros((CHUNK, dv), F32)] * sub
            for c in range(sub):
                r = pl.ds(c * CHUNK, CHUNK)
                s = s_ref[hh]
                st_ref[hh, c] = s
                sb = _bf(s)
                vn = u_ref[hh, r, :] - _dot(_bf(w_ref[hh, r, :]), sb)
                vn_parts[c] = vn
                vfull = _bf(jnp.concatenate(vn_parts, axis=0))
                o = _dot(_bf(qd_ref[hh, r, :]), sb) + _dot(_bf(p_ref[hh, r, :]), vfull)
                gl = egl_ref[hh, pl.ds(c * CHUNK, 1), :]
                s_ref[hh] = s * gl + _dot_tn(_bf(kd_ref[hh, r, :]), _bf(vn))
                o_ref[hh, r, :] = o
                og_ref[r, cols] = _bf(_rms_fwd(o, gn_ref[...]) * _silu(z_ref[r, cols]))

    def blk(width):
        return pl.BlockSpec((hp, GDN_ROWS, width), lambda h, i: (h, i, 0))

    return pl.pallas_call(
        body, grid=(heads // hp, nb),
        in_specs=[blk(dv), blk(dk), blk(GDN_ROWS), blk(dk), blk(dk), blk(1),
                  pl.BlockSpec((GDN_ROWS, hp * dv), lambda h, i: (i, col_z // (hp * dv) + h)),
                  pl.BlockSpec((1, dv), lambda h, i: (0, 0))],
        out_specs=[blk(dv), pl.BlockSpec((GDN_ROWS, hp * dv), lambda h, i: (i, h)),
                   pl.BlockSpec((hp, sub, dk, dv), lambda h, i: (h, i, 0, 0))],
        out_shape=[jax.ShapeDtypeStruct((heads, t, dv), F32), jax.ShapeDtypeStruct((t, heads * dv), BF16),
                   jax.ShapeDtypeStruct((heads, t // CHUNK, dk, dv), F32)],
        scratch_shapes=[pltpu.VMEM((hp, dk, dv), F32)], name=name,
        compiler_params=_params("parallel", "arbitrary"))(u, w, p, qd, kd, egl, proj, gain)


def _gdn_post_bwd(o, proj, gain, dout, *, col_z, name):
    heads, t, dv = o.shape
    tm = _pick(t, (384, 256, 128))

    def body(o_ref, z_ref, gn_ref, d_ref, do_ref, dz_ref, dg_ref):
        @pl.when((pl.program_id(0) == 0) & (pl.program_id(1) == 0))
        def _():
            dg_ref[...] = jnp.zeros_like(dg_ref)

        o_, z, d = o_ref[...], z_ref[...], d_ref[...]
        y = _rms_fwd(o_, gn_ref[...])
        dz_ref[...] = _bf(d * y * _silu_grad(z))
        do, dgr = _rms_bwd(o_, gn_ref[...], d * _silu(z))
        do_ref[...] = do
        dg_ref[...] += jnp.sum(dgr, axis=0, keepdims=True)

    return pl.pallas_call(
        body, grid=(t // tm, heads),
        in_specs=[pl.BlockSpec((None, tm, dv), lambda i, h: (h, i, 0)),
                  pl.BlockSpec((tm, dv), lambda i, h: (i, col_z // dv + h)),
                  pl.BlockSpec((1, dv), lambda i, h: (0, 0)), pl.BlockSpec((tm, dv), lambda i, h: (i, h))],
        out_specs=[pl.BlockSpec((None, tm, dv), lambda i, h: (h, i, 0)), pl.BlockSpec((tm, dv), lambda i, h: (i, h)),
                   pl.BlockSpec((1, dv), lambda i, h: (0, 0))],
        out_shape=[jax.ShapeDtypeStruct((heads, t, dv), F32), jax.ShapeDtypeStruct((t, heads * dv), BF16),
                   jax.ShapeDtypeStruct((1, dv), F32)], name=name,
        compiler_params=_params("arbitrary", "arbitrary"))(o, proj, gain, dout)


def _gdn_bwd_scan(u, w, p, qd, kd, egl, st, do, *, name):
    heads, t, dv = u.shape
    dk = w.shape[2]
    nb = t // GDN_ROWS
    sub = GDN_ROWS // CHUNK
    hp = SCAN_HEADS

    def body(u_ref, w_ref, p_ref, qd_ref, kd_ref, egl_ref, st_ref, do_ref,
             du_ref, dw_ref, dp_ref, dqd_ref, dkd_ref, dgl_ref, ds_ref):
        @pl.when(pl.program_id(1) == 0)
        def _():
            ds_ref[...] = jnp.zeros_like(ds_ref)

        for hh in range(hp):
            for c in reversed(range(sub)):
                r = pl.ds(c * CHUNK, CHUNK)
                s = st_ref[hh, c]
                sb = _bf(s)
                ds = ds_ref[hh]
                dsb = _bf(ds)
                dob = _bf(do_ref[hh, r, :])
                wb, pb = _bf(w_ref[hh, r, :]), _bf(p_ref[hh, r, :])
                qdb, kdb = _bf(qd_ref[hh, r, :]), _bf(kd_ref[hh, r, :])
                vn = u_ref[hh, r, :] - _dot(wb, sb)
                zeros = jnp.zeros((CHUNK, dv), BF16)
                vfull = jnp.concatenate([_bf(vn) if cc == c else zeros for cc in range(sub)], axis=0)
                dvn = _dot_tn(pb, dob)[c * CHUNK:(c + 1) * CHUNK, :] + _dot(kdb, dsb)
                dvb = _bf(dvn)
                gl = egl_ref[hh, pl.ds(c * CHUNK, 1), :]
                du_ref[hh, r, :] = dvn
                dw_ref[hh, r, :] = -_dot_nt(dvb, sb)
                dp_ref[hh, r, :] = jnp.where(_chunk_masks(CHUNK, c * CHUNK)[1], _dot_nt(dob, vfull), 0.0)
                dqd_ref[hh, r, :] = _dot_nt(dob, sb)
                dkd_ref[hh, r, :] = _dot_nt(_bf(vn), dsb)
                dgl = jnp.sum(jnp.sum(ds * s, axis=1, keepdims=True), axis=0, keepdims=True)
                dgl_ref[hh, r, :] = jnp.where(_iota((CHUNK, 1), 0) == CHUNK - 1, dgl, 0.0)
                ds_ref[hh] = ds * gl + _dot_tn(qdb, dob) - _dot_tn(wb, dvb)

    def blk(width):
        return pl.BlockSpec((hp, GDN_ROWS, width), lambda h, i: (h, nb - 1 - i, 0))

    def shp(width):
        return jax.ShapeDtypeStruct((heads, t, width), F32)

    return pl.pallas_call(
        body, grid=(heads // hp, nb),
        in_specs=[blk(dv), blk(dk), blk(GDN_ROWS), blk(dk), blk(dk), blk(1),
                  pl.BlockSpec((hp, sub, dk, dv), lambda h, i: (h, nb - 1 - i, 0, 0)), blk(dv)],
        out_specs=[blk(dv), blk(dk), blk(GDN_ROWS), blk(dk), blk(dk), blk(1)],
        out_shape=[shp(dv), shp(dk), shp(GDN_ROWS), shp(dk), shp(dk), shp(1)],
        scratch_shapes=[pltpu.VMEM((hp, dk, dv), F32)], name=name,
        compiler_params=_params("parallel", "arbitrary"))(u, w, p, qd, kd, egl, st, do)


def _gdn_bwd_prep(qn, kn, v, g, beta, tinv, u, w, du, dw, dp, dqd, dkd, dgl, *, name):
    heads, t, dk = qn.shape
    dv = v.shape[2]
    rows = _pick(t, (3 * GDN_ROWS, 2 * GDN_ROWS, GDN_ROWS))
    dk_scale = dk ** -0.5

    def rowsum(x):
        return jnp.sum(x, axis=1, keepdims=True)

    def body(q_ref, k_ref, v_ref, g_ref, b_ref, t_ref, u_ref, w_ref, du_ref, dw_ref, dp_ref, dqd_ref, dkd_ref, dgl_ref,
             dq_ref, dkk_ref, dvv_ref, dg_ref, db_ref):
        for b in range(rows // GDN_ROWS):
            r = pl.ds(b * GDN_ROWS, GDN_ROWS)
            kn_, beta_, v_ = k_ref[r, :], b_ref[r, :], v_ref[r, :]
            c = _chunk_common(q_ref[r, :], kn_, g_ref[r, :], beta_, dk_scale)
            eye, strict, decay = c["eye"], c["strict"], c["decay"]
            kb, qt, eg, ek = c["kb"], c["qt"], c["eg"], c["ek"]
            tinv_ = t_ref[r, :]
            dbv = _dot_hp(tinv_, du_ref[r, :], _dot_tn)
            dbw = _dot_hp(tinv_, dw_ref[r, :], _dot_tn)
            da = -(_dot_nt(_bf(dbv), _bf(u_ref[r, :])) + _dot_nt(_bf(dbw), _bf(w_ref[r, :])))
            dl = jnp.where(strict, da, 0.0)
            dp_ = dp_ref[r, :]
            dm = _bf(dl * decay)
            dn = _bf(dp_ * decay)
            knb = _bf(kn_)
            dkb = _dot(dm, knb) + dbw * eg
            dkn = _dot_tn(dm, _bf(kb)) + _dot_tn(dn, _bf(qt))
            dqt = _dot(dn, knb)
            gmat = dl * c["lmat"] + dp_ * c["pmat"]
            dqd_, dkd_ = dqd_ref[r, :], dkd_ref[r, :]
            qd = qt * eg
            kd = kn_ * ek
            bw = kb * eg
            kdsum = rowsum(dkd_ * kd)
            dgam = rowsum(gmat) - _row_to_col(jnp.sum(gmat, axis=0, keepdims=True), eye)
            dgam += rowsum(dbw * bw) + rowsum(dqd_ * qd) - kdsum
            last = (_iota((GDN_ROWS, 1), 0) & (CHUNK - 1)) == CHUNK - 1
            same = _chunk_masks()[0]
            same_f = jnp.where(same, 1.0, 0.0).astype(BF16)
            chunk_tot = jnp.max(_dot_exact_l(same_f, jnp.broadcast_to(kdsum, (GDN_ROWS, LANES))), axis=1, keepdims=True)
            dgam += jnp.where(last, chunk_tot, 0.0) + dgl_ref[r, :] * c["egl"]
            dq_ref[r, :] = (dqt + dqd_ * eg) * dk_scale
            dkk_ref[r, :] = dkn + dkd_ * ek + dkb * beta_
            dvv_ref[r, :] = dbv * beta_
            db_ref[r, :] = rowsum(dbv * v_) + rowsum(dkb * kn_)
            upper = jnp.where(same & (_iota((GDN_ROWS, GDN_ROWS), 0) <= _iota((GDN_ROWS, GDN_ROWS), 1)), 1.0, 0.0)
            dgb = _dot_exact_l(upper.astype(BF16), jnp.broadcast_to(dgam, (GDN_ROWS, LANES)))
            dg_ref[r, :] = _lane_pick(dgb, 0)

    def blk(width):
        return pl.BlockSpec((None, rows, width), lambda h, i: (h, i, 0))

    def shp(width):
        return jax.ShapeDtypeStruct((heads, t, width), F32)

    return pl.pallas_call(
        body, grid=(heads, t // rows),
        in_specs=[blk(dk), blk(dk), blk(dv), blk(1), blk(1), blk(GDN_ROWS), blk(dv), blk(dk),
                  blk(dv), blk(dk), blk(GDN_ROWS), blk(dk), blk(dk), blk(1)],
        out_specs=[blk(dk), blk(dk), blk(dv), blk(1), blk(1)],
        out_shape=[shp(dk), shp(dk), shp(dv), shp(1), shp(1)], name=name,
        compiler_params=_params("parallel", "parallel"))(qn, kn, v, g, beta, tinv, u, w, du, dw, dp, dqd, dkd, dgl)


def _gdn_pre_bwd_a(proj, pab, cq, ck, cv, a_log, dt_bias, dqn, dkn, dvv, dg, dbeta, *,
                   heads, dk, dv, col_q, col_k, col_v, row_lo, row_hi, name):
    t = proj.shape[0]
    tm = _pick(t, (384, 256, 128))
    nb = t // tm

    def body(pq_ref, pqp_ref, pk_ref, pkp_ref, pv_ref, pvp_ref, ab_ref, cq_ref, ck_ref, cv_ref, al_ref, dt_ref,
             dqn_ref, dkn_ref, dvv_ref, dg_ref, db_ref, dcq_ref, dck_ref, dcv_ref, dab_ref, dal_ref, ddt_ref):
        i, h = pl.program_id(0), pl.program_id(1)
        first = i == 0

        @pl.when((i == 0) & (h == 0))
        def _():
            dal_ref[...] = jnp.zeros_like(dal_ref)
            ddt_ref[...] = jnp.zeros_like(ddt_ref)

        @pl.when(h == 0)
        def _():
            dab_ref[...] = jnp.zeros_like(dab_ref)

        row = i * tm + _iota((tm, 1), 0)
        valid = (row >= row_lo) & (row < row_hi)

        def l2_bwd(c1, dn):
            x1 = _silu(c1)
            r = lax.rsqrt(jnp.sum(x1 * x1, axis=-1, keepdims=True) + L2_EPS)
            dn = jnp.where(valid, dn, 0.0)
            d1 = r * dn - x1 * (r * r * r) * jnp.sum(dn * x1, axis=-1, keepdims=True)
            return d1 * _silu_grad(c1)

        dcq_ref[...] = l2_bwd(_conv_taps(pq_ref[...], pqp_ref[...], cq_ref, first), dqn_ref[...])
        dck_ref[...] = l2_bwd(_conv_taps(pk_ref[...], pkp_ref[...], ck_ref, first), dkn_ref[...])
        cv1 = _conv_taps(pv_ref[...], pvp_ref[...], cv_ref, first)
        dcv_ref[...] = jnp.where(valid, dvv_ref[...], 0.0) * _silu_grad(cv1)
        ab = ab_ref[...]
        da = _lane_pick(ab, h)
        db = _lane_pick(ab, heads + h)
        a = _lane_pick(al_ref[...], h)
        dtb = _lane_pick(dt_ref[...], h)
        dgv = jnp.where(valid, dg_ref[...], 0.0)
        ea = jnp.exp(a)
        g = -ea * _softplus(da + dtb)
        dda = dgv * (-ea) * _sigmoid(da + dtb)
        beta = _sigmoid(db)
        ddb = jnp.where(valid, db_ref[...], 0.0) * beta * (1.0 - beta)
        lane = _iota((tm, LANES), 1)
        dab_ref[...] += jnp.where(lane == h, dda, 0.0) + jnp.where(lane == heads + h, ddb, 0.0)
        lane1 = _iota((1, LANES), 1)
        dal_ref[...] += jnp.where(lane1 == h, jnp.sum(dgv * g, axis=0, keepdims=True), 0.0)
        ddt_ref[...] += jnp.where(lane1 == h, jnp.sum(dda, axis=0, keepdims=True), 0.0)

    def cur(width, col):
        return pl.BlockSpec((tm, width), lambda i, h: (i, col // width + h))

    def prev(width, col):
        return pl.BlockSpec((8, width), lambda i, h: (jnp.maximum(i * (tm // 8) - 1, 0), col // width + h))

    def hd(width):
        return pl.BlockSpec((None, tm, width), lambda i, h: (h, i, 0))

    small = pl.BlockSpec((1, LANES), lambda i, h: (0, 0))
    return pl.pallas_call(
        body, grid=(nb, heads),
        in_specs=[cur(dk, col_q), prev(dk, col_q), cur(dk, col_k), prev(dk, col_k), cur(dv, col_v), prev(dv, col_v),
                  pl.BlockSpec((tm, LANES), lambda i, h: (i, 0)),
                  pl.BlockSpec((cq.shape[0], dk), lambda i, h: (0, h)), pl.BlockSpec((ck.shape[0], dk), lambda i, h: (0, h)),
                  pl.BlockSpec((cv.shape[0], dv), lambda i, h: (0, h)), small, small,
                  hd(dk), hd(dk), hd(dv), hd(1), hd(1)],
        out_specs=[hd(dk), hd(dk), hd(dv), pl.BlockSpec((tm, LANES), lambda i, h: (i, 0)), small, small],
        out_shape=[jax.ShapeDtypeStruct((heads, t, dk), F32), jax.ShapeDtypeStruct((heads, t, dk), F32),
                   jax.ShapeDtypeStruct((heads, t, dv), F32), jax.ShapeDtypeStruct((t, LANES), F32),
                   jax.ShapeDtypeStruct((1, LANES), F32), jax.ShapeDtypeStruct((1, LANES), F32)], name=name,
        compiler_params=_params("arbitrary", "arbitrary"))(
            proj, proj, proj, proj, proj, proj, pab, cq, ck, cv, a_log, dt_bias, dqn, dkn, dvv, dg, dbeta)


def _conv_bwd(proj, dc, cw, *, heads, width, col, name):
    t = proj.shape[0]
    tm = _pick(t, (384, 256, 128))
    nb = t // tm
    nk = cw.shape[0]

    def body(p_ref, pp_ref, d_ref, dn_ref, w_ref, dp_ref, dw_ref):
        i = pl.program_id(1)
        first, last = i == 0, i == nb - 1

        @pl.when(first)
        def _():
            dw_ref[...] = jnp.zeros_like(dw_ref)

        x, d = p_ref[...], d_ref[...]
        dx = d * w_ref[nk - 1:nk, :]
        dw_ref[nk - 1:nk, :] += jnp.sum(d * x, axis=0, keepdims=True)
        for s in range(1, nk):
            dx += _shift_up(d, dn_ref[...], s, last) * w_ref[nk - 1 - s:nk - s, :]
            dw_ref[nk - 1 - s:nk - s, :] += jnp.sum(d * _shift_down(x, pp_ref[...], s, first), axis=0, keepdims=True)
        dp_ref[...] = _bf(dx)

    return pl.pallas_call(
        body, grid=(heads, nb),
        in_specs=[pl.BlockSpec((tm, width), lambda h, i: (i, col // width + h)),
                  pl.BlockSpec((8, width), lambda h, i: (jnp.maximum(i * (tm // 8) - 1, 0), col // width + h)),
                  pl.BlockSpec((None, tm, width), lambda h, i: (h, i, 0)),
                  pl.BlockSpec((None, 8, width), lambda h, i: (h, jnp.minimum((i + 1) * (tm // 8), t // 8 - 1), 0)),
                  pl.BlockSpec((nk, width), lambda h, i: (0, h))],
        out_specs=[pl.BlockSpec((tm, width), lambda h, i: (i, h)), pl.BlockSpec((nk, width), lambda h, i: (0, h))],
        out_shape=[jax.ShapeDtypeStruct((t, heads * width), BF16), jax.ShapeDtypeStruct((nk, heads * width), F32)],
        name=name, compiler_params=_params("parallel", "arbitrary"))(proj, proj, dc, dc, cw)


def _sb_pre(proj, gq, gk, *, heads, dh, col_q, col_k, col_v, name):
    t = proj.shape[0]
    tm = _pick(t, (384, 256, 128))

    def body(q_ref, k_ref, v_ref, gq_ref, gk_ref, qo_ref, ko_ref, vo_ref):
        qo_ref[...] = _bf(_rms_fwd(q_ref[...], gq_ref[...]))
        ko_ref[...] = _bf(_rms_fwd(k_ref[...], gk_ref[...]))
        vo_ref[...] = _bf(v_ref[...])

    def cur(col):
        return pl.BlockSpec((tm, dh), lambda i, h: (i, col // dh + h))

    gspec = pl.BlockSpec((1, dh), lambda i, h: (0, 0))
    ospec = pl.BlockSpec((tm, dh), lambda i, h: (i, h))
    return pl.pallas_call(
        body, grid=(t // tm, heads), in_specs=[cur(col_q), cur(col_k), cur(col_v), gspec, gspec],
        out_specs=[ospec] * 3, out_shape=[jax.ShapeDtypeStruct((t, heads * dh), BF16)] * 3, name=name,
        compiler_params=_params("parallel", "parallel"))(proj, proj, proj, gq, gk)


def _sb_tile(q, kb, i, j, blk, key_lo, scale):
    z = _dot_nt(q, kb) * scale
    qpos = i * blk + _iota((blk, blk), 0)
    kpos = j * blk + _iota((blk, blk), 1)
    vis = (kpos < qpos) & (kpos >= key_lo)
    ls = jnp.minimum(z, 0.0) - jnp.log(1.0 + jnp.exp(-jnp.abs(z)))
    return vis, ls, jnp.where(vis, ls - z, 0.0)


def _dot2_r(x, m):
    hi, lo = _split2(x)
    return _dot(hi, m) + _dot(lo, m)


def _suffix_sums(x, later):
    return jnp.concatenate([_dot2_r(x[:, s:], later[s:, s:s + LANES]) for s in range(0, x.shape[1], LANES)], axis=1)


def _prefix_sums(x, earlier):
    return jnp.concatenate([_dot2_r(x[:, :s + LANES], earlier[:s + LANES, s:s + LANES])
                            for s in range(0, x.shape[1], LANES)], axis=1)


def _sb_fwd(qs, ks, vs, *, heads, dh, key_lo, name):
    t = qs.shape[0]
    blk = _pick(t, (3 * SB_BLOCK, 2 * SB_BLOCK, SB_BLOCK))
    nq = t // blk
    assert nq <= LANES
    scale = dh ** -0.5

    def body(q_ref, k_ref, v_ref, o_ref, c_ref):
        i = pl.program_id(1)
        q = q_ref[...]
        later = jnp.where(_iota((blk, blk), 0) > _iota((blk, blk), 1), 1.0, 0.0).astype(BF16)
        lane = _iota((blk, LANES), 1)
        c_ref[...] = jnp.zeros_like(c_ref)

        def step(n, carry):
            acc, run = carry
            j = i - n
            rows = pl.ds(pl.multiple_of(j * blk, blk), blk)
            vis, ls, lk = _sb_tile(q, k_ref[rows, :], i, j, blk, key_lo, scale)
            wgt = jnp.where(vis, jnp.exp(ls + _suffix_sums(lk, later) + run), 0.0)
            acc = acc + _dot(_bf(wgt), v_ref[rows, :])
            c_ref[...] = jnp.where(lane == j, run, c_ref[...])
            return acc, run + jnp.sum(lk, axis=1, keepdims=True)

        acc, _ = lax.fori_loop(0, i + 1, step, (jnp.zeros((blk, dh), F32), jnp.zeros((blk, 1), F32)))
        o_ref[...] = _bf(acc)

    full = pl.BlockSpec((t, dh), lambda h, i: (0, h))
    return pl.pallas_call(
        body, grid=(heads, nq),
        in_specs=[pl.BlockSpec((blk, dh), lambda h, i: (i, h)), full, full],
        out_specs=[pl.BlockSpec((blk, dh), lambda h, i: (i, h)),
                   pl.BlockSpec((None, blk, LANES), lambda h, i: (h, i, 0))],
        out_shape=[jax.ShapeDtypeStruct((t, heads * dh), BF16), jax.ShapeDtypeStruct((heads, t, LANES), F32)],
        name=name, compiler_params=_params("parallel", "parallel"))(qs, ks, vs)


def _sb_bwd(qs, ks, vs, do, carry, *, heads, dh, key_lo, name):
    t = qs.shape[0]
    blk = _pick(t, (3 * SB_BLOCK, 2 * SB_BLOCK, SB_BLOCK))
    nq = t // blk
    scale = dh ** -0.5

    def body(q_ref, k_ref, v_ref, do_ref, c_ref, dq_ref, dk_ref, dv_ref):
        i = pl.program_id(1)

        @pl.when(i == 0)
        def _():
            dk_ref[...] = jnp.zeros_like(dk_ref)
            dv_ref[...] = jnp.zeros_like(dv_ref)

        q, dob, tab = q_ref[...], do_ref[...], c_ref[...]
        r0 = _iota((blk, blk), 0)
        r1 = _iota((blk, blk), 1)
        later = jnp.where(r0 > r1, 1.0, 0.0).astype(BF16)
        earlier = jnp.where(r0 < r1, 1.0, 0.0).astype(BF16)

        def step(j, carry_):
            dq, pre = carry_
            rows = pl.ds(pl.multiple_of(j * blk, blk), blk)
            kb, vb = k_ref[rows, :], v_ref[rows, :]
            vis, ls, lk = _sb_tile(q, kb, i, j, blk, key_lo, scale)
            wgt = jnp.where(vis, jnp.exp(ls + _suffix_sums(lk, later) + _lane_pick(tab, j)), 0.0)
            e = wgt * _dot_nt(dob, vb)
            before = jnp.where(vis, _prefix_sums(e, earlier) + pre, 0.0)
            sig = jnp.exp(ls)
            dz = _bf((e * (1.0 - sig) - before * sig) * scale)
            dk_ref[rows, :] += _dot_tn(dz, q)
            dv_ref[rows, :] += _dot_tn(_bf(wgt), dob)
            return dq + _dot(dz, kb), pre + jnp.sum(e, axis=1, keepdims=True)

        dq, _ = lax.fori_loop(0, i + 1, step, (jnp.zeros((blk, dh), F32), jnp.zeros((blk, 1), F32)))
        dq_ref[...] = dq

    full = pl.BlockSpec((t, dh), lambda h, i: (0, h))
    qblk = pl.BlockSpec((blk, dh), lambda h, i: (i, h))
    return pl.pallas_call(
        body, grid=(heads, nq),
        in_specs=[qblk, full, full, qblk, pl.BlockSpec((None, blk, LANES), lambda h, i: (h, i, 0))],
        out_specs=[qblk, full, full], out_shape=[jax.ShapeDtypeStruct((t, heads * dh), F32)] * 3, name=name,
        compiler_params=_params("parallel", "arbitrary"))(qs, ks, vs, do, carry)


def _sb_pre_bwd(proj, gq, gk, dq, dk, dv, *, heads, dh, col_q, col_k, name):
    t = proj.shape[0]
    tm = _pick(t, (384, 256, 128))

    def body(q_ref, k_ref, gq_ref, gk_ref, dq_ref, dk_ref, dv_ref, oq_ref, ok_ref, ov_ref, dgq_ref, dgk_ref):
        @pl.when((pl.program_id(0) == 0) & (pl.program_id(1) == 0))
        def _():
            dgq_ref[...] = jnp.zeros_like(dgq_ref)
            dgk_ref[...] = jnp.zeros_like(dgk_ref)

        dq_, gq_r = _rms_bwd(q_ref[...], gq_ref[...], dq_ref[...])
        dk_, gk_r = _rms_bwd(k_ref[...], gk_ref[...], dk_ref[...])
        oq_ref[...] = _bf(dq_)
        ok_ref[...] = _bf(dk_)
        ov_ref[...] = _bf(dv_ref[...])
        dgq_ref[...] += jnp.sum(gq_r, axis=0, keepdims=True)
        dgk_ref[...] += jnp.sum(gk_r, axis=0, keepdims=True)

    def cur(col):
        return pl.BlockSpec((tm, dh), lambda i, h: (i, col // dh + h))

    gspec = pl.BlockSpec((1, dh), lambda i, h: (0, 0))
    ospec = pl.BlockSpec((tm, dh), lambda i, h: (i, h))
    return pl.pallas_call(
        body, grid=(t // tm, heads), in_specs=[cur(col_q), cur(col_k), gspec, gspec, ospec, ospec, ospec],
        out_specs=[ospec, ospec, ospec, gspec, gspec],
        out_shape=[jax.ShapeDtypeStruct((t, heads * dh), BF16)] * 3 + [jax.ShapeDtypeStruct((1, dh), F32)] * 2,
        name=name, compiler_params=_params("arbitrary", "arbitrary"))(proj, proj, gq, gk, dq, dk, dv)


def _exchange(srcs, *, scatter, name):
    n = len(srcs)
    peers = N_DEV - 1

    def body(*refs):
        ins, outs = refs[:n], refs[n:2 * n]
        send_sems, recv_sems, local_sems = refs[2 * n:]
        x, y, c = lax.axis_index("x"), lax.axis_index("y"), lax.axis_index("c")
        me = 4 * x + 2 * y + c
        copies = []
        for a in range(n):
            own = ins[a].at[me] if scatter else ins[a]
            local = pltpu.make_async_copy(own, outs[a].at[me], local_sems.at[a])
            local.start()
            copies.append(local)
            for k in range(1, N_DEV):
                px = (x + (k >> 2 & 1)) % 2
                py = (y + (k >> 1 & 1)) % 2
                pc = (c + (k & 1)) % 2
                src = ins[a].at[4 * px + 2 * py + pc] if scatter else ins[a]
                cp = pltpu.make_async_remote_copy(
                    src_ref=src, dst_ref=outs[a].at[me], send_sem=send_sems.at[a * peers + k - 1],
                    recv_sem=recv_sems.at[a * peers + k - 1], device_id=(px, py, pc), device_id_type=MESH)
                cp.start()
                copies.append(cp)
        for cp in copies:
            cp.wait()

    any_spec = pl.BlockSpec(memory_space=pl.ANY)
    out_shape = [jax.ShapeDtypeStruct(s.shape if scatter else (N_DEV,) + s.shape, s.dtype) for s in srcs]
    return pl.pallas_call(
        body, in_specs=[any_spec] * n, out_specs=[any_spec] * n, out_shape=out_shape,
        scratch_shapes=[pltpu.SemaphoreType.DMA((n * peers,)), pltpu.SemaphoreType.DMA((n * peers,)),
                        pltpu.SemaphoreType.DMA((n,))],
        name=name, compiler_params=pltpu.CompilerParams(has_side_effects=True))(*srcs)


def _adam_math(g, w, m, v):
    m2 = ADAM_B1 * m + (1.0 - ADAM_B1) * g
    v2 = ADAM_B2 * v + (1.0 - ADAM_B2) * (g * g)
    m_hat = m2 / (1.0 - ADAM_B1 ** ADAM_STEP)
    v_hat = v2 / (1.0 - ADAM_B2 ** ADAM_STEP)
    return -ADAM_LR * (m_hat / (jnp.sqrt(v_hat) + ADAM_EPS) + ADAM_WD * w), m2, v2


def _adamw_slabs(slabs, w, m, v, *, name):
    r, c = w.shape
    tr = _pick(r, (128, 64, 32, 16, 8)) if r % 8 == 0 else r

    def body(s_ref, w_ref, m_ref, v_ref, g_ref, d_ref, mo_ref, vo_ref):
        g = s_ref[0].astype(F32)
        for p in range(1, N_DEV):
            g = g + s_ref[p].astype(F32)
        g_ref[...] = g
        d_ref[...], mo_ref[...], vo_ref[...] = _adam_math(g, w_ref[...], m_ref[...], v_ref[...])

    spec = pl.BlockSpec((tr, c), lambda i: (i, 0))
    return pl.pallas_call(
        body, grid=(r // tr,), in_specs=[pl.BlockSpec((N_DEV, tr, c), lambda i: (0, i, 0)), spec, spec, spec],
        out_specs=[spec] * 4, out_shape=[jax.ShapeDtypeStruct((r, c), F32)] * 4, name=name,
        compiler_params=_params("parallel"))(slabs, w, m, v)


def _adamw_small(g, w, m, v, *, name):
    def body(g_ref, w_ref, m_ref, v_ref, d_ref, mo_ref, vo_ref):
        d_ref[...], mo_ref[...], vo_ref[...] = _adam_math(g_ref[...], w_ref[...], m_ref[...], v_ref[...])

    return pl.pallas_call(body, out_shape=[jax.ShapeDtypeStruct(w.shape, F32)] * 3, name=name)(g, w, m, v)


def _sum_slabs(slabs, *, name):
    def body(s_ref, o_ref):
        acc = s_ref[0]
        for p in range(1, N_DEV):
            acc = acc + s_ref[p]
        o_ref[...] = acc

    return pl.pallas_call(body, out_shape=jax.ShapeDtypeStruct(slabs.shape[1:], F32), name=name)(slabs)


def _gather_cols(g):
    return jnp.transpose(g, (1, 0, 2)).reshape(g.shape[1], -1)


def _col_slabs(a):
    return jnp.transpose(a.reshape(a.shape[0], N_DEV, -1), (1, 0, 2))


def _pad_lanes(a):
    return jnp.pad(a, ((0, 0), (0, LANES - a.shape[1])))


def _local_step(x, target, meta, g_mix, w_main, w_ab, cq, ck, cv, a_log, dt_bias, g_dn, g_sbq, g_sbk,
                w_bd, w_bs, w_out, g_ffn, w_fg, w_fu, w_fo):
    seq, d = x.shape
    n_meta = meta.shape[0]
    heads = a_log.shape[1]
    qk = cq.shape[1]
    dvt = cv.shape[1]
    dk, dv = qk // heads, dvt // heads
    dh = g_sbq.shape[1]
    sbw = w_bs.shape[0]
    sb_heads = sbw // dh
    pad_l = (-n_meta) % CHUNK
    row_x = pad_l + n_meta
    rows = row_x + seq
    t = -(-rows // GDN_ROWS) * GDN_ROWS
    col_q, col_k, col_v, col_z = 0, qk, 2 * qk, 2 * qk + dvt
    col_sq = 2 * qk + 2 * dvt
    col_sk, col_sv, col_gd, col_gs = col_sq + sbw, col_sq + 2 * sbw, col_sq + 3 * sbw, col_sq + 3 * sbw + d

    def rows_pad(a):
        return jnp.concatenate([jnp.zeros((row_x, d), F32), a, jnp.zeros((t - rows, d), F32)], axis=0)

    h0 = jnp.concatenate([jnp.zeros((pad_l, d), F32), meta, x, jnp.zeros((t - rows, d), F32)], axis=0)
    tgt = rows_pad(target)
    a_log_p, dt_p = _pad_lanes(a_log), _pad_lanes(dt_bias)

    proj, n1 = _mm_norm(h0, g_mix, w_main, name="proj")
    pab = _mm_nn(n1, w_ab, out_dtype=F32, name="proj_ab")
    gk = dict(heads=heads, dk=dk, dv=dv, col_q=col_q, col_k=col_k, col_v=col_v, row_lo=pad_l, row_hi=rows)
    qn, kn, vv, g, beta = _gdn_pre(proj, pab, cq, ck, cv, a_log_p, dt_p, name="gdn_pre", **gk)
    u, w, pm, qd, kd, egl, tinv = _gdn_prep(qn, kn, vv, g, beta, name="gdn_prep")
    o_raw, o_dn, states = _gdn_scan(u, w, pm, qd, kd, egl, proj, g_dn, col_z=col_z, name="gdn_scan")
    qs, ks, vs = _sb_pre(proj, g_sbq, g_sbk, heads=sb_heads, dh=dh, col_q=col_sq, col_k=col_sk, col_v=col_sv,
                         name="sb_pre")
    o_sb, carry = _sb_fwd(qs, ks, vs, heads=sb_heads, dh=dh, key_lo=pad_l, name="sb_fwd")
    merged, br_dn, br_sb = _merge_fwd(o_dn, o_sb, w_bd, w_bs, proj, col_gd=col_gd, col_gs=col_gs, name="merge")
    h1 = _mm_res(h0, merged, w_out, name="mix_out")
    gate, up, act, n2 = _mm_norm_swiglu(h1, g_ffn, w_fg, w_fu, name="ffn_in")
    dy, dyb, lsum = _mm_res_loss(h1, act, w_fo, tgt, row0=row_x, nrows=seq, name="ffn_out_loss")

    dgate, dup = _swiglu_bwd(dyb, w_fo, gate, up, name="ffn_out_bwd")
    d_w_fo = _mm_tn(act, dyb, name="dw_ffn_out")
    d_w_fg = _mm_tn(n2, dgate, name="dw_ffn_gate")
    d_w_fu = _mm_tn(n2, dup, name="dw_ffn_up")
    dh1, dh1b, d_g_ffn = _mm_nt_rmsbwd([(dgate, w_fg), (dup, w_fu)], None, h1, g_ffn, dy, name="ffn_in_bwd")

    dbd, dbs, dgd, dgs = _merge_bwd(dh1b, w_out, proj, br_dn, br_sb, col_gd=col_gd, col_gs=col_gs, name="mix_out_bwd")
    d_w_out = _mm_tn(merged, dh1b, name="dw_out")
    d_w_bd = _mm_tn(o_dn, dbd, name="dw_branch_dn")
    d_w_bs = _mm_tn(o_sb, dbs, name="dw_branch_sb")
    do_dn = _mm_nt(dbd, w_bd, out_dtype=F32, name="branch_dn_bwd")
    do_sb = _mm_nt(dbs, w_bs, out_dtype=BF16, name="branch_sb_bwd")

    do_raw, dz, d_g_dn = _gdn_post_bwd(o_raw, proj, g_dn, do_dn, col_z=col_z, name="gdn_post_bwd")
    du, dw, dp, dqd, dkd, dgl = _gdn_bwd_scan(u, w, pm, qd, kd, egl, states, do_raw, name="gdn_bwd_scan")
    dqn, dkn, dvv, dg, dbeta = _gdn_bwd_prep(qn, kn, vv, g, beta, tinv, u, w, du, dw, dp, dqd, dkd, dgl,
                                            name="gdn_bwd_prep")
    dcq, dck, dcv, dpab, d_a_log, d_dt = _gdn_pre_bwd_a(proj, pab, cq, ck, cv, a_log_p, dt_p, dqn, dkn, dvv, dg, dbeta,
                                                        name="gdn_pre_bwd", **gk)
    dpq, d_cq = _conv_bwd(proj, dcq, cq, heads=heads, width=dk, col=col_q, name="conv_q_bwd")
    dpk, d_ck = _conv_bwd(proj, dck, ck, heads=heads, width=dk, col=col_k, name="conv_k_bwd")
    dpv, d_cv = _conv_bwd(proj, dcv, cv, heads=heads, width=dv, col=col_v, name="conv_v_bwd")

    dqs, dks, dvs = _sb_bwd(qs, ks, vs, do_sb, carry, heads=sb_heads, dh=dh, key_lo=pad_l, name="sb_bwd")
    dsq, dsk, dsv, d_g_sbq, d_g_sbk = _sb_pre_bwd(proj, g_sbq, g_sbk, dqs, dks, dvs, heads=sb_heads, dh=dh,
                                                   col_q=col_sq, col_k=col_sk, name="sb_pre_bwd")

    dproj = jnp.concatenate([dpq, dpk, dpv, dz, dsq, dsk, dsv, dgd, dgs], axis=1)
    dpab_b = _bf(dpab)
    d_w_main = _mm_tn(n1, dproj, name="dw_in_main")
    d_w_ab = _mm_tn(n1, dpab_b, name="dw_in_ab")
    dh0, _, d_g_mix = _mm_nt_rmsbwd([(dproj, w_main)], (dpab_b, w_ab), h0, g_mix, dh1, name="proj_bwd")

    return dict(lsum=lsum, grad_x=dh0[row_x:rows], d_meta=dh0[pad_l:row_x], d_g_mix=d_g_mix, d_w_main=d_w_main,
                d_w_ab=d_w_ab, d_cq=d_cq, d_ck=d_ck, d_cv=d_cv, d_a_log=d_a_log[:, :heads], d_dt=d_dt[:, :heads],
                d_g_dn=d_g_dn, d_g_sbq=d_g_sbq, d_g_sbk=d_g_sbk, d_w_bd=d_w_bd, d_w_bs=d_w_bs, d_w_out=d_w_out,
                d_g_ffn=d_g_ffn, d_w_fg=d_w_fg, d_w_fu=d_w_fu, d_w_fo=d_w_fo)


def _pack(parts):
    flat = []
    for a in parts:
        a = a.reshape(-1)
        flat.append(jnp.pad(a, (0, (-a.shape[0]) % LANES)))
    v = jnp.concatenate(flat)
    v = jnp.pad(v, (0, (-v.shape[0]) % (8 * LANES)))
    return v.reshape(-1, LANES)


def _unpack(packed, shapes):
    flat = packed.reshape(-1)
    out, pos = [], 0
    for s in shapes:
        n = math.prod(s)
        out.append(flat[pos:pos + n].reshape(s))
        pos += n + (-n) % LANES
    return out


def kernel(x, meta_tokens, norm_mix_gain, w_in, conv_q, conv_k, conv_v, dn_a_log, dn_dt_bias, dn_out_norm_gain, sb_q_norm_gain, sb_k_norm_gain, w_branch_dn, w_branch_sb, w_out, norm_ffn_gain, w_ffn_in, w_ffn_out, loss_target, m_meta_tokens, m_norm_mix_gain, m_w_in, m_conv_q, m_conv_k, m_conv_v, m_dn_a_log, m_dn_dt_bias, m_dn_out_norm_gain, m_sb_q_norm_gain, m_sb_k_norm_gain, m_w_branch_dn, m_w_branch_sb, m_w_out, m_norm_ffn_gain, m_w_ffn_in, m_w_ffn_out, v_meta_tokens, v_norm_mix_gain, v_w_in, v_conv_q, v_conv_k, v_conv_v, v_dn_a_log, v_dn_dt_bias, v_dn_out_norm_gain, v_sb_q_norm_gain, v_sb_k_norm_gain, v_w_branch_dn, v_w_branch_sb, v_w_out, v_norm_ffn_gain, v_w_ffn_in, v_w_ffn_out):
    me = 4 * lax.axis_index("x") + 2 * lax.axis_index("y") + lax.axis_index("c")
    heads = dn_a_log.shape[1]
    d = x.shape[2]
    qk = conv_q.shape[2] * N_DEV
    dvt = conv_v.shape[2] * N_DEV
    d_ff = w_ffn_out.shape[1] * N_DEV
    n_meta = meta_tokens.shape[0]
    col_ab = 2 * qk + 2 * dvt

    small_shapes = [meta_tokens.shape, conv_q.shape[1:], conv_k.shape[1:], conv_v.shape[1:]]
    small = _pack([meta_tokens, conv_q[0], conv_k[0], conv_v[0]])
    g_in, g_fi, g_bd, g_bs, g_out, g_fo, g_small = _exchange(
        [_bf(w_in[0]), _bf(w_ffn_in[0]), _bf(w_branch_dn[0]), _bf(w_branch_sb[0]), _bf(w_out[0]), _bf(w_ffn_out[0]), small],
        scatter=False, name="gather_weights")
    w_full = _gather_cols(g_in)
    w_main = jnp.concatenate([w_full[:, :col_ab], w_full[:, col_ab + 2 * heads:]], axis=1)
    w_ab = _pad_lanes(w_full[:, col_ab:col_ab + 2 * heads])
    w_fi = _gather_cols(g_fi)
    parts = [_unpack(g_small[p], small_shapes) for p in range(N_DEV)]
    meta_f, cq_f, ck_f, cv_f = (jnp.concatenate([parts[p][a] for p in range(N_DEV)], axis=1) for a in range(4))

    r = _local_step(x[0], loss_target[0], meta_f, norm_mix_gain, w_main, w_ab, cq_f, ck_f, cv_f, dn_a_log, dn_dt_bias,
                    dn_out_norm_gain, sb_q_norm_gain, sb_k_norm_gain, g_bd.reshape(-1, d), g_bs.reshape(-1, d),
                    g_out.reshape(-1, d), norm_ffn_gain, w_fi[:, :d_ff], w_fi[:, d_ff:], g_fo.reshape(-1, d))

    d_w_in = jnp.concatenate([r["d_w_main"][:, :col_ab], r["d_w_ab"][:, :2 * heads], r["d_w_main"][:, col_ab:]], axis=1)
    d_w_fi = jnp.concatenate([r["d_w_fg"], r["d_w_fu"]], axis=1)
    loss_part = (0.5 / d) * jnp.sum(r["lsum"], axis=1, keepdims=True)
    small_g = [r["d_meta"], r["d_g_mix"], r["d_cq"], r["d_ck"], r["d_cv"], r["d_a_log"], r["d_dt"], r["d_g_dn"],
               r["d_g_sbq"], r["d_g_sbk"], r["d_g_ffn"], loss_part]
    s_in, s_fi, s_bd, s_bs, s_out, s_fo = _exchange(
        [_bf(_col_slabs(d_w_in)), _bf(_col_slabs(d_w_fi)), _bf(r["d_w_bd"]).reshape(N_DEV, -1, d),
         _bf(r["d_w_bs"]).reshape(N_DEV, -1, d), _bf(r["d_w_out"]).reshape(N_DEV, -1, d),
         _bf(r["d_w_fo"]).reshape(N_DEV, -1, d)], scatter=True, name="scatter_grads")
    (g_packs,) = _exchange([_pack(small_g)], scatter=False, name="gather_small_grads")
    (g_meta, g_mix, g_cq, g_ck, g_cv, g_al, g_dt, g_gdn, g_sbq, g_sbk, g_ffn, loss) = _unpack(
        _sum_slabs(g_packs, name="sum_small_grads"), [a.shape for a in small_g])

    def mine(a, width):
        return lax.dynamic_slice_in_dim(a, me * width, width, axis=1)

    big = dict(w_in=(s_in, w_in, m_w_in, v_w_in), w_branch_dn=(s_bd, w_branch_dn, m_w_branch_dn, v_w_branch_dn),
               w_branch_sb=(s_bs, w_branch_sb, m_w_branch_sb, v_w_branch_sb), w_out=(s_out, w_out, m_w_out, v_w_out),
               w_ffn_in=(s_fi, w_ffn_in, m_w_ffn_in, v_w_ffn_in), w_ffn_out=(s_fo, w_ffn_out, m_w_ffn_out, v_w_ffn_out))
    tiny = dict(meta_tokens=(mine(g_meta, d // N_DEV), meta_tokens, m_meta_tokens, v_meta_tokens),
                norm_mix_gain=(g_mix, norm_mix_gain, m_norm_mix_gain, v_norm_mix_gain),
                conv_q=(mine(g_cq, qk // N_DEV), conv_q[0], m_conv_q[0], v_conv_q[0]),
                conv_k=(mine(g_ck, qk // N_DEV), conv_k[0], m_conv_k[0], v_conv_k[0]),
                conv_v=(mine(g_cv, dvt // N_DEV), conv_v[0], m_conv_v[0], v_conv_v[0]),
                dn_a_log=(g_al, dn_a_log, m_dn_a_log, v_dn_a_log), dn_dt_bias=(g_dt, dn_dt_bias, m_dn_dt_bias, v_dn_dt_bias),
                dn_out_norm_gain=(g_gdn, dn_out_norm_gain, m_dn_out_norm_gain, v_dn_out_norm_gain),
                sb_q_norm_gain=(g_sbq, sb_q_norm_gain, m_sb_q_norm_gain, v_sb_q_norm_gain),
                sb_k_norm_gain=(g_sbk, sb_k_norm_gain, m_sb_k_norm_gain, v_sb_k_norm_gain),
                norm_ffn_gain=(g_ffn, norm_ffn_gain, m_norm_ffn_gain, v_norm_ffn_gain))
    order = ["meta_tokens", "norm_mix_gain", "w_in", "conv_q", "conv_k", "conv_v", "dn_a_log", "dn_dt_bias",
             "dn_out_norm_gain", "sb_q_norm_gain", "sb_k_norm_gain", "w_branch_dn", "w_branch_sb", "w_out",
             "norm_ffn_gain", "w_ffn_in", "w_ffn_out"]
    grads, deltas, new_m, new_v = [], [], [], []
    for name in order:
        if name in big:
            slabs, w, m, v = big[name]
            g, dl, mo, vo = _adamw_slabs(slabs, w[0], m[0], v[0], name="adamw_" + name)
            like = w.shape
        else:
            g, w, m, v = tiny[name]
            like = dict(conv_q=conv_q, conv_k=conv_k, conv_v=conv_v).get(name, w).shape
            dl, mo, vo = _adamw_small(g, w, m, v, name="adamw_" + name)
        for lst, a in ((grads, g), (deltas, dl), (new_m, mo), (new_v, vo)):
            lst.append(a.reshape(like))
    return (loss.reshape(()), r["grad_x"][None], *grads, *deltas, *new_m, *new_v)
```

```python
import functools
import math

import jax
import jax.numpy as jnp
from jax import lax
from jax.experimental import pallas as pl
from jax.experimental.pallas import tpu as pltpu

F32 = jnp.float32
BF16 = jnp.bfloat16

N_DEV = 8
CHUNK = 64
CHUNK_SHIFT = 6
GDN_ROWS = 2 * CHUNK
SCAN_HEADS = 2
SB_BLOCK = 128
SB_HEADS_PER_STEP = 2
LANES = 128
RMS_EPS = 1e-6
L2_EPS = 1e-6
ADAM_LR = 0.001
ADAM_B1 = 0.9
ADAM_B2 = 0.999
ADAM_EPS = 1e-08
ADAM_WD = 0.01
ADAM_STEP = 10
V7X_VMEM_LIMIT_BYTES = 56 * 1024 * 1024
MM_TN_OUT_BLOCK_BYTES = 6 * 1024 * 1024

MESH = pl.DeviceIdType.MESH


def _params(*sem):
    return pltpu.CompilerParams(dimension_semantics=sem or None, vmem_limit_bytes=V7X_VMEM_LIMIT_BYTES)


def _pick(n, cands):
    for c in cands:
        if n % c == 0:
            return c
    raise ValueError(f"no block size among {cands} divides {n}")


def _bf(x):
    return x.astype(BF16)


def _dot(a, b):
    return jnp.dot(a, b, preferred_element_type=F32)


def _dot_nt(a, b):
    return lax.dot_general(a, b, (((1,), (1,)), ((), ())), preferred_element_type=F32)


def _dot_tn(a, b):
    return lax.dot_general(a, b, (((0,), (0,)), ((), ())), preferred_element_type=F32)


def _split2(x):
    hi = _bf(x)
    return hi, _bf(x - hi.astype(F32))


def _split3(x):
    hi = _bf(x)
    r = x - hi.astype(F32)
    mid = _bf(r)
    return hi, mid, _bf(r - mid.astype(F32))


def _dot_hp(a, b, dot=_dot):
    ah, al = _split2(a)
    bh, bl = _split2(b)
    return dot(ah, bh) + dot(ah, bl) + dot(al, bh)


def _dot_exact_r(x, m, dot=_dot):
    h, mi, lo = _split3(x)
    return dot(h, m) + dot(mi, m) + dot(lo, m)


def _dot_exact_l(m, x, dot=_dot):
    h, mi, lo = _split3(x)
    return dot(m, h) + dot(m, mi) + dot(m, lo)


def _sigmoid(x):
    return 1.0 / (1.0 + jnp.exp(-x))


def _silu(x):
    return x * _sigmoid(x)


def _silu_grad(x):
    s = _sigmoid(x)
    return s * (1.0 + x * (1.0 - s))


def _softplus(x):
    return jnp.maximum(x, 0.0) + jnp.log(1.0 + jnp.exp(-jnp.abs(x)))


def _rms_fwd(h, gain):
    r = lax.rsqrt(jnp.mean(h * h, axis=-1, keepdims=True) + RMS_EPS)
    return h * r * gain


def _rms_bwd(h, gain, dy):
    r = lax.rsqrt(jnp.mean(h * h, axis=-1, keepdims=True) + RMS_EPS)
    dyg = dy * gain
    dh = r * dyg - h * (r * r * r) * jnp.mean(dyg * h, axis=-1, keepdims=True)
    return dh, dy * h * r


def _iota(shape, dim):
    return lax.broadcasted_iota(jnp.int32, shape, dim)


def _lane_pick(x, idx):
    return jnp.sum(jnp.where(_iota(x.shape, 1) == idx, x, 0.0), axis=1, keepdims=True)


def _mm_nn(a, b, *, out_dtype, name):
    m, k = a.shape
    n = b.shape[1]
    tm, tn = _pick(m, (384, 256, 128)), _pick(n, (1024, 512, 256, 128))

    def body(a_ref, b_ref, o_ref):
        o_ref[...] = _dot(a_ref[...], b_ref[...]).astype(out_dtype)

    return pl.pallas_call(
        body, grid=(m // tm, n // tn),
        in_specs=[pl.BlockSpec((tm, k), lambda i, j: (i, 0)), pl.BlockSpec((k, tn), lambda i, j: (0, j))],
        out_specs=pl.BlockSpec((tm, tn), lambda i, j: (i, j)),
        out_shape=jax.ShapeDtypeStruct((m, n), out_dtype), name=name,
        compiler_params=_params("parallel", "parallel"))(a, b)


def _mm_nt(a, b, *, out_dtype, name):
    m, k = a.shape
    n = b.shape[0]
    tm, tn = _pick(m, (384, 256, 128)), _pick(n, (1024, 512, 256, 128))

    def body(a_ref, b_ref, o_ref):
        o_ref[...] = _dot_nt(a_ref[...], b_ref[...]).astype(out_dtype)

    return pl.pallas_call(
        body, grid=(m // tm, n // tn),
        in_specs=[pl.BlockSpec((tm, k), lambda i, j: (i, 0)), pl.BlockSpec((tn, k), lambda i, j: (j, 0))],
        out_specs=pl.BlockSpec((tm, tn), lambda i, j: (i, j)),
        out_shape=jax.ShapeDtypeStruct((m, n), out_dtype), name=name,
        compiler_params=_params("parallel", "parallel"))(a, b)


def _mm_tn(a, b, *, name):
    t, m = a.shape
    n = b.shape[1]
    tn = _pick(n, (2816, 2048, 1408, 1024, 512, 256, 128))
    tm = _pick(m, tuple(c for c in (1408, 1024, 512, 256, 128) if c * tn * 4 <= MM_TN_OUT_BLOCK_BYTES))
    tk = _pick(t, (1408, 1024, 512, 384, 256, 128))

    def body(a_ref, b_ref, o_ref):
        @pl.when(pl.program_id(2) == 0)
        def _():
            o_ref[...] = jnp.zeros_like(o_ref)

        o_ref[...] += _dot_tn(a_ref[...], b_ref[...])

    return pl.pallas_call(
        body, grid=(m // tm, n // tn, t // tk),
        in_specs=[pl.BlockSpec((tk, tm), lambda i, j, k: (k, i)), pl.BlockSpec((tk, tn), lambda i, j, k: (k, j))],
        out_specs=pl.BlockSpec((tm, tn), lambda i, j, k: (i, j)),
        out_shape=jax.ShapeDtypeStruct((m, n), F32), name=name,
        compiler_params=_params("parallel", "parallel", "arbitrary"))(a, b)


def _mm_norm(h, gain, w, *, name):
    m, k = h.shape
    n = w.shape[1]
    tm, tn = _pick(m, (384, 256, 128)), _pick(n, (1024, 512, 256, 128))

    def body(h_ref, g_ref, w_ref, o_ref, n_ref):
        @pl.when(pl.program_id(1) == 0)
        def _():
            n_ref[...] = _bf(_rms_fwd(h_ref[...], g_ref[...]))

        o_ref[...] = _dot(n_ref[...], w_ref[...])

    return pl.pallas_call(
        body, grid=(m // tm, n // tn),
        in_specs=[pl.BlockSpec((tm, k), lambda i, j: (i, 0)), pl.BlockSpec((1, k), lambda i, j: (0, 0)),
                  pl.BlockSpec((k, tn), lambda i, j: (0, j))],
        out_specs=[pl.BlockSpec((tm, tn), lambda i, j: (i, j)), pl.BlockSpec((tm, k), lambda i, j: (i, 0))],
        out_shape=[jax.ShapeDtypeStruct((m, n), F32), jax.ShapeDtypeStruct((m, k), BF16)], name=name,
        compiler_params=_params("parallel", "arbitrary"))(h, gain, w)


def _mm_norm_swiglu(h, gain, wg, wu, *, name):
    m, k = h.shape
    n = wg.shape[1]
    tm, tn = _pick(m, (384, 256, 128)), _pick(n, (1408, 1024, 512, 256, 128))

    def body(h_ref, g_ref, wg_ref, wu_ref, gate_ref, up_ref, act_ref, n_ref):
        @pl.when(pl.program_id(1) == 0)
        def _():
            n_ref[...] = _bf(_rms_fwd(h_ref[...], g_ref[...]))

        gate = _dot(n_ref[...], wg_ref[...])
        up = _dot(n_ref[...], wu_ref[...])
        gate_ref[...] = gate
        up_ref[...] = up
        act_ref[...] = _bf(_silu(gate) * up)

    wspec = pl.BlockSpec((k, tn), lambda i, j: (0, j))
    ospec = pl.BlockSpec((tm, tn), lambda i, j: (i, j))
    return pl.pallas_call(
        body, grid=(m // tm, n // tn),
        in_specs=[pl.BlockSpec((tm, k), lambda i, j: (i, 0)), pl.BlockSpec((1, k), lambda i, j: (0, 0)), wspec, wspec],
        out_specs=[ospec, ospec, ospec, pl.BlockSpec((tm, k), lambda i, j: (i, 0))],
        out_shape=[jax.ShapeDtypeStruct((m, n), F32), jax.ShapeDtypeStruct((m, n), F32),
                   jax.ShapeDtypeStruct((m, n), BF16), jax.ShapeDtypeStruct((m, k), BF16)], name=name,
        compiler_params=_params("parallel", "arbitrary"))(h, gain, wg, wu)


def _mm_res(res, a, b, *, name):
    m, k = a.shape
    n = b.shape[1]
    tm, tn = _pick(m, (384, 256, 128)), _pick(n, (1024, 512, 256, 128))

    def body(r_ref, a_ref, b_ref, o_ref):
        o_ref[...] = r_ref[...] + _dot(a_ref[...], b_ref[...])

    return pl.pallas_call(
        body, grid=(m // tm, n // tn),
        in_specs=[pl.BlockSpec((tm, tn), lambda i, j: (i, j)), pl.BlockSpec((tm, k), lambda i, j: (i, 0)),
                  pl.BlockSpec((k, tn), lambda i, j: (0, j))],
        out_specs=pl.BlockSpec((tm, tn), lambda i, j: (i, j)),
        out_shape=jax.ShapeDtypeStruct((m, n), F32), name=name,
        compiler_params=_params("parallel", "parallel"))(res, a, b)


def _mm_res_loss(res, a, b, target, *, row0, nrows, name):
    m, k = a.shape
    n = b.shape[1]
    tm = _pick(m, (384, 256, 128))

    def body(r_ref, a_ref, b_ref, t_ref, dy_ref, dyb_ref, ls_ref):
        i = pl.program_id(0)

        @pl.when(i == 0)
        def _():
            ls_ref[...] = jnp.zeros_like(ls_ref)

        y = r_ref[...] + _dot(a_ref[...], b_ref[...])
        row = i * tm + _iota((tm, n), 0)
        e = jnp.where((row >= row0) & (row < row0 + nrows), y - t_ref[...], 0.0)
        dy = e / n
        dy_ref[...] = dy
        dyb_ref[...] = _bf(dy)
        ls_ref[...] += jnp.sum(e * e, axis=0, keepdims=True)

    rspec = pl.BlockSpec((tm, n), lambda i: (i, 0))
    return pl.pallas_call(
        body, grid=(m // tm,),
        in_specs=[rspec, pl.BlockSpec((tm, k), lambda i: (i, 0)), pl.BlockSpec((k, n), lambda i: (0, 0)), rspec],
        out_specs=[rspec, rspec, pl.BlockSpec((1, n), lambda i: (0, 0))],
        out_shape=[jax.ShapeDtypeStruct((m, n), F32), jax.ShapeDtypeStruct((m, n), BF16),
                   jax.ShapeDtypeStruct((1, n), F32)], name=name,
        compiler_params=_params("arbitrary"))(res, a, b, target)


def _merge_fwd(o_dn, o_sb, wbd, wbs, proj, *, col_gd, col_gs, name):
    m, kd = o_dn.shape
    ks = o_sb.shape[1]
    n = wbd.shape[1]
    tm = _pick(m, (384, 256, 128))
    tn = _pick(math.gcd(n, math.gcd(col_gd, col_gs)), (512, 256, 128))

    def body(od_ref, os_ref, wd_ref, ws_ref, gd_ref, gs_ref, mg_ref, bd_ref, bs_ref):
        bd = _dot(od_ref[...], wd_ref[...])
        bs = _dot(os_ref[...], ws_ref[...])
        bd_ref[...] = bd
        bs_ref[...] = bs
        mg_ref[...] = _bf(_sigmoid(gd_ref[...]) * bd + _sigmoid(gs_ref[...]) * bs)

    ospec = pl.BlockSpec((tm, tn), lambda i, j: (i, j))
    return pl.pallas_call(
        body, grid=(m // tm, n // tn),
        in_specs=[pl.BlockSpec((tm, kd), lambda i, j: (i, 0)), pl.BlockSpec((tm, ks), lambda i, j: (i, 0)),
                  pl.BlockSpec((kd, tn), lambda i, j: (0, j)), pl.BlockSpec((ks, tn), lambda i, j: (0, j)),
                  pl.BlockSpec((tm, tn), lambda i, j: (i, col_gd // tn + j)),
                  pl.BlockSpec((tm, tn), lambda i, j: (i, col_gs // tn + j))],
        out_specs=[ospec, ospec, ospec],
        out_shape=[jax.ShapeDtypeStruct((m, n), BF16), jax.ShapeDtypeStruct((m, n), F32),
                   jax.ShapeDtypeStruct((m, n), F32)], name=name,
        compiler_params=_params("parallel", "parallel"))(o_dn, o_sb, wbd, wbs, proj, proj)


def _merge_bwd(dh, w_out, proj, br_dn, br_sb, *, col_gd, col_gs, name):
    m, k = dh.shape
    n = w_out.shape[0]
    tm = _pick(m, (384, 256, 128))
    tn = _pick(math.gcd(n, math.gcd(col_gd, col_gs)), (512, 256, 128))

    def body(dh_ref, w_ref, gd_ref, gs_ref, bd_ref, bs_ref, dbd_ref, dbs_ref, dgd_ref, dgs_ref):
        dm = _dot_nt(dh_ref[...], w_ref[...])
        sd = _sigmoid(gd_ref[...])
        ss = _sigmoid(gs_ref[...])
        dbd_ref[...] = _bf(dm * sd)
        dbs_ref[...] = _bf(dm * ss)
        dgd_ref[...] = _bf(dm * bd_ref[...] * sd * (1.0 - sd))
        dgs_ref[...] = _bf(dm * bs_ref[...] * ss * (1.0 - ss))

    ospec = pl.BlockSpec((tm, tn), lambda i, j: (i, j))
    return pl.pallas_call(
        body, grid=(m // tm, n // tn),
        in_specs=[pl.BlockSpec((tm, k), lambda i, j: (i, 0)), pl.BlockSpec((tn, k), lambda i, j: (j, 0)),
                  pl.BlockSpec((tm, tn), lambda i, j: (i, col_gd // tn + j)),
                  pl.BlockSpec((tm, tn), lambda i, j: (i, col_gs // tn + j)), ospec, ospec],
        out_specs=[ospec] * 4,
        out_shape=[jax.ShapeDtypeStruct((m, n), BF16)] * 4, name=name,
        compiler_params=_params("parallel", "parallel"))(dh, w_out, proj, proj, br_dn, br_sb)


def _swiglu_bwd(dy, wfo, gate, up, *, name):
    m, k = dy.shape
    n = wfo.shape[0]
    tm, tn = _pick(m, (384, 256, 128)), _pick(n, (1408, 1024, 512, 256, 128))

    def body(dy_ref, w_ref, g_ref, u_ref, dg_ref, du_ref):
        da = _dot_nt(dy_ref[...], w_ref[...])
        g = g_ref[...]
        dg_ref[...] = _bf(da * u_ref[...] * _silu_grad(g))
        du_ref[...] = _bf(da * _silu(g))

    ospec = pl.BlockSpec((tm, tn), lambda i, j: (i, j))
    return pl.pallas_call(
        body, grid=(m // tm, n // tn),
        in_specs=[pl.BlockSpec((tm, k), lambda i, j: (i, 0)), pl.BlockSpec((tn, k), lambda i, j: (j, 0)), ospec, ospec],
        out_specs=[ospec, ospec], out_shape=[jax.ShapeDtypeStruct((m, n), BF16)] * 2, name=name,
        compiler_params=_params("parallel", "parallel"))(dy, wfo, gate, up)


def _mm_nt_rmsbwd(pairs, extra, h, gain, dres, *, name):
    m, k = pairs[0][0].shape
    n = h.shape[1]
    tm = _pick(m, (384, 256, 128))
    tk = _pick(k, (1408, 1024, 512, 256, 128))
    nk = k // tk
    np_ = len(pairs)

    def body(*refs):
        ab = refs[:2 * np_]
        pos = 2 * np_
        ex = refs[pos:pos + 2] if extra is not None else ()
        pos += len(ex)
        h_ref, g_ref, r_ref, dh_ref, dhb_ref, dg_ref, acc_ref = refs[pos:]
        i, kk = pl.program_id(0), pl.program_id(1)

        @pl.when((i == 0) & (kk == 0))
        def _():
            dg_ref[...] = jnp.zeros_like(dg_ref)

        part = _dot_nt(ab[0][...], ab[1][...])
        for p in range(1, np_):
            part += _dot_nt(ab[2 * p][...], ab[2 * p + 1][...])

        @pl.when(kk == 0)
        def _():
            first = part
            if ex:
                first = first + _dot_nt(ex[0][...], ex[1][...])
            acc_ref[...] = first

        @pl.when(kk > 0)
        def _():
            acc_ref[...] += part

        @pl.when(kk == nk - 1)
        def _():
            dh, dgr = _rms_bwd(h_ref[...], g_ref[...], acc_ref[...])
            dh = dh + r_ref[...]
            dh_ref[...] = dh
            dhb_ref[...] = _bf(dh)
            dg_ref[...] += jnp.sum(dgr, axis=0, keepdims=True)

    in_specs, args = [], []
    for a, b in pairs:
        in_specs += [pl.BlockSpec((tm, tk), lambda i, kk: (i, kk)), pl.BlockSpec((n, tk), lambda i, kk: (0, kk))]
        args += [a, b]
    if extra is not None:
        k2 = extra[0].shape[1]
        in_specs += [pl.BlockSpec((tm, k2), lambda i, kk: (i, 0)), pl.BlockSpec((n, k2), lambda i, kk: (0, 0))]
        args += list(extra)
    rspec = pl.BlockSpec((tm, n), lambda i, kk: (i, 0))
    in_specs += [rspec, pl.BlockSpec((1, n), lambda i, kk: (0, 0)), rspec]
    return pl.pallas_call(
        body, grid=(m // tm, nk), in_specs=in_specs,
        out_specs=[rspec, rspec, pl.BlockSpec((1, n), lambda i, kk: (0, 0))],
        out_shape=[jax.ShapeDtypeStruct((m, n), F32), jax.ShapeDtypeStruct((m, n), BF16),
                   jax.ShapeDtypeStruct((1, n), F32)],
        scratch_shapes=[pltpu.VMEM((tm, n), F32)], name=name,
        compiler_params=_params("arbitrary", "arbitrary"))(*args, h, gain, dres)


def _conv_taps(cur, prev8, w_ref, first):
    nk = w_ref.shape[0]
    out = cur * w_ref[nk - 1:nk, :]
    for s in range(1, nk):
        out += _shift_down(cur, prev8, s, first) * w_ref[nk - 1 - s:nk - s, :]
    return out


def _shift_up(cur, next8, s, last):
    rows = cur.shape[0]
    row = _iota(cur.shape, 0)
    next8 = jnp.where(last, 0.0, next8)
    sh = pltpu.roll(cur, rows - s, axis=0)
    nh = jnp.tile(pltpu.roll(next8, 8 - s, axis=0), (rows // 8, 1))
    return jnp.where(row >= rows - s, nh, sh)


def _shift_down(cur, prev8, s, first):
    rows = cur.shape[0]
    row = _iota(cur.shape, 0)
    prev8 = jnp.where(first, 0.0, prev8)
    sh = pltpu.roll(cur, s, axis=0)
    ph = jnp.tile(pltpu.roll(prev8, s, axis=0), (rows // 8, 1))
    return jnp.where(row < s, ph, sh)


def _gdn_pre(proj, pab, cq, ck, cv, a_log, dt_bias, *, heads, dk, dv, col_q, col_k, col_v, row_lo, row_hi, name):
    t = proj.shape[0]
    tm = _pick(t, (384, 256, 128))
    nb = t // tm

    def body(pq_ref, pqp_ref, pk_ref, pkp_ref, pv_ref, pvp_ref, ab_ref, cq_ref, ck_ref, cv_ref, al_ref, dt_ref,
             qn_ref, kn_ref, v_ref, g_ref, b_ref):
        h, i = pl.program_id(0), pl.program_id(1)
        first = i == 0
        row = i * tm + _iota((tm, 1), 0)
        valid = (row >= row_lo) & (row < row_hi)
        q1 = _silu(_conv_taps(pq_ref[...], pqp_ref[...], cq_ref, first))
        k1 = _silu(_conv_taps(pk_ref[...], pkp_ref[...], ck_ref, first))
        v1 = _silu(_conv_taps(pv_ref[...], pvp_ref[...], cv_ref, first))
        qn_ref[...] = jnp.where(valid, q1 * lax.rsqrt(jnp.sum(q1 * q1, axis=-1, keepdims=True) + L2_EPS), 0.0)
        kn_ref[...] = jnp.where(valid, k1 * lax.rsqrt(jnp.sum(k1 * k1, axis=-1, keepdims=True) + L2_EPS), 0.0)
        v_ref[...] = jnp.where(valid, v1, 0.0)
        ab = ab_ref[...]
        da = _lane_pick(ab, h)
        db = _lane_pick(ab, heads + h)
        a = _lane_pick(al_ref[...], h)
        dtb = _lane_pick(dt_ref[...], h)
        g_ref[...] = jnp.where(valid, -jnp.exp(a) * _softplus(da + dtb), 0.0)
        b_ref[...] = jnp.where(valid, _sigmoid(db), 0.0)

    def cur(width, col):
        return pl.BlockSpec((tm, width), lambda h, i: (i, col // width + h))

    def prev(width, col):
        return pl.BlockSpec((8, width), lambda h, i: (jnp.maximum(i * (tm // 8) - 1, 0), col // width + h))

    def out(width):
        return pl.BlockSpec((None, tm, width), lambda h, i: (h, i, 0))

    small = pl.BlockSpec((1, LANES), lambda h, i: (0, 0))
    return pl.pallas_call(
        body, grid=(heads, nb),
        in_specs=[cur(dk, col_q), prev(dk, col_q), cur(dk, col_k), prev(dk, col_k), cur(dv, col_v), prev(dv, col_v),
                  pl.BlockSpec((tm, LANES), lambda h, i: (i, 0)),
                  pl.BlockSpec((cq.shape[0], dk), lambda h, i: (0, h)), pl.BlockSpec((ck.shape[0], dk), lambda h, i: (0, h)),
                  pl.BlockSpec((cv.shape[0], dv), lambda h, i: (0, h)), small, small],
        out_specs=[out(dk), out(dk), out(dv), out(1), out(1)],
        out_shape=[jax.ShapeDtypeStruct((heads, t, dk), F32), jax.ShapeDtypeStruct((heads, t, dk), F32),
                   jax.ShapeDtypeStruct((heads, t, dv), F32), jax.ShapeDtypeStruct((heads, t, 1), F32),
                   jax.ShapeDtypeStruct((heads, t, 1), F32)], name=name,
        compiler_params=_params("parallel", "parallel"))(proj, proj, proj, proj, proj, proj, pab, cq, ck, cv, a_log, dt_bias)


def _chunk_masks(rows=GDN_ROWS, row0=0):
    ri = row0 + _iota((rows, GDN_ROWS), 0)
    ci = _iota((rows, GDN_ROWS), 1)
    same = jnp.right_shift(ri, CHUNK_SHIFT) == jnp.right_shift(ci, CHUNK_SHIFT)
    return same, same & (ri >= ci), same & (ri > ci), ri == ci


def _col_to_row(col, eye):
    return jnp.sum(jnp.where(eye, col, 0.0), axis=0, keepdims=True)


def _row_to_col(row, eye):
    return jnp.sum(jnp.where(eye, row, 0.0), axis=1, keepdims=True)


def _chunk_common(qn, kn, g, beta, dk_scale):
    same, incl, strict, eye = _chunk_masks()
    gb = jnp.broadcast_to(g, (GDN_ROWS, LANES))
    gam = jnp.max(_dot_exact_l(jnp.where(incl, 1.0, 0.0).astype(BF16), gb), axis=1, keepdims=True)
    gam_last = jnp.max(_dot_exact_l(jnp.where(same, 1.0, 0.0).astype(BF16), gb), axis=1, keepdims=True)
    diff = gam - _col_to_row(gam, eye)
    decay = jnp.where(incl, jnp.exp(jnp.where(incl, diff, 0.0)), 0.0)
    eg = jnp.exp(gam)
    ek = jnp.exp(gam_last - gam)
    kb = kn * beta
    qt = qn * dk_scale
    lmat = jnp.where(strict, _dot_nt(_bf(kb), _bf(kn)) * decay, 0.0)
    pmat = jnp.where(incl, _dot_nt(_bf(qt), _bf(kn)) * decay, 0.0)
    return dict(incl=incl, strict=strict, eye=eye, decay=decay, eg=eg, ek=ek, egl=jnp.exp(gam_last),
                kb=kb, qt=qt, lmat=lmat, pmat=pmat)


def _gdn_prep(qn, kn, v, g, beta, *, name):
    heads, t, dk = qn.shape
    dv = v.shape[2]
    rows = _pick(t, (3 * GDN_ROWS, 2 * GDN_ROWS, GDN_ROWS))
    dk_scale = dk ** -0.5

    def body(q_ref, k_ref, v_ref, g_ref, b_ref, u_ref, w_ref, p_ref, qd_ref, kd_ref, egl_ref, t_ref):
        for b in range(rows // GDN_ROWS):
            r = pl.ds(b * GDN_ROWS, GDN_ROWS)
            kn_, beta_ = k_ref[r, :], b_ref[r, :]
            c = _chunk_common(q_ref[r, :], kn_, g_ref[r, :], beta_, dk_scale)
            eye_f = jnp.where(c["eye"], 1.0, 0.0)
            x = c["lmat"]
            tinv = eye_f - x
            for _ in range(CHUNK_SHIFT - 1):
                x = _dot_hp(x, x)
                tinv = tinv + _dot_hp(tinv, x)
            u_ref[r, :] = _dot_hp(tinv, v_ref[r, :] * beta_)
            w_ref[r, :] = _dot_hp(tinv, c["kb"] * c["eg"])
            p_ref[r, :] = c["pmat"]
            qd_ref[r, :] = c["qt"] * c["eg"]
            kd_ref[r, :] = kn_ * c["ek"]
            egl_ref[r, :] = c["egl"]
            t_ref[r, :] = tinv

    def blk(width):
        return pl.BlockSpec((None, rows, width), lambda h, i: (h, i, 0))

    def shp(width):
        return jax.ShapeDtypeStruct((heads, t, width), F32)

    return pl.pallas_call(
        body, grid=(heads, t // rows),
        in_specs=[blk(dk), blk(dk), blk(dv), blk(1), blk(1)],
        out_specs=[blk(dv), blk(dk), blk(GDN_ROWS), blk(dk), blk(dk), blk(1), blk(GDN_ROWS)],
        out_shape=[shp(dv), shp(dk), shp(GDN_ROWS), shp(dk), shp(dk), shp(1), shp(GDN_ROWS)], name=name,
        compiler_params=_params("parallel", "parallel"))(qn, kn, v, g, beta)


def _gdn_scan(u, w, p, qd, kd, egl, proj, gain, *, col_z, name):
    heads, t, dv = u.shape
    dk = w.shape[2]
    nb = t // GDN_ROWS
    sub = GDN_ROWS // CHUNK
    hp = SCAN_HEADS

    def body(u_ref, w_ref, p_ref, qd_ref, kd_ref, egl_ref, z_ref, gn_ref, o_ref, og_ref, st_ref, s_ref):
        @pl.when(pl.program_id(1) == 0)
        def _():
            s_ref[...] = jnp.zeros_like(s_ref)

        for hh in range(hp):
            cols = pl.ds(hh * dv, dv)
            vn_parts = [jnp.zeros((CHUNK, dv), F32)] * sub
            for c in range(sub):
                r = pl.ds(c * CHUNK, CHUNK)
                s = s_ref[hh]
                st_ref[hh, c] = s
                sb = _bf(s)
                vn = u_ref[hh, r, :] - _dot(_bf(w_ref[hh, r, :]), sb)
                vn_parts[c] = vn
                vfull = _bf(jnp.concatenate(vn_parts, axis=0))
                o = _dot(_bf(qd_ref[hh, r, :]), sb) + _dot(_bf(p_ref[hh, r, :]), vfull)
                gl = egl_ref[hh, pl.ds(c * CHUNK, 1), :]
                s_ref[hh] = s * gl + _dot_tn(_bf(kd_ref[hh, r, :]), _bf(vn))
                o_ref[hh, r, :] = o
                og_ref[r, cols] = _bf(_rms_fwd(o, gn_ref[...]) * _silu(z_ref[r, cols]))

    def blk(width):
        return pl.BlockSpec((hp, GDN_ROWS, width), lambda h, i: (h, i, 0))

    return pl.pallas_call(
        body, grid=(heads // hp, nb),
        in_specs=[blk(dv), blk(dk), blk(GDN_ROWS), blk(dk), blk(dk), blk(1),
                  pl.BlockSpec((GDN_ROWS, hp * dv), lambda h, i: (i, col_z // (hp * dv) + h)),
                  pl.BlockSpec((1, dv), lambda h, i: (0, 0))],
        out_specs=[blk(dv), pl.BlockSpec((GDN_ROWS, hp * dv), lambda h, i: (i, h)),
                   pl.BlockSpec((hp, sub, dk, dv), lambda h, i: (h, i, 0, 0))],
        out_shape=[jax.ShapeDtypeStruct((heads, t, dv), F32), jax.ShapeDtypeStruct((t, heads * dv), BF16),
                   jax.ShapeDtypeStruct((heads, t // CHUNK, dk, dv), F32)],
        scratch_shapes=[pltpu.VMEM((hp, dk, dv), F32)], name=name,
        compiler_params=_params("parallel", "arbitrary"))(u, w, p, qd, kd, egl, proj, gain)


def _gdn_post_bwd(o, proj, gain, dout, *, col_z, name):
    heads, t, dv = o.shape
    tm = _pick(t, (384, 256, 128))

    def body(o_ref, z_ref, gn_ref, d_ref, do_ref, dz_ref, dg_ref):
        @pl.when((pl.program_id(0) == 0) & (pl.program_id(1) == 0))
        def _():
            dg_ref[...] = jnp.zeros_like(dg_ref)

        o_, z, d = o_ref[...], z_ref[...], d_ref[...]
        y = _rms_fwd(o_, gn_ref[...])
        dz_ref[...] = _bf(d * y * _silu_grad(z))
        do, dgr = _rms_bwd(o_, gn_ref[...], d * _silu(z))
        do_ref[...] = do
        dg_ref[...] += jnp.sum(dgr, axis=0, keepdims=True)

    return pl.pallas_call(
        body, grid=(t // tm, heads),
        in_specs=[pl.BlockSpec((None, tm, dv), lambda i, h: (h, i, 0)),
                  pl.BlockSpec((tm, dv), lambda i, h: (i, col_z // dv + h)),
                  pl.BlockSpec((1, dv), lambda i, h: (0, 0)), pl.BlockSpec((tm, dv), lambda i, h: (i, h))],
        out_specs=[pl.BlockSpec((None, tm, dv), lambda i, h: (h, i, 0)), pl.BlockSpec((tm, dv), lambda i, h: (i, h)),
                   pl.BlockSpec((1, dv), lambda i, h: (0, 0))],
        out_shape=[jax.ShapeDtypeStruct((heads, t, dv), F32), jax.ShapeDtypeStruct((t, heads * dv), BF16),
                   jax.ShapeDtypeStruct((1, dv), F32)], name=name,
        compiler_params=_params("arbitrary", "arbitrary"))(o, proj, gain, dout)


def _gdn_bwd_scan(u, w, p, qd, kd, egl, st, do, *, name):
    heads, t, dv = u.shape
    dk = w.shape[2]
    nb = t // GDN_ROWS
    sub = GDN_ROWS // CHUNK
    hp = SCAN_HEADS

    def body(u_ref, w_ref, p_ref, qd_ref, kd_ref, egl_ref, st_ref, do_ref,
             du_ref, dw_ref, dp_ref, dqd_ref, dkd_ref, dgl_ref, ds_ref):
        @pl.when(pl.program_id(1) == 0)
        def _():
            ds_ref[...] = jnp.zeros_like(ds_ref)

        for hh in range(hp):
            for c in reversed(range(sub)):
                r = pl.ds(c * CHUNK, CHUNK)
                s = st_ref[hh, c]
                sb = _bf(s)
                ds = ds_ref[hh]
                dsb = _bf(ds)
                dob = _bf(do_ref[hh, r, :])
                wb, pb = _bf(w_ref[hh, r, :]), _bf(p_ref[hh, r, :])
                qdb, kdb = _bf(qd_ref[hh, r, :]), _bf(kd_ref[hh, r, :])
                vn = u_ref[hh, r, :] - _dot(wb, sb)
                zeros = jnp.zeros((CHUNK, dv), BF16)
                vfull = jnp.concatenate([_bf(vn) if cc == c else zeros for cc in range(sub)], axis=0)
                dvn = _dot_tn(pb, dob)[c * CHUNK:(c + 1) * CHUNK, :] + _dot(kdb, dsb)
                dvb = _bf(dvn)
                gl = egl_ref[hh, pl.ds(c * CHUNK, 1), :]
                du_ref[hh, r, :] = dvn
                dw_ref[hh, r, :] = -_dot_nt(dvb, sb)
                dp_ref[hh, r, :] = jnp.where(_chunk_masks(CHUNK, c * CHUNK)[1], _dot_nt(dob, vfull), 0.0)
                dqd_ref[hh, r, :] = _dot_nt(dob, sb)
                dkd_ref[hh, r, :] = _dot_nt(_bf(vn), dsb)
                dgl = jnp.sum(jnp.sum(ds * s, axis=1, keepdims=True), axis=0, keepdims=True)
                dgl_ref[hh, r, :] = jnp.where(_iota((CHUNK, 1), 0) == CHUNK - 1, dgl, 0.0)
                ds_ref[hh] = ds * gl + _dot_tn(qdb, dob) - _dot_tn(wb, dvb)

    def blk(width):
        return pl.BlockSpec((hp, GDN_ROWS, width), lambda h, i: (h, nb - 1 - i, 0))

    def shp(width):
        return jax.ShapeDtypeStruct((heads, t, width), F32)

    return pl.pallas_call(
        body, grid=(heads // hp, nb),
        in_specs=[blk(dv), blk(dk), blk(GDN_ROWS), blk(dk), blk(dk), blk(1),
                  pl.BlockSpec((hp, sub, dk, dv), lambda h, i: (h, nb - 1 - i, 0, 0)), blk(dv)],
        out_specs=[blk(dv), blk(dk), blk(GDN_ROWS), blk(dk), blk(dk), blk(1)],
        out_shape=[shp(dv), shp(dk), shp(GDN_ROWS), shp(dk), shp(dk), shp(1)],
        scratch_shapes=[pltpu.VMEM((hp, dk, dv), F32)], name=name,
        compiler_params=_params("parallel", "arbitrary"))(u, w, p, qd, kd, egl, st, do)


def _gdn_bwd_prep(qn, kn, v, g, beta, tinv, u, w, du, dw, dp, dqd, dkd, dgl, *, name):
    heads, t, dk = qn.shape
    dv = v.shape[2]
    rows = _pick(t, (3 * GDN_ROWS, 2 * GDN_ROWS, GDN_ROWS))
    dk_scale = dk ** -0.5

    def rowsum(x):
        return jnp.sum(x, axis=1, keepdims=True)

    def body(q_ref, k_ref, v_ref, g_ref, b_ref, t_ref, u_ref, w_ref, du_ref, dw_ref, dp_ref, dqd_ref, dkd_ref, dgl_ref,
             dq_ref, dkk_ref, dvv_ref, dg_ref, db_ref):
        for b in range(rows // GDN_ROWS):
            r = pl.ds(b * GDN_ROWS, GDN_ROWS)
            kn_, beta_, v_ = k_ref[r, :], b_ref[r, :], v_ref[r, :]
            c = _chunk_common(q_ref[r, :], kn_, g_ref[r, :], beta_, dk_scale)
            eye, strict, decay = c["eye"], c["strict"], c["decay"]
            kb, qt, eg, ek = c["kb"], c["qt"], c["eg"], c["ek"]
            tinv_ = t_ref[r, :]
            dbv = _dot_hp(tinv_, du_ref[r, :], _dot_tn)
            dbw = _dot_hp(tinv_, dw_ref[r, :], _dot_tn)
            da = -(_dot_nt(_bf(dbv), _bf(u_ref[r, :])) + _dot_nt(_bf(dbw), _bf(w_ref[r, :])))
            dl = jnp.where(strict, da, 0.0)
            dp_ = dp_ref[r, :]
            dm = _bf(dl * decay)
            dn = _bf(dp_ * decay)
            knb = _bf(kn_)
            dkb = _dot(dm, knb) + dbw * eg
            dkn = _dot_tn(dm, _bf(kb)) + _dot_tn(dn, _bf(qt))
            dqt = _dot(dn, knb)
            gmat = dl * c["lmat"] + dp_ * c["pmat"]
            dqd_, dkd_ = dqd_ref[r, :], dkd_ref[r, :]
            qd = qt * eg
            kd = kn_ * ek
            bw = kb * eg
            kdsum = rowsum(dkd_ * kd)
            dgam = rowsum(gmat) - _row_to_col(jnp.sum(gmat, axis=0, keepdims=True), eye)
            dgam += rowsum(dbw * bw) + rowsum(dqd_ * qd) - kdsum
            last = (_iota((GDN_ROWS, 1), 0) & (CHUNK - 1)) == CHUNK - 1
            same = _chunk_masks()[0]
            same_f = jnp.where(same, 1.0, 0.0).astype(BF16)
            chunk_tot = jnp.max(_dot_exact_l(same_f, jnp.broadcast_to(kdsum, (GDN_ROWS, LANES))), axis=1, keepdims=True)
            dgam += jnp.where(last, chunk_tot, 0.0) + dgl_ref[r, :] * c["egl"]
            dq_ref[r, :] = (dqt + dqd_ * eg) * dk_scale
            dkk_ref[r, :] = dkn + dkd_ * ek + dkb * beta_
            dvv_ref[r, :] = dbv * beta_
            db_ref[r, :] = rowsum(dbv * v_) + rowsum(dkb * kn_)
            upper = jnp.where(same & (_iota((GDN_ROWS, GDN_ROWS), 0) <= _iota((GDN_ROWS, GDN_ROWS), 1)), 1.0, 0.0)
            dgb = _dot_exact_l(upper.astype(BF16), jnp.broadcast_to(dgam, (GDN_ROWS, LANES)))
            dg_ref[r, :] = _lane_pick(dgb, 0)

    def blk(width):
        return pl.BlockSpec((None, rows, width), lambda h, i: (h, i, 0))

    def shp(width):
        return jax.ShapeDtypeStruct((heads, t, width), F32)

    return pl.pallas_call(
        body, grid=(heads, t // rows),
        in_specs=[blk(dk), blk(dk), blk(dv), blk(1), blk(1), blk(GDN_ROWS), blk(dv), blk(dk),
                  blk(dv), blk(dk), blk(GDN_ROWS), blk(dk), blk(dk), blk(1)],
        out_specs=[blk(dk), blk(dk), blk(dv), blk(1), blk(1)],
        out_shape=[shp(dk), shp(dk), shp(dv), shp(1), shp(1)], name=name,
        compiler_params=_params("parallel", "parallel"))(qn, kn, v, g, beta, tinv, u, w, du, dw, dp, dqd, dkd, dgl)


def _gdn_pre_bwd_a(proj, pab, cq, ck, cv, a_log, dt_bias, dqn, dkn, dvv, dg, dbeta, *,
                   heads, dk, dv, col_q, col_k, col_v, row_lo, row_hi, name):
    t = proj.shape[0]
    tm = _pick(t, (384, 256, 128))
    nb = t // tm

    def body(pq_ref, pqp_ref, pk_ref, pkp_ref, pv_ref, pvp_ref, ab_ref, cq_ref, ck_ref, cv_ref, al_ref, dt_ref,
             dqn_ref, dkn_ref, dvv_ref, dg_ref, db_ref, dcq_ref, dck_ref, dcv_ref, dab_ref, dal_ref, ddt_ref):
        i, h = pl.program_id(0), pl.program_id(1)
        first = i == 0

        @pl.when((i == 0) & (h == 0))
        def _():
            dal_ref[...] = jnp.zeros_like(dal_ref)
            ddt_ref[...] = jnp.zeros_like(ddt_ref)

        @pl.when(h == 0)
        def _():
            dab_ref[...] = jnp.zeros_like(dab_ref)

        row = i * tm + _iota((tm, 1), 0)
        valid = (row >= row_lo) & (row < row_hi)

        def l2_bwd(c1, dn):
            x1 = _silu(c1)
            r = lax.rsqrt(jnp.sum(x1 * x1, axis=-1, keepdims=True) + L2_EPS)
            dn = jnp.where(valid, dn, 0.0)
            d1 = r * dn - x1 * (r * r * r) * jnp.sum(dn * x1, axis=-1, keepdims=True)
            return d1 * _silu_grad(c1)

        dcq_ref[...] = l2_bwd(_conv_taps(pq_ref[...], pqp_ref[...], cq_ref, first), dqn_ref[...])
        dck_ref[...] = l2_bwd(_conv_taps(pk_ref[...], pkp_ref[...], ck_ref, first), dkn_ref[...])
        cv1 = _conv_taps(pv_ref[...], pvp_ref[...], cv_ref, first)
        dcv_ref[...] = jnp.where(valid, dvv_ref[...], 0.0) * _silu_grad(cv1)
        ab = ab_ref[...]
        da = _lane_pick(ab, h)
        db = _lane_pick(ab, heads + h)
        a = _lane_pick(al_ref[...], h)
        dtb = _lane_pick(dt_ref[...], h)
        dgv = jnp.where(valid, dg_ref[...], 0.0)
        ea = jnp.exp(a)
        g = -ea * _softplus(da + dtb)
        dda = dgv * (-ea) * _sigmoid(da + dtb)
        beta = _sigmoid(db)
        ddb = jnp.where(valid, db_ref[...], 0.0) * beta * (1.0 - beta)
        lane = _iota((tm, LANES), 1)
        dab_ref[...] += jnp.where(lane == h, dda, 0.0) + jnp.where(lane == heads + h, ddb, 0.0)
        lane1 = _iota((1, LANES), 1)
        dal_ref[...] += jnp.where(lane1 == h, jnp.sum(dgv * g, axis=0, keepdims=True), 0.0)
        ddt_ref[...] += jnp.where(lane1 == h, jnp.sum(dda, axis=0, keepdims=True), 0.0)

    def cur(width, col):
        return pl.BlockSpec((tm, width), lambda i, h: (i, col // width + h))

    def prev(width, col):
        return pl.BlockSpec((8, width), lambda i, h: (jnp.maximum(i * (tm // 8) - 1, 0), col // width + h))

    def hd(width):
        return pl.BlockSpec((None, tm, width), lambda i, h: (h, i, 0))

    small = pl.BlockSpec((1, LANES), lambda i, h: (0, 0))
    return pl.pallas_call(
        body, grid=(nb, heads),
        in_specs=[cur(dk, col_q), prev(dk, col_q), cur(dk, col_k), prev(dk, col_k), cur(dv, col_v), prev(dv, col_v),
                  pl.BlockSpec((tm, LANES), lambda i, h: (i, 0)),
                  pl.BlockSpec((cq.shape[0], dk), lambda i, h: (0, h)), pl.BlockSpec((ck.shape[0], dk), lambda i, h: (0, h)),
                  pl.BlockSpec((cv.shape[0], dv), lambda i, h: (0, h)), small, small,
                  hd(dk), hd(dk), hd(dv), hd(1), hd(1)],
        out_specs=[hd(dk), hd(dk), hd(dv), pl.BlockSpec((tm, LANES), lambda i, h: (i, 0)), small, small],
        out_shape=[jax.ShapeDtypeStruct((heads, t, dk), F32), jax.ShapeDtypeStruct((heads, t, dk), F32),
                   jax.ShapeDtypeStruct((heads, t, dv), F32), jax.ShapeDtypeStruct((t, LANES), F32),
                   jax.ShapeDtypeStruct((1, LANES), F32), jax.ShapeDtypeStruct((1, LANES), F32)], name=name,
        compiler_params=_params("arbitrary", "arbitrary"))(
            proj, proj, proj, proj, proj, proj, pab, cq, ck, cv, a_log, dt_bias, dqn, dkn, dvv, dg, dbeta)


def _conv_bwd(proj, dc, cw, *, heads, width, col, name):
    t = proj.shape[0]
    tm = _pick(t, (384, 256, 128))
    nb = t // tm
    nk = cw.shape[0]

    def body(p_ref, pp_ref, d_ref, dn_ref, w_ref, dp_ref, dw_ref):
        i = pl.program_id(1)
        first, last = i == 0, i == nb - 1

        @pl.when(first)
        def _():
            dw_ref[...] = jnp.zeros_like(dw_ref)

        x, d = p_ref[...], d_ref[...]
        dx = d * w_ref[nk - 1:nk, :]
        dw_ref[nk - 1:nk, :] += jnp.sum(d * x, axis=0, keepdims=True)
        for s in range(1, nk):
            dx += _shift_up(d, dn_ref[...], s, last) * w_ref[nk - 1 - s:nk - s, :]
            dw_ref[nk - 1 - s:nk - s, :] += jnp.sum(d * _shift_down(x, pp_ref[...], s, first), axis=0, keepdims=True)
        dp_ref[...] = _bf(dx)

    return pl.pallas_call(
        body, grid=(heads, nb),
        in_specs=[pl.BlockSpec((tm, width), lambda h, i: (i, col // width + h)),
                  pl.BlockSpec((8, width), lambda h, i: (jnp.maximum(i * (tm // 8) - 1, 0), col // width + h)),
                  pl.BlockSpec((None, tm, width), lambda h, i: (h, i, 0)),
                  pl.BlockSpec((None, 8, width), lambda h, i: (h, jnp.minimum((i + 1) * (tm // 8), t // 8 - 1), 0)),
                  pl.BlockSpec((nk, width), lambda h, i: (0, h))],
        out_specs=[pl.BlockSpec((tm, width), lambda h, i: (i, h)), pl.BlockSpec((nk, width), lambda h, i: (0, h))],
        out_shape=[jax.ShapeDtypeStruct((t, heads * width), BF16), jax.ShapeDtypeStruct((nk, heads * width), F32)],
        name=name, compiler_params=_params("parallel", "arbitrary"))(proj, proj, dc, dc, cw)


def _sb_pre(proj, gq, gk, *, heads, dh, col_q, col_k, col_v, name):
    t = proj.shape[0]
    tm = _pick(t, (384, 256, 128))

    def body(q_ref, k_ref, v_ref, gq_ref, gk_ref, qo_ref, ko_ref, vo_ref):
        qo_ref[...] = _bf(_rms_fwd(q_ref[...], gq_ref[...]))
        ko_ref[...] = _bf(_rms_fwd(k_ref[...], gk_ref[...]))
        vo_ref[...] = _bf(v_ref[...])

    def cur(col):
        return pl.BlockSpec((tm, dh), lambda i, h: (i, col // dh + h))

    gspec = pl.BlockSpec((1, dh), lambda i, h: (0, 0))
    ospec = pl.BlockSpec((tm, dh), lambda i, h: (i, h))
    return pl.pallas_call(
        body, grid=(t // tm, heads), in_specs=[cur(col_q), cur(col_k), cur(col_v), gspec, gspec],
        out_specs=[ospec] * 3, out_shape=[jax.ShapeDtypeStruct((t, heads * dh), BF16)] * 3, name=name,
        compiler_params=_params("parallel", "parallel"))(proj, proj, proj, gq, gk)


def _sb_tile(q, kb, i, j, blk, key_lo, scale):
    z = _dot_nt(q, kb) * scale
    qpos = i * blk + _iota((blk, blk), 0)
    kpos = j * blk + _iota((blk, blk), 1)
    vis = (kpos < qpos) & (kpos >= key_lo)
    ls = jnp.minimum(z, 0.0) - jnp.log(1.0 + jnp.exp(-jnp.abs(z)))
    return vis, ls, jnp.where(vis, ls - z, 0.0)


def _dot2_r(x, m):
    hi, lo = _split2(x)
    return _dot(hi, m) + _dot(lo, m)


def _suffix_sums(x, later):
    return jnp.concatenate([_dot2_r(x[:, s:], later[s:, s:s + LANES]) for s in range(0, x.shape[1], LANES)], axis=1)


def _prefix_sums(x, earlier):
    return jnp.concatenate([_dot2_r(x[:, :s + LANES], earlier[:s + LANES, s:s + LANES])
                            for s in range(0, x.shape[1], LANES)], axis=1)


def _sb_fwd(qs, ks, vs, *, heads, dh, key_lo, name):
    t = qs.shape[0]
    blk = _pick(t, (3 * SB_BLOCK, 2 * SB_BLOCK, SB_BLOCK))
    nq = t // blk
    assert nq <= LANES
    scale = dh ** -0.5
    hp = SB_HEADS_PER_STEP

    def body(q_ref, k_ref, v_ref, o_ref, c_ref):
        i = pl.program_id(1)
        later = jnp.where(_iota((blk, blk), 0) > _iota((blk, blk), 1), 1.0, 0.0).astype(BF16)
        lane = _iota((blk, LANES), 1)
        c_ref[...] = jnp.zeros_like(c_ref)

        def step(n, carry):
            j = i - n
            rows = pl.ds(pl.multiple_of(j * blk, blk), blk)
            out = []
            for hh in range(hp):
                cols = pl.ds(hh * dh, dh)
                acc, run = carry[2 * hh], carry[2 * hh + 1]
                vis, ls, lk = _sb_tile(q_ref[:, cols], k_ref[rows, cols], i, j, blk, key_lo, scale)
                wgt = jnp.where(vis, jnp.exp(ls + _suffix_sums(lk, later) + run), 0.0)
                c_ref[hh] = jnp.where(lane == j, run, c_ref[hh])
                out += [acc + _dot(_bf(wgt), v_ref[rows, cols]), run + jnp.sum(lk, axis=1, keepdims=True)]
            return tuple(out)

        res = lax.fori_loop(0, i + 1, step, (jnp.zeros((blk, dh), F32), jnp.zeros((blk, 1), F32)) * hp)
        for hh in range(hp):
            o_ref[:, pl.ds(hh * dh, dh)] = _bf(res[2 * hh])

    full = pl.BlockSpec((t, hp * dh), lambda h, i: (0, h))
    return pl.pallas_call(
        body, grid=(heads // hp, nq),
        in_specs=[pl.BlockSpec((blk, hp * dh), lambda h, i: (i, h)), full, full],
        out_specs=[pl.BlockSpec((blk, hp * dh), lambda h, i: (i, h)),
                   pl.BlockSpec((hp, blk, LANES), lambda h, i: (h, i, 0))],
        out_shape=[jax.ShapeDtypeStruct((t, heads * dh), BF16), jax.ShapeDtypeStruct((heads, t, LANES), F32)],
        name=name, compiler_params=_params("parallel", "parallel"))(qs, ks, vs)


def _sb_bwd(qs, ks, vs, do, carry, *, heads, dh, key_lo, name):
    t = qs.shape[0]
    blk = _pick(t, (3 * SB_BLOCK, 2 * SB_BLOCK, SB_BLOCK))
    nq = t // blk
    scale = dh ** -0.5
    hp = SB_HEADS_PER_STEP

    def body(q_ref, k_ref, v_ref, do_ref, c_ref, dq_ref, dk_ref, dv_ref):
        i = pl.program_id(1)

        @pl.when(i == 0)
        def _():
            dk_ref[...] = jnp.zeros_like(dk_ref)
            dv_ref[...] = jnp.zeros_like(dv_ref)

        r0 = _iota((blk, blk), 0)
        r1 = _iota((blk, blk), 1)
        later = jnp.where(r0 > r1, 1.0, 0.0).astype(BF16)
        earlier = jnp.where(r0 < r1, 1.0, 0.0).astype(BF16)

        def step(j, carry):
            rows = pl.ds(pl.multiple_of(j * blk, blk), blk)
            out = []
            for hh in range(hp):
                cols = pl.ds(hh * dh, dh)
                dq, pre = carry[2 * hh], carry[2 * hh + 1]
                q, dob, kb, vb = q_ref[:, cols], do_ref[:, cols], k_ref[rows, cols], v_ref[rows, cols]
                vis, ls, lk = _sb_tile(q, kb, i, j, blk, key_lo, scale)
                wgt = jnp.where(vis, jnp.exp(ls + _suffix_sums(lk, later) + _lane_pick(c_ref[hh], j)), 0.0)
                e = wgt * _dot_nt(dob, vb)
                before = jnp.where(vis, _prefix_sums(e, earlier) + pre, 0.0)
                sig = jnp.exp(ls)
                dz = _bf((e * (1.0 - sig) - before * sig) * scale)
                dk_ref[rows, cols] += _dot_tn(dz, q)
                dv_ref[rows, cols] += _dot_tn(_bf(wgt), dob)
                out += [dq + _dot(dz, kb), pre + jnp.sum(e, axis=1, keepdims=True)]
            return tuple(out)

        res = lax.fori_loop(0, i + 1, step, (jnp.zeros((blk, dh), F32), jnp.zeros((blk, 1), F32)) * hp)
        for hh in range(hp):
            dq_ref[:, pl.ds(hh * dh, dh)] = res[2 * hh]

    full = pl.BlockSpec((t, hp * dh), lambda h, i: (0, h))
    qblk = pl.BlockSpec((blk, hp * dh), lambda h, i: (i, h))
    return pl.pallas_call(
        body, grid=(heads // hp, nq),
        in_specs=[qblk, full, full, qblk, pl.BlockSpec((hp, blk, LANES), lambda h, i: (h, i, 0))],
        out_specs=[qblk, full, full], out_shape=[jax.ShapeDtypeStruct((t, heads * dh), F32)] * 3, name=name,
        compiler_params=_params("parallel", "arbitrary"))(qs, ks, vs, do, carry)


def _sb_pre_bwd(proj, gq, gk, dq, dk, dv, *, heads, dh, col_q, col_k, name):
    t = proj.shape[0]
    tm = _pick(t, (384, 256, 128))

    def body(q_ref, k_ref, gq_ref, gk_ref, dq_ref, dk_ref, dv_ref, oq_ref, ok_ref, ov_ref, dgq_ref, dgk_ref):
        @pl.when((pl.program_id(0) == 0) & (pl.program_id(1) == 0))
        def _():
            dgq_ref[...] = jnp.zeros_like(dgq_ref)
            dgk_ref[...] = jnp.zeros_like(dgk_ref)

        dq_, gq_r = _rms_bwd(q_ref[...], gq_ref[...], dq_ref[...])
        dk_, gk_r = _rms_bwd(k_ref[...], gk_ref[...], dk_ref[...])
        oq_ref[...] = _bf(dq_)
        ok_ref[...] = _bf(dk_)
        ov_ref[...] = _bf(dv_ref[...])
        dgq_ref[...] += jnp.sum(gq_r, axis=0, keepdims=True)
        dgk_ref[...] += jnp.sum(gk_r, axis=0, keepdims=True)

    def cur(col):
        return pl.BlockSpec((tm, dh), lambda i, h: (i, col // dh + h))

    gspec = pl.BlockSpec((1, dh), lambda i, h: (0, 0))
    ospec = pl.BlockSpec((tm, dh), lambda i, h: (i, h))
    return pl.pallas_call(
        body, grid=(t // tm, heads), in_specs=[cur(col_q), cur(col_k), gspec, gspec, ospec, ospec, ospec],
        out_specs=[ospec, ospec, ospec, gspec, gspec],
        out_shape=[jax.ShapeDtypeStruct((t, heads * dh), BF16)] * 3 + [jax.ShapeDtypeStruct((1, dh), F32)] * 2,
        name=name, compiler_params=_params("arbitrary", "arbitrary"))(proj, proj, gq, gk, dq, dk, dv)


def _exchange(srcs, *, scatter, name):
    n = len(srcs)
    peers = N_DEV - 1

    def body(*refs):
        ins, outs = refs[:n], refs[n:2 * n]
        send_sems, recv_sems, local_sems = refs[2 * n:]
        x, y, c = lax.axis_index("x"), lax.axis_index("y"), lax.axis_index("c")
        me = 4 * x + 2 * y + c
        copies = []
        for a in range(n):
            own = ins[a].at[me] if scatter else ins[a]
            local = pltpu.make_async_copy(own, outs[a].at[me], local_sems.at[a])
            local.start()
            copies.append(local)
            for k in range(1, N_DEV):
                px = (x + (k >> 2 & 1)) % 2
                py = (y + (k >> 1 & 1)) % 2
                pc = (c + (k & 1)) % 2
                src = ins[a].at[4 * px + 2 * py + pc] if scatter else ins[a]
                cp = pltpu.make_async_remote_copy(
                    src_ref=src, dst_ref=outs[a].at[me], send_sem=send_sems.at[a * peers + k - 1],
                    recv_sem=recv_sems.at[a * peers + k - 1], device_id=(px, py, pc), device_id_type=MESH)
                cp.start()
                copies.append(cp)
        for cp in copies:
            cp.wait()

    any_spec = pl.BlockSpec(memory_space=pl.ANY)
    out_shape = [jax.ShapeDtypeStruct(s.shape if scatter else (N_DEV,) + s.shape, s.dtype) for s in srcs]
    return pl.pallas_call(
        body, in_specs=[any_spec] * n, out_specs=[any_spec] * n, out_shape=out_shape,
        scratch_shapes=[pltpu.SemaphoreType.DMA((n * peers,)), pltpu.SemaphoreType.DMA((n * peers,)),
                        pltpu.SemaphoreType.DMA((n,))],
        name=name, compiler_params=pltpu.CompilerParams(has_side_effects=True))(*srcs)


def _adam_math(g, w, m, v):
    m2 = ADAM_B1 * m + (1.0 - ADAM_B1) * g
    v2 = ADAM_B2 * v + (1.0 - ADAM_B2) * (g * g)
    m_hat = m2 / (1.0 - ADAM_B1 ** ADAM_STEP)
    v_hat = v2 / (1.0 - ADAM_B2 ** ADAM_STEP)
    return -ADAM_LR * (m_hat / (jnp.sqrt(v_hat) + ADAM_EPS) + ADAM_WD * w), m2, v2


def _adamw_slabs(slabs, w, m, v, *, name):
    r, c = w.shape
    tr = _pick(r, (128, 64, 32, 16, 8)) if r % 8 == 0 else r

    def body(s_ref, w_ref, m_ref, v_ref, g_ref, d_ref, mo_ref, vo_ref):
        g = s_ref[0].astype(F32)
        for p in range(1, N_DEV):
            g = g + s_ref[p].astype(F32)
        g_ref[...] = g
        d_ref[...], mo_ref[...], vo_ref[...] = _adam_math(g, w_ref[...], m_ref[...], v_ref[...])

    spec = pl.BlockSpec((tr, c), lambda i: (i, 0))
    return pl.pallas_call(
        body, grid=(r // tr,), in_specs=[pl.BlockSpec((N_DEV, tr, c), lambda i: (0, i, 0)), spec, spec, spec],
        out_specs=[spec] * 4, out_shape=[jax.ShapeDtypeStruct((r, c), F32)] * 4, name=name,
        compiler_params=_params("parallel"))(slabs, w, m, v)


def _adamw_small(g, w, m, v, *, name):
    def body(g_ref, w_ref, m_ref, v_ref, d_ref, mo_ref, vo_ref):
        d_ref[...], mo_ref[...], vo_ref[...] = _adam_math(g_ref[...], w_ref[...], m_ref[...], v_ref[...])

    return pl.pallas_call(body, out_shape=[jax.ShapeDtypeStruct(w.shape, F32)] * 3, name=name)(g, w, m, v)


def _sum_slabs(slabs, *, name):
    def body(s_ref, o_ref):
        acc = s_ref[0]
        for p in range(1, N_DEV):
            acc = acc + s_ref[p]
        o_ref[...] = acc

    return pl.pallas_call(body, out_shape=jax.ShapeDtypeStruct(slabs.shape[1:], F32), name=name)(slabs)


def _gather_cols(g):
    return jnp.transpose(g, (1, 0, 2)).reshape(g.shape[1], -1)


def _col_slabs(a):
    return jnp.transpose(a.reshape(a.shape[0], N_DEV, -1), (1, 0, 2))


def _pad_lanes(a):
    return jnp.pad(a, ((0, 0), (0, LANES - a.shape[1])))


def _local_step(x, target, meta, g_mix, w_main, w_ab, cq, ck, cv, a_log, dt_bias, g_dn, g_sbq, g_sbk,
                w_bd, w_bs, w_out, g_ffn, w_fg, w_fu, w_fo):
    seq, d = x.shape
    n_meta = meta.shape[0]
    heads = a_log.shape[1]
    qk = cq.shape[1]
    dvt = cv.shape[1]
    dk, dv = qk // heads, dvt // heads
    dh = g_sbq.shape[1]
    sbw = w_bs.shape[0]
    sb_heads = sbw // dh
    pad_l = (-n_meta) % CHUNK
    row_x = pad_l + n_meta
    rows = row_x + seq
    t = -(-rows // GDN_ROWS) * GDN_ROWS
    col_q, col_k, col_v, col_z = 0, qk, 2 * qk, 2 * qk + dvt
    col_sq = 2 * qk + 2 * dvt
    col_sk, col_sv, col_gd, col_gs = col_sq + sbw, col_sq + 2 * sbw, col_sq + 3 * sbw, col_sq + 3 * sbw + d

    def rows_pad(a):
        return jnp.concatenate([jnp.zeros((row_x, d), F32), a, jnp.zeros((t - rows, d), F32)], axis=0)

    h0 = jnp.concatenate([jnp.zeros((pad_l, d), F32), meta, x, jnp.zeros((t - rows, d), F32)], axis=0)
    tgt = rows_pad(target)
    a_log_p, dt_p = _pad_lanes(a_log), _pad_lanes(dt_bias)

    proj, n1 = _mm_norm(h0, g_mix, w_main, name="proj")
    pab = _mm_nn(n1, w_ab, out_dtype=F32, name="proj_ab")
    gk = dict(heads=heads, dk=dk, dv=dv, col_q=col_q, col_k=col_k, col_v=col_v, row_lo=pad_l, row_hi=rows)
    qn, kn, vv, g, beta = _gdn_pre(proj, pab, cq, ck, cv, a_log_p, dt_p, name="gdn_pre", **gk)
    u, w, pm, qd, kd, egl, tinv = _gdn_prep(qn, kn, vv, g, beta, name="gdn_prep")
    o_raw, o_dn, states = _gdn_scan(u, w, pm, qd, kd, egl, proj, g_dn, col_z=col_z, name="gdn_scan")
    qs, ks, vs = _sb_pre(proj, g_sbq, g_sbk, heads=sb_heads, dh=dh, col_q=col_sq, col_k=col_sk, col_v=col_sv,
                         name="sb_pre")
    o_sb, carry = _sb_fwd(qs, ks, vs, heads=sb_heads, dh=dh, key_lo=pad_l, name="sb_fwd")
    merged, br_dn, br_sb = _merge_fwd(o_dn, o_sb, w_bd, w_bs, proj, col_gd=col_gd, col_gs=col_gs, name="merge")
    h1 = _mm_res(h0, merged, w_out, name="mix_out")
    gate, up, act, n2 = _mm_norm_swiglu(h1, g_ffn, w_fg, w_fu, name="ffn_in")
    dy, dyb, lsum = _mm_res_loss(h1, act, w_fo, tgt, row0=row_x, nrows=seq, name="ffn_out_loss")

    dgate, dup = _swiglu_bwd(dyb, w_fo, gate, up, name="ffn_out_bwd")
    d_w_fo = _mm_tn(act, dyb, name="dw_ffn_out")
    d_w_fg = _mm_tn(n2, dgate, name="dw_ffn_gate")
    d_w_fu = _mm_tn(n2, dup, name="dw_ffn_up")
    dh1, dh1b, d_g_ffn = _mm_nt_rmsbwd([(dgate, w_fg), (dup, w_fu)], None, h1, g_ffn, dy, name="ffn_in_bwd")

    dbd, dbs, dgd, dgs = _merge_bwd(dh1b, w_out, proj, br_dn, br_sb, col_gd=col_gd, col_gs=col_gs, name="mix_out_bwd")
    d_w_out = _mm_tn(merged, dh1b, name="dw_out")
    d_w_bd = _mm_tn(o_dn, dbd, name="dw_branch_dn")
    d_w_bs = _mm_tn(o_sb, dbs, name="dw_branch_sb")
    do_dn = _mm_nt(dbd, w_bd, out_dtype=F32, name="branch_dn_bwd")
    do_sb = _mm_nt(dbs, w_bs, out_dtype=BF16, name="branch_sb_bwd")

    do_raw, dz, d_g_dn = _gdn_post_bwd(o_raw, proj, g_dn, do_dn, col_z=col_z, name="gdn_post_bwd")
    du, dw, dp, dqd, dkd, dgl = _gdn_bwd_scan(u, w, pm, qd, kd, egl, states, do_raw, name="gdn_bwd_scan")
    dqn, dkn, dvv, dg, dbeta = _gdn_bwd_prep(qn, kn, vv, g, beta, tinv, u, w, du, dw, dp, dqd, dkd, dgl,
                                            name="gdn_bwd_prep")
    dcq, dck, dcv, dpab, d_a_log, d_dt = _gdn_pre_bwd_a(proj, pab, cq, ck, cv, a_log_p, dt_p, dqn, dkn, dvv, dg, dbeta,
                                                        name="gdn_pre_bwd", **gk)
    dpq, d_cq = _conv_bwd(proj, dcq, cq, heads=heads, width=dk, col=col_q, name="conv_q_bwd")
    dpk, d_ck = _conv_bwd(proj, dck, ck, heads=heads, width=dk, col=col_k, name="conv_k_bwd")
    dpv, d_cv = _conv_bwd(proj, dcv, cv, heads=heads, width=dv, col=col_v, name="conv_v_bwd")

    dqs, dks, dvs = _sb_bwd(qs, ks, vs, do_sb, carry, heads=sb_heads, dh=dh, key_lo=pad_l, name="sb_bwd")
    dsq, dsk, dsv, d_g_sbq, d_g_sbk = _sb_pre_bwd(proj, g_sbq, g_sbk, dqs, dks, dvs, heads=sb_heads, dh=dh,
                                                   col_q=col_sq, col_k=col_sk, name="sb_pre_bwd")

    dproj = jnp.concatenate([dpq, dpk, dpv, dz, dsq, dsk, dsv, dgd, dgs], axis=1)
    dpab_b = _bf(dpab)
    d_w_main = _mm_tn(n1, dproj, name="dw_in_main")
    d_w_ab = _mm_tn(n1, dpab_b, name="dw_in_ab")
    dh0, _, d_g_mix = _mm_nt_rmsbwd([(dproj, w_main)], (dpab_b, w_ab), h0, g_mix, dh1, name="proj_bwd")

    return dict(lsum=lsum, grad_x=dh0[row_x:rows], d_meta=dh0[pad_l:row_x], d_g_mix=d_g_mix, d_w_main=d_w_main,
                d_w_ab=d_w_ab, d_cq=d_cq, d_ck=d_ck, d_cv=d_cv, d_a_log=d_a_log[:, :heads], d_dt=d_dt[:, :heads],
                d_g_dn=d_g_dn, d_g_sbq=d_g_sbq, d_g_sbk=d_g_sbk, d_w_bd=d_w_bd, d_w_bs=d_w_bs, d_w_out=d_w_out,
                d_g_ffn=d_g_ffn, d_w_fg=d_w_fg, d_w_fu=d_w_fu, d_w_fo=d_w_fo)


def _pack(parts):
    flat = []
    for a in parts:
        a = a.reshape(-1)
        flat.append(jnp.pad(a, (0, (-a.shape[0]) % LANES)))
    v = jnp.concatenate(flat)
    v = jnp.pad(v, (0, (-v.shape[0]) % (8 * LANES)))
    return v.reshape(-1, LANES)


def _unpack(packed, shapes):
    flat = packed.reshape(-1)
    out, pos = [], 0
    for s in shapes:
        n = math.prod(s)
        out.append(flat[pos:pos + n].reshape(s))
        pos += n + (-n) % LANES
    return out


def kernel(x, meta_tokens, norm_mix_gain, w_in, conv_q, conv_k, conv_v, dn_a_log, dn_dt_bias, dn_out_norm_gain, sb_q_norm_gain, sb_k_norm_gain, w_branch_dn, w_branch_sb, w_out, norm_ffn_gain, w_ffn_in, w_ffn_out, loss_target, m_meta_tokens, m_norm_mix_gain, m_w_in, m_conv_q, m_conv_k, m_conv_v, m_dn_a_log, m_dn_dt_bias, m_dn_out_norm_gain, m_sb_q_norm_gain, m_sb_k_norm_gain, m_w_branch_dn, m_w_branch_sb, m_w_out, m_norm_ffn_gain, m_w_ffn_in, m_w_ffn_out, v_meta_tokens, v_norm_mix_gain, v_w_in, v_conv_q, v_conv_k, v_conv_v, v_dn_a_log, v_dn_dt_bias, v_dn_out_norm_gain, v_sb_q_norm_gain, v_sb_k_norm_gain, v_w_branch_dn, v_w_branch_sb, v_w_out, v_norm_ffn_gain, v_w_ffn_in, v_w_ffn_out):
    me = 4 * lax.axis_index("x") + 2 * lax.axis_index("y") + lax.axis_index("c")
    heads = dn_a_log.shape[1]
    d = x.shape[2]
    qk = conv_q.shape[2] * N_DEV
    dvt = conv_v.shape[2] * N_DEV
    d_ff = w_ffn_out.shape[1] * N_DEV
    n_meta = meta_tokens.shape[0]
    col_ab = 2 * qk + 2 * dvt

    small_shapes = [meta_tokens.shape, conv_q.shape[1:], conv_k.shape[1:], conv_v.shape[1:]]
    small = _pack([meta_tokens, conv_q[0], conv_k[0], conv_v[0]])
    g_in, g_fi, g_bd, g_bs, g_out, g_fo, g_small = _exchange(
        [_bf(w_in[0]), _bf(w_ffn_in[0]), _bf(w_branch_dn[0]), _bf(w_branch_sb[0]), _bf(w_out[0]), _bf(w_ffn_out[0]), small],
        scatter=False, name="gather_weights")
    w_full = _gather_cols(g_in)
    w_main = jnp.concatenate([w_full[:, :col_ab], w_full[:, col_ab + 2 * heads:]], axis=1)
    w_ab = _pad_lanes(w_full[:, col_ab:col_ab + 2 * heads])
    w_fi = _gather_cols(g_fi)
    parts = [_unpack(g_small[p], small_shapes) for p in range(N_DEV)]
    meta_f, cq_f, ck_f, cv_f = (jnp.concatenate([parts[p][a] for p in range(N_DEV)], axis=1) for a in range(4))

    r = _local_step(x[0], loss_target[0], meta_f, norm_mix_gain, w_main, w_ab, cq_f, ck_f, cv_f, dn_a_log, dn_dt_bias,
                    dn_out_norm_gain, sb_q_norm_gain, sb_k_norm_gain, g_bd.reshape(-1, d), g_bs.reshape(-1, d),
                    g_out.reshape(-1, d), norm_ffn_gain, w_fi[:, :d_ff], w_fi[:, d_ff:], g_fo.reshape(-1, d))

    d_w_in = jnp.concatenate([r["d_w_main"][:, :col_ab], r["d_w_ab"][:, :2 * heads], r["d_w_main"][:, col_ab:]], axis=1)
    d_w_fi = jnp.concatenate([r["d_w_fg"], r["d_w_fu"]], axis=1)
    loss_part = (0.5 / d) * jnp.sum(r["lsum"], axis=1, keepdims=True)
    small_g = [r["d_meta"], r["d_g_mix"], r["d_cq"], r["d_ck"], r["d_cv"], r["d_a_log"], r["d_dt"], r["d_g_dn"],
               r["d_g_sbq"], r["d_g_sbk"], r["d_g_ffn"], loss_part]
    s_in, s_fi, s_bd, s_bs, s_out, s_fo = _exchange(
        [_bf(_col_slabs(d_w_in)), _bf(_col_slabs(d_w_fi)), _bf(r["d_w_bd"]).reshape(N_DEV, -1, d),
         _bf(r["d_w_bs"]).reshape(N_DEV, -1, d), _bf(r["d_w_out"]).reshape(N_DEV, -1, d),
         _bf(r["d_w_fo"]).reshape(N_DEV, -1, d)], scatter=True, name="scatter_grads")
    (g_packs,) = _exchange([_pack(small_g)], scatter=False, name="gather_small_grads")
    (g_meta, g_mix, g_cq, g_ck, g_cv, g_al, g_dt, g_gdn, g_sbq, g_sbk, g_ffn, loss) = _unpack(
        _sum_slabs(g_packs, name="sum_small_grads"), [a.shape for a in small_g])

    def mine(a, width):
        return lax.dynamic_slice_in_dim(a, me * width, width, axis=1)

    big = dict(w_in=(s_in, w_in, m_w_in, v_w_in), w_branch_dn=(s_bd, w_branch_dn, m_w_branch_dn, v_w_branch_dn),
               w_branch_sb=(s_bs, w_branch_sb, m_w_branch_sb, v_w_branch_sb), w_out=(s_out, w_out, m_w_out, v_w_out),
               w_ffn_in=(s_fi, w_ffn_in, m_w_ffn_in, v_w_ffn_in), w_ffn_out=(s_fo, w_ffn_out, m_w_ffn_out, v_w_ffn_out))
    tiny = dict(meta_tokens=(mine(g_meta, d // N_DEV), meta_tokens, m_meta_tokens, v_meta_tokens),
                norm_mix_gain=(g_mix, norm_mix_gain, m_norm_mix_gain, v_norm_mix_gain),
                conv_q=(mine(g_cq, qk // N_DEV), conv_q[0], m_conv_q[0], v_conv_q[0]),
                conv_k=(mine(g_ck, qk // N_DEV), conv_k[0], m_conv_k[0], v_conv_k[0]),
                conv_v=(mine(g_cv, dvt // N_DEV), conv_v[0], m_conv_v[0], v_conv_v[0]),
                dn_a_log=(g_al, dn_a_log, m_dn_a_log, v_dn_a_log), dn_dt_bias=(g_dt, dn_dt_bias, m_dn_dt_bias, v_dn_dt_bias),
                dn_out_norm_gain=(g_gdn, dn_out_norm_gain, m_dn_out_norm_gain, v_dn_out_norm_gain),
                sb_q_norm_gain=(g_sbq, sb_q_norm_gain, m_sb_q_norm_gain, v_sb_q_norm_gain),
                sb_k_norm_gain=(g_sbk, sb_k_norm_gain, m_sb_k_norm_gain, v_sb_k_norm_gain),
                norm_ffn_gain=(g_ffn, norm_ffn_gain, m_norm_ffn_gain, v_norm_ffn_gain))
    order = ["meta_tokens", "norm_mix_gain", "w_in", "conv_q", "conv_k", "conv_v", "dn_a_log", "dn_dt_bias",
             "dn_out_norm_gain", "sb_q_norm_gain", "sb_k_norm_gain", "w_branch_dn", "w_branch_sb", "w_out",
             "norm_ffn_gain", "w_ffn_in", "w_ffn_out"]
    grads, deltas, new_m, new_v = [], [], [], []
    for name in order:
        if name in big:
            slabs, w, m, v = big[name]
            g, dl, mo, vo = _adamw_slabs(slabs, w[0], m[0], v[0], name="adamw_" + name)
            like = w.shape
        else:
            g, w, m, v = tiny[name]
            like = dict(conv_q=conv_q, conv_k=conv_k, conv_v=conv_v).get(name, w).shape
            dl, mo, vo = _adamw_small(g, w, m, v, name="adamw_" + name)
        for lst, a in ((grads, g), (deltas, dl), (new_m, mo), (new_v, vo)):
            lst.append(a.reshape(like))
    return (loss.reshape(()), r["grad_x"][None], *grads, *deltas, *new_m, *new_v)
```

```python
import functools
import math

import jax
import jax.numpy as jnp
from jax import lax
from jax.experimental import pallas as pl
from jax.experimental.pallas import tpu as pltpu

F32 = jnp.float32
BF16 = jnp.bfloat16

N_DEV = 8
CHUNK = 64
CHUNK_SHIFT = 6
GDN_ROWS = 2 * CHUNK
SCAN_HEADS = 2
SB_BLOCK = 128
SB_HEADS_PER_STEP = 2
LANES = 128
RMS_EPS = 1e-6
L2_EPS = 1e-6
ADAM_LR = 0.001
ADAM_B1 = 0.9
ADAM_B2 = 0.999
ADAM_EPS = 1e-08
ADAM_WD = 0.01
ADAM_STEP = 10
V7X_VMEM_LIMIT_BYTES = 56 * 1024 * 1024
MM_TN_OUT_BLOCK_BYTES = 6 * 1024 * 1024

MESH = pl.DeviceIdType.MESH


def _params(*sem):
    return pltpu.CompilerParams(dimension_semantics=sem or None, vmem_limit_bytes=V7X_VMEM_LIMIT_BYTES)


def _pick(n, cands):
    for c in cands:
        if n % c == 0:
            return c
    raise ValueError(f"no block size among {cands} divides {n}")


def _bf(x):
    return x.astype(BF16)


def _dot(a, b):
    return jnp.dot(a, b, preferred_element_type=F32)


def _dot_nt(a, b):
    return lax.dot_general(a, b, (((1,), (1,)), ((), ())), preferred_element_type=F32)


def _dot_tn(a, b):
    return lax.dot_general(a, b, (((0,), (0,)), ((), ())), preferred_element_type=F32)


def _split2(x):
    hi = _bf(x)
    return hi, _bf(x - hi.astype(F32))


def _split3(x):
    hi = _bf(x)
    r = x - hi.astype(F32)
    mid = _bf(r)
    return hi, mid, _bf(r - mid.astype(F32))


def _dot_hp(a, b, dot=_dot):
    ah, al = _split2(a)
    bh, bl = _split2(b)
    return dot(ah, bh) + dot(ah, bl) + dot(al, bh)


def _dot_exact_r(x, m, dot=_dot):
    h, mi, lo = _split3(x)
    return dot(h, m) + dot(mi, m) + dot(lo, m)


def _dot_exact_l(m, x, dot=_dot):
    h, mi, lo = _split3(x)
    return dot(m, h) + dot(m, mi) + dot(m, lo)


def _sigmoid(x):
    return 1.0 / (1.0 + jnp.exp(-x))


def _silu(x):
    return x * _sigmoid(x)


def _silu_grad(x):
    s = _sigmoid(x)
    return s * (1.0 + x * (1.0 - s))


def _softplus(x):
    return jnp.maximum(x, 0.0) + jnp.log(1.0 + jnp.exp(-jnp.abs(x)))


def _rms_fwd(h, gain):
    r = lax.rsqrt(jnp.mean(h * h, axis=-1, keepdims=True) + RMS_EPS)
    return h * r * gain


def _rms_bwd(h, gain, dy):
    r = lax.rsqrt(jnp.mean(h * h, axis=-1, keepdims=True) + RMS_EPS)
    dyg = dy * gain
    dh = r * dyg - h * (r * r * r) * jnp.mean(dyg * h, axis=-1, keepdims=True)
    return dh, dy * h * r


def _iota(shape, dim):
    return lax.broadcasted_iota(jnp.int32, shape, dim)


def _lane_pick(x, idx):
    return jnp.sum(jnp.where(_iota(x.shape, 1) == idx, x, 0.0), axis=1, keepdims=True)


def _mm_nn(a, b, *, out_dtype, name):
    m, k = a.shape
    n = b.shape[1]
    tm, tn = _pick(m, (384, 256, 128)), _pick(n, (1024, 512, 256, 128))

    def body(a_ref, b_ref, o_ref):
        o_ref[...] = _dot(a_ref[...], b_ref[...]).astype(out_dtype)

    return pl.pallas_call(
        body, grid=(m // tm, n // tn),
        in_specs=[pl.BlockSpec((tm, k), lambda i, j: (i, 0)), pl.BlockSpec((k, tn), lambda i, j: (0, j))],
        out_specs=pl.BlockSpec((tm, tn), lambda i, j: (i, j)),
        out_shape=jax.ShapeDtypeStruct((m, n), out_dtype), name=name,
        compiler_params=_params("parallel", "parallel"))(a, b)


def _mm_nt(a, b, *, out_dtype, name):
    m, k = a.shape
    n = b.shape[0]
    tm, tn = _pick(m, (384, 256, 128)), _pick(n, (1024, 512, 256, 128))

    def body(a_ref, b_ref, o_ref):
        o_ref[...] = _dot_nt(a_ref[...], b_ref[...]).astype(out_dtype)

    return pl.pallas_call(
        body, grid=(m // tm, n // tn),
        in_specs=[pl.BlockSpec((tm, k), lambda i, j: (i, 0)), pl.BlockSpec((tn, k), lambda i, j: (j, 0))],
        out_specs=pl.BlockSpec((tm, tn), lambda i, j: (i, j)),
        out_shape=jax.ShapeDtypeStruct((m, n), out_dtype), name=name,
        compiler_params=_params("parallel", "parallel"))(a, b)


def _mm_tn(a, b, *, name):
    t, m = a.shape
    n = b.shape[1]
    tn = _pick(n, (2816, 2048, 1408, 1024, 512, 256, 128))
    tm = _pick(m, tuple(c for c in (1408, 1024, 512, 256, 128) if c * tn * 4 <= MM_TN_OUT_BLOCK_BYTES))
    tk = _pick(t, (1408, 1024, 512, 384, 256, 128))

    def body(a_ref, b_ref, o_ref):
        @pl.when(pl.program_id(2) == 0)
        def _():
            o_ref[...] = jnp.zeros_like(o_ref)

        o_ref[...] += _dot_tn(a_ref[...], b_ref[...])

    return pl.pallas_call(
        body, grid=(m // tm, n // tn, t // tk),
        in_specs=[pl.BlockSpec((tk, tm), lambda i, j, k: (k, i)), pl.BlockSpec((tk, tn), lambda i, j, k: (k, j))],
        out_specs=pl.BlockSpec((tm, tn), lambda i, j, k: (i, j)),
        out_shape=jax.ShapeDtypeStruct((m, n), F32), name=name,
        compiler_params=_params("parallel", "parallel", "arbitrary"))(a, b)


def _mm_norm(h, gain, w, *, name):
    m, k = h.shape
    n = w.shape[1]
    tm, tn = _pick(m, (384, 256, 128)), _pick(n, (1024, 512, 256, 128))

    def body(h_ref, g_ref, w_ref, o_ref, n_ref):
        @pl.when(pl.program_id(1) == 0)
        def _():
            n_ref[...] = _bf(_rms_fwd(h_ref[...], g_ref[...]))

        o_ref[...] = _dot(n_ref[...], w_ref[...])

    return pl.pallas_call(
        body, grid=(m // tm, n // tn),
        in_specs=[pl.BlockSpec((tm, k), lambda i, j: (i, 0)), pl.BlockSpec((1, k), lambda i, j: (0, 0)),
                  pl.BlockSpec((k, tn), lambda i, j: (0, j))],
        out_specs=[pl.BlockSpec((tm, tn), lambda i, j: (i, j)), pl.BlockSpec((tm, k), lambda i, j: (i, 0))],
        out_shape=[jax.ShapeDtypeStruct((m, n), F32), jax.ShapeDtypeStruct((m, k), BF16)], name=name,
        compiler_params=_params("parallel", "arbitrary"))(h, gain, w)


def _mm_norm_swiglu(h, gain, wg, wu, *, name):
    m, k = h.shape
    n = wg.shape[1]
    tm, tn = _pick(m, (384, 256, 128)), _pick(n, (1408, 1024, 512, 256, 128))

    def body(h_ref, g_ref, wg_ref, wu_ref, gate_ref, up_ref, act_ref, n_ref):
        @pl.when(pl.program_id(1) == 0)
        def _():
            n_ref[...] = _bf(_rms_fwd(h_ref[...], g_ref[...]))

        gate = _dot(n_ref[...], wg_ref[...])
        up = _dot(n_ref[...], wu_ref[...])
        gate_ref[...] = gate
        up_ref[...] = up
        act_ref[...] = _bf(_silu(gate) * up)

    wspec = pl.BlockSpec((k, tn), lambda i, j: (0, j))
    ospec = pl.BlockSpec((tm, tn), lambda i, j: (i, j))
    return pl.pallas_call(
        body, grid=(m // tm, n // tn),
        in_specs=[pl.BlockSpec((tm, k), lambda i, j: (i, 0)), pl.BlockSpec((1, k), lambda i, j: (0, 0)), wspec, wspec],
        out_specs=[ospec, ospec, ospec, pl.BlockSpec((tm, k), lambda i, j: (i, 0))],
        out_shape=[jax.ShapeDtypeStruct((m, n), F32), jax.ShapeDtypeStruct((m, n), F32),
                   jax.ShapeDtypeStruct((m, n), BF16), jax.ShapeDtypeStruct((m, k), BF16)], name=name,
        compiler_params=_params("parallel", "arbitrary"))(h, gain, wg, wu)


def _mm_res(res, a, b, *, name):
    m, k = a.shape
    n = b.shape[1]
    tm, tn = _pick(m, (384, 256, 128)), _pick(n, (1024, 512, 256, 128))

    def body(r_ref, a_ref, b_ref, o_ref):
        o_ref[...] = r_ref[...] + _dot(a_ref[...], b_ref[...])

    return pl.pallas_call(
        body, grid=(m // tm, n // tn),
        in_specs=[pl.BlockSpec((tm, tn), lambda i, j: (i, j)), pl.BlockSpec((tm, k), lambda i, j: (i, 0)),
                  pl.BlockSpec((k, tn), lambda i, j: (0, j))],
        out_specs=pl.BlockSpec((tm, tn), lambda i, j: (i, j)),
        out_shape=jax.ShapeDtypeStruct((m, n), F32), name=name,
        compiler_params=_params("parallel", "parallel"))(res, a, b)


def _mm_res_loss(res, a, b, target, *, row0, nrows, name):
    m, k = a.shape
    n = b.shape[1]
    tm = _pick(m, (384, 256, 128))

    def body(r_ref, a_ref, b_ref, t_ref, dy_ref, dyb_ref, ls_ref):
        i = pl.program_id(0)

        @pl.when(i == 0)
        def _():
            ls_ref[...] = jnp.zeros_like(ls_ref)

        y = r_ref[...] + _dot(a_ref[...], b_ref[...])
        row = i * tm + _iota((tm, n), 0)
        e = jnp.where((row >= row0) & (row < row0 + nrows), y - t_ref[...], 0.0)
        dy = e / n
        dy_ref[...] = dy
        dyb_ref[...] = _bf(dy)
        ls_ref[...] += jnp.sum(e * e, axis=0, keepdims=True)

    rspec = pl.BlockSpec((tm, n), lambda i: (i, 0))
    return pl.pallas_call(
        body, grid=(m // tm,),
        in_specs=[rspec, pl.BlockSpec((tm, k), lambda i: (i, 0)), pl.BlockSpec((k, n), lambda i: (0, 0)), rspec],
        out_specs=[rspec, rspec, pl.BlockSpec((1, n), lambda i: (0, 0))],
        out_shape=[jax.ShapeDtypeStruct((m, n), F32), jax.ShapeDtypeStruct((m, n), BF16),
                   jax.ShapeDtypeStruct((1, n), F32)], name=name,
        compiler_params=_params("arbitrary"))(res, a, b, target)


def _merge_fwd(o_dn, o_sb, wbd, wbs, proj, *, col_gd, col_gs, name):
    m, kd = o_dn.shape
    ks = o_sb.shape[1]
    n = wbd.shape[1]
    tm = _pick(m, (384, 256, 128))
    tn = _pick(math.gcd(n, math.gcd(col_gd, col_gs)), (512, 256, 128))

    def body(od_ref, os_ref, wd_ref, ws_ref, gd_ref, gs_ref, mg_ref, bd_ref, bs_ref):
        bd = _dot(od_ref[...], wd_ref[...])
        bs = _dot(os_ref[...], ws_ref[...])
        bd_ref[...] = bd
        bs_ref[...] = bs
        mg_ref[...] = _bf(_sigmoid(gd_ref[...]) * bd + _sigmoid(gs_ref[...]) * bs)

    ospec = pl.BlockSpec((tm, tn), lambda i, j: (i, j))
    return pl.pallas_call(
        body, grid=(m // tm, n // tn),
        in_specs=[pl.BlockSpec((tm, kd), lambda i, j: (i, 0)), pl.BlockSpec((tm, ks), lambda i, j: (i, 0)),
                  pl.BlockSpec((kd, tn), lambda i, j: (0, j)), pl.BlockSpec((ks, tn), lambda i, j: (0, j)),
                  pl.BlockSpec((tm, tn), lambda i, j: (i, col_gd // tn + j)),
                  pl.BlockSpec((tm, tn), lambda i, j: (i, col_gs // tn + j))],
        out_specs=[ospec, ospec, ospec],
        out_shape=[jax.ShapeDtypeStruct((m, n), BF16), jax.ShapeDtypeStruct((m, n), F32),
                   jax.ShapeDtypeStruct((m, n), F32)], name=name,
        compiler_params=_params("parallel", "parallel"))(o_dn, o_sb, wbd, wbs, proj, proj)


def _merge_bwd(dh, w_out, proj, br_dn, br_sb, *, col_gd, col_gs, name):
    m, k = dh.shape
    n = w_out.shape[0]
    tm = _pick(m, (384, 256, 128))
    tn = _pick(math.gcd(n, math.gcd(col_gd, col_gs)), (512, 256, 128))

    def body(dh_ref, w_ref, gd_ref, gs_ref, bd_ref, bs_ref, dbd_ref, dbs_ref, dgd_ref, dgs_ref):
        dm = _dot_nt(dh_ref[...], w_ref[...])
        sd = _sigmoid(gd_ref[...])
        ss = _sigmoid(gs_ref[...])
        dbd_ref[...] = _bf(dm * sd)
        dbs_ref[...] = _bf(dm * ss)
        dgd_ref[...] = _bf(dm * bd_ref[...] * sd * (1.0 - sd))
        dgs_ref[...] = _bf(dm * bs_ref[...] * ss * (1.0 - ss))

    ospec = pl.BlockSpec((tm, tn), lambda i, j: (i, j))
    return pl.pallas_call(
        body, grid=(m // tm, n // tn),
        in_specs=[pl.BlockSpec((tm, k), lambda i, j: (i, 0)), pl.BlockSpec((tn, k), lambda i, j: (j, 0)),
                  pl.BlockSpec((tm, tn), lambda i, j: (i, col_gd // tn + j)),
                  pl.BlockSpec((tm, tn), lambda i, j: (i, col_gs // tn + j)), ospec, ospec],
        out_specs=[ospec] * 4,
        out_shape=[jax.ShapeDtypeStruct((m, n), BF16)] * 4, name=name,
        compiler_params=_params("parallel", "parallel"))(dh, w_out, proj, proj, br_dn, br_sb)


def _swiglu_bwd(dy, wfo, gate, up, *, name):
    m, k = dy.shape
    n = wfo.shape[0]
    tm, tn = _pick(m, (384, 256, 128)), _pick(n, (1408, 1024, 512, 256, 128))

    def body(dy_ref, w_ref, g_ref, u_ref, dg_ref, du_ref):
        da = _dot_nt(dy_ref[...], w_ref[...])
        g = g_ref[...]
        dg_ref[...] = _bf(da * u_ref[...] * _silu_grad(g))
        du_ref[...] = _bf(da * _silu(g))

    ospec = pl.BlockSpec((tm, tn), lambda i, j: (i, j))
    return pl.pallas_call(
        body, grid=(m // tm, n // tn),
        in_specs=[pl.BlockSpec((tm, k), lambda i, j: (i, 0)), pl.BlockSpec((tn, k), lambda i, j: (j, 0)), ospec, ospec],
        out_specs=[ospec, ospec], out_shape=[jax.ShapeDtypeStruct((m, n), BF16)] * 2, name=name,
        compiler_params=_params("parallel", "parallel"))(dy, wfo, gate, up)


def _mm_nt_rmsbwd(pairs, extra, h, gain, dres, *, name):
    m, k = pairs[0][0].shape
    n = h.shape[1]
    tm = _pick(m, (384, 256, 128))
    tk = _pick(k, (1408, 1024, 512, 256, 128))
    nk = k // tk
    np_ = len(pairs)

    def body(*refs):
        ab = refs[:2 * np_]
        pos = 2 * np_
        ex = refs[pos:pos + 2] if extra is not None else ()
        pos += len(ex)
        h_ref, g_ref, r_ref, dh_ref, dhb_ref, dg_ref, acc_ref = refs[pos:]
        i, kk = pl.program_id(0), pl.program_id(1)

        @pl.when((i == 0) & (kk == 0))
        def _():
            dg_ref[...] = jnp.zeros_like(dg_ref)

        part = _dot_nt(ab[0][...], ab[1][...])
        for p in range(1, np_):
            part += _dot_nt(ab[2 * p][...], ab[2 * p + 1][...])

        @pl.when(kk == 0)
        def _():
            first = part
            if ex:
                first = first + _dot_nt(ex[0][...], ex[1][...])
            acc_ref[...] = first

        @pl.when(kk > 0)
        def _():
            acc_ref[...] += part

        @pl.when(kk == nk - 1)
        def _():
            dh, dgr = _rms_bwd(h_ref[...], g_ref[...], acc_ref[...])
            dh = dh + r_ref[...]
            dh_ref[...] = dh
            dhb_ref[...] = _bf(dh)
            dg_ref[...] += jnp.sum(dgr, axis=0, keepdims=True)

    in_specs, args = [], []
    for a, b in pairs:
        in_specs += [pl.BlockSpec((tm, tk), lambda i, kk: (i, kk)), pl.BlockSpec((n, tk), lambda i, kk: (0, kk))]
        args += [a, b]
    if extra is not None:
        k2 = extra[0].shape[1]
        in_specs += [pl.BlockSpec((tm, k2), lambda i, kk: (i, 0)), pl.BlockSpec((n, k2), lambda i, kk: (0, 0))]
        args += list(extra)
    rspec = pl.BlockSpec((tm, n), lambda i, kk: (i, 0))
    in_specs += [rspec, pl.BlockSpec((1, n), lambda i, kk: (0, 0)), rspec]
    return pl.pallas_call(
        body, grid=(m // tm, nk), in_specs=in_specs,
        out_specs=[rspec, rspec, pl.BlockSpec((1, n), lambda i, kk: (0, 0))],
        out_shape=[jax.ShapeDtypeStruct((m, n), F32), jax.ShapeDtypeStruct((m, n), BF16),
                   jax.ShapeDtypeStruct((1, n), F32)],
        scratch_shapes=[pltpu.VMEM((tm, n), F32)], name=name,
        compiler_params=_params("arbitrary", "arbitrary"))(*args, h, gain, dres)


def _conv_taps(cur, prev8, w_ref, first):
    nk = w_ref.shape[0]
    out = cur * w_ref[nk - 1:nk, :]
    for s in range(1, nk):
        out += _shift_down(cur, prev8, s, first) * w_ref[nk - 1 - s:nk - s, :]
    return out


def _shift_up(cur, next8, s, last):
    rows = cur.shape[0]
    row = _iota(cur.shape, 0)
    next8 = jnp.where(last, 0.0, next8)
    sh = pltpu.roll(cur, rows - s, axis=0)
    nh = jnp.tile(pltpu.roll(next8, 8 - s, axis=0), (rows // 8, 1))
    return jnp.where(row >= rows - s, nh, sh)


def _shift_down(cur, prev8, s, first):
    rows = cur.shape[0]
    row = _iota(cur.shape, 0)
    prev8 = jnp.where(first, 0.0, prev8)
    sh = pltpu.roll(cur, s, axis=0)
    ph = jnp.tile(pltpu.roll(prev8, s, axis=0), (rows // 8, 1))
    return jnp.where(row < s, ph, sh)


def _gdn_pre(proj, pab, cq, ck, cv, a_log, dt_bias, *, heads, dk, dv, col_q, col_k, col_v, row_lo, row_hi, name):
    t = proj.shape[0]
    tm = _pick(t, (384, 256, 128))
    nb = t // tm

    def body(pq_ref, pqp_ref, pk_ref, pkp_ref, pv_ref, pvp_ref, ab_ref, cq_ref, ck_ref, cv_ref, al_ref, dt_ref,
             qn_ref, kn_ref, v_ref, g_ref, b_ref):
        h, i = pl.program_id(0), pl.program_id(1)
        first = i == 0
        row = i * tm + _iota((tm, 1), 0)
        valid = (row >= row_lo) & (row < row_hi)
        q1 = _silu(_conv_taps(pq_ref[...], pqp_ref[...], cq_ref, first))
        k1 = _silu(_conv_taps(pk_ref[...], pkp_ref[...], ck_ref, first))
        v1 = _silu(_conv_taps(pv_ref[...], pvp_ref[...], cv_ref, first))
        qn_ref[...] = jnp.where(valid, q1 * lax.rsqrt(jnp.sum(q1 * q1, axis=-1, keepdims=True) + L2_EPS), 0.0)
        kn_ref[...] = jnp.where(valid, k1 * lax.rsqrt(jnp.sum(k1 * k1, axis=-1, keepdims=True) + L2_EPS), 0.0)
        v_ref[...] = jnp.where(valid, v1, 0.0)
        ab = ab_ref[...]
        da = _lane_pick(ab, h)
        db = _lane_pick(ab, heads + h)
        a = _lane_pick(al_ref[...], h)
        dtb = _lane_pick(dt_ref[...], h)
        g_ref[...] = jnp.where(valid, -jnp.exp(a) * _softplus(da + dtb), 0.0)
        b_ref[...] = jnp.where(valid, _sigmoid(db), 0.0)

    def cur(width, col):
        return pl.BlockSpec((tm, width), lambda h, i: (i, col // width + h))

    def prev(width, col):
        return pl.BlockSpec((8, width), lambda h, i: (jnp.maximum(i * (tm // 8) - 1, 0), col // width + h))

    def out(width):
        return pl.BlockSpec((None, tm, width), lambda h, i: (h, i, 0))

    small = pl.BlockSpec((1, LANES), lambda h, i: (0, 0))
    return pl.pallas_call(
        body, grid=(heads, nb),
        in_specs=[cur(dk, col_q), prev(dk, col_q), cur(dk, col_k), prev(dk, col_k), cur(dv, col_v), prev(dv, col_v),
                  pl.BlockSpec((tm, LANES), lambda h, i: (i, 0)),
                  pl.BlockSpec((cq.shape[0], dk), lambda h, i: (0, h)), pl.BlockSpec((ck.shape[0], dk), lambda h, i: (0, h)),
                  pl.BlockSpec((cv.shape[0], dv), lambda h, i: (0, h)), small, small],
        out_specs=[out(dk), out(dk), out(dv), out(1), out(1)],
        out_shape=[jax.ShapeDtypeStruct((heads, t, dk), F32), jax.ShapeDtypeStruct((heads, t, dk), F32),
                   jax.ShapeDtypeStruct((heads, t, dv), F32), jax.ShapeDtypeStruct((heads, t, 1), F32),
                   jax.ShapeDtypeStruct((heads, t, 1), F32)], name=name,
        compiler_params=_params("parallel", "parallel"))(proj, proj, proj, proj, proj, proj, pab, cq, ck, cv, a_log, dt_bias)


def _chunk_masks(rows=GDN_ROWS, row0=0):
    ri = row0 + _iota((rows, GDN_ROWS), 0)
    ci = _iota((rows, GDN_ROWS), 1)
    same = jnp.right_shift(ri, CHUNK_SHIFT) == jnp.right_shift(ci, CHUNK_SHIFT)
    return same, same & (ri >= ci), same & (ri > ci), ri == ci


def _col_to_row(col, eye):
    return jnp.sum(jnp.where(eye, col, 0.0), axis=0, keepdims=True)


def _row_to_col(row, eye):
    return jnp.sum(jnp.where(eye, row, 0.0), axis=1, keepdims=True)


def _chunk_common(qn, kn, g, beta, dk_scale):
    same, incl, strict, eye = _chunk_masks()
    gb = jnp.broadcast_to(g, (GDN_ROWS, LANES))
    gam = jnp.max(_dot_exact_l(jnp.where(incl, 1.0, 0.0).astype(BF16), gb), axis=1, keepdims=True)
    gam_last = jnp.max(_dot_exact_l(jnp.where(same, 1.0, 0.0).astype(BF16), gb), axis=1, keepdims=True)
    diff = gam - _col_to_row(gam, eye)
    decay = jnp.where(incl, jnp.exp(jnp.where(incl, diff, 0.0)), 0.0)
    eg = jnp.exp(gam)
    ek = jnp.exp(gam_last - gam)
    kb = kn * beta
    qt = qn * dk_scale
    lmat = jnp.where(strict, _dot_nt(_bf(kb), _bf(kn)) * decay, 0.0)
    pmat = jnp.where(incl, _dot_nt(_bf(qt), _bf(kn)) * decay, 0.0)
    return dict(incl=incl, strict=strict, eye=eye, decay=decay, eg=eg, ek=ek, egl=jnp.exp(gam_last),
                kb=kb, qt=qt, lmat=lmat, pmat=pmat)


def _gdn_prep(qn, kn, v, g, beta, *, name, gather=None):
    heads, t, dk = qn.shape
    dv = v.shape[2]
    rows = _pick(t, (3 * GDN_ROWS, 2 * GDN_ROWS, GDN_ROWS))
    dk_scale = dk ** -0.5

    def body(q_ref, k_ref, v_ref, g_ref, b_ref, u_ref, w_ref, p_ref, qd_ref, kd_ref, egl_ref, t_ref):
        for b in range(rows // GDN_ROWS):
            r = pl.ds(b * GDN_ROWS, GDN_ROWS)
            kn_, beta_ = k_ref[r, :], b_ref[r, :]
            c = _chunk_common(q_ref[r, :], kn_, g_ref[r, :], beta_, dk_scale)
            eye_f = jnp.where(c["eye"], 1.0, 0.0)
            x = c["lmat"]
            tinv = eye_f - x
            for _ in range(CHUNK_SHIFT - 1):
                x = _dot_hp(x, x)
                tinv = tinv + _dot_hp(tinv, x)
            u_ref[r, :] = _dot_hp(tinv, v_ref[r, :] * beta_)
            w_ref[r, :] = _dot_hp(tinv, c["kb"] * c["eg"])
            p_ref[r, :] = c["pmat"]
            qd_ref[r, :] = c["qt"] * c["eg"]
            kd_ref[r, :] = kn_ * c["ek"]
            egl_ref[r, :] = c["egl"]
            t_ref[r, :] = tinv

    def blk(width):
        return pl.BlockSpec((None, rows, width), lambda h, i: (h, i, 0))

    def shp(width):
        return jax.ShapeDtypeStruct((heads, t, width), F32)

    call = dict(grid=(heads, t // rows), in_specs=[blk(dk), blk(dk), blk(dv), blk(1), blk(1)],
                out_specs=[blk(dv), blk(dk), blk(GDN_ROWS), blk(dk), blk(dk), blk(1), blk(GDN_ROWS)],
                out_shape=[shp(dv), shp(dk), shp(GDN_ROWS), shp(dk), shp(dk), shp(1), shp(GDN_ROWS)], name=name)
    if gather is None:
        return pl.pallas_call(body, compiler_params=_params("parallel", "parallel"), **call)(qn, kn, v, g, beta)
    res = _call_with_exchange(body, scratch_shapes=[], args=(qn, kn, v, g, beta), srcs=gather,
                              scatter=[False] * len(gather), **call)
    return [*res[:7], list(res[7:])]


def _gdn_scan(u, w, p, qd, kd, egl, proj, gain, *, col_z, name):
    heads, t, dv = u.shape
    dk = w.shape[2]
    nb = t // GDN_ROWS
    sub = GDN_ROWS // CHUNK
    hp = SCAN_HEADS

    def body(u_ref, w_ref, p_ref, qd_ref, kd_ref, egl_ref, z_ref, gn_ref, o_ref, og_ref, st_ref, s_ref):
        @pl.when(pl.program_id(1) == 0)
        def _():
            s_ref[...] = jnp.zeros_like(s_ref)

        for hh in range(hp):
            cols = pl.ds(hh * dv, dv)
            vn_parts = [jnp.zeros((CHUNK, dv), F32)] * sub
            for c in range(sub):
                r = pl.ds(c * CHUNK, CHUNK)
                s = s_ref[hh]
                st_ref[hh, c] = s
                sb = _bf(s)
                vn = u_ref[hh, r, :] - _dot(_bf(w_ref[hh, r, :]), sb)
                vn_parts[c] = vn
                vfull = _bf(jnp.concatenate(vn_parts, axis=0))
                o = _dot(_bf(qd_ref[hh, r, :]), sb) + _dot(_bf(p_ref[hh, r, :]), vfull)
                gl = egl_ref[hh, pl.ds(c * CHUNK, 1), :]
                s_ref[hh] = s * gl + _dot_tn(_bf(kd_ref[hh, r, :]), _bf(vn))
                o_ref[hh, r, :] = o
                og_ref[r, cols] = _bf(_rms_fwd(o, gn_ref[...]) * _silu(z_ref[r, cols]))

    def blk(width):
        return pl.BlockSpec((hp, GDN_ROWS, width), lambda h, i: (h, i, 0))

    return pl.pallas_call(
        body, grid=(heads // hp, nb),
        in_specs=[blk(dv), blk(dk), blk(GDN_ROWS), blk(dk), blk(dk), blk(1),
                  pl.BlockSpec((GDN_ROWS, hp * dv), lambda h, i: (i, col_z // (hp * dv) + h)),
                  pl.BlockSpec((1, dv), lambda h, i: (0, 0))],
        out_specs=[blk(dv), pl.BlockSpec((GDN_ROWS, hp * dv), lambda h, i: (i, h)),
                   pl.BlockSpec((hp, sub, dk, dv), lambda h, i: (h, i, 0, 0))],
        out_shape=[jax.ShapeDtypeStruct((heads, t, dv), F32), jax.ShapeDtypeStruct((t, heads * dv), BF16),
                   jax.ShapeDtypeStruct((heads, t // CHUNK, dk, dv), F32)],
        scratch_shapes=[pltpu.VMEM((hp, dk, dv), F32)], name=name,
        compiler_params=_params("parallel", "arbitrary"))(u, w, p, qd, kd, egl, proj, gain)


def _gdn_post_bwd(o, proj, gain, dout, *, col_z, name):
    heads, t, dv = o.shape
    tm = _pick(t, (384, 256, 128))

    def body(o_ref, z_ref, gn_ref, d_ref, do_ref, dz_ref, dg_ref):
        @pl.when((pl.program_id(0) == 0) & (pl.program_id(1) == 0))
        def _():
            dg_ref[...] = jnp.zeros_like(dg_ref)

        o_, z, d = o_ref[...], z_ref[...], d_ref[...]
        y = _rms_fwd(o_, gn_ref[...])
        dz_ref[...] = _bf(d * y * _silu_grad(z))
        do, dgr = _rms_bwd(o_, gn_ref[...], d * _silu(z))
        do_ref[...] = do
        dg_ref[...] += jnp.sum(dgr, axis=0, keepdims=True)

    return pl.pallas_call(
        body, grid=(t // tm, heads),
        in_specs=[pl.BlockSpec((None, tm, dv), lambda i, h: (h, i, 0)),
                  pl.BlockSpec((tm, dv), lambda i, h: (i, col_z // dv + h)),
                  pl.BlockSpec((1, dv), lambda i, h: (0, 0)), pl.BlockSpec((tm, dv), lambda i, h: (i, h))],
        out_specs=[pl.BlockSpec((None, tm, dv), lambda i, h: (h, i, 0)), pl.BlockSpec((tm, dv), lambda i, h: (i, h)),
                   pl.BlockSpec((1, dv), lambda i, h: (0, 0))],
        out_shape=[jax.ShapeDtypeStruct((heads, t, dv), F32), jax.ShapeDtypeStruct((t, heads * dv), BF16),
                   jax.ShapeDtypeStruct((1, dv), F32)], name=name,
        compiler_params=_params("arbitrary", "arbitrary"))(o, proj, gain, dout)


def _gdn_bwd_scan(u, w, p, qd, kd, egl, st, do, *, name):
    heads, t, dv = u.shape
    dk = w.shape[2]
    nb = t // GDN_ROWS
    sub = GDN_ROWS // CHUNK
    hp = SCAN_HEADS

    def body(u_ref, w_ref, p_ref, qd_ref, kd_ref, egl_ref, st_ref, do_ref,
             du_ref, dw_ref, dp_ref, dqd_ref, dkd_ref, dgl_ref, ds_ref):
        @pl.when(pl.program_id(1) == 0)
        def _():
            ds_ref[...] = jnp.zeros_like(ds_ref)

        for hh in range(hp):
            for c in reversed(range(sub)):
                r = pl.ds(c * CHUNK, CHUNK)
                s = st_ref[hh, c]
                sb = _bf(s)
                ds = ds_ref[hh]
                dsb = _bf(ds)
                dob = _bf(do_ref[hh, r, :])
                wb, pb = _bf(w_ref[hh, r, :]), _bf(p_ref[hh, r, :])
                qdb, kdb = _bf(qd_ref[hh, r, :]), _bf(kd_ref[hh, r, :])
                vn = u_ref[hh, r, :] - _dot(wb, sb)
                zeros = jnp.zeros((CHUNK, dv), BF16)
                vfull = jnp.concatenate([_bf(vn) if cc == c else zeros for cc in range(sub)], axis=0)
                dvn = _dot_tn(pb, dob)[c * CHUNK:(c + 1) * CHUNK, :] + _dot(kdb, dsb)
                dvb = _bf(dvn)
                gl = egl_ref[hh, pl.ds(c * CHUNK, 1), :]
                du_ref[hh, r, :] = dvn
                dw_ref[hh, r, :] = -_dot_nt(dvb, sb)
                dp_ref[hh, r, :] = jnp.where(_chunk_masks(CHUNK, c * CHUNK)[1], _dot_nt(dob, vfull), 0.0)
                dqd_ref[hh, r, :] = _dot_nt(dob, sb)
                dkd_ref[hh, r, :] = _dot_nt(_bf(vn), dsb)
                dgl = jnp.sum(jnp.sum(ds * s, axis=1, keepdims=True), axis=0, keepdims=True)
                dgl_ref[hh, r, :] = jnp.where(_iota((CHUNK, 1), 0) == CHUNK - 1, dgl, 0.0)
                ds_ref[hh] = ds * gl + _dot_tn(qdb, dob) - _dot_tn(wb, dvb)

    def blk(width):
        return pl.BlockSpec((hp, GDN_ROWS, width), lambda h, i: (h, nb - 1 - i, 0))

    def shp(width):
        return jax.ShapeDtypeStruct((heads, t, width), F32)

    return pl.pallas_call(
        body, grid=(heads // hp, nb),
        in_specs=[blk(dv), blk(dk), blk(GDN_ROWS), blk(dk), blk(dk), blk(1),
                  pl.BlockSpec((hp, sub, dk, dv), lambda h, i: (h, nb - 1 - i, 0, 0)), blk(dv)],
        out_specs=[blk(dv), blk(dk), blk(GDN_ROWS), blk(dk), blk(dk), blk(1)],
        out_shape=[shp(dv), shp(dk), shp(GDN_ROWS), shp(dk), shp(dk), shp(1)],
        scratch_shapes=[pltpu.VMEM((hp, dk, dv), F32)], name=name,
        compiler_params=_params("parallel", "arbitrary"))(u, w, p, qd, kd, egl, st, do)


def _gdn_bwd_prep(qn, kn, v, g, beta, tinv, u, w, du, dw, dp, dqd, dkd, dgl, *, name):
    heads, t, dk = qn.shape
    dv = v.shape[2]
    rows = _pick(t, (3 * GDN_ROWS, 2 * GDN_ROWS, GDN_ROWS))
    dk_scale = dk ** -0.5

    def rowsum(x):
        return jnp.sum(x, axis=1, keepdims=True)

    def body(q_ref, k_ref, v_ref, g_ref, b_ref, t_ref, u_ref, w_ref, du_ref, dw_ref, dp_ref, dqd_ref, dkd_ref, dgl_ref,
             dq_ref, dkk_ref, dvv_ref, dg_ref, db_ref):
        for b in range(rows // GDN_ROWS):
            r = pl.ds(b * GDN_ROWS, GDN_ROWS)
            kn_, beta_, v_ = k_ref[r, :], b_ref[r, :], v_ref[r, :]
            c = _chunk_common(q_ref[r, :], kn_, g_ref[r, :], beta_, dk_scale)
            eye, strict, decay = c["eye"], c["strict"], c["decay"]
            kb, qt, eg, ek = c["kb"], c["qt"], c["eg"], c["ek"]
            tinv_ = t_ref[r, :]
            dbv = _dot_hp(tinv_, du_ref[r, :], _dot_tn)
            dbw = _dot_hp(tinv_, dw_ref[r, :], _dot_tn)
            da = -(_dot_nt(_bf(dbv), _bf(u_ref[r, :])) + _dot_nt(_bf(dbw), _bf(w_ref[r, :])))
            dl = jnp.where(strict, da, 0.0)
            dp_ = dp_ref[r, :]
            dm = _bf(dl * decay)
            dn = _bf(dp_ * decay)
            knb = _bf(kn_)
            dkb = _dot(dm, knb) + dbw * eg
            dkn = _dot_tn(dm, _bf(kb)) + _dot_tn(dn, _bf(qt))
            dqt = _dot(dn, knb)
            gmat = dl * c["lmat"] + dp_ * c["pmat"]
            dqd_, dkd_ = dqd_ref[r, :], dkd_ref[r, :]
            qd = qt * eg
            kd = kn_ * ek
            bw = kb * eg
            kdsum = rowsum(dkd_ * kd)
            dgam = rowsum(gmat) - _row_to_col(jnp.sum(gmat, axis=0, keepdims=True), eye)
            dgam += rowsum(dbw * bw) + rowsum(dqd_ * qd) - kdsum
            last = (_iota((GDN_ROWS, 1), 0) & (CHUNK - 1)) == CHUNK - 1
            same = _chunk_masks()[0]
            same_f = jnp.where(same, 1.0, 0.0).astype(BF16)
            chunk_tot = jnp.max(_dot_exact_l(same_f, jnp.broadcast_to(kdsum, (GDN_ROWS, LANES))), axis=1, keepdims=True)
            dgam += jnp.where(last, chunk_tot, 0.0) + dgl_ref[r, :] * c["egl"]
            dq_ref[r, :] = (dqt + dqd_ * eg) * dk_scale
            dkk_ref[r, :] = dkn + dkd_ * ek + dkb * beta_
            dvv_ref[r, :] = dbv * beta_
            db_ref[r, :] = rowsum(dbv * v_) + rowsum(dkb * kn_)
            upper = jnp.where(same & (_iota((GDN_ROWS, GDN_ROWS), 0) <= _iota((GDN_ROWS, GDN_ROWS), 1)), 1.0, 0.0)
            dgb = _dot_exact_l(upper.astype(BF16), jnp.broadcast_to(dgam, (GDN_ROWS, LANES)))
            dg_ref[r, :] = _lane_pick(dgb, 0)

    def blk(width):
        return pl.BlockSpec((None, rows, width), lambda h, i: (h, i, 0))

    def shp(width):
        return jax.ShapeDtypeStruct((heads, t, width), F32)

    return pl.pallas_call(
        body, grid=(heads, t // rows),
        in_specs=[blk(dk), blk(dk), blk(dv), blk(1), blk(1), blk(GDN_ROWS), blk(dv), blk(dk),
                  blk(dv), blk(dk), blk(GDN_ROWS), blk(dk), blk(dk), blk(1)],
        out_specs=[blk(dk), blk(dk), blk(dv), blk(1), blk(1)],
        out_shape=[shp(dk), shp(dk), shp(dv), shp(1), shp(1)], name=name,
        compiler_params=_params("parallel", "parallel"))(qn, kn, v, g, beta, tinv, u, w, du, dw, dp, dqd, dkd, dgl)


def _gdn_pre_bwd_a(proj, pab, cq, ck, cv, a_log, dt_bias, dqn, dkn, dvv, dg, dbeta, *,
                   heads, dk, dv, col_q, col_k, col_v, row_lo, row_hi, name):
    t = proj.shape[0]
    tm = _pick(t, (384, 256, 128))
    nb = t // tm

    def body(pq_ref, pqp_ref, pk_ref, pkp_ref, pv_ref, pvp_ref, ab_ref, cq_ref, ck_ref, cv_ref, al_ref, dt_ref,
             dqn_ref, dkn_ref, dvv_ref, dg_ref, db_ref, dcq_ref, dck_ref, dcv_ref, dab_ref, dal_ref, ddt_ref):
        i, h = pl.program_id(0), pl.program_id(1)
        first = i == 0

        @pl.when((i == 0) & (h == 0))
        def _():
            dal_ref[...] = jnp.zeros_like(dal_ref)
            ddt_ref[...] = jnp.zeros_like(ddt_ref)

        @pl.when(h == 0)
        def _():
            dab_ref[...] = jnp.zeros_like(dab_ref)

        row = i * tm + _iota((tm, 1), 0)
        valid = (row >= row_lo) & (row < row_hi)

        def l2_bwd(c1, dn):
            x1 = _silu(c1)
            r = lax.rsqrt(jnp.sum(x1 * x1, axis=-1, keepdims=True) + L2_EPS)
            dn = jnp.where(valid, dn, 0.0)
            d1 = r * dn - x1 * (r * r * r) * jnp.sum(dn * x1, axis=-1, keepdims=True)
            return d1 * _silu_grad(c1)

        dcq_ref[...] = l2_bwd(_conv_taps(pq_ref[...], pqp_ref[...], cq_ref, first), dqn_ref[...])
        dck_ref[...] = l2_bwd(_conv_taps(pk_ref[...], pkp_ref[...], ck_ref, first), dkn_ref[...])
        cv1 = _conv_taps(pv_ref[...], pvp_ref[...], cv_ref, first)
        dcv_ref[...] = jnp.where(valid, dvv_ref[...], 0.0) * _silu_grad(cv1)
        ab = ab_ref[...]
        da = _lane_pick(ab, h)
        db = _lane_pick(ab, heads + h)
        a = _lane_pick(al_ref[...], h)
        dtb = _lane_pick(dt_ref[...], h)
        dgv = jnp.where(valid, dg_ref[...], 0.0)
        ea = jnp.exp(a)
        g = -ea * _softplus(da + dtb)
        dda = dgv * (-ea) * _sigmoid(da + dtb)
        beta = _sigmoid(db)
        ddb = jnp.where(valid, db_ref[...], 0.0) * beta * (1.0 - beta)
        lane = _iota((tm, LANES), 1)
        dab_ref[...] += jnp.where(lane == h, dda, 0.0) + jnp.where(lane == heads + h, ddb, 0.0)
        lane1 = _iota((1, LANES), 1)
        dal_ref[...] += jnp.where(lane1 == h, jnp.sum(dgv * g, axis=0, keepdims=True), 0.0)
        ddt_ref[...] += jnp.where(lane1 == h, jnp.sum(dda, axis=0, keepdims=True), 0.0)

    def cur(width, col):
        return pl.BlockSpec((tm, width), lambda i, h: (i, col // width + h))

    def prev(width, col):
        return pl.BlockSpec((8, width), lambda i, h: (jnp.maximum(i * (tm // 8) - 1, 0), col // width + h))

    def hd(width):
        return pl.BlockSpec((None, tm, width), lambda i, h: (h, i, 0))

    small = pl.BlockSpec((1, LANES), lambda i, h: (0, 0))
    return pl.pallas_call(
        body, grid=(nb, heads),
        in_specs=[cur(dk, col_q), prev(dk, col_q), cur(dk, col_k), prev(dk, col_k), cur(dv, col_v), prev(dv, col_v),
                  pl.BlockSpec((tm, LANES), lambda i, h: (i, 0)),
                  pl.BlockSpec((cq.shape[0], dk), lambda i, h: (0, h)), pl.BlockSpec((ck.shape[0], dk), lambda i, h: (0, h)),
                  pl.BlockSpec((cv.shape[0], dv), lambda i, h: (0, h)), small, small,
                  hd(dk), hd(dk), hd(dv), hd(1), hd(1)],
        out_specs=[hd(dk), hd(dk), hd(dv), pl.BlockSpec((tm, LANES), lambda i, h: (i, 0)), small, small],
        out_shape=[jax.ShapeDtypeStruct((heads, t, dk), F32), jax.ShapeDtypeStruct((heads, t, dk), F32),
                   jax.ShapeDtypeStruct((heads, t, dv), F32), jax.ShapeDtypeStruct((t, LANES), F32),
                   jax.ShapeDtypeStruct((1, LANES), F32), jax.ShapeDtypeStruct((1, LANES), F32)], name=name,
        compiler_params=_params("arbitrary", "arbitrary"))(
            proj, proj, proj, proj, proj, proj, pab, cq, ck, cv, a_log, dt_bias, dqn, dkn, dvv, dg, dbeta)


def _conv_bwd(proj, dc, cw, *, heads, width, col, name):
    t = proj.shape[0]
    tm = _pick(t, (384, 256, 128))
    nb = t // tm
    nk = cw.shape[0]

    def body(p_ref, pp_ref, d_ref, dn_ref, w_ref, dp_ref, dw_ref):
        i = pl.program_id(1)
        first, last = i == 0, i == nb - 1

        @pl.when(first)
        def _():
            dw_ref[...] = jnp.zeros_like(dw_ref)

        x, d = p_ref[...], d_ref[...]
        dx = d * w_ref[nk - 1:nk, :]
        dw_ref[nk - 1:nk, :] += jnp.sum(d * x, axis=0, keepdims=True)
        for s in range(1, nk):
            dx += _shift_up(d, dn_ref[...], s, last) * w_ref[nk - 1 - s:nk - s, :]
            dw_ref[nk - 1 - s:nk - s, :] += jnp.sum(d * _shift_down(x, pp_ref[...], s, first), axis=0, keepdims=True)
        dp_ref[...] = _bf(dx)

    return pl.pallas_call(
        body, grid=(heads, nb),
        in_specs=[pl.BlockSpec((tm, width), lambda h, i: (i, col // width + h)),
                  pl.BlockSpec((8, width), lambda h, i: (jnp.maximum(i * (tm // 8) - 1, 0), col // width + h)),
                  pl.BlockSpec((None, tm, width), lambda h, i: (h, i, 0)),
                  pl.BlockSpec((None, 8, width), lambda h, i: (h, jnp.minimum((i + 1) * (tm // 8), t // 8 - 1), 0)),
                  pl.BlockSpec((nk, width), lambda h, i: (0, h))],
        out_specs=[pl.BlockSpec((tm, width), lambda h, i: (i, h)), pl.BlockSpec((nk, width), lambda h, i: (0, h))],
        out_shape=[jax.ShapeDtypeStruct((t, heads * width), BF16), jax.ShapeDtypeStruct((nk, heads * width), F32)],
        name=name, compiler_params=_params("parallel", "arbitrary"))(proj, proj, dc, dc, cw)


def _sb_pre(proj, gq, gk, *, heads, dh, col_q, col_k, col_v, name):
    t = proj.shape[0]
    tm = _pick(t, (384, 256, 128))

    def body(q_ref, k_ref, v_ref, gq_ref, gk_ref, qo_ref, ko_ref, vo_ref):
        qo_ref[...] = _bf(_rms_fwd(q_ref[...], gq_ref[...]))
        ko_ref[...] = _bf(_rms_fwd(k_ref[...], gk_ref[...]))
        vo_ref[...] = _bf(v_ref[...])

    def cur(col):
        return pl.BlockSpec((tm, dh), lambda i, h: (i, col // dh + h))

    gspec = pl.BlockSpec((1, dh), lambda i, h: (0, 0))
    ospec = pl.BlockSpec((tm, dh), lambda i, h: (i, h))
    return pl.pallas_call(
        body, grid=(t // tm, heads), in_specs=[cur(col_q), cur(col_k), cur(col_v), gspec, gspec],
        out_specs=[ospec] * 3, out_shape=[jax.ShapeDtypeStruct((t, heads * dh), BF16)] * 3, name=name,
        compiler_params=_params("parallel", "parallel"))(proj, proj, proj, gq, gk)


def _sb_tile(q, kb, i, j, blk, key_lo, scale):
    z = _dot_nt(q, kb) * scale
    qpos = i * blk + _iota((blk, blk), 0)
    kpos = j * blk + _iota((blk, blk), 1)
    vis = (kpos < qpos) & (kpos >= key_lo)
    ls = jnp.minimum(z, 0.0) - jnp.log(1.0 + jnp.exp(-jnp.abs(z)))
    return vis, ls, jnp.where(vis, ls - z, 0.0)


def _dot2_r(x, m):
    hi, lo = _split2(x)
    return _dot(hi, m) + _dot(lo, m)


def _suffix_sums(x, later):
    return jnp.concatenate([_dot2_r(x[:, s:], later[s:, s:s + LANES]) for s in range(0, x.shape[1], LANES)], axis=1)


def _prefix_sums(x, earlier):
    return jnp.concatenate([_dot2_r(x[:, :s + LANES], earlier[:s + LANES, s:s + LANES])
                            for s in range(0, x.shape[1], LANES)], axis=1)


def _sb_fwd(qs, ks, vs, *, heads, dh, key_lo, name):
    t = qs.shape[0]
    blk = _pick(t, (3 * SB_BLOCK, 2 * SB_BLOCK, SB_BLOCK))
    nq = t // blk
    assert nq <= LANES
    scale = dh ** -0.5
    hp = SB_HEADS_PER_STEP

    def body(q_ref, k_ref, v_ref, o_ref, c_ref):
        i = pl.program_id(1)
        later = jnp.where(_iota((blk, blk), 0) > _iota((blk, blk), 1), 1.0, 0.0).astype(BF16)
        lane = _iota((blk, LANES), 1)
        c_ref[...] = jnp.zeros_like(c_ref)

        def step(n, carry):
            j = i - n
            rows = pl.ds(pl.multiple_of(j * blk, blk), blk)
            out = []
            for hh in range(hp):
                cols = pl.ds(hh * dh, dh)
                acc, run = carry[2 * hh], carry[2 * hh + 1]
                vis, ls, lk = _sb_tile(q_ref[:, cols], k_ref[rows, cols], i, j, blk, key_lo, scale)
                wgt = jnp.where(vis, jnp.exp(ls + _suffix_sums(lk, later) + run), 0.0)
                c_ref[hh] = jnp.where(lane == j, run, c_ref[hh])
                out += [acc + _dot(_bf(wgt), v_ref[rows, cols]), run + jnp.sum(lk, axis=1, keepdims=True)]
            return tuple(out)

        res = lax.fori_loop(0, i + 1, step, (jnp.zeros((blk, dh), F32), jnp.zeros((blk, 1), F32)) * hp)
        for hh in range(hp):
            o_ref[:, pl.ds(hh * dh, dh)] = _bf(res[2 * hh])

    full = pl.BlockSpec((t, hp * dh), lambda h, i: (0, h))
    return pl.pallas_call(
        body, grid=(heads // hp, nq),
        in_specs=[pl.BlockSpec((blk, hp * dh), lambda h, i: (i, h)), full, full],
        out_specs=[pl.BlockSpec((blk, hp * dh), lambda h, i: (i, h)),
                   pl.BlockSpec((hp, blk, LANES), lambda h, i: (h, i, 0))],
        out_shape=[jax.ShapeDtypeStruct((t, heads * dh), BF16), jax.ShapeDtypeStruct((heads, t, LANES), F32)],
        name=name, compiler_params=_params("parallel", "parallel"))(qs, ks, vs)


def _sb_bwd(qs, ks, vs, do, carry, *, heads, dh, key_lo, name, scatter=None):
    t = qs.shape[0]
    blk = _pick(t, (3 * SB_BLOCK, 2 * SB_BLOCK, SB_BLOCK))
    nq = t // blk
    scale = dh ** -0.5
    hp = SB_HEADS_PER_STEP

    def body(q_ref, k_ref, v_ref, do_ref, c_ref, dq_ref, dk_ref, dv_ref):
        i = pl.program_id(1)

        @pl.when(i == 0)
        def _():
            dk_ref[...] = jnp.zeros_like(dk_ref)
            dv_ref[...] = jnp.zeros_like(dv_ref)

        r0 = _iota((blk, blk), 0)
        r1 = _iota((blk, blk), 1)
        later = jnp.where(r0 > r1, 1.0, 0.0).astype(BF16)
        earlier = jnp.where(r0 < r1, 1.0, 0.0).astype(BF16)

        def step(j, carry):
            rows = pl.ds(pl.multiple_of(j * blk, blk), blk)
            out = []
            for hh in range(hp):
                cols = pl.ds(hh * dh, dh)
                dq, pre = carry[2 * hh], carry[2 * hh + 1]
                q, dob, kb, vb = q_ref[:, cols], do_ref[:, cols], k_ref[rows, cols], v_ref[rows, cols]
                vis, ls, lk = _sb_tile(q, kb, i, j, blk, key_lo, scale)
                wgt = jnp.where(vis, jnp.exp(ls + _suffix_sums(lk, later) + _lane_pick(c_ref[hh], j)), 0.0)
                e = wgt * _dot_nt(dob, vb)
                before = jnp.where(vis, _prefix_sums(e, earlier) + pre, 0.0)
                sig = jnp.exp(ls)
                dz = _bf((e * (1.0 - sig) - before * sig) * scale)
                dk_ref[rows, cols] += _dot_tn(dz, q)
                dv_ref[rows, cols] += _dot_tn(_bf(wgt), dob)
                out += [dq + _dot(dz, kb), pre + jnp.sum(e, axis=1, keepdims=True)]
            return tuple(out)

        res = lax.fori_loop(0, i + 1, step, (jnp.zeros((blk, dh), F32), jnp.zeros((blk, 1), F32)) * hp)
        for hh in range(hp):
            dq_ref[:, pl.ds(hh * dh, dh)] = res[2 * hh]

    full = pl.BlockSpec((t, hp * dh), lambda h, i: (0, h))
    qblk = pl.BlockSpec((blk, hp * dh), lambda h, i: (i, h))
    call = dict(grid=(heads // hp, nq),
                in_specs=[qblk, full, full, qblk, pl.BlockSpec((hp, blk, LANES), lambda h, i: (h, i, 0))],
                out_specs=[qblk, full, full], out_shape=[jax.ShapeDtypeStruct((t, heads * dh), F32)] * 3, name=name)
    if scatter is None:
        return pl.pallas_call(body, compiler_params=_params("parallel", "arbitrary"), **call)(qs, ks, vs, do, carry)
    res = _call_with_exchange(body, scratch_shapes=[], args=(qs, ks, vs, do, carry), srcs=scatter,
                              scatter=[True] * len(scatter), **call)
    return [*res[:3], list(res[3:])]


def _sb_pre_bwd(proj, gq, gk, dq, dk, dv, *, heads, dh, col_q, col_k, name):
    t = proj.shape[0]
    tm = _pick(t, (384, 256, 128))

    def body(q_ref, k_ref, gq_ref, gk_ref, dq_ref, dk_ref, dv_ref, oq_ref, ok_ref, ov_ref, dgq_ref, dgk_ref):
        @pl.when((pl.program_id(0) == 0) & (pl.program_id(1) == 0))
        def _():
            dgq_ref[...] = jnp.zeros_like(dgq_ref)
            dgk_ref[...] = jnp.zeros_like(dgk_ref)

        dq_, gq_r = _rms_bwd(q_ref[...], gq_ref[...], dq_ref[...])
        dk_, gk_r = _rms_bwd(k_ref[...], gk_ref[...], dk_ref[...])
        oq_ref[...] = _bf(dq_)
        ok_ref[...] = _bf(dk_)
        ov_ref[...] = _bf(dv_ref[...])
        dgq_ref[...] += jnp.sum(gq_r, axis=0, keepdims=True)
        dgk_ref[...] += jnp.sum(gk_r, axis=0, keepdims=True)

    def cur(col):
        return pl.BlockSpec((tm, dh), lambda i, h: (i, col // dh + h))

    gspec = pl.BlockSpec((1, dh), lambda i, h: (0, 0))
    ospec = pl.BlockSpec((tm, dh), lambda i, h: (i, h))
    return pl.pallas_call(
        body, grid=(t // tm, heads), in_specs=[cur(col_q), cur(col_k), gspec, gspec, ospec, ospec, ospec],
        out_specs=[ospec, ospec, ospec, gspec, gspec],
        out_shape=[jax.ShapeDtypeStruct((t, heads * dh), BF16)] * 3 + [jax.ShapeDtypeStruct((1, dh), F32)] * 2,
        name=name, compiler_params=_params("arbitrary", "arbitrary"))(proj, proj, gq, gk, dq, dk, dv)


PEERS = N_DEV - 1


def _exchange_copies(ins, outs, send_sems, recv_sems, local_sems, scatter):
    x, y, c = lax.axis_index("x"), lax.axis_index("y"), lax.axis_index("c")
    me = 4 * x + 2 * y + c
    copies = []
    for a in range(len(ins)):
        own = ins[a].at[me] if scatter[a] else ins[a]
        copies.append(pltpu.make_async_copy(own, outs[a].at[me], local_sems.at[a]))
        for k in range(1, N_DEV):
            px = (x + (k >> 2 & 1)) % 2
            py = (y + (k >> 1 & 1)) % 2
            pc = (c + (k & 1)) % 2
            src = ins[a].at[4 * px + 2 * py + pc] if scatter[a] else ins[a]
            copies.append(pltpu.make_async_remote_copy(
                src_ref=src, dst_ref=outs[a].at[me], send_sem=send_sems.at[a * PEERS + k - 1],
                recv_sem=recv_sems.at[a * PEERS + k - 1], device_id=(px, py, pc), device_id_type=MESH))
    return copies


def _exchange_shapes(srcs, scatter):
    return [jax.ShapeDtypeStruct(s.shape if sc else (N_DEV,) + s.shape, s.dtype) for s, sc in zip(srcs, scatter)]


def _exchange_sems(n):
    return [pltpu.SemaphoreType.DMA((n * PEERS,)), pltpu.SemaphoreType.DMA((n * PEERS,)), pltpu.SemaphoreType.DMA((n,))]


def _exchange(srcs, *, scatter, name):
    n = len(srcs)

    def body(*refs):
        copies = _exchange_copies(refs[:n], refs[n:2 * n], *refs[2 * n:], scatter)
        for cp in copies:
            cp.start()
        for cp in copies:
            cp.wait()

    any_spec = pl.BlockSpec(memory_space=pl.ANY)
    return pl.pallas_call(
        body, in_specs=[any_spec] * n, out_specs=[any_spec] * n, out_shape=_exchange_shapes(srcs, scatter),
        scratch_shapes=_exchange_sems(n), name=name,
        compiler_params=pltpu.CompilerParams(has_side_effects=True))(*srcs)


def _call_with_exchange(body, *, grid, in_specs, out_specs, out_shape, scratch_shapes, args, srcs, scatter, name):
    n, n_in, n_out, n_scr = len(srcs), len(args), len(out_shape), len(scratch_shapes)

    def full_body(*refs):
        ins, xin = refs[:n_in], refs[n_in:n_in + n]
        outs, xout = refs[n_in + n:n_in + n + n_out], refs[n_in + n + n_out:n_in + 2 * n + n_out]
        scr = refs[n_in + 2 * n + n_out:]
        ids = [pl.program_id(a) for a in range(len(grid))]
        first = functools.reduce(jnp.logical_and, [i == 0 for i in ids])
        last = functools.reduce(jnp.logical_and, [i == g - 1 for i, g in zip(ids, grid)])
        copies = _exchange_copies(xin, xout, *scr[n_scr:], scatter)

        @pl.when(first)
        def _():
            for cp in copies:
                cp.start()

        body(*ins, *outs, *scr[:n_scr])

        @pl.when(last)
        def _():
            for cp in copies:
                cp.wait()

    any_spec = pl.BlockSpec(memory_space=pl.ANY)
    return pl.pallas_call(
        full_body, grid=grid, in_specs=list(in_specs) + [any_spec] * n, out_specs=list(out_specs) + [any_spec] * n,
        out_shape=list(out_shape) + _exchange_shapes(srcs, scatter),
        scratch_shapes=list(scratch_shapes) + _exchange_sems(n), name=name,
        compiler_params=pltpu.CompilerParams(dimension_semantics=("arbitrary",) * len(grid),
                                             vmem_limit_bytes=V7X_VMEM_LIMIT_BYTES, has_side_effects=True))(*args, *srcs)


def _adam_math(g, w, m, v):
    m2 = ADAM_B1 * m + (1.0 - ADAM_B1) * g
    v2 = ADAM_B2 * v + (1.0 - ADAM_B2) * (g * g)
    m_hat = m2 / (1.0 - ADAM_B1 ** ADAM_STEP)
    v_hat = v2 / (1.0 - ADAM_B2 ** ADAM_STEP)
    return -ADAM_LR * (m_hat / (jnp.sqrt(v_hat) + ADAM_EPS) + ADAM_WD * w), m2, v2


def _adamw_slabs(slabs, w, m, v, *, name):
    r, c = w.shape
    tr = _pick(r, (128, 64, 32, 16, 8)) if r % 8 == 0 else r

    def body(s_ref, w_ref, m_ref, v_ref, g_ref, d_ref, mo_ref, vo_ref):
        g = s_ref[0].astype(F32)
        for p in range(1, N_DEV):
            g = g + s_ref[p].astype(F32)
        g_ref[...] = g
        d_ref[...], mo_ref[...], vo_ref[...] = _adam_math(g, w_ref[...], m_ref[...], v_ref[...])

    spec = pl.BlockSpec((tr, c), lambda i: (i, 0))
    return pl.pallas_call(
        body, grid=(r // tr,), in_specs=[pl.BlockSpec((N_DEV, tr, c), lambda i: (0, i, 0)), spec, spec, spec],
        out_specs=[spec] * 4, out_shape=[jax.ShapeDtypeStruct((r, c), F32)] * 4, name=name,
        compiler_params=_params("parallel"))(slabs, w, m, v)


def _adamw_small(g, w, m, v, *, name):
    def body(g_ref, w_ref, m_ref, v_ref, d_ref, mo_ref, vo_ref):
        d_ref[...], mo_ref[...], vo_ref[...] = _adam_math(g_ref[...], w_ref[...], m_ref[...], v_ref[...])

    return pl.pallas_call(body, out_shape=[jax.ShapeDtypeStruct(w.shape, F32)] * 3, name=name)(g, w, m, v)


def _sum_slabs(slabs, *, name):
    def body(s_ref, o_ref):
        acc = s_ref[0]
        for p in range(1, N_DEV):
            acc = acc + s_ref[p]
        o_ref[...] = acc

    return pl.pallas_call(body, out_shape=jax.ShapeDtypeStruct(slabs.shape[1:], F32), name=name)(slabs)


def _gather_cols(g):
    return jnp.transpose(g, (1, 0, 2)).reshape(g.shape[1], -1)


def _col_slabs(a):
    return jnp.transpose(a.reshape(a.shape[0], N_DEV, -1), (1, 0, 2))


def _pad_lanes(a):
    return jnp.pad(a, ((0, 0), (0, LANES - a.shape[1])))


def _local_step(x, target, meta, g_mix, w_main, w_ab, cq, ck, cv, a_log, dt_bias, g_dn, g_sbq, g_sbk, g_ffn, rest,
                shards=False):
    seq, d = x.shape
    n_meta = meta.shape[0]
    heads = a_log.shape[1]
    qk = cq.shape[1]
    dvt = cv.shape[1]
    dk, dv = qk // heads, dvt // heads
    dh = g_sbq.shape[1]
    sbw = rest[2].shape[0] * N_DEV if shards else rest[1].shape[0]
    sb_heads = sbw // dh
    pad_l = (-n_meta) % CHUNK
    row_x = pad_l + n_meta
    rows = row_x + seq
    t = -(-rows // GDN_ROWS) * GDN_ROWS
    col_q, col_k, col_v, col_z = 0, qk, 2 * qk, 2 * qk + dvt
    col_sq = 2 * qk + 2 * dvt
    col_sk, col_sv, col_gd, col_gs = col_sq + sbw, col_sq + 2 * sbw, col_sq + 3 * sbw, col_sq + 3 * sbw + d

    def rows_pad(a):
        return jnp.concatenate([jnp.zeros((row_x, d), F32), a, jnp.zeros((t - rows, d), F32)], axis=0)

    h0 = jnp.concatenate([jnp.zeros((pad_l, d), F32), meta, x, jnp.zeros((t - rows, d), F32)], axis=0)
    tgt = rows_pad(target)
    a_log_p, dt_p = _pad_lanes(a_log), _pad_lanes(dt_bias)

    proj, n1 = _mm_norm(h0, g_mix, w_main, name="proj")
    pab = _mm_nn(n1, w_ab, out_dtype=F32, name="proj_ab")
    gk = dict(heads=heads, dk=dk, dv=dv, col_q=col_q, col_k=col_k, col_v=col_v, row_lo=pad_l, row_hi=rows)
    qn, kn, vv, g, beta = _gdn_pre(proj, pab, cq, ck, cv, a_log_p, dt_p, name="gdn_pre", **gk)
    if shards:
        u, w, pm, qd, kd, egl, tinv, (g_fi, g_bd, g_bs, g_out, g_fo) = _gdn_prep(
            qn, kn, vv, g, beta, name="gdn_prep", gather=list(rest))
        w_fi = _gather_cols(g_fi)
        d_ff = w_fi.shape[1] // 2
        w_bd, w_bs, w_out, w_fg, w_fu, w_fo = (g_bd.reshape(-1, d), g_bs.reshape(-1, d), g_out.reshape(-1, d),
                                               w_fi[:, :d_ff], w_fi[:, d_ff:], g_fo.reshape(-1, d))
    else:
        u, w, pm, qd, kd, egl, tinv = _gdn_prep(qn, kn, vv, g, beta, name="gdn_prep")
        w_bd, w_bs, w_out, w_fg, w_fu, w_fo = rest
    o_raw, o_dn, states = _gdn_scan(u, w, pm, qd, kd, egl, proj, g_dn, col_z=col_z, name="gdn_scan")
    qs, ks, vs = _sb_pre(proj, g_sbq, g_sbk, heads=sb_heads, dh=dh, col_q=col_sq, col_k=col_sk, col_v=col_sv,
                         name="sb_pre")
    o_sb, carry = _sb_fwd(qs, ks, vs, heads=sb_heads, dh=dh, key_lo=pad_l, name="sb_fwd")
    merged, br_dn, br_sb = _merge_fwd(o_dn, o_sb, w_bd, w_bs, proj, col_gd=col_gd, col_gs=col_gs, name="merge")
    h1 = _mm_res(h0, merged, w_out, name="mix_out")
    gate, up, act, n2 = _mm_norm_swiglu(h1, g_ffn, w_fg, w_fu, name="ffn_in")
    dy, dyb, lsum = _mm_res_loss(h1, act, w_fo, tgt, row0=row_x, nrows=seq, name="ffn_out_loss")

    dgate, dup = _swiglu_bwd(dyb, w_fo, gate, up, name="ffn_out_bwd")
    d_w_fo = _mm_tn(act, dyb, name="dw_ffn_out")
    d_w_fg = _mm_tn(n2, dgate, name="dw_ffn_gate")
    d_w_fu = _mm_tn(n2, dup, name="dw_ffn_up")
    dh1, dh1b, d_g_ffn = _mm_nt_rmsbwd([(dgate, w_fg), (dup, w_fu)], None, h1, g_ffn, dy, name="ffn_in_bwd")

    dbd, dbs, dgd, dgs = _merge_bwd(dh1b, w_out, proj, br_dn, br_sb, col_gd=col_gd, col_gs=col_gs, name="mix_out_bwd")
    d_w_out = _mm_tn(merged, dh1b, name="dw_out")
    d_w_bd = _mm_tn(o_dn, dbd, name="dw_branch_dn")
    d_w_bs = _mm_tn(o_sb, dbs, name="dw_branch_sb")
    do_dn = _mm_nt(dbd, w_bd, out_dtype=F32, name="branch_dn_bwd")
    do_sb = _mm_nt(dbs, w_bs, out_dtype=BF16, name="branch_sb_bwd")

    do_raw, dz, d_g_dn = _gdn_post_bwd(o_raw, proj, g_dn, do_dn, col_z=col_z, name="gdn_post_bwd")
    du, dw, dp, dqd, dkd, dgl = _gdn_bwd_scan(u, w, pm, qd, kd, egl, states, do_raw, name="gdn_bwd_scan")
    dqn, dkn, dvv, dg, dbeta = _gdn_bwd_prep(qn, kn, vv, g, beta, tinv, u, w, du, dw, dp, dqd, dkd, dgl,
                                            name="gdn_bwd_prep")
    dcq, dck, dcv, dpab, d_a_log, d_dt = _gdn_pre_bwd_a(proj, pab, cq, ck, cv, a_log_p, dt_p, dqn, dkn, dvv, dg, dbeta,
                                                        name="gdn_pre_bwd", **gk)
    dpq, d_cq = _conv_bwd(proj, dcq, cq, heads=heads, width=dk, col=col_q, name="conv_q_bwd")
    dpk, d_ck = _conv_bwd(proj, dck, ck, heads=heads, width=dk, col=col_k, name="conv_k_bwd")
    dpv, d_cv = _conv_bwd(proj, dcv, cv, heads=heads, width=dv, col=col_v, name="conv_v_bwd")

    early = None
    if shards:
        slabs = [_col_slabs(_bf(jnp.concatenate([d_w_fg, d_w_fu], axis=1)))]
        slabs += [_bf(a).reshape(N_DEV, -1, d) for a in (d_w_bd, d_w_bs, d_w_out, d_w_fo)]
        dqs, dks, dvs, early = _sb_bwd(qs, ks, vs, do_sb, carry, heads=sb_heads, dh=dh, key_lo=pad_l, name="sb_bwd",
                                       scatter=slabs)
    else:
        dqs, dks, dvs = _sb_bwd(qs, ks, vs, do_sb, carry, heads=sb_heads, dh=dh, key_lo=pad_l, name="sb_bwd")
    dsq, dsk, dsv, d_g_sbq, d_g_sbk = _sb_pre_bwd(proj, g_sbq, g_sbk, dqs, dks, dvs, heads=sb_heads, dh=dh,
                                                   col_q=col_sq, col_k=col_sk, name="sb_pre_bwd")

    dproj = jnp.concatenate([dpq, dpk, dpv, dz, dsq, dsk, dsv, dgd, dgs], axis=1)
    dpab_b = _bf(dpab)
    d_w_main = _mm_tn(n1, dproj, name="dw_in_main")
    d_w_ab = _mm_tn(n1, dpab_b, name="dw_in_ab")
    dh0, _, d_g_mix = _mm_nt_rmsbwd([(dproj, w_main)], (dpab_b, w_ab), h0, g_mix, dh1, name="proj_bwd")

    return dict(lsum=lsum, grad_x=dh0[row_x:rows], d_meta=dh0[pad_l:row_x], d_g_mix=d_g_mix, d_w_main=d_w_main,
                d_w_ab=d_w_ab, d_cq=d_cq, d_ck=d_ck, d_cv=d_cv, d_a_log=d_a_log[:, :heads], d_dt=d_dt[:, :heads],
                d_g_dn=d_g_dn, d_g_sbq=d_g_sbq, d_g_sbk=d_g_sbk, d_w_bd=d_w_bd, d_w_bs=d_w_bs, d_w_out=d_w_out,
                d_g_ffn=d_g_ffn, d_w_fg=d_w_fg, d_w_fu=d_w_fu, d_w_fo=d_w_fo, early=early)


def _pack(parts):
    flat = []
    for a in parts:
        a = a.reshape(-1)
        flat.append(jnp.pad(a, (0, (-a.shape[0]) % LANES)))
    v = jnp.concatenate(flat)
    v = jnp.pad(v, (0, (-v.shape[0]) % (8 * LANES)))
    return v.reshape(-1, LANES)


def _unpack(packed, shapes):
    flat = packed.reshape(-1)
    out, pos = [], 0
    for s in shapes:
        n = math.prod(s)
        out.append(flat[pos:pos + n].reshape(s))
        pos += n + (-n) % LANES
    return out


def kernel(x, meta_tokens, norm_mix_gain, w_in, conv_q, conv_k, conv_v, dn_a_log, dn_dt_bias, dn_out_norm_gain, sb_q_norm_gain, sb_k_norm_gain, w_branch_dn, w_branch_sb, w_out, norm_ffn_gain, w_ffn_in, w_ffn_out, loss_target, m_meta_tokens, m_norm_mix_gain, m_w_in, m_conv_q, m_conv_k, m_conv_v, m_dn_a_log, m_dn_dt_bias, m_dn_out_norm_gain, m_sb_q_norm_gain, m_sb_k_norm_gain, m_w_branch_dn, m_w_branch_sb, m_w_out, m_norm_ffn_gain, m_w_ffn_in, m_w_ffn_out, v_meta_tokens, v_norm_mix_gain, v_w_in, v_conv_q, v_conv_k, v_conv_v, v_dn_a_log, v_dn_dt_bias, v_dn_out_norm_gain, v_sb_q_norm_gain, v_sb_k_norm_gain, v_w_branch_dn, v_w_branch_sb, v_w_out, v_norm_ffn_gain, v_w_ffn_in, v_w_ffn_out):
    me = 4 * lax.axis_index("x") + 2 * lax.axis_index("y") + lax.axis_index("c")
    heads = dn_a_log.shape[1]
    d = x.shape[2]
    qk = conv_q.shape[2] * N_DEV
    dvt = conv_v.shape[2] * N_DEV
    col_ab = 2 * qk + 2 * dvt

    small_shapes = [meta_tokens.shape, conv_q.shape[1:], conv_k.shape[1:], conv_v.shape[1:]]
    small = _pack([meta_tokens, conv_q[0], conv_k[0], conv_v[0]])
    g_in, g_small = _exchange([_bf(w_in[0]), small], scatter=[False, False], name="gather_w_in")
    w_full = _gather_cols(g_in)
    w_main = jnp.concatenate([w_full[:, :col_ab], w_full[:, col_ab + 2 * heads:]], axis=1)
    w_ab = _pad_lanes(w_full[:, col_ab:col_ab + 2 * heads])
    parts = [_unpack(g_small[p], small_shapes) for p in range(N_DEV)]
    meta_f, cq_f, ck_f, cv_f = (jnp.concatenate([parts[p][a] for p in range(N_DEV)], axis=1) for a in range(4))

    r = _local_step(x[0], loss_target[0], meta_f, norm_mix_gain, w_main, w_ab, cq_f, ck_f, cv_f, dn_a_log, dn_dt_bias,
                    dn_out_norm_gain, sb_q_norm_gain, sb_k_norm_gain, norm_ffn_gain,
                    (_bf(w_ffn_in[0]), _bf(w_branch_dn[0]), _bf(w_branch_sb[0]), _bf(w_out[0]), _bf(w_ffn_out[0])),
                    shards=True)
    s_fi, s_bd, s_bs, s_out, s_fo = r["early"]

    d_w_in = jnp.concatenate([r["d_w_main"][:, :col_ab], r["d_w_ab"][:, :2 * heads], r["d_w_main"][:, col_ab:]], axis=1)
    loss_part = (0.5 / d) * jnp.sum(r["lsum"], axis=1, keepdims=True)
    small_g = [r["d_meta"], r["d_g_mix"], r["d_cq"], r["d_ck"], r["d_cv"], r["d_a_log"], r["d_dt"], r["d_g_dn"],
               r["d_g_sbq"], r["d_g_sbk"], r["d_g_ffn"], loss_part]
    s_in, g_packs = _exchange([_col_slabs(_bf(d_w_in)), _pack(small_g)], scatter=[True, False], name="scatter_dw_in")
    (g_meta, g_mix, g_cq, g_ck, g_cv, g_al, g_dt, g_gdn, g_sbq, g_sbk, g_ffn, loss) = _unpack(
        _sum_slabs(g_packs, name="sum_small_grads"), [a.shape for a in small_g])

    def mine(a, width):
        return lax.dynamic_slice_in_dim(a, me * width, width, axis=1)

    big = dict(w_in=(s_in, w_in, m_w_in, v_w_in), w_branch_dn=(s_bd, w_branch_dn, m_w_branch_dn, v_w_branch_dn),
               w_branch_sb=(s_bs, w_branch_sb, m_w_branch_sb, v_w_branch_sb), w_out=(s_out, w_out, m_w_out, v_w_out),
               w_ffn_in=(s_fi, w_ffn_in, m_w_ffn_in, v_w_ffn_in), w_ffn_out=(s_fo, w_ffn_out, m_w_ffn_out, v_w_ffn_out))
    tiny = dict(meta_tokens=(mine(g_meta, d // N_DEV), meta_tokens, m_meta_tokens, v_meta_tokens),
                norm_mix_gain=(g_mix, norm_mix_gain, m_norm_mix_gain, v_norm_mix_gain),
                conv_q=(mine(g_cq, qk // N_DEV), conv_q[0], m_conv_q[0], v_conv_q[0]),
                conv_k=(mine(g_ck, qk // N_DEV), conv_k[0], m_conv_k[0], v_conv_k[0]),
                conv_v=(mine(g_cv, dvt // N_DEV), conv_v[0], m_conv_v[0], v_conv_v[0]),
                dn_a_log=(g_al, dn_a_log, m_dn_a_log, v_dn_a_log), dn_dt_bias=(g_dt, dn_dt_bias, m_dn_dt_bias, v_dn_dt_bias),
                dn_out_norm_gain=(g_gdn, dn_out_norm_gain, m_dn_out_norm_gain, v_dn_out_norm_gain),
                sb_q_norm_gain=(g_sbq, sb_q_norm_gain, m_sb_q_norm_gain, v_sb_q_norm_gain),
                sb_k_norm_gain=(g_sbk, sb_k_norm_gain, m_sb_k_norm_gain, v_sb_k_norm_gain),
                norm_ffn_gain=(g_ffn, norm_ffn_gain, m_norm_ffn_gain, v_norm_ffn_gain))
    order = ["meta_tokens", "norm_mix_gain", "w_in", "conv_q", "conv_k", "conv_v", "dn_a_log", "dn_dt_bias",
             "dn_out_norm_gain", "sb_q_norm_gain", "sb_k_norm_gain", "w_branch_dn", "w_branch_sb", "w_out",
             "norm_ffn_gain", "w_ffn_in", "w_ffn_out"]
    grads, deltas, new_m, new_v = [], [], [], []
    for name in order:
        if name in big:
            slabs, w, m, v = big[name]
            g, dl, mo, vo = _adamw_slabs(slabs, w[0], m[0], v[0], name="adamw_" + name)
            like = w.shape
        else:
            g, w, m, v = tiny[name]
            like = dict(conv_q=conv_q, conv_k=conv_k, conv_v=conv_v).get(name, w).shape
            dl, mo, vo = _adamw_small(g, w, m, v, name="adamw_" + name)
        for lst, a in ((grads, g), (deltas, dl), (new_m, mo), (new_v, vo)):
            lst.append(a.reshape(like))
    return (loss.reshape(()), r["grad_x"][None], *grads, *deltas, *new_m, *new_v)
```

```python
import functools
import math

import jax
import jax.numpy as jnp
from jax import lax
from jax.experimental import pallas as pl
from jax.experimental.pallas import tpu as pltpu

F32 = jnp.float32
BF16 = jnp.bfloat16

N_DEV = 8
CHUNK = 64
CHUNK_SHIFT = 6
GDN_ROWS = 2 * CHUNK
SB_BLOCK = 128
SB_HEADS_PER_STEP = 2
LANES = 128
RMS_EPS = 1e-6
L2_EPS = 1e-6
ADAM_LR = 0.001
ADAM_B1 = 0.9
ADAM_B2 = 0.999
ADAM_EPS = 1e-08
ADAM_WD = 0.01
ADAM_STEP = 10
V7X_VMEM_LIMIT_BYTES = 56 * 1024 * 1024
MM_TN_OUT_BLOCK_BYTES = 6 * 1024 * 1024
ROWS_BIG = (1056, 512, 384, 256, 128)
ROWS_MID = (528, 384, 256, 128)
SCAN_HEADS = 4

MESH = pl.DeviceIdType.MESH


def _params(*sem):
    return pltpu.CompilerParams(dimension_semantics=sem or None, vmem_limit_bytes=V7X_VMEM_LIMIT_BYTES)


def _pick(n, cands):
    for c in cands:
        if n % c == 0:
            return c
    raise ValueError(f"no block size among {cands} divides {n}")


def _bf(x):
    return x.astype(BF16)


def _dot(a, b):
    return jnp.dot(a, b, preferred_element_type=F32)


def _dot_nt(a, b):
    return lax.dot_general(a, b, (((1,), (1,)), ((), ())), preferred_element_type=F32)


def _dot_tn(a, b):
    return lax.dot_general(a, b, (((0,), (0,)), ((), ())), preferred_element_type=F32)


def _split2(x):
    hi = _bf(x)
    return hi, _bf(x - hi.astype(F32))


def _split3(x):
    hi = _bf(x)
    r = x - hi.astype(F32)
    mid = _bf(r)
    return hi, mid, _bf(r - mid.astype(F32))


def _dot_hp(a, b, dot=_dot):
    ah, al = _split2(a)
    bh, bl = _split2(b)
    return dot(ah, bh) + dot(ah, bl) + dot(al, bh)


def _dot_exact_r(x, m, dot=_dot):
    h, mi, lo = _split3(x)
    return dot(h, m) + dot(mi, m) + dot(lo, m)


def _dot_exact_l(m, x, dot=_dot):
    h, mi, lo = _split3(x)
    return dot(m, h) + dot(m, mi) + dot(m, lo)


def _sigmoid(x):
    return 1.0 / (1.0 + jnp.exp(-x))


def _silu(x):
    return x * _sigmoid(x)


def _silu_grad(x):
    s = _sigmoid(x)
    return s * (1.0 + x * (1.0 - s))


def _softplus(x):
    return jnp.maximum(x, 0.0) + jnp.log(1.0 + jnp.exp(-jnp.abs(x)))


def _rms_fwd(h, gain):
    r = lax.rsqrt(jnp.mean(h * h, axis=-1, keepdims=True) + RMS_EPS)
    return h * r * gain


def _rms_bwd(h, gain, dy):
    r = lax.rsqrt(jnp.mean(h * h, axis=-1, keepdims=True) + RMS_EPS)
    dyg = dy * gain
    dh = r * dyg - h * (r * r * r) * jnp.mean(dyg * h, axis=-1, keepdims=True)
    return dh, dy * h * r


def _iota(shape, dim):
    return lax.broadcasted_iota(jnp.int32, shape, dim)


def _lane_pick(x, idx):
    return jnp.sum(jnp.where(_iota(x.shape, 1) == idx, x, 0.0), axis=1, keepdims=True)


def _mm_nn(a, b, *, out_dtype, name):
    m, k = a.shape
    n = b.shape[1]
    tm, tn = _pick(m, ROWS_BIG), _pick(n, (1024, 512, 256, 128))

    def body(a_ref, b_ref, o_ref):
        o_ref[...] = _dot(a_ref[...], b_ref[...]).astype(out_dtype)

    return pl.pallas_call(
        body, grid=(m // tm, n // tn),
        in_specs=[pl.BlockSpec((tm, k), lambda i, j: (i, 0)), pl.BlockSpec((k, tn), lambda i, j: (0, j))],
        out_specs=pl.BlockSpec((tm, tn), lambda i, j: (i, j)),
        out_shape=jax.ShapeDtypeStruct((m, n), out_dtype), name=name,
        compiler_params=_params("parallel", "parallel"))(a, b)


def _mm_nt(a, b, *, out_dtype, name):
    m, k = a.shape
    n = b.shape[0]
    tm, tn = _pick(m, ROWS_BIG), _pick(n, (1024, 512, 256, 128))

    def body(a_ref, b_ref, o_ref):
        o_ref[...] = _dot_nt(a_ref[...], b_ref[...]).astype(out_dtype)

    return pl.pallas_call(
        body, grid=(m // tm, n // tn),
        in_specs=[pl.BlockSpec((tm, k), lambda i, j: (i, 0)), pl.BlockSpec((tn, k), lambda i, j: (j, 0))],
        out_specs=pl.BlockSpec((tm, tn), lambda i, j: (i, j)),
        out_shape=jax.ShapeDtypeStruct((m, n), out_dtype), name=name,
        compiler_params=_params("parallel", "parallel"))(a, b)


def _mm_tn(a, b, *, name):
    t, m = a.shape
    n = b.shape[1]
    tn = _pick(n, (2816, 2048, 1408, 1024, 512, 256, 128))
    tm = _pick(m, tuple(c for c in (1408, 1024, 512, 256, 128) if c * tn * 4 <= MM_TN_OUT_BLOCK_BYTES))
    tk = _pick(t, (1408, 1024, 512, 384, 256, 128))

    def body(a_ref, b_ref, o_ref):
        @pl.when(pl.program_id(2) == 0)
        def _():
            o_ref[...] = jnp.zeros_like(o_ref)

        o_ref[...] += _dot_tn(a_ref[...], b_ref[...])

    return pl.pallas_call(
        body, grid=(m // tm, n // tn, t // tk),
        in_specs=[pl.BlockSpec((tk, tm), lambda i, j, k: (k, i)), pl.BlockSpec((tk, tn), lambda i, j, k: (k, j))],
        out_specs=pl.BlockSpec((tm, tn), lambda i, j, k: (i, j)),
        out_shape=jax.ShapeDtypeStruct((m, n), F32), name=name,
        compiler_params=_params("parallel", "parallel", "arbitrary"))(a, b)


def _mm_norm(h, gain, w, *, name):
    m, k = h.shape
    n = w.shape[1]
    tm, tn = _pick(m, ROWS_BIG), _pick(n, (1024, 512, 256, 128))

    def body(h_ref, g_ref, w_ref, o_ref, n_ref):
        @pl.when(pl.program_id(1) == 0)
        def _():
            n_ref[...] = _bf(_rms_fwd(h_ref[...], g_ref[...]))

        o_ref[...] = _dot(n_ref[...], w_ref[...])

    return pl.pallas_call(
        body, grid=(m // tm, n // tn),
        in_specs=[pl.BlockSpec((tm, k), lambda i, j: (i, 0)), pl.BlockSpec((1, k), lambda i, j: (0, 0)),
                  pl.BlockSpec((k, tn), lambda i, j: (0, j))],
        out_specs=[pl.BlockSpec((tm, tn), lambda i, j: (i, j)), pl.BlockSpec((tm, k), lambda i, j: (i, 0))],
        out_shape=[jax.ShapeDtypeStruct((m, n), F32), jax.ShapeDtypeStruct((m, k), BF16)], name=name,
        compiler_params=_params("parallel", "arbitrary"))(h, gain, w)


def _mm_norm_swiglu(h, gain, wg, wu, *, name):
    m, k = h.shape
    n = wg.shape[1]
    tm, tn = _pick(m, ROWS_MID), _pick(n, (1408, 1024, 512, 256, 128))

    def body(h_ref, g_ref, wg_ref, wu_ref, gate_ref, up_ref, act_ref, n_ref):
        @pl.when(pl.program_id(1) == 0)
        def _():
            n_ref[...] = _bf(_rms_fwd(h_ref[...], g_ref[...]))

        gate = _dot(n_ref[...], wg_ref[...])
        up = _dot(n_ref[...], wu_ref[...])
        gate_ref[...] = gate
        up_ref[...] = up
        act_ref[...] = _bf(_silu(gate) * up)

    wspec = pl.BlockSpec((k, tn), lambda i, j: (0, j))
    ospec = pl.BlockSpec((tm, tn), lambda i, j: (i, j))
    return pl.pallas_call(
        body, grid=(m // tm, n // tn),
        in_specs=[pl.BlockSpec((tm, k), lambda i, j: (i, 0)), pl.BlockSpec((1, k), lambda i, j: (0, 0)), wspec, wspec],
        out_specs=[ospec, ospec, ospec, pl.BlockSpec((tm, k), lambda i, j: (i, 0))],
        out_shape=[jax.ShapeDtypeStruct((m, n), F32), jax.ShapeDtypeStruct((m, n), F32),
                   jax.ShapeDtypeStruct((m, n), BF16), jax.ShapeDtypeStruct((m, k), BF16)], name=name,
        compiler_params=_params("parallel", "arbitrary"))(h, gain, wg, wu)


def _mm_res(res, a, b, *, name):
    m, k = a.shape
    n = b.shape[1]
    tm, tn = _pick(m, ROWS_BIG), _pick(n, (1024, 512, 256, 128))

    def body(r_ref, a_ref, b_ref, o_ref):
        o_ref[...] = r_ref[...] + _dot(a_ref[...], b_ref[...])

    return pl.pallas_call(
        body, grid=(m // tm, n // tn),
        in_specs=[pl.BlockSpec((tm, tn), lambda i, j: (i, j)), pl.BlockSpec((tm, k), lambda i, j: (i, 0)),
                  pl.BlockSpec((k, tn), lambda i, j: (0, j))],
        out_specs=pl.BlockSpec((tm, tn), lambda i, j: (i, j)),
        out_shape=jax.ShapeDtypeStruct((m, n), F32), name=name,
        compiler_params=_params("parallel", "parallel"))(res, a, b)


def _mm_res_loss(res, a, b, target, *, row0, nrows, name):
    m, k = a.shape
    n = b.shape[1]
    tm = _pick(m, ROWS_MID)

    def body(r_ref, a_ref, b_ref, t_ref, dy_ref, dyb_ref, ls_ref):
        i = pl.program_id(0)

        @pl.when(i == 0)
        def _():
            ls_ref[...] = jnp.zeros_like(ls_ref)

        y = r_ref[...] + _dot(a_ref[...], b_ref[...])
        row = i * tm + _iota((tm, n), 0)
        e = jnp.where((row >= row0) & (row < row0 + nrows), y - t_ref[...], 0.0)
        dy = e / n
        dy_ref[...] = dy
        dyb_ref[...] = _bf(dy)
        ls_ref[...] += jnp.sum(e * e, axis=0, keepdims=True)

    rspec = pl.BlockSpec((tm, n), lambda i: (i, 0))
    return pl.pallas_call(
        body, grid=(m // tm,),
        in_specs=[rspec, pl.BlockSpec((tm, k), lambda i: (i, 0)), pl.BlockSpec((k, n), lambda i: (0, 0)), rspec],
        out_specs=[rspec, rspec, pl.BlockSpec((1, n), lambda i: (0, 0))],
        out_shape=[jax.ShapeDtypeStruct((m, n), F32), jax.ShapeDtypeStruct((m, n), BF16),
                   jax.ShapeDtypeStruct((1, n), F32)], name=name,
        compiler_params=_params("arbitrary"))(res, a, b, target)


def _merge_fwd(o_dn, o_sb, wbd, wbs, proj, *, col_gd, col_gs, name):
    m, kd = o_dn.shape
    ks = o_sb.shape[1]
    n = wbd.shape[1]
    tm = _pick(m, ROWS_BIG)
    tn = _pick(math.gcd(n, math.gcd(col_gd, col_gs)), (512, 256, 128))

    def body(od_ref, os_ref, wd_ref, ws_ref, gd_ref, gs_ref, mg_ref, bd_ref, bs_ref):
        bd = _dot(od_ref[...], wd_ref[...])
        bs = _dot(os_ref[...], ws_ref[...])
        bd_ref[...] = bd
        bs_ref[...] = bs
        mg_ref[...] = _bf(_sigmoid(gd_ref[...]) * bd + _sigmoid(gs_ref[...]) * bs)

    ospec = pl.BlockSpec((tm, tn), lambda i, j: (i, j))
    return pl.pallas_call(
        body, grid=(m // tm, n // tn),
        in_specs=[pl.BlockSpec((tm, kd), lambda i, j: (i, 0)), pl.BlockSpec((tm, ks), lambda i, j: (i, 0)),
                  pl.BlockSpec((kd, tn), lambda i, j: (0, j)), pl.BlockSpec((ks, tn), lambda i, j: (0, j)),
                  pl.BlockSpec((tm, tn), lambda i, j: (i, col_gd // tn + j)),
                  pl.BlockSpec((tm, tn), lambda i, j: (i, col_gs // tn + j))],
        out_specs=[ospec, ospec, ospec],
        out_shape=[jax.ShapeDtypeStruct((m, n), BF16), jax.ShapeDtypeStruct((m, n), F32),
                   jax.ShapeDtypeStruct((m, n), F32)], name=name,
        compiler_params=_params("parallel", "parallel"))(o_dn, o_sb, wbd, wbs, proj, proj)


def _merge_bwd(dh, w_out, proj, br_dn, br_sb, *, col_gd, col_gs, name):
    m, k = dh.shape
    n = w_out.shape[0]
    tm = _pick(m, ROWS_BIG)
    tn = _pick(math.gcd(n, math.gcd(col_gd, col_gs)), (512, 256, 128))

    def body(dh_ref, w_ref, gd_ref, gs_ref, bd_ref, bs_ref, dbd_ref, dbs_ref, dgd_ref, dgs_ref):
        dm = _dot_nt(dh_ref[...], w_ref[...])
        sd = _sigmoid(gd_ref[...])
        ss = _sigmoid(gs_ref[...])
        dbd_ref[...] = _bf(dm * sd)
        dbs_ref[...] = _bf(dm * ss)
        dgd_ref[...] = _bf(dm * bd_ref[...] * sd * (1.0 - sd))
        dgs_ref[...] = _bf(dm * bs_ref[...] * ss * (1.0 - ss))

    ospec = pl.BlockSpec((tm, tn), lambda i, j: (i, j))
    return pl.pallas_call(
        body, grid=(m // tm, n // tn),
        in_specs=[pl.BlockSpec((tm, k), lambda i, j: (i, 0)), pl.BlockSpec((tn, k), lambda i, j: (j, 0)),
                  pl.BlockSpec((tm, tn), lambda i, j: (i, col_gd // tn + j)),
                  pl.BlockSpec((tm, tn), lambda i, j: (i, col_gs // tn + j)), ospec, ospec],
        out_specs=[ospec] * 4,
        out_shape=[jax.ShapeDtypeStruct((m, n), BF16)] * 4, name=name,
        compiler_params=_params("parallel", "parallel"))(dh, w_out, proj, proj, br_dn, br_sb)


def _swiglu_bwd(dy, wfo, gate, up, *, name):
    m, k = dy.shape
    n = wfo.shape[0]
    tm, tn = _pick(m, ROWS_MID), _pick(n, (1408, 1024, 512, 256, 128))

    def body(dy_ref, w_ref, g_ref, u_ref, dg_ref, du_ref):
        da = _dot_nt(dy_ref[...], w_ref[...])
        g = g_ref[...]
        dg_ref[...] = _bf(da * u_ref[...] * _silu_grad(g))
        du_ref[...] = _bf(da * _silu(g))

    ospec = pl.BlockSpec((tm, tn), lambda i, j: (i, j))
    return pl.pallas_call(
        body, grid=(m // tm, n // tn),
        in_specs=[pl.BlockSpec((tm, k), lambda i, j: (i, 0)), pl.BlockSpec((tn, k), lambda i, j: (j, 0)), ospec, ospec],
        out_specs=[ospec, ospec], out_shape=[jax.ShapeDtypeStruct((m, n), BF16)] * 2, name=name,
        compiler_params=_params("parallel", "parallel"))(dy, wfo, gate, up)


def _mm_nt_rmsbwd(pairs, extra, h, gain, dres, *, name):
    m, k = pairs[0][0].shape
    n = h.shape[1]
    tm = _pick(m, ROWS_MID)
    tk = _pick(k, (1408, 1024, 512, 256, 128))
    nk = k // tk
    np_ = len(pairs)

    def body(*refs):
        ab = refs[:2 * np_]
        pos = 2 * np_
        ex = refs[pos:pos + 2] if extra is not None else ()
        pos += len(ex)
        h_ref, g_ref, r_ref, dh_ref, dhb_ref, dg_ref, acc_ref = refs[pos:]
        i, kk = pl.program_id(0), pl.program_id(1)

        @pl.when((i == 0) & (kk == 0))
        def _():
            dg_ref[...] = jnp.zeros_like(dg_ref)

        part = _dot_nt(ab[0][...], ab[1][...])
        for p in range(1, np_):
            part += _dot_nt(ab[2 * p][...], ab[2 * p + 1][...])

        @pl.when(kk == 0)
        def _():
            first = part
            if ex:
                first = first + _dot_nt(ex[0][...], ex[1][...])
            acc_ref[...] = first

        @pl.when(kk > 0)
        def _():
            acc_ref[...] += part

        @pl.when(kk == nk - 1)
        def _():
            dh, dgr = _rms_bwd(h_ref[...], g_ref[...], acc_ref[...])
            dh = dh + r_ref[...]
            dh_ref[...] = dh
            dhb_ref[...] = _bf(dh)
            dg_ref[...] += jnp.sum(dgr, axis=0, keepdims=True)

    in_specs, args = [], []
    for a, b in pairs:
        in_specs += [pl.BlockSpec((tm, tk), lambda i, kk: (i, kk)), pl.BlockSpec((n, tk), lambda i, kk: (0, kk))]
        args += [a, b]
    if extra is not None:
        k2 = extra[0].shape[1]
        in_specs += [pl.BlockSpec((tm, k2), lambda i, kk: (i, 0)), pl.BlockSpec((n, k2), lambda i, kk: (0, 0))]
        args += list(extra)
    rspec = pl.BlockSpec((tm, n), lambda i, kk: (i, 0))
    in_specs += [rspec, pl.BlockSpec((1, n), lambda i, kk: (0, 0)), rspec]
    return pl.pallas_call(
        body, grid=(m // tm, nk), in_specs=in_specs,
        out_specs=[rspec, rspec, pl.BlockSpec((1, n), lambda i, kk: (0, 0))],
        out_shape=[jax.ShapeDtypeStruct((m, n), F32), jax.ShapeDtypeStruct((m, n), BF16),
                   jax.ShapeDtypeStruct((1, n), F32)],
        scratch_shapes=[pltpu.VMEM((tm, n), F32)], name=name,
        compiler_params=_params("arbitrary", "arbitrary"))(*args, h, gain, dres)


def _conv_taps(cur, prev8, w_ref, first):
    nk = w_ref.shape[0]
    out = cur * w_ref[nk - 1:nk, :]
    for s in range(1, nk):
        out += _shift_down(cur, prev8, s, first) * w_ref[nk - 1 - s:nk - s, :]
    return out


def _shift_up(cur, next8, s, last):
    rows = cur.shape[0]
    row = _iota(cur.shape, 0)
    next8 = jnp.where(last, 0.0, next8)
    sh = pltpu.roll(cur, rows - s, axis=0)
    nh = jnp.tile(pltpu.roll(next8, 8 - s, axis=0), (rows // 8, 1))
    return jnp.where(row >= rows - s, nh, sh)


def _shift_down(cur, prev8, s, first):
    rows = cur.shape[0]
    row = _iota(cur.shape, 0)
    prev8 = jnp.where(first, 0.0, prev8)
    sh = pltpu.roll(cur, s, axis=0)
    ph = jnp.tile(pltpu.roll(prev8, s, axis=0), (rows // 8, 1))
    return jnp.where(row < s, ph, sh)


def _gdn_pre(proj, pab, cq, ck, cv, a_log, dt_bias, *, heads, dk, dv, col_q, col_k, col_v, row_lo, row_hi, name):
    t = proj.shape[0]
    tm = _pick(t, (384, 256, 128))
    nb = t // tm

    def body(pq_ref, pqp_ref, pk_ref, pkp_ref, pv_ref, pvp_ref, ab_ref, cq_ref, ck_ref, cv_ref, al_ref, dt_ref,
             qn_ref, kn_ref, v_ref, g_ref, b_ref):
        h, i = pl.program_id(0), pl.program_id(1)
        first = i == 0
        row = i * tm + _iota((tm, 1), 0)
        valid = (row >= row_lo) & (row < row_hi)
        q1 = _silu(_conv_taps(pq_ref[...], pqp_ref[...], cq_ref, first))
        k1 = _silu(_conv_taps(pk_ref[...], pkp_ref[...], ck_ref, first))
        v1 = _silu(_conv_taps(pv_ref[...], pvp_ref[...], cv_ref, first))
        qn_ref[...] = jnp.where(valid, q1 * lax.rsqrt(jnp.sum(q1 * q1, axis=-1, keepdims=True) + L2_EPS), 0.0)
        kn_ref[...] = jnp.where(valid, k1 * lax.rsqrt(jnp.sum(k1 * k1, axis=-1, keepdims=True) + L2_EPS), 0.0)
        v_ref[...] = jnp.where(valid, v1, 0.0)
        ab = ab_ref[...]
        da = _lane_pick(ab, h)
        db = _lane_pick(ab, heads + h)
        a = _lane_pick(al_ref[...], h)
        dtb = _lane_pick(dt_ref[...], h)
        g_ref[...] = jnp.where(valid, -jnp.exp(a) * _softplus(da + dtb), 0.0)
        b_ref[...] = jnp.where(valid, _sigmoid(db), 0.0)

    def cur(width, col):
        return pl.BlockSpec((tm, width), lambda h, i: (i, col // width + h))

    def prev(width, col):
        return pl.BlockSpec((8, width), lambda h, i: (jnp.maximum(i * (tm // 8) - 1, 0), col // width + h))

    def out(width):
        return pl.BlockSpec((None, tm, width), lambda h, i: (h, i, 0))

    small = pl.BlockSpec((1, LANES), lambda h, i: (0, 0))
    return pl.pallas_call(
        body, grid=(heads, nb),
        in_specs=[cur(dk, col_q), prev(dk, col_q), cur(dk, col_k), prev(dk, col_k), cur(dv, col_v), prev(dv, col_v),
                  pl.BlockSpec((tm, LANES), lambda h, i: (i, 0)),
                  pl.BlockSpec((cq.shape[0], dk), lambda h, i: (0, h)), pl.BlockSpec((ck.shape[0], dk), lambda h, i: (0, h)),
                  pl.BlockSpec((cv.shape[0], dv), lambda h, i: (0, h)), small, small],
        out_specs=[out(dk), out(dk), out(dv), out(1), out(1)],
        out_shape=[jax.ShapeDtypeStruct((heads, t, dk), F32), jax.ShapeDtypeStruct((heads, t, dk), F32),
                   jax.ShapeDtypeStruct((heads, t, dv), F32), jax.ShapeDtypeStruct((heads, t, 1), F32),
                   jax.ShapeDtypeStruct((heads, t, 1), F32)], name=name,
        compiler_params=_params("parallel", "parallel"))(proj, proj, proj, proj, proj, proj, pab, cq, ck, cv, a_log, dt_bias)


def _chunk_masks(rows=GDN_ROWS, row0=0):
    ri = row0 + _iota((rows, GDN_ROWS), 0)
    ci = _iota((rows, GDN_ROWS), 1)
    same = jnp.right_shift(ri, CHUNK_SHIFT) == jnp.right_shift(ci, CHUNK_SHIFT)
    return same, same & (ri >= ci), same & (ri > ci), ri == ci


def _col_to_row(col, eye):
    return jnp.sum(jnp.where(eye, col, 0.0), axis=0, keepdims=True)


def _row_to_col(row, eye):
    return jnp.sum(jnp.where(eye, row, 0.0), axis=1, keepdims=True)


def _chunk_common(qn, kn, g, beta, dk_scale):
    same, incl, strict, eye = _chunk_masks()
    gb = jnp.broadcast_to(g, (GDN_ROWS, LANES))
    gam = jnp.max(_dot_exact_l(jnp.where(incl, 1.0, 0.0).astype(BF16), gb), axis=1, keepdims=True)
    gam_last = jnp.max(_dot_exact_l(jnp.where(same, 1.0, 0.0).astype(BF16), gb), axis=1, keepdims=True)
    diff = gam - _col_to_row(gam, eye)
    decay = jnp.where(incl, jnp.exp(jnp.where(incl, diff, 0.0)), 0.0)
    eg = jnp.exp(gam)
    ek = jnp.exp(gam_last - gam)
    kb = kn * beta
    qt = qn * dk_scale
    lmat = jnp.where(strict, _dot_nt(_bf(kb), _bf(kn)) * decay, 0.0)
    pmat = jnp.where(incl, _dot_nt(_bf(qt), _bf(kn)) * decay, 0.0)
    return dict(incl=incl, strict=strict, eye=eye, decay=decay, eg=eg, ek=ek, egl=jnp.exp(gam_last),
                kb=kb, qt=qt, lmat=lmat, pmat=pmat)


def _gdn_prep(qn, kn, v, g, beta, *, name, gather=None):
    heads, t, dk = qn.shape
    dv = v.shape[2]
    rows = _pick(t, (3 * GDN_ROWS, 2 * GDN_ROWS, GDN_ROWS))
    dk_scale = dk ** -0.5

    def body(q_ref, k_ref, v_ref, g_ref, b_ref, u_ref, w_ref, p_ref, qd_ref, kd_ref, egl_ref, t_ref):
        for b in range(rows // GDN_ROWS):
            r = pl.ds(b * GDN_ROWS, GDN_ROWS)
            kn_, beta_ = k_ref[r, :], b_ref[r, :]
            c = _chunk_common(q_ref[r, :], kn_, g_ref[r, :], beta_, dk_scale)
            eye_f = jnp.where(c["eye"], 1.0, 0.0)
            x = c["lmat"]
            tinv = eye_f - x
            for _ in range(CHUNK_SHIFT - 1):
                x = _dot_hp(x, x)
                tinv = tinv + _dot_hp(tinv, x)
            u_ref[r, :] = _dot_hp(tinv, v_ref[r, :] * beta_)
            w_ref[r, :] = _dot_hp(tinv, c["kb"] * c["eg"])
            p_ref[r, :] = c["pmat"]
            qd_ref[r, :] = c["qt"] * c["eg"]
            kd_ref[r, :] = kn_ * c["ek"]
            egl_ref[r, :] = c["egl"]
            t_ref[r, :] = tinv

    def blk(width):
        return pl.BlockSpec((None, rows, width), lambda h, i: (h, i, 0))

    def shp(width):
        return jax.ShapeDtypeStruct((heads, t, width), F32)

    call = dict(grid=(heads, t // rows), in_specs=[blk(dk), blk(dk), blk(dv), blk(1), blk(1)],
                out_specs=[blk(dv), blk(dk), blk(GDN_ROWS), blk(dk), blk(dk), blk(1), blk(GDN_ROWS)],
                out_shape=[shp(dv), shp(dk), shp(GDN_ROWS), shp(dk), shp(dk), shp(1), shp(GDN_ROWS)], name=name)
    if gather is None:
        return pl.pallas_call(body, compiler_params=_params("parallel", "parallel"), **call)(qn, kn, v, g, beta)
    res = _call_with_exchange(body, scratch_shapes=[], args=(qn, kn, v, g, beta), srcs=gather,
                              scatter=[False] * len(gather), **call)
    return [*res[:7], list(res[7:])]


def _gdn_scan(u, w, p, qd, kd, egl, proj, gain, *, col_z, name):
    heads, t, dv = u.shape
    dk = w.shape[2]
    nb = t // GDN_ROWS
    sub = GDN_ROWS // CHUNK
    hp = SCAN_HEADS

    def body(u_ref, w_ref, p_ref, qd_ref, kd_ref, egl_ref, z_ref, gn_ref, o_ref, og_ref, st_ref, s_ref):
        @pl.when(pl.program_id(1) == 0)
        def _():
            s_ref[...] = jnp.zeros_like(s_ref)

        for hh in range(hp):
            cols = pl.ds(hh * dv, dv)
            vn_parts = [jnp.zeros((CHUNK, dv), F32)] * sub
            for c in range(sub):
                r = pl.ds(c * CHUNK, CHUNK)
                s = s_ref[hh]
                st_ref[hh, c] = s
                sb = _bf(s)
                vn = u_ref[hh, r, :] - _dot(_bf(w_ref[hh, r, :]), sb)
                vn_parts[c] = vn
                vfull = _bf(jnp.concatenate(vn_parts, axis=0))
                o = _dot(_bf(qd_ref[hh, r, :]), sb) + _dot(_bf(p_ref[hh, r, :]), vfull)
                gl = egl_ref[hh, pl.ds(c * CHUNK, 1), :]
                s_ref[hh] = s * gl + _dot_tn(_bf(kd_ref[hh, r, :]), _bf(vn))
                o_ref[hh, r, :] = o
                og_ref[r, cols] = _bf(_rms_fwd(o, gn_ref[...]) * _silu(z_ref[r, cols]))

    def blk(width):
        return pl.BlockSpec((hp, GDN_ROWS, width), lambda h, i: (h, i, 0))

    return pl.pallas_call(
        body, grid=(heads // hp, nb),
        in_specs=[blk(dv), blk(dk), blk(GDN_ROWS), blk(dk), blk(dk), blk(1),
                  pl.BlockSpec((GDN_ROWS, hp * dv), lambda h, i: (i, col_z // (hp * dv) + h)),
                  pl.BlockSpec((1, dv), lambda h, i: (0, 0))],
        out_specs=[blk(dv), pl.BlockSpec((GDN_ROWS, hp * dv), lambda h, i: (i, h)),
                   pl.BlockSpec((hp, sub, dk, dv), lambda h, i: (h, i, 0, 0))],
        out_shape=[jax.ShapeDtypeStruct((heads, t, dv), F32), jax.ShapeDtypeStruct((t, heads * dv), BF16),
                   jax.ShapeDtypeStruct((heads, t // CHUNK, dk, dv), F32)],
        scratch_shapes=[pltpu.VMEM((hp, dk, dv), F32)], name=name,
        compiler_params=_params("parallel", "arbitrary"))(u, w, p, qd, kd, egl, proj, gain)


def _gdn_post_bwd(o, proj, gain, dout, *, col_z, name):
    heads, t, dv = o.shape
    tm = _pick(t, (384, 256, 128))

    def body(o_ref, z_ref, gn_ref, d_ref, do_ref, dz_ref, dg_ref):
        @pl.when((pl.program_id(0) == 0) & (pl.program_id(1) == 0))
        def _():
            dg_ref[...] = jnp.zeros_like(dg_ref)

        o_, z, d = o_ref[...], z_ref[...], d_ref[...]
        y = _rms_fwd(o_, gn_ref[...])
        dz_ref[...] = _bf(d * y * _silu_grad(z))
        do, dgr = _rms_bwd(o_, gn_ref[...], d * _silu(z))
        do_ref[...] = do
        dg_ref[...] += jnp.sum(dgr, axis=0, keepdims=True)

    return pl.pallas_call(
        body, grid=(t // tm, heads),
        in_specs=[pl.BlockSpec((None, tm, dv), lambda i, h: (h, i, 0)),
                  pl.BlockSpec((tm, dv), lambda i, h: (i, col_z // dv + h)),
                  pl.BlockSpec((1, dv), lambda i, h: (0, 0)), pl.BlockSpec((tm, dv), lambda i, h: (i, h))],
        out_specs=[pl.BlockSpec((None, tm, dv), lambda i, h: (h, i, 0)), pl.BlockSpec((tm, dv), lambda i, h: (i, h)),
                   pl.BlockSpec((1, dv), lambda i, h: (0, 0))],
        out_shape=[jax.ShapeDtypeStruct((heads, t, dv), F32), jax.ShapeDtypeStruct((t, heads * dv), BF16),
                   jax.ShapeDtypeStruct((1, dv), F32)], name=name,
        compiler_params=_params("arbitrary", "arbitrary"))(o, proj, gain, dout)


def _gdn_bwd_scan(u, w, p, qd, kd, egl, st, do, *, name):
    heads, t, dv = u.shape
    dk = w.shape[2]
    nb = t // GDN_ROWS
    sub = GDN_ROWS // CHUNK
    hp = SCAN_HEADS

    def body(u_ref, w_ref, p_ref, qd_ref, kd_ref, egl_ref, st_ref, do_ref,
             du_ref, dw_ref, dp_ref, dqd_ref, dkd_ref, dgl_ref, ds_ref):
        @pl.when(pl.program_id(1) == 0)
        def _():
            ds_ref[...] = jnp.zeros_like(ds_ref)

        for hh in range(hp):
            for c in reversed(range(sub)):
                r = pl.ds(c * CHUNK, CHUNK)
                s = st_ref[hh, c]
                sb = _bf(s)
                ds = ds_ref[hh]
                dsb = _bf(ds)
                dob = _bf(do_ref[hh, r, :])
                wb, pb = _bf(w_ref[hh, r, :]), _bf(p_ref[hh, r, :])
                qdb, kdb = _bf(qd_ref[hh, r, :]), _bf(kd_ref[hh, r, :])
                vn = u_ref[hh, r, :] - _dot(wb, sb)
                zeros = jnp.zeros((CHUNK, dv), BF16)
                vfull = jnp.concatenate([_bf(vn) if cc == c else zeros for cc in range(sub)], axis=0)
                dvn = _dot_tn(pb, dob)[c * CHUNK:(c + 1) * CHUNK, :] + _dot(kdb, dsb)
                dvb = _bf(dvn)
                gl = egl_ref[hh, pl.ds(c * CHUNK, 1), :]
                du_ref[hh, r, :] = dvn
                dw_ref[hh, r, :] = -_dot_nt(dvb, sb)
                dp_ref[hh, r, :] = jnp.where(_chunk_masks(CHUNK, c * CHUNK)[1], _dot_nt(dob, vfull), 0.0)
                dqd_ref[hh, r, :] = _dot_nt(dob, sb)
                dkd_ref[hh, r, :] = _dot_nt(_bf(vn), dsb)
                dgl = jnp.sum(jnp.sum(ds * s, axis=1, keepdims=True), axis=0, keepdims=True)
                dgl_ref[hh, r, :] = jnp.where(_iota((CHUNK, 1), 0) == CHUNK - 1, dgl, 0.0)
                ds_ref[hh] = ds * gl + _dot_tn(qdb, dob) - _dot_tn(wb, dvb)

    def blk(width):
        return pl.BlockSpec((hp, GDN_ROWS, width), lambda h, i: (h, nb - 1 - i, 0))

    def shp(width):
        return jax.ShapeDtypeStruct((heads, t, width), F32)

    return pl.pallas_call(
        body, grid=(heads // hp, nb),
        in_specs=[blk(dv), blk(dk), blk(GDN_ROWS), blk(dk), blk(dk), blk(1),
                  pl.BlockSpec((hp, sub, dk, dv), lambda h, i: (h, nb - 1 - i, 0, 0)), blk(dv)],
        out_specs=[blk(dv), blk(dk), blk(GDN_ROWS), blk(dk), blk(dk), blk(1)],
        out_shape=[shp(dv), shp(dk), shp(GDN_ROWS), shp(dk), shp(dk), shp(1)],
        scratch_shapes=[pltpu.VMEM((hp, dk, dv), F32)], name=name,
        compiler_params=_params("parallel", "arbitrary"))(u, w, p, qd, kd, egl, st, do)


def _gdn_bwd_prep(qn, kn, v, g, beta, tinv, u, w, du, dw, dp, dqd, dkd, dgl, *, name):
    heads, t, dk = qn.shape
    dv = v.shape[2]
    rows = _pick(t, (3 * GDN_ROWS, 2 * GDN_ROWS, GDN_ROWS))
    dk_scale = dk ** -0.5

    def rowsum(x):
        return jnp.sum(x, axis=1, keepdims=True)

    def body(q_ref, k_ref, v_ref, g_ref, b_ref, t_ref, u_ref, w_ref, du_ref, dw_ref, dp_ref, dqd_ref, dkd_ref, dgl_ref,
             dq_ref, dkk_ref, dvv_ref, dg_ref, db_ref):
        for b in range(rows // GDN_ROWS):
            r = pl.ds(b * GDN_ROWS, GDN_ROWS)
            kn_, beta_, v_ = k_ref[r, :], b_ref[r, :], v_ref[r, :]
            c = _chunk_common(q_ref[r, :], kn_, g_ref[r, :], beta_, dk_scale)
            eye, strict, decay = c["eye"], c["strict"], c["decay"]
            kb, qt, eg, ek = c["kb"], c["qt"], c["eg"], c["ek"]
            tinv_ = t_ref[r, :]
            dbv = _dot_hp(tinv_, du_ref[r, :], _dot_tn)
            dbw = _dot_hp(tinv_, dw_ref[r, :], _dot_tn)
            da = -(_dot_nt(_bf(dbv), _bf(u_ref[r, :])) + _dot_nt(_bf(dbw), _bf(w_ref[r, :])))
            dl = jnp.where(strict, da, 0.0)
            dp_ = dp_ref[r, :]
            dm = _bf(dl * decay)
            dn = _bf(dp_ * decay)
            knb = _bf(kn_)
            dkb = _dot(dm, knb) + dbw * eg
            dkn = _dot_tn(dm, _bf(kb)) + _dot_tn(dn, _bf(qt))
            dqt = _dot(dn, knb)
            gmat = dl * c["lmat"] + dp_ * c["pmat"]
            dqd_, dkd_ = dqd_ref[r, :], dkd_ref[r, :]
            qd = qt * eg
            kd = kn_ * ek
            bw = kb * eg
            kdsum = rowsum(dkd_ * kd)
            dgam = rowsum(gmat) - _row_to_col(jnp.sum(gmat, axis=0, keepdims=True), eye)
            dgam += rowsum(dbw * bw) + rowsum(dqd_ * qd) - kdsum
            last = (_iota((GDN_ROWS, 1), 0) & (CHUNK - 1)) == CHUNK - 1
            same = _chunk_masks()[0]
            same_f = jnp.where(same, 1.0, 0.0).astype(BF16)
            chunk_tot = jnp.max(_dot_exact_l(same_f, jnp.broadcast_to(kdsum, (GDN_ROWS, LANES))), axis=1, keepdims=True)
            dgam += jnp.where(last, chunk_tot, 0.0) + dgl_ref[r, :] * c["egl"]
            dq_ref[r, :] = (dqt + dqd_ * eg) * dk_scale
            dkk_ref[r, :] = dkn + dkd_ * ek + dkb * beta_
            dvv_ref[r, :] = dbv * beta_
            db_ref[r, :] = rowsum(dbv * v_) + rowsum(dkb * kn_)
            upper = jnp.where(same & (_iota((GDN_ROWS, GDN_ROWS), 0) <= _iota((GDN_ROWS, GDN_ROWS), 1)), 1.0, 0.0)
            dgb = _dot_exact_l(upper.astype(BF16), jnp.broadcast_to(dgam, (GDN_ROWS, LANES)))
            dg_ref[r, :] = _lane_pick(dgb, 0)

    def blk(width):
        return pl.BlockSpec((None, rows, width), lambda h, i: (h, i, 0))

    def shp(width):
        return jax.ShapeDtypeStruct((heads, t, width), F32)

    return pl.pallas_call(
        body, grid=(heads, t // rows),
        in_specs=[blk(dk), blk(dk), blk(dv), blk(1), blk(1), blk(GDN_ROWS), blk(dv), blk(dk),
                  blk(dv), blk(dk), blk(GDN_ROWS), blk(dk), blk(dk), blk(1)],
        out_specs=[blk(dk), blk(dk), blk(dv), blk(1), blk(1)],
        out_shape=[shp(dk), shp(dk), shp(dv), shp(1), shp(1)], name=name,
        compiler_params=_params("parallel", "parallel"))(qn, kn, v, g, beta, tinv, u, w, du, dw, dp, dqd, dkd, dgl)


def _gdn_pre_bwd_a(proj, pab, cq, ck, cv, a_log, dt_bias, dqn, dkn, dvv, dg, dbeta, *,
                   heads, dk, dv, col_q, col_k, col_v, row_lo, row_hi, name):
    t = proj.shape[0]
    tm = _pick(t, (384, 256, 128))
    nb = t // tm

    def body(pq_ref, pqp_ref, pk_ref, pkp_ref, pv_ref, pvp_ref, ab_ref, cq_ref, ck_ref, cv_ref, al_ref, dt_ref,
             dqn_ref, dkn_ref, dvv_ref, dg_ref, db_ref, dcq_ref, dck_ref, dcv_ref, dab_ref, dal_ref, ddt_ref):
        i, h = pl.program_id(0), pl.program_id(1)
        first = i == 0

        @pl.when((i == 0) & (h == 0))
        def _():
            dal_ref[...] = jnp.zeros_like(dal_ref)
            ddt_ref[...] = jnp.zeros_like(ddt_ref)

        @pl.when(h == 0)
        def _():
            dab_ref[...] = jnp.zeros_like(dab_ref)

        row = i * tm + _iota((tm, 1), 0)
        valid = (row >= row_lo) & (row < row_hi)

        def l2_bwd(c1, dn):
            x1 = _silu(c1)
            r = lax.rsqrt(jnp.sum(x1 * x1, axis=-1, keepdims=True) + L2_EPS)
            dn = jnp.where(valid, dn, 0.0)
            d1 = r * dn - x1 * (r * r * r) * jnp.sum(dn * x1, axis=-1, keepdims=True)
            return d1 * _silu_grad(c1)

        dcq_ref[...] = l2_bwd(_conv_taps(pq_ref[...], pqp_ref[...], cq_ref, first), dqn_ref[...])
        dck_ref[...] = l2_bwd(_conv_taps(pk_ref[...], pkp_ref[...], ck_ref, first), dkn_ref[...])
        cv1 = _conv_taps(pv_ref[...], pvp_ref[...], cv_ref, first)
        dcv_ref[...] = jnp.where(valid, dvv_ref[...], 0.0) * _silu_grad(cv1)
        ab = ab_ref[...]
        da = _lane_pick(ab, h)
        db = _lane_pick(ab, heads + h)
        a = _lane_pick(al_ref[...], h)
        dtb = _lane_pick(dt_ref[...], h)
        dgv = jnp.where(valid, dg_ref[...], 0.0)
        ea = jnp.exp(a)
        g = -ea * _softplus(da + dtb)
        dda = dgv * (-ea) * _sigmoid(da + dtb)
        beta = _sigmoid(db)
        ddb = jnp.where(valid, db_ref[...], 0.0) * beta * (1.0 - beta)
        lane = _iota((tm, LANES), 1)
        dab_ref[...] += jnp.where(lane == h, dda, 0.0) + jnp.where(lane == heads + h, ddb, 0.0)
        lane1 = _iota((1, LANES), 1)
        dal_ref[...] += jnp.where(lane1 == h, jnp.sum(dgv * g, axis=0, keepdims=True), 0.0)
        ddt_ref[...] += jnp.where(lane1 == h, jnp.sum(dda, axis=0, keepdims=True), 0.0)

    def cur(width, col):
        return pl.BlockSpec((tm, width), lambda i, h: (i, col // width + h))

    def prev(width, col):
        return pl.BlockSpec((8, width), lambda i, h: (jnp.maximum(i * (tm // 8) - 1, 0), col // width + h))

    def hd(width):
        return pl.BlockSpec((None, tm, width), lambda i, h: (h, i, 0))

    small = pl.BlockSpec((1, LANES), lambda i, h: (0, 0))
    return pl.pallas_call(
        body, grid=(nb, heads),
        in_specs=[cur(dk, col_q), prev(dk, col_q), cur(dk, col_k), prev(dk, col_k), cur(dv, col_v), prev(dv, col_v),
                  pl.BlockSpec((tm, LANES), lambda i, h: (i, 0)),
                  pl.BlockSpec((cq.shape[0], dk), lambda i, h: (0, h)), pl.BlockSpec((ck.shape[0], dk), lambda i, h: (0, h)),
                  pl.BlockSpec((cv.shape[0], dv), lambda i, h: (0, h)), small, small,
                  hd(dk), hd(dk), hd(dv), hd(1), hd(1)],
        out_specs=[hd(dk), hd(dk), hd(dv), pl.BlockSpec((tm, LANES), lambda i, h: (i, 0)), small, small],
        out_shape=[jax.ShapeDtypeStruct((heads, t, dk), F32), jax.ShapeDtypeStruct((heads, t, dk), F32),
                   jax.ShapeDtypeStruct((heads, t, dv), F32), jax.ShapeDtypeStruct((t, LANES), F32),
                   jax.ShapeDtypeStruct((1, LANES), F32), jax.ShapeDtypeStruct((1, LANES), F32)], name=name,
        compiler_params=_params("arbitrary", "arbitrary"))(
            proj, proj, proj, proj, proj, proj, pab, cq, ck, cv, a_log, dt_bias, dqn, dkn, dvv, dg, dbeta)


def _conv_bwd(proj, dc, cw, *, heads, width, col, name):
    t = proj.shape[0]
    tm = _pick(t, (384, 256, 128))
    nb = t // tm
    nk = cw.shape[0]

    def body(p_ref, pp_ref, d_ref, dn_ref, w_ref, dp_ref, dw_ref):
        i = pl.program_id(1)
        first, last = i == 0, i == nb - 1

        @pl.when(first)
        def _():
            dw_ref[...] = jnp.zeros_like(dw_ref)

        x, d = p_ref[...], d_ref[...]
        dx = d * w_ref[nk - 1:nk, :]
        dw_ref[nk - 1:nk, :] += jnp.sum(d * x, axis=0, keepdims=True)
        for s in range(1, nk):
            dx += _shift_up(d, dn_ref[...], s, last) * w_ref[nk - 1 - s:nk - s, :]
            dw_ref[nk - 1 - s:nk - s, :] += jnp.sum(d * _shift_down(x, pp_ref[...], s, first), axis=0, keepdims=True)
        dp_ref[...] = _bf(dx)

    return pl.pallas_call(
        body, grid=(heads, nb),
        in_specs=[pl.BlockSpec((tm, width), lambda h, i: (i, col // width + h)),
                  pl.BlockSpec((8, width), lambda h, i: (jnp.maximum(i * (tm // 8) - 1, 0), col // width + h)),
                  pl.BlockSpec((None, tm, width), lambda h, i: (h, i, 0)),
                  pl.BlockSpec((None, 8, width), lambda h, i: (h, jnp.minimum((i + 1) * (tm // 8), t // 8 - 1), 0)),
                  pl.BlockSpec((nk, width), lambda h, i: (0, h))],
        out_specs=[pl.BlockSpec((tm, width), lambda h, i: (i, h)), pl.BlockSpec((nk, width), lambda h, i: (0, h))],
        out_shape=[jax.ShapeDtypeStruct((t, heads * width), BF16), jax.ShapeDtypeStruct((nk, heads * width), F32)],
        name=name, compiler_params=_params("parallel", "arbitrary"))(proj, proj, dc, dc, cw)


def _sb_pre(proj, gq, gk, *, heads, dh, col_q, col_k, col_v, name):
    t = proj.shape[0]
    tm = _pick(t, (384, 256, 128))

    def body(q_ref, k_ref, v_ref, gq_ref, gk_ref, qo_ref, ko_ref, vo_ref):
        qo_ref[...] = _bf(_rms_fwd(q_ref[...], gq_ref[...]))
        ko_ref[...] = _bf(_rms_fwd(k_ref[...], gk_ref[...]))
        vo_ref[...] = _bf(v_ref[...])

    def cur(col):
        return pl.BlockSpec((tm, dh), lambda i, h: (i, col // dh + h))

    gspec = pl.BlockSpec((1, dh), lambda i, h: (0, 0))
    ospec = pl.BlockSpec((tm, dh), lambda i, h: (i, h))
    return pl.pallas_call(
        body, grid=(t // tm, heads), in_specs=[cur(col_q), cur(col_k), cur(col_v), gspec, gspec],
        out_specs=[ospec] * 3, out_shape=[jax.ShapeDtypeStruct((t, heads * dh), BF16)] * 3, name=name,
        compiler_params=_params("parallel", "parallel"))(proj, proj, proj, gq, gk)


def _sb_tile(q, kb, i, j, blk, key_lo, scale):
    z = _dot_nt(q, kb) * scale
    qpos = i * blk + _iota((blk, blk), 0)
    kpos = j * blk + _iota((blk, blk), 1)
    vis = (kpos < qpos) & (kpos >= key_lo)
    ls = jnp.minimum(z, 0.0) - jnp.log(1.0 + jnp.exp(-jnp.abs(z)))
    return vis, ls, jnp.where(vis, ls - z, 0.0)


def _dot2_r(x, m):
    hi, lo = _split2(x)
    return _dot(hi, m) + _dot(lo, m)


def _suffix_sums(x, later):
    return jnp.concatenate([_dot2_r(x[:, s:], later[s:, s:s + LANES]) for s in range(0, x.shape[1], LANES)], axis=1)


def _prefix_sums(x, earlier):
    return jnp.concatenate([_dot2_r(x[:, :s + LANES], earlier[:s + LANES, s:s + LANES])
                            for s in range(0, x.shape[1], LANES)], axis=1)


def _sb_fwd(qs, ks, vs, *, heads, dh, key_lo, name):
    t = qs.shape[0]
    blk = _pick(t, (3 * SB_BLOCK, 2 * SB_BLOCK, SB_BLOCK))
    nq = t // blk
    assert nq <= LANES
    scale = dh ** -0.5
    hp = SB_HEADS_PER_STEP

    def body(q_ref, k_ref, v_ref, o_ref, c_ref):
        i = pl.program_id(1)
        later = jnp.where(_iota((blk, blk), 0) > _iota((blk, blk), 1), 1.0, 0.0).astype(BF16)
        lane = _iota((blk, LANES), 1)
        c_ref[...] = jnp.zeros_like(c_ref)

        def step(n, carry):
            j = i - n
            rows = pl.ds(pl.multiple_of(j * blk, blk), blk)
            out = []
            for hh in range(hp):
                cols = pl.ds(hh * dh, dh)
                acc, run = carry[2 * hh], carry[2 * hh + 1]
                vis, ls, lk = _sb_tile(q_ref[:, cols], k_ref[rows, cols], i, j, blk, key_lo, scale)
                wgt = jnp.where(vis, jnp.exp(ls + _suffix_sums(lk, later) + run), 0.0)
                c_ref[hh] = jnp.where(lane == j, run, c_ref[hh])
                out += [acc + _dot(_bf(wgt), v_ref[rows, cols]), run + jnp.sum(lk, axis=1, keepdims=True)]
            return tuple(out)

        res = lax.fori_loop(0, i + 1, step, (jnp.zeros((blk, dh), F32), jnp.zeros((blk, 1), F32)) * hp)
        for hh in range(hp):
            o_ref[:, pl.ds(hh * dh, dh)] = _bf(res[2 * hh])

    full = pl.BlockSpec((t, hp * dh), lambda h, i: (0, h))
    return pl.pallas_call(
        body, grid=(heads // hp, nq),
        in_specs=[pl.BlockSpec((blk, hp * dh), lambda h, i: (i, h)), full, full],
        out_specs=[pl.BlockSpec((blk, hp * dh), lambda h, i: (i, h)),
                   pl.BlockSpec((hp, blk, LANES), lambda h, i: (h, i, 0))],
        out_shape=[jax.ShapeDtypeStruct((t, heads * dh), BF16), jax.ShapeDtypeStruct((heads, t, LANES), F32)],
        name=name, compiler_params=_params("parallel", "parallel"))(qs, ks, vs)


def _sb_bwd(qs, ks, vs, do, carry, *, heads, dh, key_lo, name, scatter=None):
    t = qs.shape[0]
    blk = _pick(t, (3 * SB_BLOCK, 2 * SB_BLOCK, SB_BLOCK))
    nq = t // blk
    scale = dh ** -0.5
    hp = SB_HEADS_PER_STEP

    def body(q_ref, k_ref, v_ref, do_ref, c_ref, dq_ref, dk_ref, dv_ref):
        i = pl.program_id(1)

        @pl.when(i == 0)
        def _():
            dk_ref[...] = jnp.zeros_like(dk_ref)
            dv_ref[...] = jnp.zeros_like(dv_ref)

        r0 = _iota((blk, blk), 0)
        r1 = _iota((blk, blk), 1)
        later = jnp.where(r0 > r1, 1.0, 0.0).astype(BF16)
        earlier = jnp.where(r0 < r1, 1.0, 0.0).astype(BF16)

        def step(j, carry):
            rows = pl.ds(pl.multiple_of(j * blk, blk), blk)
            out = []
            for hh in range(hp):
                cols = pl.ds(hh * dh, dh)
                dq, pre = carry[2 * hh], carry[2 * hh + 1]
                q, dob, kb, vb = q_ref[:, cols], do_ref[:, cols], k_ref[rows, cols], v_ref[rows, cols]
                vis, ls, lk = _sb_tile(q, kb, i, j, blk, key_lo, scale)
                wgt = jnp.where(vis, jnp.exp(ls + _suffix_sums(lk, later) + _lane_pick(c_ref[hh], j)), 0.0)
                e = wgt * _dot_nt(dob, vb)
                before = jnp.where(vis, _prefix_sums(e, earlier) + pre, 0.0)
                sig = jnp.exp(ls)
                dz = _bf((e * (1.0 - sig) - before * sig) * scale)
                dk_ref[rows, cols] += _dot_tn(dz, q)
                dv_ref[rows, cols] += _dot_tn(_bf(wgt), dob)
                out += [dq + _dot(dz, kb), pre + jnp.sum(e, axis=1, keepdims=True)]
            return tuple(out)

        res = lax.fori_loop(0, i + 1, step, (jnp.zeros((blk, dh), F32), jnp.zeros((blk, 1), F32)) * hp)
        for hh in range(hp):
            dq_ref[:, pl.ds(hh * dh, dh)] = res[2 * hh]

    full = pl.BlockSpec((t, hp * dh), lambda h, i: (0, h))
    qblk = pl.BlockSpec((blk, hp * dh), lambda h, i: (i, h))
    call = dict(grid=(heads // hp, nq),
                in_specs=[qblk, full, full, qblk, pl.BlockSpec((hp, blk, LANES), lambda h, i: (h, i, 0))],
                out_specs=[qblk, full, full], out_shape=[jax.ShapeDtypeStruct((t, heads * dh), F32)] * 3, name=name)
    if scatter is None:
        return pl.pallas_call(body, compiler_params=_params("parallel", "arbitrary"), **call)(qs, ks, vs, do, carry)
    res = _call_with_exchange(body, scratch_shapes=[], args=(qs, ks, vs, do, carry), srcs=scatter,
                              scatter=[True] * len(scatter), **call)
    return [*res[:3], list(res[3:])]


def _sb_pre_bwd(proj, gq, gk, dq, dk, dv, *, heads, dh, col_q, col_k, name):
    t = proj.shape[0]
    tm = _pick(t, (384, 256, 128))

    def body(q_ref, k_ref, gq_ref, gk_ref, dq_ref, dk_ref, dv_ref, oq_ref, ok_ref, ov_ref, dgq_ref, dgk_ref):
        @pl.when((pl.program_id(0) == 0) & (pl.program_id(1) == 0))
        def _():
            dgq_ref[...] = jnp.zeros_like(dgq_ref)
            dgk_ref[...] = jnp.zeros_like(dgk_ref)

        dq_, gq_r = _rms_bwd(q_ref[...], gq_ref[...], dq_ref[...])
        dk_, gk_r = _rms_bwd(k_ref[...], gk_ref[...], dk_ref[...])
        oq_ref[...] = _bf(dq_)
        ok_ref[...] = _bf(dk_)
        ov_ref[...] = _bf(dv_ref[...])
        dgq_ref[...] += jnp.sum(gq_r, axis=0, keepdims=True)
        dgk_ref[...] += jnp.sum(gk_r, axis=0, keepdims=True)

    def cur(col):
        return pl.BlockSpec((tm, dh), lambda i, h: (i, col // dh + h))

    gspec = pl.BlockSpec((1, dh), lambda i, h: (0, 0))
    ospec = pl.BlockSpec((tm, dh), lambda i, h: (i, h))
    return pl.pallas_call(
        body, grid=(t // tm, heads), in_specs=[cur(col_q), cur(col_k), gspec, gspec, ospec, ospec, ospec],
        out_specs=[ospec, ospec, ospec, gspec, gspec],
        out_shape=[jax.ShapeDtypeStruct((t, heads * dh), BF16)] * 3 + [jax.ShapeDtypeStruct((1, dh), F32)] * 2,
        name=name, compiler_params=_params("arbitrary", "arbitrary"))(proj, proj, gq, gk, dq, dk, dv)


PEERS = N_DEV - 1


def _exchange_copies(ins, outs, send_sems, recv_sems, local_sems, scatter):
    x, y, c = lax.axis_index("x"), lax.axis_index("y"), lax.axis_index("c")
    me = 4 * x + 2 * y + c
    copies = []
    for a in range(len(ins)):
        own = ins[a].at[me] if scatter[a] else ins[a]
        copies.append(pltpu.make_async_copy(own, outs[a].at[me], local_sems.at[a]))
        for k in range(1, N_DEV):
            px = (x + (k >> 2 & 1)) % 2
            py = (y + (k >> 1 & 1)) % 2
            pc = (c + (k & 1)) % 2
            src = ins[a].at[4 * px + 2 * py + pc] if scatter[a] else ins[a]
            copies.append(pltpu.make_async_remote_copy(
                src_ref=src, dst_ref=outs[a].at[me], send_sem=send_sems.at[a * PEERS + k - 1],
                recv_sem=recv_sems.at[a * PEERS + k - 1], device_id=(px, py, pc), device_id_type=MESH))
    return copies


def _exchange_shapes(srcs, scatter):
    return [jax.ShapeDtypeStruct(s.shape if sc else (N_DEV,) + s.shape, s.dtype) for s, sc in zip(srcs, scatter)]


def _exchange_sems(n):
    return [pltpu.SemaphoreType.DMA((n * PEERS,)), pltpu.SemaphoreType.DMA((n * PEERS,)), pltpu.SemaphoreType.DMA((n,))]


def _exchange(srcs, *, scatter, name):
    n = len(srcs)

    def body(*refs):
        copies = _exchange_copies(refs[:n], refs[n:2 * n], *refs[2 * n:], scatter)
        for cp in copies:
            cp.start()
        for cp in copies:
            cp.wait()

    any_spec = pl.BlockSpec(memory_space=pl.ANY)
    return pl.pallas_call(
        body, in_specs=[any_spec] * n, out_specs=[any_spec] * n, out_shape=_exchange_shapes(srcs, scatter),
        scratch_shapes=_exchange_sems(n), name=name,
        compiler_params=pltpu.CompilerParams(has_side_effects=True))(*srcs)


def _gather_two_level(srcs, *, name):
    n = len(srcs)

    def body(*refs):
        ins, outs = refs[:n], refs[n:2 * n]
        send_sems, recv_sems, local_sems = refs[2 * n:]
        x, y, c = lax.axis_index("x"), lax.axis_index("y"), lax.axis_index("c")
        chips = [(1 - x, y), (x, 1 - y), (1 - x, 1 - y)]

        def slab(a, px, py, pc):
            return outs[a].at[4 * px + 2 * py + pc]

        def copy(a, k, block, to, src=None):
            return pltpu.make_async_remote_copy(
                src_ref=slab(a, *block) if src is None else src, dst_ref=slab(a, *block),
                send_sem=send_sems.at[a * PEERS + k], recv_sem=recv_sems.at[a * PEERS + k],
                device_id=to, device_id_type=MESH)

        mine = [pltpu.make_async_copy(ins[a], slab(a, x, y, c), local_sems.at[a]) for a in range(n)]
        first = [copy(a, 0, (x, y, c), (x, y, 1 - c), src=ins[a]) for a in range(n)]
        first += [copy(a, 1 + j, (x, y, c), (*chip, c), src=ins[a]) for j, chip in enumerate(chips) for a in range(n)]
        for cp in mine + first:
            cp.start()
        passed = []
        for j, chip in enumerate(chips):
            for a in range(n):
                copy(a, 1 + j, (*chip, c), (x, y, c)).wait_recv()
                passed.append(copy(a, 4 + j, (*chip, c), (x, y, 1 - c)))
                passed[-1].start()
        for a in range(n):
            copy(a, 0, (x, y, 1 - c), (x, y, c)).wait_recv()
            for j, chip in enumerate(chips):
                copy(a, 4 + j, (*chip, 1 - c), (x, y, c)).wait_recv()
        for cp in first + passed:
            cp.wait_send()
        for cp in mine:
            cp.wait()

    any_spec = pl.BlockSpec(memory_space=pl.ANY)
    return pl.pallas_call(
        body, in_specs=[any_spec] * n, out_specs=[any_spec] * n, out_shape=_exchange_shapes(srcs, [False] * n),
        scratch_shapes=_exchange_sems(n), name=name,
        compiler_params=pltpu.CompilerParams(has_side_effects=True))(*srcs)


def _call_with_exchange(body, *, grid, in_specs, out_specs, out_shape, scratch_shapes, args, srcs, scatter, name):
    n, n_in, n_out, n_scr = len(srcs), len(args), len(out_shape), len(scratch_shapes)

    def full_body(*refs):
        ins, xin = refs[:n_in], refs[n_in:n_in + n]
        outs, xout = refs[n_in + n:n_in + n + n_out], refs[n_in + n + n_out:n_in + 2 * n + n_out]
        scr = refs[n_in + 2 * n + n_out:]
        ids = [pl.program_id(a) for a in range(len(grid))]
        first = functools.reduce(jnp.logical_and, [i == 0 for i in ids])
        last = functools.reduce(jnp.logical_and, [i == g - 1 for i, g in zip(ids, grid)])
        copies = _exchange_copies(xin, xout, *scr[n_scr:], scatter)

        @pl.when(first)
        def _():
            for cp in copies:
                cp.start()

        body(*ins, *outs, *scr[:n_scr])

        @pl.when(last)
        def _():
            for cp in copies:
                cp.wait()

    any_spec = pl.BlockSpec(memory_space=pl.ANY)
    return pl.pallas_call(
        full_body, grid=grid, in_specs=list(in_specs) + [any_spec] * n, out_specs=list(out_specs) + [any_spec] * n,
        out_shape=list(out_shape) + _exchange_shapes(srcs, scatter),
        scratch_shapes=list(scratch_shapes) + _exchange_sems(n), name=name,
        compiler_params=pltpu.CompilerParams(dimension_semantics=("arbitrary",) * len(grid),
                                             vmem_limit_bytes=V7X_VMEM_LIMIT_BYTES, has_side_effects=True))(*args, *srcs)


def _adam_math(g, w, m, v):
    m2 = ADAM_B1 * m + (1.0 - ADAM_B1) * g
    v2 = ADAM_B2 * v + (1.0 - ADAM_B2) * (g * g)
    m_hat = m2 / (1.0 - ADAM_B1 ** ADAM_STEP)
    v_hat = v2 / (1.0 - ADAM_B2 ** ADAM_STEP)
    return -ADAM_LR * (m_hat / (jnp.sqrt(v_hat) + ADAM_EPS) + ADAM_WD * w), m2, v2


def _adamw_slabs(slabs, w, m, v, *, name):
    r, c = w.shape
    tr = _pick(r, (128, 64, 32, 16, 8)) if r % 8 == 0 else r

    def body(s_ref, w_ref, m_ref, v_ref, g_ref, d_ref, mo_ref, vo_ref):
        g = s_ref[0].astype(F32)
        for p in range(1, N_DEV):
            g = g + s_ref[p].astype(F32)
        g_ref[...] = g
        d_ref[...], mo_ref[...], vo_ref[...] = _adam_math(g, w_ref[...], m_ref[...], v_ref[...])

    spec = pl.BlockSpec((tr, c), lambda i: (i, 0))
    return pl.pallas_call(
        body, grid=(r // tr,), in_specs=[pl.BlockSpec((N_DEV, tr, c), lambda i: (0, i, 0)), spec, spec, spec],
        out_specs=[spec] * 4, out_shape=[jax.ShapeDtypeStruct((r, c), F32)] * 4, name=name,
        compiler_params=_params("parallel"))(slabs, w, m, v)


def _adamw_small(g, w, m, v, *, name):
    def body(g_ref, w_ref, m_ref, v_ref, d_ref, mo_ref, vo_ref):
        d_ref[...], mo_ref[...], vo_ref[...] = _adam_math(g_ref[...], w_ref[...], m_ref[...], v_ref[...])

    return pl.pallas_call(body, out_shape=[jax.ShapeDtypeStruct(w.shape, F32)] * 3, name=name)(g, w, m, v)


def _sum_slabs(slabs, *, name):
    def body(s_ref, o_ref):
        acc = s_ref[0]
        for p in range(1, N_DEV):
            acc = acc + s_ref[p]
        o_ref[...] = acc

    return pl.pallas_call(body, out_shape=jax.ShapeDtypeStruct(slabs.shape[1:], F32), name=name)(slabs)


def _gather_cols(g):
    return jnp.transpose(g, (1, 0, 2)).reshape(g.shape[1], -1)


def _col_slabs(a):
    return jnp.transpose(a.reshape(a.shape[0], N_DEV, -1), (1, 0, 2))


def _pad_lanes(a):
    return jnp.pad(a, ((0, 0), (0, LANES - a.shape[1])))


def _local_step(x, target, meta, g_mix, w_main, w_ab, cq, ck, cv, a_log, dt_bias, g_dn, g_sbq, g_sbk, g_ffn, rest,
                shards=False):
    seq, d = x.shape
    n_meta = meta.shape[0]
    heads = a_log.shape[1]
    qk = cq.shape[1]
    dvt = cv.shape[1]
    dk, dv = qk // heads, dvt // heads
    dh = g_sbq.shape[1]
    sbw = rest[2].shape[0] * N_DEV if shards else rest[1].shape[0]
    sb_heads = sbw // dh
    pad_l = (-n_meta) % CHUNK
    row_x = pad_l + n_meta
    rows = row_x + seq
    t = -(-rows // GDN_ROWS) * GDN_ROWS
    col_q, col_k, col_v, col_z = 0, qk, 2 * qk, 2 * qk + dvt
    col_sq = 2 * qk + 2 * dvt
    col_sk, col_sv, col_gd, col_gs = col_sq + sbw, col_sq + 2 * sbw, col_sq + 3 * sbw, col_sq + 3 * sbw + d

    def rows_pad(a):
        return jnp.concatenate([jnp.zeros((row_x, d), F32), a, jnp.zeros((t - rows, d), F32)], axis=0)

    h0 = jnp.concatenate([jnp.zeros((pad_l, d), F32), meta, x, jnp.zeros((t - rows, d), F32)], axis=0)
    tgt = rows_pad(target)
    a_log_p, dt_p = _pad_lanes(a_log), _pad_lanes(dt_bias)

    proj, n1 = _mm_norm(h0, g_mix, w_main, name="proj")
    pab = _mm_nn(n1, w_ab, out_dtype=F32, name="proj_ab")
    gk = dict(heads=heads, dk=dk, dv=dv, col_q=col_q, col_k=col_k, col_v=col_v, row_lo=pad_l, row_hi=rows)
    qn, kn, vv, g, beta = _gdn_pre(proj, pab, cq, ck, cv, a_log_p, dt_p, name="gdn_pre", **gk)
    if shards:
        u, w, pm, qd, kd, egl, tinv, (g_fi, g_bd, g_bs, g_out, g_fo) = _gdn_prep(
            qn, kn, vv, g, beta, name="gdn_prep", gather=list(rest))
        w_fi = _gather_cols(g_fi)
        d_ff = w_fi.shape[1] // 2
        w_bd, w_bs, w_out, w_fg, w_fu, w_fo = (g_bd.reshape(-1, d), g_bs.reshape(-1, d), g_out.reshape(-1, d),
                                               w_fi[:, :d_ff], w_fi[:, d_ff:], g_fo.reshape(-1, d))
    else:
        u, w, pm, qd, kd, egl, tinv = _gdn_prep(qn, kn, vv, g, beta, name="gdn_prep")
        w_bd, w_bs, w_out, w_fg, w_fu, w_fo = rest
    o_raw, o_dn, states = _gdn_scan(u, w, pm, qd, kd, egl, proj, g_dn, col_z=col_z, name="gdn_scan")
    qs, ks, vs = _sb_pre(proj, g_sbq, g_sbk, heads=sb_heads, dh=dh, col_q=col_sq, col_k=col_sk, col_v=col_sv,
                         name="sb_pre")
    o_sb, carry = _sb_fwd(qs, ks, vs, heads=sb_heads, dh=dh, key_lo=pad_l, name="sb_fwd")
    merged, br_dn, br_sb = _merge_fwd(o_dn, o_sb, w_bd, w_bs, proj, col_gd=col_gd, col_gs=col_gs, name="merge")
    h1 = _mm_res(h0, merged, w_out, name="mix_out")
    gate, up, act, n2 = _mm_norm_swiglu(h1, g_ffn, w_fg, w_fu, name="ffn_in")
    dy, dyb, lsum = _mm_res_loss(h1, act, w_fo, tgt, row0=row_x, nrows=seq, name="ffn_out_loss")

    dgate, dup = _swiglu_bwd(dyb, w_fo, gate, up, name="ffn_out_bwd")
    d_w_fo = _mm_tn(act, dyb, name="dw_ffn_out")
    d_w_fg = _mm_tn(n2, dgate, name="dw_ffn_gate")
    d_w_fu = _mm_tn(n2, dup, name="dw_ffn_up")
    dh1, dh1b, d_g_ffn = _mm_nt_rmsbwd([(dgate, w_fg), (dup, w_fu)], None, h1, g_ffn, dy, name="ffn_in_bwd")

    dbd, dbs, dgd, dgs = _merge_bwd(dh1b, w_out, proj, br_dn, br_sb, col_gd=col_gd, col_gs=col_gs, name="mix_out_bwd")
    d_w_out = _mm_tn(merged, dh1b, name="dw_out")
    d_w_bd = _mm_tn(o_dn, dbd, name="dw_branch_dn")
    d_w_bs = _mm_tn(o_sb, dbs, name="dw_branch_sb")
    do_dn = _mm_nt(dbd, w_bd, out_dtype=F32, name="branch_dn_bwd")
    do_sb = _mm_nt(dbs, w_bs, out_dtype=BF16, name="branch_sb_bwd")

    do_raw, dz, d_g_dn = _gdn_post_bwd(o_raw, proj, g_dn, do_dn, col_z=col_z, name="gdn_post_bwd")
    du, dw, dp, dqd, dkd, dgl = _gdn_bwd_scan(u, w, pm, qd, kd, egl, states, do_raw, name="gdn_bwd_scan")
    dqn, dkn, dvv, dg, dbeta = _gdn_bwd_prep(qn, kn, vv, g, beta, tinv, u, w, du, dw, dp, dqd, dkd, dgl,
                                            name="gdn_bwd_prep")
    dcq, dck, dcv, dpab, d_a_log, d_dt = _gdn_pre_bwd_a(proj, pab, cq, ck, cv, a_log_p, dt_p, dqn, dkn, dvv, dg, dbeta,
                                                        name="gdn_pre_bwd", **gk)
    dpq, d_cq = _conv_bwd(proj, dcq, cq, heads=heads, width=dk, col=col_q, name="conv_q_bwd")
    dpk, d_ck = _conv_bwd(proj, dck, ck, heads=heads, width=dk, col=col_k, name="conv_k_bwd")
    dpv, d_cv = _conv_bwd(proj, dcv, cv, heads=heads, width=dv, col=col_v, name="conv_v_bwd")

    early = None
    if shards:
        slabs = [_col_slabs(_bf(jnp.concatenate([d_w_fg, d_w_fu], axis=1)))]
        slabs += [_bf(a).reshape(N_DEV, -1, d) for a in (d_w_bd, d_w_bs, d_w_out, d_w_fo)]
        dqs, dks, dvs, early = _sb_bwd(qs, ks, vs, do_sb, carry, heads=sb_heads, dh=dh, key_lo=pad_l, name="sb_bwd",
                                       scatter=slabs)
    else:
        dqs, dks, dvs = _sb_bwd(qs, ks, vs, do_sb, carry, heads=sb_heads, dh=dh, key_lo=pad_l, name="sb_bwd")
    dsq, dsk, dsv, d_g_sbq, d_g_sbk = _sb_pre_bwd(proj, g_sbq, g_sbk, dqs, dks, dvs, heads=sb_heads, dh=dh,
                                                   col_q=col_sq, col_k=col_sk, name="sb_pre_bwd")

    dproj = jnp.concatenate([dpq, dpk, dpv, dz, dsq, dsk, dsv, dgd, dgs], axis=1)
    dpab_b = _bf(dpab)
    d_w_main = _mm_tn(n1, dproj, name="dw_in_main")
    d_w_ab = _mm_tn(n1, dpab_b, name="dw_in_ab")
    dh0, _, d_g_mix = _mm_nt_rmsbwd([(dproj, w_main)], (dpab_b, w_ab), h0, g_mix, dh1, name="proj_bwd")

    return dict(lsum=lsum, grad_x=dh0[row_x:rows], d_meta=dh0[pad_l:row_x], d_g_mix=d_g_mix, d_w_main=d_w_main,
                d_w_ab=d_w_ab, d_cq=d_cq, d_ck=d_ck, d_cv=d_cv, d_a_log=d_a_log[:, :heads], d_dt=d_dt[:, :heads],
                d_g_dn=d_g_dn, d_g_sbq=d_g_sbq, d_g_sbk=d_g_sbk, d_w_bd=d_w_bd, d_w_bs=d_w_bs, d_w_out=d_w_out,
                d_g_ffn=d_g_ffn, d_w_fg=d_w_fg, d_w_fu=d_w_fu, d_w_fo=d_w_fo, early=early)


def _pack(parts):
    flat = []
    for a in parts:
        a = a.reshape(-1)
        flat.append(jnp.pad(a, (0, (-a.shape[0]) % LANES)))
    v = jnp.concatenate(flat)
    v = jnp.pad(v, (0, (-v.shape[0]) % (8 * LANES)))
    return v.reshape(-1, LANES)


def _unpack(packed, shapes):
    flat = packed.reshape(-1)
    out, pos = [], 0
    for s in shapes:
        n = math.prod(s)
        out.append(flat[pos:pos + n].reshape(s))
        pos += n + (-n) % LANES
    return out


def kernel(x, meta_tokens, norm_mix_gain, w_in, conv_q, conv_k, conv_v, dn_a_log, dn_dt_bias, dn_out_norm_gain, sb_q_norm_gain, sb_k_norm_gain, w_branch_dn, w_branch_sb, w_out, norm_ffn_gain, w_ffn_in, w_ffn_out, loss_target, m_meta_tokens, m_norm_mix_gain, m_w_in, m_conv_q, m_conv_k, m_conv_v, m_dn_a_log, m_dn_dt_bias, m_dn_out_norm_gain, m_sb_q_norm_gain, m_sb_k_norm_gain, m_w_branch_dn, m_w_branch_sb, m_w_out, m_norm_ffn_gain, m_w_ffn_in, m_w_ffn_out, v_meta_tokens, v_norm_mix_gain, v_w_in, v_conv_q, v_conv_k, v_conv_v, v_dn_a_log, v_dn_dt_bias, v_dn_out_norm_gain, v_sb_q_norm_gain, v_sb_k_norm_gain, v_w_branch_dn, v_w_branch_sb, v_w_out, v_norm_ffn_gain, v_w_ffn_in, v_w_ffn_out):
    me = 4 * lax.axis_index("x") + 2 * lax.axis_index("y") + lax.axis_index("c")
    heads = dn_a_log.shape[1]
    d = x.shape[2]
    qk = conv_q.shape[2] * N_DEV
    dvt = conv_v.shape[2] * N_DEV
    col_ab = 2 * qk + 2 * dvt

    small_shapes = [meta_tokens.shape, conv_q.shape[1:], conv_k.shape[1:], conv_v.shape[1:]]
    small = _pack([meta_tokens, conv_q[0], conv_k[0], conv_v[0]])
    g_in, g_small = _gather_two_level([_bf(w_in[0]), small], name="gather_w_in")
    w_full = _gather_cols(g_in)
    w_main = jnp.concatenate([w_full[:, :col_ab], w_full[:, col_ab + 2 * heads:]], axis=1)
    w_ab = _pad_lanes(w_full[:, col_ab:col_ab + 2 * heads])
    parts = [_unpack(g_small[p], small_shapes) for p in range(N_DEV)]
    meta_f, cq_f, ck_f, cv_f = (jnp.concatenate([parts[p][a] for p in range(N_DEV)], axis=1) for a in range(4))

    r = _local_step(x[0], loss_target[0], meta_f, norm_mix_gain, w_main, w_ab, cq_f, ck_f, cv_f, dn_a_log, dn_dt_bias,
                    dn_out_norm_gain, sb_q_norm_gain, sb_k_norm_gain, norm_ffn_gain,
                    (_bf(w_ffn_in[0]), _bf(w_branch_dn[0]), _bf(w_branch_sb[0]), _bf(w_out[0]), _bf(w_ffn_out[0])),
                    shards=True)
    s_fi, s_bd, s_bs, s_out, s_fo = r["early"]

    d_w_in = jnp.concatenate([r["d_w_main"][:, :col_ab], r["d_w_ab"][:, :2 * heads], r["d_w_main"][:, col_ab:]], axis=1)
    loss_part = (0.5 / d) * jnp.sum(r["lsum"], axis=1, keepdims=True)
    small_g = [r["d_meta"], r["d_g_mix"], r["d_cq"], r["d_ck"], r["d_cv"], r["d_a_log"], r["d_dt"], r["d_g_dn"],
               r["d_g_sbq"], r["d_g_sbk"], r["d_g_ffn"], loss_part]
    s_in, g_packs = _exchange([_col_slabs(_bf(d_w_in)), _pack(small_g)], scatter=[True, False], name="scatter_dw_in")
    (g_meta, g_mix, g_cq, g_ck, g_cv, g_al, g_dt, g_gdn, g_sbq, g_sbk, g_ffn, loss) = _unpack(
        _sum_slabs(g_packs, name="sum_small_grads"), [a.shape for a in small_g])

    def mine(a, width):
        return lax.dynamic_slice_in_dim(a, me * width, width, axis=1)

    big = dict(w_in=(s_in, w_in, m_w_in, v_w_in), w_branch_dn=(s_bd, w_branch_dn, m_w_branch_dn, v_w_branch_dn),
               w_branch_sb=(s_bs, w_branch_sb, m_w_branch_sb, v_w_branch_sb), w_out=(s_out, w_out, m_w_out, v_w_out),
               w_ffn_in=(s_fi, w_ffn_in, m_w_ffn_in, v_w_ffn_in), w_ffn_out=(s_fo, w_ffn_out, m_w_ffn_out, v_w_ffn_out))
    tiny = dict(meta_tokens=(mine(g_meta, d // N_DEV), meta_tokens, m_meta_tokens, v_meta_tokens),
                norm_mix_gain=(g_mix, norm_mix_gain, m_norm_mix_gain, v_norm_mix_gain),
                conv_q=(mine(g_cq, qk // N_DEV), conv_q[0], m_conv_q[0], v_conv_q[0]),
                conv_k=(mine(g_ck, qk // N_DEV), conv_k[0], m_conv_k[0], v_conv_k[0]),
                conv_v=(mine(g_cv, dvt // N_DEV), conv_v[0], m_conv_v[0], v_conv_v[0]),
                dn_a_log=(g_al, dn_a_log, m_dn_a_log, v_dn_a_log), dn_dt_bias=(g_dt, dn_dt_bias, m_dn_dt_bias, v_dn_dt_bias),
                dn_out_norm_gain=(g_gdn, dn_out_norm_gain, m_dn_out_norm_gain, v_dn_out_norm_gain),
                sb_q_norm_gain=(g_sbq, sb_q_norm_gain, m_sb_q_norm_gain, v_sb_q_norm_gain),
                sb_k_norm_gain=(g_sbk, sb_k_norm_gain, m_sb_k_norm_gain, v_sb_k_norm_gain),
                norm_ffn_gain=(g_ffn, norm_ffn_gain, m_norm_ffn_gain, v_norm_ffn_gain))
    order = ["meta_tokens", "norm_mix_gain", "w_in", "conv_q", "conv_k", "conv_v", "dn_a_log", "dn_dt_bias",
             "dn_out_norm_gain", "sb_q_norm_gain", "sb_k_norm_gain", "w_branch_dn", "w_branch_sb", "w_out",
             "norm_ffn_gain", "w_ffn_in", "w_ffn_out"]
    grads, deltas, new_m, new_v = [], [], [], []
    for name in order:
        if name in big:
            slabs, w, m, v = big[name]
            g, dl, mo, vo = _adamw_slabs(slabs, w[0], m[0], v[0], name="adamw_" + name)
            like = w.shape
        else:
            g, w, m, v = tiny[name]
            like = dict(conv_q=conv_q, conv_k=conv_k, conv_v=conv_v).get(name, w).shape
            dl, mo, vo = _adamw_small(g, w, m, v, name="adamw_" + name)
        for lst, a in ((grads, g), (deltas, dl), (new_m, mo), (new_v, vo)):
            lst.append(a.reshape(like))
    return (loss.reshape(()), r["grad_x"][None], *grads, *deltas, *new_m, *new_v)
```

```python
import functools
import math

import jax
import jax.numpy as jnp
from jax import lax
from jax.experimental import pallas as pl
from jax.experimental.pallas import tpu as pltpu

F32 = jnp.float32
BF16 = jnp.bfloat16

N_DEV = 8
CHUNK = 64
CHUNK_SHIFT = 6
GDN_ROWS = 2 * CHUNK
SB_BLOCK = 128
SB_HEADS_PER_STEP = 2
LANES = 128
RMS_EPS = 1e-6
L2_EPS = 1e-6
ADAM_LR = 0.001
ADAM_B1 = 0.9
ADAM_B2 = 0.999
ADAM_EPS = 1e-08
ADAM_WD = 0.01
ADAM_STEP = 10
V7X_VMEM_LIMIT_BYTES = 56 * 1024 * 1024
MM_TN_OUT_BLOCK_BYTES = 6 * 1024 * 1024
ROWS_BIG = (1056, 512, 384, 256, 128)
ROWS_MID = (528, 384, 256, 128)
SCAN_HEADS = 4

MESH = pl.DeviceIdType.MESH


def _params(*sem):
    return pltpu.CompilerParams(dimension_semantics=sem or None, vmem_limit_bytes=V7X_VMEM_LIMIT_BYTES)


def _pick(n, cands):
    for c in cands:
        if n % c == 0:
            return c
    raise ValueError(f"no block size among {cands} divides {n}")


def _bf(x):
    return x.astype(BF16)


def _dot(a, b):
    return jnp.dot(a, b, preferred_element_type=F32)


def _dot_nt(a, b):
    return lax.dot_general(a, b, (((1,), (1,)), ((), ())), preferred_element_type=F32)


def _dot_tn(a, b):
    return lax.dot_general(a, b, (((0,), (0,)), ((), ())), preferred_element_type=F32)


def _split2(x):
    hi = _bf(x)
    return hi, _bf(x - hi.astype(F32))


def _split3(x):
    hi = _bf(x)
    r = x - hi.astype(F32)
    mid = _bf(r)
    return hi, mid, _bf(r - mid.astype(F32))


def _dot_hp(a, b, dot=_dot):
    ah, al = _split2(a)
    bh, bl = _split2(b)
    return dot(ah, bh) + dot(ah, bl) + dot(al, bh)


def _dot_exact_r(x, m, dot=_dot):
    h, mi, lo = _split3(x)
    return dot(h, m) + dot(mi, m) + dot(lo, m)


def _dot_exact_l(m, x, dot=_dot):
    h, mi, lo = _split3(x)
    return dot(m, h) + dot(m, mi) + dot(m, lo)


def _sigmoid(x):
    return 1.0 / (1.0 + jnp.exp(-x))


def _silu(x):
    return x * _sigmoid(x)


def _silu_grad(x):
    s = _sigmoid(x)
    return s * (1.0 + x * (1.0 - s))


def _softplus(x):
    return jnp.maximum(x, 0.0) + jnp.log(1.0 + jnp.exp(-jnp.abs(x)))


def _rms_fwd(h, gain):
    r = lax.rsqrt(jnp.mean(h * h, axis=-1, keepdims=True) + RMS_EPS)
    return h * r * gain


def _rms_bwd(h, gain, dy):
    r = lax.rsqrt(jnp.mean(h * h, axis=-1, keepdims=True) + RMS_EPS)
    dyg = dy * gain
    dh = r * dyg - h * (r * r * r) * jnp.mean(dyg * h, axis=-1, keepdims=True)
    return dh, dy * h * r


def _iota(shape, dim):
    return lax.broadcasted_iota(jnp.int32, shape, dim)


def _lane_pick(x, idx):
    return jnp.sum(jnp.where(_iota(x.shape, 1) == idx, x, 0.0), axis=1, keepdims=True)


def _mm_nn(a, b, *, out_dtype, name):
    m, k = a.shape
    n = b.shape[1]
    tm, tn = _pick(m, ROWS_BIG), _pick(n, (1024, 512, 256, 128))

    def body(a_ref, b_ref, o_ref):
        o_ref[...] = _dot(a_ref[...], b_ref[...]).astype(out_dtype)

    return pl.pallas_call(
        body, grid=(m // tm, n // tn),
        in_specs=[pl.BlockSpec((tm, k), lambda i, j: (i, 0)), pl.BlockSpec((k, tn), lambda i, j: (0, j))],
        out_specs=pl.BlockSpec((tm, tn), lambda i, j: (i, j)),
        out_shape=jax.ShapeDtypeStruct((m, n), out_dtype), name=name,
        compiler_params=_params("parallel", "parallel"))(a, b)


def _mm_nt(a, b, *, out_dtype, name):
    m, k = a.shape
    n = b.shape[0]
    tm, tn = _pick(m, ROWS_BIG), _pick(n, (1024, 512, 256, 128))

    def body(a_ref, b_ref, o_ref):
        o_ref[...] = _dot_nt(a_ref[...], b_ref[...]).astype(out_dtype)

    return pl.pallas_call(
        body, grid=(m // tm, n // tn),
        in_specs=[pl.BlockSpec((tm, k), lambda i, j: (i, 0)), pl.BlockSpec((tn, k), lambda i, j: (j, 0))],
        out_specs=pl.BlockSpec((tm, tn), lambda i, j: (i, j)),
        out_shape=jax.ShapeDtypeStruct((m, n), out_dtype), name=name,
        compiler_params=_params("parallel", "parallel"))(a, b)


def _mm_tn(a, b, *, name):
    t, m = a.shape
    n = b.shape[1]
    tn = _pick(n, (2816, 2048, 1408, 1024, 512, 256, 128))
    tm = _pick(m, tuple(c for c in (1408, 1024, 512, 256, 128) if c * tn * 4 <= MM_TN_OUT_BLOCK_BYTES))
    tk = _pick(t, (1408, 1024, 512, 384, 256, 128))

    def body(a_ref, b_ref, o_ref):
        @pl.when(pl.program_id(2) == 0)
        def _():
            o_ref[...] = jnp.zeros_like(o_ref)

        o_ref[...] += _dot_tn(a_ref[...], b_ref[...])

    return pl.pallas_call(
        body, grid=(m // tm, n // tn, t // tk),
        in_specs=[pl.BlockSpec((tk, tm), lambda i, j, k: (k, i)), pl.BlockSpec((tk, tn), lambda i, j, k: (k, j))],
        out_specs=pl.BlockSpec((tm, tn), lambda i, j, k: (i, j)),
        out_shape=jax.ShapeDtypeStruct((m, n), F32), name=name,
        compiler_params=_params("parallel", "parallel", "arbitrary"))(a, b)


def _mm_norm(h, gain, w, *, name):
    m, k = h.shape
    n = w.shape[1]
    tm, tn = _pick(m, ROWS_BIG), _pick(n, (1024, 512, 256, 128))

    def body(h_ref, g_ref, w_ref, o_ref, n_ref):
        @pl.when(pl.program_id(1) == 0)
        def _():
            n_ref[...] = _bf(_rms_fwd(h_ref[...], g_ref[...]))

        o_ref[...] = _dot(n_ref[...], w_ref[...])

    return pl.pallas_call(
        body, grid=(m // tm, n // tn),
        in_specs=[pl.BlockSpec((tm, k), lambda i, j: (i, 0)), pl.BlockSpec((1, k), lambda i, j: (0, 0)),
                  pl.BlockSpec((k, tn), lambda i, j: (0, j))],
        out_specs=[pl.BlockSpec((tm, tn), lambda i, j: (i, j)), pl.BlockSpec((tm, k), lambda i, j: (i, 0))],
        out_shape=[jax.ShapeDtypeStruct((m, n), F32), jax.ShapeDtypeStruct((m, k), BF16)], name=name,
        compiler_params=_params("parallel", "arbitrary"))(h, gain, w)


def _mm_norm_swiglu(h, gain, wg, wu, *, name):
    m, k = h.shape
    n = wg.shape[1]
    tm, tn = _pick(m, ROWS_MID), _pick(n, (1408, 1024, 512, 256, 128))

    def body(h_ref, g_ref, wg_ref, wu_ref, gate_ref, up_ref, act_ref, n_ref):
        @pl.when(pl.program_id(1) == 0)
        def _():
            n_ref[...] = _bf(_rms_fwd(h_ref[...], g_ref[...]))

        gate = _dot(n_ref[...], wg_ref[...])
        up = _dot(n_ref[...], wu_ref[...])
        gate_ref[...] = gate
        up_ref[...] = up
        act_ref[...] = _bf(_silu(gate) * up)

    wspec = pl.BlockSpec((k, tn), lambda i, j: (0, j))
    ospec = pl.BlockSpec((tm, tn), lambda i, j: (i, j))
    return pl.pallas_call(
        body, grid=(m // tm, n // tn),
        in_specs=[pl.BlockSpec((tm, k), lambda i, j: (i, 0)), pl.BlockSpec((1, k), lambda i, j: (0, 0)), wspec, wspec],
        out_specs=[ospec, ospec, ospec, pl.BlockSpec((tm, k), lambda i, j: (i, 0))],
        out_shape=[jax.ShapeDtypeStruct((m, n), F32), jax.ShapeDtypeStruct((m, n), F32),
                   jax.ShapeDtypeStruct((m, n), BF16), jax.ShapeDtypeStruct((m, k), BF16)], name=name,
        compiler_params=_params("parallel", "arbitrary"))(h, gain, wg, wu)


def _mm_res(res, a, b, *, name):
    m, k = a.shape
    n = b.shape[1]
    tm, tn = _pick(m, ROWS_BIG), _pick(n, (1024, 512, 256, 128))

    def body(r_ref, a_ref, b_ref, o_ref):
        o_ref[...] = r_ref[...] + _dot(a_ref[...], b_ref[...])

    return pl.pallas_call(
        body, grid=(m // tm, n // tn),
        in_specs=[pl.BlockSpec((tm, tn), lambda i, j: (i, j)), pl.BlockSpec((tm, k), lambda i, j: (i, 0)),
                  pl.BlockSpec((k, tn), lambda i, j: (0, j))],
        out_specs=pl.BlockSpec((tm, tn), lambda i, j: (i, j)),
        out_shape=jax.ShapeDtypeStruct((m, n), F32), name=name,
        compiler_params=_params("parallel", "parallel"))(res, a, b)


def _mm_res_loss(res, a, b, target, *, row0, nrows, name):
    m, k = a.shape
    n = b.shape[1]
    tm = _pick(m, ROWS_MID)

    def body(r_ref, a_ref, b_ref, t_ref, dy_ref, dyb_ref, ls_ref):
        i = pl.program_id(0)

        @pl.when(i == 0)
        def _():
            ls_ref[...] = jnp.zeros_like(ls_ref)

        y = r_ref[...] + _dot(a_ref[...], b_ref[...])
        row = i * tm + _iota((tm, n), 0)
        e = jnp.where((row >= row0) & (row < row0 + nrows), y - t_ref[...], 0.0)
        dy = e / n
        dy_ref[...] = dy
        dyb_ref[...] = _bf(dy)
        ls_ref[...] += jnp.sum(e * e, axis=0, keepdims=True)

    rspec = pl.BlockSpec((tm, n), lambda i: (i, 0))
    return pl.pallas_call(
        body, grid=(m // tm,),
        in_specs=[rspec, pl.BlockSpec((tm, k), lambda i: (i, 0)), pl.BlockSpec((k, n), lambda i: (0, 0)), rspec],
        out_specs=[rspec, rspec, pl.BlockSpec((1, n), lambda i: (0, 0))],
        out_shape=[jax.ShapeDtypeStruct((m, n), F32), jax.ShapeDtypeStruct((m, n), BF16),
                   jax.ShapeDtypeStruct((1, n), F32)], name=name,
        compiler_params=_params("arbitrary"))(res, a, b, target)


def _merge_fwd(o_dn, o_sb, wbd, wbs, proj, *, col_gd, col_gs, name):
    m, kd = o_dn.shape
    ks = o_sb.shape[1]
    n = wbd.shape[1]
    tm = _pick(m, ROWS_BIG)
    tn = _pick(math.gcd(n, math.gcd(col_gd, col_gs)), (512, 256, 128))

    def body(od_ref, os_ref, wd_ref, ws_ref, gd_ref, gs_ref, mg_ref, bd_ref, bs_ref):
        bd = _dot(od_ref[...], wd_ref[...])
        bs = _dot(os_ref[...], ws_ref[...])
        bd_ref[...] = bd
        bs_ref[...] = bs
        mg_ref[...] = _bf(_sigmoid(gd_ref[...]) * bd + _sigmoid(gs_ref[...]) * bs)

    ospec = pl.BlockSpec((tm, tn), lambda i, j: (i, j))
    return pl.pallas_call(
        body, grid=(m // tm, n // tn),
        in_specs=[pl.BlockSpec((tm, kd), lambda i, j: (i, 0)), pl.BlockSpec((tm, ks), lambda i, j: (i, 0)),
                  pl.BlockSpec((kd, tn), lambda i, j: (0, j)), pl.BlockSpec((ks, tn), lambda i, j: (0, j)),
                  pl.BlockSpec((tm, tn), lambda i, j: (i, col_gd // tn + j)),
                  pl.BlockSpec((tm, tn), lambda i, j: (i, col_gs // tn + j))],
        out_specs=[ospec, ospec, ospec],
        out_shape=[jax.ShapeDtypeStruct((m, n), BF16), jax.ShapeDtypeStruct((m, n), F32),
                   jax.ShapeDtypeStruct((m, n), F32)], name=name,
        compiler_params=_params("parallel", "parallel"))(o_dn, o_sb, wbd, wbs, proj, proj)


def _merge_bwd(dh, w_out, proj, br_dn, br_sb, *, col_gd, col_gs, name):
    m, k = dh.shape
    n = w_out.shape[0]
    tm = _pick(m, ROWS_BIG)
    tn = _pick(math.gcd(n, math.gcd(col_gd, col_gs)), (512, 256, 128))

    def body(dh_ref, w_ref, gd_ref, gs_ref, bd_ref, bs_ref, dbd_ref, dbs_ref, dgd_ref, dgs_ref):
        dm = _dot_nt(dh_ref[...], w_ref[...])
        sd = _sigmoid(gd_ref[...])
        ss = _sigmoid(gs_ref[...])
        dbd_ref[...] = _bf(dm * sd)
        dbs_ref[...] = _bf(dm * ss)
        dgd_ref[...] = _bf(dm * bd_ref[...] * sd * (1.0 - sd))
        dgs_ref[...] = _bf(dm * bs_ref[...] * ss * (1.0 - ss))

    ospec = pl.BlockSpec((tm, tn), lambda i, j: (i, j))
    return pl.pallas_call(
        body, grid=(m // tm, n // tn),
        in_specs=[pl.BlockSpec((tm, k), lambda i, j: (i, 0)), pl.BlockSpec((tn, k), lambda i, j: (j, 0)),
                  pl.BlockSpec((tm, tn), lambda i, j: (i, col_gd // tn + j)),
                  pl.BlockSpec((tm, tn), lambda i, j: (i, col_gs // tn + j)), ospec, ospec],
        out_specs=[ospec] * 4,
        out_shape=[jax.ShapeDtypeStruct((m, n), BF16)] * 4, name=name,
        compiler_params=_params("parallel", "parallel"))(dh, w_out, proj, proj, br_dn, br_sb)


def _swiglu_bwd(dy, wfo, gate, up, *, name):
    m, k = dy.shape
    n = wfo.shape[0]
    tm, tn = _pick(m, ROWS_MID), _pick(n, (1408, 1024, 512, 256, 128))

    def body(dy_ref, w_ref, g_ref, u_ref, dg_ref, du_ref):
        da = _dot_nt(dy_ref[...], w_ref[...])
        g = g_ref[...]
        dg_ref[...] = _bf(da * u_ref[...] * _silu_grad(g))
        du_ref[...] = _bf(da * _silu(g))

    ospec = pl.BlockSpec((tm, tn), lambda i, j: (i, j))
    return pl.pallas_call(
        body, grid=(m // tm, n // tn),
        in_specs=[pl.BlockSpec((tm, k), lambda i, j: (i, 0)), pl.BlockSpec((tn, k), lambda i, j: (j, 0)), ospec, ospec],
        out_specs=[ospec, ospec], out_shape=[jax.ShapeDtypeStruct((m, n), BF16)] * 2, name=name,
        compiler_params=_params("parallel", "parallel"))(dy, wfo, gate, up)


def _mm_nt_rmsbwd(pairs, extra, h, gain, dres, *, name, scatter=None):
    m, k = pairs[0][0].shape
    n = h.shape[1]
    tm = _pick(m, ROWS_MID)
    tk = _pick(k, (1408, 1024, 512, 256, 128))
    nk = k // tk
    np_ = len(pairs)

    def body(*refs):
        ab = refs[:2 * np_]
        pos = 2 * np_
        ex = refs[pos:pos + 2] if extra is not None else ()
        pos += len(ex)
        h_ref, g_ref, r_ref, dh_ref, dhb_ref, dg_ref, acc_ref = refs[pos:]
        i, kk = pl.program_id(0), pl.program_id(1)

        @pl.when((i == 0) & (kk == 0))
        def _():
            dg_ref[...] = jnp.zeros_like(dg_ref)

        part = _dot_nt(ab[0][...], ab[1][...])
        for p in range(1, np_):
            part += _dot_nt(ab[2 * p][...], ab[2 * p + 1][...])

        @pl.when(kk == 0)
        def _():
            first = part
            if ex:
                first = first + _dot_nt(ex[0][...], ex[1][...])
            acc_ref[...] = first

        @pl.when(kk > 0)
        def _():
            acc_ref[...] += part

        @pl.when(kk == nk - 1)
        def _():
            dh, dgr = _rms_bwd(h_ref[...], g_ref[...], acc_ref[...])
            dh = dh + r_ref[...]
            dh_ref[...] = dh
            dhb_ref[...] = _bf(dh)
            dg_ref[...] += jnp.sum(dgr, axis=0, keepdims=True)

    in_specs, args = [], []
    for a, b in pairs:
        in_specs += [pl.BlockSpec((tm, tk), lambda i, kk: (i, kk)), pl.BlockSpec((n, tk), lambda i, kk: (0, kk))]
        args += [a, b]
    if extra is not None:
        k2 = extra[0].shape[1]
        in_specs += [pl.BlockSpec((tm, k2), lambda i, kk: (i, 0)), pl.BlockSpec((n, k2), lambda i, kk: (0, 0))]
        args += list(extra)
    rspec = pl.BlockSpec((tm, n), lambda i, kk: (i, 0))
    in_specs += [rspec, pl.BlockSpec((1, n), lambda i, kk: (0, 0)), rspec]
    call = dict(grid=(m // tm, nk), in_specs=in_specs,
                out_specs=[rspec, rspec, pl.BlockSpec((1, n), lambda i, kk: (0, 0))],
                out_shape=[jax.ShapeDtypeStruct((m, n), F32), jax.ShapeDtypeStruct((m, n), BF16),
                           jax.ShapeDtypeStruct((1, n), F32)],
                scratch_shapes=[pltpu.VMEM((tm, n), F32)], name=name)
    if scatter is None:
        return pl.pallas_call(body, compiler_params=_params("arbitrary", "arbitrary"), **call)(*args, h, gain, dres)
    res = _call_with_exchange(body, args=(*args, h, gain, dres), srcs=scatter, scatter=[True] * len(scatter), **call)
    return [*res[:3], list(res[3:])]


def _conv_taps(cur, prev8, w_ref, first):
    nk = w_ref.shape[0]
    out = cur * w_ref[nk - 1:nk, :]
    for s in range(1, nk):
        out += _shift_down(cur, prev8, s, first) * w_ref[nk - 1 - s:nk - s, :]
    return out


def _shift_up(cur, next8, s, last):
    rows = cur.shape[0]
    row = _iota(cur.shape, 0)
    next8 = jnp.where(last, 0.0, next8)
    sh = pltpu.roll(cur, rows - s, axis=0)
    nh = jnp.tile(pltpu.roll(next8, 8 - s, axis=0), (rows // 8, 1))
    return jnp.where(row >= rows - s, nh, sh)


def _shift_down(cur, prev8, s, first):
    rows = cur.shape[0]
    row = _iota(cur.shape, 0)
    prev8 = jnp.where(first, 0.0, prev8)
    sh = pltpu.roll(cur, s, axis=0)
    ph = jnp.tile(pltpu.roll(prev8, s, axis=0), (rows // 8, 1))
    return jnp.where(row < s, ph, sh)


def _gdn_pre(proj, pab, cq, ck, cv, a_log, dt_bias, *, heads, dk, dv, col_q, col_k, col_v, row_lo, row_hi, name):
    t = proj.shape[0]
    tm = _pick(t, (384, 256, 128))
    nb = t // tm

    def body(pq_ref, pqp_ref, pk_ref, pkp_ref, pv_ref, pvp_ref, ab_ref, cq_ref, ck_ref, cv_ref, al_ref, dt_ref,
             qn_ref, kn_ref, v_ref, g_ref, b_ref):
        h, i = pl.program_id(0), pl.program_id(1)
        first = i == 0
        row = i * tm + _iota((tm, 1), 0)
        valid = (row >= row_lo) & (row < row_hi)
        q1 = _silu(_conv_taps(pq_ref[...], pqp_ref[...], cq_ref, first))
        k1 = _silu(_conv_taps(pk_ref[...], pkp_ref[...], ck_ref, first))
        v1 = _silu(_conv_taps(pv_ref[...], pvp_ref[...], cv_ref, first))
        qn_ref[...] = jnp.where(valid, q1 * lax.rsqrt(jnp.sum(q1 * q1, axis=-1, keepdims=True) + L2_EPS), 0.0)
        kn_ref[...] = jnp.where(valid, k1 * lax.rsqrt(jnp.sum(k1 * k1, axis=-1, keepdims=True) + L2_EPS), 0.0)
        v_ref[...] = jnp.where(valid, v1, 0.0)
        ab = ab_ref[...]
        da = _lane_pick(ab, h)
        db = _lane_pick(ab, heads + h)
        a = _lane_pick(al_ref[...], h)
        dtb = _lane_pick(dt_ref[...], h)
        g_ref[...] = jnp.where(valid, -jnp.exp(a) * _softplus(da + dtb), 0.0)
        b_ref[...] = jnp.where(valid, _sigmoid(db), 0.0)

    def cur(width, col):
        return pl.BlockSpec((tm, width), lambda h, i: (i, col // width + h))

    def prev(width, col):
        return pl.BlockSpec((8, width), lambda h, i: (jnp.maximum(i * (tm // 8) - 1, 0), col // width + h))

    def out(width):
        return pl.BlockSpec((None, tm, width), lambda h, i: (h, i, 0))

    small = pl.BlockSpec((1, LANES), lambda h, i: (0, 0))
    return pl.pallas_call(
        body, grid=(heads, nb),
        in_specs=[cur(dk, col_q), prev(dk, col_q), cur(dk, col_k), prev(dk, col_k), cur(dv, col_v), prev(dv, col_v),
                  pl.BlockSpec((tm, LANES), lambda h, i: (i, 0)),
                  pl.BlockSpec((cq.shape[0], dk), lambda h, i: (0, h)), pl.BlockSpec((ck.shape[0], dk), lambda h, i: (0, h)),
                  pl.BlockSpec((cv.shape[0], dv), lambda h, i: (0, h)), small, small],
        out_specs=[out(dk), out(dk), out(dv), out(1), out(1)],
        out_shape=[jax.ShapeDtypeStruct((heads, t, dk), F32), jax.ShapeDtypeStruct((heads, t, dk), F32),
                   jax.ShapeDtypeStruct((heads, t, dv), F32), jax.ShapeDtypeStruct((heads, t, 1), F32),
                   jax.ShapeDtypeStruct((heads, t, 1), F32)], name=name,
        compiler_params=_params("parallel", "parallel"))(proj, proj, proj, proj, proj, proj, pab, cq, ck, cv, a_log, dt_bias)


def _chunk_masks(rows=GDN_ROWS, row0=0):
    ri = row0 + _iota((rows, GDN_ROWS), 0)
    ci = _iota((rows, GDN_ROWS), 1)
    same = jnp.right_shift(ri, CHUNK_SHIFT) == jnp.right_shift(ci, CHUNK_SHIFT)
    return same, same & (ri >= ci), same & (ri > ci), ri == ci


def _col_to_row(col, eye):
    return jnp.sum(jnp.where(eye, col, 0.0), axis=0, keepdims=True)


def _row_to_col(row, eye):
    return jnp.sum(jnp.where(eye, row, 0.0), axis=1, keepdims=True)


def _chunk_common(qn, kn, g, beta, dk_scale):
    same, incl, strict, eye = _chunk_masks()
    gb = jnp.broadcast_to(g, (GDN_ROWS, LANES))
    gam = jnp.max(_dot_exact_l(jnp.where(incl, 1.0, 0.0).astype(BF16), gb), axis=1, keepdims=True)
    gam_last = jnp.max(_dot_exact_l(jnp.where(same, 1.0, 0.0).astype(BF16), gb), axis=1, keepdims=True)
    diff = gam - _col_to_row(gam, eye)
    decay = jnp.where(incl, jnp.exp(jnp.where(incl, diff, 0.0)), 0.0)
    eg = jnp.exp(gam)
    ek = jnp.exp(gam_last - gam)
    kb = kn * beta
    qt = qn * dk_scale
    both = _dot_nt(_bf(jnp.concatenate([kb, qt], axis=0)), _bf(kn))
    lmat = jnp.where(strict, both[:GDN_ROWS] * decay, 0.0)
    pmat = jnp.where(incl, both[GDN_ROWS:] * decay, 0.0)
    return dict(incl=incl, strict=strict, eye=eye, decay=decay, eg=eg, ek=ek, egl=jnp.exp(gam_last),
                kb=kb, qt=qt, lmat=lmat, pmat=pmat)


def _gdn_prep(qn, kn, v, g, beta, *, name, gather=None):
    heads, t, dk = qn.shape
    dv = v.shape[2]
    rows = _pick(t, (3 * GDN_ROWS, 2 * GDN_ROWS, GDN_ROWS))
    dk_scale = dk ** -0.5

    def body(q_ref, k_ref, v_ref, g_ref, b_ref, u_ref, w_ref, p_ref, qd_ref, kd_ref, egl_ref, t_ref):
        for b in range(rows // GDN_ROWS):
            r = pl.ds(b * GDN_ROWS, GDN_ROWS)
            kn_, beta_ = k_ref[r, :], b_ref[r, :]
            c = _chunk_common(q_ref[r, :], kn_, g_ref[r, :], beta_, dk_scale)
            eye_f = jnp.where(c["eye"], 1.0, 0.0)
            tinv = eye_f - c["lmat"]
            y = _dot_hp(c["lmat"], c["lmat"])
            for _ in range(CHUNK_SHIFT - 1):
                both = _dot_hp(y, jnp.concatenate([y, tinv], axis=1))
                y, tinv = both[:, :GDN_ROWS], tinv + both[:, GDN_ROWS:]
            uw =_dot_hp(tinv, jnp.concatenate([v_ref[r, :] * beta_, c["kb"] * c["eg"]], axis=1))
            u_ref[r, :] = uw[:, :dv]
            w_ref[r, :] = uw[:, dv:]
            p_ref[r, :] = c["pmat"]
            qd_ref[r, :] = c["qt"] * c["eg"]
            kd_ref[r, :] = kn_ * c["ek"]
            egl_ref[r, :] = c["egl"]
            t_ref[r, :] = tinv

    def blk(width):
        return pl.BlockSpec((None, rows, width), lambda h, i: (h, i, 0))

    def shp(width):
        return jax.ShapeDtypeStruct((heads, t, width), F32)

    call = dict(grid=(heads, t // rows), in_specs=[blk(dk), blk(dk), blk(dv), blk(1), blk(1)],
                out_specs=[blk(dv), blk(dk), blk(GDN_ROWS), blk(dk), blk(dk), blk(1), blk(GDN_ROWS)],
                out_shape=[shp(dv), shp(dk), shp(GDN_ROWS), shp(dk), shp(dk), shp(1), shp(GDN_ROWS)], name=name)
    if gather is None:
        return pl.pallas_call(body, compiler_params=_params("parallel", "parallel"), **call)(qn, kn, v, g, beta)
    res = _call_with_exchange(body, scratch_shapes=[], args=(qn, kn, v, g, beta), srcs=gather,
                              scatter=[False] * len(gather), **call)
    return [*res[:7], list(res[7:])]


def _gdn_scan(u, w, p, qd, kd, egl, proj, gain, *, col_z, name):
    heads, t, dv = u.shape
    dk = w.shape[2]
    nb = t // GDN_ROWS
    sub = GDN_ROWS // CHUNK
    hp = SCAN_HEADS

    def body(u_ref, w_ref, p_ref, qd_ref, kd_ref, egl_ref, z_ref, gn_ref, o_ref, og_ref, st_ref, s_ref):
        @pl.when(pl.program_id(1) == 0)
        def _():
            s_ref[...] = jnp.zeros_like(s_ref)

        for hh in range(hp):
            cols = pl.ds(hh * dv, dv)
            vn_parts = [jnp.zeros((CHUNK, dv), F32)] * sub
            for c in range(sub):
                r = pl.ds(c * CHUNK, CHUNK)
                s = s_ref[hh]
                st_ref[hh, c] = s
                sb = _bf(s)
                ws = _dot(_bf(jnp.concatenate([w_ref[hh, r, :], qd_ref[hh, r, :]], axis=0)), sb)
                vn = u_ref[hh, r, :] - ws[:CHUNK]
                vn_parts[c] = vn
                vfull = _bf(jnp.concatenate(vn_parts, axis=0))
                o = ws[CHUNK:] + _dot(_bf(p_ref[hh, r, :]), vfull)
                gl = egl_ref[hh, pl.ds(c * CHUNK, 1), :]
                s_ref[hh] = s * gl + _dot_tn(_bf(kd_ref[hh, r, :]), _bf(vn))
                o_ref[hh, r, :] = o
                og_ref[r, cols] = _bf(_rms_fwd(o, gn_ref[...]) * _silu(z_ref[r, cols]))

    def blk(width):
        return pl.BlockSpec((hp, GDN_ROWS, width), lambda h, i: (h, i, 0))

    return pl.pallas_call(
        body, grid=(heads // hp, nb),
        in_specs=[blk(dv), blk(dk), blk(GDN_ROWS), blk(dk), blk(dk), blk(1),
                  pl.BlockSpec((GDN_ROWS, hp * dv), lambda h, i: (i, col_z // (hp * dv) + h)),
                  pl.BlockSpec((1, dv), lambda h, i: (0, 0))],
        out_specs=[blk(dv), pl.BlockSpec((GDN_ROWS, hp * dv), lambda h, i: (i, h)),
                   pl.BlockSpec((hp, sub, dk, dv), lambda h, i: (h, i, 0, 0))],
        out_shape=[jax.ShapeDtypeStruct((heads, t, dv), F32), jax.ShapeDtypeStruct((t, heads * dv), BF16),
                   jax.ShapeDtypeStruct((heads, t // CHUNK, dk, dv), F32)],
        scratch_shapes=[pltpu.VMEM((hp, dk, dv), F32)], name=name,
        compiler_params=_params("parallel", "arbitrary"))(u, w, p, qd, kd, egl, proj, gain)


def _gdn_post_bwd(o, proj, gain, dout, *, col_z, name):
    heads, t, dv = o.shape
    tm = _pick(t, (384, 256, 128))

    def body(o_ref, z_ref, gn_ref, d_ref, do_ref, dz_ref, dg_ref):
        @pl.when((pl.program_id(0) == 0) & (pl.program_id(1) == 0))
        def _():
            dg_ref[...] = jnp.zeros_like(dg_ref)

        o_, z, d = o_ref[...], z_ref[...], d_ref[...]
        y = _rms_fwd(o_, gn_ref[...])
        dz_ref[...] = _bf(d * y * _silu_grad(z))
        do, dgr = _rms_bwd(o_, gn_ref[...], d * _silu(z))
        do_ref[...] = do
        dg_ref[...] += jnp.sum(dgr, axis=0, keepdims=True)

    return pl.pallas_call(
        body, grid=(t // tm, heads),
        in_specs=[pl.BlockSpec((None, tm, dv), lambda i, h: (h, i, 0)),
                  pl.BlockSpec((tm, dv), lambda i, h: (i, col_z // dv + h)),
                  pl.BlockSpec((1, dv), lambda i, h: (0, 0)), pl.BlockSpec((tm, dv), lambda i, h: (i, h))],
        out_specs=[pl.BlockSpec((None, tm, dv), lambda i, h: (h, i, 0)), pl.BlockSpec((tm, dv), lambda i, h: (i, h)),
                   pl.BlockSpec((1, dv), lambda i, h: (0, 0))],
        out_shape=[jax.ShapeDtypeStruct((heads, t, dv), F32), jax.ShapeDtypeStruct((t, heads * dv), BF16),
                   jax.ShapeDtypeStruct((1, dv), F32)], name=name,
        compiler_params=_params("arbitrary", "arbitrary"))(o, proj, gain, dout)


def _gdn_bwd_scan(u, w, p, qd, kd, egl, st, do, *, name):
    heads, t, dv = u.shape
    dk = w.shape[2]
    nb = t // GDN_ROWS
    sub = GDN_ROWS // CHUNK
    hp = SCAN_HEADS

    def body(u_ref, w_ref, p_ref, qd_ref, kd_ref, egl_ref, st_ref, do_ref,
             du_ref, dw_ref, dp_ref, dqd_ref, dkd_ref, dgl_ref, ds_ref):
        @pl.when(pl.program_id(1) == 0)
        def _():
            ds_ref[...] = jnp.zeros_like(ds_ref)

        for hh in range(hp):
            for c in reversed(range(sub)):
                r = pl.ds(c * CHUNK, CHUNK)
                s = st_ref[hh, c]
                sb = _bf(s)
                ds = ds_ref[hh]
                dsb = _bf(ds)
                dob = _bf(do_ref[hh, r, :])
                wb, pb = _bf(w_ref[hh, r, :]), _bf(p_ref[hh, r, :])
                qdb, kdb = _bf(qd_ref[hh, r, :]), _bf(kd_ref[hh, r, :])
                vn = u_ref[hh, r, :] - _dot(wb, sb)
                zeros = jnp.zeros((CHUNK, dv), BF16)
                vfull = jnp.concatenate([_bf(vn) if cc == c else zeros for cc in range(sub)], axis=0)
                dvn = _dot_tn(pb, dob)[c * CHUNK:(c + 1) * CHUNK, :] + _dot(kdb, dsb)
                dvb = _bf(dvn)
                gl = egl_ref[hh, pl.ds(c * CHUNK, 1), :]
                dod = jnp.concatenate([dob, dvb], axis=0)
                both = _dot_nt(dod, sb)
                du_ref[hh, r, :] = dvn
                dw_ref[hh, r, :] = -both[CHUNK:]
                dp_ref[hh, r, :] = jnp.where(_chunk_masks(CHUNK, c * CHUNK)[1], _dot_nt(dob, vfull), 0.0)
                dqd_ref[hh, r, :] = both[:CHUNK]
                dkd_ref[hh, r, :] = _dot_nt(_bf(vn), dsb)
                dgl = jnp.sum(jnp.sum(ds * s, axis=1, keepdims=True), axis=0, keepdims=True)
                dgl_ref[hh, r, :] = jnp.where(_iota((CHUNK, 1), 0) == CHUNK - 1, dgl, 0.0)
                ds_ref[hh] = ds * gl + _dot_tn(jnp.concatenate([qdb, -wb], axis=0), dod)

    def blk(width):
        return pl.BlockSpec((hp, GDN_ROWS, width), lambda h, i: (h, nb - 1 - i, 0))

    def shp(width):
        return jax.ShapeDtypeStruct((heads, t, width), F32)

    return pl.pallas_call(
        body, grid=(heads // hp, nb),
        in_specs=[blk(dv), blk(dk), blk(GDN_ROWS), blk(dk), blk(dk), blk(1),
                  pl.BlockSpec((hp, sub, dk, dv), lambda h, i: (h, nb - 1 - i, 0, 0)), blk(dv)],
        out_specs=[blk(dv), blk(dk), blk(GDN_ROWS), blk(dk), blk(dk), blk(1)],
        out_shape=[shp(dv), shp(dk), shp(GDN_ROWS), shp(dk), shp(dk), shp(1)],
        scratch_shapes=[pltpu.VMEM((hp, dk, dv), F32)], name=name,
        compiler_params=_params("parallel", "arbitrary"))(u, w, p, qd, kd, egl, st, do)


def _gdn_bwd_prep(qn, kn, v, g, beta, tinv, u, w, du, dw, dp, dqd, dkd, dgl, *, name):
    heads, t, dk = qn.shape
    dv = v.shape[2]
    rows = _pick(t, (3 * GDN_ROWS, 2 * GDN_ROWS, GDN_ROWS))
    dk_scale = dk ** -0.5

    def rowsum(x):
        return jnp.sum(x, axis=1, keepdims=True)

    def body(q_ref, k_ref, v_ref, g_ref, b_ref, t_ref, u_ref, w_ref, du_ref, dw_ref, dp_ref, dqd_ref, dkd_ref, dgl_ref,
             dq_ref, dkk_ref, dvv_ref, dg_ref, db_ref):
        for b in range(rows // GDN_ROWS):
            r = pl.ds(b * GDN_ROWS, GDN_ROWS)
            kn_, beta_, v_ = k_ref[r, :], b_ref[r, :], v_ref[r, :]
            c = _chunk_common(q_ref[r, :], kn_, g_ref[r, :], beta_, dk_scale)
            eye, strict, decay = c["eye"], c["strict"], c["decay"]
            kb, qt, eg, ek = c["kb"], c["qt"], c["eg"], c["ek"]
            tinv_ = t_ref[r, :]
            dbvw = _dot_hp(tinv_, jnp.concatenate([du_ref[r, :], dw_ref[r, :]], axis=1), _dot_tn)
            dbv, dbw = dbvw[:, :dv], dbvw[:, dv:]
            da = -_dot_nt(_bf(dbvw), _bf(jnp.concatenate([u_ref[r, :], w_ref[r, :]], axis=1)))
            dl = jnp.where(strict, da, 0.0)
            dp_ = dp_ref[r, :]
            dmn = _bf(jnp.concatenate([dl * decay, dp_ * decay], axis=0))
            both = _dot(dmn, _bf(kn_))
            dkb = both[:GDN_ROWS] + dbw * eg
            dqt = both[GDN_ROWS:]
            dkn = _dot_tn(dmn, _bf(jnp.concatenate([kb, qt], axis=0)))
            gmat = dl * c["lmat"] + dp_ * c["pmat"]
            dqd_, dkd_ = dqd_ref[r, :], dkd_ref[r, :]
            qd = qt * eg
            kd = kn_ * ek
            bw = kb * eg
            kdsum = rowsum(dkd_ * kd)
            dgam = rowsum(gmat) - _row_to_col(jnp.sum(gmat, axis=0, keepdims=True), eye)
            dgam += rowsum(dbw * bw) + rowsum(dqd_ * qd) - kdsum
            last = (_iota((GDN_ROWS, 1), 0) & (CHUNK - 1)) == CHUNK - 1
            same = _chunk_masks()[0]
            same_f = jnp.where(same, 1.0, 0.0).astype(BF16)
            chunk_tot = jnp.max(_dot_exact_l(same_f, jnp.broadcast_to(kdsum, (GDN_ROWS, LANES))), axis=1, keepdims=True)
            dgam += jnp.where(last, chunk_tot, 0.0) + dgl_ref[r, :] * c["egl"]
            dq_ref[r, :] = (dqt + dqd_ * eg) * dk_scale
            dkk_ref[r, :] = dkn + dkd_ * ek + dkb * beta_
            dvv_ref[r, :] = dbv * beta_
            db_ref[r, :] = rowsum(dbv * v_) + rowsum(dkb * kn_)
            upper = jnp.where(same & (_iota((GDN_ROWS, GDN_ROWS), 0) <= _iota((GDN_ROWS, GDN_ROWS), 1)), 1.0, 0.0)
            dgb = _dot_exact_l(upper.astype(BF16), jnp.broadcast_to(dgam, (GDN_ROWS, LANES)))
            dg_ref[r, :] = _lane_pick(dgb, 0)

    def blk(width):
        return pl.BlockSpec((None, rows, width), lambda h, i: (h, i, 0))

    def shp(width):
        return jax.ShapeDtypeStruct((heads, t, width), F32)

    return pl.pallas_call(
        body, grid=(heads, t // rows),
        in_specs=[blk(dk), blk(dk), blk(dv), blk(1), blk(1), blk(GDN_ROWS), blk(dv), blk(dk),
                  blk(dv), blk(dk), blk(GDN_ROWS), blk(dk), blk(dk), blk(1)],
        out_specs=[blk(dk), blk(dk), blk(dv), blk(1), blk(1)],
        out_shape=[shp(dk), shp(dk), shp(dv), shp(1), shp(1)], name=name,
        compiler_params=_params("parallel", "parallel"))(qn, kn, v, g, beta, tinv, u, w, du, dw, dp, dqd, dkd, dgl)


def _gdn_pre_bwd_a(proj, pab, cq, ck, cv, a_log, dt_bias, dqn, dkn, dvv, dg, dbeta, *,
                   heads, dk, dv, col_q, col_k, col_v, row_lo, row_hi, name):
    t = proj.shape[0]
    tm = _pick(t, (384, 256, 128))
    nb = t // tm

    def body(pq_ref, pqp_ref, pk_ref, pkp_ref, pv_ref, pvp_ref, ab_ref, cq_ref, ck_ref, cv_ref, al_ref, dt_ref,
             dqn_ref, dkn_ref, dvv_ref, dg_ref, db_ref, dcq_ref, dck_ref, dcv_ref, dab_ref, dal_ref, ddt_ref):
        i, h = pl.program_id(0), pl.program_id(1)
        first = i == 0

        @pl.when((i == 0) & (h == 0))
        def _():
            dal_ref[...] = jnp.zeros_like(dal_ref)
            ddt_ref[...] = jnp.zeros_like(ddt_ref)

        @pl.when(h == 0)
        def _():
            dab_ref[...] = jnp.zeros_like(dab_ref)

        row = i * tm + _iota((tm, 1), 0)
        valid = (row >= row_lo) & (row < row_hi)

        def l2_bwd(c1, dn):
            x1 = _silu(c1)
            r = lax.rsqrt(jnp.sum(x1 * x1, axis=-1, keepdims=True) + L2_EPS)
            dn = jnp.where(valid, dn, 0.0)
            d1 = r * dn - x1 * (r * r * r) * jnp.sum(dn * x1, axis=-1, keepdims=True)
            return d1 * _silu_grad(c1)

        dcq_ref[...] = l2_bwd(_conv_taps(pq_ref[...], pqp_ref[...], cq_ref, first), dqn_ref[...])
        dck_ref[...] = l2_bwd(_conv_taps(pk_ref[...], pkp_ref[...], ck_ref, first), dkn_ref[...])
        cv1 = _conv_taps(pv_ref[...], pvp_ref[...], cv_ref, first)
        dcv_ref[...] = jnp.where(valid, dvv_ref[...], 0.0) * _silu_grad(cv1)
        ab = ab_ref[...]
        da = _lane_pick(ab, h)
        db = _lane_pick(ab, heads + h)
        a = _lane_pick(al_ref[...], h)
        dtb = _lane_pick(dt_ref[...], h)
        dgv = jnp.where(valid, dg_ref[...], 0.0)
        ea = jnp.exp(a)
        g = -ea * _softplus(da + dtb)
        dda = dgv * (-ea) * _sigmoid(da + dtb)
        beta = _sigmoid(db)
        ddb = jnp.where(valid, db_ref[...], 0.0) * beta * (1.0 - beta)
        lane = _iota((tm, LANES), 1)
        dab_ref[...] += jnp.where(lane == h, dda, 0.0) + jnp.where(lane == heads + h, ddb, 0.0)
        lane1 = _iota((1, LANES), 1)
        dal_ref[...] += jnp.where(lane1 == h, jnp.sum(dgv * g, axis=0, keepdims=True), 0.0)
        ddt_ref[...] += jnp.where(lane1 == h, jnp.sum(dda, axis=0, keepdims=True), 0.0)

    def cur(width, col):
        return pl.BlockSpec((tm, width), lambda i, h: (i, col // width + h))

    def prev(width, col):
        return pl.BlockSpec((8, width), lambda i, h: (jnp.maximum(i * (tm // 8) - 1, 0), col // width + h))

    def hd(width):
        return pl.BlockSpec((None, tm, width), lambda i, h: (h, i, 0))

    small = pl.BlockSpec((1, LANES), lambda i, h: (0, 0))
    return pl.pallas_call(
        body, grid=(nb, heads),
        in_specs=[cur(dk, col_q), prev(dk, col_q), cur(dk, col_k), prev(dk, col_k), cur(dv, col_v), prev(dv, col_v),
                  pl.BlockSpec((tm, LANES), lambda i, h: (i, 0)),
                  pl.BlockSpec((cq.shape[0], dk), lambda i, h: (0, h)), pl.BlockSpec((ck.shape[0], dk), lambda i, h: (0, h)),
                  pl.BlockSpec((cv.shape[0], dv), lambda i, h: (0, h)), small, small,
                  hd(dk), hd(dk), hd(dv), hd(1), hd(1)],
        out_specs=[hd(dk), hd(dk), hd(dv), pl.BlockSpec((tm, LANES), lambda i, h: (i, 0)), small, small],
        out_shape=[jax.ShapeDtypeStruct((heads, t, dk), F32), jax.ShapeDtypeStruct((heads, t, dk), F32),
                   jax.ShapeDtypeStruct((heads, t, dv), F32), jax.ShapeDtypeStruct((t, LANES), F32),
                   jax.ShapeDtypeStruct((1, LANES), F32), jax.ShapeDtypeStruct((1, LANES), F32)], name=name,
        compiler_params=_params("arbitrary", "arbitrary"))(
            proj, proj, proj, proj, proj, proj, pab, cq, ck, cv, a_log, dt_bias, dqn, dkn, dvv, dg, dbeta)


def _conv_bwd(proj, dc, cw, *, heads, width, col, name):
    t = proj.shape[0]
    tm = _pick(t, (384, 256, 128))
    nb = t // tm
    nk = cw.shape[0]

    def body(p_ref, pp_ref, d_ref, dn_ref, w_ref, dp_ref, dw_ref):
        i = pl.program_id(1)
        first, last = i == 0, i == nb - 1

        @pl.when(first)
        def _():
            dw_ref[...] = jnp.zeros_like(dw_ref)

        x, d = p_ref[...], d_ref[...]
        dx = d * w_ref[nk - 1:nk, :]
        dw_ref[nk - 1:nk, :] += jnp.sum(d * x, axis=0, keepdims=True)
        for s in range(1, nk):
            dx += _shift_up(d, dn_ref[...], s, last) * w_ref[nk - 1 - s:nk - s, :]
            dw_ref[nk - 1 - s:nk - s, :] += jnp.sum(d * _shift_down(x, pp_ref[...], s, first), axis=0, keepdims=True)
        dp_ref[...] = _bf(dx)

    return pl.pallas_call(
        body, grid=(heads, nb),
        in_specs=[pl.BlockSpec((tm, width), lambda h, i: (i, col // width + h)),
                  pl.BlockSpec((8, width), lambda h, i: (jnp.maximum(i * (tm // 8) - 1, 0), col // width + h)),
                  pl.BlockSpec((None, tm, width), lambda h, i: (h, i, 0)),
                  pl.BlockSpec((None, 8, width), lambda h, i: (h, jnp.minimum((i + 1) * (tm // 8), t // 8 - 1), 0)),
                  pl.BlockSpec((nk, width), lambda h, i: (0, h))],
        out_specs=[pl.BlockSpec((tm, width), lambda h, i: (i, h)), pl.BlockSpec((nk, width), lambda h, i: (0, h))],
        out_shape=[jax.ShapeDtypeStruct((t, heads * width), BF16), jax.ShapeDtypeStruct((nk, heads * width), F32)],
        name=name, compiler_params=_params("parallel", "arbitrary"))(proj, proj, dc, dc, cw)


def _sb_pre(proj, gq, gk, *, heads, dh, col_q, col_k, col_v, name):
    t = proj.shape[0]
    tm = _pick(t, (384, 256, 128))

    def body(q_ref, k_ref, v_ref, gq_ref, gk_ref, qo_ref, ko_ref, vo_ref):
        qo_ref[...] = _bf(_rms_fwd(q_ref[...], gq_ref[...]))
        ko_ref[...] = _bf(_rms_fwd(k_ref[...], gk_ref[...]))
        vo_ref[...] = _bf(v_ref[...])

    def cur(col):
        return pl.BlockSpec((tm, dh), lambda i, h: (i, col // dh + h))

    gspec = pl.BlockSpec((1, dh), lambda i, h: (0, 0))
    ospec = pl.BlockSpec((tm, dh), lambda i, h: (i, h))
    return pl.pallas_call(
        body, grid=(t // tm, heads), in_specs=[cur(col_q), cur(col_k), cur(col_v), gspec, gspec],
        out_specs=[ospec] * 3, out_shape=[jax.ShapeDtypeStruct((t, heads * dh), BF16)] * 3, name=name,
        compiler_params=_params("parallel", "parallel"))(proj, proj, proj, gq, gk)


def _sb_tile(q, kb, i, j, blk, key_lo, scale):
    z = _dot_nt(q, kb) * scale
    qpos = i * blk + _iota((blk, blk), 0)
    kpos = j * blk + _iota((blk, blk), 1)
    vis = (kpos < qpos) & (kpos >= key_lo)
    ls = jnp.minimum(z, 0.0) - jnp.log(1.0 + jnp.exp(-jnp.abs(z)))
    return vis, ls, jnp.where(vis, ls - z, 0.0)


def _dot2_r(x, m):
    hi, lo = _split2(x)
    return _dot(hi, m) + _dot(lo, m)


def _suffix_sums(x, later):
    return jnp.concatenate([_dot2_r(x[:, s:], later[s:, s:s + LANES]) for s in range(0, x.shape[1], LANES)], axis=1)


def _prefix_sums(x, earlier):
    return jnp.concatenate([_dot2_r(x[:, :s + LANES], earlier[:s + LANES, s:s + LANES])
                            for s in range(0, x.shape[1], LANES)], axis=1)


def _sb_fwd(qs, ks, vs, *, heads, dh, key_lo, name):
    t = qs.shape[0]
    blk = _pick(t, (3 * SB_BLOCK, 2 * SB_BLOCK, SB_BLOCK))
    nq = t // blk
    assert nq <= LANES
    scale = dh ** -0.5
    hp = SB_HEADS_PER_STEP

    def body(q_ref, k_ref, v_ref, o_ref, c_ref):
        i = pl.program_id(1)
        later = jnp.where(_iota((blk, blk), 0) > _iota((blk, blk), 1), 1.0, 0.0).astype(BF16)
        lane = _iota((blk, LANES), 1)
        c_ref[...] = jnp.zeros_like(c_ref)

        def step(n, carry):
            j = i - n
            rows = pl.ds(pl.multiple_of(j * blk, blk), blk)
            out = []
            for hh in range(hp):
                cols = pl.ds(hh * dh, dh)
                acc, run = carry[2 * hh], carry[2 * hh + 1]
                vis, ls, lk = _sb_tile(q_ref[:, cols], k_ref[rows, cols], i, j, blk, key_lo, scale)
                wgt = jnp.where(vis, jnp.exp(ls + _suffix_sums(lk, later) + run), 0.0)
                c_ref[hh] = jnp.where(lane == j, run, c_ref[hh])
                out += [acc + _dot(_bf(wgt), v_ref[rows, cols]), run + jnp.sum(lk, axis=1, keepdims=True)]
            return tuple(out)

        res = lax.fori_loop(0, i + 1, step, (jnp.zeros((blk, dh), F32), jnp.zeros((blk, 1), F32)) * hp)
        for hh in range(hp):
            o_ref[:, pl.ds(hh * dh, dh)] = _bf(res[2 * hh])

    full = pl.BlockSpec((t, hp * dh), lambda h, i: (0, h))
    return pl.pallas_call(
        body, grid=(heads // hp, nq),
        in_specs=[pl.BlockSpec((blk, hp * dh), lambda h, i: (i, h)), full, full],
        out_specs=[pl.BlockSpec((blk, hp * dh), lambda h, i: (i, h)),
                   pl.BlockSpec((hp, blk, LANES), lambda h, i: (h, i, 0))],
        out_shape=[jax.ShapeDtypeStruct((t, heads * dh), BF16), jax.ShapeDtypeStruct((heads, t, LANES), F32)],
        name=name, compiler_params=_params("parallel", "parallel"))(qs, ks, vs)


def _sb_bwd(qs, ks, vs, do, carry, *, heads, dh, key_lo, name, scatter=None):
    t = qs.shape[0]
    blk = _pick(t, (3 * SB_BLOCK, 2 * SB_BLOCK, SB_BLOCK))
    nq = t // blk
    scale = dh ** -0.5
    hp = SB_HEADS_PER_STEP

    def body(q_ref, k_ref, v_ref, do_ref, c_ref, dq_ref, dk_ref, dv_ref):
        i = pl.program_id(1)

        @pl.when(i == 0)
        def _():
            dk_ref[...] = jnp.zeros_like(dk_ref)
            dv_ref[...] = jnp.zeros_like(dv_ref)

        r0 = _iota((blk, blk), 0)
        r1 = _iota((blk, blk), 1)
        later = jnp.where(r0 > r1, 1.0, 0.0).astype(BF16)
        earlier = jnp.where(r0 < r1, 1.0, 0.0).astype(BF16)

        def step(j, carry):
            rows = pl.ds(pl.multiple_of(j * blk, blk), blk)
            out = []
            for hh in range(hp):
                cols = pl.ds(hh * dh, dh)
                dq, pre = carry[2 * hh], carry[2 * hh + 1]
                q, dob, kb, vb = q_ref[:, cols], do_ref[:, cols], k_ref[rows, cols], v_ref[rows, cols]
                vis, ls, lk = _sb_tile(q, kb, i, j, blk, key_lo, scale)
                wgt = jnp.where(vis, jnp.exp(ls + _suffix_sums(lk, later) + _lane_pick(c_ref[hh], j)), 0.0)
                e = wgt * _dot_nt(dob, vb)
                before = jnp.where(vis, _prefix_sums(e, earlier) + pre, 0.0)
                sig = jnp.exp(ls)
                dz = _bf((e * (1.0 - sig) - before * sig) * scale)
                dk_ref[rows, cols] += _dot_tn(dz, q)
                dv_ref[rows, cols] += _dot_tn(_bf(wgt), dob)
                out += [dq + _dot(dz, kb), pre + jnp.sum(e, axis=1, keepdims=True)]
            return tuple(out)

        res = lax.fori_loop(0, i + 1, step, (jnp.zeros((blk, dh), F32), jnp.zeros((blk, 1), F32)) * hp)
        for hh in range(hp):
            dq_ref[:, pl.ds(hh * dh, dh)] = res[2 * hh]

    full = pl.BlockSpec((t, hp * dh), lambda h, i: (0, h))
    qblk = pl.BlockSpec((blk, hp * dh), lambda h, i: (i, h))
    call = dict(grid=(heads // hp, nq),
                in_specs=[qblk, full, full, qblk, pl.BlockSpec((hp, blk, LANES), lambda h, i: (h, i, 0))],
                out_specs=[qblk, full, full], out_shape=[jax.ShapeDtypeStruct((t, heads * dh), F32)] * 3, name=name)
    if scatter is None:
        return pl.pallas_call(body, compiler_params=_params("parallel", "arbitrary"), **call)(qs, ks, vs, do, carry)
    res = _call_with_exchange(body, scratch_shapes=[], args=(qs, ks, vs, do, carry), srcs=scatter,
                              scatter=[True] * len(scatter), **call)
    return [*res[:3], list(res[3:])]


def _sb_pre_bwd(proj, gq, gk, dq, dk, dv, *, heads, dh, col_q, col_k, name):
    t = proj.shape[0]
    tm = _pick(t, (384, 256, 128))

    def body(q_ref, k_ref, gq_ref, gk_ref, dq_ref, dk_ref, dv_ref, oq_ref, ok_ref, ov_ref, dgq_ref, dgk_ref):
        @pl.when((pl.program_id(0) == 0) & (pl.program_id(1) == 0))
        def _():
            dgq_ref[...] = jnp.zeros_like(dgq_ref)
            dgk_ref[...] = jnp.zeros_like(dgk_ref)

        dq_, gq_r = _rms_bwd(q_ref[...], gq_ref[...], dq_ref[...])
        dk_, gk_r = _rms_bwd(k_ref[...], gk_ref[...], dk_ref[...])
        oq_ref[...] = _bf(dq_)
        ok_ref[...] = _bf(dk_)
        ov_ref[...] = _bf(dv_ref[...])
        dgq_ref[...] += jnp.sum(gq_r, axis=0, keepdims=True)
        dgk_ref[...] += jnp.sum(gk_r, axis=0, keepdims=True)

    def cur(col):
        return pl.BlockSpec((tm, dh), lambda i, h: (i, col // dh + h))

    gspec = pl.BlockSpec((1, dh), lambda i, h: (0, 0))
    ospec = pl.BlockSpec((tm, dh), lambda i, h: (i, h))
    return pl.pallas_call(
        body, grid=(t // tm, heads), in_specs=[cur(col_q), cur(col_k), gspec, gspec, ospec, ospec, ospec],
        out_specs=[ospec, ospec, ospec, gspec, gspec],
        out_shape=[jax.ShapeDtypeStruct((t, heads * dh), BF16)] * 3 + [jax.ShapeDtypeStruct((1, dh), F32)] * 2,
        name=name, compiler_params=_params("arbitrary", "arbitrary"))(proj, proj, gq, gk, dq, dk, dv)


PEERS = N_DEV - 1


def _exchange_copies(ins, outs, send_sems, recv_sems, local_sems, scatter):
    x, y, c = lax.axis_index("x"), lax.axis_index("y"), lax.axis_index("c")
    me = 4 * x + 2 * y + c
    copies = []
    for a in range(len(ins)):
        own = ins[a].at[me] if scatter[a] else ins[a]
        copies.append(pltpu.make_async_copy(own, outs[a].at[me], local_sems.at[a]))
        for k in range(1, N_DEV):
            px = (x + (k >> 2 & 1)) % 2
            py = (y + (k >> 1 & 1)) % 2
            pc = (c + (k & 1)) % 2
            src = ins[a].at[4 * px + 2 * py + pc] if scatter[a] else ins[a]
            copies.append(pltpu.make_async_remote_copy(
                src_ref=src, dst_ref=outs[a].at[me], send_sem=send_sems.at[a * PEERS + k - 1],
                recv_sem=recv_sems.at[a * PEERS + k - 1], device_id=(px, py, pc), device_id_type=MESH))
    return copies


def _exchange_shapes(srcs, scatter):
    return [jax.ShapeDtypeStruct(s.shape if sc else (N_DEV,) + s.shape, s.dtype) for s, sc in zip(srcs, scatter)]


def _exchange_sems(n):
    return [pltpu.SemaphoreType.DMA((n * PEERS,)), pltpu.SemaphoreType.DMA((n * PEERS,)), pltpu.SemaphoreType.DMA((n,))]


def _exchange(srcs, *, scatter, name):
    n = len(srcs)

    def body(*refs):
        copies = _exchange_copies(refs[:n], refs[n:2 * n], *refs[2 * n:], scatter)
        for cp in copies:
            cp.start()
        for cp in copies:
            cp.wait()

    any_spec = pl.BlockSpec(memory_space=pl.ANY)
    return pl.pallas_call(
        body, in_specs=[any_spec] * n, out_specs=[any_spec] * n, out_shape=_exchange_shapes(srcs, scatter),
        scratch_shapes=_exchange_sems(n), name=name,
        compiler_params=pltpu.CompilerParams(has_side_effects=True))(*srcs)


def _gather_two_level(srcs, *, name):
    n = len(srcs)

    def body(*refs):
        ins, outs = refs[:n], refs[n:2 * n]
        send_sems, recv_sems, local_sems = refs[2 * n:]
        x, y, c = lax.axis_index("x"), lax.axis_index("y"), lax.axis_index("c")
        chips = [(1 - x, y), (x, 1 - y), (1 - x, 1 - y)]

        def slab(a, px, py, pc):
            return outs[a].at[4 * px + 2 * py + pc]

        def copy(a, k, block, to, src=None):
            return pltpu.make_async_remote_copy(
                src_ref=slab(a, *block) if src is None else src, dst_ref=slab(a, *block),
                send_sem=send_sems.at[a * PEERS + k], recv_sem=recv_sems.at[a * PEERS + k],
                device_id=to, device_id_type=MESH)

        mine = [pltpu.make_async_copy(ins[a], slab(a, x, y, c), local_sems.at[a]) for a in range(n)]
        first = [copy(a, 0, (x, y, c), (x, y, 1 - c), src=ins[a]) for a in range(n)]
        first += [copy(a, 1 + j, (x, y, c), (*chip, c), src=ins[a]) for j, chip in enumerate(chips) for a in range(n)]
        for cp in mine + first:
            cp.start()
        passed = []
        for j, chip in enumerate(chips):
            for a in range(n):
                copy(a, 1 + j, (*chip, c), (x, y, c)).wait_recv()
                passed.append(copy(a, 4 + j, (*chip, c), (x, y, 1 - c)))
                passed[-1].start()
        for a in range(n):
            copy(a, 0, (x, y, 1 - c), (x, y, c)).wait_recv()
            for j, chip in enumerate(chips):
                copy(a, 4 + j, (*chip, 1 - c), (x, y, c)).wait_recv()
        for cp in first + passed:
            cp.wait_send()
        for cp in mine:
            cp.wait()

    any_spec = pl.BlockSpec(memory_space=pl.ANY)
    return pl.pallas_call(
        body, in_specs=[any_spec] * n, out_specs=[any_spec] * n, out_shape=_exchange_shapes(srcs, [False] * n),
        scratch_shapes=_exchange_sems(n), name=name,
        compiler_params=pltpu.CompilerParams(has_side_effects=True))(*srcs)


def _call_with_exchange(body, *, grid, in_specs, out_specs, out_shape, scratch_shapes, args, srcs, scatter, name):
    n, n_in, n_out, n_scr = len(srcs), len(args), len(out_shape), len(scratch_shapes)

    def full_body(*refs):
        ins, xin = refs[:n_in], refs[n_in:n_in + n]
        outs, xout = refs[n_in + n:n_in + n + n_out], refs[n_in + n + n_out:n_in + 2 * n + n_out]
        scr = refs[n_in + 2 * n + n_out:]
        ids = [pl.program_id(a) for a in range(len(grid))]
        first = functools.reduce(jnp.logical_and, [i == 0 for i in ids])
        last = functools.reduce(jnp.logical_and, [i == g - 1 for i, g in zip(ids, grid)])
        copies = _exchange_copies(xin, xout, *scr[n_scr:], scatter)

        @pl.when(first)
        def _():
            for cp in copies:
                cp.start()

        body(*ins, *outs, *scr[:n_scr])

        @pl.when(last)
        def _():
            for cp in copies:
                cp.wait()

    any_spec = pl.BlockSpec(memory_space=pl.ANY)
    return pl.pallas_call(
        full_body, grid=grid, in_specs=list(in_specs) + [any_spec] * n, out_specs=list(out_specs) + [any_spec] * n,
        out_shape=list(out_shape) + _exchange_shapes(srcs, scatter),
        scratch_shapes=list(scratch_shapes) + _exchange_sems(n), name=name,
        compiler_params=pltpu.CompilerParams(dimension_semantics=("arbitrary",) * len(grid),
                                             vmem_limit_bytes=V7X_VMEM_LIMIT_BYTES, has_side_effects=True))(*args, *srcs)


def _adam_math(g, w, m, v):
    m2 = ADAM_B1 * m + (1.0 - ADAM_B1) * g
    v2 = ADAM_B2 * v + (1.0 - ADAM_B2) * (g * g)
    m_hat = m2 / (1.0 - ADAM_B1 ** ADAM_STEP)
    v_hat = v2 / (1.0 - ADAM_B2 ** ADAM_STEP)
    return -ADAM_LR * (m_hat / (jnp.sqrt(v_hat) + ADAM_EPS) + ADAM_WD * w), m2, v2


def _adamw_slabs(slabs, w, m, v, *, name):
    r, c = w.shape
    tr = _pick(r, (128, 64, 32, 16, 8)) if r % 8 == 0 else r

    def body(s_ref, w_ref, m_ref, v_ref, g_ref, d_ref, mo_ref, vo_ref):
        g = s_ref[0].astype(F32)
        for p in range(1, N_DEV):
            g = g + s_ref[p].astype(F32)
        g_ref[...] = g
        d_ref[...], mo_ref[...], vo_ref[...] = _adam_math(g, w_ref[...], m_ref[...], v_ref[...])

    spec = pl.BlockSpec((tr, c), lambda i: (i, 0))
    return pl.pallas_call(
        body, grid=(r // tr,), in_specs=[pl.BlockSpec((N_DEV, tr, c), lambda i: (0, i, 0)), spec, spec, spec],
        out_specs=[spec] * 4, out_shape=[jax.ShapeDtypeStruct((r, c), F32)] * 4, name=name,
        compiler_params=_params("parallel"))(slabs, w, m, v)


def _adamw_small(g, w, m, v, *, name):
    def body(g_ref, w_ref, m_ref, v_ref, d_ref, mo_ref, vo_ref):
        d_ref[...], mo_ref[...], vo_ref[...] = _adam_math(g_ref[...], w_ref[...], m_ref[...], v_ref[...])

    return pl.pallas_call(body, out_shape=[jax.ShapeDtypeStruct(w.shape, F32)] * 3, name=name)(g, w, m, v)


def _sum_slabs(slabs, *, name):
    def body(s_ref, o_ref):
        acc = s_ref[0]
        for p in range(1, N_DEV):
            acc = acc + s_ref[p]
        o_ref[...] = acc

    return pl.pallas_call(body, out_shape=jax.ShapeDtypeStruct(slabs.shape[1:], F32), name=name)(slabs)


def _gather_cols(g):
    return jnp.transpose(g, (1, 0, 2)).reshape(g.shape[1], -1)


def _col_slabs(a):
    return jnp.transpose(a.reshape(a.shape[0], N_DEV, -1), (1, 0, 2))


def _pad_lanes(a):
    return jnp.pad(a, ((0, 0), (0, LANES - a.shape[1])))


def _local_step(x, target, meta, g_mix, w_main, w_ab, cq, ck, cv, a_log, dt_bias, g_dn, g_sbq, g_sbk, g_ffn, rest,
                shards=False):
    seq, d = x.shape
    n_meta = meta.shape[0]
    heads = a_log.shape[1]
    qk = cq.shape[1]
    dvt = cv.shape[1]
    dk, dv = qk // heads, dvt // heads
    dh = g_sbq.shape[1]
    sbw = rest[2].shape[0] * N_DEV if shards else rest[1].shape[0]
    sb_heads = sbw // dh
    pad_l = (-n_meta) % CHUNK
    row_x = pad_l + n_meta
    rows = row_x + seq
    t = -(-rows // GDN_ROWS) * GDN_ROWS
    col_q, col_k, col_v, col_z = 0, qk, 2 * qk, 2 * qk + dvt
    col_sq = 2 * qk + 2 * dvt
    col_sk, col_sv, col_gd, col_gs = col_sq + sbw, col_sq + 2 * sbw, col_sq + 3 * sbw, col_sq + 3 * sbw + d

    def rows_pad(a):
        return jnp.concatenate([jnp.zeros((row_x, d), F32), a, jnp.zeros((t - rows, d), F32)], axis=0)

    h0 = jnp.concatenate([jnp.zeros((pad_l, d), F32), meta, x, jnp.zeros((t - rows, d), F32)], axis=0)
    tgt = rows_pad(target)
    a_log_p, dt_p = _pad_lanes(a_log), _pad_lanes(dt_bias)

    proj, n1 = _mm_norm(h0, g_mix, w_main, name="proj")
    pab = _mm_nn(n1, w_ab, out_dtype=F32, name="proj_ab")
    gk = dict(heads=heads, dk=dk, dv=dv, col_q=col_q, col_k=col_k, col_v=col_v, row_lo=pad_l, row_hi=rows)
    qn, kn, vv, g, beta = _gdn_pre(proj, pab, cq, ck, cv, a_log_p, dt_p, name="gdn_pre", **gk)
    if shards:
        u, w, pm, qd, kd, egl, tinv, (g_fi, g_bd, g_bs, g_out, g_fo) = _gdn_prep(
            qn, kn, vv, g, beta, name="gdn_prep", gather=list(rest))
        w_fi = _gather_cols(g_fi)
        d_ff = w_fi.shape[1] // 2
        w_bd, w_bs, w_out, w_fg, w_fu, w_fo = (g_bd.reshape(-1, d), g_bs.reshape(-1, d), g_out.reshape(-1, d),
                                               w_fi[:, :d_ff], w_fi[:, d_ff:], g_fo.reshape(-1, d))
    else:
        u, w, pm, qd, kd, egl, tinv = _gdn_prep(qn, kn, vv, g, beta, name="gdn_prep")
        w_bd, w_bs, w_out, w_fg, w_fu, w_fo = rest
    o_raw, o_dn, states = _gdn_scan(u, w, pm, qd, kd, egl, proj, g_dn, col_z=col_z, name="gdn_scan")
    qs, ks, vs = _sb_pre(proj, g_sbq, g_sbk, heads=sb_heads, dh=dh, col_q=col_sq, col_k=col_sk, col_v=col_sv,
                         name="sb_pre")
    o_sb, carry = _sb_fwd(qs, ks, vs, heads=sb_heads, dh=dh, key_lo=pad_l, name="sb_fwd")
    merged, br_dn, br_sb = _merge_fwd(o_dn, o_sb, w_bd, w_bs, proj, col_gd=col_gd, col_gs=col_gs, name="merge")
    h1 = _mm_res(h0, merged, w_out, name="mix_out")
    gate, up, act, n2 = _mm_norm_swiglu(h1, g_ffn, w_fg, w_fu, name="ffn_in")
    dy, dyb, lsum = _mm_res_loss(h1, act, w_fo, tgt, row0=row_x, nrows=seq, name="ffn_out_loss")

    dgate, dup = _swiglu_bwd(dyb, w_fo, gate, up, name="ffn_out_bwd")
    d_w_fo = _mm_tn(act, dyb, name="dw_ffn_out")
    d_w_fg = _mm_tn(n2, dgate, name="dw_ffn_gate")
    d_w_fu = _mm_tn(n2, dup, name="dw_ffn_up")
    dh1, dh1b, d_g_ffn = _mm_nt_rmsbwd([(dgate, w_fg), (dup, w_fu)], None, h1, g_ffn, dy, name="ffn_in_bwd")

    dbd, dbs, dgd, dgs = _merge_bwd(dh1b, w_out, proj, br_dn, br_sb, col_gd=col_gd, col_gs=col_gs, name="mix_out_bwd")
    d_w_out = _mm_tn(merged, dh1b, name="dw_out")
    d_w_bd = _mm_tn(o_dn, dbd, name="dw_branch_dn")
    d_w_bs = _mm_tn(o_sb, dbs, name="dw_branch_sb")
    do_dn = _mm_nt(dbd, w_bd, out_dtype=F32, name="branch_dn_bwd")
    do_sb = _mm_nt(dbs, w_bs, out_dtype=BF16, name="branch_sb_bwd")

    do_raw, dz, d_g_dn = _gdn_post_bwd(o_raw, proj, g_dn, do_dn, col_z=col_z, name="gdn_post_bwd")
    du, dw, dp, dqd, dkd, dgl = _gdn_bwd_scan(u, w, pm, qd, kd, egl, states, do_raw, name="gdn_bwd_scan")
    dqn, dkn, dvv, dg, dbeta = _gdn_bwd_prep(qn, kn, vv, g, beta, tinv, u, w, du, dw, dp, dqd, dkd, dgl,
                                            name="gdn_bwd_prep")
    dcq, dck, dcv, dpab, d_a_log, d_dt = _gdn_pre_bwd_a(proj, pab, cq, ck, cv, a_log_p, dt_p, dqn, dkn, dvv, dg, dbeta,
                                                        name="gdn_pre_bwd", **gk)
    dpq, d_cq = _conv_bwd(proj, dcq, cq, heads=heads, width=dk, col=col_q, name="conv_q_bwd")
    dpk, d_ck = _conv_bwd(proj, dck, ck, heads=heads, width=dk, col=col_k, name="conv_k_bwd")
    dpv, d_cv = _conv_bwd(proj, dcv, cv, heads=heads, width=dv, col=col_v, name="conv_v_bwd")

    early = None
    if shards:
        slabs = [_col_slabs(_bf(jnp.concatenate([d_w_fg, d_w_fu], axis=1)))]
        slabs += [_bf(a).reshape(N_DEV, -1, d) for a in (d_w_bd, d_w_bs, d_w_out, d_w_fo)]
        dqs, dks, dvs, early = _sb_bwd(qs, ks, vs, do_sb, carry, heads=sb_heads, dh=dh, key_lo=pad_l, name="sb_bwd",
                                       scatter=slabs)
    else:
        dqs, dks, dvs = _sb_bwd(qs, ks, vs, do_sb, carry, heads=sb_heads, dh=dh, key_lo=pad_l, name="sb_bwd")
    dsq, dsk, dsv, d_g_sbq, d_g_sbk = _sb_pre_bwd(proj, g_sbq, g_sbk, dqs, dks, dvs, heads=sb_heads, dh=dh,
                                                   col_q=col_sq, col_k=col_sk, name="sb_pre_bwd")

    dproj = jnp.concatenate([dpq, dpk, dpv, dz, dsq, dsk, dsv, dgd, dgs], axis=1)
    dpab_b = _bf(dpab)
    d_w_main = _mm_tn(n1, dproj, name="dw_in_main")
    d_w_ab = _mm_tn(n1, dpab_b, name="dw_in_ab")
    s_in = None
    if shards:
        d_w_in = jnp.concatenate([d_w_main[:, :col_sq], d_w_ab[:, :2 * heads], d_w_main[:, col_sq:]], axis=1)
        dh0, _, d_g_mix, (s_in,) = _mm_nt_rmsbwd([(dproj, w_main)], (dpab_b, w_ab), h0, g_mix, dh1, name="proj_bwd",
                                                 scatter=[_col_slabs(_bf(d_w_in))])
    else:
        dh0, _, d_g_mix = _mm_nt_rmsbwd([(dproj, w_main)], (dpab_b, w_ab), h0, g_mix, dh1, name="proj_bwd")

    return dict(s_in=s_in,lsum=lsum, grad_x=dh0[row_x:rows], d_meta=dh0[pad_l:row_x], d_g_mix=d_g_mix, d_w_main=d_w_main,
                d_w_ab=d_w_ab, d_cq=d_cq, d_ck=d_ck, d_cv=d_cv, d_a_log=d_a_log[:, :heads], d_dt=d_dt[:, :heads],
                d_g_dn=d_g_dn, d_g_sbq=d_g_sbq, d_g_sbk=d_g_sbk, d_w_bd=d_w_bd, d_w_bs=d_w_bs, d_w_out=d_w_out,
                d_g_ffn=d_g_ffn, d_w_fg=d_w_fg, d_w_fu=d_w_fu, d_w_fo=d_w_fo, early=early)


def _pack(parts):
    flat = []
    for a in parts:
        a = a.reshape(-1)
        flat.append(jnp.pad(a, (0, (-a.shape[0]) % LANES)))
    v = jnp.concatenate(flat)
    v = jnp.pad(v, (0, (-v.shape[0]) % (8 * LANES)))
    return v.reshape(-1, LANES)


def _unpack(packed, shapes):
    flat = packed.reshape(-1)
    out, pos = [], 0
    for s in shapes:
        n = math.prod(s)
        out.append(flat[pos:pos + n].reshape(s))
        pos += n + (-n) % LANES
    return out


def kernel(x, meta_tokens, norm_mix_gain, w_in, conv_q, conv_k, conv_v, dn_a_log, dn_dt_bias, dn_out_norm_gain, sb_q_norm_gain, sb_k_norm_gain, w_branch_dn, w_branch_sb, w_out, norm_ffn_gain, w_ffn_in, w_ffn_out, loss_target, m_meta_tokens, m_norm_mix_gain, m_w_in, m_conv_q, m_conv_k, m_conv_v, m_dn_a_log, m_dn_dt_bias, m_dn_out_norm_gain, m_sb_q_norm_gain, m_sb_k_norm_gain, m_w_branch_dn, m_w_branch_sb, m_w_out, m_norm_ffn_gain, m_w_ffn_in, m_w_ffn_out, v_meta_tokens, v_norm_mix_gain, v_w_in, v_conv_q, v_conv_k, v_conv_v, v_dn_a_log, v_dn_dt_bias, v_dn_out_norm_gain, v_sb_q_norm_gain, v_sb_k_norm_gain, v_w_branch_dn, v_w_branch_sb, v_w_out, v_norm_ffn_gain, v_w_ffn_in, v_w_ffn_out):
    me = 4 * lax.axis_index("x") + 2 * lax.axis_index("y") + lax.axis_index("c")
    heads = dn_a_log.shape[1]
    d = x.shape[2]
    qk = conv_q.shape[2] * N_DEV
    dvt = conv_v.shape[2] * N_DEV
    col_ab = 2 * qk + 2 * dvt

    small_shapes = [meta_tokens.shape, conv_q.shape[1:], conv_k.shape[1:], conv_v.shape[1:]]
    small = _pack([meta_tokens, conv_q[0], conv_k[0], conv_v[0]])
    g_in, g_small = _gather_two_level([_bf(w_in[0]), small], name="gather_w_in")
    w_full = _gather_cols(g_in)
    w_main = jnp.concatenate([w_full[:, :col_ab], w_full[:, col_ab + 2 * heads:]], axis=1)
    w_ab = _pad_lanes(w_full[:, col_ab:col_ab + 2 * heads])
    parts = [_unpack(g_small[p], small_shapes) for p in range(N_DEV)]
    meta_f, cq_f, ck_f, cv_f = (jnp.concatenate([parts[p][a] for p in range(N_DEV)], axis=1) for a in range(4))

    r = _local_step(x[0], loss_target[0], meta_f, norm_mix_gain, w_main, w_ab, cq_f, ck_f, cv_f, dn_a_log, dn_dt_bias,
                    dn_out_norm_gain, sb_q_norm_gain, sb_k_norm_gain, norm_ffn_gain,
                    (_bf(w_ffn_in[0]), _bf(w_branch_dn[0]), _bf(w_branch_sb[0]), _bf(w_out[0]), _bf(w_ffn_out[0])),
                    shards=True)
    s_fi, s_bd, s_bs, s_out, s_fo = r["early"]
    s_in = r["s_in"]

    loss_part = (0.5 / d) * jnp.sum(r["lsum"], axis=1, keepdims=True)
    small_g = [r["d_meta"], r["d_g_mix"], r["d_cq"], r["d_ck"], r["d_cv"], r["d_a_log"], r["d_dt"], r["d_g_dn"],
               r["d_g_sbq"], r["d_g_sbk"], r["d_g_ffn"], loss_part]
    (g_packs,) = _exchange([_pack(small_g)], scatter=[False], name="gather_small_grads")
    (g_meta, g_mix, g_cq, g_ck, g_cv, g_al, g_dt, g_gdn, g_sbq, g_sbk, g_ffn, loss) = _unpack(
        _sum_slabs(g_packs, name="sum_small_grads"), [a.shape for a in small_g])

    def mine(a, width):
        return lax.dynamic_slice_in_dim(a, me * width, width, axis=1)

    big = dict(w_in=(s_in, w_in, m_w_in, v_w_in), w_branch_dn=(s_bd, w_branch_dn, m_w_branch_dn, v_w_branch_dn),
               w_branch_sb=(s_bs, w_branch_sb, m_w_branch_sb, v_w_branch_sb), w_out=(s_out, w_out, m_w_out, v_w_out),
               w_ffn_in=(s_fi, w_ffn_in, m_w_ffn_in, v_w_ffn_in), w_ffn_out=(s_fo, w_ffn_out, m_w_ffn_out, v_w_ffn_out))
    tiny = dict(meta_tokens=(mine(g_meta, d // N_DEV), meta_tokens, m_meta_tokens, v_meta_tokens),
                norm_mix_gain=(g_mix, norm_mix_gain, m_norm_mix_gain, v_norm_mix_gain),
                conv_q=(mine(g_cq, qk // N_DEV), conv_q[0], m_conv_q[0], v_conv_q[0]),
                conv_k=(mine(g_ck, qk // N_DEV), conv_k[0], m_conv_k[0], v_conv_k[0]),
                conv_v=(mine(g_cv, dvt // N_DEV), conv_v[0], m_conv_v[0], v_conv_v[0]),
                dn_a_log=(g_al, dn_a_log, m_dn_a_log, v_dn_a_log), dn_dt_bias=(g_dt, dn_dt_bias, m_dn_dt_bias, v_dn_dt_bias),
                dn_out_norm_gain=(g_gdn, dn_out_norm_gain, m_dn_out_norm_gain, v_dn_out_norm_gain),
                sb_q_norm_gain=(g_sbq, sb_q_norm_gain, m_sb_q_norm_gain, v_sb_q_norm_gain),
                sb_k_norm_gain=(g_sbk, sb_k_norm_gain, m_sb_k_norm_gain, v_sb_k_norm_gain),
                norm_ffn_gain=(g_ffn, norm_ffn_gain, m_norm_ffn_gain, v_norm_ffn_gain))
    order = ["meta_tokens", "norm_mix_gain", "w_in", "conv_q", "conv_k", "conv_v", "dn_a_log", "dn_dt_bias",
             "dn_out_norm_gain", "sb_q_norm_gain", "sb_k_norm_gain", "w_branch_dn", "w_branch_sb", "w_out",
             "norm_ffn_gain", "w_ffn_in", "w_ffn_out"]
    grads, deltas, new_m, new_v = [], [], [], []
    for name in order:
        if name in big:
            slabs, w, m, v = big[name]
            g, dl, mo, vo = _adamw_slabs(slabs, w[0], m[0], v[0], name="adamw_" + name)
            like = w.shape
        else:
            g, w, m, v = tiny[name]
            like = dict(conv_q=conv_q, conv_k=conv_k, conv_v=conv_v).get(name, w).shape
            dl, mo, vo = _adamw_small(g, w, m, v, name="adamw_" + name)
        for lst, a in ((grads, g), (deltas, dl), (new_m, mo), (new_v, vo)):
            lst.append(a.reshape(like))
    return (loss.reshape(()), r["grad_x"][None], *grads, *deltas, *new_m, *new_v)
```

```python
import functools
import math

import jax
import jax.numpy as jnp
from jax import lax
from jax.experimental import pallas as pl
from jax.experimental.pallas import tpu as pltpu

F32 = jnp.float32
BF16 = jnp.bfloat16

N_DEV = 8
CHUNK = 64
CHUNK_SHIFT = 6
GDN_ROWS = 2 * CHUNK
SB_BLOCK = 128
SB_HEADS_PER_STEP = 2
LANES = 128
RMS_EPS = 1e-6
L2_EPS = 1e-6
ADAM_LR = 0.001
ADAM_B1 = 0.9
ADAM_B2 = 0.999
ADAM_EPS = 1e-08
ADAM_WD = 0.01
ADAM_STEP = 10
V7X_VMEM_LIMIT_BYTES = 56 * 1024 * 1024
MM_TN_OUT_BLOCK_BYTES = 6 * 1024 * 1024
ROWS_BIG = (1056, 512, 384, 256, 128)
ROWS_MID = (528, 384, 256, 128)
SCAN_HEADS = 4

MESH = pl.DeviceIdType.MESH


def _params(*sem):
    return pltpu.CompilerParams(dimension_semantics=sem or None, vmem_limit_bytes=V7X_VMEM_LIMIT_BYTES)


def _pick(n, cands):
    for c in cands:
        if n % c == 0:
            return c
    raise ValueError(f"no block size among {cands} divides {n}")


def _bf(x):
    return x.astype(BF16)


def _dot(a, b):
    return jnp.dot(a, b, preferred_element_type=F32)


def _dot_nt(a, b):
    return lax.dot_general(a, b, (((1,), (1,)), ((), ())), preferred_element_type=F32)


def _dot_tn(a, b):
    return lax.dot_general(a, b, (((0,), (0,)), ((), ())), preferred_element_type=F32)


def _split2(x):
    hi = _bf(x)
    return hi, _bf(x - hi.astype(F32))


def _split3(x):
    hi = _bf(x)
    r = x - hi.astype(F32)
    mid = _bf(r)
    return hi, mid, _bf(r - mid.astype(F32))


def _dot_hp(a, b, dot=_dot):
    ah, al = _split2(a)
    bh, bl = _split2(b)
    return dot(ah, bh) + dot(ah, bl) + dot(al, bh)


def _dot_exact_r(x, m, dot=_dot):
    h, mi, lo = _split3(x)
    return dot(h, m) + dot(mi, m) + dot(lo, m)


def _dot_exact_l(m, x, dot=_dot):
    h, mi, lo = _split3(x)
    return dot(m, h) + dot(m, mi) + dot(m, lo)


def _sigmoid(x):
    return 1.0 / (1.0 + jnp.exp(-x))


def _silu(x):
    return x * _sigmoid(x)


def _silu_grad(x):
    s = _sigmoid(x)
    return s * (1.0 + x * (1.0 - s))


def _softplus(x):
    return jnp.maximum(x, 0.0) + jnp.log(1.0 + jnp.exp(-jnp.abs(x)))


def _rms_fwd(h, gain):
    r = lax.rsqrt(jnp.mean(h * h, axis=-1, keepdims=True) + RMS_EPS)
    return h * r * gain


def _rms_bwd(h, gain, dy):
    r = lax.rsqrt(jnp.mean(h * h, axis=-1, keepdims=True) + RMS_EPS)
    dyg = dy * gain
    dh = r * dyg - h * (r * r * r) * jnp.mean(dyg * h, axis=-1, keepdims=True)
    return dh, dy * h * r


def _iota(shape, dim):
    return lax.broadcasted_iota(jnp.int32, shape, dim)


def _lane_pick(x, idx):
    return jnp.sum(jnp.where(_iota(x.shape, 1) == idx, x, 0.0), axis=1, keepdims=True)


def _mm_nn(a, b, *, out_dtype, name):
    m, k = a.shape
    n = b.shape[1]
    tm, tn = _pick(m, ROWS_BIG), _pick(n, (1024, 512, 256, 128))

    def body(a_ref, b_ref, o_ref):
        o_ref[...] = _dot(a_ref[...], b_ref[...]).astype(out_dtype)

    return pl.pallas_call(
        body, grid=(m // tm, n // tn),
        in_specs=[pl.BlockSpec((tm, k), lambda i, j: (i, 0)), pl.BlockSpec((k, tn), lambda i, j: (0, j))],
        out_specs=pl.BlockSpec((tm, tn), lambda i, j: (i, j)),
        out_shape=jax.ShapeDtypeStruct((m, n), out_dtype), name=name,
        compiler_params=_params("parallel", "parallel"))(a, b)


def _mm_nt(a, b, *, out_dtype, name):
    m, k = a.shape
    n = b.shape[0]
    tm, tn = _pick(m, ROWS_BIG), _pick(n, (1024, 512, 256, 128))

    def body(a_ref, b_ref, o_ref):
        o_ref[...] = _dot_nt(a_ref[...], b_ref[...]).astype(out_dtype)

    return pl.pallas_call(
        body, grid=(m // tm, n // tn),
        in_specs=[pl.BlockSpec((tm, k), lambda i, j: (i, 0)), pl.BlockSpec((tn, k), lambda i, j: (j, 0))],
        out_specs=pl.BlockSpec((tm, tn), lambda i, j: (i, j)),
        out_shape=jax.ShapeDtypeStruct((m, n), out_dtype), name=name,
        compiler_params=_params("parallel", "parallel"))(a, b)


def _mm_tn(a, b, *, name):
    t, m = a.shape
    n = b.shape[1]
    tn = _pick(n, (2816, 2048, 1408, 1024, 512, 256, 128))
    tm = _pick(m, tuple(c for c in (1408, 1024, 512, 256, 128) if c * tn * 4 <= MM_TN_OUT_BLOCK_BYTES))
    tk = _pick(t, (1408, 1024, 512, 384, 256, 128))

    def body(a_ref, b_ref, o_ref):
        @pl.when(pl.program_id(2) == 0)
        def _():
            o_ref[...] = jnp.zeros_like(o_ref)

        o_ref[...] += _dot_tn(a_ref[...], b_ref[...])

    return pl.pallas_call(
        body, grid=(m // tm, n // tn, t // tk),
        in_specs=[pl.BlockSpec((tk, tm), lambda i, j, k: (k, i)), pl.BlockSpec((tk, tn), lambda i, j, k: (k, j))],
        out_specs=pl.BlockSpec((tm, tn), lambda i, j, k: (i, j)),
        out_shape=jax.ShapeDtypeStruct((m, n), F32), name=name,
        compiler_params=_params("parallel", "parallel", "arbitrary"))(a, b)


def _mm_norm(h, gain, w, *, name):
    m, k = h.shape
    n = w.shape[1]
    tm, tn = _pick(m, ROWS_BIG), _pick(n, (1024, 512, 256, 128))

    def body(h_ref, g_ref, w_ref, o_ref, n_ref):
        @pl.when(pl.program_id(1) == 0)
        def _():
            n_ref[...] = _bf(_rms_fwd(h_ref[...], g_ref[...]))

        o_ref[...] = _dot(n_ref[...], w_ref[...])

    return pl.pallas_call(
        body, grid=(m // tm, n // tn),
        in_specs=[pl.BlockSpec((tm, k), lambda i, j: (i, 0)), pl.BlockSpec((1, k), lambda i, j: (0, 0)),
                  pl.BlockSpec((k, tn), lambda i, j: (0, j))],
        out_specs=[pl.BlockSpec((tm, tn), lambda i, j: (i, j)), pl.BlockSpec((tm, k), lambda i, j: (i, 0))],
        out_shape=[jax.ShapeDtypeStruct((m, n), F32), jax.ShapeDtypeStruct((m, k), BF16)], name=name,
        compiler_params=_params("parallel", "arbitrary"))(h, gain, w)


def _mm_norm_swiglu(h, gain, wg, wu, *, name):
    m, k = h.shape
    n = wg.shape[1]
    tm, tn = _pick(m, ROWS_MID), _pick(n, (1408, 1024, 512, 256, 128))

    def body(h_ref, g_ref, wg_ref, wu_ref, gate_ref, up_ref, act_ref, n_ref):
        @pl.when(pl.program_id(1) == 0)
        def _():
            n_ref[...] = _bf(_rms_fwd(h_ref[...], g_ref[...]))

        gate = _dot(n_ref[...], wg_ref[...])
        up = _dot(n_ref[...], wu_ref[...])
        gate_ref[...] = gate
        up_ref[...] = up
        act_ref[...] = _bf(_silu(gate) * up)

    wspec = pl.BlockSpec((k, tn), lambda i, j: (0, j))
    ospec = pl.BlockSpec((tm, tn), lambda i, j: (i, j))
    return pl.pallas_call(
        body, grid=(m // tm, n // tn),
        in_specs=[pl.BlockSpec((tm, k), lambda i, j: (i, 0)), pl.BlockSpec((1, k), lambda i, j: (0, 0)), wspec, wspec],
        out_specs=[ospec, ospec, ospec, pl.BlockSpec((tm, k), lambda i, j: (i, 0))],
        out_shape=[jax.ShapeDtypeStruct((m, n), F32), jax.ShapeDtypeStruct((m, n), F32),
                   jax.ShapeDtypeStruct((m, n), BF16), jax.ShapeDtypeStruct((m, k), BF16)], name=name,
        compiler_params=_params("parallel", "arbitrary"))(h, gain, wg, wu)


def _mm_res(res, a, b, *, name):
    m, k = a.shape
    n = b.shape[1]
    tm, tn = _pick(m, ROWS_BIG), _pick(n, (1024, 512, 256, 128))

    def body(r_ref, a_ref, b_ref, o_ref):
        o_ref[...] = r_ref[...] + _dot(a_ref[...], b_ref[...])

    return pl.pallas_call(
        body, grid=(m // tm, n // tn),
        in_specs=[pl.BlockSpec((tm, tn), lambda i, j: (i, j)), pl.BlockSpec((tm, k), lambda i, j: (i, 0)),
                  pl.BlockSpec((k, tn), lambda i, j: (0, j))],
        out_specs=pl.BlockSpec((tm, tn), lambda i, j: (i, j)),
        out_shape=jax.ShapeDtypeStruct((m, n), F32), name=name,
        compiler_params=_params("parallel", "parallel"))(res, a, b)


def _mm_res_loss(res, a, b, target, *, row0, nrows, name):
    m, k = a.shape
    n = b.shape[1]
    tm = _pick(m, ROWS_MID)

    def body(r_ref, a_ref, b_ref, t_ref, dy_ref, dyb_ref, ls_ref):
        i = pl.program_id(0)

        @pl.when(i == 0)
        def _():
            ls_ref[...] = jnp.zeros_like(ls_ref)

        y = r_ref[...] + _dot(a_ref[...], b_ref[...])
        row = i * tm + _iota((tm, n), 0)
        e = jnp.where((row >= row0) & (row < row0 + nrows), y - t_ref[...], 0.0)
        dy = e / n
        dy_ref[...] = dy
        dyb_ref[...] = _bf(dy)
        ls_ref[...] += jnp.sum(e * e, axis=0, keepdims=True)

    rspec = pl.BlockSpec((tm, n), lambda i: (i, 0))
    return pl.pallas_call(
        body, grid=(m // tm,),
        in_specs=[rspec, pl.BlockSpec((tm, k), lambda i: (i, 0)), pl.BlockSpec((k, n), lambda i: (0, 0)), rspec],
        out_specs=[rspec, rspec, pl.BlockSpec((1, n), lambda i: (0, 0))],
        out_shape=[jax.ShapeDtypeStruct((m, n), F32), jax.ShapeDtypeStruct((m, n), BF16),
                   jax.ShapeDtypeStruct((1, n), F32)], name=name,
        compiler_params=_params("arbitrary"))(res, a, b, target)


def _merge_fwd(o_dn, o_sb, wbd, wbs, proj, *, col_gd, col_gs, name):
    m, kd = o_dn.shape
    ks = o_sb.shape[1]
    n = wbd.shape[1]
    tm = _pick(m, ROWS_BIG)
    tn = _pick(math.gcd(n, math.gcd(col_gd, col_gs)), (512, 256, 128))

    def body(od_ref, os_ref, wd_ref, ws_ref, gd_ref, gs_ref, mg_ref, bd_ref, bs_ref):
        bd = _dot(od_ref[...], wd_ref[...])
        bs = _dot(os_ref[...], ws_ref[...])
        bd_ref[...] = bd
        bs_ref[...] = bs
        mg_ref[...] = _bf(_sigmoid(gd_ref[...]) * bd + _sigmoid(gs_ref[...]) * bs)

    ospec = pl.BlockSpec((tm, tn), lambda i, j: (i, j))
    return pl.pallas_call(
        body, grid=(m // tm, n // tn),
        in_specs=[pl.BlockSpec((tm, kd), lambda i, j: (i, 0)), pl.BlockSpec((tm, ks), lambda i, j: (i, 0)),
                  pl.BlockSpec((kd, tn), lambda i, j: (0, j)), pl.BlockSpec((ks, tn), lambda i, j: (0, j)),
                  pl.BlockSpec((tm, tn), lambda i, j: (i, col_gd // tn + j)),
                  pl.BlockSpec((tm, tn), lambda i, j: (i, col_gs // tn + j))],
        out_specs=[ospec, ospec, ospec],
        out_shape=[jax.ShapeDtypeStruct((m, n), BF16), jax.ShapeDtypeStruct((m, n), F32),
                   jax.ShapeDtypeStruct((m, n), F32)], name=name,
        compiler_params=_params("parallel", "parallel"))(o_dn, o_sb, wbd, wbs, proj, proj)


def _merge_bwd(dh, w_out, proj, br_dn, br_sb, *, col_gd, col_gs, name):
    m, k = dh.shape
    n = w_out.shape[0]
    tm = _pick(m, ROWS_BIG)
    tn = _pick(math.gcd(n, math.gcd(col_gd, col_gs)), (512, 256, 128))

    def body(dh_ref, w_ref, gd_ref, gs_ref, bd_ref, bs_ref, dbd_ref, dbs_ref, dgd_ref, dgs_ref):
        dm = _dot_nt(dh_ref[...], w_ref[...])
        sd = _sigmoid(gd_ref[...])
        ss = _sigmoid(gs_ref[...])
        dbd_ref[...] = _bf(dm * sd)
        dbs_ref[...] = _bf(dm * ss)
        dgd_ref[...] = _bf(dm * bd_ref[...] * sd * (1.0 - sd))
        dgs_ref[...] = _bf(dm * bs_ref[...] * ss * (1.0 - ss))

    ospec = pl.BlockSpec((tm, tn), lambda i, j: (i, j))
    return pl.pallas_call(
        body, grid=(m // tm, n // tn),
        in_specs=[pl.BlockSpec((tm, k), lambda i, j: (i, 0)), pl.BlockSpec((tn, k), lambda i, j: (j, 0)),
                  pl.BlockSpec((tm, tn), lambda i, j: (i, col_gd // tn + j)),
                  pl.BlockSpec((tm, tn), lambda i, j: (i, col_gs // tn + j)), ospec, ospec],
        out_specs=[ospec] * 4,
        out_shape=[jax.ShapeDtypeStruct((m, n), BF16)] * 4, name=name,
        compiler_params=_params("parallel", "parallel"))(dh, w_out, proj, proj, br_dn, br_sb)


def _swiglu_bwd(dy, wfo, gate, up, *, name):
    m, k = dy.shape
    n = wfo.shape[0]
    tm, tn = _pick(m, ROWS_MID), _pick(n, (1408, 1024, 512, 256, 128))

    def body(dy_ref, w_ref, g_ref, u_ref, dg_ref, du_ref):
        da = _dot_nt(dy_ref[...], w_ref[...])
        g = g_ref[...]
        dg_ref[...] = _bf(da * u_ref[...] * _silu_grad(g))
        du_ref[...] = _bf(da * _silu(g))

    ospec = pl.BlockSpec((tm, tn), lambda i, j: (i, j))
    return pl.pallas_call(
        body, grid=(m // tm, n // tn),
        in_specs=[pl.BlockSpec((tm, k), lambda i, j: (i, 0)), pl.BlockSpec((tn, k), lambda i, j: (j, 0)), ospec, ospec],
        out_specs=[ospec, ospec], out_shape=[jax.ShapeDtypeStruct((m, n), BF16)] * 2, name=name,
        compiler_params=_params("parallel", "parallel"))(dy, wfo, gate, up)


def _mm_nt_rmsbwd(pairs, extra, h, gain, dres, *, name, scatter=None):
    m, k = pairs[0][0].shape
    n = h.shape[1]
    tm = _pick(m, ROWS_MID)
    tk = _pick(k, (1408, 1024, 512, 256, 128))
    nk = k // tk
    np_ = len(pairs)

    def body(*refs):
        ab = refs[:2 * np_]
        pos = 2 * np_
        ex = refs[pos:pos + 2] if extra is not None else ()
        pos += len(ex)
        h_ref, g_ref, r_ref, dh_ref, dhb_ref, dg_ref, acc_ref = refs[pos:]
        i, kk = pl.program_id(0), pl.program_id(1)

        @pl.when((i == 0) & (kk == 0))
        def _():
            dg_ref[...] = jnp.zeros_like(dg_ref)

        part = _dot_nt(ab[0][...], ab[1][...])
        for p in range(1, np_):
            part += _dot_nt(ab[2 * p][...], ab[2 * p + 1][...])

        @pl.when(kk == 0)
        def _():
            first = part
            if ex:
                first = first + _dot_nt(ex[0][...], ex[1][...])
            acc_ref[...] = first

        @pl.when(kk > 0)
        def _():
            acc_ref[...] += part

        @pl.when(kk == nk - 1)
        def _():
            dh, dgr = _rms_bwd(h_ref[...], g_ref[...], acc_ref[...])
            dh = dh + r_ref[...]
            dh_ref[...] = dh
            dhb_ref[...] = _bf(dh)
            dg_ref[...] += jnp.sum(dgr, axis=0, keepdims=True)

    in_specs, args = [], []
    for a, b in pairs:
        in_specs += [pl.BlockSpec((tm, tk), lambda i, kk: (i, kk)), pl.BlockSpec((n, tk), lambda i, kk: (0, kk))]
        args += [a, b]
    if extra is not None:
        k2 = extra[0].shape[1]
        in_specs += [pl.BlockSpec((tm, k2), lambda i, kk: (i, 0)), pl.BlockSpec((n, k2), lambda i, kk: (0, 0))]
        args += list(extra)
    rspec = pl.BlockSpec((tm, n), lambda i, kk: (i, 0))
    in_specs += [rspec, pl.BlockSpec((1, n), lambda i, kk: (0, 0)), rspec]
    call = dict(grid=(m // tm, nk), in_specs=in_specs,
                out_specs=[rspec, rspec, pl.BlockSpec((1, n), lambda i, kk: (0, 0))],
                out_shape=[jax.ShapeDtypeStruct((m, n), F32), jax.ShapeDtypeStruct((m, n), BF16),
                           jax.ShapeDtypeStruct((1, n), F32)],
                scratch_shapes=[pltpu.VMEM((tm, n), F32)], name=name)
    if scatter is None:
        return pl.pallas_call(body, compiler_params=_params("arbitrary", "arbitrary"), **call)(*args, h, gain, dres)
    res = _call_with_exchange(body, args=(*args, h, gain, dres), srcs=scatter, scatter=[True] * len(scatter), **call)
    return [*res[:3], list(res[3:])]


def _conv_taps(cur, prev8, w_ref, first):
    nk = w_ref.shape[0]
    out = cur * w_ref[nk - 1:nk, :]
    for s in range(1, nk):
        out += _shift_down(cur, prev8, s, first) * w_ref[nk - 1 - s:nk - s, :]
    return out


def _shift_up(cur, next8, s, last):
    rows = cur.shape[0]
    row = _iota(cur.shape, 0)
    next8 = jnp.where(last, 0.0, next8)
    sh = pltpu.roll(cur, rows - s, axis=0)
    nh = jnp.tile(pltpu.roll(next8, 8 - s, axis=0), (rows // 8, 1))
    return jnp.where(row >= rows - s, nh, sh)


def _shift_down(cur, prev8, s, first):
    rows = cur.shape[0]
    row = _iota(cur.shape, 0)
    prev8 = jnp.where(first, 0.0, prev8)
    sh = pltpu.roll(cur, s, axis=0)
    ph = jnp.tile(pltpu.roll(prev8, s, axis=0), (rows // 8, 1))
    return jnp.where(row < s, ph, sh)


def _gdn_pre(proj, pab, cq, ck, cv, a_log, dt_bias, *, heads, dk, dv, col_q, col_k, col_v, row_lo, row_hi, name):
    t = proj.shape[0]
    tm = _pick(t, (384, 256, 128))
    nb = t // tm

    def body(pq_ref, pqp_ref, pk_ref, pkp_ref, pv_ref, pvp_ref, ab_ref, cq_ref, ck_ref, cv_ref, al_ref, dt_ref,
             qn_ref, kn_ref, v_ref, g_ref, b_ref):
        h, i = pl.program_id(0), pl.program_id(1)
        first = i == 0
        row = i * tm + _iota((tm, 1), 0)
        valid = (row >= row_lo) & (row < row_hi)
        q1 = _silu(_conv_taps(pq_ref[...], pqp_ref[...], cq_ref, first))
        k1 = _silu(_conv_taps(pk_ref[...], pkp_ref[...], ck_ref, first))
        v1 = _silu(_conv_taps(pv_ref[...], pvp_ref[...], cv_ref, first))
        qn_ref[...] = jnp.where(valid, q1 * lax.rsqrt(jnp.sum(q1 * q1, axis=-1, keepdims=True) + L2_EPS), 0.0)
        kn_ref[...] = jnp.where(valid, k1 * lax.rsqrt(jnp.sum(k1 * k1, axis=-1, keepdims=True) + L2_EPS), 0.0)
        v_ref[...] = jnp.where(valid, v1, 0.0)
        ab = ab_ref[...]
        da = _lane_pick(ab, h)
        db = _lane_pick(ab, heads + h)
        a = _lane_pick(al_ref[...], h)
        dtb = _lane_pick(dt_ref[...], h)
        g_ref[...] = jnp.where(valid, -jnp.exp(a) * _softplus(da + dtb), 0.0)
        b_ref[...] = jnp.where(valid, _sigmoid(db), 0.0)

    def cur(width, col):
        return pl.BlockSpec((tm, width), lambda h, i: (i, col // width + h))

    def prev(width, col):
        return pl.BlockSpec((8, width), lambda h, i: (jnp.maximum(i * (tm // 8) - 1, 0), col // width + h))

    def out(width):
        return pl.BlockSpec((None, tm, width), lambda h, i: (h, i, 0))

    small = pl.BlockSpec((1, LANES), lambda h, i: (0, 0))
    return pl.pallas_call(
        body, grid=(heads, nb),
        in_specs=[cur(dk, col_q), prev(dk, col_q), cur(dk, col_k), prev(dk, col_k), cur(dv, col_v), prev(dv, col_v),
                  pl.BlockSpec((tm, LANES), lambda h, i: (i, 0)),
                  pl.BlockSpec((cq.shape[0], dk), lambda h, i: (0, h)), pl.BlockSpec((ck.shape[0], dk), lambda h, i: (0, h)),
                  pl.BlockSpec((cv.shape[0], dv), lambda h, i: (0, h)), small, small],
        out_specs=[out(dk), out(dk), out(dv), out(1), out(1)],
        out_shape=[jax.ShapeDtypeStruct((heads, t, dk), F32), jax.ShapeDtypeStruct((heads, t, dk), F32),
                   jax.ShapeDtypeStruct((heads, t, dv), F32), jax.ShapeDtypeStruct((heads, t, 1), F32),
                   jax.ShapeDtypeStruct((heads, t, 1), F32)], name=name,
        compiler_params=_params("parallel", "parallel"))(proj, proj, proj, proj, proj, proj, pab, cq, ck, cv, a_log, dt_bias)


def _chunk_masks(rows=GDN_ROWS, row0=0):
    ri = row0 + _iota((rows, GDN_ROWS), 0)
    ci = _iota((rows, GDN_ROWS), 1)
    same = jnp.right_shift(ri, CHUNK_SHIFT) == jnp.right_shift(ci, CHUNK_SHIFT)
    return same, same & (ri >= ci), same & (ri > ci), ri == ci


def _col_to_row(col, eye):
    return jnp.sum(jnp.where(eye, col, 0.0), axis=0, keepdims=True)


def _row_to_col(row, eye):
    return jnp.sum(jnp.where(eye, row, 0.0), axis=1, keepdims=True)


def _chunk_common(blocks, dk_scale):
    same, incl, strict, eye = _chunk_masks()
    tri = jnp.where(incl, 1.0, 0.0).astype(BF16)
    tot = jnp.where(same, 1.0, 0.0).astype(BF16)
    gbs = [jnp.broadcast_to(g, (GDN_ROWS, LANES)) for _, _, g, _ in blocks]
    gams = [jnp.max(_dot_exact_l(tri, gb), axis=1, keepdims=True) for gb in gbs]
    lasts = [jnp.max(_dot_exact_l(tot, gb), axis=1, keepdims=True) for gb in gbs]
    kbs = [kn * beta for _, kn, _, beta in blocks]
    qts = [qn * dk_scale for qn, _, _, _ in blocks]
    boths = [_dot_nt(_bf(jnp.concatenate([kb, qt], axis=0)), _bf(blk[1]))
             for kb, qt, blk in zip(kbs, qts, blocks)]
    out = []
    for gam, last, kb, qt, both in zip(gams, lasts, kbs, qts, boths):
        diff = gam - _col_to_row(gam, eye)
        decay = jnp.where(incl, jnp.exp(jnp.where(incl, diff, 0.0)), 0.0)
        out.append(dict(incl=incl, strict=strict, eye=eye, decay=decay, eg=jnp.exp(gam), ek=jnp.exp(last - gam),
                        egl=jnp.exp(last), kb=kb, qt=qt, lmat=jnp.where(strict, both[:GDN_ROWS] * decay, 0.0),
                        pmat=jnp.where(incl, both[GDN_ROWS:] * decay, 0.0)))
    return out


def _gdn_prep(qn, kn, v, g, beta, *, name, gather=None):
    heads, t, dk = qn.shape
    dv = v.shape[2]
    rows = _pick(t, (3 * GDN_ROWS, 2 * GDN_ROWS, GDN_ROWS))
    dk_scale = dk ** -0.5

    def body(q_ref, k_ref, v_ref, g_ref, b_ref, u_ref, w_ref, p_ref, qd_ref, kd_ref, egl_ref, t_ref):
        rs = [pl.ds(b * GDN_ROWS, GDN_ROWS) for b in range(rows // GDN_ROWS)]
        cs = _chunk_common([(q_ref[r, :], k_ref[r, :], g_ref[r, :], b_ref[r, :]) for r in rs], dk_scale)
        eye_f = jnp.where(cs[0]["eye"], 1.0, 0.0)
        tinvs = [eye_f - c["lmat"] for c in cs]
        ys = [_dot_hp(c["lmat"], c["lmat"]) for c in cs]
        for _ in range(CHUNK_SHIFT - 1):
            boths = [_dot_hp(y, jnp.concatenate([y, tinv], axis=1)) for y, tinv in zip(ys, tinvs)]
            ys = [both[:, :GDN_ROWS] for both in boths]
            tinvs = [tinv + both[:, GDN_ROWS:] for tinv, both in zip(tinvs, boths)]
        uws = [_dot_hp(tinv, jnp.concatenate([v_ref[r, :] * b_ref[r, :], c["kb"] * c["eg"]], axis=1))
               for r, c, tinv in zip(rs, cs, tinvs)]
        for r, c, tinv, uw in zip(rs, cs, tinvs, uws):
            u_ref[r, :] = uw[:, :dv]
            w_ref[r, :] = uw[:, dv:]
            p_ref[r, :] = c["pmat"]
            qd_ref[r, :] = c["qt"] * c["eg"]
            kd_ref[r, :] = k_ref[r, :] * c["ek"]
            egl_ref[r, :] = c["egl"]
            t_ref[r, :] = tinv

    def blk(width):
        return pl.BlockSpec((None, rows, width), lambda h, i: (h, i, 0))

    def shp(width):
        return jax.ShapeDtypeStruct((heads, t, width), F32)

    call = dict(grid=(heads, t // rows), in_specs=[blk(dk), blk(dk), blk(dv), blk(1), blk(1)],
                out_specs=[blk(dv), blk(dk), blk(GDN_ROWS), blk(dk), blk(dk), blk(1), blk(GDN_ROWS)],
                out_shape=[shp(dv), shp(dk), shp(GDN_ROWS), shp(dk), shp(dk), shp(1), shp(GDN_ROWS)], name=name)
    if gather is None:
        return pl.pallas_call(body, compiler_params=_params("parallel", "parallel"), **call)(qn, kn, v, g, beta)
    res = _call_with_exchange(body, scratch_shapes=[], args=(qn, kn, v, g, beta), srcs=gather,
                              scatter=[False] * len(gather), **call)
    return [*res[:7], list(res[7:])]


def _gdn_scan(u, w, p, qd, kd, egl, proj, gain, *, col_z, name):
    heads, t, dv = u.shape
    dk = w.shape[2]
    nb = t // GDN_ROWS
    sub = GDN_ROWS // CHUNK
    hp = SCAN_HEADS

    def body(u_ref, w_ref, p_ref, qd_ref, kd_ref, egl_ref, z_ref, gn_ref, o_ref, og_ref, st_ref, s_ref):
        @pl.when(pl.program_id(1) == 0)
        def _():
            s_ref[...] = jnp.zeros_like(s_ref)

        hs = range(hp)
        vn_parts = [[jnp.zeros((CHUNK, dv), F32)] * sub for _ in hs]
        for c in range(sub):
            r = pl.ds(c * CHUNK, CHUNK)
            ss = [s_ref[hh] for hh in hs]
            sbs = [_bf(s) for s in ss]
            wss = [_dot(_bf(jnp.concatenate([w_ref[hh, r, :], qd_ref[hh, r, :]], axis=0)), sbs[hh])
                   for hh in hs]
            vns = [u_ref[hh, r, :] - wss[hh][:CHUNK] for hh in hs]
            for hh in hs:
                vn_parts[hh][c] = vns[hh]
            os_ = [wss[hh][CHUNK:] + _dot(_bf(p_ref[hh, r, :]), _bf(jnp.concatenate(vn_parts[hh], axis=0))) for hh in hs]
            new = [ss[hh] * egl_ref[hh, pl.ds(c * CHUNK, 1), :] + _dot_tn(_bf(kd_ref[hh, r, :]), _bf(vns[hh])) for hh in hs]
            for hh in hs:
                cols = pl.ds(hh * dv, dv)
                st_ref[hh, c] = ss[hh]
                s_ref[hh] = new[hh]
                o_ref[hh, r, :] = os_[hh]
                og_ref[r, cols] = _bf(_rms_fwd(os_[hh], gn_ref[...]) * _silu(z_ref[r, cols]))

    def blk(width):
        return pl.BlockSpec((hp, GDN_ROWS, width), lambda h, i: (h, i, 0))

    return pl.pallas_call(
        body, grid=(heads // hp, nb),
        in_specs=[blk(dv), blk(dk), blk(GDN_ROWS), blk(dk), blk(dk), blk(1),
                  pl.BlockSpec((GDN_ROWS, hp * dv), lambda h, i: (i, col_z // (hp * dv) + h)),
                  pl.BlockSpec((1, dv), lambda h, i: (0, 0))],
        out_specs=[blk(dv), pl.BlockSpec((GDN_ROWS, hp * dv), lambda h, i: (i, h)),
                   pl.BlockSpec((hp, sub, dk, dv), lambda h, i: (h, i, 0, 0))],
        out_shape=[jax.ShapeDtypeStruct((heads, t, dv), F32), jax.ShapeDtypeStruct((t, heads * dv), BF16),
                   jax.ShapeDtypeStruct((heads, t // CHUNK, dk, dv), F32)],
        scratch_shapes=[pltpu.VMEM((hp, dk, dv), F32)], name=name,
        compiler_params=_params("parallel", "arbitrary"))(u, w, p, qd, kd, egl, proj, gain)


def _gdn_post_bwd(o, proj, gain, dout, *, col_z, name):
    heads, t, dv = o.shape
    tm = _pick(t, (384, 256, 128))

    def body(o_ref, z_ref, gn_ref, d_ref, do_ref, dz_ref, dg_ref):
        @pl.when((pl.program_id(0) == 0) & (pl.program_id(1) == 0))
        def _():
            dg_ref[...] = jnp.zeros_like(dg_ref)

        o_, z, d = o_ref[...], z_ref[...], d_ref[...]
        y = _rms_fwd(o_, gn_ref[...])
        dz_ref[...] = _bf(d * y * _silu_grad(z))
        do, dgr = _rms_bwd(o_, gn_ref[...], d * _silu(z))
        do_ref[...] = do
        dg_ref[...] += jnp.sum(dgr, axis=0, keepdims=True)

    return pl.pallas_call(
        body, grid=(t // tm, heads),
        in_specs=[pl.BlockSpec((None, tm, dv), lambda i, h: (h, i, 0)),
                  pl.BlockSpec((tm, dv), lambda i, h: (i, col_z // dv + h)),
                  pl.BlockSpec((1, dv), lambda i, h: (0, 0)), pl.BlockSpec((tm, dv), lambda i, h: (i, h))],
        out_specs=[pl.BlockSpec((None, tm, dv), lambda i, h: (h, i, 0)), pl.BlockSpec((tm, dv), lambda i, h: (i, h)),
                   pl.BlockSpec((1, dv), lambda i, h: (0, 0))],
        out_shape=[jax.ShapeDtypeStruct((heads, t, dv), F32), jax.ShapeDtypeStruct((t, heads * dv), BF16),
                   jax.ShapeDtypeStruct((1, dv), F32)], name=name,
        compiler_params=_params("arbitrary", "arbitrary"))(o, proj, gain, dout)


def _gdn_bwd_scan(u, w, p, qd, kd, egl, st, do, *, name):
    heads, t, dv = u.shape
    dk = w.shape[2]
    nb = t // GDN_ROWS
    sub = GDN_ROWS // CHUNK
    hp = SCAN_HEADS

    def body(u_ref, w_ref, p_ref, qd_ref, kd_ref, egl_ref, st_ref, do_ref,
             du_ref, dw_ref, dp_ref, dqd_ref, dkd_ref, dgl_ref, ds_ref):
        @pl.when(pl.program_id(1) == 0)
        def _():
            ds_ref[...] = jnp.zeros_like(ds_ref)

        hs = range(hp)
        zeros = jnp.zeros((CHUNK, dv), BF16)
        for c in reversed(range(sub)):
            r = pl.ds(c * CHUNK, CHUNK)
            ss = [st_ref[hh, c] for hh in hs]
            sbs = [_bf(s) for s in ss]
            dss = [ds_ref[hh] for hh in hs]
            dsbs = [_bf(ds) for ds in dss]
            dobs = [_bf(do_ref[hh, r, :]) for hh in hs]
            wbs = [_bf(w_ref[hh, r, :]) for hh in hs]
            vns = [u_ref[hh, r, :] - _dot(wbs[hh], sbs[hh]) for hh in hs]
            dvns = [_dot_tn(_bf(p_ref[hh, r, :]), dobs[hh])[c * CHUNK:(c + 1) * CHUNK, :]
                    + _dot(_bf(kd_ref[hh, r, :]), dsbs[hh]) for hh in hs]
            dods = [jnp.concatenate([dobs[hh], _bf(dvns[hh])], axis=0) for hh in hs]
            boths = [_dot_nt(dods[hh], sbs[hh]) for hh in hs]
            dps = [_dot_nt(dobs[hh], jnp.concatenate([_bf(vns[hh]) if cc == c else zeros for cc in range(sub)], axis=0))
                   for hh in hs]
            dkds = [_dot_nt(_bf(vns[hh]), dsbs[hh]) for hh in hs]
            new = [dss[hh] * egl_ref[hh, pl.ds(c * CHUNK, 1), :]
                   + _dot_tn(jnp.concatenate([_bf(qd_ref[hh, r, :]), -wbs[hh]], axis=0), dods[hh])
                   for hh in hs]
            for hh in hs:
                du_ref[hh, r, :] = dvns[hh]
                dw_ref[hh, r, :] = -boths[hh][CHUNK:]
                dp_ref[hh, r, :] = jnp.where(_chunk_masks(CHUNK, c * CHUNK)[1], dps[hh], 0.0)
                dqd_ref[hh, r, :] = boths[hh][:CHUNK]
                dkd_ref[hh, r, :] = dkds[hh]
                dgl = jnp.sum(jnp.sum(dss[hh] * ss[hh], axis=1, keepdims=True), axis=0, keepdims=True)
                dgl_ref[hh, r, :] = jnp.where(_iota((CHUNK, 1), 0) == CHUNK - 1, dgl, 0.0)
                ds_ref[hh] = new[hh]

    def blk(width):
        return pl.BlockSpec((hp, GDN_ROWS, width), lambda h, i: (h, nb - 1 - i, 0))

    def shp(width):
        return jax.ShapeDtypeStruct((heads, t, width), F32)

    return pl.pallas_call(
        body, grid=(heads // hp, nb),
        in_specs=[blk(dv), blk(dk), blk(GDN_ROWS), blk(dk), blk(dk), blk(1),
                  pl.BlockSpec((hp, sub, dk, dv), lambda h, i: (h, nb - 1 - i, 0, 0)), blk(dv)],
        out_specs=[blk(dv), blk(dk), blk(GDN_ROWS), blk(dk), blk(dk), blk(1)],
        out_shape=[shp(dv), shp(dk), shp(GDN_ROWS), shp(dk), shp(dk), shp(1)],
        scratch_shapes=[pltpu.VMEM((hp, dk, dv), F32)], name=name,
        compiler_params=_params("parallel", "arbitrary"))(u, w, p, qd, kd, egl, st, do)


def _gdn_bwd_prep(qn, kn, v, g, beta, tinv, u, w, du, dw, dp, dqd, dkd, dgl, *, name):
    heads, t, dk = qn.shape
    dv = v.shape[2]
    rows = _pick(t, (3 * GDN_ROWS, 2 * GDN_ROWS, GDN_ROWS))
    dk_scale = dk ** -0.5

    def rowsum(x):
        return jnp.sum(x, axis=1, keepdims=True)

    def body(q_ref, k_ref, v_ref, g_ref, b_ref, t_ref, u_ref, w_ref, du_ref, dw_ref, dp_ref, dqd_ref, dkd_ref, dgl_ref,
             dq_ref, dkk_ref, dvv_ref, dg_ref, db_ref):
        rs = [pl.ds(b * GDN_ROWS, GDN_ROWS) for b in range(rows // GDN_ROWS)]
        cs = _chunk_common([(q_ref[r, :], k_ref[r, :], g_ref[r, :], b_ref[r, :]) for r in rs], dk_scale)
        dbvws = [_dot_hp(t_ref[r, :], jnp.concatenate([du_ref[r, :], dw_ref[r, :]], axis=1), _dot_tn)
                 for r in rs]
        das = [-_dot_nt(_bf(dbvw), _bf(jnp.concatenate([u_ref[r, :], w_ref[r, :]], axis=1)))
               for r, dbvw in zip(rs, dbvws)]
        dls = [jnp.where(c["strict"], da, 0.0) for c, da in zip(cs, das)]
        dmns = [_bf(jnp.concatenate([dl * c["decay"], dp_ref[r, :] * c["decay"]], axis=0))
                for r, c, dl in zip(rs, cs, dls)]
        boths = [_dot(dmn, _bf(k_ref[r, :])) for r, dmn in zip(rs, dmns)]
        dkns = [_dot_tn(dmn, _bf(jnp.concatenate([c["kb"], c["qt"]], axis=0)))
                for c, dmn in zip(cs, dmns)]
        for r, c, dbvw, dl, both, dkn in zip(rs, cs, dbvws, dls, boths, dkns):
            kn_, beta_, v_ = k_ref[r, :], b_ref[r, :], v_ref[r, :]
            eye = c["eye"]
            kb, qt, eg, ek = c["kb"], c["qt"], c["eg"], c["ek"]
            dbv, dbw = dbvw[:, :dv], dbvw[:, dv:]
            dp_ = dp_ref[r, :]
            dkb = both[:GDN_ROWS] + dbw * eg
            dqt = both[GDN_ROWS:]
            gmat = dl * c["lmat"] + dp_ * c["pmat"]
            dqd_, dkd_ = dqd_ref[r, :], dkd_ref[r, :]
            qd = qt * eg
            kd = kn_ * ek
            bw = kb * eg
            kdsum = rowsum(dkd_ * kd)
            dgam = rowsum(gmat) - _row_to_col(jnp.sum(gmat, axis=0, keepdims=True), eye)
            dgam += rowsum(dbw * bw) + rowsum(dqd_ * qd) - kdsum
            last = (_iota((GDN_ROWS, 1), 0) & (CHUNK - 1)) == CHUNK - 1
            same = _chunk_masks()[0]
            same_f = jnp.where(same, 1.0, 0.0).astype(BF16)
            chunk_tot = jnp.max(_dot_exact_l(same_f, jnp.broadcast_to(kdsum, (GDN_ROWS, LANES))), axis=1, keepdims=True)
            dgam += jnp.where(last, chunk_tot, 0.0) + dgl_ref[r, :] * c["egl"]
            dq_ref[r, :] = (dqt + dqd_ * eg) * dk_scale
            dkk_ref[r, :] = dkn + dkd_ * ek + dkb * beta_
            dvv_ref[r, :] = dbv * beta_
            db_ref[r, :] = rowsum(dbv * v_) + rowsum(dkb * kn_)
            upper = jnp.where(same & (_iota((GDN_ROWS, GDN_ROWS), 0) <= _iota((GDN_ROWS, GDN_ROWS), 1)), 1.0, 0.0)
            dgb = _dot_exact_l(upper.astype(BF16), jnp.broadcast_to(dgam, (GDN_ROWS, LANES)))
            dg_ref[r, :] = _lane_pick(dgb, 0)

    def blk(width):
        return pl.BlockSpec((None, rows, width), lambda h, i: (h, i, 0))

    def shp(width):
        return jax.ShapeDtypeStruct((heads, t, width), F32)

    return pl.pallas_call(
        body, grid=(heads, t // rows),
        in_specs=[blk(dk), blk(dk), blk(dv), blk(1), blk(1), blk(GDN_ROWS), blk(dv), blk(dk),
                  blk(dv), blk(dk), blk(GDN_ROWS), blk(dk), blk(dk), blk(1)],
        out_specs=[blk(dk), blk(dk), blk(dv), blk(1), blk(1)],
        out_shape=[shp(dk), shp(dk), shp(dv), shp(1), shp(1)], name=name,
        compiler_params=_params("parallel", "parallel"))(qn, kn, v, g, beta, tinv, u, w, du, dw, dp, dqd, dkd, dgl)


def _gdn_pre_bwd_a(proj, pab, cq, ck, cv, a_log, dt_bias, dqn, dkn, dvv, dg, dbeta, *,
                   heads, dk, dv, col_q, col_k, col_v, row_lo, row_hi, name):
    t = proj.shape[0]
    tm = _pick(t, (384, 256, 128))
    nb = t // tm

    def body(pq_ref, pqp_ref, pk_ref, pkp_ref, pv_ref, pvp_ref, ab_ref, cq_ref, ck_ref, cv_ref, al_ref, dt_ref,
             dqn_ref, dkn_ref, dvv_ref, dg_ref, db_ref, dcq_ref, dck_ref, dcv_ref, dab_ref, dal_ref, ddt_ref):
        i, h = pl.program_id(0), pl.program_id(1)
        first = i == 0

        @pl.when((i == 0) & (h == 0))
        def _():
            dal_ref[...] = jnp.zeros_like(dal_ref)
            ddt_ref[...] = jnp.zeros_like(ddt_ref)

        @pl.when(h == 0)
        def _():
            dab_ref[...] = jnp.zeros_like(dab_ref)

        row = i * tm + _iota((tm, 1), 0)
        valid = (row >= row_lo) & (row < row_hi)

        def l2_bwd(c1, dn):
            x1 = _silu(c1)
            r = lax.rsqrt(jnp.sum(x1 * x1, axis=-1, keepdims=True) + L2_EPS)
            dn = jnp.where(valid, dn, 0.0)
            d1 = r * dn - x1 * (r * r * r) * jnp.sum(dn * x1, axis=-1, keepdims=True)
            return d1 * _silu_grad(c1)

        dcq_ref[...] = l2_bwd(_conv_taps(pq_ref[...], pqp_ref[...], cq_ref, first), dqn_ref[...])
        dck_ref[...] = l2_bwd(_conv_taps(pk_ref[...], pkp_ref[...], ck_ref, first), dkn_ref[...])
        cv1 = _conv_taps(pv_ref[...], pvp_ref[...], cv_ref, first)
        dcv_ref[...] = jnp.where(valid, dvv_ref[...], 0.0) * _silu_grad(cv1)
        ab = ab_ref[...]
        da = _lane_pick(ab, h)
        db = _lane_pick(ab, heads + h)
        a = _lane_pick(al_ref[...], h)
        dtb = _lane_pick(dt_ref[...], h)
        dgv = jnp.where(valid, dg_ref[...], 0.0)
        ea = jnp.exp(a)
        g = -ea * _softplus(da + dtb)
        dda = dgv * (-ea) * _sigmoid(da + dtb)
        beta = _sigmoid(db)
        ddb = jnp.where(valid, db_ref[...], 0.0) * beta * (1.0 - beta)
        lane = _iota((tm, LANES), 1)
        dab_ref[...] += jnp.where(lane == h, dda, 0.0) + jnp.where(lane == heads + h, ddb, 0.0)
        lane1 = _iota((1, LANES), 1)
        dal_ref[...] += jnp.where(lane1 == h, jnp.sum(dgv * g, axis=0, keepdims=True), 0.0)
        ddt_ref[...] += jnp.where(lane1 == h, jnp.sum(dda, axis=0, keepdims=True), 0.0)

    def cur(width, col):
        return pl.BlockSpec((tm, width), lambda i, h: (i, col // width + h))

    def prev(width, col):
        return pl.BlockSpec((8, width), lambda i, h: (jnp.maximum(i * (tm // 8) - 1, 0), col // width + h))

    def hd(width):
        return pl.BlockSpec((None, tm, width), lambda i, h: (h, i, 0))

    small = pl.BlockSpec((1, LANES), lambda i, h: (0, 0))
    return pl.pallas_call(
        body, grid=(nb, heads),
        in_specs=[cur(dk, col_q), prev(dk, col_q), cur(dk, col_k), prev(dk, col_k), cur(dv, col_v), prev(dv, col_v),
                  pl.BlockSpec((tm, LANES), lambda i, h: (i, 0)),
                  pl.BlockSpec((cq.shape[0], dk), lambda i, h: (0, h)), pl.BlockSpec((ck.shape[0], dk), lambda i, h: (0, h)),
                  pl.BlockSpec((cv.shape[0], dv), lambda i, h: (0, h)), small, small,
                  hd(dk), hd(dk), hd(dv), hd(1), hd(1)],
        out_specs=[hd(dk), hd(dk), hd(dv), pl.BlockSpec((tm, LANES), lambda i, h: (i, 0)), small, small],
        out_shape=[jax.ShapeDtypeStruct((heads, t, dk), F32), jax.ShapeDtypeStruct((heads, t, dk), F32),
                   jax.ShapeDtypeStruct((heads, t, dv), F32), jax.ShapeDtypeStruct((t, LANES), F32),
                   jax.ShapeDtypeStruct((1, LANES), F32), jax.ShapeDtypeStruct((1, LANES), F32)], name=name,
        compiler_params=_params("arbitrary", "arbitrary"))(
            proj, proj, proj, proj, proj, proj, pab, cq, ck, cv, a_log, dt_bias, dqn, dkn, dvv, dg, dbeta)


def _conv_bwd(proj, dc, cw, *, heads, width, col, name):
    t = proj.shape[0]
    tm = _pick(t, (384, 256, 128))
    nb = t // tm
    nk = cw.shape[0]

    def body(p_ref, pp_ref, d_ref, dn_ref, w_ref, dp_ref, dw_ref):
        i = pl.program_id(1)
        first, last = i == 0, i == nb - 1

        @pl.when(first)
        def _():
            dw_ref[...] = jnp.zeros_like(dw_ref)

        x, d = p_ref[...], d_ref[...]
        dx = d * w_ref[nk - 1:nk, :]
        dw_ref[nk - 1:nk, :] += jnp.sum(d * x, axis=0, keepdims=True)
        for s in range(1, nk):
            dx += _shift_up(d, dn_ref[...], s, last) * w_ref[nk - 1 - s:nk - s, :]
            dw_ref[nk - 1 - s:nk - s, :] += jnp.sum(d * _shift_down(x, pp_ref[...], s, first), axis=0, keepdims=True)
        dp_ref[...] = _bf(dx)

    return pl.pallas_call(
        body, grid=(heads, nb),
        in_specs=[pl.BlockSpec((tm, width), lambda h, i: (i, col // width + h)),
                  pl.BlockSpec((8, width), lambda h, i: (jnp.maximum(i * (tm // 8) - 1, 0), col // width + h)),
                  pl.BlockSpec((None, tm, width), lambda h, i: (h, i, 0)),
                  pl.BlockSpec((None, 8, width), lambda h, i: (h, jnp.minimum((i + 1) * (tm // 8), t // 8 - 1), 0)),
                  pl.BlockSpec((nk, width), lambda h, i: (0, h))],
        out_specs=[pl.BlockSpec((tm, width), lambda h, i: (i, h)), pl.BlockSpec((nk, width), lambda h, i: (0, h))],
        out_shape=[jax.ShapeDtypeStruct((t, heads * width), BF16), jax.ShapeDtypeStruct((nk, heads * width), F32)],
        name=name, compiler_params=_params("parallel", "arbitrary"))(proj, proj, dc, dc, cw)


def _sb_pre(proj, gq, gk, *, heads, dh, col_q, col_k, col_v, name):
    t = proj.shape[0]
    tm = _pick(t, (384, 256, 128))

    def body(q_ref, k_ref, v_ref, gq_ref, gk_ref, qo_ref, ko_ref, vo_ref):
        qo_ref[...] = _bf(_rms_fwd(q_ref[...], gq_ref[...]))
        ko_ref[...] = _bf(_rms_fwd(k_ref[...], gk_ref[...]))
        vo_ref[...] = _bf(v_ref[...])

    def cur(col):
        return pl.BlockSpec((tm, dh), lambda i, h: (i, col // dh + h))

    gspec = pl.BlockSpec((1, dh), lambda i, h: (0, 0))
    ospec = pl.BlockSpec((tm, dh), lambda i, h: (i, h))
    return pl.pallas_call(
        body, grid=(t // tm, heads), in_specs=[cur(col_q), cur(col_k), cur(col_v), gspec, gspec],
        out_specs=[ospec] * 3, out_shape=[jax.ShapeDtypeStruct((t, heads * dh), BF16)] * 3, name=name,
        compiler_params=_params("parallel", "parallel"))(proj, proj, proj, gq, gk)


def _sb_tile(z, i, j, blk, key_lo):
    qpos = i * blk + _iota((blk, blk), 0)
    kpos = j * blk + _iota((blk, blk), 1)
    vis = (kpos < qpos) & (kpos >= key_lo)
    ls = jnp.minimum(z, 0.0) - jnp.log(1.0 + jnp.exp(-jnp.abs(z)))
    return vis, ls, jnp.where(vis, ls - z, 0.0)


def _dot2_r(x, m):
    hi, lo = _split2(x)
    return _dot(hi, m) + _dot(lo, m)


def _suffix_sums(x, later):
    return jnp.concatenate([_dot2_r(x[:, s:], later[s:, s:s + LANES]) for s in range(0, x.shape[1], LANES)], axis=1)


def _prefix_sums(x, earlier):
    return jnp.concatenate([_dot2_r(x[:, :s + LANES], earlier[:s + LANES, s:s + LANES])
                            for s in range(0, x.shape[1], LANES)], axis=1)


def _sb_fwd(qs, ks, vs, *, heads, dh, key_lo, name):
    t = qs.shape[0]
    blk = _pick(t, (3 * SB_BLOCK, 2 * SB_BLOCK, SB_BLOCK))
    nq = t // blk
    assert nq <= LANES
    scale = dh ** -0.5
    hp = SB_HEADS_PER_STEP

    def body(q_ref, k_ref, v_ref, o_ref, c_ref):
        i = pl.program_id(1)
        later = jnp.where(_iota((blk, blk), 0) > _iota((blk, blk), 1), 1.0, 0.0).astype(BF16)
        lane = _iota((blk, LANES), 1)
        c_ref[...] = jnp.zeros_like(c_ref)

        def step(n, carry):
            j = i - n
            rows = pl.ds(pl.multiple_of(j * blk, blk), blk)
            hs = range(hp)
            cols = [pl.ds(hh * dh, dh) for hh in hs]
            zs = [_dot_nt(q_ref[:, cols[hh]], k_ref[rows, cols[hh]]) * scale for hh in hs]
            tiles = [_sb_tile(z, i, j, blk, key_lo) for z in zs]
            sufs = [_suffix_sums(lk, later) for _, _, lk in tiles]
            wgts = [jnp.where(vis, jnp.exp(ls + suf + carry[2 * hh + 1]), 0.0)
                    for hh, ((vis, ls, _), suf) in enumerate(zip(tiles, sufs))]
            accs = [carry[2 * hh] + _dot(_bf(wgts[hh]), v_ref[rows, cols[hh]]) for hh in hs]
            out = []
            for hh in hs:
                c_ref[hh] = jnp.where(lane == j, carry[2 * hh + 1], c_ref[hh])
                out += [accs[hh], carry[2 * hh + 1] + jnp.sum(tiles[hh][2], axis=1, keepdims=True)]
            return tuple(out)

        res = lax.fori_loop(0, i + 1, step, (jnp.zeros((blk, dh), F32), jnp.zeros((blk, 1), F32)) * hp)
        for hh in range(hp):
            o_ref[:, pl.ds(hh * dh, dh)] = _bf(res[2 * hh])

    full = pl.BlockSpec((t, hp * dh), lambda h, i: (0, h))
    return pl.pallas_call(
        body, grid=(heads // hp, nq),
        in_specs=[pl.BlockSpec((blk, hp * dh), lambda h, i: (i, h)), full, full],
        out_specs=[pl.BlockSpec((blk, hp * dh), lambda h, i: (i, h)),
                   pl.BlockSpec((hp, blk, LANES), lambda h, i: (h, i, 0))],
        out_shape=[jax.ShapeDtypeStruct((t, heads * dh), BF16), jax.ShapeDtypeStruct((heads, t, LANES), F32)],
        name=name, compiler_params=_params("parallel", "parallel"))(qs, ks, vs)


def _sb_bwd(qs, ks, vs, do, carry, *, heads, dh, key_lo, name, scatter=None):
    t = qs.shape[0]
    blk = _pick(t, (3 * SB_BLOCK, 2 * SB_BLOCK, SB_BLOCK))
    nq = t // blk
    scale = dh ** -0.5
    hp = SB_HEADS_PER_STEP

    def body(q_ref, k_ref, v_ref, do_ref, c_ref, dq_ref, dk_ref, dv_ref):
        i = pl.program_id(1)

        @pl.when(i == 0)
        def _():
            dk_ref[...] = jnp.zeros_like(dk_ref)
            dv_ref[...] = jnp.zeros_like(dv_ref)

        r0 = _iota((blk, blk), 0)
        r1 = _iota((blk, blk), 1)
        later = jnp.where(r0 > r1, 1.0, 0.0).astype(BF16)
        earlier = jnp.where(r0 < r1, 1.0, 0.0).astype(BF16)

        def step(j, carry):
            rows = pl.ds(pl.multiple_of(j * blk, blk), blk)
            hs = range(hp)
            cols = [pl.ds(hh * dh, dh) for hh in hs]
            zs = [_dot_nt(q_ref[:, cols[hh]], k_ref[rows, cols[hh]]) * scale for hh in hs]
            dws = [_dot_nt(do_ref[:, cols[hh]], v_ref[rows, cols[hh]]) for hh in hs]
            tiles = [_sb_tile(z, i, j, blk, key_lo) for z in zs]
            sufs = [_suffix_sums(lk, later) for _, _, lk in tiles]
            wgts = [jnp.where(vis, jnp.exp(ls + suf + _lane_pick(c_ref[hh], j)), 0.0)
                    for hh, ((vis, ls, _), suf) in enumerate(zip(tiles, sufs))]
            es = [wgt * dw for wgt, dw in zip(wgts, dws)]
            pres = [_prefix_sums(e, earlier) for e in es]
            dzs = []
            for hh in hs:
                vis, ls, _ = tiles[hh]
                before = jnp.where(vis, pres[hh] + carry[2 * hh + 1], 0.0)
                sig = jnp.exp(ls)
                dzs.append(_bf((es[hh] * (1.0 - sig) - before * sig) * scale))
            dks = [_dot_tn(dzs[hh], q_ref[:, cols[hh]]) for hh in hs]
            dvs = [_dot_tn(_bf(wgts[hh]), do_ref[:, cols[hh]]) for hh in hs]
            dqs = [carry[2 * hh] + _dot(dzs[hh], k_ref[rows, cols[hh]]) for hh in hs]
            out = []
            for hh in hs:
                dk_ref[rows, cols[hh]] += dks[hh]
                dv_ref[rows, cols[hh]] += dvs[hh]
                out += [dqs[hh], carry[2 * hh + 1] + jnp.sum(es[hh], axis=1, keepdims=True)]
            return tuple(out)

        res = lax.fori_loop(0, i + 1, step, (jnp.zeros((blk, dh), F32), jnp.zeros((blk, 1), F32)) * hp)
        for hh in range(hp):
            dq_ref[:, pl.ds(hh * dh, dh)] = res[2 * hh]

    full = pl.BlockSpec((t, hp * dh), lambda h, i: (0, h))
    qblk = pl.BlockSpec((blk, hp * dh), lambda h, i: (i, h))
    call = dict(grid=(heads // hp, nq),
                in_specs=[qblk, full, full, qblk, pl.BlockSpec((hp, blk, LANES), lambda h, i: (h, i, 0))],
                out_specs=[qblk, full, full], out_shape=[jax.ShapeDtypeStruct((t, heads * dh), F32)] * 3, name=name)
    if scatter is None:
        return pl.pallas_call(body, compiler_params=_params("parallel", "arbitrary"), **call)(qs, ks, vs, do, carry)
    res = _call_with_exchange(body, scratch_shapes=[], args=(qs, ks, vs, do, carry), srcs=scatter,
                              scatter=[True] * len(scatter), **call)
    return [*res[:3], list(res[3:])]


def _sb_pre_bwd(proj, gq, gk, dq, dk, dv, *, heads, dh, col_q, col_k, name):
    t = proj.shape[0]
    tm = _pick(t, (384, 256, 128))

    def body(q_ref, k_ref, gq_ref, gk_ref, dq_ref, dk_ref, dv_ref, oq_ref, ok_ref, ov_ref, dgq_ref, dgk_ref):
        @pl.when((pl.program_id(0) == 0) & (pl.program_id(1) == 0))
        def _():
            dgq_ref[...] = jnp.zeros_like(dgq_ref)
            dgk_ref[...] = jnp.zeros_like(dgk_ref)

        dq_, gq_r = _rms_bwd(q_ref[...], gq_ref[...], dq_ref[...])
        dk_, gk_r = _rms_bwd(k_ref[...], gk_ref[...], dk_ref[...])
        oq_ref[...] = _bf(dq_)
        ok_ref[...] = _bf(dk_)
        ov_ref[...] = _bf(dv_ref[...])
        dgq_ref[...] += jnp.sum(gq_r, axis=0, keepdims=True)
        dgk_ref[...] += jnp.sum(gk_r, axis=0, keepdims=True)

    def cur(col):
        return pl.BlockSpec((tm, dh), lambda i, h: (i, col // dh + h))

    gspec = pl.BlockSpec((1, dh), lambda i, h: (0, 0))
    ospec = pl.BlockSpec((tm, dh), lambda i, h: (i, h))
    return pl.pallas_call(
        body, grid=(t // tm, heads), in_specs=[cur(col_q), cur(col_k), gspec, gspec, ospec, ospec, ospec],
        out_specs=[ospec, ospec, ospec, gspec, gspec],
        out_shape=[jax.ShapeDtypeStruct((t, heads * dh), BF16)] * 3 + [jax.ShapeDtypeStruct((1, dh), F32)] * 2,
        name=name, compiler_params=_params("arbitrary", "arbitrary"))(proj, proj, gq, gk, dq, dk, dv)


PEERS = N_DEV - 1


def _exchange_copies(ins, outs, send_sems, recv_sems, local_sems, scatter):
    x, y, c = lax.axis_index("x"), lax.axis_index("y"), lax.axis_index("c")
    me = 4 * x + 2 * y + c
    copies = []
    for a in range(len(ins)):
        own = ins[a].at[me] if scatter[a] else ins[a]
        copies.append(pltpu.make_async_copy(own, outs[a].at[me], local_sems.at[a]))
        for k in range(1, N_DEV):
            px = (x + (k >> 2 & 1)) % 2
            py = (y + (k >> 1 & 1)) % 2
            pc = (c + (k & 1)) % 2
            src = ins[a].at[4 * px + 2 * py + pc] if scatter[a] else ins[a]
            copies.append(pltpu.make_async_remote_copy(
                src_ref=src, dst_ref=outs[a].at[me], send_sem=send_sems.at[a * PEERS + k - 1],
                recv_sem=recv_sems.at[a * PEERS + k - 1], device_id=(px, py, pc), device_id_type=MESH))
    return copies


def _exchange_shapes(srcs, scatter):
    return [jax.ShapeDtypeStruct(s.shape if sc else (N_DEV,) + s.shape, s.dtype) for s, sc in zip(srcs, scatter)]


def _exchange_sems(n):
    return [pltpu.SemaphoreType.DMA((n * PEERS,)), pltpu.SemaphoreType.DMA((n * PEERS,)), pltpu.SemaphoreType.DMA((n,))]


def _exchange(srcs, *, scatter, name):
    n = len(srcs)

    def body(*refs):
        copies = _exchange_copies(refs[:n], refs[n:2 * n], *refs[2 * n:], scatter)
        for cp in copies:
            cp.start()
        for cp in copies:
            cp.wait()

    any_spec = pl.BlockSpec(memory_space=pl.ANY)
    return pl.pallas_call(
        body, in_specs=[any_spec] * n, out_specs=[any_spec] * n, out_shape=_exchange_shapes(srcs, scatter),
        scratch_shapes=_exchange_sems(n), name=name,
        compiler_params=pltpu.CompilerParams(has_side_effects=True))(*srcs)


def _gather_two_level(srcs, *, name):
    n = len(srcs)

    def body(*refs):
        ins, outs = refs[:n], refs[n:2 * n]
        send_sems, recv_sems, local_sems = refs[2 * n:]
        x, y, c = lax.axis_index("x"), lax.axis_index("y"), lax.axis_index("c")
        chips = [(1 - x, y), (x, 1 - y), (1 - x, 1 - y)]

        def slab(a, px, py, pc):
            return outs[a].at[4 * px + 2 * py + pc]

        def copy(a, k, block, to, src=None):
            return pltpu.make_async_remote_copy(
                src_ref=slab(a, *block) if src is None else src, dst_ref=slab(a, *block),
                send_sem=send_sems.at[a * PEERS + k], recv_sem=recv_sems.at[a * PEERS + k],
                device_id=to, device_id_type=MESH)

        mine = [pltpu.make_async_copy(ins[a], slab(a, x, y, c), local_sems.at[a]) for a in range(n)]
        first = [copy(a, 0, (x, y, c), (x, y, 1 - c), src=ins[a]) for a in range(n)]
        first += [copy(a, 1 + j, (x, y, c), (*chip, c), src=ins[a]) for j, chip in enumerate(chips) for a in range(n)]
        for cp in mine + first:
            cp.start()
        passed = []
        for j, chip in enumerate(chips):
            for a in range(n):
                copy(a, 1 + j, (*chip, c), (x, y, c)).wait_recv()
                passed.append(copy(a, 4 + j, (*chip, c), (x, y, 1 - c)))
                passed[-1].start()
        for a in range(n):
            copy(a, 0, (x, y, 1 - c), (x, y, c)).wait_recv()
            for j, chip in enumerate(chips):
                copy(a, 4 + j, (*chip, 1 - c), (x, y, c)).wait_recv()
        for cp in first + passed:
            cp.wait_send()
        for cp in mine:
            cp.wait()

    any_spec = pl.BlockSpec(memory_space=pl.ANY)
    return pl.pallas_call(
        body, in_specs=[any_spec] * n, out_specs=[any_spec] * n, out_shape=_exchange_shapes(srcs, [False] * n),
        scratch_shapes=_exchange_sems(n), name=name,
        compiler_params=pltpu.CompilerParams(has_side_effects=True))(*srcs)


def _call_with_exchange(body, *, grid, in_specs, out_specs, out_shape, scratch_shapes, args, srcs, scatter, name):
    n, n_in, n_out, n_scr = len(srcs), len(args), len(out_shape), len(scratch_shapes)

    def full_body(*refs):
        ins, xin = refs[:n_in], refs[n_in:n_in + n]
        outs, xout = refs[n_in + n:n_in + n + n_out], refs[n_in + n + n_out:n_in + 2 * n + n_out]
        scr = refs[n_in + 2 * n + n_out:]
        ids = [pl.program_id(a) for a in range(len(grid))]
        first = functools.reduce(jnp.logical_and, [i == 0 for i in ids])
        last = functools.reduce(jnp.logical_and, [i == g - 1 for i, g in zip(ids, grid)])
        copies = _exchange_copies(xin, xout, *scr[n_scr:], scatter)

        @pl.when(first)
        def _():
            for cp in copies:
                cp.start()

        body(*ins, *outs, *scr[:n_scr])

        @pl.when(last)
        def _():
            for cp in copies:
                cp.wait()

    any_spec = pl.BlockSpec(memory_space=pl.ANY)
    return pl.pallas_call(
        full_body, grid=grid, in_specs=list(in_specs) + [any_spec] * n, out_specs=list(out_specs) + [any_spec] * n,
        out_shape=list(out_shape) + _exchange_shapes(srcs, scatter),
        scratch_shapes=list(scratch_shapes) + _exchange_sems(n), name=name,
        compiler_params=pltpu.CompilerParams(dimension_semantics=("arbitrary",) * len(grid),
                                             vmem_limit_bytes=V7X_VMEM_LIMIT_BYTES, has_side_effects=True))(*args, *srcs)


def _adam_math(g, w, m, v):
    m2 = ADAM_B1 * m + (1.0 - ADAM_B1) * g
    v2 = ADAM_B2 * v + (1.0 - ADAM_B2) * (g * g)
    m_hat = m2 / (1.0 - ADAM_B1 ** ADAM_STEP)
    v_hat = v2 / (1.0 - ADAM_B2 ** ADAM_STEP)
    return -ADAM_LR * (m_hat / (jnp.sqrt(v_hat) + ADAM_EPS) + ADAM_WD * w), m2, v2


def _adamw_slabs(slabs, w, m, v, *, name):
    r, c = w.shape
    tr = _pick(r, (128, 64, 32, 16, 8)) if r % 8 == 0 else r

    def body(s_ref, w_ref, m_ref, v_ref, g_ref, d_ref, mo_ref, vo_ref):
        g = s_ref[0].astype(F32)
        for p in range(1, N_DEV):
            g = g + s_ref[p].astype(F32)
        g_ref[...] = g
        d_ref[...], mo_ref[...], vo_ref[...] = _adam_math(g, w_ref[...], m_ref[...], v_ref[...])

    spec = pl.BlockSpec((tr, c), lambda i: (i, 0))
    return pl.pallas_call(
        body, grid=(r // tr,), in_specs=[pl.BlockSpec((N_DEV, tr, c), lambda i: (0, i, 0)), spec, spec, spec],
        out_specs=[spec] * 4, out_shape=[jax.ShapeDtypeStruct((r, c), F32)] * 4, name=name,
        compiler_params=_params("parallel"))(slabs, w, m, v)


def _adamw_small(g, w, m, v, *, name):
    def body(g_ref, w_ref, m_ref, v_ref, d_ref, mo_ref, vo_ref):
        d_ref[...], mo_ref[...], vo_ref[...] = _adam_math(g_ref[...], w_ref[...], m_ref[...], v_ref[...])

    return pl.pallas_call(body, out_shape=[jax.ShapeDtypeStruct(w.shape, F32)] * 3, name=name)(g, w, m, v)


def _sum_slabs(slabs, *, name):
    def body(s_ref, o_ref):
        acc = s_ref[0]
        for p in range(1, N_DEV):
            acc = acc + s_ref[p]
        o_ref[...] = acc

    return pl.pallas_call(body, out_shape=jax.ShapeDtypeStruct(slabs.shape[1:], F32), name=name)(slabs)


def _gather_cols(g):
    return jnp.transpose(g, (1, 0, 2)).reshape(g.shape[1], -1)


def _col_slabs(a):
    return jnp.transpose(a.reshape(a.shape[0], N_DEV, -1), (1, 0, 2))


def _pad_lanes(a):
    return jnp.pad(a, ((0, 0), (0, LANES - a.shape[1])))


def _local_step(x, target, meta, g_mix, w_main, w_ab, cq, ck, cv, a_log, dt_bias, g_dn, g_sbq, g_sbk, g_ffn, rest,
                shards=False):
    seq, d = x.shape
    n_meta = meta.shape[0]
    heads = a_log.shape[1]
    qk = cq.shape[1]
    dvt = cv.shape[1]
    dk, dv = qk // heads, dvt // heads
    dh = g_sbq.shape[1]
    sbw = rest[2].shape[0] * N_DEV if shards else rest[1].shape[0]
    sb_heads = sbw // dh
    pad_l = (-n_meta) % CHUNK
    row_x = pad_l + n_meta
    rows = row_x + seq
    t = -(-rows // GDN_ROWS) * GDN_ROWS
    col_q, col_k, col_v, col_z = 0, qk, 2 * qk, 2 * qk + dvt
    col_sq = 2 * qk + 2 * dvt
    col_sk, col_sv, col_gd, col_gs = col_sq + sbw, col_sq + 2 * sbw, col_sq + 3 * sbw, col_sq + 3 * sbw + d

    def rows_pad(a):
        return jnp.concatenate([jnp.zeros((row_x, d), F32), a, jnp.zeros((t - rows, d), F32)], axis=0)

    h0 = jnp.concatenate([jnp.zeros((pad_l, d), F32), meta, x, jnp.zeros((t - rows, d), F32)], axis=0)
    tgt = rows_pad(target)
    a_log_p, dt_p = _pad_lanes(a_log), _pad_lanes(dt_bias)

    proj, n1 = _mm_norm(h0, g_mix, w_main, name="proj")
    pab = _mm_nn(n1, w_ab, out_dtype=F32, name="proj_ab")
    gk = dict(heads=heads, dk=dk, dv=dv, col_q=col_q, col_k=col_k, col_v=col_v, row_lo=pad_l, row_hi=rows)
    qn, kn, vv, g, beta = _gdn_pre(proj, pab, cq, ck, cv, a_log_p, dt_p, name="gdn_pre", **gk)
    if shards:
        u, w, pm, qd, kd, egl, tinv, (g_fi, g_bd, g_bs, g_out, g_fo) = _gdn_prep(
            qn, kn, vv, g, beta, name="gdn_prep", gather=list(rest))
        w_fi = _gather_cols(g_fi)
        d_ff = w_fi.shape[1] // 2
        w_bd, w_bs, w_out, w_fg, w_fu, w_fo = (g_bd.reshape(-1, d), g_bs.reshape(-1, d), g_out.reshape(-1, d),
                                               w_fi[:, :d_ff], w_fi[:, d_ff:], g_fo.reshape(-1, d))
    else:
        u, w, pm, qd, kd, egl, tinv = _gdn_prep(qn, kn, vv, g, beta, name="gdn_prep")
        w_bd, w_bs, w_out, w_fg, w_fu, w_fo = rest
    o_raw, o_dn, states = _gdn_scan(u, w, pm, qd, kd, egl, proj, g_dn, col_z=col_z, name="gdn_scan")
    qs, ks, vs = _sb_pre(proj, g_sbq, g_sbk, heads=sb_heads, dh=dh, col_q=col_sq, col_k=col_sk, col_v=col_sv,
                         name="sb_pre")
    o_sb, carry = _sb_fwd(qs, ks, vs, heads=sb_heads, dh=dh, key_lo=pad_l, name="sb_fwd")
    merged, br_dn, br_sb = _merge_fwd(o_dn, o_sb, w_bd, w_bs, proj, col_gd=col_gd, col_gs=col_gs, name="merge")
    h1 = _mm_res(h0, merged, w_out, name="mix_out")
    gate, up, act, n2 = _mm_norm_swiglu(h1, g_ffn, w_fg, w_fu, name="ffn_in")
    dy, dyb, lsum = _mm_res_loss(h1, act, w_fo, tgt, row0=row_x, nrows=seq, name="ffn_out_loss")

    dgate, dup = _swiglu_bwd(dyb, w_fo, gate, up, name="ffn_out_bwd")
    d_w_fo = _mm_tn(act, dyb, name="dw_ffn_out")
    d_w_fg = _mm_tn(n2, dgate, name="dw_ffn_gate")
    d_w_fu = _mm_tn(n2, dup, name="dw_ffn_up")
    dh1, dh1b, d_g_ffn = _mm_nt_rmsbwd([(dgate, w_fg), (dup, w_fu)], None, h1, g_ffn, dy, name="ffn_in_bwd")

    dbd, dbs, dgd, dgs = _merge_bwd(dh1b, w_out, proj, br_dn, br_sb, col_gd=col_gd, col_gs=col_gs, name="mix_out_bwd")
    d_w_out = _mm_tn(merged, dh1b, name="dw_out")
    d_w_bd = _mm_tn(o_dn, dbd, name="dw_branch_dn")
    d_w_bs = _mm_tn(o_sb, dbs, name="dw_branch_sb")
    do_dn = _mm_nt(dbd, w_bd, out_dtype=F32, name="branch_dn_bwd")
    do_sb = _mm_nt(dbs, w_bs, out_dtype=BF16, name="branch_sb_bwd")

    do_raw, dz, d_g_dn = _gdn_post_bwd(o_raw, proj, g_dn, do_dn, col_z=col_z, name="gdn_post_bwd")
    du, dw, dp, dqd, dkd, dgl = _gdn_bwd_scan(u, w, pm, qd, kd, egl, states, do_raw, name="gdn_bwd_scan")
    dqn, dkn, dvv, dg, dbeta = _gdn_bwd_prep(qn, kn, vv, g, beta, tinv, u, w, du, dw, dp, dqd, dkd, dgl,
                                            name="gdn_bwd_prep")
    dcq, dck, dcv, dpab, d_a_log, d_dt = _gdn_pre_bwd_a(proj, pab, cq, ck, cv, a_log_p, dt_p, dqn, dkn, dvv, dg, dbeta,
                                                        name="gdn_pre_bwd", **gk)
    dpq, d_cq = _conv_bwd(proj, dcq, cq, heads=heads, width=dk, col=col_q, name="conv_q_bwd")
    dpk, d_ck = _conv_bwd(proj, dck, ck, heads=heads, width=dk, col=col_k, name="conv_k_bwd")
    dpv, d_cv = _conv_bwd(proj, dcv, cv, heads=heads, width=dv, col=col_v, name="conv_v_bwd")

    early = None
    if shards:
        slabs = [_col_slabs(_bf(jnp.concatenate([d_w_fg, d_w_fu], axis=1)))]
        slabs += [_bf(a).reshape(N_DEV, -1, d) for a in (d_w_bd, d_w_bs, d_w_out, d_w_fo)]
        dqs, dks, dvs, early = _sb_bwd(qs, ks, vs, do_sb, carry, heads=sb_heads, dh=dh, key_lo=pad_l, name="sb_bwd",
                                       scatter=slabs)
    else:
        dqs, dks, dvs = _sb_bwd(qs, ks, vs, do_sb, carry, heads=sb_heads, dh=dh, key_lo=pad_l, name="sb_bwd")
    dsq, dsk, dsv, d_g_sbq, d_g_sbk = _sb_pre_bwd(proj, g_sbq, g_sbk, dqs, dks, dvs, heads=sb_heads, dh=dh,
                                                   col_q=col_sq, col_k=col_sk, name="sb_pre_bwd")

    dproj = jnp.concatenate([dpq, dpk, dpv, dz, dsq, dsk, dsv, dgd, dgs], axis=1)
    dpab_b = _bf(dpab)
    d_w_main = _mm_tn(n1, dproj, name="dw_in_main")
    d_w_ab = _mm_tn(n1, dpab_b, name="dw_in_ab")
    s_in = None
    if shards:
        d_w_in = jnp.concatenate([d_w_main[:, :col_sq], d_w_ab[:, :2 * heads], d_w_main[:, col_sq:]], axis=1)
        dh0, _, d_g_mix, (s_in,) = _mm_nt_rmsbwd([(dproj, w_main)], (dpab_b, w_ab), h0, g_mix, dh1, name="proj_bwd",
                                                 scatter=[_col_slabs(_bf(d_w_in))])
    else:
        dh0, _, d_g_mix = _mm_nt_rmsbwd([(dproj, w_main)], (dpab_b, w_ab), h0, g_mix, dh1, name="proj_bwd")

    return dict(s_in=s_in,lsum=lsum, grad_x=dh0[row_x:rows], d_meta=dh0[pad_l:row_x], d_g_mix=d_g_mix, d_w_main=d_w_main,
                d_w_ab=d_w_ab, d_cq=d_cq, d_ck=d_ck, d_cv=d_cv, d_a_log=d_a_log[:, :heads], d_dt=d_dt[:, :heads],
                d_g_dn=d_g_dn, d_g_sbq=d_g_sbq, d_g_sbk=d_g_sbk, d_w_bd=d_w_bd, d_w_bs=d_w_bs, d_w_out=d_w_out,
                d_g_ffn=d_g_ffn, d_w_fg=d_w_fg, d_w_fu=d_w_fu, d_w_fo=d_w_fo, early=early)


def _pack(parts):
    flat = []
    for a in parts:
        a = a.reshape(-1)
        flat.append(jnp.pad(a, (0, (-a.shape[0]) % LANES)))
    v = jnp.concatenate(flat)
    v = jnp.pad(v, (0, (-v.shape[0]) % (8 * LANES)))
    return v.reshape(-1, LANES)


def _unpack(packed, shapes):
    flat = packed.reshape(-1)
    out, pos = [], 0
    for s in shapes:
        n = math.prod(s)
        out.append(flat[pos:pos + n].reshape(s))
        pos += n + (-n) % LANES
    return out


def kernel(x, meta_tokens, norm_mix_gain, w_in, conv_q, conv_k, conv_v, dn_a_log, dn_dt_bias, dn_out_norm_gain, sb_q_norm_gain, sb_k_norm_gain, w_branch_dn, w_branch_sb, w_out, norm_ffn_gain, w_ffn_in, w_ffn_out, loss_target, m_meta_tokens, m_norm_mix_gain, m_w_in, m_conv_q, m_conv_k, m_conv_v, m_dn_a_log, m_dn_dt_bias, m_dn_out_norm_gain, m_sb_q_norm_gain, m_sb_k_norm_gain, m_w_branch_dn, m_w_branch_sb, m_w_out, m_norm_ffn_gain, m_w_ffn_in, m_w_ffn_out, v_meta_tokens, v_norm_mix_gain, v_w_in, v_conv_q, v_conv_k, v_conv_v, v_dn_a_log, v_dn_dt_bias, v_dn_out_norm_gain, v_sb_q_norm_gain, v_sb_k_norm_gain, v_w_branch_dn, v_w_branch_sb, v_w_out, v_norm_ffn_gain, v_w_ffn_in, v_w_ffn_out):
    me = 4 * lax.axis_index("x") + 2 * lax.axis_index("y") + lax.axis_index("c")
    heads = dn_a_log.shape[1]
    d = x.shape[2]
    qk = conv_q.shape[2] * N_DEV
    dvt = conv_v.shape[2] * N_DEV
    col_ab = 2 * qk + 2 * dvt

    small_shapes = [meta_tokens.shape, conv_q.shape[1:], conv_k.shape[1:], conv_v.shape[1:]]
    small = _pack([meta_tokens, conv_q[0], conv_k[0], conv_v[0]])
    g_in, g_small = _gather_two_level([_bf(w_in[0]), small], name="gather_w_in")
    w_full = _gather_cols(g_in)
    w_main = jnp.concatenate([w_full[:, :col_ab], w_full[:, col_ab + 2 * heads:]], axis=1)
    w_ab = _pad_lanes(w_full[:, col_ab:col_ab + 2 * heads])
    parts = [_unpack(g_small[p], small_shapes) for p in range(N_DEV)]
    meta_f, cq_f, ck_f, cv_f = (jnp.concatenate([parts[p][a] for p in range(N_DEV)], axis=1) for a in range(4))

    r = _local_step(x[0], loss_target[0], meta_f, norm_mix_gain, w_main, w_ab, cq_f, ck_f, cv_f, dn_a_log, dn_dt_bias,
                    dn_out_norm_gain, sb_q_norm_gain, sb_k_norm_gain, norm_ffn_gain,
                    (_bf(w_ffn_in[0]), _bf(w_branch_dn[0]), _bf(w_branch_sb[0]), _bf(w_out[0]), _bf(w_ffn_out[0])),
                    shards=True)
    s_fi, s_bd, s_bs, s_out, s_fo = r["early"]
    s_in = r["s_in"]

    loss_part = (0.5 / d) * jnp.sum(r["lsum"], axis=1, keepdims=True)
    small_g = [r["d_meta"], r["d_g_mix"], r["d_cq"], r["d_ck"], r["d_cv"], r["d_a_log"], r["d_dt"], r["d_g_dn"],
               r["d_g_sbq"], r["d_g_sbk"], r["d_g_ffn"], loss_part]
    (g_packs,) = _exchange([_pack(small_g)], scatter=[False], name="gather_small_grads")
    (g_meta, g_mix, g_cq, g_ck, g_cv, g_al, g_dt, g_gdn, g_sbq, g_sbk, g_ffn, loss) = _unpack(
        _sum_slabs(g_packs, name="sum_small_grads"), [a.shape for a in small_g])

    def mine(a, width):
        return lax.dynamic_slice_in_dim(a, me * width, width, axis=1)

    big = dict(w_in=(s_in, w_in, m_w_in, v_w_in), w_branch_dn=(s_bd, w_branch_dn, m_w_branch_dn, v_w_branch_dn),
               w_branch_sb=(s_bs, w_branch_sb, m_w_branch_sb, v_w_branch_sb), w_out=(s_out, w_out, m_w_out, v_w_out),
               w_ffn_in=(s_fi, w_ffn_in, m_w_ffn_in, v_w_ffn_in), w_ffn_out=(s_fo, w_ffn_out, m_w_ffn_out, v_w_ffn_out))
    tiny = dict(meta_tokens=(mine(g_meta, d // N_DEV), meta_tokens, m_meta_tokens, v_meta_tokens),
                norm_mix_gain=(g_mix, norm_mix_gain, m_norm_mix_gain, v_norm_mix_gain),
                conv_q=(mine(g_cq, qk // N_DEV), conv_q[0], m_conv_q[0], v_conv_q[0]),
                conv_k=(mine(g_ck, qk // N_DEV), conv_k[0], m_conv_k[0], v_conv_k[0]),
                conv_v=(mine(g_cv, dvt // N_DEV), conv_v[0], m_conv_v[0], v_conv_v[0]),
                dn_a_log=(g_al, dn_a_log, m_dn_a_log, v_dn_a_log), dn_dt_bias=(g_dt, dn_dt_bias, m_dn_dt_bias, v_dn_dt_bias),
                dn_out_norm_gain=(g_gdn, dn_out_norm_gain, m_dn_out_norm_gain, v_dn_out_norm_gain),
                sb_q_norm_gain=(g_sbq, sb_q_norm_gain, m_sb_q_norm_gain, v_sb_q_norm_gain),
                sb_k_norm_gain=(g_sbk, sb_k_norm_gain, m_sb_k_norm_gain, v_sb_k_norm_gain),
                norm_ffn_gain=(g_ffn, norm_ffn_gain, m_norm_ffn_gain, v_norm_ffn_gain))
    order = ["meta_tokens", "norm_mix_gain", "w_in", "conv_q", "conv_k", "conv_v", "dn_a_log", "dn_dt_bias",
             "dn_out_norm_gain", "sb_q_norm_gain", "sb_k_norm_gain", "w_branch_dn", "w_branch_sb", "w_out",
             "norm_ffn_gain", "w_ffn_in", "w_ffn_out"]
    grads, deltas, new_m, new_v = [], [], [], []
    for name in order:
        if name in big:
            slabs, w, m, v = big[name]
            g, dl, mo, vo = _adamw_slabs(slabs, w[0], m[0], v[0], name="adamw_" + name)
            like = w.shape
        else:
            g, w, m, v = tiny[name]
            like = dict(conv_q=conv_q, conv_k=conv_k, conv_v=conv_v).get(name, w).shape
            dl, mo, vo = _adamw_small(g, w, m, v, name="adamw_" + name)
        for lst, a in ((grads, g), (deltas, dl), (new_m, mo), (new_v, vo)):
            lst.append(a.reshape(like))
    return (loss.reshape(()), r["grad_x"][None], *grads, *deltas, *new_m, *new_v)
```

```python
import functools
import math

import jax
import jax.numpy as jnp
from jax import lax
from jax.experimental import pallas as pl
from jax.experimental.pallas import tpu as pltpu

F32 = jnp.float32
BF16 = jnp.bfloat16

N_DEV = 8
CHUNK = 64
CHUNK_SHIFT = 6
GDN_ROWS = 2 * CHUNK
SB_BLOCK = 128
SB_HEADS_PER_STEP = 2
LANES = 128
RMS_EPS = 1e-6
L2_EPS = 1e-6
ADAM_LR = 0.001
ADAM_B1 = 0.9
ADAM_B2 = 0.999
ADAM_EPS = 1e-08
ADAM_WD = 0.01
ADAM_STEP = 10
V7X_VMEM_LIMIT_BYTES = 56 * 1024 * 1024
MM_TN_OUT_BLOCK_BYTES = 6 * 1024 * 1024
ROWS_BIG = (1056, 512, 384, 256, 128)
ROWS_MID = (528, 384, 256, 128)
SCAN_HEADS = 4

MESH = pl.DeviceIdType.MESH


def _params(*sem):
    return pltpu.CompilerParams(dimension_semantics=sem or None, vmem_limit_bytes=V7X_VMEM_LIMIT_BYTES)


def _pick(n, cands):
    for c in cands:
        if n % c == 0:
            return c
    raise ValueError(f"no block size among {cands} divides {n}")


def _bf(x):
    return x.astype(BF16)


def _dot(a, b):
    return jnp.dot(a, b, preferred_element_type=F32)


def _dot_nt(a, b):
    return lax.dot_general(a, b, (((1,), (1,)), ((), ())), preferred_element_type=F32)


def _dot_tn(a, b):
    return lax.dot_general(a, b, (((0,), (0,)), ((), ())), preferred_element_type=F32)


def _split2(x):
    hi = _bf(x)
    return hi, _bf(x - hi.astype(F32))


def _split3(x):
    hi = _bf(x)
    r = x - hi.astype(F32)
    mid = _bf(r)
    return hi, mid, _bf(r - mid.astype(F32))


def _dot_hp(a, b, dot=_dot):
    ah, al = _split2(a)
    bh, bl = _split2(b)
    return dot(ah, bh) + dot(ah, bl) + dot(al, bh)


def _dot_exact_r(x, m, dot=_dot):
    h, mi, lo = _split3(x)
    return dot(h, m) + dot(mi, m) + dot(lo, m)


def _dot_exact_l(m, x, dot=_dot):
    h, mi, lo = _split3(x)
    return dot(m, h) + dot(m, mi) + dot(m, lo)


def _sigmoid(x):
    return 1.0 / (1.0 + jnp.exp(-x))


def _silu(x):
    return x * _sigmoid(x)


def _silu_grad(x):
    s = _sigmoid(x)
    return s * (1.0 + x * (1.0 - s))


def _softplus(x):
    return jnp.maximum(x, 0.0) + jnp.log(1.0 + jnp.exp(-jnp.abs(x)))


def _rms_fwd(h, gain):
    r = lax.rsqrt(jnp.mean(h * h, axis=-1, keepdims=True) + RMS_EPS)
    return h * r * gain


def _rms_bwd(h, gain, dy):
    r = lax.rsqrt(jnp.mean(h * h, axis=-1, keepdims=True) + RMS_EPS)
    dyg = dy * gain
    dh = r * dyg - h * (r * r * r) * jnp.mean(dyg * h, axis=-1, keepdims=True)
    return dh, dy * h * r


def _iota(shape, dim):
    return lax.broadcasted_iota(jnp.int32, shape, dim)


def _lane_pick(x, idx):
    return jnp.sum(jnp.where(_iota(x.shape, 1) == idx, x, 0.0), axis=1, keepdims=True)


def _mm_nn(a, b, *, out_dtype, name):
    m, k = a.shape
    n = b.shape[1]
    tm, tn = _pick(m, ROWS_BIG), _pick(n, (1024, 512, 256, 128))

    def body(a_ref, b_ref, o_ref):
        o_ref[...] = _dot(a_ref[...], b_ref[...]).astype(out_dtype)

    return pl.pallas_call(
        body, grid=(m // tm, n // tn),
        in_specs=[pl.BlockSpec((tm, k), lambda i, j: (i, 0)), pl.BlockSpec((k, tn), lambda i, j: (0, j))],
        out_specs=pl.BlockSpec((tm, tn), lambda i, j: (i, j)),
        out_shape=jax.ShapeDtypeStruct((m, n), out_dtype), name=name,
        compiler_params=_params("parallel", "parallel"))(a, b)


def _mm_nt(a, b, *, out_dtype, name):
    m, k = a.shape
    n = b.shape[0]
    tm, tn = _pick(m, ROWS_BIG), _pick(n, (1024, 512, 256, 128))

    def body(a_ref, b_ref, o_ref):
        o_ref[...] = _dot_nt(a_ref[...], b_ref[...]).astype(out_dtype)

    return pl.pallas_call(
        body, grid=(m // tm, n // tn),
        in_specs=[pl.BlockSpec((tm, k), lambda i, j: (i, 0)), pl.BlockSpec((tn, k), lambda i, j: (j, 0))],
        out_specs=pl.BlockSpec((tm, tn), lambda i, j: (i, j)),
        out_shape=jax.ShapeDtypeStruct((m, n), out_dtype), name=name,
        compiler_params=_params("parallel", "parallel"))(a, b)


def _mm_tn(a, b, *, name):
    t, m = a.shape
    n = b.shape[1]
    tn = _pick(n, (2816, 2048, 1408, 1024, 512, 256, 128))
    tm = _pick(m, tuple(c for c in (1408, 1024, 512, 256, 128) if c * tn * 4 <= MM_TN_OUT_BLOCK_BYTES))
    tk = _pick(t, (1408, 1024, 512, 384, 256, 128))

    def body(a_ref, b_ref, o_ref):
        @pl.when(pl.program_id(2) == 0)
        def _():
            o_ref[...] = jnp.zeros_like(o_ref)

        o_ref[...] += _dot_tn(a_ref[...], b_ref[...])

    return pl.pallas_call(
        body, grid=(m // tm, n // tn, t // tk),
        in_specs=[pl.BlockSpec((tk, tm), lambda i, j, k: (k, i)), pl.BlockSpec((tk, tn), lambda i, j, k: (k, j))],
        out_specs=pl.BlockSpec((tm, tn), lambda i, j, k: (i, j)),
        out_shape=jax.ShapeDtypeStruct((m, n), F32), name=name,
        compiler_params=_params("parallel", "parallel", "arbitrary"))(a, b)


def _mm_norm(h, gain, w, *, name):
    m, k = h.shape
    n = w.shape[1]
    tm, tn = _pick(m, ROWS_BIG), _pick(n, (1024, 512, 256, 128))

    def body(h_ref, g_ref, w_ref, o_ref, n_ref):
        @pl.when(pl.program_id(1) == 0)
        def _():
            n_ref[...] = _bf(_rms_fwd(h_ref[...], g_ref[...]))

        o_ref[...] = _dot(n_ref[...], w_ref[...])

    return pl.pallas_call(
        body, grid=(m // tm, n // tn),
        in_specs=[pl.BlockSpec((tm, k), lambda i, j: (i, 0)), pl.BlockSpec((1, k), lambda i, j: (0, 0)),
                  pl.BlockSpec((k, tn), lambda i, j: (0, j))],
        out_specs=[pl.BlockSpec((tm, tn), lambda i, j: (i, j)), pl.BlockSpec((tm, k), lambda i, j: (i, 0))],
        out_shape=[jax.ShapeDtypeStruct((m, n), F32), jax.ShapeDtypeStruct((m, k), BF16)], name=name,
        compiler_params=_params("parallel", "arbitrary"))(h, gain, w)


def _mm_norm_swiglu(h, gain, wg, wu, *, name):
    m, k = h.shape
    n = wg.shape[1]
    tm, tn = _pick(m, ROWS_MID), _pick(n, (1408, 1024, 512, 256, 128))

    def body(h_ref, g_ref, wg_ref, wu_ref, gate_ref, up_ref, act_ref, n_ref):
        @pl.when(pl.program_id(1) == 0)
        def _():
            n_ref[...] = _bf(_rms_fwd(h_ref[...], g_ref[...]))

        gate = _dot(n_ref[...], wg_ref[...])
        up = _dot(n_ref[...], wu_ref[...])
        gate_ref[...] = gate
        up_ref[...] = up
        act_ref[...] = _bf(_silu(gate) * up)

    wspec = pl.BlockSpec((k, tn), lambda i, j: (0, j))
    ospec = pl.BlockSpec((tm, tn), lambda i, j: (i, j))
    return pl.pallas_call(
        body, grid=(m // tm, n // tn),
        in_specs=[pl.BlockSpec((tm, k), lambda i, j: (i, 0)), pl.BlockSpec((1, k), lambda i, j: (0, 0)), wspec, wspec],
        out_specs=[ospec, ospec, ospec, pl.BlockSpec((tm, k), lambda i, j: (i, 0))],
        out_shape=[jax.ShapeDtypeStruct((m, n), F32), jax.ShapeDtypeStruct((m, n), F32),
                   jax.ShapeDtypeStruct((m, n), BF16), jax.ShapeDtypeStruct((m, k), BF16)], name=name,
        compiler_params=_params("parallel", "arbitrary"))(h, gain, wg, wu)


def _mm_res(res, a, b, *, name):
    m, k = a.shape
    n = b.shape[1]
    tm, tn = _pick(m, ROWS_BIG), _pick(n, (1024, 512, 256, 128))

    def body(r_ref, a_ref, b_ref, o_ref):
        o_ref[...] = r_ref[...] + _dot(a_ref[...], b_ref[...])

    return pl.pallas_call(
        body, grid=(m // tm, n // tn),
        in_specs=[pl.BlockSpec((tm, tn), lambda i, j: (i, j)), pl.BlockSpec((tm, k), lambda i, j: (i, 0)),
                  pl.BlockSpec((k, tn), lambda i, j: (0, j))],
        out_specs=pl.BlockSpec((tm, tn), lambda i, j: (i, j)),
        out_shape=jax.ShapeDtypeStruct((m, n), F32), name=name,
        compiler_params=_params("parallel", "parallel"))(res, a, b)


def _mm_res_loss(res, a, b, target, *, row0, nrows, name):
    m, k = a.shape
    n = b.shape[1]
    tm = _pick(m, ROWS_MID)

    def body(r_ref, a_ref, b_ref, t_ref, dy_ref, dyb_ref, ls_ref):
        i = pl.program_id(0)

        @pl.when(i == 0)
        def _():
            ls_ref[...] = jnp.zeros_like(ls_ref)

        y = r_ref[...] + _dot(a_ref[...], b_ref[...])
        row = i * tm + _iota((tm, n), 0)
        e = jnp.where((row >= row0) & (row < row0 + nrows), y - t_ref[...], 0.0)
        dy = e / n
        dy_ref[...] = dy
        dyb_ref[...] = _bf(dy)
        ls_ref[...] += jnp.sum(e * e, axis=0, keepdims=True)

    rspec = pl.BlockSpec((tm, n), lambda i: (i, 0))
    return pl.pallas_call(
        body, grid=(m // tm,),
        in_specs=[rspec, pl.BlockSpec((tm, k), lambda i: (i, 0)), pl.BlockSpec((k, n), lambda i: (0, 0)), rspec],
        out_specs=[rspec, rspec, pl.BlockSpec((1, n), lambda i: (0, 0))],
        out_shape=[jax.ShapeDtypeStruct((m, n), F32), jax.ShapeDtypeStruct((m, n), BF16),
                   jax.ShapeDtypeStruct((1, n), F32)], name=name,
        compiler_params=_params("arbitrary"))(res, a, b, target)


def _merge_fwd(o_dn, o_sb, wbd, wbs, proj, *, col_gd, col_gs, name):
    m, kd = o_dn.shape
    ks = o_sb.shape[1]
    n = wbd.shape[1]
    tm = _pick(m, ROWS_BIG)
    tn = _pick(math.gcd(n, math.gcd(col_gd, col_gs)), (512, 256, 128))

    def body(od_ref, os_ref, wd_ref, ws_ref, gd_ref, gs_ref, mg_ref, bd_ref, bs_ref):
        bd = _dot(od_ref[...], wd_ref[...])
        bs = _dot(os_ref[...], ws_ref[...])
        bd_ref[...] = bd
        bs_ref[...] = bs
        mg_ref[...] = _bf(_sigmoid(gd_ref[...]) * bd + _sigmoid(gs_ref[...]) * bs)

    ospec = pl.BlockSpec((tm, tn), lambda i, j: (i, j))
    return pl.pallas_call(
        body, grid=(m // tm, n // tn),
        in_specs=[pl.BlockSpec((tm, kd), lambda i, j: (i, 0)), pl.BlockSpec((tm, ks), lambda i, j: (i, 0)),
                  pl.BlockSpec((kd, tn), lambda i, j: (0, j)), pl.BlockSpec((ks, tn), lambda i, j: (0, j)),
                  pl.BlockSpec((tm, tn), lambda i, j: (i, col_gd // tn + j)),
                  pl.BlockSpec((tm, tn), lambda i, j: (i, col_gs // tn + j))],
        out_specs=[ospec, ospec, ospec],
        out_shape=[jax.ShapeDtypeStruct((m, n), BF16), jax.ShapeDtypeStruct((m, n), F32),
                   jax.ShapeDtypeStruct((m, n), F32)], name=name,
        compiler_params=_params("parallel", "parallel"))(o_dn, o_sb, wbd, wbs, proj, proj)


def _merge_bwd(dh, w_out, proj, br_dn, br_sb, *, col_gd, col_gs, name):
    m, k = dh.shape
    n = w_out.shape[0]
    tm = _pick(m, ROWS_BIG)
    tn = _pick(math.gcd(n, math.gcd(col_gd, col_gs)), (512, 256, 128))

    def body(dh_ref, w_ref, gd_ref, gs_ref, bd_ref, bs_ref, dbd_ref, dbs_ref, dgd_ref, dgs_ref):
        dm = _dot_nt(dh_ref[...], w_ref[...])
        sd = _sigmoid(gd_ref[...])
        ss = _sigmoid(gs_ref[...])
        dbd_ref[...] = _bf(dm * sd)
        dbs_ref[...] = _bf(dm * ss)
        dgd_ref[...] = _bf(dm * bd_ref[...] * sd * (1.0 - sd))
        dgs_ref[...] = _bf(dm * bs_ref[...] * ss * (1.0 - ss))

    ospec = pl.BlockSpec((tm, tn), lambda i, j: (i, j))
    return pl.pallas_call(
        body, grid=(m // tm, n // tn),
        in_specs=[pl.BlockSpec((tm, k), lambda i, j: (i, 0)), pl.BlockSpec((tn, k), lambda i, j: (j, 0)),
                  pl.BlockSpec((tm, tn), lambda i, j: (i, col_gd // tn + j)),
                  pl.BlockSpec((tm, tn), lambda i, j: (i, col_gs // tn + j)), ospec, ospec],
        out_specs=[ospec] * 4,
        out_shape=[jax.ShapeDtypeStruct((m, n), BF16)] * 4, name=name,
        compiler_params=_params("parallel", "parallel"))(dh, w_out, proj, proj, br_dn, br_sb)


def _swiglu_bwd(dy, wfo, gate, up, *, name):
    m, k = dy.shape
    n = wfo.shape[0]
    tm, tn = _pick(m, ROWS_MID), _pick(n, (1408, 1024, 512, 256, 128))

    def body(dy_ref, w_ref, g_ref, u_ref, dg_ref, du_ref):
        da = _dot_nt(dy_ref[...], w_ref[...])
        g = g_ref[...]
        dg_ref[...] = _bf(da * u_ref[...] * _silu_grad(g))
        du_ref[...] = _bf(da * _silu(g))

    ospec = pl.BlockSpec((tm, tn), lambda i, j: (i, j))
    return pl.pallas_call(
        body, grid=(m // tm, n // tn),
        in_specs=[pl.BlockSpec((tm, k), lambda i, j: (i, 0)), pl.BlockSpec((tn, k), lambda i, j: (j, 0)), ospec, ospec],
        out_specs=[ospec, ospec], out_shape=[jax.ShapeDtypeStruct((m, n), BF16)] * 2, name=name,
        compiler_params=_params("parallel", "parallel"))(dy, wfo, gate, up)


def _mm_nt_rmsbwd(pairs, extra, h, gain, dres, *, name, scatter=None):
    m, k = pairs[0][0].shape
    n = h.shape[1]
    tm = _pick(m, ROWS_MID)
    tk = _pick(k, (1408, 1024, 512, 256, 128))
    nk = k // tk
    np_ = len(pairs)

    def body(*refs):
        ab = refs[:2 * np_]
        pos = 2 * np_
        ex = refs[pos:pos + 2] if extra is not None else ()
        pos += len(ex)
        h_ref, g_ref, r_ref, dh_ref, dhb_ref, dg_ref, acc_ref = refs[pos:]
        i, kk = pl.program_id(0), pl.program_id(1)

        @pl.when((i == 0) & (kk == 0))
        def _():
            dg_ref[...] = jnp.zeros_like(dg_ref)

        part = _dot_nt(ab[0][...], ab[1][...])
        for p in range(1, np_):
            part += _dot_nt(ab[2 * p][...], ab[2 * p + 1][...])

        @pl.when(kk == 0)
        def _():
            first = part
            if ex:
                first = first + _dot_nt(ex[0][...], ex[1][...])
            acc_ref[...] = first

        @pl.when(kk > 0)
        def _():
            acc_ref[...] += part

        @pl.when(kk == nk - 1)
        def _():
            dh, dgr = _rms_bwd(h_ref[...], g_ref[...], acc_ref[...])
            dh = dh + r_ref[...]
            dh_ref[...] = dh
            dhb_ref[...] = _bf(dh)
            dg_ref[...] += jnp.sum(dgr, axis=0, keepdims=True)

    in_specs, args = [], []
    for a, b in pairs:
        in_specs += [pl.BlockSpec((tm, tk), lambda i, kk: (i, kk)), pl.BlockSpec((n, tk), lambda i, kk: (0, kk))]
        args += [a, b]
    if extra is not None:
        k2 = extra[0].shape[1]
        in_specs += [pl.BlockSpec((tm, k2), lambda i, kk: (i, 0)), pl.BlockSpec((n, k2), lambda i, kk: (0, 0))]
        args += list(extra)
    rspec = pl.BlockSpec((tm, n), lambda i, kk: (i, 0))
    in_specs += [rspec, pl.BlockSpec((1, n), lambda i, kk: (0, 0)), rspec]
    call = dict(grid=(m // tm, nk), in_specs=in_specs,
                out_specs=[rspec, rspec, pl.BlockSpec((1, n), lambda i, kk: (0, 0))],
                out_shape=[jax.ShapeDtypeStruct((m, n), F32), jax.ShapeDtypeStruct((m, n), BF16),
                           jax.ShapeDtypeStruct((1, n), F32)],
                scratch_shapes=[pltpu.VMEM((tm, n), F32)], name=name)
    if scatter is None:
        return pl.pallas_call(body, compiler_params=_params("arbitrary", "arbitrary"), **call)(*args, h, gain, dres)
    res = _call_with_exchange(body, args=(*args, h, gain, dres), srcs=scatter, scatter=[True] * len(scatter), **call)
    return [*res[:3], list(res[3:])]


def _conv_taps(cur, prev8, w_ref, first):
    nk = w_ref.shape[0]
    out = cur * w_ref[nk - 1:nk, :]
    for s in range(1, nk):
        out += _shift_down(cur, prev8, s, first) * w_ref[nk - 1 - s:nk - s, :]
    return out


def _shift_up(cur, next8, s, last):
    rows = cur.shape[0]
    row = _iota(cur.shape, 0)
    next8 = jnp.where(last, 0.0, next8)
    sh = pltpu.roll(cur, rows - s, axis=0)
    nh = jnp.tile(pltpu.roll(next8, 8 - s, axis=0), (rows // 8, 1))
    return jnp.where(row >= rows - s, nh, sh)


def _shift_down(cur, prev8, s, first):
    rows = cur.shape[0]
    row = _iota(cur.shape, 0)
    prev8 = jnp.where(first, 0.0, prev8)
    sh = pltpu.roll(cur, s, axis=0)
    ph = jnp.tile(pltpu.roll(prev8, s, axis=0), (rows // 8, 1))
    return jnp.where(row < s, ph, sh)


def _gdn_pre(proj, pab, cq, ck, cv, a_log, dt_bias, *, heads, dk, dv, col_q, col_k, col_v, row_lo, row_hi, name):
    t = proj.shape[0]
    tm = _pick(t, (384, 256, 128))
    nb = t // tm

    def body(pq_ref, pqp_ref, pk_ref, pkp_ref, pv_ref, pvp_ref, ab_ref, cq_ref, ck_ref, cv_ref, al_ref, dt_ref,
             qn_ref, kn_ref, v_ref, g_ref, b_ref):
        h, i = pl.program_id(0), pl.program_id(1)
        first = i == 0
        row = i * tm + _iota((tm, 1), 0)
        valid = (row >= row_lo) & (row < row_hi)
        q1 = _silu(_conv_taps(pq_ref[...], pqp_ref[...], cq_ref, first))
        k1 = _silu(_conv_taps(pk_ref[...], pkp_ref[...], ck_ref, first))
        v1 = _silu(_conv_taps(pv_ref[...], pvp_ref[...], cv_ref, first))
        qn_ref[...] = jnp.where(valid, q1 * lax.rsqrt(jnp.sum(q1 * q1, axis=-1, keepdims=True) + L2_EPS), 0.0)
        kn_ref[...] = jnp.where(valid, k1 * lax.rsqrt(jnp.sum(k1 * k1, axis=-1, keepdims=True) + L2_EPS), 0.0)
        v_ref[...] = jnp.where(valid, v1, 0.0)
        ab = ab_ref[...]
        da = _lane_pick(ab, h)
        db = _lane_pick(ab, heads + h)
        a = _lane_pick(al_ref[...], h)
        dtb = _lane_pick(dt_ref[...], h)
        g_ref[...] = jnp.where(valid, -jnp.exp(a) * _softplus(da + dtb), 0.0)
        b_ref[...] = jnp.where(valid, _sigmoid(db), 0.0)

    def cur(width, col):
        return pl.BlockSpec((tm, width), lambda h, i: (i, col // width + h))

    def prev(width, col):
        return pl.BlockSpec((8, width), lambda h, i: (jnp.maximum(i * (tm // 8) - 1, 0), col // width + h))

    def out(width):
        return pl.BlockSpec((None, tm, width), lambda h, i: (h, i, 0))

    small = pl.BlockSpec((1, LANES), lambda h, i: (0, 0))
    return pl.pallas_call(
        body, grid=(heads, nb),
        in_specs=[cur(dk, col_q), prev(dk, col_q), cur(dk, col_k), prev(dk, col_k), cur(dv, col_v), prev(dv, col_v),
                  pl.BlockSpec((tm, LANES), lambda h, i: (i, 0)),
                  pl.BlockSpec((cq.shape[0], dk), lambda h, i: (0, h)), pl.BlockSpec((ck.shape[0], dk), lambda h, i: (0, h)),
                  pl.BlockSpec((cv.shape[0], dv), lambda h, i: (0, h)), small, small],
        out_specs=[out(dk), out(dk), out(dv), out(1), out(1)],
        out_shape=[jax.ShapeDtypeStruct((heads, t, dk), F32), jax.ShapeDtypeStruct((heads, t, dk), F32),
                   jax.ShapeDtypeStruct((heads, t, dv), F32), jax.ShapeDtypeStruct((heads, t, 1), F32),
                   jax.ShapeDtypeStruct((heads, t, 1), F32)], name=name,
        compiler_params=_params("parallel", "parallel"))(proj, proj, proj, proj, proj, proj, pab, cq, ck, cv, a_log, dt_bias)


def _chunk_masks(rows=GDN_ROWS, row0=0):
    ri = row0 + _iota((rows, GDN_ROWS), 0)
    ci = _iota((rows, GDN_ROWS), 1)
    same = jnp.right_shift(ri, CHUNK_SHIFT) == jnp.right_shift(ci, CHUNK_SHIFT)
    return same, same & (ri >= ci), same & (ri > ci), ri == ci


def _col_to_row(col, eye):
    return jnp.sum(jnp.where(eye, col, 0.0), axis=0, keepdims=True)


def _row_to_col(row, eye):
    return jnp.sum(jnp.where(eye, row, 0.0), axis=1, keepdims=True)


def _chunk_common(blocks, dk_scale):
    same, incl, strict, eye = _chunk_masks()
    tri = jnp.where(incl, 1.0, 0.0).astype(BF16)
    tot = jnp.where(same, 1.0, 0.0).astype(BF16)
    gbs = [jnp.broadcast_to(g, (GDN_ROWS, LANES)) for _, _, g, _ in blocks]
    gams = [jnp.max(_dot_exact_l(tri, gb), axis=1, keepdims=True) for gb in gbs]
    lasts = [jnp.max(_dot_exact_l(tot, gb), axis=1, keepdims=True) for gb in gbs]
    kbs = [kn * beta for _, kn, _, beta in blocks]
    qts = [qn * dk_scale for qn, _, _, _ in blocks]
    boths = [_dot_nt(_bf(jnp.concatenate([kb, qt], axis=0)), _bf(blk[1]))
             for kb, qt, blk in zip(kbs, qts, blocks)]
    out = []
    for gam, last, kb, qt, both in zip(gams, lasts, kbs, qts, boths):
        diff = gam - _col_to_row(gam, eye)
        decay = jnp.where(incl, jnp.exp(jnp.where(incl, diff, 0.0)), 0.0)
        out.append(dict(incl=incl, strict=strict, eye=eye, decay=decay, eg=jnp.exp(gam), ek=jnp.exp(last - gam),
                        egl=jnp.exp(last), kb=kb, qt=qt, lmat=jnp.where(strict, both[:GDN_ROWS] * decay, 0.0),
                        pmat=jnp.where(incl, both[GDN_ROWS:] * decay, 0.0)))
    return out


def _gdn_prep(qn, kn, v, g, beta, *, name):
    heads, t, dk = qn.shape
    dv = v.shape[2]
    rows = _pick(t, (3 * GDN_ROWS, 2 * GDN_ROWS, GDN_ROWS))
    dk_scale = dk ** -0.5

    def body(q_ref, k_ref, v_ref, g_ref, b_ref, u_ref, w_ref, p_ref, qd_ref, kd_ref, egl_ref, t_ref):
        rs = [pl.ds(b * GDN_ROWS, GDN_ROWS) for b in range(rows // GDN_ROWS)]
        cs = _chunk_common([(q_ref[r, :], k_ref[r, :], g_ref[r, :], b_ref[r, :]) for r in rs], dk_scale)
        eye_f = jnp.where(cs[0]["eye"], 1.0, 0.0)
        tinvs = [eye_f - c["lmat"] for c in cs]
        ys = [_dot_hp(c["lmat"], c["lmat"]) for c in cs]
        for _ in range(CHUNK_SHIFT - 1):
            boths = [_dot_hp(y, jnp.concatenate([y, tinv], axis=1)) for y, tinv in zip(ys, tinvs)]
            ys = [both[:, :GDN_ROWS] for both in boths]
            tinvs = [tinv + both[:, GDN_ROWS:] for tinv, both in zip(tinvs, boths)]
        uws = [_dot_hp(tinv, jnp.concatenate([v_ref[r, :] * b_ref[r, :], c["kb"] * c["eg"]], axis=1))
               for r, c, tinv in zip(rs, cs, tinvs)]
        for r, c, tinv, uw in zip(rs, cs, tinvs, uws):
            u_ref[r, :] = uw[:, :dv]
            w_ref[r, :] = uw[:, dv:]
            p_ref[r, :] = c["pmat"]
            qd_ref[r, :] = c["qt"] * c["eg"]
            kd_ref[r, :] = k_ref[r, :] * c["ek"]
            egl_ref[r, :] = c["egl"]
            t_ref[r, :] = tinv

    def blk(width):
        return pl.BlockSpec((None, rows, width), lambda h, i: (h, i, 0))

    def shp(width):
        return jax.ShapeDtypeStruct((heads, t, width), F32)

    return pl.pallas_call(
        body, grid=(heads, t // rows), in_specs=[blk(dk), blk(dk), blk(dv), blk(1), blk(1)],
        out_specs=[blk(dv), blk(dk), blk(GDN_ROWS), blk(dk), blk(dk), blk(1), blk(GDN_ROWS)],
        out_shape=[shp(dv), shp(dk), shp(GDN_ROWS), shp(dk), shp(dk), shp(1), shp(GDN_ROWS)], name=name,
        compiler_params=_params("parallel", "parallel"))(qn, kn, v, g, beta)


def _gdn_scan(u, w, p, qd, kd, egl, proj, gain, *, col_z, name):
    heads, t, dv = u.shape
    dk = w.shape[2]
    nb = t // GDN_ROWS
    sub = GDN_ROWS // CHUNK
    hp = SCAN_HEADS

    def body(u_ref, w_ref, p_ref, qd_ref, kd_ref, egl_ref, z_ref, gn_ref, o_ref, og_ref, st_ref, s_ref):
        @pl.when(pl.program_id(1) == 0)
        def _():
            s_ref[...] = jnp.zeros_like(s_ref)

        hs = range(hp)
        vn_parts = [[jnp.zeros((CHUNK, dv), F32)] * sub for _ in hs]
        for c in range(sub):
            r = pl.ds(c * CHUNK, CHUNK)
            ss = [s_ref[hh] for hh in hs]
            sbs = [_bf(s) for s in ss]
            wss = [_dot(_bf(jnp.concatenate([w_ref[hh, r, :], qd_ref[hh, r, :]], axis=0)), sbs[hh])
                   for hh in hs]
            vns = [u_ref[hh, r, :] - wss[hh][:CHUNK] for hh in hs]
            for hh in hs:
                vn_parts[hh][c] = vns[hh]
            os_ = [wss[hh][CHUNK:] + _dot(_bf(p_ref[hh, r, :]), _bf(jnp.concatenate(vn_parts[hh], axis=0))) for hh in hs]
            new = [ss[hh] * egl_ref[hh, pl.ds(c * CHUNK, 1), :] + _dot_tn(_bf(kd_ref[hh, r, :]), _bf(vns[hh])) for hh in hs]
            for hh in hs:
                cols = pl.ds(hh * dv, dv)
                st_ref[hh, c] = ss[hh]
                s_ref[hh] = new[hh]
                o_ref[hh, r, :] = os_[hh]
                og_ref[r, cols] = _bf(_rms_fwd(os_[hh], gn_ref[...]) * _silu(z_ref[r, cols]))

    def blk(width):
        return pl.BlockSpec((hp, GDN_ROWS, width), lambda h, i: (h, i, 0))

    return pl.pallas_call(
        body, grid=(heads // hp, nb),
        in_specs=[blk(dv), blk(dk), blk(GDN_ROWS), blk(dk), blk(dk), blk(1),
                  pl.BlockSpec((GDN_ROWS, hp * dv), lambda h, i: (i, col_z // (hp * dv) + h)),
                  pl.BlockSpec((1, dv), lambda h, i: (0, 0))],
        out_specs=[blk(dv), pl.BlockSpec((GDN_ROWS, hp * dv), lambda h, i: (i, h)),
                   pl.BlockSpec((hp, sub, dk, dv), lambda h, i: (h, i, 0, 0))],
        out_shape=[jax.ShapeDtypeStruct((heads, t, dv), F32), jax.ShapeDtypeStruct((t, heads * dv), BF16),
                   jax.ShapeDtypeStruct((heads, t // CHUNK, dk, dv), F32)],
        scratch_shapes=[pltpu.VMEM((hp, dk, dv), F32)], name=name,
        compiler_params=_params("parallel", "arbitrary"))(u, w, p, qd, kd, egl, proj, gain)


def _gdn_post_bwd(o, proj, gain, dout, *, col_z, name):
    heads, t, dv = o.shape
    tm = _pick(t, (384, 256, 128))

    def body(o_ref, z_ref, gn_ref, d_ref, do_ref, dz_ref, dg_ref):
        @pl.when((pl.program_id(0) == 0) & (pl.program_id(1) == 0))
        def _():
            dg_ref[...] = jnp.zeros_like(dg_ref)

        o_, z, d = o_ref[...], z_ref[...], d_ref[...]
        y = _rms_fwd(o_, gn_ref[...])
        dz_ref[...] = _bf(d * y * _silu_grad(z))
        do, dgr = _rms_bwd(o_, gn_ref[...], d * _silu(z))
        do_ref[...] = do
        dg_ref[...] += jnp.sum(dgr, axis=0, keepdims=True)

    return pl.pallas_call(
        body, grid=(t // tm, heads),
        in_specs=[pl.BlockSpec((None, tm, dv), lambda i, h: (h, i, 0)),
                  pl.BlockSpec((tm, dv), lambda i, h: (i, col_z // dv + h)),
                  pl.BlockSpec((1, dv), lambda i, h: (0, 0)), pl.BlockSpec((tm, dv), lambda i, h: (i, h))],
        out_specs=[pl.BlockSpec((None, tm, dv), lambda i, h: (h, i, 0)), pl.BlockSpec((tm, dv), lambda i, h: (i, h)),
                   pl.BlockSpec((1, dv), lambda i, h: (0, 0))],
        out_shape=[jax.ShapeDtypeStruct((heads, t, dv), F32), jax.ShapeDtypeStruct((t, heads * dv), BF16),
                   jax.ShapeDtypeStruct((1, dv), F32)], name=name,
        compiler_params=_params("arbitrary", "arbitrary"))(o, proj, gain, dout)


def _gdn_bwd_scan(u, w, p, qd, kd, egl, st, do, *, name):
    heads, t, dv = u.shape
    dk = w.shape[2]
    nb = t // GDN_ROWS
    sub = GDN_ROWS // CHUNK
    hp = SCAN_HEADS

    def body(u_ref, w_ref, p_ref, qd_ref, kd_ref, egl_ref, st_ref, do_ref,
             du_ref, dw_ref, dp_ref, dqd_ref, dkd_ref, dgl_ref, ds_ref):
        @pl.when(pl.program_id(1) == 0)
        def _():
            ds_ref[...] = jnp.zeros_like(ds_ref)

        hs = range(hp)
        zeros = jnp.zeros((CHUNK, dv), BF16)
        for c in reversed(range(sub)):
            r = pl.ds(c * CHUNK, CHUNK)
            ss = [st_ref[hh, c] for hh in hs]
            sbs = [_bf(s) for s in ss]
            dss = [ds_ref[hh] for hh in hs]
            dsbs = [_bf(ds) for ds in dss]
            dobs = [_bf(do_ref[hh, r, :]) for hh in hs]
            wbs = [_bf(w_ref[hh, r, :]) for hh in hs]
            vns = [u_ref[hh, r, :] - _dot(wbs[hh], sbs[hh]) for hh in hs]
            dvns = [_dot_tn(_bf(p_ref[hh, r, :]), dobs[hh])[c * CHUNK:(c + 1) * CHUNK, :]
                    + _dot(_bf(kd_ref[hh, r, :]), dsbs[hh]) for hh in hs]
            dods = [jnp.concatenate([dobs[hh], _bf(dvns[hh])], axis=0) for hh in hs]
            boths = [_dot_nt(dods[hh], sbs[hh]) for hh in hs]
            dps = [_dot_nt(dobs[hh], jnp.concatenate([_bf(vns[hh]) if cc == c else zeros for cc in range(sub)], axis=0))
                   for hh in hs]
            dkds = [_dot_nt(_bf(vns[hh]), dsbs[hh]) for hh in hs]
            new = [dss[hh] * egl_ref[hh, pl.ds(c * CHUNK, 1), :]
                   + _dot_tn(jnp.concatenate([_bf(qd_ref[hh, r, :]), -wbs[hh]], axis=0), dods[hh])
                   for hh in hs]
            for hh in hs:
                du_ref[hh, r, :] = dvns[hh]
                dw_ref[hh, r, :] = -boths[hh][CHUNK:]
                dp_ref[hh, r, :] = jnp.where(_chunk_masks(CHUNK, c * CHUNK)[1], dps[hh], 0.0)
                dqd_ref[hh, r, :] = boths[hh][:CHUNK]
                dkd_ref[hh, r, :] = dkds[hh]
                dgl = jnp.sum(jnp.sum(dss[hh] * ss[hh], axis=1, keepdims=True), axis=0, keepdims=True)
                dgl_ref[hh, r, :] = jnp.where(_iota((CHUNK, 1), 0) == CHUNK - 1, dgl, 0.0)
                ds_ref[hh] = new[hh]

    def blk(width):
        return pl.BlockSpec((hp, GDN_ROWS, width), lambda h, i: (h, nb - 1 - i, 0))

    def shp(width):
        return jax.ShapeDtypeStruct((heads, t, width), F32)

    return pl.pallas_call(
        body, grid=(heads // hp, nb),
        in_specs=[blk(dv), blk(dk), blk(GDN_ROWS), blk(dk), blk(dk), blk(1),
                  pl.BlockSpec((hp, sub, dk, dv), lambda h, i: (h, nb - 1 - i, 0, 0)), blk(dv)],
        out_specs=[blk(dv), blk(dk), blk(GDN_ROWS), blk(dk), blk(dk), blk(1)],
        out_shape=[shp(dv), shp(dk), shp(GDN_ROWS), shp(dk), shp(dk), shp(1)],
        scratch_shapes=[pltpu.VMEM((hp, dk, dv), F32)], name=name,
        compiler_params=_params("parallel", "arbitrary"))(u, w, p, qd, kd, egl, st, do)


def _gdn_bwd_prep(qn, kn, v, g, beta, tinv, u, w, du, dw, dp, dqd, dkd, dgl, *, name):
    heads, t, dk = qn.shape
    dv = v.shape[2]
    rows = _pick(t, (3 * GDN_ROWS, 2 * GDN_ROWS, GDN_ROWS))
    dk_scale = dk ** -0.5

    def rowsum(x):
        return jnp.sum(x, axis=1, keepdims=True)

    def body(q_ref, k_ref, v_ref, g_ref, b_ref, t_ref, u_ref, w_ref, du_ref, dw_ref, dp_ref, dqd_ref, dkd_ref, dgl_ref,
             dq_ref, dkk_ref, dvv_ref, dg_ref, db_ref):
        rs = [pl.ds(b * GDN_ROWS, GDN_ROWS) for b in range(rows // GDN_ROWS)]
        cs = _chunk_common([(q_ref[r, :], k_ref[r, :], g_ref[r, :], b_ref[r, :]) for r in rs], dk_scale)
        dbvws = [_dot_hp(t_ref[r, :], jnp.concatenate([du_ref[r, :], dw_ref[r, :]], axis=1), _dot_tn)
                 for r in rs]
        das = [-_dot_nt(_bf(dbvw), _bf(jnp.concatenate([u_ref[r, :], w_ref[r, :]], axis=1)))
               for r, dbvw in zip(rs, dbvws)]
        dls = [jnp.where(c["strict"], da, 0.0) for c, da in zip(cs, das)]
        dmns = [_bf(jnp.concatenate([dl * c["decay"], dp_ref[r, :] * c["decay"]], axis=0))
                for r, c, dl in zip(rs, cs, dls)]
        boths = [_dot(dmn, _bf(k_ref[r, :])) for r, dmn in zip(rs, dmns)]
        dkns = [_dot_tn(dmn, _bf(jnp.concatenate([c["kb"], c["qt"]], axis=0)))
                for c, dmn in zip(cs, dmns)]
        for r, c, dbvw, dl, both, dkn in zip(rs, cs, dbvws, dls, boths, dkns):
            kn_, beta_, v_ = k_ref[r, :], b_ref[r, :], v_ref[r, :]
            eye = c["eye"]
            kb, qt, eg, ek = c["kb"], c["qt"], c["eg"], c["ek"]
            dbv, dbw = dbvw[:, :dv], dbvw[:, dv:]
            dp_ = dp_ref[r, :]
            dkb = both[:GDN_ROWS] + dbw * eg
            dqt = both[GDN_ROWS:]
            gmat = dl * c["lmat"] + dp_ * c["pmat"]
            dqd_, dkd_ = dqd_ref[r, :], dkd_ref[r, :]
            qd = qt * eg
            kd = kn_ * ek
            bw = kb * eg
            kdsum = rowsum(dkd_ * kd)
            dgam = rowsum(gmat) - _row_to_col(jnp.sum(gmat, axis=0, keepdims=True), eye)
            dgam += rowsum(dbw * bw) + rowsum(dqd_ * qd) - kdsum
            last = (_iota((GDN_ROWS, 1), 0) & (CHUNK - 1)) == CHUNK - 1
            same = _chunk_masks()[0]
            same_f = jnp.where(same, 1.0, 0.0).astype(BF16)
            chunk_tot = jnp.max(_dot_exact_l(same_f, jnp.broadcast_to(kdsum, (GDN_ROWS, LANES))), axis=1, keepdims=True)
            dgam += jnp.where(last, chunk_tot, 0.0) + dgl_ref[r, :] * c["egl"]
            dq_ref[r, :] = (dqt + dqd_ * eg) * dk_scale
            dkk_ref[r, :] = dkn + dkd_ * ek + dkb * beta_
            dvv_ref[r, :] = dbv * beta_
            db_ref[r, :] = rowsum(dbv * v_) + rowsum(dkb * kn_)
            upper = jnp.where(same & (_iota((GDN_ROWS, GDN_ROWS), 0) <= _iota((GDN_ROWS, GDN_ROWS), 1)), 1.0, 0.0)
            dgb = _dot_exact_l(upper.astype(BF16), jnp.broadcast_to(dgam, (GDN_ROWS, LANES)))
            dg_ref[r, :] = _lane_pick(dgb, 0)

    def blk(width):
        return pl.BlockSpec((None, rows, width), lambda h, i: (h, i, 0))

    def shp(width):
        return jax.ShapeDtypeStruct((heads, t, width), F32)

    return pl.pallas_call(
        body, grid=(heads, t // rows),
        in_specs=[blk(dk), blk(dk), blk(dv), blk(1), blk(1), blk(GDN_ROWS), blk(dv), blk(dk),
                  blk(dv), blk(dk), blk(GDN_ROWS), blk(dk), blk(dk), blk(1)],
        out_specs=[blk(dk), blk(dk), blk(dv), blk(1), blk(1)],
        out_shape=[shp(dk), shp(dk), shp(dv), shp(1), shp(1)], name=name,
        compiler_params=_params("parallel", "parallel"))(qn, kn, v, g, beta, tinv, u, w, du, dw, dp, dqd, dkd, dgl)


def _gdn_pre_bwd_a(proj, pab, cq, ck, cv, a_log, dt_bias, dqn, dkn, dvv, dg, dbeta, *,
                   heads, dk, dv, col_q, col_k, col_v, row_lo, row_hi, name):
    t = proj.shape[0]
    tm = _pick(t, (384, 256, 128))
    nb = t // tm

    def body(pq_ref, pqp_ref, pk_ref, pkp_ref, pv_ref, pvp_ref, ab_ref, cq_ref, ck_ref, cv_ref, al_ref, dt_ref,
             dqn_ref, dkn_ref, dvv_ref, dg_ref, db_ref, dcq_ref, dck_ref, dcv_ref, dab_ref, dal_ref, ddt_ref):
        i, h = pl.program_id(0), pl.program_id(1)
        first = i == 0

        @pl.when((i == 0) & (h == 0))
        def _():
            dal_ref[...] = jnp.zeros_like(dal_ref)
            ddt_ref[...] = jnp.zeros_like(ddt_ref)

        @pl.when(h == 0)
        def _():
            dab_ref[...] = jnp.zeros_like(dab_ref)

        row = i * tm + _iota((tm, 1), 0)
        valid = (row >= row_lo) & (row < row_hi)

        def l2_bwd(c1, dn):
            x1 = _silu(c1)
            r = lax.rsqrt(jnp.sum(x1 * x1, axis=-1, keepdims=True) + L2_EPS)
            dn = jnp.where(valid, dn, 0.0)
            d1 = r * dn - x1 * (r * r * r) * jnp.sum(dn * x1, axis=-1, keepdims=True)
            return d1 * _silu_grad(c1)

        dcq_ref[...] = l2_bwd(_conv_taps(pq_ref[...], pqp_ref[...], cq_ref, first), dqn_ref[...])
        dck_ref[...] = l2_bwd(_conv_taps(pk_ref[...], pkp_ref[...], ck_ref, first), dkn_ref[...])
        cv1 = _conv_taps(pv_ref[...], pvp_ref[...], cv_ref, first)
        dcv_ref[...] = jnp.where(valid, dvv_ref[...], 0.0) * _silu_grad(cv1)
        ab = ab_ref[...]
        da = _lane_pick(ab, h)
        db = _lane_pick(ab, heads + h)
        a = _lane_pick(al_ref[...], h)
        dtb = _lane_pick(dt_ref[...], h)
        dgv = jnp.where(valid, dg_ref[...], 0.0)
        ea = jnp.exp(a)
        g = -ea * _softplus(da + dtb)
        dda = dgv * (-ea) * _sigmoid(da + dtb)
        beta = _sigmoid(db)
        ddb = jnp.where(valid, db_ref[...], 0.0) * beta * (1.0 - beta)
        lane = _iota((tm, LANES), 1)
        dab_ref[...] += jnp.where(lane == h, dda, 0.0) + jnp.where(lane == heads + h, ddb, 0.0)
        lane1 = _iota((1, LANES), 1)
        dal_ref[...] += jnp.where(lane1 == h, jnp.sum(dgv * g, axis=0, keepdims=True), 0.0)
        ddt_ref[...] += jnp.where(lane1 == h, jnp.sum(dda, axis=0, keepdims=True), 0.0)

    def cur(width, col):
        return pl.BlockSpec((tm, width), lambda i, h: (i, col // width + h))

    def prev(width, col):
        return pl.BlockSpec((8, width), lambda i, h: (jnp.maximum(i * (tm // 8) - 1, 0), col // width + h))

    def hd(width):
        return pl.BlockSpec((None, tm, width), lambda i, h: (h, i, 0))

    small = pl.BlockSpec((1, LANES), lambda i, h: (0, 0))
    return pl.pallas_call(
        body, grid=(nb, heads),
        in_specs=[cur(dk, col_q), prev(dk, col_q), cur(dk, col_k), prev(dk, col_k), cur(dv, col_v), prev(dv, col_v),
                  pl.BlockSpec((tm, LANES), lambda i, h: (i, 0)),
                  pl.BlockSpec((cq.shape[0], dk), lambda i, h: (0, h)), pl.BlockSpec((ck.shape[0], dk), lambda i, h: (0, h)),
                  pl.BlockSpec((cv.shape[0], dv), lambda i, h: (0, h)), small, small,
                  hd(dk), hd(dk), hd(dv), hd(1), hd(1)],
        out_specs=[hd(dk), hd(dk), hd(dv), pl.BlockSpec((tm, LANES), lambda i, h: (i, 0)), small, small],
        out_shape=[jax.ShapeDtypeStruct((heads, t, dk), F32), jax.ShapeDtypeStruct((heads, t, dk), F32),
                   jax.ShapeDtypeStruct((heads, t, dv), F32), jax.ShapeDtypeStruct((t, LANES), F32),
                   jax.ShapeDtypeStruct((1, LANES), F32), jax.ShapeDtypeStruct((1, LANES), F32)], name=name,
        compiler_params=_params("arbitrary", "arbitrary"))(
            proj, proj, proj, proj, proj, proj, pab, cq, ck, cv, a_log, dt_bias, dqn, dkn, dvv, dg, dbeta)


def _conv_bwd(proj, dc, cw, *, heads, width, col, name):
    t = proj.shape[0]
    tm = _pick(t, (384, 256, 128))
    nb = t // tm
    nk = cw.shape[0]

    def body(p_ref, pp_ref, d_ref, dn_ref, w_ref, dp_ref, dw_ref):
        i = pl.program_id(1)
        first, last = i == 0, i == nb - 1

        @pl.when(first)
        def _():
            dw_ref[...] = jnp.zeros_like(dw_ref)

        x, d = p_ref[...], d_ref[...]
        dx = d * w_ref[nk - 1:nk, :]
        dw_ref[nk - 1:nk, :] += jnp.sum(d * x, axis=0, keepdims=True)
        for s in range(1, nk):
            dx += _shift_up(d, dn_ref[...], s, last) * w_ref[nk - 1 - s:nk - s, :]
            dw_ref[nk - 1 - s:nk - s, :] += jnp.sum(d * _shift_down(x, pp_ref[...], s, first), axis=0, keepdims=True)
        dp_ref[...] = _bf(dx)

    return pl.pallas_call(
        body, grid=(heads, nb),
        in_specs=[pl.BlockSpec((tm, width), lambda h, i: (i, col // width + h)),
                  pl.BlockSpec((8, width), lambda h, i: (jnp.maximum(i * (tm // 8) - 1, 0), col // width + h)),
                  pl.BlockSpec((None, tm, width), lambda h, i: (h, i, 0)),
                  pl.BlockSpec((None, 8, width), lambda h, i: (h, jnp.minimum((i + 1) * (tm // 8), t // 8 - 1), 0)),
                  pl.BlockSpec((nk, width), lambda h, i: (0, h))],
        out_specs=[pl.BlockSpec((tm, width), lambda h, i: (i, h)), pl.BlockSpec((nk, width), lambda h, i: (0, h))],
        out_shape=[jax.ShapeDtypeStruct((t, heads * width), BF16), jax.ShapeDtypeStruct((nk, heads * width), F32)],
        name=name, compiler_params=_params("parallel", "arbitrary"))(proj, proj, dc, dc, cw)


def _sb_pre(proj, gq, gk, *, heads, dh, col_q, col_k, col_v, name):
    t = proj.shape[0]
    tm = _pick(t, (384, 256, 128))

    def body(q_ref, k_ref, v_ref, gq_ref, gk_ref, qo_ref, ko_ref, vo_ref):
        qo_ref[...] = _bf(_rms_fwd(q_ref[...], gq_ref[...]))
        ko_ref[...] = _bf(_rms_fwd(k_ref[...], gk_ref[...]))
        vo_ref[...] = _bf(v_ref[...])

    def cur(col):
        return pl.BlockSpec((tm, dh), lambda i, h: (i, col // dh + h))

    gspec = pl.BlockSpec((1, dh), lambda i, h: (0, 0))
    ospec = pl.BlockSpec((tm, dh), lambda i, h: (i, h))
    return pl.pallas_call(
        body, grid=(t // tm, heads), in_specs=[cur(col_q), cur(col_k), cur(col_v), gspec, gspec],
        out_specs=[ospec] * 3, out_shape=[jax.ShapeDtypeStruct((t, heads * dh), BF16)] * 3, name=name,
        compiler_params=_params("parallel", "parallel"))(proj, proj, proj, gq, gk)


def _sb_tile(z, i, j, blk, key_lo):
    qpos = i * blk + _iota((blk, blk), 0)
    kpos = j * blk + _iota((blk, blk), 1)
    vis = (kpos < qpos) & (kpos >= key_lo)
    ls = jnp.minimum(z, 0.0) - jnp.log(1.0 + jnp.exp(-jnp.abs(z)))
    return vis, ls, jnp.where(vis, ls - z, 0.0)


def _dot2_r(x, m):
    hi, lo = _split2(x)
    return _dot(hi, m) + _dot(lo, m)


def _suffix_sums(x, later):
    return jnp.concatenate([_dot2_r(x[:, s:], later[s:, s:s + LANES]) for s in range(0, x.shape[1], LANES)], axis=1)


def _prefix_sums(x, earlier):
    return jnp.concatenate([_dot2_r(x[:, :s + LANES], earlier[:s + LANES, s:s + LANES])
                            for s in range(0, x.shape[1], LANES)], axis=1)


def _sb_fwd(qs, ks, vs, *, heads, dh, key_lo, name, gather=None):
    t = qs.shape[0]
    blk = _pick(t, (3 * SB_BLOCK, 2 * SB_BLOCK, SB_BLOCK))
    nq = t // blk
    assert nq <= LANES
    scale = dh ** -0.5
    hp = SB_HEADS_PER_STEP

    def body(q_ref, k_ref, v_ref, o_ref, c_ref):
        i = pl.program_id(1)
        later = jnp.where(_iota((blk, blk), 0) > _iota((blk, blk), 1), 1.0, 0.0).astype(BF16)
        lane = _iota((blk, LANES), 1)
        c_ref[...] = jnp.zeros_like(c_ref)

        def step(n, carry):
            j = i - n
            rows = pl.ds(pl.multiple_of(j * blk, blk), blk)
            hs = range(hp)
            cols = [pl.ds(hh * dh, dh) for hh in hs]
            zs = [_dot_nt(q_ref[:, cols[hh]], k_ref[rows, cols[hh]]) * scale for hh in hs]
            tiles = [_sb_tile(z, i, j, blk, key_lo) for z in zs]
            sufs = [_suffix_sums(lk, later) for _, _, lk in tiles]
            wgts = [jnp.where(vis, jnp.exp(ls + suf + carry[2 * hh + 1]), 0.0)
                    for hh, ((vis, ls, _), suf) in enumerate(zip(tiles, sufs))]
            accs = [carry[2 * hh] + _dot(_bf(wgts[hh]), v_ref[rows, cols[hh]]) for hh in hs]
            out = []
            for hh in hs:
                c_ref[hh] = jnp.where(lane == j, carry[2 * hh + 1], c_ref[hh])
                out += [accs[hh], carry[2 * hh + 1] + jnp.sum(tiles[hh][2], axis=1, keepdims=True)]
            return tuple(out)

        res = lax.fori_loop(0, i + 1, step, (jnp.zeros((blk, dh), F32), jnp.zeros((blk, 1), F32)) * hp)
        for hh in range(hp):
            o_ref[:, pl.ds(hh * dh, dh)] = _bf(res[2 * hh])

    full = pl.BlockSpec((t, hp * dh), lambda h, i: (0, h))
    call = dict(grid=(heads // hp, nq),
                in_specs=[pl.BlockSpec((blk, hp * dh), lambda h, i: (i, h)), full, full],
                out_specs=[pl.BlockSpec((blk, hp * dh), lambda h, i: (i, h)),
                           pl.BlockSpec((hp, blk, LANES), lambda h, i: (h, i, 0))],
                out_shape=[jax.ShapeDtypeStruct((t, heads * dh), BF16), jax.ShapeDtypeStruct((heads, t, LANES), F32)],
                name=name)
    if gather is None:
        return pl.pallas_call(body, compiler_params=_params("parallel", "parallel"), **call)(qs, ks, vs)
    res = _call_with_exchange(body, scratch_shapes=[], args=(qs, ks, vs), srcs=gather, scatter=[False] * len(gather),
                              **call)
    return [*res[:2], list(res[2:])]


def _sb_bwd(qs, ks, vs, do, carry, *, heads, dh, key_lo, name, scatter=None):
    t = qs.shape[0]
    blk = _pick(t, (3 * SB_BLOCK, 2 * SB_BLOCK, SB_BLOCK))
    nq = t // blk
    scale = dh ** -0.5
    hp = SB_HEADS_PER_STEP

    def body(q_ref, k_ref, v_ref, do_ref, c_ref, dq_ref, dk_ref, dv_ref):
        i = pl.program_id(1)

        @pl.when(i == 0)
        def _():
            dk_ref[...] = jnp.zeros_like(dk_ref)
            dv_ref[...] = jnp.zeros_like(dv_ref)

        r0 = _iota((blk, blk), 0)
        r1 = _iota((blk, blk), 1)
        later = jnp.where(r0 > r1, 1.0, 0.0).astype(BF16)
        earlier = jnp.where(r0 < r1, 1.0, 0.0).astype(BF16)

        def step(j, carry):
            rows = pl.ds(pl.multiple_of(j * blk, blk), blk)
            hs = range(hp)
            cols = [pl.ds(hh * dh, dh) for hh in hs]
            zs = [_dot_nt(q_ref[:, cols[hh]], k_ref[rows, cols[hh]]) * scale for hh in hs]
            dws = [_dot_nt(do_ref[:, cols[hh]], v_ref[rows, cols[hh]]) for hh in hs]
            tiles = [_sb_tile(z, i, j, blk, key_lo) for z in zs]
            sufs = [_suffix_sums(lk, later) for _, _, lk in tiles]
            wgts = [jnp.where(vis, jnp.exp(ls + suf + _lane_pick(c_ref[hh], j)), 0.0)
                    for hh, ((vis, ls, _), suf) in enumerate(zip(tiles, sufs))]
            es = [wgt * dw for wgt, dw in zip(wgts, dws)]
            pres = [_prefix_sums(e, earlier) for e in es]
            dzs = []
            for hh in hs:
                vis, ls, _ = tiles[hh]
                before = jnp.where(vis, pres[hh] + carry[2 * hh + 1], 0.0)
                sig = jnp.exp(ls)
                dzs.append(_bf((es[hh] * (1.0 - sig) - before * sig) * scale))
            dks = [_dot_tn(dzs[hh], q_ref[:, cols[hh]]) for hh in hs]
            dvs = [_dot_tn(_bf(wgts[hh]), do_ref[:, cols[hh]]) for hh in hs]
            dqs = [carry[2 * hh] + _dot(dzs[hh], k_ref[rows, cols[hh]]) for hh in hs]
            out = []
            for hh in hs:
                dk_ref[rows, cols[hh]] += dks[hh]
                dv_ref[rows, cols[hh]] += dvs[hh]
                out += [dqs[hh], carry[2 * hh + 1] + jnp.sum(es[hh], axis=1, keepdims=True)]
            return tuple(out)

        res = lax.fori_loop(0, i + 1, step, (jnp.zeros((blk, dh), F32), jnp.zeros((blk, 1), F32)) * hp)
        for hh in range(hp):
            dq_ref[:, pl.ds(hh * dh, dh)] = res[2 * hh]

    full = pl.BlockSpec((t, hp * dh), lambda h, i: (0, h))
    qblk = pl.BlockSpec((blk, hp * dh), lambda h, i: (i, h))
    call = dict(grid=(heads // hp, nq),
                in_specs=[qblk, full, full, qblk, pl.BlockSpec((hp, blk, LANES), lambda h, i: (h, i, 0))],
                out_specs=[qblk, full, full], out_shape=[jax.ShapeDtypeStruct((t, heads * dh), F32)] * 3, name=name)
    if scatter is None:
        return pl.pallas_call(body, compiler_params=_params("parallel", "arbitrary"), **call)(qs, ks, vs, do, carry)
    res = _call_with_exchange(body, scratch_shapes=[], args=(qs, ks, vs, do, carry), srcs=scatter,
                              scatter=[True] * len(scatter), **call)
    return [*res[:3], list(res[3:])]


def _sb_pre_bwd(proj, gq, gk, dq, dk, dv, *, heads, dh, col_q, col_k, name):
    t = proj.shape[0]
    tm = _pick(t, (384, 256, 128))

    def body(q_ref, k_ref, gq_ref, gk_ref, dq_ref, dk_ref, dv_ref, oq_ref, ok_ref, ov_ref, dgq_ref, dgk_ref):
        @pl.when((pl.program_id(0) == 0) & (pl.program_id(1) == 0))
        def _():
            dgq_ref[...] = jnp.zeros_like(dgq_ref)
            dgk_ref[...] = jnp.zeros_like(dgk_ref)

        dq_, gq_r = _rms_bwd(q_ref[...], gq_ref[...], dq_ref[...])
        dk_, gk_r = _rms_bwd(k_ref[...], gk_ref[...], dk_ref[...])
        oq_ref[...] = _bf(dq_)
        ok_ref[...] = _bf(dk_)
        ov_ref[...] = _bf(dv_ref[...])
        dgq_ref[...] += jnp.sum(gq_r, axis=0, keepdims=True)
        dgk_ref[...] += jnp.sum(gk_r, axis=0, keepdims=True)

    def cur(col):
        return pl.BlockSpec((tm, dh), lambda i, h: (i, col // dh + h))

    gspec = pl.BlockSpec((1, dh), lambda i, h: (0, 0))
    ospec = pl.BlockSpec((tm, dh), lambda i, h: (i, h))
    return pl.pallas_call(
        body, grid=(t // tm, heads), in_specs=[cur(col_q), cur(col_k), gspec, gspec, ospec, ospec, ospec],
        out_specs=[ospec, ospec, ospec, gspec, gspec],
        out_shape=[jax.ShapeDtypeStruct((t, heads * dh), BF16)] * 3 + [jax.ShapeDtypeStruct((1, dh), F32)] * 2,
        name=name, compiler_params=_params("arbitrary", "arbitrary"))(proj, proj, gq, gk, dq, dk, dv)


PEERS = N_DEV - 1


def _exchange_copies(ins, outs, send_sems, recv_sems, local_sems, scatter):
    x, y, c = lax.axis_index("x"), lax.axis_index("y"), lax.axis_index("c")
    me = 4 * x + 2 * y + c
    copies = []
    for a in range(len(ins)):
        own = ins[a].at[me] if scatter[a] else ins[a]
        copies.append(pltpu.make_async_copy(own, outs[a].at[me], local_sems.at[a]))
        for k in range(1, N_DEV):
            px = (x + (k >> 2 & 1)) % 2
            py = (y + (k >> 1 & 1)) % 2
            pc = (c + (k & 1)) % 2
            src = ins[a].at[4 * px + 2 * py + pc] if scatter[a] else ins[a]
            copies.append(pltpu.make_async_remote_copy(
                src_ref=src, dst_ref=outs[a].at[me], send_sem=send_sems.at[a * PEERS + k - 1],
                recv_sem=recv_sems.at[a * PEERS + k - 1], device_id=(px, py, pc), device_id_type=MESH))
    return copies


def _exchange_shapes(srcs, scatter):
    return [jax.ShapeDtypeStruct(s.shape if sc else (N_DEV,) + s.shape, s.dtype) for s, sc in zip(srcs, scatter)]


def _exchange_sems(n):
    return [pltpu.SemaphoreType.DMA((n * PEERS,)), pltpu.SemaphoreType.DMA((n * PEERS,)), pltpu.SemaphoreType.DMA((n,))]


def _exchange(srcs, *, scatter, name):
    n = len(srcs)

    def body(*refs):
        copies = _exchange_copies(refs[:n], refs[n:2 * n], *refs[2 * n:], scatter)
        for cp in copies:
            cp.start()
        for cp in copies:
            cp.wait()

    any_spec = pl.BlockSpec(memory_space=pl.ANY)
    return pl.pallas_call(
        body, in_specs=[any_spec] * n, out_specs=[any_spec] * n, out_shape=_exchange_shapes(srcs, scatter),
        scratch_shapes=_exchange_sems(n), name=name,
        compiler_params=pltpu.CompilerParams(has_side_effects=True))(*srcs)


def _gather_two_level(srcs, *, name):
    n = len(srcs)

    def body(*refs):
        ins, outs = refs[:n], refs[n:2 * n]
        send_sems, recv_sems, local_sems = refs[2 * n:]
        x, y, c = lax.axis_index("x"), lax.axis_index("y"), lax.axis_index("c")
        chips = [(1 - x, y), (x, 1 - y), (1 - x, 1 - y)]

        def slab(a, px, py, pc):
            return outs[a].at[4 * px + 2 * py + pc]

        def copy(a, k, block, to, src=None):
            return pltpu.make_async_remote_copy(
                src_ref=slab(a, *block) if src is None else src, dst_ref=slab(a, *block),
                send_sem=send_sems.at[a * PEERS + k], recv_sem=recv_sems.at[a * PEERS + k],
                device_id=to, device_id_type=MESH)

        mine = [pltpu.make_async_copy(ins[a], slab(a, x, y, c), local_sems.at[a]) for a in range(n)]
        first = [copy(a, 0, (x, y, c), (x, y, 1 - c), src=ins[a]) for a in range(n)]
        first += [copy(a, 1 + j, (x, y, c), (*chip, c), src=ins[a]) for j, chip in enumerate(chips) for a in range(n)]
        for cp in mine + first:
            cp.start()
        passed = []
        for j, chip in enumerate(chips):
            for a in range(n):
                copy(a, 1 + j, (*chip, c), (x, y, c)).wait_recv()
                passed.append(copy(a, 4 + j, (*chip, c), (x, y, 1 - c)))
                passed[-1].start()
        for a in range(n):
            copy(a, 0, (x, y, 1 - c), (x, y, c)).wait_recv()
            for j, chip in enumerate(chips):
                copy(a, 4 + j, (*chip, 1 - c), (x, y, c)).wait_recv()
        for cp in first + passed:
            cp.wait_send()
        for cp in mine:
            cp.wait()

    any_spec = pl.BlockSpec(memory_space=pl.ANY)
    return pl.pallas_call(
        body, in_specs=[any_spec] * n, out_specs=[any_spec] * n, out_shape=_exchange_shapes(srcs, [False] * n),
        scratch_shapes=_exchange_sems(n), name=name,
        compiler_params=pltpu.CompilerParams(has_side_effects=True))(*srcs)


def _call_with_exchange(body, *, grid, in_specs, out_specs, out_shape, scratch_shapes, args, srcs, scatter, name):
    n, n_in, n_out, n_scr = len(srcs), len(args), len(out_shape), len(scratch_shapes)

    def full_body(*refs):
        ins, xin = refs[:n_in], refs[n_in:n_in + n]
        outs, xout = refs[n_in + n:n_in + n + n_out], refs[n_in + n + n_out:n_in + 2 * n + n_out]
        scr = refs[n_in + 2 * n + n_out:]
        ids = [pl.program_id(a) for a in range(len(grid))]
        first = functools.reduce(jnp.logical_and, [i == 0 for i in ids])
        last = functools.reduce(jnp.logical_and, [i == g - 1 for i, g in zip(ids, grid)])
        copies = _exchange_copies(xin, xout, *scr[n_scr:], scatter)

        @pl.when(first)
        def _():
            for cp in copies:
                cp.start()

        body(*ins, *outs, *scr[:n_scr])

        @pl.when(last)
        def _():
            for cp in copies:
                cp.wait()

    any_spec = pl.BlockSpec(memory_space=pl.ANY)
    return pl.pallas_call(
        full_body, grid=grid, in_specs=list(in_specs) + [any_spec] * n, out_specs=list(out_specs) + [any_spec] * n,
        out_shape=list(out_shape) + _exchange_shapes(srcs, scatter),
        scratch_shapes=list(scratch_shapes) + _exchange_sems(n), name=name,
        compiler_params=pltpu.CompilerParams(dimension_semantics=("arbitrary",) * len(grid),
                                             vmem_limit_bytes=V7X_VMEM_LIMIT_BYTES, has_side_effects=True))(*args, *srcs)


def _adam_math(g, w, m, v):
    m2 = ADAM_B1 * m + (1.0 - ADAM_B1) * g
    v2 = ADAM_B2 * v + (1.0 - ADAM_B2) * (g * g)
    m_hat = m2 / (1.0 - ADAM_B1 ** ADAM_STEP)
    v_hat = v2 / (1.0 - ADAM_B2 ** ADAM_STEP)
    return -ADAM_LR * (m_hat / (jnp.sqrt(v_hat) + ADAM_EPS) + ADAM_WD * w), m2, v2


def _adamw_slabs(slabs, w, m, v, *, name):
    _, r, c = w.shape
    tr = _pick(r, (128, 64, 32, 16, 8)) if r % 8 == 0 else r

    def body(s_ref, w_ref, m_ref, v_ref, g_ref, d_ref, mo_ref, vo_ref):
        g = s_ref[0].astype(F32)
        for p in range(1, N_DEV):
            g = g + s_ref[p].astype(F32)
        g_ref[...] = g
        d_ref[...], mo_ref[...], vo_ref[...] = _adam_math(g, w_ref[...], m_ref[...], v_ref[...])

    spec = pl.BlockSpec((None, tr, c), lambda i: (0, i, 0))
    return pl.pallas_call(
        body, grid=(r // tr,), in_specs=[pl.BlockSpec((N_DEV, tr, c), lambda i: (0, i, 0)), spec, spec, spec],
        out_specs=[spec] * 4, out_shape=[jax.ShapeDtypeStruct((1, r, c), F32)] * 4, name=name,
        compiler_params=_params("parallel"))(slabs, w, m, v)


def _adamw_small(g, w, m, v, *, name):
    def body(g_ref, w_ref, m_ref, v_ref, d_ref, mo_ref, vo_ref):
        d_ref[...], mo_ref[...], vo_ref[...] = _adam_math(g_ref[...], w_ref[...], m_ref[...], v_ref[...])

    return pl.pallas_call(body, out_shape=[jax.ShapeDtypeStruct(w.shape, F32)] * 3, name=name)(g, w, m, v)


def _sum_slabs(slabs, *, name):
    def body(s_ref, o_ref):
        acc = s_ref[0]
        for p in range(1, N_DEV):
            acc = acc + s_ref[p]
        o_ref[...] = acc

    return pl.pallas_call(body, out_shape=jax.ShapeDtypeStruct(slabs.shape[1:], F32), name=name)(slabs)


def _gather_cols(g):
    return jnp.transpose(g, (1, 0, 2)).reshape(g.shape[1], -1)


def _col_slabs(a):
    return jnp.transpose(a.reshape(a.shape[0], N_DEV, -1), (1, 0, 2))


def _pad_lanes(a):
    return jnp.pad(a, ((0, 0), (0, LANES - a.shape[1])))


def _orig_cols(slabs, lo, hi):
    c = slabs.shape[2]
    return [slabs[p][:, max(lo, c * p) - c * p:min(hi, c * (p + 1)) - c * p]
            for p in range(N_DEV) if max(lo, c * p) < min(hi, c * (p + 1))]


def _w_in_slabs(main, ab, col_ab, n_ab):
    pw = main.shape[1] + n_ab
    c = pw // N_DEV
    parts = [(0, col_ab, main, 0), (col_ab, col_ab + n_ab, ab, col_ab), (col_ab + n_ab, pw, main, n_ab)]
    slabs = []
    for p in range(N_DEV):
        pieces = []
        for lo, hi, src, shift in parts:
            a, b = max(lo, c * p), min(hi, c * (p + 1))
            if a < b:
                pieces.append(src[:, a - shift:b - shift])
        slabs.append(jnp.concatenate(pieces, axis=1))
    return jnp.stack(slabs)


def _local_step(x, target, meta, g_mix, w_main, w_ab, cq, ck, cv, a_log, dt_bias, g_dn, g_sbq, g_sbk, g_ffn, rest,
                shards=False):
    seq, d = x.shape
    n_meta = meta.shape[0]
    heads = a_log.shape[1]
    qk = cq.shape[1]
    dvt = cv.shape[1]
    dk, dv = qk // heads, dvt // heads
    dh = g_sbq.shape[1]
    sbw = rest[2].shape[0] * N_DEV if shards else rest[1].shape[0]
    sb_heads = sbw // dh
    pad_l = (-n_meta) % CHUNK
    row_x = pad_l + n_meta
    rows = row_x + seq
    t = -(-rows // GDN_ROWS) * GDN_ROWS
    col_q, col_k, col_v, col_z = 0, qk, 2 * qk, 2 * qk + dvt
    col_sq = 2 * qk + 2 * dvt
    col_sk, col_sv, col_gd, col_gs = col_sq + sbw, col_sq + 2 * sbw, col_sq + 3 * sbw, col_sq + 3 * sbw + d

    def rows_pad(a):
        return jnp.concatenate([jnp.zeros((row_x, d), F32), a, jnp.zeros((t - rows, d), F32)], axis=0)

    h0 = jnp.concatenate([jnp.zeros((pad_l, d), F32), meta, x, jnp.zeros((t - rows, d), F32)], axis=0)
    tgt = rows_pad(target)
    a_log_p, dt_p = _pad_lanes(a_log), _pad_lanes(dt_bias)

    proj, n1 = _mm_norm(h0, g_mix, w_main, name="proj")
    pab = _mm_nn(n1, w_ab, out_dtype=F32, name="proj_ab")
    gk = dict(heads=heads, dk=dk, dv=dv, col_q=col_q, col_k=col_k, col_v=col_v, row_lo=pad_l, row_hi=rows)
    qn, kn, vv, g, beta = _gdn_pre(proj, pab, cq, ck, cv, a_log_p, dt_p, name="gdn_pre", **gk)
    u, w, pm, qd, kd, egl, tinv = _gdn_prep(qn, kn, vv, g, beta, name="gdn_prep")
    o_raw, o_dn, states = _gdn_scan(u, w, pm, qd, kd, egl, proj, g_dn, col_z=col_z, name="gdn_scan")
    qs, ks, vs = _sb_pre(proj, g_sbq, g_sbk, heads=sb_heads, dh=dh, col_q=col_sq, col_k=col_sk, col_v=col_sv,
                         name="sb_pre")
    if shards:
        o_sb, carry, (g_fi, g_bd, g_bs, g_out, g_fo) = _sb_fwd(qs, ks, vs, heads=sb_heads, dh=dh, key_lo=pad_l,
                                                                name="sb_fwd", gather=list(rest))
        w_fi = _gather_cols(g_fi)
        d_ff = w_fi.shape[1] // 2
        w_bd, w_bs, w_out, w_fg, w_fu, w_fo = (g_bd.reshape(-1, d), g_bs.reshape(-1, d), g_out.reshape(-1, d),
                                               w_fi[:, :d_ff], w_fi[:, d_ff:], g_fo.reshape(-1, d))
    else:
        o_sb, carry = _sb_fwd(qs, ks, vs, heads=sb_heads, dh=dh, key_lo=pad_l, name="sb_fwd")
        w_bd, w_bs, w_out, w_fg, w_fu, w_fo = rest
    merged, br_dn, br_sb = _merge_fwd(o_dn, o_sb, w_bd, w_bs, proj, col_gd=col_gd, col_gs=col_gs, name="merge")
    h1 = _mm_res(h0, merged, w_out, name="mix_out")
    gate, up, act, n2 = _mm_norm_swiglu(h1, g_ffn, w_fg, w_fu, name="ffn_in")
    dy, dyb, lsum = _mm_res_loss(h1, act, w_fo, tgt, row0=row_x, nrows=seq, name="ffn_out_loss")

    dgate, dup = _swiglu_bwd(dyb, w_fo, gate, up, name="ffn_out_bwd")
    d_w_fo = _mm_tn(act, dyb, name="dw_ffn_out")
    d_w_fg = _mm_tn(n2, dgate, name="dw_ffn_gate")
    d_w_fu = _mm_tn(n2, dup, name="dw_ffn_up")
    dh1, dh1b, d_g_ffn = _mm_nt_rmsbwd([(dgate, w_fg), (dup, w_fu)], None, h1, g_ffn, dy, name="ffn_in_bwd")

    dbd, dbs, dgd, dgs = _merge_bwd(dh1b, w_out, proj, br_dn, br_sb, col_gd=col_gd, col_gs=col_gs, name="mix_out_bwd")
    d_w_out = _mm_tn(merged, dh1b, name="dw_out")
    d_w_bd = _mm_tn(o_dn, dbd, name="dw_branch_dn")
    d_w_bs = _mm_tn(o_sb, dbs, name="dw_branch_sb")
    do_dn = _mm_nt(dbd, w_bd, out_dtype=F32, name="branch_dn_bwd")
    do_sb = _mm_nt(dbs, w_bs, out_dtype=BF16, name="branch_sb_bwd")

    do_raw, dz, d_g_dn = _gdn_post_bwd(o_raw, proj, g_dn, do_dn, col_z=col_z, name="gdn_post_bwd")
    du, dw, dp, dqd, dkd, dgl = _gdn_bwd_scan(u, w, pm, qd, kd, egl, states, do_raw, name="gdn_bwd_scan")
    dqn, dkn, dvv, dg, dbeta = _gdn_bwd_prep(qn, kn, vv, g, beta, tinv, u, w, du, dw, dp, dqd, dkd, dgl,
                                            name="gdn_bwd_prep")
    dcq, dck, dcv, dpab, d_a_log, d_dt = _gdn_pre_bwd_a(proj, pab, cq, ck, cv, a_log_p, dt_p, dqn, dkn, dvv, dg, dbeta,
                                                        name="gdn_pre_bwd", **gk)
    dpq, d_cq = _conv_bwd(proj, dcq, cq, heads=heads, width=dk, col=col_q, name="conv_q_bwd")
    dpk, d_ck = _conv_bwd(proj, dck, ck, heads=heads, width=dk, col=col_k, name="conv_k_bwd")
    dpv, d_cv = _conv_bwd(proj, dcv, cv, heads=heads, width=dv, col=col_v, name="conv_v_bwd")

    early = None
    if shards:
        slabs = [_col_slabs(_bf(jnp.concatenate([d_w_fg, d_w_fu], axis=1)))]
        slabs += [_bf(a).reshape(N_DEV, -1, d) for a in (d_w_bd, d_w_bs, d_w_out, d_w_fo)]
        dqs, dks, dvs, early = _sb_bwd(qs, ks, vs, do_sb, carry, heads=sb_heads, dh=dh, key_lo=pad_l, name="sb_bwd",
                                       scatter=slabs)
    else:
        dqs, dks, dvs = _sb_bwd(qs, ks, vs, do_sb, carry, heads=sb_heads, dh=dh, key_lo=pad_l, name="sb_bwd")
    dsq, dsk, dsv, d_g_sbq, d_g_sbk = _sb_pre_bwd(proj, g_sbq, g_sbk, dqs, dks, dvs, heads=sb_heads, dh=dh,
                                                   col_q=col_sq, col_k=col_sk, name="sb_pre_bwd")

    dproj = jnp.concatenate([dpq, dpk, dpv, dz, dsq, dsk, dsv, dgd, dgs], axis=1)
    dpab_b = _bf(dpab)
    d_w_main = _mm_tn(n1, dproj, name="dw_in_main")
    d_w_ab = _mm_tn(n1, dpab_b, name="dw_in_ab")
    s_in = None
    if shards:
        dh0, _, d_g_mix, (s_in,) = _mm_nt_rmsbwd([(dproj, w_main)], (dpab_b, w_ab), h0, g_mix, dh1, name="proj_bwd",
                                                 scatter=[_w_in_slabs(_bf(d_w_main), _bf(d_w_ab), col_sq, 2 * heads)])
    else:
        dh0, _, d_g_mix = _mm_nt_rmsbwd([(dproj, w_main)], (dpab_b, w_ab), h0, g_mix, dh1, name="proj_bwd")

    return dict(s_in=s_in,lsum=lsum, grad_x=dh0[row_x:rows], d_meta=dh0[pad_l:row_x], d_g_mix=d_g_mix, d_w_main=d_w_main,
                d_w_ab=d_w_ab, d_cq=d_cq, d_ck=d_ck, d_cv=d_cv, d_a_log=d_a_log[:, :heads], d_dt=d_dt[:, :heads],
                d_g_dn=d_g_dn, d_g_sbq=d_g_sbq, d_g_sbk=d_g_sbk, d_w_bd=d_w_bd, d_w_bs=d_w_bs, d_w_out=d_w_out,
                d_g_ffn=d_g_ffn, d_w_fg=d_w_fg, d_w_fu=d_w_fu, d_w_fo=d_w_fo, early=early)


def _pack(parts):
    flat = []
    for a in parts:
        a = a.reshape(-1)
        flat.append(jnp.pad(a, (0, (-a.shape[0]) % LANES)))
    v = jnp.concatenate(flat)
    v = jnp.pad(v, (0, (-v.shape[0]) % (8 * LANES)))
    return v.reshape(-1, LANES)


def _unpack(packed, shapes):
    flat = packed.reshape(-1)
    out, pos = [], 0
    for s in shapes:
        n = math.prod(s)
        out.append(flat[pos:pos + n].reshape(s))
        pos += n + (-n) % LANES
    return out


def kernel(x, meta_tokens, norm_mix_gain, w_in, conv_q, conv_k, conv_v, dn_a_log, dn_dt_bias, dn_out_norm_gain, sb_q_norm_gain, sb_k_norm_gain, w_branch_dn, w_branch_sb, w_out, norm_ffn_gain, w_ffn_in, w_ffn_out, loss_target, m_meta_tokens, m_norm_mix_gain, m_w_in, m_conv_q, m_conv_k, m_conv_v, m_dn_a_log, m_dn_dt_bias, m_dn_out_norm_gain, m_sb_q_norm_gain, m_sb_k_norm_gain, m_w_branch_dn, m_w_branch_sb, m_w_out, m_norm_ffn_gain, m_w_ffn_in, m_w_ffn_out, v_meta_tokens, v_norm_mix_gain, v_w_in, v_conv_q, v_conv_k, v_conv_v, v_dn_a_log, v_dn_dt_bias, v_dn_out_norm_gain, v_sb_q_norm_gain, v_sb_k_norm_gain, v_w_branch_dn, v_w_branch_sb, v_w_out, v_norm_ffn_gain, v_w_ffn_in, v_w_ffn_out):
    me = 4 * lax.axis_index("x") + 2 * lax.axis_index("y") + lax.axis_index("c")
    heads = dn_a_log.shape[1]
    d = x.shape[2]
    qk = conv_q.shape[2] * N_DEV
    dvt = conv_v.shape[2] * N_DEV
    col_ab = 2 * qk + 2 * dvt

    small_shapes = [meta_tokens.shape, conv_q.shape[1:], conv_k.shape[1:], conv_v.shape[1:]]
    small = _pack([meta_tokens, conv_q[0], conv_k[0], conv_v[0]])
    g_in, g_small = _gather_two_level([_bf(w_in[0]), small], name="gather_w_in")
    width = N_DEV * g_in.shape[2]
    w_main = jnp.concatenate(_orig_cols(g_in, 0, col_ab) + _orig_cols(g_in, col_ab + 2 * heads, width), axis=1)
    w_ab = _pad_lanes(jnp.concatenate(_orig_cols(g_in, col_ab, col_ab + 2 * heads), axis=1))
    parts = [_unpack(g_small[p], small_shapes) for p in range(N_DEV)]
    meta_f, cq_f, ck_f, cv_f = (jnp.concatenate([parts[p][a] for p in range(N_DEV)], axis=1) for a in range(4))

    r = _local_step(x[0], loss_target[0], meta_f, norm_mix_gain, w_main, w_ab, cq_f, ck_f, cv_f, dn_a_log, dn_dt_bias,
                    dn_out_norm_gain, sb_q_norm_gain, sb_k_norm_gain, norm_ffn_gain,
                    (_bf(w_ffn_in[0]), _bf(w_branch_dn[0]), _bf(w_branch_sb[0]), _bf(w_out[0]), _bf(w_ffn_out[0])),
                    shards=True)
    s_fi, s_bd, s_bs, s_out, s_fo = r["early"]
    s_in = r["s_in"]

    loss_part = (0.5 / d) * jnp.sum(r["lsum"], axis=1, keepdims=True)
    small_g = [r["d_meta"], r["d_g_mix"], r["d_cq"], r["d_ck"], r["d_cv"], r["d_a_log"], r["d_dt"], r["d_g_dn"],
               r["d_g_sbq"], r["d_g_sbk"], r["d_g_ffn"], loss_part]
    (g_packs,) = _exchange([_pack(small_g)], scatter=[False], name="gather_small_grads")
    (g_meta, g_mix, g_cq, g_ck, g_cv, g_al, g_dt, g_gdn, g_sbq, g_sbk, g_ffn, loss) = _unpack(
        _sum_slabs(g_packs, name="sum_small_grads"), [a.shape for a in small_g])

    def mine(a, width):
        return lax.dynamic_slice_in_dim(a, me * width, width, axis=1)

    big = dict(w_in=(s_in, w_in, m_w_in, v_w_in), w_branch_dn=(s_bd, w_branch_dn, m_w_branch_dn, v_w_branch_dn),
               w_branch_sb=(s_bs, w_branch_sb, m_w_branch_sb, v_w_branch_sb), w_out=(s_out, w_out, m_w_out, v_w_out),
               w_ffn_in=(s_fi, w_ffn_in, m_w_ffn_in, v_w_ffn_in), w_ffn_out=(s_fo, w_ffn_out, m_w_ffn_out, v_w_ffn_out))
    tiny = dict(meta_tokens=(mine(g_meta, d // N_DEV), meta_tokens, m_meta_tokens, v_meta_tokens),
                norm_mix_gain=(g_mix, norm_mix_gain, m_norm_mix_gain, v_norm_mix_gain),
                conv_q=(mine(g_cq, qk // N_DEV), conv_q[0], m_conv_q[0], v_conv_q[0]),
                conv_k=(mine(g_ck, qk // N_DEV), conv_k[0], m_conv_k[0], v_conv_k[0]),
                conv_v=(mine(g_cv, dvt // N_DEV), conv_v[0], m_conv_v[0], v_conv_v[0]),
                dn_a_log=(g_al, dn_a_log, m_dn_a_log, v_dn_a_log), dn_dt_bias=(g_dt, dn_dt_bias, m_dn_dt_bias, v_dn_dt_bias),
                dn_out_norm_gain=(g_gdn, dn_out_norm_gain, m_dn_out_norm_gain, v_dn_out_norm_gain),
                sb_q_norm_gain=(g_sbq, sb_q_norm_gain, m_sb_q_norm_gain, v_sb_q_norm_gain),
                sb_k_norm_gain=(g_sbk, sb_k_norm_gain, m_sb_k_norm_gain, v_sb_k_norm_gain),
                norm_ffn_gain=(g_ffn, norm_ffn_gain, m_norm_ffn_gain, v_norm_ffn_gain))
    order = ["meta_tokens", "norm_mix_gain", "w_in", "conv_q", "conv_k", "conv_v", "dn_a_log", "dn_dt_bias",
             "dn_out_norm_gain", "sb_q_norm_gain", "sb_k_norm_gain", "w_branch_dn", "w_branch_sb", "w_out",
             "norm_ffn_gain", "w_ffn_in", "w_ffn_out"]
    grads, deltas, new_m, new_v = [], [], [], []
    for name in order:
        if name in big:
            slabs, w, m, v = big[name]
            g, dl, mo, vo = _adamw_slabs(slabs, w, m, v, name="adamw_" + name)
            like = w.shape
        else:
            g, w, m, v = tiny[name]
            like = dict(conv_q=conv_q, conv_k=conv_k, conv_v=conv_v).get(name, w).shape
            dl, mo, vo = _adamw_small(g, w, m, v, name="adamw_" + name)
        for lst, a in ((grads, g), (deltas, dl), (new_m, mo), (new_v, vo)):
            lst.append(a.reshape(like))
    return (loss.reshape(()), r["grad_x"][None], *grads, *deltas, *new_m, *new_v)
```

```python
import functools
import math

import jax
import jax.numpy as jnp
from jax import lax
from jax.experimental import pallas as pl
from jax.experimental.pallas import tpu as pltpu

F32 = jnp.float32
BF16 = jnp.bfloat16

N_DEV = 8
CHUNK = 64
CHUNK_SHIFT = 6
GDN_ROWS = 2 * CHUNK
SB_BLOCK = 128
SB_HEADS_PER_STEP = 2
LANES = 128
RMS_EPS = 1e-6
L2_EPS = 1e-6
ADAM_LR = 0.001
ADAM_B1 = 0.9
ADAM_B2 = 0.999
ADAM_EPS = 1e-08
ADAM_WD = 0.01
ADAM_STEP = 10
V7X_VMEM_LIMIT_BYTES = 56 * 1024 * 1024
MM_TN_OUT_BLOCK_BYTES = 6 * 1024 * 1024
ROWS_BIG = (1056, 512, 384, 256, 128)
ROWS_MID = (528, 384, 256, 128)
SCAN_HEADS = 4

MESH = pl.DeviceIdType.MESH


def _params(*sem):
    return pltpu.CompilerParams(dimension_semantics=sem or None, vmem_limit_bytes=V7X_VMEM_LIMIT_BYTES)


def _pick(n, cands):
    for c in cands:
        if n % c == 0:
            return c
    raise ValueError(f"no block size among {cands} divides {n}")


def _bf(x):
    return x.astype(BF16)


def _dot(a, b):
    return jnp.dot(a, b, preferred_element_type=F32)


def _dot_nt(a, b):
    return lax.dot_general(a, b, (((1,), (1,)), ((), ())), preferred_element_type=F32)


def _dot_tn(a, b):
    return lax.dot_general(a, b, (((0,), (0,)), ((), ())), preferred_element_type=F32)


def _split2(x):
    hi = _bf(x)
    return hi, _bf(x - hi.astype(F32))


def _split3(x):
    hi = _bf(x)
    r = x - hi.astype(F32)
    mid = _bf(r)
    return hi, mid, _bf(r - mid.astype(F32))


def _dot_hp(a, b, dot=_dot):
    ah, al = _split2(a)
    bh, bl = _split2(b)
    return dot(ah, bh) + dot(ah, bl) + dot(al, bh)


def _dot_exact_l(m, x, dot=_dot):
    h, mi, lo = _split3(x)
    return dot(m, h) + dot(m, mi) + dot(m, lo)


def _sigmoid(x):
    return 1.0 / (1.0 + jnp.exp(-x))


def _silu(x):
    return x * _sigmoid(x)


def _silu_grad(x):
    s = _sigmoid(x)
    return s * (1.0 + x * (1.0 - s))


def _softplus(x):
    return jnp.maximum(x, 0.0) + jnp.log(1.0 + jnp.exp(-jnp.abs(x)))


def _rms_fwd(h, gain):
    r = lax.rsqrt(jnp.mean(h * h, axis=-1, keepdims=True) + RMS_EPS)
    return h * r * gain


def _rms_bwd(h, gain, dy):
    r = lax.rsqrt(jnp.mean(h * h, axis=-1, keepdims=True) + RMS_EPS)
    dyg = dy * gain
    dh = r * dyg - h * (r * r * r) * jnp.mean(dyg * h, axis=-1, keepdims=True)
    return dh, dy * h * r


def _iota(shape, dim):
    return lax.broadcasted_iota(jnp.int32, shape, dim)


def _lane_pick(x, idx):
    return jnp.sum(jnp.where(_iota(x.shape, 1) == idx, x, 0.0), axis=1, keepdims=True)


def _mm_nt(a, b, *, out_dtype, name):
    m, k = a.shape
    n = b.shape[0]
    tm, tn = _pick(m, ROWS_BIG), _pick(n, (1024, 512, 256, 128))

    def body(a_ref, b_ref, o_ref):
        o_ref[...] = _dot_nt(a_ref[...], b_ref[...]).astype(out_dtype)

    return pl.pallas_call(
        body, grid=(m // tm, n // tn),
        in_specs=[pl.BlockSpec((tm, k), lambda i, j: (i, 0)), pl.BlockSpec((tn, k), lambda i, j: (j, 0))],
        out_specs=pl.BlockSpec((tm, tn), lambda i, j: (i, j)),
        out_shape=jax.ShapeDtypeStruct((m, n), out_dtype), name=name,
        compiler_params=_params("parallel", "parallel"))(a, b)


def _mm_tn(a, b, *, name):
    t, m = a.shape
    n = b.shape[1]
    tn = _pick(n, (2816, 2048, 1408, 1024, 512, 256, 128))
    tm = _pick(m, tuple(c for c in (1408, 1024, 512, 256, 128) if c * tn * 4 <= MM_TN_OUT_BLOCK_BYTES))
    tk = _pick(t, (1408, 1024, 512, 384, 256, 128))

    def body(a_ref, b_ref, o_ref):
        @pl.when(pl.program_id(2) == 0)
        def _():
            o_ref[...] = jnp.zeros_like(o_ref)

        o_ref[...] += _dot_tn(a_ref[...], b_ref[...])

    return pl.pallas_call(
        body, grid=(m // tm, n // tn, t // tk),
        in_specs=[pl.BlockSpec((tk, tm), lambda i, j, k: (k, i)), pl.BlockSpec((tk, tn), lambda i, j, k: (k, j))],
        out_specs=pl.BlockSpec((tm, tn), lambda i, j, k: (i, j)),
        out_shape=jax.ShapeDtypeStruct((m, n), F32), name=name,
        compiler_params=_params("parallel", "parallel", "arbitrary"))(a, b)


def _mm_norm(h, gain, wt, *, name):
    m, k = h.shape
    n = wt.shape[0]
    tm, tn = _pick(m, ROWS_BIG), _pick(n, (1024, 512, 256, 128))

    def body(h_ref, g_ref, w_ref, o_ref, n_ref):
        @pl.when(pl.program_id(1) == 0)
        def _():
            n_ref[...] = _bf(_rms_fwd(h_ref[...], g_ref[...]))

        o_ref[...] = _dot_nt(n_ref[...], w_ref[...])

    return pl.pallas_call(
        body, grid=(m // tm, n // tn),
        in_specs=[pl.BlockSpec((tm, k), lambda i, j: (i, 0)), pl.BlockSpec((1, k), lambda i, j: (0, 0)),
                  pl.BlockSpec((tn, k), lambda i, j: (j, 0))],
        out_specs=[pl.BlockSpec((tm, tn), lambda i, j: (i, j)), pl.BlockSpec((tm, k), lambda i, j: (i, 0))],
        out_shape=[jax.ShapeDtypeStruct((m, n), F32), jax.ShapeDtypeStruct((m, k), BF16)], name=name,
        compiler_params=_params("parallel", "arbitrary"))(h, gain, wt)


def _mm_norm_swiglu(h, gain, wgt, wut, *, name):
    m, k = h.shape
    n = wgt.shape[0]
    tm, tn = _pick(m, ROWS_MID), _pick(n, (1408, 1024, 512, 256, 128))

    def body(h_ref, g_ref, wg_ref, wu_ref, gate_ref, up_ref, act_ref, n_ref):
        @pl.when(pl.program_id(1) == 0)
        def _():
            n_ref[...] = _bf(_rms_fwd(h_ref[...], g_ref[...]))

        gate = _dot_nt(n_ref[...], wg_ref[...])
        up = _dot_nt(n_ref[...], wu_ref[...])
        gate_ref[...] = gate
        up_ref[...] = up
        act_ref[...] = _bf(_silu(gate) * up)

    wspec = pl.BlockSpec((tn, k), lambda i, j: (j, 0))
    ospec = pl.BlockSpec((tm, tn), lambda i, j: (i, j))
    return pl.pallas_call(
        body, grid=(m // tm, n // tn),
        in_specs=[pl.BlockSpec((tm, k), lambda i, j: (i, 0)), pl.BlockSpec((1, k), lambda i, j: (0, 0)), wspec, wspec],
        out_specs=[ospec, ospec, ospec, pl.BlockSpec((tm, k), lambda i, j: (i, 0))],
        out_shape=[jax.ShapeDtypeStruct((m, n), F32), jax.ShapeDtypeStruct((m, n), F32),
                   jax.ShapeDtypeStruct((m, n), BF16), jax.ShapeDtypeStruct((m, k), BF16)], name=name,
        compiler_params=_params("parallel", "arbitrary"))(h, gain, wgt, wut)


def _mm_res(res, a, b, *, name):
    m, k = a.shape
    n = b.shape[1]
    tm, tn = _pick(m, ROWS_BIG), _pick(n, (1024, 512, 256, 128))

    def body(r_ref, a_ref, b_ref, o_ref):
        o_ref[...] = r_ref[...] + _dot(a_ref[...], b_ref[...])

    return pl.pallas_call(
        body, grid=(m // tm, n // tn),
        in_specs=[pl.BlockSpec((tm, tn), lambda i, j: (i, j)), pl.BlockSpec((tm, k), lambda i, j: (i, 0)),
                  pl.BlockSpec((k, tn), lambda i, j: (0, j))],
        out_specs=pl.BlockSpec((tm, tn), lambda i, j: (i, j)),
        out_shape=jax.ShapeDtypeStruct((m, n), F32), name=name,
        compiler_params=_params("parallel", "parallel"))(res, a, b)


def _mm_res_loss(res, a, b, target, *, row0, nrows, name):
    m, k = a.shape
    n = b.shape[1]
    tm = _pick(m, ROWS_MID)

    def body(r_ref, a_ref, b_ref, t_ref, dy_ref, dyb_ref, ls_ref):
        i = pl.program_id(0)

        @pl.when(i == 0)
        def _():
            ls_ref[...] = jnp.zeros_like(ls_ref)

        y = r_ref[...] + _dot(a_ref[...], b_ref[...])
        row = i * tm + _iota((tm, n), 0)
        e = jnp.where((row >= row0) & (row < row0 + nrows), y - t_ref[...], 0.0)
        dy = e / n
        dy_ref[...] = dy
        dyb_ref[...] = _bf(dy)
        ls_ref[...] += jnp.sum(e * e, axis=0, keepdims=True)

    rspec = pl.BlockSpec((tm, n), lambda i: (i, 0))
    return pl.pallas_call(
        body, grid=(m // tm,),
        in_specs=[rspec, pl.BlockSpec((tm, k), lambda i: (i, 0)), pl.BlockSpec((k, n), lambda i: (0, 0)), rspec],
        out_specs=[rspec, rspec, pl.BlockSpec((1, n), lambda i: (0, 0))],
        out_shape=[jax.ShapeDtypeStruct((m, n), F32), jax.ShapeDtypeStruct((m, n), BF16),
                   jax.ShapeDtypeStruct((1, n), F32)], name=name,
        compiler_params=_params("arbitrary"))(res, a, b, target)


def _merge_fwd(o_dn, o_sb, wbd, wbs, proj, *, col_gd, col_gs, name):
    m, kd = o_dn.shape
    ks = o_sb.shape[1]
    n = wbd.shape[1]
    tm = _pick(m, ROWS_BIG)
    tn = _pick(math.gcd(n, math.gcd(col_gd, col_gs)), (512, 256, 128))

    def body(od_ref, os_ref, wd_ref, ws_ref, gd_ref, gs_ref, mg_ref, bd_ref, bs_ref):
        bd = _dot(od_ref[...], wd_ref[...])
        bs = _dot(os_ref[...], ws_ref[...])
        bd_ref[...] = bd
        bs_ref[...] = bs
        mg_ref[...] = _bf(_sigmoid(gd_ref[...]) * bd + _sigmoid(gs_ref[...]) * bs)

    ospec = pl.BlockSpec((tm, tn), lambda i, j: (i, j))
    return pl.pallas_call(
        body, grid=(m // tm, n // tn),
        in_specs=[pl.BlockSpec((tm, kd), lambda i, j: (i, 0)), pl.BlockSpec((tm, ks), lambda i, j: (i, 0)),
                  pl.BlockSpec((kd, tn), lambda i, j: (0, j)), pl.BlockSpec((ks, tn), lambda i, j: (0, j)),
                  pl.BlockSpec((tm, tn), lambda i, j: (i, col_gd // tn + j)),
                  pl.BlockSpec((tm, tn), lambda i, j: (i, col_gs // tn + j))],
        out_specs=[ospec, ospec, ospec],
        out_shape=[jax.ShapeDtypeStruct((m, n), BF16), jax.ShapeDtypeStruct((m, n), F32),
                   jax.ShapeDtypeStruct((m, n), F32)], name=name,
        compiler_params=_params("parallel", "parallel"))(o_dn, o_sb, wbd, wbs, proj, proj)


def _merge_bwd(dh, w_out, proj, br_dn, br_sb, *, col_gd, col_gs, name):
    m, k = dh.shape
    n = w_out.shape[0]
    tm = _pick(m, ROWS_BIG)
    tn = _pick(math.gcd(n, math.gcd(col_gd, col_gs)), (512, 256, 128))

    def body(dh_ref, w_ref, gd_ref, gs_ref, bd_ref, bs_ref, dbd_ref, dbs_ref, dgd_ref, dgs_ref):
        dm = _dot_nt(dh_ref[...], w_ref[...])
        sd = _sigmoid(gd_ref[...])
        ss = _sigmoid(gs_ref[...])
        dbd_ref[...] = _bf(dm * sd)
        dbs_ref[...] = _bf(dm * ss)
        dgd_ref[...] = _bf(dm * bd_ref[...] * sd * (1.0 - sd))
        dgs_ref[...] = _bf(dm * bs_ref[...] * ss * (1.0 - ss))

    ospec = pl.BlockSpec((tm, tn), lambda i, j: (i, j))
    return pl.pallas_call(
        body, grid=(m // tm, n // tn),
        in_specs=[pl.BlockSpec((tm, k), lambda i, j: (i, 0)), pl.BlockSpec((tn, k), lambda i, j: (j, 0)),
                  pl.BlockSpec((tm, tn), lambda i, j: (i, col_gd // tn + j)),
                  pl.BlockSpec((tm, tn), lambda i, j: (i, col_gs // tn + j)), ospec, ospec],
        out_specs=[ospec] * 4,
        out_shape=[jax.ShapeDtypeStruct((m, n), BF16)] * 4, name=name,
        compiler_params=_params("parallel", "parallel"))(dh, w_out, proj, proj, br_dn, br_sb)


def _swiglu_bwd(dy, wfo, gate, up, *, name):
    m, k = dy.shape
    n = wfo.shape[0]
    tm, tn = _pick(m, ROWS_MID), _pick(n, (1408, 1024, 512, 256, 128))

    def body(dy_ref, w_ref, g_ref, u_ref, dg_ref, du_ref):
        da = _dot_nt(dy_ref[...], w_ref[...])
        g = g_ref[...]
        dg_ref[...] = _bf(da * u_ref[...] * _silu_grad(g))
        du_ref[...] = _bf(da * _silu(g))

    ospec = pl.BlockSpec((tm, tn), lambda i, j: (i, j))
    return pl.pallas_call(
        body, grid=(m // tm, n // tn),
        in_specs=[pl.BlockSpec((tm, k), lambda i, j: (i, 0)), pl.BlockSpec((tn, k), lambda i, j: (j, 0)), ospec, ospec],
        out_specs=[ospec, ospec], out_shape=[jax.ShapeDtypeStruct((m, n), BF16)] * 2, name=name,
        compiler_params=_params("parallel", "parallel"))(dy, wfo, gate, up)


def _mm_rmsbwd(pairs, extra, h, gain, dres, *, name, scatter=None):
    m, k = pairs[0][0].shape
    n = h.shape[1]
    tm = _pick(m, ROWS_MID)
    tk = _pick(k, (1408, 1024, 512, 256, 128))
    nk = k // tk
    np_ = len(pairs)

    def body(*refs):
        ab = refs[:2 * np_]
        pos = 2 * np_
        ex = refs[pos:pos + 2] if extra is not None else ()
        pos += len(ex)
        h_ref, g_ref, r_ref, dh_ref, dhb_ref, dg_ref, acc_ref = refs[pos:]
        i, kk = pl.program_id(0), pl.program_id(1)

        @pl.when((i == 0) & (kk == 0))
        def _():
            dg_ref[...] = jnp.zeros_like(dg_ref)

        part = _dot(ab[0][...], ab[1][...])
        for p in range(1, np_):
            part += _dot(ab[2 * p][...], ab[2 * p + 1][...])

        @pl.when(kk == 0)
        def _():
            first = part
            if ex:
                first = first + _dot(ex[0][...], ex[1][...])
            acc_ref[...] = first

        @pl.when(kk > 0)
        def _():
            acc_ref[...] += part

        @pl.when(kk == nk - 1)
        def _():
            dh, dgr = _rms_bwd(h_ref[...], g_ref[...], acc_ref[...])
            dh = dh + r_ref[...]
            dh_ref[...] = dh
            dhb_ref[...] = _bf(dh)
            dg_ref[...] += jnp.sum(dgr, axis=0, keepdims=True)

    in_specs, args = [], []
    for a, b in pairs:
        in_specs += [pl.BlockSpec((tm, tk), lambda i, kk: (i, kk)), pl.BlockSpec((tk, n), lambda i, kk: (kk, 0))]
        args += [a, b]
    if extra is not None:
        k2 = extra[0].shape[1]
        in_specs += [pl.BlockSpec((tm, k2), lambda i, kk: (i, 0)), pl.BlockSpec((k2, n), lambda i, kk: (0, 0))]
        args += list(extra)
    rspec = pl.BlockSpec((tm, n), lambda i, kk: (i, 0))
    in_specs += [rspec, pl.BlockSpec((1, n), lambda i, kk: (0, 0)), rspec]
    call = dict(grid=(m // tm, nk), in_specs=in_specs,
                out_specs=[rspec, rspec, pl.BlockSpec((1, n), lambda i, kk: (0, 0))],
                out_shape=[jax.ShapeDtypeStruct((m, n), F32), jax.ShapeDtypeStruct((m, n), BF16),
                           jax.ShapeDtypeStruct((1, n), F32)],
                scratch_shapes=[pltpu.VMEM((tm, n), F32)], name=name)
    if scatter is None:
        return pl.pallas_call(body, compiler_params=_params("arbitrary", "arbitrary"), **call)(*args, h, gain, dres)
    res = _call_with_exchange(body, args=(*args, h, gain, dres), srcs=scatter, scatter=[True] * len(scatter), **call)
    return [*res[:3], list(res[3:])]


def _conv_taps(cur, prev8, w_ref, first):
    nk = w_ref.shape[0]
    out = cur * w_ref[nk - 1:nk, :]
    for s in range(1, nk):
        out += _shift_down(cur, prev8, s, first) * w_ref[nk - 1 - s:nk - s, :]
    return out


def _shift_up(cur, next8, s, last):
    rows = cur.shape[0]
    row = _iota(cur.shape, 0)
    next8 = jnp.where(last, 0.0, next8)
    sh = pltpu.roll(cur, rows - s, axis=0)
    nh = jnp.tile(pltpu.roll(next8, 8 - s, axis=0), (rows // 8, 1))
    return jnp.where(row >= rows - s, nh, sh)


def _shift_down(cur, prev8, s, first):
    rows = cur.shape[0]
    row = _iota(cur.shape, 0)
    prev8 = jnp.where(first, 0.0, prev8)
    sh = pltpu.roll(cur, s, axis=0)
    ph = jnp.tile(pltpu.roll(prev8, s, axis=0), (rows // 8, 1))
    return jnp.where(row < s, ph, sh)


def _gdn_pre(proj, pab, cq, ck, cv, a_log, dt_bias, *, heads, dk, dv, col_q, col_k, col_v, row_lo, row_hi, name):
    t = proj.shape[0]
    tm = _pick(t, (384, 256, 128))
    nb = t // tm

    def body(pq_ref, pqp_ref, pk_ref, pkp_ref, pv_ref, pvp_ref, ab_ref, cq_ref, ck_ref, cv_ref, al_ref, dt_ref,
             qn_ref, kn_ref, v_ref, g_ref, b_ref):
        h, i = pl.program_id(0), pl.program_id(1)
        first = i == 0
        row = i * tm + _iota((tm, 1), 0)
        valid = (row >= row_lo) & (row < row_hi)
        q1 = _silu(_conv_taps(pq_ref[...], pqp_ref[...], cq_ref, first))
        k1 = _silu(_conv_taps(pk_ref[...], pkp_ref[...], ck_ref, first))
        v1 = _silu(_conv_taps(pv_ref[...], pvp_ref[...], cv_ref, first))
        qn_ref[...] = jnp.where(valid, q1 * lax.rsqrt(jnp.sum(q1 * q1, axis=-1, keepdims=True) + L2_EPS), 0.0)
        kn_ref[...] = jnp.where(valid, k1 * lax.rsqrt(jnp.sum(k1 * k1, axis=-1, keepdims=True) + L2_EPS), 0.0)
        v_ref[...] = jnp.where(valid, v1, 0.0)
        ab = ab_ref[...]
        da = _lane_pick(ab, h)
        db = _lane_pick(ab, heads + h)
        a = _lane_pick(al_ref[...], h)
        dtb = _lane_pick(dt_ref[...], h)
        g_ref[...] = jnp.where(valid, -jnp.exp(a) * _softplus(da + dtb), 0.0)
        b_ref[...] = jnp.where(valid, _sigmoid(db), 0.0)

    def cur(width, col):
        return pl.BlockSpec((tm, width), lambda h, i: (i, col // width + h))

    def prev(width, col):
        return pl.BlockSpec((8, width), lambda h, i: (jnp.maximum(i * (tm // 8) - 1, 0), col // width + h))

    def out(width):
        return pl.BlockSpec((None, tm, width), lambda h, i: (h, i, 0))

    small = pl.BlockSpec((1, LANES), lambda h, i: (0, 0))
    return pl.pallas_call(
        body, grid=(heads, nb),
        in_specs=[cur(dk, col_q), prev(dk, col_q), cur(dk, col_k), prev(dk, col_k), cur(dv, col_v), prev(dv, col_v),
                  pl.BlockSpec((tm, LANES), lambda h, i: (i, 0)),
                  pl.BlockSpec((cq.shape[0], dk), lambda h, i: (0, h)), pl.BlockSpec((ck.shape[0], dk), lambda h, i: (0, h)),
                  pl.BlockSpec((cv.shape[0], dv), lambda h, i: (0, h)), small, small],
        out_specs=[out(dk), out(dk), out(dv), out(1), out(1)],
        out_shape=[jax.ShapeDtypeStruct((heads, t, dk), F32), jax.ShapeDtypeStruct((heads, t, dk), F32),
                   jax.ShapeDtypeStruct((heads, t, dv), F32), jax.ShapeDtypeStruct((heads, t, 1), F32),
                   jax.ShapeDtypeStruct((heads, t, 1), F32)], name=name,
        compiler_params=_params("parallel", "parallel"))(proj, proj, proj, proj, proj, proj, pab, cq, ck, cv, a_log, dt_bias)


def _chunk_masks(rows=GDN_ROWS, row0=0):
    ri = row0 + _iota((rows, GDN_ROWS), 0)
    ci = _iota((rows, GDN_ROWS), 1)
    same = jnp.right_shift(ri, CHUNK_SHIFT) == jnp.right_shift(ci, CHUNK_SHIFT)
    return same, same & (ri >= ci), same & (ri > ci), ri == ci


def _col_to_row(col, eye):
    return jnp.sum(jnp.where(eye, col, 0.0), axis=0, keepdims=True)


def _row_to_col(row, eye):
    return jnp.sum(jnp.where(eye, row, 0.0), axis=1, keepdims=True)


def _chunk_common(blocks, dk_scale):
    same, incl, strict, eye = _chunk_masks()
    tri = jnp.where(incl, 1.0, 0.0).astype(BF16)
    tot = jnp.where(same, 1.0, 0.0).astype(BF16)
    gbs = [jnp.broadcast_to(g, (GDN_ROWS, LANES)) for _, _, g, _ in blocks]
    gams = [jnp.max(_dot_exact_l(tri, gb), axis=1, keepdims=True) for gb in gbs]
    lasts = [jnp.max(_dot_exact_l(tot, gb), axis=1, keepdims=True) for gb in gbs]
    kbs = [kn * beta for _, kn, _, beta in blocks]
    qts = [qn * dk_scale for qn, _, _, _ in blocks]
    boths = [_dot_nt(_bf(jnp.concatenate([kb, qt], axis=0)), _bf(blk[1]))
             for kb, qt, blk in zip(kbs, qts, blocks)]
    out = []
    for gam, last, kb, qt, both in zip(gams, lasts, kbs, qts, boths):
        diff = gam - _col_to_row(gam, eye)
        decay = jnp.where(incl, jnp.exp(jnp.where(incl, diff, 0.0)), 0.0)
        out.append(dict(incl=incl, strict=strict, eye=eye, decay=decay, eg=jnp.exp(gam), ek=jnp.exp(last - gam),
                        egl=jnp.exp(last), kb=kb, qt=qt, lmat=jnp.where(strict, both[:GDN_ROWS] * decay, 0.0),
                        pmat=jnp.where(incl, both[GDN_ROWS:] * decay, 0.0)))
    return out


def _gdn_prep(qn, kn, v, g, beta, *, name):
    heads, t, dk = qn.shape
    dv = v.shape[2]
    rows = _pick(t, (3 * GDN_ROWS, 2 * GDN_ROWS, GDN_ROWS))
    dk_scale = dk ** -0.5

    def body(q_ref, k_ref, v_ref, g_ref, b_ref, u_ref, w_ref, p_ref, qd_ref, kd_ref, egl_ref, t_ref):
        rs = [pl.ds(b * GDN_ROWS, GDN_ROWS) for b in range(rows // GDN_ROWS)]
        cs = _chunk_common([(q_ref[r, :], k_ref[r, :], g_ref[r, :], b_ref[r, :]) for r in rs], dk_scale)
        eye_f = jnp.where(cs[0]["eye"], 1.0, 0.0)
        tinvs = [eye_f - c["lmat"] for c in cs]
        ys = [_dot_hp(c["lmat"], c["lmat"]) for c in cs]
        for _ in range(CHUNK_SHIFT - 1):
            boths = [_dot_hp(y, jnp.concatenate([y, tinv], axis=1)) for y, tinv in zip(ys, tinvs)]
            ys = [both[:, :GDN_ROWS] for both in boths]
            tinvs = [tinv + both[:, GDN_ROWS:] for tinv, both in zip(tinvs, boths)]
        uws = [_dot_hp(tinv, jnp.concatenate([v_ref[r, :] * b_ref[r, :], c["kb"] * c["eg"]], axis=1))
               for r, c, tinv in zip(rs, cs, tinvs)]
        for r, c, tinv, uw in zip(rs, cs, tinvs, uws):
            u_ref[r, :] = uw[:, :dv]
            w_ref[r, :] = uw[:, dv:]
            p_ref[r, :] = c["pmat"]
            qd_ref[r, :] = c["qt"] * c["eg"]
            kd_ref[r, :] = k_ref[r, :] * c["ek"]
            egl_ref[r, :] = c["egl"]
            t_ref[r, :] = tinv

    def blk(width):
        return pl.BlockSpec((None, rows, width), lambda h, i: (h, i, 0))

    def shp(width):
        return jax.ShapeDtypeStruct((heads, t, width), F32)

    return pl.pallas_call(
        body, grid=(heads, t // rows), in_specs=[blk(dk), blk(dk), blk(dv), blk(1), blk(1)],
        out_specs=[blk(dv), blk(dk), blk(GDN_ROWS), blk(dk), blk(dk), blk(1), blk(GDN_ROWS)],
        out_shape=[shp(dv), shp(dk), shp(GDN_ROWS), shp(dk), shp(dk), shp(1), shp(GDN_ROWS)], name=name,
        compiler_params=_params("parallel", "parallel"))(qn, kn, v, g, beta)


def _gdn_scan(u, w, p, qd, kd, egl, proj, gain, *, col_z, name):
    heads, t, dv = u.shape
    dk = w.shape[2]
    nb = t // GDN_ROWS
    sub = GDN_ROWS // CHUNK
    hp = SCAN_HEADS

    def body(u_ref, w_ref, p_ref, qd_ref, kd_ref, egl_ref, z_ref, gn_ref, o_ref, og_ref, st_ref, s_ref):
        @pl.when(pl.program_id(1) == 0)
        def _():
            s_ref[...] = jnp.zeros_like(s_ref)

        hs = range(hp)
        vn_parts = [[jnp.zeros((CHUNK, dv), F32)] * sub for _ in hs]
        for c in range(sub):
            r = pl.ds(c * CHUNK, CHUNK)
            ss = [s_ref[hh] for hh in hs]
            sbs = [_bf(s) for s in ss]
            wss = [_dot(_bf(jnp.concatenate([w_ref[hh, r, :], qd_ref[hh, r, :]], axis=0)), sbs[hh])
                   for hh in hs]
            vns = [u_ref[hh, r, :] - wss[hh][:CHUNK] for hh in hs]
            for hh in hs:
                vn_parts[hh][c] = vns[hh]
            os_ = [wss[hh][CHUNK:] + _dot(_bf(p_ref[hh, r, :]), _bf(jnp.concatenate(vn_parts[hh], axis=0))) for hh in hs]
            new = [ss[hh] * egl_ref[hh, pl.ds(c * CHUNK, 1), :] + _dot_tn(_bf(kd_ref[hh, r, :]), _bf(vns[hh])) for hh in hs]
            for hh in hs:
                cols = pl.ds(hh * dv, dv)
                st_ref[hh, c] = ss[hh]
                s_ref[hh] = new[hh]
                o_ref[hh, r, :] = os_[hh]
                og_ref[r, cols] = _bf(_rms_fwd(os_[hh], gn_ref[...]) * _silu(z_ref[r, cols]))

    def blk(width):
        return pl.BlockSpec((hp, GDN_ROWS, width), lambda h, i: (h, i, 0))

    return pl.pallas_call(
        body, grid=(heads // hp, nb),
        in_specs=[blk(dv), blk(dk), blk(GDN_ROWS), blk(dk), blk(dk), blk(1),
                  pl.BlockSpec((GDN_ROWS, hp * dv), lambda h, i: (i, col_z // (hp * dv) + h)),
                  pl.BlockSpec((1, dv), lambda h, i: (0, 0))],
        out_specs=[blk(dv), pl.BlockSpec((GDN_ROWS, hp * dv), lambda h, i: (i, h)),
                   pl.BlockSpec((hp, sub, dk, dv), lambda h, i: (h, i, 0, 0))],
        out_shape=[jax.ShapeDtypeStruct((heads, t, dv), F32), jax.ShapeDtypeStruct((t, heads * dv), BF16),
                   jax.ShapeDtypeStruct((heads, t // CHUNK, dk, dv), F32)],
        scratch_shapes=[pltpu.VMEM((hp, dk, dv), F32)], name=name,
        compiler_params=_params("parallel", "arbitrary"))(u, w, p, qd, kd, egl, proj, gain)


def _gdn_post_bwd(o, proj, gain, dout, *, col_z, name):
    heads, t, dv = o.shape
    tm = _pick(t, (384, 256, 128))

    def body(o_ref, z_ref, gn_ref, d_ref, do_ref, dz_ref, dg_ref):
        @pl.when((pl.program_id(0) == 0) & (pl.program_id(1) == 0))
        def _():
            dg_ref[...] = jnp.zeros_like(dg_ref)

        o_, z, d = o_ref[...], z_ref[...], d_ref[...]
        y = _rms_fwd(o_, gn_ref[...])
        dz_ref[...] = _bf(d * y * _silu_grad(z))
        do, dgr = _rms_bwd(o_, gn_ref[...], d * _silu(z))
        do_ref[...] = do
        dg_ref[...] += jnp.sum(dgr, axis=0, keepdims=True)

    return pl.pallas_call(
        body, grid=(t // tm, heads),
        in_specs=[pl.BlockSpec((None, tm, dv), lambda i, h: (h, i, 0)),
                  pl.BlockSpec((tm, dv), lambda i, h: (i, col_z // dv + h)),
                  pl.BlockSpec((1, dv), lambda i, h: (0, 0)), pl.BlockSpec((tm, dv), lambda i, h: (i, h))],
        out_specs=[pl.BlockSpec((None, tm, dv), lambda i, h: (h, i, 0)), pl.BlockSpec((tm, dv), lambda i, h: (i, h)),
                   pl.BlockSpec((1, dv), lambda i, h: (0, 0))],
        out_shape=[jax.ShapeDtypeStruct((heads, t, dv), F32), jax.ShapeDtypeStruct((t, heads * dv), BF16),
                   jax.ShapeDtypeStruct((1, dv), F32)], name=name,
        compiler_params=_params("arbitrary", "arbitrary"))(o, proj, gain, dout)


def _gdn_bwd_scan(u, w, p, qd, kd, egl, st, do, *, name):
    heads, t, dv = u.shape
    dk = w.shape[2]
    nb = t // GDN_ROWS
    sub = GDN_ROWS // CHUNK
    hp = SCAN_HEADS

    def body(u_ref, w_ref, p_ref, qd_ref, kd_ref, egl_ref, st_ref, do_ref,
             du_ref, dw_ref, dp_ref, dqd_ref, dkd_ref, dgl_ref, ds_ref):
        @pl.when(pl.program_id(1) == 0)
        def _():
            ds_ref[...] = jnp.zeros_like(ds_ref)

        hs = range(hp)
        zeros = jnp.zeros((CHUNK, dv), BF16)
        for c in reversed(range(sub)):
            r = pl.ds(c * CHUNK, CHUNK)
            ss = [st_ref[hh, c] for hh in hs]
            sbs = [_bf(s) for s in ss]
            dss = [ds_ref[hh] for hh in hs]
            dsbs = [_bf(ds) for ds in dss]
            dobs = [_bf(do_ref[hh, r, :]) for hh in hs]
            wbs = [_bf(w_ref[hh, r, :]) for hh in hs]
            vns = [u_ref[hh, r, :] - _dot(wbs[hh], sbs[hh]) for hh in hs]
            dvns = [_dot_tn(_bf(p_ref[hh, r, :]), dobs[hh])[c * CHUNK:(c + 1) * CHUNK, :]
                    + _dot(_bf(kd_ref[hh, r, :]), dsbs[hh]) for hh in hs]
            dods = [jnp.concatenate([dobs[hh], _bf(dvns[hh])], axis=0) for hh in hs]
            boths = [_dot_nt(dods[hh], sbs[hh]) for hh in hs]
            dps = [_dot_nt(dobs[hh], jnp.concatenate([_bf(vns[hh]) if cc == c else zeros for cc in range(sub)], axis=0))
                   for hh in hs]
            dkds = [_dot_nt(_bf(vns[hh]), dsbs[hh]) for hh in hs]
            new = [dss[hh] * egl_ref[hh, pl.ds(c * CHUNK, 1), :]
                   + _dot_tn(jnp.concatenate([_bf(qd_ref[hh, r, :]), -wbs[hh]], axis=0), dods[hh])
                   for hh in hs]
            for hh in hs:
                du_ref[hh, r, :] = dvns[hh]
                dw_ref[hh, r, :] = -boths[hh][CHUNK:]
                dp_ref[hh, r, :] = jnp.where(_chunk_masks(CHUNK, c * CHUNK)[1], dps[hh], 0.0)
                dqd_ref[hh, r, :] = boths[hh][:CHUNK]
                dkd_ref[hh, r, :] = dkds[hh]
                dgl = jnp.sum(jnp.sum(dss[hh] * ss[hh], axis=1, keepdims=True), axis=0, keepdims=True)
                dgl_ref[hh, r, :] = jnp.where(_iota((CHUNK, 1), 0) == CHUNK - 1, dgl, 0.0)
                ds_ref[hh] = new[hh]

    def blk(width):
        return pl.BlockSpec((hp, GDN_ROWS, width), lambda h, i: (h, nb - 1 - i, 0))

    def shp(width):
        return jax.ShapeDtypeStruct((heads, t, width), F32)

    return pl.pallas_call(
        body, grid=(heads // hp, nb),
        in_specs=[blk(dv), blk(dk), blk(GDN_ROWS), blk(dk), blk(dk), blk(1),
                  pl.BlockSpec((hp, sub, dk, dv), lambda h, i: (h, nb - 1 - i, 0, 0)), blk(dv)],
        out_specs=[blk(dv), blk(dk), blk(GDN_ROWS), blk(dk), blk(dk), blk(1)],
        out_shape=[shp(dv), shp(dk), shp(GDN_ROWS), shp(dk), shp(dk), shp(1)],
        scratch_shapes=[pltpu.VMEM((hp, dk, dv), F32)], name=name,
        compiler_params=_params("parallel", "arbitrary"))(u, w, p, qd, kd, egl, st, do)


def _gdn_bwd_prep(qn, kn, v, g, beta, tinv, u, w, du, dw, dp, dqd, dkd, dgl, *, name):
    heads, t, dk = qn.shape
    dv = v.shape[2]
    rows = _pick(t, (3 * GDN_ROWS, 2 * GDN_ROWS, GDN_ROWS))
    dk_scale = dk ** -0.5

    def rowsum(x):
        return jnp.sum(x, axis=1, keepdims=True)

    def body(q_ref, k_ref, v_ref, g_ref, b_ref, t_ref, u_ref, w_ref, du_ref, dw_ref, dp_ref, dqd_ref, dkd_ref, dgl_ref,
             dq_ref, dkk_ref, dvv_ref, dg_ref, db_ref):
        rs = [pl.ds(b * GDN_ROWS, GDN_ROWS) for b in range(rows // GDN_ROWS)]
        cs = _chunk_common([(q_ref[r, :], k_ref[r, :], g_ref[r, :], b_ref[r, :]) for r in rs], dk_scale)
        dbvws = [_dot_hp(t_ref[r, :], jnp.concatenate([du_ref[r, :], dw_ref[r, :]], axis=1), _dot_tn)
                 for r in rs]
        das = [-_dot_nt(_bf(dbvw), _bf(jnp.concatenate([u_ref[r, :], w_ref[r, :]], axis=1)))
               for r, dbvw in zip(rs, dbvws)]
        dls = [jnp.where(c["strict"], da, 0.0) for c, da in zip(cs, das)]
        dmns = [_bf(jnp.concatenate([dl * c["decay"], dp_ref[r, :] * c["decay"]], axis=0))
                for r, c, dl in zip(rs, cs, dls)]
        boths = [_dot(dmn, _bf(k_ref[r, :])) for r, dmn in zip(rs, dmns)]
        dkns = [_dot_tn(dmn, _bf(jnp.concatenate([c["kb"], c["qt"]], axis=0)))
                for c, dmn in zip(cs, dmns)]
        for r, c, dbvw, dl, both, dkn in zip(rs, cs, dbvws, dls, boths, dkns):
            kn_, beta_, v_ = k_ref[r, :], b_ref[r, :], v_ref[r, :]
            eye = c["eye"]
            kb, qt, eg, ek = c["kb"], c["qt"], c["eg"], c["ek"]
            dbv, dbw = dbvw[:, :dv], dbvw[:, dv:]
            dp_ = dp_ref[r, :]
            dkb = both[:GDN_ROWS] + dbw * eg
            dqt = both[GDN_ROWS:]
            gmat = dl * c["lmat"] + dp_ * c["pmat"]
            dqd_, dkd_ = dqd_ref[r, :], dkd_ref[r, :]
            qd = qt * eg
            kd = kn_ * ek
            bw = kb * eg
            kdsum = rowsum(dkd_ * kd)
            dgam = rowsum(gmat) - _row_to_col(jnp.sum(gmat, axis=0, keepdims=True), eye)
            dgam += rowsum(dbw * bw) + rowsum(dqd_ * qd) - kdsum
            last = (_iota((GDN_ROWS, 1), 0) & (CHUNK - 1)) == CHUNK - 1
            same = _chunk_masks()[0]
            same_f = jnp.where(same, 1.0, 0.0).astype(BF16)
            chunk_tot = jnp.max(_dot_exact_l(same_f, jnp.broadcast_to(kdsum, (GDN_ROWS, LANES))), axis=1, keepdims=True)
            dgam += jnp.where(last, chunk_tot, 0.0) + dgl_ref[r, :] * c["egl"]
            dq_ref[r, :] = (dqt + dqd_ * eg) * dk_scale
            dkk_ref[r, :] = dkn + dkd_ * ek + dkb * beta_
            dvv_ref[r, :] = dbv * beta_
            db_ref[r, :] = rowsum(dbv * v_) + rowsum(dkb * kn_)
            upper = jnp.where(same & (_iota((GDN_ROWS, GDN_ROWS), 0) <= _iota((GDN_ROWS, GDN_ROWS), 1)), 1.0, 0.0)
            dgb = _dot_exact_l(upper.astype(BF16), jnp.broadcast_to(dgam, (GDN_ROWS, LANES)))
            dg_ref[r, :] = _lane_pick(dgb, 0)

    def blk(width):
        return pl.BlockSpec((None, rows, width), lambda h, i: (h, i, 0))

    def shp(width):
        return jax.ShapeDtypeStruct((heads, t, width), F32)

    return pl.pallas_call(
        body, grid=(heads, t // rows),
        in_specs=[blk(dk), blk(dk), blk(dv), blk(1), blk(1), blk(GDN_ROWS), blk(dv), blk(dk),
                  blk(dv), blk(dk), blk(GDN_ROWS), blk(dk), blk(dk), blk(1)],
        out_specs=[blk(dk), blk(dk), blk(dv), blk(1), blk(1)],
        out_shape=[shp(dk), shp(dk), shp(dv), shp(1), shp(1)], name=name,
        compiler_params=_params("parallel", "parallel"))(qn, kn, v, g, beta, tinv, u, w, du, dw, dp, dqd, dkd, dgl)


def _gdn_pre_bwd_a(proj, pab, cq, ck, cv, a_log, dt_bias, dqn, dkn, dvv, dg, dbeta, *,
                   heads, dk, dv, col_q, col_k, col_v, row_lo, row_hi, name):
    t = proj.shape[0]
    tm = _pick(t, (384, 256, 128))
    nb = t // tm

    def body(pq_ref, pqp_ref, pk_ref, pkp_ref, pv_ref, pvp_ref, ab_ref, cq_ref, ck_ref, cv_ref, al_ref, dt_ref,
             dqn_ref, dkn_ref, dvv_ref, dg_ref, db_ref, dcq_ref, dck_ref, dcv_ref, dab_ref, dal_ref, ddt_ref):
        i, h = pl.program_id(0), pl.program_id(1)
        first = i == 0

        @pl.when((i == 0) & (h == 0))
        def _():
            dal_ref[...] = jnp.zeros_like(dal_ref)
            ddt_ref[...] = jnp.zeros_like(ddt_ref)

        @pl.when(h == 0)
        def _():
            dab_ref[...] = jnp.zeros_like(dab_ref)

        row = i * tm + _iota((tm, 1), 0)
        valid = (row >= row_lo) & (row < row_hi)

        def l2_bwd(c1, dn):
            x1 = _silu(c1)
            r = lax.rsqrt(jnp.sum(x1 * x1, axis=-1, keepdims=True) + L2_EPS)
            dn = jnp.where(valid, dn, 0.0)
            d1 = r * dn - x1 * (r * r * r) * jnp.sum(dn * x1, axis=-1, keepdims=True)
            return d1 * _silu_grad(c1)

        dcq_ref[...] = l2_bwd(_conv_taps(pq_ref[...], pqp_ref[...], cq_ref, first), dqn_ref[...])
        dck_ref[...] = l2_bwd(_conv_taps(pk_ref[...], pkp_ref[...], ck_ref, first), dkn_ref[...])
        cv1 = _conv_taps(pv_ref[...], pvp_ref[...], cv_ref, first)
        dcv_ref[...] = jnp.where(valid, dvv_ref[...], 0.0) * _silu_grad(cv1)
        ab = ab_ref[...]
        da = _lane_pick(ab, h)
        db = _lane_pick(ab, heads + h)
        a = _lane_pick(al_ref[...], h)
        dtb = _lane_pick(dt_ref[...], h)
        dgv = jnp.where(valid, dg_ref[...], 0.0)
        ea = jnp.exp(a)
        g = -ea * _softplus(da + dtb)
        dda = dgv * (-ea) * _sigmoid(da + dtb)
        beta = _sigmoid(db)
        ddb = jnp.where(valid, db_ref[...], 0.0) * beta * (1.0 - beta)
        lane = _iota((tm, LANES), 1)
        dab_ref[...] += jnp.where(lane == h, dda, 0.0) + jnp.where(lane == heads + h, ddb, 0.0)
        lane1 = _iota((1, LANES), 1)
        dal_ref[...] += jnp.where(lane1 == h, jnp.sum(dgv * g, axis=0, keepdims=True), 0.0)
        ddt_ref[...] += jnp.where(lane1 == h, jnp.sum(dda, axis=0, keepdims=True), 0.0)

    def cur(width, col):
        return pl.BlockSpec((tm, width), lambda i, h: (i, col // width + h))

    def prev(width, col):
        return pl.BlockSpec((8, width), lambda i, h: (jnp.maximum(i * (tm // 8) - 1, 0), col // width + h))

    def hd(width):
        return pl.BlockSpec((None, tm, width), lambda i, h: (h, i, 0))

    small = pl.BlockSpec((1, LANES), lambda i, h: (0, 0))
    return pl.pallas_call(
        body, grid=(nb, heads),
        in_specs=[cur(dk, col_q), prev(dk, col_q), cur(dk, col_k), prev(dk, col_k), cur(dv, col_v), prev(dv, col_v),
                  pl.BlockSpec((tm, LANES), lambda i, h: (i, 0)),
                  pl.BlockSpec((cq.shape[0], dk), lambda i, h: (0, h)), pl.BlockSpec((ck.shape[0], dk), lambda i, h: (0, h)),
                  pl.BlockSpec((cv.shape[0], dv), lambda i, h: (0, h)), small, small,
                  hd(dk), hd(dk), hd(dv), hd(1), hd(1)],
        out_specs=[hd(dk), hd(dk), hd(dv), pl.BlockSpec((tm, LANES), lambda i, h: (i, 0)), small, small],
        out_shape=[jax.ShapeDtypeStruct((heads, t, dk), F32), jax.ShapeDtypeStruct((heads, t, dk), F32),
                   jax.ShapeDtypeStruct((heads, t, dv), F32), jax.ShapeDtypeStruct((t, LANES), F32),
                   jax.ShapeDtypeStruct((1, LANES), F32), jax.ShapeDtypeStruct((1, LANES), F32)], name=name,
        compiler_params=_params("arbitrary", "arbitrary"))(
            proj, proj, proj, proj, proj, proj, pab, cq, ck, cv, a_log, dt_bias, dqn, dkn, dvv, dg, dbeta)


def _conv_bwd(proj, dc, cw, *, heads, width, col, name):
    t = proj.shape[0]
    tm = _pick(t, (384, 256, 128))
    nb = t // tm
    nk = cw.shape[0]

    def body(p_ref, pp_ref, d_ref, dn_ref, w_ref, dp_ref, dw_ref):
        i = pl.program_id(1)
        first, last = i == 0, i == nb - 1

        @pl.when(first)
        def _():
            dw_ref[...] = jnp.zeros_like(dw_ref)

        x, d = p_ref[...], d_ref[...]
        dx = d * w_ref[nk - 1:nk, :]
        dw_ref[nk - 1:nk, :] += jnp.sum(d * x, axis=0, keepdims=True)
        for s in range(1, nk):
            dx += _shift_up(d, dn_ref[...], s, last) * w_ref[nk - 1 - s:nk - s, :]
            dw_ref[nk - 1 - s:nk - s, :] += jnp.sum(d * _shift_down(x, pp_ref[...], s, first), axis=0, keepdims=True)
        dp_ref[...] = _bf(dx)

    return pl.pallas_call(
        body, grid=(heads, nb),
        in_specs=[pl.BlockSpec((tm, width), lambda h, i: (i, col // width + h)),
                  pl.BlockSpec((8, width), lambda h, i: (jnp.maximum(i * (tm // 8) - 1, 0), col // width + h)),
                  pl.BlockSpec((None, tm, width), lambda h, i: (h, i, 0)),
                  pl.BlockSpec((None, 8, width), lambda h, i: (h, jnp.minimum((i + 1) * (tm // 8), t // 8 - 1), 0)),
                  pl.BlockSpec((nk, width), lambda h, i: (0, h))],
        out_specs=[pl.BlockSpec((tm, width), lambda h, i: (i, h)), pl.BlockSpec((nk, width), lambda h, i: (0, h))],
        out_shape=[jax.ShapeDtypeStruct((t, heads * width), BF16), jax.ShapeDtypeStruct((nk, heads * width), F32)],
        name=name, compiler_params=_params("parallel", "arbitrary"))(proj, proj, dc, dc, cw)


def _sb_pre(proj, gq, gk, *, heads, dh, col_q, col_k, col_v, name):
    t = proj.shape[0]
    tm = _pick(t, (384, 256, 128))

    def body(q_ref, k_ref, v_ref, gq_ref, gk_ref, qo_ref, ko_ref, vo_ref):
        qo_ref[...] = _bf(_rms_fwd(q_ref[...], gq_ref[...]))
        ko_ref[...] = _bf(_rms_fwd(k_ref[...], gk_ref[...]))
        vo_ref[...] = _bf(v_ref[...])

    def cur(col):
        return pl.BlockSpec((tm, dh), lambda i, h: (i, col // dh + h))

    gspec = pl.BlockSpec((1, dh), lambda i, h: (0, 0))
    ospec = pl.BlockSpec((tm, dh), lambda i, h: (i, h))
    return pl.pallas_call(
        body, grid=(t // tm, heads), in_specs=[cur(col_q), cur(col_k), cur(col_v), gspec, gspec],
        out_specs=[ospec] * 3, out_shape=[jax.ShapeDtypeStruct((t, heads * dh), BF16)] * 3, name=name,
        compiler_params=_params("parallel", "parallel"))(proj, proj, proj, gq, gk)


def _sb_tile(z, i, j, blk, key_lo):
    qpos = i * blk + _iota((blk, blk), 0)
    kpos = j * blk + _iota((blk, blk), 1)
    vis = (kpos < qpos) & (kpos >= key_lo)
    ls = jnp.minimum(z, 0.0) - jnp.log(1.0 + jnp.exp(-jnp.abs(z)))
    return vis, ls, jnp.where(vis, ls - z, 0.0)


def _dot2_r(x, m):
    hi, lo = _split2(x)
    return _dot(hi, m) + _dot(lo, m)


def _suffix_sums(x, later):
    return jnp.concatenate([_dot2_r(x[:, s:], later[s:, s:s + LANES]) for s in range(0, x.shape[1], LANES)], axis=1)


def _prefix_sums(x, earlier):
    return jnp.concatenate([_dot2_r(x[:, :s + LANES], earlier[:s + LANES, s:s + LANES])
                            for s in range(0, x.shape[1], LANES)], axis=1)


def _sb_fwd(qs, ks, vs, *, heads, dh, key_lo, name, gather=None):
    t = qs.shape[0]
    blk = _pick(t, (3 * SB_BLOCK, 2 * SB_BLOCK, SB_BLOCK))
    nq = t // blk
    assert nq <= LANES
    scale = dh ** -0.5
    hp = SB_HEADS_PER_STEP

    def body(q_ref, k_ref, v_ref, o_ref, c_ref):
        i = pl.program_id(1)
        later = jnp.where(_iota((blk, blk), 0) > _iota((blk, blk), 1), 1.0, 0.0).astype(BF16)
        lane = _iota((blk, LANES), 1)
        c_ref[...] = jnp.zeros_like(c_ref)

        def step(n, carry):
            j = i - n
            rows = pl.ds(pl.multiple_of(j * blk, blk), blk)
            hs = range(hp)
            cols = [pl.ds(hh * dh, dh) for hh in hs]
            zs = [_dot_nt(q_ref[:, cols[hh]], k_ref[rows, cols[hh]]) * scale for hh in hs]
            tiles = [_sb_tile(z, i, j, blk, key_lo) for z in zs]
            sufs = [_suffix_sums(lk, later) for _, _, lk in tiles]
            wgts = [jnp.where(vis, jnp.exp(ls + suf + carry[2 * hh + 1]), 0.0)
                    for hh, ((vis, ls, _), suf) in enumerate(zip(tiles, sufs))]
            accs = [carry[2 * hh] + _dot(_bf(wgts[hh]), v_ref[rows, cols[hh]]) for hh in hs]
            out = []
            for hh in hs:
                c_ref[hh] = jnp.where(lane == j, carry[2 * hh + 1], c_ref[hh])
                out += [accs[hh], carry[2 * hh + 1] + jnp.sum(tiles[hh][2], axis=1, keepdims=True)]
            return tuple(out)

        res = lax.fori_loop(0, i + 1, step, (jnp.zeros((blk, dh), F32), jnp.zeros((blk, 1), F32)) * hp)
        for hh in range(hp):
            o_ref[:, pl.ds(hh * dh, dh)] = _bf(res[2 * hh])

    full = pl.BlockSpec((t, hp * dh), lambda h, i: (0, h))
    call = dict(grid=(heads // hp, nq),
                in_specs=[pl.BlockSpec((blk, hp * dh), lambda h, i: (i, h)), full, full],
                out_specs=[pl.BlockSpec((blk, hp * dh), lambda h, i: (i, h)),
                           pl.BlockSpec((hp, blk, LANES), lambda h, i: (h, i, 0))],
                out_shape=[jax.ShapeDtypeStruct((t, heads * dh), BF16), jax.ShapeDtypeStruct((heads, t, LANES), F32)],
                name=name)
    if gather is None:
        return pl.pallas_call(body, compiler_params=_params("parallel", "parallel"), **call)(qs, ks, vs)
    res = _call_with_exchange(body, scratch_shapes=[], args=(qs, ks, vs), srcs=gather, scatter=[False] * len(gather),
                              **call)
    return [*res[:2], list(res[2:])]


def _sb_bwd(qs, ks, vs, do, carry, *, heads, dh, key_lo, name, scatter=None):
    t = qs.shape[0]
    blk = _pick(t, (3 * SB_BLOCK, 2 * SB_BLOCK, SB_BLOCK))
    nq = t // blk
    scale = dh ** -0.5
    hp = SB_HEADS_PER_STEP

    def body(q_ref, k_ref, v_ref, do_ref, c_ref, dq_ref, dk_ref, dv_ref):
        i = pl.program_id(1)

        @pl.when(i == 0)
        def _():
            dk_ref[...] = jnp.zeros_like(dk_ref)
            dv_ref[...] = jnp.zeros_like(dv_ref)

        r0 = _iota((blk, blk), 0)
        r1 = _iota((blk, blk), 1)
        later = jnp.where(r0 > r1, 1.0, 0.0).astype(BF16)
        earlier = jnp.where(r0 < r1, 1.0, 0.0).astype(BF16)

        def step(j, carry):
            rows = pl.ds(pl.multiple_of(j * blk, blk), blk)
            hs = range(hp)
            cols = [pl.ds(hh * dh, dh) for hh in hs]
            zs = [_dot_nt(q_ref[:, cols[hh]], k_ref[rows, cols[hh]]) * scale for hh in hs]
            dws = [_dot_nt(do_ref[:, cols[hh]], v_ref[rows, cols[hh]]) for hh in hs]
            tiles = [_sb_tile(z, i, j, blk, key_lo) for z in zs]
            sufs = [_suffix_sums(lk, later) for _, _, lk in tiles]
            wgts = [jnp.where(vis, jnp.exp(ls + suf + _lane_pick(c_ref[hh], j)), 0.0)
                    for hh, ((vis, ls, _), suf) in enumerate(zip(tiles, sufs))]
            es = [wgt * dw for wgt, dw in zip(wgts, dws)]
            pres = [_prefix_sums(e, earlier) for e in es]
            dzs = []
            for hh in hs:
                vis, ls, _ = tiles[hh]
                before = jnp.where(vis, pres[hh] + carry[2 * hh + 1], 0.0)
                sig = jnp.exp(ls)
                dzs.append(_bf((es[hh] * (1.0 - sig) - before * sig) * scale))
            dks = [_dot_tn(dzs[hh], q_ref[:, cols[hh]]) for hh in hs]
            dvs = [_dot_tn(_bf(wgts[hh]), do_ref[:, cols[hh]]) for hh in hs]
            dqs = [carry[2 * hh] + _dot(dzs[hh], k_ref[rows, cols[hh]]) for hh in hs]
            out = []
            for hh in hs:
                dk_ref[rows, cols[hh]] += dks[hh]
                dv_ref[rows, cols[hh]] += dvs[hh]
                out += [dqs[hh], carry[2 * hh + 1] + jnp.sum(es[hh], axis=1, keepdims=True)]
            return tuple(out)

        res = lax.fori_loop(0, i + 1, step, (jnp.zeros((blk, dh), F32), jnp.zeros((blk, 1), F32)) * hp)
        for hh in range(hp):
            dq_ref[:, pl.ds(hh * dh, dh)] = res[2 * hh]

    full = pl.BlockSpec((t, hp * dh), lambda h, i: (0, h))
    qblk = pl.BlockSpec((blk, hp * dh), lambda h, i: (i, h))
    call = dict(grid=(heads // hp, nq),
                in_specs=[qblk, full, full, qblk, pl.BlockSpec((hp, blk, LANES), lambda h, i: (h, i, 0))],
                out_specs=[qblk, full, full], out_shape=[jax.ShapeDtypeStruct((t, heads * dh), F32)] * 3, name=name)
    if scatter is None:
        return pl.pallas_call(body, compiler_params=_params("parallel", "arbitrary"), **call)(qs, ks, vs, do, carry)
    res = _call_with_exchange(body, scratch_shapes=[], args=(qs, ks, vs, do, carry), srcs=scatter,
                              scatter=[True] * len(scatter), **call)
    return [*res[:3], list(res[3:])]


def _sb_pre_bwd(proj, gq, gk, dq, dk, dv, *, heads, dh, col_q, col_k, name):
    t = proj.shape[0]
    tm = _pick(t, (384, 256, 128))

    def body(q_ref, k_ref, gq_ref, gk_ref, dq_ref, dk_ref, dv_ref, oq_ref, ok_ref, ov_ref, dgq_ref, dgk_ref):
        @pl.when((pl.program_id(0) == 0) & (pl.program_id(1) == 0))
        def _():
            dgq_ref[...] = jnp.zeros_like(dgq_ref)
            dgk_ref[...] = jnp.zeros_like(dgk_ref)

        dq_, gq_r = _rms_bwd(q_ref[...], gq_ref[...], dq_ref[...])
        dk_, gk_r = _rms_bwd(k_ref[...], gk_ref[...], dk_ref[...])
        oq_ref[...] = _bf(dq_)
        ok_ref[...] = _bf(dk_)
        ov_ref[...] = _bf(dv_ref[...])
        dgq_ref[...] += jnp.sum(gq_r, axis=0, keepdims=True)
        dgk_ref[...] += jnp.sum(gk_r, axis=0, keepdims=True)

    def cur(col):
        return pl.BlockSpec((tm, dh), lambda i, h: (i, col // dh + h))

    gspec = pl.BlockSpec((1, dh), lambda i, h: (0, 0))
    ospec = pl.BlockSpec((tm, dh), lambda i, h: (i, h))
    return pl.pallas_call(
        body, grid=(t // tm, heads), in_specs=[cur(col_q), cur(col_k), gspec, gspec, ospec, ospec, ospec],
        out_specs=[ospec, ospec, ospec, gspec, gspec],
        out_shape=[jax.ShapeDtypeStruct((t, heads * dh), BF16)] * 3 + [jax.ShapeDtypeStruct((1, dh), F32)] * 2,
        name=name, compiler_params=_params("arbitrary", "arbitrary"))(proj, proj, gq, gk, dq, dk, dv)


PEERS = N_DEV - 1


def _exchange_copies(ins, outs, send_sems, recv_sems, local_sems, scatter):
    x, y, c = lax.axis_index("x"), lax.axis_index("y"), lax.axis_index("c")
    me = 4 * x + 2 * y + c
    copies = []
    for a in range(len(ins)):
        own = ins[a].at[me] if scatter[a] else ins[a]
        copies.append(pltpu.make_async_copy(own, outs[a].at[me], local_sems.at[a]))
        for k in range(1, N_DEV):
            px = (x + (k >> 2 & 1)) % 2
            py = (y + (k >> 1 & 1)) % 2
            pc = (c + (k & 1)) % 2
            src = ins[a].at[4 * px + 2 * py + pc] if scatter[a] else ins[a]
            copies.append(pltpu.make_async_remote_copy(
                src_ref=src, dst_ref=outs[a].at[me], send_sem=send_sems.at[a * PEERS + k - 1],
                recv_sem=recv_sems.at[a * PEERS + k - 1], device_id=(px, py, pc), device_id_type=MESH))
    return copies


def _exchange_shapes(srcs, scatter):
    return [jax.ShapeDtypeStruct(s.shape if sc else (N_DEV,) + s.shape, s.dtype) for s, sc in zip(srcs, scatter)]


def _exchange_sems(n):
    return [pltpu.SemaphoreType.DMA((n * PEERS,)), pltpu.SemaphoreType.DMA((n * PEERS,)), pltpu.SemaphoreType.DMA((n,))]


def _exchange(srcs, *, scatter, name):
    n = len(srcs)

    def body(*refs):
        copies = _exchange_copies(refs[:n], refs[n:2 * n], *refs[2 * n:], scatter)
        for cp in copies:
            cp.start()
        for cp in copies:
            cp.wait()

    any_spec = pl.BlockSpec(memory_space=pl.ANY)
    return pl.pallas_call(
        body, in_specs=[any_spec] * n, out_specs=[any_spec] * n, out_shape=_exchange_shapes(srcs, scatter),
        scratch_shapes=_exchange_sems(n), name=name,
        compiler_params=pltpu.CompilerParams(has_side_effects=True))(*srcs)


def _gather_two_level(srcs, *, name):
    n = len(srcs)

    def body(*refs):
        ins, outs = refs[:n], refs[n:2 * n]
        send_sems, recv_sems, local_sems = refs[2 * n:]
        x, y, c = lax.axis_index("x"), lax.axis_index("y"), lax.axis_index("c")
        chips = [(1 - x, y), (x, 1 - y), (1 - x, 1 - y)]

        def slab(a, px, py, pc):
            return outs[a].at[4 * px + 2 * py + pc]

        def copy(a, k, block, to, src=None):
            return pltpu.make_async_remote_copy(
                src_ref=slab(a, *block) if src is None else src, dst_ref=slab(a, *block),
                send_sem=send_sems.at[a * PEERS + k], recv_sem=recv_sems.at[a * PEERS + k],
                device_id=to, device_id_type=MESH)

        mine = [pltpu.make_async_copy(ins[a], slab(a, x, y, c), local_sems.at[a]) for a in range(n)]
        first = [copy(a, 0, (x, y, c), (x, y, 1 - c), src=ins[a]) for a in range(n)]
        first += [copy(a, 1 + j, (x, y, c), (*chip, c), src=ins[a]) for j, chip in enumerate(chips) for a in range(n)]
        for cp in mine + first:
            cp.start()
        passed = []
        for j, chip in enumerate(chips):
            for a in range(n):
                copy(a, 1 + j, (*chip, c), (x, y, c)).wait_recv()
                passed.append(copy(a, 4 + j, (*chip, c), (x, y, 1 - c)))
                passed[-1].start()
        for a in range(n):
            copy(a, 0, (x, y, 1 - c), (x, y, c)).wait_recv()
            for j, chip in enumerate(chips):
                copy(a, 4 + j, (*chip, 1 - c), (x, y, c)).wait_recv()
        for cp in first + passed:
            cp.wait_send()
        for cp in mine:
            cp.wait()

    any_spec = pl.BlockSpec(memory_space=pl.ANY)
    return pl.pallas_call(
        body, in_specs=[any_spec] * n, out_specs=[any_spec] * n, out_shape=_exchange_shapes(srcs, [False] * n),
        scratch_shapes=_exchange_sems(n), name=name,
        compiler_params=pltpu.CompilerParams(has_side_effects=True))(*srcs)


def _call_with_exchange(body, *, grid, in_specs, out_specs, out_shape, scratch_shapes, args, srcs, scatter, name):
    n, n_in, n_out, n_scr = len(srcs), len(args), len(out_shape), len(scratch_shapes)

    def full_body(*refs):
        ins, xin = refs[:n_in], refs[n_in:n_in + n]
        outs, xout = refs[n_in + n:n_in + n + n_out], refs[n_in + n + n_out:n_in + 2 * n + n_out]
        scr = refs[n_in + 2 * n + n_out:]
        ids = [pl.program_id(a) for a in range(len(grid))]
        first = functools.reduce(jnp.logical_and, [i == 0 for i in ids])
        last = functools.reduce(jnp.logical_and, [i == g - 1 for i, g in zip(ids, grid)])
        copies = _exchange_copies(xin, xout, *scr[n_scr:], scatter)

        @pl.when(first)
        def _():
            for cp in copies:
                cp.start()

        body(*ins, *outs, *scr[:n_scr])

        @pl.when(last)
        def _():
            for cp in copies:
                cp.wait()

    any_spec = pl.BlockSpec(memory_space=pl.ANY)
    return pl.pallas_call(
        full_body, grid=grid, in_specs=list(in_specs) + [any_spec] * n, out_specs=list(out_specs) + [any_spec] * n,
        out_shape=list(out_shape) + _exchange_shapes(srcs, scatter),
        scratch_shapes=list(scratch_shapes) + _exchange_sems(n), name=name,
        compiler_params=pltpu.CompilerParams(dimension_semantics=("arbitrary",) * len(grid),
                                             vmem_limit_bytes=V7X_VMEM_LIMIT_BYTES, has_side_effects=True))(*args, *srcs)


def _adam_math(g, w, m, v):
    m2 = ADAM_B1 * m + (1.0 - ADAM_B1) * g
    v2 = ADAM_B2 * v + (1.0 - ADAM_B2) * (g * g)
    m_hat = m2 / (1.0 - ADAM_B1 ** ADAM_STEP)
    v_hat = v2 / (1.0 - ADAM_B2 ** ADAM_STEP)
    return -ADAM_LR * (m_hat / (jnp.sqrt(v_hat) + ADAM_EPS) + ADAM_WD * w), m2, v2


def _adamw_slabs(slabs, w, m, v, *, name):
    r, c = w.shape
    tr = next((t for t in (2256, 752, 512, 240, 128, 64, 32, 16) if r % t == 0), r)

    def body(s_ref, w_ref, m_ref, v_ref, g_ref, d_ref, mo_ref, vo_ref):
        g = s_ref[0].astype(F32)
        for p in range(1, N_DEV):
            g = g + s_ref[p].astype(F32)
        g_ref[...] = g
        d_ref[...], mo_ref[...], vo_ref[...] = _adam_math(g, w_ref[...], m_ref[...], v_ref[...])

    spec = pl.BlockSpec((tr, c), lambda i: (i, 0))
    return pl.pallas_call(
        body, grid=(r // tr,), in_specs=[pl.BlockSpec((N_DEV, tr, c), lambda i: (0, i, 0)), spec, spec, spec],
        out_specs=[spec] * 4, out_shape=[jax.ShapeDtypeStruct((r, c), F32)] * 4, name=name,
        compiler_params=_params("parallel"))(slabs, w, m, v)


def _adamw_small(g, w, m, v, *, name):
    def body(g_ref, w_ref, m_ref, v_ref, d_ref, mo_ref, vo_ref):
        d_ref[...], mo_ref[...], vo_ref[...] = _adam_math(g_ref[...], w_ref[...], m_ref[...], v_ref[...])

    return pl.pallas_call(body, out_shape=[jax.ShapeDtypeStruct(w.shape, F32)] * 3, name=name)(g, w, m, v)


def _sum_slabs(slabs, *, name):
    def body(s_ref, o_ref):
        acc = s_ref[0]
        for p in range(1, N_DEV):
            acc = acc + s_ref[p]
        o_ref[...] = acc

    return pl.pallas_call(body, out_shape=jax.ShapeDtypeStruct(slabs.shape[1:], F32), name=name)(slabs)


def _pad_lanes(a):
    return jnp.pad(a, ((0, 0), (0, LANES - a.shape[1])))


def _w_in_slabs(main, ab, col_ab, n_ab):
    pw = main.shape[0] + n_ab
    c = pw // N_DEV
    parts = [(0, col_ab, main, 0), (col_ab, col_ab + n_ab, ab, col_ab), (col_ab + n_ab, pw, main, n_ab)]
    slabs = []
    for p in range(N_DEV):
        pieces = []
        for lo, hi, src, shift in parts:
            a, b = max(lo, c * p), min(hi, c * (p + 1))
            if a < b:
                pieces.append(src[a - shift:b - shift])
        slabs.append(jnp.concatenate(pieces, axis=0))
    return jnp.stack(slabs)


def _local_step(x, target, meta, g_mix, wt_main, wt_ab, cq, ck, cv, a_log, dt_bias, g_dn, g_sbq, g_sbk, g_ffn, rest,
                shards=False):
    seq, d = x.shape
    n_meta = meta.shape[0]
    heads = a_log.shape[1]
    qk = cq.shape[1]
    dvt = cv.shape[1]
    dk, dv = qk // heads, dvt // heads
    dh = g_sbq.shape[1]
    sbw = rest[2].shape[0] * N_DEV if shards else rest[1].shape[0]
    sb_heads = sbw // dh
    pad_l = (-n_meta) % CHUNK
    row_x = pad_l + n_meta
    rows = row_x + seq
    t = -(-rows // GDN_ROWS) * GDN_ROWS
    col_q, col_k, col_v, col_z = 0, qk, 2 * qk, 2 * qk + dvt
    col_sq = 2 * qk + 2 * dvt
    col_sk, col_sv, col_gd, col_gs = col_sq + sbw, col_sq + 2 * sbw, col_sq + 3 * sbw, col_sq + 3 * sbw + d

    def rows_pad(a):
        return jnp.concatenate([jnp.zeros((row_x, d), F32), a, jnp.zeros((t - rows, d), F32)], axis=0)

    h0 = jnp.concatenate([jnp.zeros((pad_l, d), F32), meta, x, jnp.zeros((t - rows, d), F32)], axis=0)
    tgt = rows_pad(target)
    a_log_p, dt_p = _pad_lanes(a_log), _pad_lanes(dt_bias)

    proj, n1 = _mm_norm(h0, g_mix, wt_main, name="proj")
    pab = _mm_nt(n1, wt_ab, out_dtype=F32, name="proj_ab")
    gk = dict(heads=heads, dk=dk, dv=dv, col_q=col_q, col_k=col_k, col_v=col_v, row_lo=pad_l, row_hi=rows)
    qn, kn, vv, g, beta = _gdn_pre(proj, pab, cq, ck, cv, a_log_p, dt_p, name="gdn_pre", **gk)
    u, w, pm, qd, kd, egl, tinv = _gdn_prep(qn, kn, vv, g, beta, name="gdn_prep")
    o_raw, o_dn, states = _gdn_scan(u, w, pm, qd, kd, egl, proj, g_dn, col_z=col_z, name="gdn_scan")
    qs, ks, vs = _sb_pre(proj, g_sbq, g_sbk, heads=sb_heads, dh=dh, col_q=col_sq, col_k=col_sk, col_v=col_sv,
                         name="sb_pre")
    if shards:
        o_sb, carry, (g_fi, g_bd, g_bs, g_out, g_fo) = _sb_fwd(qs, ks, vs, heads=sb_heads, dh=dh, key_lo=pad_l,
                                                                name="sb_fwd", gather=list(rest))
        wt_fi = g_fi.reshape(-1, d)
        d_ff = wt_fi.shape[0] // 2
        w_bd, w_bs, w_out, wt_fg, wt_fu, w_fo = (g_bd.reshape(-1, d), g_bs.reshape(-1, d), g_out.reshape(-1, d),
                                                 wt_fi[:d_ff], wt_fi[d_ff:], g_fo.reshape(-1, d))
    else:
        o_sb, carry = _sb_fwd(qs, ks, vs, heads=sb_heads, dh=dh, key_lo=pad_l, name="sb_fwd")
        w_bd, w_bs, w_out, wt_fg, wt_fu, w_fo = rest
    merged, br_dn, br_sb = _merge_fwd(o_dn, o_sb, w_bd, w_bs, proj, col_gd=col_gd, col_gs=col_gs, name="merge")
    h1 = _mm_res(h0, merged, w_out, name="mix_out")
    gate, up, act, n2 = _mm_norm_swiglu(h1, g_ffn, wt_fg, wt_fu, name="ffn_in")
    dy, dyb, lsum = _mm_res_loss(h1, act, w_fo, tgt, row0=row_x, nrows=seq, name="ffn_out_loss")

    dgate, dup = _swiglu_bwd(dyb, w_fo, gate, up, name="ffn_out_bwd")
    d_w_fo = _mm_tn(act, dyb, name="dw_ffn_out")
    d_wt_fg = _mm_tn(dgate, n2, name="dw_ffn_gate")
    d_wt_fu = _mm_tn(dup, n2, name="dw_ffn_up")
    dh1, dh1b, d_g_ffn = _mm_rmsbwd([(dgate, wt_fg), (dup, wt_fu)], None, h1, g_ffn, dy, name="ffn_in_bwd")

    dbd, dbs, dgd, dgs = _merge_bwd(dh1b, w_out, proj, br_dn, br_sb, col_gd=col_gd, col_gs=col_gs, name="mix_out_bwd")
    d_w_out = _mm_tn(merged, dh1b, name="dw_out")
    d_w_bd = _mm_tn(o_dn, dbd, name="dw_branch_dn")
    d_w_bs = _mm_tn(o_sb, dbs, name="dw_branch_sb")
    do_dn = _mm_nt(dbd, w_bd, out_dtype=F32, name="branch_dn_bwd")
    do_sb = _mm_nt(dbs, w_bs, out_dtype=BF16, name="branch_sb_bwd")

    do_raw, dz, d_g_dn = _gdn_post_bwd(o_raw, proj, g_dn, do_dn, col_z=col_z, name="gdn_post_bwd")
    du, dw, dp, dqd, dkd, dgl = _gdn_bwd_scan(u, w, pm, qd, kd, egl, states, do_raw, name="gdn_bwd_scan")
    dqn, dkn, dvv, dg, dbeta = _gdn_bwd_prep(qn, kn, vv, g, beta, tinv, u, w, du, dw, dp, dqd, dkd, dgl,
                                            name="gdn_bwd_prep")
    dcq, dck, dcv, dpab, d_a_log, d_dt = _gdn_pre_bwd_a(proj, pab, cq, ck, cv, a_log_p, dt_p, dqn, dkn, dvv, dg, dbeta,
                                                        name="gdn_pre_bwd", **gk)
    dpq, d_cq = _conv_bwd(proj, dcq, cq, heads=heads, width=dk, col=col_q, name="conv_q_bwd")
    dpk, d_ck = _conv_bwd(proj, dck, ck, heads=heads, width=dk, col=col_k, name="conv_k_bwd")
    dpv, d_cv = _conv_bwd(proj, dcv, cv, heads=heads, width=dv, col=col_v, name="conv_v_bwd")

    early = None
    if shards:
        slabs = [_bf(jnp.concatenate([d_wt_fg, d_wt_fu], axis=0)).reshape(N_DEV, -1, LANES)]
        slabs += [_bf(a).reshape(N_DEV, -1, d) for a in (d_w_bd, d_w_bs, d_w_out, d_w_fo)]
        dqs, dks, dvs, early = _sb_bwd(qs, ks, vs, do_sb, carry, heads=sb_heads, dh=dh, key_lo=pad_l, name="sb_bwd",
                                       scatter=slabs)
    else:
        dqs, dks, dvs = _sb_bwd(qs, ks, vs, do_sb, carry, heads=sb_heads, dh=dh, key_lo=pad_l, name="sb_bwd")
    dsq, dsk, dsv, d_g_sbq, d_g_sbk = _sb_pre_bwd(proj, g_sbq, g_sbk, dqs, dks, dvs, heads=sb_heads, dh=dh,
                                                   col_q=col_sq, col_k=col_sk, name="sb_pre_bwd")

    dproj = jnp.concatenate([dpq, dpk, dpv, dz, dsq, dsk, dsv, dgd, dgs], axis=1)
    dpab_b = _bf(dpab)
    d_wt_main = _mm_tn(dproj, n1, name="dw_in_main")
    d_wt_ab = _mm_tn(dpab_b, n1, name="dw_in_ab")
    s_in = None
    if shards:
        slabs = _w_in_slabs(_bf(d_wt_main), _bf(d_wt_ab), col_sq, 2 * heads).reshape(N_DEV, -1, LANES)
        dh0, _, d_g_mix, (s_in,) = _mm_rmsbwd([(dproj, wt_main)], (dpab_b, wt_ab), h0, g_mix, dh1, name="proj_bwd",
                                              scatter=[slabs])
    else:
        dh0, _, d_g_mix = _mm_rmsbwd([(dproj, wt_main)], (dpab_b, wt_ab), h0, g_mix, dh1, name="proj_bwd")

    return dict(s_in=s_in, lsum=lsum, grad_x=dh0[row_x:rows], d_meta=dh0[pad_l:row_x], d_g_mix=d_g_mix,
                d_wt_main=d_wt_main, d_wt_ab=d_wt_ab, d_cq=d_cq, d_ck=d_ck, d_cv=d_cv, d_a_log=d_a_log[:, :heads],
                d_dt=d_dt[:, :heads], d_g_dn=d_g_dn, d_g_sbq=d_g_sbq, d_g_sbk=d_g_sbk, d_w_bd=d_w_bd, d_w_bs=d_w_bs,
                d_w_out=d_w_out, d_g_ffn=d_g_ffn, d_wt_fg=d_wt_fg, d_wt_fu=d_wt_fu, d_w_fo=d_w_fo, early=early)


def _pack(parts):
    flat = []
    for a in parts:
        a = a.reshape(-1)
        flat.append(jnp.pad(a, (0, (-a.shape[0]) % LANES)))
    v = jnp.concatenate(flat)
    v = jnp.pad(v, (0, (-v.shape[0]) % (8 * LANES)))
    return v.reshape(-1, LANES)


def _unpack(packed, shapes):
    flat = packed.reshape(-1)
    out, pos = [], 0
    for s in shapes:
        n = math.prod(s)
        out.append(flat[pos:pos + n].reshape(s))
        pos += n + (-n) % LANES
    return out


def kernel(x, meta_tokens, norm_mix_gain, w_in, conv_q, conv_k, conv_v, dn_a_log, dn_dt_bias, dn_out_norm_gain, sb_q_norm_gain, sb_k_norm_gain, w_branch_dn, w_branch_sb, w_out, norm_ffn_gain, w_ffn_in, w_ffn_out, loss_target, m_meta_tokens, m_norm_mix_gain, m_w_in, m_conv_q, m_conv_k, m_conv_v, m_dn_a_log, m_dn_dt_bias, m_dn_out_norm_gain, m_sb_q_norm_gain, m_sb_k_norm_gain, m_w_branch_dn, m_w_branch_sb, m_w_out, m_norm_ffn_gain, m_w_ffn_in, m_w_ffn_out, v_meta_tokens, v_norm_mix_gain, v_w_in, v_conv_q, v_conv_k, v_conv_v, v_dn_a_log, v_dn_dt_bias, v_dn_out_norm_gain, v_sb_q_norm_gain, v_sb_k_norm_gain, v_w_branch_dn, v_w_branch_sb, v_w_out, v_norm_ffn_gain, v_w_ffn_in, v_w_ffn_out):
    me = 4 * lax.axis_index("x") + 2 * lax.axis_index("y") + lax.axis_index("c")
    heads = dn_a_log.shape[1]
    d = x.shape[2]
    qk = conv_q.shape[2] * N_DEV
    dvt = conv_v.shape[2] * N_DEV
    col_ab = 2 * qk + 2 * dvt

    small_shapes = [meta_tokens.shape, conv_q.shape[1:], conv_k.shape[1:], conv_v.shape[1:]]
    small = _pack([meta_tokens, conv_q[0], conv_k[0], conv_v[0]])
    def features_major(a):
        return jnp.transpose(a, (2, 0, 1)).reshape(a.shape[2], a.shape[1])

    g_in, g_small = _gather_two_level([_bf(features_major(w_in)), small], name="gather_w_in")
    wt_full = g_in.reshape(-1, d)
    wt_main = jnp.concatenate([wt_full[:col_ab], wt_full[col_ab + 2 * heads:]], axis=0)
    wt_ab = jnp.pad(wt_full[col_ab:col_ab + 2 * heads], ((0, LANES - 2 * heads), (0, 0)))
    parts = [_unpack(g_small[p], small_shapes) for p in range(N_DEV)]
    meta_f, cq_f, ck_f, cv_f = (jnp.concatenate([parts[p][a] for p in range(N_DEV)], axis=1) for a in range(4))

    r = _local_step(x[0], loss_target[0], meta_f, norm_mix_gain, wt_main, wt_ab, cq_f, ck_f, cv_f, dn_a_log, dn_dt_bias,
                    dn_out_norm_gain, sb_q_norm_gain, sb_k_norm_gain, norm_ffn_gain,
                    (_bf(features_major(w_ffn_in)), _bf(w_branch_dn[0]), _bf(w_branch_sb[0]), _bf(w_out[0]),
                     _bf(w_ffn_out[0])), shards=True)
    s_fi, s_bd, s_bs, s_out, s_fo = r["early"]
    s_in = r["s_in"]

    loss_part = (0.5 / d) * jnp.sum(r["lsum"], axis=1, keepdims=True)
    small_g = [r["d_meta"], r["d_g_mix"], r["d_cq"], r["d_ck"], r["d_cv"], r["d_a_log"], r["d_dt"], r["d_g_dn"],
               r["d_g_sbq"], r["d_g_sbk"], r["d_g_ffn"], loss_part]
    (g_packs,) = _exchange([_pack(small_g)], scatter=[False], name="gather_small_grads")
    (g_meta, g_mix, g_cq, g_ck, g_cv, g_al, g_dt, g_gdn, g_sbq, g_sbk, g_ffn, loss) = _unpack(
        _sum_slabs(g_packs, name="sum_small_grads"), [a.shape for a in small_g])

    def mine(a, width):
        return lax.dynamic_slice_in_dim(a, me * width, width, axis=1)

    big = dict(w_in=(s_in, w_in, m_w_in, v_w_in), w_branch_dn=(s_bd, w_branch_dn, m_w_branch_dn, v_w_branch_dn),
               w_branch_sb=(s_bs, w_branch_sb, m_w_branch_sb, v_w_branch_sb), w_out=(s_out, w_out, m_w_out, v_w_out),
               w_ffn_in=(s_fi, w_ffn_in, m_w_ffn_in, v_w_ffn_in), w_ffn_out=(s_fo, w_ffn_out, m_w_ffn_out, v_w_ffn_out))
    tiny = dict(meta_tokens=(mine(g_meta, d // N_DEV), meta_tokens, m_meta_tokens, v_meta_tokens),
                norm_mix_gain=(g_mix, norm_mix_gain, m_norm_mix_gain, v_norm_mix_gain),
                conv_q=(mine(g_cq, qk // N_DEV), conv_q[0], m_conv_q[0], v_conv_q[0]),
                conv_k=(mine(g_ck, qk // N_DEV), conv_k[0], m_conv_k[0], v_conv_k[0]),
                conv_v=(mine(g_cv, dvt // N_DEV), conv_v[0], m_conv_v[0], v_conv_v[0]),
                dn_a_log=(g_al, dn_a_log, m_dn_a_log, v_dn_a_log), dn_dt_bias=(g_dt, dn_dt_bias, m_dn_dt_bias, v_dn_dt_bias),
                dn_out_norm_gain=(g_gdn, dn_out_norm_gain, m_dn_out_norm_gain, v_dn_out_norm_gain),
                sb_q_norm_gain=(g_sbq, sb_q_norm_gain, m_sb_q_norm_gain, v_sb_q_norm_gain),
                sb_k_norm_gain=(g_sbk, sb_k_norm_gain, m_sb_k_norm_gain, v_sb_k_norm_gain),
                norm_ffn_gain=(g_ffn, norm_ffn_gain, m_norm_ffn_gain, v_norm_ffn_gain))
    order = ["meta_tokens", "norm_mix_gain", "w_in", "conv_q", "conv_k", "conv_v", "dn_a_log", "dn_dt_bias",
             "dn_out_norm_gain", "sb_q_norm_gain", "sb_k_norm_gain", "w_branch_dn", "w_branch_sb", "w_out",
             "norm_ffn_gain", "w_ffn_in", "w_ffn_out"]
    grads, deltas, new_m, new_v = [], [], [], []
    for name in order:
        if name in ("w_in", "w_ffn_in"):
            slabs, w, m, v = big[name]
            res = _adamw_slabs(slabs, *(features_major(a).reshape(-1, LANES) for a in (w, m, v)), name="adamw_" + name)
            g, dl, mo, vo = (jnp.transpose(a.reshape(w.shape[2], 1, w.shape[1]), (1, 2, 0)) for a in res)
            like = w.shape
        elif name in big:
            slabs, w, m, v = big[name]
            g, dl, mo, vo = _adamw_slabs(slabs, w[0], m[0], v[0], name="adamw_" + name)
            like = w.shape
        else:
            g, w, m, v = tiny[name]
            like = dict(conv_q=conv_q, conv_k=conv_k, conv_v=conv_v).get(name, w).shape
            dl, mo, vo = _adamw_small(g, w, m, v, name="adamw_" + name)
        for lst, a in ((grads, g), (deltas, dl), (new_m, mo), (new_v, vo)):
            lst.append(a.reshape(like))
    return (loss.reshape(()), r["grad_x"][None], *grads, *deltas, *new_m, *new_v)
```

```python
import functools
import math

import jax
import jax.numpy as jnp
from jax import lax
from jax.experimental import pallas as pl
from jax.experimental.pallas import tpu as pltpu

F32 = jnp.float32
BF16 = jnp.bfloat16

N_DEV = 8
CHUNK = 64
CHUNK_SHIFT = 6
GDN_ROWS = 2 * CHUNK
SB_BLOCK = 128
SB_HEADS_PER_STEP = 2
LANES = 128
RMS_EPS = 1e-6
L2_EPS = 1e-6
ADAM_LR = 0.001
ADAM_B1 = 0.9
ADAM_B2 = 0.999
ADAM_EPS = 1e-08
ADAM_WD = 0.01
ADAM_STEP = 10
V7X_VMEM_LIMIT_BYTES = 56 * 1024 * 1024
MM_TN_OUT_BLOCK_BYTES = 6 * 1024 * 1024
ROWS_BIG = (1056, 512, 384, 256, 128)
ROWS_MID = (528, 384, 256, 128)
SCAN_HEADS = 4

MESH = pl.DeviceIdType.MESH


def _params(*sem):
    return pltpu.CompilerParams(dimension_semantics=sem or None, vmem_limit_bytes=V7X_VMEM_LIMIT_BYTES)


def _pick(n, cands):
    for c in cands:
        if n % c == 0:
            return c
    raise ValueError(f"no block size among {cands} divides {n}")


def _bf(x):
    return x.astype(BF16)


def _dot(a, b):
    return jnp.dot(a, b, preferred_element_type=F32)


def _dot_nt(a, b):
    return lax.dot_general(a, b, (((1,), (1,)), ((), ())), preferred_element_type=F32)


def _dot_tn(a, b):
    return lax.dot_general(a, b, (((0,), (0,)), ((), ())), preferred_element_type=F32)


def _split2(x):
    hi = _bf(x)
    return hi, _bf(x - hi.astype(F32))


def _split3(x):
    hi = _bf(x)
    r = x - hi.astype(F32)
    mid = _bf(r)
    return hi, mid, _bf(r - mid.astype(F32))


def _dot_hp(a, b, dot=_dot):
    ah, al = _split2(a)
    bh, bl = _split2(b)
    return dot(ah, bh) + dot(ah, bl) + dot(al, bh)


def _dot_exact_l(m, x, dot=_dot):
    h, mi, lo = _split3(x)
    return dot(m, h) + dot(m, mi) + dot(m, lo)


def _sigmoid(x):
    return 1.0 / (1.0 + jnp.exp(-x))


def _silu(x):
    return x * _sigmoid(x)


def _silu_grad(x):
    s = _sigmoid(x)
    return s * (1.0 + x * (1.0 - s))


def _softplus(x):
    return jnp.maximum(x, 0.0) + jnp.log(1.0 + jnp.exp(-jnp.abs(x)))


def _rms_fwd(h, gain):
    r = lax.rsqrt(jnp.mean(h * h, axis=-1, keepdims=True) + RMS_EPS)
    return h * r * gain


def _rms_bwd(h, gain, dy):
    r = lax.rsqrt(jnp.mean(h * h, axis=-1, keepdims=True) + RMS_EPS)
    dyg = dy * gain
    dh = r * dyg - h * (r * r * r) * jnp.mean(dyg * h, axis=-1, keepdims=True)
    return dh, dy * h * r


def _iota(shape, dim):
    return lax.broadcasted_iota(jnp.int32, shape, dim)


def _lane_pick(x, idx):
    return jnp.sum(jnp.where(_iota(x.shape, 1) == idx, x, 0.0), axis=1, keepdims=True)


def _mm_nt(a, b, *, out_dtype, name):
    m, k = a.shape
    n = b.shape[0]
    tm, tn = _pick(m, ROWS_BIG), _pick(n, (1024, 512, 256, 128))

    def body(a_ref, b_ref, o_ref):
        o_ref[...] = _dot_nt(a_ref[...], b_ref[...]).astype(out_dtype)

    return pl.pallas_call(
        body, grid=(m // tm, n // tn),
        in_specs=[pl.BlockSpec((tm, k), lambda i, j: (i, 0)), pl.BlockSpec((tn, k), lambda i, j: (j, 0))],
        out_specs=pl.BlockSpec((tm, tn), lambda i, j: (i, j)),
        out_shape=jax.ShapeDtypeStruct((m, n), out_dtype), name=name,
        compiler_params=_params("parallel", "parallel"))(a, b)


def _mm_tn(a, b, *, name):
    t, m = a.shape
    n = b.shape[1]
    tn = _pick(n, (2816, 2048, 1408, 1024, 512, 256, 128))
    tm = _pick(m, tuple(c for c in (1408, 1024, 512, 256, 128) if c * tn * 4 <= MM_TN_OUT_BLOCK_BYTES))
    tk = _pick(t, (1408, 1024, 512, 384, 256, 128))

    def body(a_ref, b_ref, o_ref):
        @pl.when(pl.program_id(2) == 0)
        def _():
            o_ref[...] = jnp.zeros_like(o_ref)

        o_ref[...] += _dot_tn(a_ref[...], b_ref[...])

    return pl.pallas_call(
        body, grid=(m // tm, n // tn, t // tk),
        in_specs=[pl.BlockSpec((tk, tm), lambda i, j, k: (k, i)), pl.BlockSpec((tk, tn), lambda i, j, k: (k, j))],
        out_specs=pl.BlockSpec((tm, tn), lambda i, j, k: (i, j)),
        out_shape=jax.ShapeDtypeStruct((m, n), F32), name=name,
        compiler_params=_params("parallel", "parallel", "arbitrary"))(a, b)


def _mm_norm(h, gain, wt, *, name):
    m, k = h.shape
    n = wt.shape[0]
    tm, tn = _pick(m, ROWS_BIG), _pick(n, (1024, 512, 256, 128))

    def body(h_ref, g_ref, w_ref, o_ref, n_ref):
        @pl.when(pl.program_id(1) == 0)
        def _():
            n_ref[...] = _bf(_rms_fwd(h_ref[...], g_ref[...]))

        o_ref[...] = _dot_nt(n_ref[...], w_ref[...])

    return pl.pallas_call(
        body, grid=(m // tm, n // tn),
        in_specs=[pl.BlockSpec((tm, k), lambda i, j: (i, 0)), pl.BlockSpec((1, k), lambda i, j: (0, 0)),
                  pl.BlockSpec((tn, k), lambda i, j: (j, 0))],
        out_specs=[pl.BlockSpec((tm, tn), lambda i, j: (i, j)), pl.BlockSpec((tm, k), lambda i, j: (i, 0))],
        out_shape=[jax.ShapeDtypeStruct((m, n), F32), jax.ShapeDtypeStruct((m, k), BF16)], name=name,
        compiler_params=_params("parallel", "arbitrary"))(h, gain, wt)


def _mm_norm_swiglu(h, gain, wgt, wut, *, name):
    m, k = h.shape
    n = wgt.shape[0]
    tm, tn = _pick(m, ROWS_MID), _pick(n, (1408, 1024, 512, 256, 128))

    def body(h_ref, g_ref, wg_ref, wu_ref, gate_ref, up_ref, act_ref, n_ref):
        @pl.when(pl.program_id(1) == 0)
        def _():
            n_ref[...] = _bf(_rms_fwd(h_ref[...], g_ref[...]))

        gate = _dot_nt(n_ref[...], wg_ref[...])
        up = _dot_nt(n_ref[...], wu_ref[...])
        gate_ref[...] = gate
        up_ref[...] = up
        act_ref[...] = _bf(_silu(gate) * up)

    wspec = pl.BlockSpec((tn, k), lambda i, j: (j, 0))
    ospec = pl.BlockSpec((tm, tn), lambda i, j: (i, j))
    return pl.pallas_call(
        body, grid=(m // tm, n // tn),
        in_specs=[pl.BlockSpec((tm, k), lambda i, j: (i, 0)), pl.BlockSpec((1, k), lambda i, j: (0, 0)), wspec, wspec],
        out_specs=[ospec, ospec, ospec, pl.BlockSpec((tm, k), lambda i, j: (i, 0))],
        out_shape=[jax.ShapeDtypeStruct((m, n), F32), jax.ShapeDtypeStruct((m, n), F32),
                   jax.ShapeDtypeStruct((m, n), BF16), jax.ShapeDtypeStruct((m, k), BF16)], name=name,
        compiler_params=_params("parallel", "arbitrary"))(h, gain, wgt, wut)


def _mm_res(res, a, b, *, name):
    m, k = a.shape
    n = b.shape[1]
    tm, tn = _pick(m, ROWS_BIG), _pick(n, (1024, 512, 256, 128))

    def body(r_ref, a_ref, b_ref, o_ref):
        o_ref[...] = r_ref[...] + _dot(a_ref[...], b_ref[...])

    return pl.pallas_call(
        body, grid=(m // tm, n // tn),
        in_specs=[pl.BlockSpec((tm, tn), lambda i, j: (i, j)), pl.BlockSpec((tm, k), lambda i, j: (i, 0)),
                  pl.BlockSpec((k, tn), lambda i, j: (0, j))],
        out_specs=pl.BlockSpec((tm, tn), lambda i, j: (i, j)),
        out_shape=jax.ShapeDtypeStruct((m, n), F32), name=name,
        compiler_params=_params("parallel", "parallel"))(res, a, b)


def _mm_res_loss(res, a, b, target, *, row0, nrows, name):
    m, k = a.shape
    n = b.shape[1]
    tm = _pick(m, ROWS_MID)

    def body(r_ref, a_ref, b_ref, t_ref, dy_ref, dyb_ref, ls_ref):
        i = pl.program_id(0)

        @pl.when(i == 0)
        def _():
            ls_ref[...] = jnp.zeros_like(ls_ref)

        y = r_ref[...] + _dot(a_ref[...], b_ref[...])
        row = i * tm + _iota((tm, n), 0)
        e = jnp.where((row >= row0) & (row < row0 + nrows), y - t_ref[...], 0.0)
        dy = e / n
        dy_ref[...] = dy
        dyb_ref[...] = _bf(dy)
        ls_ref[...] += jnp.sum(e * e, axis=0, keepdims=True)

    rspec = pl.BlockSpec((tm, n), lambda i: (i, 0))
    return pl.pallas_call(
        body, grid=(m // tm,),
        in_specs=[rspec, pl.BlockSpec((tm, k), lambda i: (i, 0)), pl.BlockSpec((k, n), lambda i: (0, 0)), rspec],
        out_specs=[rspec, rspec, pl.BlockSpec((1, n), lambda i: (0, 0))],
        out_shape=[jax.ShapeDtypeStruct((m, n), F32), jax.ShapeDtypeStruct((m, n), BF16),
                   jax.ShapeDtypeStruct((1, n), F32)], name=name,
        compiler_params=_params("arbitrary"))(res, a, b, target)


def _merge_fwd(o_dn, o_sb, wbd, wbs, proj, *, col_gd, col_gs, name):
    m, kd = o_dn.shape
    ks = o_sb.shape[1]
    n = wbd.shape[1]
    tm = _pick(m, ROWS_BIG)
    tn = _pick(math.gcd(n, math.gcd(col_gd, col_gs)), (512, 256, 128))

    def body(od_ref, os_ref, wd_ref, ws_ref, gd_ref, gs_ref, mg_ref, bd_ref, bs_ref):
        bd = _dot(od_ref[...], wd_ref[...])
        bs = _dot(os_ref[...], ws_ref[...])
        bd_ref[...] = bd
        bs_ref[...] = bs
        mg_ref[...] = _bf(_sigmoid(gd_ref[...]) * bd + _sigmoid(gs_ref[...]) * bs)

    ospec = pl.BlockSpec((tm, tn), lambda i, j: (i, j))
    return pl.pallas_call(
        body, grid=(m // tm, n // tn),
        in_specs=[pl.BlockSpec((tm, kd), lambda i, j: (i, 0)), pl.BlockSpec((tm, ks), lambda i, j: (i, 0)),
                  pl.BlockSpec((kd, tn), lambda i, j: (0, j)), pl.BlockSpec((ks, tn), lambda i, j: (0, j)),
                  pl.BlockSpec((tm, tn), lambda i, j: (i, col_gd // tn + j)),
                  pl.BlockSpec((tm, tn), lambda i, j: (i, col_gs // tn + j))],
        out_specs=[ospec, ospec, ospec],
        out_shape=[jax.ShapeDtypeStruct((m, n), BF16), jax.ShapeDtypeStruct((m, n), F32),
                   jax.ShapeDtypeStruct((m, n), F32)], name=name,
        compiler_params=_params("parallel", "parallel"))(o_dn, o_sb, wbd, wbs, proj, proj)


def _merge_bwd(dh, w_out, proj, br_dn, br_sb, *, col_gd, col_gs, name):
    m, k = dh.shape
    n = w_out.shape[0]
    tm = _pick(m, ROWS_BIG)
    tn = _pick(math.gcd(n, math.gcd(col_gd, col_gs)), (512, 256, 128))

    def body(dh_ref, w_ref, gd_ref, gs_ref, bd_ref, bs_ref, dbd_ref, dbs_ref, dgd_ref, dgs_ref):
        dm = _dot_nt(dh_ref[...], w_ref[...])
        sd = _sigmoid(gd_ref[...])
        ss = _sigmoid(gs_ref[...])
        dbd_ref[...] = _bf(dm * sd)
        dbs_ref[...] = _bf(dm * ss)
        dgd_ref[...] = _bf(dm * bd_ref[...] * sd * (1.0 - sd))
        dgs_ref[...] = _bf(dm * bs_ref[...] * ss * (1.0 - ss))

    ospec = pl.BlockSpec((tm, tn), lambda i, j: (i, j))
    return pl.pallas_call(
        body, grid=(m // tm, n // tn),
        in_specs=[pl.BlockSpec((tm, k), lambda i, j: (i, 0)), pl.BlockSpec((tn, k), lambda i, j: (j, 0)),
                  pl.BlockSpec((tm, tn), lambda i, j: (i, col_gd // tn + j)),
                  pl.BlockSpec((tm, tn), lambda i, j: (i, col_gs // tn + j)), ospec, ospec],
        out_specs=[ospec] * 4,
        out_shape=[jax.ShapeDtypeStruct((m, n), BF16)] * 4, name=name,
        compiler_params=_params("parallel", "parallel"))(dh, w_out, proj, proj, br_dn, br_sb)


def _swiglu_bwd(dy, wfo, gate, up, *, name):
    m, k = dy.shape
    n = wfo.shape[0]
    tm, tn = _pick(m, ROWS_MID), _pick(n, (1408, 1024, 512, 256, 128))

    def body(dy_ref, w_ref, g_ref, u_ref, dg_ref, du_ref):
        da = _dot_nt(dy_ref[...], w_ref[...])
        g = g_ref[...]
        dg_ref[...] = _bf(da * u_ref[...] * _silu_grad(g))
        du_ref[...] = _bf(da * _silu(g))

    ospec = pl.BlockSpec((tm, tn), lambda i, j: (i, j))
    return pl.pallas_call(
        body, grid=(m // tm, n // tn),
        in_specs=[pl.BlockSpec((tm, k), lambda i, j: (i, 0)), pl.BlockSpec((tn, k), lambda i, j: (j, 0)), ospec, ospec],
        out_specs=[ospec, ospec], out_shape=[jax.ShapeDtypeStruct((m, n), BF16)] * 2, name=name,
        compiler_params=_params("parallel", "parallel"))(dy, wfo, gate, up)


def _mm_rmsbwd(pairs, extra, h, gain, dres, *, name, scatter=None):
    m, k = pairs[0][0].shape
    n = h.shape[1]
    tm = _pick(m, ROWS_MID)
    tk = _pick(k, (1408, 1024, 512, 256, 128))
    nk = k // tk
    np_ = len(pairs)

    def body(*refs):
        ab = refs[:2 * np_]
        pos = 2 * np_
        ex = refs[pos:pos + 2] if extra is not None else ()
        pos += len(ex)
        h_ref, g_ref, r_ref, dh_ref, dhb_ref, dg_ref, acc_ref = refs[pos:]
        i, kk = pl.program_id(0), pl.program_id(1)

        @pl.when((i == 0) & (kk == 0))
        def _():
            dg_ref[...] = jnp.zeros_like(dg_ref)

        part = _dot(ab[0][...], ab[1][...])
        for p in range(1, np_):
            part += _dot(ab[2 * p][...], ab[2 * p + 1][...])

        @pl.when(kk == 0)
        def _():
            first = part
            if ex:
                first = first + _dot(ex[0][...], ex[1][...])
            acc_ref[...] = first

        @pl.when(kk > 0)
        def _():
            acc_ref[...] += part

        @pl.when(kk == nk - 1)
        def _():
            dh, dgr = _rms_bwd(h_ref[...], g_ref[...], acc_ref[...])
            dh = dh + r_ref[...]
            dh_ref[...] = dh
            dhb_ref[...] = _bf(dh)
            dg_ref[...] += jnp.sum(dgr, axis=0, keepdims=True)

    in_specs, args = [], []
    for a, b in pairs:
        in_specs += [pl.BlockSpec((tm, tk), lambda i, kk: (i, kk)), pl.BlockSpec((tk, n), lambda i, kk: (kk, 0))]
        args += [a, b]
    if extra is not None:
        k2 = extra[0].shape[1]
        in_specs += [pl.BlockSpec((tm, k2), lambda i, kk: (i, 0)), pl.BlockSpec((k2, n), lambda i, kk: (0, 0))]
        args += list(extra)
    rspec = pl.BlockSpec((tm, n), lambda i, kk: (i, 0))
    in_specs += [rspec, pl.BlockSpec((1, n), lambda i, kk: (0, 0)), rspec]
    call = dict(grid=(m // tm, nk), in_specs=in_specs,
                out_specs=[rspec, rspec, pl.BlockSpec((1, n), lambda i, kk: (0, 0))],
                out_shape=[jax.ShapeDtypeStruct((m, n), F32), jax.ShapeDtypeStruct((m, n), BF16),
                           jax.ShapeDtypeStruct((1, n), F32)],
                scratch_shapes=[pltpu.VMEM((tm, n), F32)], name=name)
    if scatter is None:
        return pl.pallas_call(body, compiler_params=_params("arbitrary", "arbitrary"), **call)(*args, h, gain, dres)
    res = _call_with_exchange(body, args=(*args, h, gain, dres), srcs=scatter, scatter=[True] * len(scatter), **call)
    return [*res[:3], list(res[3:])]


def _conv_taps(cur, prev8, w_ref, first):
    nk = w_ref.shape[0]
    out = cur * w_ref[nk - 1:nk, :]
    for s in range(1, nk):
        out += _shift_down(cur, prev8, s, first) * w_ref[nk - 1 - s:nk - s, :]
    return out


def _shift_up(cur, next8, s, last):
    rows = cur.shape[0]
    row = _iota(cur.shape, 0)
    next8 = jnp.where(last, 0.0, next8)
    sh = pltpu.roll(cur, rows - s, axis=0)
    nh = jnp.tile(pltpu.roll(next8, 8 - s, axis=0), (rows // 8, 1))
    return jnp.where(row >= rows - s, nh, sh)


def _shift_down(cur, prev8, s, first):
    rows = cur.shape[0]
    row = _iota(cur.shape, 0)
    prev8 = jnp.where(first, 0.0, prev8)
    sh = pltpu.roll(cur, s, axis=0)
    ph = jnp.tile(pltpu.roll(prev8, s, axis=0), (rows // 8, 1))
    return jnp.where(row < s, ph, sh)


def _gdn_pre(proj, pab, cq, ck, cv, a_log, dt_bias, *, heads, dk, dv, col_q, col_k, col_v, row_lo, row_hi, name):
    t = proj.shape[0]
    tm = _pick(t, (384, 256, 128))
    nb = t // tm

    def body(pq_ref, pqp_ref, pk_ref, pkp_ref, pv_ref, pvp_ref, ab_ref, cq_ref, ck_ref, cv_ref, al_ref, dt_ref,
             qn_ref, kn_ref, v_ref, g_ref, b_ref):
        h, i = pl.program_id(0), pl.program_id(1)
        first = i == 0
        row = i * tm + _iota((tm, 1), 0)
        valid = (row >= row_lo) & (row < row_hi)
        q1 = _silu(_conv_taps(pq_ref[...], pqp_ref[...], cq_ref, first))
        k1 = _silu(_conv_taps(pk_ref[...], pkp_ref[...], ck_ref, first))
        v1 = _silu(_conv_taps(pv_ref[...], pvp_ref[...], cv_ref, first))
        qn_ref[...] = jnp.where(valid, q1 * lax.rsqrt(jnp.sum(q1 * q1, axis=-1, keepdims=True) + L2_EPS), 0.0)
        kn_ref[...] = jnp.where(valid, k1 * lax.rsqrt(jnp.sum(k1 * k1, axis=-1, keepdims=True) + L2_EPS), 0.0)
        v_ref[...] = jnp.where(valid, v1, 0.0)
        ab = ab_ref[...]
        da = _lane_pick(ab, h)
        db = _lane_pick(ab, heads + h)
        a = _lane_pick(al_ref[...], h)
        dtb = _lane_pick(dt_ref[...], h)
        g_ref[...] = jnp.where(valid, -jnp.exp(a) * _softplus(da + dtb), 0.0)
        b_ref[...] = jnp.where(valid, _sigmoid(db), 0.0)

    def cur(width, col):
        return pl.BlockSpec((tm, width), lambda h, i: (i, col // width + h))

    def prev(width, col):
        return pl.BlockSpec((8, width), lambda h, i: (jnp.maximum(i * (tm // 8) - 1, 0), col // width + h))

    def out(width):
        return pl.BlockSpec((None, tm, width), lambda h, i: (h, i, 0))

    small = pl.BlockSpec((1, LANES), lambda h, i: (0, 0))
    return pl.pallas_call(
        body, grid=(heads, nb),
        in_specs=[cur(dk, col_q), prev(dk, col_q), cur(dk, col_k), prev(dk, col_k), cur(dv, col_v), prev(dv, col_v),
                  pl.BlockSpec((tm, LANES), lambda h, i: (i, 0)),
                  pl.BlockSpec((cq.shape[0], dk), lambda h, i: (0, h)), pl.BlockSpec((ck.shape[0], dk), lambda h, i: (0, h)),
                  pl.BlockSpec((cv.shape[0], dv), lambda h, i: (0, h)), small, small],
        out_specs=[out(dk), out(dk), out(dv), out(1), out(1)],
        out_shape=[jax.ShapeDtypeStruct((heads, t, dk), F32), jax.ShapeDtypeStruct((heads, t, dk), F32),
                   jax.ShapeDtypeStruct((heads, t, dv), F32), jax.ShapeDtypeStruct((heads, t, 1), F32),
                   jax.ShapeDtypeStruct((heads, t, 1), F32)], name=name,
        compiler_params=_params("parallel", "parallel"))(proj, proj, proj, proj, proj, proj, pab, cq, ck, cv, a_log, dt_bias)


def _chunk_masks(rows=GDN_ROWS, row0=0):
    ri = row0 + _iota((rows, GDN_ROWS), 0)
    ci = _iota((rows, GDN_ROWS), 1)
    same = jnp.right_shift(ri, CHUNK_SHIFT) == jnp.right_shift(ci, CHUNK_SHIFT)
    return same, same & (ri >= ci), same & (ri > ci), ri == ci


def _col_to_row(col, eye):
    return jnp.sum(jnp.where(eye, col, 0.0), axis=0, keepdims=True)


def _row_to_col(row, eye):
    return jnp.sum(jnp.where(eye, row, 0.0), axis=1, keepdims=True)


def _chunk_common(blocks, dk_scale):
    same, incl, strict, eye = _chunk_masks()
    tri = jnp.where(incl, 1.0, 0.0).astype(BF16)
    tot = jnp.where(same, 1.0, 0.0).astype(BF16)
    gbs = [jnp.broadcast_to(g, (GDN_ROWS, LANES)) for _, _, g, _ in blocks]
    gams = [jnp.max(_dot_exact_l(tri, gb), axis=1, keepdims=True) for gb in gbs]
    lasts = [jnp.max(_dot_exact_l(tot, gb), axis=1, keepdims=True) for gb in gbs]
    kbs = [kn * beta for _, kn, _, beta in blocks]
    qts = [qn * dk_scale for qn, _, _, _ in blocks]
    boths = [_dot_nt(_bf(jnp.concatenate([kb, qt], axis=0)), _bf(blk[1]))
             for kb, qt, blk in zip(kbs, qts, blocks)]
    out = []
    for gam, last, kb, qt, both in zip(gams, lasts, kbs, qts, boths):
        diff = gam - _col_to_row(gam, eye)
        decay = jnp.where(incl, jnp.exp(jnp.where(incl, diff, 0.0)), 0.0)
        out.append(dict(incl=incl, strict=strict, eye=eye, decay=decay, eg=jnp.exp(gam), ek=jnp.exp(last - gam),
                        egl=jnp.exp(last), kb=kb, qt=qt, lmat=jnp.where(strict, both[:GDN_ROWS] * decay, 0.0),
                        pmat=jnp.where(incl, both[GDN_ROWS:] * decay, 0.0)))
    return out


def _gdn_prep(qn, kn, v, g, beta, *, name):
    heads, t, dk = qn.shape
    dv = v.shape[2]
    rows = _pick(t, (3 * GDN_ROWS, 2 * GDN_ROWS, GDN_ROWS))
    dk_scale = dk ** -0.5

    def body(q_ref, k_ref, v_ref, g_ref, b_ref, u_ref, w_ref, p_ref, qd_ref, kd_ref, egl_ref, t_ref):
        rs = [pl.ds(b * GDN_ROWS, GDN_ROWS) for b in range(rows // GDN_ROWS)]
        cs = _chunk_common([(q_ref[r, :], k_ref[r, :], g_ref[r, :], b_ref[r, :]) for r in rs], dk_scale)
        eye_f = jnp.where(cs[0]["eye"], 1.0, 0.0)
        tinvs = [eye_f - c["lmat"] for c in cs]
        ys = [_dot_hp(c["lmat"], c["lmat"]) for c in cs]
        for _ in range(CHUNK_SHIFT - 1):
            boths = [_dot_hp(y, jnp.concatenate([y, tinv], axis=1)) for y, tinv in zip(ys, tinvs)]
            ys = [both[:, :GDN_ROWS] for both in boths]
            tinvs = [tinv + both[:, GDN_ROWS:] for tinv, both in zip(tinvs, boths)]
        uws = [_dot_hp(tinv, jnp.concatenate([v_ref[r, :] * b_ref[r, :], c["kb"] * c["eg"]], axis=1))
               for r, c, tinv in zip(rs, cs, tinvs)]
        for r, c, tinv, uw in zip(rs, cs, tinvs, uws):
            u_ref[r, :] = uw[:, :dv]
            w_ref[r, :] = uw[:, dv:]
            p_ref[r, :] = c["pmat"]
            qd_ref[r, :] = c["qt"] * c["eg"]
            kd_ref[r, :] = k_ref[r, :] * c["ek"]
            egl_ref[r, :] = c["egl"]
            t_ref[r, :] = tinv

    def blk(width):
        return pl.BlockSpec((None, rows, width), lambda h, i: (h, i, 0))

    def shp(width):
        return jax.ShapeDtypeStruct((heads, t, width), F32)

    return pl.pallas_call(
        body, grid=(heads, t // rows), in_specs=[blk(dk), blk(dk), blk(dv), blk(1), blk(1)],
        out_specs=[blk(dv), blk(dk), blk(GDN_ROWS), blk(dk), blk(dk), blk(1), blk(GDN_ROWS)],
        out_shape=[shp(dv), shp(dk), shp(GDN_ROWS), shp(dk), shp(dk), shp(1), shp(GDN_ROWS)], name=name,
        compiler_params=_params("parallel", "parallel"))(qn, kn, v, g, beta)


def _gdn_scan(u, w, p, qd, kd, egl, proj, gain, *, col_z, name):
    heads, t, dv = u.shape
    dk = w.shape[2]
    nb = t // GDN_ROWS
    sub = GDN_ROWS // CHUNK
    hp = SCAN_HEADS

    def body(u_ref, w_ref, p_ref, qd_ref, kd_ref, egl_ref, z_ref, gn_ref, o_ref, og_ref, st_ref, s_ref):
        @pl.when(pl.program_id(1) == 0)
        def _():
            s_ref[...] = jnp.zeros_like(s_ref)

        hs = range(hp)
        vn_parts = [[jnp.zeros((CHUNK, dv), F32)] * sub for _ in hs]
        for c in range(sub):
            r = pl.ds(c * CHUNK, CHUNK)
            ss = [s_ref[hh] for hh in hs]
            sbs = [_bf(s) for s in ss]
            wss = [_dot(_bf(jnp.concatenate([w_ref[hh, r, :], qd_ref[hh, r, :]], axis=0)), sbs[hh])
                   for hh in hs]
            vns = [u_ref[hh, r, :] - wss[hh][:CHUNK] for hh in hs]
            for hh in hs:
                vn_parts[hh][c] = vns[hh]
            os_ = [wss[hh][CHUNK:] + _dot(_bf(p_ref[hh, r, :]), _bf(jnp.concatenate(vn_parts[hh], axis=0))) for hh in hs]
            new = [ss[hh] * egl_ref[hh, pl.ds(c * CHUNK, 1), :] + _dot_tn(_bf(kd_ref[hh, r, :]), _bf(vns[hh])) for hh in hs]
            for hh in hs:
                cols = pl.ds(hh * dv, dv)
                st_ref[hh, c] = ss[hh]
                s_ref[hh] = new[hh]
                o_ref[hh, r, :] = os_[hh]
                og_ref[r, cols] = _bf(_rms_fwd(os_[hh], gn_ref[...]) * _silu(z_ref[r, cols]))

    def blk(width):
        return pl.BlockSpec((hp, GDN_ROWS, width), lambda h, i: (h, i, 0))

    return pl.pallas_call(
        body, grid=(heads // hp, nb),
        in_specs=[blk(dv), blk(dk), blk(GDN_ROWS), blk(dk), blk(dk), blk(1),
                  pl.BlockSpec((GDN_ROWS, hp * dv), lambda h, i: (i, col_z // (hp * dv) + h)),
                  pl.BlockSpec((1, dv), lambda h, i: (0, 0))],
        out_specs=[blk(dv), pl.BlockSpec((GDN_ROWS, hp * dv), lambda h, i: (i, h)),
                   pl.BlockSpec((hp, sub, dk, dv), lambda h, i: (h, i, 0, 0))],
        out_shape=[jax.ShapeDtypeStruct((heads, t, dv), F32), jax.ShapeDtypeStruct((t, heads * dv), BF16),
                   jax.ShapeDtypeStruct((heads, t // CHUNK, dk, dv), F32)],
        scratch_shapes=[pltpu.VMEM((hp, dk, dv), F32)], name=name,
        compiler_params=_params("parallel", "arbitrary"))(u, w, p, qd, kd, egl, proj, gain)


def _gdn_post_bwd(o, proj, gain, dout, *, col_z, name):
    heads, t, dv = o.shape
    tm = _pick(t, (384, 256, 128))

    def body(o_ref, z_ref, gn_ref, d_ref, do_ref, dz_ref, dg_ref):
        @pl.when((pl.program_id(0) == 0) & (pl.program_id(1) == 0))
        def _():
            dg_ref[...] = jnp.zeros_like(dg_ref)

        o_, z, d = o_ref[...], z_ref[...], d_ref[...]
        y = _rms_fwd(o_, gn_ref[...])
        dz_ref[...] = _bf(d * y * _silu_grad(z))
        do, dgr = _rms_bwd(o_, gn_ref[...], d * _silu(z))
        do_ref[...] = do
        dg_ref[...] += jnp.sum(dgr, axis=0, keepdims=True)

    return pl.pallas_call(
        body, grid=(t // tm, heads),
        in_specs=[pl.BlockSpec((None, tm, dv), lambda i, h: (h, i, 0)),
                  pl.BlockSpec((tm, dv), lambda i, h: (i, col_z // dv + h)),
                  pl.BlockSpec((1, dv), lambda i, h: (0, 0)), pl.BlockSpec((tm, dv), lambda i, h: (i, h))],
        out_specs=[pl.BlockSpec((None, tm, dv), lambda i, h: (h, i, 0)), pl.BlockSpec((tm, dv), lambda i, h: (i, h)),
                   pl.BlockSpec((1, dv), lambda i, h: (0, 0))],
        out_shape=[jax.ShapeDtypeStruct((heads, t, dv), F32), jax.ShapeDtypeStruct((t, heads * dv), BF16),
                   jax.ShapeDtypeStruct((1, dv), F32)], name=name,
        compiler_params=_params("arbitrary", "arbitrary"))(o, proj, gain, dout)


def _gdn_bwd_scan(u, w, p, qd, kd, egl, st, do, *, name):
    heads, t, dv = u.shape
    dk = w.shape[2]
    nb = t // GDN_ROWS
    sub = GDN_ROWS // CHUNK
    hp = SCAN_HEADS

    def body(u_ref, w_ref, p_ref, qd_ref, kd_ref, egl_ref, st_ref, do_ref,
             du_ref, dw_ref, dp_ref, dqd_ref, dkd_ref, dgl_ref, ds_ref):
        @pl.when(pl.program_id(1) == 0)
        def _():
            ds_ref[...] = jnp.zeros_like(ds_ref)

        hs = range(hp)
        zeros = jnp.zeros((CHUNK, dv), BF16)
        for c in reversed(range(sub)):
            r = pl.ds(c * CHUNK, CHUNK)
            ss = [st_ref[hh, c] for hh in hs]
            sbs = [_bf(s) for s in ss]
            dss = [ds_ref[hh] for hh in hs]
            dsbs = [_bf(ds) for ds in dss]
            dobs = [_bf(do_ref[hh, r, :]) for hh in hs]
            wbs = [_bf(w_ref[hh, r, :]) for hh in hs]
            vns = [u_ref[hh, r, :] - _dot(wbs[hh], sbs[hh]) for hh in hs]
            dvns = [_dot_tn(_bf(p_ref[hh, r, :]), dobs[hh])[c * CHUNK:(c + 1) * CHUNK, :]
                    + _dot(_bf(kd_ref[hh, r, :]), dsbs[hh]) for hh in hs]
            dods = [jnp.concatenate([dobs[hh], _bf(dvns[hh])], axis=0) for hh in hs]
            boths = [_dot_nt(dods[hh], sbs[hh]) for hh in hs]
            dps = [_dot_nt(dobs[hh], jnp.concatenate([_bf(vns[hh]) if cc == c else zeros for cc in range(sub)], axis=0))
                   for hh in hs]
            dkds = [_dot_nt(_bf(vns[hh]), dsbs[hh]) for hh in hs]
            new = [dss[hh] * egl_ref[hh, pl.ds(c * CHUNK, 1), :]
                   + _dot_tn(jnp.concatenate([_bf(qd_ref[hh, r, :]), -wbs[hh]], axis=0), dods[hh])
                   for hh in hs]
            for hh in hs:
                du_ref[hh, r, :] = dvns[hh]
                dw_ref[hh, r, :] = -boths[hh][CHUNK:]
                dp_ref[hh, r, :] = jnp.where(_chunk_masks(CHUNK, c * CHUNK)[1], dps[hh], 0.0)
                dqd_ref[hh, r, :] = boths[hh][:CHUNK]
                dkd_ref[hh, r, :] = dkds[hh]
                dgl = jnp.sum(jnp.sum(dss[hh] * ss[hh], axis=1, keepdims=True), axis=0, keepdims=True)
                dgl_ref[hh, r, :] = jnp.where(_iota((CHUNK, 1), 0) == CHUNK - 1, dgl, 0.0)
                ds_ref[hh] = new[hh]

    def blk(width):
        return pl.BlockSpec((hp, GDN_ROWS, width), lambda h, i: (h, nb - 1 - i, 0))

    def shp(width):
        return jax.ShapeDtypeStruct((heads, t, width), F32)

    return pl.pallas_call(
        body, grid=(heads // hp, nb),
        in_specs=[blk(dv), blk(dk), blk(GDN_ROWS), blk(dk), blk(dk), blk(1),
                  pl.BlockSpec((hp, sub, dk, dv), lambda h, i: (h, nb - 1 - i, 0, 0)), blk(dv)],
        out_specs=[blk(dv), blk(dk), blk(GDN_ROWS), blk(dk), blk(dk), blk(1)],
        out_shape=[shp(dv), shp(dk), shp(GDN_ROWS), shp(dk), shp(dk), shp(1)],
        scratch_shapes=[pltpu.VMEM((hp, dk, dv), F32)], name=name,
        compiler_params=_params("parallel", "arbitrary"))(u, w, p, qd, kd, egl, st, do)


def _gdn_bwd_prep(qn, kn, v, g, beta, tinv, u, w, du, dw, dp, dqd, dkd, dgl, *, name):
    heads, t, dk = qn.shape
    dv = v.shape[2]
    rows = _pick(t, (3 * GDN_ROWS, 2 * GDN_ROWS, GDN_ROWS))
    dk_scale = dk ** -0.5

    def rowsum(x):
        return jnp.sum(x, axis=1, keepdims=True)

    def body(q_ref, k_ref, v_ref, g_ref, b_ref, t_ref, u_ref, w_ref, du_ref, dw_ref, dp_ref, dqd_ref, dkd_ref, dgl_ref,
             dq_ref, dkk_ref, dvv_ref, dg_ref, db_ref):
        rs = [pl.ds(b * GDN_ROWS, GDN_ROWS) for b in range(rows // GDN_ROWS)]
        cs = _chunk_common([(q_ref[r, :], k_ref[r, :], g_ref[r, :], b_ref[r, :]) for r in rs], dk_scale)
        dbvws = [_dot_hp(t_ref[r, :], jnp.concatenate([du_ref[r, :], dw_ref[r, :]], axis=1), _dot_tn)
                 for r in rs]
        das = [-_dot_nt(_bf(dbvw), _bf(jnp.concatenate([u_ref[r, :], w_ref[r, :]], axis=1)))
               for r, dbvw in zip(rs, dbvws)]
        dls = [jnp.where(c["strict"], da, 0.0) for c, da in zip(cs, das)]
        dmns = [_bf(jnp.concatenate([dl * c["decay"], dp_ref[r, :] * c["decay"]], axis=0))
                for r, c, dl in zip(rs, cs, dls)]
        boths = [_dot(dmn, _bf(k_ref[r, :])) for r, dmn in zip(rs, dmns)]
        dkns = [_dot_tn(dmn, _bf(jnp.concatenate([c["kb"], c["qt"]], axis=0)))
                for c, dmn in zip(cs, dmns)]
        for r, c, dbvw, dl, both, dkn in zip(rs, cs, dbvws, dls, boths, dkns):
            kn_, beta_, v_ = k_ref[r, :], b_ref[r, :], v_ref[r, :]
            eye = c["eye"]
            kb, qt, eg, ek = c["kb"], c["qt"], c["eg"], c["ek"]
            dbv, dbw = dbvw[:, :dv], dbvw[:, dv:]
            dp_ = dp_ref[r, :]
            dkb = both[:GDN_ROWS] + dbw * eg
            dqt = both[GDN_ROWS:]
            gmat = dl * c["lmat"] + dp_ * c["pmat"]
            dqd_, dkd_ = dqd_ref[r, :], dkd_ref[r, :]
            qd = qt * eg
            kd = kn_ * ek
            bw = kb * eg
            kdsum = rowsum(dkd_ * kd)
            dgam = rowsum(gmat) - _row_to_col(jnp.sum(gmat, axis=0, keepdims=True), eye)
            dgam += rowsum(dbw * bw) + rowsum(dqd_ * qd) - kdsum
            last = (_iota((GDN_ROWS, 1), 0) & (CHUNK - 1)) == CHUNK - 1
            same = _chunk_masks()[0]
            same_f = jnp.where(same, 1.0, 0.0).astype(BF16)
            chunk_tot = jnp.max(_dot_exact_l(same_f, jnp.broadcast_to(kdsum, (GDN_ROWS, LANES))), axis=1, keepdims=True)
            dgam += jnp.where(last, chunk_tot, 0.0) + dgl_ref[r, :] * c["egl"]
            dq_ref[r, :] = (dqt + dqd_ * eg) * dk_scale
            dkk_ref[r, :] = dkn + dkd_ * ek + dkb * beta_
            dvv_ref[r, :] = dbv * beta_
            db_ref[r, :] = rowsum(dbv * v_) + rowsum(dkb * kn_)
            upper = jnp.where(same & (_iota((GDN_ROWS, GDN_ROWS), 0) <= _iota((GDN_ROWS, GDN_ROWS), 1)), 1.0, 0.0)
            dgb = _dot_exact_l(upper.astype(BF16), jnp.broadcast_to(dgam, (GDN_ROWS, LANES)))
            dg_ref[r, :] = _lane_pick(dgb, 0)

    def blk(width):
        return pl.BlockSpec((None, rows, width), lambda h, i: (h, i, 0))

    def shp(width):
        return jax.ShapeDtypeStruct((heads, t, width), F32)

    return pl.pallas_call(
        body, grid=(heads, t // rows),
        in_specs=[blk(dk), blk(dk), blk(dv), blk(1), blk(1), blk(GDN_ROWS), blk(dv), blk(dk),
                  blk(dv), blk(dk), blk(GDN_ROWS), blk(dk), blk(dk), blk(1)],
        out_specs=[blk(dk), blk(dk), blk(dv), blk(1), blk(1)],
        out_shape=[shp(dk), shp(dk), shp(dv), shp(1), shp(1)], name=name,
        compiler_params=_params("parallel", "parallel"))(qn, kn, v, g, beta, tinv, u, w, du, dw, dp, dqd, dkd, dgl)


def _gdn_pre_bwd_a(proj, pab, cq, ck, cv, a_log, dt_bias, dqn, dkn, dvv, dg, dbeta, *,
                   heads, dk, dv, col_q, col_k, col_v, row_lo, row_hi, name):
    t = proj.shape[0]
    tm = _pick(t, (384, 256, 128))
    nb = t // tm

    def body(pq_ref, pqp_ref, pk_ref, pkp_ref, pv_ref, pvp_ref, ab_ref, cq_ref, ck_ref, cv_ref, al_ref, dt_ref,
             dqn_ref, dkn_ref, dvv_ref, dg_ref, db_ref, dcq_ref, dck_ref, dcv_ref, dab_ref, dal_ref, ddt_ref):
        i, h = pl.program_id(0), pl.program_id(1)
        first = i == 0

        @pl.when((i == 0) & (h == 0))
        def _():
            dal_ref[...] = jnp.zeros_like(dal_ref)
            ddt_ref[...] = jnp.zeros_like(ddt_ref)

        @pl.when(h == 0)
        def _():
            dab_ref[...] = jnp.zeros_like(dab_ref)

        row = i * tm + _iota((tm, 1), 0)
        valid = (row >= row_lo) & (row < row_hi)

        def l2_bwd(c1, dn):
            x1 = _silu(c1)
            r = lax.rsqrt(jnp.sum(x1 * x1, axis=-1, keepdims=True) + L2_EPS)
            dn = jnp.where(valid, dn, 0.0)
            d1 = r * dn - x1 * (r * r * r) * jnp.sum(dn * x1, axis=-1, keepdims=True)
            return d1 * _silu_grad(c1)

        dcq_ref[...] = l2_bwd(_conv_taps(pq_ref[...], pqp_ref[...], cq_ref, first), dqn_ref[...])
        dck_ref[...] = l2_bwd(_conv_taps(pk_ref[...], pkp_ref[...], ck_ref, first), dkn_ref[...])
        cv1 = _conv_taps(pv_ref[...], pvp_ref[...], cv_ref, first)
        dcv_ref[...] = jnp.where(valid, dvv_ref[...], 0.0) * _silu_grad(cv1)
        ab = ab_ref[...]
        da = _lane_pick(ab, h)
        db = _lane_pick(ab, heads + h)
        a = _lane_pick(al_ref[...], h)
        dtb = _lane_pick(dt_ref[...], h)
        dgv = jnp.where(valid, dg_ref[...], 0.0)
        ea = jnp.exp(a)
        g = -ea * _softplus(da + dtb)
        dda = dgv * (-ea) * _sigmoid(da + dtb)
        beta = _sigmoid(db)
        ddb = jnp.where(valid, db_ref[...], 0.0) * beta * (1.0 - beta)
        lane = _iota((tm, LANES), 1)
        dab_ref[...] += jnp.where(lane == h, dda, 0.0) + jnp.where(lane == heads + h, ddb, 0.0)
        lane1 = _iota((1, LANES), 1)
        dal_ref[...] += jnp.where(lane1 == h, jnp.sum(dgv * g, axis=0, keepdims=True), 0.0)
        ddt_ref[...] += jnp.where(lane1 == h, jnp.sum(dda, axis=0, keepdims=True), 0.0)

    def cur(width, col):
        return pl.BlockSpec((tm, width), lambda i, h: (i, col // width + h))

    def prev(width, col):
        return pl.BlockSpec((8, width), lambda i, h: (jnp.maximum(i * (tm // 8) - 1, 0), col // width + h))

    def hd(width):
        return pl.BlockSpec((None, tm, width), lambda i, h: (h, i, 0))

    small = pl.BlockSpec((1, LANES), lambda i, h: (0, 0))
    return pl.pallas_call(
        body, grid=(nb, heads),
        in_specs=[cur(dk, col_q), prev(dk, col_q), cur(dk, col_k), prev(dk, col_k), cur(dv, col_v), prev(dv, col_v),
                  pl.BlockSpec((tm, LANES), lambda i, h: (i, 0)),
                  pl.BlockSpec((cq.shape[0], dk), lambda i, h: (0, h)), pl.BlockSpec((ck.shape[0], dk), lambda i, h: (0, h)),
                  pl.BlockSpec((cv.shape[0], dv), lambda i, h: (0, h)), small, small,
                  hd(dk), hd(dk), hd(dv), hd(1), hd(1)],
        out_specs=[hd(dk), hd(dk), hd(dv), pl.BlockSpec((tm, LANES), lambda i, h: (i, 0)), small, small],
        out_shape=[jax.ShapeDtypeStruct((heads, t, dk), F32), jax.ShapeDtypeStruct((heads, t, dk), F32),
                   jax.ShapeDtypeStruct((heads, t, dv), F32), jax.ShapeDtypeStruct((t, LANES), F32),
                   jax.ShapeDtypeStruct((1, LANES), F32), jax.ShapeDtypeStruct((1, LANES), F32)], name=name,
        compiler_params=_params("arbitrary", "arbitrary"))(
            proj, proj, proj, proj, proj, proj, pab, cq, ck, cv, a_log, dt_bias, dqn, dkn, dvv, dg, dbeta)


def _conv_bwd(proj, dc, cw, *, heads, width, col, name):
    t = proj.shape[0]
    tm = _pick(t, (384, 256, 128))
    nb = t // tm
    nk = cw.shape[0]

    def body(p_ref, pp_ref, d_ref, dn_ref, w_ref, dp_ref, dw_ref):
        i = pl.program_id(1)
        first, last = i == 0, i == nb - 1

        @pl.when(first)
        def _():
            dw_ref[...] = jnp.zeros_like(dw_ref)

        x, d = p_ref[...], d_ref[...]
        dx = d * w_ref[nk - 1:nk, :]
        dw_ref[nk - 1:nk, :] += jnp.sum(d * x, axis=0, keepdims=True)
        for s in range(1, nk):
            dx += _shift_up(d, dn_ref[...], s, last) * w_ref[nk - 1 - s:nk - s, :]
            dw_ref[nk - 1 - s:nk - s, :] += jnp.sum(d * _shift_down(x, pp_ref[...], s, first), axis=0, keepdims=True)
        dp_ref[...] = _bf(dx)

    return pl.pallas_call(
        body, grid=(heads, nb),
        in_specs=[pl.BlockSpec((tm, width), lambda h, i: (i, col // width + h)),
                  pl.BlockSpec((8, width), lambda h, i: (jnp.maximum(i * (tm // 8) - 1, 0), col // width + h)),
                  pl.BlockSpec((None, tm, width), lambda h, i: (h, i, 0)),
                  pl.BlockSpec((None, 8, width), lambda h, i: (h, jnp.minimum((i + 1) * (tm // 8), t // 8 - 1), 0)),
                  pl.BlockSpec((nk, width), lambda h, i: (0, h))],
        out_specs=[pl.BlockSpec((tm, width), lambda h, i: (i, h)), pl.BlockSpec((nk, width), lambda h, i: (0, h))],
        out_shape=[jax.ShapeDtypeStruct((t, heads * width), BF16), jax.ShapeDtypeStruct((nk, heads * width), F32)],
        name=name, compiler_params=_params("parallel", "arbitrary"))(proj, proj, dc, dc, cw)


def _sb_pre(proj, gq, gk, *, heads, dh, col_q, col_k, col_v, name):
    t = proj.shape[0]
    tm = _pick(t, (384, 256, 128))

    def body(q_ref, k_ref, v_ref, gq_ref, gk_ref, qo_ref, ko_ref, vo_ref):
        qo_ref[...] = _bf(_rms_fwd(q_ref[...], gq_ref[...]))
        ko_ref[...] = _bf(_rms_fwd(k_ref[...], gk_ref[...]))
        vo_ref[...] = _bf(v_ref[...])

    def cur(col):
        return pl.BlockSpec((tm, dh), lambda i, h: (i, col // dh + h))

    gspec = pl.BlockSpec((1, dh), lambda i, h: (0, 0))
    ospec = pl.BlockSpec((tm, dh), lambda i, h: (i, h))
    return pl.pallas_call(
        body, grid=(t // tm, heads), in_specs=[cur(col_q), cur(col_k), cur(col_v), gspec, gspec],
        out_specs=[ospec] * 3, out_shape=[jax.ShapeDtypeStruct((t, heads * dh), BF16)] * 3, name=name,
        compiler_params=_params("parallel", "parallel"))(proj, proj, proj, gq, gk)


def _sb_tile(z, i, j, blk, key_lo):
    qpos = i * blk + _iota((blk, blk), 0)
    kpos = j * blk + _iota((blk, blk), 1)
    vis = (kpos < qpos) & (kpos >= key_lo)
    ls = jnp.minimum(z, 0.0) - jnp.log(1.0 + jnp.exp(-jnp.abs(z)))
    return vis, ls, jnp.where(vis, ls - z, 0.0)


def _dot2_r(x, m):
    hi, lo = _split2(x)
    return _dot(hi, m) + _dot(lo, m)


def _running_sums(x, tri, reverse):
    groups = [x[:, s:s + LANES] for s in range(0, x.shape[1], LANES)]
    inside = [_dot2_r(g, tri) for g in groups]
    sums = [jnp.sum(g, axis=1, keepdims=True) for g in groups]
    order = list(reversed(range(len(groups)))) if reverse else list(range(len(groups)))
    out, acc = [None] * len(groups), None
    for gi in order:
        out[gi] = inside[gi] if acc is None else inside[gi] + acc
        acc = sums[gi] if acc is None else acc + sums[gi]
    return jnp.concatenate(out, axis=1), acc


def _sb_fwd(qs, ks, vs, *, heads, dh, key_lo, name, gather=None):
    t = qs.shape[0]
    blk = _pick(t, (3 * SB_BLOCK, 2 * SB_BLOCK, SB_BLOCK))
    nq = t // blk
    assert nq <= LANES
    scale = dh ** -0.5
    hp = SB_HEADS_PER_STEP

    def body(q_ref, k_ref, v_ref, o_ref, c_ref):
        i = pl.program_id(1)
        later = jnp.where(_iota((LANES, LANES), 0) > _iota((LANES, LANES), 1), 1.0, 0.0).astype(BF16)
        lane = _iota((blk, LANES), 1)
        c_ref[...] = jnp.zeros_like(c_ref)

        def step(n, carry):
            j = i - n
            rows = pl.ds(pl.multiple_of(j * blk, blk), blk)
            hs = range(hp)
            cols = [pl.ds(hh * dh, dh) for hh in hs]
            zs = [_dot_nt(q_ref[:, cols[hh]], k_ref[rows, cols[hh]]) * scale for hh in hs]
            tiles = [_sb_tile(z, i, j, blk, key_lo) for z in zs]
            sufs = [_running_sums(lk, later, reverse=True) for _, _, lk in tiles]
            wgts = [jnp.where(vis, jnp.exp(ls + suf + carry[2 * hh + 1]), 0.0)
                    for hh, ((vis, ls, _), (suf, _)) in enumerate(zip(tiles, sufs))]
            accs = [carry[2 * hh] + _dot(_bf(wgts[hh]), v_ref[rows, cols[hh]]) for hh in hs]
            out = []
            for hh in hs:
                c_ref[hh] = jnp.where(lane == j, carry[2 * hh + 1], c_ref[hh])
                out += [accs[hh], carry[2 * hh + 1] + sufs[hh][1]]
            return tuple(out)

        res = lax.fori_loop(0, i + 1, step, (jnp.zeros((blk, dh), F32), jnp.zeros((blk, 1), F32)) * hp)
        for hh in range(hp):
            o_ref[:, pl.ds(hh * dh, dh)] = _bf(res[2 * hh])

    full = pl.BlockSpec((t, hp * dh), lambda h, i: (0, h))
    call = dict(grid=(heads // hp, nq),
                in_specs=[pl.BlockSpec((blk, hp * dh), lambda h, i: (i, h)), full, full],
                out_specs=[pl.BlockSpec((blk, hp * dh), lambda h, i: (i, h)),
                           pl.BlockSpec((hp, blk, LANES), lambda h, i: (h, i, 0))],
                out_shape=[jax.ShapeDtypeStruct((t, heads * dh), BF16), jax.ShapeDtypeStruct((heads, t, LANES), F32)],
                name=name)
    if gather is None:
        return pl.pallas_call(body, compiler_params=_params("parallel", "parallel"), **call)(qs, ks, vs)
    res = _call_with_exchange(body, scratch_shapes=[], args=(qs, ks, vs), srcs=gather, scatter=[False] * len(gather),
                              **call)
    return [*res[:2], list(res[2:])]


def _sb_bwd(qs, ks, vs, do, carry, *, heads, dh, key_lo, name, scatter=None):
    t = qs.shape[0]
    blk = _pick(t, (3 * SB_BLOCK, 2 * SB_BLOCK, SB_BLOCK))
    nq = t // blk
    scale = dh ** -0.5
    hp = SB_HEADS_PER_STEP

    def body(q_ref, k_ref, v_ref, do_ref, c_ref, dq_ref, dk_ref, dv_ref):
        i = pl.program_id(1)

        @pl.when(i == 0)
        def _():
            dk_ref[...] = jnp.zeros_like(dk_ref)
            dv_ref[...] = jnp.zeros_like(dv_ref)

        r0 = _iota((LANES, LANES), 0)
        r1 = _iota((LANES, LANES), 1)
        later = jnp.where(r0 > r1, 1.0, 0.0).astype(BF16)
        earlier = jnp.where(r0 < r1, 1.0, 0.0).astype(BF16)

        def step(j, carry):
            rows = pl.ds(pl.multiple_of(j * blk, blk), blk)
            hs = range(hp)
            cols = [pl.ds(hh * dh, dh) for hh in hs]
            zs = [_dot_nt(q_ref[:, cols[hh]], k_ref[rows, cols[hh]]) * scale for hh in hs]
            dws = [_dot_nt(do_ref[:, cols[hh]], v_ref[rows, cols[hh]]) for hh in hs]
            tiles = [_sb_tile(z, i, j, blk, key_lo) for z in zs]
            sufs = [_running_sums(lk, later, reverse=True)[0] for _, _, lk in tiles]
            wgts = [jnp.where(vis, jnp.exp(ls + suf + _lane_pick(c_ref[hh], j)), 0.0)
                    for hh, ((vis, ls, _), suf) in enumerate(zip(tiles, sufs))]
            es = [wgt * dw for wgt, dw in zip(wgts, dws)]
            pres = [_running_sums(e, earlier, reverse=False) for e in es]
            dzs = []
            for hh in hs:
                vis, ls, _ = tiles[hh]
                before = jnp.where(vis, pres[hh][0] + carry[2 * hh + 1], 0.0)
                sig = jnp.exp(ls)
                dzs.append(_bf((es[hh] * (1.0 - sig) - before * sig) * scale))
            dks = [_dot_tn(dzs[hh], q_ref[:, cols[hh]]) for hh in hs]
            dvs = [_dot_tn(_bf(wgts[hh]), do_ref[:, cols[hh]]) for hh in hs]
            dqs = [carry[2 * hh] + _dot(dzs[hh], k_ref[rows, cols[hh]]) for hh in hs]
            out = []
            for hh in hs:
                dk_ref[rows, cols[hh]] += dks[hh]
                dv_ref[rows, cols[hh]] += dvs[hh]
                out += [dqs[hh], carry[2 * hh + 1] + pres[hh][1]]
            return tuple(out)

        res = lax.fori_loop(0, i + 1, step, (jnp.zeros((blk, dh), F32), jnp.zeros((blk, 1), F32)) * hp)
        for hh in range(hp):
            dq_ref[:, pl.ds(hh * dh, dh)] = res[2 * hh]

    full = pl.BlockSpec((t, hp * dh), lambda h, i: (0, h))
    qblk = pl.BlockSpec((blk, hp * dh), lambda h, i: (i, h))
    call = dict(grid=(heads // hp, nq),
                in_specs=[qblk, full, full, qblk, pl.BlockSpec((hp, blk, LANES), lambda h, i: (h, i, 0))],
                out_specs=[qblk, full, full], out_shape=[jax.ShapeDtypeStruct((t, heads * dh), F32)] * 3, name=name)
    if scatter is None:
        return pl.pallas_call(body, compiler_params=_params("parallel", "arbitrary"), **call)(qs, ks, vs, do, carry)
    res = _call_with_exchange(body, scratch_shapes=[], args=(qs, ks, vs, do, carry), srcs=scatter,
                              scatter=[True] * len(scatter), **call)
    return [*res[:3], list(res[3:])]


def _sb_pre_bwd(proj, gq, gk, dq, dk, dv, *, heads, dh, col_q, col_k, name):
    t = proj.shape[0]
    tm = _pick(t, (384, 256, 128))

    def body(q_ref, k_ref, gq_ref, gk_ref, dq_ref, dk_ref, dv_ref, oq_ref, ok_ref, ov_ref, dgq_ref, dgk_ref):
        @pl.when((pl.program_id(0) == 0) & (pl.program_id(1) == 0))
        def _():
            dgq_ref[...] = jnp.zeros_like(dgq_ref)
            dgk_ref[...] = jnp.zeros_like(dgk_ref)

        dq_, gq_r = _rms_bwd(q_ref[...], gq_ref[...], dq_ref[...])
        dk_, gk_r = _rms_bwd(k_ref[...], gk_ref[...], dk_ref[...])
        oq_ref[...] = _bf(dq_)
        ok_ref[...] = _bf(dk_)
        ov_ref[...] = _bf(dv_ref[...])
        dgq_ref[...] += jnp.sum(gq_r, axis=0, keepdims=True)
        dgk_ref[...] += jnp.sum(gk_r, axis=0, keepdims=True)

    def cur(col):
        return pl.BlockSpec((tm, dh), lambda i, h: (i, col // dh + h))

    gspec = pl.BlockSpec((1, dh), lambda i, h: (0, 0))
    ospec = pl.BlockSpec((tm, dh), lambda i, h: (i, h))
    return pl.pallas_call(
        body, grid=(t // tm, heads), in_specs=[cur(col_q), cur(col_k), gspec, gspec, ospec, ospec, ospec],
        out_specs=[ospec, ospec, ospec, gspec, gspec],
        out_shape=[jax.ShapeDtypeStruct((t, heads * dh), BF16)] * 3 + [jax.ShapeDtypeStruct((1, dh), F32)] * 2,
        name=name, compiler_params=_params("arbitrary", "arbitrary"))(proj, proj, gq, gk, dq, dk, dv)


PEERS = N_DEV - 1


def _exchange_copies(ins, outs, send_sems, recv_sems, local_sems, scatter):
    x, y, c = lax.axis_index("x"), lax.axis_index("y"), lax.axis_index("c")
    me = 4 * x + 2 * y + c
    copies = []
    for a in range(len(ins)):
        own = ins[a].at[me] if scatter[a] else ins[a]
        copies.append(pltpu.make_async_copy(own, outs[a].at[me], local_sems.at[a]))
        for k in range(1, N_DEV):
            px = (x + (k >> 2 & 1)) % 2
            py = (y + (k >> 1 & 1)) % 2
            pc = (c + (k & 1)) % 2
            src = ins[a].at[4 * px + 2 * py + pc] if scatter[a] else ins[a]
            copies.append(pltpu.make_async_remote_copy(
                src_ref=src, dst_ref=outs[a].at[me], send_sem=send_sems.at[a * PEERS + k - 1],
                recv_sem=recv_sems.at[a * PEERS + k - 1], device_id=(px, py, pc), device_id_type=MESH))
    return copies


def _exchange_shapes(srcs, scatter):
    return [jax.ShapeDtypeStruct(s.shape if sc else (N_DEV,) + s.shape, s.dtype) for s, sc in zip(srcs, scatter)]


def _exchange_sems(n):
    return [pltpu.SemaphoreType.DMA((n * PEERS,)), pltpu.SemaphoreType.DMA((n * PEERS,)), pltpu.SemaphoreType.DMA((n,))]


def _exchange(srcs, *, scatter, name):
    n = len(srcs)

    def body(*refs):
        copies = _exchange_copies(refs[:n], refs[n:2 * n], *refs[2 * n:], scatter)
        for cp in copies:
            cp.start()
        for cp in copies:
            cp.wait()

    any_spec = pl.BlockSpec(memory_space=pl.ANY)
    return pl.pallas_call(
        body, in_specs=[any_spec] * n, out_specs=[any_spec] * n, out_shape=_exchange_shapes(srcs, scatter),
        scratch_shapes=_exchange_sems(n), name=name,
        compiler_params=pltpu.CompilerParams(has_side_effects=True))(*srcs)


def _gather_two_level(srcs, *, name):
    n = len(srcs)

    def body(*refs):
        ins, outs = refs[:n], refs[n:2 * n]
        send_sems, recv_sems, local_sems = refs[2 * n:]
        x, y, c = lax.axis_index("x"), lax.axis_index("y"), lax.axis_index("c")
        chips = [(1 - x, y), (x, 1 - y), (1 - x, 1 - y)]

        def slab(a, px, py, pc):
            return outs[a].at[4 * px + 2 * py + pc]

        def copy(a, k, block, to, src=None):
            return pltpu.make_async_remote_copy(
                src_ref=slab(a, *block) if src is None else src, dst_ref=slab(a, *block),
                send_sem=send_sems.at[a * PEERS + k], recv_sem=recv_sems.at[a * PEERS + k],
                device_id=to, device_id_type=MESH)

        mine = [pltpu.make_async_copy(ins[a], slab(a, x, y, c), local_sems.at[a]) for a in range(n)]
        first = [copy(a, 0, (x, y, c), (x, y, 1 - c), src=ins[a]) for a in range(n)]
        first += [copy(a, 1 + j, (x, y, c), (*chip, c), src=ins[a]) for j, chip in enumerate(chips) for a in range(n)]
        for cp in mine + first:
            cp.start()
        passed = []
        for j, chip in enumerate(chips):
            for a in range(n):
                copy(a, 1 + j, (*chip, c), (x, y, c)).wait_recv()
                passed.append(copy(a, 4 + j, (*chip, c), (x, y, 1 - c)))
                passed[-1].start()
        for a in range(n):
            copy(a, 0, (x, y, 1 - c), (x, y, c)).wait_recv()
            for j, chip in enumerate(chips):
                copy(a, 4 + j, (*chip, 1 - c), (x, y, c)).wait_recv()
        for cp in first + passed:
            cp.wait_send()
        for cp in mine:
            cp.wait()

    any_spec = pl.BlockSpec(memory_space=pl.ANY)
    return pl.pallas_call(
        body, in_specs=[any_spec] * n, out_specs=[any_spec] * n, out_shape=_exchange_shapes(srcs, [False] * n),
        scratch_shapes=_exchange_sems(n), name=name,
        compiler_params=pltpu.CompilerParams(has_side_effects=True))(*srcs)


def _call_with_exchange(body, *, grid, in_specs, out_specs, out_shape, scratch_shapes, args, srcs, scatter, name):
    n, n_in, n_out, n_scr = len(srcs), len(args), len(out_shape), len(scratch_shapes)

    def full_body(*refs):
        ins, xin = refs[:n_in], refs[n_in:n_in + n]
        outs, xout = refs[n_in + n:n_in + n + n_out], refs[n_in + n + n_out:n_in + 2 * n + n_out]
        scr = refs[n_in + 2 * n + n_out:]
        ids = [pl.program_id(a) for a in range(len(grid))]
        first = functools.reduce(jnp.logical_and, [i == 0 for i in ids])
        last = functools.reduce(jnp.logical_and, [i == g - 1 for i, g in zip(ids, grid)])
        copies = _exchange_copies(xin, xout, *scr[n_scr:], scatter)

        @pl.when(first)
        def _():
            for cp in copies:
                cp.start()

        body(*ins, *outs, *scr[:n_scr])

        @pl.when(last)
        def _():
            for cp in copies:
                cp.wait()

    any_spec = pl.BlockSpec(memory_space=pl.ANY)
    return pl.pallas_call(
        full_body, grid=grid, in_specs=list(in_specs) + [any_spec] * n, out_specs=list(out_specs) + [any_spec] * n,
        out_shape=list(out_shape) + _exchange_shapes(srcs, scatter),
        scratch_shapes=list(scratch_shapes) + _exchange_sems(n), name=name,
        compiler_params=pltpu.CompilerParams(dimension_semantics=("arbitrary",) * len(grid),
                                             vmem_limit_bytes=V7X_VMEM_LIMIT_BYTES, has_side_effects=True))(*args, *srcs)


def _adam_math(g, w, m, v):
    m2 = ADAM_B1 * m + (1.0 - ADAM_B1) * g
    v2 = ADAM_B2 * v + (1.0 - ADAM_B2) * (g * g)
    m_hat = m2 / (1.0 - ADAM_B1 ** ADAM_STEP)
    v_hat = v2 / (1.0 - ADAM_B2 ** ADAM_STEP)
    return -ADAM_LR * (m_hat / (jnp.sqrt(v_hat) + ADAM_EPS) + ADAM_WD * w), m2, v2


def _adamw_slabs(slabs, w, m, v, *, name):
    r, c = w.shape
    tr = next((t for t in (2256, 752, 512, 240, 128, 64, 32, 16) if r % t == 0), r)

    def body(s_ref, w_ref, m_ref, v_ref, g_ref, d_ref, mo_ref, vo_ref):
        g = s_ref[0].astype(F32)
        for p in range(1, N_DEV):
            g = g + s_ref[p].astype(F32)
        g_ref[...] = g
        d_ref[...], mo_ref[...], vo_ref[...] = _adam_math(g, w_ref[...], m_ref[...], v_ref[...])

    spec = pl.BlockSpec((tr, c), lambda i: (i, 0))
    return pl.pallas_call(
        body, grid=(r // tr,), in_specs=[pl.BlockSpec((N_DEV, tr, c), lambda i: (0, i, 0)), spec, spec, spec],
        out_specs=[spec] * 4, out_shape=[jax.ShapeDtypeStruct((r, c), F32)] * 4, name=name,
        compiler_params=_params("parallel"))(slabs, w, m, v)


def _adamw_small(g, w, m, v, *, name):
    def body(g_ref, w_ref, m_ref, v_ref, d_ref, mo_ref, vo_ref):
        d_ref[...], mo_ref[...], vo_ref[...] = _adam_math(g_ref[...], w_ref[...], m_ref[...], v_ref[...])

    return pl.pallas_call(body, out_shape=[jax.ShapeDtypeStruct(w.shape, F32)] * 3, name=name)(g, w, m, v)


def _sum_slabs(slabs, *, name):
    def body(s_ref, o_ref):
        acc = s_ref[0]
        for p in range(1, N_DEV):
            acc = acc + s_ref[p]
        o_ref[...] = acc

    return pl.pallas_call(body, out_shape=jax.ShapeDtypeStruct(slabs.shape[1:], F32), name=name)(slabs)


def _pad_lanes(a):
    return jnp.pad(a, ((0, 0), (0, LANES - a.shape[1])))


def _w_in_slabs(main, ab, col_ab, n_ab):
    pw = main.shape[0] + n_ab
    c = pw // N_DEV
    parts = [(0, col_ab, main, 0), (col_ab, col_ab + n_ab, ab, col_ab), (col_ab + n_ab, pw, main, n_ab)]
    slabs = []
    for p in range(N_DEV):
        pieces = []
        for lo, hi, src, shift in parts:
            a, b = max(lo, c * p), min(hi, c * (p + 1))
            if a < b:
                pieces.append(src[a - shift:b - shift])
        slabs.append(jnp.concatenate(pieces, axis=0))
    return jnp.stack(slabs)


def _local_step(x, target, meta, g_mix, wt_main, wt_ab, cq, ck, cv, a_log, dt_bias, g_dn, g_sbq, g_sbk, g_ffn, rest,
                shards=False):
    seq, d = x.shape
    n_meta = meta.shape[0]
    heads = a_log.shape[1]
    qk = cq.shape[1]
    dvt = cv.shape[1]
    dk, dv = qk // heads, dvt // heads
    dh = g_sbq.shape[1]
    sbw = rest[2].shape[0] * N_DEV if shards else rest[1].shape[0]
    sb_heads = sbw // dh
    pad_l = (-n_meta) % CHUNK
    row_x = pad_l + n_meta
    rows = row_x + seq
    t = -(-rows // GDN_ROWS) * GDN_ROWS
    col_q, col_k, col_v, col_z = 0, qk, 2 * qk, 2 * qk + dvt
    col_sq = 2 * qk + 2 * dvt
    col_sk, col_sv, col_gd, col_gs = col_sq + sbw, col_sq + 2 * sbw, col_sq + 3 * sbw, col_sq + 3 * sbw + d

    def rows_pad(a):
        return jnp.concatenate([jnp.zeros((row_x, d), F32), a, jnp.zeros((t - rows, d), F32)], axis=0)

    h0 = jnp.concatenate([jnp.zeros((pad_l, d), F32), meta, x, jnp.zeros((t - rows, d), F32)], axis=0)
    tgt = rows_pad(target)
    a_log_p, dt_p = _pad_lanes(a_log), _pad_lanes(dt_bias)

    proj, n1 = _mm_norm(h0, g_mix, wt_main, name="proj")
    pab = _mm_nt(n1, wt_ab, out_dtype=F32, name="proj_ab")
    gk = dict(heads=heads, dk=dk, dv=dv, col_q=col_q, col_k=col_k, col_v=col_v, row_lo=pad_l, row_hi=rows)
    qn, kn, vv, g, beta = _gdn_pre(proj, pab, cq, ck, cv, a_log_p, dt_p, name="gdn_pre", **gk)
    u, w, pm, qd, kd, egl, tinv = _gdn_prep(qn, kn, vv, g, beta, name="gdn_prep")
    o_raw, o_dn, states = _gdn_scan(u, w, pm, qd, kd, egl, proj, g_dn, col_z=col_z, name="gdn_scan")
    qs, ks, vs = _sb_pre(proj, g_sbq, g_sbk, heads=sb_heads, dh=dh, col_q=col_sq, col_k=col_sk, col_v=col_sv,
                         name="sb_pre")
    if shards:
        o_sb, carry, (g_fi, g_bd, g_bs, g_out, g_fo) = _sb_fwd(qs, ks, vs, heads=sb_heads, dh=dh, key_lo=pad_l,
                                                                name="sb_fwd", gather=list(rest))
        wt_fi = g_fi.reshape(-1, d)
        d_ff = wt_fi.shape[0] // 2
        w_bd, w_bs, w_out, wt_fg, wt_fu, w_fo = (g_bd.reshape(-1, d), g_bs.reshape(-1, d), g_out.reshape(-1, d),
                                                 wt_fi[:d_ff], wt_fi[d_ff:], g_fo.reshape(-1, d))
    else:
        o_sb, carry = _sb_fwd(qs, ks, vs, heads=sb_heads, dh=dh, key_lo=pad_l, name="sb_fwd")
        w_bd, w_bs, w_out, wt_fg, wt_fu, w_fo = rest
    merged, br_dn, br_sb = _merge_fwd(o_dn, o_sb, w_bd, w_bs, proj, col_gd=col_gd, col_gs=col_gs, name="merge")
    h1 = _mm_res(h0, merged, w_out, name="mix_out")
    gate, up, act, n2 = _mm_norm_swiglu(h1, g_ffn, wt_fg, wt_fu, name="ffn_in")
    dy, dyb, lsum = _mm_res_loss(h1, act, w_fo, tgt, row0=row_x, nrows=seq, name="ffn_out_loss")

    dgate, dup = _swiglu_bwd(dyb, w_fo, gate, up, name="ffn_out_bwd")
    d_w_fo = _mm_tn(act, dyb, name="dw_ffn_out")
    d_wt_fg = _mm_tn(dgate, n2, name="dw_ffn_gate")
    d_wt_fu = _mm_tn(dup, n2, name="dw_ffn_up")
    dh1, dh1b, d_g_ffn = _mm_rmsbwd([(dgate, wt_fg), (dup, wt_fu)], None, h1, g_ffn, dy, name="ffn_in_bwd")

    dbd, dbs, dgd, dgs = _merge_bwd(dh1b, w_out, proj, br_dn, br_sb, col_gd=col_gd, col_gs=col_gs, name="mix_out_bwd")
    d_w_out = _mm_tn(merged, dh1b, name="dw_out")
    d_w_bd = _mm_tn(o_dn, dbd, name="dw_branch_dn")
    d_w_bs = _mm_tn(o_sb, dbs, name="dw_branch_sb")
    do_dn = _mm_nt(dbd, w_bd, out_dtype=F32, name="branch_dn_bwd")
    do_sb = _mm_nt(dbs, w_bs, out_dtype=BF16, name="branch_sb_bwd")

    do_raw, dz, d_g_dn = _gdn_post_bwd(o_raw, proj, g_dn, do_dn, col_z=col_z, name="gdn_post_bwd")
    du, dw, dp, dqd, dkd, dgl = _gdn_bwd_scan(u, w, pm, qd, kd, egl, states, do_raw, name="gdn_bwd_scan")
    dqn, dkn, dvv, dg, dbeta = _gdn_bwd_prep(qn, kn, vv, g, beta, tinv, u, w, du, dw, dp, dqd, dkd, dgl,
                                            name="gdn_bwd_prep")
    dcq, dck, dcv, dpab, d_a_log, d_dt = _gdn_pre_bwd_a(proj, pab, cq, ck, cv, a_log_p, dt_p, dqn, dkn, dvv, dg, dbeta,
                                                        name="gdn_pre_bwd", **gk)
    dpq, d_cq = _conv_bwd(proj, dcq, cq, heads=heads, width=dk, col=col_q, name="conv_q_bwd")
    dpk, d_ck = _conv_bwd(proj, dck, ck, heads=heads, width=dk, col=col_k, name="conv_k_bwd")
    dpv, d_cv = _conv_bwd(proj, dcv, cv, heads=heads, width=dv, col=col_v, name="conv_v_bwd")

    early = None
    if shards:
        slabs = [_bf(jnp.concatenate([d_wt_fg, d_wt_fu], axis=0)).reshape(N_DEV, -1, LANES)]
        slabs += [_bf(a).reshape(N_DEV, -1, d) for a in (d_w_bd, d_w_bs, d_w_out, d_w_fo)]
        dqs, dks, dvs, early = _sb_bwd(qs, ks, vs, do_sb, carry, heads=sb_heads, dh=dh, key_lo=pad_l, name="sb_bwd",
                                       scatter=slabs)
    else:
        dqs, dks, dvs = _sb_bwd(qs, ks, vs, do_sb, carry, heads=sb_heads, dh=dh, key_lo=pad_l, name="sb_bwd")
    dsq, dsk, dsv, d_g_sbq, d_g_sbk = _sb_pre_bwd(proj, g_sbq, g_sbk, dqs, dks, dvs, heads=sb_heads, dh=dh,
                                                   col_q=col_sq, col_k=col_sk, name="sb_pre_bwd")

    dproj = jnp.concatenate([dpq, dpk, dpv, dz, dsq, dsk, dsv, dgd, dgs], axis=1)
    dpab_b = _bf(dpab)
    d_wt_main = _mm_tn(dproj, n1, name="dw_in_main")
    d_wt_ab = _mm_tn(dpab_b, n1, name="dw_in_ab")
    s_in = None
    if shards:
        slabs = _w_in_slabs(_bf(d_wt_main), _bf(d_wt_ab), col_sq, 2 * heads).reshape(N_DEV, -1, LANES)
        dh0, _, d_g_mix, (s_in,) = _mm_rmsbwd([(dproj, wt_main)], (dpab_b, wt_ab), h0, g_mix, dh1, name="proj_bwd",
                                              scatter=[slabs])
    else:
        dh0, _, d_g_mix = _mm_rmsbwd([(dproj, wt_main)], (dpab_b, wt_ab), h0, g_mix, dh1, name="proj_bwd")

    return dict(s_in=s_in, lsum=lsum, grad_x=dh0[row_x:rows], d_meta=dh0[pad_l:row_x], d_g_mix=d_g_mix,
                d_wt_main=d_wt_main, d_wt_ab=d_wt_ab, d_cq=d_cq, d_ck=d_ck, d_cv=d_cv, d_a_log=d_a_log[:, :heads],
                d_dt=d_dt[:, :heads], d_g_dn=d_g_dn, d_g_sbq=d_g_sbq, d_g_sbk=d_g_sbk, d_w_bd=d_w_bd, d_w_bs=d_w_bs,
                d_w_out=d_w_out, d_g_ffn=d_g_ffn, d_wt_fg=d_wt_fg, d_wt_fu=d_wt_fu, d_w_fo=d_w_fo, early=early)


def _pack(parts):
    flat = []
    for a in parts:
        a = a.reshape(-1)
        flat.append(jnp.pad(a, (0, (-a.shape[0]) % LANES)))
    v = jnp.concatenate(flat)
    v = jnp.pad(v, (0, (-v.shape[0]) % (8 * LANES)))
    return v.reshape(-1, LANES)


def _unpack(packed, shapes):
    flat = packed.reshape(-1)
    out, pos = [], 0
    for s in shapes:
        n = math.prod(s)
        out.append(flat[pos:pos + n].reshape(s))
        pos += n + (-n) % LANES
    return out


def kernel(x, meta_tokens, norm_mix_gain, w_in, conv_q, conv_k, conv_v, dn_a_log, dn_dt_bias, dn_out_norm_gain, sb_q_norm_gain, sb_k_norm_gain, w_branch_dn, w_branch_sb, w_out, norm_ffn_gain, w_ffn_in, w_ffn_out, loss_target, m_meta_tokens, m_norm_mix_gain, m_w_in, m_conv_q, m_conv_k, m_conv_v, m_dn_a_log, m_dn_dt_bias, m_dn_out_norm_gain, m_sb_q_norm_gain, m_sb_k_norm_gain, m_w_branch_dn, m_w_branch_sb, m_w_out, m_norm_ffn_gain, m_w_ffn_in, m_w_ffn_out, v_meta_tokens, v_norm_mix_gain, v_w_in, v_conv_q, v_conv_k, v_conv_v, v_dn_a_log, v_dn_dt_bias, v_dn_out_norm_gain, v_sb_q_norm_gain, v_sb_k_norm_gain, v_w_branch_dn, v_w_branch_sb, v_w_out, v_norm_ffn_gain, v_w_ffn_in, v_w_ffn_out):
    me = 4 * lax.axis_index("x") + 2 * lax.axis_index("y") + lax.axis_index("c")
    heads = dn_a_log.shape[1]
    d = x.shape[2]
    qk = conv_q.shape[2] * N_DEV
    dvt = conv_v.shape[2] * N_DEV
    col_ab = 2 * qk + 2 * dvt

    small_shapes = [meta_tokens.shape, conv_q.shape[1:], conv_k.shape[1:], conv_v.shape[1:]]
    small = _pack([meta_tokens, conv_q[0], conv_k[0], conv_v[0]])
    def features_major(a):
        return jnp.transpose(a, (2, 0, 1)).reshape(a.shape[2], a.shape[1])

    g_in, g_small = _gather_two_level([_bf(features_major(w_in)), small], name="gather_w_in")
    wt_full = g_in.reshape(-1, d)
    wt_main = jnp.concatenate([wt_full[:col_ab], wt_full[col_ab + 2 * heads:]], axis=0)
    wt_ab = jnp.pad(wt_full[col_ab:col_ab + 2 * heads], ((0, LANES - 2 * heads), (0, 0)))
    parts = [_unpack(g_small[p], small_shapes) for p in range(N_DEV)]
    meta_f, cq_f, ck_f, cv_f = (jnp.concatenate([parts[p][a] for p in range(N_DEV)], axis=1) for a in range(4))

    r = _local_step(x[0], loss_target[0], meta_f, norm_mix_gain, wt_main, wt_ab, cq_f, ck_f, cv_f, dn_a_log, dn_dt_bias,
                    dn_out_norm_gain, sb_q_norm_gain, sb_k_norm_gain, norm_ffn_gain,
                    (_bf(features_major(w_ffn_in)), _bf(w_branch_dn[0]), _bf(w_branch_sb[0]), _bf(w_out[0]),
                     _bf(w_ffn_out[0])), shards=True)
    s_fi, s_bd, s_bs, s_out, s_fo = r["early"]
    s_in = r["s_in"]

    loss_part = (0.5 / d) * jnp.sum(r["lsum"], axis=1, keepdims=True)
    small_g = [r["d_meta"], r["d_g_mix"], r["d_cq"], r["d_ck"], r["d_cv"], r["d_a_log"], r["d_dt"], r["d_g_dn"],
               r["d_g_sbq"], r["d_g_sbk"], r["d_g_ffn"], loss_part]
    (g_packs,) = _exchange([_pack(small_g)], scatter=[False], name="gather_small_grads")
    (g_meta, g_mix, g_cq, g_ck, g_cv, g_al, g_dt, g_gdn, g_sbq, g_sbk, g_ffn, loss) = _unpack(
        _sum_slabs(g_packs, name="sum_small_grads"), [a.shape for a in small_g])

    def mine(a, width):
        return lax.dynamic_slice_in_dim(a, me * width, width, axis=1)

    big = dict(w_in=(s_in, w_in, m_w_in, v_w_in), w_branch_dn=(s_bd, w_branch_dn, m_w_branch_dn, v_w_branch_dn),
               w_branch_sb=(s_bs, w_branch_sb, m_w_branch_sb, v_w_branch_sb), w_out=(s_out, w_out, m_w_out, v_w_out),
               w_ffn_in=(s_fi, w_ffn_in, m_w_ffn_in, v_w_ffn_in), w_ffn_out=(s_fo, w_ffn_out, m_w_ffn_out, v_w_ffn_out))
    tiny = dict(meta_tokens=(mine(g_meta, d // N_DEV), meta_tokens, m_meta_tokens, v_meta_tokens),
                norm_mix_gain=(g_mix, norm_mix_gain, m_norm_mix_gain, v_norm_mix_gain),
                conv_q=(mine(g_cq, qk // N_DEV), conv_q[0], m_conv_q[0], v_conv_q[0]),
                conv_k=(mine(g_ck, qk // N_DEV), conv_k[0], m_conv_k[0], v_conv_k[0]),
                conv_v=(mine(g_cv, dvt // N_DEV), conv_v[0], m_conv_v[0], v_conv_v[0]),
                dn_a_log=(g_al, dn_a_log, m_dn_a_log, v_dn_a_log), dn_dt_bias=(g_dt, dn_dt_bias, m_dn_dt_bias, v_dn_dt_bias),
                dn_out_norm_gain=(g_gdn, dn_out_norm_gain, m_dn_out_norm_gain, v_dn_out_norm_gain),
                sb_q_norm_gain=(g_sbq, sb_q_norm_gain, m_sb_q_norm_gain, v_sb_q_norm_gain),
                sb_k_norm_gain=(g_sbk, sb_k_norm_gain, m_sb_k_norm_gain, v_sb_k_norm_gain),
                norm_ffn_gain=(g_ffn, norm_ffn_gain, m_norm_ffn_gain, v_norm_ffn_gain))
    order = ["meta_tokens", "norm_mix_gain", "w_in", "conv_q", "conv_k", "conv_v", "dn_a_log", "dn_dt_bias",
             "dn_out_norm_gain", "sb_q_norm_gain", "sb_k_norm_gain", "w_branch_dn", "w_branch_sb", "w_out",
             "norm_ffn_gain", "w_ffn_in", "w_ffn_out"]
    grads, deltas, new_m, new_v = [], [], [], []
    for name in order:
        if name in ("w_in", "w_ffn_in"):
            slabs, w, m, v = big[name]
            res = _adamw_slabs(slabs, *(features_major(a).reshape(-1, LANES) for a in (w, m, v)), name="adamw_" + name)
            g, dl, mo, vo = (jnp.transpose(a.reshape(w.shape[2], 1, w.shape[1]), (1, 2, 0)) for a in res)
            like = w.shape
        elif name in big:
            slabs, w, m, v = big[name]
            g, dl, mo, vo = _adamw_slabs(slabs, w[0], m[0], v[0], name="adamw_" + name)
            like = w.shape
        else:
            g, w, m, v = tiny[name]
            like = dict(conv_q=conv_q, conv_k=conv_k, conv_v=conv_v).get(name, w).shape
            dl, mo, vo = _adamw_small(g, w, m, v, name="adamw_" + name)
        for lst, a in ((grads, g), (deltas, dl), (new_m, mo), (new_v, vo)):
            lst.append(a.reshape(like))
    return (loss.reshape(()), r["grad_x"][None], *grads, *deltas, *new_m, *new_v)
```

```python
import functools
import math

import jax
import jax.numpy as jnp
from jax import lax
from jax.experimental import pallas as pl
from jax.experimental.pallas import tpu as pltpu

F32 = jnp.float32
BF16 = jnp.bfloat16

N_DEV = 8
CHUNK = 64
CHUNK_SHIFT = 6
GDN_ROWS = 2 * CHUNK
SB_BLOCK = 128
SB_HEADS_PER_STEP = 2
LANES = 128
RMS_EPS = 1e-6
L2_EPS = 1e-6
ADAM_LR = 0.001
ADAM_B1 = 0.9
ADAM_B2 = 0.999
ADAM_EPS = 1e-08
ADAM_WD = 0.01
ADAM_STEP = 10
V7X_VMEM_LIMIT_BYTES = 56 * 1024 * 1024
MM_TN_OUT_BLOCK_BYTES = 6 * 1024 * 1024
ROWS_BIG = (1056, 512, 384, 256, 128)
ROWS_MID = (528, 384, 256, 128)
SCAN_HEADS = 4

MESH = pl.DeviceIdType.MESH


def _params(*sem):
    return pltpu.CompilerParams(dimension_semantics=sem or None, vmem_limit_bytes=V7X_VMEM_LIMIT_BYTES)


def _pick(n, cands):
    for c in cands:
        if n % c == 0:
            return c
    raise ValueError(f"no block size among {cands} divides {n}")


def _bf(x):
    return x.astype(BF16)


def _dot(a, b):
    return jnp.dot(a, b, preferred_element_type=F32)


def _dot_nt(a, b):
    return lax.dot_general(a, b, (((1,), (1,)), ((), ())), preferred_element_type=F32)


def _dot_tn(a, b):
    return lax.dot_general(a, b, (((0,), (0,)), ((), ())), preferred_element_type=F32)


def _split2(x):
    hi = _bf(x)
    return hi, _bf(x - hi.astype(F32))


def _split3(x):
    hi = _bf(x)
    r = x - hi.astype(F32)
    mid = _bf(r)
    return hi, mid, _bf(r - mid.astype(F32))


def _dot_hp(a, b, dot=_dot):
    ah, al = _split2(a)
    bh, bl = _split2(b)
    return dot(ah, bh) + dot(ah, bl) + dot(al, bh)


def _dot_exact_l(m, x, dot=_dot):
    h, mi, lo = _split3(x)
    return dot(m, h) + dot(m, mi) + dot(m, lo)


def _sigmoid(x):
    return 1.0 / (1.0 + jnp.exp(-x))


def _silu(x):
    return x * _sigmoid(x)


def _silu_grad(x):
    s = _sigmoid(x)
    return s * (1.0 + x * (1.0 - s))


def _softplus(x):
    return jnp.maximum(x, 0.0) + jnp.log(1.0 + jnp.exp(-jnp.abs(x)))


def _rms_fwd(h, gain):
    r = lax.rsqrt(jnp.mean(h * h, axis=-1, keepdims=True) + RMS_EPS)
    return h * r * gain


def _rms_bwd(h, gain, dy):
    r = lax.rsqrt(jnp.mean(h * h, axis=-1, keepdims=True) + RMS_EPS)
    dyg = dy * gain
    dh = r * dyg - h * (r * r * r) * jnp.mean(dyg * h, axis=-1, keepdims=True)
    return dh, dy * h * r


def _iota(shape, dim):
    return lax.broadcasted_iota(jnp.int32, shape, dim)


def _lane_pick(x, idx):
    return jnp.sum(jnp.where(_iota(x.shape, 1) == idx, x, 0.0), axis=1, keepdims=True)


def _mm_nt(a, b, *, out_dtype, name):
    m, k = a.shape
    n = b.shape[0]
    tm, tn = _pick(m, ROWS_BIG), _pick(n, (1024, 512, 256, 128))

    def body(a_ref, b_ref, o_ref):
        o_ref[...] = _dot_nt(a_ref[...], b_ref[...]).astype(out_dtype)

    return pl.pallas_call(
        body, grid=(m // tm, n // tn),
        in_specs=[pl.BlockSpec((tm, k), lambda i, j: (i, 0)), pl.BlockSpec((tn, k), lambda i, j: (j, 0))],
        out_specs=pl.BlockSpec((tm, tn), lambda i, j: (i, j)),
        out_shape=jax.ShapeDtypeStruct((m, n), out_dtype), name=name,
        compiler_params=_params("parallel", "parallel"))(a, b)


def _mm_tn(a, b, *, name):
    t, m = a.shape
    n = b.shape[1]
    tn = _pick(n, (2816, 2048, 1408, 1024, 512, 256, 128))
    tm = _pick(m, tuple(c for c in (1408, 1024, 512, 256, 128) if c * tn * 4 <= MM_TN_OUT_BLOCK_BYTES))
    tk = _pick(t, (1408, 1024, 512, 384, 256, 128))

    def body(a_ref, b_ref, o_ref):
        @pl.when(pl.program_id(2) == 0)
        def _():
            o_ref[...] = jnp.zeros_like(o_ref)

        o_ref[...] += _dot_tn(a_ref[...], b_ref[...])

    return pl.pallas_call(
        body, grid=(m // tm, n // tn, t // tk),
        in_specs=[pl.BlockSpec((tk, tm), lambda i, j, k: (k, i)), pl.BlockSpec((tk, tn), lambda i, j, k: (k, j))],
        out_specs=pl.BlockSpec((tm, tn), lambda i, j, k: (i, j)),
        out_shape=jax.ShapeDtypeStruct((m, n), F32), name=name,
        compiler_params=_params("parallel", "parallel", "arbitrary"))(a, b)


def _mm_norm(h, gain, wt, *, name):
    m, k = h.shape
    n = wt.shape[0]
    tm, tn = _pick(m, ROWS_BIG), _pick(n, (1024, 512, 256, 128))

    def body(h_ref, g_ref, w_ref, o_ref, n_ref):
        @pl.when(pl.program_id(1) == 0)
        def _():
            n_ref[...] = _bf(_rms_fwd(h_ref[...], g_ref[...]))

        o_ref[...] = _dot_nt(n_ref[...], w_ref[...])

    return pl.pallas_call(
        body, grid=(m // tm, n // tn),
        in_specs=[pl.BlockSpec((tm, k), lambda i, j: (i, 0)), pl.BlockSpec((1, k), lambda i, j: (0, 0)),
                  pl.BlockSpec((tn, k), lambda i, j: (j, 0))],
        out_specs=[pl.BlockSpec((tm, tn), lambda i, j: (i, j)), pl.BlockSpec((tm, k), lambda i, j: (i, 0))],
        out_shape=[jax.ShapeDtypeStruct((m, n), F32), jax.ShapeDtypeStruct((m, k), BF16)], name=name,
        compiler_params=_params("parallel", "arbitrary"))(h, gain, wt)


def _mm_norm_swiglu(h, gain, wgt, wut, *, name):
    m, k = h.shape
    n = wgt.shape[0]
    tm, tn = _pick(m, ROWS_MID), _pick(n, (1408, 1024, 512, 256, 128))

    def body(h_ref, g_ref, wg_ref, wu_ref, gate_ref, up_ref, act_ref, n_ref):
        @pl.when(pl.program_id(1) == 0)
        def _():
            n_ref[...] = _bf(_rms_fwd(h_ref[...], g_ref[...]))

        gate = _dot_nt(n_ref[...], wg_ref[...])
        up = _dot_nt(n_ref[...], wu_ref[...])
        gate_ref[...] = gate
        up_ref[...] = up
        act_ref[...] = _bf(_silu(gate) * up)

    wspec = pl.BlockSpec((tn, k), lambda i, j: (j, 0))
    ospec = pl.BlockSpec((tm, tn), lambda i, j: (i, j))
    return pl.pallas_call(
        body, grid=(m // tm, n // tn),
        in_specs=[pl.BlockSpec((tm, k), lambda i, j: (i, 0)), pl.BlockSpec((1, k), lambda i, j: (0, 0)), wspec, wspec],
        out_specs=[ospec, ospec, ospec, pl.BlockSpec((tm, k), lambda i, j: (i, 0))],
        out_shape=[jax.ShapeDtypeStruct((m, n), F32), jax.ShapeDtypeStruct((m, n), F32),
                   jax.ShapeDtypeStruct((m, n), BF16), jax.ShapeDtypeStruct((m, k), BF16)], name=name,
        compiler_params=_params("parallel", "arbitrary"))(h, gain, wgt, wut)


def _mm_res(res, a, b, *, name):
    m, k = a.shape
    n = b.shape[1]
    tm, tn = _pick(m, ROWS_BIG), _pick(n, (1024, 512, 256, 128))

    def body(r_ref, a_ref, b_ref, o_ref):
        o_ref[...] = r_ref[...] + _dot(a_ref[...], b_ref[...])

    return pl.pallas_call(
        body, grid=(m // tm, n // tn),
        in_specs=[pl.BlockSpec((tm, tn), lambda i, j: (i, j)), pl.BlockSpec((tm, k), lambda i, j: (i, 0)),
                  pl.BlockSpec((k, tn), lambda i, j: (0, j))],
        out_specs=pl.BlockSpec((tm, tn), lambda i, j: (i, j)),
        out_shape=jax.ShapeDtypeStruct((m, n), F32), name=name,
        compiler_params=_params("parallel", "parallel"))(res, a, b)


def _mm_res_loss(res, a, b, target, *, row0, nrows, name):
    m, k = a.shape
    n = b.shape[1]
    tm = _pick(m, ROWS_MID)

    def body(r_ref, a_ref, b_ref, t_ref, dy_ref, dyb_ref, ls_ref):
        i = pl.program_id(0)

        @pl.when(i == 0)
        def _():
            ls_ref[...] = jnp.zeros_like(ls_ref)

        y = r_ref[...] + _dot(a_ref[...], b_ref[...])
        row = i * tm + _iota((tm, n), 0)
        e = jnp.where((row >= row0) & (row < row0 + nrows), y - t_ref[...], 0.0)
        dy = e / n
        dy_ref[...] = dy
        dyb_ref[...] = _bf(dy)
        ls_ref[...] += jnp.sum(e * e, axis=0, keepdims=True)

    rspec = pl.BlockSpec((tm, n), lambda i: (i, 0))
    return pl.pallas_call(
        body, grid=(m // tm,),
        in_specs=[rspec, pl.BlockSpec((tm, k), lambda i: (i, 0)), pl.BlockSpec((k, n), lambda i: (0, 0)), rspec],
        out_specs=[rspec, rspec, pl.BlockSpec((1, n), lambda i: (0, 0))],
        out_shape=[jax.ShapeDtypeStruct((m, n), F32), jax.ShapeDtypeStruct((m, n), BF16),
                   jax.ShapeDtypeStruct((1, n), F32)], name=name,
        compiler_params=_params("arbitrary"))(res, a, b, target)


def _merge_fwd(o_dn, o_sb, wbd, wbs, proj, *, col_gd, col_gs, name):
    m, kd = o_dn.shape
    ks = o_sb.shape[1]
    n = wbd.shape[1]
    tm = _pick(m, ROWS_BIG)
    tn = _pick(math.gcd(n, math.gcd(col_gd, col_gs)), (512, 256, 128))

    def body(od_ref, os_ref, wd_ref, ws_ref, gd_ref, gs_ref, mg_ref, bd_ref, bs_ref):
        bd = _dot(od_ref[...], wd_ref[...])
        bs = _dot(os_ref[...], ws_ref[...])
        bd_ref[...] = bd
        bs_ref[...] = bs
        mg_ref[...] = _bf(_sigmoid(gd_ref[...]) * bd + _sigmoid(gs_ref[...]) * bs)

    ospec = pl.BlockSpec((tm, tn), lambda i, j: (i, j))
    return pl.pallas_call(
        body, grid=(m // tm, n // tn),
        in_specs=[pl.BlockSpec((tm, kd), lambda i, j: (i, 0)), pl.BlockSpec((tm, ks), lambda i, j: (i, 0)),
                  pl.BlockSpec((kd, tn), lambda i, j: (0, j)), pl.BlockSpec((ks, tn), lambda i, j: (0, j)),
                  pl.BlockSpec((tm, tn), lambda i, j: (i, col_gd // tn + j)),
                  pl.BlockSpec((tm, tn), lambda i, j: (i, col_gs // tn + j))],
        out_specs=[ospec, ospec, ospec],
        out_shape=[jax.ShapeDtypeStruct((m, n), BF16), jax.ShapeDtypeStruct((m, n), F32),
                   jax.ShapeDtypeStruct((m, n), F32)], name=name,
        compiler_params=_params("parallel", "parallel"))(o_dn, o_sb, wbd, wbs, proj, proj)


def _merge_bwd(dh, w_out, proj, br_dn, br_sb, *, col_gd, col_gs, name):
    m, k = dh.shape
    n = w_out.shape[0]
    tm = _pick(m, ROWS_BIG)
    tn = _pick(math.gcd(n, math.gcd(col_gd, col_gs)), (512, 256, 128))

    def body(dh_ref, w_ref, gd_ref, gs_ref, bd_ref, bs_ref, dbd_ref, dbs_ref, dgd_ref, dgs_ref):
        dm = _dot_nt(dh_ref[...], w_ref[...])
        sd = _sigmoid(gd_ref[...])
        ss = _sigmoid(gs_ref[...])
        dbd_ref[...] = _bf(dm * sd)
        dbs_ref[...] = _bf(dm * ss)
        dgd_ref[...] = _bf(dm * bd_ref[...] * sd * (1.0 - sd))
        dgs_ref[...] = _bf(dm * bs_ref[...] * ss * (1.0 - ss))

    ospec = pl.BlockSpec((tm, tn), lambda i, j: (i, j))
    return pl.pallas_call(
        body, grid=(m // tm, n // tn),
        in_specs=[pl.BlockSpec((tm, k), lambda i, j: (i, 0)), pl.BlockSpec((tn, k), lambda i, j: (j, 0)),
                  pl.BlockSpec((tm, tn), lambda i, j: (i, col_gd // tn + j)),
                  pl.BlockSpec((tm, tn), lambda i, j: (i, col_gs // tn + j)), ospec, ospec],
        out_specs=[ospec] * 4,
        out_shape=[jax.ShapeDtypeStruct((m, n), BF16)] * 4, name=name,
        compiler_params=_params("parallel", "parallel"))(dh, w_out, proj, proj, br_dn, br_sb)


def _swiglu_bwd(dy, wfo, gate, up, *, name):
    m, k = dy.shape
    n = wfo.shape[0]
    tm, tn = _pick(m, ROWS_MID), _pick(n, (1408, 1024, 512, 256, 128))

    def body(dy_ref, w_ref, g_ref, u_ref, dg_ref, du_ref):
        da = _dot_nt(dy_ref[...], w_ref[...])
        g = g_ref[...]
        dg_ref[...] = _bf(da * u_ref[...] * _silu_grad(g))
        du_ref[...] = _bf(da * _silu(g))

    ospec = pl.BlockSpec((tm, tn), lambda i, j: (i, j))
    return pl.pallas_call(
        body, grid=(m // tm, n // tn),
        in_specs=[pl.BlockSpec((tm, k), lambda i, j: (i, 0)), pl.BlockSpec((tn, k), lambda i, j: (j, 0)), ospec, ospec],
        out_specs=[ospec, ospec], out_shape=[jax.ShapeDtypeStruct((m, n), BF16)] * 2, name=name,
        compiler_params=_params("parallel", "parallel"))(dy, wfo, gate, up)


def _mm_rmsbwd(pairs, extra, h, gain, dres, *, name, scatter=None):
    m, k = pairs[0][0].shape
    n = h.shape[1]
    tm = _pick(m, ROWS_MID)
    tk = _pick(k, (1408, 1024, 512, 256, 128))
    nk = k // tk
    np_ = len(pairs)

    def body(*refs):
        ab = refs[:2 * np_]
        pos = 2 * np_
        ex = refs[pos:pos + 2] if extra is not None else ()
        pos += len(ex)
        h_ref, g_ref, r_ref, dh_ref, dhb_ref, dg_ref, acc_ref = refs[pos:]
        i, kk = pl.program_id(0), pl.program_id(1)

        @pl.when((i == 0) & (kk == 0))
        def _():
            dg_ref[...] = jnp.zeros_like(dg_ref)

        part = _dot(ab[0][...], ab[1][...])
        for p in range(1, np_):
            part += _dot(ab[2 * p][...], ab[2 * p + 1][...])

        @pl.when(kk == 0)
        def _():
            first = part
            if ex:
                first = first + _dot(ex[0][...], ex[1][...])
            acc_ref[...] = first

        @pl.when(kk > 0)
        def _():
            acc_ref[...] += part

        @pl.when(kk == nk - 1)
        def _():
            dh, dgr = _rms_bwd(h_ref[...], g_ref[...], acc_ref[...])
            dh = dh + r_ref[...]
            dh_ref[...] = dh
            dhb_ref[...] = _bf(dh)
            dg_ref[...] += jnp.sum(dgr, axis=0, keepdims=True)

    in_specs, args = [], []
    for a, b in pairs:
        in_specs += [pl.BlockSpec((tm, tk), lambda i, kk: (i, kk)), pl.BlockSpec((tk, n), lambda i, kk: (kk, 0))]
        args += [a, b]
    if extra is not None:
        k2 = extra[0].shape[1]
        in_specs += [pl.BlockSpec((tm, k2), lambda i, kk: (i, 0)), pl.BlockSpec((k2, n), lambda i, kk: (0, 0))]
        args += list(extra)
    rspec = pl.BlockSpec((tm, n), lambda i, kk: (i, 0))
    in_specs += [rspec, pl.BlockSpec((1, n), lambda i, kk: (0, 0)), rspec]
    call = dict(grid=(m // tm, nk), in_specs=in_specs,
                out_specs=[rspec, rspec, pl.BlockSpec((1, n), lambda i, kk: (0, 0))],
                out_shape=[jax.ShapeDtypeStruct((m, n), F32), jax.ShapeDtypeStruct((m, n), BF16),
                           jax.ShapeDtypeStruct((1, n), F32)],
                scratch_shapes=[pltpu.VMEM((tm, n), F32)], name=name)
    if scatter is None:
        return pl.pallas_call(body, compiler_params=_params("arbitrary", "arbitrary"), **call)(*args, h, gain, dres)
    res = _call_with_exchange(body, args=(*args, h, gain, dres), srcs=scatter, scatter=[True] * len(scatter), **call)
    return [*res[:3], list(res[3:])]


def _conv_taps(cur, prev8, w_ref, first):
    nk = w_ref.shape[0]
    out = cur * w_ref[nk - 1:nk, :]
    for s in range(1, nk):
        out += _shift_down(cur, prev8, s, first) * w_ref[nk - 1 - s:nk - s, :]
    return out


def _shift_up(cur, next8, s, last):
    rows = cur.shape[0]
    row = _iota(cur.shape, 0)
    next8 = jnp.where(last, 0.0, next8)
    sh = pltpu.roll(cur, rows - s, axis=0)
    nh = jnp.tile(pltpu.roll(next8, 8 - s, axis=0), (rows // 8, 1))
    return jnp.where(row >= rows - s, nh, sh)


def _shift_down(cur, prev8, s, first):
    rows = cur.shape[0]
    row = _iota(cur.shape, 0)
    prev8 = jnp.where(first, 0.0, prev8)
    sh = pltpu.roll(cur, s, axis=0)
    ph = jnp.tile(pltpu.roll(prev8, s, axis=0), (rows // 8, 1))
    return jnp.where(row < s, ph, sh)


def _gdn_pre(proj, pab, cq, ck, cv, a_log, dt_bias, *, heads, dk, dv, col_q, col_k, col_v, row_lo, row_hi, name):
    t = proj.shape[0]
    tm = _pick(t, (384, 256, 128))
    nb = t // tm

    def body(pq_ref, pqp_ref, pk_ref, pkp_ref, pv_ref, pvp_ref, ab_ref, cq_ref, ck_ref, cv_ref, al_ref, dt_ref,
             qn_ref, kn_ref, v_ref, g_ref, b_ref):
        h, i = pl.program_id(0), pl.program_id(1)
        first = i == 0
        row = i * tm + _iota((tm, 1), 0)
        valid = (row >= row_lo) & (row < row_hi)
        q1 = _silu(_conv_taps(pq_ref[...], pqp_ref[...], cq_ref, first))
        k1 = _silu(_conv_taps(pk_ref[...], pkp_ref[...], ck_ref, first))
        v1 = _silu(_conv_taps(pv_ref[...], pvp_ref[...], cv_ref, first))
        qn_ref[...] = jnp.where(valid, q1 * lax.rsqrt(jnp.sum(q1 * q1, axis=-1, keepdims=True) + L2_EPS), 0.0)
        kn_ref[...] = jnp.where(valid, k1 * lax.rsqrt(jnp.sum(k1 * k1, axis=-1, keepdims=True) + L2_EPS), 0.0)
        v_ref[...] = jnp.where(valid, v1, 0.0)
        ab = ab_ref[...]
        da = _lane_pick(ab, h)
        db = _lane_pick(ab, heads + h)
        a = _lane_pick(al_ref[...], h)
        dtb = _lane_pick(dt_ref[...], h)
        g_ref[...] = jnp.where(valid, -jnp.exp(a) * _softplus(da + dtb), 0.0)
        b_ref[...] = jnp.where(valid, _sigmoid(db), 0.0)

    def cur(width, col):
        return pl.BlockSpec((tm, width), lambda h, i: (i, col // width + h))

    def prev(width, col):
        return pl.BlockSpec((8, width), lambda h, i: (jnp.maximum(i * (tm // 8) - 1, 0), col // width + h))

    def out(width):
        return pl.BlockSpec((None, tm, width), lambda h, i: (h, i, 0))

    small = pl.BlockSpec((1, LANES), lambda h, i: (0, 0))
    return pl.pallas_call(
        body, grid=(heads, nb),
        in_specs=[cur(dk, col_q), prev(dk, col_q), cur(dk, col_k), prev(dk, col_k), cur(dv, col_v), prev(dv, col_v),
                  pl.BlockSpec((tm, LANES), lambda h, i: (i, 0)),
                  pl.BlockSpec((cq.shape[0], dk), lambda h, i: (0, h)), pl.BlockSpec((ck.shape[0], dk), lambda h, i: (0, h)),
                  pl.BlockSpec((cv.shape[0], dv), lambda h, i: (0, h)), small, small],
        out_specs=[out(dk), out(dk), out(dv), out(1), out(1)],
        out_shape=[jax.ShapeDtypeStruct((heads, t, dk), F32), jax.ShapeDtypeStruct((heads, t, dk), F32),
                   jax.ShapeDtypeStruct((heads, t, dv), F32), jax.ShapeDtypeStruct((heads, t, 1), F32),
                   jax.ShapeDtypeStruct((heads, t, 1), F32)], name=name,
        compiler_params=_params("parallel", "parallel"))(proj, proj, proj, proj, proj, proj, pab, cq, ck, cv, a_log, dt_bias)


def _chunk_masks(rows=GDN_ROWS, row0=0):
    ri = row0 + _iota((rows, GDN_ROWS), 0)
    ci = _iota((rows, GDN_ROWS), 1)
    same = jnp.right_shift(ri, CHUNK_SHIFT) == jnp.right_shift(ci, CHUNK_SHIFT)
    return same, same & (ri >= ci), same & (ri > ci), ri == ci


def _col_to_row(col, eye):
    return jnp.sum(jnp.where(eye, col, 0.0), axis=0, keepdims=True)


def _row_to_col(row, eye):
    return jnp.sum(jnp.where(eye, row, 0.0), axis=1, keepdims=True)


def _chunk_common(blocks, dk_scale):
    same, incl, strict, eye = _chunk_masks()
    tri = jnp.where(incl, 1.0, 0.0).astype(BF16)
    tot = jnp.where(same, 1.0, 0.0).astype(BF16)
    gbs = [jnp.broadcast_to(g, (GDN_ROWS, LANES)) for _, _, g, _ in blocks]
    gams = [jnp.max(_dot_exact_l(tri, gb), axis=1, keepdims=True) for gb in gbs]
    lasts = [jnp.max(_dot_exact_l(tot, gb), axis=1, keepdims=True) for gb in gbs]
    kbs = [kn * beta for _, kn, _, beta in blocks]
    qts = [qn * dk_scale for qn, _, _, _ in blocks]
    boths = [_dot_nt(_bf(jnp.concatenate([kb, qt], axis=0)), _bf(blk[1]))
             for kb, qt, blk in zip(kbs, qts, blocks)]
    out = []
    for gam, last, kb, qt, both in zip(gams, lasts, kbs, qts, boths):
        diff = gam - _col_to_row(gam, eye)
        decay = jnp.where(incl, jnp.exp(jnp.where(incl, diff, 0.0)), 0.0)
        out.append(dict(incl=incl, strict=strict, eye=eye, decay=decay, eg=jnp.exp(gam), ek=jnp.exp(last - gam),
                        egl=jnp.exp(last), kb=kb, qt=qt, lmat=jnp.where(strict, both[:GDN_ROWS] * decay, 0.0),
                        pmat=jnp.where(incl, both[GDN_ROWS:] * decay, 0.0)))
    return out


def _gdn_prep(qn, kn, v, g, beta, *, name):
    heads, t, dk = qn.shape
    dv = v.shape[2]
    rows = _pick(t, (3 * GDN_ROWS, 2 * GDN_ROWS, GDN_ROWS))
    dk_scale = dk ** -0.5

    def body(q_ref, k_ref, v_ref, g_ref, b_ref, u_ref, w_ref, p_ref, qd_ref, kd_ref, egl_ref, t_ref):
        rs = [pl.ds(b * GDN_ROWS, GDN_ROWS) for b in range(rows // GDN_ROWS)]
        cs = _chunk_common([(q_ref[r, :], k_ref[r, :], g_ref[r, :], b_ref[r, :]) for r in rs], dk_scale)
        eye_f = jnp.where(cs[0]["eye"], 1.0, 0.0)
        tinvs = [eye_f - c["lmat"] for c in cs]
        ys = [_dot_hp(c["lmat"], c["lmat"]) for c in cs]
        for _ in range(CHUNK_SHIFT - 1):
            boths = [_dot_hp(y, jnp.concatenate([y, tinv], axis=1)) for y, tinv in zip(ys, tinvs)]
            ys = [both[:, :GDN_ROWS] for both in boths]
            tinvs = [tinv + both[:, GDN_ROWS:] for tinv, both in zip(tinvs, boths)]
        uws = [_dot_hp(tinv, jnp.concatenate([v_ref[r, :] * b_ref[r, :], c["kb"] * c["eg"]], axis=1))
               for r, c, tinv in zip(rs, cs, tinvs)]
        for r, c, tinv, uw in zip(rs, cs, tinvs, uws):
            u_ref[r, :] = uw[:, :dv]
            w_ref[r, :] = uw[:, dv:]
            p_ref[r, :] = c["pmat"]
            qd_ref[r, :] = c["qt"] * c["eg"]
            kd_ref[r, :] = k_ref[r, :] * c["ek"]
            egl_ref[r, :] = c["egl"]
            t_ref[r, :] = tinv

    def blk(width):
        return pl.BlockSpec((None, rows, width), lambda h, i: (h, i, 0))

    def shp(width):
        return jax.ShapeDtypeStruct((heads, t, width), F32)

    return pl.pallas_call(
        body, grid=(heads, t // rows), in_specs=[blk(dk), blk(dk), blk(dv), blk(1), blk(1)],
        out_specs=[blk(dv), blk(dk), blk(GDN_ROWS), blk(dk), blk(dk), blk(1), blk(GDN_ROWS)],
        out_shape=[shp(dv), shp(dk), shp(GDN_ROWS), shp(dk), shp(dk), shp(1), shp(GDN_ROWS)], name=name,
        compiler_params=_params("parallel", "parallel"))(qn, kn, v, g, beta)


def _gdn_scan(u, w, p, qd, kd, egl, proj, gain, *, col_z, name):
    heads, t, dv = u.shape
    dk = w.shape[2]
    nb = t // GDN_ROWS
    sub = GDN_ROWS // CHUNK
    hp = SCAN_HEADS

    def body(u_ref, w_ref, p_ref, qd_ref, kd_ref, egl_ref, z_ref, gn_ref, o_ref, og_ref, st_ref, s_ref):
        @pl.when(pl.program_id(1) == 0)
        def _():
            s_ref[...] = jnp.zeros_like(s_ref)

        hs = range(hp)
        vn_parts = [[jnp.zeros((CHUNK, dv), F32)] * sub for _ in hs]
        for c in range(sub):
            r = pl.ds(c * CHUNK, CHUNK)
            ss = [s_ref[hh] for hh in hs]
            sbs = [_bf(s) for s in ss]
            wss = [_dot(_bf(jnp.concatenate([w_ref[hh, r, :], qd_ref[hh, r, :]], axis=0)), sbs[hh])
                   for hh in hs]
            vns = [u_ref[hh, r, :] - wss[hh][:CHUNK] for hh in hs]
            for hh in hs:
                vn_parts[hh][c] = vns[hh]
            os_ = [wss[hh][CHUNK:] + _dot(_bf(p_ref[hh, r, :]), _bf(jnp.concatenate(vn_parts[hh], axis=0))) for hh in hs]
            new = [ss[hh] * egl_ref[hh, pl.ds(c * CHUNK, 1), :] + _dot_tn(_bf(kd_ref[hh, r, :]), _bf(vns[hh])) for hh in hs]
            for hh in hs:
                cols = pl.ds(hh * dv, dv)
                st_ref[hh, c] = ss[hh]
                s_ref[hh] = new[hh]
                o_ref[hh, r, :] = os_[hh]
                og_ref[r, cols] = _bf(_rms_fwd(os_[hh], gn_ref[...]) * _silu(z_ref[r, cols]))

    def blk(width):
        return pl.BlockSpec((hp, GDN_ROWS, width), lambda h, i: (h, i, 0))

    return pl.pallas_call(
        body, grid=(heads // hp, nb),
        in_specs=[blk(dv), blk(dk), blk(GDN_ROWS), blk(dk), blk(dk), blk(1),
                  pl.BlockSpec((GDN_ROWS, hp * dv), lambda h, i: (i, col_z // (hp * dv) + h)),
                  pl.BlockSpec((1, dv), lambda h, i: (0, 0))],
        out_specs=[blk(dv), pl.BlockSpec((GDN_ROWS, hp * dv), lambda h, i: (i, h)),
                   pl.BlockSpec((hp, sub, dk, dv), lambda h, i: (h, i, 0, 0))],
        out_shape=[jax.ShapeDtypeStruct((heads, t, dv), F32), jax.ShapeDtypeStruct((t, heads * dv), BF16),
                   jax.ShapeDtypeStruct((heads, t // CHUNK, dk, dv), F32)],
        scratch_shapes=[pltpu.VMEM((hp, dk, dv), F32)], name=name,
        compiler_params=_params("parallel", "arbitrary"))(u, w, p, qd, kd, egl, proj, gain)


def _gdn_post_bwd(o, proj, gain, dout, *, col_z, name):
    heads, t, dv = o.shape
    tm = _pick(t, (384, 256, 128))

    def body(o_ref, z_ref, gn_ref, d_ref, do_ref, dz_ref, dg_ref):
        @pl.when((pl.program_id(0) == 0) & (pl.program_id(1) == 0))
        def _():
            dg_ref[...] = jnp.zeros_like(dg_ref)

        o_, z, d = o_ref[...], z_ref[...], d_ref[...]
        y = _rms_fwd(o_, gn_ref[...])
        dz_ref[...] = _bf(d * y * _silu_grad(z))
        do, dgr = _rms_bwd(o_, gn_ref[...], d * _silu(z))
        do_ref[...] = do
        dg_ref[...] += jnp.sum(dgr, axis=0, keepdims=True)

    return pl.pallas_call(
        body, grid=(t // tm, heads),
        in_specs=[pl.BlockSpec((None, tm, dv), lambda i, h: (h, i, 0)),
                  pl.BlockSpec((tm, dv), lambda i, h: (i, col_z // dv + h)),
                  pl.BlockSpec((1, dv), lambda i, h: (0, 0)), pl.BlockSpec((tm, dv), lambda i, h: (i, h))],
        out_specs=[pl.BlockSpec((None, tm, dv), lambda i, h: (h, i, 0)), pl.BlockSpec((tm, dv), lambda i, h: (i, h)),
                   pl.BlockSpec((1, dv), lambda i, h: (0, 0))],
        out_shape=[jax.ShapeDtypeStruct((heads, t, dv), F32), jax.ShapeDtypeStruct((t, heads * dv), BF16),
                   jax.ShapeDtypeStruct((1, dv), F32)], name=name,
        compiler_params=_params("arbitrary", "arbitrary"))(o, proj, gain, dout)


def _gdn_bwd_scan(u, w, p, qd, kd, egl, st, do, *, name):
    heads, t, dv = u.shape
    dk = w.shape[2]
    nb = t // GDN_ROWS
    sub = GDN_ROWS // CHUNK
    hp = SCAN_HEADS

    def body(u_ref, w_ref, p_ref, qd_ref, kd_ref, egl_ref, st_ref, do_ref,
             du_ref, dw_ref, dp_ref, dqd_ref, dkd_ref, dgl_ref, ds_ref):
        @pl.when(pl.program_id(1) == 0)
        def _():
            ds_ref[...] = jnp.zeros_like(ds_ref)

        hs = range(hp)
        zeros = jnp.zeros((CHUNK, dv), BF16)
        for c in reversed(range(sub)):
            r = pl.ds(c * CHUNK, CHUNK)
            ss = [st_ref[hh, c] for hh in hs]
            sbs = [_bf(s) for s in ss]
            dss = [ds_ref[hh] for hh in hs]
            dsbs = [_bf(ds) for ds in dss]
            dobs = [_bf(do_ref[hh, r, :]) for hh in hs]
            wbs = [_bf(w_ref[hh, r, :]) for hh in hs]
            vns = [u_ref[hh, r, :] - _dot(wbs[hh], sbs[hh]) for hh in hs]
            dvns = [_dot_tn(_bf(p_ref[hh, r, :]), dobs[hh])[c * CHUNK:(c + 1) * CHUNK, :]
                    + _dot(_bf(kd_ref[hh, r, :]), dsbs[hh]) for hh in hs]
            dods = [jnp.concatenate([dobs[hh], _bf(dvns[hh])], axis=0) for hh in hs]
            boths = [_dot_nt(dods[hh], sbs[hh]) for hh in hs]
            dps = [_dot_nt(dobs[hh], jnp.concatenate([_bf(vns[hh]) if cc == c else zeros for cc in range(sub)], axis=0))
                   for hh in hs]
            dkds = [_dot_nt(_bf(vns[hh]), dsbs[hh]) for hh in hs]
            new = [dss[hh] * egl_ref[hh, pl.ds(c * CHUNK, 1), :]
                   + _dot_tn(jnp.concatenate([_bf(qd_ref[hh, r, :]), -wbs[hh]], axis=0), dods[hh])
                   for hh in hs]
            for hh in hs:
                du_ref[hh, r, :] = dvns[hh]
                dw_ref[hh, r, :] = -boths[hh][CHUNK:]
                dp_ref[hh, r, :] = jnp.where(_chunk_masks(CHUNK, c * CHUNK)[1], dps[hh], 0.0)
                dqd_ref[hh, r, :] = boths[hh][:CHUNK]
                dkd_ref[hh, r, :] = dkds[hh]
                dgl = jnp.sum(jnp.sum(dss[hh] * ss[hh], axis=1, keepdims=True), axis=0, keepdims=True)
                dgl_ref[hh, r, :] = jnp.where(_iota((CHUNK, 1), 0) == CHUNK - 1, dgl, 0.0)
                ds_ref[hh] = new[hh]

    def blk(width):
        return pl.BlockSpec((hp, GDN_ROWS, width), lambda h, i: (h, nb - 1 - i, 0))

    def shp(width):
        return jax.ShapeDtypeStruct((heads, t, width), F32)

    return pl.pallas_call(
        body, grid=(heads // hp, nb),
        in_specs=[blk(dv), blk(dk), blk(GDN_ROWS), blk(dk), blk(dk), blk(1),
                  pl.BlockSpec((hp, sub, dk, dv), lambda h, i: (h, nb - 1 - i, 0, 0)), blk(dv)],
        out_specs=[blk(dv), blk(dk), blk(GDN_ROWS), blk(dk), blk(dk), blk(1)],
        out_shape=[shp(dv), shp(dk), shp(GDN_ROWS), shp(dk), shp(dk), shp(1)],
        scratch_shapes=[pltpu.VMEM((hp, dk, dv), F32)], name=name,
        compiler_params=_params("parallel", "arbitrary"))(u, w, p, qd, kd, egl, st, do)


def _gdn_bwd_prep(qn, kn, v, g, beta, tinv, u, w, du, dw, dp, dqd, dkd, dgl, *, name):
    heads, t, dk = qn.shape
    dv = v.shape[2]
    rows = _pick(t, (3 * GDN_ROWS, 2 * GDN_ROWS, GDN_ROWS))
    dk_scale = dk ** -0.5

    def rowsum(x):
        return jnp.sum(x, axis=1, keepdims=True)

    def body(q_ref, k_ref, v_ref, g_ref, b_ref, t_ref, u_ref, w_ref, du_ref, dw_ref, dp_ref, dqd_ref, dkd_ref, dgl_ref,
             dq_ref, dkk_ref, dvv_ref, dg_ref, db_ref):
        rs = [pl.ds(b * GDN_ROWS, GDN_ROWS) for b in range(rows // GDN_ROWS)]
        cs = _chunk_common([(q_ref[r, :], k_ref[r, :], g_ref[r, :], b_ref[r, :]) for r in rs], dk_scale)
        dbvws = [_dot_hp(t_ref[r, :], jnp.concatenate([du_ref[r, :], dw_ref[r, :]], axis=1), _dot_tn)
                 for r in rs]
        das = [-_dot_nt(_bf(dbvw), _bf(jnp.concatenate([u_ref[r, :], w_ref[r, :]], axis=1)))
               for r, dbvw in zip(rs, dbvws)]
        dls = [jnp.where(c["strict"], da, 0.0) for c, da in zip(cs, das)]
        dmns = [_bf(jnp.concatenate([dl * c["decay"], dp_ref[r, :] * c["decay"]], axis=0))
                for r, c, dl in zip(rs, cs, dls)]
        boths = [_dot(dmn, _bf(k_ref[r, :])) for r, dmn in zip(rs, dmns)]
        dkns = [_dot_tn(dmn, _bf(jnp.concatenate([c["kb"], c["qt"]], axis=0)))
                for c, dmn in zip(cs, dmns)]
        for r, c, dbvw, dl, both, dkn in zip(rs, cs, dbvws, dls, boths, dkns):
            kn_, beta_, v_ = k_ref[r, :], b_ref[r, :], v_ref[r, :]
            eye = c["eye"]
            kb, qt, eg, ek = c["kb"], c["qt"], c["eg"], c["ek"]
            dbv, dbw = dbvw[:, :dv], dbvw[:, dv:]
            dp_ = dp_ref[r, :]
            dkb = both[:GDN_ROWS] + dbw * eg
            dqt = both[GDN_ROWS:]
            gmat = dl * c["lmat"] + dp_ * c["pmat"]
            dqd_, dkd_ = dqd_ref[r, :], dkd_ref[r, :]
            qd = qt * eg
            kd = kn_ * ek
            bw = kb * eg
            kdsum = rowsum(dkd_ * kd)
            dgam = rowsum(gmat) - _row_to_col(jnp.sum(gmat, axis=0, keepdims=True), eye)
            dgam += rowsum(dbw * bw) + rowsum(dqd_ * qd) - kdsum
            last = (_iota((GDN_ROWS, 1), 0) & (CHUNK - 1)) == CHUNK - 1
            same = _chunk_masks()[0]
            same_f = jnp.where(same, 1.0, 0.0).astype(BF16)
            chunk_tot = jnp.max(_dot_exact_l(same_f, jnp.broadcast_to(kdsum, (GDN_ROWS, LANES))), axis=1, keepdims=True)
            dgam += jnp.where(last, chunk_tot, 0.0) + dgl_ref[r, :] * c["egl"]
            dq_ref[r, :] = (dqt + dqd_ * eg) * dk_scale
            dkk_ref[r, :] = dkn + dkd_ * ek + dkb * beta_
            dvv_ref[r, :] = dbv * beta_
            db_ref[r, :] = rowsum(dbv * v_) + rowsum(dkb * kn_)
            upper = jnp.where(same & (_iota((GDN_ROWS, GDN_ROWS), 0) <= _iota((GDN_ROWS, GDN_ROWS), 1)), 1.0, 0.0)
            dgb = _dot_exact_l(upper.astype(BF16), jnp.broadcast_to(dgam, (GDN_ROWS, LANES)))
            dg_ref[r, :] = _lane_pick(dgb, 0)

    def blk(width):
        return pl.BlockSpec((None, rows, width), lambda h, i: (h, i, 0))

    def shp(width):
        return jax.ShapeDtypeStruct((heads, t, width), F32)

    return pl.pallas_call(
        body, grid=(heads, t // rows),
        in_specs=[blk(dk), blk(dk), blk(dv), blk(1), blk(1), blk(GDN_ROWS), blk(dv), blk(dk),
                  blk(dv), blk(dk), blk(GDN_ROWS), blk(dk), blk(dk), blk(1)],
        out_specs=[blk(dk), blk(dk), blk(dv), blk(1), blk(1)],
        out_shape=[shp(dk), shp(dk), shp(dv), shp(1), shp(1)], name=name,
        compiler_params=_params("parallel", "parallel"))(qn, kn, v, g, beta, tinv, u, w, du, dw, dp, dqd, dkd, dgl)


def _gdn_pre_bwd_a(proj, pab, cq, ck, cv, a_log, dt_bias, dqn, dkn, dvv, dg, dbeta, *,
                   heads, dk, dv, col_q, col_k, col_v, row_lo, row_hi, name):
    t = proj.shape[0]
    tm = _pick(t, (384, 256, 128))
    nb = t // tm

    def body(pq_ref, pqp_ref, pk_ref, pkp_ref, pv_ref, pvp_ref, ab_ref, cq_ref, ck_ref, cv_ref, al_ref, dt_ref,
             dqn_ref, dkn_ref, dvv_ref, dg_ref, db_ref, dcq_ref, dck_ref, dcv_ref, dab_ref, dal_ref, ddt_ref):
        i, h = pl.program_id(0), pl.program_id(1)
        first = i == 0

        @pl.when((i == 0) & (h == 0))
        def _():
            dal_ref[...] = jnp.zeros_like(dal_ref)
            ddt_ref[...] = jnp.zeros_like(ddt_ref)

        @pl.when(h == 0)
        def _():
            dab_ref[...] = jnp.zeros_like(dab_ref)

        row = i * tm + _iota((tm, 1), 0)
        valid = (row >= row_lo) & (row < row_hi)

        def l2_bwd(c1, dn):
            x1 = _silu(c1)
            r = lax.rsqrt(jnp.sum(x1 * x1, axis=-1, keepdims=True) + L2_EPS)
            dn = jnp.where(valid, dn, 0.0)
            d1 = r * dn - x1 * (r * r * r) * jnp.sum(dn * x1, axis=-1, keepdims=True)
            return d1 * _silu_grad(c1)

        dcq_ref[...] = l2_bwd(_conv_taps(pq_ref[...], pqp_ref[...], cq_ref, first), dqn_ref[...])
        dck_ref[...] = l2_bwd(_conv_taps(pk_ref[...], pkp_ref[...], ck_ref, first), dkn_ref[...])
        cv1 = _conv_taps(pv_ref[...], pvp_ref[...], cv_ref, first)
        dcv_ref[...] = jnp.where(valid, dvv_ref[...], 0.0) * _silu_grad(cv1)
        ab = ab_ref[...]
        da = _lane_pick(ab, h)
        db = _lane_pick(ab, heads + h)
        a = _lane_pick(al_ref[...], h)
        dtb = _lane_pick(dt_ref[...], h)
        dgv = jnp.where(valid, dg_ref[...], 0.0)
        ea = jnp.exp(a)
        g = -ea * _softplus(da + dtb)
        dda = dgv * (-ea) * _sigmoid(da + dtb)
        beta = _sigmoid(db)
        ddb = jnp.where(valid, db_ref[...], 0.0) * beta * (1.0 - beta)
        lane = _iota((tm, LANES), 1)
        dab_ref[...] += jnp.where(lane == h, dda, 0.0) + jnp.where(lane == heads + h, ddb, 0.0)
        lane1 = _iota((1, LANES), 1)
        dal_ref[...] += jnp.where(lane1 == h, jnp.sum(dgv * g, axis=0, keepdims=True), 0.0)
        ddt_ref[...] += jnp.where(lane1 == h, jnp.sum(dda, axis=0, keepdims=True), 0.0)

    def cur(width, col):
        return pl.BlockSpec((tm, width), lambda i, h: (i, col // width + h))

    def prev(width, col):
        return pl.BlockSpec((8, width), lambda i, h: (jnp.maximum(i * (tm // 8) - 1, 0), col // width + h))

    def hd(width):
        return pl.BlockSpec((None, tm, width), lambda i, h: (h, i, 0))

    small = pl.BlockSpec((1, LANES), lambda i, h: (0, 0))
    return pl.pallas_call(
        body, grid=(nb, heads),
        in_specs=[cur(dk, col_q), prev(dk, col_q), cur(dk, col_k), prev(dk, col_k), cur(dv, col_v), prev(dv, col_v),
                  pl.BlockSpec((tm, LANES), lambda i, h: (i, 0)),
                  pl.BlockSpec((cq.shape[0], dk), lambda i, h: (0, h)), pl.BlockSpec((ck.shape[0], dk), lambda i, h: (0, h)),
                  pl.BlockSpec((cv.shape[0], dv), lambda i, h: (0, h)), small, small,
                  hd(dk), hd(dk), hd(dv), hd(1), hd(1)],
        out_specs=[hd(dk), hd(dk), hd(dv), pl.BlockSpec((tm, LANES), lambda i, h: (i, 0)), small, small],
        out_shape=[jax.ShapeDtypeStruct((heads, t, dk), F32), jax.ShapeDtypeStruct((heads, t, dk), F32),
                   jax.ShapeDtypeStruct((heads, t, dv), F32), jax.ShapeDtypeStruct((t, LANES), F32),
                   jax.ShapeDtypeStruct((1, LANES), F32), jax.ShapeDtypeStruct((1, LANES), F32)], name=name,
        compiler_params=_params("arbitrary", "arbitrary"))(
            proj, proj, proj, proj, proj, proj, pab, cq, ck, cv, a_log, dt_bias, dqn, dkn, dvv, dg, dbeta)


def _conv_bwd(proj, dc, cw, *, heads, width, col, name):
    t = proj.shape[0]
    tm = _pick(t, (384, 256, 128))
    nb = t // tm
    nk = cw.shape[0]

    def body(p_ref, pp_ref, d_ref, dn_ref, w_ref, dp_ref, dw_ref):
        i = pl.program_id(1)
        first, last = i == 0, i == nb - 1

        @pl.when(first)
        def _():
            dw_ref[...] = jnp.zeros_like(dw_ref)

        x, d = p_ref[...], d_ref[...]
        dx = d * w_ref[nk - 1:nk, :]
        dw_ref[nk - 1:nk, :] += jnp.sum(d * x, axis=0, keepdims=True)
        for s in range(1, nk):
            dx += _shift_up(d, dn_ref[...], s, last) * w_ref[nk - 1 - s:nk - s, :]
            dw_ref[nk - 1 - s:nk - s, :] += jnp.sum(d * _shift_down(x, pp_ref[...], s, first), axis=0, keepdims=True)
        dp_ref[...] = _bf(dx)

    return pl.pallas_call(
        body, grid=(heads, nb),
        in_specs=[pl.BlockSpec((tm, width), lambda h, i: (i, col // width + h)),
                  pl.BlockSpec((8, width), lambda h, i: (jnp.maximum(i * (tm // 8) - 1, 0), col // width + h)),
                  pl.BlockSpec((None, tm, width), lambda h, i: (h, i, 0)),
                  pl.BlockSpec((None, 8, width), lambda h, i: (h, jnp.minimum((i + 1) * (tm // 8), t // 8 - 1), 0)),
                  pl.BlockSpec((nk, width), lambda h, i: (0, h))],
        out_specs=[pl.BlockSpec((tm, width), lambda h, i: (i, h)), pl.BlockSpec((nk, width), lambda h, i: (0, h))],
        out_shape=[jax.ShapeDtypeStruct((t, heads * width), BF16), jax.ShapeDtypeStruct((nk, heads * width), F32)],
        name=name, compiler_params=_params("parallel", "arbitrary"))(proj, proj, dc, dc, cw)


def _sb_pre(proj, gq, gk, *, heads, dh, col_q, col_k, col_v, name):
    t = proj.shape[0]
    tm = _pick(t, (384, 256, 128))

    def body(q_ref, k_ref, v_ref, gq_ref, gk_ref, qo_ref, ko_ref, vo_ref):
        qo_ref[...] = _bf(_rms_fwd(q_ref[...], gq_ref[...]))
        ko_ref[...] = _bf(_rms_fwd(k_ref[...], gk_ref[...]))
        vo_ref[...] = _bf(v_ref[...])

    def cur(col):
        return pl.BlockSpec((tm, dh), lambda i, h: (i, col // dh + h))

    gspec = pl.BlockSpec((1, dh), lambda i, h: (0, 0))
    ospec = pl.BlockSpec((tm, dh), lambda i, h: (i, h))
    return pl.pallas_call(
        body, grid=(t // tm, heads), in_specs=[cur(col_q), cur(col_k), cur(col_v), gspec, gspec],
        out_specs=[ospec] * 3, out_shape=[jax.ShapeDtypeStruct((t, heads * dh), BF16)] * 3, name=name,
        compiler_params=_params("parallel", "parallel"))(proj, proj, proj, gq, gk)


def _sb_tile(z, i, j, blk, key_lo, masked):
    ls = jnp.minimum(z, 0.0) - jnp.log(1.0 + jnp.exp(-jnp.abs(z)))
    if not masked:
        return None, ls, ls - z
    qpos = i * blk + _iota((blk, blk), 0)
    kpos = j * blk + _iota((blk, blk), 1)
    vis = (kpos < qpos) & (kpos >= key_lo)
    return vis, ls, jnp.where(vis, ls - z, 0.0)


def _where_vis(vis, x):
    return x if vis is None else jnp.where(vis, x, 0.0)


def _sb_sweep(i, step, init, descending):
    first, last = (i, 0) if descending else (0, i)
    carry = step(first, init, True)
    carry = lax.fori_loop(1, i, lambda n, c: step(i - n if descending else n, c, False), carry)
    return lax.cond(i > 0, lambda c: step(last, c, True), lambda c: c, carry)


def _dot2_r(x, m):
    hi, lo = _split2(x)
    return _dot(hi, m) + _dot(lo, m)


def _running_sums(x, tri, reverse):
    groups = [x[:, s:s + LANES] for s in range(0, x.shape[1], LANES)]
    inside = [_dot2_r(g, tri) for g in groups]
    sums = [jnp.sum(g, axis=1, keepdims=True) for g in groups]
    order = list(reversed(range(len(groups)))) if reverse else list(range(len(groups)))
    out, acc = [None] * len(groups), None
    for gi in order:
        out[gi] = inside[gi] if acc is None else inside[gi] + acc
        acc = sums[gi] if acc is None else acc + sums[gi]
    return jnp.concatenate(out, axis=1), acc


def _sb_fwd(qs, ks, vs, *, heads, dh, key_lo, name, gather=None):
    t = qs.shape[0]
    blk = _pick(t, (3 * SB_BLOCK, 2 * SB_BLOCK, SB_BLOCK))
    assert key_lo <= blk
    nq = t // blk
    assert nq <= LANES
    scale = dh ** -0.5
    hp = SB_HEADS_PER_STEP

    def body(q_ref, k_ref, v_ref, o_ref, c_ref):
        i = pl.program_id(1)
        later = jnp.where(_iota((LANES, LANES), 0) > _iota((LANES, LANES), 1), 1.0, 0.0).astype(BF16)
        lane = _iota((blk, LANES), 1)
        c_ref[...] = jnp.zeros_like(c_ref)

        def step(j, carry, masked):
            rows = pl.ds(pl.multiple_of(j * blk, blk), blk)
            hs = range(hp)
            cols = [pl.ds(hh * dh, dh) for hh in hs]
            zs = [_dot_nt(q_ref[:, cols[hh]], k_ref[rows, cols[hh]]) * scale for hh in hs]
            tiles = [_sb_tile(z, i, j, blk, key_lo, masked) for z in zs]
            sufs = [_running_sums(lk, later, reverse=True) for _, _, lk in tiles]
            wgts = [_where_vis(vis, jnp.exp(ls + suf + carry[2 * hh + 1]))
                    for hh, ((vis, ls, _), (suf, _)) in enumerate(zip(tiles, sufs))]
            accs = [carry[2 * hh] + _dot(_bf(wgts[hh]), v_ref[rows, cols[hh]]) for hh in hs]
            out = []
            for hh in hs:
                c_ref[hh] = jnp.where(lane == j, carry[2 * hh + 1], c_ref[hh])
                out += [accs[hh], carry[2 * hh + 1] + sufs[hh][1]]
            return tuple(out)

        res = _sb_sweep(i, step, (jnp.zeros((blk, dh), F32), jnp.zeros((blk, 1), F32)) * hp, descending=True)
        for hh in range(hp):
            o_ref[:, pl.ds(hh * dh, dh)] = _bf(res[2 * hh])

    full = pl.BlockSpec((t, hp * dh), lambda h, i: (0, h))
    call = dict(grid=(heads // hp, nq),
                in_specs=[pl.BlockSpec((blk, hp * dh), lambda h, i: (i, h)), full, full],
                out_specs=[pl.BlockSpec((blk, hp * dh), lambda h, i: (i, h)),
                           pl.BlockSpec((hp, blk, LANES), lambda h, i: (h, i, 0))],
                out_shape=[jax.ShapeDtypeStruct((t, heads * dh), BF16), jax.ShapeDtypeStruct((heads, t, LANES), F32)],
                name=name)
    if gather is None:
        return pl.pallas_call(body, compiler_params=_params("parallel", "parallel"), **call)(qs, ks, vs)
    res = _call_with_exchange(body, scratch_shapes=[], args=(qs, ks, vs), srcs=gather, scatter=[False] * len(gather),
                              **call)
    return [*res[:2], list(res[2:])]


def _sb_bwd(qs, ks, vs, do, carry, *, heads, dh, key_lo, name, scatter=None):
    t = qs.shape[0]
    blk = _pick(t, (3 * SB_BLOCK, 2 * SB_BLOCK, SB_BLOCK))
    assert key_lo <= blk
    nq = t // blk
    scale = dh ** -0.5
    hp = SB_HEADS_PER_STEP

    def body(q_ref, k_ref, v_ref, do_ref, c_ref, dq_ref, dk_ref, dv_ref):
        i = pl.program_id(1)

        @pl.when(i == 0)
        def _():
            dk_ref[...] = jnp.zeros_like(dk_ref)
            dv_ref[...] = jnp.zeros_like(dv_ref)

        r0 = _iota((LANES, LANES), 0)
        r1 = _iota((LANES, LANES), 1)
        later = jnp.where(r0 > r1, 1.0, 0.0).astype(BF16)
        earlier = jnp.where(r0 < r1, 1.0, 0.0).astype(BF16)

        def step(j, carry, masked):
            rows = pl.ds(pl.multiple_of(j * blk, blk), blk)
            hs = range(hp)
            cols = [pl.ds(hh * dh, dh) for hh in hs]
            zs = [_dot_nt(q_ref[:, cols[hh]], k_ref[rows, cols[hh]]) * scale for hh in hs]
            dws = [_dot_nt(do_ref[:, cols[hh]], v_ref[rows, cols[hh]]) for hh in hs]
            tiles = [_sb_tile(z, i, j, blk, key_lo, masked) for z in zs]
            sufs = [_running_sums(lk, later, reverse=True)[0] for _, _, lk in tiles]
            wgts = [_where_vis(vis, jnp.exp(ls + suf + _lane_pick(c_ref[hh], j)))
                    for hh, ((vis, ls, _), suf) in enumerate(zip(tiles, sufs))]
            es = [wgt * dw for wgt, dw in zip(wgts, dws)]
            pres = [_running_sums(e, earlier, reverse=False) for e in es]
            dzs = []
            for hh in hs:
                vis, ls, _ = tiles[hh]
                before = _where_vis(vis, pres[hh][0] + carry[2 * hh + 1])
                sig = jnp.exp(ls)
                dzs.append(_bf((es[hh] * (1.0 - sig) - before * sig) * scale))
            dks = [_dot_tn(dzs[hh], q_ref[:, cols[hh]]) for hh in hs]
            dvs = [_dot_tn(_bf(wgts[hh]), do_ref[:, cols[hh]]) for hh in hs]
            dqs = [carry[2 * hh] + _dot(dzs[hh], k_ref[rows, cols[hh]]) for hh in hs]
            out = []
            for hh in hs:
                dk_ref[rows, cols[hh]] += dks[hh]
                dv_ref[rows, cols[hh]] += dvs[hh]
                out += [dqs[hh], carry[2 * hh + 1] + pres[hh][1]]
            return tuple(out)

        res = _sb_sweep(i, step, (jnp.zeros((blk, dh), F32), jnp.zeros((blk, 1), F32)) * hp, descending=False)
        for hh in range(hp):
            dq_ref[:, pl.ds(hh * dh, dh)] = res[2 * hh]

    full = pl.BlockSpec((t, hp * dh), lambda h, i: (0, h))
    qblk = pl.BlockSpec((blk, hp * dh), lambda h, i: (i, h))
    call = dict(grid=(heads // hp, nq),
                in_specs=[qblk, full, full, qblk, pl.BlockSpec((hp, blk, LANES), lambda h, i: (h, i, 0))],
                out_specs=[qblk, full, full], out_shape=[jax.ShapeDtypeStruct((t, heads * dh), F32)] * 3, name=name)
    if scatter is None:
        return pl.pallas_call(body, compiler_params=_params("parallel", "arbitrary"), **call)(qs, ks, vs, do, carry)
    res = _call_with_exchange(body, scratch_shapes=[], args=(qs, ks, vs, do, carry), srcs=scatter,
                              scatter=[True] * len(scatter), **call)
    return [*res[:3], list(res[3:])]


def _sb_pre_bwd(proj, gq, gk, dq, dk, dv, *, heads, dh, col_q, col_k, name):
    t = proj.shape[0]
    tm = _pick(t, (384, 256, 128))

    def body(q_ref, k_ref, gq_ref, gk_ref, dq_ref, dk_ref, dv_ref, oq_ref, ok_ref, ov_ref, dgq_ref, dgk_ref):
        @pl.when((pl.program_id(0) == 0) & (pl.program_id(1) == 0))
        def _():
            dgq_ref[...] = jnp.zeros_like(dgq_ref)
            dgk_ref[...] = jnp.zeros_like(dgk_ref)

        dq_, gq_r = _rms_bwd(q_ref[...], gq_ref[...], dq_ref[...])
        dk_, gk_r = _rms_bwd(k_ref[...], gk_ref[...], dk_ref[...])
        oq_ref[...] = _bf(dq_)
        ok_ref[...] = _bf(dk_)
        ov_ref[...] = _bf(dv_ref[...])
        dgq_ref[...] += jnp.sum(gq_r, axis=0, keepdims=True)
        dgk_ref[...] += jnp.sum(gk_r, axis=0, keepdims=True)

    def cur(col):
        return pl.BlockSpec((tm, dh), lambda i, h: (i, col // dh + h))

    gspec = pl.BlockSpec((1, dh), lambda i, h: (0, 0))
    ospec = pl.BlockSpec((tm, dh), lambda i, h: (i, h))
    return pl.pallas_call(
        body, grid=(t // tm, heads), in_specs=[cur(col_q), cur(col_k), gspec, gspec, ospec, ospec, ospec],
        out_specs=[ospec, ospec, ospec, gspec, gspec],
        out_shape=[jax.ShapeDtypeStruct((t, heads * dh), BF16)] * 3 + [jax.ShapeDtypeStruct((1, dh), F32)] * 2,
        name=name, compiler_params=_params("arbitrary", "arbitrary"))(proj, proj, gq, gk, dq, dk, dv)


PEERS = N_DEV - 1


def _exchange_copies(ins, outs, send_sems, recv_sems, local_sems, scatter):
    x, y, c = lax.axis_index("x"), lax.axis_index("y"), lax.axis_index("c")
    me = 4 * x + 2 * y + c
    copies = []
    for a in range(len(ins)):
        own = ins[a].at[me] if scatter[a] else ins[a]
        copies.append(pltpu.make_async_copy(own, outs[a].at[me], local_sems.at[a]))
        for k in range(1, N_DEV):
            px = (x + (k >> 2 & 1)) % 2
            py = (y + (k >> 1 & 1)) % 2
            pc = (c + (k & 1)) % 2
            src = ins[a].at[4 * px + 2 * py + pc] if scatter[a] else ins[a]
            copies.append(pltpu.make_async_remote_copy(
                src_ref=src, dst_ref=outs[a].at[me], send_sem=send_sems.at[a * PEERS + k - 1],
                recv_sem=recv_sems.at[a * PEERS + k - 1], device_id=(px, py, pc), device_id_type=MESH))
    return copies


def _exchange_shapes(srcs, scatter):
    return [jax.ShapeDtypeStruct(s.shape if sc else (N_DEV,) + s.shape, s.dtype) for s, sc in zip(srcs, scatter)]


def _exchange_sems(n):
    return [pltpu.SemaphoreType.DMA((n * PEERS,)), pltpu.SemaphoreType.DMA((n * PEERS,)), pltpu.SemaphoreType.DMA((n,))]


def _exchange(srcs, *, scatter, name):
    n = len(srcs)

    def body(*refs):
        copies = _exchange_copies(refs[:n], refs[n:2 * n], *refs[2 * n:], scatter)
        for cp in copies:
            cp.start()
        for cp in copies:
            cp.wait()

    any_spec = pl.BlockSpec(memory_space=pl.ANY)
    return pl.pallas_call(
        body, in_specs=[any_spec] * n, out_specs=[any_spec] * n, out_shape=_exchange_shapes(srcs, scatter),
        scratch_shapes=_exchange_sems(n), name=name,
        compiler_params=pltpu.CompilerParams(has_side_effects=True))(*srcs)


def _gather_two_level(srcs, *, name):
    n = len(srcs)

    def body(*refs):
        ins, outs = refs[:n], refs[n:2 * n]
        send_sems, recv_sems, local_sems = refs[2 * n:]
        x, y, c = lax.axis_index("x"), lax.axis_index("y"), lax.axis_index("c")
        chips = [(1 - x, y), (x, 1 - y), (1 - x, 1 - y)]

        def slab(a, px, py, pc):
            return outs[a].at[4 * px + 2 * py + pc]

        def copy(a, k, block, to, src=None):
            return pltpu.make_async_remote_copy(
                src_ref=slab(a, *block) if src is None else src, dst_ref=slab(a, *block),
                send_sem=send_sems.at[a * PEERS + k], recv_sem=recv_sems.at[a * PEERS + k],
                device_id=to, device_id_type=MESH)

        mine = [pltpu.make_async_copy(ins[a], slab(a, x, y, c), local_sems.at[a]) for a in range(n)]
        first = [copy(a, 0, (x, y, c), (x, y, 1 - c), src=ins[a]) for a in range(n)]
        first += [copy(a, 1 + j, (x, y, c), (*chip, c), src=ins[a]) for j, chip in enumerate(chips) for a in range(n)]
        for cp in mine + first:
            cp.start()
        passed = []
        for j, chip in enumerate(chips):
            for a in range(n):
                copy(a, 1 + j, (*chip, c), (x, y, c)).wait_recv()
                passed.append(copy(a, 4 + j, (*chip, c), (x, y, 1 - c)))
                passed[-1].start()
        for a in range(n):
            copy(a, 0, (x, y, 1 - c), (x, y, c)).wait_recv()
            for j, chip in enumerate(chips):
                copy(a, 4 + j, (*chip, 1 - c), (x, y, c)).wait_recv()
        for cp in first + passed:
            cp.wait_send()
        for cp in mine:
            cp.wait()

    any_spec = pl.BlockSpec(memory_space=pl.ANY)
    return pl.pallas_call(
        body, in_specs=[any_spec] * n, out_specs=[any_spec] * n, out_shape=_exchange_shapes(srcs, [False] * n),
        scratch_shapes=_exchange_sems(n), name=name,
        compiler_params=pltpu.CompilerParams(has_side_effects=True))(*srcs)


def _call_with_exchange(body, *, grid, in_specs, out_specs, out_shape, scratch_shapes, args, srcs, scatter, name):
    n, n_in, n_out, n_scr = len(srcs), len(args), len(out_shape), len(scratch_shapes)

    def full_body(*refs):
        ins, xin = refs[:n_in], refs[n_in:n_in + n]
        outs, xout = refs[n_in + n:n_in + n + n_out], refs[n_in + n + n_out:n_in + 2 * n + n_out]
        scr = refs[n_in + 2 * n + n_out:]
        ids = [pl.program_id(a) for a in range(len(grid))]
        first = functools.reduce(jnp.logical_and, [i == 0 for i in ids])
        last = functools.reduce(jnp.logical_and, [i == g - 1 for i, g in zip(ids, grid)])
        copies = _exchange_copies(xin, xout, *scr[n_scr:], scatter)

        @pl.when(first)
        def _():
            for cp in copies:
                cp.start()

        body(*ins, *outs, *scr[:n_scr])

        @pl.when(last)
        def _():
            for cp in copies:
                cp.wait()

    any_spec = pl.BlockSpec(memory_space=pl.ANY)
    return pl.pallas_call(
        full_body, grid=grid, in_specs=list(in_specs) + [any_spec] * n, out_specs=list(out_specs) + [any_spec] * n,
        out_shape=list(out_shape) + _exchange_shapes(srcs, scatter),
        scratch_shapes=list(scratch_shapes) + _exchange_sems(n), name=name,
        compiler_params=pltpu.CompilerParams(dimension_semantics=("arbitrary",) * len(grid),
                                             vmem_limit_bytes=V7X_VMEM_LIMIT_BYTES, has_side_effects=True))(*args, *srcs)


def _adam_math(g, w, m, v):
    m2 = ADAM_B1 * m + (1.0 - ADAM_B1) * g
    v2 = ADAM_B2 * v + (1.0 - ADAM_B2) * (g * g)
    m_hat = m2 / (1.0 - ADAM_B1 ** ADAM_STEP)
    v_hat = v2 / (1.0 - ADAM_B2 ** ADAM_STEP)
    return -ADAM_LR * (m_hat / (jnp.sqrt(v_hat) + ADAM_EPS) + ADAM_WD * w), m2, v2


def _adamw_slabs(slabs, w, m, v, *, name):
    r, c = w.shape
    tr = next((t for t in (2256, 752, 512, 240, 128, 64, 32, 16) if r % t == 0), r)

    def body(s_ref, w_ref, m_ref, v_ref, g_ref, d_ref, mo_ref, vo_ref):
        g = s_ref[0].astype(F32)
        for p in range(1, N_DEV):
            g = g + s_ref[p].astype(F32)
        g_ref[...] = g
        d_ref[...], mo_ref[...], vo_ref[...] = _adam_math(g, w_ref[...], m_ref[...], v_ref[...])

    spec = pl.BlockSpec((tr, c), lambda i: (i, 0))
    return pl.pallas_call(
        body, grid=(r // tr,), in_specs=[pl.BlockSpec((N_DEV, tr, c), lambda i: (0, i, 0)), spec, spec, spec],
        out_specs=[spec] * 4, out_shape=[jax.ShapeDtypeStruct((r, c), F32)] * 4, name=name,
        compiler_params=_params("parallel"))(slabs, w, m, v)


def _adamw_small(g, w, m, v, *, name):
    def body(g_ref, w_ref, m_ref, v_ref, d_ref, mo_ref, vo_ref):
        d_ref[...], mo_ref[...], vo_ref[...] = _adam_math(g_ref[...], w_ref[...], m_ref[...], v_ref[...])

    return pl.pallas_call(body, out_shape=[jax.ShapeDtypeStruct(w.shape, F32)] * 3, name=name)(g, w, m, v)


def _sum_slabs(slabs, *, name):
    def body(s_ref, o_ref):
        acc = s_ref[0]
        for p in range(1, N_DEV):
            acc = acc + s_ref[p]
        o_ref[...] = acc

    return pl.pallas_call(body, out_shape=jax.ShapeDtypeStruct(slabs.shape[1:], F32), name=name)(slabs)


def _pad_lanes(a):
    return jnp.pad(a, ((0, 0), (0, LANES - a.shape[1])))


def _w_in_slabs(main, ab, col_ab, n_ab):
    pw = main.shape[0] + n_ab
    c = pw // N_DEV
    parts = [(0, col_ab, main, 0), (col_ab, col_ab + n_ab, ab, col_ab), (col_ab + n_ab, pw, main, n_ab)]
    slabs = []
    for p in range(N_DEV):
        pieces = []
        for lo, hi, src, shift in parts:
            a, b = max(lo, c * p), min(hi, c * (p + 1))
            if a < b:
                pieces.append(src[a - shift:b - shift])
        slabs.append(jnp.concatenate(pieces, axis=0))
    return jnp.stack(slabs)


def _local_step(x, target, meta, g_mix, wt_main, wt_ab, cq, ck, cv, a_log, dt_bias, g_dn, g_sbq, g_sbk, g_ffn, rest,
                shards=False):
    seq, d = x.shape
    n_meta = meta.shape[0]
    heads = a_log.shape[1]
    qk = cq.shape[1]
    dvt = cv.shape[1]
    dk, dv = qk // heads, dvt // heads
    dh = g_sbq.shape[1]
    sbw = rest[2].shape[0] * N_DEV if shards else rest[1].shape[0]
    sb_heads = sbw // dh
    pad_l = (-n_meta) % CHUNK
    row_x = pad_l + n_meta
    rows = row_x + seq
    t = -(-rows // GDN_ROWS) * GDN_ROWS
    col_q, col_k, col_v, col_z = 0, qk, 2 * qk, 2 * qk + dvt
    col_sq = 2 * qk + 2 * dvt
    col_sk, col_sv, col_gd, col_gs = col_sq + sbw, col_sq + 2 * sbw, col_sq + 3 * sbw, col_sq + 3 * sbw + d

    def rows_pad(a):
        return jnp.concatenate([jnp.zeros((row_x, d), F32), a, jnp.zeros((t - rows, d), F32)], axis=0)

    h0 = jnp.concatenate([jnp.zeros((pad_l, d), F32), meta, x, jnp.zeros((t - rows, d), F32)], axis=0)
    tgt = rows_pad(target)
    a_log_p, dt_p = _pad_lanes(a_log), _pad_lanes(dt_bias)

    proj, n1 = _mm_norm(h0, g_mix, wt_main, name="proj")
    pab = _mm_nt(n1, wt_ab, out_dtype=F32, name="proj_ab")
    gk = dict(heads=heads, dk=dk, dv=dv, col_q=col_q, col_k=col_k, col_v=col_v, row_lo=pad_l, row_hi=rows)
    qn, kn, vv, g, beta = _gdn_pre(proj, pab, cq, ck, cv, a_log_p, dt_p, name="gdn_pre", **gk)
    u, w, pm, qd, kd, egl, tinv = _gdn_prep(qn, kn, vv, g, beta, name="gdn_prep")
    o_raw, o_dn, states = _gdn_scan(u, w, pm, qd, kd, egl, proj, g_dn, col_z=col_z, name="gdn_scan")
    qs, ks, vs = _sb_pre(proj, g_sbq, g_sbk, heads=sb_heads, dh=dh, col_q=col_sq, col_k=col_sk, col_v=col_sv,
                         name="sb_pre")
    if shards:
        o_sb, carry, (g_fi, g_bd, g_bs, g_out, g_fo) = _sb_fwd(qs, ks, vs, heads=sb_heads, dh=dh, key_lo=pad_l,
                                                                name="sb_fwd", gather=list(rest))
        wt_fi = g_fi.reshape(-1, d)
        d_ff = wt_fi.shape[0] // 2
        w_bd, w_bs, w_out, wt_fg, wt_fu, w_fo = (g_bd.reshape(-1, d), g_bs.reshape(-1, d), g_out.reshape(-1, d),
                                                 wt_fi[:d_ff], wt_fi[d_ff:], g_fo.reshape(-1, d))
    else:
        o_sb, carry = _sb_fwd(qs, ks, vs, heads=sb_heads, dh=dh, key_lo=pad_l, name="sb_fwd")
        w_bd, w_bs, w_out, wt_fg, wt_fu, w_fo = rest
    merged, br_dn, br_sb = _merge_fwd(o_dn, o_sb, w_bd, w_bs, proj, col_gd=col_gd, col_gs=col_gs, name="merge")
    h1 = _mm_res(h0, merged, w_out, name="mix_out")
    gate, up, act, n2 = _mm_norm_swiglu(h1, g_ffn, wt_fg, wt_fu, name="ffn_in")
    dy, dyb, lsum = _mm_res_loss(h1, act, w_fo, tgt, row0=row_x, nrows=seq, name="ffn_out_loss")

    dgate, dup = _swiglu_bwd(dyb, w_fo, gate, up, name="ffn_out_bwd")
    d_w_fo = _mm_tn(act, dyb, name="dw_ffn_out")
    d_wt_fg = _mm_tn(dgate, n2, name="dw_ffn_gate")
    d_wt_fu = _mm_tn(dup, n2, name="dw_ffn_up")
    dh1, dh1b, d_g_ffn = _mm_rmsbwd([(dgate, wt_fg), (dup, wt_fu)], None, h1, g_ffn, dy, name="ffn_in_bwd")

    dbd, dbs, dgd, dgs = _merge_bwd(dh1b, w_out, proj, br_dn, br_sb, col_gd=col_gd, col_gs=col_gs, name="mix_out_bwd")
    d_w_out = _mm_tn(merged, dh1b, name="dw_out")
    d_w_bd = _mm_tn(o_dn, dbd, name="dw_branch_dn")
    d_w_bs = _mm_tn(o_sb, dbs, name="dw_branch_sb")
    do_dn = _mm_nt(dbd, w_bd, out_dtype=F32, name="branch_dn_bwd")
    do_sb = _mm_nt(dbs, w_bs, out_dtype=BF16, name="branch_sb_bwd")

    do_raw, dz, d_g_dn = _gdn_post_bwd(o_raw, proj, g_dn, do_dn, col_z=col_z, name="gdn_post_bwd")
    du, dw, dp, dqd, dkd, dgl = _gdn_bwd_scan(u, w, pm, qd, kd, egl, states, do_raw, name="gdn_bwd_scan")
    dqn, dkn, dvv, dg, dbeta = _gdn_bwd_prep(qn, kn, vv, g, beta, tinv, u, w, du, dw, dp, dqd, dkd, dgl,
                                            name="gdn_bwd_prep")
    dcq, dck, dcv, dpab, d_a_log, d_dt = _gdn_pre_bwd_a(proj, pab, cq, ck, cv, a_log_p, dt_p, dqn, dkn, dvv, dg, dbeta,
                                                        name="gdn_pre_bwd", **gk)
    dpq, d_cq = _conv_bwd(proj, dcq, cq, heads=heads, width=dk, col=col_q, name="conv_q_bwd")
    dpk, d_ck = _conv_bwd(proj, dck, ck, heads=heads, width=dk, col=col_k, name="conv_k_bwd")
    dpv, d_cv = _conv_bwd(proj, dcv, cv, heads=heads, width=dv, col=col_v, name="conv_v_bwd")

    early = None
    if shards:
        slabs = [_bf(jnp.concatenate([d_wt_fg, d_wt_fu], axis=0)).reshape(N_DEV, -1, LANES)]
        slabs += [_bf(a).reshape(N_DEV, -1, d) for a in (d_w_bd, d_w_bs, d_w_out, d_w_fo)]
        dqs, dks, dvs, early = _sb_bwd(qs, ks, vs, do_sb, carry, heads=sb_heads, dh=dh, key_lo=pad_l, name="sb_bwd",
                                       scatter=slabs)
    else:
        dqs, dks, dvs = _sb_bwd(qs, ks, vs, do_sb, carry, heads=sb_heads, dh=dh, key_lo=pad_l, name="sb_bwd")
    dsq, dsk, dsv, d_g_sbq, d_g_sbk = _sb_pre_bwd(proj, g_sbq, g_sbk, dqs, dks, dvs, heads=sb_heads, dh=dh,
                                                   col_q=col_sq, col_k=col_sk, name="sb_pre_bwd")

    dproj = jnp.concatenate([dpq, dpk, dpv, dz, dsq, dsk, dsv, dgd, dgs], axis=1)
    dpab_b = _bf(dpab)
    d_wt_main = _mm_tn(dproj, n1, name="dw_in_main")
    d_wt_ab = _mm_tn(dpab_b, n1, name="dw_in_ab")
    s_in = None
    if shards:
        slabs = _w_in_slabs(_bf(d_wt_main), _bf(d_wt_ab), col_sq, 2 * heads).reshape(N_DEV, -1, LANES)
        dh0, _, d_g_mix, (s_in,) = _mm_rmsbwd([(dproj, wt_main)], (dpab_b, wt_ab), h0, g_mix, dh1, name="proj_bwd",
                                              scatter=[slabs])
    else:
        dh0, _, d_g_mix = _mm_rmsbwd([(dproj, wt_main)], (dpab_b, wt_ab), h0, g_mix, dh1, name="proj_bwd")

    return dict(s_in=s_in, lsum=lsum, grad_x=dh0[row_x:rows], d_meta=dh0[pad_l:row_x], d_g_mix=d_g_mix,
                d_wt_main=d_wt_main, d_wt_ab=d_wt_ab, d_cq=d_cq, d_ck=d_ck, d_cv=d_cv, d_a_log=d_a_log[:, :heads],
                d_dt=d_dt[:, :heads], d_g_dn=d_g_dn, d_g_sbq=d_g_sbq, d_g_sbk=d_g_sbk, d_w_bd=d_w_bd, d_w_bs=d_w_bs,
                d_w_out=d_w_out, d_g_ffn=d_g_ffn, d_wt_fg=d_wt_fg, d_wt_fu=d_wt_fu, d_w_fo=d_w_fo, early=early)


def _pack(parts):
    flat = []
    for a in parts:
        a = a.reshape(-1)
        flat.append(jnp.pad(a, (0, (-a.shape[0]) % LANES)))
    v = jnp.concatenate(flat)
    v = jnp.pad(v, (0, (-v.shape[0]) % (8 * LANES)))
    return v.reshape(-1, LANES)


def _unpack(packed, shapes):
    flat = packed.reshape(-1)
    out, pos = [], 0
    for s in shapes:
        n = math.prod(s)
        out.append(flat[pos:pos + n].reshape(s))
        pos += n + (-n) % LANES
    return out


def kernel(x, meta_tokens, norm_mix_gain, w_in, conv_q, conv_k, conv_v, dn_a_log, dn_dt_bias, dn_out_norm_gain, sb_q_norm_gain, sb_k_norm_gain, w_branch_dn, w_branch_sb, w_out, norm_ffn_gain, w_ffn_in, w_ffn_out, loss_target, m_meta_tokens, m_norm_mix_gain, m_w_in, m_conv_q, m_conv_k, m_conv_v, m_dn_a_log, m_dn_dt_bias, m_dn_out_norm_gain, m_sb_q_norm_gain, m_sb_k_norm_gain, m_w_branch_dn, m_w_branch_sb, m_w_out, m_norm_ffn_gain, m_w_ffn_in, m_w_ffn_out, v_meta_tokens, v_norm_mix_gain, v_w_in, v_conv_q, v_conv_k, v_conv_v, v_dn_a_log, v_dn_dt_bias, v_dn_out_norm_gain, v_sb_q_norm_gain, v_sb_k_norm_gain, v_w_branch_dn, v_w_branch_sb, v_w_out, v_norm_ffn_gain, v_w_ffn_in, v_w_ffn_out):
    me = 4 * lax.axis_index("x") + 2 * lax.axis_index("y") + lax.axis_index("c")
    heads = dn_a_log.shape[1]
    d = x.shape[2]
    qk = conv_q.shape[2] * N_DEV
    dvt = conv_v.shape[2] * N_DEV
    col_ab = 2 * qk + 2 * dvt

    small_shapes = [meta_tokens.shape, conv_q.shape[1:], conv_k.shape[1:], conv_v.shape[1:]]
    small = _pack([meta_tokens, conv_q[0], conv_k[0], conv_v[0]])
    def features_major(a):
        return jnp.transpose(a, (2, 0, 1)).reshape(a.shape[2], a.shape[1])

    g_in, g_small = _gather_two_level([_bf(features_major(w_in)), small], name="gather_w_in")
    wt_full = g_in.reshape(-1, d)
    wt_main = jnp.concatenate([wt_full[:col_ab], wt_full[col_ab + 2 * heads:]], axis=0)
    wt_ab = jnp.pad(wt_full[col_ab:col_ab + 2 * heads], ((0, LANES - 2 * heads), (0, 0)))
    parts = [_unpack(g_small[p], small_shapes) for p in range(N_DEV)]
    meta_f, cq_f, ck_f, cv_f = (jnp.concatenate([parts[p][a] for p in range(N_DEV)], axis=1) for a in range(4))

    r = _local_step(x[0], loss_target[0], meta_f, norm_mix_gain, wt_main, wt_ab, cq_f, ck_f, cv_f, dn_a_log, dn_dt_bias,
                    dn_out_norm_gain, sb_q_norm_gain, sb_k_norm_gain, norm_ffn_gain,
                    (_bf(features_major(w_ffn_in)), _bf(w_branch_dn[0]), _bf(w_branch_sb[0]), _bf(w_out[0]),
                     _bf(w_ffn_out[0])), shards=True)
    s_fi, s_bd, s_bs, s_out, s_fo = r["early"]
    s_in = r["s_in"]

    loss_part = (0.5 / d) * jnp.sum(r["lsum"], axis=1, keepdims=True)
    small_g = [r["d_meta"], r["d_g_mix"], r["d_cq"], r["d_ck"], r["d_cv"], r["d_a_log"], r["d_dt"], r["d_g_dn"],
               r["d_g_sbq"], r["d_g_sbk"], r["d_g_ffn"], loss_part]
    (g_packs,) = _exchange([_pack(small_g)], scatter=[False], name="gather_small_grads")
    (g_meta, g_mix, g_cq, g_ck, g_cv, g_al, g_dt, g_gdn, g_sbq, g_sbk, g_ffn, loss) = _unpack(
        _sum_slabs(g_packs, name="sum_small_grads"), [a.shape for a in small_g])

    def mine(a, width):
        return lax.dynamic_slice_in_dim(a, me * width, width, axis=1)

    big = dict(w_in=(s_in, w_in, m_w_in, v_w_in), w_branch_dn=(s_bd, w_branch_dn, m_w_branch_dn, v_w_branch_dn),
               w_branch_sb=(s_bs, w_branch_sb, m_w_branch_sb, v_w_branch_sb), w_out=(s_out, w_out, m_w_out, v_w_out),
               w_ffn_in=(s_fi, w_ffn_in, m_w_ffn_in, v_w_ffn_in), w_ffn_out=(s_fo, w_ffn_out, m_w_ffn_out, v_w_ffn_out))
    tiny = dict(meta_tokens=(mine(g_meta, d // N_DEV), meta_tokens, m_meta_tokens, v_meta_tokens),
                norm_mix_gain=(g_mix, norm_mix_gain, m_norm_mix_gain, v_norm_mix_gain),
                conv_q=(mine(g_cq, qk // N_DEV), conv_q[0], m_conv_q[0], v_conv_q[0]),
                conv_k=(mine(g_ck, qk // N_DEV), conv_k[0], m_conv_k[0], v_conv_k[0]),
                conv_v=(mine(g_cv, dvt // N_DEV), conv_v[0], m_conv_v[0], v_conv_v[0]),
                dn_a_log=(g_al, dn_a_log, m_dn_a_log, v_dn_a_log), dn_dt_bias=(g_dt, dn_dt_bias, m_dn_dt_bias, v_dn_dt_bias),
                dn_out_norm_gain=(g_gdn, dn_out_norm_gain, m_dn_out_norm_gain, v_dn_out_norm_gain),
                sb_q_norm_gain=(g_sbq, sb_q_norm_gain, m_sb_q_norm_gain, v_sb_q_norm_gain),
                sb_k_norm_gain=(g_sbk, sb_k_norm_gain, m_sb_k_norm_gain, v_sb_k_norm_gain),
                norm_ffn_gain=(g_ffn, norm_ffn_gain, m_norm_ffn_gain, v_norm_ffn_gain))
    order = ["meta_tokens", "norm_mix_gain", "w_in", "conv_q", "conv_k", "conv_v", "dn_a_log", "dn_dt_bias",
             "dn_out_norm_gain", "sb_q_norm_gain", "sb_k_norm_gain", "w_branch_dn", "w_branch_sb", "w_out",
             "norm_ffn_gain", "w_ffn_in", "w_ffn_out"]
    grads, deltas, new_m, new_v = [], [], [], []
    for name in order:
        if name in ("w_in", "w_ffn_in"):
            slabs, w, m, v = big[name]
            res = _adamw_slabs(slabs, *(features_major(a).reshape(-1, LANES) for a in (w, m, v)), name="adamw_" + name)
            g, dl, mo, vo = (jnp.transpose(a.reshape(w.shape[2], 1, w.shape[1]), (1, 2, 0)) for a in res)
            like = w.shape
        elif name in big:
            slabs, w, m, v = big[name]
            g, dl, mo, vo = _adamw_slabs(slabs, w[0], m[0], v[0], name="adamw_" + name)
            like = w.shape
        else:
            g, w, m, v = tiny[name]
            like = dict(conv_q=conv_q, conv_k=conv_k, conv_v=conv_v).get(name, w).shape
            dl, mo, vo = _adamw_small(g, w, m, v, name="adamw_" + name)
        for lst, a in ((grads, g), (deltas, dl), (new_m, mo), (new_v, vo)):
            lst.append(a.reshape(like))
    return (loss.reshape(()), r["grad_x"][None], *grads, *deltas, *new_m, *new_v)
```

```python
import functools
import math

import jax
import jax.numpy as jnp
from jax import lax
from jax.experimental import pallas as pl
from jax.experimental.pallas import tpu as pltpu

F32 = jnp.float32
BF16 = jnp.bfloat16

N_DEV = 8
CHUNK = 64
CHUNK_SHIFT = 6
GDN_ROWS = 2 * CHUNK
SB_BLOCK = 128
SB_HEADS_PER_STEP = 2
LANES = 128
RMS_EPS = 1e-6
L2_EPS = 1e-6
ADAM_LR = 0.001
ADAM_B1 = 0.9
ADAM_B2 = 0.999
ADAM_EPS = 1e-08
ADAM_WD = 0.01
ADAM_STEP = 10
V7X_VMEM_LIMIT_BYTES = 56 * 1024 * 1024
MM_TN_OUT_BLOCK_BYTES = 6 * 1024 * 1024
ROWS_BIG = (1056, 512, 384, 256, 128)
ROWS_MID = (528, 384, 256, 128)
SCAN_HEADS = 4

MESH = pl.DeviceIdType.MESH


def _params(*sem):
    return pltpu.CompilerParams(dimension_semantics=sem or None, vmem_limit_bytes=V7X_VMEM_LIMIT_BYTES)


def _pick(n, cands):
    for c in cands:
        if n % c == 0:
            return c
    raise ValueError(f"no block size among {cands} divides {n}")


def _bf(x):
    return x.astype(BF16)


def _dot(a, b):
    return jnp.dot(a, b, preferred_element_type=F32)


def _dot_nt(a, b):
    return lax.dot_general(a, b, (((1,), (1,)), ((), ())), preferred_element_type=F32)


def _dot_tn(a, b):
    return lax.dot_general(a, b, (((0,), (0,)), ((), ())), preferred_element_type=F32)


def _split2(x):
    hi = _bf(x)
    return hi, _bf(x - hi.astype(F32))


def _split3(x):
    hi = _bf(x)
    r = x - hi.astype(F32)
    mid = _bf(r)
    return hi, mid, _bf(r - mid.astype(F32))


def _dot_hp(a, b, dot=_dot):
    ah, al = _split2(a)
    bh, bl = _split2(b)
    return dot(ah, bh) + dot(ah, bl) + dot(al, bh)


def _dot_exact_l(m, x, dot=_dot):
    h, mi, lo = _split3(x)
    return dot(m, h) + dot(m, mi) + dot(m, lo)


def _sigmoid(x):
    return 1.0 / (1.0 + jnp.exp(-x))


def _silu(x):
    return x * _sigmoid(x)


def _silu_grad(x):
    s = _sigmoid(x)
    return s * (1.0 + x * (1.0 - s))


def _softplus(x):
    return jnp.maximum(x, 0.0) + jnp.log(1.0 + jnp.exp(-jnp.abs(x)))


def _rms_fwd(h, gain):
    r = lax.rsqrt(jnp.mean(h * h, axis=-1, keepdims=True) + RMS_EPS)
    return h * r * gain


def _rms_bwd(h, gain, dy):
    r = lax.rsqrt(jnp.mean(h * h, axis=-1, keepdims=True) + RMS_EPS)
    dyg = dy * gain
    dh = r * dyg - h * (r * r * r) * jnp.mean(dyg * h, axis=-1, keepdims=True)
    return dh, dy * h * r


def _iota(shape, dim):
    return lax.broadcasted_iota(jnp.int32, shape, dim)


def _lane_pick(x, idx):
    return jnp.sum(jnp.where(_iota(x.shape, 1) == idx, x, 0.0), axis=1, keepdims=True)


def _mm_nt(a, b, *, out_dtype, name):
    m, k = a.shape
    n = b.shape[0]
    tm, tn = _pick(m, ROWS_BIG), _pick(n, (1024, 512, 256, 128))

    def body(a_ref, b_ref, o_ref):
        o_ref[...] = _dot_nt(a_ref[...], b_ref[...]).astype(out_dtype)

    return pl.pallas_call(
        body, grid=(m // tm, n // tn),
        in_specs=[pl.BlockSpec((tm, k), lambda i, j: (i, 0)), pl.BlockSpec((tn, k), lambda i, j: (j, 0))],
        out_specs=pl.BlockSpec((tm, tn), lambda i, j: (i, j)),
        out_shape=jax.ShapeDtypeStruct((m, n), out_dtype), name=name,
        compiler_params=_params("parallel", "parallel"))(a, b)


def _mm_tn(a, b, *, name):
    t, m = a.shape
    n = b.shape[1]
    tn = _pick(n, (2816, 2048, 1408, 1024, 512, 256, 128))
    tm = _pick(m, tuple(c for c in (1408, 1024, 512, 256, 128) if c * tn * 4 <= MM_TN_OUT_BLOCK_BYTES))
    tk = _pick(t, (1408, 1024, 512, 384, 256, 128))
    nk = t // tk

    def body(a_ref, b_ref, o_ref, acc_ref):
        k = pl.program_id(2)

        @pl.when(k == 0)
        def _():
            acc_ref[...] = jnp.zeros_like(acc_ref)

        acc_ref[...] += _dot_tn(a_ref[...], b_ref[...])

        @pl.when(k == nk - 1)
        def _():
            o_ref[...] = _bf(acc_ref[...])

    return pl.pallas_call(
        body, grid=(m // tm, n // tn, nk),
        in_specs=[pl.BlockSpec((tk, tm), lambda i, j, k: (k, i)), pl.BlockSpec((tk, tn), lambda i, j, k: (k, j))],
        out_specs=pl.BlockSpec((tm, tn), lambda i, j, k: (i, j)),
        out_shape=jax.ShapeDtypeStruct((m, n), BF16), scratch_shapes=[pltpu.VMEM((tm, tn), F32)], name=name,
        compiler_params=_params("parallel", "parallel", "arbitrary"))(a, b)


def _mm_norm(h, gain, wt, *, name):
    m, k = h.shape
    n = wt.shape[0]
    tm, tn = _pick(m, ROWS_BIG), _pick(n, (1024, 512, 256, 128))

    def body(h_ref, g_ref, w_ref, o_ref, n_ref):
        @pl.when(pl.program_id(1) == 0)
        def _():
            n_ref[...] = _bf(_rms_fwd(h_ref[...], g_ref[...]))

        o_ref[...] = _dot_nt(n_ref[...], w_ref[...])

    return pl.pallas_call(
        body, grid=(m // tm, n // tn),
        in_specs=[pl.BlockSpec((tm, k), lambda i, j: (i, 0)), pl.BlockSpec((1, k), lambda i, j: (0, 0)),
                  pl.BlockSpec((tn, k), lambda i, j: (j, 0))],
        out_specs=[pl.BlockSpec((tm, tn), lambda i, j: (i, j)), pl.BlockSpec((tm, k), lambda i, j: (i, 0))],
        out_shape=[jax.ShapeDtypeStruct((m, n), F32), jax.ShapeDtypeStruct((m, k), BF16)], name=name,
        compiler_params=_params("parallel", "arbitrary"))(h, gain, wt)


def _mm_norm_swiglu(h, gain, wgt, wut, *, name):
    m, k = h.shape
    n = wgt.shape[0]
    tm, tn = _pick(m, ROWS_MID), _pick(n, (1408, 1024, 512, 256, 128))

    def body(h_ref, g_ref, wg_ref, wu_ref, gate_ref, up_ref, act_ref, n_ref):
        @pl.when(pl.program_id(1) == 0)
        def _():
            n_ref[...] = _bf(_rms_fwd(h_ref[...], g_ref[...]))

        gate = _dot_nt(n_ref[...], wg_ref[...])
        up = _dot_nt(n_ref[...], wu_ref[...])
        gate_ref[...] = gate
        up_ref[...] = up
        act_ref[...] = _bf(_silu(gate) * up)

    wspec = pl.BlockSpec((tn, k), lambda i, j: (j, 0))
    ospec = pl.BlockSpec((tm, tn), lambda i, j: (i, j))
    return pl.pallas_call(
        body, grid=(m // tm, n // tn),
        in_specs=[pl.BlockSpec((tm, k), lambda i, j: (i, 0)), pl.BlockSpec((1, k), lambda i, j: (0, 0)), wspec, wspec],
        out_specs=[ospec, ospec, ospec, pl.BlockSpec((tm, k), lambda i, j: (i, 0))],
        out_shape=[jax.ShapeDtypeStruct((m, n), F32), jax.ShapeDtypeStruct((m, n), F32),
                   jax.ShapeDtypeStruct((m, n), BF16), jax.ShapeDtypeStruct((m, k), BF16)], name=name,
        compiler_params=_params("parallel", "arbitrary"))(h, gain, wgt, wut)


def _mm_res(res, a, b, *, name):
    m, k = a.shape
    n = b.shape[1]
    tm, tn = _pick(m, ROWS_BIG), _pick(n, (1024, 512, 256, 128))

    def body(r_ref, a_ref, b_ref, o_ref):
        o_ref[...] = r_ref[...] + _dot(a_ref[...], b_ref[...])

    return pl.pallas_call(
        body, grid=(m // tm, n // tn),
        in_specs=[pl.BlockSpec((tm, tn), lambda i, j: (i, j)), pl.BlockSpec((tm, k), lambda i, j: (i, 0)),
                  pl.BlockSpec((k, tn), lambda i, j: (0, j))],
        out_specs=pl.BlockSpec((tm, tn), lambda i, j: (i, j)),
        out_shape=jax.ShapeDtypeStruct((m, n), F32), name=name,
        compiler_params=_params("parallel", "parallel"))(res, a, b)


def _mm_res_loss(res, a, b, target, *, row0, nrows, name):
    m, k = a.shape
    n = b.shape[1]
    tm = _pick(m, ROWS_MID)

    def body(r_ref, a_ref, b_ref, t_ref, dy_ref, dyb_ref, ls_ref):
        i = pl.program_id(0)

        @pl.when(i == 0)
        def _():
            ls_ref[...] = jnp.zeros_like(ls_ref)

        y = r_ref[...] + _dot(a_ref[...], b_ref[...])
        row = i * tm + _iota((tm, n), 0)
        e = jnp.where((row >= row0) & (row < row0 + nrows), y - t_ref[...], 0.0)
        dy = e / n
        dy_ref[...] = dy
        dyb_ref[...] = _bf(dy)
        ls_ref[...] += jnp.sum(e * e, axis=0, keepdims=True)

    rspec = pl.BlockSpec((tm, n), lambda i: (i, 0))
    return pl.pallas_call(
        body, grid=(m // tm,),
        in_specs=[rspec, pl.BlockSpec((tm, k), lambda i: (i, 0)), pl.BlockSpec((k, n), lambda i: (0, 0)), rspec],
        out_specs=[rspec, rspec, pl.BlockSpec((1, n), lambda i: (0, 0))],
        out_shape=[jax.ShapeDtypeStruct((m, n), F32), jax.ShapeDtypeStruct((m, n), BF16),
                   jax.ShapeDtypeStruct((1, n), F32)], name=name,
        compiler_params=_params("arbitrary"))(res, a, b, target)


def _merge_fwd(o_dn, o_sb, wbd, wbs, proj, *, col_gd, col_gs, name):
    m, kd = o_dn.shape
    ks = o_sb.shape[1]
    n = wbd.shape[1]
    tm = _pick(m, ROWS_BIG)
    tn = _pick(math.gcd(n, math.gcd(col_gd, col_gs)), (512, 256, 128))

    def body(od_ref, os_ref, wd_ref, ws_ref, gd_ref, gs_ref, mg_ref, bd_ref, bs_ref):
        bd = _dot(od_ref[...], wd_ref[...])
        bs = _dot(os_ref[...], ws_ref[...])
        bd_ref[...] = bd
        bs_ref[...] = bs
        mg_ref[...] = _bf(_sigmoid(gd_ref[...]) * bd + _sigmoid(gs_ref[...]) * bs)

    ospec = pl.BlockSpec((tm, tn), lambda i, j: (i, j))
    return pl.pallas_call(
        body, grid=(m // tm, n // tn),
        in_specs=[pl.BlockSpec((tm, kd), lambda i, j: (i, 0)), pl.BlockSpec((tm, ks), lambda i, j: (i, 0)),
                  pl.BlockSpec((kd, tn), lambda i, j: (0, j)), pl.BlockSpec((ks, tn), lambda i, j: (0, j)),
                  pl.BlockSpec((tm, tn), lambda i, j: (i, col_gd // tn + j)),
                  pl.BlockSpec((tm, tn), lambda i, j: (i, col_gs // tn + j))],
        out_specs=[ospec, ospec, ospec],
        out_shape=[jax.ShapeDtypeStruct((m, n), BF16), jax.ShapeDtypeStruct((m, n), F32),
                   jax.ShapeDtypeStruct((m, n), F32)], name=name,
        compiler_params=_params("parallel", "parallel"))(o_dn, o_sb, wbd, wbs, proj, proj)


def _merge_bwd(dh, w_out, proj, br_dn, br_sb, *, col_gd, col_gs, name):
    m, k = dh.shape
    n = w_out.shape[0]
    tm = _pick(m, ROWS_BIG)
    tn = _pick(math.gcd(n, math.gcd(col_gd, col_gs)), (512, 256, 128))

    def body(dh_ref, w_ref, gd_ref, gs_ref, bd_ref, bs_ref, dbd_ref, dbs_ref, dgd_ref, dgs_ref):
        dm = _dot_nt(dh_ref[...], w_ref[...])
        sd = _sigmoid(gd_ref[...])
        ss = _sigmoid(gs_ref[...])
        dbd_ref[...] = _bf(dm * sd)
        dbs_ref[...] = _bf(dm * ss)
        dgd_ref[...] = _bf(dm * bd_ref[...] * sd * (1.0 - sd))
        dgs_ref[...] = _bf(dm * bs_ref[...] * ss * (1.0 - ss))

    ospec = pl.BlockSpec((tm, tn), lambda i, j: (i, j))
    return pl.pallas_call(
        body, grid=(m // tm, n // tn),
        in_specs=[pl.BlockSpec((tm, k), lambda i, j: (i, 0)), pl.BlockSpec((tn, k), lambda i, j: (j, 0)),
                  pl.BlockSpec((tm, tn), lambda i, j: (i, col_gd // tn + j)),
                  pl.BlockSpec((tm, tn), lambda i, j: (i, col_gs // tn + j)), ospec, ospec],
        out_specs=[ospec] * 4,
        out_shape=[jax.ShapeDtypeStruct((m, n), BF16)] * 4, name=name,
        compiler_params=_params("parallel", "parallel"))(dh, w_out, proj, proj, br_dn, br_sb)


def _swiglu_bwd(dy, wfo, gate, up, *, name):
    m, k = dy.shape
    n = wfo.shape[0]
    tm, tn = _pick(m, ROWS_MID), _pick(n, (1408, 1024, 512, 256, 128))

    def body(dy_ref, w_ref, g_ref, u_ref, dg_ref, du_ref):
        da = _dot_nt(dy_ref[...], w_ref[...])
        g = g_ref[...]
        dg_ref[...] = _bf(da * u_ref[...] * _silu_grad(g))
        du_ref[...] = _bf(da * _silu(g))

    ospec = pl.BlockSpec((tm, tn), lambda i, j: (i, j))
    return pl.pallas_call(
        body, grid=(m // tm, n // tn),
        in_specs=[pl.BlockSpec((tm, k), lambda i, j: (i, 0)), pl.BlockSpec((tn, k), lambda i, j: (j, 0)), ospec, ospec],
        out_specs=[ospec, ospec], out_shape=[jax.ShapeDtypeStruct((m, n), BF16)] * 2, name=name,
        compiler_params=_params("parallel", "parallel"))(dy, wfo, gate, up)


def _mm_rmsbwd(pairs, extra, h, gain, dres, *, name, scatter=None):
    m, k = pairs[0][0].shape
    n = h.shape[1]
    tm = _pick(m, ROWS_MID)
    tk = _pick(k, (1408, 1024, 512, 256, 128))
    nk = k // tk
    np_ = len(pairs)

    def body(*refs):
        ab = refs[:2 * np_]
        pos = 2 * np_
        ex = refs[pos:pos + 2] if extra is not None else ()
        pos += len(ex)
        h_ref, g_ref, r_ref, dh_ref, dhb_ref, dg_ref, acc_ref = refs[pos:]
        i, kk = pl.program_id(0), pl.program_id(1)

        @pl.when((i == 0) & (kk == 0))
        def _():
            dg_ref[...] = jnp.zeros_like(dg_ref)

        part = _dot(ab[0][...], ab[1][...])
        for p in range(1, np_):
            part += _dot(ab[2 * p][...], ab[2 * p + 1][...])

        @pl.when(kk == 0)
        def _():
            first = part
            if ex:
                first = first + _dot(ex[0][...], ex[1][...])
            acc_ref[...] = first

        @pl.when(kk > 0)
        def _():
            acc_ref[...] += part

        @pl.when(kk == nk - 1)
        def _():
            dh, dgr = _rms_bwd(h_ref[...], g_ref[...], acc_ref[...])
            dh = dh + r_ref[...]
            dh_ref[...] = dh
            dhb_ref[...] = _bf(dh)
            dg_ref[...] += jnp.sum(dgr, axis=0, keepdims=True)

    in_specs, args = [], []
    for a, b in pairs:
        in_specs += [pl.BlockSpec((tm, tk), lambda i, kk: (i, kk)), pl.BlockSpec((tk, n), lambda i, kk: (kk, 0))]
        args += [a, b]
    if extra is not None:
        k2 = extra[0].shape[1]
        in_specs += [pl.BlockSpec((tm, k2), lambda i, kk: (i, 0)), pl.BlockSpec((k2, n), lambda i, kk: (0, 0))]
        args += list(extra)
    rspec = pl.BlockSpec((tm, n), lambda i, kk: (i, 0))
    in_specs += [rspec, pl.BlockSpec((1, n), lambda i, kk: (0, 0)), rspec]
    call = dict(grid=(m // tm, nk), in_specs=in_specs,
                out_specs=[rspec, rspec, pl.BlockSpec((1, n), lambda i, kk: (0, 0))],
                out_shape=[jax.ShapeDtypeStruct((m, n), F32), jax.ShapeDtypeStruct((m, n), BF16),
                           jax.ShapeDtypeStruct((1, n), F32)],
                scratch_shapes=[pltpu.VMEM((tm, n), F32)], name=name)
    if scatter is None:
        return pl.pallas_call(body, compiler_params=_params("arbitrary", "arbitrary"), **call)(*args, h, gain, dres)
    res = _call_with_exchange(body, args=(*args, h, gain, dres), srcs=scatter, scatter=[True] * len(scatter), **call)
    return [*res[:3], list(res[3:])]


HALO = 8


def _stage(buf_ref, before, cur):
    buf_ref[0:HALO, :] = before
    buf_ref[HALO:HALO + cur.shape[0], :] = cur


def _stage_after(buf_ref, cur, after):
    r = cur.shape[0]
    buf_ref[0:r, :] = cur
    buf_ref[r:r + HALO, :] = after


def _conv_taps(buf_ref, rows, w_ref):
    nk = w_ref.shape[0]
    out = buf_ref[HALO:HALO + rows, :] * w_ref[nk - 1:nk, :]
    for s in range(1, nk):
        out += buf_ref[HALO - s:HALO - s + rows, :] * w_ref[nk - 1 - s:nk - s, :]
    return out


def _gdn_pre(proj, pab, cq, ck, cv, a_log, dt_bias, *, heads, dk, dv, col_q, col_k, col_v, row_lo, row_hi, name):
    t = proj.shape[0]
    tm = _pick(t, (384, 256, 128))
    nb = t // tm

    def body(pq_ref, pqp_ref, pk_ref, pkp_ref, pv_ref, pvp_ref, ab_ref, cq_ref, ck_ref, cv_ref, al_ref, dt_ref,
             qn_ref, kn_ref, v_ref, g_ref, b_ref, bq_ref, bk_ref, bv_ref):
        h, i = pl.program_id(0), pl.program_id(1)
        first = i == 0
        row = i * tm + _iota((tm, 1), 0)
        valid = (row >= row_lo) & (row < row_hi)
        _stage(bq_ref, jnp.where(first, 0.0, pqp_ref[...]), pq_ref[...])
        _stage(bk_ref, jnp.where(first, 0.0, pkp_ref[...]), pk_ref[...])
        _stage(bv_ref, jnp.where(first, 0.0, pvp_ref[...]), pv_ref[...])
        q1 = _silu(_conv_taps(bq_ref, tm, cq_ref))
        k1 = _silu(_conv_taps(bk_ref, tm, ck_ref))
        v1 = _silu(_conv_taps(bv_ref, tm, cv_ref))
        qn_ref[...] = jnp.where(valid, q1 * lax.rsqrt(jnp.sum(q1 * q1, axis=-1, keepdims=True) + L2_EPS), 0.0)
        kn_ref[...] = jnp.where(valid, k1 * lax.rsqrt(jnp.sum(k1 * k1, axis=-1, keepdims=True) + L2_EPS), 0.0)
        v_ref[...] = jnp.where(valid, v1, 0.0)
        ab = ab_ref[...]
        da = _lane_pick(ab, h)
        db = _lane_pick(ab, heads + h)
        a = _lane_pick(al_ref[...], h)
        dtb = _lane_pick(dt_ref[...], h)
        g_ref[...] = jnp.where(valid, -jnp.exp(a) * _softplus(da + dtb), 0.0)
        b_ref[...] = jnp.where(valid, _sigmoid(db), 0.0)

    def cur(width, col):
        return pl.BlockSpec((tm, width), lambda h, i: (i, col // width + h))

    def prev(width, col):
        return pl.BlockSpec((8, width), lambda h, i: (jnp.maximum(i * (tm // 8) - 1, 0), col // width + h))

    def out(width):
        return pl.BlockSpec((None, tm, width), lambda h, i: (h, i, 0))

    small = pl.BlockSpec((1, LANES), lambda h, i: (0, 0))
    return pl.pallas_call(
        body, grid=(heads, nb),
        in_specs=[cur(dk, col_q), prev(dk, col_q), cur(dk, col_k), prev(dk, col_k), cur(dv, col_v), prev(dv, col_v),
                  pl.BlockSpec((tm, LANES), lambda h, i: (i, 0)),
                  pl.BlockSpec((cq.shape[0], dk), lambda h, i: (0, h)), pl.BlockSpec((ck.shape[0], dk), lambda h, i: (0, h)),
                  pl.BlockSpec((cv.shape[0], dv), lambda h, i: (0, h)), small, small],
        out_specs=[out(dk), out(dk), out(dv), out(1), out(1)],
        out_shape=[jax.ShapeDtypeStruct((heads, t, dk), F32), jax.ShapeDtypeStruct((heads, t, dk), F32),
                   jax.ShapeDtypeStruct((heads, t, dv), F32), jax.ShapeDtypeStruct((heads, t, 1), F32),
                   jax.ShapeDtypeStruct((heads, t, 1), F32)],
        scratch_shapes=[pltpu.VMEM((HALO + tm, dk), F32), pltpu.VMEM((HALO + tm, dk), F32),
                        pltpu.VMEM((HALO + tm, dv), F32)], name=name,
        compiler_params=_params("parallel", "parallel"))(proj, proj, proj, proj, proj, proj, pab, cq, ck, cv, a_log, dt_bias)


def _chunk_masks(rows=GDN_ROWS, row0=0):
    ri = row0 + _iota((rows, GDN_ROWS), 0)
    ci = _iota((rows, GDN_ROWS), 1)
    same = jnp.right_shift(ri, CHUNK_SHIFT) == jnp.right_shift(ci, CHUNK_SHIFT)
    return same, same & (ri >= ci), same & (ri > ci), ri == ci


def _col_to_row(col, eye):
    return jnp.sum(jnp.where(eye, col, 0.0), axis=0, keepdims=True)


def _row_to_col(row, eye):
    return jnp.sum(jnp.where(eye, row, 0.0), axis=1, keepdims=True)


def _chunk_common(blocks, dk_scale):
    same, incl, strict, eye = _chunk_masks()
    tri = jnp.where(incl, 1.0, 0.0).astype(BF16)
    tot = jnp.where(same, 1.0, 0.0).astype(BF16)
    gbs = [jnp.broadcast_to(g, (GDN_ROWS, LANES)) for _, _, g, _ in blocks]
    gams = [jnp.max(_dot_exact_l(tri, gb), axis=1, keepdims=True) for gb in gbs]
    lasts = [jnp.max(_dot_exact_l(tot, gb), axis=1, keepdims=True) for gb in gbs]
    kbs = [kn * beta for _, kn, _, beta in blocks]
    qts = [qn * dk_scale for qn, _, _, _ in blocks]
    boths = [_dot_nt(_bf(jnp.concatenate([kb, qt], axis=0)), _bf(blk[1]))
             for kb, qt, blk in zip(kbs, qts, blocks)]
    out = []
    for gam, last, kb, qt, both in zip(gams, lasts, kbs, qts, boths):
        diff = gam - _col_to_row(gam, eye)
        decay = jnp.where(incl, jnp.exp(jnp.where(incl, diff, 0.0)), 0.0)
        out.append(dict(incl=incl, strict=strict, eye=eye, decay=decay, eg=jnp.exp(gam), ek=jnp.exp(last - gam),
                        egl=jnp.exp(last), kb=kb, qt=qt, lmat=jnp.where(strict, both[:GDN_ROWS] * decay, 0.0),
                        pmat=jnp.where(incl, both[GDN_ROWS:] * decay, 0.0)))
    return out


def _gdn_prep(qn, kn, v, g, beta, *, name):
    heads, t, dk = qn.shape
    dv = v.shape[2]
    rows = _pick(t, (3 * GDN_ROWS, 2 * GDN_ROWS, GDN_ROWS))
    dk_scale = dk ** -0.5

    def body(q_ref, k_ref, v_ref, g_ref, b_ref, u_ref, w_ref, p_ref, qd_ref, kd_ref, egl_ref, t_ref):
        rs = [pl.ds(b * GDN_ROWS, GDN_ROWS) for b in range(rows // GDN_ROWS)]
        cs = _chunk_common([(q_ref[r, :], k_ref[r, :], g_ref[r, :], b_ref[r, :]) for r in rs], dk_scale)
        eye_f = jnp.where(cs[0]["eye"], 1.0, 0.0)
        tinvs = [eye_f - c["lmat"] for c in cs]
        ys = [_dot_hp(c["lmat"], c["lmat"]) for c in cs]
        for _ in range(CHUNK_SHIFT - 1):
            boths = [_dot_hp(y, jnp.concatenate([y, tinv], axis=1)) for y, tinv in zip(ys, tinvs)]
            ys = [both[:, :GDN_ROWS] for both in boths]
            tinvs = [tinv + both[:, GDN_ROWS:] for tinv, both in zip(tinvs, boths)]
        uws = [_dot_hp(tinv, jnp.concatenate([v_ref[r, :] * b_ref[r, :], c["kb"] * c["eg"]], axis=1))
               for r, c, tinv in zip(rs, cs, tinvs)]
        for r, c, tinv, uw in zip(rs, cs, tinvs, uws):
            u_ref[r, :] = uw[:, :dv]
            w_ref[r, :] = _bf(uw[:, dv:])
            p_ref[r, :] = _bf(c["pmat"])
            qd_ref[r, :] = _bf(c["qt"] * c["eg"])
            kd_ref[r, :] = _bf(k_ref[r, :] * c["ek"])
            egl_ref[r, :] = c["egl"]
            t_ref[r, :] = tinv

    def blk(width):
        return pl.BlockSpec((None, rows, width), lambda h, i: (h, i, 0))

    def shp(width, dtype=F32):
        return jax.ShapeDtypeStruct((heads, t, width), dtype)

    return pl.pallas_call(
        body, grid=(heads, t // rows), in_specs=[blk(dk), blk(dk), blk(dv), blk(1), blk(1)],
        out_specs=[blk(dv), blk(dk), blk(GDN_ROWS), blk(dk), blk(dk), blk(1), blk(GDN_ROWS)],
        out_shape=[shp(dv), shp(dk, BF16), shp(GDN_ROWS, BF16), shp(dk, BF16), shp(dk, BF16), shp(1), shp(GDN_ROWS)],
        name=name,
        compiler_params=_params("parallel", "parallel"))(qn, kn, v, g, beta)


def _gdn_scan(u, w, p, qd, kd, egl, proj, gain, *, col_z, name):
    heads, t, dv = u.shape
    dk = w.shape[2]
    nb = t // GDN_ROWS
    sub = GDN_ROWS // CHUNK
    hp = SCAN_HEADS

    def body(u_ref, w_ref, p_ref, qd_ref, kd_ref, egl_ref, z_ref, gn_ref, o_ref, og_ref, st_ref, s_ref):
        @pl.when(pl.program_id(1) == 0)
        def _():
            s_ref[...] = jnp.zeros_like(s_ref)

        hs = range(hp)
        vn_parts = [[jnp.zeros((CHUNK, dv), F32)] * sub for _ in hs]
        for c in range(sub):
            r = pl.ds(c * CHUNK, CHUNK)
            ss = [s_ref[hh] for hh in hs]
            sbs = [_bf(s) for s in ss]
            wss = [_dot(_bf(jnp.concatenate([w_ref[hh, r, :], qd_ref[hh, r, :]], axis=0)), sbs[hh])
                   for hh in hs]
            vns = [u_ref[hh, r, :] - wss[hh][:CHUNK] for hh in hs]
            for hh in hs:
                vn_parts[hh][c] = vns[hh]
            os_ = [wss[hh][CHUNK:] + _dot(_bf(p_ref[hh, r, :]), _bf(jnp.concatenate(vn_parts[hh], axis=0))) for hh in hs]
            new = [ss[hh] * egl_ref[hh, pl.ds(c * CHUNK, 1), :] + _dot_tn(_bf(kd_ref[hh, r, :]), _bf(vns[hh])) for hh in hs]
            for hh in hs:
                cols = pl.ds(hh * dv, dv)
                st_ref[hh, c] = ss[hh]
                s_ref[hh] = new[hh]
                o_ref[hh, r, :] = os_[hh]
                og_ref[r, cols] = _bf(_rms_fwd(os_[hh], gn_ref[...]) * _silu(z_ref[r, cols]))

    def blk(width):
        return pl.BlockSpec((hp, GDN_ROWS, width), lambda h, i: (h, i, 0))

    return pl.pallas_call(
        body, grid=(heads // hp, nb),
        in_specs=[blk(dv), blk(dk), blk(GDN_ROWS), blk(dk), blk(dk), blk(1),
                  pl.BlockSpec((GDN_ROWS, hp * dv), lambda h, i: (i, col_z // (hp * dv) + h)),
                  pl.BlockSpec((1, dv), lambda h, i: (0, 0))],
        out_specs=[blk(dv), pl.BlockSpec((GDN_ROWS, hp * dv), lambda h, i: (i, h)),
                   pl.BlockSpec((hp, sub, dk, dv), lambda h, i: (h, i, 0, 0))],
        out_shape=[jax.ShapeDtypeStruct((heads, t, dv), F32), jax.ShapeDtypeStruct((t, heads * dv), BF16),
                   jax.ShapeDtypeStruct((heads, t // CHUNK, dk, dv), F32)],
        scratch_shapes=[pltpu.VMEM((hp, dk, dv), F32)], name=name,
        compiler_params=_params("parallel", "arbitrary"))(u, w, p, qd, kd, egl, proj, gain)


def _gdn_post_bwd(o, proj, gain, dout, *, col_z, name):
    heads, t, dv = o.shape
    tm = _pick(t, (384, 256, 128))

    def body(o_ref, z_ref, gn_ref, d_ref, do_ref, dz_ref, dg_ref):
        @pl.when((pl.program_id(0) == 0) & (pl.program_id(1) == 0))
        def _():
            dg_ref[...] = jnp.zeros_like(dg_ref)

        o_, z, d = o_ref[...], z_ref[...], d_ref[...]
        y = _rms_fwd(o_, gn_ref[...])
        dz_ref[...] = _bf(d * y * _silu_grad(z))
        do, dgr = _rms_bwd(o_, gn_ref[...], d * _silu(z))
        do_ref[...] = do
        dg_ref[...] += jnp.sum(dgr, axis=0, keepdims=True)

    return pl.pallas_call(
        body, grid=(t // tm, heads),
        in_specs=[pl.BlockSpec((None, tm, dv), lambda i, h: (h, i, 0)),
                  pl.BlockSpec((tm, dv), lambda i, h: (i, col_z // dv + h)),
                  pl.BlockSpec((1, dv), lambda i, h: (0, 0)), pl.BlockSpec((tm, dv), lambda i, h: (i, h))],
        out_specs=[pl.BlockSpec((None, tm, dv), lambda i, h: (h, i, 0)), pl.BlockSpec((tm, dv), lambda i, h: (i, h)),
                   pl.BlockSpec((1, dv), lambda i, h: (0, 0))],
        out_shape=[jax.ShapeDtypeStruct((heads, t, dv), F32), jax.ShapeDtypeStruct((t, heads * dv), BF16),
                   jax.ShapeDtypeStruct((1, dv), F32)], name=name,
        compiler_params=_params("arbitrary", "arbitrary"))(o, proj, gain, dout)


def _gdn_bwd_scan(u, w, p, qd, kd, egl, st, do, *, name):
    heads, t, dv = u.shape
    dk = w.shape[2]
    nb = t // GDN_ROWS
    sub = GDN_ROWS // CHUNK
    hp = SCAN_HEADS

    def body(u_ref, w_ref, p_ref, qd_ref, kd_ref, egl_ref, st_ref, do_ref,
             du_ref, dw_ref, dp_ref, dqd_ref, dkd_ref, dgl_ref, ds_ref):
        @pl.when(pl.program_id(1) == 0)
        def _():
            ds_ref[...] = jnp.zeros_like(ds_ref)

        hs = range(hp)
        zeros = jnp.zeros((CHUNK, dv), BF16)
        for c in reversed(range(sub)):
            r = pl.ds(c * CHUNK, CHUNK)
            ss = [st_ref[hh, c] for hh in hs]
            sbs = [_bf(s) for s in ss]
            dss = [ds_ref[hh] for hh in hs]
            dsbs = [_bf(ds) for ds in dss]
            dobs = [_bf(do_ref[hh, r, :]) for hh in hs]
            wbs = [_bf(w_ref[hh, r, :]) for hh in hs]
            vns = [u_ref[hh, r, :] - _dot(wbs[hh], sbs[hh]) for hh in hs]
            dvns = [_dot_tn(_bf(p_ref[hh, r, :]), dobs[hh])[c * CHUNK:(c + 1) * CHUNK, :]
                    + _dot(_bf(kd_ref[hh, r, :]), dsbs[hh]) for hh in hs]
            dods = [jnp.concatenate([dobs[hh], _bf(dvns[hh])], axis=0) for hh in hs]
            boths = [_dot_nt(dods[hh], sbs[hh]) for hh in hs]
            dps = [_dot_nt(dobs[hh], jnp.concatenate([_bf(vns[hh]) if cc == c else zeros for cc in range(sub)], axis=0))
                   for hh in hs]
            dkds = [_dot_nt(_bf(vns[hh]), dsbs[hh]) for hh in hs]
            new = [dss[hh] * egl_ref[hh, pl.ds(c * CHUNK, 1), :]
                   + _dot_tn(jnp.concatenate([_bf(qd_ref[hh, r, :]), -wbs[hh]], axis=0), dods[hh])
                   for hh in hs]
            for hh in hs:
                du_ref[hh, r, :] = dvns[hh]
                dw_ref[hh, r, :] = -boths[hh][CHUNK:]
                dp_ref[hh, r, :] = jnp.where(_chunk_masks(CHUNK, c * CHUNK)[1], dps[hh], 0.0)
                dqd_ref[hh, r, :] = boths[hh][:CHUNK]
                dkd_ref[hh, r, :] = dkds[hh]
                dgl = jnp.sum(jnp.sum(dss[hh] * ss[hh], axis=1, keepdims=True), axis=0, keepdims=True)
                dgl_ref[hh, r, :] = jnp.where(_iota((CHUNK, 1), 0) == CHUNK - 1, dgl, 0.0)
                ds_ref[hh] = new[hh]

    def blk(width):
        return pl.BlockSpec((hp, GDN_ROWS, width), lambda h, i: (h, nb - 1 - i, 0))

    def shp(width):
        return jax.ShapeDtypeStruct((heads, t, width), F32)

    return pl.pallas_call(
        body, grid=(heads // hp, nb),
        in_specs=[blk(dv), blk(dk), blk(GDN_ROWS), blk(dk), blk(dk), blk(1),
                  pl.BlockSpec((hp, sub, dk, dv), lambda h, i: (h, nb - 1 - i, 0, 0)), blk(dv)],
        out_specs=[blk(dv), blk(dk), blk(GDN_ROWS), blk(dk), blk(dk), blk(1)],
        out_shape=[shp(dv), shp(dk), shp(GDN_ROWS), shp(dk), shp(dk), shp(1)],
        scratch_shapes=[pltpu.VMEM((hp, dk, dv), F32)], name=name,
        compiler_params=_params("parallel", "arbitrary"))(u, w, p, qd, kd, egl, st, do)


def _gdn_bwd_prep(qn, kn, v, g, beta, tinv, u, w, du, dw, dp, dqd, dkd, dgl, *, name):
    heads, t, dk = qn.shape
    dv = v.shape[2]
    rows = _pick(t, (3 * GDN_ROWS, 2 * GDN_ROWS, GDN_ROWS))
    dk_scale = dk ** -0.5

    def rowsum(x):
        return jnp.sum(x, axis=1, keepdims=True)

    def body(q_ref, k_ref, v_ref, g_ref, b_ref, t_ref, u_ref, w_ref, du_ref, dw_ref, dp_ref, dqd_ref, dkd_ref, dgl_ref,
             dq_ref, dkk_ref, dvv_ref, dg_ref, db_ref):
        rs = [pl.ds(b * GDN_ROWS, GDN_ROWS) for b in range(rows // GDN_ROWS)]
        cs = _chunk_common([(q_ref[r, :], k_ref[r, :], g_ref[r, :], b_ref[r, :]) for r in rs], dk_scale)
        dbvws = [_dot_hp(t_ref[r, :], jnp.concatenate([du_ref[r, :], dw_ref[r, :]], axis=1), _dot_tn)
                 for r in rs]
        das = [-_dot_nt(_bf(dbvw), jnp.concatenate([_bf(u_ref[r, :]), w_ref[r, :]], axis=1))
               for r, dbvw in zip(rs, dbvws)]
        dls = [jnp.where(c["strict"], da, 0.0) for c, da in zip(cs, das)]
        dmns = [_bf(jnp.concatenate([dl * c["decay"], dp_ref[r, :] * c["decay"]], axis=0))
                for r, c, dl in zip(rs, cs, dls)]
        boths = [_dot(dmn, _bf(k_ref[r, :])) for r, dmn in zip(rs, dmns)]
        dkns = [_dot_tn(dmn, _bf(jnp.concatenate([c["kb"], c["qt"]], axis=0)))
                for c, dmn in zip(cs, dmns)]
        for r, c, dbvw, dl, both, dkn in zip(rs, cs, dbvws, dls, boths, dkns):
            kn_, beta_, v_ = k_ref[r, :], b_ref[r, :], v_ref[r, :]
            eye = c["eye"]
            kb, qt, eg, ek = c["kb"], c["qt"], c["eg"], c["ek"]
            dbv, dbw = dbvw[:, :dv], dbvw[:, dv:]
            dp_ = dp_ref[r, :]
            dkb = both[:GDN_ROWS] + dbw * eg
            dqt = both[GDN_ROWS:]
            gmat = dl * c["lmat"] + dp_ * c["pmat"]
            dqd_, dkd_ = dqd_ref[r, :], dkd_ref[r, :]
            qd = qt * eg
            kd = kn_ * ek
            bw = kb * eg
            kdsum = rowsum(dkd_ * kd)
            dgam = rowsum(gmat) - _row_to_col(jnp.sum(gmat, axis=0, keepdims=True), eye)
            dgam += rowsum(dbw * bw) + rowsum(dqd_ * qd) - kdsum
            last = (_iota((GDN_ROWS, 1), 0) & (CHUNK - 1)) == CHUNK - 1
            same = _chunk_masks()[0]
            same_f = jnp.where(same, 1.0, 0.0).astype(BF16)
            chunk_tot = jnp.max(_dot_exact_l(same_f, jnp.broadcast_to(kdsum, (GDN_ROWS, LANES))), axis=1, keepdims=True)
            dgam += jnp.where(last, chunk_tot, 0.0) + dgl_ref[r, :] * c["egl"]
            dq_ref[r, :] = (dqt + dqd_ * eg) * dk_scale
            dkk_ref[r, :] = dkn + dkd_ * ek + dkb * beta_
            dvv_ref[r, :] = dbv * beta_
            db_ref[r, :] = rowsum(dbv * v_) + rowsum(dkb * kn_)
            upper = jnp.where(same & (_iota((GDN_ROWS, GDN_ROWS), 0) <= _iota((GDN_ROWS, GDN_ROWS), 1)), 1.0, 0.0)
            dgb = _dot_exact_l(upper.astype(BF16), jnp.broadcast_to(dgam, (GDN_ROWS, LANES)))
            dg_ref[r, :] = _lane_pick(dgb, 0)

    def blk(width):
        return pl.BlockSpec((None, rows, width), lambda h, i: (h, i, 0))

    def shp(width):
        return jax.ShapeDtypeStruct((heads, t, width), F32)

    return pl.pallas_call(
        body, grid=(heads, t // rows),
        in_specs=[blk(dk), blk(dk), blk(dv), blk(1), blk(1), blk(GDN_ROWS), blk(dv), blk(dk),
                  blk(dv), blk(dk), blk(GDN_ROWS), blk(dk), blk(dk), blk(1)],
        out_specs=[blk(dk), blk(dk), blk(dv), blk(1), blk(1)],
        out_shape=[shp(dk), shp(dk), shp(dv), shp(1), shp(1)], name=name,
        compiler_params=_params("parallel", "parallel"))(qn, kn, v, g, beta, tinv, u, w, du, dw, dp, dqd, dkd, dgl)


def _gdn_pre_bwd_a(proj, pab, cq, ck, cv, a_log, dt_bias, dqn, dkn, dvv, dg, dbeta, *,
                   heads, dk, dv, col_q, col_k, col_v, row_lo, row_hi, name):
    t = proj.shape[0]
    tm = _pick(t, (384, 256, 128))
    nb = t // tm

    def body(pq_ref, pqp_ref, pk_ref, pkp_ref, pv_ref, pvp_ref, ab_ref, cq_ref, ck_ref, cv_ref, al_ref, dt_ref,
             dqn_ref, dkn_ref, dvv_ref, dg_ref, db_ref, dcq_ref, dck_ref, dcv_ref, dab_ref, dal_ref, ddt_ref,
             bq_ref, bk_ref, bv_ref):
        i, h = pl.program_id(0), pl.program_id(1)
        first = i == 0

        @pl.when((i == 0) & (h == 0))
        def _():
            dal_ref[...] = jnp.zeros_like(dal_ref)
            ddt_ref[...] = jnp.zeros_like(ddt_ref)

        @pl.when(h == 0)
        def _():
            dab_ref[...] = jnp.zeros_like(dab_ref)

        row = i * tm + _iota((tm, 1), 0)
        valid = (row >= row_lo) & (row < row_hi)

        def l2_bwd(c1, dn):
            x1 = _silu(c1)
            r = lax.rsqrt(jnp.sum(x1 * x1, axis=-1, keepdims=True) + L2_EPS)
            dn = jnp.where(valid, dn, 0.0)
            d1 = r * dn - x1 * (r * r * r) * jnp.sum(dn * x1, axis=-1, keepdims=True)
            return d1 * _silu_grad(c1)

        _stage(bq_ref, jnp.where(first, 0.0, pqp_ref[...]), pq_ref[...])
        _stage(bk_ref, jnp.where(first, 0.0, pkp_ref[...]), pk_ref[...])
        _stage(bv_ref, jnp.where(first, 0.0, pvp_ref[...]), pv_ref[...])
        dcq_ref[...] = l2_bwd(_conv_taps(bq_ref, tm, cq_ref), dqn_ref[...])
        dck_ref[...] = l2_bwd(_conv_taps(bk_ref, tm, ck_ref), dkn_ref[...])
        cv1 = _conv_taps(bv_ref, tm, cv_ref)
        dcv_ref[...] = jnp.where(valid, dvv_ref[...], 0.0) * _silu_grad(cv1)
        ab = ab_ref[...]
        da = _lane_pick(ab, h)
        db = _lane_pick(ab, heads + h)
        a = _lane_pick(al_ref[...], h)
        dtb = _lane_pick(dt_ref[...], h)
        dgv = jnp.where(valid, dg_ref[...], 0.0)
        ea = jnp.exp(a)
        g = -ea * _softplus(da + dtb)
        dda = dgv * (-ea) * _sigmoid(da + dtb)
        beta = _sigmoid(db)
        ddb = jnp.where(valid, db_ref[...], 0.0) * beta * (1.0 - beta)
        lane = _iota((tm, LANES), 1)
        dab_ref[...] += jnp.where(lane == h, dda, 0.0) + jnp.where(lane == heads + h, ddb, 0.0)
        lane1 = _iota((1, LANES), 1)
        dal_ref[...] += jnp.where(lane1 == h, jnp.sum(dgv * g, axis=0, keepdims=True), 0.0)
        ddt_ref[...] += jnp.where(lane1 == h, jnp.sum(dda, axis=0, keepdims=True), 0.0)

    def cur(width, col):
        return pl.BlockSpec((tm, width), lambda i, h: (i, col // width + h))

    def prev(width, col):
        return pl.BlockSpec((8, width), lambda i, h: (jnp.maximum(i * (tm // 8) - 1, 0), col // width + h))

    def hd(width):
        return pl.BlockSpec((None, tm, width), lambda i, h: (h, i, 0))

    small = pl.BlockSpec((1, LANES), lambda i, h: (0, 0))
    return pl.pallas_call(
        body, grid=(nb, heads),
        in_specs=[cur(dk, col_q), prev(dk, col_q), cur(dk, col_k), prev(dk, col_k), cur(dv, col_v), prev(dv, col_v),
                  pl.BlockSpec((tm, LANES), lambda i, h: (i, 0)),
                  pl.BlockSpec((cq.shape[0], dk), lambda i, h: (0, h)), pl.BlockSpec((ck.shape[0], dk), lambda i, h: (0, h)),
                  pl.BlockSpec((cv.shape[0], dv), lambda i, h: (0, h)), small, small,
                  hd(dk), hd(dk), hd(dv), hd(1), hd(1)],
        out_specs=[hd(dk), hd(dk), hd(dv), pl.BlockSpec((tm, LANES), lambda i, h: (i, 0)), small, small],
        out_shape=[jax.ShapeDtypeStruct((heads, t, dk), F32), jax.ShapeDtypeStruct((heads, t, dk), F32),
                   jax.ShapeDtypeStruct((heads, t, dv), F32), jax.ShapeDtypeStruct((t, LANES), F32),
                   jax.ShapeDtypeStruct((1, LANES), F32), jax.ShapeDtypeStruct((1, LANES), F32)],
        scratch_shapes=[pltpu.VMEM((HALO + tm, dk), F32), pltpu.VMEM((HALO + tm, dk), F32),
                        pltpu.VMEM((HALO + tm, dv), F32)], name=name,
        compiler_params=_params("arbitrary", "arbitrary"))(
            proj, proj, proj, proj, proj, proj, pab, cq, ck, cv, a_log, dt_bias, dqn, dkn, dvv, dg, dbeta)


def _conv_bwd(proj, dc, cw, *, heads, width, col, name):
    t = proj.shape[0]
    tm = _pick(t, (384, 256, 128))
    nb = t // tm
    nk = cw.shape[0]

    def body(p_ref, pp_ref, d_ref, dn_ref, w_ref, dp_ref, dw_ref, bx_ref, bd_ref):
        i = pl.program_id(1)
        first, last = i == 0, i == nb - 1

        @pl.when(first)
        def _():
            dw_ref[...] = jnp.zeros_like(dw_ref)

        d = d_ref[...]
        _stage(bx_ref, jnp.where(first, 0.0, pp_ref[...]), p_ref[...])
        _stage_after(bd_ref, d, jnp.where(last, 0.0, dn_ref[...]))
        dx = d * w_ref[nk - 1:nk, :]
        dw_ref[nk - 1:nk, :] += jnp.sum(d * p_ref[...], axis=0, keepdims=True)
        for s in range(1, nk):
            dx += bd_ref[s:s + tm, :] * w_ref[nk - 1 - s:nk - s, :]
            dw_ref[nk - 1 - s:nk - s, :] += jnp.sum(d * bx_ref[HALO - s:HALO - s + tm, :], axis=0, keepdims=True)
        dp_ref[...] = _bf(dx)

    return pl.pallas_call(
        body, grid=(heads, nb),
        in_specs=[pl.BlockSpec((tm, width), lambda h, i: (i, col // width + h)),
                  pl.BlockSpec((8, width), lambda h, i: (jnp.maximum(i * (tm // 8) - 1, 0), col // width + h)),
                  pl.BlockSpec((None, tm, width), lambda h, i: (h, i, 0)),
                  pl.BlockSpec((None, 8, width), lambda h, i: (h, jnp.minimum((i + 1) * (tm // 8), t // 8 - 1), 0)),
                  pl.BlockSpec((nk, width), lambda h, i: (0, h))],
        out_specs=[pl.BlockSpec((tm, width), lambda h, i: (i, h)), pl.BlockSpec((nk, width), lambda h, i: (0, h))],
        out_shape=[jax.ShapeDtypeStruct((t, heads * width), BF16), jax.ShapeDtypeStruct((nk, heads * width), F32)],
        scratch_shapes=[pltpu.VMEM((HALO + tm, width), F32), pltpu.VMEM((tm + HALO, width), F32)],
        name=name, compiler_params=_params("parallel", "arbitrary"))(proj, proj, dc, dc, cw)


def _sb_pre(proj, gq, gk, *, heads, dh, col_q, col_k, col_v, name):
    t = proj.shape[0]
    tm = _pick(t, (384, 256, 128))

    def body(q_ref, k_ref, v_ref, gq_ref, gk_ref, qo_ref, ko_ref, vo_ref):
        qo_ref[...] = _bf(_rms_fwd(q_ref[...], gq_ref[...]))
        ko_ref[...] = _bf(_rms_fwd(k_ref[...], gk_ref[...]))
        vo_ref[...] = _bf(v_ref[...])

    def cur(col):
        return pl.BlockSpec((tm, dh), lambda i, h: (i, col // dh + h))

    gspec = pl.BlockSpec((1, dh), lambda i, h: (0, 0))
    ospec = pl.BlockSpec((tm, dh), lambda i, h: (i, h))
    return pl.pallas_call(
        body, grid=(t // tm, heads), in_specs=[cur(col_q), cur(col_k), cur(col_v), gspec, gspec],
        out_specs=[ospec] * 3, out_shape=[jax.ShapeDtypeStruct((t, heads * dh), BF16)] * 3, name=name,
        compiler_params=_params("parallel", "parallel"))(proj, proj, proj, gq, gk)


def _sb_tile(z, i, j, blk, key_lo, masked):
    ls = jnp.minimum(z, 0.0) - jnp.log(1.0 + jnp.exp(-jnp.abs(z)))
    if not masked:
        return None, ls, ls - z
    qpos = i * blk + _iota((blk, blk), 0)
    kpos = j * blk + _iota((blk, blk), 1)
    vis = (kpos < qpos) & (kpos >= key_lo)
    return vis, ls, jnp.where(vis, ls - z, 0.0)


def _where_vis(vis, x):
    return x if vis is None else jnp.where(vis, x, 0.0)


def _sb_sweep(i, step, init, descending):
    first, last = (i, 0) if descending else (0, i)
    carry = step(first, init, True)
    carry = lax.fori_loop(1, i, lambda n, c: step(i - n if descending else n, c, False), carry)
    return lax.cond(i > 0, lambda c: step(last, c, True), lambda c: c, carry)


def _dot2_r(x, m):
    hi, lo = _split2(x)
    return _dot(hi, m) + _dot(lo, m)


def _running_sums(x, tri, reverse):
    groups = [x[:, s:s + LANES] for s in range(0, x.shape[1], LANES)]
    inside = [_dot2_r(g, tri) for g in groups]
    sums = [jnp.sum(g, axis=1, keepdims=True) for g in groups]
    order = list(reversed(range(len(groups)))) if reverse else list(range(len(groups)))
    out, acc = [None] * len(groups), None
    for gi in order:
        out[gi] = inside[gi] if acc is None else inside[gi] + acc
        acc = sums[gi] if acc is None else acc + sums[gi]
    return jnp.concatenate(out, axis=1), acc


def _sb_fwd(qs, ks, vs, *, heads, dh, key_lo, name, gather=None):
    t = qs.shape[0]
    blk = _pick(t, (3 * SB_BLOCK, 2 * SB_BLOCK, SB_BLOCK))
    assert key_lo <= blk
    nq = t // blk
    assert nq <= LANES
    scale = dh ** -0.5
    hp = SB_HEADS_PER_STEP

    def body(q_ref, k_ref, v_ref, o_ref, c_ref):
        i = pl.program_id(1)
        later = jnp.where(_iota((LANES, LANES), 0) > _iota((LANES, LANES), 1), 1.0, 0.0).astype(BF16)
        lane = _iota((blk, LANES), 1)
        c_ref[...] = jnp.zeros_like(c_ref)

        def step(j, carry, masked):
            rows = pl.ds(pl.multiple_of(j * blk, blk), blk)
            hs = range(hp)
            cols = [pl.ds(hh * dh, dh) for hh in hs]
            zs = [_dot_nt(q_ref[:, cols[hh]], k_ref[rows, cols[hh]]) * scale for hh in hs]
            tiles = [_sb_tile(z, i, j, blk, key_lo, masked) for z in zs]
            sufs = [_running_sums(lk, later, reverse=True) for _, _, lk in tiles]
            wgts = [_where_vis(vis, jnp.exp(ls + suf + carry[2 * hh + 1]))
                    for hh, ((vis, ls, _), (suf, _)) in enumerate(zip(tiles, sufs))]
            accs = [carry[2 * hh] + _dot(_bf(wgts[hh]), v_ref[rows, cols[hh]]) for hh in hs]
            out = []
            for hh in hs:
                c_ref[hh] = jnp.where(lane == j, carry[2 * hh + 1], c_ref[hh])
                out += [accs[hh], carry[2 * hh + 1] + sufs[hh][1]]
            return tuple(out)

        res = _sb_sweep(i, step, (jnp.zeros((blk, dh), F32), jnp.zeros((blk, 1), F32)) * hp, descending=True)
        for hh in range(hp):
            o_ref[:, pl.ds(hh * dh, dh)] = _bf(res[2 * hh])

    full = pl.BlockSpec((t, hp * dh), lambda h, i: (0, h))
    call = dict(grid=(heads // hp, nq),
                in_specs=[pl.BlockSpec((blk, hp * dh), lambda h, i: (i, h)), full, full],
                out_specs=[pl.BlockSpec((blk, hp * dh), lambda h, i: (i, h)),
                           pl.BlockSpec((hp, blk, LANES), lambda h, i: (h, i, 0))],
                out_shape=[jax.ShapeDtypeStruct((t, heads * dh), BF16), jax.ShapeDtypeStruct((heads, t, LANES), F32)],
                name=name)
    if gather is None:
        return pl.pallas_call(body, compiler_params=_params("parallel", "parallel"), **call)(qs, ks, vs)
    res = _call_with_exchange(body, scratch_shapes=[], args=(qs, ks, vs), srcs=gather, scatter=[False] * len(gather),
                              **call)
    return [*res[:2], list(res[2:])]


def _sb_bwd(qs, ks, vs, do, carry, *, heads, dh, key_lo, name, scatter=None):
    t = qs.shape[0]
    blk = _pick(t, (3 * SB_BLOCK, 2 * SB_BLOCK, SB_BLOCK))
    assert key_lo <= blk
    nq = t // blk
    scale = dh ** -0.5
    hp = SB_HEADS_PER_STEP

    def body(q_ref, k_ref, v_ref, do_ref, c_ref, dq_ref, dk_ref, dv_ref):
        i = pl.program_id(1)

        @pl.when(i == 0)
        def _():
            dk_ref[...] = jnp.zeros_like(dk_ref)
            dv_ref[...] = jnp.zeros_like(dv_ref)

        r0 = _iota((LANES, LANES), 0)
        r1 = _iota((LANES, LANES), 1)
        later = jnp.where(r0 > r1, 1.0, 0.0).astype(BF16)
        earlier = jnp.where(r0 < r1, 1.0, 0.0).astype(BF16)

        def step(j, carry, masked):
            rows = pl.ds(pl.multiple_of(j * blk, blk), blk)
            hs = range(hp)
            cols = [pl.ds(hh * dh, dh) for hh in hs]
            zs = [_dot_nt(q_ref[:, cols[hh]], k_ref[rows, cols[hh]]) * scale for hh in hs]
            dws = [_dot_nt(do_ref[:, cols[hh]], v_ref[rows, cols[hh]]) for hh in hs]
            tiles = [_sb_tile(z, i, j, blk, key_lo, masked) for z in zs]
            sufs = [_running_sums(lk, later, reverse=True)[0] for _, _, lk in tiles]
            wgts = [_where_vis(vis, jnp.exp(ls + suf + _lane_pick(c_ref[hh], j)))
                    for hh, ((vis, ls, _), suf) in enumerate(zip(tiles, sufs))]
            es = [wgt * dw for wgt, dw in zip(wgts, dws)]
            pres = [_running_sums(e, earlier, reverse=False) for e in es]
            dzs = []
            for hh in hs:
                vis, ls, _ = tiles[hh]
                before = _where_vis(vis, pres[hh][0] + carry[2 * hh + 1])
                sig = jnp.exp(ls)
                dzs.append(_bf((es[hh] * (1.0 - sig) - before * sig) * scale))
            dks = [_dot_tn(dzs[hh], q_ref[:, cols[hh]]) for hh in hs]
            dvs = [_dot_tn(_bf(wgts[hh]), do_ref[:, cols[hh]]) for hh in hs]
            dqs = [carry[2 * hh] + _dot(dzs[hh], k_ref[rows, cols[hh]]) for hh in hs]
            out = []
            for hh in hs:
                dk_ref[rows, cols[hh]] += dks[hh]
                dv_ref[rows, cols[hh]] += dvs[hh]
                out += [dqs[hh], carry[2 * hh + 1] + pres[hh][1]]
            return tuple(out)

        res = _sb_sweep(i, step, (jnp.zeros((blk, dh), F32), jnp.zeros((blk, 1), F32)) * hp, descending=False)
        for hh in range(hp):
            dq_ref[:, pl.ds(hh * dh, dh)] = res[2 * hh]

    full = pl.BlockSpec((t, hp * dh), lambda h, i: (0, h))
    qblk = pl.BlockSpec((blk, hp * dh), lambda h, i: (i, h))
    call = dict(grid=(heads // hp, nq),
                in_specs=[qblk, full, full, qblk, pl.BlockSpec((hp, blk, LANES), lambda h, i: (h, i, 0))],
                out_specs=[qblk, full, full], out_shape=[jax.ShapeDtypeStruct((t, heads * dh), F32)] * 3, name=name)
    if scatter is None:
        return pl.pallas_call(body, compiler_params=_params("parallel", "arbitrary"), **call)(qs, ks, vs, do, carry)
    res = _call_with_exchange(body, scratch_shapes=[], args=(qs, ks, vs, do, carry), srcs=scatter,
                              scatter=[True] * len(scatter), **call)
    return [*res[:3], list(res[3:])]


def _sb_pre_bwd(proj, gq, gk, dq, dk, dv, *, heads, dh, col_q, col_k, name):
    t = proj.shape[0]
    tm = _pick(t, (384, 256, 128))

    def body(q_ref, k_ref, gq_ref, gk_ref, dq_ref, dk_ref, dv_ref, oq_ref, ok_ref, ov_ref, dgq_ref, dgk_ref):
        @pl.when((pl.program_id(0) == 0) & (pl.program_id(1) == 0))
        def _():
            dgq_ref[...] = jnp.zeros_like(dgq_ref)
            dgk_ref[...] = jnp.zeros_like(dgk_ref)

        dq_, gq_r = _rms_bwd(q_ref[...], gq_ref[...], dq_ref[...])
        dk_, gk_r = _rms_bwd(k_ref[...], gk_ref[...], dk_ref[...])
        oq_ref[...] = _bf(dq_)
        ok_ref[...] = _bf(dk_)
        ov_ref[...] = _bf(dv_ref[...])
        dgq_ref[...] += jnp.sum(gq_r, axis=0, keepdims=True)
        dgk_ref[...] += jnp.sum(gk_r, axis=0, keepdims=True)

    def cur(col):
        return pl.BlockSpec((tm, dh), lambda i, h: (i, col // dh + h))

    gspec = pl.BlockSpec((1, dh), lambda i, h: (0, 0))
    ospec = pl.BlockSpec((tm, dh), lambda i, h: (i, h))
    return pl.pallas_call(
        body, grid=(t // tm, heads), in_specs=[cur(col_q), cur(col_k), gspec, gspec, ospec, ospec, ospec],
        out_specs=[ospec, ospec, ospec, gspec, gspec],
        out_shape=[jax.ShapeDtypeStruct((t, heads * dh), BF16)] * 3 + [jax.ShapeDtypeStruct((1, dh), F32)] * 2,
        name=name, compiler_params=_params("arbitrary", "arbitrary"))(proj, proj, gq, gk, dq, dk, dv)


PEERS = N_DEV - 1


def _exchange_copies(ins, outs, send_sems, recv_sems, local_sems, scatter):
    x, y, c = lax.axis_index("x"), lax.axis_index("y"), lax.axis_index("c")
    me = 4 * x + 2 * y + c
    copies = []
    for a in range(len(ins)):
        own = ins[a].at[me] if scatter[a] else ins[a]
        copies.append(pltpu.make_async_copy(own, outs[a].at[me], local_sems.at[a]))
        for k in range(1, N_DEV):
            px = (x + (k >> 2 & 1)) % 2
            py = (y + (k >> 1 & 1)) % 2
            pc = (c + (k & 1)) % 2
            src = ins[a].at[4 * px + 2 * py + pc] if scatter[a] else ins[a]
            copies.append(pltpu.make_async_remote_copy(
                src_ref=src, dst_ref=outs[a].at[me], send_sem=send_sems.at[a * PEERS + k - 1],
                recv_sem=recv_sems.at[a * PEERS + k - 1], device_id=(px, py, pc), device_id_type=MESH))
    return copies


def _exchange_shapes(srcs, scatter):
    return [jax.ShapeDtypeStruct(s.shape if sc else (N_DEV,) + s.shape, s.dtype) for s, sc in zip(srcs, scatter)]


def _exchange_sems(n):
    return [pltpu.SemaphoreType.DMA((n * PEERS,)), pltpu.SemaphoreType.DMA((n * PEERS,)), pltpu.SemaphoreType.DMA((n,))]


def _exchange(srcs, *, scatter, name):
    n = len(srcs)

    def body(*refs):
        copies = _exchange_copies(refs[:n], refs[n:2 * n], *refs[2 * n:], scatter)
        for cp in copies:
            cp.start()
        for cp in copies:
            cp.wait()

    any_spec = pl.BlockSpec(memory_space=pl.ANY)
    return pl.pallas_call(
        body, in_specs=[any_spec] * n, out_specs=[any_spec] * n, out_shape=_exchange_shapes(srcs, scatter),
        scratch_shapes=_exchange_sems(n), name=name,
        compiler_params=pltpu.CompilerParams(has_side_effects=True))(*srcs)


def _gather_two_level(srcs, *, name):
    n = len(srcs)

    def body(*refs):
        ins, outs = refs[:n], refs[n:2 * n]
        send_sems, recv_sems, local_sems = refs[2 * n:]
        x, y, c = lax.axis_index("x"), lax.axis_index("y"), lax.axis_index("c")
        chips = [(1 - x, y), (x, 1 - y), (1 - x, 1 - y)]

        def slab(a, px, py, pc):
            return outs[a].at[4 * px + 2 * py + pc]

        def copy(a, k, block, to, src=None):
            return pltpu.make_async_remote_copy(
                src_ref=slab(a, *block) if src is None else src, dst_ref=slab(a, *block),
                send_sem=send_sems.at[a * PEERS + k], recv_sem=recv_sems.at[a * PEERS + k],
                device_id=to, device_id_type=MESH)

        mine = [pltpu.make_async_copy(ins[a], slab(a, x, y, c), local_sems.at[a]) for a in range(n)]
        first = [copy(a, 0, (x, y, c), (x, y, 1 - c), src=ins[a]) for a in range(n)]
        first += [copy(a, 1 + j, (x, y, c), (*chip, c), src=ins[a]) for j, chip in enumerate(chips) for a in range(n)]
        for cp in mine + first:
            cp.start()
        passed = []
        for j, chip in enumerate(chips):
            for a in range(n):
                copy(a, 1 + j, (*chip, c), (x, y, c)).wait_recv()
                passed.append(copy(a, 4 + j, (*chip, c), (x, y, 1 - c)))
                passed[-1].start()
        for a in range(n):
            copy(a, 0, (x, y, 1 - c), (x, y, c)).wait_recv()
            for j, chip in enumerate(chips):
                copy(a, 4 + j, (*chip, 1 - c), (x, y, c)).wait_recv()
        for cp in first + passed:
            cp.wait_send()
        for cp in mine:
            cp.wait()

    any_spec = pl.BlockSpec(memory_space=pl.ANY)
    return pl.pallas_call(
        body, in_specs=[any_spec] * n, out_specs=[any_spec] * n, out_shape=_exchange_shapes(srcs, [False] * n),
        scratch_shapes=_exchange_sems(n), name=name,
        compiler_params=pltpu.CompilerParams(has_side_effects=True))(*srcs)


def _call_with_exchange(body, *, grid, in_specs, out_specs, out_shape, scratch_shapes, args, srcs, scatter, name):
    n, n_in, n_out, n_scr = len(srcs), len(args), len(out_shape), len(scratch_shapes)

    def full_body(*refs):
        ins, xin = refs[:n_in], refs[n_in:n_in + n]
        outs, xout = refs[n_in + n:n_in + n + n_out], refs[n_in + n + n_out:n_in + 2 * n + n_out]
        scr = refs[n_in + 2 * n + n_out:]
        ids = [pl.program_id(a) for a in range(len(grid))]
        first = functools.reduce(jnp.logical_and, [i == 0 for i in ids])
        last = functools.reduce(jnp.logical_and, [i == g - 1 for i, g in zip(ids, grid)])
        copies = _exchange_copies(xin, xout, *scr[n_scr:], scatter)

        @pl.when(first)
        def _():
            for cp in copies:
                cp.start()

        body(*ins, *outs, *scr[:n_scr])

        @pl.when(last)
        def _():
            for cp in copies:
                cp.wait()

    any_spec = pl.BlockSpec(memory_space=pl.ANY)
    return pl.pallas_call(
        full_body, grid=grid, in_specs=list(in_specs) + [any_spec] * n, out_specs=list(out_specs) + [any_spec] * n,
        out_shape=list(out_shape) + _exchange_shapes(srcs, scatter),
        scratch_shapes=list(scratch_shapes) + _exchange_sems(n), name=name,
        compiler_params=pltpu.CompilerParams(dimension_semantics=("arbitrary",) * len(grid),
                                             vmem_limit_bytes=V7X_VMEM_LIMIT_BYTES, has_side_effects=True))(*args, *srcs)


def _adam_math(g, w, m, v):
    m2 = ADAM_B1 * m + (1.0 - ADAM_B1) * g
    v2 = ADAM_B2 * v + (1.0 - ADAM_B2) * (g * g)
    m_hat = m2 / (1.0 - ADAM_B1 ** ADAM_STEP)
    v_hat = v2 / (1.0 - ADAM_B2 ** ADAM_STEP)
    return -ADAM_LR * (m_hat / (jnp.sqrt(v_hat) + ADAM_EPS) + ADAM_WD * w), m2, v2


def _adamw_slabs(slabs, w, m, v, *, name):
    r, c = w.shape
    tr = next((t for t in (2256, 752, 512, 240, 128, 64, 32, 16) if r % t == 0), r)

    def body(s_ref, w_ref, m_ref, v_ref, g_ref, d_ref, mo_ref, vo_ref):
        g = s_ref[0].astype(F32)
        for p in range(1, N_DEV):
            g = g + s_ref[p].astype(F32)
        g_ref[...] = g
        d_ref[...], mo_ref[...], vo_ref[...] = _adam_math(g, w_ref[...], m_ref[...], v_ref[...])

    spec = pl.BlockSpec((tr, c), lambda i: (i, 0))
    return pl.pallas_call(
        body, grid=(r // tr,), in_specs=[pl.BlockSpec((N_DEV, tr, c), lambda i: (0, i, 0)), spec, spec, spec],
        out_specs=[spec] * 4, out_shape=[jax.ShapeDtypeStruct((r, c), F32)] * 4, name=name,
        compiler_params=_params("parallel"))(slabs, w, m, v)


def _adamw_small(g, w, m, v, *, name):
    def body(g_ref, w_ref, m_ref, v_ref, d_ref, mo_ref, vo_ref):
        d_ref[...], mo_ref[...], vo_ref[...] = _adam_math(g_ref[...], w_ref[...], m_ref[...], v_ref[...])

    return pl.pallas_call(body, out_shape=[jax.ShapeDtypeStruct(w.shape, F32)] * 3, name=name)(g, w, m, v)


def _sum_slabs(slabs, *, name):
    def body(s_ref, o_ref):
        acc = s_ref[0]
        for p in range(1, N_DEV):
            acc = acc + s_ref[p]
        o_ref[...] = acc

    return pl.pallas_call(body, out_shape=jax.ShapeDtypeStruct(slabs.shape[1:], F32), name=name)(slabs)


def _pad_lanes(a):
    return jnp.pad(a, ((0, 0), (0, LANES - a.shape[1])))


def _w_in_slabs(main, ab, col_ab, n_ab):
    pw = main.shape[0] + n_ab
    c = pw // N_DEV
    parts = [(0, col_ab, main, 0), (col_ab, col_ab + n_ab, ab, col_ab), (col_ab + n_ab, pw, main, n_ab)]
    slabs = []
    for p in range(N_DEV):
        pieces = []
        for lo, hi, src, shift in parts:
            a, b = max(lo, c * p), min(hi, c * (p + 1))
            if a < b:
                pieces.append(src[a - shift:b - shift])
        slabs.append(jnp.concatenate(pieces, axis=0))
    return jnp.stack(slabs)


def _local_step(x, target, meta, g_mix, wt_main, wt_ab, cq, ck, cv, a_log, dt_bias, g_dn, g_sbq, g_sbk, g_ffn, rest,
                shards=False):
    seq, d = x.shape
    n_meta = meta.shape[0]
    heads = a_log.shape[1]
    qk = cq.shape[1]
    dvt = cv.shape[1]
    dk, dv = qk // heads, dvt // heads
    dh = g_sbq.shape[1]
    sbw = rest[2].shape[0] * N_DEV if shards else rest[1].shape[0]
    sb_heads = sbw // dh
    pad_l = (-n_meta) % CHUNK
    row_x = pad_l + n_meta
    rows = row_x + seq
    t = -(-rows // GDN_ROWS) * GDN_ROWS
    col_q, col_k, col_v, col_z = 0, qk, 2 * qk, 2 * qk + dvt
    col_sq = 2 * qk + 2 * dvt
    col_sk, col_sv, col_gd, col_gs = col_sq + sbw, col_sq + 2 * sbw, col_sq + 3 * sbw, col_sq + 3 * sbw + d

    def rows_pad(a):
        return jnp.concatenate([jnp.zeros((row_x, d), F32), a, jnp.zeros((t - rows, d), F32)], axis=0)

    h0 = jnp.concatenate([jnp.zeros((pad_l, d), F32), meta, x, jnp.zeros((t - rows, d), F32)], axis=0)
    tgt = rows_pad(target)
    a_log_p, dt_p = _pad_lanes(a_log), _pad_lanes(dt_bias)

    proj, n1 = _mm_norm(h0, g_mix, wt_main, name="proj")
    pab = _mm_nt(n1, wt_ab, out_dtype=F32, name="proj_ab")
    gk = dict(heads=heads, dk=dk, dv=dv, col_q=col_q, col_k=col_k, col_v=col_v, row_lo=pad_l, row_hi=rows)
    qn, kn, vv, g, beta = _gdn_pre(proj, pab, cq, ck, cv, a_log_p, dt_p, name="gdn_pre", **gk)
    u, w, pm, qd, kd, egl, tinv = _gdn_prep(qn, kn, vv, g, beta, name="gdn_prep")
    o_raw, o_dn, states = _gdn_scan(u, w, pm, qd, kd, egl, proj, g_dn, col_z=col_z, name="gdn_scan")
    qs, ks, vs = _sb_pre(proj, g_sbq, g_sbk, heads=sb_heads, dh=dh, col_q=col_sq, col_k=col_sk, col_v=col_sv,
                         name="sb_pre")
    if shards:
        o_sb, carry, (g_fi, g_bd, g_bs, g_out, g_fo) = _sb_fwd(qs, ks, vs, heads=sb_heads, dh=dh, key_lo=pad_l,
                                                                name="sb_fwd", gather=list(rest))
        wt_fi = g_fi.reshape(-1, d)
        d_ff = wt_fi.shape[0] // 2
        w_bd, w_bs, w_out, wt_fg, wt_fu, w_fo = (g_bd.reshape(-1, d), g_bs.reshape(-1, d), g_out.reshape(-1, d),
                                                 wt_fi[:d_ff], wt_fi[d_ff:], g_fo.reshape(-1, d))
    else:
        o_sb, carry = _sb_fwd(qs, ks, vs, heads=sb_heads, dh=dh, key_lo=pad_l, name="sb_fwd")
        w_bd, w_bs, w_out, wt_fg, wt_fu, w_fo = rest
    merged, br_dn, br_sb = _merge_fwd(o_dn, o_sb, w_bd, w_bs, proj, col_gd=col_gd, col_gs=col_gs, name="merge")
    h1 = _mm_res(h0, merged, w_out, name="mix_out")
    gate, up, act, n2 = _mm_norm_swiglu(h1, g_ffn, wt_fg, wt_fu, name="ffn_in")
    dy, dyb, lsum = _mm_res_loss(h1, act, w_fo, tgt, row0=row_x, nrows=seq, name="ffn_out_loss")

    dgate, dup = _swiglu_bwd(dyb, w_fo, gate, up, name="ffn_out_bwd")
    d_w_fo = _mm_tn(act, dyb, name="dw_ffn_out")
    d_wt_fg = _mm_tn(dgate, n2, name="dw_ffn_gate")
    d_wt_fu = _mm_tn(dup, n2, name="dw_ffn_up")
    dh1, dh1b, d_g_ffn = _mm_rmsbwd([(dgate, wt_fg), (dup, wt_fu)], None, h1, g_ffn, dy, name="ffn_in_bwd")

    dbd, dbs, dgd, dgs = _merge_bwd(dh1b, w_out, proj, br_dn, br_sb, col_gd=col_gd, col_gs=col_gs, name="mix_out_bwd")
    d_w_out = _mm_tn(merged, dh1b, name="dw_out")
    d_w_bd = _mm_tn(o_dn, dbd, name="dw_branch_dn")
    d_w_bs = _mm_tn(o_sb, dbs, name="dw_branch_sb")
    do_dn = _mm_nt(dbd, w_bd, out_dtype=F32, name="branch_dn_bwd")
    do_sb = _mm_nt(dbs, w_bs, out_dtype=BF16, name="branch_sb_bwd")

    do_raw, dz, d_g_dn = _gdn_post_bwd(o_raw, proj, g_dn, do_dn, col_z=col_z, name="gdn_post_bwd")
    du, dw, dp, dqd, dkd, dgl = _gdn_bwd_scan(u, w, pm, qd, kd, egl, states, do_raw, name="gdn_bwd_scan")
    dqn, dkn, dvv, dg, dbeta = _gdn_bwd_prep(qn, kn, vv, g, beta, tinv, u, w, du, dw, dp, dqd, dkd, dgl,
                                            name="gdn_bwd_prep")
    dcq, dck, dcv, dpab, d_a_log, d_dt = _gdn_pre_bwd_a(proj, pab, cq, ck, cv, a_log_p, dt_p, dqn, dkn, dvv, dg, dbeta,
                                                        name="gdn_pre_bwd", **gk)
    dpq, d_cq = _conv_bwd(proj, dcq, cq, heads=heads, width=dk, col=col_q, name="conv_q_bwd")
    dpk, d_ck = _conv_bwd(proj, dck, ck, heads=heads, width=dk, col=col_k, name="conv_k_bwd")
    dpv, d_cv = _conv_bwd(proj, dcv, cv, heads=heads, width=dv, col=col_v, name="conv_v_bwd")

    early = None
    if shards:
        slabs = [_bf(jnp.concatenate([d_wt_fg, d_wt_fu], axis=0)).reshape(N_DEV, -1, LANES)]
        slabs += [_bf(a).reshape(N_DEV, -1, d) for a in (d_w_bd, d_w_bs, d_w_out, d_w_fo)]
        dqs, dks, dvs, early = _sb_bwd(qs, ks, vs, do_sb, carry, heads=sb_heads, dh=dh, key_lo=pad_l, name="sb_bwd",
                                       scatter=slabs)
    else:
        dqs, dks, dvs = _sb_bwd(qs, ks, vs, do_sb, carry, heads=sb_heads, dh=dh, key_lo=pad_l, name="sb_bwd")
    dsq, dsk, dsv, d_g_sbq, d_g_sbk = _sb_pre_bwd(proj, g_sbq, g_sbk, dqs, dks, dvs, heads=sb_heads, dh=dh,
                                                   col_q=col_sq, col_k=col_sk, name="sb_pre_bwd")

    dproj = jnp.concatenate([dpq, dpk, dpv, dz, dsq, dsk, dsv, dgd, dgs], axis=1)
    dpab_b = _bf(dpab)
    d_wt_main = _mm_tn(dproj, n1, name="dw_in_main")
    d_wt_ab = _mm_tn(dpab_b, n1, name="dw_in_ab")
    s_in = None
    if shards:
        slabs = _w_in_slabs(_bf(d_wt_main), _bf(d_wt_ab), col_sq, 2 * heads).reshape(N_DEV, -1, LANES)
        dh0, _, d_g_mix, (s_in,) = _mm_rmsbwd([(dproj, wt_main)], (dpab_b, wt_ab), h0, g_mix, dh1, name="proj_bwd",
                                              scatter=[slabs])
    else:
        dh0, _, d_g_mix = _mm_rmsbwd([(dproj, wt_main)], (dpab_b, wt_ab), h0, g_mix, dh1, name="proj_bwd")

    return dict(s_in=s_in, lsum=lsum, grad_x=dh0[row_x:rows], d_meta=dh0[pad_l:row_x], d_g_mix=d_g_mix,
                d_wt_main=d_wt_main, d_wt_ab=d_wt_ab, d_cq=d_cq, d_ck=d_ck, d_cv=d_cv, d_a_log=d_a_log[:, :heads],
                d_dt=d_dt[:, :heads], d_g_dn=d_g_dn, d_g_sbq=d_g_sbq, d_g_sbk=d_g_sbk, d_w_bd=d_w_bd, d_w_bs=d_w_bs,
                d_w_out=d_w_out, d_g_ffn=d_g_ffn, d_wt_fg=d_wt_fg, d_wt_fu=d_wt_fu, d_w_fo=d_w_fo, early=early)


def _pack(parts):
    flat = []
    for a in parts:
        a = a.reshape(-1)
        flat.append(jnp.pad(a, (0, (-a.shape[0]) % LANES)))
    v = jnp.concatenate(flat)
    v = jnp.pad(v, (0, (-v.shape[0]) % (8 * LANES)))
    return v.reshape(-1, LANES)


def _unpack(packed, shapes):
    flat = packed.reshape(-1)
    out, pos = [], 0
    for s in shapes:
        n = math.prod(s)
        out.append(flat[pos:pos + n].reshape(s))
        pos += n + (-n) % LANES
    return out


def kernel(x, meta_tokens, norm_mix_gain, w_in, conv_q, conv_k, conv_v, dn_a_log, dn_dt_bias, dn_out_norm_gain, sb_q_norm_gain, sb_k_norm_gain, w_branch_dn, w_branch_sb, w_out, norm_ffn_gain, w_ffn_in, w_ffn_out, loss_target, m_meta_tokens, m_norm_mix_gain, m_w_in, m_conv_q, m_conv_k, m_conv_v, m_dn_a_log, m_dn_dt_bias, m_dn_out_norm_gain, m_sb_q_norm_gain, m_sb_k_norm_gain, m_w_branch_dn, m_w_branch_sb, m_w_out, m_norm_ffn_gain, m_w_ffn_in, m_w_ffn_out, v_meta_tokens, v_norm_mix_gain, v_w_in, v_conv_q, v_conv_k, v_conv_v, v_dn_a_log, v_dn_dt_bias, v_dn_out_norm_gain, v_sb_q_norm_gain, v_sb_k_norm_gain, v_w_branch_dn, v_w_branch_sb, v_w_out, v_norm_ffn_gain, v_w_ffn_in, v_w_ffn_out):
    me = 4 * lax.axis_index("x") + 2 * lax.axis_index("y") + lax.axis_index("c")
    heads = dn_a_log.shape[1]
    d = x.shape[2]
    qk = conv_q.shape[2] * N_DEV
    dvt = conv_v.shape[2] * N_DEV
    col_ab = 2 * qk + 2 * dvt

    small_shapes = [meta_tokens.shape, conv_q.shape[1:], conv_k.shape[1:], conv_v.shape[1:]]
    small = _pack([meta_tokens, conv_q[0], conv_k[0], conv_v[0]])
    def features_major(a):
        return jnp.transpose(a, (2, 0, 1)).reshape(a.shape[2], a.shape[1])

    g_in, g_small = _gather_two_level([_bf(features_major(w_in)), small], name="gather_w_in")
    wt_full = g_in.reshape(-1, d)
    wt_main = jnp.concatenate([wt_full[:col_ab], wt_full[col_ab + 2 * heads:]], axis=0)
    wt_ab = jnp.pad(wt_full[col_ab:col_ab + 2 * heads], ((0, LANES - 2 * heads), (0, 0)))
    parts = [_unpack(g_small[p], small_shapes) for p in range(N_DEV)]
    meta_f, cq_f, ck_f, cv_f = (jnp.concatenate([parts[p][a] for p in range(N_DEV)], axis=1) for a in range(4))

    r = _local_step(x[0], loss_target[0], meta_f, norm_mix_gain, wt_main, wt_ab, cq_f, ck_f, cv_f, dn_a_log, dn_dt_bias,
                    dn_out_norm_gain, sb_q_norm_gain, sb_k_norm_gain, norm_ffn_gain,
                    (_bf(features_major(w_ffn_in)), _bf(w_branch_dn[0]), _bf(w_branch_sb[0]), _bf(w_out[0]),
                     _bf(w_ffn_out[0])), shards=True)
    s_fi, s_bd, s_bs, s_out, s_fo = r["early"]
    s_in = r["s_in"]

    loss_part = (0.5 / d) * jnp.sum(r["lsum"], axis=1, keepdims=True)
    small_g = [r["d_meta"], r["d_g_mix"], r["d_cq"], r["d_ck"], r["d_cv"], r["d_a_log"], r["d_dt"], r["d_g_dn"],
               r["d_g_sbq"], r["d_g_sbk"], r["d_g_ffn"], loss_part]
    (g_packs,) = _exchange([_pack(small_g)], scatter=[False], name="gather_small_grads")
    (g_meta, g_mix, g_cq, g_ck, g_cv, g_al, g_dt, g_gdn, g_sbq, g_sbk, g_ffn, loss) = _unpack(
        _sum_slabs(g_packs, name="sum_small_grads"), [a.shape for a in small_g])

    def mine(a, width):
        return lax.dynamic_slice_in_dim(a, me * width, width, axis=1)

    big = dict(w_in=(s_in, w_in, m_w_in, v_w_in), w_branch_dn=(s_bd, w_branch_dn, m_w_branch_dn, v_w_branch_dn),
               w_branch_sb=(s_bs, w_branch_sb, m_w_branch_sb, v_w_branch_sb), w_out=(s_out, w_out, m_w_out, v_w_out),
               w_ffn_in=(s_fi, w_ffn_in, m_w_ffn_in, v_w_ffn_in), w_ffn_out=(s_fo, w_ffn_out, m_w_ffn_out, v_w_ffn_out))
    tiny = dict(meta_tokens=(mine(g_meta, d // N_DEV), meta_tokens, m_meta_tokens, v_meta_tokens),
                norm_mix_gain=(g_mix, norm_mix_gain, m_norm_mix_gain, v_norm_mix_gain),
                conv_q=(mine(g_cq, qk // N_DEV), conv_q[0], m_conv_q[0], v_conv_q[0]),
                conv_k=(mine(g_ck, qk // N_DEV), conv_k[0], m_conv_k[0], v_conv_k[0]),
                conv_v=(mine(g_cv, dvt // N_DEV), conv_v[0], m_conv_v[0], v_conv_v[0]),
                dn_a_log=(g_al, dn_a_log, m_dn_a_log, v_dn_a_log), dn_dt_bias=(g_dt, dn_dt_bias, m_dn_dt_bias, v_dn_dt_bias),
                dn_out_norm_gain=(g_gdn, dn_out_norm_gain, m_dn_out_norm_gain, v_dn_out_norm_gain),
                sb_q_norm_gain=(g_sbq, sb_q_norm_gain, m_sb_q_norm_gain, v_sb_q_norm_gain),
                sb_k_norm_gain=(g_sbk, sb_k_norm_gain, m_sb_k_norm_gain, v_sb_k_norm_gain),
                norm_ffn_gain=(g_ffn, norm_ffn_gain, m_norm_ffn_gain, v_norm_ffn_gain))
    order = ["meta_tokens", "norm_mix_gain", "w_in", "conv_q", "conv_k", "conv_v", "dn_a_log", "dn_dt_bias",
             "dn_out_norm_gain", "sb_q_norm_gain", "sb_k_norm_gain", "w_branch_dn", "w_branch_sb", "w_out",
             "norm_ffn_gain", "w_ffn_in", "w_ffn_out"]
    grads, deltas, new_m, new_v = [], [], [], []
    for name in order:
        if name in ("w_in", "w_ffn_in"):
            slabs, w, m, v = big[name]
            res = _adamw_slabs(slabs, *(features_major(a).reshape(-1, LANES) for a in (w, m, v)), name="adamw_" + name)
            g, dl, mo, vo = (jnp.transpose(a.reshape(w.shape[2], 1, w.shape[1]), (1, 2, 0)) for a in res)
            like = w.shape
        elif name in big:
            slabs, w, m, v = big[name]
            g, dl, mo, vo = _adamw_slabs(slabs, w[0], m[0], v[0], name="adamw_" + name)
            like = w.shape
        else:
            g, w, m, v = tiny[name]
            like = dict(conv_q=conv_q, conv_k=conv_k, conv_v=conv_v).get(name, w).shape
            dl, mo, vo = _adamw_small(g, w, m, v, name="adamw_" + name)
        for lst, a in ((grads, g), (deltas, dl), (new_m, mo), (new_v, vo)):
            lst.append(a.reshape(like))
    return (loss.reshape(()), r["grad_x"][None], *grads, *deltas, *new_m, *new_v)
```

```python
import functools
import math

import jax
import jax.numpy as jnp
from jax import lax
from jax.experimental import pallas as pl
from jax.experimental.pallas import tpu as pltpu

F32 = jnp.float32
BF16 = jnp.bfloat16

N_DEV = 8
CHUNK = 64
CHUNK_SHIFT = 6
GDN_ROWS = 2 * CHUNK
SB_BLOCK = 128
SB_HEADS_PER_STEP = 2
LANES = 128
RMS_EPS = 1e-6
L2_EPS = 1e-6
ADAM_LR = 0.001
ADAM_B1 = 0.9
ADAM_B2 = 0.999
ADAM_EPS = 1e-08
ADAM_WD = 0.01
ADAM_STEP = 10
V7X_VMEM_LIMIT_BYTES = 56 * 1024 * 1024
MM_TN_OUT_BLOCK_BYTES = 6 * 1024 * 1024
ROWS_BIG = (1056, 512, 384, 256, 128)
ROWS_MID = (528, 384, 256, 128)
SCAN_HEADS = 4
PREP_HEADS = 2

MESH = pl.DeviceIdType.MESH


def _params(*sem):
    return pltpu.CompilerParams(dimension_semantics=sem or None, vmem_limit_bytes=V7X_VMEM_LIMIT_BYTES)


def _pick(n, cands):
    for c in cands:
        if n % c == 0:
            return c
    raise ValueError(f"no block size among {cands} divides {n}")


def _bf(x):
    return x.astype(BF16)


def _dot(a, b):
    return jnp.dot(a, b, preferred_element_type=F32)


def _dot_nt(a, b):
    return lax.dot_general(a, b, (((1,), (1,)), ((), ())), preferred_element_type=F32)


def _dot_tn(a, b):
    return lax.dot_general(a, b, (((0,), (0,)), ((), ())), preferred_element_type=F32)


def _split2(x):
    hi = _bf(x)
    return hi, _bf(x - hi.astype(F32))


def _split3(x):
    hi = _bf(x)
    r = x - hi.astype(F32)
    mid = _bf(r)
    return hi, mid, _bf(r - mid.astype(F32))


def _dot_hp(a, b, dot=_dot):
    ah, al = _split2(a)
    bh, bl = _split2(b)
    return dot(ah, bh) + dot(ah, bl) + dot(al, bh)


def _dot_exact_l(m, x, dot=_dot):
    h, mi, lo = _split3(x)
    return dot(m, h) + dot(m, mi) + dot(m, lo)


def _sigmoid(x):
    return 1.0 / (1.0 + jnp.exp(-x))


def _silu(x):
    return x * _sigmoid(x)


def _silu_grad(x):
    s = _sigmoid(x)
    return s * (1.0 + x * (1.0 - s))


def _softplus(x):
    return jnp.maximum(x, 0.0) + jnp.log(1.0 + jnp.exp(-jnp.abs(x)))


def _rms_fwd(h, gain):
    r = lax.rsqrt(jnp.mean(h * h, axis=-1, keepdims=True) + RMS_EPS)
    return h * r * gain


def _rms_bwd(h, gain, dy):
    r = lax.rsqrt(jnp.mean(h * h, axis=-1, keepdims=True) + RMS_EPS)
    dyg = dy * gain
    dh = r * dyg - h * (r * r * r) * jnp.mean(dyg * h, axis=-1, keepdims=True)
    return dh, dy * h * r


def _iota(shape, dim):
    return lax.broadcasted_iota(jnp.int32, shape, dim)


def _lane_pick(x, idx):
    return jnp.sum(jnp.where(_iota(x.shape, 1) == idx, x, 0.0), axis=1, keepdims=True)


def _mm_nt(a, b, *, out_dtype, name):
    m, k = a.shape
    n = b.shape[0]
    tm, tn = _pick(m, ROWS_BIG), _pick(n, (1024, 512, 256, 128))

    def body(a_ref, b_ref, o_ref):
        o_ref[...] = _dot_nt(a_ref[...], b_ref[...]).astype(out_dtype)

    return pl.pallas_call(
        body, grid=(m // tm, n // tn),
        in_specs=[pl.BlockSpec((tm, k), lambda i, j: (i, 0)), pl.BlockSpec((tn, k), lambda i, j: (j, 0))],
        out_specs=pl.BlockSpec((tm, tn), lambda i, j: (i, j)),
        out_shape=jax.ShapeDtypeStruct((m, n), out_dtype), name=name,
        compiler_params=_params("parallel", "parallel"))(a, b)


def _mm_tn(a, b, *, name):
    t, m = a.shape
    n = b.shape[1]
    tn = _pick(n, (2816, 2048, 1408, 1024, 512, 256, 128))
    tm = _pick(m, tuple(c for c in (1408, 1024, 512, 256, 128) if c * tn * 4 <= MM_TN_OUT_BLOCK_BYTES))
    tk = _pick(t, (1408, 1024, 512, 384, 256, 128))
    nk = t // tk

    def body(a_ref, b_ref, o_ref, acc_ref):
        k = pl.program_id(2)

        @pl.when(k == 0)
        def _():
            acc_ref[...] = jnp.zeros_like(acc_ref)

        acc_ref[...] += _dot_tn(a_ref[...], b_ref[...])

        @pl.when(k == nk - 1)
        def _():
            o_ref[...] = _bf(acc_ref[...])

    return pl.pallas_call(
        body, grid=(m // tm, n // tn, nk),
        in_specs=[pl.BlockSpec((tk, tm), lambda i, j, k: (k, i)), pl.BlockSpec((tk, tn), lambda i, j, k: (k, j))],
        out_specs=pl.BlockSpec((tm, tn), lambda i, j, k: (i, j)),
        out_shape=jax.ShapeDtypeStruct((m, n), BF16), scratch_shapes=[pltpu.VMEM((tm, tn), F32)], name=name,
        compiler_params=_params("parallel", "parallel", "arbitrary"))(a, b)


def _mm_norm(h, gain, wt, *, name):
    m, k = h.shape
    n = wt.shape[0]
    tm, tn = _pick(m, ROWS_BIG), _pick(n, (1024, 512, 256, 128))

    def body(h_ref, g_ref, w_ref, o_ref, n_ref):
        @pl.when(pl.program_id(1) == 0)
        def _():
            n_ref[...] = _bf(_rms_fwd(h_ref[...], g_ref[...]))

        o_ref[...] = _dot_nt(n_ref[...], w_ref[...])

    return pl.pallas_call(
        body, grid=(m // tm, n // tn),
        in_specs=[pl.BlockSpec((tm, k), lambda i, j: (i, 0)), pl.BlockSpec((1, k), lambda i, j: (0, 0)),
                  pl.BlockSpec((tn, k), lambda i, j: (j, 0))],
        out_specs=[pl.BlockSpec((tm, tn), lambda i, j: (i, j)), pl.BlockSpec((tm, k), lambda i, j: (i, 0))],
        out_shape=[jax.ShapeDtypeStruct((m, n), F32), jax.ShapeDtypeStruct((m, k), BF16)], name=name,
        compiler_params=_params("parallel", "arbitrary"))(h, gain, wt)


def _mm_norm_swiglu(h, gain, wgt, wut, *, name):
    m, k = h.shape
    n = wgt.shape[0]
    tm, tn = _pick(m, ROWS_MID), _pick(n, (1408, 1024, 512, 256, 128))

    def body(h_ref, g_ref, wg_ref, wu_ref, gate_ref, up_ref, act_ref, n_ref):
        @pl.when(pl.program_id(1) == 0)
        def _():
            n_ref[...] = _bf(_rms_fwd(h_ref[...], g_ref[...]))

        gate = _dot_nt(n_ref[...], wg_ref[...])
        up = _dot_nt(n_ref[...], wu_ref[...])
        gate_ref[...] = gate
        up_ref[...] = up
        act_ref[...] = _bf(_silu(gate) * up)

    wspec = pl.BlockSpec((tn, k), lambda i, j: (j, 0))
    ospec = pl.BlockSpec((tm, tn), lambda i, j: (i, j))
    return pl.pallas_call(
        body, grid=(m // tm, n // tn),
        in_specs=[pl.BlockSpec((tm, k), lambda i, j: (i, 0)), pl.BlockSpec((1, k), lambda i, j: (0, 0)), wspec, wspec],
        out_specs=[ospec, ospec, ospec, pl.BlockSpec((tm, k), lambda i, j: (i, 0))],
        out_shape=[jax.ShapeDtypeStruct((m, n), F32), jax.ShapeDtypeStruct((m, n), F32),
                   jax.ShapeDtypeStruct((m, n), BF16), jax.ShapeDtypeStruct((m, k), BF16)], name=name,
        compiler_params=_params("parallel", "arbitrary"))(h, gain, wgt, wut)


def _mm_res(res, a, b, *, name):
    m, k = a.shape
    n = b.shape[1]
    tm, tn = _pick(m, ROWS_BIG), _pick(n, (1024, 512, 256, 128))

    def body(r_ref, a_ref, b_ref, o_ref):
        o_ref[...] = r_ref[...] + _dot(a_ref[...], b_ref[...])

    return pl.pallas_call(
        body, grid=(m // tm, n // tn),
        in_specs=[pl.BlockSpec((tm, tn), lambda i, j: (i, j)), pl.BlockSpec((tm, k), lambda i, j: (i, 0)),
                  pl.BlockSpec((k, tn), lambda i, j: (0, j))],
        out_specs=pl.BlockSpec((tm, tn), lambda i, j: (i, j)),
        out_shape=jax.ShapeDtypeStruct((m, n), F32), name=name,
        compiler_params=_params("parallel", "parallel"))(res, a, b)


def _mm_res_loss(res, a, b, target, *, row0, nrows, name):
    m, k = a.shape
    n = b.shape[1]
    tm = _pick(m, ROWS_MID)

    def body(r_ref, a_ref, b_ref, t_ref, dy_ref, dyb_ref, ls_ref):
        i = pl.program_id(0)

        @pl.when(i == 0)
        def _():
            ls_ref[...] = jnp.zeros_like(ls_ref)

        y = r_ref[...] + _dot(a_ref[...], b_ref[...])
        row = i * tm + _iota((tm, n), 0)
        e = jnp.where((row >= row0) & (row < row0 + nrows), y - t_ref[...], 0.0)
        dy = e / n
        dy_ref[...] = dy
        dyb_ref[...] = _bf(dy)
        ls_ref[...] += jnp.sum(e * e, axis=0, keepdims=True)

    rspec = pl.BlockSpec((tm, n), lambda i: (i, 0))
    return pl.pallas_call(
        body, grid=(m // tm,),
        in_specs=[rspec, pl.BlockSpec((tm, k), lambda i: (i, 0)), pl.BlockSpec((k, n), lambda i: (0, 0)), rspec],
        out_specs=[rspec, rspec, pl.BlockSpec((1, n), lambda i: (0, 0))],
        out_shape=[jax.ShapeDtypeStruct((m, n), F32), jax.ShapeDtypeStruct((m, n), BF16),
                   jax.ShapeDtypeStruct((1, n), F32)], name=name,
        compiler_params=_params("arbitrary"))(res, a, b, target)


def _merge_fwd(o_dn, o_sb, wbd, wbs, proj, *, col_gd, col_gs, name):
    m, kd = o_dn.shape
    ks = o_sb.shape[1]
    n = wbd.shape[1]
    tm = _pick(m, ROWS_BIG)
    tn = _pick(math.gcd(n, math.gcd(col_gd, col_gs)), (512, 256, 128))

    def body(od_ref, os_ref, wd_ref, ws_ref, gd_ref, gs_ref, mg_ref, bd_ref, bs_ref):
        bd = _dot(od_ref[...], wd_ref[...])
        bs = _dot(os_ref[...], ws_ref[...])
        bd_ref[...] = bd
        bs_ref[...] = bs
        mg_ref[...] = _bf(_sigmoid(gd_ref[...]) * bd + _sigmoid(gs_ref[...]) * bs)

    ospec = pl.BlockSpec((tm, tn), lambda i, j: (i, j))
    return pl.pallas_call(
        body, grid=(m // tm, n // tn),
        in_specs=[pl.BlockSpec((tm, kd), lambda i, j: (i, 0)), pl.BlockSpec((tm, ks), lambda i, j: (i, 0)),
                  pl.BlockSpec((kd, tn), lambda i, j: (0, j)), pl.BlockSpec((ks, tn), lambda i, j: (0, j)),
                  pl.BlockSpec((tm, tn), lambda i, j: (i, col_gd // tn + j)),
                  pl.BlockSpec((tm, tn), lambda i, j: (i, col_gs // tn + j))],
        out_specs=[ospec, ospec, ospec],
        out_shape=[jax.ShapeDtypeStruct((m, n), BF16), jax.ShapeDtypeStruct((m, n), F32),
                   jax.ShapeDtypeStruct((m, n), F32)], name=name,
        compiler_params=_params("parallel", "parallel"))(o_dn, o_sb, wbd, wbs, proj, proj)


def _merge_bwd(dh, w_out, proj, br_dn, br_sb, *, col_gd, col_gs, name):
    m, k = dh.shape
    n = w_out.shape[0]
    tm = _pick(m, ROWS_BIG)
    tn = _pick(math.gcd(n, math.gcd(col_gd, col_gs)), (512, 256, 128))

    def body(dh_ref, w_ref, gd_ref, gs_ref, bd_ref, bs_ref, dbd_ref, dbs_ref, dgd_ref, dgs_ref):
        dm = _dot_nt(dh_ref[...], w_ref[...])
        sd = _sigmoid(gd_ref[...])
        ss = _sigmoid(gs_ref[...])
        dbd_ref[...] = _bf(dm * sd)
        dbs_ref[...] = _bf(dm * ss)
        dgd_ref[...] = _bf(dm * bd_ref[...] * sd * (1.0 - sd))
        dgs_ref[...] = _bf(dm * bs_ref[...] * ss * (1.0 - ss))

    ospec = pl.BlockSpec((tm, tn), lambda i, j: (i, j))
    return pl.pallas_call(
        body, grid=(m // tm, n // tn),
        in_specs=[pl.BlockSpec((tm, k), lambda i, j: (i, 0)), pl.BlockSpec((tn, k), lambda i, j: (j, 0)),
                  pl.BlockSpec((tm, tn), lambda i, j: (i, col_gd // tn + j)),
                  pl.BlockSpec((tm, tn), lambda i, j: (i, col_gs // tn + j)), ospec, ospec],
        out_specs=[ospec] * 4,
        out_shape=[jax.ShapeDtypeStruct((m, n), BF16)] * 4, name=name,
        compiler_params=_params("parallel", "parallel"))(dh, w_out, proj, proj, br_dn, br_sb)


def _swiglu_bwd(dy, wfo, gate, up, *, name):
    m, k = dy.shape
    n = wfo.shape[0]
    tm, tn = _pick(m, ROWS_MID), _pick(n, (1408, 1024, 512, 256, 128))

    def body(dy_ref, w_ref, g_ref, u_ref, dg_ref, du_ref):
        da = _dot_nt(dy_ref[...], w_ref[...])
        g = g_ref[...]
        dg_ref[...] = _bf(da * u_ref[...] * _silu_grad(g))
        du_ref[...] = _bf(da * _silu(g))

    ospec = pl.BlockSpec((tm, tn), lambda i, j: (i, j))
    return pl.pallas_call(
        body, grid=(m // tm, n // tn),
        in_specs=[pl.BlockSpec((tm, k), lambda i, j: (i, 0)), pl.BlockSpec((tn, k), lambda i, j: (j, 0)), ospec, ospec],
        out_specs=[ospec, ospec], out_shape=[jax.ShapeDtypeStruct((m, n), BF16)] * 2, name=name,
        compiler_params=_params("parallel", "parallel"))(dy, wfo, gate, up)


def _mm_rmsbwd(pairs, extra, h, gain, dres, *, name, scatter=None):
    m, k = pairs[0][0].shape
    n = h.shape[1]
    tm = _pick(m, ROWS_MID)
    tk = _pick(k, (1408, 1024, 512, 256, 128))
    nk = k // tk
    np_ = len(pairs)

    def body(*refs):
        ab = refs[:2 * np_]
        pos = 2 * np_
        ex = refs[pos:pos + 2] if extra is not None else ()
        pos += len(ex)
        h_ref, g_ref, r_ref, dh_ref, dhb_ref, dg_ref, acc_ref = refs[pos:]
        i, kk = pl.program_id(0), pl.program_id(1)

        @pl.when((i == 0) & (kk == 0))
        def _():
            dg_ref[...] = jnp.zeros_like(dg_ref)

        part = _dot(ab[0][...], ab[1][...])
        for p in range(1, np_):
            part += _dot(ab[2 * p][...], ab[2 * p + 1][...])

        @pl.when(kk == 0)
        def _():
            first = part
            if ex:
                first = first + _dot(ex[0][...], ex[1][...])
            acc_ref[...] = first

        @pl.when(kk > 0)
        def _():
            acc_ref[...] += part

        @pl.when(kk == nk - 1)
        def _():
            dh, dgr = _rms_bwd(h_ref[...], g_ref[...], acc_ref[...])
            dh = dh + r_ref[...]
            dh_ref[...] = dh
            dhb_ref[...] = _bf(dh)
            dg_ref[...] += jnp.sum(dgr, axis=0, keepdims=True)

    in_specs, args = [], []
    for a, b in pairs:
        in_specs += [pl.BlockSpec((tm, tk), lambda i, kk: (i, kk)), pl.BlockSpec((tk, n), lambda i, kk: (kk, 0))]
        args += [a, b]
    if extra is not None:
        k2 = extra[0].shape[1]
        in_specs += [pl.BlockSpec((tm, k2), lambda i, kk: (i, 0)), pl.BlockSpec((k2, n), lambda i, kk: (0, 0))]
        args += list(extra)
    rspec = pl.BlockSpec((tm, n), lambda i, kk: (i, 0))
    in_specs += [rspec, pl.BlockSpec((1, n), lambda i, kk: (0, 0)), rspec]
    call = dict(grid=(m // tm, nk), in_specs=in_specs,
                out_specs=[rspec, rspec, pl.BlockSpec((1, n), lambda i, kk: (0, 0))],
                out_shape=[jax.ShapeDtypeStruct((m, n), F32), jax.ShapeDtypeStruct((m, n), BF16),
                           jax.ShapeDtypeStruct((1, n), F32)],
                scratch_shapes=[pltpu.VMEM((tm, n), F32)], name=name)
    if scatter is None:
        return pl.pallas_call(body, compiler_params=_params("arbitrary", "arbitrary"), **call)(*args, h, gain, dres)
    res = _call_with_exchange(body, args=(*args, h, gain, dres), srcs=scatter, scatter=[True] * len(scatter), **call)
    return [*res[:3], list(res[3:])]


HALO = 8


def _stage(buf_ref, before, cur):
    buf_ref[0:HALO, :] = before
    buf_ref[HALO:HALO + cur.shape[0], :] = cur


def _stage_after(buf_ref, cur, after):
    r = cur.shape[0]
    buf_ref[0:r, :] = cur
    buf_ref[r:r + HALO, :] = after


def _conv_taps(buf_ref, rows, w_ref):
    nk = w_ref.shape[0]
    out = buf_ref[HALO:HALO + rows, :] * w_ref[nk - 1:nk, :]
    for s in range(1, nk):
        out += buf_ref[HALO - s:HALO - s + rows, :] * w_ref[nk - 1 - s:nk - s, :]
    return out


def _gdn_pre(proj, pab, cq, ck, cv, a_log, dt_bias, *, heads, dk, dv, col_q, col_k, col_v, row_lo, row_hi, name):
    t = proj.shape[0]
    tm = _pick(t, (384, 256, 128))
    nb = t // tm

    def body(pq_ref, pqp_ref, pk_ref, pkp_ref, pv_ref, pvp_ref, ab_ref, cq_ref, ck_ref, cv_ref, al_ref, dt_ref,
             qn_ref, kn_ref, v_ref, g_ref, b_ref, bq_ref, bk_ref, bv_ref):
        h, i = pl.program_id(0), pl.program_id(1)
        first = i == 0
        row = i * tm + _iota((tm, 1), 0)
        valid = (row >= row_lo) & (row < row_hi)
        _stage(bq_ref, jnp.where(first, 0.0, pqp_ref[...]), pq_ref[...])
        _stage(bk_ref, jnp.where(first, 0.0, pkp_ref[...]), pk_ref[...])
        _stage(bv_ref, jnp.where(first, 0.0, pvp_ref[...]), pv_ref[...])
        q1 = _silu(_conv_taps(bq_ref, tm, cq_ref))
        k1 = _silu(_conv_taps(bk_ref, tm, ck_ref))
        v1 = _silu(_conv_taps(bv_ref, tm, cv_ref))
        qn_ref[...] = jnp.where(valid, q1 * lax.rsqrt(jnp.sum(q1 * q1, axis=-1, keepdims=True) + L2_EPS), 0.0)
        kn_ref[...] = jnp.where(valid, k1 * lax.rsqrt(jnp.sum(k1 * k1, axis=-1, keepdims=True) + L2_EPS), 0.0)
        v_ref[...] = jnp.where(valid, v1, 0.0)
        ab = ab_ref[...]
        da = _lane_pick(ab, h)
        db = _lane_pick(ab, heads + h)
        a = _lane_pick(al_ref[...], h)
        dtb = _lane_pick(dt_ref[...], h)
        g_ref[...] = jnp.where(valid, -jnp.exp(a) * _softplus(da + dtb), 0.0)
        b_ref[...] = jnp.where(valid, _sigmoid(db), 0.0)

    def cur(width, col):
        return pl.BlockSpec((tm, width), lambda h, i: (i, col // width + h))

    def prev(width, col):
        return pl.BlockSpec((8, width), lambda h, i: (jnp.maximum(i * (tm // 8) - 1, 0), col // width + h))

    def out(width):
        return pl.BlockSpec((None, tm, width), lambda h, i: (h, i, 0))

    small = pl.BlockSpec((1, LANES), lambda h, i: (0, 0))
    return pl.pallas_call(
        body, grid=(heads, nb),
        in_specs=[cur(dk, col_q), prev(dk, col_q), cur(dk, col_k), prev(dk, col_k), cur(dv, col_v), prev(dv, col_v),
                  pl.BlockSpec((tm, LANES), lambda h, i: (i, 0)),
                  pl.BlockSpec((cq.shape[0], dk), lambda h, i: (0, h)), pl.BlockSpec((ck.shape[0], dk), lambda h, i: (0, h)),
                  pl.BlockSpec((cv.shape[0], dv), lambda h, i: (0, h)), small, small],
        out_specs=[out(dk), out(dk), out(dv), out(1), out(1)],
        out_shape=[jax.ShapeDtypeStruct((heads, t, dk), F32), jax.ShapeDtypeStruct((heads, t, dk), F32),
                   jax.ShapeDtypeStruct((heads, t, dv), F32), jax.ShapeDtypeStruct((heads, t, 1), F32),
                   jax.ShapeDtypeStruct((heads, t, 1), F32)],
        scratch_shapes=[pltpu.VMEM((HALO + tm, dk), F32), pltpu.VMEM((HALO + tm, dk), F32),
                        pltpu.VMEM((HALO + tm, dv), F32)], name=name,
        compiler_params=_params("parallel", "parallel"))(proj, proj, proj, proj, proj, proj, pab, cq, ck, cv, a_log, dt_bias)


def _chunk_masks(rows=GDN_ROWS, row0=0):
    ri = row0 + _iota((rows, GDN_ROWS), 0)
    ci = _iota((rows, GDN_ROWS), 1)
    same = jnp.right_shift(ri, CHUNK_SHIFT) == jnp.right_shift(ci, CHUNK_SHIFT)
    return same, same & (ri >= ci), same & (ri > ci), ri == ci


def _col_to_row(col, eye):
    return jnp.sum(jnp.where(eye, col, 0.0), axis=0, keepdims=True)


def _row_to_col(row, eye):
    return jnp.sum(jnp.where(eye, row, 0.0), axis=1, keepdims=True)


def _chunk_common(blocks, dk_scale):
    same, incl, strict, eye = _chunk_masks()
    tri = jnp.where(incl, 1.0, 0.0).astype(BF16)
    tot = jnp.where(same, 1.0, 0.0).astype(BF16)
    gbs = [jnp.broadcast_to(g, (GDN_ROWS, LANES)) for _, _, g, _ in blocks]
    gams = [jnp.max(_dot_exact_l(tri, gb), axis=1, keepdims=True) for gb in gbs]
    lasts = [jnp.max(_dot_exact_l(tot, gb), axis=1, keepdims=True) for gb in gbs]
    kbs = [kn * beta for _, kn, _, beta in blocks]
    qts = [qn * dk_scale for qn, _, _, _ in blocks]
    boths = [_dot_nt(_bf(jnp.concatenate([kb, qt], axis=0)), _bf(blk[1]))
             for kb, qt, blk in zip(kbs, qts, blocks)]
    out = []
    for gam, last, kb, qt, both in zip(gams, lasts, kbs, qts, boths):
        diff = gam - _col_to_row(gam, eye)
        decay = jnp.where(incl, jnp.exp(jnp.where(incl, diff, 0.0)), 0.0)
        out.append(dict(incl=incl, strict=strict, eye=eye, decay=decay, eg=jnp.exp(gam), ek=jnp.exp(last - gam),
                        egl=jnp.exp(last), kb=kb, qt=qt, lmat=jnp.where(strict, both[:GDN_ROWS] * decay, 0.0),
                        pmat=jnp.where(incl, both[GDN_ROWS:] * decay, 0.0)))
    return out


def _gdn_prep(qn, kn, v, g, beta, *, name):
    heads, t, dk = qn.shape
    dv = v.shape[2]
    rows = _pick(t, (3 * GDN_ROWS, 2 * GDN_ROWS, GDN_ROWS))
    dk_scale = dk ** -0.5

    hp = PREP_HEADS

    def body(q_ref, k_ref, v_ref, g_ref, b_ref, u_ref, w_ref, p_ref, qd_ref, kd_ref, egl_ref, t_ref):
        rs = [(hh, pl.ds(b * GDN_ROWS, GDN_ROWS), slice(None)) for hh in range(hp) for b in range(rows // GDN_ROWS)]
        cs = _chunk_common([(q_ref[r], k_ref[r], g_ref[r], b_ref[r]) for r in rs], dk_scale)
        eye_f = jnp.where(cs[0]["eye"], 1.0, 0.0)
        tinvs = [eye_f - c["lmat"] for c in cs]
        ys = [_dot_hp(c["lmat"], c["lmat"]) for c in cs]
        for _ in range(CHUNK_SHIFT - 1):
            boths = [_dot_hp(y, jnp.concatenate([y, tinv], axis=1)) for y, tinv in zip(ys, tinvs)]
            ys = [both[:, :GDN_ROWS] for both in boths]
            tinvs = [tinv + both[:, GDN_ROWS:] for tinv, both in zip(tinvs, boths)]
        uws = [_dot_hp(tinv, jnp.concatenate([v_ref[r] * b_ref[r], c["kb"] * c["eg"]], axis=1))
               for r, c, tinv in zip(rs, cs, tinvs)]
        for r, c, tinv, uw in zip(rs, cs, tinvs, uws):
            u_ref[r] = uw[:, :dv]
            w_ref[r] = _bf(uw[:, dv:])
            p_ref[r] = _bf(c["pmat"])
            qd_ref[r] = _bf(c["qt"] * c["eg"])
            kd_ref[r] = _bf(k_ref[r] * c["ek"])
            egl_ref[r] = c["egl"]
            t_ref[r] = tinv

    def blk(width):
        return pl.BlockSpec((hp, rows, width), lambda h, i: (h, i, 0))

    def shp(width, dtype=F32):
        return jax.ShapeDtypeStruct((heads, t, width), dtype)

    return pl.pallas_call(
        body, grid=(heads // hp, t // rows), in_specs=[blk(dk), blk(dk), blk(dv), blk(1), blk(1)],
        out_specs=[blk(dv), blk(dk), blk(GDN_ROWS), blk(dk), blk(dk), blk(1), blk(GDN_ROWS)],
        out_shape=[shp(dv), shp(dk, BF16), shp(GDN_ROWS, BF16), shp(dk, BF16), shp(dk, BF16), shp(1), shp(GDN_ROWS)],
        name=name,
        compiler_params=_params("parallel", "parallel"))(qn, kn, v, g, beta)


def _gdn_scan(u, w, p, qd, kd, egl, proj, gain, *, col_z, name):
    heads, t, dv = u.shape
    dk = w.shape[2]
    nb = t // GDN_ROWS
    sub = GDN_ROWS // CHUNK
    hp = SCAN_HEADS

    def body(u_ref, w_ref, p_ref, qd_ref, kd_ref, egl_ref, z_ref, gn_ref, o_ref, og_ref, st_ref, s_ref):
        @pl.when(pl.program_id(1) == 0)
        def _():
            s_ref[...] = jnp.zeros_like(s_ref)

        hs = range(hp)
        vn_parts = [[jnp.zeros((CHUNK, dv), F32)] * sub for _ in hs]
        for c in range(sub):
            r = pl.ds(c * CHUNK, CHUNK)
            ss = [s_ref[hh] for hh in hs]
            sbs = [_bf(s) for s in ss]
            wss = [_dot(_bf(jnp.concatenate([w_ref[hh, r, :], qd_ref[hh, r, :]], axis=0)), sbs[hh])
                   for hh in hs]
            vns = [u_ref[hh, r, :] - wss[hh][:CHUNK] for hh in hs]
            for hh in hs:
                vn_parts[hh][c] = vns[hh]
            os_ = [wss[hh][CHUNK:] + _dot(_bf(p_ref[hh, r, :]), _bf(jnp.concatenate(vn_parts[hh], axis=0))) for hh in hs]
            new = [ss[hh] * egl_ref[hh, pl.ds(c * CHUNK, 1), :] + _dot_tn(_bf(kd_ref[hh, r, :]), _bf(vns[hh])) for hh in hs]
            for hh in hs:
                cols = pl.ds(hh * dv, dv)
                st_ref[hh, c] = ss[hh]
                s_ref[hh] = new[hh]
                o_ref[hh, r, :] = os_[hh]
                og_ref[r, cols] = _bf(_rms_fwd(os_[hh], gn_ref[...]) * _silu(z_ref[r, cols]))

    def blk(width):
        return pl.BlockSpec((hp, GDN_ROWS, width), lambda h, i: (h, i, 0))

    return pl.pallas_call(
        body, grid=(heads // hp, nb),
        in_specs=[blk(dv), blk(dk), blk(GDN_ROWS), blk(dk), blk(dk), blk(1),
                  pl.BlockSpec((GDN_ROWS, hp * dv), lambda h, i: (i, col_z // (hp * dv) + h)),
                  pl.BlockSpec((1, dv), lambda h, i: (0, 0))],
        out_specs=[blk(dv), pl.BlockSpec((GDN_ROWS, hp * dv), lambda h, i: (i, h)),
                   pl.BlockSpec((hp, sub, dk, dv), lambda h, i: (h, i, 0, 0))],
        out_shape=[jax.ShapeDtypeStruct((heads, t, dv), F32), jax.ShapeDtypeStruct((t, heads * dv), BF16),
                   jax.ShapeDtypeStruct((heads, t // CHUNK, dk, dv), F32)],
        scratch_shapes=[pltpu.VMEM((hp, dk, dv), F32)], name=name,
        compiler_params=_params("parallel", "arbitrary"))(u, w, p, qd, kd, egl, proj, gain)


def _gdn_post_bwd(o, proj, gain, dout, *, col_z, name):
    heads, t, dv = o.shape
    tm = _pick(t, (384, 256, 128))

    def body(o_ref, z_ref, gn_ref, d_ref, do_ref, dz_ref, dg_ref):
        @pl.when((pl.program_id(0) == 0) & (pl.program_id(1) == 0))
        def _():
            dg_ref[...] = jnp.zeros_like(dg_ref)

        o_, z, d = o_ref[...], z_ref[...], d_ref[...]
        y = _rms_fwd(o_, gn_ref[...])
        dz_ref[...] = _bf(d * y * _silu_grad(z))
        do, dgr = _rms_bwd(o_, gn_ref[...], d * _silu(z))
        do_ref[...] = do
        dg_ref[...] += jnp.sum(dgr, axis=0, keepdims=True)

    return pl.pallas_call(
        body, grid=(t // tm, heads),
        in_specs=[pl.BlockSpec((None, tm, dv), lambda i, h: (h, i, 0)),
                  pl.BlockSpec((tm, dv), lambda i, h: (i, col_z // dv + h)),
                  pl.BlockSpec((1, dv), lambda i, h: (0, 0)), pl.BlockSpec((tm, dv), lambda i, h: (i, h))],
        out_specs=[pl.BlockSpec((None, tm, dv), lambda i, h: (h, i, 0)), pl.BlockSpec((tm, dv), lambda i, h: (i, h)),
                   pl.BlockSpec((1, dv), lambda i, h: (0, 0))],
        out_shape=[jax.ShapeDtypeStruct((heads, t, dv), F32), jax.ShapeDtypeStruct((t, heads * dv), BF16),
                   jax.ShapeDtypeStruct((1, dv), F32)], name=name,
        compiler_params=_params("arbitrary", "arbitrary"))(o, proj, gain, dout)


def _gdn_bwd_scan(u, w, p, qd, kd, egl, st, do, *, name):
    heads, t, dv = u.shape
    dk = w.shape[2]
    nb = t // GDN_ROWS
    sub = GDN_ROWS // CHUNK
    hp = SCAN_HEADS

    def body(u_ref, w_ref, p_ref, qd_ref, kd_ref, egl_ref, st_ref, do_ref,
             du_ref, dw_ref, dp_ref, dqd_ref, dkd_ref, dgl_ref, ds_ref):
        @pl.when(pl.program_id(1) == 0)
        def _():
            ds_ref[...] = jnp.zeros_like(ds_ref)

        hs = range(hp)
        zeros = jnp.zeros((CHUNK, dv), BF16)
        for c in reversed(range(sub)):
            r = pl.ds(c * CHUNK, CHUNK)
            ss = [st_ref[hh, c] for hh in hs]
            sbs = [_bf(s) for s in ss]
            dss = [ds_ref[hh] for hh in hs]
            dsbs = [_bf(ds) for ds in dss]
            dobs = [_bf(do_ref[hh, r, :]) for hh in hs]
            wbs = [_bf(w_ref[hh, r, :]) for hh in hs]
            vns = [u_ref[hh, r, :] - _dot(wbs[hh], sbs[hh]) for hh in hs]
            dvns = [_dot_tn(_bf(p_ref[hh, r, :]), dobs[hh])[c * CHUNK:(c + 1) * CHUNK, :]
                    + _dot(_bf(kd_ref[hh, r, :]), dsbs[hh]) for hh in hs]
            dods = [jnp.concatenate([dobs[hh], _bf(dvns[hh])], axis=0) for hh in hs]
            boths = [_dot_nt(dods[hh], sbs[hh]) for hh in hs]
            dps = [_dot_nt(dobs[hh], jnp.concatenate([_bf(vns[hh]) if cc == c else zeros for cc in range(sub)], axis=0))
                   for hh in hs]
            dkds = [_dot_nt(_bf(vns[hh]), dsbs[hh]) for hh in hs]
            new = [dss[hh] * egl_ref[hh, pl.ds(c * CHUNK, 1), :]
                   + _dot_tn(jnp.concatenate([_bf(qd_ref[hh, r, :]), -wbs[hh]], axis=0), dods[hh])
                   for hh in hs]
            for hh in hs:
                du_ref[hh, r, :] = dvns[hh]
                dw_ref[hh, r, :] = -boths[hh][CHUNK:]
                dp_ref[hh, r, :] = jnp.where(_chunk_masks(CHUNK, c * CHUNK)[1], dps[hh], 0.0)
                dqd_ref[hh, r, :] = boths[hh][:CHUNK]
                dkd_ref[hh, r, :] = dkds[hh]
                dgl = jnp.sum(jnp.sum(dss[hh] * ss[hh], axis=1, keepdims=True), axis=0, keepdims=True)
                dgl_ref[hh, r, :] = jnp.where(_iota((CHUNK, 1), 0) == CHUNK - 1, dgl, 0.0)
                ds_ref[hh] = new[hh]

    def blk(width):
        return pl.BlockSpec((hp, GDN_ROWS, width), lambda h, i: (h, nb - 1 - i, 0))

    def shp(width):
        return jax.ShapeDtypeStruct((heads, t, width), F32)

    return pl.pallas_call(
        body, grid=(heads // hp, nb),
        in_specs=[blk(dv), blk(dk), blk(GDN_ROWS), blk(dk), blk(dk), blk(1),
                  pl.BlockSpec((hp, sub, dk, dv), lambda h, i: (h, nb - 1 - i, 0, 0)), blk(dv)],
        out_specs=[blk(dv), blk(dk), blk(GDN_ROWS), blk(dk), blk(dk), blk(1)],
        out_shape=[shp(dv), shp(dk), shp(GDN_ROWS), shp(dk), shp(dk), shp(1)],
        scratch_shapes=[pltpu.VMEM((hp, dk, dv), F32)], name=name,
        compiler_params=_params("parallel", "arbitrary"))(u, w, p, qd, kd, egl, st, do)


def _gdn_bwd_prep(qn, kn, v, g, beta, tinv, u, w, du, dw, dp, dqd, dkd, dgl, *, name):
    heads, t, dk = qn.shape
    dv = v.shape[2]
    rows = _pick(t, (3 * GDN_ROWS, 2 * GDN_ROWS, GDN_ROWS))
    dk_scale = dk ** -0.5
    hp = PREP_HEADS

    def rowsum(x):
        return jnp.sum(x, axis=1, keepdims=True)

    def body(q_ref, k_ref, v_ref, g_ref, b_ref, t_ref, u_ref, w_ref, du_ref, dw_ref, dp_ref, dqd_ref, dkd_ref, dgl_ref,
             dq_ref, dkk_ref, dvv_ref, dg_ref, db_ref):
        rs = [(hh, pl.ds(b * GDN_ROWS, GDN_ROWS), slice(None)) for hh in range(hp) for b in range(rows // GDN_ROWS)]
        cs = _chunk_common([(q_ref[r], k_ref[r], g_ref[r], b_ref[r]) for r in rs], dk_scale)
        dbvws = [_dot_hp(t_ref[r], jnp.concatenate([du_ref[r], dw_ref[r]], axis=1), _dot_tn)
                 for r in rs]
        das = [-_dot_nt(_bf(dbvw), jnp.concatenate([_bf(u_ref[r]), w_ref[r]], axis=1))
               for r, dbvw in zip(rs, dbvws)]
        dls = [jnp.where(c["strict"], da, 0.0) for c, da in zip(cs, das)]
        dmns = [_bf(jnp.concatenate([dl * c["decay"], dp_ref[r] * c["decay"]], axis=0))
                for r, c, dl in zip(rs, cs, dls)]
        boths = [_dot(dmn, _bf(k_ref[r])) for r, dmn in zip(rs, dmns)]
        dkns = [_dot_tn(dmn, _bf(jnp.concatenate([c["kb"], c["qt"]], axis=0)))
                for c, dmn in zip(cs, dmns)]
        for r, c, dbvw, dl, both, dkn in zip(rs, cs, dbvws, dls, boths, dkns):
            kn_, beta_, v_ = k_ref[r], b_ref[r], v_ref[r]
            eye = c["eye"]
            kb, qt, eg, ek = c["kb"], c["qt"], c["eg"], c["ek"]
            dbv, dbw = dbvw[:, :dv], dbvw[:, dv:]
            dp_ = dp_ref[r]
            dkb = both[:GDN_ROWS] + dbw * eg
            dqt = both[GDN_ROWS:]
            gmat = dl * c["lmat"] + dp_ * c["pmat"]
            dqd_, dkd_ = dqd_ref[r], dkd_ref[r]
            qd = qt * eg
            kd = kn_ * ek
            bw = kb * eg
            kdsum = rowsum(dkd_ * kd)
            dgam = rowsum(gmat) - _row_to_col(jnp.sum(gmat, axis=0, keepdims=True), eye)
            dgam += rowsum(dbw * bw) + rowsum(dqd_ * qd) - kdsum
            last = (_iota((GDN_ROWS, 1), 0) & (CHUNK - 1)) == CHUNK - 1
            same = _chunk_masks()[0]
            same_f = jnp.where(same, 1.0, 0.0).astype(BF16)
            chunk_tot = jnp.max(_dot_exact_l(same_f, jnp.broadcast_to(kdsum, (GDN_ROWS, LANES))), axis=1, keepdims=True)
            dgam += jnp.where(last, chunk_tot, 0.0) + dgl_ref[r] * c["egl"]
            dq_ref[r] = (dqt + dqd_ * eg) * dk_scale
            dkk_ref[r] = dkn + dkd_ * ek + dkb * beta_
            dvv_ref[r] = dbv * beta_
            db_ref[r] = rowsum(dbv * v_) + rowsum(dkb * kn_)
            upper = jnp.where(same & (_iota((GDN_ROWS, GDN_ROWS), 0) <= _iota((GDN_ROWS, GDN_ROWS), 1)), 1.0, 0.0)
            dgb = _dot_exact_l(upper.astype(BF16), jnp.broadcast_to(dgam, (GDN_ROWS, LANES)))
            dg_ref[r] = _lane_pick(dgb, 0)

    def blk(width):
        return pl.BlockSpec((hp, rows, width), lambda h, i: (h, i, 0))

    def shp(width):
        return jax.ShapeDtypeStruct((heads, t, width), F32)

    return pl.pallas_call(
        body, grid=(heads // hp, t // rows),
        in_specs=[blk(dk), blk(dk), blk(dv), blk(1), blk(1), blk(GDN_ROWS), blk(dv), blk(dk),
                  blk(dv), blk(dk), blk(GDN_ROWS), blk(dk), blk(dk), blk(1)],
        out_specs=[blk(dk), blk(dk), blk(dv), blk(1), blk(1)],
        out_shape=[shp(dk), shp(dk), shp(dv), shp(1), shp(1)], name=name,
        compiler_params=_params("parallel", "parallel"))(qn, kn, v, g, beta, tinv, u, w, du, dw, dp, dqd, dkd, dgl)


def _gdn_pre_bwd_a(proj, pab, cq, ck, cv, a_log, dt_bias, dqn, dkn, dvv, dg, dbeta, *,
                   heads, dk, dv, col_q, col_k, col_v, row_lo, row_hi, name):
    t = proj.shape[0]
    tm = _pick(t, (384, 256, 128))
    nb = t // tm

    def body(pq_ref, pqp_ref, pk_ref, pkp_ref, pv_ref, pvp_ref, ab_ref, cq_ref, ck_ref, cv_ref, al_ref, dt_ref,
             dqn_ref, dkn_ref, dvv_ref, dg_ref, db_ref, dcq_ref, dck_ref, dcv_ref, dab_ref, dal_ref, ddt_ref,
             bq_ref, bk_ref, bv_ref):
        i, h = pl.program_id(0), pl.program_id(1)
        first = i == 0

        @pl.when((i == 0) & (h == 0))
        def _():
            dal_ref[...] = jnp.zeros_like(dal_ref)
            ddt_ref[...] = jnp.zeros_like(ddt_ref)

        @pl.when(h == 0)
        def _():
            dab_ref[...] = jnp.zeros_like(dab_ref)

        row = i * tm + _iota((tm, 1), 0)
        valid = (row >= row_lo) & (row < row_hi)

        def l2_bwd(c1, dn):
            x1 = _silu(c1)
            r = lax.rsqrt(jnp.sum(x1 * x1, axis=-1, keepdims=True) + L2_EPS)
            dn = jnp.where(valid, dn, 0.0)
            d1 = r * dn - x1 * (r * r * r) * jnp.sum(dn * x1, axis=-1, keepdims=True)
            return d1 * _silu_grad(c1)

        _stage(bq_ref, jnp.where(first, 0.0, pqp_ref[...]), pq_ref[...])
        _stage(bk_ref, jnp.where(first, 0.0, pkp_ref[...]), pk_ref[...])
        _stage(bv_ref, jnp.where(first, 0.0, pvp_ref[...]), pv_ref[...])
        dcq_ref[...] = l2_bwd(_conv_taps(bq_ref, tm, cq_ref), dqn_ref[...])
        dck_ref[...] = l2_bwd(_conv_taps(bk_ref, tm, ck_ref), dkn_ref[...])
        cv1 = _conv_taps(bv_ref, tm, cv_ref)
        dcv_ref[...] = jnp.where(valid, dvv_ref[...], 0.0) * _silu_grad(cv1)
        ab = ab_ref[...]
        da = _lane_pick(ab, h)
        db = _lane_pick(ab, heads + h)
        a = _lane_pick(al_ref[...], h)
        dtb = _lane_pick(dt_ref[...], h)
        dgv = jnp.where(valid, dg_ref[...], 0.0)
        ea = jnp.exp(a)
        g = -ea * _softplus(da + dtb)
        dda = dgv * (-ea) * _sigmoid(da + dtb)
        beta = _sigmoid(db)
        ddb = jnp.where(valid, db_ref[...], 0.0) * beta * (1.0 - beta)
        lane = _iota((tm, LANES), 1)
        dab_ref[...] += jnp.where(lane == h, dda, 0.0) + jnp.where(lane == heads + h, ddb, 0.0)
        lane1 = _iota((1, LANES), 1)
        dal_ref[...] += jnp.where(lane1 == h, jnp.sum(dgv * g, axis=0, keepdims=True), 0.0)
        ddt_ref[...] += jnp.where(lane1 == h, jnp.sum(dda, axis=0, keepdims=True), 0.0)

    def cur(width, col):
        return pl.BlockSpec((tm, width), lambda i, h: (i, col // width + h))

    def prev(width, col):
        return pl.BlockSpec((8, width), lambda i, h: (jnp.maximum(i * (tm // 8) - 1, 0), col // width + h))

    def hd(width):
        return pl.BlockSpec((None, tm, width), lambda i, h: (h, i, 0))

    small = pl.BlockSpec((1, LANES), lambda i, h: (0, 0))
    return pl.pallas_call(
        body, grid=(nb, heads),
        in_specs=[cur(dk, col_q), prev(dk, col_q), cur(dk, col_k), prev(dk, col_k), cur(dv, col_v), prev(dv, col_v),
                  pl.BlockSpec((tm, LANES), lambda i, h: (i, 0)),
                  pl.BlockSpec((cq.shape[0], dk), lambda i, h: (0, h)), pl.BlockSpec((ck.shape[0], dk), lambda i, h: (0, h)),
                  pl.BlockSpec((cv.shape[0], dv), lambda i, h: (0, h)), small, small,
                  hd(dk), hd(dk), hd(dv), hd(1), hd(1)],
        out_specs=[hd(dk), hd(dk), hd(dv), pl.BlockSpec((tm, LANES), lambda i, h: (i, 0)), small, small],
        out_shape=[jax.ShapeDtypeStruct((heads, t, dk), F32), jax.ShapeDtypeStruct((heads, t, dk), F32),
                   jax.ShapeDtypeStruct((heads, t, dv), F32), jax.ShapeDtypeStruct((t, LANES), F32),
                   jax.ShapeDtypeStruct((1, LANES), F32), jax.ShapeDtypeStruct((1, LANES), F32)],
        scratch_shapes=[pltpu.VMEM((HALO + tm, dk), F32), pltpu.VMEM((HALO + tm, dk), F32),
                        pltpu.VMEM((HALO + tm, dv), F32)], name=name,
        compiler_params=_params("arbitrary", "arbitrary"))(
            proj, proj, proj, proj, proj, proj, pab, cq, ck, cv, a_log, dt_bias, dqn, dkn, dvv, dg, dbeta)


def _conv_bwd(proj, dc, cw, *, heads, width, col, name):
    t = proj.shape[0]
    tm = _pick(t, (384, 256, 128))
    nb = t // tm
    nk = cw.shape[0]

    def body(p_ref, pp_ref, d_ref, dn_ref, w_ref, dp_ref, dw_ref, bx_ref, bd_ref):
        i = pl.program_id(1)
        first, last = i == 0, i == nb - 1

        @pl.when(first)
        def _():
            dw_ref[...] = jnp.zeros_like(dw_ref)

        d = d_ref[...]
        _stage(bx_ref, jnp.where(first, 0.0, pp_ref[...]), p_ref[...])
        _stage_after(bd_ref, d, jnp.where(last, 0.0, dn_ref[...]))
        dx = d * w_ref[nk - 1:nk, :]
        dw_ref[nk - 1:nk, :] += jnp.sum(d * p_ref[...], axis=0, keepdims=True)
        for s in range(1, nk):
            dx += bd_ref[s:s + tm, :] * w_ref[nk - 1 - s:nk - s, :]
            dw_ref[nk - 1 - s:nk - s, :] += jnp.sum(d * bx_ref[HALO - s:HALO - s + tm, :], axis=0, keepdims=True)
        dp_ref[...] = _bf(dx)

    return pl.pallas_call(
        body, grid=(heads, nb),
        in_specs=[pl.BlockSpec((tm, width), lambda h, i: (i, col // width + h)),
                  pl.BlockSpec((8, width), lambda h, i: (jnp.maximum(i * (tm // 8) - 1, 0), col // width + h)),
                  pl.BlockSpec((None, tm, width), lambda h, i: (h, i, 0)),
                  pl.BlockSpec((None, 8, width), lambda h, i: (h, jnp.minimum((i + 1) * (tm // 8), t // 8 - 1), 0)),
                  pl.BlockSpec((nk, width), lambda h, i: (0, h))],
        out_specs=[pl.BlockSpec((tm, width), lambda h, i: (i, h)), pl.BlockSpec((nk, width), lambda h, i: (0, h))],
        out_shape=[jax.ShapeDtypeStruct((t, heads * width), BF16), jax.ShapeDtypeStruct((nk, heads * width), F32)],
        scratch_shapes=[pltpu.VMEM((HALO + tm, width), F32), pltpu.VMEM((tm + HALO, width), F32)],
        name=name, compiler_params=_params("parallel", "arbitrary"))(proj, proj, dc, dc, cw)


def _sb_pre(proj, gq, gk, *, heads, dh, col_q, col_k, col_v, name):
    t = proj.shape[0]
    tm = _pick(t, (384, 256, 128))

    def body(q_ref, k_ref, v_ref, gq_ref, gk_ref, qo_ref, ko_ref, vo_ref):
        qo_ref[...] = _bf(_rms_fwd(q_ref[...], gq_ref[...]))
        ko_ref[...] = _bf(_rms_fwd(k_ref[...], gk_ref[...]))
        vo_ref[...] = _bf(v_ref[...])

    def cur(col):
        return pl.BlockSpec((tm, dh), lambda i, h: (i, col // dh + h))

    gspec = pl.BlockSpec((1, dh), lambda i, h: (0, 0))
    ospec = pl.BlockSpec((tm, dh), lambda i, h: (i, h))
    return pl.pallas_call(
        body, grid=(t // tm, heads), in_specs=[cur(col_q), cur(col_k), cur(col_v), gspec, gspec],
        out_specs=[ospec] * 3, out_shape=[jax.ShapeDtypeStruct((t, heads * dh), BF16)] * 3, name=name,
        compiler_params=_params("parallel", "parallel"))(proj, proj, proj, gq, gk)


def _sb_tile(z, i, j, blk, key_lo, masked):
    ls = jnp.minimum(z, 0.0) - jnp.log(1.0 + jnp.exp(-jnp.abs(z)))
    if not masked:
        return None, ls, ls - z
    qpos = i * blk + _iota((blk, blk), 0)
    kpos = j * blk + _iota((blk, blk), 1)
    vis = (kpos < qpos) & (kpos >= key_lo)
    return vis, ls, jnp.where(vis, ls - z, 0.0)


def _where_vis(vis, x):
    return x if vis is None else jnp.where(vis, x, 0.0)


def _sb_sweep(i, step, init, descending):
    first, last = (i, 0) if descending else (0, i)
    carry = step(first, init, True)
    carry = lax.fori_loop(1, i, lambda n, c: step(i - n if descending else n, c, False), carry)
    return lax.cond(i > 0, lambda c: step(last, c, True), lambda c: c, carry)


def _dot2_r(x, m):
    hi, lo = _split2(x)
    return _dot(hi, m) + _dot(lo, m)


def _running_sums(x, tri, reverse, exact=True):
    groups = [x[:, s:s + LANES] for s in range(0, x.shape[1], LANES)]
    inside = [_dot2_r(g, tri) if exact else _dot(_bf(g), tri) for g in groups]
    sums = [jnp.sum(g, axis=1, keepdims=True) for g in groups]
    order = list(reversed(range(len(groups)))) if reverse else list(range(len(groups)))
    out, acc = [None] * len(groups), None
    for gi in order:
        out[gi] = inside[gi] if acc is None else inside[gi] + acc
        acc = sums[gi] if acc is None else acc + sums[gi]
    return jnp.concatenate(out, axis=1), acc


def _sb_fwd(qs, ks, vs, *, heads, dh, key_lo, name, gather=None):
    t = qs.shape[0]
    blk = _pick(t, (3 * SB_BLOCK, 2 * SB_BLOCK, SB_BLOCK))
    assert key_lo <= blk
    nq = t // blk
    assert nq <= LANES
    scale = dh ** -0.5
    hp = SB_HEADS_PER_STEP

    def body(q_ref, k_ref, v_ref, o_ref, c_ref):
        i = pl.program_id(1)
        later = jnp.where(_iota((LANES, LANES), 0) > _iota((LANES, LANES), 1), 1.0, 0.0).astype(BF16)
        lane = _iota((blk, LANES), 1)
        c_ref[...] = jnp.zeros_like(c_ref)

        def step(j, carry, masked):
            rows = pl.ds(pl.multiple_of(j * blk, blk), blk)
            hs = range(hp)
            cols = [pl.ds(hh * dh, dh) for hh in hs]
            zs = [_dot_nt(q_ref[:, cols[hh]], k_ref[rows, cols[hh]]) * scale for hh in hs]
            tiles = [_sb_tile(z, i, j, blk, key_lo, masked) for z in zs]
            sufs = [_running_sums(lk, later, reverse=True) for _, _, lk in tiles]
            wgts = [_where_vis(vis, jnp.exp(ls + suf + carry[2 * hh + 1]))
                    for hh, ((vis, ls, _), (suf, _)) in enumerate(zip(tiles, sufs))]
            accs = [carry[2 * hh] + _dot(_bf(wgts[hh]), v_ref[rows, cols[hh]]) for hh in hs]
            out = []
            for hh in hs:
                c_ref[hh] = jnp.where(lane == j, carry[2 * hh + 1], c_ref[hh])
                out += [accs[hh], carry[2 * hh + 1] + sufs[hh][1]]
            return tuple(out)

        res = _sb_sweep(i, step, (jnp.zeros((blk, dh), F32), jnp.zeros((blk, 1), F32)) * hp, descending=True)
        for hh in range(hp):
            o_ref[:, pl.ds(hh * dh, dh)] = _bf(res[2 * hh])

    full = pl.BlockSpec((t, hp * dh), lambda h, i: (0, h))
    call = dict(grid=(heads // hp, nq),
                in_specs=[pl.BlockSpec((blk, hp * dh), lambda h, i: (i, h)), full, full],
                out_specs=[pl.BlockSpec((blk, hp * dh), lambda h, i: (i, h)),
                           pl.BlockSpec((hp, blk, LANES), lambda h, i: (h, i, 0))],
                out_shape=[jax.ShapeDtypeStruct((t, heads * dh), BF16), jax.ShapeDtypeStruct((heads, t, LANES), F32)],
                name=name)
    if gather is None:
        return pl.pallas_call(body, compiler_params=_params("parallel", "parallel"), **call)(qs, ks, vs)
    res = _call_with_exchange(body, scratch_shapes=[], args=(qs, ks, vs), srcs=gather, scatter=[False] * len(gather),
                              **call)
    return [*res[:2], list(res[2:])]


def _sb_bwd(qs, ks, vs, do, carry, *, heads, dh, key_lo, name, scatter=None):
    t = qs.shape[0]
    blk = _pick(t, (3 * SB_BLOCK, 2 * SB_BLOCK, SB_BLOCK))
    assert key_lo <= blk
    nq = t // blk
    scale = dh ** -0.5
    hp = SB_HEADS_PER_STEP

    def body(q_ref, k_ref, v_ref, do_ref, c_ref, dq_ref, dk_ref, dv_ref):
        i = pl.program_id(1)

        @pl.when(i == 0)
        def _():
            dk_ref[...] = jnp.zeros_like(dk_ref)
            dv_ref[...] = jnp.zeros_like(dv_ref)

        r0 = _iota((LANES, LANES), 0)
        r1 = _iota((LANES, LANES), 1)
        later = jnp.where(r0 > r1, 1.0, 0.0).astype(BF16)
        earlier = jnp.where(r0 < r1, 1.0, 0.0).astype(BF16)

        def step(j, carry, masked):
            rows = pl.ds(pl.multiple_of(j * blk, blk), blk)
            hs = range(hp)
            cols = [pl.ds(hh * dh, dh) for hh in hs]
            zs = [_dot_nt(q_ref[:, cols[hh]], k_ref[rows, cols[hh]]) * scale for hh in hs]
            dws = [_dot_nt(do_ref[:, cols[hh]], v_ref[rows, cols[hh]]) for hh in hs]
            tiles = [_sb_tile(z, i, j, blk, key_lo, masked) for z in zs]
            sufs = [_running_sums(lk, later, reverse=True)[0] for _, _, lk in tiles]
            wgts = [_where_vis(vis, jnp.exp(ls + suf + _lane_pick(c_ref[hh], j)))
                    for hh, ((vis, ls, _), suf) in enumerate(zip(tiles, sufs))]
            es = [wgt * dw for wgt, dw in zip(wgts, dws)]
            pres = [_running_sums(e, earlier, reverse=False, exact=False) for e in es]
            dzs = []
            for hh in hs:
                vis, ls, _ = tiles[hh]
                before = _where_vis(vis, pres[hh][0] + carry[2 * hh + 1])
                sig = jnp.exp(ls)
                dzs.append(_bf((es[hh] * (1.0 - sig) - before * sig) * scale))
            dks = [_dot_tn(dzs[hh], q_ref[:, cols[hh]]) for hh in hs]
            dvs = [_dot_tn(_bf(wgts[hh]), do_ref[:, cols[hh]]) for hh in hs]
            dqs = [carry[2 * hh] + _dot(dzs[hh], k_ref[rows, cols[hh]]) for hh in hs]
            out = []
            for hh in hs:
                dk_ref[rows, cols[hh]] += dks[hh]
                dv_ref[rows, cols[hh]] += dvs[hh]
                out += [dqs[hh], carry[2 * hh + 1] + pres[hh][1]]
            return tuple(out)

        res = _sb_sweep(i, step, (jnp.zeros((blk, dh), F32), jnp.zeros((blk, 1), F32)) * hp, descending=False)
        for hh in range(hp):
            dq_ref[:, pl.ds(hh * dh, dh)] = res[2 * hh]

    full = pl.BlockSpec((t, hp * dh), lambda h, i: (0, h))
    qblk = pl.BlockSpec((blk, hp * dh), lambda h, i: (i, h))
    call = dict(grid=(heads // hp, nq),
                in_specs=[qblk, full, full, qblk, pl.BlockSpec((hp, blk, LANES), lambda h, i: (h, i, 0))],
                out_specs=[qblk, full, full], out_shape=[jax.ShapeDtypeStruct((t, heads * dh), F32)] * 3, name=name)
    if scatter is None:
        return pl.pallas_call(body, compiler_params=_params("parallel", "arbitrary"), **call)(qs, ks, vs, do, carry)
    res = _call_with_exchange(body, scratch_shapes=[], args=(qs, ks, vs, do, carry), srcs=scatter,
                              scatter=[True] * len(scatter), **call)
    return [*res[:3], list(res[3:])]


def _sb_pre_bwd(proj, gq, gk, dq, dk, dv, *, heads, dh, col_q, col_k, name):
    t = proj.shape[0]
    tm = _pick(t, (384, 256, 128))

    def body(q_ref, k_ref, gq_ref, gk_ref, dq_ref, dk_ref, dv_ref, oq_ref, ok_ref, ov_ref, dgq_ref, dgk_ref):
        @pl.when((pl.program_id(0) == 0) & (pl.program_id(1) == 0))
        def _():
            dgq_ref[...] = jnp.zeros_like(dgq_ref)
            dgk_ref[...] = jnp.zeros_like(dgk_ref)

        dq_, gq_r = _rms_bwd(q_ref[...], gq_ref[...], dq_ref[...])
        dk_, gk_r = _rms_bwd(k_ref[...], gk_ref[...], dk_ref[...])
        oq_ref[...] = _bf(dq_)
        ok_ref[...] = _bf(dk_)
        ov_ref[...] = _bf(dv_ref[...])
        dgq_ref[...] += jnp.sum(gq_r, axis=0, keepdims=True)
        dgk_ref[...] += jnp.sum(gk_r, axis=0, keepdims=True)

    def cur(col):
        return pl.BlockSpec((tm, dh), lambda i, h: (i, col // dh + h))

    gspec = pl.BlockSpec((1, dh), lambda i, h: (0, 0))
    ospec = pl.BlockSpec((tm, dh), lambda i, h: (i, h))
    return pl.pallas_call(
        body, grid=(t // tm, heads), in_specs=[cur(col_q), cur(col_k), gspec, gspec, ospec, ospec, ospec],
        out_specs=[ospec, ospec, ospec, gspec, gspec],
        out_shape=[jax.ShapeDtypeStruct((t, heads * dh), BF16)] * 3 + [jax.ShapeDtypeStruct((1, dh), F32)] * 2,
        name=name, compiler_params=_params("arbitrary", "arbitrary"))(proj, proj, gq, gk, dq, dk, dv)


PEERS = N_DEV - 1


def _exchange_copies(ins, outs, send_sems, recv_sems, local_sems, scatter):
    x, y, c = lax.axis_index("x"), lax.axis_index("y"), lax.axis_index("c")
    me = 4 * x + 2 * y + c
    copies = []
    for a in range(len(ins)):
        own = ins[a].at[me] if scatter[a] else ins[a]
        copies.append(pltpu.make_async_copy(own, outs[a].at[me], local_sems.at[a]))
        for k in range(1, N_DEV):
            px = (x + (k >> 2 & 1)) % 2
            py = (y + (k >> 1 & 1)) % 2
            pc = (c + (k & 1)) % 2
            src = ins[a].at[4 * px + 2 * py + pc] if scatter[a] else ins[a]
            copies.append(pltpu.make_async_remote_copy(
                src_ref=src, dst_ref=outs[a].at[me], send_sem=send_sems.at[a * PEERS + k - 1],
                recv_sem=recv_sems.at[a * PEERS + k - 1], device_id=(px, py, pc), device_id_type=MESH))
    return copies


def _exchange_shapes(srcs, scatter):
    return [jax.ShapeDtypeStruct(s.shape if sc else (N_DEV,) + s.shape, s.dtype) for s, sc in zip(srcs, scatter)]


def _exchange_sems(n):
    return [pltpu.SemaphoreType.DMA((n * PEERS,)), pltpu.SemaphoreType.DMA((n * PEERS,)), pltpu.SemaphoreType.DMA((n,))]


def _exchange(srcs, *, scatter, name):
    n = len(srcs)

    def body(*refs):
        copies = _exchange_copies(refs[:n], refs[n:2 * n], *refs[2 * n:], scatter)
        for cp in copies:
            cp.start()
        for cp in copies:
            cp.wait()

    any_spec = pl.BlockSpec(memory_space=pl.ANY)
    return pl.pallas_call(
        body, in_specs=[any_spec] * n, out_specs=[any_spec] * n, out_shape=_exchange_shapes(srcs, scatter),
        scratch_shapes=_exchange_sems(n), name=name,
        compiler_params=pltpu.CompilerParams(has_side_effects=True))(*srcs)


def _gather_two_level(srcs, *, name):
    n = len(srcs)

    def body(*refs):
        ins, outs = refs[:n], refs[n:2 * n]
        send_sems, recv_sems, local_sems = refs[2 * n:]
        x, y, c = lax.axis_index("x"), lax.axis_index("y"), lax.axis_index("c")
        chips = [(1 - x, y), (x, 1 - y), (1 - x, 1 - y)]

        def slab(a, px, py, pc):
            return outs[a].at[4 * px + 2 * py + pc]

        def copy(a, k, block, to, src=None):
            return pltpu.make_async_remote_copy(
                src_ref=slab(a, *block) if src is None else src, dst_ref=slab(a, *block),
                send_sem=send_sems.at[a * PEERS + k], recv_sem=recv_sems.at[a * PEERS + k],
                device_id=to, device_id_type=MESH)

        mine = [pltpu.make_async_copy(ins[a], slab(a, x, y, c), local_sems.at[a]) for a in range(n)]
        first = [copy(a, 0, (x, y, c), (x, y, 1 - c), src=ins[a]) for a in range(n)]
        first += [copy(a, 1 + j, (x, y, c), (*chip, c), src=ins[a]) for j, chip in enumerate(chips) for a in range(n)]
        for cp in mine + first:
            cp.start()
        passed = []
        for j, chip in enumerate(chips):
            for a in range(n):
                copy(a, 1 + j, (*chip, c), (x, y, c)).wait_recv()
                passed.append(copy(a, 4 + j, (*chip, c), (x, y, 1 - c)))
                passed[-1].start()
        for a in range(n):
            copy(a, 0, (x, y, 1 - c), (x, y, c)).wait_recv()
            for j, chip in enumerate(chips):
                copy(a, 4 + j, (*chip, 1 - c), (x, y, c)).wait_recv()
        for cp in first + passed:
            cp.wait_send()
        for cp in mine:
            cp.wait()

    any_spec = pl.BlockSpec(memory_space=pl.ANY)
    return pl.pallas_call(
        body, in_specs=[any_spec] * n, out_specs=[any_spec] * n, out_shape=_exchange_shapes(srcs, [False] * n),
        scratch_shapes=_exchange_sems(n), name=name,
        compiler_params=pltpu.CompilerParams(has_side_effects=True))(*srcs)


def _call_with_exchange(body, *, grid, in_specs, out_specs, out_shape, scratch_shapes, args, srcs, scatter, name):
    n, n_in, n_out, n_scr = len(srcs), len(args), len(out_shape), len(scratch_shapes)

    def full_body(*refs):
        ins, xin = refs[:n_in], refs[n_in:n_in + n]
        outs, xout = refs[n_in + n:n_in + n + n_out], refs[n_in + n + n_out:n_in + 2 * n + n_out]
        scr = refs[n_in + 2 * n + n_out:]
        ids = [pl.program_id(a) for a in range(len(grid))]
        first = functools.reduce(jnp.logical_and, [i == 0 for i in ids])
        last = functools.reduce(jnp.logical_and, [i == g - 1 for i, g in zip(ids, grid)])
        copies = _exchange_copies(xin, xout, *scr[n_scr:], scatter)

        @pl.when(first)
        def _():
            for cp in copies:
                cp.start()

        body(*ins, *outs, *scr[:n_scr])

        @pl.when(last)
        def _():
            for cp in copies:
                cp.wait()

    any_spec = pl.BlockSpec(memory_space=pl.ANY)
    return pl.pallas_call(
        full_body, grid=grid, in_specs=list(in_specs) + [any_spec] * n, out_specs=list(out_specs) + [any_spec] * n,
        out_shape=list(out_shape) + _exchange_shapes(srcs, scatter),
        scratch_shapes=list(scratch_shapes) + _exchange_sems(n), name=name,
        compiler_params=pltpu.CompilerParams(dimension_semantics=("arbitrary",) * len(grid),
                                             vmem_limit_bytes=V7X_VMEM_LIMIT_BYTES, has_side_effects=True))(*args, *srcs)


def _adam_math(g, w, m, v):
    m2 = ADAM_B1 * m + (1.0 - ADAM_B1) * g
    v2 = ADAM_B2 * v + (1.0 - ADAM_B2) * (g * g)
    m_hat = m2 / (1.0 - ADAM_B1 ** ADAM_STEP)
    v_hat = v2 / (1.0 - ADAM_B2 ** ADAM_STEP)
    return -ADAM_LR * (m_hat / (jnp.sqrt(v_hat) + ADAM_EPS) + ADAM_WD * w), m2, v2


def _adamw_slabs(slabs, w, m, v, *, name):
    r, c = w.shape
    tr = next((t for t in (2256, 752, 512, 240, 128, 64, 32, 16) if r % t == 0), r)

    def body(s_ref, w_ref, m_ref, v_ref, g_ref, d_ref, mo_ref, vo_ref):
        g = s_ref[0].astype(F32)
        for p in range(1, N_DEV):
            g = g + s_ref[p].astype(F32)
        g_ref[...] = g
        d_ref[...], mo_ref[...], vo_ref[...] = _adam_math(g, w_ref[...], m_ref[...], v_ref[...])

    spec = pl.BlockSpec((tr, c), lambda i: (i, 0))
    return pl.pallas_call(
        body, grid=(r // tr,), in_specs=[pl.BlockSpec((N_DEV, tr, c), lambda i: (0, i, 0)), spec, spec, spec],
        out_specs=[spec] * 4, out_shape=[jax.ShapeDtypeStruct((r, c), F32)] * 4, name=name,
        compiler_params=_params("parallel"))(slabs, w, m, v)


def _adamw_small(g, w, m, v, *, name):
    def body(g_ref, w_ref, m_ref, v_ref, d_ref, mo_ref, vo_ref):
        d_ref[...], mo_ref[...], vo_ref[...] = _adam_math(g_ref[...], w_ref[...], m_ref[...], v_ref[...])

    return pl.pallas_call(body, out_shape=[jax.ShapeDtypeStruct(w.shape, F32)] * 3, name=name)(g, w, m, v)


def _sum_slabs(slabs, *, name):
    def body(s_ref, o_ref):
        acc = s_ref[0]
        for p in range(1, N_DEV):
            acc = acc + s_ref[p]
        o_ref[...] = acc

    return pl.pallas_call(body, out_shape=jax.ShapeDtypeStruct(slabs.shape[1:], F32), name=name)(slabs)


def _pad_lanes(a):
    return jnp.pad(a, ((0, 0), (0, LANES - a.shape[1])))


def _w_in_slabs(main, ab, col_ab, n_ab):
    pw = main.shape[0] + n_ab
    c = pw // N_DEV
    parts = [(0, col_ab, main, 0), (col_ab, col_ab + n_ab, ab, col_ab), (col_ab + n_ab, pw, main, n_ab)]
    slabs = []
    for p in range(N_DEV):
        pieces = []
        for lo, hi, src, shift in parts:
            a, b = max(lo, c * p), min(hi, c * (p + 1))
            if a < b:
                pieces.append(src[a - shift:b - shift])
        slabs.append(jnp.concatenate(pieces, axis=0))
    return jnp.stack(slabs)


def _local_step(x, target, meta, g_mix, wt_main, wt_ab, cq, ck, cv, a_log, dt_bias, g_dn, g_sbq, g_sbk, g_ffn, rest,
                shards=False):
    seq, d = x.shape
    n_meta = meta.shape[0]
    heads = a_log.shape[1]
    qk = cq.shape[1]
    dvt = cv.shape[1]
    dk, dv = qk // heads, dvt // heads
    dh = g_sbq.shape[1]
    sbw = rest[2].shape[0] * N_DEV if shards else rest[1].shape[0]
    sb_heads = sbw // dh
    pad_l = (-n_meta) % CHUNK
    row_x = pad_l + n_meta
    rows = row_x + seq
    t = -(-rows // GDN_ROWS) * GDN_ROWS
    col_q, col_k, col_v, col_z = 0, qk, 2 * qk, 2 * qk + dvt
    col_sq = 2 * qk + 2 * dvt
    col_sk, col_sv, col_gd, col_gs = col_sq + sbw, col_sq + 2 * sbw, col_sq + 3 * sbw, col_sq + 3 * sbw + d

    def rows_pad(a):
        return jnp.concatenate([jnp.zeros((row_x, d), F32), a, jnp.zeros((t - rows, d), F32)], axis=0)

    h0 = jnp.concatenate([jnp.zeros((pad_l, d), F32), meta, x, jnp.zeros((t - rows, d), F32)], axis=0)
    tgt = rows_pad(target)
    a_log_p, dt_p = _pad_lanes(a_log), _pad_lanes(dt_bias)

    proj, n1 = _mm_norm(h0, g_mix, wt_main, name="proj")
    pab = _mm_nt(n1, wt_ab, out_dtype=F32, name="proj_ab")
    gk = dict(heads=heads, dk=dk, dv=dv, col_q=col_q, col_k=col_k, col_v=col_v, row_lo=pad_l, row_hi=rows)
    qn, kn, vv, g, beta = _gdn_pre(proj, pab, cq, ck, cv, a_log_p, dt_p, name="gdn_pre", **gk)
    u, w, pm, qd, kd, egl, tinv = _gdn_prep(qn, kn, vv, g, beta, name="gdn_prep")
    o_raw, o_dn, states = _gdn_scan(u, w, pm, qd, kd, egl, proj, g_dn, col_z=col_z, name="gdn_scan")
    qs, ks, vs = _sb_pre(proj, g_sbq, g_sbk, heads=sb_heads, dh=dh, col_q=col_sq, col_k=col_sk, col_v=col_sv,
                         name="sb_pre")
    if shards:
        o_sb, carry, (g_fi, g_bd, g_bs, g_out, g_fo) = _sb_fwd(qs, ks, vs, heads=sb_heads, dh=dh, key_lo=pad_l,
                                                                name="sb_fwd", gather=list(rest))
        wt_fi = g_fi.reshape(-1, d)
        d_ff = wt_fi.shape[0] // 2
        w_bd, w_bs, w_out, wt_fg, wt_fu, w_fo = (g_bd.reshape(-1, d), g_bs.reshape(-1, d), g_out.reshape(-1, d),
                                                 wt_fi[:d_ff], wt_fi[d_ff:], g_fo.reshape(-1, d))
    else:
        o_sb, carry = _sb_fwd(qs, ks, vs, heads=sb_heads, dh=dh, key_lo=pad_l, name="sb_fwd")
        w_bd, w_bs, w_out, wt_fg, wt_fu, w_fo = rest
    merged, br_dn, br_sb = _merge_fwd(o_dn, o_sb, w_bd, w_bs, proj, col_gd=col_gd, col_gs=col_gs, name="merge")
    h1 = _mm_res(h0, merged, w_out, name="mix_out")
    gate, up, act, n2 = _mm_norm_swiglu(h1, g_ffn, wt_fg, wt_fu, name="ffn_in")
    dy, dyb, lsum = _mm_res_loss(h1, act, w_fo, tgt, row0=row_x, nrows=seq, name="ffn_out_loss")

    dgate, dup = _swiglu_bwd(dyb, w_fo, gate, up, name="ffn_out_bwd")
    d_w_fo = _mm_tn(act, dyb, name="dw_ffn_out")
    d_wt_fg = _mm_tn(dgate, n2, name="dw_ffn_gate")
    d_wt_fu = _mm_tn(dup, n2, name="dw_ffn_up")
    dh1, dh1b, d_g_ffn = _mm_rmsbwd([(dgate, wt_fg), (dup, wt_fu)], None, h1, g_ffn, dy, name="ffn_in_bwd")

    dbd, dbs, dgd, dgs = _merge_bwd(dh1b, w_out, proj, br_dn, br_sb, col_gd=col_gd, col_gs=col_gs, name="mix_out_bwd")
    d_w_out = _mm_tn(merged, dh1b, name="dw_out")
    d_w_bd = _mm_tn(o_dn, dbd, name="dw_branch_dn")
    d_w_bs = _mm_tn(o_sb, dbs, name="dw_branch_sb")
    do_dn = _mm_nt(dbd, w_bd, out_dtype=F32, name="branch_dn_bwd")
    do_sb = _mm_nt(dbs, w_bs, out_dtype=BF16, name="branch_sb_bwd")

    do_raw, dz, d_g_dn = _gdn_post_bwd(o_raw, proj, g_dn, do_dn, col_z=col_z, name="gdn_post_bwd")
    du, dw, dp, dqd, dkd, dgl = _gdn_bwd_scan(u, w, pm, qd, kd, egl, states, do_raw, name="gdn_bwd_scan")
    dqn, dkn, dvv, dg, dbeta = _gdn_bwd_prep(qn, kn, vv, g, beta, tinv, u, w, du, dw, dp, dqd, dkd, dgl,
                                            name="gdn_bwd_prep")
    dcq, dck, dcv, dpab, d_a_log, d_dt = _gdn_pre_bwd_a(proj, pab, cq, ck, cv, a_log_p, dt_p, dqn, dkn, dvv, dg, dbeta,
                                                        name="gdn_pre_bwd", **gk)
    dpq, d_cq = _conv_bwd(proj, dcq, cq, heads=heads, width=dk, col=col_q, name="conv_q_bwd")
    dpk, d_ck = _conv_bwd(proj, dck, ck, heads=heads, width=dk, col=col_k, name="conv_k_bwd")
    dpv, d_cv = _conv_bwd(proj, dcv, cv, heads=heads, width=dv, col=col_v, name="conv_v_bwd")

    early = None
    if shards:
        slabs = [_bf(jnp.concatenate([d_wt_fg, d_wt_fu], axis=0)).reshape(N_DEV, -1, LANES)]
        slabs += [_bf(a).reshape(N_DEV, -1, d) for a in (d_w_bd, d_w_bs, d_w_out, d_w_fo)]
        dqs, dks, dvs, early = _sb_bwd(qs, ks, vs, do_sb, carry, heads=sb_heads, dh=dh, key_lo=pad_l, name="sb_bwd",
                                       scatter=slabs)
    else:
        dqs, dks, dvs = _sb_bwd(qs, ks, vs, do_sb, carry, heads=sb_heads, dh=dh, key_lo=pad_l, name="sb_bwd")
    dsq, dsk, dsv, d_g_sbq, d_g_sbk = _sb_pre_bwd(proj, g_sbq, g_sbk, dqs, dks, dvs, heads=sb_heads, dh=dh,
                                                   col_q=col_sq, col_k=col_sk, name="sb_pre_bwd")

    dproj = jnp.concatenate([dpq, dpk, dpv, dz, dsq, dsk, dsv, dgd, dgs], axis=1)
    dpab_b = _bf(dpab)
    d_wt_main = _mm_tn(dproj, n1, name="dw_in_main")
    d_wt_ab = _mm_tn(dpab_b, n1, name="dw_in_ab")
    s_in = None
    if shards:
        slabs = _w_in_slabs(_bf(d_wt_main), _bf(d_wt_ab), col_sq, 2 * heads).reshape(N_DEV, -1, LANES)
        dh0, _, d_g_mix, (s_in,) = _mm_rmsbwd([(dproj, wt_main)], (dpab_b, wt_ab), h0, g_mix, dh1, name="proj_bwd",
                                              scatter=[slabs])
    else:
        dh0, _, d_g_mix = _mm_rmsbwd([(dproj, wt_main)], (dpab_b, wt_ab), h0, g_mix, dh1, name="proj_bwd")

    return dict(s_in=s_in, lsum=lsum, grad_x=dh0[row_x:rows], d_meta=dh0[pad_l:row_x], d_g_mix=d_g_mix,
                d_wt_main=d_wt_main, d_wt_ab=d_wt_ab, d_cq=d_cq, d_ck=d_ck, d_cv=d_cv, d_a_log=d_a_log[:, :heads],
                d_dt=d_dt[:, :heads], d_g_dn=d_g_dn, d_g_sbq=d_g_sbq, d_g_sbk=d_g_sbk, d_w_bd=d_w_bd, d_w_bs=d_w_bs,
                d_w_out=d_w_out, d_g_ffn=d_g_ffn, d_wt_fg=d_wt_fg, d_wt_fu=d_wt_fu, d_w_fo=d_w_fo, early=early)


def _pack(parts):
    flat = []
    for a in parts:
        a = a.reshape(-1)
        flat.append(jnp.pad(a, (0, (-a.shape[0]) % LANES)))
    v = jnp.concatenate(flat)
    v = jnp.pad(v, (0, (-v.shape[0]) % (8 * LANES)))
    return v.reshape(-1, LANES)


def _unpack(packed, shapes):
    flat = packed.reshape(-1)
    out, pos = [], 0
    for s in shapes:
        n = math.prod(s)
        out.append(flat[pos:pos + n].reshape(s))
        pos += n + (-n) % LANES
    return out


def kernel(x, meta_tokens, norm_mix_gain, w_in, conv_q, conv_k, conv_v, dn_a_log, dn_dt_bias, dn_out_norm_gain, sb_q_norm_gain, sb_k_norm_gain, w_branch_dn, w_branch_sb, w_out, norm_ffn_gain, w_ffn_in, w_ffn_out, loss_target, m_meta_tokens, m_norm_mix_gain, m_w_in, m_conv_q, m_conv_k, m_conv_v, m_dn_a_log, m_dn_dt_bias, m_dn_out_norm_gain, m_sb_q_norm_gain, m_sb_k_norm_gain, m_w_branch_dn, m_w_branch_sb, m_w_out, m_norm_ffn_gain, m_w_ffn_in, m_w_ffn_out, v_meta_tokens, v_norm_mix_gain, v_w_in, v_conv_q, v_conv_k, v_conv_v, v_dn_a_log, v_dn_dt_bias, v_dn_out_norm_gain, v_sb_q_norm_gain, v_sb_k_norm_gain, v_w_branch_dn, v_w_branch_sb, v_w_out, v_norm_ffn_gain, v_w_ffn_in, v_w_ffn_out):
    me = 4 * lax.axis_index("x") + 2 * lax.axis_index("y") + lax.axis_index("c")
    heads = dn_a_log.shape[1]
    d = x.shape[2]
    qk = conv_q.shape[2] * N_DEV
    dvt = conv_v.shape[2] * N_DEV
    col_ab = 2 * qk + 2 * dvt

    small_shapes = [meta_tokens.shape, conv_q.shape[1:], conv_k.shape[1:], conv_v.shape[1:]]
    small = _pack([meta_tokens, conv_q[0], conv_k[0], conv_v[0]])
    def features_major(a):
        return jnp.transpose(a, (2, 0, 1)).reshape(a.shape[2], a.shape[1])

    g_in, g_small = _gather_two_level([_bf(features_major(w_in)), small], name="gather_w_in")
    wt_full = g_in.reshape(-1, d)
    wt_main = jnp.concatenate([wt_full[:col_ab], wt_full[col_ab + 2 * heads:]], axis=0)
    wt_ab = jnp.pad(wt_full[col_ab:col_ab + 2 * heads], ((0, LANES - 2 * heads), (0, 0)))
    parts = [_unpack(g_small[p], small_shapes) for p in range(N_DEV)]
    meta_f, cq_f, ck_f, cv_f = (jnp.concatenate([parts[p][a] for p in range(N_DEV)], axis=1) for a in range(4))

    r = _local_step(x[0], loss_target[0], meta_f, norm_mix_gain, wt_main, wt_ab, cq_f, ck_f, cv_f, dn_a_log, dn_dt_bias,
                    dn_out_norm_gain, sb_q_norm_gain, sb_k_norm_gain, norm_ffn_gain,
                    (_bf(features_major(w_ffn_in)), _bf(w_branch_dn[0]), _bf(w_branch_sb[0]), _bf(w_out[0]),
                     _bf(w_ffn_out[0])), shards=True)
    s_fi, s_bd, s_bs, s_out, s_fo = r["early"]
    s_in = r["s_in"]

    loss_part = (0.5 / d) * jnp.sum(r["lsum"], axis=1, keepdims=True)
    small_g = [r["d_meta"], r["d_g_mix"], r["d_cq"], r["d_ck"], r["d_cv"], r["d_a_log"], r["d_dt"], r["d_g_dn"],
               r["d_g_sbq"], r["d_g_sbk"], r["d_g_ffn"], loss_part]
    (g_packs,) = _exchange([_pack(small_g)], scatter=[False], name="gather_small_grads")
    (g_meta, g_mix, g_cq, g_ck, g_cv, g_al, g_dt, g_gdn, g_sbq, g_sbk, g_ffn, loss) = _unpack(
        _sum_slabs(g_packs, name="sum_small_grads"), [a.shape for a in small_g])

    def mine(a, width):
        return lax.dynamic_slice_in_dim(a, me * width, width, axis=1)

    big = dict(w_in=(s_in, w_in, m_w_in, v_w_in), w_branch_dn=(s_bd, w_branch_dn, m_w_branch_dn, v_w_branch_dn),
               w_branch_sb=(s_bs, w_branch_sb, m_w_branch_sb, v_w_branch_sb), w_out=(s_out, w_out, m_w_out, v_w_out),
               w_ffn_in=(s_fi, w_ffn_in, m_w_ffn_in, v_w_ffn_in), w_ffn_out=(s_fo, w_ffn_out, m_w_ffn_out, v_w_ffn_out))
    tiny = dict(meta_tokens=(mine(g_meta, d // N_DEV), meta_tokens, m_meta_tokens, v_meta_tokens),
                norm_mix_gain=(g_mix, norm_mix_gain, m_norm_mix_gain, v_norm_mix_gain),
                conv_q=(mine(g_cq, qk // N_DEV), conv_q[0], m_conv_q[0], v_conv_q[0]),
                conv_k=(mine(g_ck, qk // N_DEV), conv_k[0], m_conv_k[0], v_conv_k[0]),
                conv_v=(mine(g_cv, dvt // N_DEV), conv_v[0], m_conv_v[0], v_conv_v[0]),
                dn_a_log=(g_al, dn_a_log, m_dn_a_log, v_dn_a_log), dn_dt_bias=(g_dt, dn_dt_bias, m_dn_dt_bias, v_dn_dt_bias),
                dn_out_norm_gain=(g_gdn, dn_out_norm_gain, m_dn_out_norm_gain, v_dn_out_norm_gain),
                sb_q_norm_gain=(g_sbq, sb_q_norm_gain, m_sb_q_norm_gain, v_sb_q_norm_gain),
                sb_k_norm_gain=(g_sbk, sb_k_norm_gain, m_sb_k_norm_gain, v_sb_k_norm_gain),
                norm_ffn_gain=(g_ffn, norm_ffn_gain, m_norm_ffn_gain, v_norm_ffn_gain))
    order = ["meta_tokens", "norm_mix_gain", "w_in", "conv_q", "conv_k", "conv_v", "dn_a_log", "dn_dt_bias",
             "dn_out_norm_gain", "sb_q_norm_gain", "sb_k_norm_gain", "w_branch_dn", "w_branch_sb", "w_out",
             "norm_ffn_gain", "w_ffn_in", "w_ffn_out"]
    grads, deltas, new_m, new_v = [], [], [], []
    for name in order:
        if name in ("w_in", "w_ffn_in"):
            slabs, w, m, v = big[name]
            res = _adamw_slabs(slabs, *(features_major(a).reshape(-1, LANES) for a in (w, m, v)), name="adamw_" + name)
            g, dl, mo, vo = (jnp.transpose(a.reshape(w.shape[2], 1, w.shape[1]), (1, 2, 0)) for a in res)
            like = w.shape
        elif name in big:
            slabs, w, m, v = big[name]
            g, dl, mo, vo = _adamw_slabs(slabs, w[0], m[0], v[0], name="adamw_" + name)
            like = w.shape
        else:
            g, w, m, v = tiny[name]
            like = dict(conv_q=conv_q, conv_k=conv_k, conv_v=conv_v).get(name, w).shape
            dl, mo, vo = _adamw_small(g, w, m, v, name="adamw_" + name)
        for lst, a in ((grads, g), (deltas, dl), (new_m, mo), (new_v, vo)):
            lst.append(a.reshape(like))
    return (loss.reshape(()), r["grad_x"][None], *grads, *deltas, *new_m, *new_v)
```

```python
import functools
import math

import jax
import jax.numpy as jnp
from jax import lax
from jax.experimental import pallas as pl
from jax.experimental.pallas import tpu as pltpu

F32 = jnp.float32
BF16 = jnp.bfloat16

N_DEV = 8
CHUNK = 64
CHUNK_SHIFT = 6
GDN_ROWS = 2 * CHUNK
SB_BLOCK = 128
SB_HEADS_PER_STEP = 2
LANES = 128
RMS_EPS = 1e-6
L2_EPS = 1e-6
ADAM_LR = 0.001
ADAM_B1 = 0.9
ADAM_B2 = 0.999
ADAM_EPS = 1e-08
ADAM_WD = 0.01
ADAM_STEP = 10
V7X_VMEM_LIMIT_BYTES = 56 * 1024 * 1024
MM_TN_OUT_BLOCK_BYTES = 6 * 1024 * 1024
ROWS_BIG = (1056, 512, 384, 256, 128)
ROWS_MID = (528, 384, 256, 128)
SCAN_HEADS = 4
PREP_HEADS = 2

MESH = pl.DeviceIdType.MESH


def _params(*sem):
    return pltpu.CompilerParams(dimension_semantics=sem or None, vmem_limit_bytes=V7X_VMEM_LIMIT_BYTES)


def _pick(n, cands):
    for c in cands:
        if n % c == 0:
            return c
    raise ValueError(f"no block size among {cands} divides {n}")


def _bf(x):
    return x.astype(BF16)


def _dot(a, b):
    return jnp.dot(a, b, preferred_element_type=F32)


def _dot_nt(a, b):
    return lax.dot_general(a, b, (((1,), (1,)), ((), ())), preferred_element_type=F32)


def _dot_tn(a, b):
    return lax.dot_general(a, b, (((0,), (0,)), ((), ())), preferred_element_type=F32)


def _split2(x):
    hi = _bf(x)
    return hi, _bf(x - hi.astype(F32))


def _split3(x):
    hi = _bf(x)
    r = x - hi.astype(F32)
    mid = _bf(r)
    return hi, mid, _bf(r - mid.astype(F32))


def _dot_hp(a, b, dot=_dot):
    ah, al = _split2(a)
    bh, bl = _split2(b)
    return dot(ah, bh) + dot(ah, bl) + dot(al, bh)


def _dot_exact_l(m, x, dot=_dot):
    h, mi, lo = _split3(x)
    return dot(m, h) + dot(m, mi) + dot(m, lo)


def _sigmoid(x):
    return 1.0 / (1.0 + jnp.exp(-x))


def _silu(x):
    return x * _sigmoid(x)


def _silu_grad(x):
    s = _sigmoid(x)
    return s * (1.0 + x * (1.0 - s))


def _softplus(x):
    return jnp.maximum(x, 0.0) + jnp.log(1.0 + jnp.exp(-jnp.abs(x)))


def _rms_fwd(h, gain):
    r = lax.rsqrt(jnp.mean(h * h, axis=-1, keepdims=True) + RMS_EPS)
    return h * r * gain


def _rms_bwd(h, gain, dy):
    r = lax.rsqrt(jnp.mean(h * h, axis=-1, keepdims=True) + RMS_EPS)
    dyg = dy * gain
    dh = r * dyg - h * (r * r * r) * jnp.mean(dyg * h, axis=-1, keepdims=True)
    return dh, dy * h * r


def _iota(shape, dim):
    return lax.broadcasted_iota(jnp.int32, shape, dim)


def _lane_pick(x, idx):
    return jnp.sum(jnp.where(_iota(x.shape, 1) == idx, x, 0.0), axis=1, keepdims=True)


def _mm_nt(a, b, *, out_dtype, name):
    m, k = a.shape
    n = b.shape[0]
    tm, tn = _pick(m, ROWS_BIG), _pick(n, (1024, 512, 256, 128))

    def body(a_ref, b_ref, o_ref):
        o_ref[...] = _dot_nt(a_ref[...], b_ref[...]).astype(out_dtype)

    return pl.pallas_call(
        body, grid=(m // tm, n // tn),
        in_specs=[pl.BlockSpec((tm, k), lambda i, j: (i, 0)), pl.BlockSpec((tn, k), lambda i, j: (j, 0))],
        out_specs=pl.BlockSpec((tm, tn), lambda i, j: (i, j)),
        out_shape=jax.ShapeDtypeStruct((m, n), out_dtype), name=name,
        compiler_params=_params("parallel", "parallel"))(a, b)


def _mm_tn(a, b, *, name):
    t, m = a.shape
    n = b.shape[1]
    tn = _pick(n, (2816, 2048, 1408, 1024, 512, 256, 128))
    tm = _pick(m, tuple(c for c in (1408, 1024, 512, 256, 128) if c * tn * 4 <= MM_TN_OUT_BLOCK_BYTES))
    tk = _pick(t, (1408, 1024, 512, 384, 256, 128))
    nk = t // tk

    def body(a_ref, b_ref, o_ref, acc_ref):
        k = pl.program_id(2)

        @pl.when(k == 0)
        def _():
            acc_ref[...] = jnp.zeros_like(acc_ref)

        acc_ref[...] += _dot_tn(a_ref[...], b_ref[...])

        @pl.when(k == nk - 1)
        def _():
            o_ref[...] = _bf(acc_ref[...])

    return pl.pallas_call(
        body, grid=(m // tm, n // tn, nk),
        in_specs=[pl.BlockSpec((tk, tm), lambda i, j, k: (k, i)), pl.BlockSpec((tk, tn), lambda i, j, k: (k, j))],
        out_specs=pl.BlockSpec((tm, tn), lambda i, j, k: (i, j)),
        out_shape=jax.ShapeDtypeStruct((m, n), BF16), scratch_shapes=[pltpu.VMEM((tm, tn), F32)], name=name,
        compiler_params=_params("parallel", "parallel", "arbitrary"))(a, b)


def _mm_norm(h, gain, wt, *, name):
    m, k = h.shape
    n = wt.shape[0]
    tm, tn = _pick(m, ROWS_BIG), _pick(n, (1024, 512, 256, 128))

    def body(h_ref, g_ref, w_ref, o_ref, n_ref):
        @pl.when(pl.program_id(1) == 0)
        def _():
            n_ref[...] = _bf(_rms_fwd(h_ref[...], g_ref[...]))

        o_ref[...] = _dot_nt(n_ref[...], w_ref[...])

    return pl.pallas_call(
        body, grid=(m // tm, n // tn),
        in_specs=[pl.BlockSpec((tm, k), lambda i, j: (i, 0)), pl.BlockSpec((1, k), lambda i, j: (0, 0)),
                  pl.BlockSpec((tn, k), lambda i, j: (j, 0))],
        out_specs=[pl.BlockSpec((tm, tn), lambda i, j: (i, j)), pl.BlockSpec((tm, k), lambda i, j: (i, 0))],
        out_shape=[jax.ShapeDtypeStruct((m, n), F32), jax.ShapeDtypeStruct((m, k), BF16)], name=name,
        compiler_params=_params("parallel", "arbitrary"))(h, gain, wt)


def _mm_norm_swiglu(h, gain, wgt, wut, *, name):
    m, k = h.shape
    n = wgt.shape[0]
    tm, tn = _pick(m, ROWS_MID), _pick(n, (1408, 1024, 512, 256, 128))

    def body(h_ref, g_ref, wg_ref, wu_ref, gate_ref, up_ref, act_ref, n_ref):
        @pl.when(pl.program_id(1) == 0)
        def _():
            n_ref[...] = _bf(_rms_fwd(h_ref[...], g_ref[...]))

        gate = _dot_nt(n_ref[...], wg_ref[...])
        up = _dot_nt(n_ref[...], wu_ref[...])
        gate_ref[...] = gate
        up_ref[...] = up
        act_ref[...] = _bf(_silu(gate) * up)

    wspec = pl.BlockSpec((tn, k), lambda i, j: (j, 0))
    ospec = pl.BlockSpec((tm, tn), lambda i, j: (i, j))
    return pl.pallas_call(
        body, grid=(m // tm, n // tn),
        in_specs=[pl.BlockSpec((tm, k), lambda i, j: (i, 0)), pl.BlockSpec((1, k), lambda i, j: (0, 0)), wspec, wspec],
        out_specs=[ospec, ospec, ospec, pl.BlockSpec((tm, k), lambda i, j: (i, 0))],
        out_shape=[jax.ShapeDtypeStruct((m, n), F32), jax.ShapeDtypeStruct((m, n), F32),
                   jax.ShapeDtypeStruct((m, n), BF16), jax.ShapeDtypeStruct((m, k), BF16)], name=name,
        compiler_params=_params("parallel", "arbitrary"))(h, gain, wgt, wut)


def _mm_res(res, a, b, *, name):
    m, k = a.shape
    n = b.shape[1]
    tm, tn = _pick(m, ROWS_BIG), _pick(n, (1024, 512, 256, 128))

    def body(r_ref, a_ref, b_ref, o_ref):
        o_ref[...] = r_ref[...] + _dot(a_ref[...], b_ref[...])

    return pl.pallas_call(
        body, grid=(m // tm, n // tn),
        in_specs=[pl.BlockSpec((tm, tn), lambda i, j: (i, j)), pl.BlockSpec((tm, k), lambda i, j: (i, 0)),
                  pl.BlockSpec((k, tn), lambda i, j: (0, j))],
        out_specs=pl.BlockSpec((tm, tn), lambda i, j: (i, j)),
        out_shape=jax.ShapeDtypeStruct((m, n), F32), name=name,
        compiler_params=_params("parallel", "parallel"))(res, a, b)


def _mm_res_loss(res, a, b, target, *, row0, nrows, name):
    m, k = a.shape
    n = b.shape[1]
    tm = _pick(m, ROWS_MID)

    def body(r_ref, a_ref, b_ref, t_ref, dy_ref, dyb_ref, ls_ref):
        i = pl.program_id(0)

        @pl.when(i == 0)
        def _():
            ls_ref[...] = jnp.zeros_like(ls_ref)

        y = r_ref[...] + _dot(a_ref[...], b_ref[...])
        row = i * tm + _iota((tm, n), 0)
        e = jnp.where((row >= row0) & (row < row0 + nrows), y - t_ref[...], 0.0)
        dy = e / n
        dy_ref[...] = dy
        dyb_ref[...] = _bf(dy)
        ls_ref[...] += jnp.sum(e * e, axis=0, keepdims=True)

    rspec = pl.BlockSpec((tm, n), lambda i: (i, 0))
    return pl.pallas_call(
        body, grid=(m // tm,),
        in_specs=[rspec, pl.BlockSpec((tm, k), lambda i: (i, 0)), pl.BlockSpec((k, n), lambda i: (0, 0)), rspec],
        out_specs=[rspec, rspec, pl.BlockSpec((1, n), lambda i: (0, 0))],
        out_shape=[jax.ShapeDtypeStruct((m, n), F32), jax.ShapeDtypeStruct((m, n), BF16),
                   jax.ShapeDtypeStruct((1, n), F32)], name=name,
        compiler_params=_params("arbitrary"))(res, a, b, target)


def _merge_fwd(o_dn, o_sb, wbd, wbs, proj, *, col_gd, col_gs, name):
    m, kd = o_dn.shape
    ks = o_sb.shape[1]
    n = wbd.shape[1]
    tm = _pick(m, ROWS_BIG)
    tn = _pick(math.gcd(n, math.gcd(col_gd, col_gs)), (512, 256, 128))

    def body(od_ref, os_ref, wd_ref, ws_ref, gd_ref, gs_ref, mg_ref, bd_ref, bs_ref):
        bd = _dot(od_ref[...], wd_ref[...])
        bs = _dot(os_ref[...], ws_ref[...])
        bd_ref[...] = bd
        bs_ref[...] = bs
        mg_ref[...] = _bf(_sigmoid(gd_ref[...]) * bd + _sigmoid(gs_ref[...]) * bs)

    ospec = pl.BlockSpec((tm, tn), lambda i, j: (i, j))
    return pl.pallas_call(
        body, grid=(m // tm, n // tn),
        in_specs=[pl.BlockSpec((tm, kd), lambda i, j: (i, 0)), pl.BlockSpec((tm, ks), lambda i, j: (i, 0)),
                  pl.BlockSpec((kd, tn), lambda i, j: (0, j)), pl.BlockSpec((ks, tn), lambda i, j: (0, j)),
                  pl.BlockSpec((tm, tn), lambda i, j: (i, col_gd // tn + j)),
                  pl.BlockSpec((tm, tn), lambda i, j: (i, col_gs // tn + j))],
        out_specs=[ospec, ospec, ospec],
        out_shape=[jax.ShapeDtypeStruct((m, n), BF16), jax.ShapeDtypeStruct((m, n), F32),
                   jax.ShapeDtypeStruct((m, n), F32)], name=name,
        compiler_params=_params("parallel", "parallel"))(o_dn, o_sb, wbd, wbs, proj, proj)


def _merge_bwd(dh, w_out, proj, br_dn, br_sb, *, col_gd, col_gs, name):
    m, k = dh.shape
    n = w_out.shape[0]
    tm = _pick(m, ROWS_BIG)
    tn = _pick(math.gcd(n, math.gcd(col_gd, col_gs)), (512, 256, 128))

    def body(dh_ref, w_ref, gd_ref, gs_ref, bd_ref, bs_ref, dbd_ref, dbs_ref, dgd_ref, dgs_ref):
        dm = _dot_nt(dh_ref[...], w_ref[...])
        sd = _sigmoid(gd_ref[...])
        ss = _sigmoid(gs_ref[...])
        dbd_ref[...] = _bf(dm * sd)
        dbs_ref[...] = _bf(dm * ss)
        dgd_ref[...] = _bf(dm * bd_ref[...] * sd * (1.0 - sd))
        dgs_ref[...] = _bf(dm * bs_ref[...] * ss * (1.0 - ss))

    ospec = pl.BlockSpec((tm, tn), lambda i, j: (i, j))
    return pl.pallas_call(
        body, grid=(m // tm, n // tn),
        in_specs=[pl.BlockSpec((tm, k), lambda i, j: (i, 0)), pl.BlockSpec((tn, k), lambda i, j: (j, 0)),
                  pl.BlockSpec((tm, tn), lambda i, j: (i, col_gd // tn + j)),
                  pl.BlockSpec((tm, tn), lambda i, j: (i, col_gs // tn + j)), ospec, ospec],
        out_specs=[ospec] * 4,
        out_shape=[jax.ShapeDtypeStruct((m, n), BF16)] * 4, name=name,
        compiler_params=_params("parallel", "parallel"))(dh, w_out, proj, proj, br_dn, br_sb)


def _swiglu_bwd(dy, wfo, gate, up, *, name):
    m, k = dy.shape
    n = wfo.shape[0]
    tm, tn = _pick(m, ROWS_MID), _pick(n, (1408, 1024, 512, 256, 128))

    def body(dy_ref, w_ref, g_ref, u_ref, dg_ref, du_ref):
        da = _dot_nt(dy_ref[...], w_ref[...])
        g = g_ref[...]
        dg_ref[...] = _bf(da * u_ref[...] * _silu_grad(g))
        du_ref[...] = _bf(da * _silu(g))

    ospec = pl.BlockSpec((tm, tn), lambda i, j: (i, j))
    return pl.pallas_call(
        body, grid=(m // tm, n // tn),
        in_specs=[pl.BlockSpec((tm, k), lambda i, j: (i, 0)), pl.BlockSpec((tn, k), lambda i, j: (j, 0)), ospec, ospec],
        out_specs=[ospec, ospec], out_shape=[jax.ShapeDtypeStruct((m, n), BF16)] * 2, name=name,
        compiler_params=_params("parallel", "parallel"))(dy, wfo, gate, up)


def _mm_rmsbwd(pairs, extra, h, gain, dres, *, name, scatter=None):
    m, k = pairs[0][0].shape
    n = h.shape[1]
    tm = _pick(m, ROWS_MID)
    tk = _pick(k, (1408, 1024, 512, 256, 128))
    nk = k // tk
    np_ = len(pairs)

    def body(*refs):
        ab = refs[:2 * np_]
        pos = 2 * np_
        ex = refs[pos:pos + 2] if extra is not None else ()
        pos += len(ex)
        h_ref, g_ref, r_ref, dh_ref, dhb_ref, dg_ref, acc_ref = refs[pos:]
        i, kk = pl.program_id(0), pl.program_id(1)

        @pl.when((i == 0) & (kk == 0))
        def _():
            dg_ref[...] = jnp.zeros_like(dg_ref)

        part = _dot(ab[0][...], ab[1][...])
        for p in range(1, np_):
            part += _dot(ab[2 * p][...], ab[2 * p + 1][...])

        @pl.when(kk == 0)
        def _():
            first = part
            if ex:
                first = first + _dot(ex[0][...], ex[1][...])
            acc_ref[...] = first

        @pl.when(kk > 0)
        def _():
            acc_ref[...] += part

        @pl.when(kk == nk - 1)
        def _():
            dh, dgr = _rms_bwd(h_ref[...], g_ref[...], acc_ref[...])
            dh = dh + r_ref[...]
            dh_ref[...] = dh
            dhb_ref[...] = _bf(dh)
            dg_ref[...] += jnp.sum(dgr, axis=0, keepdims=True)

    in_specs, args = [], []
    for a, b in pairs:
        in_specs += [pl.BlockSpec((tm, tk), lambda i, kk: (i, kk)), pl.BlockSpec((tk, n), lambda i, kk: (kk, 0))]
        args += [a, b]
    if extra is not None:
        k2 = extra[0].shape[1]
        in_specs += [pl.BlockSpec((tm, k2), lambda i, kk: (i, 0)), pl.BlockSpec((k2, n), lambda i, kk: (0, 0))]
        args += list(extra)
    rspec = pl.BlockSpec((tm, n), lambda i, kk: (i, 0))
    in_specs += [rspec, pl.BlockSpec((1, n), lambda i, kk: (0, 0)), rspec]
    call = dict(grid=(m // tm, nk), in_specs=in_specs,
                out_specs=[rspec, rspec, pl.BlockSpec((1, n), lambda i, kk: (0, 0))],
                out_shape=[jax.ShapeDtypeStruct((m, n), F32), jax.ShapeDtypeStruct((m, n), BF16),
                           jax.ShapeDtypeStruct((1, n), F32)],
                scratch_shapes=[pltpu.VMEM((tm, n), F32)], name=name)
    if scatter is None:
        return pl.pallas_call(body, compiler_params=_params("arbitrary", "arbitrary"), **call)(*args, h, gain, dres)
    res = _call_with_exchange(body, args=(*args, h, gain, dres), srcs=scatter, scatter=[True] * len(scatter), **call)
    return [*res[:3], list(res[3:])]


HALO = 8


def _stage(buf_ref, before, cur):
    buf_ref[0:HALO, :] = before
    buf_ref[HALO:HALO + cur.shape[0], :] = cur


def _stage_after(buf_ref, cur, after):
    r = cur.shape[0]
    buf_ref[0:r, :] = cur
    buf_ref[r:r + HALO, :] = after


def _conv_taps(buf_ref, rows, w_ref):
    nk = w_ref.shape[0]
    out = buf_ref[HALO:HALO + rows, :] * w_ref[nk - 1:nk, :]
    for s in range(1, nk):
        out += buf_ref[HALO - s:HALO - s + rows, :] * w_ref[nk - 1 - s:nk - s, :]
    return out


def _gdn_pre(proj, pab, cq, ck, cv, a_log, dt_bias, *, heads, dk, dv, col_q, col_k, col_v, row_lo, row_hi, name):
    t = proj.shape[0]
    tm = _pick(t, (384, 256, 128))
    nb = t // tm

    def body(pq_ref, pqp_ref, pk_ref, pkp_ref, pv_ref, pvp_ref, ab_ref, cq_ref, ck_ref, cv_ref, al_ref, dt_ref,
             qn_ref, kn_ref, v_ref, g_ref, b_ref, bq_ref, bk_ref, bv_ref):
        h, i = pl.program_id(0), pl.program_id(1)
        first = i == 0
        row = i * tm + _iota((tm, 1), 0)
        valid = (row >= row_lo) & (row < row_hi)
        _stage(bq_ref, jnp.where(first, 0.0, pqp_ref[...]), pq_ref[...])
        _stage(bk_ref, jnp.where(first, 0.0, pkp_ref[...]), pk_ref[...])
        _stage(bv_ref, jnp.where(first, 0.0, pvp_ref[...]), pv_ref[...])
        q1 = _silu(_conv_taps(bq_ref, tm, cq_ref))
        k1 = _silu(_conv_taps(bk_ref, tm, ck_ref))
        v1 = _silu(_conv_taps(bv_ref, tm, cv_ref))
        qn_ref[...] = jnp.where(valid, q1 * lax.rsqrt(jnp.sum(q1 * q1, axis=-1, keepdims=True) + L2_EPS), 0.0)
        kn_ref[...] = jnp.where(valid, k1 * lax.rsqrt(jnp.sum(k1 * k1, axis=-1, keepdims=True) + L2_EPS), 0.0)
        v_ref[...] = jnp.where(valid, v1, 0.0)
        ab = ab_ref[...]
        da = _lane_pick(ab, h)
        db = _lane_pick(ab, heads + h)
        a = _lane_pick(al_ref[...], h)
        dtb = _lane_pick(dt_ref[...], h)
        g_ref[...] = jnp.where(valid, -jnp.exp(a) * _softplus(da + dtb), 0.0)
        b_ref[...] = jnp.where(valid, _sigmoid(db), 0.0)

    def cur(width, col):
        return pl.BlockSpec((tm, width), lambda h, i: (i, col // width + h))

    def prev(width, col):
        return pl.BlockSpec((8, width), lambda h, i: (jnp.maximum(i * (tm // 8) - 1, 0), col // width + h))

    def out(width):
        return pl.BlockSpec((None, tm, width), lambda h, i: (h, i, 0))

    small = pl.BlockSpec((1, LANES), lambda h, i: (0, 0))
    return pl.pallas_call(
        body, grid=(heads, nb),
        in_specs=[cur(dk, col_q), prev(dk, col_q), cur(dk, col_k), prev(dk, col_k), cur(dv, col_v), prev(dv, col_v),
                  pl.BlockSpec((tm, LANES), lambda h, i: (i, 0)),
                  pl.BlockSpec((cq.shape[0], dk), lambda h, i: (0, h)), pl.BlockSpec((ck.shape[0], dk), lambda h, i: (0, h)),
                  pl.BlockSpec((cv.shape[0], dv), lambda h, i: (0, h)), small, small],
        out_specs=[out(dk), out(dk), out(dv), out(1), out(1)],
        out_shape=[jax.ShapeDtypeStruct((heads, t, dk), F32), jax.ShapeDtypeStruct((heads, t, dk), F32),
                   jax.ShapeDtypeStruct((heads, t, dv), F32), jax.ShapeDtypeStruct((heads, t, 1), F32),
                   jax.ShapeDtypeStruct((heads, t, 1), F32)],
        scratch_shapes=[pltpu.VMEM((HALO + tm, dk), F32), pltpu.VMEM((HALO + tm, dk), F32),
                        pltpu.VMEM((HALO + tm, dv), F32)], name=name,
        compiler_params=_params("parallel", "parallel"))(proj, proj, proj, proj, proj, proj, pab, cq, ck, cv, a_log, dt_bias)


def _chunk_masks(rows=GDN_ROWS, row0=0):
    ri = row0 + _iota((rows, GDN_ROWS), 0)
    ci = _iota((rows, GDN_ROWS), 1)
    same = jnp.right_shift(ri, CHUNK_SHIFT) == jnp.right_shift(ci, CHUNK_SHIFT)
    return same, same & (ri >= ci), same & (ri > ci), ri == ci


def _col_to_row(col, eye):
    return jnp.sum(jnp.where(eye, col, 0.0), axis=0, keepdims=True)


def _row_to_col(row, eye):
    return jnp.sum(jnp.where(eye, row, 0.0), axis=1, keepdims=True)


def _chunk_common(blocks, dk_scale):
    same, incl, strict, eye = _chunk_masks()
    tri = jnp.where(incl, 1.0, 0.0).astype(BF16)
    tot = jnp.where(same, 1.0, 0.0).astype(BF16)
    gbs = [jnp.broadcast_to(g, (GDN_ROWS, LANES)) for _, _, g, _ in blocks]
    gams = [jnp.max(_dot_exact_l(tri, gb), axis=1, keepdims=True) for gb in gbs]
    lasts = [jnp.max(_dot_exact_l(tot, gb), axis=1, keepdims=True) for gb in gbs]
    kbs = [kn * beta for _, kn, _, beta in blocks]
    qts = [qn * dk_scale for qn, _, _, _ in blocks]
    boths = [_dot_nt(_bf(jnp.concatenate([kb, qt], axis=0)), _bf(blk[1]))
             for kb, qt, blk in zip(kbs, qts, blocks)]
    out = []
    for gam, last, kb, qt, both in zip(gams, lasts, kbs, qts, boths):
        diff = gam - _col_to_row(gam, eye)
        decay = jnp.where(incl, jnp.exp(jnp.where(incl, diff, 0.0)), 0.0)
        out.append(dict(incl=incl, strict=strict, eye=eye, decay=decay, eg=jnp.exp(gam), ek=jnp.exp(last - gam),
                        egl=jnp.exp(last), kb=kb, qt=qt, lmat=jnp.where(strict, both[:GDN_ROWS] * decay, 0.0),
                        pmat=jnp.where(incl, both[GDN_ROWS:] * decay, 0.0)))
    return out


def _gdn_prep(qn, kn, v, g, beta, *, name):
    heads, t, dk = qn.shape
    dv = v.shape[2]
    rows = _pick(t, (3 * GDN_ROWS, 2 * GDN_ROWS, GDN_ROWS))
    dk_scale = dk ** -0.5

    hp = PREP_HEADS

    def body(q_ref, k_ref, v_ref, g_ref, b_ref, u_ref, w_ref, p_ref, qd_ref, kd_ref, egl_ref, t_ref):
        rs = [(hh, pl.ds(b * GDN_ROWS, GDN_ROWS), slice(None)) for hh in range(hp) for b in range(rows // GDN_ROWS)]
        cs = _chunk_common([(q_ref[r], k_ref[r], g_ref[r], b_ref[r]) for r in rs], dk_scale)
        eye_f = jnp.where(cs[0]["eye"], 1.0, 0.0)
        tinvs = [eye_f - c["lmat"] for c in cs]
        ys = [_dot_hp(c["lmat"], c["lmat"]) for c in cs]
        for _ in range(CHUNK_SHIFT - 1):
            boths = [_dot_hp(y, jnp.concatenate([y, tinv], axis=1)) for y, tinv in zip(ys, tinvs)]
            ys = [both[:, :GDN_ROWS] for both in boths]
            tinvs = [tinv + both[:, GDN_ROWS:] for tinv, both in zip(tinvs, boths)]
        uws = [_dot_hp(tinv, jnp.concatenate([v_ref[r] * b_ref[r], c["kb"] * c["eg"]], axis=1))
               for r, c, tinv in zip(rs, cs, tinvs)]
        for r, c, tinv, uw in zip(rs, cs, tinvs, uws):
            u_ref[r] = uw[:, :dv]
            w_ref[r] = _bf(uw[:, dv:])
            p_ref[r] = _bf(c["pmat"])
            qd_ref[r] = _bf(c["qt"] * c["eg"])
            kd_ref[r] = _bf(k_ref[r] * c["ek"])
            egl_ref[r] = c["egl"]
            t_ref[r] = tinv

    def blk(width):
        return pl.BlockSpec((hp, rows, width), lambda h, i: (h, i, 0))

    def shp(width, dtype=F32):
        return jax.ShapeDtypeStruct((heads, t, width), dtype)

    return pl.pallas_call(
        body, grid=(heads // hp, t // rows), in_specs=[blk(dk), blk(dk), blk(dv), blk(1), blk(1)],
        out_specs=[blk(dv), blk(dk), blk(GDN_ROWS), blk(dk), blk(dk), blk(1), blk(GDN_ROWS)],
        out_shape=[shp(dv), shp(dk, BF16), shp(GDN_ROWS, BF16), shp(dk, BF16), shp(dk, BF16), shp(1), shp(GDN_ROWS)],
        name=name,
        compiler_params=_params("parallel", "parallel"))(qn, kn, v, g, beta)


def _gdn_scan(u, w, p, qd, kd, egl, proj, gain, *, col_z, name):
    heads, t, dv = u.shape
    dk = w.shape[2]
    nb = t // GDN_ROWS
    sub = GDN_ROWS // CHUNK
    hp = SCAN_HEADS

    def body(u_ref, w_ref, p_ref, qd_ref, kd_ref, egl_ref, z_ref, gn_ref, o_ref, og_ref, st_ref, s_ref):
        @pl.when(pl.program_id(1) == 0)
        def _():
            s_ref[...] = jnp.zeros_like(s_ref)

        hs = range(hp)
        vn_parts = [[jnp.zeros((CHUNK, dv), F32)] * sub for _ in hs]
        for c in range(sub):
            r = pl.ds(c * CHUNK, CHUNK)
            ss = [s_ref[hh] for hh in hs]
            sbs = [_bf(s) for s in ss]
            wss = [_dot(_bf(jnp.concatenate([w_ref[hh, r, :], qd_ref[hh, r, :]], axis=0)), sbs[hh])
                   for hh in hs]
            vns = [u_ref[hh, r, :] - wss[hh][:CHUNK] for hh in hs]
            for hh in hs:
                vn_parts[hh][c] = vns[hh]
            os_ = [wss[hh][CHUNK:] + _dot(_bf(p_ref[hh, r, :]), _bf(jnp.concatenate(vn_parts[hh], axis=0))) for hh in hs]
            new = [ss[hh] * egl_ref[hh, pl.ds(c * CHUNK, 1), :] + _dot_tn(_bf(kd_ref[hh, r, :]), _bf(vns[hh])) for hh in hs]
            for hh in hs:
                cols = pl.ds(hh * dv, dv)
                st_ref[hh, c] = sbs[hh]
                s_ref[hh] = new[hh]
                o_ref[hh, r, :] = os_[hh]
                og_ref[r, cols] = _bf(_rms_fwd(os_[hh], gn_ref[...]) * _silu(z_ref[r, cols]))

    def blk(width):
        return pl.BlockSpec((hp, GDN_ROWS, width), lambda h, i: (h, i, 0))

    return pl.pallas_call(
        body, grid=(heads // hp, nb),
        in_specs=[blk(dv), blk(dk), blk(GDN_ROWS), blk(dk), blk(dk), blk(1),
                  pl.BlockSpec((GDN_ROWS, hp * dv), lambda h, i: (i, col_z // (hp * dv) + h)),
                  pl.BlockSpec((1, dv), lambda h, i: (0, 0))],
        out_specs=[blk(dv), pl.BlockSpec((GDN_ROWS, hp * dv), lambda h, i: (i, h)),
                   pl.BlockSpec((hp, sub, dk, dv), lambda h, i: (h, i, 0, 0))],
        out_shape=[jax.ShapeDtypeStruct((heads, t, dv), F32), jax.ShapeDtypeStruct((t, heads * dv), BF16),
                   jax.ShapeDtypeStruct((heads, t // CHUNK, dk, dv), BF16)],
        scratch_shapes=[pltpu.VMEM((hp, dk, dv), F32)], name=name,
        compiler_params=_params("parallel", "arbitrary"))(u, w, p, qd, kd, egl, proj, gain)


def _gdn_post_bwd(o, proj, gain, dout, *, col_z, name):
    heads, t, dv = o.shape
    tm = _pick(t, (384, 256, 128))

    def body(o_ref, z_ref, gn_ref, d_ref, do_ref, dz_ref, dg_ref):
        @pl.when((pl.program_id(0) == 0) & (pl.program_id(1) == 0))
        def _():
            dg_ref[...] = jnp.zeros_like(dg_ref)

        o_, z, d = o_ref[...], z_ref[...], d_ref[...]
        y = _rms_fwd(o_, gn_ref[...])
        dz_ref[...] = _bf(d * y * _silu_grad(z))
        do, dgr = _rms_bwd(o_, gn_ref[...], d * _silu(z))
        do_ref[...] = do
        dg_ref[...] += jnp.sum(dgr, axis=0, keepdims=True)

    return pl.pallas_call(
        body, grid=(t // tm, heads),
        in_specs=[pl.BlockSpec((None, tm, dv), lambda i, h: (h, i, 0)),
                  pl.BlockSpec((tm, dv), lambda i, h: (i, col_z // dv + h)),
                  pl.BlockSpec((1, dv), lambda i, h: (0, 0)), pl.BlockSpec((tm, dv), lambda i, h: (i, h))],
        out_specs=[pl.BlockSpec((None, tm, dv), lambda i, h: (h, i, 0)), pl.BlockSpec((tm, dv), lambda i, h: (i, h)),
                   pl.BlockSpec((1, dv), lambda i, h: (0, 0))],
        out_shape=[jax.ShapeDtypeStruct((heads, t, dv), F32), jax.ShapeDtypeStruct((t, heads * dv), BF16),
                   jax.ShapeDtypeStruct((1, dv), F32)], name=name,
        compiler_params=_params("arbitrary", "arbitrary"))(o, proj, gain, dout)


def _gdn_bwd_scan(u, w, p, qd, kd, egl, st, do, *, name):
    heads, t, dv = u.shape
    dk = w.shape[2]
    nb = t // GDN_ROWS
    sub = GDN_ROWS // CHUNK
    hp = SCAN_HEADS

    def body(u_ref, w_ref, p_ref, qd_ref, kd_ref, egl_ref, st_ref, do_ref,
             du_ref, dw_ref, dp_ref, dqd_ref, dkd_ref, dgl_ref, ds_ref):
        @pl.when(pl.program_id(1) == 0)
        def _():
            ds_ref[...] = jnp.zeros_like(ds_ref)

        hs = range(hp)
        zeros = jnp.zeros((CHUNK, dv), BF16)
        for c in reversed(range(sub)):
            r = pl.ds(c * CHUNK, CHUNK)
            sbs = [st_ref[hh, c] for hh in hs]
            dss = [ds_ref[hh] for hh in hs]
            dsbs = [_bf(ds) for ds in dss]
            dobs = [_bf(do_ref[hh, r, :]) for hh in hs]
            wbs = [_bf(w_ref[hh, r, :]) for hh in hs]
            vns = [u_ref[hh, r, :] - _dot(wbs[hh], sbs[hh]) for hh in hs]
            dvns = [_dot_tn(_bf(p_ref[hh, r, :]), dobs[hh])[c * CHUNK:(c + 1) * CHUNK, :]
                    + _dot(_bf(kd_ref[hh, r, :]), dsbs[hh]) for hh in hs]
            dods = [jnp.concatenate([dobs[hh], _bf(dvns[hh])], axis=0) for hh in hs]
            boths = [_dot_nt(dods[hh], sbs[hh]) for hh in hs]
            dps = [_dot_nt(dobs[hh], jnp.concatenate([_bf(vns[hh]) if cc == c else zeros for cc in range(sub)], axis=0))
                   for hh in hs]
            dkds = [_dot_nt(_bf(vns[hh]), dsbs[hh]) for hh in hs]
            new = [dss[hh] * egl_ref[hh, pl.ds(c * CHUNK, 1), :]
                   + _dot_tn(jnp.concatenate([_bf(qd_ref[hh, r, :]), -wbs[hh]], axis=0), dods[hh])
                   for hh in hs]
            for hh in hs:
                du_ref[hh, r, :] = dvns[hh]
                dw_ref[hh, r, :] = -boths[hh][CHUNK:]
                dp_ref[hh, r, :] = jnp.where(_chunk_masks(CHUNK, c * CHUNK)[1], dps[hh], 0.0)
                dqd_ref[hh, r, :] = boths[hh][:CHUNK]
                dkd_ref[hh, r, :] = dkds[hh]
                dgl = jnp.sum(jnp.sum(dss[hh] * sbs[hh].astype(F32), axis=1, keepdims=True), axis=0, keepdims=True)
                dgl_ref[hh, r, :] = jnp.where(_iota((CHUNK, 1), 0) == CHUNK - 1, dgl, 0.0)
                ds_ref[hh] = new[hh]

    def blk(width):
        return pl.BlockSpec((hp, GDN_ROWS, width), lambda h, i: (h, nb - 1 - i, 0))

    def shp(width):
        return jax.ShapeDtypeStruct((heads, t, width), F32)

    return pl.pallas_call(
        body, grid=(heads // hp, nb),
        in_specs=[blk(dv), blk(dk), blk(GDN_ROWS), blk(dk), blk(dk), blk(1),
                  pl.BlockSpec((hp, sub, dk, dv), lambda h, i: (h, nb - 1 - i, 0, 0)), blk(dv)],
        out_specs=[blk(dv), blk(dk), blk(GDN_ROWS), blk(dk), blk(dk), blk(1)],
        out_shape=[shp(dv), shp(dk), shp(GDN_ROWS), shp(dk), shp(dk), shp(1)],
        scratch_shapes=[pltpu.VMEM((hp, dk, dv), F32)], name=name,
        compiler_params=_params("parallel", "arbitrary"))(u, w, p, qd, kd, egl, st, do)


def _gdn_bwd_prep(qn, kn, v, g, beta, tinv, u, w, du, dw, dp, dqd, dkd, dgl, *, name):
    heads, t, dk = qn.shape
    dv = v.shape[2]
    rows = _pick(t, (3 * GDN_ROWS, 2 * GDN_ROWS, GDN_ROWS))
    dk_scale = dk ** -0.5
    hp = PREP_HEADS

    def rowsum(x):
        return jnp.sum(x, axis=1, keepdims=True)

    def body(q_ref, k_ref, v_ref, g_ref, b_ref, t_ref, u_ref, w_ref, du_ref, dw_ref, dp_ref, dqd_ref, dkd_ref, dgl_ref,
             dq_ref, dkk_ref, dvv_ref, dg_ref, db_ref):
        rs = [(hh, pl.ds(b * GDN_ROWS, GDN_ROWS), slice(None)) for hh in range(hp) for b in range(rows // GDN_ROWS)]
        cs = _chunk_common([(q_ref[r], k_ref[r], g_ref[r], b_ref[r]) for r in rs], dk_scale)
        dbvws = [_dot_hp(t_ref[r], jnp.concatenate([du_ref[r], dw_ref[r]], axis=1), _dot_tn)
                 for r in rs]
        das = [-_dot_nt(_bf(dbvw), jnp.concatenate([_bf(u_ref[r]), w_ref[r]], axis=1))
               for r, dbvw in zip(rs, dbvws)]
        dls = [jnp.where(c["strict"], da, 0.0) for c, da in zip(cs, das)]
        dmns = [_bf(jnp.concatenate([dl * c["decay"], dp_ref[r] * c["decay"]], axis=0))
                for r, c, dl in zip(rs, cs, dls)]
        boths = [_dot(dmn, _bf(k_ref[r])) for r, dmn in zip(rs, dmns)]
        dkns = [_dot_tn(dmn, _bf(jnp.concatenate([c["kb"], c["qt"]], axis=0)))
                for c, dmn in zip(cs, dmns)]
        for r, c, dbvw, dl, both, dkn in zip(rs, cs, dbvws, dls, boths, dkns):
            kn_, beta_, v_ = k_ref[r], b_ref[r], v_ref[r]
            eye = c["eye"]
            kb, qt, eg, ek = c["kb"], c["qt"], c["eg"], c["ek"]
            dbv, dbw = dbvw[:, :dv], dbvw[:, dv:]
            dp_ = dp_ref[r]
            dkb = both[:GDN_ROWS] + dbw * eg
            dqt = both[GDN_ROWS:]
            gmat = dl * c["lmat"] + dp_ * c["pmat"]
            dqd_, dkd_ = dqd_ref[r], dkd_ref[r]
            qd = qt * eg
            kd = kn_ * ek
            bw = kb * eg
            kdsum = rowsum(dkd_ * kd)
            dgam = rowsum(gmat) - _row_to_col(jnp.sum(gmat, axis=0, keepdims=True), eye)
            dgam += rowsum(dbw * bw) + rowsum(dqd_ * qd) - kdsum
            last = (_iota((GDN_ROWS, 1), 0) & (CHUNK - 1)) == CHUNK - 1
            same = _chunk_masks()[0]
            same_f = jnp.where(same, 1.0, 0.0).astype(BF16)
            chunk_tot = jnp.max(_dot_exact_l(same_f, jnp.broadcast_to(kdsum, (GDN_ROWS, LANES))), axis=1, keepdims=True)
            dgam += jnp.where(last, chunk_tot, 0.0) + dgl_ref[r] * c["egl"]
            dq_ref[r] = (dqt + dqd_ * eg) * dk_scale
            dkk_ref[r] = dkn + dkd_ * ek + dkb * beta_
            dvv_ref[r] = dbv * beta_
            db_ref[r] = rowsum(dbv * v_) + rowsum(dkb * kn_)
            upper = jnp.where(same & (_iota((GDN_ROWS, GDN_ROWS), 0) <= _iota((GDN_ROWS, GDN_ROWS), 1)), 1.0, 0.0)
            dgb = _dot_exact_l(upper.astype(BF16), jnp.broadcast_to(dgam, (GDN_ROWS, LANES)))
            dg_ref[r] = _lane_pick(dgb, 0)

    def blk(width):
        return pl.BlockSpec((hp, rows, width), lambda h, i: (h, i, 0))

    def shp(width):
        return jax.ShapeDtypeStruct((heads, t, width), F32)

    return pl.pallas_call(
        body, grid=(heads // hp, t // rows),
        in_specs=[blk(dk), blk(dk), blk(dv), blk(1), blk(1), blk(GDN_ROWS), blk(dv), blk(dk),
                  blk(dv), blk(dk), blk(GDN_ROWS), blk(dk), blk(dk), blk(1)],
        out_specs=[blk(dk), blk(dk), blk(dv), blk(1), blk(1)],
        out_shape=[shp(dk), shp(dk), shp(dv), shp(1), shp(1)], name=name,
        compiler_params=_params("parallel", "parallel"))(qn, kn, v, g, beta, tinv, u, w, du, dw, dp, dqd, dkd, dgl)


def _gdn_pre_bwd_a(proj, pab, cq, ck, cv, a_log, dt_bias, dqn, dkn, dvv, dg, dbeta, *,
                   heads, dk, dv, col_q, col_k, col_v, row_lo, row_hi, name):
    t = proj.shape[0]
    tm = _pick(t, (384, 256, 128))
    nb = t // tm

    def body(pq_ref, pqp_ref, pk_ref, pkp_ref, pv_ref, pvp_ref, ab_ref, cq_ref, ck_ref, cv_ref, al_ref, dt_ref,
             dqn_ref, dkn_ref, dvv_ref, dg_ref, db_ref, dcq_ref, dck_ref, dcv_ref, dab_ref, dal_ref, ddt_ref,
             bq_ref, bk_ref, bv_ref):
        i, h = pl.program_id(0), pl.program_id(1)
        first = i == 0

        @pl.when((i == 0) & (h == 0))
        def _():
            dal_ref[...] = jnp.zeros_like(dal_ref)
            ddt_ref[...] = jnp.zeros_like(ddt_ref)

        @pl.when(h == 0)
        def _():
            dab_ref[...] = jnp.zeros_like(dab_ref)

        row = i * tm + _iota((tm, 1), 0)
        valid = (row >= row_lo) & (row < row_hi)

        def l2_bwd(c1, dn):
            x1 = _silu(c1)
            r = lax.rsqrt(jnp.sum(x1 * x1, axis=-1, keepdims=True) + L2_EPS)
            dn = jnp.where(valid, dn, 0.0)
            d1 = r * dn - x1 * (r * r * r) * jnp.sum(dn * x1, axis=-1, keepdims=True)
            return d1 * _silu_grad(c1)

        _stage(bq_ref, jnp.where(first, 0.0, pqp_ref[...]), pq_ref[...])
        _stage(bk_ref, jnp.where(first, 0.0, pkp_ref[...]), pk_ref[...])
        _stage(bv_ref, jnp.where(first, 0.0, pvp_ref[...]), pv_ref[...])
        dcq_ref[...] = l2_bwd(_conv_taps(bq_ref, tm, cq_ref), dqn_ref[...])
        dck_ref[...] = l2_bwd(_conv_taps(bk_ref, tm, ck_ref), dkn_ref[...])
        cv1 = _conv_taps(bv_ref, tm, cv_ref)
        dcv_ref[...] = jnp.where(valid, dvv_ref[...], 0.0) * _silu_grad(cv1)
        ab = ab_ref[...]
        da = _lane_pick(ab, h)
        db = _lane_pick(ab, heads + h)
        a = _lane_pick(al_ref[...], h)
        dtb = _lane_pick(dt_ref[...], h)
        dgv = jnp.where(valid, dg_ref[...], 0.0)
        ea = jnp.exp(a)
        g = -ea * _softplus(da + dtb)
        dda = dgv * (-ea) * _sigmoid(da + dtb)
        beta = _sigmoid(db)
        ddb = jnp.where(valid, db_ref[...], 0.0) * beta * (1.0 - beta)
        lane = _iota((tm, LANES), 1)
        dab_ref[...] += jnp.where(lane == h, dda, 0.0) + jnp.where(lane == heads + h, ddb, 0.0)
        lane1 = _iota((1, LANES), 1)
        dal_ref[...] += jnp.where(lane1 == h, jnp.sum(dgv * g, axis=0, keepdims=True), 0.0)
        ddt_ref[...] += jnp.where(lane1 == h, jnp.sum(dda, axis=0, keepdims=True), 0.0)

    def cur(width, col):
        return pl.BlockSpec((tm, width), lambda i, h: (i, col // width + h))

    def prev(width, col):
        return pl.BlockSpec((8, width), lambda i, h: (jnp.maximum(i * (tm // 8) - 1, 0), col // width + h))

    def hd(width):
        return pl.BlockSpec((None, tm, width), lambda i, h: (h, i, 0))

    small = pl.BlockSpec((1, LANES), lambda i, h: (0, 0))
    return pl.pallas_call(
        body, grid=(nb, heads),
        in_specs=[cur(dk, col_q), prev(dk, col_q), cur(dk, col_k), prev(dk, col_k), cur(dv, col_v), prev(dv, col_v),
                  pl.BlockSpec((tm, LANES), lambda i, h: (i, 0)),
                  pl.BlockSpec((cq.shape[0], dk), lambda i, h: (0, h)), pl.BlockSpec((ck.shape[0], dk), lambda i, h: (0, h)),
                  pl.BlockSpec((cv.shape[0], dv), lambda i, h: (0, h)), small, small,
                  hd(dk), hd(dk), hd(dv), hd(1), hd(1)],
        out_specs=[hd(dk), hd(dk), hd(dv), pl.BlockSpec((tm, LANES), lambda i, h: (i, 0)), small, small],
        out_shape=[jax.ShapeDtypeStruct((heads, t, dk), F32), jax.ShapeDtypeStruct((heads, t, dk), F32),
                   jax.ShapeDtypeStruct((heads, t, dv), F32), jax.ShapeDtypeStruct((t, LANES), F32),
                   jax.ShapeDtypeStruct((1, LANES), F32), jax.ShapeDtypeStruct((1, LANES), F32)],
        scratch_shapes=[pltpu.VMEM((HALO + tm, dk), F32), pltpu.VMEM((HALO + tm, dk), F32),
                        pltpu.VMEM((HALO + tm, dv), F32)], name=name,
        compiler_params=_params("arbitrary", "arbitrary"))(
            proj, proj, proj, proj, proj, proj, pab, cq, ck, cv, a_log, dt_bias, dqn, dkn, dvv, dg, dbeta)


def _conv_bwd(proj, dc, cw, *, heads, width, col, name):
    t = proj.shape[0]
    tm = _pick(t, (384, 256, 128))
    nb = t // tm
    nk = cw.shape[0]

    def body(p_ref, pp_ref, d_ref, dn_ref, w_ref, dp_ref, dw_ref, bx_ref, bd_ref):
        i = pl.program_id(1)
        first, last = i == 0, i == nb - 1

        @pl.when(first)
        def _():
            dw_ref[...] = jnp.zeros_like(dw_ref)

        d = d_ref[...]
        _stage(bx_ref, jnp.where(first, 0.0, pp_ref[...]), p_ref[...])
        _stage_after(bd_ref, d, jnp.where(last, 0.0, dn_ref[...]))
        dx = d * w_ref[nk - 1:nk, :]
        dw_ref[nk - 1:nk, :] += jnp.sum(d * p_ref[...], axis=0, keepdims=True)
        for s in range(1, nk):
            dx += bd_ref[s:s + tm, :] * w_ref[nk - 1 - s:nk - s, :]
            dw_ref[nk - 1 - s:nk - s, :] += jnp.sum(d * bx_ref[HALO - s:HALO - s + tm, :], axis=0, keepdims=True)
        dp_ref[...] = _bf(dx)

    return pl.pallas_call(
        body, grid=(heads, nb),
        in_specs=[pl.BlockSpec((tm, width), lambda h, i: (i, col // width + h)),
                  pl.BlockSpec((8, width), lambda h, i: (jnp.maximum(i * (tm // 8) - 1, 0), col // width + h)),
                  pl.BlockSpec((None, tm, width), lambda h, i: (h, i, 0)),
                  pl.BlockSpec((None, 8, width), lambda h, i: (h, jnp.minimum((i + 1) * (tm // 8), t // 8 - 1), 0)),
                  pl.BlockSpec((nk, width), lambda h, i: (0, h))],
        out_specs=[pl.BlockSpec((tm, width), lambda h, i: (i, h)), pl.BlockSpec((nk, width), lambda h, i: (0, h))],
        out_shape=[jax.ShapeDtypeStruct((t, heads * width), BF16), jax.ShapeDtypeStruct((nk, heads * width), F32)],
        scratch_shapes=[pltpu.VMEM((HALO + tm, width), F32), pltpu.VMEM((tm + HALO, width), F32)],
        name=name, compiler_params=_params("parallel", "arbitrary"))(proj, proj, dc, dc, cw)


def _sb_pre(proj, gq, gk, *, heads, dh, col_q, col_k, col_v, name):
    t = proj.shape[0]
    tm = _pick(t, (384, 256, 128))

    def body(q_ref, k_ref, v_ref, gq_ref, gk_ref, qo_ref, ko_ref, vo_ref):
        qo_ref[...] = _bf(_rms_fwd(q_ref[...], gq_ref[...]))
        ko_ref[...] = _bf(_rms_fwd(k_ref[...], gk_ref[...]))
        vo_ref[...] = _bf(v_ref[...])

    def cur(col):
        return pl.BlockSpec((tm, dh), lambda i, h: (i, col // dh + h))

    gspec = pl.BlockSpec((1, dh), lambda i, h: (0, 0))
    ospec = pl.BlockSpec((tm, dh), lambda i, h: (i, h))
    return pl.pallas_call(
        body, grid=(t // tm, heads), in_specs=[cur(col_q), cur(col_k), cur(col_v), gspec, gspec],
        out_specs=[ospec] * 3, out_shape=[jax.ShapeDtypeStruct((t, heads * dh), BF16)] * 3, name=name,
        compiler_params=_params("parallel", "parallel"))(proj, proj, proj, gq, gk)


def _sb_tile(z, i, j, blk, key_lo, masked):
    ls = jnp.minimum(z, 0.0) - jnp.log(1.0 + jnp.exp(-jnp.abs(z)))
    if not masked:
        return None, ls, ls - z
    qpos = i * blk + _iota((blk, blk), 0)
    kpos = j * blk + _iota((blk, blk), 1)
    vis = (kpos < qpos) & (kpos >= key_lo)
    return vis, ls, jnp.where(vis, ls - z, 0.0)


def _where_vis(vis, x):
    return x if vis is None else jnp.where(vis, x, 0.0)


def _sb_sweep(i, step, init, descending):
    first, last = (i, 0) if descending else (0, i)
    carry = step(first, init, True)
    carry = lax.fori_loop(1, i, lambda n, c: step(i - n if descending else n, c, False), carry)
    return lax.cond(i > 0, lambda c: step(last, c, True), lambda c: c, carry)


def _dot2_r(x, m):
    hi, lo = _split2(x)
    return _dot(hi, m) + _dot(lo, m)


def _running_sums(x, tri, reverse, exact=True):
    groups = [x[:, s:s + LANES] for s in range(0, x.shape[1], LANES)]
    inside = [_dot2_r(g, tri) if exact else _dot(_bf(g), tri) for g in groups]
    sums = [jnp.sum(g, axis=1, keepdims=True) for g in groups]
    order = list(reversed(range(len(groups)))) if reverse else list(range(len(groups)))
    out, acc = [None] * len(groups), None
    for gi in order:
        out[gi] = inside[gi] if acc is None else inside[gi] + acc
        acc = sums[gi] if acc is None else acc + sums[gi]
    return jnp.concatenate(out, axis=1), acc


def _sb_fwd(qs, ks, vs, *, heads, dh, key_lo, name, gather=None):
    t = qs.shape[0]
    blk = _pick(t, (3 * SB_BLOCK, 2 * SB_BLOCK, SB_BLOCK))
    assert key_lo <= blk
    nq = t // blk
    assert nq <= LANES
    scale = dh ** -0.5
    hp = SB_HEADS_PER_STEP

    def body(q_ref, k_ref, v_ref, o_ref, c_ref):
        i = pl.program_id(1)
        later = jnp.where(_iota((LANES, LANES), 0) > _iota((LANES, LANES), 1), 1.0, 0.0).astype(BF16)
        lane = _iota((blk, LANES), 1)
        c_ref[...] = jnp.zeros_like(c_ref)

        def step(j, carry, masked):
            rows = pl.ds(pl.multiple_of(j * blk, blk), blk)
            hs = range(hp)
            cols = [pl.ds(hh * dh, dh) for hh in hs]
            zs = [_dot_nt(q_ref[:, cols[hh]], k_ref[rows, cols[hh]]) * scale for hh in hs]
            tiles = [_sb_tile(z, i, j, blk, key_lo, masked) for z in zs]
            sufs = [_running_sums(lk, later, reverse=True) for _, _, lk in tiles]
            wgts = [_where_vis(vis, jnp.exp(ls + suf + carry[2 * hh + 1]))
                    for hh, ((vis, ls, _), (suf, _)) in enumerate(zip(tiles, sufs))]
            accs = [carry[2 * hh] + _dot(_bf(wgts[hh]), v_ref[rows, cols[hh]]) for hh in hs]
            out = []
            for hh in hs:
                c_ref[hh] = jnp.where(lane == j, carry[2 * hh + 1], c_ref[hh])
                out += [accs[hh], carry[2 * hh + 1] + sufs[hh][1]]
            return tuple(out)

        res = _sb_sweep(i, step, (jnp.zeros((blk, dh), F32), jnp.zeros((blk, 1), F32)) * hp, descending=True)
        for hh in range(hp):
            o_ref[:, pl.ds(hh * dh, dh)] = _bf(res[2 * hh])

    full = pl.BlockSpec((t, hp * dh), lambda h, i: (0, h))
    call = dict(grid=(heads // hp, nq),
                in_specs=[pl.BlockSpec((blk, hp * dh), lambda h, i: (i, h)), full, full],
                out_specs=[pl.BlockSpec((blk, hp * dh), lambda h, i: (i, h)),
                           pl.BlockSpec((hp, blk, LANES), lambda h, i: (h, i, 0))],
                out_shape=[jax.ShapeDtypeStruct((t, heads * dh), BF16), jax.ShapeDtypeStruct((heads, t, LANES), F32)],
                name=name)
    if gather is None:
        return pl.pallas_call(body, compiler_params=_params("parallel", "parallel"), **call)(qs, ks, vs)
    res = _call_with_exchange(body, scratch_shapes=[], args=(qs, ks, vs), srcs=gather, scatter=[False] * len(gather),
                              **call)
    return [*res[:2], list(res[2:])]


def _sb_bwd(qs, ks, vs, do, carry, *, heads, dh, key_lo, name, scatter=None):
    t = qs.shape[0]
    blk = _pick(t, (3 * SB_BLOCK, 2 * SB_BLOCK, SB_BLOCK))
    assert key_lo <= blk
    nq = t // blk
    scale = dh ** -0.5
    hp = SB_HEADS_PER_STEP

    def body(q_ref, k_ref, v_ref, do_ref, c_ref, dq_ref, dk_ref, dv_ref):
        i = pl.program_id(1)

        @pl.when(i == 0)
        def _():
            dk_ref[...] = jnp.zeros_like(dk_ref)
            dv_ref[...] = jnp.zeros_like(dv_ref)

        r0 = _iota((LANES, LANES), 0)
        r1 = _iota((LANES, LANES), 1)
        later = jnp.where(r0 > r1, 1.0, 0.0).astype(BF16)
        earlier = jnp.where(r0 < r1, 1.0, 0.0).astype(BF16)

        def step(j, carry, masked):
            rows = pl.ds(pl.multiple_of(j * blk, blk), blk)
            hs = range(hp)
            cols = [pl.ds(hh * dh, dh) for hh in hs]
            zs = [_dot_nt(q_ref[:, cols[hh]], k_ref[rows, cols[hh]]) * scale for hh in hs]
            dws = [_dot_nt(do_ref[:, cols[hh]], v_ref[rows, cols[hh]]) for hh in hs]
            tiles = [_sb_tile(z, i, j, blk, key_lo, masked) for z in zs]
            sufs = [_running_sums(lk, later, reverse=True)[0] for _, _, lk in tiles]
            wgts = [_where_vis(vis, jnp.exp(ls + suf + _lane_pick(c_ref[hh], j)))
                    for hh, ((vis, ls, _), suf) in enumerate(zip(tiles, sufs))]
            es = [wgt * dw for wgt, dw in zip(wgts, dws)]
            pres = [_running_sums(e, earlier, reverse=False, exact=False) for e in es]
            dzs = []
            for hh in hs:
                vis, ls, _ = tiles[hh]
                before = _where_vis(vis, pres[hh][0] + carry[2 * hh + 1])
                sig = jnp.exp(ls)
                dzs.append(_bf((es[hh] * (1.0 - sig) - before * sig) * scale))
            dks = [_dot_tn(dzs[hh], q_ref[:, cols[hh]]) for hh in hs]
            dvs = [_dot_tn(_bf(wgts[hh]), do_ref[:, cols[hh]]) for hh in hs]
            dqs = [carry[2 * hh] + _dot(dzs[hh], k_ref[rows, cols[hh]]) for hh in hs]
            out = []
            for hh in hs:
                dk_ref[rows, cols[hh]] += dks[hh]
                dv_ref[rows, cols[hh]] += dvs[hh]
                out += [dqs[hh], carry[2 * hh + 1] + pres[hh][1]]
            return tuple(out)

        res = _sb_sweep(i, step, (jnp.zeros((blk, dh), F32), jnp.zeros((blk, 1), F32)) * hp, descending=False)
        for hh in range(hp):
            dq_ref[:, pl.ds(hh * dh, dh)] = res[2 * hh]

    full = pl.BlockSpec((t, hp * dh), lambda h, i: (0, h))
    qblk = pl.BlockSpec((blk, hp * dh), lambda h, i: (i, h))
    call = dict(grid=(heads // hp, nq),
                in_specs=[qblk, full, full, qblk, pl.BlockSpec((hp, blk, LANES), lambda h, i: (h, i, 0))],
                out_specs=[qblk, full, full], out_shape=[jax.ShapeDtypeStruct((t, heads * dh), F32)] * 3, name=name)
    if scatter is None:
        return pl.pallas_call(body, compiler_params=_params("parallel", "arbitrary"), **call)(qs, ks, vs, do, carry)
    res = _call_with_exchange(body, scratch_shapes=[], args=(qs, ks, vs, do, carry), srcs=scatter,
                              scatter=[True] * len(scatter), **call)
    return [*res[:3], list(res[3:])]


def _sb_pre_bwd(proj, gq, gk, dq, dk, dv, *, heads, dh, col_q, col_k, name):
    t = proj.shape[0]
    tm = _pick(t, (384, 256, 128))

    def body(q_ref, k_ref, gq_ref, gk_ref, dq_ref, dk_ref, dv_ref, oq_ref, ok_ref, ov_ref, dgq_ref, dgk_ref):
        @pl.when((pl.program_id(0) == 0) & (pl.program_id(1) == 0))
        def _():
            dgq_ref[...] = jnp.zeros_like(dgq_ref)
            dgk_ref[...] = jnp.zeros_like(dgk_ref)

        dq_, gq_r = _rms_bwd(q_ref[...], gq_ref[...], dq_ref[...])
        dk_, gk_r = _rms_bwd(k_ref[...], gk_ref[...], dk_ref[...])
        oq_ref[...] = _bf(dq_)
        ok_ref[...] = _bf(dk_)
        ov_ref[...] = _bf(dv_ref[...])
        dgq_ref[...] += jnp.sum(gq_r, axis=0, keepdims=True)
        dgk_ref[...] += jnp.sum(gk_r, axis=0, keepdims=True)

    def cur(col):
        return pl.BlockSpec((tm, dh), lambda i, h: (i, col // dh + h))

    gspec = pl.BlockSpec((1, dh), lambda i, h: (0, 0))
    ospec = pl.BlockSpec((tm, dh), lambda i, h: (i, h))
    return pl.pallas_call(
        body, grid=(t // tm, heads), in_specs=[cur(col_q), cur(col_k), gspec, gspec, ospec, ospec, ospec],
        out_specs=[ospec, ospec, ospec, gspec, gspec],
        out_shape=[jax.ShapeDtypeStruct((t, heads * dh), BF16)] * 3 + [jax.ShapeDtypeStruct((1, dh), F32)] * 2,
        name=name, compiler_params=_params("arbitrary", "arbitrary"))(proj, proj, gq, gk, dq, dk, dv)


PEERS = N_DEV - 1


def _exchange_copies(ins, outs, send_sems, recv_sems, local_sems, scatter):
    x, y, c = lax.axis_index("x"), lax.axis_index("y"), lax.axis_index("c")
    me = 4 * x + 2 * y + c
    copies = []
    for a in range(len(ins)):
        own = ins[a].at[me] if scatter[a] else ins[a]
        copies.append(pltpu.make_async_copy(own, outs[a].at[me], local_sems.at[a]))
        for k in range(1, N_DEV):
            px = (x + (k >> 2 & 1)) % 2
            py = (y + (k >> 1 & 1)) % 2
            pc = (c + (k & 1)) % 2
            src = ins[a].at[4 * px + 2 * py + pc] if scatter[a] else ins[a]
            copies.append(pltpu.make_async_remote_copy(
                src_ref=src, dst_ref=outs[a].at[me], send_sem=send_sems.at[a * PEERS + k - 1],
                recv_sem=recv_sems.at[a * PEERS + k - 1], device_id=(px, py, pc), device_id_type=MESH))
    return copies


def _exchange_shapes(srcs, scatter):
    return [jax.ShapeDtypeStruct(s.shape if sc else (N_DEV,) + s.shape, s.dtype) for s, sc in zip(srcs, scatter)]


def _exchange_sems(n):
    return [pltpu.SemaphoreType.DMA((n * PEERS,)), pltpu.SemaphoreType.DMA((n * PEERS,)), pltpu.SemaphoreType.DMA((n,))]


def _exchange(srcs, *, scatter, name):
    n = len(srcs)

    def body(*refs):
        copies = _exchange_copies(refs[:n], refs[n:2 * n], *refs[2 * n:], scatter)
        for cp in copies:
            cp.start()
        for cp in copies:
            cp.wait()

    any_spec = pl.BlockSpec(memory_space=pl.ANY)
    return pl.pallas_call(
        body, in_specs=[any_spec] * n, out_specs=[any_spec] * n, out_shape=_exchange_shapes(srcs, scatter),
        scratch_shapes=_exchange_sems(n), name=name,
        compiler_params=pltpu.CompilerParams(has_side_effects=True))(*srcs)


def _gather_two_level(srcs, *, name):
    n = len(srcs)

    def body(*refs):
        ins, outs = refs[:n], refs[n:2 * n]
        send_sems, recv_sems, local_sems = refs[2 * n:]
        x, y, c = lax.axis_index("x"), lax.axis_index("y"), lax.axis_index("c")
        chips = [(1 - x, y), (x, 1 - y), (1 - x, 1 - y)]

        def slab(a, px, py, pc):
            return outs[a].at[4 * px + 2 * py + pc]

        def copy(a, k, block, to, src=None):
            return pltpu.make_async_remote_copy(
                src_ref=slab(a, *block) if src is None else src, dst_ref=slab(a, *block),
                send_sem=send_sems.at[a * PEERS + k], recv_sem=recv_sems.at[a * PEERS + k],
                device_id=to, device_id_type=MESH)

        mine = [pltpu.make_async_copy(ins[a], slab(a, x, y, c), local_sems.at[a]) for a in range(n)]
        first = [copy(a, 0, (x, y, c), (x, y, 1 - c), src=ins[a]) for a in range(n)]
        first += [copy(a, 1 + j, (x, y, c), (*chip, c), src=ins[a]) for j, chip in enumerate(chips) for a in range(n)]
        for cp in mine + first:
            cp.start()
        passed = []
        for j, chip in enumerate(chips):
            for a in range(n):
                copy(a, 1 + j, (*chip, c), (x, y, c)).wait_recv()
                passed.append(copy(a, 4 + j, (*chip, c), (x, y, 1 - c)))
                passed[-1].start()
        for a in range(n):
            copy(a, 0, (x, y, 1 - c), (x, y, c)).wait_recv()
            for j, chip in enumerate(chips):
                copy(a, 4 + j, (*chip, 1 - c), (x, y, c)).wait_recv()
        for cp in first + passed:
            cp.wait_send()
        for cp in mine:
            cp.wait()

    any_spec = pl.BlockSpec(memory_space=pl.ANY)
    return pl.pallas_call(
        body, in_specs=[any_spec] * n, out_specs=[any_spec] * n, out_shape=_exchange_shapes(srcs, [False] * n),
        scratch_shapes=_exchange_sems(n), name=name,
        compiler_params=pltpu.CompilerParams(has_side_effects=True))(*srcs)


def _call_with_exchange(body, *, grid, in_specs, out_specs, out_shape, scratch_shapes, args, srcs, scatter, name):
    n, n_in, n_out, n_scr = len(srcs), len(args), len(out_shape), len(scratch_shapes)

    def full_body(*refs):
        ins, xin = refs[:n_in], refs[n_in:n_in + n]
        outs, xout = refs[n_in + n:n_in + n + n_out], refs[n_in + n + n_out:n_in + 2 * n + n_out]
        scr = refs[n_in + 2 * n + n_out:]
        ids = [pl.program_id(a) for a in range(len(grid))]
        first = functools.reduce(jnp.logical_and, [i == 0 for i in ids])
        last = functools.reduce(jnp.logical_and, [i == g - 1 for i, g in zip(ids, grid)])
        copies = _exchange_copies(xin, xout, *scr[n_scr:], scatter)

        @pl.when(first)
        def _():
            for cp in copies:
                cp.start()

        body(*ins, *outs, *scr[:n_scr])

        @pl.when(last)
        def _():
            for cp in copies:
                cp.wait()

    any_spec = pl.BlockSpec(memory_space=pl.ANY)
    return pl.pallas_call(
        full_body, grid=grid, in_specs=list(in_specs) + [any_spec] * n, out_specs=list(out_specs) + [any_spec] * n,
        out_shape=list(out_shape) + _exchange_shapes(srcs, scatter),
        scratch_shapes=list(scratch_shapes) + _exchange_sems(n), name=name,
        compiler_params=pltpu.CompilerParams(dimension_semantics=("arbitrary",) * len(grid),
                                             vmem_limit_bytes=V7X_VMEM_LIMIT_BYTES, has_side_effects=True))(*args, *srcs)


def _adam_math(g, w, m, v):
    m2 = ADAM_B1 * m + (1.0 - ADAM_B1) * g
    v2 = ADAM_B2 * v + (1.0 - ADAM_B2) * (g * g)
    m_hat = m2 / (1.0 - ADAM_B1 ** ADAM_STEP)
    v_hat = v2 / (1.0 - ADAM_B2 ** ADAM_STEP)
    return -ADAM_LR * (m_hat / (jnp.sqrt(v_hat) + ADAM_EPS) + ADAM_WD * w), m2, v2


def _adamw_slabs(slabs, w, m, v, *, name):
    r, c = w.shape
    tr = next((t for t in (2256, 752, 512, 240, 128, 64, 32, 16) if r % t == 0), r)

    def body(s_ref, w_ref, m_ref, v_ref, g_ref, d_ref, mo_ref, vo_ref):
        g = s_ref[0].astype(F32)
        for p in range(1, N_DEV):
            g = g + s_ref[p].astype(F32)
        g_ref[...] = g
        d_ref[...], mo_ref[...], vo_ref[...] = _adam_math(g, w_ref[...], m_ref[...], v_ref[...])

    spec = pl.BlockSpec((tr, c), lambda i: (i, 0))
    return pl.pallas_call(
        body, grid=(r // tr,), in_specs=[pl.BlockSpec((N_DEV, tr, c), lambda i: (0, i, 0)), spec, spec, spec],
        out_specs=[spec] * 4, out_shape=[jax.ShapeDtypeStruct((r, c), F32)] * 4, name=name,
        compiler_params=_params("parallel"))(slabs, w, m, v)


def _adamw_small(g, w, m, v, *, name):
    def body(g_ref, w_ref, m_ref, v_ref, d_ref, mo_ref, vo_ref):
        d_ref[...], mo_ref[...], vo_ref[...] = _adam_math(g_ref[...], w_ref[...], m_ref[...], v_ref[...])

    return pl.pallas_call(body, out_shape=[jax.ShapeDtypeStruct(w.shape, F32)] * 3, name=name)(g, w, m, v)


def _sum_slabs(slabs, *, name):
    def body(s_ref, o_ref):
        acc = s_ref[0]
        for p in range(1, N_DEV):
            acc = acc + s_ref[p]
        o_ref[...] = acc

    return pl.pallas_call(body, out_shape=jax.ShapeDtypeStruct(slabs.shape[1:], F32), name=name)(slabs)


def _pad_lanes(a):
    return jnp.pad(a, ((0, 0), (0, LANES - a.shape[1])))


def _w_in_slabs(main, ab, col_ab, n_ab):
    pw = main.shape[0] + n_ab
    c = pw // N_DEV
    parts = [(0, col_ab, main, 0), (col_ab, col_ab + n_ab, ab, col_ab), (col_ab + n_ab, pw, main, n_ab)]
    slabs = []
    for p in range(N_DEV):
        pieces = []
        for lo, hi, src, shift in parts:
            a, b = max(lo, c * p), min(hi, c * (p + 1))
            if a < b:
                pieces.append(src[a - shift:b - shift])
        slabs.append(jnp.concatenate(pieces, axis=0))
    return jnp.stack(slabs)


def _local_step(x, target, meta, g_mix, wt_main, wt_ab, cq, ck, cv, a_log, dt_bias, g_dn, g_sbq, g_sbk, g_ffn, rest,
                shards=False):
    seq, d = x.shape
    n_meta = meta.shape[0]
    heads = a_log.shape[1]
    qk = cq.shape[1]
    dvt = cv.shape[1]
    dk, dv = qk // heads, dvt // heads
    dh = g_sbq.shape[1]
    sbw = rest[2].shape[0] * N_DEV if shards else rest[1].shape[0]
    sb_heads = sbw // dh
    pad_l = (-n_meta) % CHUNK
    row_x = pad_l + n_meta
    rows = row_x + seq
    t = -(-rows // GDN_ROWS) * GDN_ROWS
    col_q, col_k, col_v, col_z = 0, qk, 2 * qk, 2 * qk + dvt
    col_sq = 2 * qk + 2 * dvt
    col_sk, col_sv, col_gd, col_gs = col_sq + sbw, col_sq + 2 * sbw, col_sq + 3 * sbw, col_sq + 3 * sbw + d

    def rows_pad(a):
        return jnp.concatenate([jnp.zeros((row_x, d), F32), a, jnp.zeros((t - rows, d), F32)], axis=0)

    h0 = jnp.concatenate([jnp.zeros((pad_l, d), F32), meta, x, jnp.zeros((t - rows, d), F32)], axis=0)
    tgt = rows_pad(target)
    a_log_p, dt_p = _pad_lanes(a_log), _pad_lanes(dt_bias)

    proj, n1 = _mm_norm(h0, g_mix, wt_main, name="proj")
    pab = _mm_nt(n1, wt_ab, out_dtype=F32, name="proj_ab")
    gk = dict(heads=heads, dk=dk, dv=dv, col_q=col_q, col_k=col_k, col_v=col_v, row_lo=pad_l, row_hi=rows)
    qn, kn, vv, g, beta = _gdn_pre(proj, pab, cq, ck, cv, a_log_p, dt_p, name="gdn_pre", **gk)
    u, w, pm, qd, kd, egl, tinv = _gdn_prep(qn, kn, vv, g, beta, name="gdn_prep")
    o_raw, o_dn, states = _gdn_scan(u, w, pm, qd, kd, egl, proj, g_dn, col_z=col_z, name="gdn_scan")
    qs, ks, vs = _sb_pre(proj, g_sbq, g_sbk, heads=sb_heads, dh=dh, col_q=col_sq, col_k=col_sk, col_v=col_sv,
                         name="sb_pre")
    if shards:
        o_sb, carry, (g_fi, g_bd, g_bs, g_out, g_fo) = _sb_fwd(qs, ks, vs, heads=sb_heads, dh=dh, key_lo=pad_l,
                                                                name="sb_fwd", gather=list(rest))
        wt_fi = g_fi.reshape(-1, d)
        d_ff = wt_fi.shape[0] // 2
        w_bd, w_bs, w_out, wt_fg, wt_fu, w_fo = (g_bd.reshape(-1, d), g_bs.reshape(-1, d), g_out.reshape(-1, d),
                                                 wt_fi[:d_ff], wt_fi[d_ff:], g_fo.reshape(-1, d))
    else:
        o_sb, carry = _sb_fwd(qs, ks, vs, heads=sb_heads, dh=dh, key_lo=pad_l, name="sb_fwd")
        w_bd, w_bs, w_out, wt_fg, wt_fu, w_fo = rest
    merged, br_dn, br_sb = _merge_fwd(o_dn, o_sb, w_bd, w_bs, proj, col_gd=col_gd, col_gs=col_gs, name="merge")
    h1 = _mm_res(h0, merged, w_out, name="mix_out")
    gate, up, act, n2 = _mm_norm_swiglu(h1, g_ffn, wt_fg, wt_fu, name="ffn_in")
    dy, dyb, lsum = _mm_res_loss(h1, act, w_fo, tgt, row0=row_x, nrows=seq, name="ffn_out_loss")

    dgate, dup = _swiglu_bwd(dyb, w_fo, gate, up, name="ffn_out_bwd")
    d_w_fo = _mm_tn(act, dyb, name="dw_ffn_out")
    d_wt_fg = _mm_tn(dgate, n2, name="dw_ffn_gate")
    d_wt_fu = _mm_tn(dup, n2, name="dw_ffn_up")
    dh1, dh1b, d_g_ffn = _mm_rmsbwd([(dgate, wt_fg), (dup, wt_fu)], None, h1, g_ffn, dy, name="ffn_in_bwd")

    dbd, dbs, dgd, dgs = _merge_bwd(dh1b, w_out, proj, br_dn, br_sb, col_gd=col_gd, col_gs=col_gs, name="mix_out_bwd")
    d_w_out = _mm_tn(merged, dh1b, name="dw_out")
    d_w_bd = _mm_tn(o_dn, dbd, name="dw_branch_dn")
    d_w_bs = _mm_tn(o_sb, dbs, name="dw_branch_sb")
    do_dn = _mm_nt(dbd, w_bd, out_dtype=F32, name="branch_dn_bwd")
    do_sb = _mm_nt(dbs, w_bs, out_dtype=BF16, name="branch_sb_bwd")

    do_raw, dz, d_g_dn = _gdn_post_bwd(o_raw, proj, g_dn, do_dn, col_z=col_z, name="gdn_post_bwd")
    du, dw, dp, dqd, dkd, dgl = _gdn_bwd_scan(u, w, pm, qd, kd, egl, states, do_raw, name="gdn_bwd_scan")
    dqn, dkn, dvv, dg, dbeta = _gdn_bwd_prep(qn, kn, vv, g, beta, tinv, u, w, du, dw, dp, dqd, dkd, dgl,
                                            name="gdn_bwd_prep")
    dcq, dck, dcv, dpab, d_a_log, d_dt = _gdn_pre_bwd_a(proj, pab, cq, ck, cv, a_log_p, dt_p, dqn, dkn, dvv, dg, dbeta,
                                                        name="gdn_pre_bwd", **gk)
    dpq, d_cq = _conv_bwd(proj, dcq, cq, heads=heads, width=dk, col=col_q, name="conv_q_bwd")
    dpk, d_ck = _conv_bwd(proj, dck, ck, heads=heads, width=dk, col=col_k, name="conv_k_bwd")
    dpv, d_cv = _conv_bwd(proj, dcv, cv, heads=heads, width=dv, col=col_v, name="conv_v_bwd")

    early = None
    if shards:
        slabs = [_bf(jnp.concatenate([d_wt_fg, d_wt_fu], axis=0)).reshape(N_DEV, -1, LANES)]
        slabs += [_bf(a).reshape(N_DEV, -1, d) for a in (d_w_bd, d_w_bs, d_w_out, d_w_fo)]
        dqs, dks, dvs, early = _sb_bwd(qs, ks, vs, do_sb, carry, heads=sb_heads, dh=dh, key_lo=pad_l, name="sb_bwd",
                                       scatter=slabs)
    else:
        dqs, dks, dvs = _sb_bwd(qs, ks, vs, do_sb, carry, heads=sb_heads, dh=dh, key_lo=pad_l, name="sb_bwd")
    dsq, dsk, dsv, d_g_sbq, d_g_sbk = _sb_pre_bwd(proj, g_sbq, g_sbk, dqs, dks, dvs, heads=sb_heads, dh=dh,
                                                   col_q=col_sq, col_k=col_sk, name="sb_pre_bwd")

    dproj = jnp.concatenate([dpq, dpk, dpv, dz, dsq, dsk, dsv, dgd, dgs], axis=1)
    dpab_b = _bf(dpab)
    d_wt_main = _mm_tn(dproj, n1, name="dw_in_main")
    d_wt_ab = _mm_tn(dpab_b, n1, name="dw_in_ab")
    s_in = None
    if shards:
        slabs = _w_in_slabs(_bf(d_wt_main), _bf(d_wt_ab), col_sq, 2 * heads).reshape(N_DEV, -1, LANES)
        dh0, _, d_g_mix, (s_in,) = _mm_rmsbwd([(dproj, wt_main)], (dpab_b, wt_ab), h0, g_mix, dh1, name="proj_bwd",
                                              scatter=[slabs])
    else:
        dh0, _, d_g_mix = _mm_rmsbwd([(dproj, wt_main)], (dpab_b, wt_ab), h0, g_mix, dh1, name="proj_bwd")

    return dict(s_in=s_in, lsum=lsum, grad_x=dh0[row_x:rows], d_meta=dh0[pad_l:row_x], d_g_mix=d_g_mix,
                d_wt_main=d_wt_main, d_wt_ab=d_wt_ab, d_cq=d_cq, d_ck=d_ck, d_cv=d_cv, d_a_log=d_a_log[:, :heads],
                d_dt=d_dt[:, :heads], d_g_dn=d_g_dn, d_g_sbq=d_g_sbq, d_g_sbk=d_g_sbk, d_w_bd=d_w_bd, d_w_bs=d_w_bs,
                d_w_out=d_w_out, d_g_ffn=d_g_ffn, d_wt_fg=d_wt_fg, d_wt_fu=d_wt_fu, d_w_fo=d_w_fo, early=early)


def _pack(parts):
    flat = []
    for a in parts:
        a = a.reshape(-1)
        flat.append(jnp.pad(a, (0, (-a.shape[0]) % LANES)))
    v = jnp.concatenate(flat)
    v = jnp.pad(v, (0, (-v.shape[0]) % (8 * LANES)))
    return v.reshape(-1, LANES)


def _unpack(packed, shapes):
    flat = packed.reshape(-1)
    out, pos = [], 0
    for s in shapes:
        n = math.prod(s)
        out.append(flat[pos:pos + n].reshape(s))
        pos += n + (-n) % LANES
    return out


def kernel(x, meta_tokens, norm_mix_gain, w_in, conv_q, conv_k, conv_v, dn_a_log, dn_dt_bias, dn_out_norm_gain, sb_q_norm_gain, sb_k_norm_gain, w_branch_dn, w_branch_sb, w_out, norm_ffn_gain, w_ffn_in, w_ffn_out, loss_target, m_meta_tokens, m_norm_mix_gain, m_w_in, m_conv_q, m_conv_k, m_conv_v, m_dn_a_log, m_dn_dt_bias, m_dn_out_norm_gain, m_sb_q_norm_gain, m_sb_k_norm_gain, m_w_branch_dn, m_w_branch_sb, m_w_out, m_norm_ffn_gain, m_w_ffn_in, m_w_ffn_out, v_meta_tokens, v_norm_mix_gain, v_w_in, v_conv_q, v_conv_k, v_conv_v, v_dn_a_log, v_dn_dt_bias, v_dn_out_norm_gain, v_sb_q_norm_gain, v_sb_k_norm_gain, v_w_branch_dn, v_w_branch_sb, v_w_out, v_norm_ffn_gain, v_w_ffn_in, v_w_ffn_out):
    me = 4 * lax.axis_index("x") + 2 * lax.axis_index("y") + lax.axis_index("c")
    heads = dn_a_log.shape[1]
    d = x.shape[2]
    qk = conv_q.shape[2] * N_DEV
    dvt = conv_v.shape[2] * N_DEV
    col_ab = 2 * qk + 2 * dvt

    small_shapes = [meta_tokens.shape, conv_q.shape[1:], conv_k.shape[1:], conv_v.shape[1:]]
    small = _pack([meta_tokens, conv_q[0], conv_k[0], conv_v[0]])
    def features_major(a):
        return jnp.transpose(a, (2, 0, 1)).reshape(a.shape[2], a.shape[1])

    g_in, g_small = _gather_two_level([_bf(features_major(w_in)), small], name="gather_w_in")
    per = g_in.shape[1]
    def features(lo, hi):
        return [g_in[p, max(lo, p * per) - p * per:min(hi, (p + 1) * per) - p * per]
                for p in range(N_DEV) if max(lo, p * per) < min(hi, (p + 1) * per)]

    wt_main = jnp.concatenate(features(0, col_ab) + features(col_ab + 2 * heads, N_DEV * per), axis=0)
    wt_ab = jnp.concatenate(features(col_ab, col_ab + 2 * heads) + [jnp.zeros((LANES - 2 * heads, d), BF16)], axis=0)
    parts = [_unpack(g_small[p], small_shapes) for p in range(N_DEV)]
    meta_f, cq_f, ck_f, cv_f = (jnp.concatenate([parts[p][a] for p in range(N_DEV)], axis=1) for a in range(4))

    r = _local_step(x[0], loss_target[0], meta_f, norm_mix_gain, wt_main, wt_ab, cq_f, ck_f, cv_f, dn_a_log, dn_dt_bias,
                    dn_out_norm_gain, sb_q_norm_gain, sb_k_norm_gain, norm_ffn_gain,
                    (_bf(features_major(w_ffn_in)), _bf(w_branch_dn[0]), _bf(w_branch_sb[0]), _bf(w_out[0]),
                     _bf(w_ffn_out[0])), shards=True)
    s_fi, s_bd, s_bs, s_out, s_fo = r["early"]
    s_in = r["s_in"]

    loss_part = (0.5 / d) * jnp.sum(r["lsum"], axis=1, keepdims=True)
    small_g = [r["d_meta"], r["d_g_mix"], r["d_cq"], r["d_ck"], r["d_cv"], r["d_a_log"], r["d_dt"], r["d_g_dn"],
               r["d_g_sbq"], r["d_g_sbk"], r["d_g_ffn"], loss_part]
    (g_packs,) = _exchange([_pack(small_g)], scatter=[False], name="gather_small_grads")
    (g_meta, g_mix, g_cq, g_ck, g_cv, g_al, g_dt, g_gdn, g_sbq, g_sbk, g_ffn, loss) = _unpack(
        _sum_slabs(g_packs, name="sum_small_grads"), [a.shape for a in small_g])

    def mine(a, width):
        return lax.dynamic_slice_in_dim(a, me * width, width, axis=1)

    big = dict(w_in=(s_in, w_in, m_w_in, v_w_in), w_branch_dn=(s_bd, w_branch_dn, m_w_branch_dn, v_w_branch_dn),
               w_branch_sb=(s_bs, w_branch_sb, m_w_branch_sb, v_w_branch_sb), w_out=(s_out, w_out, m_w_out, v_w_out),
               w_ffn_in=(s_fi, w_ffn_in, m_w_ffn_in, v_w_ffn_in), w_ffn_out=(s_fo, w_ffn_out, m_w_ffn_out, v_w_ffn_out))
    tiny = dict(meta_tokens=(mine(g_meta, d // N_DEV), meta_tokens, m_meta_tokens, v_meta_tokens),
                norm_mix_gain=(g_mix, norm_mix_gain, m_norm_mix_gain, v_norm_mix_gain),
                conv_q=(mine(g_cq, qk // N_DEV), conv_q[0], m_conv_q[0], v_conv_q[0]),
                conv_k=(mine(g_ck, qk // N_DEV), conv_k[0], m_conv_k[0], v_conv_k[0]),
                conv_v=(mine(g_cv, dvt // N_DEV), conv_v[0], m_conv_v[0], v_conv_v[0]),
                dn_a_log=(g_al, dn_a_log, m_dn_a_log, v_dn_a_log), dn_dt_bias=(g_dt, dn_dt_bias, m_dn_dt_bias, v_dn_dt_bias),
                dn_out_norm_gain=(g_gdn, dn_out_norm_gain, m_dn_out_norm_gain, v_dn_out_norm_gain),
                sb_q_norm_gain=(g_sbq, sb_q_norm_gain, m_sb_q_norm_gain, v_sb_q_norm_gain),
                sb_k_norm_gain=(g_sbk, sb_k_norm_gain, m_sb_k_norm_gain, v_sb_k_norm_gain),
                norm_ffn_gain=(g_ffn, norm_ffn_gain, m_norm_ffn_gain, v_norm_ffn_gain))
    order = ["meta_tokens", "norm_mix_gain", "w_in", "conv_q", "conv_k", "conv_v", "dn_a_log", "dn_dt_bias",
             "dn_out_norm_gain", "sb_q_norm_gain", "sb_k_norm_gain", "w_branch_dn", "w_branch_sb", "w_out",
             "norm_ffn_gain", "w_ffn_in", "w_ffn_out"]
    grads, deltas, new_m, new_v = [], [], [], []
    for name in order:
        if name in ("w_in", "w_ffn_in"):
            slabs, w, m, v = big[name]
            res = _adamw_slabs(slabs, *(features_major(a).reshape(-1, LANES) for a in (w, m, v)), name="adamw_" + name)
            g, dl, mo, vo = (jnp.transpose(a.reshape(w.shape[2], 1, w.shape[1]), (1, 2, 0)) for a in res)
            like = w.shape
        elif name in big:
            slabs, w, m, v = big[name]
            g, dl, mo, vo = _adamw_slabs(slabs, w[0], m[0], v[0], name="adamw_" + name)
            like = w.shape
        else:
            g, w, m, v = tiny[name]
            like = dict(conv_q=conv_q, conv_k=conv_k, conv_v=conv_v).get(name, w).shape
            dl, mo, vo = _adamw_small(g, w, m, v, name="adamw_" + name)
        for lst, a in ((grads, g), (deltas, dl), (new_m, mo), (new_v, vo)):
            lst.append(a.reshape(like))
    return (loss.reshape(()), r["grad_x"][None], *grads, *deltas, *new_m, *new_v)
```

```python
import functools
import math

import jax
import jax.numpy as jnp
from jax import lax
from jax.experimental import pallas as pl
from jax.experimental.pallas import tpu as pltpu

F32 = jnp.float32
BF16 = jnp.bfloat16

N_DEV = 8
CHUNK = 64
CHUNK_SHIFT = 6
GDN_ROWS = 2 * CHUNK
SB_BLOCK = 128
SB_HEADS_PER_STEP = 2
LANES = 128
RMS_EPS = 1e-6
L2_EPS = 1e-6
ADAM_LR = 0.001
ADAM_B1 = 0.9
ADAM_B2 = 0.999
ADAM_EPS = 1e-08
ADAM_WD = 0.01
ADAM_STEP = 10
V7X_VMEM_LIMIT_BYTES = 56 * 1024 * 1024
MM_TN_OUT_BLOCK_BYTES = 6 * 1024 * 1024
MM_PIECES_VMEM_BYTES = V7X_VMEM_LIMIT_BYTES // 2
ROWS_BIG = (1056, 512, 384, 256, 128)
ROWS_MID = (528, 384, 256, 128)
SCAN_HEADS = 4
PREP_HEADS = 2

MESH = pl.DeviceIdType.MESH


def _params(*sem):
    return pltpu.CompilerParams(dimension_semantics=sem or None, vmem_limit_bytes=V7X_VMEM_LIMIT_BYTES)


def _pick(n, cands):
    for c in cands:
        if n % c == 0:
            return c
    raise ValueError(f"no block size among {cands} divides {n}")


def _bf(x):
    return x.astype(BF16)


def _dot(a, b):
    return jnp.dot(a, b, preferred_element_type=F32)


def _dot_nt(a, b):
    return lax.dot_general(a, b, (((1,), (1,)), ((), ())), preferred_element_type=F32)


def _dot_tn(a, b):
    return lax.dot_general(a, b, (((0,), (0,)), ((), ())), preferred_element_type=F32)


def _split2(x):
    hi = _bf(x)
    return hi, _bf(x - hi.astype(F32))


def _split3(x):
    hi = _bf(x)
    r = x - hi.astype(F32)
    mid = _bf(r)
    return hi, mid, _bf(r - mid.astype(F32))


def _dot_hp(a, b, dot=_dot):
    ah, al = _split2(a)
    bh, bl = _split2(b)
    return dot(ah, bh) + dot(ah, bl) + dot(al, bh)


def _dot_exact_l(m, x, dot=_dot):
    h, mi, lo = _split3(x)
    return dot(m, h) + dot(m, mi) + dot(m, lo)


def _sigmoid(x):
    return 1.0 / (1.0 + jnp.exp(-x))


def _silu(x):
    return x * _sigmoid(x)


def _silu_grad(x):
    s = _sigmoid(x)
    return s * (1.0 + x * (1.0 - s))


def _softplus(x):
    return jnp.maximum(x, 0.0) + jnp.log(1.0 + jnp.exp(-jnp.abs(x)))


def _rms_fwd(h, gain):
    r = lax.rsqrt(jnp.mean(h * h, axis=-1, keepdims=True) + RMS_EPS)
    return h * r * gain


def _rms_bwd(h, gain, dy):
    r = lax.rsqrt(jnp.mean(h * h, axis=-1, keepdims=True) + RMS_EPS)
    dyg = dy * gain
    dh = r * dyg - h * (r * r * r) * jnp.mean(dyg * h, axis=-1, keepdims=True)
    return dh, dy * h * r


def _iota(shape, dim):
    return lax.broadcasted_iota(jnp.int32, shape, dim)


def _lane_pick(x, idx):
    return jnp.sum(jnp.where(_iota(x.shape, 1) == idx, x, 0.0), axis=1, keepdims=True)


def _mm_nt(a, b, *, out_dtype, name):
    m, k = a.shape
    n = b.shape[0]
    tm, tn = _pick(m, ROWS_BIG), _pick(n, (1024, 512, 256, 128))

    def body(a_ref, b_ref, o_ref):
        o_ref[...] = _dot_nt(a_ref[...], b_ref[...]).astype(out_dtype)

    return pl.pallas_call(
        body, grid=(m // tm, n // tn),
        in_specs=[pl.BlockSpec((tm, k), lambda i, j: (i, 0)), pl.BlockSpec((tn, k), lambda i, j: (j, 0))],
        out_specs=pl.BlockSpec((tm, tn), lambda i, j: (i, j)),
        out_shape=jax.ShapeDtypeStruct((m, n), out_dtype), name=name,
        compiler_params=_params("parallel", "parallel"))(a, b)


def _column_pieces(a):
    pieces = list(a) if isinstance(a, (list, tuple)) else [a]
    widths = [p.shape[1] for p in pieces]
    return pieces, widths, functools.reduce(math.gcd, widths)


def _piece_spans(widths, block):
    ends = [sum(widths[:j + 1]) // block for j in range(len(widths))]
    return list(zip([0] + ends[:-1], ends))


def _mm_tn(a, b, *, name):
    pieces, widths, unit = _column_pieces(a)
    t, m = pieces[0].shape[0], sum(widths)
    n = b.shape[1]
    tn = _pick(n, (2816, 2048, 1408, 1024, 512, 256, 128))
    tm = _pick(unit, tuple(c for c in (1408, 1024, 512, 256, 128) if c * tn * 4 <= MM_TN_OUT_BLOCK_BYTES))
    held = 2 * len(pieces) * tm * pieces[0].dtype.itemsize
    tk = _pick(t, tuple(c for c in (1408, 1024, 704, 512, 384, 256, 128) if c * held <= MM_PIECES_VMEM_BYTES))
    nk = t // tk
    spans = _piece_spans(widths, tm)

    def body(*refs):
        a_refs, (b_ref, o_ref, acc_ref) = refs[:len(pieces)], refs[len(pieces):]
        i, k = pl.program_id(0), pl.program_id(2)

        @pl.when(k == 0)
        def _():
            acc_ref[...] = jnp.zeros_like(acc_ref)

        for a_ref, (lo, hi) in zip(a_refs, spans):
            def add(a_ref=a_ref):
                acc_ref[...] += _dot_tn(a_ref[...], b_ref[...])
            if len(pieces) == 1:
                add()
            else:
                pl.when((i >= lo) & (i < hi))(add)

        @pl.when(k == nk - 1)
        def _():
            o_ref[...] = _bf(acc_ref[...])

    def a_spec(lo, hi):
        if len(pieces) == 1:
            return pl.BlockSpec((tk, tm), lambda i, j, k: (k, i))
        return pl.BlockSpec((tk, tm), lambda i, j, k: (jnp.where(i < lo, 0, jnp.where(i >= hi, nk - 1, k)),
                                                       jnp.clip(i - lo, 0, hi - lo - 1)))

    return pl.pallas_call(
        body, grid=(m // tm, n // tn, nk),
        in_specs=[a_spec(lo, hi) for lo, hi in spans] + [pl.BlockSpec((tk, tn), lambda i, j, k: (k, j))],
        out_specs=pl.BlockSpec((tm, tn), lambda i, j, k: (i, j)),
        out_shape=jax.ShapeDtypeStruct((m, n), BF16), scratch_shapes=[pltpu.VMEM((tm, tn), F32)], name=name,
        compiler_params=_params("parallel", "parallel", "arbitrary"))(*pieces, b)


def _mm_norm(h, gain, wt, *, name):
    m, k = h.shape
    n = wt.shape[0]
    tm, tn = _pick(m, ROWS_BIG), _pick(n, (1024, 512, 256, 128))

    def body(h_ref, g_ref, w_ref, o_ref, n_ref):
        @pl.when(pl.program_id(1) == 0)
        def _():
            n_ref[...] = _bf(_rms_fwd(h_ref[...], g_ref[...]))

        o_ref[...] = _dot_nt(n_ref[...], w_ref[...])

    return pl.pallas_call(
        body, grid=(m // tm, n // tn),
        in_specs=[pl.BlockSpec((tm, k), lambda i, j: (i, 0)), pl.BlockSpec((1, k), lambda i, j: (0, 0)),
                  pl.BlockSpec((tn, k), lambda i, j: (j, 0))],
        out_specs=[pl.BlockSpec((tm, tn), lambda i, j: (i, j)), pl.BlockSpec((tm, k), lambda i, j: (i, 0))],
        out_shape=[jax.ShapeDtypeStruct((m, n), F32), jax.ShapeDtypeStruct((m, k), BF16)], name=name,
        compiler_params=_params("parallel", "arbitrary"))(h, gain, wt)


def _mm_norm_swiglu(h, gain, wgt, wut, *, name):
    m, k = h.shape
    n = wgt.shape[0]
    tm, tn = _pick(m, ROWS_MID), _pick(n, (1408, 1024, 512, 256, 128))

    def body(h_ref, g_ref, wg_ref, wu_ref, gate_ref, up_ref, act_ref, n_ref):
        @pl.when(pl.program_id(1) == 0)
        def _():
            n_ref[...] = _bf(_rms_fwd(h_ref[...], g_ref[...]))

        gate = _dot_nt(n_ref[...], wg_ref[...])
        up = _dot_nt(n_ref[...], wu_ref[...])
        gate_ref[...] = gate
        up_ref[...] = up
        act_ref[...] = _bf(_silu(gate) * up)

    wspec = pl.BlockSpec((tn, k), lambda i, j: (j, 0))
    ospec = pl.BlockSpec((tm, tn), lambda i, j: (i, j))
    return pl.pallas_call(
        body, grid=(m // tm, n // tn),
        in_specs=[pl.BlockSpec((tm, k), lambda i, j: (i, 0)), pl.BlockSpec((1, k), lambda i, j: (0, 0)), wspec, wspec],
        out_specs=[ospec, ospec, ospec, pl.BlockSpec((tm, k), lambda i, j: (i, 0))],
        out_shape=[jax.ShapeDtypeStruct((m, n), F32), jax.ShapeDtypeStruct((m, n), F32),
                   jax.ShapeDtypeStruct((m, n), BF16), jax.ShapeDtypeStruct((m, k), BF16)], name=name,
        compiler_params=_params("parallel", "arbitrary"))(h, gain, wgt, wut)


def _mm_res(res, a, b, *, name):
    m, k = a.shape
    n = b.shape[1]
    tm, tn = _pick(m, ROWS_BIG), _pick(n, (1024, 512, 256, 128))

    def body(r_ref, a_ref, b_ref, o_ref):
        o_ref[...] = r_ref[...] + _dot(a_ref[...], b_ref[...])

    return pl.pallas_call(
        body, grid=(m // tm, n // tn),
        in_specs=[pl.BlockSpec((tm, tn), lambda i, j: (i, j)), pl.BlockSpec((tm, k), lambda i, j: (i, 0)),
                  pl.BlockSpec((k, tn), lambda i, j: (0, j))],
        out_specs=pl.BlockSpec((tm, tn), lambda i, j: (i, j)),
        out_shape=jax.ShapeDtypeStruct((m, n), F32), name=name,
        compiler_params=_params("parallel", "parallel"))(res, a, b)


def _mm_res_loss(res, a, b, target, *, row0, nrows, name):
    m, k = a.shape
    n = b.shape[1]
    tm = _pick(m, ROWS_MID)

    def body(r_ref, a_ref, b_ref, t_ref, dy_ref, dyb_ref, ls_ref):
        i = pl.program_id(0)

        @pl.when(i == 0)
        def _():
            ls_ref[...] = jnp.zeros_like(ls_ref)

        y = r_ref[...] + _dot(a_ref[...], b_ref[...])
        row = i * tm + _iota((tm, n), 0)
        e = jnp.where((row >= row0) & (row < row0 + nrows), y - t_ref[...], 0.0)
        dy = e / n
        dy_ref[...] = dy
        dyb_ref[...] = _bf(dy)
        ls_ref[...] += jnp.sum(e * e, axis=0, keepdims=True)

    rspec = pl.BlockSpec((tm, n), lambda i: (i, 0))
    return pl.pallas_call(
        body, grid=(m // tm,),
        in_specs=[rspec, pl.BlockSpec((tm, k), lambda i: (i, 0)), pl.BlockSpec((k, n), lambda i: (0, 0)), rspec],
        out_specs=[rspec, rspec, pl.BlockSpec((1, n), lambda i: (0, 0))],
        out_shape=[jax.ShapeDtypeStruct((m, n), F32), jax.ShapeDtypeStruct((m, n), BF16),
                   jax.ShapeDtypeStruct((1, n), F32)], name=name,
        compiler_params=_params("arbitrary"))(res, a, b, target)


def _merge_fwd(o_dn, o_sb, wbd, wbs, proj, *, col_gd, col_gs, name):
    m, kd = o_dn.shape
    ks = o_sb.shape[1]
    n = wbd.shape[1]
    tm = _pick(m, ROWS_BIG)
    tn = _pick(math.gcd(n, math.gcd(col_gd, col_gs)), (512, 256, 128))

    def body(od_ref, os_ref, wd_ref, ws_ref, gd_ref, gs_ref, mg_ref, bd_ref, bs_ref):
        bd = _dot(od_ref[...], wd_ref[...])
        bs = _dot(os_ref[...], ws_ref[...])
        bd_ref[...] = bd
        bs_ref[...] = bs
        mg_ref[...] = _bf(_sigmoid(gd_ref[...]) * bd + _sigmoid(gs_ref[...]) * bs)

    ospec = pl.BlockSpec((tm, tn), lambda i, j: (i, j))
    return pl.pallas_call(
        body, grid=(m // tm, n // tn),
        in_specs=[pl.BlockSpec((tm, kd), lambda i, j: (i, 0)), pl.BlockSpec((tm, ks), lambda i, j: (i, 0)),
                  pl.BlockSpec((kd, tn), lambda i, j: (0, j)), pl.BlockSpec((ks, tn), lambda i, j: (0, j)),
                  pl.BlockSpec((tm, tn), lambda i, j: (i, col_gd // tn + j)),
                  pl.BlockSpec((tm, tn), lambda i, j: (i, col_gs // tn + j))],
        out_specs=[ospec, ospec, ospec],
        out_shape=[jax.ShapeDtypeStruct((m, n), BF16), jax.ShapeDtypeStruct((m, n), F32),
                   jax.ShapeDtypeStruct((m, n), F32)], name=name,
        compiler_params=_params("parallel", "parallel"))(o_dn, o_sb, wbd, wbs, proj, proj)


def _merge_bwd(dh, w_out, proj, br_dn, br_sb, *, col_gd, col_gs, name):
    m, k = dh.shape
    n = w_out.shape[0]
    tm = _pick(m, ROWS_BIG)
    tn = _pick(math.gcd(n, math.gcd(col_gd, col_gs)), (512, 256, 128))

    def body(dh_ref, w_ref, gd_ref, gs_ref, bd_ref, bs_ref, dbd_ref, dbs_ref, dgd_ref, dgs_ref):
        dm = _dot_nt(dh_ref[...], w_ref[...])
        sd = _sigmoid(gd_ref[...])
        ss = _sigmoid(gs_ref[...])
        dbd_ref[...] = _bf(dm * sd)
        dbs_ref[...] = _bf(dm * ss)
        dgd_ref[...] = _bf(dm * bd_ref[...] * sd * (1.0 - sd))
        dgs_ref[...] = _bf(dm * bs_ref[...] * ss * (1.0 - ss))

    ospec = pl.BlockSpec((tm, tn), lambda i, j: (i, j))
    return pl.pallas_call(
        body, grid=(m // tm, n // tn),
        in_specs=[pl.BlockSpec((tm, k), lambda i, j: (i, 0)), pl.BlockSpec((tn, k), lambda i, j: (j, 0)),
                  pl.BlockSpec((tm, tn), lambda i, j: (i, col_gd // tn + j)),
                  pl.BlockSpec((tm, tn), lambda i, j: (i, col_gs // tn + j)), ospec, ospec],
        out_specs=[ospec] * 4,
        out_shape=[jax.ShapeDtypeStruct((m, n), BF16)] * 4, name=name,
        compiler_params=_params("parallel", "parallel"))(dh, w_out, proj, proj, br_dn, br_sb)


def _swiglu_bwd(dy, wfo, gate, up, *, name):
    m, k = dy.shape
    n = wfo.shape[0]
    tm, tn = _pick(m, ROWS_MID), _pick(n, (1408, 1024, 512, 256, 128))

    def body(dy_ref, w_ref, g_ref, u_ref, dg_ref, du_ref):
        da = _dot_nt(dy_ref[...], w_ref[...])
        g = g_ref[...]
        dg_ref[...] = _bf(da * u_ref[...] * _silu_grad(g))
        du_ref[...] = _bf(da * _silu(g))

    ospec = pl.BlockSpec((tm, tn), lambda i, j: (i, j))
    return pl.pallas_call(
        body, grid=(m // tm, n // tn),
        in_specs=[pl.BlockSpec((tm, k), lambda i, j: (i, 0)), pl.BlockSpec((tn, k), lambda i, j: (j, 0)), ospec, ospec],
        out_specs=[ospec, ospec], out_shape=[jax.ShapeDtypeStruct((m, n), BF16)] * 2, name=name,
        compiler_params=_params("parallel", "parallel"))(dy, wfo, gate, up)


def _mm_rmsbwd(pairs, extra, h, gain, dres, *, name, scatter=None):
    split = [_column_pieces(a) for a, _ in pairs]
    m, k = h.shape[0], sum(split[0][1])
    n = h.shape[1]
    tm = _pick(m, ROWS_MID)
    tk = _pick(functools.reduce(math.gcd, [unit for _, _, unit in split]), (1408, 1024, 512, 256, 128))
    nk = k // tk
    spans = [_piece_spans(widths, tk) for _, widths, _ in split]
    n_ab = sum(len(s) + 1 for s in spans)

    def body(*refs):
        ex = refs[n_ab:n_ab + 2] if extra is not None else ()
        h_ref, g_ref, r_ref, dh_ref, dhb_ref, dg_ref, acc_ref = refs[n_ab + len(ex):]
        i, kk = pl.program_id(0), pl.program_id(1)

        @pl.when((i == 0) & (kk == 0))
        def _():
            dg_ref[...] = jnp.zeros_like(dg_ref)

        @pl.when(kk == 0)
        def _():
            acc_ref[...] = _dot(ex[0][...], ex[1][...]) if ex else jnp.zeros_like(acc_ref)

        pos, whole = 0, []
        for s in spans:
            a_refs, b_ref = refs[pos:pos + len(s)], refs[pos + len(s)]
            pos += len(s) + 1
            if len(s) == 1:
                whole.append(_dot(a_refs[0][...], b_ref[...]))
                continue
            for a_ref, (lo, hi) in zip(a_refs, s):
                def add(a_ref=a_ref, b_ref=b_ref):
                    acc_ref[...] += _dot(a_ref[...], b_ref[...])
                pl.when((kk >= lo) & (kk < hi))(add)
        if whole:
            acc_ref[...] += functools.reduce(lambda x, y: x + y, whole)

        @pl.when(kk == nk - 1)
        def _():
            dh, dgr = _rms_bwd(h_ref[...], g_ref[...], acc_ref[...])
            dh = dh + r_ref[...]
            dh_ref[...] = dh
            dhb_ref[...] = _bf(dh)
            dg_ref[...] += jnp.sum(dgr, axis=0, keepdims=True)

    in_specs, args = [], []
    def a_spec(lo, hi, alone):
        if alone:
            return pl.BlockSpec((tm, tk), lambda i, kk: (i, kk))
        return pl.BlockSpec((tm, tk), lambda i, kk: (i, jnp.clip(kk - lo, 0, hi - lo - 1)))

    for (pieces, _, _), s, (_, b) in zip(split, spans, pairs):
        in_specs += [a_spec(lo, hi, len(s) == 1) for lo, hi in s] + [pl.BlockSpec((tk, n), lambda i, kk: (kk, 0))]
        args += [*pieces, b]
    if extra is not None:
        k2 = extra[0].shape[1]
        in_specs += [pl.BlockSpec((tm, k2), lambda i, kk: (i, 0)), pl.BlockSpec((k2, n), lambda i, kk: (0, 0))]
        args += list(extra)
    rspec = pl.BlockSpec((tm, n), lambda i, kk: (i, 0))
    in_specs += [rspec, pl.BlockSpec((1, n), lambda i, kk: (0, 0)), rspec]
    call = dict(grid=(m // tm, nk), in_specs=in_specs,
                out_specs=[rspec, rspec, pl.BlockSpec((1, n), lambda i, kk: (0, 0))],
                out_shape=[jax.ShapeDtypeStruct((m, n), F32), jax.ShapeDtypeStruct((m, n), BF16),
                           jax.ShapeDtypeStruct((1, n), F32)],
                scratch_shapes=[pltpu.VMEM((tm, n), F32)], name=name)
    if scatter is None:
        return pl.pallas_call(body, compiler_params=_params("arbitrary", "arbitrary"), **call)(*args, h, gain, dres)
    res = _call_with_exchange(body, args=(*args, h, gain, dres), srcs=scatter, scatter=[True] * len(scatter), **call)
    return [*res[:3], list(res[3:])]


HALO = 8


def _stage(buf_ref, before, cur):
    buf_ref[0:HALO, :] = before
    buf_ref[HALO:HALO + cur.shape[0], :] = cur


def _stage_after(buf_ref, cur, after):
    r = cur.shape[0]
    buf_ref[0:r, :] = cur
    buf_ref[r:r + HALO, :] = after


def _conv_taps(buf_ref, rows, w_ref):
    nk = w_ref.shape[0]
    out = buf_ref[HALO:HALO + rows, :] * w_ref[nk - 1:nk, :]
    for s in range(1, nk):
        out += buf_ref[HALO - s:HALO - s + rows, :] * w_ref[nk - 1 - s:nk - s, :]
    return out


def _gdn_pre(proj, pab, cq, ck, cv, a_log, dt_bias, *, heads, dk, dv, col_q, col_k, col_v, row_lo, row_hi, name):
    t = proj.shape[0]
    tm = _pick(t, (384, 256, 128))
    nb = t // tm

    def body(pq_ref, pqp_ref, pk_ref, pkp_ref, pv_ref, pvp_ref, ab_ref, cq_ref, ck_ref, cv_ref, al_ref, dt_ref,
             qn_ref, kn_ref, v_ref, g_ref, b_ref, bq_ref, bk_ref, bv_ref):
        h, i = pl.program_id(0), pl.program_id(1)
        first = i == 0
        row = i * tm + _iota((tm, 1), 0)
        valid = (row >= row_lo) & (row < row_hi)
        _stage(bq_ref, jnp.where(first, 0.0, pqp_ref[...]), pq_ref[...])
        _stage(bk_ref, jnp.where(first, 0.0, pkp_ref[...]), pk_ref[...])
        _stage(bv_ref, jnp.where(first, 0.0, pvp_ref[...]), pv_ref[...])
        q1 = _silu(_conv_taps(bq_ref, tm, cq_ref))
        k1 = _silu(_conv_taps(bk_ref, tm, ck_ref))
        v1 = _silu(_conv_taps(bv_ref, tm, cv_ref))
        qn_ref[...] = jnp.where(valid, q1 * lax.rsqrt(jnp.sum(q1 * q1, axis=-1, keepdims=True) + L2_EPS), 0.0)
        kn_ref[...] = jnp.where(valid, k1 * lax.rsqrt(jnp.sum(k1 * k1, axis=-1, keepdims=True) + L2_EPS), 0.0)
        v_ref[...] = jnp.where(valid, v1, 0.0)
        ab = ab_ref[...]
        da = _lane_pick(ab, h)
        db = _lane_pick(ab, heads + h)
        a = _lane_pick(al_ref[...], h)
        dtb = _lane_pick(dt_ref[...], h)
        g_ref[...] = jnp.where(valid, -jnp.exp(a) * _softplus(da + dtb), 0.0)
        b_ref[...] = jnp.where(valid, _sigmoid(db), 0.0)

    def cur(width, col):
        return pl.BlockSpec((tm, width), lambda h, i: (i, col // width + h))

    def prev(width, col):
        return pl.BlockSpec((8, width), lambda h, i: (jnp.maximum(i * (tm // 8) - 1, 0), col // width + h))

    def out(width):
        return pl.BlockSpec((None, tm, width), lambda h, i: (h, i, 0))

    small = pl.BlockSpec((1, LANES), lambda h, i: (0, 0))
    return pl.pallas_call(
        body, grid=(heads, nb),
        in_specs=[cur(dk, col_q), prev(dk, col_q), cur(dk, col_k), prev(dk, col_k), cur(dv, col_v), prev(dv, col_v),
                  pl.BlockSpec((tm, LANES), lambda h, i: (i, 0)),
                  pl.BlockSpec((cq.shape[0], dk), lambda h, i: (0, h)), pl.BlockSpec((ck.shape[0], dk), lambda h, i: (0, h)),
                  pl.BlockSpec((cv.shape[0], dv), lambda h, i: (0, h)), small, small],
        out_specs=[out(dk), out(dk), out(dv), out(1), out(1)],
        out_shape=[jax.ShapeDtypeStruct((heads, t, dk), F32), jax.ShapeDtypeStruct((heads, t, dk), F32),
                   jax.ShapeDtypeStruct((heads, t, dv), F32), jax.ShapeDtypeStruct((heads, t, 1), F32),
                   jax.ShapeDtypeStruct((heads, t, 1), F32)],
        scratch_shapes=[pltpu.VMEM((HALO + tm, dk), F32), pltpu.VMEM((HALO + tm, dk), F32),
                        pltpu.VMEM((HALO + tm, dv), F32)], name=name,
        compiler_params=_params("parallel", "parallel"))(proj, proj, proj, proj, proj, proj, pab, cq, ck, cv, a_log, dt_bias)


def _chunk_masks(rows=GDN_ROWS, row0=0):
    ri = row0 + _iota((rows, GDN_ROWS), 0)
    ci = _iota((rows, GDN_ROWS), 1)
    same = jnp.right_shift(ri, CHUNK_SHIFT) == jnp.right_shift(ci, CHUNK_SHIFT)
    return same, same & (ri >= ci), same & (ri > ci), ri == ci


def _col_to_row(col, eye):
    return jnp.sum(jnp.where(eye, col, 0.0), axis=0, keepdims=True)


def _row_to_col(row, eye):
    return jnp.sum(jnp.where(eye, row, 0.0), axis=1, keepdims=True)


def _chunk_common(blocks, dk_scale):
    same, incl, strict, eye = _chunk_masks()
    tri = jnp.where(incl, 1.0, 0.0).astype(BF16)
    tot = jnp.where(same, 1.0, 0.0).astype(BF16)
    gbs = [jnp.broadcast_to(g, (GDN_ROWS, LANES)) for _, _, g, _ in blocks]
    gams = [jnp.max(_dot_exact_l(tri, gb), axis=1, keepdims=True) for gb in gbs]
    lasts = [jnp.max(_dot_exact_l(tot, gb), axis=1, keepdims=True) for gb in gbs]
    kbs = [kn * beta for _, kn, _, beta in blocks]
    qts = [qn * dk_scale for qn, _, _, _ in blocks]
    boths = [_dot_nt(_bf(jnp.concatenate([kb, qt], axis=0)), _bf(blk[1]))
             for kb, qt, blk in zip(kbs, qts, blocks)]
    out = []
    for gam, last, kb, qt, both in zip(gams, lasts, kbs, qts, boths):
        diff = gam - _col_to_row(gam, eye)
        decay = jnp.where(incl, jnp.exp(jnp.where(incl, diff, 0.0)), 0.0)
        out.append(dict(incl=incl, strict=strict, eye=eye, decay=decay, eg=jnp.exp(gam), ek=jnp.exp(last - gam),
                        egl=jnp.exp(last), kb=kb, qt=qt, lmat=jnp.where(strict, both[:GDN_ROWS] * decay, 0.0),
                        pmat=jnp.where(incl, both[GDN_ROWS:] * decay, 0.0)))
    return out


def _gdn_prep(qn, kn, v, g, beta, *, name):
    heads, t, dk = qn.shape
    dv = v.shape[2]
    rows = _pick(t, (3 * GDN_ROWS, 2 * GDN_ROWS, GDN_ROWS))
    dk_scale = dk ** -0.5

    hp = PREP_HEADS

    def body(q_ref, k_ref, v_ref, g_ref, b_ref, u_ref, w_ref, p_ref, qd_ref, kd_ref, egl_ref, t_ref):
        rs = [(hh, pl.ds(b * GDN_ROWS, GDN_ROWS), slice(None)) for hh in range(hp) for b in range(rows // GDN_ROWS)]
        cs = _chunk_common([(q_ref[r], k_ref[r], g_ref[r], b_ref[r]) for r in rs], dk_scale)
        eye_f = jnp.where(cs[0]["eye"], 1.0, 0.0)
        tinvs = [eye_f - c["lmat"] for c in cs]
        ys = [_dot_hp(c["lmat"], c["lmat"]) for c in cs]
        for _ in range(CHUNK_SHIFT - 1):
            boths = [_dot_hp(y, jnp.concatenate([y, tinv], axis=1)) for y, tinv in zip(ys, tinvs)]
            ys = [both[:, :GDN_ROWS] for both in boths]
            tinvs = [tinv + both[:, GDN_ROWS:] for tinv, both in zip(tinvs, boths)]
        uws = [_dot_hp(tinv, jnp.concatenate([v_ref[r] * b_ref[r], c["kb"] * c["eg"]], axis=1))
               for r, c, tinv in zip(rs, cs, tinvs)]
        for r, c, tinv, uw in zip(rs, cs, tinvs, uws):
            u_ref[r] = uw[:, :dv]
            w_ref[r] = _bf(uw[:, dv:])
            p_ref[r] = _bf(c["pmat"])
            qd_ref[r] = _bf(c["qt"] * c["eg"])
            kd_ref[r] = _bf(k_ref[r] * c["ek"])
            egl_ref[r] = c["egl"]
            t_ref[r] = tinv

    def blk(width):
        return pl.BlockSpec((hp, rows, width), lambda h, i: (h, i, 0))

    def shp(width, dtype=F32):
        return jax.ShapeDtypeStruct((heads, t, width), dtype)

    return pl.pallas_call(
        body, grid=(heads // hp, t // rows), in_specs=[blk(dk), blk(dk), blk(dv), blk(1), blk(1)],
        out_specs=[blk(dv), blk(dk), blk(GDN_ROWS), blk(dk), blk(dk), blk(1), blk(GDN_ROWS)],
        out_shape=[shp(dv), shp(dk, BF16), shp(GDN_ROWS, BF16), shp(dk, BF16), shp(dk, BF16), shp(1), shp(GDN_ROWS)],
        name=name,
        compiler_params=_params("parallel", "parallel"))(qn, kn, v, g, beta)


def _gdn_scan(u, w, p, qd, kd, egl, proj, gain, *, col_z, name):
    heads, t, dv = u.shape
    dk = w.shape[2]
    nb = t // GDN_ROWS
    sub = GDN_ROWS // CHUNK
    hp = SCAN_HEADS

    def body(u_ref, w_ref, p_ref, qd_ref, kd_ref, egl_ref, z_ref, gn_ref, o_ref, og_ref, st_ref, s_ref):
        @pl.when(pl.program_id(1) == 0)
        def _():
            s_ref[...] = jnp.zeros_like(s_ref)

        hs = range(hp)
        vn_parts = [[jnp.zeros((CHUNK, dv), F32)] * sub for _ in hs]
        for c in range(sub):
            r = pl.ds(c * CHUNK, CHUNK)
            ss = [s_ref[hh] for hh in hs]
            sbs = [_bf(s) for s in ss]
            wss = [_dot(_bf(jnp.concatenate([w_ref[hh, r, :], qd_ref[hh, r, :]], axis=0)), sbs[hh])
                   for hh in hs]
            vns = [u_ref[hh, r, :] - wss[hh][:CHUNK] for hh in hs]
            for hh in hs:
                vn_parts[hh][c] = vns[hh]
            os_ = [wss[hh][CHUNK:] + _dot(_bf(p_ref[hh, r, :]), _bf(jnp.concatenate(vn_parts[hh], axis=0))) for hh in hs]
            new = [ss[hh] * egl_ref[hh, pl.ds(c * CHUNK, 1), :] + _dot_tn(_bf(kd_ref[hh, r, :]), _bf(vns[hh])) for hh in hs]
            for hh in hs:
                cols = pl.ds(hh * dv, dv)
                st_ref[hh, c] = sbs[hh]
                s_ref[hh] = new[hh]
                o_ref[hh, r, :] = os_[hh]
                og_ref[r, cols] = _bf(_rms_fwd(os_[hh], gn_ref[...]) * _silu(z_ref[r, cols]))

    def blk(width):
        return pl.BlockSpec((hp, GDN_ROWS, width), lambda h, i: (h, i, 0))

    return pl.pallas_call(
        body, grid=(heads // hp, nb),
        in_specs=[blk(dv), blk(dk), blk(GDN_ROWS), blk(dk), blk(dk), blk(1),
                  pl.BlockSpec((GDN_ROWS, hp * dv), lambda h, i: (i, col_z // (hp * dv) + h)),
                  pl.BlockSpec((1, dv), lambda h, i: (0, 0))],
        out_specs=[blk(dv), pl.BlockSpec((GDN_ROWS, hp * dv), lambda h, i: (i, h)),
                   pl.BlockSpec((hp, sub, dk, dv), lambda h, i: (h, i, 0, 0))],
        out_shape=[jax.ShapeDtypeStruct((heads, t, dv), F32), jax.ShapeDtypeStruct((t, heads * dv), BF16),
                   jax.ShapeDtypeStruct((heads, t // CHUNK, dk, dv), BF16)],
        scratch_shapes=[pltpu.VMEM((hp, dk, dv), F32)], name=name,
        compiler_params=_params("parallel", "arbitrary"))(u, w, p, qd, kd, egl, proj, gain)


def _gdn_post_bwd(o, proj, gain, dout, *, col_z, name):
    heads, t, dv = o.shape
    tm = _pick(t, (384, 256, 128))

    def body(o_ref, z_ref, gn_ref, d_ref, do_ref, dz_ref, dg_ref):
        @pl.when((pl.program_id(0) == 0) & (pl.program_id(1) == 0))
        def _():
            dg_ref[...] = jnp.zeros_like(dg_ref)

        o_, z, d = o_ref[...], z_ref[...], d_ref[...]
        y = _rms_fwd(o_, gn_ref[...])
        dz_ref[...] = _bf(d * y * _silu_grad(z))
        do, dgr = _rms_bwd(o_, gn_ref[...], d * _silu(z))
        do_ref[...] = do
        dg_ref[...] += jnp.sum(dgr, axis=0, keepdims=True)

    return pl.pallas_call(
        body, grid=(t // tm, heads),
        in_specs=[pl.BlockSpec((None, tm, dv), lambda i, h: (h, i, 0)),
                  pl.BlockSpec((tm, dv), lambda i, h: (i, col_z // dv + h)),
                  pl.BlockSpec((1, dv), lambda i, h: (0, 0)), pl.BlockSpec((tm, dv), lambda i, h: (i, h))],
        out_specs=[pl.BlockSpec((None, tm, dv), lambda i, h: (h, i, 0)), pl.BlockSpec((tm, dv), lambda i, h: (i, h)),
                   pl.BlockSpec((1, dv), lambda i, h: (0, 0))],
        out_shape=[jax.ShapeDtypeStruct((heads, t, dv), F32), jax.ShapeDtypeStruct((t, heads * dv), BF16),
                   jax.ShapeDtypeStruct((1, dv), F32)], name=name,
        compiler_params=_params("arbitrary", "arbitrary"))(o, proj, gain, dout)


def _gdn_bwd_scan(u, w, p, qd, kd, egl, st, do, *, name):
    heads, t, dv = u.shape
    dk = w.shape[2]
    nb = t // GDN_ROWS
    sub = GDN_ROWS // CHUNK
    hp = SCAN_HEADS

    def body(u_ref, w_ref, p_ref, qd_ref, kd_ref, egl_ref, st_ref, do_ref,
             du_ref, dw_ref, dp_ref, dqd_ref, dkd_ref, dgl_ref, ds_ref):
        @pl.when(pl.program_id(1) == 0)
        def _():
            ds_ref[...] = jnp.zeros_like(ds_ref)

        hs = range(hp)
        zeros = jnp.zeros((CHUNK, dv), BF16)
        for c in reversed(range(sub)):
            r = pl.ds(c * CHUNK, CHUNK)
            sbs = [st_ref[hh, c] for hh in hs]
            dss = [ds_ref[hh] for hh in hs]
            dsbs = [_bf(ds) for ds in dss]
            dobs = [_bf(do_ref[hh, r, :]) for hh in hs]
            wbs = [_bf(w_ref[hh, r, :]) for hh in hs]
            vns = [u_ref[hh, r, :] - _dot(wbs[hh], sbs[hh]) for hh in hs]
            dvns = [_dot_tn(_bf(p_ref[hh, r, :]), dobs[hh])[c * CHUNK:(c + 1) * CHUNK, :]
                    + _dot(_bf(kd_ref[hh, r, :]), dsbs[hh]) for hh in hs]
            dods = [jnp.concatenate([dobs[hh], _bf(dvns[hh])], axis=0) for hh in hs]
            boths = [_dot_nt(dods[hh], sbs[hh]) for hh in hs]
            dps = [_dot_nt(dobs[hh], jnp.concatenate([_bf(vns[hh]) if cc == c else zeros for cc in range(sub)], axis=0))
                   for hh in hs]
            dkds = [_dot_nt(_bf(vns[hh]), dsbs[hh]) for hh in hs]
            new = [dss[hh] * egl_ref[hh, pl.ds(c * CHUNK, 1), :]
                   + _dot_tn(jnp.concatenate([_bf(qd_ref[hh, r, :]), -wbs[hh]], axis=0), dods[hh])
                   for hh in hs]
            for hh in hs:
                du_ref[hh, r, :] = dvns[hh]
                dw_ref[hh, r, :] = -boths[hh][CHUNK:]
                dp_ref[hh, r, :] = jnp.where(_chunk_masks(CHUNK, c * CHUNK)[1], dps[hh], 0.0)
                dqd_ref[hh, r, :] = boths[hh][:CHUNK]
                dkd_ref[hh, r, :] = dkds[hh]
                dgl = jnp.sum(jnp.sum(dss[hh] * sbs[hh].astype(F32), axis=1, keepdims=True), axis=0, keepdims=True)
                dgl_ref[hh, r, :] = jnp.where(_iota((CHUNK, 1), 0) == CHUNK - 1, dgl, 0.0)
                ds_ref[hh] = new[hh]

    def blk(width):
        return pl.BlockSpec((hp, GDN_ROWS, width), lambda h, i: (h, nb - 1 - i, 0))

    def shp(width):
        return jax.ShapeDtypeStruct((heads, t, width), F32)

    return pl.pallas_call(
        body, grid=(heads // hp, nb),
        in_specs=[blk(dv), blk(dk), blk(GDN_ROWS), blk(dk), blk(dk), blk(1),
                  pl.BlockSpec((hp, sub, dk, dv), lambda h, i: (h, nb - 1 - i, 0, 0)), blk(dv)],
        out_specs=[blk(dv), blk(dk), blk(GDN_ROWS), blk(dk), blk(dk), blk(1)],
        out_shape=[shp(dv), shp(dk), shp(GDN_ROWS), shp(dk), shp(dk), shp(1)],
        scratch_shapes=[pltpu.VMEM((hp, dk, dv), F32)], name=name,
        compiler_params=_params("parallel", "arbitrary"))(u, w, p, qd, kd, egl, st, do)


def _gdn_bwd_prep(qn, kn, v, g, beta, tinv, u, w, du, dw, dp, dqd, dkd, dgl, *, name):
    heads, t, dk = qn.shape
    dv = v.shape[2]
    rows = _pick(t, (3 * GDN_ROWS, 2 * GDN_ROWS, GDN_ROWS))
    dk_scale = dk ** -0.5
    hp = PREP_HEADS

    def rowsum(x):
        return jnp.sum(x, axis=1, keepdims=True)

    def body(q_ref, k_ref, v_ref, g_ref, b_ref, t_ref, u_ref, w_ref, du_ref, dw_ref, dp_ref, dqd_ref, dkd_ref, dgl_ref,
             dq_ref, dkk_ref, dvv_ref, dg_ref, db_ref):
        rs = [(hh, pl.ds(b * GDN_ROWS, GDN_ROWS), slice(None)) for hh in range(hp) for b in range(rows // GDN_ROWS)]
        cs = _chunk_common([(q_ref[r], k_ref[r], g_ref[r], b_ref[r]) for r in rs], dk_scale)
        dbvws = [_dot_hp(t_ref[r], jnp.concatenate([du_ref[r], dw_ref[r]], axis=1), _dot_tn)
                 for r in rs]
        das = [-_dot_nt(_bf(dbvw), jnp.concatenate([_bf(u_ref[r]), w_ref[r]], axis=1))
               for r, dbvw in zip(rs, dbvws)]
        dls = [jnp.where(c["strict"], da, 0.0) for c, da in zip(cs, das)]
        dmns = [_bf(jnp.concatenate([dl * c["decay"], dp_ref[r] * c["decay"]], axis=0))
                for r, c, dl in zip(rs, cs, dls)]
        boths = [_dot(dmn, _bf(k_ref[r])) for r, dmn in zip(rs, dmns)]
        dkns = [_dot_tn(dmn, _bf(jnp.concatenate([c["kb"], c["qt"]], axis=0)))
                for c, dmn in zip(cs, dmns)]
        for r, c, dbvw, dl, both, dkn in zip(rs, cs, dbvws, dls, boths, dkns):
            kn_, beta_, v_ = k_ref[r], b_ref[r], v_ref[r]
            eye = c["eye"]
            kb, qt, eg, ek = c["kb"], c["qt"], c["eg"], c["ek"]
            dbv, dbw = dbvw[:, :dv], dbvw[:, dv:]
            dp_ = dp_ref[r]
            dkb = both[:GDN_ROWS] + dbw * eg
            dqt = both[GDN_ROWS:]
            gmat = dl * c["lmat"] + dp_ * c["pmat"]
            dqd_, dkd_ = dqd_ref[r], dkd_ref[r]
            qd = qt * eg
            kd = kn_ * ek
            bw = kb * eg
            kdsum = rowsum(dkd_ * kd)
            dgam = rowsum(gmat) - _row_to_col(jnp.sum(gmat, axis=0, keepdims=True), eye)
            dgam += rowsum(dbw * bw) + rowsum(dqd_ * qd) - kdsum
            last = (_iota((GDN_ROWS, 1), 0) & (CHUNK - 1)) == CHUNK - 1
            same = _chunk_masks()[0]
            same_f = jnp.where(same, 1.0, 0.0).astype(BF16)
            chunk_tot = jnp.max(_dot_exact_l(same_f, jnp.broadcast_to(kdsum, (GDN_ROWS, LANES))), axis=1, keepdims=True)
            dgam += jnp.where(last, chunk_tot, 0.0) + dgl_ref[r] * c["egl"]
            dq_ref[r] = (dqt + dqd_ * eg) * dk_scale
            dkk_ref[r] = dkn + dkd_ * ek + dkb * beta_
            dvv_ref[r] = dbv * beta_
            db_ref[r] = rowsum(dbv * v_) + rowsum(dkb * kn_)
            upper = jnp.where(same & (_iota((GDN_ROWS, GDN_ROWS), 0) <= _iota((GDN_ROWS, GDN_ROWS), 1)), 1.0, 0.0)
            dgb = _dot_exact_l(upper.astype(BF16), jnp.broadcast_to(dgam, (GDN_ROWS, LANES)))
            dg_ref[r] = _lane_pick(dgb, 0)

    def blk(width):
        return pl.BlockSpec((hp, rows, width), lambda h, i: (h, i, 0))

    def shp(width):
        return jax.ShapeDtypeStruct((heads, t, width), F32)

    return pl.pallas_call(
        body, grid=(heads // hp, t // rows),
        in_specs=[blk(dk), blk(dk), blk(dv), blk(1), blk(1), blk(GDN_ROWS), blk(dv), blk(dk),
                  blk(dv), blk(dk), blk(GDN_ROWS), blk(dk), blk(dk), blk(1)],
        out_specs=[blk(dk), blk(dk), blk(dv), blk(1), blk(1)],
        out_shape=[shp(dk), shp(dk), shp(dv), shp(1), shp(1)], name=name,
        compiler_params=_params("parallel", "parallel"))(qn, kn, v, g, beta, tinv, u, w, du, dw, dp, dqd, dkd, dgl)


def _gdn_pre_bwd_a(proj, pab, cq, ck, cv, a_log, dt_bias, dqn, dkn, dvv, dg, dbeta, *,
                   heads, dk, dv, col_q, col_k, col_v, row_lo, row_hi, name):
    t = proj.shape[0]
    tm = _pick(t, (384, 256, 128))
    nb = t // tm

    def body(pq_ref, pqp_ref, pk_ref, pkp_ref, pv_ref, pvp_ref, ab_ref, cq_ref, ck_ref, cv_ref, al_ref, dt_ref,
             dqn_ref, dkn_ref, dvv_ref, dg_ref, db_ref, dcq_ref, dck_ref, dcv_ref, dab_ref, dal_ref, ddt_ref,
             bq_ref, bk_ref, bv_ref):
        i, h = pl.program_id(0), pl.program_id(1)
        first = i == 0

        @pl.when((i == 0) & (h == 0))
        def _():
            dal_ref[...] = jnp.zeros_like(dal_ref)
            ddt_ref[...] = jnp.zeros_like(ddt_ref)

        @pl.when(h == 0)
        def _():
            dab_ref[...] = jnp.zeros_like(dab_ref)

        row = i * tm + _iota((tm, 1), 0)
        valid = (row >= row_lo) & (row < row_hi)

        def l2_bwd(c1, dn):
            x1 = _silu(c1)
            r = lax.rsqrt(jnp.sum(x1 * x1, axis=-1, keepdims=True) + L2_EPS)
            dn = jnp.where(valid, dn, 0.0)
            d1 = r * dn - x1 * (r * r * r) * jnp.sum(dn * x1, axis=-1, keepdims=True)
            return d1 * _silu_grad(c1)

        _stage(bq_ref, jnp.where(first, 0.0, pqp_ref[...]), pq_ref[...])
        _stage(bk_ref, jnp.where(first, 0.0, pkp_ref[...]), pk_ref[...])
        _stage(bv_ref, jnp.where(first, 0.0, pvp_ref[...]), pv_ref[...])
        dcq_ref[...] = l2_bwd(_conv_taps(bq_ref, tm, cq_ref), dqn_ref[...])
        dck_ref[...] = l2_bwd(_conv_taps(bk_ref, tm, ck_ref), dkn_ref[...])
        cv1 = _conv_taps(bv_ref, tm, cv_ref)
        dcv_ref[...] = jnp.where(valid, dvv_ref[...], 0.0) * _silu_grad(cv1)
        ab = ab_ref[...]
        da = _lane_pick(ab, h)
        db = _lane_pick(ab, heads + h)
        a = _lane_pick(al_ref[...], h)
        dtb = _lane_pick(dt_ref[...], h)
        dgv = jnp.where(valid, dg_ref[...], 0.0)
        ea = jnp.exp(a)
        g = -ea * _softplus(da + dtb)
        dda = dgv * (-ea) * _sigmoid(da + dtb)
        beta = _sigmoid(db)
        ddb = jnp.where(valid, db_ref[...], 0.0) * beta * (1.0 - beta)
        lane = _iota((tm, LANES), 1)
        dab_ref[...] += jnp.where(lane == h, dda, 0.0) + jnp.where(lane == heads + h, ddb, 0.0)
        lane1 = _iota((1, LANES), 1)
        dal_ref[...] += jnp.where(lane1 == h, jnp.sum(dgv * g, axis=0, keepdims=True), 0.0)
        ddt_ref[...] += jnp.where(lane1 == h, jnp.sum(dda, axis=0, keepdims=True), 0.0)

    def cur(width, col):
        return pl.BlockSpec((tm, width), lambda i, h: (i, col // width + h))

    def prev(width, col):
        return pl.BlockSpec((8, width), lambda i, h: (jnp.maximum(i * (tm // 8) - 1, 0), col // width + h))

    def hd(width):
        return pl.BlockSpec((None, tm, width), lambda i, h: (h, i, 0))

    small = pl.BlockSpec((1, LANES), lambda i, h: (0, 0))
    return pl.pallas_call(
        body, grid=(nb, heads),
        in_specs=[cur(dk, col_q), prev(dk, col_q), cur(dk, col_k), prev(dk, col_k), cur(dv, col_v), prev(dv, col_v),
                  pl.BlockSpec((tm, LANES), lambda i, h: (i, 0)),
                  pl.BlockSpec((cq.shape[0], dk), lambda i, h: (0, h)), pl.BlockSpec((ck.shape[0], dk), lambda i, h: (0, h)),
                  pl.BlockSpec((cv.shape[0], dv), lambda i, h: (0, h)), small, small,
                  hd(dk), hd(dk), hd(dv), hd(1), hd(1)],
        out_specs=[hd(dk), hd(dk), hd(dv), pl.BlockSpec((tm, LANES), lambda i, h: (i, 0)), small, small],
        out_shape=[jax.ShapeDtypeStruct((heads, t, dk), F32), jax.ShapeDtypeStruct((heads, t, dk), F32),
                   jax.ShapeDtypeStruct((heads, t, dv), F32), jax.ShapeDtypeStruct((t, LANES), F32),
                   jax.ShapeDtypeStruct((1, LANES), F32), jax.ShapeDtypeStruct((1, LANES), F32)],
        scratch_shapes=[pltpu.VMEM((HALO + tm, dk), F32), pltpu.VMEM((HALO + tm, dk), F32),
                        pltpu.VMEM((HALO + tm, dv), F32)], name=name,
        compiler_params=_params("arbitrary", "arbitrary"))(
            proj, proj, proj, proj, proj, proj, pab, cq, ck, cv, a_log, dt_bias, dqn, dkn, dvv, dg, dbeta)


def _conv_bwd(proj, dc, cw, *, heads, width, col, name):
    t = proj.shape[0]
    tm = _pick(t, (384, 256, 128))
    nb = t // tm
    nk = cw.shape[0]

    def body(p_ref, pp_ref, d_ref, dn_ref, w_ref, dp_ref, dw_ref, bx_ref, bd_ref):
        i = pl.program_id(1)
        first, last = i == 0, i == nb - 1

        @pl.when(first)
        def _():
            dw_ref[...] = jnp.zeros_like(dw_ref)

        d = d_ref[...]
        _stage(bx_ref, jnp.where(first, 0.0, pp_ref[...]), p_ref[...])
        _stage_after(bd_ref, d, jnp.where(last, 0.0, dn_ref[...]))
        dx = d * w_ref[nk - 1:nk, :]
        dw_ref[nk - 1:nk, :] += jnp.sum(d * p_ref[...], axis=0, keepdims=True)
        for s in range(1, nk):
            dx += bd_ref[s:s + tm, :] * w_ref[nk - 1 - s:nk - s, :]
            dw_ref[nk - 1 - s:nk - s, :] += jnp.sum(d * bx_ref[HALO - s:HALO - s + tm, :], axis=0, keepdims=True)
        dp_ref[...] = _bf(dx)

    return pl.pallas_call(
        body, grid=(heads, nb),
        in_specs=[pl.BlockSpec((tm, width), lambda h, i: (i, col // width + h)),
                  pl.BlockSpec((8, width), lambda h, i: (jnp.maximum(i * (tm // 8) - 1, 0), col // width + h)),
                  pl.BlockSpec((None, tm, width), lambda h, i: (h, i, 0)),
                  pl.BlockSpec((None, 8, width), lambda h, i: (h, jnp.minimum((i + 1) * (tm // 8), t // 8 - 1), 0)),
                  pl.BlockSpec((nk, width), lambda h, i: (0, h))],
        out_specs=[pl.BlockSpec((tm, width), lambda h, i: (i, h)), pl.BlockSpec((nk, width), lambda h, i: (0, h))],
        out_shape=[jax.ShapeDtypeStruct((t, heads * width), BF16), jax.ShapeDtypeStruct((nk, heads * width), F32)],
        scratch_shapes=[pltpu.VMEM((HALO + tm, width), F32), pltpu.VMEM((tm + HALO, width), F32)],
        name=name, compiler_params=_params("parallel", "arbitrary"))(proj, proj, dc, dc, cw)


def _sb_pre(proj, gq, gk, *, heads, dh, col_q, col_k, col_v, name):
    t = proj.shape[0]
    tm = _pick(t, (384, 256, 128))

    def body(q_ref, k_ref, v_ref, gq_ref, gk_ref, qo_ref, ko_ref, vo_ref):
        qo_ref[...] = _bf(_rms_fwd(q_ref[...], gq_ref[...]))
        ko_ref[...] = _bf(_rms_fwd(k_ref[...], gk_ref[...]))
        vo_ref[...] = _bf(v_ref[...])

    def cur(col):
        return pl.BlockSpec((tm, dh), lambda i, h: (i, col // dh + h))

    gspec = pl.BlockSpec((1, dh), lambda i, h: (0, 0))
    ospec = pl.BlockSpec((tm, dh), lambda i, h: (i, h))
    return pl.pallas_call(
        body, grid=(t // tm, heads), in_specs=[cur(col_q), cur(col_k), cur(col_v), gspec, gspec],
        out_specs=[ospec] * 3, out_shape=[jax.ShapeDtypeStruct((t, heads * dh), BF16)] * 3, name=name,
        compiler_params=_params("parallel", "parallel"))(proj, proj, proj, gq, gk)


def _sb_tile(z, i, j, blk, key_lo, masked):
    ls = jnp.minimum(z, 0.0) - jnp.log(1.0 + jnp.exp(-jnp.abs(z)))
    if not masked:
        return None, ls, ls - z
    qpos = i * blk + _iota((blk, blk), 0)
    kpos = j * blk + _iota((blk, blk), 1)
    vis = (kpos < qpos) & (kpos >= key_lo)
    return vis, ls, jnp.where(vis, ls - z, 0.0)


def _where_vis(vis, x):
    return x if vis is None else jnp.where(vis, x, 0.0)


def _sb_sweep(i, step, init, descending):
    first, last = (i, 0) if descending else (0, i)
    carry = step(first, init, True)
    carry = lax.fori_loop(1, i, lambda n, c: step(i - n if descending else n, c, False), carry)
    return lax.cond(i > 0, lambda c: step(last, c, True), lambda c: c, carry)


def _dot2_r(x, m):
    hi, lo = _split2(x)
    return _dot(hi, m) + _dot(lo, m)


def _running_sums(x, tri, reverse, exact=True):
    groups = [x[:, s:s + LANES] for s in range(0, x.shape[1], LANES)]
    inside = [_dot2_r(g, tri) if exact else _dot(_bf(g), tri) for g in groups]
    sums = [jnp.sum(g, axis=1, keepdims=True) for g in groups]
    order = list(reversed(range(len(groups)))) if reverse else list(range(len(groups)))
    out, acc = [None] * len(groups), None
    for gi in order:
        out[gi] = inside[gi] if acc is None else inside[gi] + acc
        acc = sums[gi] if acc is None else acc + sums[gi]
    return jnp.concatenate(out, axis=1), acc


def _sb_fwd(qs, ks, vs, *, heads, dh, key_lo, name, gather=None):
    t = qs.shape[0]
    blk = _pick(t, (3 * SB_BLOCK, 2 * SB_BLOCK, SB_BLOCK))
    assert key_lo <= blk
    nq = t // blk
    assert nq <= LANES
    scale = dh ** -0.5
    hp = SB_HEADS_PER_STEP

    def body(q_ref, k_ref, v_ref, o_ref, c_ref):
        i = pl.program_id(1)
        later = jnp.where(_iota((LANES, LANES), 0) > _iota((LANES, LANES), 1), 1.0, 0.0).astype(BF16)
        lane = _iota((blk, LANES), 1)
        c_ref[...] = jnp.zeros_like(c_ref)

        def step(j, carry, masked):
            rows = pl.ds(pl.multiple_of(j * blk, blk), blk)
            hs = range(hp)
            cols = [pl.ds(hh * dh, dh) for hh in hs]
            zs = [_dot_nt(q_ref[:, cols[hh]], k_ref[rows, cols[hh]]) * scale for hh in hs]
            tiles = [_sb_tile(z, i, j, blk, key_lo, masked) for z in zs]
            sufs = [_running_sums(lk, later, reverse=True) for _, _, lk in tiles]
            wgts = [_where_vis(vis, jnp.exp(ls + suf + carry[2 * hh + 1]))
                    for hh, ((vis, ls, _), (suf, _)) in enumerate(zip(tiles, sufs))]
            accs = [carry[2 * hh] + _dot(_bf(wgts[hh]), v_ref[rows, cols[hh]]) for hh in hs]
            out = []
            for hh in hs:
                c_ref[hh] = jnp.where(lane == j, carry[2 * hh + 1], c_ref[hh])
                out += [accs[hh], carry[2 * hh + 1] + sufs[hh][1]]
            return tuple(out)

        res = _sb_sweep(i, step, (jnp.zeros((blk, dh), F32), jnp.zeros((blk, 1), F32)) * hp, descending=True)
        for hh in range(hp):
            o_ref[:, pl.ds(hh * dh, dh)] = _bf(res[2 * hh])

    full = pl.BlockSpec((t, hp * dh), lambda h, i: (0, h))
    call = dict(grid=(heads // hp, nq),
                in_specs=[pl.BlockSpec((blk, hp * dh), lambda h, i: (i, h)), full, full],
                out_specs=[pl.BlockSpec((blk, hp * dh), lambda h, i: (i, h)),
                           pl.BlockSpec((hp, blk, LANES), lambda h, i: (h, i, 0))],
                out_shape=[jax.ShapeDtypeStruct((t, heads * dh), BF16), jax.ShapeDtypeStruct((heads, t, LANES), F32)],
                name=name)
    if gather is None:
        return pl.pallas_call(body, compiler_params=_params("parallel", "parallel"), **call)(qs, ks, vs)
    res = _call_with_exchange(body, scratch_shapes=[], args=(qs, ks, vs), srcs=gather, scatter=[False] * len(gather),
                              **call)
    return [*res[:2], list(res[2:])]


def _sb_bwd(qs, ks, vs, do, carry, *, heads, dh, key_lo, name, scatter=None):
    t = qs.shape[0]
    blk = _pick(t, (3 * SB_BLOCK, 2 * SB_BLOCK, SB_BLOCK))
    assert key_lo <= blk
    nq = t // blk
    scale = dh ** -0.5
    hp = SB_HEADS_PER_STEP

    def body(q_ref, k_ref, v_ref, do_ref, c_ref, dq_ref, dk_ref, dv_ref):
        i = pl.program_id(1)

        @pl.when(i == 0)
        def _():
            dk_ref[...] = jnp.zeros_like(dk_ref)
            dv_ref[...] = jnp.zeros_like(dv_ref)

        r0 = _iota((LANES, LANES), 0)
        r1 = _iota((LANES, LANES), 1)
        later = jnp.where(r0 > r1, 1.0, 0.0).astype(BF16)
        earlier = jnp.where(r0 < r1, 1.0, 0.0).astype(BF16)

        def step(j, carry, masked):
            rows = pl.ds(pl.multiple_of(j * blk, blk), blk)
            hs = range(hp)
            cols = [pl.ds(hh * dh, dh) for hh in hs]
            zs = [_dot_nt(q_ref[:, cols[hh]], k_ref[rows, cols[hh]]) * scale for hh in hs]
            dws = [_dot_nt(do_ref[:, cols[hh]], v_ref[rows, cols[hh]]) for hh in hs]
            tiles = [_sb_tile(z, i, j, blk, key_lo, masked) for z in zs]
            sufs = [_running_sums(lk, later, reverse=True)[0] for _, _, lk in tiles]
            wgts = [_where_vis(vis, jnp.exp(ls + suf + _lane_pick(c_ref[hh], j)))
                    for hh, ((vis, ls, _), suf) in enumerate(zip(tiles, sufs))]
            es = [wgt * dw for wgt, dw in zip(wgts, dws)]
            pres = [_running_sums(e, earlier, reverse=False, exact=False) for e in es]
            dzs = []
            for hh in hs:
                vis, ls, _ = tiles[hh]
                before = _where_vis(vis, pres[hh][0] + carry[2 * hh + 1])
                sig = jnp.exp(ls)
                dzs.append(_bf((es[hh] * (1.0 - sig) - before * sig) * scale))
            dks = [_dot_tn(dzs[hh], q_ref[:, cols[hh]]) for hh in hs]
            dvs = [_dot_tn(_bf(wgts[hh]), do_ref[:, cols[hh]]) for hh in hs]
            dqs = [carry[2 * hh] + _dot(dzs[hh], k_ref[rows, cols[hh]]) for hh in hs]
            out = []
            for hh in hs:
                dk_ref[rows, cols[hh]] += dks[hh]
                dv_ref[rows, cols[hh]] += dvs[hh]
                out += [dqs[hh], carry[2 * hh + 1] + pres[hh][1]]
            return tuple(out)

        res = _sb_sweep(i, step, (jnp.zeros((blk, dh), F32), jnp.zeros((blk, 1), F32)) * hp, descending=False)
        for hh in range(hp):
            dq_ref[:, pl.ds(hh * dh, dh)] = res[2 * hh]

    full = pl.BlockSpec((t, hp * dh), lambda h, i: (0, h))
    qblk = pl.BlockSpec((blk, hp * dh), lambda h, i: (i, h))
    call = dict(grid=(heads // hp, nq),
                in_specs=[qblk, full, full, qblk, pl.BlockSpec((hp, blk, LANES), lambda h, i: (h, i, 0))],
                out_specs=[qblk, full, full], out_shape=[jax.ShapeDtypeStruct((t, heads * dh), F32)] * 3, name=name)
    if scatter is None:
        return pl.pallas_call(body, compiler_params=_params("parallel", "arbitrary"), **call)(qs, ks, vs, do, carry)
    res = _call_with_exchange(body, scratch_shapes=[], args=(qs, ks, vs, do, carry), srcs=scatter,
                              scatter=[True] * len(scatter), **call)
    return [*res[:3], list(res[3:])]


def _sb_pre_bwd(proj, gq, gk, dq, dk, dv, *, heads, dh, col_q, col_k, name):
    t = proj.shape[0]
    tm = _pick(t, (384, 256, 128))

    def body(q_ref, k_ref, gq_ref, gk_ref, dq_ref, dk_ref, dv_ref, oq_ref, ok_ref, ov_ref, dgq_ref, dgk_ref):
        @pl.when((pl.program_id(0) == 0) & (pl.program_id(1) == 0))
        def _():
            dgq_ref[...] = jnp.zeros_like(dgq_ref)
            dgk_ref[...] = jnp.zeros_like(dgk_ref)

        dq_, gq_r = _rms_bwd(q_ref[...], gq_ref[...], dq_ref[...])
        dk_, gk_r = _rms_bwd(k_ref[...], gk_ref[...], dk_ref[...])
        oq_ref[...] = _bf(dq_)
        ok_ref[...] = _bf(dk_)
        ov_ref[...] = _bf(dv_ref[...])
        dgq_ref[...] += jnp.sum(gq_r, axis=0, keepdims=True)
        dgk_ref[...] += jnp.sum(gk_r, axis=0, keepdims=True)

    def cur(col):
        return pl.BlockSpec((tm, dh), lambda i, h: (i, col // dh + h))

    gspec = pl.BlockSpec((1, dh), lambda i, h: (0, 0))
    ospec = pl.BlockSpec((tm, dh), lambda i, h: (i, h))
    return pl.pallas_call(
        body, grid=(t // tm, heads), in_specs=[cur(col_q), cur(col_k), gspec, gspec, ospec, ospec, ospec],
        out_specs=[ospec, ospec, ospec, gspec, gspec],
        out_shape=[jax.ShapeDtypeStruct((t, heads * dh), BF16)] * 3 + [jax.ShapeDtypeStruct((1, dh), F32)] * 2,
        name=name, compiler_params=_params("arbitrary", "arbitrary"))(proj, proj, gq, gk, dq, dk, dv)


PEERS = N_DEV - 1


def _exchange_copies(ins, outs, send_sems, recv_sems, local_sems, scatter):
    x, y, c = lax.axis_index("x"), lax.axis_index("y"), lax.axis_index("c")
    me = 4 * x + 2 * y + c
    copies = []
    for a in range(len(ins)):
        own = ins[a].at[me] if scatter[a] else ins[a]
        copies.append(pltpu.make_async_copy(own, outs[a].at[me], local_sems.at[a]))
        for k in range(1, N_DEV):
            px = (x + (k >> 2 & 1)) % 2
            py = (y + (k >> 1 & 1)) % 2
            pc = (c + (k & 1)) % 2
            src = ins[a].at[4 * px + 2 * py + pc] if scatter[a] else ins[a]
            copies.append(pltpu.make_async_remote_copy(
                src_ref=src, dst_ref=outs[a].at[me], send_sem=send_sems.at[a * PEERS + k - 1],
                recv_sem=recv_sems.at[a * PEERS + k - 1], device_id=(px, py, pc), device_id_type=MESH))
    return copies


def _exchange_shapes(srcs, scatter):
    return [jax.ShapeDtypeStruct(s.shape if sc else (N_DEV,) + s.shape, s.dtype) for s, sc in zip(srcs, scatter)]


def _exchange_sems(n):
    return [pltpu.SemaphoreType.DMA((n * PEERS,)), pltpu.SemaphoreType.DMA((n * PEERS,)), pltpu.SemaphoreType.DMA((n,))]


def _exchange(srcs, *, scatter, name):
    n = len(srcs)

    def body(*refs):
        copies = _exchange_copies(refs[:n], refs[n:2 * n], *refs[2 * n:], scatter)
        for cp in copies:
            cp.start()
        for cp in copies:
            cp.wait()

    any_spec = pl.BlockSpec(memory_space=pl.ANY)
    return pl.pallas_call(
        body, in_specs=[any_spec] * n, out_specs=[any_spec] * n, out_shape=_exchange_shapes(srcs, scatter),
        scratch_shapes=_exchange_sems(n), name=name,
        compiler_params=pltpu.CompilerParams(has_side_effects=True))(*srcs)


def _gather_two_level(srcs, *, name):
    n = len(srcs)

    def body(*refs):
        ins, outs = refs[:n], refs[n:2 * n]
        send_sems, recv_sems, local_sems = refs[2 * n:]
        x, y, c = lax.axis_index("x"), lax.axis_index("y"), lax.axis_index("c")
        chips = [(1 - x, y), (x, 1 - y), (1 - x, 1 - y)]

        def slab(a, px, py, pc):
            return outs[a].at[4 * px + 2 * py + pc]

        def copy(a, k, block, to, src=None):
            return pltpu.make_async_remote_copy(
                src_ref=slab(a, *block) if src is None else src, dst_ref=slab(a, *block),
                send_sem=send_sems.at[a * PEERS + k], recv_sem=recv_sems.at[a * PEERS + k],
                device_id=to, device_id_type=MESH)

        mine = [pltpu.make_async_copy(ins[a], slab(a, x, y, c), local_sems.at[a]) for a in range(n)]
        first = [copy(a, 0, (x, y, c), (x, y, 1 - c), src=ins[a]) for a in range(n)]
        first += [copy(a, 1 + j, (x, y, c), (*chip, c), src=ins[a]) for j, chip in enumerate(chips) for a in range(n)]
        for cp in mine + first:
            cp.start()
        passed = []
        for j, chip in enumerate(chips):
            for a in range(n):
                copy(a, 1 + j, (*chip, c), (x, y, c)).wait_recv()
                passed.append(copy(a, 4 + j, (*chip, c), (x, y, 1 - c)))
                passed[-1].start()
        for a in range(n):
            copy(a, 0, (x, y, 1 - c), (x, y, c)).wait_recv()
            for j, chip in enumerate(chips):
                copy(a, 4 + j, (*chip, 1 - c), (x, y, c)).wait_recv()
        for cp in first + passed:
            cp.wait_send()
        for cp in mine:
            cp.wait()

    any_spec = pl.BlockSpec(memory_space=pl.ANY)
    return pl.pallas_call(
        body, in_specs=[any_spec] * n, out_specs=[any_spec] * n, out_shape=_exchange_shapes(srcs, [False] * n),
        scratch_shapes=_exchange_sems(n), name=name,
        compiler_params=pltpu.CompilerParams(has_side_effects=True))(*srcs)


def _call_with_exchange(body, *, grid, in_specs, out_specs, out_shape, scratch_shapes, args, srcs, scatter, name):
    n, n_in, n_out, n_scr = len(srcs), len(args), len(out_shape), len(scratch_shapes)

    def full_body(*refs):
        ins, xin = refs[:n_in], refs[n_in:n_in + n]
        outs, xout = refs[n_in + n:n_in + n + n_out], refs[n_in + n + n_out:n_in + 2 * n + n_out]
        scr = refs[n_in + 2 * n + n_out:]
        ids = [pl.program_id(a) for a in range(len(grid))]
        first = functools.reduce(jnp.logical_and, [i == 0 for i in ids])
        last = functools.reduce(jnp.logical_and, [i == g - 1 for i, g in zip(ids, grid)])
        copies = _exchange_copies(xin, xout, *scr[n_scr:], scatter)

        @pl.when(first)
        def _():
            for cp in copies:
                cp.start()

        body(*ins, *outs, *scr[:n_scr])

        @pl.when(last)
        def _():
            for cp in copies:
                cp.wait()

    any_spec = pl.BlockSpec(memory_space=pl.ANY)
    return pl.pallas_call(
        full_body, grid=grid, in_specs=list(in_specs) + [any_spec] * n, out_specs=list(out_specs) + [any_spec] * n,
        out_shape=list(out_shape) + _exchange_shapes(srcs, scatter),
        scratch_shapes=list(scratch_shapes) + _exchange_sems(n), name=name,
        compiler_params=pltpu.CompilerParams(dimension_semantics=("arbitrary",) * len(grid),
                                             vmem_limit_bytes=V7X_VMEM_LIMIT_BYTES, has_side_effects=True))(*args, *srcs)


def _adam_math(g, w, m, v):
    m2 = ADAM_B1 * m + (1.0 - ADAM_B1) * g
    v2 = ADAM_B2 * v + (1.0 - ADAM_B2) * (g * g)
    m_hat = m2 / (1.0 - ADAM_B1 ** ADAM_STEP)
    v_hat = v2 / (1.0 - ADAM_B2 ** ADAM_STEP)
    return -ADAM_LR * (m_hat / (jnp.sqrt(v_hat) + ADAM_EPS) + ADAM_WD * w), m2, v2


def _adamw_slabs(slabs, w, m, v, *, name):
    r, c = w.shape
    tr = next((t for t in (2256, 752, 512, 240, 128, 64, 32, 16) if r % t == 0), r)

    def body(s_ref, w_ref, m_ref, v_ref, g_ref, d_ref, mo_ref, vo_ref):
        g = s_ref[0].astype(F32)
        for p in range(1, N_DEV):
            g = g + s_ref[p].astype(F32)
        g_ref[...] = g
        d_ref[...], mo_ref[...], vo_ref[...] = _adam_math(g, w_ref[...], m_ref[...], v_ref[...])

    spec = pl.BlockSpec((tr, c), lambda i: (i, 0))
    return pl.pallas_call(
        body, grid=(r // tr,), in_specs=[pl.BlockSpec((N_DEV, tr, c), lambda i: (0, i, 0)), spec, spec, spec],
        out_specs=[spec] * 4, out_shape=[jax.ShapeDtypeStruct((r, c), F32)] * 4, name=name,
        compiler_params=_params("parallel"))(slabs, w, m, v)


def _adamw_small(g, w, m, v, *, name):
    def body(g_ref, w_ref, m_ref, v_ref, d_ref, mo_ref, vo_ref):
        d_ref[...], mo_ref[...], vo_ref[...] = _adam_math(g_ref[...], w_ref[...], m_ref[...], v_ref[...])

    return pl.pallas_call(body, out_shape=[jax.ShapeDtypeStruct(w.shape, F32)] * 3, name=name)(g, w, m, v)


def _sum_slabs(slabs, *, name):
    def body(s_ref, o_ref):
        acc = s_ref[0]
        for p in range(1, N_DEV):
            acc = acc + s_ref[p]
        o_ref[...] = acc

    return pl.pallas_call(body, out_shape=jax.ShapeDtypeStruct(slabs.shape[1:], F32), name=name)(slabs)


def _pad_lanes(a):
    return jnp.pad(a, ((0, 0), (0, LANES - a.shape[1])))


def _w_in_slabs(main, ab, col_ab, n_ab):
    pw = main.shape[0] + n_ab
    c = pw // N_DEV
    parts = [(0, col_ab, main, 0), (col_ab, col_ab + n_ab, ab, col_ab), (col_ab + n_ab, pw, main, n_ab)]
    slabs = []
    for p in range(N_DEV):
        pieces = []
        for lo, hi, src, shift in parts:
            a, b = max(lo, c * p), min(hi, c * (p + 1))
            if a < b:
                pieces.append(src[a - shift:b - shift])
        slabs.append(jnp.concatenate(pieces, axis=0))
    return jnp.stack(slabs)


def _local_step(x, target, meta, g_mix, wt_main, wt_ab, cq, ck, cv, a_log, dt_bias, g_dn, g_sbq, g_sbk, g_ffn, rest,
                shards=False):
    seq, d = x.shape
    n_meta = meta.shape[0]
    heads = a_log.shape[1]
    qk = cq.shape[1]
    dvt = cv.shape[1]
    dk, dv = qk // heads, dvt // heads
    dh = g_sbq.shape[1]
    sbw = rest[2].shape[0] * N_DEV if shards else rest[1].shape[0]
    sb_heads = sbw // dh
    pad_l = (-n_meta) % CHUNK
    row_x = pad_l + n_meta
    rows = row_x + seq
    t = -(-rows // GDN_ROWS) * GDN_ROWS
    col_q, col_k, col_v, col_z = 0, qk, 2 * qk, 2 * qk + dvt
    col_sq = 2 * qk + 2 * dvt
    col_sk, col_sv, col_gd, col_gs = col_sq + sbw, col_sq + 2 * sbw, col_sq + 3 * sbw, col_sq + 3 * sbw + d

    def rows_pad(a):
        return jnp.concatenate([jnp.zeros((row_x, d), F32), a, jnp.zeros((t - rows, d), F32)], axis=0)

    h0 = jnp.concatenate([jnp.zeros((pad_l, d), F32), meta, x, jnp.zeros((t - rows, d), F32)], axis=0)
    tgt = rows_pad(target)
    a_log_p, dt_p = _pad_lanes(a_log), _pad_lanes(dt_bias)

    proj, n1 = _mm_norm(h0, g_mix, wt_main, name="proj")
    pab = _mm_nt(n1, wt_ab, out_dtype=F32, name="proj_ab")
    gk = dict(heads=heads, dk=dk, dv=dv, col_q=col_q, col_k=col_k, col_v=col_v, row_lo=pad_l, row_hi=rows)
    qn, kn, vv, g, beta = _gdn_pre(proj, pab, cq, ck, cv, a_log_p, dt_p, name="gdn_pre", **gk)
    u, w, pm, qd, kd, egl, tinv = _gdn_prep(qn, kn, vv, g, beta, name="gdn_prep")
    o_raw, o_dn, states = _gdn_scan(u, w, pm, qd, kd, egl, proj, g_dn, col_z=col_z, name="gdn_scan")
    qs, ks, vs = _sb_pre(proj, g_sbq, g_sbk, heads=sb_heads, dh=dh, col_q=col_sq, col_k=col_sk, col_v=col_sv,
                         name="sb_pre")
    if shards:
        o_sb, carry, (g_fi, g_bd, g_bs, g_out, g_fo) = _sb_fwd(qs, ks, vs, heads=sb_heads, dh=dh, key_lo=pad_l,
                                                                name="sb_fwd", gather=list(rest))
        wt_fi = g_fi.reshape(-1, d)
        d_ff = wt_fi.shape[0] // 2
        w_bd, w_bs, w_out, wt_fg, wt_fu, w_fo = (g_bd.reshape(-1, d), g_bs.reshape(-1, d), g_out.reshape(-1, d),
                                                 wt_fi[:d_ff], wt_fi[d_ff:], g_fo.reshape(-1, d))
    else:
        o_sb, carry = _sb_fwd(qs, ks, vs, heads=sb_heads, dh=dh, key_lo=pad_l, name="sb_fwd")
        w_bd, w_bs, w_out, wt_fg, wt_fu, w_fo = rest
    merged, br_dn, br_sb = _merge_fwd(o_dn, o_sb, w_bd, w_bs, proj, col_gd=col_gd, col_gs=col_gs, name="merge")
    h1 = _mm_res(h0, merged, w_out, name="mix_out")
    gate, up, act, n2 = _mm_norm_swiglu(h1, g_ffn, wt_fg, wt_fu, name="ffn_in")
    dy, dyb, lsum = _mm_res_loss(h1, act, w_fo, tgt, row0=row_x, nrows=seq, name="ffn_out_loss")

    dgate, dup = _swiglu_bwd(dyb, w_fo, gate, up, name="ffn_out_bwd")
    d_w_fo = _mm_tn(act, dyb, name="dw_ffn_out")
    d_wt_fg = _mm_tn(dgate, n2, name="dw_ffn_gate")
    d_wt_fu = _mm_tn(dup, n2, name="dw_ffn_up")
    dh1, dh1b, d_g_ffn = _mm_rmsbwd([(dgate, wt_fg), (dup, wt_fu)], None, h1, g_ffn, dy, name="ffn_in_bwd")

    dbd, dbs, dgd, dgs = _merge_bwd(dh1b, w_out, proj, br_dn, br_sb, col_gd=col_gd, col_gs=col_gs, name="mix_out_bwd")
    d_w_out = _mm_tn(merged, dh1b, name="dw_out")
    d_w_bd = _mm_tn(o_dn, dbd, name="dw_branch_dn")
    d_w_bs = _mm_tn(o_sb, dbs, name="dw_branch_sb")
    do_dn = _mm_nt(dbd, w_bd, out_dtype=F32, name="branch_dn_bwd")
    do_sb = _mm_nt(dbs, w_bs, out_dtype=BF16, name="branch_sb_bwd")

    do_raw, dz, d_g_dn = _gdn_post_bwd(o_raw, proj, g_dn, do_dn, col_z=col_z, name="gdn_post_bwd")
    du, dw, dp, dqd, dkd, dgl = _gdn_bwd_scan(u, w, pm, qd, kd, egl, states, do_raw, name="gdn_bwd_scan")
    dqn, dkn, dvv, dg, dbeta = _gdn_bwd_prep(qn, kn, vv, g, beta, tinv, u, w, du, dw, dp, dqd, dkd, dgl,
                                            name="gdn_bwd_prep")
    dcq, dck, dcv, dpab, d_a_log, d_dt = _gdn_pre_bwd_a(proj, pab, cq, ck, cv, a_log_p, dt_p, dqn, dkn, dvv, dg, dbeta,
                                                        name="gdn_pre_bwd", **gk)
    dpq, d_cq = _conv_bwd(proj, dcq, cq, heads=heads, width=dk, col=col_q, name="conv_q_bwd")
    dpk, d_ck = _conv_bwd(proj, dck, ck, heads=heads, width=dk, col=col_k, name="conv_k_bwd")
    dpv, d_cv = _conv_bwd(proj, dcv, cv, heads=heads, width=dv, col=col_v, name="conv_v_bwd")

    early = None
    if shards:
        slabs = [_bf(jnp.concatenate([d_wt_fg, d_wt_fu], axis=0)).reshape(N_DEV, -1, LANES)]
        slabs += [_bf(a).reshape(N_DEV, -1, d) for a in (d_w_bd, d_w_bs, d_w_out, d_w_fo)]
        dqs, dks, dvs, early = _sb_bwd(qs, ks, vs, do_sb, carry, heads=sb_heads, dh=dh, key_lo=pad_l, name="sb_bwd",
                                       scatter=slabs)
    else:
        dqs, dks, dvs = _sb_bwd(qs, ks, vs, do_sb, carry, heads=sb_heads, dh=dh, key_lo=pad_l, name="sb_bwd")
    dsq, dsk, dsv, d_g_sbq, d_g_sbk = _sb_pre_bwd(proj, g_sbq, g_sbk, dqs, dks, dvs, heads=sb_heads, dh=dh,
                                                   col_q=col_sq, col_k=col_sk, name="sb_pre_bwd")

    dproj = [dpq, dpk, dpv, dz, dsq, dsk, dsv, dgd, dgs]
    dpab_b = _bf(dpab)
    d_wt_main = _mm_tn(dproj, n1, name="dw_in_main")
    d_wt_ab = _mm_tn(dpab_b, n1, name="dw_in_ab")
    s_in = None
    if shards:
        slabs = _w_in_slabs(_bf(d_wt_main), _bf(d_wt_ab), col_sq, 2 * heads).reshape(N_DEV, -1, LANES)
        dh0, _, d_g_mix, (s_in,) = _mm_rmsbwd([(dproj, wt_main)], (dpab_b, wt_ab), h0, g_mix, dh1, name="proj_bwd",
                                              scatter=[slabs])
    else:
        dh0, _, d_g_mix = _mm_rmsbwd([(dproj, wt_main)], (dpab_b, wt_ab), h0, g_mix, dh1, name="proj_bwd")

    return dict(s_in=s_in, lsum=lsum, grad_x=dh0[row_x:rows], d_meta=dh0[pad_l:row_x], d_g_mix=d_g_mix,
                d_wt_main=d_wt_main, d_wt_ab=d_wt_ab, d_cq=d_cq, d_ck=d_ck, d_cv=d_cv, d_a_log=d_a_log[:, :heads],
                d_dt=d_dt[:, :heads], d_g_dn=d_g_dn, d_g_sbq=d_g_sbq, d_g_sbk=d_g_sbk, d_w_bd=d_w_bd, d_w_bs=d_w_bs,
                d_w_out=d_w_out, d_g_ffn=d_g_ffn, d_wt_fg=d_wt_fg, d_wt_fu=d_wt_fu, d_w_fo=d_w_fo, early=early)


def _pack(parts):
    flat = []
    for a in parts:
        a = a.reshape(-1)
        flat.append(jnp.pad(a, (0, (-a.shape[0]) % LANES)))
    v = jnp.concatenate(flat)
    v = jnp.pad(v, (0, (-v.shape[0]) % (8 * LANES)))
    return v.reshape(-1, LANES)


def _unpack(packed, shapes):
    flat = packed.reshape(-1)
    out, pos = [], 0
    for s in shapes:
        n = math.prod(s)
        out.append(flat[pos:pos + n].reshape(s))
        pos += n + (-n) % LANES
    return out


def kernel(x, meta_tokens, norm_mix_gain, w_in, conv_q, conv_k, conv_v, dn_a_log, dn_dt_bias, dn_out_norm_gain, sb_q_norm_gain, sb_k_norm_gain, w_branch_dn, w_branch_sb, w_out, norm_ffn_gain, w_ffn_in, w_ffn_out, loss_target, m_meta_tokens, m_norm_mix_gain, m_w_in, m_conv_q, m_conv_k, m_conv_v, m_dn_a_log, m_dn_dt_bias, m_dn_out_norm_gain, m_sb_q_norm_gain, m_sb_k_norm_gain, m_w_branch_dn, m_w_branch_sb, m_w_out, m_norm_ffn_gain, m_w_ffn_in, m_w_ffn_out, v_meta_tokens, v_norm_mix_gain, v_w_in, v_conv_q, v_conv_k, v_conv_v, v_dn_a_log, v_dn_dt_bias, v_dn_out_norm_gain, v_sb_q_norm_gain, v_sb_k_norm_gain, v_w_branch_dn, v_w_branch_sb, v_w_out, v_norm_ffn_gain, v_w_ffn_in, v_w_ffn_out):
    me = 4 * lax.axis_index("x") + 2 * lax.axis_index("y") + lax.axis_index("c")
    heads = dn_a_log.shape[1]
    d = x.shape[2]
    qk = conv_q.shape[2] * N_DEV
    dvt = conv_v.shape[2] * N_DEV
    col_ab = 2 * qk + 2 * dvt

    small_shapes = [meta_tokens.shape, conv_q.shape[1:], conv_k.shape[1:], conv_v.shape[1:]]
    small = _pack([meta_tokens, conv_q[0], conv_k[0], conv_v[0]])
    def features_major(a):
        return jnp.transpose(a, (2, 0, 1)).reshape(a.shape[2], a.shape[1])

    g_in, g_small = _gather_two_level([_bf(features_major(w_in)), small], name="gather_w_in")
    wt_full = g_in.reshape(-1, d)
    wt_main = jnp.concatenate([wt_full[:col_ab], wt_full[col_ab + 2 * heads:]], axis=0)
    wt_ab = jnp.pad(wt_full[col_ab:col_ab + 2 * heads], ((0, LANES - 2 * heads), (0, 0)))
    parts = [_unpack(g_small[p], small_shapes) for p in range(N_DEV)]
    meta_f, cq_f, ck_f, cv_f = (jnp.concatenate([parts[p][a] for p in range(N_DEV)], axis=1) for a in range(4))

    r = _local_step(x[0], loss_target[0], meta_f, norm_mix_gain, wt_main, wt_ab, cq_f, ck_f, cv_f, dn_a_log, dn_dt_bias,
                    dn_out_norm_gain, sb_q_norm_gain, sb_k_norm_gain, norm_ffn_gain,
                    (_bf(features_major(w_ffn_in)), _bf(w_branch_dn[0]), _bf(w_branch_sb[0]), _bf(w_out[0]),
                     _bf(w_ffn_out[0])), shards=True)
    s_fi, s_bd, s_bs, s_out, s_fo = r["early"]
    s_in = r["s_in"]

    loss_part = (0.5 / d) * jnp.sum(r["lsum"], axis=1, keepdims=True)
    small_g = [r["d_meta"], r["d_g_mix"], r["d_cq"], r["d_ck"], r["d_cv"], r["d_a_log"], r["d_dt"], r["d_g_dn"],
               r["d_g_sbq"], r["d_g_sbk"], r["d_g_ffn"], loss_part]
    (g_packs,) = _exchange([_pack(small_g)], scatter=[False], name="gather_small_grads")
    (g_meta, g_mix, g_cq, g_ck, g_cv, g_al, g_dt, g_gdn, g_sbq, g_sbk, g_ffn, loss) = _unpack(
        _sum_slabs(g_packs, name="sum_small_grads"), [a.shape for a in small_g])

    def mine(a, width):
        return lax.dynamic_slice_in_dim(a, me * width, width, axis=1)

    big = dict(w_in=(s_in, w_in, m_w_in, v_w_in), w_branch_dn=(s_bd, w_branch_dn, m_w_branch_dn, v_w_branch_dn),
               w_branch_sb=(s_bs, w_branch_sb, m_w_branch_sb, v_w_branch_sb), w_out=(s_out, w_out, m_w_out, v_w_out),
               w_ffn_in=(s_fi, w_ffn_in, m_w_ffn_in, v_w_ffn_in), w_ffn_out=(s_fo, w_ffn_out, m_w_ffn_out, v_w_ffn_out))
    tiny = dict(meta_tokens=(mine(g_meta, d // N_DEV), meta_tokens, m_meta_tokens, v_meta_tokens),
                norm_mix_gain=(g_mix, norm_mix_gain, m_norm_mix_gain, v_norm_mix_gain),
                conv_q=(mine(g_cq, qk // N_DEV), conv_q[0], m_conv_q[0], v_conv_q[0]),
                conv_k=(mine(g_ck, qk // N_DEV), conv_k[0], m_conv_k[0], v_conv_k[0]),
                conv_v=(mine(g_cv, dvt // N_DEV), conv_v[0], m_conv_v[0], v_conv_v[0]),
                dn_a_log=(g_al, dn_a_log, m_dn_a_log, v_dn_a_log), dn_dt_bias=(g_dt, dn_dt_bias, m_dn_dt_bias, v_dn_dt_bias),
                dn_out_norm_gain=(g_gdn, dn_out_norm_gain, m_dn_out_norm_gain, v_dn_out_norm_gain),
                sb_q_norm_gain=(g_sbq, sb_q_norm_gain, m_sb_q_norm_gain, v_sb_q_norm_gain),
                sb_k_norm_gain=(g_sbk, sb_k_norm_gain, m_sb_k_norm_gain, v_sb_k_norm_gain),
                norm_ffn_gain=(g_ffn, norm_ffn_gain, m_norm_ffn_gain, v_norm_ffn_gain))
    order = ["meta_tokens", "norm_mix_gain", "w_in", "conv_q", "conv_k", "conv_v", "dn_a_log", "dn_dt_bias",
             "dn_out_norm_gain", "sb_q_norm_gain", "sb_k_norm_gain", "w_branch_dn", "w_branch_sb", "w_out",
             "norm_ffn_gain", "w_ffn_in", "w_ffn_out"]
    grads, deltas, new_m, new_v = [], [], [], []
    for name in order:
        if name in ("w_in", "w_ffn_in"):
            slabs, w, m, v = big[name]
            res = _adamw_slabs(slabs, *(features_major(a).reshape(-1, LANES) for a in (w, m, v)), name="adamw_" + name)
            g, dl, mo, vo = (jnp.transpose(a.reshape(w.shape[2], 1, w.shape[1]), (1, 2, 0)) for a in res)
            like = w.shape
        elif name in big:
            slabs, w, m, v = big[name]
            g, dl, mo, vo = _adamw_slabs(slabs, w[0], m[0], v[0], name="adamw_" + name)
            like = w.shape
        else:
            g, w, m, v = tiny[name]
            like = dict(conv_q=conv_q, conv_k=conv_k, conv_v=conv_v).get(name, w).shape
            dl, mo, vo = _adamw_small(g, w, m, v, name="adamw_" + name)
        for lst, a in ((grads, g), (deltas, dl), (new_m, mo), (new_v, vo)):
            lst.append(a.reshape(like))
    return (loss.reshape(()), r["grad_x"][None], *grads, *deltas, *new_m, *new_v)
```

```python
import functools
import math

import jax
import jax.numpy as jnp
from jax import lax
from jax.experimental import pallas as pl
from jax.experimental.pallas import tpu as pltpu

F32 = jnp.float32
BF16 = jnp.bfloat16

N_DEV = 8
CHUNK = 64
CHUNK_SHIFT = 6
GDN_ROWS = 2 * CHUNK
SB_BLOCK = 128
SB_HEADS_PER_STEP = 2
LANES = 128
RMS_EPS = 1e-6
L2_EPS = 1e-6
ADAM_LR = 0.001
ADAM_B1 = 0.9
ADAM_B2 = 0.999
ADAM_EPS = 1e-08
ADAM_WD = 0.01
ADAM_STEP = 10
V7X_VMEM_LIMIT_BYTES = 56 * 1024 * 1024
MM_TN_OUT_BLOCK_BYTES = 6 * 1024 * 1024
MM_PIECES_VMEM_BYTES = V7X_VMEM_LIMIT_BYTES // 2
ROWS_BIG = (1056, 512, 384, 256, 128)
ROWS_MID = (528, 384, 256, 128)
SCAN_HEADS = 4
PREP_HEADS = 2

MESH = pl.DeviceIdType.MESH


def _params(*sem):
    return pltpu.CompilerParams(dimension_semantics=sem or None, vmem_limit_bytes=V7X_VMEM_LIMIT_BYTES)


def _pick(n, cands):
    for c in cands:
        if n % c == 0:
            return c
    raise ValueError(f"no block size among {cands} divides {n}")


def _bf(x):
    return x.astype(BF16)


def _dot(a, b):
    return jnp.dot(a, b, preferred_element_type=F32)


def _dot_nt(a, b):
    return lax.dot_general(a, b, (((1,), (1,)), ((), ())), preferred_element_type=F32)


def _dot_tn(a, b):
    return lax.dot_general(a, b, (((0,), (0,)), ((), ())), preferred_element_type=F32)


def _split2(x):
    hi = _bf(x)
    return hi, _bf(x - hi.astype(F32))


def _split3(x):
    hi = _bf(x)
    r = x - hi.astype(F32)
    mid = _bf(r)
    return hi, mid, _bf(r - mid.astype(F32))


def _dot_hp(a, b, dot=_dot):
    ah, al = _split2(a)
    bh, bl = _split2(b)
    return dot(ah, bh) + dot(ah, bl) + dot(al, bh)


def _dot_exact_l(m, x, dot=_dot):
    h, mi, lo = _split3(x)
    return dot(m, h) + dot(m, mi) + dot(m, lo)


def _sigmoid(x):
    return 1.0 / (1.0 + jnp.exp(-x))


def _silu(x):
    return x * _sigmoid(x)


def _silu_grad(x):
    s = _sigmoid(x)
    return s * (1.0 + x * (1.0 - s))


def _softplus(x):
    return jnp.maximum(x, 0.0) + jnp.log(1.0 + jnp.exp(-jnp.abs(x)))


def _rms_fwd(h, gain):
    r = lax.rsqrt(jnp.mean(h * h, axis=-1, keepdims=True) + RMS_EPS)
    return h * r * gain


def _rms_bwd(h, gain, dy):
    r = lax.rsqrt(jnp.mean(h * h, axis=-1, keepdims=True) + RMS_EPS)
    dyg = dy * gain
    dh = r * dyg - h * (r * r * r) * jnp.mean(dyg * h, axis=-1, keepdims=True)
    return dh, dy * h * r


def _iota(shape, dim):
    return lax.broadcasted_iota(jnp.int32, shape, dim)


def _lane_pick(x, idx):
    return jnp.sum(jnp.where(_iota(x.shape, 1) == idx, x, 0.0), axis=1, keepdims=True)


def _mm_nt(a, b, *, out_dtype, name):
    m, k = a.shape
    n = b.shape[0]
    tm, tn = _pick(m, ROWS_BIG), _pick(n, (1024, 512, 256, 128))

    def body(a_ref, b_ref, o_ref):
        o_ref[...] = _dot_nt(a_ref[...], b_ref[...]).astype(out_dtype)

    return pl.pallas_call(
        body, grid=(m // tm, n // tn),
        in_specs=[pl.BlockSpec((tm, k), lambda i, j: (i, 0)), pl.BlockSpec((tn, k), lambda i, j: (j, 0))],
        out_specs=pl.BlockSpec((tm, tn), lambda i, j: (i, j)),
        out_shape=jax.ShapeDtypeStruct((m, n), out_dtype), name=name,
        compiler_params=_params("parallel", "parallel"))(a, b)


def _column_pieces(a):
    pieces = list(a) if isinstance(a, (list, tuple)) else [a]
    widths = [p.shape[1] for p in pieces]
    return pieces, widths, functools.reduce(math.gcd, widths)


def _piece_spans(widths, block):
    ends = [sum(widths[:j + 1]) // block for j in range(len(widths))]
    return list(zip([0] + ends[:-1], ends))


def _mm_tn(a, b, *, name):
    pieces, widths, unit = _column_pieces(a)
    t, m = pieces[0].shape[0], sum(widths)
    n = b.shape[1]
    tn = _pick(n, (2816, 2048, 1408, 1024, 512, 256, 128))
    tm = _pick(unit, tuple(c for c in (1408, 1024, 512, 256, 128) if c * tn * 4 <= MM_TN_OUT_BLOCK_BYTES))
    held = 2 * len(pieces) * tm * pieces[0].dtype.itemsize
    tk = _pick(t, tuple(c for c in (1408, 1024, 704, 512, 384, 256, 128) if c * held <= MM_PIECES_VMEM_BYTES))
    nk = t // tk
    spans = _piece_spans(widths, tm)

    def body(*refs):
        a_refs, (b_ref, o_ref, acc_ref) = refs[:len(pieces)], refs[len(pieces):]
        i, k = pl.program_id(0), pl.program_id(2)

        @pl.when(k == 0)
        def _():
            acc_ref[...] = jnp.zeros_like(acc_ref)

        for a_ref, (lo, hi) in zip(a_refs, spans):
            def add(a_ref=a_ref):
                acc_ref[...] += _dot_tn(a_ref[...], b_ref[...])
            if len(pieces) == 1:
                add()
            else:
                pl.when((i >= lo) & (i < hi))(add)

        @pl.when(k == nk - 1)
        def _():
            o_ref[...] = _bf(acc_ref[...])

    def a_spec(lo, hi):
        if len(pieces) == 1:
            return pl.BlockSpec((tk, tm), lambda i, j, k: (k, i))
        return pl.BlockSpec((tk, tm), lambda i, j, k: (jnp.where(i < lo, 0, jnp.where(i >= hi, nk - 1, k)),
                                                       jnp.clip(i - lo, 0, hi - lo - 1)))

    return pl.pallas_call(
        body, grid=(m // tm, n // tn, nk),
        in_specs=[a_spec(lo, hi) for lo, hi in spans] + [pl.BlockSpec((tk, tn), lambda i, j, k: (k, j))],
        out_specs=pl.BlockSpec((tm, tn), lambda i, j, k: (i, j)),
        out_shape=jax.ShapeDtypeStruct((m, n), BF16), scratch_shapes=[pltpu.VMEM((tm, tn), F32)], name=name,
        compiler_params=_params("parallel", "parallel", "arbitrary"))(*pieces, b)


def _mm_norm(h, gain, wt, *, name):
    m, k = h.shape
    n = wt.shape[0]
    tm, tn = _pick(m, ROWS_BIG), _pick(n, (1024, 512, 256, 128))

    def body(h_ref, g_ref, w_ref, o_ref, n_ref):
        @pl.when(pl.program_id(1) == 0)
        def _():
            n_ref[...] = _bf(_rms_fwd(h_ref[...], g_ref[...]))

        o_ref[...] = _dot_nt(n_ref[...], w_ref[...])

    return pl.pallas_call(
        body, grid=(m // tm, n // tn),
        in_specs=[pl.BlockSpec((tm, k), lambda i, j: (i, 0)), pl.BlockSpec((1, k), lambda i, j: (0, 0)),
                  pl.BlockSpec((tn, k), lambda i, j: (j, 0))],
        out_specs=[pl.BlockSpec((tm, tn), lambda i, j: (i, j)), pl.BlockSpec((tm, k), lambda i, j: (i, 0))],
        out_shape=[jax.ShapeDtypeStruct((m, n), F32), jax.ShapeDtypeStruct((m, k), BF16)], name=name,
        compiler_params=_params("parallel", "arbitrary"))(h, gain, wt)


def _mm_norm_swiglu(h, gain, wgt, wut, *, name):
    m, k = h.shape
    n = wgt.shape[0]
    tm, tn = _pick(m, ROWS_MID), _pick(n, (1408, 1024, 512, 256, 128))

    def body(h_ref, g_ref, wg_ref, wu_ref, gate_ref, up_ref, act_ref, n_ref):
        @pl.when(pl.program_id(1) == 0)
        def _():
            n_ref[...] = _bf(_rms_fwd(h_ref[...], g_ref[...]))

        gate = _dot_nt(n_ref[...], wg_ref[...])
        up = _dot_nt(n_ref[...], wu_ref[...])
        gate_ref[...] = gate
        up_ref[...] = up
        act_ref[...] = _bf(_silu(gate) * up)

    wspec = pl.BlockSpec((tn, k), lambda i, j: (j, 0))
    ospec = pl.BlockSpec((tm, tn), lambda i, j: (i, j))
    return pl.pallas_call(
        body, grid=(m // tm, n // tn),
        in_specs=[pl.BlockSpec((tm, k), lambda i, j: (i, 0)), pl.BlockSpec((1, k), lambda i, j: (0, 0)), wspec, wspec],
        out_specs=[ospec, ospec, ospec, pl.BlockSpec((tm, k), lambda i, j: (i, 0))],
        out_shape=[jax.ShapeDtypeStruct((m, n), F32), jax.ShapeDtypeStruct((m, n), F32),
                   jax.ShapeDtypeStruct((m, n), BF16), jax.ShapeDtypeStruct((m, k), BF16)], name=name,
        compiler_params=_params("parallel", "arbitrary"))(h, gain, wgt, wut)


def _mm_res(res, a, b, *, name):
    m, k = a.shape
    n = b.shape[1]
    tm, tn = _pick(m, ROWS_BIG), _pick(n, (1024, 512, 256, 128))

    def body(r_ref, a_ref, b_ref, o_ref):
        o_ref[...] = r_ref[...] + _dot(a_ref[...], b_ref[...])

    return pl.pallas_call(
        body, grid=(m // tm, n // tn),
        in_specs=[pl.BlockSpec((tm, tn), lambda i, j: (i, j)), pl.BlockSpec((tm, k), lambda i, j: (i, 0)),
                  pl.BlockSpec((k, tn), lambda i, j: (0, j))],
        out_specs=pl.BlockSpec((tm, tn), lambda i, j: (i, j)),
        out_shape=jax.ShapeDtypeStruct((m, n), F32), name=name,
        compiler_params=_params("parallel", "parallel"))(res, a, b)


def _mm_res_loss(res, a, b, target, *, row0, nrows, name):
    m, k = a.shape
    n = b.shape[1]
    tm = _pick(m, ROWS_MID)

    def body(r_ref, a_ref, b_ref, t_ref, dy_ref, dyb_ref, ls_ref):
        i = pl.program_id(0)

        @pl.when(i == 0)
        def _():
            ls_ref[...] = jnp.zeros_like(ls_ref)

        y = r_ref[...] + _dot(a_ref[...], b_ref[...])
        row = i * tm + _iota((tm, n), 0)
        e = jnp.where((row >= row0) & (row < row0 + nrows), y - t_ref[...], 0.0)
        dy = e / n
        dy_ref[...] = dy
        dyb_ref[...] = _bf(dy)
        ls_ref[...] += jnp.sum(e * e, axis=0, keepdims=True)

    rspec = pl.BlockSpec((tm, n), lambda i: (i, 0))
    return pl.pallas_call(
        body, grid=(m // tm,),
        in_specs=[rspec, pl.BlockSpec((tm, k), lambda i: (i, 0)), pl.BlockSpec((k, n), lambda i: (0, 0)), rspec],
        out_specs=[rspec, rspec, pl.BlockSpec((1, n), lambda i: (0, 0))],
        out_shape=[jax.ShapeDtypeStruct((m, n), F32), jax.ShapeDtypeStruct((m, n), BF16),
                   jax.ShapeDtypeStruct((1, n), F32)], name=name,
        compiler_params=_params("arbitrary"))(res, a, b, target)


def _merge_fwd(o_dn, o_sb, wbd, wbs, proj, *, col_gd, col_gs, name):
    m, kd = o_dn.shape
    ks = o_sb.shape[1]
    n = wbd.shape[1]
    tm = _pick(m, ROWS_BIG)
    tn = _pick(math.gcd(n, math.gcd(col_gd, col_gs)), (512, 256, 128))

    def body(od_ref, os_ref, wd_ref, ws_ref, gd_ref, gs_ref, mg_ref, bd_ref, bs_ref):
        bd = _dot(od_ref[...], wd_ref[...])
        bs = _dot(os_ref[...], ws_ref[...])
        bd_ref[...] = bd
        bs_ref[...] = bs
        mg_ref[...] = _bf(_sigmoid(gd_ref[...]) * bd + _sigmoid(gs_ref[...]) * bs)

    ospec = pl.BlockSpec((tm, tn), lambda i, j: (i, j))
    return pl.pallas_call(
        body, grid=(m // tm, n // tn),
        in_specs=[pl.BlockSpec((tm, kd), lambda i, j: (i, 0)), pl.BlockSpec((tm, ks), lambda i, j: (i, 0)),
                  pl.BlockSpec((kd, tn), lambda i, j: (0, j)), pl.BlockSpec((ks, tn), lambda i, j: (0, j)),
                  pl.BlockSpec((tm, tn), lambda i, j: (i, col_gd // tn + j)),
                  pl.BlockSpec((tm, tn), lambda i, j: (i, col_gs // tn + j))],
        out_specs=[ospec, ospec, ospec],
        out_shape=[jax.ShapeDtypeStruct((m, n), BF16), jax.ShapeDtypeStruct((m, n), F32),
                   jax.ShapeDtypeStruct((m, n), F32)], name=name,
        compiler_params=_params("parallel", "parallel"))(o_dn, o_sb, wbd, wbs, proj, proj)


def _merge_bwd(dh, w_out, proj, br_dn, br_sb, *, col_gd, col_gs, name):
    m, k = dh.shape
    n = w_out.shape[0]
    tm = _pick(m, ROWS_BIG)
    tn = _pick(math.gcd(n, math.gcd(col_gd, col_gs)), (512, 256, 128))

    def body(dh_ref, w_ref, gd_ref, gs_ref, bd_ref, bs_ref, dbd_ref, dbs_ref, dgd_ref, dgs_ref):
        dm = _dot_nt(dh_ref[...], w_ref[...])
        sd = _sigmoid(gd_ref[...])
        ss = _sigmoid(gs_ref[...])
        dbd_ref[...] = _bf(dm * sd)
        dbs_ref[...] = _bf(dm * ss)
        dgd_ref[...] = _bf(dm * bd_ref[...] * sd * (1.0 - sd))
        dgs_ref[...] = _bf(dm * bs_ref[...] * ss * (1.0 - ss))

    ospec = pl.BlockSpec((tm, tn), lambda i, j: (i, j))
    return pl.pallas_call(
        body, grid=(m // tm, n // tn),
        in_specs=[pl.BlockSpec((tm, k), lambda i, j: (i, 0)), pl.BlockSpec((tn, k), lambda i, j: (j, 0)),
                  pl.BlockSpec((tm, tn), lambda i, j: (i, col_gd // tn + j)),
                  pl.BlockSpec((tm, tn), lambda i, j: (i, col_gs // tn + j)), ospec, ospec],
        out_specs=[ospec] * 4,
        out_shape=[jax.ShapeDtypeStruct((m, n), BF16)] * 4, name=name,
        compiler_params=_params("parallel", "parallel"))(dh, w_out, proj, proj, br_dn, br_sb)


def _swiglu_bwd(dy, wfo, gate, up, *, name):
    m, k = dy.shape
    n = wfo.shape[0]
    tm, tn = _pick(m, ROWS_MID), _pick(n, (1408, 1024, 512, 256, 128))

    def body(dy_ref, w_ref, g_ref, u_ref, dg_ref, du_ref):
        da = _dot_nt(dy_ref[...], w_ref[...])
        g = g_ref[...]
        dg_ref[...] = _bf(da * u_ref[...] * _silu_grad(g))
        du_ref[...] = _bf(da * _silu(g))

    ospec = pl.BlockSpec((tm, tn), lambda i, j: (i, j))
    return pl.pallas_call(
        body, grid=(m // tm, n // tn),
        in_specs=[pl.BlockSpec((tm, k), lambda i, j: (i, 0)), pl.BlockSpec((tn, k), lambda i, j: (j, 0)), ospec, ospec],
        out_specs=[ospec, ospec], out_shape=[jax.ShapeDtypeStruct((m, n), BF16)] * 2, name=name,
        compiler_params=_params("parallel", "parallel"))(dy, wfo, gate, up)


def _mm_rmsbwd(pairs, extra, h, gain, dres, *, name, scatter=None):
    split = [_column_pieces(a) for a, _ in pairs]
    m, k = h.shape[0], sum(split[0][1])
    n = h.shape[1]
    tm = _pick(m, ROWS_MID)
    tk = _pick(functools.reduce(math.gcd, [unit for _, _, unit in split]), (1408, 1024, 512, 256, 128))
    nk = k // tk
    spans = [_piece_spans(widths, tk) for _, widths, _ in split]
    n_ab = sum(len(s) + 1 for s in spans)

    def body(*refs):
        ex = refs[n_ab:n_ab + 2] if extra is not None else ()
        h_ref, g_ref, r_ref, dh_ref, dhb_ref, dg_ref, acc_ref = refs[n_ab + len(ex):]
        i, kk = pl.program_id(0), pl.program_id(1)

        @pl.when((i == 0) & (kk == 0))
        def _():
            dg_ref[...] = jnp.zeros_like(dg_ref)

        @pl.when(kk == 0)
        def _():
            acc_ref[...] = _dot(ex[0][...], ex[1][...]) if ex else jnp.zeros_like(acc_ref)

        pos, whole = 0, []
        for s in spans:
            a_refs, b_ref = refs[pos:pos + len(s)], refs[pos + len(s)]
            pos += len(s) + 1
            if len(s) == 1:
                whole.append(_dot(a_refs[0][...], b_ref[...]))
                continue
            for a_ref, (lo, hi) in zip(a_refs, s):
                def add(a_ref=a_ref, b_ref=b_ref):
                    acc_ref[...] += _dot(a_ref[...], b_ref[...])
                pl.when((kk >= lo) & (kk < hi))(add)
        if whole:
            acc_ref[...] += functools.reduce(lambda x, y: x + y, whole)

        @pl.when(kk == nk - 1)
        def _():
            dh, dgr = _rms_bwd(h_ref[...], g_ref[...], acc_ref[...])
            dh = dh + r_ref[...]
            dh_ref[...] = dh
            dhb_ref[...] = _bf(dh)
            dg_ref[...] += jnp.sum(dgr, axis=0, keepdims=True)

    in_specs, args = [], []
    def a_spec(lo, hi, alone):
        if alone:
            return pl.BlockSpec((tm, tk), lambda i, kk: (i, kk))
        return pl.BlockSpec((tm, tk), lambda i, kk: (i, jnp.clip(kk - lo, 0, hi - lo - 1)))

    for (pieces, _, _), s, (_, b) in zip(split, spans, pairs):
        in_specs += [a_spec(lo, hi, len(s) == 1) for lo, hi in s] + [pl.BlockSpec((tk, n), lambda i, kk: (kk, 0))]
        args += [*pieces, b]
    if extra is not None:
        k2 = extra[0].shape[1]
        in_specs += [pl.BlockSpec((tm, k2), lambda i, kk: (i, 0)), pl.BlockSpec((k2, n), lambda i, kk: (0, 0))]
        args += list(extra)
    rspec = pl.BlockSpec((tm, n), lambda i, kk: (i, 0))
    in_specs += [rspec, pl.BlockSpec((1, n), lambda i, kk: (0, 0)), rspec]
    call = dict(grid=(m // tm, nk), in_specs=in_specs,
                out_specs=[rspec, rspec, pl.BlockSpec((1, n), lambda i, kk: (0, 0))],
                out_shape=[jax.ShapeDtypeStruct((m, n), F32), jax.ShapeDtypeStruct((m, n), BF16),
                           jax.ShapeDtypeStruct((1, n), F32)],
                scratch_shapes=[pltpu.VMEM((tm, n), F32)], name=name)
    if scatter is None:
        return pl.pallas_call(body, compiler_params=_params("arbitrary", "arbitrary"), **call)(*args, h, gain, dres)
    res = _call_with_exchange(body, args=(*args, h, gain, dres), srcs=scatter, scatter=[True] * len(scatter), **call)
    return [*res[:3], list(res[3:])]


HALO = 8


def _stage(buf_ref, before, cur):
    buf_ref[0:HALO, :] = before
    buf_ref[HALO:HALO + cur.shape[0], :] = cur


def _stage_after(buf_ref, cur, after):
    r = cur.shape[0]
    buf_ref[0:r, :] = cur
    buf_ref[r:r + HALO, :] = after


def _conv_taps(buf_ref, rows, w_ref):
    nk = w_ref.shape[0]
    out = buf_ref[HALO:HALO + rows, :] * w_ref[nk - 1:nk, :]
    for s in range(1, nk):
        out += buf_ref[HALO - s:HALO - s + rows, :] * w_ref[nk - 1 - s:nk - s, :]
    return out


def _gdn_pre(proj, pab, cq, ck, cv, a_log, dt_bias, *, heads, dk, dv, col_q, col_k, col_v, row_lo, row_hi, name):
    t = proj.shape[0]
    tm = _pick(t, (384, 256, 128))
    nb = t // tm

    def body(pq_ref, pqp_ref, pk_ref, pkp_ref, pv_ref, pvp_ref, ab_ref, cq_ref, ck_ref, cv_ref, al_ref, dt_ref,
             qn_ref, kn_ref, v_ref, g_ref, b_ref, bq_ref, bk_ref, bv_ref):
        h, i = pl.program_id(0), pl.program_id(1)
        first = i == 0
        row = i * tm + _iota((tm, 1), 0)
        valid = (row >= row_lo) & (row < row_hi)
        _stage(bq_ref, jnp.where(first, 0.0, pqp_ref[...]), pq_ref[...])
        _stage(bk_ref, jnp.where(first, 0.0, pkp_ref[...]), pk_ref[...])
        _stage(bv_ref, jnp.where(first, 0.0, pvp_ref[...]), pv_ref[...])
        q1 = _silu(_conv_taps(bq_ref, tm, cq_ref))
        k1 = _silu(_conv_taps(bk_ref, tm, ck_ref))
        v1 = _silu(_conv_taps(bv_ref, tm, cv_ref))
        qn_ref[...] = jnp.where(valid, q1 * lax.rsqrt(jnp.sum(q1 * q1, axis=-1, keepdims=True) + L2_EPS), 0.0)
        kn_ref[...] = jnp.where(valid, k1 * lax.rsqrt(jnp.sum(k1 * k1, axis=-1, keepdims=True) + L2_EPS), 0.0)
        v_ref[...] = jnp.where(valid, v1, 0.0)
        ab = ab_ref[...]
        da = _lane_pick(ab, h)
        db = _lane_pick(ab, heads + h)
        a = _lane_pick(al_ref[...], h)
        dtb = _lane_pick(dt_ref[...], h)
        g_ref[...] = jnp.where(valid, -jnp.exp(a) * _softplus(da + dtb), 0.0)
        b_ref[...] = jnp.where(valid, _sigmoid(db), 0.0)

    def cur(width, col):
        return pl.BlockSpec((tm, width), lambda h, i: (i, col // width + h))

    def prev(width, col):
        return pl.BlockSpec((8, width), lambda h, i: (jnp.maximum(i * (tm // 8) - 1, 0), col // width + h))

    def out(width):
        return pl.BlockSpec((None, tm, width), lambda h, i: (h, i, 0))

    small = pl.BlockSpec((1, LANES), lambda h, i: (0, 0))
    return pl.pallas_call(
        body, grid=(heads, nb),
        in_specs=[cur(dk, col_q), prev(dk, col_q), cur(dk, col_k), prev(dk, col_k), cur(dv, col_v), prev(dv, col_v),
                  pl.BlockSpec((tm, LANES), lambda h, i: (i, 0)),
                  pl.BlockSpec((cq.shape[0], dk), lambda h, i: (0, h)), pl.BlockSpec((ck.shape[0], dk), lambda h, i: (0, h)),
                  pl.BlockSpec((cv.shape[0], dv), lambda h, i: (0, h)), small, small],
        out_specs=[out(dk), out(dk), out(dv), out(1), out(1)],
        out_shape=[jax.ShapeDtypeStruct((heads, t, dk), F32), jax.ShapeDtypeStruct((heads, t, dk), F32),
                   jax.ShapeDtypeStruct((heads, t, dv), F32), jax.ShapeDtypeStruct((heads, t, 1), F32),
                   jax.ShapeDtypeStruct((heads, t, 1), F32)],
        scratch_shapes=[pltpu.VMEM((HALO + tm, dk), F32), pltpu.VMEM((HALO + tm, dk), F32),
                        pltpu.VMEM((HALO + tm, dv), F32)], name=name,
        compiler_params=_params("parallel", "parallel"))(proj, proj, proj, proj, proj, proj, pab, cq, ck, cv, a_log, dt_bias)


def _chunk_masks(rows=GDN_ROWS, row0=0):
    ri = row0 + _iota((rows, GDN_ROWS), 0)
    ci = _iota((rows, GDN_ROWS), 1)
    same = jnp.right_shift(ri, CHUNK_SHIFT) == jnp.right_shift(ci, CHUNK_SHIFT)
    return same, same & (ri >= ci), same & (ri > ci), ri == ci


def _col_to_row(col, eye):
    return jnp.sum(jnp.where(eye, col, 0.0), axis=0, keepdims=True)


def _row_to_col(row, eye):
    return jnp.sum(jnp.where(eye, row, 0.0), axis=1, keepdims=True)


def _chunk_common(blocks, dk_scale):
    same, incl, strict, eye = _chunk_masks()
    tri = jnp.where(incl, 1.0, 0.0).astype(BF16)
    tot = jnp.where(same, 1.0, 0.0).astype(BF16)
    gbs = [jnp.broadcast_to(g, (GDN_ROWS, LANES)) for _, _, g, _ in blocks]
    gams = [jnp.max(_dot_exact_l(tri, gb), axis=1, keepdims=True) for gb in gbs]
    lasts = [jnp.max(_dot_exact_l(tot, gb), axis=1, keepdims=True) for gb in gbs]
    kbs = [kn * beta for _, kn, _, beta in blocks]
    qts = [qn * dk_scale for qn, _, _, _ in blocks]
    boths = [_dot_nt(_bf(jnp.concatenate([kb, qt], axis=0)), _bf(blk[1]))
             for kb, qt, blk in zip(kbs, qts, blocks)]
    out = []
    for gam, last, kb, qt, both in zip(gams, lasts, kbs, qts, boths):
        diff = gam - _col_to_row(gam, eye)
        decay = jnp.where(incl, jnp.exp(jnp.where(incl, diff, 0.0)), 0.0)
        out.append(dict(incl=incl, strict=strict, eye=eye, decay=decay, eg=jnp.exp(gam), ek=jnp.exp(last - gam),
                        egl=jnp.exp(last), kb=kb, qt=qt, lmat=jnp.where(strict, both[:GDN_ROWS] * decay, 0.0),
                        pmat=jnp.where(incl, both[GDN_ROWS:] * decay, 0.0)))
    return out


def _gdn_prep(qn, kn, v, g, beta, *, name):
    heads, t, dk = qn.shape
    dv = v.shape[2]
    rows = _pick(t, (3 * GDN_ROWS, 2 * GDN_ROWS, GDN_ROWS))
    dk_scale = dk ** -0.5

    hp = PREP_HEADS

    def body(q_ref, k_ref, v_ref, g_ref, b_ref, u_ref, w_ref, p_ref, qd_ref, kd_ref, egl_ref, t_ref):
        rs = [(hh, pl.ds(b * GDN_ROWS, GDN_ROWS), slice(None)) for hh in range(hp) for b in range(rows // GDN_ROWS)]
        cs = _chunk_common([(q_ref[r], k_ref[r], g_ref[r], b_ref[r]) for r in rs], dk_scale)
        eye_f = jnp.where(cs[0]["eye"], 1.0, 0.0)
        tinvs = [eye_f - c["lmat"] for c in cs]
        ys = [_dot_hp(c["lmat"], c["lmat"]) for c in cs]
        for _ in range(CHUNK_SHIFT - 1):
            boths = [_dot_hp(y, jnp.concatenate([y, tinv], axis=1)) for y, tinv in zip(ys, tinvs)]
            ys = [both[:, :GDN_ROWS] for both in boths]
            tinvs = [tinv + both[:, GDN_ROWS:] for tinv, both in zip(tinvs, boths)]
        uws = [_dot_hp(tinv, jnp.concatenate([v_ref[r] * b_ref[r], c["kb"] * c["eg"]], axis=1))
               for r, c, tinv in zip(rs, cs, tinvs)]
        for r, c, tinv, uw in zip(rs, cs, tinvs, uws):
            u_ref[r] = uw[:, :dv]
            w_ref[r] = _bf(uw[:, dv:])
            p_ref[r] = _bf(c["pmat"])
            qd_ref[r] = _bf(c["qt"] * c["eg"])
            kd_ref[r] = _bf(k_ref[r] * c["ek"])
            egl_ref[r] = c["egl"]
            t_ref[r] = tinv

    def blk(width):
        return pl.BlockSpec((hp, rows, width), lambda h, i: (h, i, 0))

    def shp(width, dtype=F32):
        return jax.ShapeDtypeStruct((heads, t, width), dtype)

    return pl.pallas_call(
        body, grid=(heads // hp, t // rows), in_specs=[blk(dk), blk(dk), blk(dv), blk(1), blk(1)],
        out_specs=[blk(dv), blk(dk), blk(GDN_ROWS), blk(dk), blk(dk), blk(1), blk(GDN_ROWS)],
        out_shape=[shp(dv), shp(dk, BF16), shp(GDN_ROWS, BF16), shp(dk, BF16), shp(dk, BF16), shp(1), shp(GDN_ROWS)],
        name=name,
        compiler_params=_params("parallel", "parallel"))(qn, kn, v, g, beta)


def _gdn_scan(u, w, p, qd, kd, egl, proj, gain, *, col_z, name):
    heads, t, dv = u.shape
    dk = w.shape[2]
    nb = t // GDN_ROWS
    sub = GDN_ROWS // CHUNK
    hp = SCAN_HEADS

    def body(u_ref, w_ref, p_ref, qd_ref, kd_ref, egl_ref, z_ref, gn_ref, o_ref, og_ref, st_ref, s_ref):
        @pl.when(pl.program_id(1) == 0)
        def _():
            s_ref[...] = jnp.zeros_like(s_ref)

        hs = range(hp)
        vn_parts = [[jnp.zeros((CHUNK, dv), F32)] * sub for _ in hs]
        for c in range(sub):
            r = pl.ds(c * CHUNK, CHUNK)
            ss = [s_ref[hh] for hh in hs]
            sbs = [_bf(s) for s in ss]
            wss = [_dot(_bf(jnp.concatenate([w_ref[hh, r, :], qd_ref[hh, r, :]], axis=0)), sbs[hh])
                   for hh in hs]
            vns = [u_ref[hh, r, :] - wss[hh][:CHUNK] for hh in hs]
            for hh in hs:
                vn_parts[hh][c] = vns[hh]
            os_ = [wss[hh][CHUNK:] + _dot(_bf(p_ref[hh, r, :]), _bf(jnp.concatenate(vn_parts[hh], axis=0))) for hh in hs]
            new = [ss[hh] * egl_ref[hh, pl.ds(c * CHUNK, 1), :] + _dot_tn(_bf(kd_ref[hh, r, :]), _bf(vns[hh])) for hh in hs]
            for hh in hs:
                cols = pl.ds(hh * dv, dv)
                st_ref[hh, c] = sbs[hh]
                s_ref[hh] = new[hh]
                o_ref[hh, r, :] = os_[hh]
                og_ref[r, cols] = _bf(_rms_fwd(os_[hh], gn_ref[...]) * _silu(z_ref[r, cols]))

    def blk(width):
        return pl.BlockSpec((hp, GDN_ROWS, width), lambda h, i: (h, i, 0))

    return pl.pallas_call(
        body, grid=(heads // hp, nb),
        in_specs=[blk(dv), blk(dk), blk(GDN_ROWS), blk(dk), blk(dk), blk(1),
                  pl.BlockSpec((GDN_ROWS, hp * dv), lambda h, i: (i, col_z // (hp * dv) + h)),
                  pl.BlockSpec((1, dv), lambda h, i: (0, 0))],
        out_specs=[blk(dv), pl.BlockSpec((GDN_ROWS, hp * dv), lambda h, i: (i, h)),
                   pl.BlockSpec((hp, sub, dk, dv), lambda h, i: (h, i, 0, 0))],
        out_shape=[jax.ShapeDtypeStruct((heads, t, dv), F32), jax.ShapeDtypeStruct((t, heads * dv), BF16),
                   jax.ShapeDtypeStruct((heads, t // CHUNK, dk, dv), BF16)],
        scratch_shapes=[pltpu.VMEM((hp, dk, dv), F32)], name=name,
        compiler_params=_params("parallel", "arbitrary"))(u, w, p, qd, kd, egl, proj, gain)


def _gdn_post_bwd(o, proj, gain, dout, *, col_z, name):
    heads, t, dv = o.shape
    tm = _pick(t, (384, 256, 128))

    def body(o_ref, z_ref, gn_ref, d_ref, do_ref, dz_ref, dg_ref):
        @pl.when((pl.program_id(0) == 0) & (pl.program_id(1) == 0))
        def _():
            dg_ref[...] = jnp.zeros_like(dg_ref)

        o_, z, d = o_ref[...], z_ref[...], d_ref[...]
        y = _rms_fwd(o_, gn_ref[...])
        dz_ref[...] = _bf(d * y * _silu_grad(z))
        do, dgr = _rms_bwd(o_, gn_ref[...], d * _silu(z))
        do_ref[...] = do
        dg_ref[...] += jnp.sum(dgr, axis=0, keepdims=True)

    return pl.pallas_call(
        body, grid=(t // tm, heads),
        in_specs=[pl.BlockSpec((None, tm, dv), lambda i, h: (h, i, 0)),
                  pl.BlockSpec((tm, dv), lambda i, h: (i, col_z // dv + h)),
                  pl.BlockSpec((1, dv), lambda i, h: (0, 0)), pl.BlockSpec((tm, dv), lambda i, h: (i, h))],
        out_specs=[pl.BlockSpec((None, tm, dv), lambda i, h: (h, i, 0)), pl.BlockSpec((tm, dv), lambda i, h: (i, h)),
                   pl.BlockSpec((1, dv), lambda i, h: (0, 0))],
        out_shape=[jax.ShapeDtypeStruct((heads, t, dv), F32), jax.ShapeDtypeStruct((t, heads * dv), BF16),
                   jax.ShapeDtypeStruct((1, dv), F32)], name=name,
        compiler_params=_params("arbitrary", "arbitrary"))(o, proj, gain, dout)


def _gdn_bwd_scan(u, w, p, qd, kd, egl, st, do, *, name):
    heads, t, dv = u.shape
    dk = w.shape[2]
    nb = t // GDN_ROWS
    sub = GDN_ROWS // CHUNK
    hp = SCAN_HEADS

    def body(u_ref, w_ref, p_ref, qd_ref, kd_ref, egl_ref, st_ref, do_ref,
             du_ref, dw_ref, dp_ref, dqd_ref, dkd_ref, dgl_ref, ds_ref):
        @pl.when(pl.program_id(1) == 0)
        def _():
            ds_ref[...] = jnp.zeros_like(ds_ref)

        hs = range(hp)
        zeros = jnp.zeros((CHUNK, dv), BF16)
        for c in reversed(range(sub)):
            r = pl.ds(c * CHUNK, CHUNK)
            sbs = [st_ref[hh, c] for hh in hs]
            dss = [ds_ref[hh] for hh in hs]
            dsbs = [_bf(ds) for ds in dss]
            dobs = [_bf(do_ref[hh, r, :]) for hh in hs]
            wbs = [_bf(w_ref[hh, r, :]) for hh in hs]
            vns = [u_ref[hh, r, :] - _dot(wbs[hh], sbs[hh]) for hh in hs]
            dvns = [_dot_tn(_bf(p_ref[hh, r, :]), dobs[hh])[c * CHUNK:(c + 1) * CHUNK, :]
                    + _dot(_bf(kd_ref[hh, r, :]), dsbs[hh]) for hh in hs]
            dods = [jnp.concatenate([dobs[hh], _bf(dvns[hh])], axis=0) for hh in hs]
            boths = [_dot_nt(dods[hh], sbs[hh]) for hh in hs]
            dps = [_dot_nt(dobs[hh], jnp.concatenate([_bf(vns[hh]) if cc == c else zeros for cc in range(sub)], axis=0))
                   for hh in hs]
            dkds = [_dot_nt(_bf(vns[hh]), dsbs[hh]) for hh in hs]
            new = [dss[hh] * egl_ref[hh, pl.ds(c * CHUNK, 1), :]
                   + _dot_tn(jnp.concatenate([_bf(qd_ref[hh, r, :]), -wbs[hh]], axis=0), dods[hh])
                   for hh in hs]
            for hh in hs:
                du_ref[hh, r, :] = dvns[hh]
                dw_ref[hh, r, :] = -boths[hh][CHUNK:]
                dp_ref[hh, r, :] = jnp.where(_chunk_masks(CHUNK, c * CHUNK)[1], dps[hh], 0.0)
                dqd_ref[hh, r, :] = boths[hh][:CHUNK]
                dkd_ref[hh, r, :] = dkds[hh]
                dgl = jnp.sum(jnp.sum(dss[hh] * sbs[hh].astype(F32), axis=1, keepdims=True), axis=0, keepdims=True)
                dgl_ref[hh, r, :] = jnp.where(_iota((CHUNK, 1), 0) == CHUNK - 1, dgl, 0.0)
                ds_ref[hh] = new[hh]

    def blk(width):
        return pl.BlockSpec((hp, GDN_ROWS, width), lambda h, i: (h, nb - 1 - i, 0))

    def shp(width):
        return jax.ShapeDtypeStruct((heads, t, width), F32)

    return pl.pallas_call(
        body, grid=(heads // hp, nb),
        in_specs=[blk(dv), blk(dk), blk(GDN_ROWS), blk(dk), blk(dk), blk(1),
                  pl.BlockSpec((hp, sub, dk, dv), lambda h, i: (h, nb - 1 - i, 0, 0)), blk(dv)],
        out_specs=[blk(dv), blk(dk), blk(GDN_ROWS), blk(dk), blk(dk), blk(1)],
        out_shape=[shp(dv), shp(dk), shp(GDN_ROWS), shp(dk), shp(dk), shp(1)],
        scratch_shapes=[pltpu.VMEM((hp, dk, dv), F32)], name=name,
        compiler_params=_params("parallel", "arbitrary"))(u, w, p, qd, kd, egl, st, do)


def _gdn_bwd_prep(qn, kn, v, g, beta, tinv, u, w, du, dw, dp, dqd, dkd, dgl, *, name):
    heads, t, dk = qn.shape
    dv = v.shape[2]
    rows = _pick(t, (3 * GDN_ROWS, 2 * GDN_ROWS, GDN_ROWS))
    dk_scale = dk ** -0.5
    hp = PREP_HEADS

    def rowsum(x):
        return jnp.sum(x, axis=1, keepdims=True)

    def body(q_ref, k_ref, v_ref, g_ref, b_ref, t_ref, u_ref, w_ref, du_ref, dw_ref, dp_ref, dqd_ref, dkd_ref, dgl_ref,
             dq_ref, dkk_ref, dvv_ref, dg_ref, db_ref):
        rs = [(hh, pl.ds(b * GDN_ROWS, GDN_ROWS), slice(None)) for hh in range(hp) for b in range(rows // GDN_ROWS)]
        cs = _chunk_common([(q_ref[r], k_ref[r], g_ref[r], b_ref[r]) for r in rs], dk_scale)
        dbvws = [_dot_hp(t_ref[r], jnp.concatenate([du_ref[r], dw_ref[r]], axis=1), _dot_tn)
                 for r in rs]
        das = [-_dot_nt(_bf(dbvw), jnp.concatenate([_bf(u_ref[r]), w_ref[r]], axis=1))
               for r, dbvw in zip(rs, dbvws)]
        dls = [jnp.where(c["strict"], da, 0.0) for c, da in zip(cs, das)]
        dmns = [_bf(jnp.concatenate([dl * c["decay"], dp_ref[r] * c["decay"]], axis=0))
                for r, c, dl in zip(rs, cs, dls)]
        boths = [_dot(dmn, _bf(k_ref[r])) for r, dmn in zip(rs, dmns)]
        dkns = [_dot_tn(dmn, _bf(jnp.concatenate([c["kb"], c["qt"]], axis=0)))
                for c, dmn in zip(cs, dmns)]
        for r, c, dbvw, dl, both, dkn in zip(rs, cs, dbvws, dls, boths, dkns):
            kn_, beta_, v_ = k_ref[r], b_ref[r], v_ref[r]
            eye = c["eye"]
            kb, qt, eg, ek = c["kb"], c["qt"], c["eg"], c["ek"]
            dbv, dbw = dbvw[:, :dv], dbvw[:, dv:]
            dp_ = dp_ref[r]
            dkb = both[:GDN_ROWS] + dbw * eg
            dqt = both[GDN_ROWS:]
            gmat = dl * c["lmat"] + dp_ * c["pmat"]
            dqd_, dkd_ = dqd_ref[r], dkd_ref[r]
            qd = qt * eg
            kd = kn_ * ek
            bw = kb * eg
            kdsum = rowsum(dkd_ * kd)
            dgam = rowsum(gmat) - _row_to_col(jnp.sum(gmat, axis=0, keepdims=True), eye)
            dgam += rowsum(dbw * bw) + rowsum(dqd_ * qd) - kdsum
            last = (_iota((GDN_ROWS, 1), 0) & (CHUNK - 1)) == CHUNK - 1
            same = _chunk_masks()[0]
            same_f = jnp.where(same, 1.0, 0.0).astype(BF16)
            chunk_tot = jnp.max(_dot_exact_l(same_f, jnp.broadcast_to(kdsum, (GDN_ROWS, LANES))), axis=1, keepdims=True)
            dgam += jnp.where(last, chunk_tot, 0.0) + dgl_ref[r] * c["egl"]
            dq_ref[r] = (dqt + dqd_ * eg) * dk_scale
            dkk_ref[r] = dkn + dkd_ * ek + dkb * beta_
            dvv_ref[r] = dbv * beta_
            db_ref[r] = rowsum(dbv * v_) + rowsum(dkb * kn_)
            upper = jnp.where(same & (_iota((GDN_ROWS, GDN_ROWS), 0) <= _iota((GDN_ROWS, GDN_ROWS), 1)), 1.0, 0.0)
            dgb = _dot_exact_l(upper.astype(BF16), jnp.broadcast_to(dgam, (GDN_ROWS, LANES)))
            dg_ref[r] = _lane_pick(dgb, 0)

    def blk(width):
        return pl.BlockSpec((hp, rows, width), lambda h, i: (h, i, 0))

    def shp(width):
        return jax.ShapeDtypeStruct((heads, t, width), F32)

    return pl.pallas_call(
        body, grid=(heads // hp, t // rows),
        in_specs=[blk(dk), blk(dk), blk(dv), blk(1), blk(1), blk(GDN_ROWS), blk(dv), blk(dk),
                  blk(dv), blk(dk), blk(GDN_ROWS), blk(dk), blk(dk), blk(1)],
        out_specs=[blk(dk), blk(dk), blk(dv), blk(1), blk(1)],
        out_shape=[shp(dk), shp(dk), shp(dv), shp(1), shp(1)], name=name,
        compiler_params=_params("parallel", "parallel"))(qn, kn, v, g, beta, tinv, u, w, du, dw, dp, dqd, dkd, dgl)


def _gdn_pre_bwd_a(proj, pab, cq, ck, cv, a_log, dt_bias, dqn, dkn, dvv, dg, dbeta, *,
                   heads, dk, dv, col_q, col_k, col_v, row_lo, row_hi, name):
    t = proj.shape[0]
    tm = _pick(t, (384, 256, 128))
    nb = t // tm

    def body(pq_ref, pqp_ref, pk_ref, pkp_ref, pv_ref, pvp_ref, ab_ref, cq_ref, ck_ref, cv_ref, al_ref, dt_ref,
             dqn_ref, dkn_ref, dvv_ref, dg_ref, db_ref, dcq_ref, dck_ref, dcv_ref, dab_ref, dal_ref, ddt_ref,
             bq_ref, bk_ref, bv_ref):
        i, h = pl.program_id(0), pl.program_id(1)
        first = i == 0

        @pl.when((i == 0) & (h == 0))
        def _():
            dal_ref[...] = jnp.zeros_like(dal_ref)
            ddt_ref[...] = jnp.zeros_like(ddt_ref)

        @pl.when(h == 0)
        def _():
            dab_ref[...] = jnp.zeros_like(dab_ref)

        row = i * tm + _iota((tm, 1), 0)
        valid = (row >= row_lo) & (row < row_hi)

        def l2_bwd(c1, dn):
            x1 = _silu(c1)
            r = lax.rsqrt(jnp.sum(x1 * x1, axis=-1, keepdims=True) + L2_EPS)
            dn = jnp.where(valid, dn, 0.0)
            d1 = r * dn - x1 * (r * r * r) * jnp.sum(dn * x1, axis=-1, keepdims=True)
            return d1 * _silu_grad(c1)

        _stage(bq_ref, jnp.where(first, 0.0, pqp_ref[...]), pq_ref[...])
        _stage(bk_ref, jnp.where(first, 0.0, pkp_ref[...]), pk_ref[...])
        _stage(bv_ref, jnp.where(first, 0.0, pvp_ref[...]), pv_ref[...])
        dcq_ref[...] = l2_bwd(_conv_taps(bq_ref, tm, cq_ref), dqn_ref[...])
        dck_ref[...] = l2_bwd(_conv_taps(bk_ref, tm, ck_ref), dkn_ref[...])
        cv1 = _conv_taps(bv_ref, tm, cv_ref)
        dcv_ref[...] = jnp.where(valid, dvv_ref[...], 0.0) * _silu_grad(cv1)
        ab = ab_ref[...]
        da = _lane_pick(ab, h)
        db = _lane_pick(ab, heads + h)
        a = _lane_pick(al_ref[...], h)
        dtb = _lane_pick(dt_ref[...], h)
        dgv = jnp.where(valid, dg_ref[...], 0.0)
        ea = jnp.exp(a)
        g = -ea * _softplus(da + dtb)
        dda = dgv * (-ea) * _sigmoid(da + dtb)
        beta = _sigmoid(db)
        ddb = jnp.where(valid, db_ref[...], 0.0) * beta * (1.0 - beta)
        lane = _iota((tm, LANES), 1)
        dab_ref[...] += jnp.where(lane == h, dda, 0.0) + jnp.where(lane == heads + h, ddb, 0.0)
        lane1 = _iota((1, LANES), 1)
        dal_ref[...] += jnp.where(lane1 == h, jnp.sum(dgv * g, axis=0, keepdims=True), 0.0)
        ddt_ref[...] += jnp.where(lane1 == h, jnp.sum(dda, axis=0, keepdims=True), 0.0)

    def cur(width, col):
        return pl.BlockSpec((tm, width), lambda i, h: (i, col // width + h))

    def prev(width, col):
        return pl.BlockSpec((8, width), lambda i, h: (jnp.maximum(i * (tm // 8) - 1, 0), col // width + h))

    def hd(width):
        return pl.BlockSpec((None, tm, width), lambda i, h: (h, i, 0))

    small = pl.BlockSpec((1, LANES), lambda i, h: (0, 0))
    return pl.pallas_call(
        body, grid=(nb, heads),
        in_specs=[cur(dk, col_q), prev(dk, col_q), cur(dk, col_k), prev(dk, col_k), cur(dv, col_v), prev(dv, col_v),
                  pl.BlockSpec((tm, LANES), lambda i, h: (i, 0)),
                  pl.BlockSpec((cq.shape[0], dk), lambda i, h: (0, h)), pl.BlockSpec((ck.shape[0], dk), lambda i, h: (0, h)),
                  pl.BlockSpec((cv.shape[0], dv), lambda i, h: (0, h)), small, small,
                  hd(dk), hd(dk), hd(dv), hd(1), hd(1)],
        out_specs=[hd(dk), hd(dk), hd(dv), pl.BlockSpec((tm, LANES), lambda i, h: (i, 0)), small, small],
        out_shape=[jax.ShapeDtypeStruct((heads, t, dk), F32), jax.ShapeDtypeStruct((heads, t, dk), F32),
                   jax.ShapeDtypeStruct((heads, t, dv), F32), jax.ShapeDtypeStruct((t, LANES), F32),
                   jax.ShapeDtypeStruct((1, LANES), F32), jax.ShapeDtypeStruct((1, LANES), F32)],
        scratch_shapes=[pltpu.VMEM((HALO + tm, dk), F32), pltpu.VMEM((HALO + tm, dk), F32),
                        pltpu.VMEM((HALO + tm, dv), F32)], name=name,
        compiler_params=_params("arbitrary", "arbitrary"))(
            proj, proj, proj, proj, proj, proj, pab, cq, ck, cv, a_log, dt_bias, dqn, dkn, dvv, dg, dbeta)


def _conv_bwd(proj, dc, cw, *, heads, width, col, name):
    t = proj.shape[0]
    tm = _pick(t, (384, 256, 128))
    nb = t // tm
    nk = cw.shape[0]

    def body(p_ref, pp_ref, d_ref, dn_ref, w_ref, dp_ref, dw_ref, bx_ref, bd_ref):
        i = pl.program_id(1)
        first, last = i == 0, i == nb - 1

        @pl.when(first)
        def _():
            dw_ref[...] = jnp.zeros_like(dw_ref)

        d = d_ref[...]
        _stage(bx_ref, jnp.where(first, 0.0, pp_ref[...]), p_ref[...])
        _stage_after(bd_ref, d, jnp.where(last, 0.0, dn_ref[...]))
        dx = d * w_ref[nk - 1:nk, :]
        dw_ref[nk - 1:nk, :] += jnp.sum(d * p_ref[...], axis=0, keepdims=True)
        for s in range(1, nk):
            dx += bd_ref[s:s + tm, :] * w_ref[nk - 1 - s:nk - s, :]
            dw_ref[nk - 1 - s:nk - s, :] += jnp.sum(d * bx_ref[HALO - s:HALO - s + tm, :], axis=0, keepdims=True)
        dp_ref[...] = _bf(dx)

    return pl.pallas_call(
        body, grid=(heads, nb),
        in_specs=[pl.BlockSpec((tm, width), lambda h, i: (i, col // width + h)),
                  pl.BlockSpec((8, width), lambda h, i: (jnp.maximum(i * (tm // 8) - 1, 0), col // width + h)),
                  pl.BlockSpec((None, tm, width), lambda h, i: (h, i, 0)),
                  pl.BlockSpec((None, 8, width), lambda h, i: (h, jnp.minimum((i + 1) * (tm // 8), t // 8 - 1), 0)),
                  pl.BlockSpec((nk, width), lambda h, i: (0, h))],
        out_specs=[pl.BlockSpec((tm, width), lambda h, i: (i, h)), pl.BlockSpec((nk, width), lambda h, i: (0, h))],
        out_shape=[jax.ShapeDtypeStruct((t, heads * width), BF16), jax.ShapeDtypeStruct((nk, heads * width), F32)],
        scratch_shapes=[pltpu.VMEM((HALO + tm, width), F32), pltpu.VMEM((tm + HALO, width), F32)],
        name=name, compiler_params=_params("parallel", "arbitrary"))(proj, proj, dc, dc, cw)


def _sb_pre(proj, gq, gk, *, heads, dh, col_q, col_k, col_v, name):
    t = proj.shape[0]
    tm = _pick(t, (384, 256, 128))

    def body(q_ref, k_ref, v_ref, gq_ref, gk_ref, qo_ref, ko_ref, vo_ref):
        qo_ref[...] = _bf(_rms_fwd(q_ref[...], gq_ref[...]))
        ko_ref[...] = _bf(_rms_fwd(k_ref[...], gk_ref[...]))
        vo_ref[...] = _bf(v_ref[...])

    def cur(col):
        return pl.BlockSpec((tm, dh), lambda i, h: (i, col // dh + h))

    gspec = pl.BlockSpec((1, dh), lambda i, h: (0, 0))
    ospec = pl.BlockSpec((tm, dh), lambda i, h: (i, h))
    return pl.pallas_call(
        body, grid=(t // tm, heads), in_specs=[cur(col_q), cur(col_k), cur(col_v), gspec, gspec],
        out_specs=[ospec] * 3, out_shape=[jax.ShapeDtypeStruct((t, heads * dh), BF16)] * 3, name=name,
        compiler_params=_params("parallel", "parallel"))(proj, proj, proj, gq, gk)


def _sb_tile(z, i, j, blk, key_lo, masked):
    ls = jnp.minimum(z, 0.0) - jnp.log(1.0 + jnp.exp(-jnp.abs(z)))
    if not masked:
        return None, ls, ls - z
    qpos = i * blk + _iota((blk, blk), 0)
    kpos = j * blk + _iota((blk, blk), 1)
    vis = (kpos < qpos) & (kpos >= key_lo)
    return vis, ls, jnp.where(vis, ls - z, 0.0)


def _where_vis(vis, x):
    return x if vis is None else jnp.where(vis, x, 0.0)


def _sb_sweep(i, step, init, descending):
    first, last = (i, 0) if descending else (0, i)
    carry = step(first, init, True)
    carry = lax.fori_loop(1, i, lambda n, c: step(i - n if descending else n, c, False), carry)
    return lax.cond(i > 0, lambda c: step(last, c, True), lambda c: c, carry)


def _dot2_r(x, m):
    hi, lo = _split2(x)
    return _dot(hi, m) + _dot(lo, m)


def _running_sums(x, tri, reverse, exact=True):
    groups = [x[:, s:s + LANES] for s in range(0, x.shape[1], LANES)]
    inside = [_dot2_r(g, tri) if exact else _dot(_bf(g), tri) for g in groups]
    sums = [jnp.sum(g, axis=1, keepdims=True) for g in groups]
    order = list(reversed(range(len(groups)))) if reverse else list(range(len(groups)))
    out, acc = [None] * len(groups), None
    for gi in order:
        out[gi] = inside[gi] if acc is None else inside[gi] + acc
        acc = sums[gi] if acc is None else acc + sums[gi]
    return jnp.concatenate(out, axis=1), acc


def _sb_fwd(qs, ks, vs, *, heads, dh, key_lo, name, gather=None):
    t = qs.shape[0]
    blk = _pick(t, (3 * SB_BLOCK, 2 * SB_BLOCK, SB_BLOCK))
    assert key_lo <= blk
    nq = t // blk
    assert nq <= LANES
    scale = dh ** -0.5
    hp = SB_HEADS_PER_STEP

    def body(q_ref, k_ref, v_ref, o_ref, c_ref):
        i = pl.program_id(1)
        later = jnp.where(_iota((LANES, LANES), 0) > _iota((LANES, LANES), 1), 1.0, 0.0).astype(BF16)
        lane = _iota((blk, LANES), 1)
        c_ref[...] = jnp.zeros_like(c_ref)

        def step(j, carry, masked):
            rows = pl.ds(pl.multiple_of(j * blk, blk), blk)
            hs = range(hp)
            cols = [pl.ds(hh * dh, dh) for hh in hs]
            zs = [_dot_nt(q_ref[:, cols[hh]], k_ref[rows, cols[hh]]) * scale for hh in hs]
            tiles = [_sb_tile(z, i, j, blk, key_lo, masked) for z in zs]
            sufs = [_running_sums(lk, later, reverse=True) for _, _, lk in tiles]
            wgts = [_where_vis(vis, jnp.exp(ls + suf + carry[2 * hh + 1]))
                    for hh, ((vis, ls, _), (suf, _)) in enumerate(zip(tiles, sufs))]
            accs = [carry[2 * hh] + _dot(_bf(wgts[hh]), v_ref[rows, cols[hh]]) for hh in hs]
            out = []
            for hh in hs:
                c_ref[hh] = jnp.where(lane == j, carry[2 * hh + 1], c_ref[hh])
                out += [accs[hh], carry[2 * hh + 1] + sufs[hh][1]]
            return tuple(out)

        res = _sb_sweep(i, step, (jnp.zeros((blk, dh), F32), jnp.zeros((blk, 1), F32)) * hp, descending=True)
        for hh in range(hp):
            o_ref[:, pl.ds(hh * dh, dh)] = _bf(res[2 * hh])

    full = pl.BlockSpec((t, hp * dh), lambda h, i: (0, h))
    call = dict(grid=(heads // hp, nq),
                in_specs=[pl.BlockSpec((blk, hp * dh), lambda h, i: (i, h)), full, full],
                out_specs=[pl.BlockSpec((blk, hp * dh), lambda h, i: (i, h)),
                           pl.BlockSpec((hp, blk, LANES), lambda h, i: (h, i, 0))],
                out_shape=[jax.ShapeDtypeStruct((t, heads * dh), BF16), jax.ShapeDtypeStruct((heads, t, LANES), F32)],
                name=name)
    if gather is None:
        return pl.pallas_call(body, compiler_params=_params("parallel", "parallel"), **call)(qs, ks, vs)
    res = _call_with_exchange(body, scratch_shapes=[], args=(qs, ks, vs), srcs=gather, scatter=[False] * len(gather),
                              **call)
    return [*res[:2], list(res[2:])]


def _sb_bwd(qs, ks, vs, do, carry, *, heads, dh, key_lo, name, scatter=None):
    t = qs.shape[0]
    blk = _pick(t, (3 * SB_BLOCK, 2 * SB_BLOCK, SB_BLOCK))
    assert key_lo <= blk
    nq = t // blk
    scale = dh ** -0.5
    hp = SB_HEADS_PER_STEP

    def body(q_ref, k_ref, v_ref, do_ref, c_ref, dq_ref, dk_ref, dv_ref):
        i = pl.program_id(1)

        @pl.when(i == 0)
        def _():
            dk_ref[...] = jnp.zeros_like(dk_ref)
            dv_ref[...] = jnp.zeros_like(dv_ref)

        r0 = _iota((LANES, LANES), 0)
        r1 = _iota((LANES, LANES), 1)
        later = jnp.where(r0 > r1, 1.0, 0.0).astype(BF16)
        earlier = jnp.where(r0 < r1, 1.0, 0.0).astype(BF16)

        def step(j, carry, masked):
            rows = pl.ds(pl.multiple_of(j * blk, blk), blk)
            hs = range(hp)
            cols = [pl.ds(hh * dh, dh) for hh in hs]
            zs = [_dot_nt(q_ref[:, cols[hh]], k_ref[rows, cols[hh]]) * scale for hh in hs]
            dws = [_dot_nt(do_ref[:, cols[hh]], v_ref[rows, cols[hh]]) for hh in hs]
            tiles = [_sb_tile(z, i, j, blk, key_lo, masked) for z in zs]
            sufs = [_running_sums(lk, later, reverse=True)[0] for _, _, lk in tiles]
            wgts = [_where_vis(vis, jnp.exp(ls + suf + _lane_pick(c_ref[hh], j)))
                    for hh, ((vis, ls, _), suf) in enumerate(zip(tiles, sufs))]
            es = [wgt * dw for wgt, dw in zip(wgts, dws)]
            pres = [_running_sums(e, earlier, reverse=False, exact=False) for e in es]
            dzs = []
            for hh in hs:
                vis, ls, _ = tiles[hh]
                before = _where_vis(vis, pres[hh][0] + carry[2 * hh + 1])
                sig = jnp.exp(ls)
                dzs.append(_bf((es[hh] * (1.0 - sig) - before * sig) * scale))
            dks = [_dot_tn(dzs[hh], q_ref[:, cols[hh]]) for hh in hs]
            dvs = [_dot_tn(_bf(wgts[hh]), do_ref[:, cols[hh]]) for hh in hs]
            dqs = [carry[2 * hh] + _dot(dzs[hh], k_ref[rows, cols[hh]]) for hh in hs]
            out = []
            for hh in hs:
                dk_ref[rows, cols[hh]] += dks[hh]
                dv_ref[rows, cols[hh]] += dvs[hh]
                out += [dqs[hh], carry[2 * hh + 1] + pres[hh][1]]
            return tuple(out)

        res = _sb_sweep(i, step, (jnp.zeros((blk, dh), F32), jnp.zeros((blk, 1), F32)) * hp, descending=False)
        for hh in range(hp):
            dq_ref[:, pl.ds(hh * dh, dh)] = res[2 * hh]

    full = pl.BlockSpec((t, hp * dh), lambda h, i: (0, h))
    qblk = pl.BlockSpec((blk, hp * dh), lambda h, i: (i, h))
    call = dict(grid=(heads // hp, nq),
                in_specs=[qblk, full, full, qblk, pl.BlockSpec((hp, blk, LANES), lambda h, i: (h, i, 0))],
                out_specs=[qblk, full, full], out_shape=[jax.ShapeDtypeStruct((t, heads * dh), F32)] * 3, name=name)
    if scatter is None:
        return pl.pallas_call(body, compiler_params=_params("parallel", "arbitrary"), **call)(qs, ks, vs, do, carry)
    res = _call_with_exchange(body, scratch_shapes=[], args=(qs, ks, vs, do, carry), srcs=scatter,
                              scatter=[True] * len(scatter), **call)
    return [*res[:3], list(res[3:])]


def _sb_pre_bwd(proj, gq, gk, dq, dk, dv, *, heads, dh, col_q, col_k, name):
    t = proj.shape[0]
    tm = _pick(t, (384, 256, 128))

    def body(q_ref, k_ref, gq_ref, gk_ref, dq_ref, dk_ref, dv_ref, oq_ref, ok_ref, ov_ref, dgq_ref, dgk_ref):
        @pl.when((pl.program_id(0) == 0) & (pl.program_id(1) == 0))
        def _():
            dgq_ref[...] = jnp.zeros_like(dgq_ref)
            dgk_ref[...] = jnp.zeros_like(dgk_ref)

        dq_, gq_r = _rms_bwd(q_ref[...], gq_ref[...], dq_ref[...])
        dk_, gk_r = _rms_bwd(k_ref[...], gk_ref[...], dk_ref[...])
        oq_ref[...] = _bf(dq_)
        ok_ref[...] = _bf(dk_)
        ov_ref[...] = _bf(dv_ref[...])
        dgq_ref[...] += jnp.sum(gq_r, axis=0, keepdims=True)
        dgk_ref[...] += jnp.sum(gk_r, axis=0, keepdims=True)

    def cur(col):
        return pl.BlockSpec((tm, dh), lambda i, h: (i, col // dh + h))

    gspec = pl.BlockSpec((1, dh), lambda i, h: (0, 0))
    ospec = pl.BlockSpec((tm, dh), lambda i, h: (i, h))
    return pl.pallas_call(
        body, grid=(t // tm, heads), in_specs=[cur(col_q), cur(col_k), gspec, gspec, ospec, ospec, ospec],
        out_specs=[ospec, ospec, ospec, gspec, gspec],
        out_shape=[jax.ShapeDtypeStruct((t, heads * dh), BF16)] * 3 + [jax.ShapeDtypeStruct((1, dh), F32)] * 2,
        name=name, compiler_params=_params("arbitrary", "arbitrary"))(proj, proj, gq, gk, dq, dk, dv)


PEERS = N_DEV - 1


def _exchange_copies(ins, outs, send_sems, recv_sems, local_sems, scatter):
    x, y, c = lax.axis_index("x"), lax.axis_index("y"), lax.axis_index("c")
    me = 4 * x + 2 * y + c
    copies = []
    for a in range(len(ins)):
        own = ins[a].at[me] if scatter[a] else ins[a]
        copies.append(pltpu.make_async_copy(own, outs[a].at[me], local_sems.at[a]))
        for k in range(1, N_DEV):
            px = (x + (k >> 2 & 1)) % 2
            py = (y + (k >> 1 & 1)) % 2
            pc = (c + (k & 1)) % 2
            src = ins[a].at[4 * px + 2 * py + pc] if scatter[a] else ins[a]
            copies.append(pltpu.make_async_remote_copy(
                src_ref=src, dst_ref=outs[a].at[me], send_sem=send_sems.at[a * PEERS + k - 1],
                recv_sem=recv_sems.at[a * PEERS + k - 1], device_id=(px, py, pc), device_id_type=MESH))
    return copies


def _exchange_shapes(srcs, scatter):
    return [jax.ShapeDtypeStruct(s.shape if sc else (N_DEV,) + s.shape, s.dtype) for s, sc in zip(srcs, scatter)]


def _exchange_sems(n):
    return [pltpu.SemaphoreType.DMA((n * PEERS,)), pltpu.SemaphoreType.DMA((n * PEERS,)), pltpu.SemaphoreType.DMA((n,))]


def _exchange(srcs, *, scatter, name):
    n = len(srcs)

    def body(*refs):
        copies = _exchange_copies(refs[:n], refs[n:2 * n], *refs[2 * n:], scatter)
        for cp in copies:
            cp.start()
        for cp in copies:
            cp.wait()

    any_spec = pl.BlockSpec(memory_space=pl.ANY)
    return pl.pallas_call(
        body, in_specs=[any_spec] * n, out_specs=[any_spec] * n, out_shape=_exchange_shapes(srcs, scatter),
        scratch_shapes=_exchange_sems(n), name=name,
        compiler_params=pltpu.CompilerParams(has_side_effects=True))(*srcs)


def _gather_two_level(srcs, *, name):
    n = len(srcs)

    def body(*refs):
        ins, outs = refs[:n], refs[n:2 * n]
        send_sems, recv_sems, local_sems = refs[2 * n:]
        x, y, c = lax.axis_index("x"), lax.axis_index("y"), lax.axis_index("c")
        chips = [(1 - x, y), (x, 1 - y), (1 - x, 1 - y)]

        def slab(a, px, py, pc):
            return outs[a].at[4 * px + 2 * py + pc]

        def copy(a, k, block, to, src=None):
            return pltpu.make_async_remote_copy(
                src_ref=slab(a, *block) if src is None else src, dst_ref=slab(a, *block),
                send_sem=send_sems.at[a * PEERS + k], recv_sem=recv_sems.at[a * PEERS + k],
                device_id=to, device_id_type=MESH)

        mine = [pltpu.make_async_copy(ins[a], slab(a, x, y, c), local_sems.at[a]) for a in range(n)]
        first = [copy(a, 0, (x, y, c), (x, y, 1 - c), src=ins[a]) for a in range(n)]
        first += [copy(a, 1 + j, (x, y, c), (*chip, c), src=ins[a]) for j, chip in enumerate(chips) for a in range(n)]
        for cp in mine + first:
            cp.start()
        passed = []
        for j, chip in enumerate(chips):
            for a in range(n):
                copy(a, 1 + j, (*chip, c), (x, y, c)).wait_recv()
                passed.append(copy(a, 4 + j, (*chip, c), (x, y, 1 - c)))
                passed[-1].start()
        for a in range(n):
            copy(a, 0, (x, y, 1 - c), (x, y, c)).wait_recv()
            for j, chip in enumerate(chips):
                copy(a, 4 + j, (*chip, 1 - c), (x, y, c)).wait_recv()
        for cp in first + passed:
            cp.wait_send()
        for cp in mine:
            cp.wait()

    any_spec = pl.BlockSpec(memory_space=pl.ANY)
    return pl.pallas_call(
        body, in_specs=[any_spec] * n, out_specs=[any_spec] * n, out_shape=_exchange_shapes(srcs, [False] * n),
        scratch_shapes=_exchange_sems(n), name=name,
        compiler_params=pltpu.CompilerParams(has_side_effects=True))(*srcs)


def _call_with_exchange(body, *, grid, in_specs, out_specs, out_shape, scratch_shapes, args, srcs, scatter, name):
    n, n_in, n_out, n_scr = len(srcs), len(args), len(out_shape), len(scratch_shapes)

    def full_body(*refs):
        ins, xin = refs[:n_in], refs[n_in:n_in + n]
        outs, xout = refs[n_in + n:n_in + n + n_out], refs[n_in + n + n_out:n_in + 2 * n + n_out]
        scr = refs[n_in + 2 * n + n_out:]
        ids = [pl.program_id(a) for a in range(len(grid))]
        first = functools.reduce(jnp.logical_and, [i == 0 for i in ids])
        last = functools.reduce(jnp.logical_and, [i == g - 1 for i, g in zip(ids, grid)])
        copies = _exchange_copies(xin, xout, *scr[n_scr:], scatter)

        @pl.when(first)
        def _():
            for cp in copies:
                cp.start()

        body(*ins, *outs, *scr[:n_scr])

        @pl.when(last)
        def _():
            for cp in copies:
                cp.wait()

    any_spec = pl.BlockSpec(memory_space=pl.ANY)
    return pl.pallas_call(
        full_body, grid=grid, in_specs=list(in_specs) + [any_spec] * n, out_specs=list(out_specs) + [any_spec] * n,
        out_shape=list(out_shape) + _exchange_shapes(srcs, scatter),
        scratch_shapes=list(scratch_shapes) + _exchange_sems(n), name=name,
        compiler_params=pltpu.CompilerParams(dimension_semantics=("arbitrary",) * len(grid),
                                             vmem_limit_bytes=V7X_VMEM_LIMIT_BYTES, has_side_effects=True))(*args, *srcs)


def _adam_math(g, w, m, v):
    m2 = ADAM_B1 * m + (1.0 - ADAM_B1) * g
    v2 = ADAM_B2 * v + (1.0 - ADAM_B2) * (g * g)
    m_hat = m2 / (1.0 - ADAM_B1 ** ADAM_STEP)
    v_hat = v2 / (1.0 - ADAM_B2 ** ADAM_STEP)
    return -ADAM_LR * (m_hat / (jnp.sqrt(v_hat) + ADAM_EPS) + ADAM_WD * w), m2, v2


def _adamw_slabs(slabs, w, m, v, *, name):
    r, c = w.shape
    tr = next((t for t in (2256, 752, 512, 240, 128, 64, 32, 16) if r % t == 0), r)

    def body(s_ref, w_ref, m_ref, v_ref, g_ref, d_ref, mo_ref, vo_ref):
        g = s_ref[0].astype(F32)
        for p in range(1, N_DEV):
            g = g + s_ref[p].astype(F32)
        g_ref[...] = g
        d_ref[...], mo_ref[...], vo_ref[...] = _adam_math(g, w_ref[...], m_ref[...], v_ref[...])

    spec = pl.BlockSpec((tr, c), lambda i: (i, 0))
    return pl.pallas_call(
        body, grid=(r // tr,), in_specs=[pl.BlockSpec((N_DEV, tr, c), lambda i: (0, i, 0)), spec, spec, spec],
        out_specs=[spec] * 4, out_shape=[jax.ShapeDtypeStruct((r, c), F32)] * 4, name=name,
        compiler_params=_params("parallel"))(slabs, w, m, v)


def _adamw_small(g, w, m, v, *, name):
    def body(g_ref, w_ref, m_ref, v_ref, d_ref, mo_ref, vo_ref):
        d_ref[...], mo_ref[...], vo_ref[...] = _adam_math(g_ref[...], w_ref[...], m_ref[...], v_ref[...])

    return pl.pallas_call(body, out_shape=[jax.ShapeDtypeStruct(w.shape, F32)] * 3, name=name)(g, w, m, v)


def _sum_slabs(slabs, *, name):
    def body(s_ref, o_ref):
        acc = s_ref[0]
        for p in range(1, N_DEV):
            acc = acc + s_ref[p]
        o_ref[...] = acc

    return pl.pallas_call(body, out_shape=jax.ShapeDtypeStruct(slabs.shape[1:], F32), name=name)(slabs)


def _pad_lanes(a):
    return jnp.pad(a, ((0, 0), (0, LANES - a.shape[1])))


def _w_in_slabs(before, after, ab, n_ab):
    col_ab = before.shape[0]
    pw = col_ab + n_ab + after.shape[0]
    c = pw // N_DEV
    parts = [(0, col_ab, before, 0), (col_ab, col_ab + n_ab, ab, col_ab), (col_ab + n_ab, pw, after, col_ab + n_ab)]
    slabs = []
    for p in range(N_DEV):
        pieces = []
        for lo, hi, src, shift in parts:
            a, b = max(lo, c * p), min(hi, c * (p + 1))
            if a < b:
                pieces.append(src[a - shift:b - shift])
        slabs.append(jnp.concatenate(pieces, axis=0))
    return jnp.stack(slabs)


def _local_step(x, target, meta, g_mix, wt_main, wt_ab, cq, ck, cv, a_log, dt_bias, g_dn, g_sbq, g_sbk, g_ffn, rest,
                shards=False):
    seq, d = x.shape
    n_meta = meta.shape[0]
    heads = a_log.shape[1]
    qk = cq.shape[1]
    dvt = cv.shape[1]
    dk, dv = qk // heads, dvt // heads
    dh = g_sbq.shape[1]
    sbw = rest[2].shape[0] * N_DEV if shards else rest[1].shape[0]
    sb_heads = sbw // dh
    pad_l = (-n_meta) % CHUNK
    row_x = pad_l + n_meta
    rows = row_x + seq
    t = -(-rows // GDN_ROWS) * GDN_ROWS
    col_q, col_k, col_v, col_z = 0, qk, 2 * qk, 2 * qk + dvt
    col_sq = 2 * qk + 2 * dvt
    col_sk, col_sv, col_gd, col_gs = col_sq + sbw, col_sq + 2 * sbw, col_sq + 3 * sbw, col_sq + 3 * sbw + d

    def rows_pad(a):
        return jnp.concatenate([jnp.zeros((row_x, d), F32), a, jnp.zeros((t - rows, d), F32)], axis=0)

    h0 = jnp.concatenate([jnp.zeros((pad_l, d), F32), meta, x, jnp.zeros((t - rows, d), F32)], axis=0)
    tgt = rows_pad(target)
    a_log_p, dt_p = _pad_lanes(a_log), _pad_lanes(dt_bias)

    proj, n1 = _mm_norm(h0, g_mix, wt_main, name="proj")
    pab = _mm_nt(n1, wt_ab, out_dtype=F32, name="proj_ab")
    gk = dict(heads=heads, dk=dk, dv=dv, col_q=col_q, col_k=col_k, col_v=col_v, row_lo=pad_l, row_hi=rows)
    qn, kn, vv, g, beta = _gdn_pre(proj, pab, cq, ck, cv, a_log_p, dt_p, name="gdn_pre", **gk)
    u, w, pm, qd, kd, egl, tinv = _gdn_prep(qn, kn, vv, g, beta, name="gdn_prep")
    o_raw, o_dn, states = _gdn_scan(u, w, pm, qd, kd, egl, proj, g_dn, col_z=col_z, name="gdn_scan")
    qs, ks, vs = _sb_pre(proj, g_sbq, g_sbk, heads=sb_heads, dh=dh, col_q=col_sq, col_k=col_sk, col_v=col_sv,
                         name="sb_pre")
    if shards:
        o_sb, carry, (g_fi, g_bd, g_bs, g_out, g_fo) = _sb_fwd(qs, ks, vs, heads=sb_heads, dh=dh, key_lo=pad_l,
                                                                name="sb_fwd", gather=list(rest))
        wt_fi = g_fi.reshape(-1, d)
        d_ff = wt_fi.shape[0] // 2
        w_bd, w_bs, w_out, wt_fg, wt_fu, w_fo = (g_bd.reshape(-1, d), g_bs.reshape(-1, d), g_out.reshape(-1, d),
                                                 wt_fi[:d_ff], wt_fi[d_ff:], g_fo.reshape(-1, d))
    else:
        o_sb, carry = _sb_fwd(qs, ks, vs, heads=sb_heads, dh=dh, key_lo=pad_l, name="sb_fwd")
        w_bd, w_bs, w_out, wt_fg, wt_fu, w_fo = rest
    merged, br_dn, br_sb = _merge_fwd(o_dn, o_sb, w_bd, w_bs, proj, col_gd=col_gd, col_gs=col_gs, name="merge")
    h1 = _mm_res(h0, merged, w_out, name="mix_out")
    gate, up, act, n2 = _mm_norm_swiglu(h1, g_ffn, wt_fg, wt_fu, name="ffn_in")
    dy, dyb, lsum = _mm_res_loss(h1, act, w_fo, tgt, row0=row_x, nrows=seq, name="ffn_out_loss")

    dgate, dup = _swiglu_bwd(dyb, w_fo, gate, up, name="ffn_out_bwd")
    d_w_fo = _mm_tn(act, dyb, name="dw_ffn_out")
    d_wt_fg = _mm_tn(dgate, n2, name="dw_ffn_gate")
    d_wt_fu = _mm_tn(dup, n2, name="dw_ffn_up")
    dh1, dh1b, d_g_ffn = _mm_rmsbwd([(dgate, wt_fg), (dup, wt_fu)], None, h1, g_ffn, dy, name="ffn_in_bwd")

    dbd, dbs, dgd, dgs = _merge_bwd(dh1b, w_out, proj, br_dn, br_sb, col_gd=col_gd, col_gs=col_gs, name="mix_out_bwd")
    d_w_out = _mm_tn(merged, dh1b, name="dw_out")
    d_w_bd = _mm_tn(o_dn, dbd, name="dw_branch_dn")
    d_w_bs = _mm_tn(o_sb, dbs, name="dw_branch_sb")
    do_dn = _mm_nt(dbd, w_bd, out_dtype=F32, name="branch_dn_bwd")
    do_sb = _mm_nt(dbs, w_bs, out_dtype=BF16, name="branch_sb_bwd")

    do_raw, dz, d_g_dn = _gdn_post_bwd(o_raw, proj, g_dn, do_dn, col_z=col_z, name="gdn_post_bwd")
    du, dw, dp, dqd, dkd, dgl = _gdn_bwd_scan(u, w, pm, qd, kd, egl, states, do_raw, name="gdn_bwd_scan")
    dqn, dkn, dvv, dg, dbeta = _gdn_bwd_prep(qn, kn, vv, g, beta, tinv, u, w, du, dw, dp, dqd, dkd, dgl,
                                            name="gdn_bwd_prep")
    dcq, dck, dcv, dpab, d_a_log, d_dt = _gdn_pre_bwd_a(proj, pab, cq, ck, cv, a_log_p, dt_p, dqn, dkn, dvv, dg, dbeta,
                                                        name="gdn_pre_bwd", **gk)
    dpq, d_cq = _conv_bwd(proj, dcq, cq, heads=heads, width=dk, col=col_q, name="conv_q_bwd")
    dpk, d_ck = _conv_bwd(proj, dck, ck, heads=heads, width=dk, col=col_k, name="conv_k_bwd")
    dpv, d_cv = _conv_bwd(proj, dcv, cv, heads=heads, width=dv, col=col_v, name="conv_v_bwd")

    early = None
    if shards:
        slabs = [_bf(jnp.concatenate([d_wt_fg, d_wt_fu], axis=0)).reshape(N_DEV, -1, LANES)]
        slabs += [_bf(a).reshape(N_DEV, -1, d) for a in (d_w_bd, d_w_bs, d_w_out, d_w_fo)]
        dqs, dks, dvs, early = _sb_bwd(qs, ks, vs, do_sb, carry, heads=sb_heads, dh=dh, key_lo=pad_l, name="sb_bwd",
                                       scatter=slabs)
    else:
        dqs, dks, dvs = _sb_bwd(qs, ks, vs, do_sb, carry, heads=sb_heads, dh=dh, key_lo=pad_l, name="sb_bwd")
    dsq, dsk, dsv, d_g_sbq, d_g_sbk = _sb_pre_bwd(proj, g_sbq, g_sbk, dqs, dks, dvs, heads=sb_heads, dh=dh,
                                                   col_q=col_sq, col_k=col_sk, name="sb_pre_bwd")

    dproj = [dpq, dpk, dpv, dz, dsq, dsk, dsv, dgd, dgs]
    dpab_b = _bf(dpab)
    n_dn = 4
    d_wt_dn = _mm_tn(dproj[:n_dn], n1, name="dw_in_dn")
    d_wt_sb = _mm_tn(dproj[n_dn:], n1, name="dw_in_sb")
    assert d_wt_dn.shape[0] == col_sq
    d_wt_main = jnp.concatenate([d_wt_dn, d_wt_sb], axis=0)
    d_wt_ab = _mm_tn(dpab_b, n1, name="dw_in_ab")
    s_in = None
    if shards:
        slabs = _w_in_slabs(d_wt_dn, d_wt_sb, d_wt_ab, 2 * heads).reshape(N_DEV, -1, LANES)
        dh0, _, d_g_mix, (s_in,) = _mm_rmsbwd([(dproj, wt_main)], (dpab_b, wt_ab), h0, g_mix, dh1, name="proj_bwd",
                                              scatter=[slabs])
    else:
        dh0, _, d_g_mix = _mm_rmsbwd([(dproj, wt_main)], (dpab_b, wt_ab), h0, g_mix, dh1, name="proj_bwd")

    return dict(s_in=s_in, lsum=lsum, grad_x=dh0[row_x:rows], d_meta=dh0[pad_l:row_x], d_g_mix=d_g_mix,
                d_wt_main=d_wt_main, d_wt_ab=d_wt_ab, d_cq=d_cq, d_ck=d_ck, d_cv=d_cv, d_a_log=d_a_log[:, :heads],
                d_dt=d_dt[:, :heads], d_g_dn=d_g_dn, d_g_sbq=d_g_sbq, d_g_sbk=d_g_sbk, d_w_bd=d_w_bd, d_w_bs=d_w_bs,
                d_w_out=d_w_out, d_g_ffn=d_g_ffn, d_wt_fg=d_wt_fg, d_wt_fu=d_wt_fu, d_w_fo=d_w_fo, early=early)


def _pack(parts):
    flat = []
    for a in parts:
        a = a.reshape(-1)
        flat.append(jnp.pad(a, (0, (-a.shape[0]) % LANES)))
    v = jnp.concatenate(flat)
    v = jnp.pad(v, (0, (-v.shape[0]) % (8 * LANES)))
    return v.reshape(-1, LANES)


def _unpack(packed, shapes):
    flat = packed.reshape(-1)
    out, pos = [], 0
    for s in shapes:
        n = math.prod(s)
        out.append(flat[pos:pos + n].reshape(s))
        pos += n + (-n) % LANES
    return out


def kernel(x, meta_tokens, norm_mix_gain, w_in, conv_q, conv_k, conv_v, dn_a_log, dn_dt_bias, dn_out_norm_gain, sb_q_norm_gain, sb_k_norm_gain, w_branch_dn, w_branch_sb, w_out, norm_ffn_gain, w_ffn_in, w_ffn_out, loss_target, m_meta_tokens, m_norm_mix_gain, m_w_in, m_conv_q, m_conv_k, m_conv_v, m_dn_a_log, m_dn_dt_bias, m_dn_out_norm_gain, m_sb_q_norm_gain, m_sb_k_norm_gain, m_w_branch_dn, m_w_branch_sb, m_w_out, m_norm_ffn_gain, m_w_ffn_in, m_w_ffn_out, v_meta_tokens, v_norm_mix_gain, v_w_in, v_conv_q, v_conv_k, v_conv_v, v_dn_a_log, v_dn_dt_bias, v_dn_out_norm_gain, v_sb_q_norm_gain, v_sb_k_norm_gain, v_w_branch_dn, v_w_branch_sb, v_w_out, v_norm_ffn_gain, v_w_ffn_in, v_w_ffn_out):
    me = 4 * lax.axis_index("x") + 2 * lax.axis_index("y") + lax.axis_index("c")
    heads = dn_a_log.shape[1]
    d = x.shape[2]
    qk = conv_q.shape[2] * N_DEV
    dvt = conv_v.shape[2] * N_DEV
    col_ab = 2 * qk + 2 * dvt

    small_shapes = [meta_tokens.shape, conv_q.shape[1:], conv_k.shape[1:], conv_v.shape[1:]]
    small = _pack([meta_tokens, conv_q[0], conv_k[0], conv_v[0]])
    def features_major(a):
        return jnp.transpose(a, (2, 0, 1)).reshape(a.shape[2], a.shape[1])

    g_in, g_small = _gather_two_level([_bf(features_major(w_in)), small], name="gather_w_in")
    wt_full = g_in.reshape(-1, d)
    wt_main = jnp.concatenate([wt_full[:col_ab], wt_full[col_ab + 2 * heads:]], axis=0)
    wt_ab = jnp.pad(wt_full[col_ab:col_ab + 2 * heads], ((0, LANES - 2 * heads), (0, 0)))
    parts = [_unpack(g_small[p], small_shapes) for p in range(N_DEV)]
    meta_f, cq_f, ck_f, cv_f = (jnp.concatenate([parts[p][a] for p in range(N_DEV)], axis=1) for a in range(4))

    r = _local_step(x[0], loss_target[0], meta_f, norm_mix_gain, wt_main, wt_ab, cq_f, ck_f, cv_f, dn_a_log, dn_dt_bias,
                    dn_out_norm_gain, sb_q_norm_gain, sb_k_norm_gain, norm_ffn_gain,
                    (_bf(features_major(w_ffn_in)), _bf(w_branch_dn[0]), _bf(w_branch_sb[0]), _bf(w_out[0]),
                     _bf(w_ffn_out[0])), shards=True)
    s_fi, s_bd, s_bs, s_out, s_fo = r["early"]
    s_in = r["s_in"]

    loss_part = (0.5 / d) * jnp.sum(r["lsum"], axis=1, keepdims=True)
    small_g = [r["d_meta"], r["d_g_mix"], r["d_cq"], r["d_ck"], r["d_cv"], r["d_a_log"], r["d_dt"], r["d_g_dn"],
               r["d_g_sbq"], r["d_g_sbk"], r["d_g_ffn"], loss_part]
    (g_packs,) = _exchange([_pack(small_g)], scatter=[False], name="gather_small_grads")
    (g_meta, g_mix, g_cq, g_ck, g_cv, g_al, g_dt, g_gdn, g_sbq, g_sbk, g_ffn, loss) = _unpack(
        _sum_slabs(g_packs, name="sum_small_grads"), [a.shape for a in small_g])

    def mine(a, width):
        return lax.dynamic_slice_in_dim(a, me * width, width, axis=1)

    big = dict(w_in=(s_in, w_in, m_w_in, v_w_in), w_branch_dn=(s_bd, w_branch_dn, m_w_branch_dn, v_w_branch_dn),
               w_branch_sb=(s_bs, w_branch_sb, m_w_branch_sb, v_w_branch_sb), w_out=(s_out, w_out, m_w_out, v_w_out),
               w_ffn_in=(s_fi, w_ffn_in, m_w_ffn_in, v_w_ffn_in), w_ffn_out=(s_fo, w_ffn_out, m_w_ffn_out, v_w_ffn_out))
    tiny = dict(meta_tokens=(mine(g_meta, d // N_DEV), meta_tokens, m_meta_tokens, v_meta_tokens),
                norm_mix_gain=(g_mix, norm_mix_gain, m_norm_mix_gain, v_norm_mix_gain),
                conv_q=(mine(g_cq, qk // N_DEV), conv_q[0], m_conv_q[0], v_conv_q[0]),
                conv_k=(mine(g_ck, qk // N_DEV), conv_k[0], m_conv_k[0], v_conv_k[0]),
                conv_v=(mine(g_cv, dvt // N_DEV), conv_v[0], m_conv_v[0], v_conv_v[0]),
                dn_a_log=(g_al, dn_a_log, m_dn_a_log, v_dn_a_log), dn_dt_bias=(g_dt, dn_dt_bias, m_dn_dt_bias, v_dn_dt_bias),
                dn_out_norm_gain=(g_gdn, dn_out_norm_gain, m_dn_out_norm_gain, v_dn_out_norm_gain),
                sb_q_norm_gain=(g_sbq, sb_q_norm_gain, m_sb_q_norm_gain, v_sb_q_norm_gain),
                sb_k_norm_gain=(g_sbk, sb_k_norm_gain, m_sb_k_norm_gain, v_sb_k_norm_gain),
                norm_ffn_gain=(g_ffn, norm_ffn_gain, m_norm_ffn_gain, v_norm_ffn_gain))
    order = ["meta_tokens", "norm_mix_gain", "w_in", "conv_q", "conv_k", "conv_v", "dn_a_log", "dn_dt_bias",
             "dn_out_norm_gain", "sb_q_norm_gain", "sb_k_norm_gain", "w_branch_dn", "w_branch_sb", "w_out",
             "norm_ffn_gain", "w_ffn_in", "w_ffn_out"]
    grads, deltas, new_m, new_v = [], [], [], []
    for name in order:
        if name in ("w_in", "w_ffn_in"):
            slabs, w, m, v = big[name]
            res = _adamw_slabs(slabs, *(features_major(a).reshape(-1, LANES) for a in (w, m, v)), name="adamw_" + name)
            g, dl, mo, vo = (jnp.transpose(a.reshape(w.shape[2], 1, w.shape[1]), (1, 2, 0)) for a in res)
            like = w.shape
        elif name in big:
            slabs, w, m, v = big[name]
            g, dl, mo, vo = _adamw_slabs(slabs, w[0], m[0], v[0], name="adamw_" + name)
            like = w.shape
        else:
            g, w, m, v = tiny[name]
            like = dict(conv_q=conv_q, conv_k=conv_k, conv_v=conv_v).get(name, w).shape
            dl, mo, vo = _adamw_small(g, w, m, v, name="adamw_" + name)
        for lst, a in ((grads, g), (deltas, dl), (new_m, mo), (new_v, vo)):
            lst.append(a.reshape(like))
    return (loss.reshape(()), r["grad_x"][None], *grads, *deltas, *new_m, *new_v)
```

```python
import functools
import math

import jax
import jax.numpy as jnp
from jax import lax
from jax.experimental import pallas as pl
from jax.experimental.pallas import tpu as pltpu

F32 = jnp.float32
BF16 = jnp.bfloat16

N_DEV = 8
CHUNK = 64
CHUNK_SHIFT = 6
GDN_ROWS = 2 * CHUNK
SB_BLOCK = 128
SB_HEADS_PER_STEP = 2
LANES = 128
RMS_EPS = 1e-6
L2_EPS = 1e-6
ADAM_LR = 0.001
ADAM_B1 = 0.9
ADAM_B2 = 0.999
ADAM_EPS = 1e-08
ADAM_WD = 0.01
ADAM_STEP = 10
V7X_VMEM_LIMIT_BYTES = 56 * 1024 * 1024
MM_TN_OUT_BLOCK_BYTES = 6 * 1024 * 1024
MM_PIECES_VMEM_BYTES = V7X_VMEM_LIMIT_BYTES // 2
ROWS_BIG = (1056, 512, 384, 256, 128)
ROWS_MID = (528, 384, 256, 128)
SCAN_HEADS = 4
PREP_HEADS = 2

MESH = pl.DeviceIdType.MESH


def _params(*sem):
    return pltpu.CompilerParams(dimension_semantics=sem or None, vmem_limit_bytes=V7X_VMEM_LIMIT_BYTES)


def _pick(n, cands):
    for c in cands:
        if n % c == 0:
            return c
    raise ValueError(f"no block size among {cands} divides {n}")


def _bf(x):
    return x.astype(BF16)


def _dot(a, b):
    return jnp.dot(a, b, preferred_element_type=F32)


def _dot_nt(a, b):
    return lax.dot_general(a, b, (((1,), (1,)), ((), ())), preferred_element_type=F32)


def _dot_tn(a, b):
    return lax.dot_general(a, b, (((0,), (0,)), ((), ())), preferred_element_type=F32)


def _split2(x):
    hi = _bf(x)
    return hi, _bf(x - hi.astype(F32))


def _split3(x):
    hi = _bf(x)
    r = x - hi.astype(F32)
    mid = _bf(r)
    return hi, mid, _bf(r - mid.astype(F32))


def _dot_hp(a, b, dot=_dot):
    ah, al = _split2(a)
    bh, bl = _split2(b)
    return dot(ah, bh) + dot(ah, bl) + dot(al, bh)


def _dot_exact_l(m, x, dot=_dot):
    h, mi, lo = _split3(x)
    return dot(m, h) + dot(m, mi) + dot(m, lo)


def _sigmoid(x):
    return 1.0 / (1.0 + jnp.exp(-x))


def _silu(x):
    return x * _sigmoid(x)


def _silu_grad(x):
    s = _sigmoid(x)
    return s * (1.0 + x * (1.0 - s))


def _softplus(x):
    return jnp.maximum(x, 0.0) + jnp.log(1.0 + jnp.exp(-jnp.abs(x)))


def _rms_fwd(h, gain):
    r = lax.rsqrt(jnp.mean(h * h, axis=-1, keepdims=True) + RMS_EPS)
    return h * r * gain


def _rms_bwd(h, gain, dy):
    r = lax.rsqrt(jnp.mean(h * h, axis=-1, keepdims=True) + RMS_EPS)
    dyg = dy * gain
    dh = r * dyg - h * (r * r * r) * jnp.mean(dyg * h, axis=-1, keepdims=True)
    return dh, dy * h * r


def _iota(shape, dim):
    return lax.broadcasted_iota(jnp.int32, shape, dim)


def _lane_pick(x, idx):
    return jnp.sum(jnp.where(_iota(x.shape, 1) == idx, x, 0.0), axis=1, keepdims=True)


def _mm_nt(a, b, *, out_dtype, name):
    m, k = a.shape
    n = b.shape[0]
    tm, tn = _pick(m, ROWS_BIG), _pick(n, (1024, 512, 256, 128))

    def body(a_ref, b_ref, o_ref):
        o_ref[...] = _dot_nt(a_ref[...], b_ref[...]).astype(out_dtype)

    return pl.pallas_call(
        body, grid=(m // tm, n // tn),
        in_specs=[pl.BlockSpec((tm, k), lambda i, j: (i, 0)), pl.BlockSpec((tn, k), lambda i, j: (j, 0))],
        out_specs=pl.BlockSpec((tm, tn), lambda i, j: (i, j)),
        out_shape=jax.ShapeDtypeStruct((m, n), out_dtype), name=name,
        compiler_params=_params("parallel", "parallel"))(a, b)


def _column_pieces(a):
    pieces = list(a) if isinstance(a, (list, tuple)) else [a]
    widths = [p.shape[1] for p in pieces]
    return pieces, widths, functools.reduce(math.gcd, widths)


def _piece_spans(widths, block):
    ends = [sum(widths[:j + 1]) // block for j in range(len(widths))]
    return list(zip([0] + ends[:-1], ends))


def _mm_tn(a, b, *, name):
    pieces, widths, unit = _column_pieces(a)
    t, m = pieces[0].shape[0], sum(widths)
    n = b.shape[1]
    tn = _pick(n, (2816, 2048, 1408, 1024, 512, 256, 128))
    tm = _pick(unit, tuple(c for c in (1408, 1024, 512, 256, 128) if c * tn * 4 <= MM_TN_OUT_BLOCK_BYTES))
    held = 2 * len(pieces) * tm * pieces[0].dtype.itemsize
    tk = _pick(t, tuple(c for c in (1408, 1024, 704, 512, 384, 256, 128) if c * held <= MM_PIECES_VMEM_BYTES))
    nk = t // tk
    spans = _piece_spans(widths, tm)

    def body(*refs):
        a_refs, (b_ref, o_ref, acc_ref) = refs[:len(pieces)], refs[len(pieces):]
        i, k = pl.program_id(0), pl.program_id(2)

        @pl.when(k == 0)
        def _():
            acc_ref[...] = jnp.zeros_like(acc_ref)

        for a_ref, (lo, hi) in zip(a_refs, spans):
            def add(a_ref=a_ref):
                acc_ref[...] += _dot_tn(a_ref[...], b_ref[...])
            if len(pieces) == 1:
                add()
            else:
                pl.when((i >= lo) & (i < hi))(add)

        @pl.when(k == nk - 1)
        def _():
            o_ref[...] = _bf(acc_ref[...])

    def a_spec(lo, hi):
        if len(pieces) == 1:
            return pl.BlockSpec((tk, tm), lambda i, j, k: (k, i))
        return pl.BlockSpec((tk, tm), lambda i, j, k: (jnp.where(i < lo, 0, jnp.where(i >= hi, nk - 1, k)),
                                                       jnp.clip(i - lo, 0, hi - lo - 1)))

    return pl.pallas_call(
        body, grid=(m // tm, n // tn, nk),
        in_specs=[a_spec(lo, hi) for lo, hi in spans] + [pl.BlockSpec((tk, tn), lambda i, j, k: (k, j))],
        out_specs=pl.BlockSpec((tm, tn), lambda i, j, k: (i, j)),
        out_shape=jax.ShapeDtypeStruct((m, n), BF16), scratch_shapes=[pltpu.VMEM((tm, tn), F32)], name=name,
        compiler_params=_params("parallel", "parallel", "arbitrary"))(*pieces, b)


def _mm_norm(h, gain, wt, *, name):
    m, k = h.shape
    n = wt.shape[0]
    tm, tn = _pick(m, ROWS_BIG), _pick(n, (1024, 512, 256, 128))

    def body(h_ref, g_ref, w_ref, o_ref, n_ref):
        @pl.when(pl.program_id(1) == 0)
        def _():
            n_ref[...] = _bf(_rms_fwd(h_ref[...], g_ref[...]))

        o_ref[...] = _dot_nt(n_ref[...], w_ref[...])

    return pl.pallas_call(
        body, grid=(m // tm, n // tn),
        in_specs=[pl.BlockSpec((tm, k), lambda i, j: (i, 0)), pl.BlockSpec((1, k), lambda i, j: (0, 0)),
                  pl.BlockSpec((tn, k), lambda i, j: (j, 0))],
        out_specs=[pl.BlockSpec((tm, tn), lambda i, j: (i, j)), pl.BlockSpec((tm, k), lambda i, j: (i, 0))],
        out_shape=[jax.ShapeDtypeStruct((m, n), F32), jax.ShapeDtypeStruct((m, k), BF16)], name=name,
        compiler_params=_params("parallel", "arbitrary"))(h, gain, wt)


def _mm_norm_swiglu(h, gain, wgt, wut, *, name):
    m, k = h.shape
    n = wgt.shape[0]
    tm, tn = _pick(m, ROWS_MID), _pick(n, (1408, 1024, 512, 256, 128))

    def body(h_ref, g_ref, wg_ref, wu_ref, gate_ref, up_ref, act_ref, n_ref):
        @pl.when(pl.program_id(1) == 0)
        def _():
            n_ref[...] = _bf(_rms_fwd(h_ref[...], g_ref[...]))

        gate = _dot_nt(n_ref[...], wg_ref[...])
        up = _dot_nt(n_ref[...], wu_ref[...])
        gate_ref[...] = gate
        up_ref[...] = up
        act_ref[...] = _bf(_silu(gate) * up)

    wspec = pl.BlockSpec((tn, k), lambda i, j: (j, 0))
    ospec = pl.BlockSpec((tm, tn), lambda i, j: (i, j))
    return pl.pallas_call(
        body, grid=(m // tm, n // tn),
        in_specs=[pl.BlockSpec((tm, k), lambda i, j: (i, 0)), pl.BlockSpec((1, k), lambda i, j: (0, 0)), wspec, wspec],
        out_specs=[ospec, ospec, ospec, pl.BlockSpec((tm, k), lambda i, j: (i, 0))],
        out_shape=[jax.ShapeDtypeStruct((m, n), F32), jax.ShapeDtypeStruct((m, n), F32),
                   jax.ShapeDtypeStruct((m, n), BF16), jax.ShapeDtypeStruct((m, k), BF16)], name=name,
        compiler_params=_params("parallel", "arbitrary"))(h, gain, wgt, wut)


def _mm_res(res, a, b, *, name):
    m, k = a.shape
    n = b.shape[1]
    tm, tn = _pick(m, ROWS_BIG), _pick(n, (1024, 512, 256, 128))

    def body(r_ref, a_ref, b_ref, o_ref):
        o_ref[...] = r_ref[...] + _dot(a_ref[...], b_ref[...])

    return pl.pallas_call(
        body, grid=(m // tm, n // tn),
        in_specs=[pl.BlockSpec((tm, tn), lambda i, j: (i, j)), pl.BlockSpec((tm, k), lambda i, j: (i, 0)),
                  pl.BlockSpec((k, tn), lambda i, j: (0, j))],
        out_specs=pl.BlockSpec((tm, tn), lambda i, j: (i, j)),
        out_shape=jax.ShapeDtypeStruct((m, n), F32), name=name,
        compiler_params=_params("parallel", "parallel"))(res, a, b)


def _mm_res_loss(res, a, b, target, *, row0, nrows, name):
    m, k = a.shape
    n = b.shape[1]
    tm = _pick(m, ROWS_MID)

    def body(r_ref, a_ref, b_ref, t_ref, dy_ref, dyb_ref, ls_ref):
        i = pl.program_id(0)

        @pl.when(i == 0)
        def _():
            ls_ref[...] = jnp.zeros_like(ls_ref)

        y = r_ref[...] + _dot(a_ref[...], b_ref[...])
        row = i * tm + _iota((tm, n), 0)
        e = jnp.where((row >= row0) & (row < row0 + nrows), y - t_ref[...], 0.0)
        dy = e / n
        dy_ref[...] = dy
        dyb_ref[...] = _bf(dy)
        ls_ref[...] += jnp.sum(e * e, axis=0, keepdims=True)

    rspec = pl.BlockSpec((tm, n), lambda i: (i, 0))
    return pl.pallas_call(
        body, grid=(m // tm,),
        in_specs=[rspec, pl.BlockSpec((tm, k), lambda i: (i, 0)), pl.BlockSpec((k, n), lambda i: (0, 0)), rspec],
        out_specs=[rspec, rspec, pl.BlockSpec((1, n), lambda i: (0, 0))],
        out_shape=[jax.ShapeDtypeStruct((m, n), F32), jax.ShapeDtypeStruct((m, n), BF16),
                   jax.ShapeDtypeStruct((1, n), F32)], name=name,
        compiler_params=_params("arbitrary"))(res, a, b, target)


def _merge_fwd(o_dn, o_sb, wbd, wbs, proj, *, col_gd, col_gs, name):
    m, kd = o_dn.shape
    ks = o_sb.shape[1]
    n = wbd.shape[1]
    tm = _pick(m, ROWS_BIG)
    tn = _pick(math.gcd(n, math.gcd(col_gd, col_gs)), (512, 256, 128))

    def body(od_ref, os_ref, wd_ref, ws_ref, gd_ref, gs_ref, mg_ref, bd_ref, bs_ref):
        bd = _dot(od_ref[...], wd_ref[...])
        bs = _dot(os_ref[...], ws_ref[...])
        bd_ref[...] = bd
        bs_ref[...] = bs
        mg_ref[...] = _bf(_sigmoid(gd_ref[...]) * bd + _sigmoid(gs_ref[...]) * bs)

    ospec = pl.BlockSpec((tm, tn), lambda i, j: (i, j))
    return pl.pallas_call(
        body, grid=(m // tm, n // tn),
        in_specs=[pl.BlockSpec((tm, kd), lambda i, j: (i, 0)), pl.BlockSpec((tm, ks), lambda i, j: (i, 0)),
                  pl.BlockSpec((kd, tn), lambda i, j: (0, j)), pl.BlockSpec((ks, tn), lambda i, j: (0, j)),
                  pl.BlockSpec((tm, tn), lambda i, j: (i, col_gd // tn + j)),
                  pl.BlockSpec((tm, tn), lambda i, j: (i, col_gs // tn + j))],
        out_specs=[ospec, ospec, ospec],
        out_shape=[jax.ShapeDtypeStruct((m, n), BF16), jax.ShapeDtypeStruct((m, n), F32),
                   jax.ShapeDtypeStruct((m, n), F32)], name=name,
        compiler_params=_params("parallel", "parallel"))(o_dn, o_sb, wbd, wbs, proj, proj)


def _merge_bwd(dh, w_out, proj, br_dn, br_sb, *, col_gd, col_gs, name):
    m, k = dh.shape
    n = w_out.shape[0]
    tm = _pick(m, ROWS_BIG)
    tn = _pick(math.gcd(n, math.gcd(col_gd, col_gs)), (512, 256, 128))

    def body(dh_ref, w_ref, gd_ref, gs_ref, bd_ref, bs_ref, dbd_ref, dbs_ref, dgd_ref, dgs_ref):
        dm = _dot_nt(dh_ref[...], w_ref[...])
        sd = _sigmoid(gd_ref[...])
        ss = _sigmoid(gs_ref[...])
        dbd_ref[...] = _bf(dm * sd)
        dbs_ref[...] = _bf(dm * ss)
        dgd_ref[...] = _bf(dm * bd_ref[...] * sd * (1.0 - sd))
        dgs_ref[...] = _bf(dm * bs_ref[...] * ss * (1.0 - ss))

    ospec = pl.BlockSpec((tm, tn), lambda i, j: (i, j))
    return pl.pallas_call(
        body, grid=(m // tm, n // tn),
        in_specs=[pl.BlockSpec((tm, k), lambda i, j: (i, 0)), pl.BlockSpec((tn, k), lambda i, j: (j, 0)),
                  pl.BlockSpec((tm, tn), lambda i, j: (i, col_gd // tn + j)),
                  pl.BlockSpec((tm, tn), lambda i, j: (i, col_gs // tn + j)), ospec, ospec],
        out_specs=[ospec] * 4,
        out_shape=[jax.ShapeDtypeStruct((m, n), BF16)] * 4, name=name,
        compiler_params=_params("parallel", "parallel"))(dh, w_out, proj, proj, br_dn, br_sb)


def _swiglu_bwd(dy, wfo, gate, up, *, name):
    m, k = dy.shape
    n = wfo.shape[0]
    tm, tn = _pick(m, ROWS_MID), _pick(n, (1408, 1024, 512, 256, 128))

    def body(dy_ref, w_ref, g_ref, u_ref, dg_ref, du_ref):
        da = _dot_nt(dy_ref[...], w_ref[...])
        g = g_ref[...]
        dg_ref[...] = _bf(da * u_ref[...] * _silu_grad(g))
        du_ref[...] = _bf(da * _silu(g))

    ospec = pl.BlockSpec((tm, tn), lambda i, j: (i, j))
    return pl.pallas_call(
        body, grid=(m // tm, n // tn),
        in_specs=[pl.BlockSpec((tm, k), lambda i, j: (i, 0)), pl.BlockSpec((tn, k), lambda i, j: (j, 0)), ospec, ospec],
        out_specs=[ospec, ospec], out_shape=[jax.ShapeDtypeStruct((m, n), BF16)] * 2, name=name,
        compiler_params=_params("parallel", "parallel"))(dy, wfo, gate, up)


def _mm_rmsbwd(pairs, extra, h, gain, dres, *, name, scatter=None):
    split = [_column_pieces(a) for a, _ in pairs]
    m, k = h.shape[0], sum(split[0][1])
    n = h.shape[1]
    tm = _pick(m, ROWS_MID)
    tk = _pick(functools.reduce(math.gcd, [unit for _, _, unit in split]), (1408, 1024, 512, 256, 128))
    nk = k // tk
    spans = [_piece_spans(widths, tk) for _, widths, _ in split]
    n_ab = sum(len(s) + 1 for s in spans)

    def body(*refs):
        ex = refs[n_ab:n_ab + 2] if extra is not None else ()
        h_ref, g_ref, r_ref, dh_ref, dhb_ref, dg_ref, acc_ref = refs[n_ab + len(ex):]
        i, kk = pl.program_id(0), pl.program_id(1)

        @pl.when((i == 0) & (kk == 0))
        def _():
            dg_ref[...] = jnp.zeros_like(dg_ref)

        @pl.when(kk == 0)
        def _():
            acc_ref[...] = _dot(ex[0][...], ex[1][...]) if ex else jnp.zeros_like(acc_ref)

        pos, whole = 0, []
        for s in spans:
            a_refs, b_ref = refs[pos:pos + len(s)], refs[pos + len(s)]
            pos += len(s) + 1
            if len(s) == 1:
                whole.append(_dot(a_refs[0][...], b_ref[...]))
                continue
            for a_ref, (lo, hi) in zip(a_refs, s):
                def add(a_ref=a_ref, b_ref=b_ref):
                    acc_ref[...] += _dot(a_ref[...], b_ref[...])
                pl.when((kk >= lo) & (kk < hi))(add)
        if whole:
            acc_ref[...] += functools.reduce(lambda x, y: x + y, whole)

        @pl.when(kk == nk - 1)
        def _():
            dh, dgr = _rms_bwd(h_ref[...], g_ref[...], acc_ref[...])
            dh = dh + r_ref[...]
            dh_ref[...] = dh
            dhb_ref[...] = _bf(dh)
            dg_ref[...] += jnp.sum(dgr, axis=0, keepdims=True)

    in_specs, args = [], []
    def a_spec(lo, hi, alone):
        if alone:
            return pl.BlockSpec((tm, tk), lambda i, kk: (i, kk))
        return pl.BlockSpec((tm, tk), lambda i, kk: (i, jnp.clip(kk - lo, 0, hi - lo - 1)))

    for (pieces, _, _), s, (_, b) in zip(split, spans, pairs):
        in_specs += [a_spec(lo, hi, len(s) == 1) for lo, hi in s] + [pl.BlockSpec((tk, n), lambda i, kk: (kk, 0))]
        args += [*pieces, b]
    if extra is not None:
        k2 = extra[0].shape[1]
        in_specs += [pl.BlockSpec((tm, k2), lambda i, kk: (i, 0)), pl.BlockSpec((k2, n), lambda i, kk: (0, 0))]
        args += list(extra)
    rspec = pl.BlockSpec((tm, n), lambda i, kk: (i, 0))
    in_specs += [rspec, pl.BlockSpec((1, n), lambda i, kk: (0, 0)), rspec]
    call = dict(grid=(m // tm, nk), in_specs=in_specs,
                out_specs=[rspec, rspec, pl.BlockSpec((1, n), lambda i, kk: (0, 0))],
                out_shape=[jax.ShapeDtypeStruct((m, n), F32), jax.ShapeDtypeStruct((m, n), BF16),
                           jax.ShapeDtypeStruct((1, n), F32)],
                scratch_shapes=[pltpu.VMEM((tm, n), F32)], name=name)
    if scatter is None:
        return pl.pallas_call(body, compiler_params=_params("arbitrary", "arbitrary"), **call)(*args, h, gain, dres)
    res = _call_with_exchange(body, args=(*args, h, gain, dres), srcs=scatter, scatter=[True] * len(scatter), **call)
    return [*res[:3], list(res[3:])]


HALO = 8


def _stage(buf_ref, before, cur):
    buf_ref[0:HALO, :] = before
    buf_ref[HALO:HALO + cur.shape[0], :] = cur


def _stage_after(buf_ref, cur, after):
    r = cur.shape[0]
    buf_ref[0:r, :] = cur
    buf_ref[r:r + HALO, :] = after


def _conv_taps(buf_ref, rows, w_ref):
    nk = w_ref.shape[0]
    out = buf_ref[HALO:HALO + rows, :] * w_ref[nk - 1:nk, :]
    for s in range(1, nk):
        out += buf_ref[HALO - s:HALO - s + rows, :] * w_ref[nk - 1 - s:nk - s, :]
    return out


def _gdn_pre(proj, pab, cq, ck, cv, a_log, dt_bias, *, heads, dk, dv, col_q, col_k, col_v, row_lo, row_hi, name):
    t = proj.shape[0]
    tm = _pick(t, (384, 256, 128))
    nb = t // tm

    def body(pq_ref, pqp_ref, pk_ref, pkp_ref, pv_ref, pvp_ref, ab_ref, cq_ref, ck_ref, cv_ref, al_ref, dt_ref,
             qn_ref, kn_ref, v_ref, g_ref, b_ref, bq_ref, bk_ref, bv_ref):
        h, i = pl.program_id(0), pl.program_id(1)
        first = i == 0
        row = i * tm + _iota((tm, 1), 0)
        valid = (row >= row_lo) & (row < row_hi)
        _stage(bq_ref, jnp.where(first, 0.0, pqp_ref[...]), pq_ref[...])
        _stage(bk_ref, jnp.where(first, 0.0, pkp_ref[...]), pk_ref[...])
        _stage(bv_ref, jnp.where(first, 0.0, pvp_ref[...]), pv_ref[...])
        q1 = _silu(_conv_taps(bq_ref, tm, cq_ref))
        k1 = _silu(_conv_taps(bk_ref, tm, ck_ref))
        v1 = _silu(_conv_taps(bv_ref, tm, cv_ref))
        qn_ref[...] = jnp.where(valid, q1 * lax.rsqrt(jnp.sum(q1 * q1, axis=-1, keepdims=True) + L2_EPS), 0.0)
        kn_ref[...] = jnp.where(valid, k1 * lax.rsqrt(jnp.sum(k1 * k1, axis=-1, keepdims=True) + L2_EPS), 0.0)
        v_ref[...] = jnp.where(valid, v1, 0.0)
        ab = ab_ref[...]
        da = _lane_pick(ab, h)
        db = _lane_pick(ab, heads + h)
        a = _lane_pick(al_ref[...], h)
        dtb = _lane_pick(dt_ref[...], h)
        g_ref[...] = jnp.where(valid, -jnp.exp(a) * _softplus(da + dtb), 0.0)
        b_ref[...] = jnp.where(valid, _sigmoid(db), 0.0)

    def cur(width, col):
        return pl.BlockSpec((tm, width), lambda h, i: (i, col // width + h))

    def prev(width, col):
        return pl.BlockSpec((8, width), lambda h, i: (jnp.maximum(i * (tm // 8) - 1, 0), col // width + h))

    def out(width):
        return pl.BlockSpec((None, tm, width), lambda h, i: (h, i, 0))

    small = pl.BlockSpec((1, LANES), lambda h, i: (0, 0))
    return pl.pallas_call(
        body, grid=(heads, nb),
        in_specs=[cur(dk, col_q), prev(dk, col_q), cur(dk, col_k), prev(dk, col_k), cur(dv, col_v), prev(dv, col_v),
                  pl.BlockSpec((tm, LANES), lambda h, i: (i, 0)),
                  pl.BlockSpec((cq.shape[0], dk), lambda h, i: (0, h)), pl.BlockSpec((ck.shape[0], dk), lambda h, i: (0, h)),
                  pl.BlockSpec((cv.shape[0], dv), lambda h, i: (0, h)), small, small],
        out_specs=[out(dk), out(dk), out(dv), out(1), out(1)],
        out_shape=[jax.ShapeDtypeStruct((heads, t, dk), F32), jax.ShapeDtypeStruct((heads, t, dk), F32),
                   jax.ShapeDtypeStruct((heads, t, dv), F32), jax.ShapeDtypeStruct((heads, t, 1), F32),
                   jax.ShapeDtypeStruct((heads, t, 1), F32)],
        scratch_shapes=[pltpu.VMEM((HALO + tm, dk), F32), pltpu.VMEM((HALO + tm, dk), F32),
                        pltpu.VMEM((HALO + tm, dv), F32)], name=name,
        compiler_params=_params("parallel", "parallel"))(proj, proj, proj, proj, proj, proj, pab, cq, ck, cv, a_log, dt_bias)


def _chunk_masks(rows=GDN_ROWS, row0=0):
    ri = row0 + _iota((rows, GDN_ROWS), 0)
    ci = _iota((rows, GDN_ROWS), 1)
    same = jnp.right_shift(ri, CHUNK_SHIFT) == jnp.right_shift(ci, CHUNK_SHIFT)
    return same, same & (ri >= ci), same & (ri > ci), ri == ci


def _col_to_row(col, eye):
    return jnp.sum(jnp.where(eye, col, 0.0), axis=0, keepdims=True)


def _row_to_col(row, eye):
    return jnp.sum(jnp.where(eye, row, 0.0), axis=1, keepdims=True)


def _chunk_common(blocks, dk_scale):
    same, incl, strict, eye = _chunk_masks()
    tri = jnp.where(incl, 1.0, 0.0).astype(BF16)
    tot = jnp.where(same, 1.0, 0.0).astype(BF16)
    gbs = [jnp.broadcast_to(g, (GDN_ROWS, LANES)) for _, _, g, _ in blocks]
    gams = [jnp.max(_dot_exact_l(tri, gb), axis=1, keepdims=True) for gb in gbs]
    lasts = [jnp.max(_dot_exact_l(tot, gb), axis=1, keepdims=True) for gb in gbs]
    kbs = [kn * beta for _, kn, _, beta in blocks]
    qts = [qn * dk_scale for qn, _, _, _ in blocks]
    boths = [_dot_nt(_bf(jnp.concatenate([kb, qt], axis=0)), _bf(blk[1]))
             for kb, qt, blk in zip(kbs, qts, blocks)]
    out = []
    for gam, last, kb, qt, both in zip(gams, lasts, kbs, qts, boths):
        diff = gam - _col_to_row(gam, eye)
        decay = jnp.where(incl, jnp.exp(jnp.where(incl, diff, 0.0)), 0.0)
        out.append(dict(incl=incl, strict=strict, eye=eye, decay=decay, eg=jnp.exp(gam), ek=jnp.exp(last - gam),
                        egl=jnp.exp(last), kb=kb, qt=qt, lmat=jnp.where(strict, both[:GDN_ROWS] * decay, 0.0),
                        pmat=jnp.where(incl, both[GDN_ROWS:] * decay, 0.0)))
    return out


def _gdn_prep(qn, kn, v, g, beta, *, name):
    heads, t, dk = qn.shape
    dv = v.shape[2]
    rows = _pick(t, (3 * GDN_ROWS, 2 * GDN_ROWS, GDN_ROWS))
    dk_scale = dk ** -0.5

    hp = PREP_HEADS

    def body(q_ref, k_ref, v_ref, g_ref, b_ref, u_ref, w_ref, p_ref, qd_ref, kd_ref, egl_ref, t_ref):
        rs = [(hh, pl.ds(b * GDN_ROWS, GDN_ROWS), slice(None)) for hh in range(hp) for b in range(rows // GDN_ROWS)]
        cs = _chunk_common([(q_ref[r], k_ref[r], g_ref[r], b_ref[r]) for r in rs], dk_scale)
        eye_f = jnp.where(cs[0]["eye"], 1.0, 0.0)
        tinvs = [eye_f - c["lmat"] for c in cs]
        ys = [_dot_hp(c["lmat"], c["lmat"]) for c in cs]
        for _ in range(CHUNK_SHIFT - 1):
            boths = [_dot_hp(y, jnp.concatenate([y, tinv], axis=1)) for y, tinv in zip(ys, tinvs)]
            ys = [both[:, :GDN_ROWS] for both in boths]
            tinvs = [tinv + both[:, GDN_ROWS:] for tinv, both in zip(tinvs, boths)]
        uws = [_dot_hp(tinv, jnp.concatenate([v_ref[r] * b_ref[r], c["kb"] * c["eg"]], axis=1))
               for r, c, tinv in zip(rs, cs, tinvs)]
        for r, c, tinv, uw in zip(rs, cs, tinvs, uws):
            u_ref[r] = uw[:, :dv]
            w_ref[r] = _bf(uw[:, dv:])
            p_ref[r] = _bf(c["pmat"])
            qd_ref[r] = _bf(c["qt"] * c["eg"])
            kd_ref[r] = _bf(k_ref[r] * c["ek"])
            egl_ref[r] = c["egl"]
            t_ref[r] = tinv

    def blk(width):
        return pl.BlockSpec((hp, rows, width), lambda h, i: (h, i, 0))

    def shp(width, dtype=F32):
        return jax.ShapeDtypeStruct((heads, t, width), dtype)

    return pl.pallas_call(
        body, grid=(heads // hp, t // rows), in_specs=[blk(dk), blk(dk), blk(dv), blk(1), blk(1)],
        out_specs=[blk(dv), blk(dk), blk(GDN_ROWS), blk(dk), blk(dk), blk(1), blk(GDN_ROWS)],
        out_shape=[shp(dv), shp(dk, BF16), shp(GDN_ROWS, BF16), shp(dk, BF16), shp(dk, BF16), shp(1), shp(GDN_ROWS)],
        name=name,
        compiler_params=_params("parallel", "parallel"))(qn, kn, v, g, beta)


def _gdn_scan(u, w, p, qd, kd, egl, proj, gain, *, col_z, name):
    heads, t, dv = u.shape
    dk = w.shape[2]
    nb = t // GDN_ROWS
    sub = GDN_ROWS // CHUNK
    hp = SCAN_HEADS

    def body(u_ref, w_ref, p_ref, qd_ref, kd_ref, egl_ref, z_ref, gn_ref, o_ref, og_ref, st_ref, s_ref):
        @pl.when(pl.program_id(1) == 0)
        def _():
            s_ref[...] = jnp.zeros_like(s_ref)

        hs = range(hp)
        vn_parts = [[jnp.zeros((CHUNK, dv), F32)] * sub for _ in hs]
        for c in range(sub):
            r = pl.ds(c * CHUNK, CHUNK)
            ss = [s_ref[hh] for hh in hs]
            sbs = [_bf(s) for s in ss]
            wss = [_dot(_bf(jnp.concatenate([w_ref[hh, r, :], qd_ref[hh, r, :]], axis=0)), sbs[hh])
                   for hh in hs]
            vns = [u_ref[hh, r, :] - wss[hh][:CHUNK] for hh in hs]
            for hh in hs:
                vn_parts[hh][c] = vns[hh]
            os_ = [wss[hh][CHUNK:] + _dot(_bf(p_ref[hh, r, :]), _bf(jnp.concatenate(vn_parts[hh], axis=0))) for hh in hs]
            new = [ss[hh] * egl_ref[hh, pl.ds(c * CHUNK, 1), :] + _dot_tn(_bf(kd_ref[hh, r, :]), _bf(vns[hh])) for hh in hs]
            for hh in hs:
                cols = pl.ds(hh * dv, dv)
                st_ref[hh, c] = sbs[hh]
                s_ref[hh] = new[hh]
                o_ref[hh, r, :] = os_[hh]
                og_ref[r, cols] = _bf(_rms_fwd(os_[hh], gn_ref[...]) * _silu(z_ref[r, cols]))

    def blk(width):
        return pl.BlockSpec((hp, GDN_ROWS, width), lambda h, i: (h, i, 0))

    return pl.pallas_call(
        body, grid=(heads // hp, nb),
        in_specs=[blk(dv), blk(dk), blk(GDN_ROWS), blk(dk), blk(dk), blk(1),
                  pl.BlockSpec((GDN_ROWS, hp * dv), lambda h, i: (i, col_z // (hp * dv) + h)),
                  pl.BlockSpec((1, dv), lambda h, i: (0, 0))],
        out_specs=[blk(dv), pl.BlockSpec((GDN_ROWS, hp * dv), lambda h, i: (i, h)),
                   pl.BlockSpec((hp, sub, dk, dv), lambda h, i: (h, i, 0, 0))],
        out_shape=[jax.ShapeDtypeStruct((heads, t, dv), F32), jax.ShapeDtypeStruct((t, heads * dv), BF16),
                   jax.ShapeDtypeStruct((heads, t // CHUNK, dk, dv), BF16)],
        scratch_shapes=[pltpu.VMEM((hp, dk, dv), F32)], name=name,
        compiler_params=_params("parallel", "arbitrary"))(u, w, p, qd, kd, egl, proj, gain)


def _gdn_post_bwd(o, proj, gain, dout, *, col_z, name):
    heads, t, dv = o.shape
    tm = _pick(t, (384, 256, 128))

    def body(o_ref, z_ref, gn_ref, d_ref, do_ref, dz_ref, dg_ref):
        @pl.when((pl.program_id(0) == 0) & (pl.program_id(1) == 0))
        def _():
            dg_ref[...] = jnp.zeros_like(dg_ref)

        o_, z, d = o_ref[...], z_ref[...], d_ref[...]
        y = _rms_fwd(o_, gn_ref[...])
        dz_ref[...] = _bf(d * y * _silu_grad(z))
        do, dgr = _rms_bwd(o_, gn_ref[...], d * _silu(z))
        do_ref[...] = do
        dg_ref[...] += jnp.sum(dgr, axis=0, keepdims=True)

    return pl.pallas_call(
        body, grid=(t // tm, heads),
        in_specs=[pl.BlockSpec((None, tm, dv), lambda i, h: (h, i, 0)),
                  pl.BlockSpec((tm, dv), lambda i, h: (i, col_z // dv + h)),
                  pl.BlockSpec((1, dv), lambda i, h: (0, 0)), pl.BlockSpec((tm, dv), lambda i, h: (i, h))],
        out_specs=[pl.BlockSpec((None, tm, dv), lambda i, h: (h, i, 0)), pl.BlockSpec((tm, dv), lambda i, h: (i, h)),
                   pl.BlockSpec((1, dv), lambda i, h: (0, 0))],
        out_shape=[jax.ShapeDtypeStruct((heads, t, dv), F32), jax.ShapeDtypeStruct((t, heads * dv), BF16),
                   jax.ShapeDtypeStruct((1, dv), F32)], name=name,
        compiler_params=_params("arbitrary", "arbitrary"))(o, proj, gain, dout)


def _gdn_bwd_scan(u, w, p, qd, kd, egl, st, do, *, name):
    heads, t, dv = u.shape
    dk = w.shape[2]
    nb = t // GDN_ROWS
    sub = GDN_ROWS // CHUNK
    hp = SCAN_HEADS

    def body(u_ref, w_ref, p_ref, qd_ref, kd_ref, egl_ref, st_ref, do_ref,
             du_ref, dw_ref, dp_ref, dqd_ref, dkd_ref, dgl_ref, ds_ref):
        @pl.when(pl.program_id(1) == 0)
        def _():
            ds_ref[...] = jnp.zeros_like(ds_ref)

        hs = range(hp)
        zeros = jnp.zeros((CHUNK, dv), BF16)
        for c in reversed(range(sub)):
            r = pl.ds(c * CHUNK, CHUNK)
            sbs = [st_ref[hh, c] for hh in hs]
            dss = [ds_ref[hh] for hh in hs]
            dsbs = [_bf(ds) for ds in dss]
            dobs = [_bf(do_ref[hh, r, :]) for hh in hs]
            wbs = [_bf(w_ref[hh, r, :]) for hh in hs]
            vns = [u_ref[hh, r, :] - _dot(wbs[hh], sbs[hh]) for hh in hs]
            dvns = [_dot_tn(_bf(p_ref[hh, r, :]), dobs[hh])[c * CHUNK:(c + 1) * CHUNK, :]
                    + _dot(_bf(kd_ref[hh, r, :]), dsbs[hh]) for hh in hs]
            dods = [jnp.concatenate([dobs[hh], _bf(dvns[hh])], axis=0) for hh in hs]
            boths = [_dot_nt(dods[hh], sbs[hh]) for hh in hs]
            dps = [_dot_nt(dobs[hh], jnp.concatenate([_bf(vns[hh]) if cc == c else zeros for cc in range(sub)], axis=0))
                   for hh in hs]
            dkds = [_dot_nt(_bf(vns[hh]), dsbs[hh]) for hh in hs]
            new = [dss[hh] * egl_ref[hh, pl.ds(c * CHUNK, 1), :]
                   + _dot_tn(jnp.concatenate([_bf(qd_ref[hh, r, :]), -wbs[hh]], axis=0), dods[hh])
                   for hh in hs]
            for hh in hs:
                du_ref[hh, r, :] = dvns[hh]
                dw_ref[hh, r, :] = -boths[hh][CHUNK:]
                dp_ref[hh, r, :] = jnp.where(_chunk_masks(CHUNK, c * CHUNK)[1], dps[hh], 0.0)
                dqd_ref[hh, r, :] = boths[hh][:CHUNK]
                dkd_ref[hh, r, :] = dkds[hh]
                dgl = jnp.sum(jnp.sum(dss[hh] * sbs[hh].astype(F32), axis=1, keepdims=True), axis=0, keepdims=True)
                dgl_ref[hh, r, :] = jnp.where(_iota((CHUNK, 1), 0) == CHUNK - 1, dgl, 0.0)
                ds_ref[hh] = new[hh]

    def blk(width):
        return pl.BlockSpec((hp, GDN_ROWS, width), lambda h, i: (h, nb - 1 - i, 0))

    def shp(width):
        return jax.ShapeDtypeStruct((heads, t, width), F32)

    return pl.pallas_call(
        body, grid=(heads // hp, nb),
        in_specs=[blk(dv), blk(dk), blk(GDN_ROWS), blk(dk), blk(dk), blk(1),
                  pl.BlockSpec((hp, sub, dk, dv), lambda h, i: (h, nb - 1 - i, 0, 0)), blk(dv)],
        out_specs=[blk(dv), blk(dk), blk(GDN_ROWS), blk(dk), blk(dk), blk(1)],
        out_shape=[shp(dv), shp(dk), shp(GDN_ROWS), shp(dk), shp(dk), shp(1)],
        scratch_shapes=[pltpu.VMEM((hp, dk, dv), F32)], name=name,
        compiler_params=_params("parallel", "arbitrary"))(u, w, p, qd, kd, egl, st, do)


def _gdn_bwd_prep(qn, kn, v, g, beta, tinv, u, w, du, dw, dp, dqd, dkd, dgl, *, name):
    heads, t, dk = qn.shape
    dv = v.shape[2]
    rows = _pick(t, (3 * GDN_ROWS, 2 * GDN_ROWS, GDN_ROWS))
    dk_scale = dk ** -0.5
    hp = PREP_HEADS

    def rowsum(x):
        return jnp.sum(x, axis=1, keepdims=True)

    def body(q_ref, k_ref, v_ref, g_ref, b_ref, t_ref, u_ref, w_ref, du_ref, dw_ref, dp_ref, dqd_ref, dkd_ref, dgl_ref,
             dq_ref, dkk_ref, dvv_ref, dg_ref, db_ref):
        rs = [(hh, pl.ds(b * GDN_ROWS, GDN_ROWS), slice(None)) for hh in range(hp) for b in range(rows // GDN_ROWS)]
        cs = _chunk_common([(q_ref[r], k_ref[r], g_ref[r], b_ref[r]) for r in rs], dk_scale)
        dbvws = [_dot_hp(t_ref[r], jnp.concatenate([du_ref[r], dw_ref[r]], axis=1), _dot_tn)
                 for r in rs]
        das = [-_dot_nt(_bf(dbvw), jnp.concatenate([_bf(u_ref[r]), w_ref[r]], axis=1))
               for r, dbvw in zip(rs, dbvws)]
        dls = [jnp.where(c["strict"], da, 0.0) for c, da in zip(cs, das)]
        dmns = [_bf(jnp.concatenate([dl * c["decay"], dp_ref[r] * c["decay"]], axis=0))
                for r, c, dl in zip(rs, cs, dls)]
        boths = [_dot(dmn, _bf(k_ref[r])) for r, dmn in zip(rs, dmns)]
        dkns = [_dot_tn(dmn, _bf(jnp.concatenate([c["kb"], c["qt"]], axis=0)))
                for c, dmn in zip(cs, dmns)]
        for r, c, dbvw, dl, both, dkn in zip(rs, cs, dbvws, dls, boths, dkns):
            kn_, beta_, v_ = k_ref[r], b_ref[r], v_ref[r]
            eye = c["eye"]
            kb, qt, eg, ek = c["kb"], c["qt"], c["eg"], c["ek"]
            dbv, dbw = dbvw[:, :dv], dbvw[:, dv:]
            dp_ = dp_ref[r]
            dkb = both[:GDN_ROWS] + dbw * eg
            dqt = both[GDN_ROWS:]
            gmat = dl * c["lmat"] + dp_ * c["pmat"]
            dqd_, dkd_ = dqd_ref[r], dkd_ref[r]
            qd = qt * eg
            kd = kn_ * ek
            bw = kb * eg
            kdsum = rowsum(dkd_ * kd)
            dgam = rowsum(gmat) - _row_to_col(jnp.sum(gmat, axis=0, keepdims=True), eye)
            dgam += rowsum(dbw * bw) + rowsum(dqd_ * qd) - kdsum
            last = (_iota((GDN_ROWS, 1), 0) & (CHUNK - 1)) == CHUNK - 1
            same = _chunk_masks()[0]
            same_f = jnp.where(same, 1.0, 0.0).astype(BF16)
            chunk_tot = jnp.max(_dot_exact_l(same_f, jnp.broadcast_to(kdsum, (GDN_ROWS, LANES))), axis=1, keepdims=True)
            dgam += jnp.where(last, chunk_tot, 0.0) + dgl_ref[r] * c["egl"]
            dq_ref[r] = (dqt + dqd_ * eg) * dk_scale
            dkk_ref[r] = dkn + dkd_ * ek + dkb * beta_
            dvv_ref[r] = dbv * beta_
            db_ref[r] = rowsum(dbv * v_) + rowsum(dkb * kn_)
            upper = jnp.where(same & (_iota((GDN_ROWS, GDN_ROWS), 0) <= _iota((GDN_ROWS, GDN_ROWS), 1)), 1.0, 0.0)
            dgb = _dot_exact_l(upper.astype(BF16), jnp.broadcast_to(dgam, (GDN_ROWS, LANES)))
            dg_ref[r] = _lane_pick(dgb, 0)

    def blk(width):
        return pl.BlockSpec((hp, rows, width), lambda h, i: (h, i, 0))

    def shp(width):
        return jax.ShapeDtypeStruct((heads, t, width), F32)

    return pl.pallas_call(
        body, grid=(heads // hp, t // rows),
        in_specs=[blk(dk), blk(dk), blk(dv), blk(1), blk(1), blk(GDN_ROWS), blk(dv), blk(dk),
                  blk(dv), blk(dk), blk(GDN_ROWS), blk(dk), blk(dk), blk(1)],
        out_specs=[blk(dk), blk(dk), blk(dv), blk(1), blk(1)],
        out_shape=[shp(dk), shp(dk), shp(dv), shp(1), shp(1)], name=name,
        compiler_params=_params("parallel", "parallel"))(qn, kn, v, g, beta, tinv, u, w, du, dw, dp, dqd, dkd, dgl)


def _gdn_pre_bwd_a(proj, pab, cq, ck, cv, a_log, dt_bias, dqn, dkn, dvv, dg, dbeta, *,
                   heads, dk, dv, col_q, col_k, col_v, row_lo, row_hi, name):
    t = proj.shape[0]
    tm = _pick(t, (384, 256, 128))
    nb = t // tm

    def body(pq_ref, pqp_ref, pk_ref, pkp_ref, pv_ref, pvp_ref, ab_ref, cq_ref, ck_ref, cv_ref, al_ref, dt_ref,
             dqn_ref, dkn_ref, dvv_ref, dg_ref, db_ref, dcq_ref, dck_ref, dcv_ref, dab_ref, dal_ref, ddt_ref,
             bq_ref, bk_ref, bv_ref):
        i, h = pl.program_id(0), pl.program_id(1)
        first = i == 0

        @pl.when((i == 0) & (h == 0))
        def _():
            dal_ref[...] = jnp.zeros_like(dal_ref)
            ddt_ref[...] = jnp.zeros_like(ddt_ref)

        @pl.when(h == 0)
        def _():
            dab_ref[...] = jnp.zeros_like(dab_ref)

        row = i * tm + _iota((tm, 1), 0)
        valid = (row >= row_lo) & (row < row_hi)

        def l2_bwd(c1, dn):
            x1 = _silu(c1)
            r = lax.rsqrt(jnp.sum(x1 * x1, axis=-1, keepdims=True) + L2_EPS)
            dn = jnp.where(valid, dn, 0.0)
            d1 = r * dn - x1 * (r * r * r) * jnp.sum(dn * x1, axis=-1, keepdims=True)
            return d1 * _silu_grad(c1)

        _stage(bq_ref, jnp.where(first, 0.0, pqp_ref[...]), pq_ref[...])
        _stage(bk_ref, jnp.where(first, 0.0, pkp_ref[...]), pk_ref[...])
        _stage(bv_ref, jnp.where(first, 0.0, pvp_ref[...]), pv_ref[...])
        dcq_ref[...] = l2_bwd(_conv_taps(bq_ref, tm, cq_ref), dqn_ref[...])
        dck_ref[...] = l2_bwd(_conv_taps(bk_ref, tm, ck_ref), dkn_ref[...])
        cv1 = _conv_taps(bv_ref, tm, cv_ref)
        dcv_ref[...] = jnp.where(valid, dvv_ref[...], 0.0) * _silu_grad(cv1)
        ab = ab_ref[...]
        da = _lane_pick(ab, h)
        db = _lane_pick(ab, heads + h)
        a = _lane_pick(al_ref[...], h)
        dtb = _lane_pick(dt_ref[...], h)
        dgv = jnp.where(valid, dg_ref[...], 0.0)
        ea = jnp.exp(a)
        g = -ea * _softplus(da + dtb)
        dda = dgv * (-ea) * _sigmoid(da + dtb)
        beta = _sigmoid(db)
        ddb = jnp.where(valid, db_ref[...], 0.0) * beta * (1.0 - beta)
        lane = _iota((tm, LANES), 1)
        dab_ref[...] += jnp.where(lane == h, dda, 0.0) + jnp.where(lane == heads + h, ddb, 0.0)
        lane1 = _iota((1, LANES), 1)
        dal_ref[...] += jnp.where(lane1 == h, jnp.sum(dgv * g, axis=0, keepdims=True), 0.0)
        ddt_ref[...] += jnp.where(lane1 == h, jnp.sum(dda, axis=0, keepdims=True), 0.0)

    def cur(width, col):
        return pl.BlockSpec((tm, width), lambda i, h: (i, col // width + h))

    def prev(width, col):
        return pl.BlockSpec((8, width), lambda i, h: (jnp.maximum(i * (tm // 8) - 1, 0), col // width + h))

    def hd(width):
        return pl.BlockSpec((None, tm, width), lambda i, h: (h, i, 0))

    small = pl.BlockSpec((1, LANES), lambda i, h: (0, 0))
    return pl.pallas_call(
        body, grid=(nb, heads),
        in_specs=[cur(dk, col_q), prev(dk, col_q), cur(dk, col_k), prev(dk, col_k), cur(dv, col_v), prev(dv, col_v),
                  pl.BlockSpec((tm, LANES), lambda i, h: (i, 0)),
                  pl.BlockSpec((cq.shape[0], dk), lambda i, h: (0, h)), pl.BlockSpec((ck.shape[0], dk), lambda i, h: (0, h)),
                  pl.BlockSpec((cv.shape[0], dv), lambda i, h: (0, h)), small, small,
                  hd(dk), hd(dk), hd(dv), hd(1), hd(1)],
        out_specs=[hd(dk), hd(dk), hd(dv), pl.BlockSpec((tm, LANES), lambda i, h: (i, 0)), small, small],
        out_shape=[jax.ShapeDtypeStruct((heads, t, dk), F32), jax.ShapeDtypeStruct((heads, t, dk), F32),
                   jax.ShapeDtypeStruct((heads, t, dv), F32), jax.ShapeDtypeStruct((t, LANES), F32),
                   jax.ShapeDtypeStruct((1, LANES), F32), jax.ShapeDtypeStruct((1, LANES), F32)],
        scratch_shapes=[pltpu.VMEM((HALO + tm, dk), F32), pltpu.VMEM((HALO + tm, dk), F32),
                        pltpu.VMEM((HALO + tm, dv), F32)], name=name,
        compiler_params=_params("arbitrary", "arbitrary"))(
            proj, proj, proj, proj, proj, proj, pab, cq, ck, cv, a_log, dt_bias, dqn, dkn, dvv, dg, dbeta)


def _conv_bwd(proj, dc, cw, *, heads, width, col, name):
    t = proj.shape[0]
    tm = _pick(t, (384, 256, 128))
    nb = t // tm
    nk = cw.shape[0]

    def body(p_ref, pp_ref, d_ref, dn_ref, w_ref, dp_ref, dw_ref, bx_ref, bd_ref):
        i = pl.program_id(1)
        first, last = i == 0, i == nb - 1

        @pl.when(first)
        def _():
            dw_ref[...] = jnp.zeros_like(dw_ref)

        d = d_ref[...]
        _stage(bx_ref, jnp.where(first, 0.0, pp_ref[...]), p_ref[...])
        _stage_after(bd_ref, d, jnp.where(last, 0.0, dn_ref[...]))
        dx = d * w_ref[nk - 1:nk, :]
        dw_ref[nk - 1:nk, :] += jnp.sum(d * p_ref[...], axis=0, keepdims=True)
        for s in range(1, nk):
            dx += bd_ref[s:s + tm, :] * w_ref[nk - 1 - s:nk - s, :]
            dw_ref[nk - 1 - s:nk - s, :] += jnp.sum(d * bx_ref[HALO - s:HALO - s + tm, :], axis=0, keepdims=True)
        dp_ref[...] = _bf(dx)

    return pl.pallas_call(
        body, grid=(heads, nb),
        in_specs=[pl.BlockSpec((tm, width), lambda h, i: (i, col // width + h)),
                  pl.BlockSpec((8, width), lambda h, i: (jnp.maximum(i * (tm // 8) - 1, 0), col // width + h)),
                  pl.BlockSpec((None, tm, width), lambda h, i: (h, i, 0)),
                  pl.BlockSpec((None, 8, width), lambda h, i: (h, jnp.minimum((i + 1) * (tm // 8), t // 8 - 1), 0)),
                  pl.BlockSpec((nk, width), lambda h, i: (0, h))],
        out_specs=[pl.BlockSpec((tm, width), lambda h, i: (i, h)), pl.BlockSpec((nk, width), lambda h, i: (0, h))],
        out_shape=[jax.ShapeDtypeStruct((t, heads * width), BF16), jax.ShapeDtypeStruct((nk, heads * width), F32)],
        scratch_shapes=[pltpu.VMEM((HALO + tm, width), F32), pltpu.VMEM((tm + HALO, width), F32)],
        name=name, compiler_params=_params("parallel", "arbitrary"))(proj, proj, dc, dc, cw)


def _sb_pre(proj, gq, gk, *, heads, dh, col_q, col_k, col_v, name):
    t = proj.shape[0]
    tm = _pick(t, (384, 256, 128))

    def body(q_ref, k_ref, v_ref, gq_ref, gk_ref, qo_ref, ko_ref, vo_ref):
        qo_ref[...] = _bf(_rms_fwd(q_ref[...], gq_ref[...]))
        ko_ref[...] = _bf(_rms_fwd(k_ref[...], gk_ref[...]))
        vo_ref[...] = _bf(v_ref[...])

    def cur(col):
        return pl.BlockSpec((tm, dh), lambda i, h: (i, col // dh + h))

    gspec = pl.BlockSpec((1, dh), lambda i, h: (0, 0))
    ospec = pl.BlockSpec((tm, dh), lambda i, h: (i, h))
    return pl.pallas_call(
        body, grid=(t // tm, heads), in_specs=[cur(col_q), cur(col_k), cur(col_v), gspec, gspec],
        out_specs=[ospec] * 3, out_shape=[jax.ShapeDtypeStruct((t, heads * dh), BF16)] * 3, name=name,
        compiler_params=_params("parallel", "parallel"))(proj, proj, proj, gq, gk)


def _sb_tile(z, i, j, blk, key_lo, masked):
    ls = jnp.minimum(z, 0.0) - jnp.log(1.0 + jnp.exp(-jnp.abs(z)))
    if not masked:
        return None, ls, ls - z
    qpos = i * blk + _iota((blk, blk), 0)
    kpos = j * blk + _iota((blk, blk), 1)
    vis = (kpos < qpos) & (kpos >= key_lo)
    return vis, ls, jnp.where(vis, ls - z, 0.0)


def _where_vis(vis, x):
    return x if vis is None else jnp.where(vis, x, 0.0)


def _sb_sweep(i, step, init, descending):
    first, last = (i, 0) if descending else (0, i)
    carry = step(first, init, True)
    carry = lax.fori_loop(1, i, lambda n, c: step(i - n if descending else n, c, False), carry)
    return lax.cond(i > 0, lambda c: step(last, c, True), lambda c: c, carry)


def _dot2_r(x, m):
    hi, lo = _split2(x)
    return _dot(hi, m) + _dot(lo, m)


def _running_sums(x, tri, reverse, exact=True):
    groups = [x[:, s:s + LANES] for s in range(0, x.shape[1], LANES)]
    inside = [_dot2_r(g, tri) if exact else _dot(_bf(g), tri) for g in groups]
    sums = [jnp.sum(g, axis=1, keepdims=True) for g in groups]
    order = list(reversed(range(len(groups)))) if reverse else list(range(len(groups)))
    out, acc = [None] * len(groups), None
    for gi in order:
        out[gi] = inside[gi] if acc is None else inside[gi] + acc
        acc = sums[gi] if acc is None else acc + sums[gi]
    return jnp.concatenate(out, axis=1), acc


def _sb_fwd(qs, ks, vs, *, heads, dh, key_lo, name, gather=None):
    t = qs.shape[0]
    blk = _pick(t, (3 * SB_BLOCK, 2 * SB_BLOCK, SB_BLOCK))
    assert key_lo <= blk
    nq = t // blk
    assert nq <= LANES
    scale = dh ** -0.5
    hp = SB_HEADS_PER_STEP

    def body(q_ref, k_ref, v_ref, o_ref, c_ref):
        i = pl.program_id(1)
        later = jnp.where(_iota((LANES, LANES), 0) > _iota((LANES, LANES), 1), 1.0, 0.0).astype(BF16)
        lane = _iota((blk, LANES), 1)
        c_ref[...] = jnp.zeros_like(c_ref)

        def step(j, carry, masked):
            rows = pl.ds(pl.multiple_of(j * blk, blk), blk)
            hs = range(hp)
            cols = [pl.ds(hh * dh, dh) for hh in hs]
            zs = [_dot_nt(q_ref[:, cols[hh]], k_ref[rows, cols[hh]]) * scale for hh in hs]
            tiles = [_sb_tile(z, i, j, blk, key_lo, masked) for z in zs]
            sufs = [_running_sums(lk, later, reverse=True) for _, _, lk in tiles]
            wgts = [_where_vis(vis, jnp.exp(ls + suf + carry[2 * hh + 1]))
                    for hh, ((vis, ls, _), (suf, _)) in enumerate(zip(tiles, sufs))]
            accs = [carry[2 * hh] + _dot(_bf(wgts[hh]), v_ref[rows, cols[hh]]) for hh in hs]
            out = []
            for hh in hs:
                c_ref[hh] = jnp.where(lane == j, carry[2 * hh + 1], c_ref[hh])
                out += [accs[hh], carry[2 * hh + 1] + sufs[hh][1]]
            return tuple(out)

        res = _sb_sweep(i, step, (jnp.zeros((blk, dh), F32), jnp.zeros((blk, 1), F32)) * hp, descending=True)
        for hh in range(hp):
            o_ref[:, pl.ds(hh * dh, dh)] = _bf(res[2 * hh])

    full = pl.BlockSpec((t, hp * dh), lambda h, i: (0, h))
    call = dict(grid=(heads // hp, nq),
                in_specs=[pl.BlockSpec((blk, hp * dh), lambda h, i: (i, h)), full, full],
                out_specs=[pl.BlockSpec((blk, hp * dh), lambda h, i: (i, h)),
                           pl.BlockSpec((hp, blk, LANES), lambda h, i: (h, i, 0))],
                out_shape=[jax.ShapeDtypeStruct((t, heads * dh), BF16), jax.ShapeDtypeStruct((heads, t, LANES), F32)],
                name=name)
    if gather is None:
        return pl.pallas_call(body, compiler_params=_params("parallel", "parallel"), **call)(qs, ks, vs)
    res = _call_with_exchange(body, scratch_shapes=[], args=(qs, ks, vs), srcs=gather, scatter=[False] * len(gather),
                              **call)
    return [*res[:2], list(res[2:])]


def _sb_bwd(qs, ks, vs, do, carry, *, heads, dh, key_lo, name, scatter=None):
    t = qs.shape[0]
    blk = _pick(t, (3 * SB_BLOCK, 2 * SB_BLOCK, SB_BLOCK))
    assert key_lo <= blk
    nq = t // blk
    scale = dh ** -0.5
    hp = SB_HEADS_PER_STEP

    def body(q_ref, k_ref, v_ref, do_ref, c_ref, dq_ref, dk_ref, dv_ref):
        i = pl.program_id(1)

        @pl.when(i == 0)
        def _():
            dk_ref[...] = jnp.zeros_like(dk_ref)
            dv_ref[...] = jnp.zeros_like(dv_ref)

        r0 = _iota((LANES, LANES), 0)
        r1 = _iota((LANES, LANES), 1)
        later = jnp.where(r0 > r1, 1.0, 0.0).astype(BF16)
        earlier = jnp.where(r0 < r1, 1.0, 0.0).astype(BF16)

        def step(j, carry, masked):
            rows = pl.ds(pl.multiple_of(j * blk, blk), blk)
            hs = range(hp)
            cols = [pl.ds(hh * dh, dh) for hh in hs]
            zs = [_dot_nt(q_ref[:, cols[hh]], k_ref[rows, cols[hh]]) * scale for hh in hs]
            dws = [_dot_nt(do_ref[:, cols[hh]], v_ref[rows, cols[hh]]) for hh in hs]
            tiles = [_sb_tile(z, i, j, blk, key_lo, masked) for z in zs]
            sufs = [_running_sums(lk, later, reverse=True)[0] for _, _, lk in tiles]
            wgts = [_where_vis(vis, jnp.exp(ls + suf + _lane_pick(c_ref[hh], j)))
                    for hh, ((vis, ls, _), suf) in enumerate(zip(tiles, sufs))]
            es = [wgt * dw for wgt, dw in zip(wgts, dws)]
            pres = [_running_sums(e, earlier, reverse=False, exact=False) for e in es]
            dzs = []
            for hh in hs:
                vis, ls, _ = tiles[hh]
                before = _where_vis(vis, pres[hh][0] + carry[2 * hh + 1])
                sig = jnp.exp(ls)
                dzs.append(_bf((es[hh] * (1.0 - sig) - before * sig) * scale))
            dks = [_dot_tn(dzs[hh], q_ref[:, cols[hh]]) for hh in hs]
            dvs = [_dot_tn(_bf(wgts[hh]), do_ref[:, cols[hh]]) for hh in hs]
            dqs = [carry[2 * hh] + _dot(dzs[hh], k_ref[rows, cols[hh]]) for hh in hs]
            out = []
            for hh in hs:
                dk_ref[rows, cols[hh]] += dks[hh]
                dv_ref[rows, cols[hh]] += dvs[hh]
                out += [dqs[hh], carry[2 * hh + 1] + pres[hh][1]]
            return tuple(out)

        res = _sb_sweep(i, step, (jnp.zeros((blk, dh), F32), jnp.zeros((blk, 1), F32)) * hp, descending=False)
        for hh in range(hp):
            dq_ref[:, pl.ds(hh * dh, dh)] = res[2 * hh]

    full = pl.BlockSpec((t, hp * dh), lambda h, i: (0, h))
    qblk = pl.BlockSpec((blk, hp * dh), lambda h, i: (i, h))
    call = dict(grid=(heads // hp, nq),
                in_specs=[qblk, full, full, qblk, pl.BlockSpec((hp, blk, LANES), lambda h, i: (h, i, 0))],
                out_specs=[qblk, full, full], out_shape=[jax.ShapeDtypeStruct((t, heads * dh), F32)] * 3, name=name)
    if scatter is None:
        return pl.pallas_call(body, compiler_params=_params("parallel", "arbitrary"), **call)(qs, ks, vs, do, carry)
    res = _call_with_exchange(body, scratch_shapes=[], args=(qs, ks, vs, do, carry), srcs=scatter,
                              scatter=[True] * len(scatter), **call)
    return [*res[:3], list(res[3:])]


def _sb_pre_bwd(proj, gq, gk, dq, dk, dv, *, heads, dh, col_q, col_k, name):
    t = proj.shape[0]
    tm = _pick(t, (384, 256, 128))

    def body(q_ref, k_ref, gq_ref, gk_ref, dq_ref, dk_ref, dv_ref, oq_ref, ok_ref, ov_ref, dgq_ref, dgk_ref):
        @pl.when((pl.program_id(0) == 0) & (pl.program_id(1) == 0))
        def _():
            dgq_ref[...] = jnp.zeros_like(dgq_ref)
            dgk_ref[...] = jnp.zeros_like(dgk_ref)

        dq_, gq_r = _rms_bwd(q_ref[...], gq_ref[...], dq_ref[...])
        dk_, gk_r = _rms_bwd(k_ref[...], gk_ref[...], dk_ref[...])
        oq_ref[...] = _bf(dq_)
        ok_ref[...] = _bf(dk_)
        ov_ref[...] = _bf(dv_ref[...])
        dgq_ref[...] += jnp.sum(gq_r, axis=0, keepdims=True)
        dgk_ref[...] += jnp.sum(gk_r, axis=0, keepdims=True)

    def cur(col):
        return pl.BlockSpec((tm, dh), lambda i, h: (i, col // dh + h))

    gspec = pl.BlockSpec((1, dh), lambda i, h: (0, 0))
    ospec = pl.BlockSpec((tm, dh), lambda i, h: (i, h))
    return pl.pallas_call(
        body, grid=(t // tm, heads), in_specs=[cur(col_q), cur(col_k), gspec, gspec, ospec, ospec, ospec],
        out_specs=[ospec, ospec, ospec, gspec, gspec],
        out_shape=[jax.ShapeDtypeStruct((t, heads * dh), BF16)] * 3 + [jax.ShapeDtypeStruct((1, dh), F32)] * 2,
        name=name, compiler_params=_params("arbitrary", "arbitrary"))(proj, proj, gq, gk, dq, dk, dv)


PEERS = N_DEV - 1


def _exchange_copies(ins, outs, send_sems, recv_sems, local_sems, scatter):
    x, y, c = lax.axis_index("x"), lax.axis_index("y"), lax.axis_index("c")
    me = 4 * x + 2 * y + c
    copies = []
    for a in range(len(ins)):
        own = ins[a].at[me] if scatter[a] else ins[a]
        copies.append(pltpu.make_async_copy(own, outs[a].at[me], local_sems.at[a]))
        for k in range(1, N_DEV):
            px = (x + (k >> 2 & 1)) % 2
            py = (y + (k >> 1 & 1)) % 2
            pc = (c + (k & 1)) % 2
            src = ins[a].at[4 * px + 2 * py + pc] if scatter[a] else ins[a]
            copies.append(pltpu.make_async_remote_copy(
                src_ref=src, dst_ref=outs[a].at[me], send_sem=send_sems.at[a * PEERS + k - 1],
                recv_sem=recv_sems.at[a * PEERS + k - 1], device_id=(px, py, pc), device_id_type=MESH))
    return copies


def _exchange_shapes(srcs, scatter):
    return [jax.ShapeDtypeStruct(s.shape if sc else (N_DEV,) + s.shape, s.dtype) for s, sc in zip(srcs, scatter)]


def _exchange_sems(n):
    return [pltpu.SemaphoreType.DMA((n * PEERS,)), pltpu.SemaphoreType.DMA((n * PEERS,)), pltpu.SemaphoreType.DMA((n,))]


def _exchange(srcs, *, scatter, name):
    n = len(srcs)

    def body(*refs):
        copies = _exchange_copies(refs[:n], refs[n:2 * n], *refs[2 * n:], scatter)
        for cp in copies:
            cp.start()
        for cp in copies:
            cp.wait()

    any_spec = pl.BlockSpec(memory_space=pl.ANY)
    return pl.pallas_call(
        body, in_specs=[any_spec] * n, out_specs=[any_spec] * n, out_shape=_exchange_shapes(srcs, scatter),
        scratch_shapes=_exchange_sems(n), name=name,
        compiler_params=pltpu.CompilerParams(has_side_effects=True))(*srcs)


def _gather_two_level(srcs, *, name):
    n = len(srcs)

    def body(*refs):
        ins, outs = refs[:n], refs[n:2 * n]
        send_sems, recv_sems, local_sems = refs[2 * n:]
        x, y, c = lax.axis_index("x"), lax.axis_index("y"), lax.axis_index("c")
        chips = [(1 - x, y), (x, 1 - y), (1 - x, 1 - y)]

        def slab(a, px, py, pc):
            return outs[a].at[4 * px + 2 * py + pc]

        def copy(a, k, block, to, src=None):
            return pltpu.make_async_remote_copy(
                src_ref=slab(a, *block) if src is None else src, dst_ref=slab(a, *block),
                send_sem=send_sems.at[a * PEERS + k], recv_sem=recv_sems.at[a * PEERS + k],
                device_id=to, device_id_type=MESH)

        mine = [pltpu.make_async_copy(ins[a], slab(a, x, y, c), local_sems.at[a]) for a in range(n)]
        first = [copy(a, 0, (x, y, c), (x, y, 1 - c), src=ins[a]) for a in range(n)]
        first += [copy(a, 1 + j, (x, y, c), (*chip, c), src=ins[a]) for j, chip in enumerate(chips) for a in range(n)]
        for cp in mine + first:
            cp.start()
        passed = []
        for j, chip in enumerate(chips):
            for a in range(n):
                copy(a, 1 + j, (*chip, c), (x, y, c)).wait_recv()
                passed.append(copy(a, 4 + j, (*chip, c), (x, y, 1 - c)))
                passed[-1].start()
        for a in range(n):
            copy(a, 0, (x, y, 1 - c), (x, y, c)).wait_recv()
            for j, chip in enumerate(chips):
                copy(a, 4 + j, (*chip, 1 - c), (x, y, c)).wait_recv()
        for cp in first + passed:
            cp.wait_send()
        for cp in mine:
            cp.wait()

    any_spec = pl.BlockSpec(memory_space=pl.ANY)
    return pl.pallas_call(
        body, in_specs=[any_spec] * n, out_specs=[any_spec] * n, out_shape=_exchange_shapes(srcs, [False] * n),
        scratch_shapes=_exchange_sems(n), name=name,
        compiler_params=pltpu.CompilerParams(has_side_effects=True))(*srcs)


def _call_with_exchange(body, *, grid, in_specs, out_specs, out_shape, scratch_shapes, args, srcs, scatter, name):
    n, n_in, n_out, n_scr = len(srcs), len(args), len(out_shape), len(scratch_shapes)

    def full_body(*refs):
        ins, xin = refs[:n_in], refs[n_in:n_in + n]
        outs, xout = refs[n_in + n:n_in + n + n_out], refs[n_in + n + n_out:n_in + 2 * n + n_out]
        scr = refs[n_in + 2 * n + n_out:]
        ids = [pl.program_id(a) for a in range(len(grid))]
        first = functools.reduce(jnp.logical_and, [i == 0 for i in ids])
        last = functools.reduce(jnp.logical_and, [i == g - 1 for i, g in zip(ids, grid)])
        copies = _exchange_copies(xin, xout, *scr[n_scr:], scatter)

        @pl.when(first)
        def _():
            for cp in copies:
                cp.start()

        body(*ins, *outs, *scr[:n_scr])

        @pl.when(last)
        def _():
            for cp in copies:
                cp.wait()

    any_spec = pl.BlockSpec(memory_space=pl.ANY)
    return pl.pallas_call(
        full_body, grid=grid, in_specs=list(in_specs) + [any_spec] * n, out_specs=list(out_specs) + [any_spec] * n,
        out_shape=list(out_shape) + _exchange_shapes(srcs, scatter),
        scratch_shapes=list(scratch_shapes) + _exchange_sems(n), name=name,
        compiler_params=pltpu.CompilerParams(dimension_semantics=("arbitrary",) * len(grid),
                                             vmem_limit_bytes=V7X_VMEM_LIMIT_BYTES, has_side_effects=True))(*args, *srcs)


def _adam_math(g, w, m, v):
    m2 = ADAM_B1 * m + (1.0 - ADAM_B1) * g
    v2 = ADAM_B2 * v + (1.0 - ADAM_B2) * (g * g)
    m_hat = m2 / (1.0 - ADAM_B1 ** ADAM_STEP)
    v_hat = v2 / (1.0 - ADAM_B2 ** ADAM_STEP)
    return -ADAM_LR * (m_hat / (jnp.sqrt(v_hat) + ADAM_EPS) + ADAM_WD * w), m2, v2


def _adamw_slabs(slabs, w, m, v, *, name):
    r, c = w.shape
    tr = next((t for t in (2256, 752, 512, 240, 128, 64, 32, 16) if r % t == 0), r)
    tc = c if r % 16 == 0 else _pick(c, (2 * LANES, LANES))

    def body(s_ref, w_ref, m_ref, v_ref, g_ref, d_ref, mo_ref, vo_ref):
        g = s_ref[0].astype(F32)
        for p in range(1, N_DEV):
            g = g + s_ref[p].astype(F32)
        g_ref[...] = g
        d_ref[...], mo_ref[...], vo_ref[...] = _adam_math(g, w_ref[...], m_ref[...], v_ref[...])

    spec = pl.BlockSpec((tr, tc), lambda i, j: (i, j))
    return pl.pallas_call(
        body, grid=(r // tr, c // tc), in_specs=[pl.BlockSpec((N_DEV, tr, tc), lambda i, j: (0, i, j)), spec, spec, spec],
        out_specs=[spec] * 4, out_shape=[jax.ShapeDtypeStruct((r, c), F32)] * 4, name=name,
        compiler_params=_params("parallel", "parallel"))(slabs, w, m, v)


def _adamw_small(g, w, m, v, *, name):
    def body(g_ref, w_ref, m_ref, v_ref, d_ref, mo_ref, vo_ref):
        d_ref[...], mo_ref[...], vo_ref[...] = _adam_math(g_ref[...], w_ref[...], m_ref[...], v_ref[...])

    return pl.pallas_call(body, out_shape=[jax.ShapeDtypeStruct(w.shape, F32)] * 3, name=name)(g, w, m, v)


def _sum_slabs(slabs, *, name):
    def body(s_ref, o_ref):
        acc = s_ref[0]
        for p in range(1, N_DEV):
            acc = acc + s_ref[p]
        o_ref[...] = acc

    return pl.pallas_call(body, out_shape=jax.ShapeDtypeStruct(slabs.shape[1:], F32), name=name)(slabs)


def _pad_lanes(a):
    return jnp.pad(a, ((0, 0), (0, LANES - a.shape[1])))


def _w_in_slabs(before, after, ab, n_ab):
    col_ab = before.shape[0]
    pw = col_ab + n_ab + after.shape[0]
    c = pw // N_DEV
    parts = [(0, col_ab, before, 0), (col_ab, col_ab + n_ab, ab, col_ab), (col_ab + n_ab, pw, after, col_ab + n_ab)]
    slabs = []
    for p in range(N_DEV):
        pieces = []
        for lo, hi, src, shift in parts:
            a, b = max(lo, c * p), min(hi, c * (p + 1))
            if a < b:
                pieces.append(src[a - shift:b - shift])
        slabs.append(jnp.concatenate(pieces, axis=0))
    return jnp.stack(slabs)


def _local_step(x, target, meta, g_mix, wt_main, wt_ab, cq, ck, cv, a_log, dt_bias, g_dn, g_sbq, g_sbk, g_ffn, rest,
                shards=False):
    seq, d = x.shape
    n_meta = meta.shape[0]
    heads = a_log.shape[1]
    qk = cq.shape[1]
    dvt = cv.shape[1]
    dk, dv = qk // heads, dvt // heads
    dh = g_sbq.shape[1]
    sbw = rest[2].shape[0] * N_DEV if shards else rest[1].shape[0]
    sb_heads = sbw // dh
    pad_l = (-n_meta) % CHUNK
    row_x = pad_l + n_meta
    rows = row_x + seq
    t = -(-rows // GDN_ROWS) * GDN_ROWS
    col_q, col_k, col_v, col_z = 0, qk, 2 * qk, 2 * qk + dvt
    col_sq = 2 * qk + 2 * dvt
    col_sk, col_sv, col_gd, col_gs = col_sq + sbw, col_sq + 2 * sbw, col_sq + 3 * sbw, col_sq + 3 * sbw + d

    def rows_pad(a):
        return jnp.concatenate([jnp.zeros((row_x, d), F32), a, jnp.zeros((t - rows, d), F32)], axis=0)

    h0 = jnp.concatenate([jnp.zeros((pad_l, d), F32), meta, x, jnp.zeros((t - rows, d), F32)], axis=0)
    tgt = rows_pad(target)
    a_log_p, dt_p = _pad_lanes(a_log), _pad_lanes(dt_bias)

    proj, n1 = _mm_norm(h0, g_mix, wt_main, name="proj")
    pab = _mm_nt(n1, wt_ab, out_dtype=F32, name="proj_ab")
    gk = dict(heads=heads, dk=dk, dv=dv, col_q=col_q, col_k=col_k, col_v=col_v, row_lo=pad_l, row_hi=rows)
    qn, kn, vv, g, beta = _gdn_pre(proj, pab, cq, ck, cv, a_log_p, dt_p, name="gdn_pre", **gk)
    u, w, pm, qd, kd, egl, tinv = _gdn_prep(qn, kn, vv, g, beta, name="gdn_prep")
    o_raw, o_dn, states = _gdn_scan(u, w, pm, qd, kd, egl, proj, g_dn, col_z=col_z, name="gdn_scan")
    qs, ks, vs = _sb_pre(proj, g_sbq, g_sbk, heads=sb_heads, dh=dh, col_q=col_sq, col_k=col_sk, col_v=col_sv,
                         name="sb_pre")
    if shards:
        o_sb, carry, (g_fi, g_bd, g_bs, g_out, g_fo) = _sb_fwd(qs, ks, vs, heads=sb_heads, dh=dh, key_lo=pad_l,
                                                                name="sb_fwd", gather=list(rest))
        wt_fi = g_fi.reshape(-1, d)
        d_ff = wt_fi.shape[0] // 2
        w_bd, w_bs, w_out, wt_fg, wt_fu, w_fo = (g_bd.reshape(-1, d), g_bs.reshape(-1, d), g_out.reshape(-1, d),
                                                 wt_fi[:d_ff], wt_fi[d_ff:], g_fo.reshape(-1, d))
    else:
        o_sb, carry = _sb_fwd(qs, ks, vs, heads=sb_heads, dh=dh, key_lo=pad_l, name="sb_fwd")
        w_bd, w_bs, w_out, wt_fg, wt_fu, w_fo = rest
    merged, br_dn, br_sb = _merge_fwd(o_dn, o_sb, w_bd, w_bs, proj, col_gd=col_gd, col_gs=col_gs, name="merge")
    h1 = _mm_res(h0, merged, w_out, name="mix_out")
    gate, up, act, n2 = _mm_norm_swiglu(h1, g_ffn, wt_fg, wt_fu, name="ffn_in")
    dy, dyb, lsum = _mm_res_loss(h1, act, w_fo, tgt, row0=row_x, nrows=seq, name="ffn_out_loss")

    dgate, dup = _swiglu_bwd(dyb, w_fo, gate, up, name="ffn_out_bwd")
    d_w_fo = _mm_tn(act, dyb, name="dw_ffn_out")
    d_wt_fg = _mm_tn(dgate, n2, name="dw_ffn_gate")
    d_wt_fu = _mm_tn(dup, n2, name="dw_ffn_up")
    dh1, dh1b, d_g_ffn = _mm_rmsbwd([(dgate, wt_fg), (dup, wt_fu)], None, h1, g_ffn, dy, name="ffn_in_bwd")

    dbd, dbs, dgd, dgs = _merge_bwd(dh1b, w_out, proj, br_dn, br_sb, col_gd=col_gd, col_gs=col_gs, name="mix_out_bwd")
    d_w_out = _mm_tn(merged, dh1b, name="dw_out")
    d_w_bd = _mm_tn(o_dn, dbd, name="dw_branch_dn")
    d_w_bs = _mm_tn(o_sb, dbs, name="dw_branch_sb")
    do_dn = _mm_nt(dbd, w_bd, out_dtype=F32, name="branch_dn_bwd")
    do_sb = _mm_nt(dbs, w_bs, out_dtype=BF16, name="branch_sb_bwd")

    do_raw, dz, d_g_dn = _gdn_post_bwd(o_raw, proj, g_dn, do_dn, col_z=col_z, name="gdn_post_bwd")
    du, dw, dp, dqd, dkd, dgl = _gdn_bwd_scan(u, w, pm, qd, kd, egl, states, do_raw, name="gdn_bwd_scan")
    dqn, dkn, dvv, dg, dbeta = _gdn_bwd_prep(qn, kn, vv, g, beta, tinv, u, w, du, dw, dp, dqd, dkd, dgl,
                                            name="gdn_bwd_prep")
    dcq, dck, dcv, dpab, d_a_log, d_dt = _gdn_pre_bwd_a(proj, pab, cq, ck, cv, a_log_p, dt_p, dqn, dkn, dvv, dg, dbeta,
                                                        name="gdn_pre_bwd", **gk)
    dpq, d_cq = _conv_bwd(proj, dcq, cq, heads=heads, width=dk, col=col_q, name="conv_q_bwd")
    dpk, d_ck = _conv_bwd(proj, dck, ck, heads=heads, width=dk, col=col_k, name="conv_k_bwd")
    dpv, d_cv = _conv_bwd(proj, dcv, cv, heads=heads, width=dv, col=col_v, name="conv_v_bwd")

    early = None
    if shards:
        slabs = [_bf(jnp.concatenate([d_wt_fg, d_wt_fu], axis=0)).reshape(N_DEV, -1, LANES)]
        slabs += [_bf(a).reshape(N_DEV, -1, d) for a in (d_w_bd, d_w_bs, d_w_out, d_w_fo)]
        dqs, dks, dvs, early = _sb_bwd(qs, ks, vs, do_sb, carry, heads=sb_heads, dh=dh, key_lo=pad_l, name="sb_bwd",
                                       scatter=slabs)
    else:
        dqs, dks, dvs = _sb_bwd(qs, ks, vs, do_sb, carry, heads=sb_heads, dh=dh, key_lo=pad_l, name="sb_bwd")
    dsq, dsk, dsv, d_g_sbq, d_g_sbk = _sb_pre_bwd(proj, g_sbq, g_sbk, dqs, dks, dvs, heads=sb_heads, dh=dh,
                                                   col_q=col_sq, col_k=col_sk, name="sb_pre_bwd")

    dproj = [dpq, dpk, dpv, dz, dsq, dsk, dsv, dgd, dgs]
    dpab_b = _bf(dpab)
    n_dn = 4
    d_wt_dn = _mm_tn(dproj[:n_dn], n1, name="dw_in_dn")
    d_wt_sb = _mm_tn(dproj[n_dn:], n1, name="dw_in_sb")
    assert d_wt_dn.shape[0] == col_sq
    d_wt_main = jnp.concatenate([d_wt_dn, d_wt_sb], axis=0)
    d_wt_ab = _mm_tn(dpab_b, n1, name="dw_in_ab")
    s_in = None
    if shards:
        slabs = _w_in_slabs(d_wt_dn, d_wt_sb, d_wt_ab, 2 * heads)
        dh0, _, d_g_mix, (s_in,) = _mm_rmsbwd([(dproj, wt_main)], (dpab_b, wt_ab), h0, g_mix, dh1, name="proj_bwd",
                                              scatter=[slabs])
    else:
        dh0, _, d_g_mix = _mm_rmsbwd([(dproj, wt_main)], (dpab_b, wt_ab), h0, g_mix, dh1, name="proj_bwd")

    return dict(s_in=s_in, lsum=lsum, grad_x=dh0[row_x:rows], d_meta=dh0[pad_l:row_x], d_g_mix=d_g_mix,
                d_wt_main=d_wt_main, d_wt_ab=d_wt_ab, d_cq=d_cq, d_ck=d_ck, d_cv=d_cv, d_a_log=d_a_log[:, :heads],
                d_dt=d_dt[:, :heads], d_g_dn=d_g_dn, d_g_sbq=d_g_sbq, d_g_sbk=d_g_sbk, d_w_bd=d_w_bd, d_w_bs=d_w_bs,
                d_w_out=d_w_out, d_g_ffn=d_g_ffn, d_wt_fg=d_wt_fg, d_wt_fu=d_wt_fu, d_w_fo=d_w_fo, early=early)


def _pack(parts):
    flat = []
    for a in parts:
        a = a.reshape(-1)
        flat.append(jnp.pad(a, (0, (-a.shape[0]) % LANES)))
    v = jnp.concatenate(flat)
    v = jnp.pad(v, (0, (-v.shape[0]) % (8 * LANES)))
    return v.reshape(-1, LANES)


def _unpack(packed, shapes):
    flat = packed.reshape(-1)
    out, pos = [], 0
    for s in shapes:
        n = math.prod(s)
        out.append(flat[pos:pos + n].reshape(s))
        pos += n + (-n) % LANES
    return out


def kernel(x, meta_tokens, norm_mix_gain, w_in, conv_q, conv_k, conv_v, dn_a_log, dn_dt_bias, dn_out_norm_gain, sb_q_norm_gain, sb_k_norm_gain, w_branch_dn, w_branch_sb, w_out, norm_ffn_gain, w_ffn_in, w_ffn_out, loss_target, m_meta_tokens, m_norm_mix_gain, m_w_in, m_conv_q, m_conv_k, m_conv_v, m_dn_a_log, m_dn_dt_bias, m_dn_out_norm_gain, m_sb_q_norm_gain, m_sb_k_norm_gain, m_w_branch_dn, m_w_branch_sb, m_w_out, m_norm_ffn_gain, m_w_ffn_in, m_w_ffn_out, v_meta_tokens, v_norm_mix_gain, v_w_in, v_conv_q, v_conv_k, v_conv_v, v_dn_a_log, v_dn_dt_bias, v_dn_out_norm_gain, v_sb_q_norm_gain, v_sb_k_norm_gain, v_w_branch_dn, v_w_branch_sb, v_w_out, v_norm_ffn_gain, v_w_ffn_in, v_w_ffn_out):
    me = 4 * lax.axis_index("x") + 2 * lax.axis_index("y") + lax.axis_index("c")
    heads = dn_a_log.shape[1]
    d = x.shape[2]
    qk = conv_q.shape[2] * N_DEV
    dvt = conv_v.shape[2] * N_DEV
    col_ab = 2 * qk + 2 * dvt

    small_shapes = [meta_tokens.shape, conv_q.shape[1:], conv_k.shape[1:], conv_v.shape[1:]]
    small = _pack([meta_tokens, conv_q[0], conv_k[0], conv_v[0]])
    def features_major(a):
        return jnp.transpose(a, (2, 0, 1)).reshape(a.shape[2], a.shape[1])

    g_in, g_small = _gather_two_level([_bf(features_major(w_in)), small], name="gather_w_in")
    wt_full = g_in.reshape(-1, d)
    wt_main = jnp.concatenate([wt_full[:col_ab], wt_full[col_ab + 2 * heads:]], axis=0)
    wt_ab = jnp.pad(wt_full[col_ab:col_ab + 2 * heads], ((0, LANES - 2 * heads), (0, 0)))
    parts = [_unpack(g_small[p], small_shapes) for p in range(N_DEV)]
    meta_f, cq_f, ck_f, cv_f = (jnp.concatenate([parts[p][a] for p in range(N_DEV)], axis=1) for a in range(4))

    r = _local_step(x[0], loss_target[0], meta_f, norm_mix_gain, wt_main, wt_ab, cq_f, ck_f, cv_f, dn_a_log, dn_dt_bias,
                    dn_out_norm_gain, sb_q_norm_gain, sb_k_norm_gain, norm_ffn_gain,
                    (_bf(features_major(w_ffn_in)), _bf(w_branch_dn[0]), _bf(w_branch_sb[0]), _bf(w_out[0]),
                     _bf(w_ffn_out[0])), shards=True)
    s_fi, s_bd, s_bs, s_out, s_fo = r["early"]
    s_in = r["s_in"]

    loss_part = (0.5 / d) * jnp.sum(r["lsum"], axis=1, keepdims=True)
    small_g = [r["d_meta"], r["d_g_mix"], r["d_cq"], r["d_ck"], r["d_cv"], r["d_a_log"], r["d_dt"], r["d_g_dn"],
               r["d_g_sbq"], r["d_g_sbk"], r["d_g_ffn"], loss_part]
    (g_packs,) = _exchange([_pack(small_g)], scatter=[False], name="gather_small_grads")
    (g_meta, g_mix, g_cq, g_ck, g_cv, g_al, g_dt, g_gdn, g_sbq, g_sbk, g_ffn, loss) = _unpack(
        _sum_slabs(g_packs, name="sum_small_grads"), [a.shape for a in small_g])

    def mine(a, width):
        return lax.dynamic_slice_in_dim(a, me * width, width, axis=1)

    big = dict(w_in=(s_in, w_in, m_w_in, v_w_in), w_branch_dn=(s_bd, w_branch_dn, m_w_branch_dn, v_w_branch_dn),
               w_branch_sb=(s_bs, w_branch_sb, m_w_branch_sb, v_w_branch_sb), w_out=(s_out, w_out, m_w_out, v_w_out),
               w_ffn_in=(s_fi, w_ffn_in, m_w_ffn_in, v_w_ffn_in), w_ffn_out=(s_fo, w_ffn_out, m_w_ffn_out, v_w_ffn_out))
    tiny = dict(meta_tokens=(mine(g_meta, d // N_DEV), meta_tokens, m_meta_tokens, v_meta_tokens),
                norm_mix_gain=(g_mix, norm_mix_gain, m_norm_mix_gain, v_norm_mix_gain),
                conv_q=(mine(g_cq, qk // N_DEV), conv_q[0], m_conv_q[0], v_conv_q[0]),
                conv_k=(mine(g_ck, qk // N_DEV), conv_k[0], m_conv_k[0], v_conv_k[0]),
                conv_v=(mine(g_cv, dvt // N_DEV), conv_v[0], m_conv_v[0], v_conv_v[0]),
                dn_a_log=(g_al, dn_a_log, m_dn_a_log, v_dn_a_log), dn_dt_bias=(g_dt, dn_dt_bias, m_dn_dt_bias, v_dn_dt_bias),
                dn_out_norm_gain=(g_gdn, dn_out_norm_gain, m_dn_out_norm_gain, v_dn_out_norm_gain),
                sb_q_norm_gain=(g_sbq, sb_q_norm_gain, m_sb_q_norm_gain, v_sb_q_norm_gain),
                sb_k_norm_gain=(g_sbk, sb_k_norm_gain, m_sb_k_norm_gain, v_sb_k_norm_gain),
                norm_ffn_gain=(g_ffn, norm_ffn_gain, m_norm_ffn_gain, v_norm_ffn_gain))
    order = ["meta_tokens", "norm_mix_gain", "w_in", "conv_q", "conv_k", "conv_v", "dn_a_log", "dn_dt_bias",
             "dn_out_norm_gain", "sb_q_norm_gain", "sb_k_norm_gain", "w_branch_dn", "w_branch_sb", "w_out",
             "norm_ffn_gain", "w_ffn_in", "w_ffn_out"]
    grads, deltas, new_m, new_v = [], [], [], []
    for name in order:
        if name in ("w_in", "w_ffn_in"):
            slabs, w, m, v = big[name]
            res = _adamw_slabs(slabs, *(features_major(a).reshape(slabs.shape[1:]) for a in (w, m, v)), name="adamw_" + name)
            g, dl, mo, vo = (jnp.transpose(a.reshape(w.shape[2], 1, w.shape[1]), (1, 2, 0)) for a in res)
            like = w.shape
        elif name in big:
            slabs, w, m, v = big[name]
            g, dl, mo, vo = _adamw_slabs(slabs, w[0], m[0], v[0], name="adamw_" + name)
            like = w.shape
        else:
            g, w, m, v = tiny[name]
            like = dict(conv_q=conv_q, conv_k=conv_k, conv_v=conv_v).get(name, w).shape
            dl, mo, vo = _adamw_small(g, w, m, v, name="adamw_" + name)
        for lst, a in ((grads, g), (deltas, dl), (new_m, mo), (new_v, vo)):
            lst.append(a.reshape(like))
    return (loss.reshape(()), r["grad_x"][None], *grads, *deltas, *new_m, *new_v)
```

```python
import functools
import math

import jax
import jax.numpy as jnp
from jax import lax
from jax.experimental import pallas as pl
from jax.experimental.pallas import tpu as pltpu

F32 = jnp.float32
BF16 = jnp.bfloat16

N_DEV = 8
CHUNK = 64
CHUNK_SHIFT = 6
GDN_ROWS = 2 * CHUNK
SB_BLOCK = 128
SB_HEADS_PER_STEP = 2
LANES = 128
RMS_EPS = 1e-6
L2_EPS = 1e-6
ADAM_LR = 0.001
ADAM_B1 = 0.9
ADAM_B2 = 0.999
ADAM_EPS = 1e-08
ADAM_WD = 0.01
ADAM_STEP = 10
V7X_VMEM_LIMIT_BYTES = 56 * 1024 * 1024
MM_TN_OUT_BLOCK_BYTES = 6 * 1024 * 1024
MM_PIECES_VMEM_BYTES = V7X_VMEM_LIMIT_BYTES // 2
ROWS_BIG = (1056, 512, 384, 256, 128)
ROWS_MID = (528, 384, 256, 128)
SCAN_HEADS = 4
PREP_HEADS = 2

MESH = pl.DeviceIdType.MESH


def _params(*sem):
    return pltpu.CompilerParams(dimension_semantics=sem or None, vmem_limit_bytes=V7X_VMEM_LIMIT_BYTES)


def _pick(n, cands):
    for c in cands:
        if n % c == 0:
            return c
    raise ValueError(f"no block size among {cands} divides {n}")


def _bf(x):
    return x.astype(BF16)


def _dot(a, b):
    return jnp.dot(a, b, preferred_element_type=F32)


def _dot_nt(a, b):
    return lax.dot_general(a, b, (((1,), (1,)), ((), ())), preferred_element_type=F32)


def _dot_tn(a, b):
    return lax.dot_general(a, b, (((0,), (0,)), ((), ())), preferred_element_type=F32)


def _split2(x):
    hi = _bf(x)
    return hi, _bf(x - hi.astype(F32))


def _split3(x):
    hi = _bf(x)
    r = x - hi.astype(F32)
    mid = _bf(r)
    return hi, mid, _bf(r - mid.astype(F32))


def _dot_hp(a, b, dot=_dot):
    ah, al = _split2(a)
    bh, bl = _split2(b)
    return dot(ah, bh) + dot(ah, bl) + dot(al, bh)


def _dot_exact_l(m, x, dot=_dot):
    h, mi, lo = _split3(x)
    return dot(m, h) + dot(m, mi) + dot(m, lo)


def _sigmoid(x):
    return 1.0 / (1.0 + jnp.exp(-x))


def _silu(x):
    return x * _sigmoid(x)


def _silu_grad(x):
    s = _sigmoid(x)
    return s * (1.0 + x * (1.0 - s))


def _softplus(x):
    return jnp.maximum(x, 0.0) + jnp.log(1.0 + jnp.exp(-jnp.abs(x)))


def _rms_fwd(h, gain):
    r = lax.rsqrt(jnp.mean(h * h, axis=-1, keepdims=True) + RMS_EPS)
    return h * r * gain


def _rms_bwd(h, gain, dy):
    r = lax.rsqrt(jnp.mean(h * h, axis=-1, keepdims=True) + RMS_EPS)
    dyg = dy * gain
    dh = r * dyg - h * (r * r * r) * jnp.mean(dyg * h, axis=-1, keepdims=True)
    return dh, dy * h * r


def _iota(shape, dim):
    return lax.broadcasted_iota(jnp.int32, shape, dim)


def _lane_pick(x, idx):
    return jnp.sum(jnp.where(_iota(x.shape, 1) == idx, x, 0.0), axis=1, keepdims=True)


def _mm_nt(a, b, *, out_dtype, name):
    m, k = a.shape
    n = b.shape[0]
    tm, tn = _pick(m, ROWS_BIG), _pick(n, (1024, 512, 256, 128))

    def body(a_ref, b_ref, o_ref):
        o_ref[...] = _dot_nt(a_ref[...], b_ref[...]).astype(out_dtype)

    return pl.pallas_call(
        body, grid=(m // tm, n // tn),
        in_specs=[pl.BlockSpec((tm, k), lambda i, j: (i, 0)), pl.BlockSpec((tn, k), lambda i, j: (j, 0))],
        out_specs=pl.BlockSpec((tm, tn), lambda i, j: (i, j)),
        out_shape=jax.ShapeDtypeStruct((m, n), out_dtype), name=name,
        compiler_params=_params("parallel", "parallel"))(a, b)


def _column_pieces(a):
    pieces = list(a) if isinstance(a, (list, tuple)) else [a]
    widths = [p.shape[1] for p in pieces]
    return pieces, widths, functools.reduce(math.gcd, widths)


def _piece_spans(widths, block):
    ends = [sum(widths[:j + 1]) // block for j in range(len(widths))]
    return list(zip([0] + ends[:-1], ends))


def _mm_tn(a, b, *, name):
    pieces, widths, unit = _column_pieces(a)
    t, m = pieces[0].shape[0], sum(widths)
    n = b.shape[1]
    tn = _pick(n, (2816, 2048, 1408, 1024, 512, 256, 128))
    tm = _pick(unit, tuple(c for c in (1408, 1024, 512, 256, 128) if c * tn * 4 <= MM_TN_OUT_BLOCK_BYTES))
    held = 2 * len(pieces) * tm * pieces[0].dtype.itemsize
    tk = _pick(t, tuple(c for c in (1408, 1024, 704, 512, 384, 256, 128) if c * held <= MM_PIECES_VMEM_BYTES))
    nk = t // tk
    spans = _piece_spans(widths, tm)

    def body(*refs):
        a_refs, (b_ref, o_ref, acc_ref) = refs[:len(pieces)], refs[len(pieces):]
        i, k = pl.program_id(0), pl.program_id(2)

        @pl.when(k == 0)
        def _():
            acc_ref[...] = jnp.zeros_like(acc_ref)

        for a_ref, (lo, hi) in zip(a_refs, spans):
            def add(a_ref=a_ref):
                acc_ref[...] += _dot_tn(a_ref[...], b_ref[...])
            if len(pieces) == 1:
                add()
            else:
                pl.when((i >= lo) & (i < hi))(add)

        @pl.when(k == nk - 1)
        def _():
            o_ref[...] = _bf(acc_ref[...])

    def a_spec(lo, hi):
        if len(pieces) == 1:
            return pl.BlockSpec((tk, tm), lambda i, j, k: (k, i))
        return pl.BlockSpec((tk, tm), lambda i, j, k: (jnp.where(i < lo, 0, jnp.where(i >= hi, nk - 1, k)),
                                                       jnp.clip(i - lo, 0, hi - lo - 1)))

    return pl.pallas_call(
        body, grid=(m // tm, n // tn, nk),
        in_specs=[a_spec(lo, hi) for lo, hi in spans] + [pl.BlockSpec((tk, tn), lambda i, j, k: (k, j))],
        out_specs=pl.BlockSpec((tm, tn), lambda i, j, k: (i, j)),
        out_shape=jax.ShapeDtypeStruct((m, n), BF16), scratch_shapes=[pltpu.VMEM((tm, tn), F32)], name=name,
        compiler_params=_params("parallel", "parallel", "arbitrary"))(*pieces, b)


def _mm_norm(h, gain, wt, *, name):
    m, k = h.shape
    n = wt.shape[0]
    tm, tn = _pick(m, ROWS_BIG), _pick(n, (1024, 512, 256, 128))

    def body(h_ref, g_ref, w_ref, o_ref, n_ref):
        @pl.when(pl.program_id(1) == 0)
        def _():
            n_ref[...] = _bf(_rms_fwd(h_ref[...], g_ref[...]))

        o_ref[...] = _dot_nt(n_ref[...], w_ref[...])

    return pl.pallas_call(
        body, grid=(m // tm, n // tn),
        in_specs=[pl.BlockSpec((tm, k), lambda i, j: (i, 0)), pl.BlockSpec((1, k), lambda i, j: (0, 0)),
                  pl.BlockSpec((tn, k), lambda i, j: (j, 0))],
        out_specs=[pl.BlockSpec((tm, tn), lambda i, j: (i, j)), pl.BlockSpec((tm, k), lambda i, j: (i, 0))],
        out_shape=[jax.ShapeDtypeStruct((m, n), F32), jax.ShapeDtypeStruct((m, k), BF16)], name=name,
        compiler_params=_params("parallel", "arbitrary"))(h, gain, wt)


def _mm_norm_swiglu(h, gain, wgt, wut, *, name):
    m, k = h.shape
    n = wgt.shape[0]
    tm, tn = _pick(m, ROWS_MID), _pick(n, (1408, 1024, 512, 256, 128))

    def body(h_ref, g_ref, wg_ref, wu_ref, gate_ref, up_ref, act_ref, n_ref):
        @pl.when(pl.program_id(1) == 0)
        def _():
            n_ref[...] = _bf(_rms_fwd(h_ref[...], g_ref[...]))

        gate = _dot_nt(n_ref[...], wg_ref[...])
        up = _dot_nt(n_ref[...], wu_ref[...])
        gate_ref[...] = gate
        up_ref[...] = up
        act_ref[...] = _bf(_silu(gate) * up)

    wspec = pl.BlockSpec((tn, k), lambda i, j: (j, 0))
    ospec = pl.BlockSpec((tm, tn), lambda i, j: (i, j))
    return pl.pallas_call(
        body, grid=(m // tm, n // tn),
        in_specs=[pl.BlockSpec((tm, k), lambda i, j: (i, 0)), pl.BlockSpec((1, k), lambda i, j: (0, 0)), wspec, wspec],
        out_specs=[ospec, ospec, ospec, pl.BlockSpec((tm, k), lambda i, j: (i, 0))],
        out_shape=[jax.ShapeDtypeStruct((m, n), F32), jax.ShapeDtypeStruct((m, n), F32),
                   jax.ShapeDtypeStruct((m, n), BF16), jax.ShapeDtypeStruct((m, k), BF16)], name=name,
        compiler_params=_params("parallel", "arbitrary"))(h, gain, wgt, wut)


def _mm_res(res, a, b, *, name):
    m, k = a.shape
    n = b.shape[1]
    tm, tn = _pick(m, ROWS_BIG), _pick(n, (1024, 512, 256, 128))

    def body(r_ref, a_ref, b_ref, o_ref):
        o_ref[...] = r_ref[...] + _dot(a_ref[...], b_ref[...])

    return pl.pallas_call(
        body, grid=(m // tm, n // tn),
        in_specs=[pl.BlockSpec((tm, tn), lambda i, j: (i, j)), pl.BlockSpec((tm, k), lambda i, j: (i, 0)),
                  pl.BlockSpec((k, tn), lambda i, j: (0, j))],
        out_specs=pl.BlockSpec((tm, tn), lambda i, j: (i, j)),
        out_shape=jax.ShapeDtypeStruct((m, n), F32), name=name,
        compiler_params=_params("parallel", "parallel"))(res, a, b)


def _mm_res_loss(res, a, b, target, *, row0, nrows, name):
    m, k = a.shape
    n = b.shape[1]
    tm = _pick(m, ROWS_MID)

    def body(r_ref, a_ref, b_ref, t_ref, dy_ref, dyb_ref, ls_ref):
        i = pl.program_id(0)

        @pl.when(i == 0)
        def _():
            ls_ref[...] = jnp.zeros_like(ls_ref)

        y = r_ref[...] + _dot(a_ref[...], b_ref[...])
        row = i * tm + _iota((tm, n), 0)
        e = jnp.where((row >= row0) & (row < row0 + nrows), y - t_ref[...], 0.0)
        dy = e / n
        dy_ref[...] = dy
        dyb_ref[...] = _bf(dy)
        ls_ref[...] += jnp.sum(e * e, axis=0, keepdims=True)

    rspec = pl.BlockSpec((tm, n), lambda i: (i, 0))
    return pl.pallas_call(
        body, grid=(m // tm,),
        in_specs=[rspec, pl.BlockSpec((tm, k), lambda i: (i, 0)), pl.BlockSpec((k, n), lambda i: (0, 0)), rspec],
        out_specs=[rspec, rspec, pl.BlockSpec((1, n), lambda i: (0, 0))],
        out_shape=[jax.ShapeDtypeStruct((m, n), F32), jax.ShapeDtypeStruct((m, n), BF16),
                   jax.ShapeDtypeStruct((1, n), F32)], name=name,
        compiler_params=_params("arbitrary"))(res, a, b, target)


def _merge_fwd(o_dn, o_sb, wbd, wbs, proj, *, col_gd, col_gs, name):
    m, kd = o_dn.shape
    ks = o_sb.shape[1]
    n = wbd.shape[1]
    tm = _pick(m, ROWS_BIG)
    tn = _pick(math.gcd(n, math.gcd(col_gd, col_gs)), (512, 256, 128))

    def body(od_ref, os_ref, wd_ref, ws_ref, gd_ref, gs_ref, mg_ref, bd_ref, bs_ref):
        bd = _dot(od_ref[...], wd_ref[...])
        bs = _dot(os_ref[...], ws_ref[...])
        bd_ref[...] = bd
        bs_ref[...] = bs
        mg_ref[...] = _bf(_sigmoid(gd_ref[...]) * bd + _sigmoid(gs_ref[...]) * bs)

    ospec = pl.BlockSpec((tm, tn), lambda i, j: (i, j))
    return pl.pallas_call(
        body, grid=(m // tm, n // tn),
        in_specs=[pl.BlockSpec((tm, kd), lambda i, j: (i, 0)), pl.BlockSpec((tm, ks), lambda i, j: (i, 0)),
                  pl.BlockSpec((kd, tn), lambda i, j: (0, j)), pl.BlockSpec((ks, tn), lambda i, j: (0, j)),
                  pl.BlockSpec((tm, tn), lambda i, j: (i, col_gd // tn + j)),
                  pl.BlockSpec((tm, tn), lambda i, j: (i, col_gs // tn + j))],
        out_specs=[ospec, ospec, ospec],
        out_shape=[jax.ShapeDtypeStruct((m, n), BF16), jax.ShapeDtypeStruct((m, n), F32),
                   jax.ShapeDtypeStruct((m, n), F32)], name=name,
        compiler_params=_params("parallel", "parallel"))(o_dn, o_sb, wbd, wbs, proj, proj)


def _merge_bwd(dh, w_out, proj, br_dn, br_sb, *, col_gd, col_gs, name):
    m, k = dh.shape
    n = w_out.shape[0]
    tm = _pick(m, ROWS_BIG)
    tn = _pick(math.gcd(n, math.gcd(col_gd, col_gs)), (512, 256, 128))

    def body(dh_ref, w_ref, gd_ref, gs_ref, bd_ref, bs_ref, dbd_ref, dbs_ref, dgd_ref, dgs_ref):
        dm = _dot_nt(dh_ref[...], w_ref[...])
        sd = _sigmoid(gd_ref[...])
        ss = _sigmoid(gs_ref[...])
        dbd_ref[...] = _bf(dm * sd)
        dbs_ref[...] = _bf(dm * ss)
        dgd_ref[...] = _bf(dm * bd_ref[...] * sd * (1.0 - sd))
        dgs_ref[...] = _bf(dm * bs_ref[...] * ss * (1.0 - ss))

    ospec = pl.BlockSpec((tm, tn), lambda i, j: (i, j))
    return pl.pallas_call(
        body, grid=(m // tm, n // tn),
        in_specs=[pl.BlockSpec((tm, k), lambda i, j: (i, 0)), pl.BlockSpec((tn, k), lambda i, j: (j, 0)),
                  pl.BlockSpec((tm, tn), lambda i, j: (i, col_gd // tn + j)),
                  pl.BlockSpec((tm, tn), lambda i, j: (i, col_gs // tn + j)), ospec, ospec],
        out_specs=[ospec] * 4,
        out_shape=[jax.ShapeDtypeStruct((m, n), BF16)] * 4, name=name,
        compiler_params=_params("parallel", "parallel"))(dh, w_out, proj, proj, br_dn, br_sb)


def _swiglu_bwd(dy, wfo, gate, up, *, name):
    m, k = dy.shape
    n = wfo.shape[0]
    tm, tn = _pick(m, ROWS_MID), _pick(n, (1408, 1024, 512, 256, 128))

    def body(dy_ref, w_ref, g_ref, u_ref, dg_ref, du_ref):
        da = _dot_nt(dy_ref[...], w_ref[...])
        g = g_ref[...]
        dg_ref[...] = _bf(da * u_ref[...] * _silu_grad(g))
        du_ref[...] = _bf(da * _silu(g))

    ospec = pl.BlockSpec((tm, tn), lambda i, j: (i, j))
    return pl.pallas_call(
        body, grid=(m // tm, n // tn),
        in_specs=[pl.BlockSpec((tm, k), lambda i, j: (i, 0)), pl.BlockSpec((tn, k), lambda i, j: (j, 0)), ospec, ospec],
        out_specs=[ospec, ospec], out_shape=[jax.ShapeDtypeStruct((m, n), BF16)] * 2, name=name,
        compiler_params=_params("parallel", "parallel"))(dy, wfo, gate, up)


def _mm_rmsbwd(pairs, extra, h, gain, dres, *, name, scatter=None):
    split = [_column_pieces(a) for a, _ in pairs]
    m, k = h.shape[0], sum(split[0][1])
    n = h.shape[1]
    tm = _pick(m, ROWS_MID)
    tk = _pick(functools.reduce(math.gcd, [unit for _, _, unit in split]), (1408, 1024, 512, 256, 128))
    nk = k // tk
    spans = [_piece_spans(widths, tk) for _, widths, _ in split]
    n_ab = sum(len(s) + 1 for s in spans)

    def body(*refs):
        ex = refs[n_ab:n_ab + 2] if extra is not None else ()
        h_ref, g_ref, r_ref, dh_ref, dhb_ref, dg_ref, acc_ref = refs[n_ab + len(ex):]
        i, kk = pl.program_id(0), pl.program_id(1)

        @pl.when((i == 0) & (kk == 0))
        def _():
            dg_ref[...] = jnp.zeros_like(dg_ref)

        @pl.when(kk == 0)
        def _():
            acc_ref[...] = _dot(ex[0][...], ex[1][...]) if ex else jnp.zeros_like(acc_ref)

        pos, whole = 0, []
        for s in spans:
            a_refs, b_ref = refs[pos:pos + len(s)], refs[pos + len(s)]
            pos += len(s) + 1
            if len(s) == 1:
                whole.append(_dot(a_refs[0][...], b_ref[...]))
                continue
            for a_ref, (lo, hi) in zip(a_refs, s):
                def add(a_ref=a_ref, b_ref=b_ref):
                    acc_ref[...] += _dot(a_ref[...], b_ref[...])
                pl.when((kk >= lo) & (kk < hi))(add)
        if whole:
            acc_ref[...] += functools.reduce(lambda x, y: x + y, whole)

        @pl.when(kk == nk - 1)
        def _():
            dh, dgr = _rms_bwd(h_ref[...], g_ref[...], acc_ref[...])
            dh = dh + r_ref[...]
            dh_ref[...] = dh
            dhb_ref[...] = _bf(dh)
            dg_ref[...] += jnp.sum(dgr, axis=0, keepdims=True)

    in_specs, args = [], []
    def a_spec(lo, hi, alone):
        if alone:
            return pl.BlockSpec((tm, tk), lambda i, kk: (i, kk))
        return pl.BlockSpec((tm, tk), lambda i, kk: (i, jnp.clip(kk - lo, 0, hi - lo - 1)))

    for (pieces, _, _), s, (_, b) in zip(split, spans, pairs):
        in_specs += [a_spec(lo, hi, len(s) == 1) for lo, hi in s] + [pl.BlockSpec((tk, n), lambda i, kk: (kk, 0))]
        args += [*pieces, b]
    if extra is not None:
        k2 = extra[0].shape[1]
        in_specs += [pl.BlockSpec((tm, k2), lambda i, kk: (i, 0)), pl.BlockSpec((k2, n), lambda i, kk: (0, 0))]
        args += list(extra)
    rspec = pl.BlockSpec((tm, n), lambda i, kk: (i, 0))
    in_specs += [rspec, pl.BlockSpec((1, n), lambda i, kk: (0, 0)), rspec]
    call = dict(grid=(m // tm, nk), in_specs=in_specs,
                out_specs=[rspec, rspec, pl.BlockSpec((1, n), lambda i, kk: (0, 0))],
                out_shape=[jax.ShapeDtypeStruct((m, n), F32), jax.ShapeDtypeStruct((m, n), BF16),
                           jax.ShapeDtypeStruct((1, n), F32)],
                scratch_shapes=[pltpu.VMEM((tm, n), F32)], name=name)
    if scatter is None:
        return pl.pallas_call(body, compiler_params=_params("arbitrary", "arbitrary"), **call)(*args, h, gain, dres)
    res = _call_with_exchange(body, args=(*args, h, gain, dres), srcs=scatter, scatter=[True] * len(scatter), **call)
    return [*res[:3], list(res[3:])]


HALO = 8


def _stage(buf_ref, before, cur):
    buf_ref[0:HALO, :] = before
    buf_ref[HALO:HALO + cur.shape[0], :] = cur


def _stage_after(buf_ref, cur, after):
    r = cur.shape[0]
    buf_ref[0:r, :] = cur
    buf_ref[r:r + HALO, :] = after


def _conv_taps(buf_ref, rows, w_ref):
    nk = w_ref.shape[0]
    out = buf_ref[HALO:HALO + rows, :] * w_ref[nk - 1:nk, :]
    for s in range(1, nk):
        out += buf_ref[HALO - s:HALO - s + rows, :] * w_ref[nk - 1 - s:nk - s, :]
    return out


def _gdn_pre(proj, pab, cq, ck, cv, a_log, dt_bias, *, heads, dk, dv, col_q, col_k, col_v, row_lo, row_hi, name):
    t = proj.shape[0]
    tm = _pick(t, (384, 256, 128))
    nb = t // tm

    def body(pq_ref, pqp_ref, pk_ref, pkp_ref, pv_ref, pvp_ref, ab_ref, cq_ref, ck_ref, cv_ref, al_ref, dt_ref,
             qn_ref, kn_ref, v_ref, g_ref, b_ref, bq_ref, bk_ref, bv_ref):
        h, i = pl.program_id(0), pl.program_id(1)
        first = i == 0
        row = i * tm + _iota((tm, 1), 0)
        valid = (row >= row_lo) & (row < row_hi)
        _stage(bq_ref, jnp.where(first, 0.0, pqp_ref[...]), pq_ref[...])
        _stage(bk_ref, jnp.where(first, 0.0, pkp_ref[...]), pk_ref[...])
        _stage(bv_ref, jnp.where(first, 0.0, pvp_ref[...]), pv_ref[...])
        q1 = _silu(_conv_taps(bq_ref, tm, cq_ref))
        k1 = _silu(_conv_taps(bk_ref, tm, ck_ref))
        v1 = _silu(_conv_taps(bv_ref, tm, cv_ref))
        qn_ref[...] = jnp.where(valid, q1 * lax.rsqrt(jnp.sum(q1 * q1, axis=-1, keepdims=True) + L2_EPS), 0.0)
        kn_ref[...] = jnp.where(valid, k1 * lax.rsqrt(jnp.sum(k1 * k1, axis=-1, keepdims=True) + L2_EPS), 0.0)
        v_ref[...] = jnp.where(valid, v1, 0.0)
        ab = ab_ref[...]
        da = _lane_pick(ab, h)
        db = _lane_pick(ab, heads + h)
        a = _lane_pick(al_ref[...], h)
        dtb = _lane_pick(dt_ref[...], h)
        g_ref[...] = jnp.where(valid, -jnp.exp(a) * _softplus(da + dtb), 0.0)
        b_ref[...] = jnp.where(valid, _sigmoid(db), 0.0)

    def cur(width, col):
        return pl.BlockSpec((tm, width), lambda h, i: (i, col // width + h))

    def prev(width, col):
        return pl.BlockSpec((8, width), lambda h, i: (jnp.maximum(i * (tm // 8) - 1, 0), col // width + h))

    def out(width):
        return pl.BlockSpec((None, tm, width), lambda h, i: (h, i, 0))

    small = pl.BlockSpec((1, LANES), lambda h, i: (0, 0))
    return pl.pallas_call(
        body, grid=(heads, nb),
        in_specs=[cur(dk, col_q), prev(dk, col_q), cur(dk, col_k), prev(dk, col_k), cur(dv, col_v), prev(dv, col_v),
                  pl.BlockSpec((tm, LANES), lambda h, i: (i, 0)),
                  pl.BlockSpec((cq.shape[0], dk), lambda h, i: (0, h)), pl.BlockSpec((ck.shape[0], dk), lambda h, i: (0, h)),
                  pl.BlockSpec((cv.shape[0], dv), lambda h, i: (0, h)), small, small],
        out_specs=[out(dk), out(dk), out(dv), out(1), out(1)],
        out_shape=[jax.ShapeDtypeStruct((heads, t, dk), F32), jax.ShapeDtypeStruct((heads, t, dk), F32),
                   jax.ShapeDtypeStruct((heads, t, dv), F32), jax.ShapeDtypeStruct((heads, t, 1), F32),
                   jax.ShapeDtypeStruct((heads, t, 1), F32)],
        scratch_shapes=[pltpu.VMEM((HALO + tm, dk), F32), pltpu.VMEM((HALO + tm, dk), F32),
                        pltpu.VMEM((HALO + tm, dv), F32)], name=name,
        compiler_params=_params("parallel", "parallel"))(proj, proj, proj, proj, proj, proj, pab, cq, ck, cv, a_log, dt_bias)


def _chunk_masks(rows=GDN_ROWS, row0=0):
    ri = row0 + _iota((rows, GDN_ROWS), 0)
    ci = _iota((rows, GDN_ROWS), 1)
    same = jnp.right_shift(ri, CHUNK_SHIFT) == jnp.right_shift(ci, CHUNK_SHIFT)
    return same, same & (ri >= ci), same & (ri > ci), ri == ci


def _col_to_row(col, eye):
    return jnp.sum(jnp.where(eye, col, 0.0), axis=0, keepdims=True)


def _row_to_col(row, eye):
    return jnp.sum(jnp.where(eye, row, 0.0), axis=1, keepdims=True)


def _chunk_common(blocks, dk_scale):
    same, incl, strict, eye = _chunk_masks()
    tri = jnp.where(incl, 1.0, 0.0).astype(BF16)
    tot = jnp.where(same, 1.0, 0.0).astype(BF16)
    gbs = [jnp.broadcast_to(g, (GDN_ROWS, LANES)) for _, _, g, _ in blocks]
    gams = [jnp.max(_dot_exact_l(tri, gb), axis=1, keepdims=True) for gb in gbs]
    lasts = [jnp.max(_dot_exact_l(tot, gb), axis=1, keepdims=True) for gb in gbs]
    kbs = [kn * beta for _, kn, _, beta in blocks]
    qts = [qn * dk_scale for qn, _, _, _ in blocks]
    boths = [_dot_nt(_bf(jnp.concatenate([kb, qt], axis=0)), _bf(blk[1]))
             for kb, qt, blk in zip(kbs, qts, blocks)]
    out = []
    for gam, last, kb, qt, both in zip(gams, lasts, kbs, qts, boths):
        diff = gam - _col_to_row(gam, eye)
        decay = jnp.where(incl, jnp.exp(jnp.where(incl, diff, 0.0)), 0.0)
        out.append(dict(incl=incl, strict=strict, eye=eye, decay=decay, eg=jnp.exp(gam), ek=jnp.exp(last - gam),
                        egl=jnp.exp(last), kb=kb, qt=qt, lmat=jnp.where(strict, both[:GDN_ROWS] * decay, 0.0),
                        pmat=jnp.where(incl, both[GDN_ROWS:] * decay, 0.0)))
    return out


def _gdn_prep(qn, kn, v, g, beta, *, name):
    heads, t, dk = qn.shape
    dv = v.shape[2]
    rows = _pick(t, (3 * GDN_ROWS, 2 * GDN_ROWS, GDN_ROWS))
    dk_scale = dk ** -0.5

    hp = PREP_HEADS

    def body(q_ref, k_ref, v_ref, g_ref, b_ref, u_ref, w_ref, p_ref, qd_ref, kd_ref, egl_ref, t_ref):
        rs = [(hh, pl.ds(b * GDN_ROWS, GDN_ROWS), slice(None)) for hh in range(hp) for b in range(rows // GDN_ROWS)]
        cs = _chunk_common([(q_ref[r], k_ref[r], g_ref[r], b_ref[r]) for r in rs], dk_scale)
        eye_f = jnp.where(cs[0]["eye"], 1.0, 0.0)
        tinvs = [eye_f - c["lmat"] for c in cs]
        ys = [_dot_hp(c["lmat"], c["lmat"]) for c in cs]
        for _ in range(CHUNK_SHIFT - 1):
            boths = [_dot_hp(y, jnp.concatenate([y, tinv], axis=1)) for y, tinv in zip(ys, tinvs)]
            ys = [both[:, :GDN_ROWS] for both in boths]
            tinvs = [tinv + both[:, GDN_ROWS:] for tinv, both in zip(tinvs, boths)]
        uws = [_dot_hp(tinv, jnp.concatenate([v_ref[r] * b_ref[r], c["kb"] * c["eg"]], axis=1))
               for r, c, tinv in zip(rs, cs, tinvs)]
        for r, c, tinv, uw in zip(rs, cs, tinvs, uws):
            u_ref[r] = uw[:, :dv]
            w_ref[r] = _bf(uw[:, dv:])
            p_ref[r] = _bf(c["pmat"])
            qd_ref[r] = _bf(c["qt"] * c["eg"])
            kd_ref[r] = _bf(k_ref[r] * c["ek"])
            egl_ref[r] = c["egl"]
            t_ref[r] = tinv

    def blk(width):
        return pl.BlockSpec((hp, rows, width), lambda h, i: (h, i, 0))

    def shp(width, dtype=F32):
        return jax.ShapeDtypeStruct((heads, t, width), dtype)

    return pl.pallas_call(
        body, grid=(heads // hp, t // rows), in_specs=[blk(dk), blk(dk), blk(dv), blk(1), blk(1)],
        out_specs=[blk(dv), blk(dk), blk(GDN_ROWS), blk(dk), blk(dk), blk(1), blk(GDN_ROWS)],
        out_shape=[shp(dv), shp(dk, BF16), shp(GDN_ROWS, BF16), shp(dk, BF16), shp(dk, BF16), shp(1), shp(GDN_ROWS)],
        name=name,
        compiler_params=_params("parallel", "parallel"))(qn, kn, v, g, beta)


def _gdn_scan(u, w, p, qd, kd, egl, proj, gain, *, col_z, name):
    heads, t, dv = u.shape
    dk = w.shape[2]
    nb = t // GDN_ROWS
    sub = GDN_ROWS // CHUNK
    hp = SCAN_HEADS

    def body(u_ref, w_ref, p_ref, qd_ref, kd_ref, egl_ref, z_ref, gn_ref, o_ref, og_ref, st_ref, s_ref):
        @pl.when(pl.program_id(1) == 0)
        def _():
            s_ref[...] = jnp.zeros_like(s_ref)

        hs = range(hp)
        vn_parts = [[jnp.zeros((CHUNK, dv), F32)] * sub for _ in hs]
        for c in range(sub):
            r = pl.ds(c * CHUNK, CHUNK)
            ss = [s_ref[hh] for hh in hs]
            sbs = [_bf(s) for s in ss]
            wss = [_dot(_bf(jnp.concatenate([w_ref[hh, r, :], qd_ref[hh, r, :]], axis=0)), sbs[hh])
                   for hh in hs]
            vns = [u_ref[hh, r, :] - wss[hh][:CHUNK] for hh in hs]
            for hh in hs:
                vn_parts[hh][c] = vns[hh]
            os_ = [wss[hh][CHUNK:] + _dot(_bf(p_ref[hh, r, :]), _bf(jnp.concatenate(vn_parts[hh], axis=0))) for hh in hs]
            new = [ss[hh] * egl_ref[hh, pl.ds(c * CHUNK, 1), :] + _dot_tn(_bf(kd_ref[hh, r, :]), _bf(vns[hh])) for hh in hs]
            for hh in hs:
                cols = pl.ds(hh * dv, dv)
                st_ref[hh, c] = sbs[hh]
                s_ref[hh] = new[hh]
                o_ref[hh, r, :] = os_[hh]
                og_ref[r, cols] = _bf(_rms_fwd(os_[hh], gn_ref[...]) * _silu(z_ref[r, cols]))

    def blk(width):
        return pl.BlockSpec((hp, GDN_ROWS, width), lambda h, i: (h, i, 0))

    return pl.pallas_call(
        body, grid=(heads // hp, nb),
        in_specs=[blk(dv), blk(dk), blk(GDN_ROWS), blk(dk), blk(dk), blk(1),
                  pl.BlockSpec((GDN_ROWS, hp * dv), lambda h, i: (i, col_z // (hp * dv) + h)),
                  pl.BlockSpec((1, dv), lambda h, i: (0, 0))],
        out_specs=[blk(dv), pl.BlockSpec((GDN_ROWS, hp * dv), lambda h, i: (i, h)),
                   pl.BlockSpec((hp, sub, dk, dv), lambda h, i: (h, i, 0, 0))],
        out_shape=[jax.ShapeDtypeStruct((heads, t, dv), F32), jax.ShapeDtypeStruct((t, heads * dv), BF16),
                   jax.ShapeDtypeStruct((heads, t // CHUNK, dk, dv), BF16)],
        scratch_shapes=[pltpu.VMEM((hp, dk, dv), F32)], name=name,
        compiler_params=_params("parallel", "arbitrary"))(u, w, p, qd, kd, egl, proj, gain)


def _gdn_post_bwd(o, proj, gain, dout, *, col_z, name):
    heads, t, dv = o.shape
    tm = _pick(t, (384, 256, 128))

    def body(o_ref, z_ref, gn_ref, d_ref, do_ref, dz_ref, dg_ref):
        @pl.when((pl.program_id(0) == 0) & (pl.program_id(1) == 0))
        def _():
            dg_ref[...] = jnp.zeros_like(dg_ref)

        o_, z, d = o_ref[...], z_ref[...], d_ref[...]
        y = _rms_fwd(o_, gn_ref[...])
        dz_ref[...] = _bf(d * y * _silu_grad(z))
        do, dgr = _rms_bwd(o_, gn_ref[...], d * _silu(z))
        do_ref[...] = do
        dg_ref[...] += jnp.sum(dgr, axis=0, keepdims=True)

    return pl.pallas_call(
        body, grid=(t // tm, heads),
        in_specs=[pl.BlockSpec((None, tm, dv), lambda i, h: (h, i, 0)),
                  pl.BlockSpec((tm, dv), lambda i, h: (i, col_z // dv + h)),
                  pl.BlockSpec((1, dv), lambda i, h: (0, 0)), pl.BlockSpec((tm, dv), lambda i, h: (i, h))],
        out_specs=[pl.BlockSpec((None, tm, dv), lambda i, h: (h, i, 0)), pl.BlockSpec((tm, dv), lambda i, h: (i, h)),
                   pl.BlockSpec((1, dv), lambda i, h: (0, 0))],
        out_shape=[jax.ShapeDtypeStruct((heads, t, dv), F32), jax.ShapeDtypeStruct((t, heads * dv), BF16),
                   jax.ShapeDtypeStruct((1, dv), F32)], name=name,
        compiler_params=_params("arbitrary", "arbitrary"))(o, proj, gain, dout)


def _gdn_bwd_scan(u, w, p, qd, kd, egl, st, do, *, name):
    heads, t, dv = u.shape
    dk = w.shape[2]
    nb = t // GDN_ROWS
    sub = GDN_ROWS // CHUNK
    hp = SCAN_HEADS

    def body(u_ref, w_ref, p_ref, qd_ref, kd_ref, egl_ref, st_ref, do_ref,
             du_ref, dw_ref, dp_ref, dqd_ref, dkd_ref, dgl_ref, ds_ref):
        @pl.when(pl.program_id(1) == 0)
        def _():
            ds_ref[...] = jnp.zeros_like(ds_ref)

        hs = range(hp)
        zeros = jnp.zeros((CHUNK, dv), BF16)
        for c in reversed(range(sub)):
            r = pl.ds(c * CHUNK, CHUNK)
            sbs = [st_ref[hh, c] for hh in hs]
            dss = [ds_ref[hh] for hh in hs]
            dsbs = [_bf(ds) for ds in dss]
            dobs = [_bf(do_ref[hh, r, :]) for hh in hs]
            wbs = [_bf(w_ref[hh, r, :]) for hh in hs]
            vns = [u_ref[hh, r, :] - _dot(wbs[hh], sbs[hh]) for hh in hs]
            dvns = [_dot_tn(_bf(p_ref[hh, r, :]), dobs[hh])[c * CHUNK:(c + 1) * CHUNK, :]
                    + _dot(_bf(kd_ref[hh, r, :]), dsbs[hh]) for hh in hs]
            dods = [jnp.concatenate([dobs[hh], _bf(dvns[hh])], axis=0) for hh in hs]
            boths = [_dot_nt(dods[hh], sbs[hh]) for hh in hs]
            dps = [_dot_nt(dobs[hh], jnp.concatenate([_bf(vns[hh]) if cc == c else zeros for cc in range(sub)], axis=0))
                   for hh in hs]
            dkds = [_dot_nt(_bf(vns[hh]), dsbs[hh]) for hh in hs]
            new = [dss[hh] * egl_ref[hh, pl.ds(c * CHUNK, 1), :]
                   + _dot_tn(jnp.concatenate([_bf(qd_ref[hh, r, :]), -wbs[hh]], axis=0), dods[hh])
                   for hh in hs]
            for hh in hs:
                du_ref[hh, r, :] = dvns[hh]
                dw_ref[hh, r, :] = -boths[hh][CHUNK:]
                dp_ref[hh, r, :] = jnp.where(_chunk_masks(CHUNK, c * CHUNK)[1], dps[hh], 0.0)
                dqd_ref[hh, r, :] = boths[hh][:CHUNK]
                dkd_ref[hh, r, :] = dkds[hh]
                dgl = jnp.sum(jnp.sum(dss[hh] * sbs[hh].astype(F32), axis=1, keepdims=True), axis=0, keepdims=True)
                dgl_ref[hh, r, :] = jnp.where(_iota((CHUNK, 1), 0) == CHUNK - 1, dgl, 0.0)
                ds_ref[hh] = new[hh]

    def blk(width):
        return pl.BlockSpec((hp, GDN_ROWS, width), lambda h, i: (h, nb - 1 - i, 0))

    def shp(width):
        return jax.ShapeDtypeStruct((heads, t, width), F32)

    return pl.pallas_call(
        body, grid=(heads // hp, nb),
        in_specs=[blk(dv), blk(dk), blk(GDN_ROWS), blk(dk), blk(dk), blk(1),
                  pl.BlockSpec((hp, sub, dk, dv), lambda h, i: (h, nb - 1 - i, 0, 0)), blk(dv)],
        out_specs=[blk(dv), blk(dk), blk(GDN_ROWS), blk(dk), blk(dk), blk(1)],
        out_shape=[shp(dv), shp(dk), shp(GDN_ROWS), shp(dk), shp(dk), shp(1)],
        scratch_shapes=[pltpu.VMEM((hp, dk, dv), F32)], name=name,
        compiler_params=_params("parallel", "arbitrary"))(u, w, p, qd, kd, egl, st, do)


def _gdn_bwd_prep(qn, kn, v, g, beta, tinv, u, w, du, dw, dp, dqd, dkd, dgl, *, name):
    heads, t, dk = qn.shape
    dv = v.shape[2]
    rows = _pick(t, (3 * GDN_ROWS, 2 * GDN_ROWS, GDN_ROWS))
    dk_scale = dk ** -0.5
    hp = PREP_HEADS

    def rowsum(x):
        return jnp.sum(x, axis=1, keepdims=True)

    def body(q_ref, k_ref, v_ref, g_ref, b_ref, t_ref, u_ref, w_ref, du_ref, dw_ref, dp_ref, dqd_ref, dkd_ref, dgl_ref,
             dq_ref, dkk_ref, dvv_ref, dg_ref, db_ref):
        rs = [(hh, pl.ds(b * GDN_ROWS, GDN_ROWS), slice(None)) for hh in range(hp) for b in range(rows // GDN_ROWS)]
        cs = _chunk_common([(q_ref[r], k_ref[r], g_ref[r], b_ref[r]) for r in rs], dk_scale)
        dbvws = [_dot_hp(t_ref[r], jnp.concatenate([du_ref[r], dw_ref[r]], axis=1), _dot_tn)
                 for r in rs]
        das = [-_dot_nt(_bf(dbvw), jnp.concatenate([_bf(u_ref[r]), w_ref[r]], axis=1))
               for r, dbvw in zip(rs, dbvws)]
        dls = [jnp.where(c["strict"], da, 0.0) for c, da in zip(cs, das)]
        dmns = [_bf(jnp.concatenate([dl * c["decay"], dp_ref[r] * c["decay"]], axis=0))
                for r, c, dl in zip(rs, cs, dls)]
        boths = [_dot(dmn, _bf(k_ref[r])) for r, dmn in zip(rs, dmns)]
        dkns = [_dot_tn(dmn, _bf(jnp.concatenate([c["kb"], c["qt"]], axis=0)))
                for c, dmn in zip(cs, dmns)]
        for r, c, dbvw, dl, both, dkn in zip(rs, cs, dbvws, dls, boths, dkns):
            kn_, beta_, v_ = k_ref[r], b_ref[r], v_ref[r]
            eye = c["eye"]
            kb, qt, eg, ek = c["kb"], c["qt"], c["eg"], c["ek"]
            dbv, dbw = dbvw[:, :dv], dbvw[:, dv:]
            dp_ = dp_ref[r]
            dkb = both[:GDN_ROWS] + dbw * eg
            dqt = both[GDN_ROWS:]
            gmat = dl * c["lmat"] + dp_ * c["pmat"]
            dqd_, dkd_ = dqd_ref[r], dkd_ref[r]
            qd = qt * eg
            kd = kn_ * ek
            bw = kb * eg
            kdsum = rowsum(dkd_ * kd)
            dgam = rowsum(gmat) - _row_to_col(jnp.sum(gmat, axis=0, keepdims=True), eye)
            dgam += rowsum(dbw * bw) + rowsum(dqd_ * qd) - kdsum
            last = (_iota((GDN_ROWS, 1), 0) & (CHUNK - 1)) == CHUNK - 1
            same = _chunk_masks()[0]
            same_f = jnp.where(same, 1.0, 0.0).astype(BF16)
            chunk_tot = jnp.max(_dot_exact_l(same_f, jnp.broadcast_to(kdsum, (GDN_ROWS, LANES))), axis=1, keepdims=True)
            dgam += jnp.where(last, chunk_tot, 0.0) + dgl_ref[r] * c["egl"]
            dq_ref[r] = (dqt + dqd_ * eg) * dk_scale
            dkk_ref[r] = dkn + dkd_ * ek + dkb * beta_
            dvv_ref[r] = dbv * beta_
            db_ref[r] = rowsum(dbv * v_) + rowsum(dkb * kn_)
            upper = jnp.where(same & (_iota((GDN_ROWS, GDN_ROWS), 0) <= _iota((GDN_ROWS, GDN_ROWS), 1)), 1.0, 0.0)
            dgb = _dot_exact_l(upper.astype(BF16), jnp.broadcast_to(dgam, (GDN_ROWS, LANES)))
            dg_ref[r] = _lane_pick(dgb, 0)

    def blk(width):
        return pl.BlockSpec((hp, rows, width), lambda h, i: (h, i, 0))

    def shp(width):
        return jax.ShapeDtypeStruct((heads, t, width), F32)

    return pl.pallas_call(
        body, grid=(heads // hp, t // rows),
        in_specs=[blk(dk), blk(dk), blk(dv), blk(1), blk(1), blk(GDN_ROWS), blk(dv), blk(dk),
                  blk(dv), blk(dk), blk(GDN_ROWS), blk(dk), blk(dk), blk(1)],
        out_specs=[blk(dk), blk(dk), blk(dv), blk(1), blk(1)],
        out_shape=[shp(dk), shp(dk), shp(dv), shp(1), shp(1)], name=name,
        compiler_params=_params("parallel", "parallel"))(qn, kn, v, g, beta, tinv, u, w, du, dw, dp, dqd, dkd, dgl)


def _gdn_pre_bwd_a(proj, pab, cq, ck, cv, a_log, dt_bias, dqn, dkn, dvv, dg, dbeta, *,
                   heads, dk, dv, col_q, col_k, col_v, row_lo, row_hi, name):
    t = proj.shape[0]
    tm = _pick(t, (384, 256, 128))
    nb = t // tm

    def body(pq_ref, pqp_ref, pk_ref, pkp_ref, pv_ref, pvp_ref, ab_ref, cq_ref, ck_ref, cv_ref, al_ref, dt_ref,
             dqn_ref, dkn_ref, dvv_ref, dg_ref, db_ref, dcq_ref, dck_ref, dcv_ref, dab_ref, dal_ref, ddt_ref,
             bq_ref, bk_ref, bv_ref):
        i, h = pl.program_id(0), pl.program_id(1)
        first = i == 0

        @pl.when((i == 0) & (h == 0))
        def _():
            dal_ref[...] = jnp.zeros_like(dal_ref)
            ddt_ref[...] = jnp.zeros_like(ddt_ref)

        @pl.when(h == 0)
        def _():
            dab_ref[...] = jnp.zeros_like(dab_ref)

        row = i * tm + _iota((tm, 1), 0)
        valid = (row >= row_lo) & (row < row_hi)

        def l2_bwd(c1, dn):
            x1 = _silu(c1)
            r = lax.rsqrt(jnp.sum(x1 * x1, axis=-1, keepdims=True) + L2_EPS)
            dn = jnp.where(valid, dn, 0.0)
            d1 = r * dn - x1 * (r * r * r) * jnp.sum(dn * x1, axis=-1, keepdims=True)
            return d1 * _silu_grad(c1)

        _stage(bq_ref, jnp.where(first, 0.0, pqp_ref[...]), pq_ref[...])
        _stage(bk_ref, jnp.where(first, 0.0, pkp_ref[...]), pk_ref[...])
        _stage(bv_ref, jnp.where(first, 0.0, pvp_ref[...]), pv_ref[...])
        dcq_ref[...] = l2_bwd(_conv_taps(bq_ref, tm, cq_ref), dqn_ref[...])
        dck_ref[...] = l2_bwd(_conv_taps(bk_ref, tm, ck_ref), dkn_ref[...])
        cv1 = _conv_taps(bv_ref, tm, cv_ref)
        dcv_ref[...] = jnp.where(valid, dvv_ref[...], 0.0) * _silu_grad(cv1)
        ab = ab_ref[...]
        da = _lane_pick(ab, h)
        db = _lane_pick(ab, heads + h)
        a = _lane_pick(al_ref[...], h)
        dtb = _lane_pick(dt_ref[...], h)
        dgv = jnp.where(valid, dg_ref[...], 0.0)
        ea = jnp.exp(a)
        g = -ea * _softplus(da + dtb)
        dda = dgv * (-ea) * _sigmoid(da + dtb)
        beta = _sigmoid(db)
        ddb = jnp.where(valid, db_ref[...], 0.0) * beta * (1.0 - beta)
        lane = _iota((tm, LANES), 1)
        dab_ref[...] += jnp.where(lane == h, dda, 0.0) + jnp.where(lane == heads + h, ddb, 0.0)
        lane1 = _iota((1, LANES), 1)
        dal_ref[...] += jnp.where(lane1 == h, jnp.sum(dgv * g, axis=0, keepdims=True), 0.0)
        ddt_ref[...] += jnp.where(lane1 == h, jnp.sum(dda, axis=0, keepdims=True), 0.0)

    def cur(width, col):
        return pl.BlockSpec((tm, width), lambda i, h: (i, col // width + h))

    def prev(width, col):
        return pl.BlockSpec((8, width), lambda i, h: (jnp.maximum(i * (tm // 8) - 1, 0), col // width + h))

    def hd(width):
        return pl.BlockSpec((None, tm, width), lambda i, h: (h, i, 0))

    small = pl.BlockSpec((1, LANES), lambda i, h: (0, 0))
    return pl.pallas_call(
        body, grid=(nb, heads),
        in_specs=[cur(dk, col_q), prev(dk, col_q), cur(dk, col_k), prev(dk, col_k), cur(dv, col_v), prev(dv, col_v),
                  pl.BlockSpec((tm, LANES), lambda i, h: (i, 0)),
                  pl.BlockSpec((cq.shape[0], dk), lambda i, h: (0, h)), pl.BlockSpec((ck.shape[0], dk), lambda i, h: (0, h)),
                  pl.BlockSpec((cv.shape[0], dv), lambda i, h: (0, h)), small, small,
                  hd(dk), hd(dk), hd(dv), hd(1), hd(1)],
        out_specs=[hd(dk), hd(dk), hd(dv), pl.BlockSpec((tm, LANES), lambda i, h: (i, 0)), small, small],
        out_shape=[jax.ShapeDtypeStruct((heads, t, dk), F32), jax.ShapeDtypeStruct((heads, t, dk), F32),
                   jax.ShapeDtypeStruct((heads, t, dv), F32), jax.ShapeDtypeStruct((t, LANES), F32),
                   jax.ShapeDtypeStruct((1, LANES), F32), jax.ShapeDtypeStruct((1, LANES), F32)],
        scratch_shapes=[pltpu.VMEM((HALO + tm, dk), F32), pltpu.VMEM((HALO + tm, dk), F32),
                        pltpu.VMEM((HALO + tm, dv), F32)], name=name,
        compiler_params=_params("arbitrary", "arbitrary"))(
            proj, proj, proj, proj, proj, proj, pab, cq, ck, cv, a_log, dt_bias, dqn, dkn, dvv, dg, dbeta)


def _conv_bwd(proj, dc, cw, *, heads, width, col, name):
    t = proj.shape[0]
    tm = _pick(t, (384, 256, 128))
    nb = t // tm
    nk = cw.shape[0]

    def body(p_ref, pp_ref, d_ref, dn_ref, w_ref, dp_ref, dw_ref, bx_ref, bd_ref):
        i = pl.program_id(1)
        first, last = i == 0, i == nb - 1

        @pl.when(first)
        def _():
            dw_ref[...] = jnp.zeros_like(dw_ref)

        d = d_ref[...]
        _stage(bx_ref, jnp.where(first, 0.0, pp_ref[...]), p_ref[...])
        _stage_after(bd_ref, d, jnp.where(last, 0.0, dn_ref[...]))
        dx = d * w_ref[nk - 1:nk, :]
        dw_ref[nk - 1:nk, :] += jnp.sum(d * p_ref[...], axis=0, keepdims=True)
        for s in range(1, nk):
            dx += bd_ref[s:s + tm, :] * w_ref[nk - 1 - s:nk - s, :]
            dw_ref[nk - 1 - s:nk - s, :] += jnp.sum(d * bx_ref[HALO - s:HALO - s + tm, :], axis=0, keepdims=True)
        dp_ref[...] = _bf(dx)

    return pl.pallas_call(
        body, grid=(heads, nb),
        in_specs=[pl.BlockSpec((tm, width), lambda h, i: (i, col // width + h)),
                  pl.BlockSpec((8, width), lambda h, i: (jnp.maximum(i * (tm // 8) - 1, 0), col // width + h)),
                  pl.BlockSpec((None, tm, width), lambda h, i: (h, i, 0)),
                  pl.BlockSpec((None, 8, width), lambda h, i: (h, jnp.minimum((i + 1) * (tm // 8), t // 8 - 1), 0)),
                  pl.BlockSpec((nk, width), lambda h, i: (0, h))],
        out_specs=[pl.BlockSpec((tm, width), lambda h, i: (i, h)), pl.BlockSpec((nk, width), lambda h, i: (0, h))],
        out_shape=[jax.ShapeDtypeStruct((t, heads * width), BF16), jax.ShapeDtypeStruct((nk, heads * width), F32)],
        scratch_shapes=[pltpu.VMEM((HALO + tm, width), F32), pltpu.VMEM((tm + HALO, width), F32)],
        name=name, compiler_params=_params("parallel", "arbitrary"))(proj, proj, dc, dc, cw)


def _sb_pre(proj, gq, gk, *, heads, dh, col_q, col_k, col_v, name):
    t = proj.shape[0]
    tm = _pick(t, (384, 256, 128))

    def body(q_ref, k_ref, v_ref, gq_ref, gk_ref, qo_ref, ko_ref, vo_ref):
        qo_ref[...] = _bf(_rms_fwd(q_ref[...], gq_ref[...]))
        ko_ref[...] = _bf(_rms_fwd(k_ref[...], gk_ref[...]))
        vo_ref[...] = _bf(v_ref[...])

    def cur(col):
        return pl.BlockSpec((tm, dh), lambda i, h: (i, col // dh + h))

    gspec = pl.BlockSpec((1, dh), lambda i, h: (0, 0))
    ospec = pl.BlockSpec((tm, dh), lambda i, h: (i, h))
    return pl.pallas_call(
        body, grid=(t // tm, heads), in_specs=[cur(col_q), cur(col_k), cur(col_v), gspec, gspec],
        out_specs=[ospec] * 3, out_shape=[jax.ShapeDtypeStruct((t, heads * dh), BF16)] * 3, name=name,
        compiler_params=_params("parallel", "parallel"))(proj, proj, proj, gq, gk)


def _sb_tile(z, i, j, blk, key_lo, masked):
    ls = jnp.minimum(z, 0.0) - jnp.log(1.0 + jnp.exp(-jnp.abs(z)))
    if not masked:
        return None, ls, ls - z
    qpos = i * blk + _iota((blk, blk), 0)
    kpos = j * blk + _iota((blk, blk), 1)
    vis = (kpos < qpos) & (kpos >= key_lo)
    return vis, ls, jnp.where(vis, ls - z, 0.0)


def _where_vis(vis, x):
    return x if vis is None else jnp.where(vis, x, 0.0)


def _sb_sweep(i, step, init, descending):
    first, last = (i, 0) if descending else (0, i)
    carry = step(first, init, True)
    carry = lax.fori_loop(1, i, lambda n, c: step(i - n if descending else n, c, False), carry)
    return lax.cond(i > 0, lambda c: step(last, c, True), lambda c: c, carry)


def _dot2_r(x, m):
    hi, lo = _split2(x)
    return _dot(hi, m) + _dot(lo, m)


def _running_sums(x, tri, reverse, exact=True):
    groups = [x[:, s:s + LANES] for s in range(0, x.shape[1], LANES)]
    inside = [_dot2_r(g, tri) if exact else _dot(_bf(g), tri) for g in groups]
    sums = [jnp.sum(g, axis=1, keepdims=True) for g in groups]
    order = list(reversed(range(len(groups)))) if reverse else list(range(len(groups)))
    out, acc = [None] * len(groups), None
    for gi in order:
        out[gi] = inside[gi] if acc is None else inside[gi] + acc
        acc = sums[gi] if acc is None else acc + sums[gi]
    return jnp.concatenate(out, axis=1), acc


def _sb_fwd(qs, ks, vs, *, heads, dh, key_lo, name, gather=None):
    t = qs.shape[0]
    blk = _pick(t, (3 * SB_BLOCK, 2 * SB_BLOCK, SB_BLOCK))
    assert key_lo <= blk
    nq = t // blk
    assert nq <= LANES
    scale = dh ** -0.5
    hp = SB_HEADS_PER_STEP

    def body(q_ref, k_ref, v_ref, o_ref, c_ref):
        i = pl.program_id(1)
        later = jnp.where(_iota((LANES, LANES), 0) > _iota((LANES, LANES), 1), 1.0, 0.0).astype(BF16)
        lane = _iota((blk, LANES), 1)
        c_ref[...] = jnp.zeros_like(c_ref)

        def step(j, carry, masked):
            rows = pl.ds(pl.multiple_of(j * blk, blk), blk)
            hs = range(hp)
            cols = [pl.ds(hh * dh, dh) for hh in hs]
            zs = [_dot_nt(q_ref[:, cols[hh]], k_ref[rows, cols[hh]]) * scale for hh in hs]
            tiles = [_sb_tile(z, i, j, blk, key_lo, masked) for z in zs]
            sufs = [_running_sums(lk, later, reverse=True) for _, _, lk in tiles]
            wgts = [_where_vis(vis, jnp.exp(ls + suf + carry[2 * hh + 1]))
                    for hh, ((vis, ls, _), (suf, _)) in enumerate(zip(tiles, sufs))]
            accs = [carry[2 * hh] + _dot(_bf(wgts[hh]), v_ref[rows, cols[hh]]) for hh in hs]
            out = []
            for hh in hs:
                c_ref[hh] = jnp.where(lane == j, carry[2 * hh + 1], c_ref[hh])
                out += [accs[hh], carry[2 * hh + 1] + sufs[hh][1]]
            return tuple(out)

        res = _sb_sweep(i, step, (jnp.zeros((blk, dh), F32), jnp.zeros((blk, 1), F32)) * hp, descending=True)
        for hh in range(hp):
            o_ref[:, pl.ds(hh * dh, dh)] = _bf(res[2 * hh])

    full = pl.BlockSpec((t, hp * dh), lambda h, i: (0, h))
    call = dict(grid=(heads // hp, nq),
                in_specs=[pl.BlockSpec((blk, hp * dh), lambda h, i: (i, h)), full, full],
                out_specs=[pl.BlockSpec((blk, hp * dh), lambda h, i: (i, h)),
                           pl.BlockSpec((hp, blk, LANES), lambda h, i: (h, i, 0))],
                out_shape=[jax.ShapeDtypeStruct((t, heads * dh), BF16), jax.ShapeDtypeStruct((heads, t, LANES), F32)],
                name=name)
    if gather is None:
        return pl.pallas_call(body, compiler_params=_params("parallel", "parallel"), **call)(qs, ks, vs)
    res = _call_with_exchange(body, scratch_shapes=[], args=(qs, ks, vs), srcs=gather, scatter=[False] * len(gather),
                              **call)
    return [*res[:2], list(res[2:])]


def _sb_bwd(qs, ks, vs, do, carry, *, heads, dh, key_lo, name, scatter=None):
    t = qs.shape[0]
    blk = _pick(t, (3 * SB_BLOCK, 2 * SB_BLOCK, SB_BLOCK))
    assert key_lo <= blk
    nq = t // blk
    scale = dh ** -0.5
    hp = SB_HEADS_PER_STEP

    def body(q_ref, k_ref, v_ref, do_ref, c_ref, dq_ref, dk_ref, dv_ref):
        i = pl.program_id(1)

        @pl.when(i == 0)
        def _():
            dk_ref[...] = jnp.zeros_like(dk_ref)
            dv_ref[...] = jnp.zeros_like(dv_ref)

        r0 = _iota((LANES, LANES), 0)
        r1 = _iota((LANES, LANES), 1)
        later = jnp.where(r0 > r1, 1.0, 0.0).astype(BF16)
        earlier = jnp.where(r0 < r1, 1.0, 0.0).astype(BF16)

        def step(j, carry, masked):
            rows = pl.ds(pl.multiple_of(j * blk, blk), blk)
            hs = range(hp)
            cols = [pl.ds(hh * dh, dh) for hh in hs]
            zs = [_dot_nt(q_ref[:, cols[hh]], k_ref[rows, cols[hh]]) * scale for hh in hs]
            dws = [_dot_nt(do_ref[:, cols[hh]], v_ref[rows, cols[hh]]) for hh in hs]
            tiles = [_sb_tile(z, i, j, blk, key_lo, masked) for z in zs]
            sufs = [_running_sums(lk, later, reverse=True)[0] for _, _, lk in tiles]
            wgts = [_where_vis(vis, jnp.exp(ls + suf + _lane_pick(c_ref[hh], j)))
                    for hh, ((vis, ls, _), suf) in enumerate(zip(tiles, sufs))]
            es = [wgt * dw for wgt, dw in zip(wgts, dws)]
            pres = [_running_sums(e, earlier, reverse=False, exact=False) for e in es]
            dzs = []
            for hh in hs:
                vis, ls, _ = tiles[hh]
                before = _where_vis(vis, pres[hh][0] + carry[2 * hh + 1])
                sig = jnp.exp(ls)
                dzs.append(_bf((es[hh] * (1.0 - sig) - before * sig) * scale))
            dks = [_dot_tn(dzs[hh], q_ref[:, cols[hh]]) for hh in hs]
            dvs = [_dot_tn(_bf(wgts[hh]), do_ref[:, cols[hh]]) for hh in hs]
            dqs = [carry[2 * hh] + _dot(dzs[hh], k_ref[rows, cols[hh]]) for hh in hs]
            out = []
            for hh in hs:
                dk_ref[rows, cols[hh]] += dks[hh]
                dv_ref[rows, cols[hh]] += dvs[hh]
                out += [dqs[hh], carry[2 * hh + 1] + pres[hh][1]]
            return tuple(out)

        res = _sb_sweep(i, step, (jnp.zeros((blk, dh), F32), jnp.zeros((blk, 1), F32)) * hp, descending=False)
        for hh in range(hp):
            dq_ref[:, pl.ds(hh * dh, dh)] = res[2 * hh]

    full = pl.BlockSpec((t, hp * dh), lambda h, i: (0, h))
    qblk = pl.BlockSpec((blk, hp * dh), lambda h, i: (i, h))
    call = dict(grid=(heads // hp, nq),
                in_specs=[qblk, full, full, qblk, pl.BlockSpec((hp, blk, LANES), lambda h, i: (h, i, 0))],
                out_specs=[qblk, full, full], out_shape=[jax.ShapeDtypeStruct((t, heads * dh), F32)] * 3, name=name)
    if scatter is None:
        return pl.pallas_call(body, compiler_params=_params("parallel", "arbitrary"), **call)(qs, ks, vs, do, carry)
    res = _call_with_exchange(body, scratch_shapes=[], args=(qs, ks, vs, do, carry), srcs=scatter,
                              scatter=[True] * len(scatter), **call)
    return [*res[:3], list(res[3:])]


def _sb_pre_bwd(proj, gq, gk, dq, dk, dv, *, heads, dh, col_q, col_k, name):
    t = proj.shape[0]
    tm = _pick(t, (384, 256, 128))

    def body(q_ref, k_ref, gq_ref, gk_ref, dq_ref, dk_ref, dv_ref, oq_ref, ok_ref, ov_ref, dgq_ref, dgk_ref):
        @pl.when((pl.program_id(0) == 0) & (pl.program_id(1) == 0))
        def _():
            dgq_ref[...] = jnp.zeros_like(dgq_ref)
            dgk_ref[...] = jnp.zeros_like(dgk_ref)

        dq_, gq_r = _rms_bwd(q_ref[...], gq_ref[...], dq_ref[...])
        dk_, gk_r = _rms_bwd(k_ref[...], gk_ref[...], dk_ref[...])
        oq_ref[...] = _bf(dq_)
        ok_ref[...] = _bf(dk_)
        ov_ref[...] = _bf(dv_ref[...])
        dgq_ref[...] += jnp.sum(gq_r, axis=0, keepdims=True)
        dgk_ref[...] += jnp.sum(gk_r, axis=0, keepdims=True)

    def cur(col):
        return pl.BlockSpec((tm, dh), lambda i, h: (i, col // dh + h))

    gspec = pl.BlockSpec((1, dh), lambda i, h: (0, 0))
    ospec = pl.BlockSpec((tm, dh), lambda i, h: (i, h))
    return pl.pallas_call(
        body, grid=(t // tm, heads), in_specs=[cur(col_q), cur(col_k), gspec, gspec, ospec, ospec, ospec],
        out_specs=[ospec, ospec, ospec, gspec, gspec],
        out_shape=[jax.ShapeDtypeStruct((t, heads * dh), BF16)] * 3 + [jax.ShapeDtypeStruct((1, dh), F32)] * 2,
        name=name, compiler_params=_params("arbitrary", "arbitrary"))(proj, proj, gq, gk, dq, dk, dv)


PEERS = N_DEV - 1


def _exchange_copies(ins, outs, send_sems, recv_sems, local_sems, scatter):
    x, y, c = lax.axis_index("x"), lax.axis_index("y"), lax.axis_index("c")
    me = 4 * x + 2 * y + c
    copies = []
    for a in range(len(ins)):
        own = ins[a].at[me] if scatter[a] else ins[a]
        copies.append(pltpu.make_async_copy(own, outs[a].at[me], local_sems.at[a]))
        for k in range(1, N_DEV):
            px = (x + (k >> 2 & 1)) % 2
            py = (y + (k >> 1 & 1)) % 2
            pc = (c + (k & 1)) % 2
            src = ins[a].at[4 * px + 2 * py + pc] if scatter[a] else ins[a]
            copies.append(pltpu.make_async_remote_copy(
                src_ref=src, dst_ref=outs[a].at[me], send_sem=send_sems.at[a * PEERS + k - 1],
                recv_sem=recv_sems.at[a * PEERS + k - 1], device_id=(px, py, pc), device_id_type=MESH))
    return copies


def _exchange_shapes(srcs, scatter):
    return [jax.ShapeDtypeStruct(s.shape if sc else (N_DEV,) + s.shape, s.dtype) for s, sc in zip(srcs, scatter)]


def _exchange_sems(n):
    return [pltpu.SemaphoreType.DMA((n * PEERS,)), pltpu.SemaphoreType.DMA((n * PEERS,)), pltpu.SemaphoreType.DMA((n,))]


def _exchange(srcs, *, scatter, name):
    n = len(srcs)

    def body(*refs):
        copies = _exchange_copies(refs[:n], refs[n:2 * n], *refs[2 * n:], scatter)
        for cp in copies:
            cp.start()
        for cp in copies:
            cp.wait()

    any_spec = pl.BlockSpec(memory_space=pl.ANY)
    return pl.pallas_call(
        body, in_specs=[any_spec] * n, out_specs=[any_spec] * n, out_shape=_exchange_shapes(srcs, scatter),
        scratch_shapes=_exchange_sems(n), name=name,
        compiler_params=pltpu.CompilerParams(has_side_effects=True))(*srcs)


def _gather_two_level(srcs, *, name):
    n = len(srcs)

    def body(*refs):
        ins, outs = refs[:n], refs[n:2 * n]
        send_sems, recv_sems, local_sems = refs[2 * n:]
        x, y, c = lax.axis_index("x"), lax.axis_index("y"), lax.axis_index("c")
        chips = [(1 - x, y), (x, 1 - y), (1 - x, 1 - y)]

        def slab(a, px, py, pc):
            return outs[a].at[4 * px + 2 * py + pc]

        def copy(a, k, block, to, src=None):
            return pltpu.make_async_remote_copy(
                src_ref=slab(a, *block) if src is None else src, dst_ref=slab(a, *block),
                send_sem=send_sems.at[a * PEERS + k], recv_sem=recv_sems.at[a * PEERS + k],
                device_id=to, device_id_type=MESH)

        mine = [pltpu.make_async_copy(ins[a], slab(a, x, y, c), local_sems.at[a]) for a in range(n)]
        first = [copy(a, 0, (x, y, c), (x, y, 1 - c), src=ins[a]) for a in range(n)]
        first += [copy(a, 1 + j, (x, y, c), (*chip, c), src=ins[a]) for j, chip in enumerate(chips) for a in range(n)]
        for cp in mine + first:
            cp.start()
        passed = []
        for j, chip in enumerate(chips):
            for a in range(n):
                copy(a, 1 + j, (*chip, c), (x, y, c)).wait_recv()
                passed.append(copy(a, 4 + j, (*chip, c), (x, y, 1 - c)))
                passed[-1].start()
        for a in range(n):
            copy(a, 0, (x, y, 1 - c), (x, y, c)).wait_recv()
            for j, chip in enumerate(chips):
                copy(a, 4 + j, (*chip, 1 - c), (x, y, c)).wait_recv()
        for cp in first + passed:
            cp.wait_send()
        for cp in mine:
            cp.wait()

    any_spec = pl.BlockSpec(memory_space=pl.ANY)
    return pl.pallas_call(
        body, in_specs=[any_spec] * n, out_specs=[any_spec] * n, out_shape=_exchange_shapes(srcs, [False] * n),
        scratch_shapes=_exchange_sems(n), name=name,
        compiler_params=pltpu.CompilerParams(has_side_effects=True))(*srcs)


def _call_with_exchange(body, *, grid, in_specs, out_specs, out_shape, scratch_shapes, args, srcs, scatter, name):
    n, n_in, n_out, n_scr = len(srcs), len(args), len(out_shape), len(scratch_shapes)

    def full_body(*refs):
        ins, xin = refs[:n_in], refs[n_in:n_in + n]
        outs, xout = refs[n_in + n:n_in + n + n_out], refs[n_in + n + n_out:n_in + 2 * n + n_out]
        scr = refs[n_in + 2 * n + n_out:]
        ids = [pl.program_id(a) for a in range(len(grid))]
        first = functools.reduce(jnp.logical_and, [i == 0 for i in ids])
        last = functools.reduce(jnp.logical_and, [i == g - 1 for i, g in zip(ids, grid)])
        copies = _exchange_copies(xin, xout, *scr[n_scr:], scatter)

        @pl.when(first)
        def _():
            for cp in copies:
                cp.start()

        body(*ins, *outs, *scr[:n_scr])

        @pl.when(last)
        def _():
            for cp in copies:
                cp.wait()

    any_spec = pl.BlockSpec(memory_space=pl.ANY)
    return pl.pallas_call(
        full_body, grid=grid, in_specs=list(in_specs) + [any_spec] * n, out_specs=list(out_specs) + [any_spec] * n,
        out_shape=list(out_shape) + _exchange_shapes(srcs, scatter),
        scratch_shapes=list(scratch_shapes) + _exchange_sems(n), name=name,
        compiler_params=pltpu.CompilerParams(dimension_semantics=("arbitrary",) * len(grid),
                                             vmem_limit_bytes=V7X_VMEM_LIMIT_BYTES, has_side_effects=True))(*args, *srcs)


def _adam_math(g, w, m, v):
    m2 = ADAM_B1 * m + (1.0 - ADAM_B1) * g
    v2 = ADAM_B2 * v + (1.0 - ADAM_B2) * (g * g)
    m_hat = m2 / (1.0 - ADAM_B1 ** ADAM_STEP)
    v_hat = v2 / (1.0 - ADAM_B2 ** ADAM_STEP)
    return -ADAM_LR * (m_hat / (jnp.sqrt(v_hat) + ADAM_EPS) + ADAM_WD * w), m2, v2


def _adamw_slabs(slabs, w, m, v, *, name):
    r, c = w.shape
    tr = next((t for t in (2256, 752, 512, 240, 176, 128, 64, 32, 16) if r % t == 0), r)
    tc = c if r % 16 == 0 else _pick(c, (2 * LANES, LANES))

    def body(s_ref, w_ref, m_ref, v_ref, g_ref, d_ref, mo_ref, vo_ref):
        g = s_ref[0].astype(F32)
        for p in range(1, N_DEV):
            g = g + s_ref[p].astype(F32)
        g_ref[...] = g
        d_ref[...], mo_ref[...], vo_ref[...] = _adam_math(g, w_ref[...], m_ref[...], v_ref[...])

    spec = pl.BlockSpec((tr, tc), lambda i, j: (i, j))
    return pl.pallas_call(
        body, grid=(r // tr, c // tc), in_specs=[pl.BlockSpec((N_DEV, tr, tc), lambda i, j: (0, i, j)), spec, spec, spec],
        out_specs=[spec] * 4, out_shape=[jax.ShapeDtypeStruct((r, c), F32)] * 4, name=name,
        compiler_params=_params("parallel", "parallel"))(slabs, w, m, v)


def _adamw_small(g, w, m, v, *, name):
    def body(g_ref, w_ref, m_ref, v_ref, d_ref, mo_ref, vo_ref):
        d_ref[...], mo_ref[...], vo_ref[...] = _adam_math(g_ref[...], w_ref[...], m_ref[...], v_ref[...])

    return pl.pallas_call(body, out_shape=[jax.ShapeDtypeStruct(w.shape, F32)] * 3, name=name)(g, w, m, v)


def _sum_slabs(slabs, *, name):
    def body(s_ref, o_ref):
        acc = s_ref[0]
        for p in range(1, N_DEV):
            acc = acc + s_ref[p]
        o_ref[...] = acc

    return pl.pallas_call(body, out_shape=jax.ShapeDtypeStruct(slabs.shape[1:], F32), name=name)(slabs)


def _pad_lanes(a):
    return jnp.pad(a, ((0, 0), (0, LANES - a.shape[1])))


def _w_in_slabs(before, after, ab, n_ab):
    col_ab = before.shape[0]
    pw = col_ab + n_ab + after.shape[0]
    c = pw // N_DEV
    parts = [(0, col_ab, before, 0), (col_ab, col_ab + n_ab, ab, col_ab), (col_ab + n_ab, pw, after, col_ab + n_ab)]
    slabs = []
    for p in range(N_DEV):
        pieces = []
        for lo, hi, src, shift in parts:
            a, b = max(lo, c * p), min(hi, c * (p + 1))
            if a < b:
                pieces.append(src[a - shift:b - shift])
        slabs.append(jnp.concatenate(pieces, axis=0))
    return jnp.stack(slabs)


def _local_step(x, target, meta, g_mix, wt_main, wt_ab, cq, ck, cv, a_log, dt_bias, g_dn, g_sbq, g_sbk, g_ffn, rest,
                shards=False):
    seq, d = x.shape
    n_meta = meta.shape[0]
    heads = a_log.shape[1]
    qk = cq.shape[1]
    dvt = cv.shape[1]
    dk, dv = qk // heads, dvt // heads
    dh = g_sbq.shape[1]
    sbw = rest[2].shape[0] * N_DEV if shards else rest[1].shape[0]
    sb_heads = sbw // dh
    pad_l = (-n_meta) % CHUNK
    row_x = pad_l + n_meta
    rows = row_x + seq
    t = -(-rows // GDN_ROWS) * GDN_ROWS
    col_q, col_k, col_v, col_z = 0, qk, 2 * qk, 2 * qk + dvt
    col_sq = 2 * qk + 2 * dvt
    col_sk, col_sv, col_gd, col_gs = col_sq + sbw, col_sq + 2 * sbw, col_sq + 3 * sbw, col_sq + 3 * sbw + d

    def rows_pad(a):
        return jnp.concatenate([jnp.zeros((row_x, d), F32), a, jnp.zeros((t - rows, d), F32)], axis=0)

    h0 = jnp.concatenate([jnp.zeros((pad_l, d), F32), meta, x, jnp.zeros((t - rows, d), F32)], axis=0)
    tgt = rows_pad(target)
    a_log_p, dt_p = _pad_lanes(a_log), _pad_lanes(dt_bias)

    proj, n1 = _mm_norm(h0, g_mix, wt_main, name="proj")
    pab = _mm_nt(n1, wt_ab, out_dtype=F32, name="proj_ab")
    gk = dict(heads=heads, dk=dk, dv=dv, col_q=col_q, col_k=col_k, col_v=col_v, row_lo=pad_l, row_hi=rows)
    qn, kn, vv, g, beta = _gdn_pre(proj, pab, cq, ck, cv, a_log_p, dt_p, name="gdn_pre", **gk)
    u, w, pm, qd, kd, egl, tinv = _gdn_prep(qn, kn, vv, g, beta, name="gdn_prep")
    o_raw, o_dn, states = _gdn_scan(u, w, pm, qd, kd, egl, proj, g_dn, col_z=col_z, name="gdn_scan")
    qs, ks, vs = _sb_pre(proj, g_sbq, g_sbk, heads=sb_heads, dh=dh, col_q=col_sq, col_k=col_sk, col_v=col_sv,
                         name="sb_pre")
    if shards:
        o_sb, carry, (g_fi, g_bd, g_bs, g_out, g_fo) = _sb_fwd(qs, ks, vs, heads=sb_heads, dh=dh, key_lo=pad_l,
                                                                name="sb_fwd", gather=list(rest))
        wt_fi = g_fi.reshape(-1, d)
        d_ff = wt_fi.shape[0] // 2
        w_bd, w_bs, w_out, wt_fg, wt_fu, w_fo = (g_bd.reshape(-1, d), g_bs.reshape(-1, d), g_out.reshape(-1, d),
                                                 wt_fi[:d_ff], wt_fi[d_ff:], g_fo.reshape(-1, d))
    else:
        o_sb, carry = _sb_fwd(qs, ks, vs, heads=sb_heads, dh=dh, key_lo=pad_l, name="sb_fwd")
        w_bd, w_bs, w_out, wt_fg, wt_fu, w_fo = rest
    merged, br_dn, br_sb = _merge_fwd(o_dn, o_sb, w_bd, w_bs, proj, col_gd=col_gd, col_gs=col_gs, name="merge")
    h1 = _mm_res(h0, merged, w_out, name="mix_out")
    gate, up, act, n2 = _mm_norm_swiglu(h1, g_ffn, wt_fg, wt_fu, name="ffn_in")
    dy, dyb, lsum = _mm_res_loss(h1, act, w_fo, tgt, row0=row_x, nrows=seq, name="ffn_out_loss")

    dgate, dup = _swiglu_bwd(dyb, w_fo, gate, up, name="ffn_out_bwd")
    d_w_fo = _mm_tn(act, dyb, name="dw_ffn_out")
    d_wt_fi = _mm_tn([dgate, dup], n2, name="dw_ffn_in")
    d_wt_fg, d_wt_fu = d_wt_fi[:dgate.shape[1]], d_wt_fi[dgate.shape[1]:]
    dh1, dh1b, d_g_ffn = _mm_rmsbwd([(dgate, wt_fg), (dup, wt_fu)], None, h1, g_ffn, dy, name="ffn_in_bwd")

    dbd, dbs, dgd, dgs = _merge_bwd(dh1b, w_out, proj, br_dn, br_sb, col_gd=col_gd, col_gs=col_gs, name="mix_out_bwd")
    d_w_out = _mm_tn(merged, dh1b, name="dw_out")
    d_w_bd = _mm_tn(o_dn, dbd, name="dw_branch_dn")
    d_w_bs = _mm_tn(o_sb, dbs, name="dw_branch_sb")
    do_dn = _mm_nt(dbd, w_bd, out_dtype=F32, name="branch_dn_bwd")
    do_sb = _mm_nt(dbs, w_bs, out_dtype=BF16, name="branch_sb_bwd")

    do_raw, dz, d_g_dn = _gdn_post_bwd(o_raw, proj, g_dn, do_dn, col_z=col_z, name="gdn_post_bwd")
    du, dw, dp, dqd, dkd, dgl = _gdn_bwd_scan(u, w, pm, qd, kd, egl, states, do_raw, name="gdn_bwd_scan")
    dqn, dkn, dvv, dg, dbeta = _gdn_bwd_prep(qn, kn, vv, g, beta, tinv, u, w, du, dw, dp, dqd, dkd, dgl,
                                            name="gdn_bwd_prep")
    dcq, dck, dcv, dpab, d_a_log, d_dt = _gdn_pre_bwd_a(proj, pab, cq, ck, cv, a_log_p, dt_p, dqn, dkn, dvv, dg, dbeta,
                                                        name="gdn_pre_bwd", **gk)
    dpq, d_cq = _conv_bwd(proj, dcq, cq, heads=heads, width=dk, col=col_q, name="conv_q_bwd")
    dpk, d_ck = _conv_bwd(proj, dck, ck, heads=heads, width=dk, col=col_k, name="conv_k_bwd")
    dpv, d_cv = _conv_bwd(proj, dcv, cv, heads=heads, width=dv, col=col_v, name="conv_v_bwd")

    early = None
    if shards:
        slabs = [_bf(a).reshape(N_DEV, -1, d) for a in (d_wt_fi, d_w_bd, d_w_bs, d_w_out, d_w_fo)]
        dqs, dks, dvs, early = _sb_bwd(qs, ks, vs, do_sb, carry, heads=sb_heads, dh=dh, key_lo=pad_l, name="sb_bwd",
                                       scatter=slabs)
    else:
        dqs, dks, dvs = _sb_bwd(qs, ks, vs, do_sb, carry, heads=sb_heads, dh=dh, key_lo=pad_l, name="sb_bwd")
    dsq, dsk, dsv, d_g_sbq, d_g_sbk = _sb_pre_bwd(proj, g_sbq, g_sbk, dqs, dks, dvs, heads=sb_heads, dh=dh,
                                                   col_q=col_sq, col_k=col_sk, name="sb_pre_bwd")

    dproj = [dpq, dpk, dpv, dz, dsq, dsk, dsv, dgd, dgs]
    dpab_b = _bf(dpab)
    n_dn = 4
    d_wt_dn = _mm_tn(dproj[:n_dn], n1, name="dw_in_dn")
    d_wt_sb = _mm_tn(dproj[n_dn:], n1, name="dw_in_sb")
    assert d_wt_dn.shape[0] == col_sq
    d_wt_main = jnp.concatenate([d_wt_dn, d_wt_sb], axis=0)
    d_wt_ab = _mm_tn(dpab_b, n1, name="dw_in_ab")
    s_in = None
    if shards:
        slabs = _w_in_slabs(d_wt_dn, d_wt_sb, d_wt_ab, 2 * heads)
        dh0, _, d_g_mix, (s_in,) = _mm_rmsbwd([(dproj, wt_main)], (dpab_b, wt_ab), h0, g_mix, dh1, name="proj_bwd",
                                              scatter=[slabs])
    else:
        dh0, _, d_g_mix = _mm_rmsbwd([(dproj, wt_main)], (dpab_b, wt_ab), h0, g_mix, dh1, name="proj_bwd")

    return dict(s_in=s_in, lsum=lsum, grad_x=dh0[row_x:rows], d_meta=dh0[pad_l:row_x], d_g_mix=d_g_mix,
                d_wt_main=d_wt_main, d_wt_ab=d_wt_ab, d_cq=d_cq, d_ck=d_ck, d_cv=d_cv, d_a_log=d_a_log[:, :heads],
                d_dt=d_dt[:, :heads], d_g_dn=d_g_dn, d_g_sbq=d_g_sbq, d_g_sbk=d_g_sbk, d_w_bd=d_w_bd, d_w_bs=d_w_bs,
                d_w_out=d_w_out, d_g_ffn=d_g_ffn, d_wt_fg=d_wt_fg, d_wt_fu=d_wt_fu, d_w_fo=d_w_fo, early=early)


def _pack(parts):
    flat = []
    for a in parts:
        a = a.reshape(-1)
        flat.append(jnp.pad(a, (0, (-a.shape[0]) % LANES)))
    v = jnp.concatenate(flat)
    v = jnp.pad(v, (0, (-v.shape[0]) % (8 * LANES)))
    return v.reshape(-1, LANES)


def _unpack(packed, shapes):
    flat = packed.reshape(-1)
    out, pos = [], 0
    for s in shapes:
        n = math.prod(s)
        out.append(flat[pos:pos + n].reshape(s))
        pos += n + (-n) % LANES
    return out


def kernel(x, meta_tokens, norm_mix_gain, w_in, conv_q, conv_k, conv_v, dn_a_log, dn_dt_bias, dn_out_norm_gain, sb_q_norm_gain, sb_k_norm_gain, w_branch_dn, w_branch_sb, w_out, norm_ffn_gain, w_ffn_in, w_ffn_out, loss_target, m_meta_tokens, m_norm_mix_gain, m_w_in, m_conv_q, m_conv_k, m_conv_v, m_dn_a_log, m_dn_dt_bias, m_dn_out_norm_gain, m_sb_q_norm_gain, m_sb_k_norm_gain, m_w_branch_dn, m_w_branch_sb, m_w_out, m_norm_ffn_gain, m_w_ffn_in, m_w_ffn_out, v_meta_tokens, v_norm_mix_gain, v_w_in, v_conv_q, v_conv_k, v_conv_v, v_dn_a_log, v_dn_dt_bias, v_dn_out_norm_gain, v_sb_q_norm_gain, v_sb_k_norm_gain, v_w_branch_dn, v_w_branch_sb, v_w_out, v_norm_ffn_gain, v_w_ffn_in, v_w_ffn_out):
    me = 4 * lax.axis_index("x") + 2 * lax.axis_index("y") + lax.axis_index("c")
    heads = dn_a_log.shape[1]
    d = x.shape[2]
    qk = conv_q.shape[2] * N_DEV
    dvt = conv_v.shape[2] * N_DEV
    col_ab = 2 * qk + 2 * dvt

    small_shapes = [meta_tokens.shape, conv_q.shape[1:], conv_k.shape[1:], conv_v.shape[1:]]
    small = _pack([meta_tokens, conv_q[0], conv_k[0], conv_v[0]])
    def features_major(a):
        return jnp.transpose(a, (2, 0, 1)).reshape(a.shape[2], a.shape[1])

    g_in, g_small = _gather_two_level([_bf(features_major(w_in)), small], name="gather_w_in")
    wt_full = g_in.reshape(-1, d)
    wt_main = jnp.concatenate([wt_full[:col_ab], wt_full[col_ab + 2 * heads:]], axis=0)
    wt_ab = jnp.pad(wt_full[col_ab:col_ab + 2 * heads], ((0, LANES - 2 * heads), (0, 0)))
    parts = [_unpack(g_small[p], small_shapes) for p in range(N_DEV)]
    meta_f, cq_f, ck_f, cv_f = (jnp.concatenate([parts[p][a] for p in range(N_DEV)], axis=1) for a in range(4))

    r = _local_step(x[0], loss_target[0], meta_f, norm_mix_gain, wt_main, wt_ab, cq_f, ck_f, cv_f, dn_a_log, dn_dt_bias,
                    dn_out_norm_gain, sb_q_norm_gain, sb_k_norm_gain, norm_ffn_gain,
                    (_bf(features_major(w_ffn_in)), _bf(w_branch_dn[0]), _bf(w_branch_sb[0]), _bf(w_out[0]),
                     _bf(w_ffn_out[0])), shards=True)
    s_fi, s_bd, s_bs, s_out, s_fo = r["early"]
    s_in = r["s_in"]

    loss_part = (0.5 / d) * jnp.sum(r["lsum"], axis=1, keepdims=True)
    small_g = [r["d_meta"], r["d_g_mix"], r["d_cq"], r["d_ck"], r["d_cv"], r["d_a_log"], r["d_dt"], r["d_g_dn"],
               r["d_g_sbq"], r["d_g_sbk"], r["d_g_ffn"], loss_part]
    (g_packs,) = _exchange([_pack(small_g)], scatter=[False], name="gather_small_grads")
    (g_meta, g_mix, g_cq, g_ck, g_cv, g_al, g_dt, g_gdn, g_sbq, g_sbk, g_ffn, loss) = _unpack(
        _sum_slabs(g_packs, name="sum_small_grads"), [a.shape for a in small_g])

    def mine(a, width):
        return lax.dynamic_slice_in_dim(a, me * width, width, axis=1)

    big = dict(w_in=(s_in, w_in, m_w_in, v_w_in), w_branch_dn=(s_bd, w_branch_dn, m_w_branch_dn, v_w_branch_dn),
               w_branch_sb=(s_bs, w_branch_sb, m_w_branch_sb, v_w_branch_sb), w_out=(s_out, w_out, m_w_out, v_w_out),
               w_ffn_in=(s_fi, w_ffn_in, m_w_ffn_in, v_w_ffn_in), w_ffn_out=(s_fo, w_ffn_out, m_w_ffn_out, v_w_ffn_out))
    tiny = dict(meta_tokens=(mine(g_meta, d // N_DEV), meta_tokens, m_meta_tokens, v_meta_tokens),
                norm_mix_gain=(g_mix, norm_mix_gain, m_norm_mix_gain, v_norm_mix_gain),
                conv_q=(mine(g_cq, qk // N_DEV), conv_q[0], m_conv_q[0], v_conv_q[0]),
                conv_k=(mine(g_ck, qk // N_DEV), conv_k[0], m_conv_k[0], v_conv_k[0]),
                conv_v=(mine(g_cv, dvt // N_DEV), conv_v[0], m_conv_v[0], v_conv_v[0]),
                dn_a_log=(g_al, dn_a_log, m_dn_a_log, v_dn_a_log), dn_dt_bias=(g_dt, dn_dt_bias, m_dn_dt_bias, v_dn_dt_bias),
                dn_out_norm_gain=(g_gdn, dn_out_norm_gain, m_dn_out_norm_gain, v_dn_out_norm_gain),
                sb_q_norm_gain=(g_sbq, sb_q_norm_gain, m_sb_q_norm_gain, v_sb_q_norm_gain),
                sb_k_norm_gain=(g_sbk, sb_k_norm_gain, m_sb_k_norm_gain, v_sb_k_norm_gain),
                norm_ffn_gain=(g_ffn, norm_ffn_gain, m_norm_ffn_gain, v_norm_ffn_gain))
    order = ["meta_tokens", "norm_mix_gain", "w_in", "conv_q", "conv_k", "conv_v", "dn_a_log", "dn_dt_bias",
             "dn_out_norm_gain", "sb_q_norm_gain", "sb_k_norm_gain", "w_branch_dn", "w_branch_sb", "w_out",
             "norm_ffn_gain", "w_ffn_in", "w_ffn_out"]
    grads, deltas, new_m, new_v = [], [], [], []
    for name in order:
        if name in ("w_in", "w_ffn_in"):
            slabs, w, m, v = big[name]
            res = _adamw_slabs(slabs, *(features_major(a).reshape(slabs.shape[1:]) for a in (w, m, v)), name="adamw_" + name)
            g, dl, mo, vo = (jnp.transpose(a.reshape(w.shape[2], 1, w.shape[1]), (1, 2, 0)) for a in res)
            like = w.shape
        elif name in big:
            slabs, w, m, v = big[name]
            g, dl, mo, vo = _adamw_slabs(slabs, w[0], m[0], v[0], name="adamw_" + name)
            like = w.shape
        else:
            g, w, m, v = tiny[name]
            like = dict(conv_q=conv_q, conv_k=conv_k, conv_v=conv_v).get(name, w).shape
            dl, mo, vo = _adamw_small(g, w, m, v, name="adamw_" + name)
        for lst, a in ((grads, g), (deltas, dl), (new_m, mo), (new_v, vo)):
            lst.append(a.reshape(like))
    return (loss.reshape(()), r["grad_x"][None], *grads, *deltas, *new_m, *new_v)
```

```python
import functools
import math

import jax
import jax.numpy as jnp
from jax import lax
from jax.experimental import pallas as pl
from jax.experimental.pallas import tpu as pltpu

F32 = jnp.float32
BF16 = jnp.bfloat16

N_DEV = 8
CHUNK = 64
CHUNK_SHIFT = 6
GDN_ROWS = 2 * CHUNK
SB_BLOCK = 128
SB_HEADS_PER_STEP = 2
LANES = 128
RMS_EPS = 1e-6
L2_EPS = 1e-6
ADAM_LR = 0.001
ADAM_B1 = 0.9
ADAM_B2 = 0.999
ADAM_EPS = 1e-08
ADAM_WD = 0.01
ADAM_STEP = 10
V7X_VMEM_LIMIT_BYTES = 56 * 1024 * 1024
MM_TN_OUT_BLOCK_BYTES = 6 * 1024 * 1024
MM_PIECES_VMEM_BYTES = V7X_VMEM_LIMIT_BYTES // 2
ROWS_BIG = (1056, 512, 384, 256, 128)
ROWS_MID = (528, 384, 256, 128)
SCAN_HEADS = 4
PREP_HEADS = 2

MESH = pl.DeviceIdType.MESH


def _params(*sem):
    return pltpu.CompilerParams(dimension_semantics=sem or None, vmem_limit_bytes=V7X_VMEM_LIMIT_BYTES)


def _pick(n, cands):
    for c in cands:
        if n % c == 0:
            return c
    raise ValueError(f"no block size among {cands} divides {n}")


def _bf(x):
    return x.astype(BF16)


def _dot(a, b):
    return jnp.dot(a, b, preferred_element_type=F32)


def _dot_nt(a, b):
    return lax.dot_general(a, b, (((1,), (1,)), ((), ())), preferred_element_type=F32)


def _dot_tn(a, b):
    return lax.dot_general(a, b, (((0,), (0,)), ((), ())), preferred_element_type=F32)


def _split2(x):
    hi = _bf(x)
    return hi, _bf(x - hi.astype(F32))


def _split3(x):
    hi = _bf(x)
    r = x - hi.astype(F32)
    mid = _bf(r)
    return hi, mid, _bf(r - mid.astype(F32))


def _dot_hp(a, b, dot=_dot):
    ah, al = _split2(a)
    bh, bl = _split2(b)
    return dot(ah, bh) + dot(ah, bl) + dot(al, bh)


def _dot_exact_l(m, x, dot=_dot):
    h, mi, lo = _split3(x)
    return dot(m, h) + dot(m, mi) + dot(m, lo)


def _sigmoid(x):
    return 1.0 / (1.0 + jnp.exp(-x))


def _silu(x):
    return x * _sigmoid(x)


def _silu_grad(x):
    s = _sigmoid(x)
    return s * (1.0 + x * (1.0 - s))


def _softplus(x):
    return jnp.maximum(x, 0.0) + jnp.log(1.0 + jnp.exp(-jnp.abs(x)))


def _rms_fwd(h, gain):
    r = lax.rsqrt(jnp.mean(h * h, axis=-1, keepdims=True) + RMS_EPS)
    return h * r * gain


def _rms_bwd(h, gain, dy):
    r = lax.rsqrt(jnp.mean(h * h, axis=-1, keepdims=True) + RMS_EPS)
    dyg = dy * gain
    dh = r * dyg - h * (r * r * r) * jnp.mean(dyg * h, axis=-1, keepdims=True)
    return dh, dy * h * r


def _iota(shape, dim):
    return lax.broadcasted_iota(jnp.int32, shape, dim)


def _lane_pick(x, idx):
    return jnp.sum(jnp.where(_iota(x.shape, 1) == idx, x, 0.0), axis=1, keepdims=True)


def _mm_nt(a, b, *, out_dtype, name):
    m, k = a.shape
    n = b.shape[0]
    tm, tn = _pick(m, ROWS_BIG), _pick(n, (1024, 512, 256, 128))

    def body(a_ref, b_ref, o_ref):
        o_ref[...] = _dot_nt(a_ref[...], b_ref[...]).astype(out_dtype)

    return pl.pallas_call(
        body, grid=(m // tm, n // tn),
        in_specs=[pl.BlockSpec((tm, k), lambda i, j: (i, 0)), pl.BlockSpec((tn, k), lambda i, j: (j, 0))],
        out_specs=pl.BlockSpec((tm, tn), lambda i, j: (i, j)),
        out_shape=jax.ShapeDtypeStruct((m, n), out_dtype), name=name,
        compiler_params=_params("parallel", "parallel"))(a, b)


def _column_pieces(a):
    pieces = list(a) if isinstance(a, (list, tuple)) else [a]
    widths = [p.shape[1] for p in pieces]
    return pieces, widths, functools.reduce(math.gcd, widths)


def _piece_spans(widths, block):
    ends = [sum(widths[:j + 1]) // block for j in range(len(widths))]
    return list(zip([0] + ends[:-1], ends))


def _mm_tn(a, b, *, name):
    pieces, widths, unit = _column_pieces(a)
    t, m = pieces[0].shape[0], sum(widths)
    n = b.shape[1]
    tn = _pick(n, (2816, 2048, 1408, 1024, 512, 256, 128))
    tm = _pick(unit, tuple(c for c in (1408, 1024, 512, 256, 128) if c * tn * 4 <= MM_TN_OUT_BLOCK_BYTES))
    held = 2 * len(pieces) * tm * pieces[0].dtype.itemsize
    tk = _pick(t, tuple(c for c in (1408, 1024, 704, 512, 384, 256, 128) if c * held <= MM_PIECES_VMEM_BYTES))
    nk = t // tk
    spans = _piece_spans(widths, tm)

    def body(*refs):
        a_refs, (b_ref, o_ref, acc_ref) = refs[:len(pieces)], refs[len(pieces):]
        i, k = pl.program_id(0), pl.program_id(2)

        @pl.when(k == 0)
        def _():
            acc_ref[...] = jnp.zeros_like(acc_ref)

        for a_ref, (lo, hi) in zip(a_refs, spans):
            def add(a_ref=a_ref):
                acc_ref[...] += _dot_tn(a_ref[...], b_ref[...])
            if len(pieces) == 1:
                add()
            else:
                pl.when((i >= lo) & (i < hi))(add)

        @pl.when(k == nk - 1)
        def _():
            o_ref[...] = _bf(acc_ref[...])

    def a_spec(lo, hi):
        if len(pieces) == 1:
            return pl.BlockSpec((tk, tm), lambda i, j, k: (k, i))
        return pl.BlockSpec((tk, tm), lambda i, j, k: (jnp.where(i < lo, 0, jnp.where(i >= hi, nk - 1, k)),
                                                       jnp.clip(i - lo, 0, hi - lo - 1)))

    return pl.pallas_call(
        body, grid=(m // tm, n // tn, nk),
        in_specs=[a_spec(lo, hi) for lo, hi in spans] + [pl.BlockSpec((tk, tn), lambda i, j, k: (k, j))],
        out_specs=pl.BlockSpec((tm, tn), lambda i, j, k: (i, j)),
        out_shape=jax.ShapeDtypeStruct((m, n), BF16), scratch_shapes=[pltpu.VMEM((tm, tn), F32)], name=name,
        compiler_params=_params("parallel", "parallel", "arbitrary"))(*pieces, b)


def _mm_norm(h, gain, wt, *, name):
    m, k = h.shape
    n = wt.shape[0]
    tm, tn = _pick(m, ROWS_BIG), _pick(n, (1024, 512, 256, 128))

    def body(h_ref, g_ref, w_ref, o_ref, n_ref):
        @pl.when(pl.program_id(1) == 0)
        def _():
            n_ref[...] = _bf(_rms_fwd(h_ref[...], g_ref[...]))

        o_ref[...] = _dot_nt(n_ref[...], w_ref[...])

    return pl.pallas_call(
        body, grid=(m // tm, n // tn),
        in_specs=[pl.BlockSpec((tm, k), lambda i, j: (i, 0)), pl.BlockSpec((1, k), lambda i, j: (0, 0)),
                  pl.BlockSpec((tn, k), lambda i, j: (j, 0))],
        out_specs=[pl.BlockSpec((tm, tn), lambda i, j: (i, j)), pl.BlockSpec((tm, k), lambda i, j: (i, 0))],
        out_shape=[jax.ShapeDtypeStruct((m, n), F32), jax.ShapeDtypeStruct((m, k), BF16)], name=name,
        compiler_params=_params("parallel", "arbitrary"))(h, gain, wt)


def _mm_norm_swiglu(h, gain, wt, *, name):
    m, k = h.shape
    n = wt.shape[0] // 2
    tm, tn = _pick(m, ROWS_MID), _pick(n, (1408, 1024, 512, 256, 128))

    def body(h_ref, g_ref, wg_ref, wu_ref, gate_ref, up_ref, act_ref, n_ref):
        @pl.when(pl.program_id(1) == 0)
        def _():
            n_ref[...] = _bf(_rms_fwd(h_ref[...], g_ref[...]))

        gate = _dot_nt(n_ref[...], wg_ref[...])
        up = _dot_nt(n_ref[...], wu_ref[...])
        gate_ref[...] = gate
        up_ref[...] = up
        act_ref[...] = _bf(_silu(gate) * up)

    ospec = pl.BlockSpec((tm, tn), lambda i, j: (i, j))
    return pl.pallas_call(
        body, grid=(m // tm, n // tn),
        in_specs=[pl.BlockSpec((tm, k), lambda i, j: (i, 0)), pl.BlockSpec((1, k), lambda i, j: (0, 0)),
                  pl.BlockSpec((tn, k), lambda i, j: (j, 0)), pl.BlockSpec((tn, k), lambda i, j: (n // tn + j, 0))],
        out_specs=[ospec, ospec, ospec, pl.BlockSpec((tm, k), lambda i, j: (i, 0))],
        out_shape=[jax.ShapeDtypeStruct((m, n), F32), jax.ShapeDtypeStruct((m, n), F32),
                   jax.ShapeDtypeStruct((m, n), BF16), jax.ShapeDtypeStruct((m, k), BF16)], name=name,
        compiler_params=_params("parallel", "arbitrary"))(h, gain, wt, wt)


def _mm_res(res, a, b, *, name):
    m, k = a.shape
    n = b.shape[1]
    tm, tn = _pick(m, ROWS_BIG), _pick(n, (1024, 512, 256, 128))

    def body(r_ref, a_ref, b_ref, o_ref):
        o_ref[...] = r_ref[...] + _dot(a_ref[...], b_ref[...])

    return pl.pallas_call(
        body, grid=(m // tm, n // tn),
        in_specs=[pl.BlockSpec((tm, tn), lambda i, j: (i, j)), pl.BlockSpec((tm, k), lambda i, j: (i, 0)),
                  pl.BlockSpec((k, tn), lambda i, j: (0, j))],
        out_specs=pl.BlockSpec((tm, tn), lambda i, j: (i, j)),
        out_shape=jax.ShapeDtypeStruct((m, n), F32), name=name,
        compiler_params=_params("parallel", "parallel"))(res, a, b)


def _mm_res_loss(res, a, b, target, *, row0, nrows, name):
    m, k = a.shape
    n = b.shape[1]
    tm = _pick(m, ROWS_MID)

    def body(r_ref, a_ref, b_ref, t_ref, dy_ref, dyb_ref, ls_ref):
        i = pl.program_id(0)

        @pl.when(i == 0)
        def _():
            ls_ref[...] = jnp.zeros_like(ls_ref)

        y = r_ref[...] + _dot(a_ref[...], b_ref[...])
        row = i * tm + _iota((tm, n), 0)
        e = jnp.where((row >= row0) & (row < row0 + nrows), y - t_ref[...], 0.0)
        dy = e / n
        dy_ref[...] = dy
        dyb_ref[...] = _bf(dy)
        ls_ref[...] += jnp.sum(e * e, axis=0, keepdims=True)

    rspec = pl.BlockSpec((tm, n), lambda i: (i, 0))
    return pl.pallas_call(
        body, grid=(m // tm,),
        in_specs=[rspec, pl.BlockSpec((tm, k), lambda i: (i, 0)), pl.BlockSpec((k, n), lambda i: (0, 0)), rspec],
        out_specs=[rspec, rspec, pl.BlockSpec((1, n), lambda i: (0, 0))],
        out_shape=[jax.ShapeDtypeStruct((m, n), F32), jax.ShapeDtypeStruct((m, n), BF16),
                   jax.ShapeDtypeStruct((1, n), F32)], name=name,
        compiler_params=_params("arbitrary"))(res, a, b, target)


def _merge_fwd(o_dn, o_sb, wbd, wbs, proj, *, col_gd, col_gs, name):
    m, kd = o_dn.shape
    ks = o_sb.shape[1]
    n = wbd.shape[1]
    tm = _pick(m, ROWS_BIG)
    tn = _pick(math.gcd(n, math.gcd(col_gd, col_gs)), (512, 256, 128))

    def body(od_ref, os_ref, wd_ref, ws_ref, gd_ref, gs_ref, mg_ref, bd_ref, bs_ref):
        bd = _dot(od_ref[...], wd_ref[...])
        bs = _dot(os_ref[...], ws_ref[...])
        bd_ref[...] = bd
        bs_ref[...] = bs
        mg_ref[...] = _bf(_sigmoid(gd_ref[...]) * bd + _sigmoid(gs_ref[...]) * bs)

    ospec = pl.BlockSpec((tm, tn), lambda i, j: (i, j))
    return pl.pallas_call(
        body, grid=(m // tm, n // tn),
        in_specs=[pl.BlockSpec((tm, kd), lambda i, j: (i, 0)), pl.BlockSpec((tm, ks), lambda i, j: (i, 0)),
                  pl.BlockSpec((kd, tn), lambda i, j: (0, j)), pl.BlockSpec((ks, tn), lambda i, j: (0, j)),
                  pl.BlockSpec((tm, tn), lambda i, j: (i, col_gd // tn + j)),
                  pl.BlockSpec((tm, tn), lambda i, j: (i, col_gs // tn + j))],
        out_specs=[ospec, ospec, ospec],
        out_shape=[jax.ShapeDtypeStruct((m, n), BF16), jax.ShapeDtypeStruct((m, n), F32),
                   jax.ShapeDtypeStruct((m, n), F32)], name=name,
        compiler_params=_params("parallel", "parallel"))(o_dn, o_sb, wbd, wbs, proj, proj)


def _merge_bwd(dh, w_out, proj, br_dn, br_sb, *, col_gd, col_gs, name):
    m, k = dh.shape
    n = w_out.shape[0]
    tm = _pick(m, ROWS_BIG)
    tn = _pick(math.gcd(n, math.gcd(col_gd, col_gs)), (512, 256, 128))

    def body(dh_ref, w_ref, gd_ref, gs_ref, bd_ref, bs_ref, dbd_ref, dbs_ref, dgd_ref, dgs_ref):
        dm = _dot_nt(dh_ref[...], w_ref[...])
        sd = _sigmoid(gd_ref[...])
        ss = _sigmoid(gs_ref[...])
        dbd_ref[...] = _bf(dm * sd)
        dbs_ref[...] = _bf(dm * ss)
        dgd_ref[...] = _bf(dm * bd_ref[...] * sd * (1.0 - sd))
        dgs_ref[...] = _bf(dm * bs_ref[...] * ss * (1.0 - ss))

    ospec = pl.BlockSpec((tm, tn), lambda i, j: (i, j))
    return pl.pallas_call(
        body, grid=(m // tm, n // tn),
        in_specs=[pl.BlockSpec((tm, k), lambda i, j: (i, 0)), pl.BlockSpec((tn, k), lambda i, j: (j, 0)),
                  pl.BlockSpec((tm, tn), lambda i, j: (i, col_gd // tn + j)),
                  pl.BlockSpec((tm, tn), lambda i, j: (i, col_gs // tn + j)), ospec, ospec],
        out_specs=[ospec] * 4,
        out_shape=[jax.ShapeDtypeStruct((m, n), BF16)] * 4, name=name,
        compiler_params=_params("parallel", "parallel"))(dh, w_out, proj, proj, br_dn, br_sb)


def _swiglu_bwd(dy, wfo, gate, up, *, name):
    m, k = dy.shape
    n = wfo.shape[0]
    tm, tn = _pick(m, ROWS_MID), _pick(n, (1408, 1024, 512, 256, 128))

    def body(dy_ref, w_ref, g_ref, u_ref, dg_ref, du_ref):
        da = _dot_nt(dy_ref[...], w_ref[...])
        g = g_ref[...]
        dg_ref[...] = _bf(da * u_ref[...] * _silu_grad(g))
        du_ref[...] = _bf(da * _silu(g))

    ospec = pl.BlockSpec((tm, tn), lambda i, j: (i, j))
    return pl.pallas_call(
        body, grid=(m // tm, n // tn),
        in_specs=[pl.BlockSpec((tm, k), lambda i, j: (i, 0)), pl.BlockSpec((tn, k), lambda i, j: (j, 0)), ospec, ospec],
        out_specs=[ospec, ospec], out_shape=[jax.ShapeDtypeStruct((m, n), BF16)] * 2, name=name,
        compiler_params=_params("parallel", "parallel"))(dy, wfo, gate, up)


def _mm_rmsbwd(pairs, extra, h, gain, dres, *, name, scatter=None):
    split = [_column_pieces(a) for a, _ in pairs]
    m, k = h.shape[0], sum(split[0][1])
    n = h.shape[1]
    tm = _pick(m, ROWS_MID)
    tk = _pick(functools.reduce(math.gcd, [unit for _, _, unit in split]), (1408, 1024, 512, 256, 128))
    nk = k // tk
    spans = [_piece_spans(widths, tk) for _, widths, _ in split]
    n_ab = sum(len(s) + 1 for s in spans)

    def body(*refs):
        ex = refs[n_ab:n_ab + 2] if extra is not None else ()
        h_ref, g_ref, r_ref, dh_ref, dhb_ref, dg_ref, acc_ref = refs[n_ab + len(ex):]
        i, kk = pl.program_id(0), pl.program_id(1)

        @pl.when((i == 0) & (kk == 0))
        def _():
            dg_ref[...] = jnp.zeros_like(dg_ref)

        @pl.when(kk == 0)
        def _():
            acc_ref[...] = _dot(ex[0][...], ex[1][...]) if ex else jnp.zeros_like(acc_ref)

        pos, whole = 0, []
        for s in spans:
            a_refs, b_ref = refs[pos:pos + len(s)], refs[pos + len(s)]
            pos += len(s) + 1
            if len(s) == 1:
                whole.append(_dot(a_refs[0][...], b_ref[...]))
                continue
            for a_ref, (lo, hi) in zip(a_refs, s):
                def add(a_ref=a_ref, b_ref=b_ref):
                    acc_ref[...] += _dot(a_ref[...], b_ref[...])
                pl.when((kk >= lo) & (kk < hi))(add)
        if whole:
            acc_ref[...] += functools.reduce(lambda x, y: x + y, whole)

        @pl.when(kk == nk - 1)
        def _():
            dh, dgr = _rms_bwd(h_ref[...], g_ref[...], acc_ref[...])
            dh = dh + r_ref[...]
            dh_ref[...] = dh
            dhb_ref[...] = _bf(dh)
            dg_ref[...] += jnp.sum(dgr, axis=0, keepdims=True)

    in_specs, args = [], []
    def a_spec(lo, hi, alone):
        if alone:
            return pl.BlockSpec((tm, tk), lambda i, kk: (i, kk))
        return pl.BlockSpec((tm, tk), lambda i, kk: (i, jnp.clip(kk - lo, 0, hi - lo - 1)))

    for (pieces, _, _), s, (_, b) in zip(split, spans, pairs):
        in_specs += [a_spec(lo, hi, len(s) == 1) for lo, hi in s] + [pl.BlockSpec((tk, n), lambda i, kk: (kk, 0))]
        args += [*pieces, b]
    if extra is not None:
        k2 = extra[0].shape[1]
        in_specs += [pl.BlockSpec((tm, k2), lambda i, kk: (i, 0)), pl.BlockSpec((k2, n), lambda i, kk: (0, 0))]
        args += list(extra)
    rspec = pl.BlockSpec((tm, n), lambda i, kk: (i, 0))
    in_specs += [rspec, pl.BlockSpec((1, n), lambda i, kk: (0, 0)), rspec]
    call = dict(grid=(m // tm, nk), in_specs=in_specs,
                out_specs=[rspec, rspec, pl.BlockSpec((1, n), lambda i, kk: (0, 0))],
                out_shape=[jax.ShapeDtypeStruct((m, n), F32), jax.ShapeDtypeStruct((m, n), BF16),
                           jax.ShapeDtypeStruct((1, n), F32)],
                scratch_shapes=[pltpu.VMEM((tm, n), F32)], name=name)
    if scatter is None:
        return pl.pallas_call(body, compiler_params=_params("arbitrary", "arbitrary"), **call)(*args, h, gain, dres)
    res = _call_with_exchange(body, args=(*args, h, gain, dres), srcs=scatter, scatter=[True] * len(scatter), **call)
    return [*res[:3], list(res[3:])]


HALO = 8


def _stage(buf_ref, before, cur):
    buf_ref[0:HALO, :] = before
    buf_ref[HALO:HALO + cur.shape[0], :] = cur


def _stage_after(buf_ref, cur, after):
    r = cur.shape[0]
    buf_ref[0:r, :] = cur
    buf_ref[r:r + HALO, :] = after


def _conv_taps(buf_ref, rows, w_ref):
    nk = w_ref.shape[0]
    out = buf_ref[HALO:HALO + rows, :] * w_ref[nk - 1:nk, :]
    for s in range(1, nk):
        out += buf_ref[HALO - s:HALO - s + rows, :] * w_ref[nk - 1 - s:nk - s, :]
    return out


def _gdn_pre(proj, pab, cq, ck, cv, a_log, dt_bias, *, heads, dk, dv, col_q, col_k, col_v, row_lo, row_hi, name):
    t = proj.shape[0]
    tm = _pick(t, (384, 256, 128))
    nb = t // tm

    def body(pq_ref, pqp_ref, pk_ref, pkp_ref, pv_ref, pvp_ref, ab_ref, cq_ref, ck_ref, cv_ref, al_ref, dt_ref,
             qn_ref, kn_ref, v_ref, g_ref, b_ref, bq_ref, bk_ref, bv_ref):
        h, i = pl.program_id(0), pl.program_id(1)
        first = i == 0
        row = i * tm + _iota((tm, 1), 0)
        valid = (row >= row_lo) & (row < row_hi)
        _stage(bq_ref, jnp.where(first, 0.0, pqp_ref[...]), pq_ref[...])
        _stage(bk_ref, jnp.where(first, 0.0, pkp_ref[...]), pk_ref[...])
        _stage(bv_ref, jnp.where(first, 0.0, pvp_ref[...]), pv_ref[...])
        q1 = _silu(_conv_taps(bq_ref, tm, cq_ref))
        k1 = _silu(_conv_taps(bk_ref, tm, ck_ref))
        v1 = _silu(_conv_taps(bv_ref, tm, cv_ref))
        qn_ref[...] = jnp.where(valid, q1 * lax.rsqrt(jnp.sum(q1 * q1, axis=-1, keepdims=True) + L2_EPS), 0.0)
        kn_ref[...] = jnp.where(valid, k1 * lax.rsqrt(jnp.sum(k1 * k1, axis=-1, keepdims=True) + L2_EPS), 0.0)
        v_ref[...] = jnp.where(valid, v1, 0.0)
        ab = ab_ref[...]
        da = _lane_pick(ab, h)
        db = _lane_pick(ab, heads + h)
        a = _lane_pick(al_ref[...], h)
        dtb = _lane_pick(dt_ref[...], h)
        g_ref[...] = jnp.where(valid, -jnp.exp(a) * _softplus(da + dtb), 0.0)
        b_ref[...] = jnp.where(valid, _sigmoid(db), 0.0)

    def cur(width, col):
        return pl.BlockSpec((tm, width), lambda h, i: (i, col // width + h))

    def prev(width, col):
        return pl.BlockSpec((8, width), lambda h, i: (jnp.maximum(i * (tm // 8) - 1, 0), col // width + h))

    def out(width):
        return pl.BlockSpec((None, tm, width), lambda h, i: (h, i, 0))

    small = pl.BlockSpec((1, LANES), lambda h, i: (0, 0))
    return pl.pallas_call(
        body, grid=(heads, nb),
        in_specs=[cur(dk, col_q), prev(dk, col_q), cur(dk, col_k), prev(dk, col_k), cur(dv, col_v), prev(dv, col_v),
                  pl.BlockSpec((tm, LANES), lambda h, i: (i, 0)),
                  pl.BlockSpec((cq.shape[0], dk), lambda h, i: (0, h)), pl.BlockSpec((ck.shape[0], dk), lambda h, i: (0, h)),
                  pl.BlockSpec((cv.shape[0], dv), lambda h, i: (0, h)), small, small],
        out_specs=[out(dk), out(dk), out(dv), out(1), out(1)],
        out_shape=[jax.ShapeDtypeStruct((heads, t, dk), F32), jax.ShapeDtypeStruct((heads, t, dk), F32),
                   jax.ShapeDtypeStruct((heads, t, dv), F32), jax.ShapeDtypeStruct((heads, t, 1), F32),
                   jax.ShapeDtypeStruct((heads, t, 1), F32)],
        scratch_shapes=[pltpu.VMEM((HALO + tm, dk), F32), pltpu.VMEM((HALO + tm, dk), F32),
                        pltpu.VMEM((HALO + tm, dv), F32)], name=name,
        compiler_params=_params("parallel", "parallel"))(proj, proj, proj, proj, proj, proj, pab, cq, ck, cv, a_log, dt_bias)


def _chunk_masks(rows=GDN_ROWS, row0=0):
    ri = row0 + _iota((rows, GDN_ROWS), 0)
    ci = _iota((rows, GDN_ROWS), 1)
    same = jnp.right_shift(ri, CHUNK_SHIFT) == jnp.right_shift(ci, CHUNK_SHIFT)
    return same, same & (ri >= ci), same & (ri > ci), ri == ci


def _col_to_row(col, eye):
    return jnp.sum(jnp.where(eye, col, 0.0), axis=0, keepdims=True)


def _row_to_col(row, eye):
    return jnp.sum(jnp.where(eye, row, 0.0), axis=1, keepdims=True)


def _chunk_common(blocks, dk_scale):
    same, incl, strict, eye = _chunk_masks()
    tri = jnp.where(incl, 1.0, 0.0).astype(BF16)
    tot = jnp.where(same, 1.0, 0.0).astype(BF16)
    gbs = [jnp.broadcast_to(g, (GDN_ROWS, LANES)) for _, _, g, _ in blocks]
    gams = [jnp.max(_dot_exact_l(tri, gb), axis=1, keepdims=True) for gb in gbs]
    lasts = [jnp.max(_dot_exact_l(tot, gb), axis=1, keepdims=True) for gb in gbs]
    kbs = [kn * beta for _, kn, _, beta in blocks]
    qts = [qn * dk_scale for qn, _, _, _ in blocks]
    boths = [_dot_nt(_bf(jnp.concatenate([kb, qt], axis=0)), _bf(blk[1]))
             for kb, qt, blk in zip(kbs, qts, blocks)]
    out = []
    for gam, last, kb, qt, both in zip(gams, lasts, kbs, qts, boths):
        diff = gam - _col_to_row(gam, eye)
        decay = jnp.where(incl, jnp.exp(jnp.where(incl, diff, 0.0)), 0.0)
        out.append(dict(incl=incl, strict=strict, eye=eye, decay=decay, eg=jnp.exp(gam), ek=jnp.exp(last - gam),
                        egl=jnp.exp(last), kb=kb, qt=qt, lmat=jnp.where(strict, both[:GDN_ROWS] * decay, 0.0),
                        pmat=jnp.where(incl, both[GDN_ROWS:] * decay, 0.0)))
    return out


def _gdn_prep(qn, kn, v, g, beta, *, name):
    heads, t, dk = qn.shape
    dv = v.shape[2]
    rows = _pick(t, (3 * GDN_ROWS, 2 * GDN_ROWS, GDN_ROWS))
    dk_scale = dk ** -0.5

    hp = PREP_HEADS

    def body(q_ref, k_ref, v_ref, g_ref, b_ref, u_ref, w_ref, p_ref, qd_ref, kd_ref, egl_ref, t_ref):
        rs = [(hh, pl.ds(b * GDN_ROWS, GDN_ROWS), slice(None)) for hh in range(hp) for b in range(rows // GDN_ROWS)]
        cs = _chunk_common([(q_ref[r], k_ref[r], g_ref[r], b_ref[r]) for r in rs], dk_scale)
        eye_f = jnp.where(cs[0]["eye"], 1.0, 0.0)
        tinvs = [eye_f - c["lmat"] for c in cs]
        ys = [_dot_hp(c["lmat"], c["lmat"]) for c in cs]
        for _ in range(CHUNK_SHIFT - 1):
            boths = [_dot_hp(y, jnp.concatenate([y, tinv], axis=1)) for y, tinv in zip(ys, tinvs)]
            ys = [both[:, :GDN_ROWS] for both in boths]
            tinvs = [tinv + both[:, GDN_ROWS:] for tinv, both in zip(tinvs, boths)]
        uws = [_dot_hp(tinv, jnp.concatenate([v_ref[r] * b_ref[r], c["kb"] * c["eg"]], axis=1))
               for r, c, tinv in zip(rs, cs, tinvs)]
        for r, c, tinv, uw in zip(rs, cs, tinvs, uws):
            u_ref[r] = uw[:, :dv]
            w_ref[r] = _bf(uw[:, dv:])
            p_ref[r] = _bf(c["pmat"])
            qd_ref[r] = _bf(c["qt"] * c["eg"])
            kd_ref[r] = _bf(k_ref[r] * c["ek"])
            egl_ref[r] = c["egl"]
            t_ref[r] = tinv

    def blk(width):
        return pl.BlockSpec((hp, rows, width), lambda h, i: (h, i, 0))

    def shp(width, dtype=F32):
        return jax.ShapeDtypeStruct((heads, t, width), dtype)

    return pl.pallas_call(
        body, grid=(heads // hp, t // rows), in_specs=[blk(dk), blk(dk), blk(dv), blk(1), blk(1)],
        out_specs=[blk(dv), blk(dk), blk(GDN_ROWS), blk(dk), blk(dk), blk(1), blk(GDN_ROWS)],
        out_shape=[shp(dv), shp(dk, BF16), shp(GDN_ROWS, BF16), shp(dk, BF16), shp(dk, BF16), shp(1), shp(GDN_ROWS)],
        name=name,
        compiler_params=_params("parallel", "parallel"))(qn, kn, v, g, beta)


def _gdn_scan(u, w, p, qd, kd, egl, proj, gain, *, col_z, name):
    heads, t, dv = u.shape
    dk = w.shape[2]
    nb = t // GDN_ROWS
    sub = GDN_ROWS // CHUNK
    hp = SCAN_HEADS

    def body(u_ref, w_ref, p_ref, qd_ref, kd_ref, egl_ref, z_ref, gn_ref, o_ref, og_ref, st_ref, s_ref):
        @pl.when(pl.program_id(1) == 0)
        def _():
            s_ref[...] = jnp.zeros_like(s_ref)

        hs = range(hp)
        vn_parts = [[jnp.zeros((CHUNK, dv), F32)] * sub for _ in hs]
        for c in range(sub):
            r = pl.ds(c * CHUNK, CHUNK)
            ss = [s_ref[hh] for hh in hs]
            sbs = [_bf(s) for s in ss]
            wss = [_dot(_bf(jnp.concatenate([w_ref[hh, r, :], qd_ref[hh, r, :]], axis=0)), sbs[hh])
                   for hh in hs]
            vns = [u_ref[hh, r, :] - wss[hh][:CHUNK] for hh in hs]
            for hh in hs:
                vn_parts[hh][c] = vns[hh]
            os_ = [wss[hh][CHUNK:] + _dot(_bf(p_ref[hh, r, :]), _bf(jnp.concatenate(vn_parts[hh], axis=0))) for hh in hs]
            new = [ss[hh] * egl_ref[hh, pl.ds(c * CHUNK, 1), :] + _dot_tn(_bf(kd_ref[hh, r, :]), _bf(vns[hh])) for hh in hs]
            for hh in hs:
                cols = pl.ds(hh * dv, dv)
                st_ref[hh, c] = sbs[hh]
                s_ref[hh] = new[hh]
                o_ref[hh, r, :] = os_[hh]
                og_ref[r, cols] = _bf(_rms_fwd(os_[hh], gn_ref[...]) * _silu(z_ref[r, cols]))

    def blk(width):
        return pl.BlockSpec((hp, GDN_ROWS, width), lambda h, i: (h, i, 0))

    return pl.pallas_call(
        body, grid=(heads // hp, nb),
        in_specs=[blk(dv), blk(dk), blk(GDN_ROWS), blk(dk), blk(dk), blk(1),
                  pl.BlockSpec((GDN_ROWS, hp * dv), lambda h, i: (i, col_z // (hp * dv) + h)),
                  pl.BlockSpec((1, dv), lambda h, i: (0, 0))],
        out_specs=[blk(dv), pl.BlockSpec((GDN_ROWS, hp * dv), lambda h, i: (i, h)),
                   pl.BlockSpec((hp, sub, dk, dv), lambda h, i: (h, i, 0, 0))],
        out_shape=[jax.ShapeDtypeStruct((heads, t, dv), F32), jax.ShapeDtypeStruct((t, heads * dv), BF16),
                   jax.ShapeDtypeStruct((heads, t // CHUNK, dk, dv), BF16)],
        scratch_shapes=[pltpu.VMEM((hp, dk, dv), F32)], name=name,
        compiler_params=_params("parallel", "arbitrary"))(u, w, p, qd, kd, egl, proj, gain)


def _gdn_post_bwd(o, proj, gain, dout, *, col_z, name):
    heads, t, dv = o.shape
    tm = _pick(t, (384, 256, 128))

    def body(o_ref, z_ref, gn_ref, d_ref, do_ref, dz_ref, dg_ref):
        @pl.when((pl.program_id(0) == 0) & (pl.program_id(1) == 0))
        def _():
            dg_ref[...] = jnp.zeros_like(dg_ref)

        o_, z, d = o_ref[...], z_ref[...], d_ref[...]
        y = _rms_fwd(o_, gn_ref[...])
        dz_ref[...] = _bf(d * y * _silu_grad(z))
        do, dgr = _rms_bwd(o_, gn_ref[...], d * _silu(z))
        do_ref[...] = do
        dg_ref[...] += jnp.sum(dgr, axis=0, keepdims=True)

    return pl.pallas_call(
        body, grid=(t // tm, heads),
        in_specs=[pl.BlockSpec((None, tm, dv), lambda i, h: (h, i, 0)),
                  pl.BlockSpec((tm, dv), lambda i, h: (i, col_z // dv + h)),
                  pl.BlockSpec((1, dv), lambda i, h: (0, 0)), pl.BlockSpec((tm, dv), lambda i, h: (i, h))],
        out_specs=[pl.BlockSpec((None, tm, dv), lambda i, h: (h, i, 0)), pl.BlockSpec((tm, dv), lambda i, h: (i, h)),
                   pl.BlockSpec((1, dv), lambda i, h: (0, 0))],
        out_shape=[jax.ShapeDtypeStruct((heads, t, dv), F32), jax.ShapeDtypeStruct((t, heads * dv), BF16),
                   jax.ShapeDtypeStruct((1, dv), F32)], name=name,
        compiler_params=_params("arbitrary", "arbitrary"))(o, proj, gain, dout)


def _gdn_bwd_scan(u, w, p, qd, kd, egl, st, do, *, name):
    heads, t, dv = u.shape
    dk = w.shape[2]
    nb = t // GDN_ROWS
    sub = GDN_ROWS // CHUNK
    hp = SCAN_HEADS

    def body(u_ref, w_ref, p_ref, qd_ref, kd_ref, egl_ref, st_ref, do_ref,
             du_ref, dw_ref, dp_ref, dqd_ref, dkd_ref, dgl_ref, ds_ref):
        @pl.when(pl.program_id(1) == 0)
        def _():
            ds_ref[...] = jnp.zeros_like(ds_ref)

        hs = range(hp)
        zeros = jnp.zeros((CHUNK, dv), BF16)
        for c in reversed(range(sub)):
            r = pl.ds(c * CHUNK, CHUNK)
            sbs = [st_ref[hh, c] for hh in hs]
            dss = [ds_ref[hh] for hh in hs]
            dsbs = [_bf(ds) for ds in dss]
            dobs = [_bf(do_ref[hh, r, :]) for hh in hs]
            wbs = [_bf(w_ref[hh, r, :]) for hh in hs]
            vns = [u_ref[hh, r, :] - _dot(wbs[hh], sbs[hh]) for hh in hs]
            dvns = [_dot_tn(_bf(p_ref[hh, r, :]), dobs[hh])[c * CHUNK:(c + 1) * CHUNK, :]
                    + _dot(_bf(kd_ref[hh, r, :]), dsbs[hh]) for hh in hs]
            dods = [jnp.concatenate([dobs[hh], _bf(dvns[hh])], axis=0) for hh in hs]
            boths = [_dot_nt(dods[hh], sbs[hh]) for hh in hs]
            dps = [_dot_nt(dobs[hh], jnp.concatenate([_bf(vns[hh]) if cc == c else zeros for cc in range(sub)], axis=0))
                   for hh in hs]
            dkds = [_dot_nt(_bf(vns[hh]), dsbs[hh]) for hh in hs]
            new = [dss[hh] * egl_ref[hh, pl.ds(c * CHUNK, 1), :]
                   + _dot_tn(jnp.concatenate([_bf(qd_ref[hh, r, :]), -wbs[hh]], axis=0), dods[hh])
                   for hh in hs]
            for hh in hs:
                du_ref[hh, r, :] = dvns[hh]
                dw_ref[hh, r, :] = -boths[hh][CHUNK:]
                dp_ref[hh, r, :] = jnp.where(_chunk_masks(CHUNK, c * CHUNK)[1], dps[hh], 0.0)
                dqd_ref[hh, r, :] = boths[hh][:CHUNK]
                dkd_ref[hh, r, :] = dkds[hh]
                dgl = jnp.sum(jnp.sum(dss[hh] * sbs[hh].astype(F32), axis=1, keepdims=True), axis=0, keepdims=True)
                dgl_ref[hh, r, :] = jnp.where(_iota((CHUNK, 1), 0) == CHUNK - 1, dgl, 0.0)
                ds_ref[hh] = new[hh]

    def blk(width):
        return pl.BlockSpec((hp, GDN_ROWS, width), lambda h, i: (h, nb - 1 - i, 0))

    def shp(width):
        return jax.ShapeDtypeStruct((heads, t, width), F32)

    return pl.pallas_call(
        body, grid=(heads // hp, nb),
        in_specs=[blk(dv), blk(dk), blk(GDN_ROWS), blk(dk), blk(dk), blk(1),
                  pl.BlockSpec((hp, sub, dk, dv), lambda h, i: (h, nb - 1 - i, 0, 0)), blk(dv)],
        out_specs=[blk(dv), blk(dk), blk(GDN_ROWS), blk(dk), blk(dk), blk(1)],
        out_shape=[shp(dv), shp(dk), shp(GDN_ROWS), shp(dk), shp(dk), shp(1)],
        scratch_shapes=[pltpu.VMEM((hp, dk, dv), F32)], name=name,
        compiler_params=_params("parallel", "arbitrary"))(u, w, p, qd, kd, egl, st, do)


def _gdn_bwd_prep(qn, kn, v, g, beta, tinv, u, w, du, dw, dp, dqd, dkd, dgl, *, name):
    heads, t, dk = qn.shape
    dv = v.shape[2]
    rows = _pick(t, (3 * GDN_ROWS, 2 * GDN_ROWS, GDN_ROWS))
    dk_scale = dk ** -0.5
    hp = PREP_HEADS

    def rowsum(x):
        return jnp.sum(x, axis=1, keepdims=True)

    def body(q_ref, k_ref, v_ref, g_ref, b_ref, t_ref, u_ref, w_ref, du_ref, dw_ref, dp_ref, dqd_ref, dkd_ref, dgl_ref,
             dq_ref, dkk_ref, dvv_ref, dg_ref, db_ref):
        rs = [(hh, pl.ds(b * GDN_ROWS, GDN_ROWS), slice(None)) for hh in range(hp) for b in range(rows // GDN_ROWS)]
        cs = _chunk_common([(q_ref[r], k_ref[r], g_ref[r], b_ref[r]) for r in rs], dk_scale)
        dbvws = [_dot_hp(t_ref[r], jnp.concatenate([du_ref[r], dw_ref[r]], axis=1), _dot_tn)
                 for r in rs]
        das = [-_dot_nt(_bf(dbvw), jnp.concatenate([_bf(u_ref[r]), w_ref[r]], axis=1))
               for r, dbvw in zip(rs, dbvws)]
        dls = [jnp.where(c["strict"], da, 0.0) for c, da in zip(cs, das)]
        dmns = [_bf(jnp.concatenate([dl * c["decay"], dp_ref[r] * c["decay"]], axis=0))
                for r, c, dl in zip(rs, cs, dls)]
        boths = [_dot(dmn, _bf(k_ref[r])) for r, dmn in zip(rs, dmns)]
        dkns = [_dot_tn(dmn, _bf(jnp.concatenate([c["kb"], c["qt"]], axis=0)))
                for c, dmn in zip(cs, dmns)]
        for r, c, dbvw, dl, both, dkn in zip(rs, cs, dbvws, dls, boths, dkns):
            kn_, beta_, v_ = k_ref[r], b_ref[r], v_ref[r]
            eye = c["eye"]
            kb, qt, eg, ek = c["kb"], c["qt"], c["eg"], c["ek"]
            dbv, dbw = dbvw[:, :dv], dbvw[:, dv:]
            dp_ = dp_ref[r]
            dkb = both[:GDN_ROWS] + dbw * eg
            dqt = both[GDN_ROWS:]
            gmat = dl * c["lmat"] + dp_ * c["pmat"]
            dqd_, dkd_ = dqd_ref[r], dkd_ref[r]
            qd = qt * eg
            kd = kn_ * ek
            bw = kb * eg
            kdsum = rowsum(dkd_ * kd)
            dgam = rowsum(gmat) - _row_to_col(jnp.sum(gmat, axis=0, keepdims=True), eye)
            dgam += rowsum(dbw * bw) + rowsum(dqd_ * qd) - kdsum
            last = (_iota((GDN_ROWS, 1), 0) & (CHUNK - 1)) == CHUNK - 1
            same = _chunk_masks()[0]
            same_f = jnp.where(same, 1.0, 0.0).astype(BF16)
            chunk_tot = jnp.max(_dot_exact_l(same_f, jnp.broadcast_to(kdsum, (GDN_ROWS, LANES))), axis=1, keepdims=True)
            dgam += jnp.where(last, chunk_tot, 0.0) + dgl_ref[r] * c["egl"]
            dq_ref[r] = (dqt + dqd_ * eg) * dk_scale
            dkk_ref[r] = dkn + dkd_ * ek + dkb * beta_
            dvv_ref[r] = dbv * beta_
            db_ref[r] = rowsum(dbv * v_) + rowsum(dkb * kn_)
            upper = jnp.where(same & (_iota((GDN_ROWS, GDN_ROWS), 0) <= _iota((GDN_ROWS, GDN_ROWS), 1)), 1.0, 0.0)
            dgb = _dot_exact_l(upper.astype(BF16), jnp.broadcast_to(dgam, (GDN_ROWS, LANES)))
            dg_ref[r] = _lane_pick(dgb, 0)

    def blk(width):
        return pl.BlockSpec((hp, rows, width), lambda h, i: (h, i, 0))

    def shp(width):
        return jax.ShapeDtypeStruct((heads, t, width), F32)

    return pl.pallas_call(
        body, grid=(heads // hp, t // rows),
        in_specs=[blk(dk), blk(dk), blk(dv), blk(1), blk(1), blk(GDN_ROWS), blk(dv), blk(dk),
                  blk(dv), blk(dk), blk(GDN_ROWS), blk(dk), blk(dk), blk(1)],
        out_specs=[blk(dk), blk(dk), blk(dv), blk(1), blk(1)],
        out_shape=[shp(dk), shp(dk), shp(dv), shp(1), shp(1)], name=name,
        compiler_params=_params("parallel", "parallel"))(qn, kn, v, g, beta, tinv, u, w, du, dw, dp, dqd, dkd, dgl)


def _gdn_pre_bwd_a(proj, pab, cq, ck, cv, a_log, dt_bias, dqn, dkn, dvv, dg, dbeta, *,
                   heads, dk, dv, col_q, col_k, col_v, row_lo, row_hi, name):
    t = proj.shape[0]
    tm = _pick(t, (384, 256, 128))
    nb = t // tm

    def body(pq_ref, pqp_ref, pk_ref, pkp_ref, pv_ref, pvp_ref, ab_ref, cq_ref, ck_ref, cv_ref, al_ref, dt_ref,
             dqn_ref, dkn_ref, dvv_ref, dg_ref, db_ref, dcq_ref, dck_ref, dcv_ref, dab_ref, dal_ref, ddt_ref,
             bq_ref, bk_ref, bv_ref):
        i, h = pl.program_id(0), pl.program_id(1)
        first = i == 0

        @pl.when((i == 0) & (h == 0))
        def _():
            dal_ref[...] = jnp.zeros_like(dal_ref)
            ddt_ref[...] = jnp.zeros_like(ddt_ref)

        @pl.when(h == 0)
        def _():
            dab_ref[...] = jnp.zeros_like(dab_ref)

        row = i * tm + _iota((tm, 1), 0)
        valid = (row >= row_lo) & (row < row_hi)

        def l2_bwd(c1, dn):
            x1 = _silu(c1)
            r = lax.rsqrt(jnp.sum(x1 * x1, axis=-1, keepdims=True) + L2_EPS)
            dn = jnp.where(valid, dn, 0.0)
            d1 = r * dn - x1 * (r * r * r) * jnp.sum(dn * x1, axis=-1, keepdims=True)
            return d1 * _silu_grad(c1)

        _stage(bq_ref, jnp.where(first, 0.0, pqp_ref[...]), pq_ref[...])
        _stage(bk_ref, jnp.where(first, 0.0, pkp_ref[...]), pk_ref[...])
        _stage(bv_ref, jnp.where(first, 0.0, pvp_ref[...]), pv_ref[...])
        dcq_ref[...] = l2_bwd(_conv_taps(bq_ref, tm, cq_ref), dqn_ref[...])
        dck_ref[...] = l2_bwd(_conv_taps(bk_ref, tm, ck_ref), dkn_ref[...])
        cv1 = _conv_taps(bv_ref, tm, cv_ref)
        dcv_ref[...] = jnp.where(valid, dvv_ref[...], 0.0) * _silu_grad(cv1)
        ab = ab_ref[...]
        da = _lane_pick(ab, h)
        db = _lane_pick(ab, heads + h)
        a = _lane_pick(al_ref[...], h)
        dtb = _lane_pick(dt_ref[...], h)
        dgv = jnp.where(valid, dg_ref[...], 0.0)
        ea = jnp.exp(a)
        g = -ea * _softplus(da + dtb)
        dda = dgv * (-ea) * _sigmoid(da + dtb)
        beta = _sigmoid(db)
        ddb = jnp.where(valid, db_ref[...], 0.0) * beta * (1.0 - beta)
        lane = _iota((tm, LANES), 1)
        dab_ref[...] += jnp.where(lane == h, dda, 0.0) + jnp.where(lane == heads + h, ddb, 0.0)
        lane1 = _iota((1, LANES), 1)
        dal_ref[...] += jnp.where(lane1 == h, jnp.sum(dgv * g, axis=0, keepdims=True), 0.0)
        ddt_ref[...] += jnp.where(lane1 == h, jnp.sum(dda, axis=0, keepdims=True), 0.0)

    def cur(width, col):
        return pl.BlockSpec((tm, width), lambda i, h: (i, col // width + h))

    def prev(width, col):
        return pl.BlockSpec((8, width), lambda i, h: (jnp.maximum(i * (tm // 8) - 1, 0), col // width + h))

    def hd(width):
        return pl.BlockSpec((None, tm, width), lambda i, h: (h, i, 0))

    small = pl.BlockSpec((1, LANES), lambda i, h: (0, 0))
    return pl.pallas_call(
        body, grid=(nb, heads),
        in_specs=[cur(dk, col_q), prev(dk, col_q), cur(dk, col_k), prev(dk, col_k), cur(dv, col_v), prev(dv, col_v),
                  pl.BlockSpec((tm, LANES), lambda i, h: (i, 0)),
                  pl.BlockSpec((cq.shape[0], dk), lambda i, h: (0, h)), pl.BlockSpec((ck.shape[0], dk), lambda i, h: (0, h)),
                  pl.BlockSpec((cv.shape[0], dv), lambda i, h: (0, h)), small, small,
                  hd(dk), hd(dk), hd(dv), hd(1), hd(1)],
        out_specs=[hd(dk), hd(dk), hd(dv), pl.BlockSpec((tm, LANES), lambda i, h: (i, 0)), small, small],
        out_shape=[jax.ShapeDtypeStruct((heads, t, dk), F32), jax.ShapeDtypeStruct((heads, t, dk), F32),
                   jax.ShapeDtypeStruct((heads, t, dv), F32), jax.ShapeDtypeStruct((t, LANES), F32),
                   jax.ShapeDtypeStruct((1, LANES), F32), jax.ShapeDtypeStruct((1, LANES), F32)],
        scratch_shapes=[pltpu.VMEM((HALO + tm, dk), F32), pltpu.VMEM((HALO + tm, dk), F32),
                        pltpu.VMEM((HALO + tm, dv), F32)], name=name,
        compiler_params=_params("arbitrary", "arbitrary"))(
            proj, proj, proj, proj, proj, proj, pab, cq, ck, cv, a_log, dt_bias, dqn, dkn, dvv, dg, dbeta)


def _conv_bwd(proj, dc, cw, *, heads, width, col, name):
    t = proj.shape[0]
    tm = _pick(t, (384, 256, 128))
    nb = t // tm
    nk = cw.shape[0]

    def body(p_ref, pp_ref, d_ref, dn_ref, w_ref, dp_ref, dw_ref, bx_ref, bd_ref):
        i = pl.program_id(1)
        first, last = i == 0, i == nb - 1

        @pl.when(first)
        def _():
            dw_ref[...] = jnp.zeros_like(dw_ref)

        d = d_ref[...]
        _stage(bx_ref, jnp.where(first, 0.0, pp_ref[...]), p_ref[...])
        _stage_after(bd_ref, d, jnp.where(last, 0.0, dn_ref[...]))
        dx = d * w_ref[nk - 1:nk, :]
        dw_ref[nk - 1:nk, :] += jnp.sum(d * p_ref[...], axis=0, keepdims=True)
        for s in range(1, nk):
            dx += bd_ref[s:s + tm, :] * w_ref[nk - 1 - s:nk - s, :]
            dw_ref[nk - 1 - s:nk - s, :] += jnp.sum(d * bx_ref[HALO - s:HALO - s + tm, :], axis=0, keepdims=True)
        dp_ref[...] = _bf(dx)

    return pl.pallas_call(
        body, grid=(heads, nb),
        in_specs=[pl.BlockSpec((tm, width), lambda h, i: (i, col // width + h)),
                  pl.BlockSpec((8, width), lambda h, i: (jnp.maximum(i * (tm // 8) - 1, 0), col // width + h)),
                  pl.BlockSpec((None, tm, width), lambda h, i: (h, i, 0)),
                  pl.BlockSpec((None, 8, width), lambda h, i: (h, jnp.minimum((i + 1) * (tm // 8), t // 8 - 1), 0)),
                  pl.BlockSpec((nk, width), lambda h, i: (0, h))],
        out_specs=[pl.BlockSpec((tm, width), lambda h, i: (i, h)), pl.BlockSpec((nk, width), lambda h, i: (0, h))],
        out_shape=[jax.ShapeDtypeStruct((t, heads * width), BF16), jax.ShapeDtypeStruct((nk, heads * width), F32)],
        scratch_shapes=[pltpu.VMEM((HALO + tm, width), F32), pltpu.VMEM((tm + HALO, width), F32)],
        name=name, compiler_params=_params("parallel", "arbitrary"))(proj, proj, dc, dc, cw)


def _sb_pre(proj, gq, gk, *, heads, dh, col_q, col_k, col_v, name):
    t = proj.shape[0]
    tm = _pick(t, (384, 256, 128))

    def body(q_ref, k_ref, v_ref, gq_ref, gk_ref, qo_ref, ko_ref, vo_ref):
        qo_ref[...] = _bf(_rms_fwd(q_ref[...], gq_ref[...]))
        ko_ref[...] = _bf(_rms_fwd(k_ref[...], gk_ref[...]))
        vo_ref[...] = _bf(v_ref[...])

    def cur(col):
        return pl.BlockSpec((tm, dh), lambda i, h: (i, col // dh + h))

    gspec = pl.BlockSpec((1, dh), lambda i, h: (0, 0))
    ospec = pl.BlockSpec((tm, dh), lambda i, h: (i, h))
    return pl.pallas_call(
        body, grid=(t // tm, heads), in_specs=[cur(col_q), cur(col_k), cur(col_v), gspec, gspec],
        out_specs=[ospec] * 3, out_shape=[jax.ShapeDtypeStruct((t, heads * dh), BF16)] * 3, name=name,
        compiler_params=_params("parallel", "parallel"))(proj, proj, proj, gq, gk)


def _sb_tile(z, i, j, blk, key_lo, masked):
    ls = jnp.minimum(z, 0.0) - jnp.log(1.0 + jnp.exp(-jnp.abs(z)))
    if not masked:
        return None, ls, ls - z
    qpos = i * blk + _iota((blk, blk), 0)
    kpos = j * blk + _iota((blk, blk), 1)
    vis = (kpos < qpos) & (kpos >= key_lo)
    return vis, ls, jnp.where(vis, ls - z, 0.0)


def _where_vis(vis, x):
    return x if vis is None else jnp.where(vis, x, 0.0)


def _sb_sweep(i, step, init, descending):
    first, last = (i, 0) if descending else (0, i)
    carry = step(first, init, True)
    carry = lax.fori_loop(1, i, lambda n, c: step(i - n if descending else n, c, False), carry)
    return lax.cond(i > 0, lambda c: step(last, c, True), lambda c: c, carry)


def _dot2_r(x, m):
    hi, lo = _split2(x)
    return _dot(hi, m) + _dot(lo, m)


def _running_sums(x, tri, reverse, exact=True):
    groups = [x[:, s:s + LANES] for s in range(0, x.shape[1], LANES)]
    inside = [_dot2_r(g, tri) if exact else _dot(_bf(g), tri) for g in groups]
    sums = [jnp.sum(g, axis=1, keepdims=True) for g in groups]
    order = list(reversed(range(len(groups)))) if reverse else list(range(len(groups)))
    out, acc = [None] * len(groups), None
    for gi in order:
        out[gi] = inside[gi] if acc is None else inside[gi] + acc
        acc = sums[gi] if acc is None else acc + sums[gi]
    return jnp.concatenate(out, axis=1), acc


def _sb_fwd(qs, ks, vs, *, heads, dh, key_lo, name, gather=None):
    t = qs.shape[0]
    blk = _pick(t, (3 * SB_BLOCK, 2 * SB_BLOCK, SB_BLOCK))
    assert key_lo <= blk
    nq = t // blk
    assert nq <= LANES
    scale = dh ** -0.5
    hp = SB_HEADS_PER_STEP

    def body(q_ref, k_ref, v_ref, o_ref, c_ref):
        i = pl.program_id(1)
        later = jnp.where(_iota((LANES, LANES), 0) > _iota((LANES, LANES), 1), 1.0, 0.0).astype(BF16)
        lane = _iota((blk, LANES), 1)
        c_ref[...] = jnp.zeros_like(c_ref)

        def step(j, carry, masked):
            rows = pl.ds(pl.multiple_of(j * blk, blk), blk)
            hs = range(hp)
            cols = [pl.ds(hh * dh, dh) for hh in hs]
            zs = [_dot_nt(q_ref[:, cols[hh]], k_ref[rows, cols[hh]]) * scale for hh in hs]
            tiles = [_sb_tile(z, i, j, blk, key_lo, masked) for z in zs]
            sufs = [_running_sums(lk, later, reverse=True) for _, _, lk in tiles]
            wgts = [_where_vis(vis, jnp.exp(ls + suf + carry[2 * hh + 1]))
                    for hh, ((vis, ls, _), (suf, _)) in enumerate(zip(tiles, sufs))]
            accs = [carry[2 * hh] + _dot(_bf(wgts[hh]), v_ref[rows, cols[hh]]) for hh in hs]
            out = []
            for hh in hs:
                c_ref[hh] = jnp.where(lane == j, carry[2 * hh + 1], c_ref[hh])
                out += [accs[hh], carry[2 * hh + 1] + sufs[hh][1]]
            return tuple(out)

        res = _sb_sweep(i, step, (jnp.zeros((blk, dh), F32), jnp.zeros((blk, 1), F32)) * hp, descending=True)
        for hh in range(hp):
            o_ref[:, pl.ds(hh * dh, dh)] = _bf(res[2 * hh])

    full = pl.BlockSpec((t, hp * dh), lambda h, i: (0, h))
    call = dict(grid=(heads // hp, nq),
                in_specs=[pl.BlockSpec((blk, hp * dh), lambda h, i: (i, h)), full, full],
                out_specs=[pl.BlockSpec((blk, hp * dh), lambda h, i: (i, h)),
                           pl.BlockSpec((hp, blk, LANES), lambda h, i: (h, i, 0))],
                out_shape=[jax.ShapeDtypeStruct((t, heads * dh), BF16), jax.ShapeDtypeStruct((heads, t, LANES), F32)],
                name=name)
    if gather is None:
        return pl.pallas_call(body, compiler_params=_params("parallel", "parallel"), **call)(qs, ks, vs)
    res = _call_with_exchange(body, scratch_shapes=[], args=(qs, ks, vs), srcs=gather, scatter=[False] * len(gather),
                              **call)
    return [*res[:2], list(res[2:])]


def _sb_bwd(qs, ks, vs, do, carry, *, heads, dh, key_lo, name, scatter=None):
    t = qs.shape[0]
    blk = _pick(t, (3 * SB_BLOCK, 2 * SB_BLOCK, SB_BLOCK))
    assert key_lo <= blk
    nq = t // blk
    scale = dh ** -0.5
    hp = SB_HEADS_PER_STEP

    def body(q_ref, k_ref, v_ref, do_ref, c_ref, dq_ref, dk_ref, dv_ref):
        i = pl.program_id(1)

        @pl.when(i == 0)
        def _():
            dk_ref[...] = jnp.zeros_like(dk_ref)
            dv_ref[...] = jnp.zeros_like(dv_ref)

        r0 = _iota((LANES, LANES), 0)
        r1 = _iota((LANES, LANES), 1)
        later = jnp.where(r0 > r1, 1.0, 0.0).astype(BF16)
        earlier = jnp.where(r0 < r1, 1.0, 0.0).astype(BF16)

        def step(j, carry, masked):
            rows = pl.ds(pl.multiple_of(j * blk, blk), blk)
            hs = range(hp)
            cols = [pl.ds(hh * dh, dh) for hh in hs]
            zs = [_dot_nt(q_ref[:, cols[hh]], k_ref[rows, cols[hh]]) * scale for hh in hs]
            dws = [_dot_nt(do_ref[:, cols[hh]], v_ref[rows, cols[hh]]) for hh in hs]
            tiles = [_sb_tile(z, i, j, blk, key_lo, masked) for z in zs]
            sufs = [_running_sums(lk, later, reverse=True)[0] for _, _, lk in tiles]
            wgts = [_where_vis(vis, jnp.exp(ls + suf + _lane_pick(c_ref[hh], j)))
                    for hh, ((vis, ls, _), suf) in enumerate(zip(tiles, sufs))]
            es = [wgt * dw for wgt, dw in zip(wgts, dws)]
            pres = [_running_sums(e, earlier, reverse=False, exact=False) for e in es]
            dzs = []
            for hh in hs:
                vis, ls, _ = tiles[hh]
                before = _where_vis(vis, pres[hh][0] + carry[2 * hh + 1])
                sig = jnp.exp(ls)
                dzs.append(_bf((es[hh] * (1.0 - sig) - before * sig) * scale))
            dks = [_dot_tn(dzs[hh], q_ref[:, cols[hh]]) for hh in hs]
            dvs = [_dot_tn(_bf(wgts[hh]), do_ref[:, cols[hh]]) for hh in hs]
            dqs = [carry[2 * hh] + _dot(dzs[hh], k_ref[rows, cols[hh]]) for hh in hs]
            out = []
            for hh in hs:
                dk_ref[rows, cols[hh]] += dks[hh]
                dv_ref[rows, cols[hh]] += dvs[hh]
                out += [dqs[hh], carry[2 * hh + 1] + pres[hh][1]]
            return tuple(out)

        res = _sb_sweep(i, step, (jnp.zeros((blk, dh), F32), jnp.zeros((blk, 1), F32)) * hp, descending=False)
        for hh in range(hp):
            dq_ref[:, pl.ds(hh * dh, dh)] = res[2 * hh]

    full = pl.BlockSpec((t, hp * dh), lambda h, i: (0, h))
    qblk = pl.BlockSpec((blk, hp * dh), lambda h, i: (i, h))
    call = dict(grid=(heads // hp, nq),
                in_specs=[qblk, full, full, qblk, pl.BlockSpec((hp, blk, LANES), lambda h, i: (h, i, 0))],
                out_specs=[qblk, full, full], out_shape=[jax.ShapeDtypeStruct((t, heads * dh), F32)] * 3, name=name)
    if scatter is None:
        return pl.pallas_call(body, compiler_params=_params("parallel", "arbitrary"), **call)(qs, ks, vs, do, carry)
    res = _call_with_exchange(body, scratch_shapes=[], args=(qs, ks, vs, do, carry), srcs=scatter,
                              scatter=[True] * len(scatter), **call)
    return [*res[:3], list(res[3:])]


def _sb_pre_bwd(proj, gq, gk, dq, dk, dv, *, heads, dh, col_q, col_k, name):
    t = proj.shape[0]
    tm = _pick(t, (384, 256, 128))

    def body(q_ref, k_ref, gq_ref, gk_ref, dq_ref, dk_ref, dv_ref, oq_ref, ok_ref, ov_ref, dgq_ref, dgk_ref):
        @pl.when((pl.program_id(0) == 0) & (pl.program_id(1) == 0))
        def _():
            dgq_ref[...] = jnp.zeros_like(dgq_ref)
            dgk_ref[...] = jnp.zeros_like(dgk_ref)

        dq_, gq_r = _rms_bwd(q_ref[...], gq_ref[...], dq_ref[...])
        dk_, gk_r = _rms_bwd(k_ref[...], gk_ref[...], dk_ref[...])
        oq_ref[...] = _bf(dq_)
        ok_ref[...] = _bf(dk_)
        ov_ref[...] = _bf(dv_ref[...])
        dgq_ref[...] += jnp.sum(gq_r, axis=0, keepdims=True)
        dgk_ref[...] += jnp.sum(gk_r, axis=0, keepdims=True)

    def cur(col):
        return pl.BlockSpec((tm, dh), lambda i, h: (i, col // dh + h))

    gspec = pl.BlockSpec((1, dh), lambda i, h: (0, 0))
    ospec = pl.BlockSpec((tm, dh), lambda i, h: (i, h))
    return pl.pallas_call(
        body, grid=(t // tm, heads), in_specs=[cur(col_q), cur(col_k), gspec, gspec, ospec, ospec, ospec],
        out_specs=[ospec, ospec, ospec, gspec, gspec],
        out_shape=[jax.ShapeDtypeStruct((t, heads * dh), BF16)] * 3 + [jax.ShapeDtypeStruct((1, dh), F32)] * 2,
        name=name, compiler_params=_params("arbitrary", "arbitrary"))(proj, proj, gq, gk, dq, dk, dv)


PEERS = N_DEV - 1


def _exchange_copies(ins, outs, send_sems, recv_sems, local_sems, scatter):
    x, y, c = lax.axis_index("x"), lax.axis_index("y"), lax.axis_index("c")
    me = 4 * x + 2 * y + c
    copies = []
    for a in range(len(ins)):
        own = ins[a].at[me] if scatter[a] else ins[a]
        copies.append(pltpu.make_async_copy(own, outs[a].at[me], local_sems.at[a]))
        for k in range(1, N_DEV):
            px = (x + (k >> 2 & 1)) % 2
            py = (y + (k >> 1 & 1)) % 2
            pc = (c + (k & 1)) % 2
            src = ins[a].at[4 * px + 2 * py + pc] if scatter[a] else ins[a]
            copies.append(pltpu.make_async_remote_copy(
                src_ref=src, dst_ref=outs[a].at[me], send_sem=send_sems.at[a * PEERS + k - 1],
                recv_sem=recv_sems.at[a * PEERS + k - 1], device_id=(px, py, pc), device_id_type=MESH))
    return copies


def _exchange_shapes(srcs, scatter):
    return [jax.ShapeDtypeStruct(s.shape if sc else (N_DEV,) + s.shape, s.dtype) for s, sc in zip(srcs, scatter)]


def _exchange_sems(n):
    return [pltpu.SemaphoreType.DMA((n * PEERS,)), pltpu.SemaphoreType.DMA((n * PEERS,)), pltpu.SemaphoreType.DMA((n,))]


def _exchange(srcs, *, scatter, name):
    n = len(srcs)

    def body(*refs):
        copies = _exchange_copies(refs[:n], refs[n:2 * n], *refs[2 * n:], scatter)
        for cp in copies:
            cp.start()
        for cp in copies:
            cp.wait()

    any_spec = pl.BlockSpec(memory_space=pl.ANY)
    return pl.pallas_call(
        body, in_specs=[any_spec] * n, out_specs=[any_spec] * n, out_shape=_exchange_shapes(srcs, scatter),
        scratch_shapes=_exchange_sems(n), name=name,
        compiler_params=pltpu.CompilerParams(has_side_effects=True))(*srcs)


def _gather_two_level(srcs, *, name):
    n = len(srcs)

    def body(*refs):
        ins, outs = refs[:n], refs[n:2 * n]
        send_sems, recv_sems, local_sems = refs[2 * n:]
        x, y, c = lax.axis_index("x"), lax.axis_index("y"), lax.axis_index("c")
        chips = [(1 - x, y), (x, 1 - y), (1 - x, 1 - y)]

        def slab(a, px, py, pc):
            return outs[a].at[4 * px + 2 * py + pc]

        def copy(a, k, block, to, src=None):
            return pltpu.make_async_remote_copy(
                src_ref=slab(a, *block) if src is None else src, dst_ref=slab(a, *block),
                send_sem=send_sems.at[a * PEERS + k], recv_sem=recv_sems.at[a * PEERS + k],
                device_id=to, device_id_type=MESH)

        mine = [pltpu.make_async_copy(ins[a], slab(a, x, y, c), local_sems.at[a]) for a in range(n)]
        first = [copy(a, 0, (x, y, c), (x, y, 1 - c), src=ins[a]) for a in range(n)]
        first += [copy(a, 1 + j, (x, y, c), (*chip, c), src=ins[a]) for j, chip in enumerate(chips) for a in range(n)]
        for cp in mine + first:
            cp.start()
        passed = []
        for j, chip in enumerate(chips):
            for a in range(n):
                copy(a, 1 + j, (*chip, c), (x, y, c)).wait_recv()
                passed.append(copy(a, 4 + j, (*chip, c), (x, y, 1 - c)))
                passed[-1].start()
        for a in range(n):
            copy(a, 0, (x, y, 1 - c), (x, y, c)).wait_recv()
            for j, chip in enumerate(chips):
                copy(a, 4 + j, (*chip, 1 - c), (x, y, c)).wait_recv()
        for cp in first + passed:
            cp.wait_send()
        for cp in mine:
            cp.wait()

    any_spec = pl.BlockSpec(memory_space=pl.ANY)
    return pl.pallas_call(
        body, in_specs=[any_spec] * n, out_specs=[any_spec] * n, out_shape=_exchange_shapes(srcs, [False] * n),
        scratch_shapes=_exchange_sems(n), name=name,
        compiler_params=pltpu.CompilerParams(has_side_effects=True))(*srcs)


def _call_with_exchange(body, *, grid, in_specs, out_specs, out_shape, scratch_shapes, args, srcs, scatter, name):
    n, n_in, n_out, n_scr = len(srcs), len(args), len(out_shape), len(scratch_shapes)

    def full_body(*refs):
        ins, xin = refs[:n_in], refs[n_in:n_in + n]
        outs, xout = refs[n_in + n:n_in + n + n_out], refs[n_in + n + n_out:n_in + 2 * n + n_out]
        scr = refs[n_in + 2 * n + n_out:]
        ids = [pl.program_id(a) for a in range(len(grid))]
        first = functools.reduce(jnp.logical_and, [i == 0 for i in ids])
        last = functools.reduce(jnp.logical_and, [i == g - 1 for i, g in zip(ids, grid)])
        copies = _exchange_copies(xin, xout, *scr[n_scr:], scatter)

        @pl.when(first)
        def _():
            for cp in copies:
                cp.start()

        body(*ins, *outs, *scr[:n_scr])

        @pl.when(last)
        def _():
            for cp in copies:
                cp.wait()

    any_spec = pl.BlockSpec(memory_space=pl.ANY)
    return pl.pallas_call(
        full_body, grid=grid, in_specs=list(in_specs) + [any_spec] * n, out_specs=list(out_specs) + [any_spec] * n,
        out_shape=list(out_shape) + _exchange_shapes(srcs, scatter),
        scratch_shapes=list(scratch_shapes) + _exchange_sems(n), name=name,
        compiler_params=pltpu.CompilerParams(dimension_semantics=("arbitrary",) * len(grid),
                                             vmem_limit_bytes=V7X_VMEM_LIMIT_BYTES, has_side_effects=True))(*args, *srcs)


def _adam_math(g, w, m, v):
    m2 = ADAM_B1 * m + (1.0 - ADAM_B1) * g
    v2 = ADAM_B2 * v + (1.0 - ADAM_B2) * (g * g)
    m_hat = m2 / (1.0 - ADAM_B1 ** ADAM_STEP)
    v_hat = v2 / (1.0 - ADAM_B2 ** ADAM_STEP)
    return -ADAM_LR * (m_hat / (jnp.sqrt(v_hat) + ADAM_EPS) + ADAM_WD * w), m2, v2


def _adamw_slabs(slabs, w, m, v, *, name):
    r, c = w.shape
    tr = next((t for t in (2256, 752, 512, 240, 176, 128, 64, 32, 16) if r % t == 0), r)
    tc = c if r % 16 == 0 else _pick(c, (2 * LANES, LANES))

    def body(s_ref, w_ref, m_ref, v_ref, g_ref, d_ref, mo_ref, vo_ref):
        g = s_ref[0].astype(F32)
        for p in range(1, N_DEV):
            g = g + s_ref[p].astype(F32)
        g_ref[...] = g
        d_ref[...], mo_ref[...], vo_ref[...] = _adam_math(g, w_ref[...], m_ref[...], v_ref[...])

    spec = pl.BlockSpec((tr, tc), lambda i, j: (i, j))
    return pl.pallas_call(
        body, grid=(r // tr, c // tc), in_specs=[pl.BlockSpec((N_DEV, tr, tc), lambda i, j: (0, i, j)), spec, spec, spec],
        out_specs=[spec] * 4, out_shape=[jax.ShapeDtypeStruct((r, c), F32)] * 4, name=name,
        compiler_params=_params("parallel", "parallel"))(slabs, w, m, v)


def _adamw_small(g, w, m, v, *, name):
    def body(g_ref, w_ref, m_ref, v_ref, d_ref, mo_ref, vo_ref):
        d_ref[...], mo_ref[...], vo_ref[...] = _adam_math(g_ref[...], w_ref[...], m_ref[...], v_ref[...])

    return pl.pallas_call(body, out_shape=[jax.ShapeDtypeStruct(w.shape, F32)] * 3, name=name)(g, w, m, v)


def _sum_slabs(slabs, *, name):
    def body(s_ref, o_ref):
        acc = s_ref[0]
        for p in range(1, N_DEV):
            acc = acc + s_ref[p]
        o_ref[...] = acc

    return pl.pallas_call(body, out_shape=jax.ShapeDtypeStruct(slabs.shape[1:], F32), name=name)(slabs)


def _pad_lanes(a):
    return jnp.pad(a, ((0, 0), (0, LANES - a.shape[1])))


def _w_in_slabs(before, after, ab, n_ab):
    col_ab = before.shape[0]
    pw = col_ab + n_ab + after.shape[0]
    c = pw // N_DEV
    parts = [(0, col_ab, before, 0), (col_ab, col_ab + n_ab, ab, col_ab), (col_ab + n_ab, pw, after, col_ab + n_ab)]
    slabs = []
    for p in range(N_DEV):
        pieces = []
        for lo, hi, src, shift in parts:
            a, b = max(lo, c * p), min(hi, c * (p + 1))
            if a < b:
                pieces.append(src[a - shift:b - shift])
        slabs.append(jnp.concatenate(pieces, axis=0))
    return jnp.stack(slabs)


def _local_step(x, target, meta, g_mix, wt_main, wt_ab, cq, ck, cv, a_log, dt_bias, g_dn, g_sbq, g_sbk, g_ffn, rest,
                shards=False):
    seq, d = x.shape
    n_meta = meta.shape[0]
    heads = a_log.shape[1]
    qk = cq.shape[1]
    dvt = cv.shape[1]
    dk, dv = qk // heads, dvt // heads
    dh = g_sbq.shape[1]
    sbw = rest[2].shape[0] * N_DEV if shards else rest[1].shape[0]
    sb_heads = sbw // dh
    pad_l = (-n_meta) % CHUNK
    row_x = pad_l + n_meta
    rows = row_x + seq
    t = -(-rows // GDN_ROWS) * GDN_ROWS
    col_q, col_k, col_v, col_z = 0, qk, 2 * qk, 2 * qk + dvt
    col_sq = 2 * qk + 2 * dvt
    col_sk, col_sv, col_gd, col_gs = col_sq + sbw, col_sq + 2 * sbw, col_sq + 3 * sbw, col_sq + 3 * sbw + d

    def rows_pad(a):
        return jnp.concatenate([jnp.zeros((row_x, d), F32), a, jnp.zeros((t - rows, d), F32)], axis=0)

    h0 = jnp.concatenate([jnp.zeros((pad_l, d), F32), meta, x, jnp.zeros((t - rows, d), F32)], axis=0)
    tgt = rows_pad(target)
    a_log_p, dt_p = _pad_lanes(a_log), _pad_lanes(dt_bias)

    proj, n1 = _mm_norm(h0, g_mix, wt_main, name="proj")
    pab = _mm_nt(n1, wt_ab, out_dtype=F32, name="proj_ab")
    gk = dict(heads=heads, dk=dk, dv=dv, col_q=col_q, col_k=col_k, col_v=col_v, row_lo=pad_l, row_hi=rows)
    qn, kn, vv, g, beta = _gdn_pre(proj, pab, cq, ck, cv, a_log_p, dt_p, name="gdn_pre", **gk)
    u, w, pm, qd, kd, egl, tinv = _gdn_prep(qn, kn, vv, g, beta, name="gdn_prep")
    o_raw, o_dn, states = _gdn_scan(u, w, pm, qd, kd, egl, proj, g_dn, col_z=col_z, name="gdn_scan")
    qs, ks, vs = _sb_pre(proj, g_sbq, g_sbk, heads=sb_heads, dh=dh, col_q=col_sq, col_k=col_sk, col_v=col_sv,
                         name="sb_pre")
    if shards:
        o_sb, carry, (g_fi, g_bd, g_bs, g_out, g_fo) = _sb_fwd(qs, ks, vs, heads=sb_heads, dh=dh, key_lo=pad_l,
                                                                name="sb_fwd", gather=list(rest))
        w_bd, w_bs, w_out, wt_fi, w_fo = (g.reshape(-1, d) for g in (g_bd, g_bs, g_out, g_fi, g_fo))
    else:
        o_sb, carry = _sb_fwd(qs, ks, vs, heads=sb_heads, dh=dh, key_lo=pad_l, name="sb_fwd")
        w_bd, w_bs, w_out, wt_fg, wt_fu, w_fo = rest
        wt_fi = jnp.concatenate([wt_fg, wt_fu], axis=0)
    merged, br_dn, br_sb = _merge_fwd(o_dn, o_sb, w_bd, w_bs, proj, col_gd=col_gd, col_gs=col_gs, name="merge")
    h1 = _mm_res(h0, merged, w_out, name="mix_out")
    gate, up, act, n2 = _mm_norm_swiglu(h1, g_ffn, wt_fi, name="ffn_in")
    dy, dyb, lsum = _mm_res_loss(h1, act, w_fo, tgt, row0=row_x, nrows=seq, name="ffn_out_loss")

    dgate, dup = _swiglu_bwd(dyb, w_fo, gate, up, name="ffn_out_bwd")
    d_w_fo = _mm_tn(act, dyb, name="dw_ffn_out")
    d_wt_fi = _mm_tn([dgate, dup], n2, name="dw_ffn_in")
    d_wt_fg, d_wt_fu = d_wt_fi[:dgate.shape[1]], d_wt_fi[dgate.shape[1]:]
    dh1, dh1b, d_g_ffn = _mm_rmsbwd([([dgate, dup], wt_fi)], None, h1, g_ffn, dy, name="ffn_in_bwd")

    dbd, dbs, dgd, dgs = _merge_bwd(dh1b, w_out, proj, br_dn, br_sb, col_gd=col_gd, col_gs=col_gs, name="mix_out_bwd")
    d_w_out = _mm_tn(merged, dh1b, name="dw_out")
    d_w_bd = _mm_tn(o_dn, dbd, name="dw_branch_dn")
    d_w_bs = _mm_tn(o_sb, dbs, name="dw_branch_sb")
    do_dn = _mm_nt(dbd, w_bd, out_dtype=F32, name="branch_dn_bwd")
    do_sb = _mm_nt(dbs, w_bs, out_dtype=BF16, name="branch_sb_bwd")

    do_raw, dz, d_g_dn = _gdn_post_bwd(o_raw, proj, g_dn, do_dn, col_z=col_z, name="gdn_post_bwd")
    du, dw, dp, dqd, dkd, dgl = _gdn_bwd_scan(u, w, pm, qd, kd, egl, states, do_raw, name="gdn_bwd_scan")
    dqn, dkn, dvv, dg, dbeta = _gdn_bwd_prep(qn, kn, vv, g, beta, tinv, u, w, du, dw, dp, dqd, dkd, dgl,
                                            name="gdn_bwd_prep")
    dcq, dck, dcv, dpab, d_a_log, d_dt = _gdn_pre_bwd_a(proj, pab, cq, ck, cv, a_log_p, dt_p, dqn, dkn, dvv, dg, dbeta,
                                                        name="gdn_pre_bwd", **gk)
    dpq, d_cq = _conv_bwd(proj, dcq, cq, heads=heads, width=dk, col=col_q, name="conv_q_bwd")
    dpk, d_ck = _conv_bwd(proj, dck, ck, heads=heads, width=dk, col=col_k, name="conv_k_bwd")
    dpv, d_cv = _conv_bwd(proj, dcv, cv, heads=heads, width=dv, col=col_v, name="conv_v_bwd")

    early = None
    if shards:
        slabs = [_bf(a).reshape(N_DEV, -1, d) for a in (d_wt_fi, d_w_bd, d_w_bs, d_w_out, d_w_fo)]
        dqs, dks, dvs, early = _sb_bwd(qs, ks, vs, do_sb, carry, heads=sb_heads, dh=dh, key_lo=pad_l, name="sb_bwd",
                                       scatter=slabs)
    else:
        dqs, dks, dvs = _sb_bwd(qs, ks, vs, do_sb, carry, heads=sb_heads, dh=dh, key_lo=pad_l, name="sb_bwd")
    dsq, dsk, dsv, d_g_sbq, d_g_sbk = _sb_pre_bwd(proj, g_sbq, g_sbk, dqs, dks, dvs, heads=sb_heads, dh=dh,
                                                   col_q=col_sq, col_k=col_sk, name="sb_pre_bwd")

    dproj = [dpq, dpk, dpv, dz, dsq, dsk, dsv, dgd, dgs]
    dpab_b = _bf(dpab)
    n_dn = 4
    d_wt_dn = _mm_tn(dproj[:n_dn], n1, name="dw_in_dn")
    d_wt_sb = _mm_tn(dproj[n_dn:], n1, name="dw_in_sb")
    assert d_wt_dn.shape[0] == col_sq
    d_wt_main = jnp.concatenate([d_wt_dn, d_wt_sb], axis=0)
    d_wt_ab = _mm_tn(dpab_b, n1, name="dw_in_ab")
    s_in = None
    if shards:
        slabs = _w_in_slabs(d_wt_dn, d_wt_sb, d_wt_ab, 2 * heads)
        dh0, _, d_g_mix, (s_in,) = _mm_rmsbwd([(dproj, wt_main)], (dpab_b, wt_ab), h0, g_mix, dh1, name="proj_bwd",
                                              scatter=[slabs])
    else:
        dh0, _, d_g_mix = _mm_rmsbwd([(dproj, wt_main)], (dpab_b, wt_ab), h0, g_mix, dh1, name="proj_bwd")

    return dict(s_in=s_in, lsum=lsum, grad_x=dh0[row_x:rows], d_meta=dh0[pad_l:row_x], d_g_mix=d_g_mix,
                d_wt_main=d_wt_main, d_wt_ab=d_wt_ab, d_cq=d_cq, d_ck=d_ck, d_cv=d_cv, d_a_log=d_a_log[:, :heads],
                d_dt=d_dt[:, :heads], d_g_dn=d_g_dn, d_g_sbq=d_g_sbq, d_g_sbk=d_g_sbk, d_w_bd=d_w_bd, d_w_bs=d_w_bs,
                d_w_out=d_w_out, d_g_ffn=d_g_ffn, d_wt_fg=d_wt_fg, d_wt_fu=d_wt_fu, d_w_fo=d_w_fo, early=early)


def _pack(parts):
    flat = []
    for a in parts:
        a = a.reshape(-1)
        flat.append(jnp.pad(a, (0, (-a.shape[0]) % LANES)))
    v = jnp.concatenate(flat)
    v = jnp.pad(v, (0, (-v.shape[0]) % (8 * LANES)))
    return v.reshape(-1, LANES)


def _unpack(packed, shapes):
    flat = packed.reshape(-1)
    out, pos = [], 0
    for s in shapes:
        n = math.prod(s)
        out.append(flat[pos:pos + n].reshape(s))
        pos += n + (-n) % LANES
    return out


def kernel(x, meta_tokens, norm_mix_gain, w_in, conv_q, conv_k, conv_v, dn_a_log, dn_dt_bias, dn_out_norm_gain, sb_q_norm_gain, sb_k_norm_gain, w_branch_dn, w_branch_sb, w_out, norm_ffn_gain, w_ffn_in, w_ffn_out, loss_target, m_meta_tokens, m_norm_mix_gain, m_w_in, m_conv_q, m_conv_k, m_conv_v, m_dn_a_log, m_dn_dt_bias, m_dn_out_norm_gain, m_sb_q_norm_gain, m_sb_k_norm_gain, m_w_branch_dn, m_w_branch_sb, m_w_out, m_norm_ffn_gain, m_w_ffn_in, m_w_ffn_out, v_meta_tokens, v_norm_mix_gain, v_w_in, v_conv_q, v_conv_k, v_conv_v, v_dn_a_log, v_dn_dt_bias, v_dn_out_norm_gain, v_sb_q_norm_gain, v_sb_k_norm_gain, v_w_branch_dn, v_w_branch_sb, v_w_out, v_norm_ffn_gain, v_w_ffn_in, v_w_ffn_out):
    me = 4 * lax.axis_index("x") + 2 * lax.axis_index("y") + lax.axis_index("c")
    heads = dn_a_log.shape[1]
    d = x.shape[2]
    qk = conv_q.shape[2] * N_DEV
    dvt = conv_v.shape[2] * N_DEV
    col_ab = 2 * qk + 2 * dvt

    small_shapes = [meta_tokens.shape, conv_q.shape[1:], conv_k.shape[1:], conv_v.shape[1:]]
    small = _pack([meta_tokens, conv_q[0], conv_k[0], conv_v[0]])
    def features_major(a):
        return jnp.transpose(a, (2, 0, 1)).reshape(a.shape[2], a.shape[1])

    g_in, g_small = _gather_two_level([_bf(features_major(w_in)), small], name="gather_w_in")
    wt_full = g_in.reshape(-1, d)
    wt_main = jnp.concatenate([wt_full[:col_ab], wt_full[col_ab + 2 * heads:]], axis=0)
    wt_ab = jnp.pad(wt_full[col_ab:col_ab + 2 * heads], ((0, LANES - 2 * heads), (0, 0)))
    parts = [_unpack(g_small[p], small_shapes) for p in range(N_DEV)]
    meta_f, cq_f, ck_f, cv_f = (jnp.concatenate([parts[p][a] for p in range(N_DEV)], axis=1) for a in range(4))

    r = _local_step(x[0], loss_target[0], meta_f, norm_mix_gain, wt_main, wt_ab, cq_f, ck_f, cv_f, dn_a_log, dn_dt_bias,
                    dn_out_norm_gain, sb_q_norm_gain, sb_k_norm_gain, norm_ffn_gain,
                    (_bf(features_major(w_ffn_in)), _bf(w_branch_dn[0]), _bf(w_branch_sb[0]), _bf(w_out[0]),
                     _bf(w_ffn_out[0])), shards=True)
    s_fi, s_bd, s_bs, s_out, s_fo = r["early"]
    s_in = r["s_in"]

    loss_part = (0.5 / d) * jnp.sum(r["lsum"], axis=1, keepdims=True)
    small_g = [r["d_meta"], r["d_g_mix"], r["d_cq"], r["d_ck"], r["d_cv"], r["d_a_log"], r["d_dt"], r["d_g_dn"],
               r["d_g_sbq"], r["d_g_sbk"], r["d_g_ffn"], loss_part]
    (g_packs,) = _exchange([_pack(small_g)], scatter=[False], name="gather_small_grads")
    (g_meta, g_mix, g_cq, g_ck, g_cv, g_al, g_dt, g_gdn, g_sbq, g_sbk, g_ffn, loss) = _unpack(
        _sum_slabs(g_packs, name="sum_small_grads"), [a.shape for a in small_g])

    def mine(a, width):
        return lax.dynamic_slice_in_dim(a, me * width, width, axis=1)

    big = dict(w_in=(s_in, w_in, m_w_in, v_w_in), w_branch_dn=(s_bd, w_branch_dn, m_w_branch_dn, v_w_branch_dn),
               w_branch_sb=(s_bs, w_branch_sb, m_w_branch_sb, v_w_branch_sb), w_out=(s_out, w_out, m_w_out, v_w_out),
               w_ffn_in=(s_fi, w_ffn_in, m_w_ffn_in, v_w_ffn_in), w_ffn_out=(s_fo, w_ffn_out, m_w_ffn_out, v_w_ffn_out))
    tiny = dict(meta_tokens=(mine(g_meta, d // N_DEV), meta_tokens, m_meta_tokens, v_meta_tokens),
                norm_mix_gain=(g_mix, norm_mix_gain, m_norm_mix_gain, v_norm_mix_gain),
                conv_q=(mine(g_cq, qk // N_DEV), conv_q[0], m_conv_q[0], v_conv_q[0]),
                conv_k=(mine(g_ck, qk // N_DEV), conv_k[0], m_conv_k[0], v_conv_k[0]),
                conv_v=(mine(g_cv, dvt // N_DEV), conv_v[0], m_conv_v[0], v_conv_v[0]),
                dn_a_log=(g_al, dn_a_log, m_dn_a_log, v_dn_a_log), dn_dt_bias=(g_dt, dn_dt_bias, m_dn_dt_bias, v_dn_dt_bias),
                dn_out_norm_gain=(g_gdn, dn_out_norm_gain, m_dn_out_norm_gain, v_dn_out_norm_gain),
                sb_q_norm_gain=(g_sbq, sb_q_norm_gain, m_sb_q_norm_gain, v_sb_q_norm_gain),
                sb_k_norm_gain=(g_sbk, sb_k_norm_gain, m_sb_k_norm_gain, v_sb_k_norm_gain),
                norm_ffn_gain=(g_ffn, norm_ffn_gain, m_norm_ffn_gain, v_norm_ffn_gain))
    order = ["meta_tokens", "norm_mix_gain", "w_in", "conv_q", "conv_k", "conv_v", "dn_a_log", "dn_dt_bias",
             "dn_out_norm_gain", "sb_q_norm_gain", "sb_k_norm_gain", "w_branch_dn", "w_branch_sb", "w_out",
             "norm_ffn_gain", "w_ffn_in", "w_ffn_out"]
    grads, deltas, new_m, new_v = [], [], [], []
    for name in order:
        if name in ("w_in", "w_ffn_in"):
            slabs, w, m, v = big[name]
            res = _adamw_slabs(slabs, *(features_major(a).reshape(slabs.shape[1:]) for a in (w, m, v)), name="adamw_" + name)
            g, dl, mo, vo = (jnp.transpose(a.reshape(w.shape[2], 1, w.shape[1]), (1, 2, 0)) for a in res)
            like = w.shape
        elif name in big:
            slabs, w, m, v = big[name]
            g, dl, mo, vo = _adamw_slabs(slabs, w[0], m[0], v[0], name="adamw_" + name)
            like = w.shape
        else:
            g, w, m, v = tiny[name]
            like = dict(conv_q=conv_q, conv_k=conv_k, conv_v=conv_v).get(name, w).shape
            dl, mo, vo = _adamw_small(g, w, m, v, name="adamw_" + name)
        for lst, a in ((grads, g), (deltas, dl), (new_m, mo), (new_v, vo)):
            lst.append(a.reshape(like))
    return (loss.reshape(()), r["grad_x"][None], *grads, *deltas, *new_m, *new_v)
```
